```python
import math
import jax, jax.numpy as jnp
from jax import lax
import numpy as np

D_MODEL = 1024
BATCH = 8
SEQ = 4096
DEPTH = 2

N_EVEN = (DEPTH + 1) // 2
N_ODD = DEPTH // 2
DN_ALPHA = (2.0 * DEPTH) ** 0.25
DN_BETA = (8.0 * DEPTH) ** -0.25

RNN_WIDTH = D_MODEL // 2
RNN_HEADS = 8
RNN_HEAD_DIM = RNN_WIDTH // RNN_HEADS
CONV_WIDTH = 4
RG_C = 8.0

MLA_HEADS = 8
MLA_NOPE = 64
MLA_ROPE = 32
MLA_V = 64
MLA_Q_RANK = D_MODEL // 4
MLA_KV_RANK = D_MODEL // 8
MLA_WIDTH = MLA_HEADS * MLA_V
ROPE_THETA = 10000.0
Q_BLOCK = 128

AB_WIDTH = RNN_WIDTH + MLA_WIDTH
AB_IN = RNN_WIDTH + AB_WIDTH + MLA_Q_RANK + MLA_KV_RANK + MLA_ROPE

SSD_INNER = 2 * D_MODEL
SSD_HEAD_DIM = 64
SSD_HEADS = SSD_INNER // SSD_HEAD_DIM
SSD_GROUPS = 4
SSD_STATE = 128
SSD_CHUNK = 128
SSD_CONV_DIM = SSD_INNER + 2 * SSD_GROUPS * SSD_STATE
SSD_IN = SSD_INNER + SSD_CONV_DIM + SSD_HEADS

kernel_name = "hybrid_rglru_mla_ssd_deepnorm"


def _rmsnorm(x, g, eps=1e-6):
    xf = x.astype(jnp.float32)
    y = xf * lax.rsqrt(jnp.mean(xf * xf, axis=-1, keepdims=True) + eps)
    return (y * g.astype(jnp.float32)).astype(x.dtype)


def _layernorm(x, g, b, eps=1e-5):
    xf = x.astype(jnp.float32)
    mu = jnp.mean(xf, axis=-1, keepdims=True)
    xc = xf - mu
    var = jnp.mean(xc * xc, axis=-1, keepdims=True)
    y = xc * lax.rsqrt(var + eps) * g.astype(jnp.float32) + b.astype(jnp.float32)
    return y.astype(x.dtype)


def _causal_conv(x, w, b):
    k_taps = w.shape[0]
    seqlen = x.shape[1]
    xp = jnp.pad(x, ((0, 0), (k_taps - 1, 0), (0, 0)))
    return sum(xp[:, k:k + seqlen] * w[k] for k in range(k_taps)) + b


def _rope(x, cos, sin):
    half = x.shape[-1] // 2
    x1, x2 = x[..., :half], x[..., half:]
    return jnp.concatenate([x1 * cos - x2 * sin, x2 * cos + x1 * sin], axis=-1)


def _rg_lru(x, w_a, b_a, w_x, b_x, lam):
    bsz, seqlen, _ = x.shape
    xh = x.reshape(bsz, seqlen, RNN_HEADS, RNN_HEAD_DIM)
    r = jax.nn.sigmoid(jnp.einsum('bshi,hij->bshj', xh, w_a).reshape(bsz, seqlen, RNN_WIDTH) + b_a)
    i = jax.nn.sigmoid(jnp.einsum('bshi,hij->bshj', xh, w_x).reshape(bsz, seqlen, RNN_WIDTH) + b_x)
    log_a = (-RG_C * r.astype(jnp.float32)) * jax.nn.softplus(-lam.astype(jnp.float32))
    a = jnp.exp(log_a)
    mult = jnp.sqrt(-jnp.expm1(2.0 * log_a))
    u = mult * (i * x).astype(jnp.float32)

    def combine(c1, c2):
        a1, b1 = c1
        a2, b2 = c2
        return a1 * a2, a2 * b1 + b2

    _, h = lax.associative_scan(combine, (a, u), axis=1)
    return h.astype(x.dtype)


def _mla_attention(q_nope, q_rope, k_nope, k_rope, v):
    seqlen = q_nope.shape[1]
    scale = (MLA_NOPE + MLA_ROPE) ** -0.5
    outs = []
    for blk in range(seqlen // Q_BLOCK):
        q0 = blk * Q_BLOCK
        kend = q0 + Q_BLOCK
        s = (jnp.einsum('bqhd,bkhd->bhqk', q_nope[:, q0:kend], k_nope[:, :kend])
             + jnp.einsum('bqhr,bkr->bhqk', q_rope[:, q0:kend], k_rope[:, :kend]))
        s = s.astype(jnp.float32) * scale
        mask = jnp.arange(kend)[None, :] <= (q0 + jnp.arange(Q_BLOCK))[:, None]
        s = jnp.where(mask, s, -jnp.inf)
        p = jax.nn.softmax(s, axis=-1).astype(v.dtype)
        outs.append(jnp.einsum('bhqk,bkhd->bqhd', p, v[:, :kend]))
    return jnp.concatenate(outs, axis=1)


def _rglru_mla_layer(x, cos, sin, w_in, conv_w, conv_b, gate_a_w, gate_a_b, gate_x_w,
                     gate_x_b, lam, q_norm, kv_norm, w_uq, w_ukv, w_out):
    bsz, seqlen, _ = x.shape
    proj = jnp.einsum('bsd,de->bse', x, w_in)
    x_rnn, gate, c_q, c_kv, k_rope = jnp.split(
        proj, [RNN_WIDTH, RNN_WIDTH + AB_WIDTH, RNN_WIDTH + AB_WIDTH + MLA_Q_RANK,
               RNN_WIDTH + AB_WIDTH + MLA_Q_RANK + MLA_KV_RANK], axis=-1)
    x_rnn = _causal_conv(x_rnn, conv_w, conv_b)
    y_rnn = _rg_lru(x_rnn, gate_a_w, gate_a_b, gate_x_w, gate_x_b, lam)
    q = jnp.einsum('bsr,re->bse', _rmsnorm(c_q, q_norm), w_uq)
    q = q.reshape(bsz, seqlen, MLA_HEADS, MLA_NOPE + MLA_ROPE)
    q_nope = q[..., :MLA_NOPE]
    q_rope = _rope(q[..., MLA_NOPE:], cos[:, :, None], sin[:, :, None])
    kv = jnp.einsum('bsr,re->bse', _rmsnorm(c_kv, kv_norm), w_ukv)
    kv = kv.reshape(bsz, seqlen, MLA_HEADS, MLA_NOPE + MLA_V)
    k_nope, v = kv[..., :MLA_NOPE], kv[..., MLA_NOPE:]
    k_rope = _rope(k_rope, cos, sin)
    y_mla = _mla_attention(q_nope, q_rope, k_nope, k_rope, v).reshape(bsz, seqlen, MLA_WIDTH)
    y = jnp.concatenate([y_rnn, y_mla], axis=-1) * jax.nn.silu(gate)
    return jnp.einsum('bse,ed->bsd', y, w_out)


def _ssd_scan(x, dt, a_neg, bm, cm):
    bsz, seqlen, n_heads, p_dim = x.shape
    n_groups, n_state = bm.shape[2], bm.shape[3]
    hpg = n_heads // n_groups
    nc, L = seqlen // SSD_CHUNK, SSD_CHUNK
    xf = (x.astype(jnp.float32) * dt[..., None]).reshape(bsz, nc, L, n_groups, hpg, p_dim)
    a_dt = (dt * a_neg).reshape(bsz, nc, L, n_groups, hpg)
    bc = bm.astype(jnp.float32).reshape(bsz, nc, L, n_groups, n_state)
    cc = cm.astype(jnp.float32).reshape(bsz, nc, L, n_groups, n_state)
    cs = jnp.cumsum(a_dt, axis=2)
    cs_h = jnp.transpose(cs, (0, 1, 3, 4, 2))
    seg = cs_h[..., :, None] - cs_h[..., None, :]
    causal = jnp.tril(jnp.ones((L, L), dtype=bool))
    decay = jnp.where(causal, jnp.exp(jnp.where(causal, seg, 0.0)), 0.0)
    cb = jnp.einsum('bclgn,bcsgn->bcgls', cc, bc)
    y_diag = jnp.einsum('bcghls,bcsghp->bclghp', cb[:, :, :, None] * decay, xf)
    decay_states = jnp.exp(cs[:, :, -1:] - cs)
    states = jnp.einsum('bclgn,bclghp->bcghpn', bc, xf * decay_states[..., None])
    chunk_decay = jnp.exp(cs[:, :, -1])

    def step(h, inp):
        dec, st = inp
        return h * dec[..., None, None] + st, h

    h0 = jnp.zeros((bsz, n_groups, hpg, p_dim, n_state), jnp.float32)
    _, prev = lax.scan(step, h0, (jnp.moveaxis(chunk_decay, 1, 0), jnp.moveaxis(states, 1, 0)))
    prev = jnp.moveaxis(prev, 0, 1)
    y_off = jnp.einsum('bclgn,bcghpn->bclghp', cc, prev) * jnp.exp(cs)[..., None]
    return (y_diag + y_off).reshape(bsz, seqlen, n_heads, p_dim)


def _ssd_layer(x, w_in, conv_w, conv_b, dt_bias, a_log, d_skip, norm_w, w_out):
    bsz, seqlen, _ = x.shape
    proj = jnp.einsum('bsd,de->bse', x, w_in)
    z, xbc, dt = jnp.split(proj, [SSD_INNER, SSD_INNER + SSD_CONV_DIM], axis=-1)
    xbc = jax.nn.silu(_causal_conv(xbc, conv_w, conv_b))
    xs, bm, cm = jnp.split(xbc, [SSD_INNER, SSD_INNER + SSD_GROUPS * SSD_STATE], axis=-1)
    xs = xs.reshape(bsz, seqlen, SSD_HEADS, SSD_HEAD_DIM)
    bm = bm.reshape(bsz, seqlen, SSD_GROUPS, SSD_STATE)
    cm = cm.reshape(bsz, seqlen, SSD_GROUPS, SSD_STATE)
    dt = jax.nn.softplus(dt.astype(jnp.float32) + dt_bias.astype(jnp.float32))
    a_neg = -jnp.exp(a_log.astype(jnp.float32))
    y = _ssd_scan(xs, dt, a_neg, bm, cm) + d_skip.astype(jnp.float32)[:, None] * xs.astype(jnp.float32)
    y = y.reshape(bsz, seqlen, SSD_INNER) * jax.nn.silu(z.astype(jnp.float32))
    yg = y.reshape(bsz, seqlen, SSD_GROUPS, SSD_INNER // SSD_GROUPS)
    yg = yg * lax.rsqrt(jnp.mean(yg * yg, axis=-1, keepdims=True) + 1e-6)
    y = (yg.reshape(bsz, seqlen, SSD_INNER) * norm_w.astype(jnp.float32)).astype(x.dtype)
    return jnp.einsum('bse,ed->bsd', y, w_out)


def _fwd_setup_inputs(seed: int = 0) -> dict:
    key = jax.random.key(seed)
    ks = jax.random.split(key, 32)
    f32 = jnp.float32
    nrm = lambda k, shape, s: jax.random.normal(k, shape, f32) * s

    x = jax.random.normal(ks[0], (BATCH, SEQ, D_MODEL), f32)
    offset = jax.random.randint(ks[1], (BATCH, 1), 0, 1024, dtype=jnp.int32)
    positions = offset + jnp.arange(SEQ, dtype=jnp.int32)[None, :]

    ab_w_in = nrm(ks[2], (N_EVEN, D_MODEL, AB_IN), D_MODEL ** -0.5)
    ab_conv_w = nrm(ks[3], (N_EVEN, CONV_WIDTH, RNN_WIDTH), CONV_WIDTH ** -0.5)
    ab_conv_b = nrm(ks[4], (N_EVEN, RNN_WIDTH), 0.02)
    ab_gate_a_w = nrm(ks[5], (N_EVEN, RNN_HEADS, RNN_HEAD_DIM, RNN_HEAD_DIM), RNN_HEAD_DIM ** -0.5)
    ab_gate_a_b = nrm(ks[6], (N_EVEN, RNN_WIDTH), 0.1)
    ab_gate_x_w = nrm(ks[7], (N_EVEN, RNN_HEADS, RNN_HEAD_DIM, RNN_HEAD_DIM), RNN_HEAD_DIM ** -0.5)
    ab_gate_x_b = nrm(ks[8], (N_EVEN, RNN_WIDTH), 0.1)
    u = jax.random.uniform(ks[9], (N_EVEN, RNN_WIDTH), f32, 0.9, 0.999)
    a0 = u ** (1.0 / RG_C)
    ab_lambda = jnp.log(a0) - jnp.log1p(-a0)
    mla_q_norm = 1.0 + nrm(ks[10], (N_EVEN, MLA_Q_RANK), 0.05)
    mla_kv_norm = 1.0 + nrm(ks[11], (N_EVEN, MLA_KV_RANK), 0.05)
    mla_w_uq = nrm(ks[12], (N_EVEN, MLA_Q_RANK, MLA_HEADS * (MLA_NOPE + MLA_ROPE)), MLA_Q_RANK ** -0.5)
    mla_w_ukv = nrm(ks[13], (N_EVEN, MLA_KV_RANK, MLA_HEADS * (MLA_NOPE + MLA_V)), MLA_KV_RANK ** -0.5)
    ab_w_out = nrm(ks[14], (N_EVEN, AB_WIDTH, D_MODEL), DN_BETA * math.sqrt(2.0 / (AB_WIDTH + D_MODEL)))
    ab_ln_g = 1.0 + nrm(ks[15], (N_EVEN, D_MODEL), 0.05)
    ab_ln_b = nrm(ks[16], (N_EVEN, D_MODEL), 0.02)

    ssd_w_in = nrm(ks[17], (N_ODD, D_MODEL, SSD_IN), D_MODEL ** -0.5)
    ssd_conv_w = nrm(ks[18], (N_ODD, CONV_WIDTH, SSD_CONV_DIM), CONV_WIDTH ** -0.5)
    ssd_conv_b = nrm(ks[19], (N_ODD, SSD_CONV_DIM), 0.02)
    dt0 = jnp.exp(jax.random.uniform(ks[20], (N_ODD, SSD_HEADS), f32, math.log(1e-3), math.log(1e-1)))
    ssd_dt_bias = dt0 + jnp.log(-jnp.expm1(-dt0))
    ssd_a_log = jnp.log(jax.random.uniform(ks[21], (N_ODD, SSD_HEADS), f32, 1.0, 16.0))
    ssd_d = 1.0 + nrm(ks[22], (N_ODD, SSD_HEADS), 0.1)
    ssd_norm = 1.0 + nrm(ks[23], (N_ODD, SSD_INNER), 0.05)
    ssd_w_out = nrm(ks[24], (N_ODD, SSD_INNER, D_MODEL), DN_BETA * math.sqrt(2.0 / (SSD_INNER + D_MODEL)))
    ssd_ln_g = 1.0 + nrm(ks[25], (N_ODD, D_MODEL), 0.05)
    ssd_ln_b = nrm(ks[26], (N_ODD, D_MODEL), 0.02)

    return {"x": x, "positions": positions,
            "ab_w_in": ab_w_in, "ab_conv_w": ab_conv_w, "ab_conv_b": ab_conv_b,
            "ab_gate_a_w": ab_gate_a_w, "ab_gate_a_b": ab_gate_a_b,
            "ab_gate_x_w": ab_gate_x_w, "ab_gate_x_b": ab_gate_x_b,
            "ab_lambda": ab_lambda, "mla_q_norm": mla_q_norm, "mla_kv_norm": mla_kv_norm,
            "mla_w_uq": mla_w_uq, "mla_w_ukv": mla_w_ukv, "ab_w_out": ab_w_out,
            "ab_ln_g": ab_ln_g, "ab_ln_b": ab_ln_b,
            "ssd_w_in": ssd_w_in, "ssd_conv_w": ssd_conv_w, "ssd_conv_b": ssd_conv_b,
            "ssd_dt_bias": ssd_dt_bias, "ssd_a_log": ssd_a_log, "ssd_d": ssd_d,
            "ssd_norm": ssd_norm, "ssd_w_out": ssd_w_out,
            "ssd_ln_g": ssd_ln_g, "ssd_ln_b": ssd_ln_b}


def _fwd_reference(x, positions, ab_w_in, ab_conv_w, ab_conv_b, ab_gate_a_w, ab_gate_a_b,
              ab_gate_x_w, ab_gate_x_b, ab_lambda, mla_q_norm, mla_kv_norm, mla_w_uq,
              mla_w_ukv, ab_w_out, ab_ln_g, ab_ln_b, ssd_w_in, ssd_conv_w, ssd_conv_b,
              ssd_dt_bias, ssd_a_log, ssd_d, ssd_norm, ssd_w_out, ssd_ln_g, ssd_ln_b):
    inv_freq = ROPE_THETA ** (-jnp.arange(0, MLA_ROPE, 2, dtype=jnp.float32) / MLA_ROPE)
    ang = positions.astype(jnp.float32)[..., None] * inv_freq
    cos = jnp.cos(ang).astype(x.dtype)
    sin = jnp.sin(ang).astype(x.dtype)
    for layer in range(DEPTH):
        j = layer // 2
        if layer % 2 == 0:
            y = _rglru_mla_layer(x, cos, sin, ab_w_in[j], ab_conv_w[j], ab_conv_b[j],
                                 ab_gate_a_w[j], ab_gate_a_b[j], ab_gate_x_w[j], ab_gate_x_b[j],
                                 ab_lambda[j], mla_q_norm[j], mla_kv_norm[j], mla_w_uq[j],
                                 mla_w_ukv[j], ab_w_out[j])
            x = _layernorm(DN_ALPHA * x + y, ab_ln_g[j], ab_ln_b[j])
        else:
            y = _ssd_layer(x, ssd_w_in[j], ssd_conv_w[j], ssd_conv_b[j], ssd_dt_bias[j],
                           ssd_a_log[j], ssd_d[j], ssd_norm[j], ssd_w_out[j])
            x = _layernorm(DN_ALPHA * x + y, ssd_ln_g[j], ssd_ln_b[j])
    return x


import jax as _jax
import jax.numpy as _jnp

TWIN_FORMAT = 'train_step'
FWD_PARAMS = ['x', 'positions', 'ab_w_in', 'ab_conv_w', 'ab_conv_b', 'ab_gate_a_w', 'ab_gate_a_b', 'ab_gate_x_w', 'ab_gate_x_b', 'ab_lambda', 'mla_q_norm', 'mla_kv_norm', 'mla_w_uq', 'mla_w_ukv', 'ab_w_out', 'ab_ln_g', 'ab_ln_b', 'ssd_w_in', 'ssd_conv_w', 'ssd_conv_b', 'ssd_dt_bias', 'ssd_a_log', 'ssd_d', 'ssd_norm', 'ssd_w_out', 'ssd_ln_g', 'ssd_ln_b']
TWIN_WEIGHTS = ['ab_w_in', 'ab_conv_w', 'ab_conv_b', 'ab_gate_a_w', 'ab_gate_a_b', 'ab_gate_x_w', 'ab_gate_x_b', 'ab_lambda', 'mla_q_norm', 'mla_kv_norm', 'mla_w_uq', 'mla_w_ukv', 'ab_w_out', 'ab_ln_g', 'ab_ln_b', 'ssd_w_in', 'ssd_conv_w', 'ssd_conv_b', 'ssd_dt_bias', 'ssd_a_log', 'ssd_d', 'ssd_norm', 'ssd_w_out', 'ssd_ln_g', 'ssd_ln_b']
TWIN_DIFF_INPUT = 'x'
TWIN_INPUTS = ['x', 'positions', 'ab_w_in', 'ab_conv_w', 'ab_conv_b', 'ab_gate_a_w', 'ab_gate_a_b', 'ab_gate_x_w', 'ab_gate_x_b', 'ab_lambda', 'mla_q_norm', 'mla_kv_norm', 'mla_w_uq', 'mla_w_ukv', 'ab_w_out', 'ab_ln_g', 'ab_ln_b', 'ssd_w_in', 'ssd_conv_w', 'ssd_conv_b', 'ssd_dt_bias', 'ssd_a_log', 'ssd_d', 'ssd_norm', 'ssd_w_out', 'ssd_ln_g', 'ssd_ln_b', 'loss_target', 'm_ab_w_in', 'm_ab_conv_w', 'm_ab_conv_b', 'm_ab_gate_a_w', 'm_ab_gate_a_b', 'm_ab_gate_x_w', 'm_ab_gate_x_b', 'm_ab_lambda', 'm_mla_q_norm', 'm_mla_kv_norm', 'm_mla_w_uq', 'm_mla_w_ukv', 'm_ab_w_out', 'm_ab_ln_g', 'm_ab_ln_b', 'm_ssd_w_in', 'm_ssd_conv_w', 'm_ssd_conv_b', 'm_ssd_dt_bias', 'm_ssd_a_log', 'm_ssd_d', 'm_ssd_norm', 'm_ssd_w_out', 'm_ssd_ln_g', 'm_ssd_ln_b', 'v_ab_w_in', 'v_ab_conv_w', 'v_ab_conv_b', 'v_ab_gate_a_w', 'v_ab_gate_a_b', 'v_ab_gate_x_w', 'v_ab_gate_x_b', 'v_ab_lambda', 'v_mla_q_norm', 'v_mla_kv_norm', 'v_mla_w_uq', 'v_mla_w_ukv', 'v_ab_w_out', 'v_ab_ln_g', 'v_ab_ln_b', 'v_ssd_w_in', 'v_ssd_conv_w', 'v_ssd_conv_b', 'v_ssd_dt_bias', 'v_ssd_a_log', 'v_ssd_d', 'v_ssd_norm', 'v_ssd_w_out', 'v_ssd_ln_g', 'v_ssd_ln_b']
TWIN_OUTPUTS = ['loss', 'grad_x', 'grad_ab_w_in', 'grad_ab_conv_w', 'grad_ab_conv_b', 'grad_ab_gate_a_w', 'grad_ab_gate_a_b', 'grad_ab_gate_x_w', 'grad_ab_gate_x_b', 'grad_ab_lambda', 'grad_mla_q_norm', 'grad_mla_kv_norm', 'grad_mla_w_uq', 'grad_mla_w_ukv', 'grad_ab_w_out', 'grad_ab_ln_g', 'grad_ab_ln_b', 'grad_ssd_w_in', 'grad_ssd_conv_w', 'grad_ssd_conv_b', 'grad_ssd_dt_bias', 'grad_ssd_a_log', 'grad_ssd_d', 'grad_ssd_norm', 'grad_ssd_w_out', 'grad_ssd_ln_g', 'grad_ssd_ln_b', 'delta_ab_w_in', 'delta_ab_conv_w', 'delta_ab_conv_b', 'delta_ab_gate_a_w', 'delta_ab_gate_a_b', 'delta_ab_gate_x_w', 'delta_ab_gate_x_b', 'delta_ab_lambda', 'delta_mla_q_norm', 'delta_mla_kv_norm', 'delta_mla_w_uq', 'delta_mla_w_ukv', 'delta_ab_w_out', 'delta_ab_ln_g', 'delta_ab_ln_b', 'delta_ssd_w_in', 'delta_ssd_conv_w', 'delta_ssd_conv_b', 'delta_ssd_dt_bias', 'delta_ssd_a_log', 'delta_ssd_d', 'delta_ssd_norm', 'delta_ssd_w_out', 'delta_ssd_ln_g', 'delta_ssd_ln_b', 'new_m_ab_w_in', 'new_m_ab_conv_w', 'new_m_ab_conv_b', 'new_m_ab_gate_a_w', 'new_m_ab_gate_a_b', 'new_m_ab_gate_x_w', 'new_m_ab_gate_x_b', 'new_m_ab_lambda', 'new_m_mla_q_norm', 'new_m_mla_kv_norm', 'new_m_mla_w_uq', 'new_m_mla_w_ukv', 'new_m_ab_w_out', 'new_m_ab_ln_g', 'new_m_ab_ln_b', 'new_m_ssd_w_in', 'new_m_ssd_conv_w', 'new_m_ssd_conv_b', 'new_m_ssd_dt_bias', 'new_m_ssd_a_log', 'new_m_ssd_d', 'new_m_ssd_norm', 'new_m_ssd_w_out', 'new_m_ssd_ln_g', 'new_m_ssd_ln_b', 'new_v_ab_w_in', 'new_v_ab_conv_w', 'new_v_ab_conv_b', 'new_v_ab_gate_a_w', 'new_v_ab_gate_a_b', 'new_v_ab_gate_x_w', 'new_v_ab_gate_x_b', 'new_v_ab_lambda', 'new_v_mla_q_norm', 'new_v_mla_kv_norm', 'new_v_mla_w_uq', 'new_v_mla_w_ukv', 'new_v_ab_w_out', 'new_v_ab_ln_g', 'new_v_ab_ln_b', 'new_v_ssd_w_in', 'new_v_ssd_conv_w', 'new_v_ssd_conv_b', 'new_v_ssd_dt_bias', 'new_v_ssd_a_log', 'new_v_ssd_d', 'new_v_ssd_norm', 'new_v_ssd_w_out', 'new_v_ssd_ln_g', 'new_v_ssd_ln_b']
TWIN_LEAF_KINDS = {'loss': 'loss', 'grad_x': 'grad_x', 'grad_ab_w_in': 'grad_w', 'grad_ab_conv_w': 'grad_w', 'grad_ab_conv_b': 'grad_w', 'grad_ab_gate_a_w': 'grad_w', 'grad_ab_gate_a_b': 'grad_w', 'grad_ab_gate_x_w': 'grad_w', 'grad_ab_gate_x_b': 'grad_w', 'grad_ab_lambda': 'grad_w', 'grad_mla_q_norm': 'grad_w', 'grad_mla_kv_norm': 'grad_w', 'grad_mla_w_uq': 'grad_w', 'grad_mla_w_ukv': 'grad_w', 'grad_ab_w_out': 'grad_w', 'grad_ab_ln_g': 'grad_w', 'grad_ab_ln_b': 'grad_w', 'grad_ssd_w_in': 'grad_w', 'grad_ssd_conv_w': 'grad_w', 'grad_ssd_conv_b': 'grad_w', 'grad_ssd_dt_bias': 'grad_w', 'grad_ssd_a_log': 'grad_w', 'grad_ssd_d': 'grad_w', 'grad_ssd_norm': 'grad_w', 'grad_ssd_w_out': 'grad_w', 'grad_ssd_ln_g': 'grad_w', 'grad_ssd_ln_b': 'grad_w', 'delta_ab_w_in': 'delta_w', 'delta_ab_conv_w': 'delta_w', 'delta_ab_conv_b': 'delta_w', 'delta_ab_gate_a_w': 'delta_w', 'delta_ab_gate_a_b': 'delta_w', 'delta_ab_gate_x_w': 'delta_w', 'delta_ab_gate_x_b': 'delta_w', 'delta_ab_lambda': 'delta_w', 'delta_mla_q_norm': 'delta_w', 'delta_mla_kv_norm': 'delta_w', 'delta_mla_w_uq': 'delta_w', 'delta_mla_w_ukv': 'delta_w', 'delta_ab_w_out': 'delta_w', 'delta_ab_ln_g': 'delta_w', 'delta_ab_ln_b': 'delta_w', 'delta_ssd_w_in': 'delta_w', 'delta_ssd_conv_w': 'delta_w', 'delta_ssd_conv_b': 'delta_w', 'delta_ssd_dt_bias': 'delta_w', 'delta_ssd_a_log': 'delta_w', 'delta_ssd_d': 'delta_w', 'delta_ssd_norm': 'delta_w', 'delta_ssd_w_out': 'delta_w', 'delta_ssd_ln_g': 'delta_w', 'delta_ssd_ln_b': 'delta_w', 'new_m_ab_w_in': 'new_m', 'new_m_ab_conv_w': 'new_m', 'new_m_ab_conv_b': 'new_m', 'new_m_ab_gate_a_w': 'new_m', 'new_m_ab_gate_a_b': 'new_m', 'new_m_ab_gate_x_w': 'new_m', 'new_m_ab_gate_x_b': 'new_m', 'new_m_ab_lambda': 'new_m', 'new_m_mla_q_norm': 'new_m', 'new_m_mla_kv_norm': 'new_m', 'new_m_mla_w_uq': 'new_m', 'new_m_mla_w_ukv': 'new_m', 'new_m_ab_w_out': 'new_m', 'new_m_ab_ln_g': 'new_m', 'new_m_ab_ln_b': 'new_m', 'new_m_ssd_w_in': 'new_m', 'new_m_ssd_conv_w': 'new_m', 'new_m_ssd_conv_b': 'new_m', 'new_m_ssd_dt_bias': 'new_m', 'new_m_ssd_a_log': 'new_m', 'new_m_ssd_d': 'new_m', 'new_m_ssd_norm': 'new_m', 'new_m_ssd_w_out': 'new_m', 'new_m_ssd_ln_g': 'new_m', 'new_m_ssd_ln_b': 'new_m', 'new_v_ab_w_in': 'new_v', 'new_v_ab_conv_w': 'new_v', 'new_v_ab_conv_b': 'new_v', 'new_v_ab_gate_a_w': 'new_v', 'new_v_ab_gate_a_b': 'new_v', 'new_v_ab_gate_x_w': 'new_v', 'new_v_ab_gate_x_b': 'new_v', 'new_v_ab_lambda': 'new_v', 'new_v_mla_q_norm': 'new_v', 'new_v_mla_kv_norm': 'new_v', 'new_v_mla_w_uq': 'new_v', 'new_v_mla_w_ukv': 'new_v', 'new_v_ab_w_out': 'new_v', 'new_v_ab_ln_g': 'new_v', 'new_v_ab_ln_b': 'new_v', 'new_v_ssd_w_in': 'new_v', 'new_v_ssd_conv_w': 'new_v', 'new_v_ssd_conv_b': 'new_v', 'new_v_ssd_dt_bias': 'new_v', 'new_v_ssd_a_log': 'new_v', 'new_v_ssd_d': 'new_v', 'new_v_ssd_norm': 'new_v', 'new_v_ssd_w_out': 'new_v', 'new_v_ssd_ln_g': 'new_v', 'new_v_ssd_ln_b': 'new_v'}


def _forward(args):
    return _fwd_reference(*[args[k] for k in FWD_PARAMS])


def _output_shape():
    out = _jax.eval_shape(lambda: _forward(_fwd_setup_inputs(0)))
    return out.shape, out.dtype

N_MICROBATCH = 1
ADAM_LR = 0.001
ADAM_B1 = 0.9
ADAM_B2 = 0.999
ADAM_EPS = 1e-08
ADAM_WD = 0.01
ADAM_STEP = 10
PER_EXAMPLE_BATCH_AXIS = {'x': 0, 'positions': 0, 'loss_target': 0}
SHARED_INPUTS = []
_WEIGHT_DTYPES = {'ab_w_in': _jnp.float32, 'ab_conv_w': _jnp.float32, 'ab_conv_b': _jnp.float32, 'ab_gate_a_w': _jnp.float32, 'ab_gate_a_b': _jnp.float32, 'ab_gate_x_w': _jnp.float32, 'ab_gate_x_b': _jnp.float32, 'ab_lambda': _jnp.float32, 'mla_q_norm': _jnp.float32, 'mla_kv_norm': _jnp.float32, 'mla_w_uq': _jnp.float32, 'mla_w_ukv': _jnp.float32, 'ab_w_out': _jnp.float32, 'ab_ln_g': _jnp.float32, 'ab_ln_b': _jnp.float32, 'ssd_w_in': _jnp.float32, 'ssd_conv_w': _jnp.float32, 'ssd_conv_b': _jnp.float32, 'ssd_dt_bias': _jnp.float32, 'ssd_a_log': _jnp.float32, 'ssd_d': _jnp.float32, 'ssd_norm': _jnp.float32, 'ssd_w_out': _jnp.float32, 'ssd_ln_g': _jnp.float32, 'ssd_ln_b': _jnp.float32}
MOMENT_SCALE = {'ab_w_in': 1.955563e-02, 'ab_conv_w': 3.181567e-02, 'ab_conv_b': 3.737812e-01, 'ab_gate_a_w': 1.224773e-02, 'ab_gate_a_b': 9.490860e-03, 'ab_gate_x_w': 2.257489e-02, 'ab_gate_x_b': 1.156034e-02, 'ab_lambda': 1.545642e-02, 'mla_q_norm': 1.150554e-02, 'mla_kv_norm': 2.462087e-02, 'mla_w_uq': 6.322327e-03, 'mla_w_ukv': 8.096591e-03, 'ab_w_out': 4.569637e-02, 'ab_ln_g': 2.713709e+00, 'ab_ln_b': 4.411295e-01, 'ssd_w_in': 4.573934e-02, 'ssd_conv_w': 4.404421e-02, 'ssd_conv_b': 6.775505e-02, 'ssd_dt_bias': 1.114093e-01, 'ssd_a_log': 9.223959e-02, 'ssd_d': 2.766476e-01, 'ssd_norm': 5.117410e-02, 'ssd_w_out': 1.252889e-01, 'ssd_ln_g': 3.212553e+01, 'ssd_ln_b': 2.144839e+00}


def _to_microbatches(a, axis):
    t = _jnp.moveaxis(a, axis, 0)
    t = t.reshape((N_MICROBATCH, t.shape[0] // N_MICROBATCH) + t.shape[1:])
    return _jnp.moveaxis(t, 1, axis + 1)


def setup_inputs(seed: int = 0) -> dict:
    inp = _fwd_setup_inputs(seed)
    key = _jax.random.fold_in(_jax.random.key(seed), 7919)
    shape, _ = _output_shape()
    out = dict(inp)
    out["loss_target"] = _jax.random.normal(_jax.random.fold_in(key, 0), shape, _jnp.float32)
    for i, name in enumerate(TWIN_WEIGHTS):
        w = inp[name].astype(_jnp.float32)
        if MOMENT_SCALE is None:
            s = _jnp.sqrt(_jnp.mean(_jnp.square(w)) + 1e-30)
        else:
            s = MOMENT_SCALE[name]
        km, kv = _jax.random.split(_jax.random.fold_in(key, i + 1))
        out[name] = w
        out["m_" + name] = s * _jax.random.normal(km, w.shape, _jnp.float32)
        out["v_" + name] = (s * s) * _jax.random.uniform(kv, w.shape, _jnp.float32, 0.5, 1.5)
    if N_MICROBATCH > 1:
        for name, axis in PER_EXAMPLE_BATCH_AXIS.items():
            out[name] = _to_microbatches(out[name], axis)
    return {'x': out['x'], 'positions': out['positions'], 'ab_w_in': out['ab_w_in'], 'ab_conv_w': out['ab_conv_w'], 'ab_conv_b': out['ab_conv_b'], 'ab_gate_a_w': out['ab_gate_a_w'], 'ab_gate_a_b': out['ab_gate_a_b'], 'ab_gate_x_w': out['ab_gate_x_w'], 'ab_gate_x_b': out['ab_gate_x_b'], 'ab_lambda': out['ab_lambda'], 'mla_q_norm': out['mla_q_norm'], 'mla_kv_norm': out['mla_kv_norm'], 'mla_w_uq': out['mla_w_uq'], 'mla_w_ukv': out['mla_w_ukv'], 'ab_w_out': out['ab_w_out'], 'ab_ln_g': out['ab_ln_g'], 'ab_ln_b': out['ab_ln_b'], 'ssd_w_in': out['ssd_w_in'], 'ssd_conv_w': out['ssd_conv_w'], 'ssd_conv_b': out['ssd_conv_b'], 'ssd_dt_bias': out['ssd_dt_bias'], 'ssd_a_log': out['ssd_a_log'], 'ssd_d': out['ssd_d'], 'ssd_norm': out['ssd_norm'], 'ssd_w_out': out['ssd_w_out'], 'ssd_ln_g': out['ssd_ln_g'], 'ssd_ln_b': out['ssd_ln_b'], 'loss_target': out['loss_target'], 'm_ab_w_in': out['m_ab_w_in'], 'm_ab_conv_w': out['m_ab_conv_w'], 'm_ab_conv_b': out['m_ab_conv_b'], 'm_ab_gate_a_w': out['m_ab_gate_a_w'], 'm_ab_gate_a_b': out['m_ab_gate_a_b'], 'm_ab_gate_x_w': out['m_ab_gate_x_w'], 'm_ab_gate_x_b': out['m_ab_gate_x_b'], 'm_ab_lambda': out['m_ab_lambda'], 'm_mla_q_norm': out['m_mla_q_norm'], 'm_mla_kv_norm': out['m_mla_kv_norm'], 'm_mla_w_uq': out['m_mla_w_uq'], 'm_mla_w_ukv': out['m_mla_w_ukv'], 'm_ab_w_out': out['m_ab_w_out'], 'm_ab_ln_g': out['m_ab_ln_g'], 'm_ab_ln_b': out['m_ab_ln_b'], 'm_ssd_w_in': out['m_ssd_w_in'], 'm_ssd_conv_w': out['m_ssd_conv_w'], 'm_ssd_conv_b': out['m_ssd_conv_b'], 'm_ssd_dt_bias': out['m_ssd_dt_bias'], 'm_ssd_a_log': out['m_ssd_a_log'], 'm_ssd_d': out['m_ssd_d'], 'm_ssd_norm': out['m_ssd_norm'], 'm_ssd_w_out': out['m_ssd_w_out'], 'm_ssd_ln_g': out['m_ssd_ln_g'], 'm_ssd_ln_b': out['m_ssd_ln_b'], 'v_ab_w_in': out['v_ab_w_in'], 'v_ab_conv_w': out['v_ab_conv_w'], 'v_ab_conv_b': out['v_ab_conv_b'], 'v_ab_gate_a_w': out['v_ab_gate_a_w'], 'v_ab_gate_a_b': out['v_ab_gate_a_b'], 'v_ab_gate_x_w': out['v_ab_gate_x_w'], 'v_ab_gate_x_b': out['v_ab_gate_x_b'], 'v_ab_lambda': out['v_ab_lambda'], 'v_mla_q_norm': out['v_mla_q_norm'], 'v_mla_kv_norm': out['v_mla_kv_norm'], 'v_mla_w_uq': out['v_mla_w_uq'], 'v_mla_w_ukv': out['v_mla_w_ukv'], 'v_ab_w_out': out['v_ab_w_out'], 'v_ab_ln_g': out['v_ab_ln_g'], 'v_ab_ln_b': out['v_ab_ln_b'], 'v_ssd_w_in': out['v_ssd_w_in'], 'v_ssd_conv_w': out['v_ssd_conv_w'], 'v_ssd_conv_b': out['v_ssd_conv_b'], 'v_ssd_dt_bias': out['v_ssd_dt_bias'], 'v_ssd_a_log': out['v_ssd_a_log'], 'v_ssd_d': out['v_ssd_d'], 'v_ssd_norm': out['v_ssd_norm'], 'v_ssd_w_out': out['v_ssd_w_out'], 'v_ssd_ln_g': out['v_ssd_ln_g'], 'v_ssd_ln_b': out['v_ssd_ln_b']}


def _loss(weights, diff, rest, loss_target):
    with _jax.named_scope("forward"):
        args = {**rest, TWIN_DIFF_INPUT: diff, **{k: w.astype(_WEIGHT_DTYPES[k]) for k, w in weights.items()}}
        y = _forward(args)
    with _jax.named_scope("loss_head"):
        err = _jnp.square(y.astype(_jnp.float32) - loss_target)
        return 0.5 * _jnp.sum(_jnp.mean(err, axis=-1)) if err.ndim else 0.5 * err


def _adamw(w, g, m, v):
    m = ADAM_B1 * m + (1.0 - ADAM_B1) * g
    v = ADAM_B2 * v + (1.0 - ADAM_B2) * _jnp.square(g)
    m_hat = m / (1.0 - ADAM_B1 ** ADAM_STEP)
    v_hat = v / (1.0 - ADAM_B2 ** ADAM_STEP)
    delta = -ADAM_LR * (m_hat / (_jnp.sqrt(v_hat) + ADAM_EPS) + ADAM_WD * w)
    return delta, m, v


def reference(x, positions, ab_w_in, ab_conv_w, ab_conv_b, ab_gate_a_w, ab_gate_a_b, ab_gate_x_w, ab_gate_x_b, ab_lambda, mla_q_norm, mla_kv_norm, mla_w_uq, mla_w_ukv, ab_w_out, ab_ln_g, ab_ln_b, ssd_w_in, ssd_conv_w, ssd_conv_b, ssd_dt_bias, ssd_a_log, ssd_d, ssd_norm, ssd_w_out, ssd_ln_g, ssd_ln_b, loss_target, m_ab_w_in, m_ab_conv_w, m_ab_conv_b, m_ab_gate_a_w, m_ab_gate_a_b, m_ab_gate_x_w, m_ab_gate_x_b, m_ab_lambda, m_mla_q_norm, m_mla_kv_norm, m_mla_w_uq, m_mla_w_ukv, m_ab_w_out, m_ab_ln_g, m_ab_ln_b, m_ssd_w_in, m_ssd_conv_w, m_ssd_conv_b, m_ssd_dt_bias, m_ssd_a_log, m_ssd_d, m_ssd_norm, m_ssd_w_out, m_ssd_ln_g, m_ssd_ln_b, v_ab_w_in, v_ab_conv_w, v_ab_conv_b, v_ab_gate_a_w, v_ab_gate_a_b, v_ab_gate_x_w, v_ab_gate_x_b, v_ab_lambda, v_mla_q_norm, v_mla_kv_norm, v_mla_w_uq, v_mla_w_ukv, v_ab_w_out, v_ab_ln_g, v_ab_ln_b, v_ssd_w_in, v_ssd_conv_w, v_ssd_conv_b, v_ssd_dt_bias, v_ssd_a_log, v_ssd_d, v_ssd_norm, v_ssd_w_out, v_ssd_ln_g, v_ssd_ln_b):
    given = dict(x=x, positions=positions, ab_w_in=ab_w_in, ab_conv_w=ab_conv_w, ab_conv_b=ab_conv_b, ab_gate_a_w=ab_gate_a_w, ab_gate_a_b=ab_gate_a_b, ab_gate_x_w=ab_gate_x_w, ab_gate_x_b=ab_gate_x_b, ab_lambda=ab_lambda, mla_q_norm=mla_q_norm, mla_kv_norm=mla_kv_norm, mla_w_uq=mla_w_uq, mla_w_ukv=mla_w_ukv, ab_w_out=ab_w_out, ab_ln_g=ab_ln_g, ab_ln_b=ab_ln_b, ssd_w_in=ssd_w_in, ssd_conv_w=ssd_conv_w, ssd_conv_b=ssd_conv_b, ssd_dt_bias=ssd_dt_bias, ssd_a_log=ssd_a_log, ssd_d=ssd_d, ssd_norm=ssd_norm, ssd_w_out=ssd_w_out, ssd_ln_g=ssd_ln_g, ssd_ln_b=ssd_ln_b, loss_target=loss_target, m_ab_w_in=m_ab_w_in, m_ab_conv_w=m_ab_conv_w, m_ab_conv_b=m_ab_conv_b, m_ab_gate_a_w=m_ab_gate_a_w, m_ab_gate_a_b=m_ab_gate_a_b, m_ab_gate_x_w=m_ab_gate_x_w, m_ab_gate_x_b=m_ab_gate_x_b, m_ab_lambda=m_ab_lambda, m_mla_q_norm=m_mla_q_norm, m_mla_kv_norm=m_mla_kv_norm, m_mla_w_uq=m_mla_w_uq, m_mla_w_ukv=m_mla_w_ukv, m_ab_w_out=m_ab_w_out, m_ab_ln_g=m_ab_ln_g, m_ab_ln_b=m_ab_ln_b, m_ssd_w_in=m_ssd_w_in, m_ssd_conv_w=m_ssd_conv_w, m_ssd_conv_b=m_ssd_conv_b, m_ssd_dt_bias=m_ssd_dt_bias, m_ssd_a_log=m_ssd_a_log, m_ssd_d=m_ssd_d, m_ssd_norm=m_ssd_norm, m_ssd_w_out=m_ssd_w_out, m_ssd_ln_g=m_ssd_ln_g, m_ssd_ln_b=m_ssd_ln_b, v_ab_w_in=v_ab_w_in, v_ab_conv_w=v_ab_conv_w, v_ab_conv_b=v_ab_conv_b, v_ab_gate_a_w=v_ab_gate_a_w, v_ab_gate_a_b=v_ab_gate_a_b, v_ab_gate_x_w=v_ab_gate_x_w, v_ab_gate_x_b=v_ab_gate_x_b, v_ab_lambda=v_ab_lambda, v_mla_q_norm=v_mla_q_norm, v_mla_kv_norm=v_mla_kv_norm, v_mla_w_uq=v_mla_w_uq, v_mla_w_ukv=v_mla_w_ukv, v_ab_w_out=v_ab_w_out, v_ab_ln_g=v_ab_ln_g, v_ab_ln_b=v_ab_ln_b, v_ssd_w_in=v_ssd_w_in, v_ssd_conv_w=v_ssd_conv_w, v_ssd_conv_b=v_ssd_conv_b, v_ssd_dt_bias=v_ssd_dt_bias, v_ssd_a_log=v_ssd_a_log, v_ssd_d=v_ssd_d, v_ssd_norm=v_ssd_norm, v_ssd_w_out=v_ssd_w_out, v_ssd_ln_g=v_ssd_ln_g, v_ssd_ln_b=v_ssd_ln_b)
    weights = {n: given[n] for n in TWIN_WEIGHTS}
    shared = {n: given[n] for n in SHARED_INPUTS}
    per_example = {n: given[n] for n in ['x', 'positions']}
    grad_fn = _jax.value_and_grad(_loss, argnums=(0, 1))

    def one_microbatch(ex, loss_target):
        ex = dict(ex)
        diff = ex.pop(TWIN_DIFF_INPUT)
        return grad_fn(weights, diff, {**shared, **ex}, loss_target)

    if N_MICROBATCH == 1:
        loss, (grad_w, grad_x) = one_microbatch(per_example, given["loss_target"])
    else:
        def body(carry, xs):
            loss_sum, grad_sum = carry
            l_k, (gw_k, gx_k) = one_microbatch(xs[0], xs[1])
            with _jax.named_scope("update"):
                return (loss_sum + l_k, _jax.tree.map(_jnp.add, grad_sum, gw_k)), gx_k

        init = (_jnp.zeros((), _jnp.float32), _jax.tree.map(_jnp.zeros_like, weights))
        (loss, grad_w), grad_x = _jax.lax.scan(body, init, (per_example, given["loss_target"]))
    with _jax.named_scope("update"):
        delta_w, new_m, new_v = {}, {}, {}
        for n in TWIN_WEIGHTS:
            delta_w[n], new_m[n], new_v[n] = _adamw(weights[n], grad_w[n], given["m_" + n], given["v_" + n])
    return (loss, grad_x, *[grad_w[n] for n in TWIN_WEIGHTS], *[delta_w[n] for n in TWIN_WEIGHTS],
            *[new_m[n] for n in TWIN_WEIGHTS], *[new_v[n] for n in TWIN_WEIGHTS])
```

```python
import math

import jax
import jax.numpy as jnp
from jax import lax
from jax.experimental import pallas as pl
from jax.experimental.pallas import tpu as pltpu

F32 = jnp.float32
MXU_DTYPE = jnp.bfloat16

N_DEV = 8
SEQ = 4096
D_MODEL = 1024
DN_ALPHA = 4.0 ** 0.25
RNN_W = 512
MLA_HEADS = 8
ATT_SCALE = 96.0 ** -0.5
RG_C = 8.0
SSD_INNER = 2048
SSD_HEADS = 32
SSD_P = 64
SSD_GROUPS = 4
SSD_N = 128
SSD_L = 128
SSD_CONV = 3072
LANES = 128
SUBLANES = 8
VMEM_LIMIT = 56 * 1024 * 1024

ADAM_LR, ADAM_B1, ADAM_B2, ADAM_EPS, ADAM_WD, ADAM_STEP = 0.001, 0.9, 0.999, 1e-08, 0.01, 10

HIGHEST = lax.Precision.HIGHEST


def _params(sem, limit=VMEM_LIMIT):
    return pltpu.CompilerParams(dimension_semantics=sem, vmem_limit_bytes=limit)


def _dot(a, b):
    return lax.dot_general(a, b, (((1,), (0,)), ((), ())), preferred_element_type=F32)


def _dot_nt(a, b):
    return lax.dot_general(a, b, (((1,), (1,)), ((), ())), preferred_element_type=F32)


def _dot_tn(a, b):
    return lax.dot_general(a, b, (((0,), (0,)), ((), ())), preferred_element_type=F32)


def _dot_hi(a, b):
    return lax.dot_general(a, b, (((1,), (0,)), ((), ())), precision=HIGHEST, preferred_element_type=F32)


def _mx(v):
    return v.astype(MXU_DTYPE)


def _sigmoid(v):
    return 1.0 / (1.0 + jnp.exp(-v))


def _log1p_pos(e):
    poly = e * (1.0 - e * (0.5 - e * (1.0 / 3.0 - e * 0.25)))
    return jnp.where(e < 0.01, poly, jnp.log(1.0 + e))


def _softplus(v):
    return jnp.maximum(v, 0.0) + _log1p_pos(jnp.exp(-jnp.abs(v)))


def _neg_expm1(v):
    poly = -v * (1.0 + v * (0.5 + v * (1.0 / 6.0 + v * (1.0 / 24.0 + v * (1.0 / 120.0)))))
    return jnp.where(jnp.abs(v) < 0.1, poly, 1.0 - jnp.exp(v))


def _silu(v):
    return v * _sigmoid(v)


def _dsilu(v):
    s = _sigmoid(v)
    return s * (1.0 + v * (1.0 - s))


def _mm(a, b, mode, *, name, add=None, add_scale=1.0, out_dtype=F32, tm=512, tn=1024, tk=512):
    if mode == "tn":
        kdim, m = a.shape
        n = b.shape[1]
        tm, tn, tk = min(tm, m), min(tn, n), min(tk, kdim)

        def body_tn(a_ref, b_ref, o_ref):
            @pl.when(pl.program_id(2) == 0)
            def _():
                o_ref[...] = jnp.zeros_like(o_ref)

            o_ref[...] += _dot_tn(_mx(a_ref[...]), _mx(b_ref[...]))

        return pl.pallas_call(
            body_tn, name=name, grid=(m // tm, n // tn, kdim // tk),
            in_specs=[pl.BlockSpec((tk, tm), lambda i, j, k: (k, i)), pl.BlockSpec((tk, tn), lambda i, j, k: (k, j))],
            out_specs=pl.BlockSpec((tm, tn), lambda i, j, k: (i, j)),
            out_shape=jax.ShapeDtypeStruct((m, n), F32),
            compiler_params=_params(("parallel", "parallel", "arbitrary")),
        )(a, b)

    m, kdim = a.shape
    n = b.shape[1] if mode == "nn" else b.shape[0]
    tm, tn = min(tm, m), min(tn, n)
    has_add = add is not None

    def body(*refs):
        a_ref, b_ref = refs[0], refs[1]
        o_ref = refs[-1]
        av, bv = _mx(a_ref[...]), _mx(b_ref[...])
        acc = _dot(av, bv) if mode == "nn" else _dot_nt(av, bv)
        if has_add:
            acc = acc + add_scale * refs[2][...]
        o_ref[...] = acc.astype(out_dtype)

    b_spec = (pl.BlockSpec((kdim, tn), lambda i, j: (0, j)) if mode == "nn"
              else pl.BlockSpec((tn, kdim), lambda i, j: (j, 0)))
    in_specs = [pl.BlockSpec((tm, kdim), lambda i, j: (i, 0)), b_spec]
    args = [a, b]
    if has_add:
        in_specs.append(pl.BlockSpec((tm, tn), lambda i, j: (i, j)))
        args.append(add)
    return pl.pallas_call(
        body, name=name, grid=(m // tm, n // tn), in_specs=in_specs,
        out_specs=pl.BlockSpec((tm, tn), lambda i, j: (i, j)),
        out_shape=jax.ShapeDtypeStruct((m, n), out_dtype),
        compiler_params=_params(("parallel", "parallel")),
    )(*args)


def _shift_down(blk, halo, s):
    if s == 0:
        return blk
    t = blk.shape[0]
    r = pltpu.roll(blk, s, 0)
    hr = pltpu.roll(halo, s, 0)
    row8 = lax.broadcasted_iota(jnp.int32, hr.shape, 0)
    head = jnp.where(row8 < s, hr, r[:SUBLANES])
    return jnp.concatenate([head, r[SUBLANES:]], axis=0) if t > SUBLANES else head


def _shift_up(blk, halo, s):
    if s == 0:
        return blk
    t = blk.shape[0]
    r = pltpu.roll(blk, t - s, 0)
    hr = pltpu.roll(halo, SUBLANES - s, 0)
    row8 = lax.broadcasted_iota(jnp.int32, hr.shape, 0)
    tail = jnp.where(row8 >= SUBLANES - s, hr, r[t - SUBLANES:])
    return jnp.concatenate([r[:t - SUBLANES], tail], axis=0) if t > SUBLANES else tail


def _scan_down(a, u):
    t = a.shape[0]
    row = lax.broadcasted_iota(jnp.int32, a.shape, 0)
    d = 1
    while d < t:
        keep = row >= d
        a_sh = jnp.where(keep, pltpu.roll(a, d, 0), 1.0)
        u_sh = jnp.where(keep, pltpu.roll(u, d, 0), 0.0)
        u = a * u_sh + u
        a = a * a_sh
        d *= 2
    return a, u


def _scan_up(a, u):
    t = a.shape[0]
    row = lax.broadcasted_iota(jnp.int32, a.shape, 0)
    d = 1
    while d < t:
        keep = row < t - d
        a_sh = jnp.where(keep, pltpu.roll(a, t - d, 0), 1.0)
        u_sh = jnp.where(keep, pltpu.roll(u, t - d, 0), 0.0)
        u = a * u_sh + u
        a = a * a_sh
        d *= 2
    return a, u


def _conv4(blk, halo, cw, cb):
    out = cb + blk * cw[3:4]
    for k in range(3):
        out = out + _shift_down(blk, halo, 3 - k) * cw[k:k + 1]
    return out


RG_T = 512


def _rg_gates(xc, wa, ba, wx, bx, lam):
    xcb = _mx(xc)
    r = _sigmoid(_dot(xcb, wa) + ba)
    ig = _sigmoid(_dot(xcb, wx) + bx)
    sp = _softplus(-lam)
    la = (-RG_C * r) * sp
    a = jnp.exp(la)
    mult = jnp.sqrt(_neg_expm1(2.0 * la))
    return r, ig, sp, a, mult


def _rglru_fwd(proj0, cw8, cb, wa, ba, wx, bx, lam):
    t, w = RG_T, RNN_W
    nb = SEQ // t

    def body(x_ref, halo_ref, cw_ref, cb_ref, wa_ref, ba_ref, wx_ref, bx_ref, lam_ref, xc_ref, h_ref, carry):
        i = pl.program_id(0)

        @pl.when(i == 0)
        def _():
            carry[...] = jnp.zeros_like(carry)

        blk = x_ref[...]
        halo = jnp.where(i > 0, halo_ref[...], 0.0)
        xc = _conv4(blk, halo, cw_ref[...], cb_ref[...])
        _, ig, _, a, mult = _rg_gates(xc, wa_ref[...], ba_ref[...], wx_ref[...], bx_ref[...], lam_ref[...])
        u = mult * (ig * xc)
        big_a, big_u = _scan_down(a, u)
        h = big_a * carry[SUBLANES - 1:SUBLANES, :] + big_u
        carry[...] = h[t - SUBLANES:]
        xc_ref[...] = xc
        h_ref[...] = h

    vec = pl.BlockSpec((1, w), lambda i: (0, 0))
    mat = pl.BlockSpec((w, w), lambda i: (0, 0))
    return pl.pallas_call(
        body, name="rglru_fwd", grid=(nb,),
        in_specs=[pl.BlockSpec((t, w), lambda i: (i, 0)),
                  pl.BlockSpec((SUBLANES, w), lambda i: (jnp.maximum(i * (t // SUBLANES) - 1, 0), 0)),
                  pl.BlockSpec((SUBLANES, w), lambda i: (0, 0)), vec, mat, vec, mat, vec, vec],
        out_specs=[pl.BlockSpec((t, w), lambda i: (i, 0)), pl.BlockSpec((t, w), lambda i: (i, 0))],
        out_shape=[jax.ShapeDtypeStruct((SEQ, w), F32), jax.ShapeDtypeStruct((SEQ, w), F32)],
        scratch_shapes=[pltpu.VMEM((SUBLANES, w), F32)],
        compiler_params=_params(("arbitrary",)),
    )(proj0, proj0, cw8, cb, wa, ba, wx, bx, lam)


def _rglru_bwd(dh, xc, h, proj0, cw8, wa, ba, wx, bx, lam):
    t, w = RG_T, RNN_W
    nb = SEQ // t
    tb = t // SUBLANES

    def body(dh_ref, xc_ref, h_ref, hh_ref, x_ref, xh_ref, cw_ref, wa_ref, ba_ref, wx_ref, bx_ref, lam_ref,
             dx_ref, dwa_ref, dwx_ref, dvec_ref, gcarry, dxc_next):
        i = pl.program_id(0)
        rev = nb - 1 - i

        @pl.when(i == 0)
        def _():
            gcarry[...] = jnp.zeros_like(gcarry)
            dxc_next[...] = jnp.zeros_like(dxc_next)
            dwa_ref[...] = jnp.zeros_like(dwa_ref)
            dwx_ref[...] = jnp.zeros_like(dwx_ref)
            dvec_ref[...] = jnp.zeros_like(dvec_ref)

        xc = xc_ref[...]
        wa_v, wx_v = wa_ref[...], wx_ref[...]
        lam_v = lam_ref[...]
        r, ig, sp, a, mult = _rg_gates(xc, wa_v, ba_ref[...], wx_v, bx_ref[...], lam_v)
        dhv = dh_ref[...]
        big_a, big_u = _scan_up(a, a * dhv)
        gg = big_a * gcarry[0:1, :] + big_u
        g = dhv + _shift_up(gg, gcarry[...], 1)
        gcarry[...] = gg[:SUBLANES]
        hhalo = jnp.where(rev > 0, hh_ref[...], 0.0)
        da = g * _shift_down(h_ref[...], hhalo, 1)
        d_mult = g * (ig * xc)
        d_i = g * (mult * xc)
        dxc = g * (mult * ig)
        d_la = da * a - d_mult * (a * a) / mult
        d_r = d_la * (-RG_C * sp)
        d_sp = jnp.sum(d_la * (-RG_C * r), axis=0, keepdims=True)
        d_pa = d_r * r * (1.0 - r)
        d_px = d_i * ig * (1.0 - ig)
        d_pab, d_pxb = _mx(d_pa), _mx(d_px)
        dxc = dxc + _dot_nt(d_pab, wa_v) + _dot_nt(d_pxb, wx_v)
        xcb = _mx(xc)
        dwa_ref[...] += _dot_tn(xcb, d_pab)
        dwx_ref[...] += _dot_tn(xcb, d_pxb)
        dvec_ref[0:1, :] += jnp.sum(d_pa, axis=0, keepdims=True)
        dvec_ref[1:2, :] += jnp.sum(d_px, axis=0, keepdims=True)
        dvec_ref[2:3, :] += d_sp * (-_sigmoid(-lam_v))
        dvec_ref[3:4, :] += jnp.sum(dxc, axis=0, keepdims=True)
        xblk = x_ref[...]
        xhalo = jnp.where(rev > 0, xh_ref[...], 0.0)
        cw = cw_ref[...]
        dx = dxc * cw[3:4]
        nxt = dxc_next[...]
        for k in range(4):
            dvec_ref[4 + k:5 + k, :] += jnp.sum(dxc * _shift_down(xblk, xhalo, 3 - k), axis=0, keepdims=True)
            if k < 3:
                dx = dx + _shift_up(dxc, nxt, 3 - k) * cw[k:k + 1]
        dxc_next[...] = dxc[:SUBLANES]
        dx_ref[...] = dx

    blk = pl.BlockSpec((t, w), lambda i: (nb - 1 - i, 0))
    halo = pl.BlockSpec((SUBLANES, w), lambda i: (jnp.maximum((nb - 1 - i) * tb - 1, 0), 0))
    vec = pl.BlockSpec((1, w), lambda i: (0, 0))
    mat = pl.BlockSpec((w, w), lambda i: (0, 0))
    return pl.pallas_call(
        body, name="rglru_bwd", grid=(nb,),
        in_specs=[blk, blk, blk, halo, blk, halo, pl.BlockSpec((SUBLANES, w), lambda i: (0, 0)), mat, vec, mat, vec, vec],
        out_specs=[blk, mat, mat, pl.BlockSpec((16, w), lambda i: (0, 0))],
        out_shape=[jax.ShapeDtypeStruct((SEQ, w), F32), jax.ShapeDtypeStruct((w, w), F32),
                   jax.ShapeDtypeStruct((w, w), F32), jax.ShapeDtypeStruct((16, w), F32)],
        scratch_shapes=[pltpu.VMEM((SUBLANES, w), F32), pltpu.VMEM((SUBLANES, w), F32)],
        compiler_params=_params(("arbitrary",)),
    )(dh, xc, h, h, proj0, proj0, cw8, wa, ba, wx, bx, lam)


MLA_T = 512


def _rope(v, c, sa, sb):
    return v * c + pltpu.roll(v, LANES - 16, 1) * sa + pltpu.roll(v, 16, 1) * sb


def _rope_t(dv, c, sa, sb):
    return dv * c + pltpu.roll(dv * sa, 16, 1) + pltpu.roll(dv * sb, LANES - 16, 1)


def _rms(v, g, eps=1e-6):
    rs = lax.rsqrt(jnp.mean(v * v, axis=-1, keepdims=True) + eps)
    return v * rs * g, rs


def _mla_norm_fwd(proj0, q_norm, kv_norm, tc, tsa, tsb):
    t = MLA_T

    def body(cq_ref, ck_ref, qn_ref, kn_ref, c_ref, sa_ref, sb_ref, oq_ref, ok_ref, okr_ref):
        oq_ref[...] = _mx(_rms(cq_ref[...], qn_ref[...])[0])
        ck = ck_ref[...]
        ok_ref[...] = _mx(_rms(ck[:, :LANES], kn_ref[...])[0])
        okr_ref[...] = _rope(ck[:, LANES:], c_ref[...], sa_ref[...], sb_ref[...])

    tab = pl.BlockSpec((t, LANES), lambda i: (i, 0))
    return pl.pallas_call(
        body, name="mla_norm_fwd", grid=(SEQ // t,),
        in_specs=[pl.BlockSpec((t, 256), lambda i: (i, 6)), pl.BlockSpec((t, 256), lambda i: (i, 7)),
                  pl.BlockSpec((1, 256), lambda i: (0, 0)), pl.BlockSpec((1, LANES), lambda i: (0, 0)), tab, tab, tab],
        out_specs=[pl.BlockSpec((t, 256), lambda i: (i, 0)), tab, tab],
        out_shape=[jax.ShapeDtypeStruct((SEQ, 256), MXU_DTYPE), jax.ShapeDtypeStruct((SEQ, LANES), MXU_DTYPE),
                   jax.ShapeDtypeStruct((SEQ, LANES), F32)],
        compiler_params=_params(("parallel",)),
    )(proj0, proj0, q_norm, kv_norm, tc, tsa, tsb)


def _mla_assemble(qraw, kvraw, kr, tc, tsa, tsb):
    t = MLA_T

    def body(q_ref, k_ref, v_ref, kr_ref, c_ref, sa_ref, sb_ref, oq_ref, ok_ref, ov_ref):
        c, sa, sb, krv = c_ref[...], sa_ref[...], sb_ref[...], kr_ref[...]
        for hd in range(MLA_HEADS):
            sl = slice(hd * LANES, (hd + 1) * LANES)
            oq_ref[:, sl] = _mx(_rope(q_ref[:, sl], c, sa, sb))
            ok_ref[:, sl] = _mx(k_ref[:, sl] + krv)
        ov_ref[...] = _mx(v_ref[...])

    tab = pl.BlockSpec((t, LANES), lambda i: (i, 0))
    wide = pl.BlockSpec((t, 1024), lambda i: (i, 0))
    return pl.pallas_call(
        body, name="mla_assemble", grid=(SEQ // t,),
        in_specs=[wide, wide, pl.BlockSpec((t, 512), lambda i: (i, 2)), tab, tab, tab, tab],
        out_specs=[wide, wide, pl.BlockSpec((t, 512), lambda i: (i, 0))],
        out_shape=[jax.ShapeDtypeStruct((SEQ, 1024), MXU_DTYPE), jax.ShapeDtypeStruct((SEQ, 1024), MXU_DTYPE),
                   jax.ShapeDtypeStruct((SEQ, 512), MXU_DTYPE)],
        compiler_params=_params(("parallel",)),
    )(qraw, kvraw, kvraw, kr, tc, tsa, tsb)


ATT_T = 512


def _flash_fwd(q, k, v):
    t = ATT_T
    nb = SEQ // t

    def body(q_ref, k_ref, v_ref, o_ref, lse_ref, m_sc, l_sc, acc_sc):
        qi, ki = pl.program_id(1), pl.program_id(2)

        @pl.when(ki == 0)
        def _():
            m_sc[...] = jnp.full_like(m_sc, -jnp.inf)
            l_sc[...] = jnp.zeros_like(l_sc)
            acc_sc[...] = jnp.zeros_like(acc_sc)

        @pl.when(ki <= qi)
        def _():
            row = qi * t + lax.broadcasted_iota(jnp.int32, (t, t), 0)
            col = ki * t + lax.broadcasted_iota(jnp.int32, (t, t), 1)
            causal = col <= row
            vv = v_ref[...]
            lane_v = lax.broadcasted_iota(jnp.int32, vv.shape, 1)
            lane_o = lax.broadcasted_iota(jnp.int32, (t, LANES), 1)
            pv = jnp.zeros((t, LANES), F32)
            alphas = []
            for hd in range(2):
                sl = slice(hd * LANES, (hd + 1) * LANES)
                s = _dot_nt(q_ref[:, sl], k_ref[:, sl]) * ATT_SCALE
                s = jnp.where(causal, s, -jnp.inf)
                m_prev = m_sc[hd]
                m_new = jnp.maximum(m_prev, jnp.max(s, axis=1, keepdims=True))
                p = jnp.exp(s - m_new[:, :1])
                alpha = jnp.exp(m_prev - m_new)
                l_sc[hd] = alpha * l_sc[hd] + jnp.sum(p, axis=1, keepdims=True)
                m_sc[hd] = m_new
                alphas.append(alpha)
                vh = jnp.where((lane_v >= hd * 64) & (lane_v < (hd + 1) * 64), vv, jnp.zeros_like(vv))
                pv = pv + _dot(_mx(p), vh)
            acc_sc[...] = acc_sc[...] * jnp.where(lane_o < 64, alphas[0], alphas[1]) + pv

        @pl.when(ki == qi)
        def _():
            lane_o = lax.broadcasted_iota(jnp.int32, (t, LANES), 1)
            first = lane_o < 64
            l_pair = jnp.where(first, l_sc[0], l_sc[1])
            o_ref[...] = acc_sc[...] / l_pair
            lse_ref[0] = jnp.where(first, m_sc[0], m_sc[1]) + jnp.log(l_pair)

    return pl.pallas_call(
        body, name="flash_fwd", grid=(4, nb, nb),
        in_specs=[pl.BlockSpec((t, 256), lambda p, qi, ki: (qi, p)),
                  pl.BlockSpec((t, 256), lambda p, qi, ki: (jnp.minimum(ki, qi), p)),
                  pl.BlockSpec((t, LANES), lambda p, qi, ki: (jnp.minimum(ki, qi), p))],
        out_specs=[pl.BlockSpec((t, LANES), lambda p, qi, ki: (qi, p)),
                   pl.BlockSpec((1, t, LANES), lambda p, qi, ki: (p, qi, 0))],
        out_shape=[jax.ShapeDtypeStruct((SEQ, 512), F32), jax.ShapeDtypeStruct((4, SEQ, LANES), F32)],
        scratch_shapes=[pltpu.VMEM((2, t, LANES), F32), pltpu.VMEM((2, t, LANES), F32), pltpu.VMEM((t, LANES), F32)],
        compiler_params=_params(("parallel", "arbitrary", "arbitrary")),
    )(q, k, v)


def _flash_bwd(q, k, v, o, do, lse):
    t = ATT_T
    nb = SEQ // t

    def body(q_ref, k_ref, v_ref, o_ref, do_ref, lse_ref, dq_ref, dk_ref, dv_ref):
        ki, qi = pl.program_id(1), pl.program_id(2)

        @pl.when((ki == 0) & (qi == 0))
        def _():
            dq_ref[...] = jnp.zeros_like(dq_ref)

        @pl.when(qi == 0)
        def _():
            dk_ref[...] = jnp.zeros_like(dk_ref)
            dv_ref[...] = jnp.zeros_like(dv_ref)

        @pl.when(qi >= ki)
        def _():
            row = qi * t + lax.broadcasted_iota(jnp.int32, (t, t), 0)
            col = ki * t + lax.broadcasted_iota(jnp.int32, (t, t), 1)
            causal = col <= row
            dov, ov, vv, lse_v = do_ref[...], o_ref[...], v_ref[...], lse_ref[0]
            lane = lax.broadcasted_iota(jnp.int32, (t, LANES), 1)
            prod = dov * ov
            qrows = pl.ds(pl.multiple_of(qi * t, t), t)
            dv_acc = jnp.zeros((t, LANES), F32)
            for hd in range(2):
                sl = slice(hd * LANES, (hd + 1) * LANES)
                mine = (lane >= hd * 64) & (lane < (hd + 1) * 64)
                qh, kh = q_ref[:, sl], k_ref[:, sl]
                s = _dot_nt(qh, kh) * ATT_SCALE
                p = jnp.where(causal, jnp.exp(s - lse_v[:, hd * 64:hd * 64 + 1]), 0.0)
                do_h = jnp.where(mine, dov, 0.0)
                delta = jnp.sum(jnp.where(mine, prod, 0.0), axis=1, keepdims=True)
                dp = _dot_nt(_mx(do_h), vv)
                ds = _mx(p * (dp - delta) * ATT_SCALE)
                dv_acc = dv_acc + jnp.where(mine, _dot_tn(_mx(p), _mx(dov)), 0.0)
                dk_ref[:, sl] += _dot_tn(ds, qh)
                dq_ref[qrows, sl] += _dot(ds, kh)
            dv_ref[...] += dv_acc

    qmap = lambda p, ki, qi: (jnp.maximum(qi, ki), p)
    return pl.pallas_call(
        body, name="flash_bwd", grid=(4, nb, nb),
        in_specs=[pl.BlockSpec((t, 256), qmap), pl.BlockSpec((t, 256), lambda p, ki, qi: (ki, p)),
                  pl.BlockSpec((t, LANES), lambda p, ki, qi: (ki, p)), pl.BlockSpec((t, LANES), qmap),
                  pl.BlockSpec((t, LANES), qmap),
                  pl.BlockSpec((1, t, LANES), lambda p, ki, qi: (p, jnp.maximum(qi, ki), 0))],
        out_specs=[pl.BlockSpec((SEQ, 256), lambda p, ki, qi: (0, p)),
                   pl.BlockSpec((t, 256), lambda p, ki, qi: (ki, p)),
                   pl.BlockSpec((t, LANES), lambda p, ki, qi: (ki, p))],
        out_shape=[jax.ShapeDtypeStruct((SEQ, 1024), F32), jax.ShapeDtypeStruct((SEQ, 1024), F32),
                   jax.ShapeDtypeStruct((SEQ, 512), F32)],
        compiler_params=_params(("parallel", "arbitrary", "arbitrary")),
    )(q, k, v, o, do, lse)


def _mla_bwd_rope(dq, dk, dv, tc, tsa, tsb):
    t = MLA_T

    def body(dq_ref, dk_ref, dv_ref, c_ref, sa_ref, sb_ref, oq_ref, okv_ref, okr_ref):
        c, sa, sb = c_ref[...], sa_ref[...], sb_ref[...]
        lane = lax.broadcasted_iota(jnp.int32, (t, LANES), 1)
        dkr = jnp.zeros((t, LANES), F32)
        for hd in range(MLA_HEADS):
            sl = slice(hd * LANES, (hd + 1) * LANES)
            oq_ref[:, sl] = _mx(_rope_t(dq_ref[:, sl], c, sa, sb))
            dkh = dk_ref[:, sl]
            okv_ref[:, sl] = _mx(dkh)
            dkr = dkr + dkh
        okv_ref[:, 1024:] = _mx(dv_ref[...])
        dkr = jnp.where((lane >= 64) & (lane < 96), dkr, 0.0)
        okr_ref[...] = _rope_t(dkr, c, sa, sb)

    tab = pl.BlockSpec((t, LANES), lambda i: (i, 0))
    wide = pl.BlockSpec((t, 1024), lambda i: (i, 0))
    return pl.pallas_call(
        body, name="mla_bwd_rope", grid=(SEQ // t,),
        in_specs=[wide, wide, pl.BlockSpec((t, 512), lambda i: (i, 0)), tab, tab, tab],
        out_specs=[wide, pl.BlockSpec((t, 1536), lambda i: (i, 0)), tab],
        out_shape=[jax.ShapeDtypeStruct((SEQ, 1024), MXU_DTYPE), jax.ShapeDtypeStruct((SEQ, 1536), MXU_DTYPE),
                   jax.ShapeDtypeStruct((SEQ, LANES), F32)],
        compiler_params=_params(("parallel",)),
    )(dq, dk, dv, tc, tsa, tsb)


def _rms_bwd(v, g, dy, eps=1e-6):
    rs = lax.rsqrt(jnp.mean(v * v, axis=-1, keepdims=True) + eps)
    xh = v * rs
    dxh = dy * g
    dv = rs * (dxh - xh * jnp.mean(dxh * xh, axis=-1, keepdims=True))
    return dv, jnp.sum(dy * xh, axis=0, keepdims=True)


def _mla_norm_bwd(proj0, dqn, dkn, dkr, q_norm, kv_norm):
    t = MLA_T

    def body(cq_ref, ck_ref, dqn_ref, dkn_ref, dkr_ref, qn_ref, kn_ref, o_ref, dgq_ref, dgk_ref):
        @pl.when(pl.program_id(0) == 0)
        def _():
            dgq_ref[...] = jnp.zeros_like(dgq_ref)
            dgk_ref[...] = jnp.zeros_like(dgk_ref)

        dcq, dgq = _rms_bwd(cq_ref[...], qn_ref[...], dqn_ref[...])
        dck, dgk = _rms_bwd(ck_ref[:, :LANES], kn_ref[...], dkn_ref[...])
        o_ref[:, :256] = dcq
        o_ref[:, 256:384] = dck
        o_ref[:, 384:] = dkr_ref[...]
        dgq_ref[0:1, :] += dgq
        dgk_ref[0:1, :] += dgk

    tab = pl.BlockSpec((t, LANES), lambda i: (i, 0))
    return pl.pallas_call(
        body, name="mla_norm_bwd", grid=(SEQ // t,),
        in_specs=[pl.BlockSpec((t, 256), lambda i: (i, 6)), pl.BlockSpec((t, 256), lambda i: (i, 7)),
                  pl.BlockSpec((t, 256), lambda i: (i, 0)), tab, tab,
                  pl.BlockSpec((1, 256), lambda i: (0, 0)), pl.BlockSpec((1, LANES), lambda i: (0, 0))],
        out_specs=[pl.BlockSpec((t, 512), lambda i: (i, 0)), pl.BlockSpec((SUBLANES, 256), lambda i: (0, 0)),
                   pl.BlockSpec((SUBLANES, LANES), lambda i: (0, 0))],
        out_shape=[jax.ShapeDtypeStruct((SEQ, 512), F32), jax.ShapeDtypeStruct((SUBLANES, 256), F32),
                   jax.ShapeDtypeStruct((SUBLANES, LANES), F32)],
        compiler_params=_params(("arbitrary",)),
    )(proj0, proj0, dqn, dkn, dkr, q_norm, kv_norm)


LN_T = 512


def _ln(v, g, b, eps=1e-5):
    mu = jnp.mean(v, axis=-1, keepdims=True)
    xc = v - mu
    rs = lax.rsqrt(jnp.mean(xc * xc, axis=-1, keepdims=True) + eps)
    return xc * rs * g + b


def _ln_bwd(v, g, dy, eps=1e-5):
    mu = jnp.mean(v, axis=-1, keepdims=True)
    xc = v - mu
    rs = lax.rsqrt(jnp.mean(xc * xc, axis=-1, keepdims=True) + eps)
    xh = xc * rs
    dxh = dy * g
    dv = rs * (dxh - jnp.mean(dxh, axis=-1, keepdims=True) - xh * jnp.mean(dxh * xh, axis=-1, keepdims=True))
    return dv, jnp.sum(dy * xh, axis=0, keepdims=True), jnp.sum(dy, axis=0, keepdims=True)


def _l0_out(h, o, proj0, x, w_out, g, b):
    t = LN_T

    def body(h_ref, o_ref, ga_ref, gb_ref, x_ref, w_ref, g_ref, b_ref, y_ref, v_ref, x1_ref):
        y = _mx(jnp.concatenate([h_ref[...] * _silu(ga_ref[...]), o_ref[...] * _silu(gb_ref[...])], axis=1))
        v = DN_ALPHA * x_ref[...] + _dot(y, w_ref[...])
        y_ref[...] = y
        v_ref[...] = v
        x1_ref[...] = _ln(v, g_ref[...], b_ref[...])

    half = pl.BlockSpec((t, 512), lambda i: (i, 0))
    full = pl.BlockSpec((t, D_MODEL), lambda i: (i, 0))
    vec = pl.BlockSpec((1, D_MODEL), lambda i: (0, 0))
    return pl.pallas_call(
        body, name="l0_out", grid=(SEQ // t,),
        in_specs=[half, half, pl.BlockSpec((t, 512), lambda i: (i, 1)), pl.BlockSpec((t, 512), lambda i: (i, 2)), full,
                  pl.BlockSpec((D_MODEL, D_MODEL), lambda i: (0, 0)), vec, vec],
        out_specs=[full, full, full],
        out_shape=[jax.ShapeDtypeStruct((SEQ, D_MODEL), MXU_DTYPE), jax.ShapeDtypeStruct((SEQ, D_MODEL), F32),
                   jax.ShapeDtypeStruct((SEQ, D_MODEL), F32)],
        compiler_params=_params(("parallel",)),
    )(h, o, proj0, proj0, x, w_out, g, b)


def _ln_bwd_call(v, dy, g):
    t = LN_T

    def body(v_ref, dy_ref, g_ref, dv_ref, dgb_ref):
        @pl.when(pl.program_id(0) == 0)
        def _():
            dgb_ref[...] = jnp.zeros_like(dgb_ref)

        dv, dg, db = _ln_bwd(v_ref[...], g_ref[...], dy_ref[...])
        dv_ref[...] = dv
        dgb_ref[0:1, :] += dg
        dgb_ref[1:2, :] += db

    full = pl.BlockSpec((t, D_MODEL), lambda i: (i, 0))
    return pl.pallas_call(
        body, name="ln_bwd", grid=(SEQ // t,),
        in_specs=[full, full, pl.BlockSpec((1, D_MODEL), lambda i: (0, 0))],
        out_specs=[full, pl.BlockSpec((SUBLANES, D_MODEL), lambda i: (0, 0))],
        out_shape=[jax.ShapeDtypeStruct((SEQ, D_MODEL), F32), jax.ShapeDtypeStruct((SUBLANES, D_MODEL), F32)],
        compiler_params=_params(("arbitrary",)),
    )(v, dy, g)


def _gate_bwd(dy, h, o, proj0):
    t = LN_T

    def body(dya_ref, dyb_ref, h_ref, o_ref, ga_ref, gb_ref, dh_ref, do_ref, dg_ref):
        ga, gb, dya, dyb = ga_ref[...], gb_ref[...], dya_ref[...], dyb_ref[...]
        dh_ref[...] = dya * _silu(ga)
        do_ref[...] = dyb * _silu(gb)
        dg_ref[:, :512] = dya * h_ref[...] * _dsilu(ga)
        dg_ref[:, 512:] = dyb * o_ref[...] * _dsilu(gb)

    half = pl.BlockSpec((t, 512), lambda i: (i, 0))
    half1 = pl.BlockSpec((t, 512), lambda i: (i, 1))
    full = pl.BlockSpec((t, 1024), lambda i: (i, 0))
    return pl.pallas_call(
        body, name="gate_bwd", grid=(SEQ // t,),
        in_specs=[half, half1, half, half, half1, pl.BlockSpec((t, 512), lambda i: (i, 2))],
        out_specs=[half, half, full],
        out_shape=[jax.ShapeDtypeStruct((SEQ, 512), F32), jax.ShapeDtypeStruct((SEQ, 512), F32),
                   jax.ShapeDtypeStruct((SEQ, 1024), F32)],
        compiler_params=_params(("parallel",)),
    )(dy, dy, h, o, proj0, proj0)


CONV_T = 512
CONV_CB = 1024


def _ssd_conv_fwd(xbc, cw8, cb):
    t, cbk = CONV_T, CONV_CB
    tb = t // SUBLANES

    def body(x_ref, halo_ref, cw_ref, cb_ref, pre_ref, act_ref):
        halo = jnp.where(pl.program_id(1) > 0, halo_ref[...], 0.0)
        pre = _conv4(x_ref[...], halo, cw_ref[...], cb_ref[...])
        pre_ref[...] = pre
        act_ref[...] = _silu(pre)

    blk = pl.BlockSpec((t, cbk), lambda j, i: (i, j))
    return pl.pallas_call(
        body, name="ssd_conv_fwd", grid=(SSD_CONV // cbk, SEQ // t),
        in_specs=[blk, pl.BlockSpec((SUBLANES, cbk), lambda j, i: (jnp.maximum(i * tb - 1, 0), j)),
                  pl.BlockSpec((SUBLANES, cbk), lambda j, i: (0, j)), pl.BlockSpec((1, cbk), lambda j, i: (0, j))],
        out_specs=[blk, blk],
        out_shape=[jax.ShapeDtypeStruct((SEQ, SSD_CONV), F32), jax.ShapeDtypeStruct((SEQ, SSD_CONV), F32)],
        compiler_params=_params(("parallel", "parallel")),
    )(xbc, xbc, cw8, cb)


def _ssd_conv_bwd(dact, pre, xbc, cw8):
    t, cbk = CONV_T, CONV_CB
    tb = t // SUBLANES
    nb = SEQ // t

    def body(da_ref, dan_ref, pre_ref, pren_ref, x_ref, xh_ref, cw_ref, dx_ref, dcw_ref):
        i = pl.program_id(1)

        @pl.when(i == 0)
        def _():
            dcw_ref[...] = jnp.zeros_like(dcw_ref)

        dpre = da_ref[...] * _dsilu(pre_ref[...])
        dpre_next = jnp.where(i < nb - 1, dan_ref[...] * _dsilu(pren_ref[...]), 0.0)
        xblk = x_ref[...]
        xhalo = jnp.where(i > 0, xh_ref[...], 0.0)
        cw = cw_ref[...]
        dx = dpre * cw[3:4]
        for k in range(4):
            dcw_ref[k:k + 1, :] += jnp.sum(dpre * _shift_down(xblk, xhalo, 3 - k), axis=0, keepdims=True)
            if k < 3:
                dx = dx + _shift_up(dpre, dpre_next, 3 - k) * cw[k:k + 1]
        dcw_ref[4:5, :] += jnp.sum(dpre, axis=0, keepdims=True)
        dx_ref[...] = dx

    blk = pl.BlockSpec((t, cbk), lambda j, i: (i, j))
    nxt = pl.BlockSpec((SUBLANES, cbk), lambda j, i: (jnp.minimum((i + 1) * tb, SEQ // SUBLANES - 1), j))
    prv = pl.BlockSpec((SUBLANES, cbk), lambda j, i: (jnp.maximum(i * tb - 1, 0), j))
    acc = pl.BlockSpec((SUBLANES, cbk), lambda j, i: (0, j))
    return pl.pallas_call(
        body, name="ssd_conv_bwd", grid=(SSD_CONV // cbk, nb),
        in_specs=[blk, nxt, blk, nxt, blk, prv, acc],
        out_specs=[blk, acc],
        out_shape=[jax.ShapeDtypeStruct((SEQ, SSD_CONV), F32), jax.ShapeDtypeStruct((SUBLANES, SSD_CONV), F32)],
        compiler_params=_params(("parallel", "arbitrary")),
    )(dact, dact, pre, pre, xbc, xbc, cw8)


def _ssd_common(dt_raw, bias, alog, tril, expand, xs):
    lane = lax.broadcasted_iota(jnp.int32, dt_raw.shape, 1)
    dt = jnp.where(lane < SSD_HEADS, _softplus(dt_raw + bias), 0.0)
    a_neg = -jnp.exp(alog)
    cs = _dot_hi(tril, dt * a_neg)
    dt_x = _dot_hi(dt, expand)
    cs_x = _dot_hi(cs, expand)
    last = cs_x[SSD_L - 1:SSD_L, :]
    return dt, a_neg, cs, dt_x, cs_x, xs * dt_x, jnp.exp(last - cs_x), jnp.exp(cs_x), jnp.exp(last)


def _ssd_decay(cs, cs_t, hh, causal):
    seg = cs[:, hh:hh + 1] - cs_t[hh:hh + 1, :]
    return jnp.where(causal, jnp.exp(jnp.where(causal, seg, 0.0)), 0.0)


def _ssd_scan_fwd(act, dt_raw, bias, alog, d_x, tril, expand):
    nc = SEQ // SSD_L
    gw = SSD_INNER // SSD_GROUPS

    def body(act_ref, dt_ref, bias_ref, alog_ref, dx_ref, tril_ref, e_ref, y_ref, hp_ref, h_sc):
        @pl.when(pl.program_id(0) == 0)
        def _():
            h_sc[...] = jnp.zeros_like(h_sc)

        xs = act_ref[:, :SSD_INNER]
        _, _, cs, _, _, xdt, ds_x, ecs_x, elast = _ssd_common(
            dt_ref[...], bias_ref[...], alog_ref[...], tril_ref[...], e_ref[...], xs)
        cs_t = cs.T
        causal = (lax.broadcasted_iota(jnp.int32, (SSD_L, SSD_L), 0)
                  >= lax.broadcasted_iota(jnp.int32, (SSD_L, SSD_L), 1))
        lane = lax.broadcasted_iota(jnp.int32, (SSD_L, LANES), 1)
        xdt_b = _mx(xdt)
        xds_b = _mx(xdt * ds_x)
        hp_ref[0] = h_sc[...]
        for g in range(SSD_GROUPS):
            gs = slice(g * gw, (g + 1) * gw)
            bg = _mx(act_ref[:, SSD_INNER + g * SSD_N:SSD_INNER + (g + 1) * SSD_N])
            cg = _mx(act_ref[:, SSD_INNER + 512 + g * SSD_N:SSD_INNER + 512 + (g + 1) * SSD_N])
            cb = _dot_nt(cg, bg)
            hprev = h_sc[:, gs]
            yoff = _dot(cg, _mx(hprev)) * ecs_x[:, gs]
            h_sc[:, gs] = hprev * elast[:, gs] + _dot_tn(bg, xds_b[:, gs])
            for pr in range(4):
                ps = slice(g * gw + pr * LANES, g * gw + (pr + 1) * LANES)
                xp = xdt_b[:, ps]
                ydiag = jnp.zeros((SSD_L, LANES), F32)
                for j in range(2):
                    dm = _ssd_decay(cs, cs_t, g * 8 + pr * 2 + j, causal)
                    mine = (lane >= j * 64) & (lane < (j + 1) * 64)
                    ydiag = ydiag + _dot(_mx(cb * dm), jnp.where(mine, xp, jnp.zeros_like(xp)))
                y_ref[:, ps] = ydiag + yoff[:, pr * LANES:(pr + 1) * LANES] + dx_ref[:, ps] * xs[:, ps]

    const = lambda shape: pl.BlockSpec(shape, lambda c: (0, 0))
    return pl.pallas_call(
        body, name="ssd_scan_fwd", grid=(nc,),
        in_specs=[pl.BlockSpec((SSD_L, SSD_CONV), lambda c: (c, 0)), pl.BlockSpec((SSD_L, LANES), lambda c: (c, 0)),
                  const((1, LANES)), const((1, LANES)), const((1, SSD_INNER)), const((SSD_L, SSD_L)),
                  const((LANES, SSD_INNER))],
        out_specs=[pl.BlockSpec((SSD_L, SSD_INNER), lambda c: (c, 0)),
                   pl.BlockSpec((1, SSD_N, SSD_INNER), lambda c: (c, 0, 0))],
        out_shape=[jax.ShapeDtypeStruct((SEQ, SSD_INNER), F32), jax.ShapeDtypeStruct((nc, SSD_N, SSD_INNER), F32)],
        scratch_shapes=[pltpu.VMEM((SSD_N, SSD_INNER), F32)],
        compiler_params=_params(("arbitrary",)),
    )(act, dt_raw, bias, alog, d_x, tril, expand)


def _ssd_scan_bwd(dy, act, dt_raw, hprev_all, bias, alog, d_x, tril, expand, expand_t):
    nc = SEQ // SSD_L
    gw = SSD_INNER // SSD_GROUPS

    def body(dy_ref, act_ref, dt_ref, hp_ref, bias_ref, alog_ref, dx_ref, tril_ref, e_ref, et_ref,
             dact_ref, ddt_ref, dvec_ref, dh_sc, dd_sc):
        i = pl.program_id(0)

        @pl.when(i == 0)
        def _():
            dh_sc[...] = jnp.zeros_like(dh_sc)
            dd_sc[...] = jnp.zeros_like(dd_sc)
            dvec_ref[...] = jnp.zeros_like(dvec_ref)

        xs = act_ref[:, :SSD_INNER]
        dt_raw_v, bias_v = dt_ref[...], bias_ref[...]
        dt, a_neg, cs, dt_x, _, xdt, ds_x, ecs_x, elast = _ssd_common(
            dt_raw_v, bias_v, alog_ref[...], tril_ref[...], e_ref[...], xs)
        cs_t = cs.T
        rowi = lax.broadcasted_iota(jnp.int32, (SSD_L, SSD_L), 0)
        coli = lax.broadcasted_iota(jnp.int32, (SSD_L, SSD_L), 1)
        causal = rowi >= coli
        lane = lax.broadcasted_iota(jnp.int32, (SSD_L, LANES), 1)
        row_g = lax.broadcasted_iota(jnp.int32, (SSD_L, gw), 0)
        dyv = dy_ref[...]
        dd_sc[0:1, :] += jnp.sum(dyv * xs, axis=0, keepdims=True)
        xdt_b = _mx(xdt)
        xds = xdt * ds_x
        xds_b = _mx(xds)
        dy_b = _mx(dyv)
        dye_b = _mx(dyv * ecs_x)
        dcs = jnp.zeros((SSD_L, LANES), F32)
        dcs_t = jnp.zeros((LANES, SSD_L), F32)
        dcs_parts = []
        dxdt_parts = []
        for g in range(SSD_GROUPS):
            gs = slice(g * gw, (g + 1) * gw)
            bcol = slice(SSD_INNER + g * SSD_N, SSD_INNER + (g + 1) * SSD_N)
            ccol = slice(SSD_INNER + 512 + g * SSD_N, SSD_INNER + 512 + (g + 1) * SSD_N)
            bg, cg = _mx(act_ref[:, bcol]), _mx(act_ref[:, ccol])
            cb = _dot_nt(cg, bg)
            hp = hp_ref[0, :, gs]
            hp_b = _mx(hp)
            dh = dh_sc[:, gs]
            dh_b = _mx(dh)
            yoff = _dot(cg, hp_b) * ecs_x[:, gs]
            bdh = _dot(bg, dh_b)
            tt = xds[:, gs] * bdh
            last_row = (jnp.sum(tt, axis=0, keepdims=True)
                        + jnp.sum(dh * hp, axis=0, keepdims=True) * elast[:, gs])
            dcs_parts.append(dyv[:, gs] * yoff - tt + jnp.where(row_g == SSD_L - 1, last_row, 0.0))
            dc_g = _dot_nt(dye_b[:, gs], hp_b)
            db_g = _dot_nt(xds_b[:, gs], dh_b)
            dh_sc[:, gs] = _dot_tn(cg, dye_b[:, gs]) + dh * elast[:, gs]
            wsum = jnp.zeros((SSD_L, SSD_L), F32)
            dxdt_g = []
            for pr in range(4):
                ps = slice(g * gw + pr * LANES, g * gw + (pr + 1) * LANES)
                xp, dyp = xdt_b[:, ps], dy_b[:, ps]
                dxp = jnp.zeros((SSD_L, LANES), F32)
                for j in range(2):
                    hh = g * 8 + pr * 2 + j
                    dm = _ssd_decay(cs, cs_t, hh, causal)
                    mine = (lane >= j * 64) & (lane < (j + 1) * 64)
                    dy_h = jnp.where(mine, dyp, jnp.zeros_like(dyp))
                    wd = _dot_nt(dy_h, xp) * dm
                    wsum = wsum + wd
                    gmat = wd * cb
                    dcs = dcs + jnp.where(lane == hh, jnp.sum(gmat, axis=1, keepdims=True), 0.0)
                    dcs_t = dcs_t - jnp.where(rowi == hh, jnp.sum(gmat, axis=0, keepdims=True), 0.0)
                    dxp = dxp + _dot_tn(_mx(cb * dm), dy_h)
                dxdt_g.append(dxp)
            dxdt_parts.append(jnp.concatenate(dxdt_g, axis=1) + bdh * ds_x[:, gs])
            ws_b = _mx(wsum)
            dact_ref[:, ccol] = dc_g + _dot(ws_b, bg)
            dact_ref[:, bcol] = db_g + _dot_tn(ws_b, cg)
        dxdt = jnp.concatenate(dxdt_parts, axis=1)
        dcs_x = jnp.concatenate(dcs_parts, axis=1)
        et = et_ref[...]
        dcs_tot = dcs + dcs_t.T + _dot_hi(dcs_x, et)
        da_dt = _dot_hi((coli >= rowi).astype(F32), dcs_tot)
        ddt = da_dt * a_neg + _dot_hi(dxdt * xs, et)
        ddt_raw = ddt * _sigmoid(dt_raw_v + bias_v)
        ddt_ref[...] = ddt_raw
        dvec_ref[0:1, :] += jnp.sum(ddt_raw, axis=0, keepdims=True)
        dvec_ref[1:2, :] += jnp.sum(da_dt * dt, axis=0, keepdims=True) * a_neg
        dact_ref[:, :SSD_INNER] = dyv * dx_ref[...] + dxdt * dt_x

        @pl.when(i == nc - 1)
        def _():
            dvec_ref[2:3, :] = _dot_hi(dd_sc[...], et)[0:1, :]

    const = lambda shape: pl.BlockSpec(shape, lambda c: (0, 0))
    rev = lambda c: (nc - 1 - c, 0)
    return pl.pallas_call(
        body, name="ssd_scan_bwd", grid=(nc,),
        in_specs=[pl.BlockSpec((SSD_L, SSD_INNER), rev), pl.BlockSpec((SSD_L, SSD_CONV), rev),
                  pl.BlockSpec((SSD_L, LANES), rev),
                  pl.BlockSpec((1, SSD_N, SSD_INNER), lambda c: (nc - 1 - c, 0, 0)),
                  const((1, LANES)), const((1, LANES)), const((1, SSD_INNER)), const((SSD_L, SSD_L)),
                  const((LANES, SSD_INNER)), const((SSD_INNER, LANES))],
        out_specs=[pl.BlockSpec((SSD_L, SSD_CONV), rev), pl.BlockSpec((SSD_L, LANES), rev), const((SUBLANES, LANES))],
        out_shape=[jax.ShapeDtypeStruct((SEQ, SSD_CONV), F32), jax.ShapeDtypeStruct((SEQ, LANES), F32),
                   jax.ShapeDtypeStruct((SUBLANES, LANES), F32)],
        scratch_shapes=[pltpu.VMEM((SSD_N, SSD_INNER), F32), pltpu.VMEM((SUBLANES, SSD_INNER), F32)],
        compiler_params=_params(("arbitrary",)),
    )(dy, act, dt_raw, hprev_all, bias, alog, d_x, tril, expand, expand_t)


L1_T = 256


def _gated_norm(y, z, nw):
    y2 = y * _silu(z)
    gw = SSD_INNER // SSD_GROUPS
    outs, xhs, rss = [], [], []
    for g in range(SSD_GROUPS):
        gs = slice(g * gw, (g + 1) * gw)
        v = y2[:, gs]
        rs = lax.rsqrt(jnp.mean(v * v, axis=-1, keepdims=True) + 1e-6)
        xhs.append(v * rs)
        rss.append(rs)
        outs.append(v * rs * nw[:, gs])
    return outs, xhs, rss


def _l1_out(y, z, nw, w_out, x1, g, b, target):
    t = L1_T

    def body(y_ref, z_ref, nw_ref, w_ref, x1_ref, g_ref, b_ref, tg_ref, yn_ref, dv_ref, dgb_ref, loss_ref):
        @pl.when(pl.program_id(0) == 0)
        def _():
            dgb_ref[...] = jnp.zeros_like(dgb_ref)
            loss_ref[...] = jnp.zeros_like(loss_ref)

        outs, _, _ = _gated_norm(y_ref[...], z_ref[...], nw_ref[...])
        yn = _mx(jnp.concatenate(outs, axis=1))
        yn_ref[...] = yn
        v = DN_ALPHA * x1_ref[...] + _dot(yn, w_ref[...])
        gv = g_ref[...]
        err = _ln(v, gv, b_ref[...]) - tg_ref[...]
        rowsum = jnp.sum(err * err, axis=1, keepdims=True)
        loss_ref[...] += 0.5 * jnp.sum(rowsum, axis=0, keepdims=True) / D_MODEL
        dv, dg, db = _ln_bwd(v, gv, err / D_MODEL)
        dv_ref[...] = dv
        dgb_ref[0:1, :] += dg
        dgb_ref[1:2, :] += db

    wide = pl.BlockSpec((t, SSD_INNER), lambda i: (i, 0))
    full = pl.BlockSpec((t, D_MODEL), lambda i: (i, 0))
    vec = pl.BlockSpec((1, D_MODEL), lambda i: (0, 0))
    return pl.pallas_call(
        body, name="l1_out", grid=(SEQ // t,),
        in_specs=[wide, wide, pl.BlockSpec((1, SSD_INNER), lambda i: (0, 0)),
                  pl.BlockSpec((SSD_INNER, D_MODEL), lambda i: (0, 0)), full, vec, vec, full],
        out_specs=[wide, full, pl.BlockSpec((SUBLANES, D_MODEL), lambda i: (0, 0)),
                   pl.BlockSpec((SUBLANES, LANES), lambda i: (0, 0))],
        out_shape=[jax.ShapeDtypeStruct((SEQ, SSD_INNER), MXU_DTYPE), jax.ShapeDtypeStruct((SEQ, D_MODEL), F32),
                   jax.ShapeDtypeStruct((SUBLANES, D_MODEL), F32), jax.ShapeDtypeStruct((SUBLANES, LANES), F32)],
        compiler_params=_params(("arbitrary",)),
    )(y, z, nw, w_out, x1, g, b, target)


def _l1_gate_bwd(dyn, y, z, nw):
    t = L1_T
    gw = SSD_INNER // SSD_GROUPS

    def body(dyn_ref, y_ref, z_ref, nw_ref, dy_ref, dz_ref, dnw_ref):
        @pl.when(pl.program_id(0) == 0)
        def _():
            dnw_ref[...] = jnp.zeros_like(dnw_ref)

        yv, zv, nwv = y_ref[...], z_ref[...], nw_ref[...]
        _, xhs, rss = _gated_norm(yv, zv, nwv)
        sz, dsz = _silu(zv), _dsilu(zv)
        for g in range(SSD_GROUPS):
            gs = slice(g * gw, (g + 1) * gw)
            d_out = dyn_ref[:, gs]
            xh = xhs[g]
            dnw_ref[0:1, gs] += jnp.sum(d_out * xh, axis=0, keepdims=True)
            dxh = d_out * nwv[:, gs]
            dy2 = rss[g] * (dxh - xh * jnp.mean(dxh * xh, axis=-1, keepdims=True))
            dy_ref[:, gs] = dy2 * sz[:, gs]
            dz_ref[:, gs] = dy2 * yv[:, gs] * dsz[:, gs]

    wide = pl.BlockSpec((t, SSD_INNER), lambda i: (i, 0))
    return pl.pallas_call(
        body, name="l1_gate_bwd", grid=(SEQ // t,),
        in_specs=[wide, wide, wide, pl.BlockSpec((1, SSD_INNER), lambda i: (0, 0))],
        out_specs=[wide, wide, pl.BlockSpec((SUBLANES, SSD_INNER), lambda i: (0, 0))],
        out_shape=[jax.ShapeDtypeStruct((SEQ, SSD_INNER), F32), jax.ShapeDtypeStruct((SEQ, SSD_INNER), F32),
                   jax.ShapeDtypeStruct((SUBLANES, SSD_INNER), F32)],
        compiler_params=_params(("arbitrary",)),
    )(dyn, y, z, nw)


MESH = pl.DeviceIdType.MESH
ANY = pl.BlockSpec(memory_space=pl.ANY)


def _flip(v, bit):
    return 1 - v if bit else v


def _all_gather(x, name):
    r = x.shape[0]

    def body(x_ref, out_ref, send_sems, recv_sems, local_sem):
        mx, my, mc = lax.axis_index("x"), lax.axis_index("y"), lax.axis_index("c")
        me, sibling = (mx, my, mc), (mx, my, 1 - mc)
        chips = [(1 - mx, my), (mx, 1 - my), (1 - mx, 1 - my)]

        def slot(px, py, pc):
            return out_ref.at[4 * px + 2 * py + pc]

        def copy(k, block, to, src=None):
            return pltpu.make_async_remote_copy(
                src_ref=slot(*block) if src is None else src, dst_ref=slot(*block),
                send_sem=send_sems.at[k], recv_sem=recv_sems.at[k], device_id=to, device_id_type=MESH)

        mine = pltpu.make_async_copy(x_ref, slot(*me), local_sem)
        mine.start()
        first = [copy(0, me, sibling, src=x_ref)]
        first += [copy(1 + j, me, (*chip, mc), src=x_ref) for j, chip in enumerate(chips)]
        for cp in first:
            cp.start()
        passed = [copy(4 + j, (*chip, mc), sibling) for j, chip in enumerate(chips)]
        for j, chip in enumerate(chips):
            copy(1 + j, (*chip, mc), me).wait_recv()
            passed[j].start()
        copy(0, sibling, me).wait_recv()
        for j, chip in enumerate(chips):
            copy(4 + j, (*chip, 1 - mc), me).wait_recv()
        for cp in first + passed:
            cp.wait_send()
        mine.wait()

    return pl.pallas_call(
        body, name=name, in_specs=[ANY], out_specs=ANY,
        out_shape=jax.ShapeDtypeStruct((N_DEV, r, LANES), x.dtype),
        scratch_shapes=[pltpu.SemaphoreType.DMA((7,)), pltpu.SemaphoreType.DMA((7,)), pltpu.SemaphoreType.DMA(())],
    )(x)


def _all_to_all(buf, name):
    def body(in_ref, out_ref, send_sems, recv_sems, local_sem):
        mx, my, mc = lax.axis_index("x"), lax.axis_index("y"), lax.axis_index("c")
        me = 4 * mx + 2 * my + mc
        local = pltpu.make_async_copy(in_ref.at[me], out_ref.at[me], local_sem)
        local.start()
        copies = []
        for k in range(1, N_DEV):
            px, py, pc = _flip(mx, (k >> 2) & 1), _flip(my, (k >> 1) & 1), _flip(mc, k & 1)
            cp = pltpu.make_async_remote_copy(
                src_ref=in_ref.at[4 * px + 2 * py + pc], dst_ref=out_ref.at[me],
                send_sem=send_sems.at[k - 1], recv_sem=recv_sems.at[k - 1], device_id=(px, py, pc), device_id_type=MESH)
            cp.start()
            copies.append(cp)
        for cp in copies:
            cp.wait()
        local.wait()

    return pl.pallas_call(
        body, name=name, in_specs=[ANY], out_specs=ANY,
        out_shape=jax.ShapeDtypeStruct(buf.shape, buf.dtype),
        scratch_shapes=[pltpu.SemaphoreType.DMA((7,)), pltpu.SemaphoreType.DMA((7,)), pltpu.SemaphoreType.DMA(())],
    )(buf)


def _adamw(parts, w, m, v, name, tr):
    r = w.shape[0]

    def body(p_ref, w_ref, m_ref, v_ref, g_ref, d_ref, mo_ref, vo_ref):
        g = p_ref[0]
        for s in range(1, N_DEV):
            g = g + p_ref[s]
        mn = ADAM_B1 * m_ref[...] + (1.0 - ADAM_B1) * g
        vn = ADAM_B2 * v_ref[...] + (1.0 - ADAM_B2) * (g * g)
        m_hat = mn / (1.0 - ADAM_B1 ** ADAM_STEP)
        v_hat = vn / (1.0 - ADAM_B2 ** ADAM_STEP)
        g_ref[...] = g
        d_ref[...] = -ADAM_LR * (m_hat / (jnp.sqrt(v_hat) + ADAM_EPS) + ADAM_WD * w_ref[...])
        mo_ref[...] = mn
        vo_ref[...] = vn

    blk = pl.BlockSpec((tr, LANES), lambda i: (i, 0))
    out = jax.ShapeDtypeStruct((r, LANES), F32)
    return pl.pallas_call(
        body, name=name, grid=(r // tr,),
        in_specs=[pl.BlockSpec((N_DEV, tr, LANES), lambda i: (0, i, 0)), blk, blk, blk],
        out_specs=[blk, blk, blk, blk], out_shape=[out, out, out, out],
        compiler_params=_params(("parallel",)),
    )(parts, w, m, v)


BIG = (("ab_w_in", (1024, 244), 1), ("ssd_w_in", (1024, 644), 1), ("ab_w_out", (128, 1024), 0),
       ("ssd_w_out", (256, 1024), 0), ("mla_w_uq", (256, 96), 1), ("mla_w_ukv", (128, 128), 1))
SMALL = (("ab_conv_w", (4, 64), 1), ("ssd_conv_w", (4, 384), 1), ("ssd_conv_b", (384,), 0),
         ("ssd_norm", (256,), 0), ("ssd_ln_g", (128,), 0), ("ssd_ln_b", (128,), 0))
REPL = (("ab_conv_b", (512,)), ("ab_gate_a_w", (8, 64, 64)), ("ab_gate_a_b", (512,)), ("ab_gate_x_w", (8, 64, 64)),
        ("ab_gate_x_b", (512,)), ("ab_lambda", (512,)), ("mla_q_norm", (256,)), ("mla_kv_norm", (128,)),
        ("ab_ln_g", (1024,)), ("ab_ln_b", (1024,)), ("ssd_dt_bias", (32,)), ("ssd_a_log", (32,)), ("ssd_d", (32,)))
BIG_ROWS = sum(math.prod(s) for _, s, _ in BIG) // LANES
SMALL_ROWS = 24
SHARD_ROWS = 10752
SHARD_TR = 1536
REPL_ROWS = 552


def _pack_rows(pieces, rows, dtype):
    flat = jnp.concatenate([p.reshape(-1).astype(dtype) for p in pieces])
    return jnp.pad(flat, (0, rows * LANES - flat.shape[0])).reshape(rows, LANES)


def _unshard(flat8, off, shape, axis):
    n = math.prod(shape)
    sh = flat8[:, off:off + n].reshape((N_DEV,) + shape)
    if axis == 0:
        return sh.reshape((N_DEV * shape[0],) + shape[1:])
    return jnp.moveaxis(sh, 0, len(shape) - 1).reshape(shape[:-1] + (N_DEV * shape[-1],))


def _to_shards(full, shape, axis):
    if axis == 0:
        return full.reshape(N_DEV, -1)
    sh = full.reshape(shape[:-1] + (N_DEV, shape[-1]))
    return jnp.moveaxis(sh, len(shape) - 1, 0).reshape(N_DEV, -1)


def _pad128(v):
    flat = v.reshape(-1)
    return jnp.pad(flat, (0, (-flat.shape[0]) % LANES))


def _block_diag(w):
    eye = jnp.eye(8, dtype=w.dtype)
    return (eye[:, None, :, None] * w[:, :, None, :]).reshape(RNN_W, RNN_W)


def _block_diag_t(d):
    d4 = d.reshape(8, 64, 8, 64)
    return jnp.stack([d4[hd, :, hd, :] for hd in range(8)])


def _row(v, width=None):
    v = v.reshape(1, -1)
    return v if width is None else jnp.pad(v, ((0, 0), (0, width - v.shape[1])))


def _taps8(w):
    return jnp.pad(w, ((0, 4), (0, 0)))


def kernel(x, positions, ab_w_in, ab_conv_w, ab_conv_b, ab_gate_a_w, ab_gate_a_b, ab_gate_x_w, ab_gate_x_b, ab_lambda, mla_q_norm, mla_kv_norm, mla_w_uq, mla_w_ukv, ab_w_out, ab_ln_g, ab_ln_b, ssd_w_in, ssd_conv_w, ssd_conv_b, ssd_dt_bias, ssd_a_log, ssd_d, ssd_norm, ssd_w_out, ssd_ln_g, ssd_ln_b, loss_target, m_ab_w_in, m_ab_conv_w, m_ab_conv_b, m_ab_gate_a_w, m_ab_gate_a_b, m_ab_gate_x_w, m_ab_gate_x_b, m_ab_lambda, m_mla_q_norm, m_mla_kv_norm, m_mla_w_uq, m_mla_w_ukv, m_ab_w_out, m_ab_ln_g, m_ab_ln_b, m_ssd_w_in, m_ssd_conv_w, m_ssd_conv_b, m_ssd_dt_bias, m_ssd_a_log, m_ssd_d, m_ssd_norm, m_ssd_w_out, m_ssd_ln_g, m_ssd_ln_b, v_ab_w_in, v_ab_conv_w, v_ab_conv_b, v_ab_gate_a_w, v_ab_gate_a_b, v_ab_gate_x_w, v_ab_gate_x_b, v_ab_lambda, v_mla_q_norm, v_mla_kv_norm, v_mla_w_uq, v_mla_w_ukv, v_ab_w_out, v_ab_ln_g, v_ab_ln_b, v_ssd_w_in, v_ssd_conv_w, v_ssd_conv_b, v_ssd_dt_bias, v_ssd_a_log, v_ssd_d, v_ssd_norm, v_ssd_w_out, v_ssd_ln_g, v_ssd_ln_b):
    args = dict(locals())
    w_in = {n: args[n][0] for n, *_ in BIG + SMALL + REPL}
    m_in = {n: args["m_" + n][0] for n, *_ in BIG + SMALL + REPL}
    v_in = {n: args["v_" + n][0] for n, *_ in BIG + SMALL + REPL}

    big8 = _all_gather(_pack_rows([w_in[n] for n, *_ in BIG], BIG_ROWS, MXU_DTYPE), "gather_big").reshape(N_DEV, -1)
    small8 = _all_gather(_pack_rows([w_in[n] for n, *_ in SMALL], SMALL_ROWS, F32), "gather_small").reshape(N_DEV, -1)
    full, off = {}, 0
    for n, shape, axis in BIG:
        full[n] = _unshard(big8, off, shape, axis)
        off += math.prod(shape)
    off = 0
    for n, shape, axis in SMALL:
        full[n] = _unshard(small8, off, shape, axis)
        off += math.prod(shape)

    grads, loss_part, grad_x = _local_step(x[0], positions[0], loss_target[0], full, w_in)

    send = jnp.concatenate([_to_shards(grads[n], shape, axis) for n, shape, axis in BIG + SMALL], axis=1)
    send = jnp.pad(send, ((0, 0), (0, SHARD_ROWS * LANES - send.shape[1]))).reshape(N_DEV, SHARD_ROWS, LANES)
    recv = _all_to_all(send, "exchange_grads")
    packs = [_pack_rows([src[n] for n, *_ in BIG + SMALL], SHARD_ROWS, F32) for src in (w_in, m_in, v_in)]
    res_s = [r.reshape(-1) for r in _adamw(recv, *packs, "adamw_sharded", SHARD_TR)]

    rparts = _all_gather(_pack_rows([_pad128(grads[n]) for n, _ in REPL], REPL_ROWS, F32), "gather_repl_grads")
    packs = [_pack_rows([_pad128(src[n]) for n, _ in REPL], REPL_ROWS, F32) for src in (w_in, m_in, v_in)]
    res_r = [r.reshape(-1) for r in _adamw(rparts, *packs, "adamw_repl", REPL_ROWS)]

    outs = {}
    off = 0
    for n, shape, _ in BIG + SMALL:
        size = math.prod(shape)
        for kind, flat in zip(("grad", "delta", "new_m", "new_v"), res_s):
            outs[kind, n] = flat[off:off + size].reshape(args[n].shape)
        off += size
    off = 0
    for n, shape in REPL:
        size = math.prod(shape)
        for kind, flat in zip(("grad", "delta", "new_m", "new_v"), res_r):
            outs[kind, n] = flat[off:off + size].reshape(args[n].shape)
        off += size + (-size) % LANES

    loss = lax.psum(loss_part, ("x", "y", "c"))
    order = ["ab_w_in", "ab_conv_w", "ab_conv_b", "ab_gate_a_w", "ab_gate_a_b", "ab_gate_x_w", "ab_gate_x_b",
             "ab_lambda", "mla_q_norm", "mla_kv_norm", "mla_w_uq", "mla_w_ukv", "ab_w_out", "ab_ln_g", "ab_ln_b",
             "ssd_w_in", "ssd_conv_w", "ssd_conv_b", "ssd_dt_bias", "ssd_a_log", "ssd_d", "ssd_norm", "ssd_w_out",
             "ssd_ln_g", "ssd_ln_b"]
    return (loss, grad_x[None], *[outs[kind, n] for kind in ("grad", "delta", "new_m", "new_v") for n in order])


def _local_step(x, pos, target, full, rep):
    bf = MXU_DTYPE
    inv_freq = 10000.0 ** (-jnp.arange(0, 32, 2, dtype=F32) / 32)
    ang = pos.astype(F32)[:, None] * inv_freq
    cos, sin = jnp.cos(ang), jnp.sin(ang)
    zeros = lambda n: jnp.zeros((SEQ, n), F32)
    tc = jnp.concatenate([jnp.ones((SEQ, 64), F32), cos, cos, zeros(32)], axis=1)
    tsa = jnp.concatenate([zeros(64), -sin, zeros(48)], axis=1)
    tsb = jnp.concatenate([zeros(80), sin, zeros(32)], axis=1)

    w0 = full["ab_w_in"]
    w0p = jnp.concatenate([w0[:, :1920], jnp.zeros((D_MODEL, 64), bf), w0[:, 1920:], jnp.zeros((D_MODEL, 32), bf)], axis=1)
    wq = jnp.pad(full["mla_w_uq"].reshape(256, 8, 96), ((0, 0), (0, 0), (0, 32))).reshape(256, 1024)
    wkv3 = full["mla_w_ukv"].reshape(128, 8, 128)
    wkv = jnp.concatenate([jnp.pad(wkv3[:, :, :64], ((0, 0), (0, 0), (0, 64))).reshape(128, 1024),
                           wkv3[:, :, 64:].reshape(128, 512)], axis=1)
    w1 = full["ssd_w_in"]
    w1z, w1x, w1d = w1[:, :2048], w1[:, 2048:5120], jnp.pad(w1[:, 5120:], ((0, 0), (0, 96)))
    wo0, wo1 = full["ab_w_out"], full["ssd_w_out"]
    wa, wxg = _block_diag(rep["ab_gate_a_w"]).astype(bf), _block_diag(rep["ab_gate_x_w"]).astype(bf)
    cw0, cb0 = _taps8(full["ab_conv_w"]), _row(rep["ab_conv_b"])
    ba, bx, lam = _row(rep["ab_gate_a_b"]), _row(rep["ab_gate_x_b"]), _row(rep["ab_lambda"])
    qn_w, kn_w = _row(rep["mla_q_norm"]), _row(rep["mla_kv_norm"])
    g0, b0 = _row(rep["ab_ln_g"]), _row(rep["ab_ln_b"])
    cw1, cb1 = _taps8(full["ssd_conv_w"]), _row(full["ssd_conv_b"])
    dt_bias, a_log = _row(rep["ssd_dt_bias"], LANES), _row(rep["ssd_a_log"], LANES)
    d_x = _row(jnp.repeat(rep["ssd_d"], SSD_P))
    nw, g1, b1 = _row(full["ssd_norm"]), _row(full["ssd_ln_g"]), _row(full["ssd_ln_b"])
    tril = jnp.tril(jnp.ones((SSD_L, SSD_L), F32))
    expand = (jnp.arange(LANES)[:, None] == jnp.arange(SSD_INNER)[None, :] // SSD_P).astype(F32)
    expand_t = expand.T

    proj0 = _mm(x, w0p, "nn", name="l0_in")
    xc, h = _rglru_fwd(proj0, cw0, cb0, wa, ba, wxg, bx, lam)
    qn, kn, kr = _mla_norm_fwd(proj0, qn_w, kn_w, tc, tsa, tsb)
    qraw = _mm(qn, wq, "nn", name="mla_q")
    kvraw = _mm(kn, wkv, "nn", name="mla_kv", tn=512)
    qc, kc, vc = _mla_assemble(qraw, kvraw, kr, tc, tsa, tsb)
    o, lse = _flash_fwd(qc, kc, vc)
    y0, v0, x1 = _l0_out(h, o, proj0, x, wo0, g0, b0)

    z = _mm(x1, w1z, "nn", name="l1_in_z")
    xbc = _mm(x1, w1x, "nn", name="l1_in_xbc")
    dt_raw = _mm(x1, w1d, "nn", name="l1_in_dt")
    pre, act = _ssd_conv_fwd(xbc, cw1, cb1)
    ys, hprev = _ssd_scan_fwd(act, dt_raw, dt_bias, a_log, d_x, tril, expand)
    yn, dv1, dgb1, loss8 = _l1_out(ys, z, nw, wo1, x1, g1, b1, target)

    g_wo1 = _mm(yn, dv1, "tn", name="l1_dwout")
    dyn = _mm(dv1, wo1, "nt", name="l1_dyn", tn=1024)
    dys, dz, dnw = _l1_gate_bwd(dyn, ys, z, nw)
    dact, ddt_raw, dvec1 = _ssd_scan_bwd(dys, act, dt_raw, hprev, dt_bias, a_log, d_x, tril, expand, expand_t)
    dxbc, dcw1 = _ssd_conv_bwd(dact, pre, xbc, cw1)
    g_w1 = jnp.concatenate([_mm(x1, dz, "tn", name="l1_dw_z"), _mm(x1, dxbc, "tn", name="l1_dw_xbc"),
                            _mm(x1, ddt_raw, "tn", name="l1_dw_dt")[:, :32]], axis=1)
    dx1 = _mm(dz, w1z, "nt", name="l1_dx_z", add=dv1, add_scale=DN_ALPHA)
    dx1 = _mm(dxbc, w1x, "nt", name="l1_dx_xbc", add=dx1)
    dx1 = _mm(ddt_raw, w1d, "nt", name="l1_dx_dt", add=dx1)

    dv0, dgb0 = _ln_bwd_call(v0, dx1, g0)
    g_wo0 = _mm(y0, dv0, "tn", name="l0_dwout")
    dy0 = _mm(dv0, wo0, "nt", name="l0_dy")
    dh, do, dgate = _gate_bwd(dy0, h, o, proj0)
    dq, dk, dvv = _flash_bwd(qc, kc, vc, o, do, lse)
    dqraw, dkvraw, dkr = _mla_bwd_rope(dq, dk, dvv, tc, tsa, tsb)
    g_wq = _mm(qn, dqraw, "tn", name="mla_dwq", tm=256)
    g_wkv = _mm(kn, dkvraw, "tn", name="mla_dwkv", tm=128, tn=512)
    dqn = _mm(dqraw, wq, "nt", name="mla_dqn", tn=256)
    dkn = _mm(dkvraw, wkv, "nt", name="mla_dkn", tn=128)
    dtail, dqnw, dknw = _mla_norm_bwd(proj0, dqn, dkn, dkr, qn_w, kn_w)
    dxr, g_wa, g_wx, dvec0 = _rglru_bwd(dh, xc, h, proj0, cw0, wa, ba, wxg, bx, lam)
    g_tail = _mm(x, dtail, "tn", name="l0_dw_tail")
    g_w0 = jnp.concatenate([_mm(x, dxr, "tn", name="l0_dw_rnn"), _mm(x, dgate, "tn", name="l0_dw_gate"),
                            g_tail[:, :384], g_tail[:, 448:480]], axis=1)
    dx = _mm(dxr, w0p[:, :512], "nt", name="l0_dx_rnn", add=dv0, add_scale=DN_ALPHA)
    dx = _mm(dgate, w0p[:, 512:1536], "nt", name="l0_dx_gate", add=dx)
    dx = _mm(dtail, w0p[:, 1536:], "nt", name="l0_dx_tail", add=dx)

    g_wkv3 = g_wkv[:, :1024].reshape(128, 8, 128)[:, :, :64]
    grads = {
        "ab_w_in": g_w0, "ssd_w_in": g_w1, "ab_w_out": g_wo0, "ssd_w_out": g_wo1,
        "mla_w_uq": g_wq.reshape(256, 8, 128)[:, :, :96].reshape(256, 768),
        "mla_w_ukv": jnp.concatenate([g_wkv3, g_wkv[:, 1024:].reshape(128, 8, 64)], axis=2).reshape(128, 1024),
        "ab_conv_w": dvec0[4:8], "ssd_conv_w": dcw1[0:4], "ssd_conv_b": dcw1[4], "ssd_norm": dnw[0],
        "ssd_ln_g": dgb1[0], "ssd_ln_b": dgb1[1],
        "ab_conv_b": dvec0[3], "ab_gate_a_w": _block_diag_t(g_wa), "ab_gate_a_b": dvec0[0],
        "ab_gate_x_w": _block_diag_t(g_wx), "ab_gate_x_b": dvec0[1], "ab_lambda": dvec0[2],
        "mla_q_norm": dqnw[0], "mla_kv_norm": dknw[0], "ab_ln_g": dgb0[0], "ab_ln_b": dgb0[1],
        "ssd_dt_bias": dvec1[0, :32], "ssd_a_log": dvec1[1, :32], "ssd_d": dvec1[2, :32],
    }
    return grads, loss8[0, 0], dx
```

```python
import math

import jax
import jax.numpy as jnp
from jax import lax
from jax.experimental import pallas as pl
from jax.experimental.pallas import tpu as pltpu

F32 = jnp.float32
MXU_DTYPE = jnp.bfloat16

N_DEV = 8
SEQ = 4096
D_MODEL = 1024
DN_ALPHA = 4.0 ** 0.25
RNN_W = 512
MLA_HEADS = 8
ATT_SCALE = 96.0 ** -0.5
RG_C = 8.0
SSD_INNER = 2048
SSD_HEADS = 32
SSD_P = 64
SSD_GROUPS = 4
SSD_N = 128
SSD_L = 128
SSD_CONV = 3072
LANES = 128
SUBLANES = 8
VMEM_LIMIT = 56 * 1024 * 1024

ADAM_LR, ADAM_B1, ADAM_B2, ADAM_EPS, ADAM_WD, ADAM_STEP = 0.001, 0.9, 0.999, 1e-08, 0.01, 10

HIGHEST = lax.Precision.HIGHEST


def _params(sem, limit=VMEM_LIMIT):
    return pltpu.CompilerParams(dimension_semantics=sem, vmem_limit_bytes=limit)


def _dot(a, b):
    return lax.dot_general(a, b, (((1,), (0,)), ((), ())), preferred_element_type=F32)


def _dot_nt(a, b):
    return lax.dot_general(a, b, (((1,), (1,)), ((), ())), preferred_element_type=F32)


def _dot_tn(a, b):
    return lax.dot_general(a, b, (((0,), (0,)), ((), ())), preferred_element_type=F32)


def _dot_hi(a, b):
    return lax.dot_general(a, b, (((1,), (0,)), ((), ())), precision=HIGHEST, preferred_element_type=F32)


def _mx(v):
    return v.astype(MXU_DTYPE)


def _sigmoid(v):
    return 1.0 / (1.0 + jnp.exp(-v))


def _log1p_pos(e):
    poly = e * (1.0 - e * (0.5 - e * (1.0 / 3.0 - e * 0.25)))
    return jnp.where(e < 0.01, poly, jnp.log(1.0 + e))


def _softplus(v):
    return jnp.maximum(v, 0.0) + _log1p_pos(jnp.exp(-jnp.abs(v)))


def _neg_expm1(v):
    poly = -v * (1.0 + v * (0.5 + v * (1.0 / 6.0 + v * (1.0 / 24.0 + v * (1.0 / 120.0)))))
    return jnp.where(jnp.abs(v) < 0.1, poly, 1.0 - jnp.exp(v))


def _silu(v):
    return v * _sigmoid(v)


def _dsilu(v):
    s = _sigmoid(v)
    return s * (1.0 + v * (1.0 - s))


def _mm(a, b, mode, *, name, add=None, add_scale=1.0, out_dtype=F32, tm=512, tn=1024, tk=512):
    if mode == "tn":
        kdim, m = a.shape
        n = b.shape[1]
        tm, tn, tk = min(tm, m), min(tn, n), min(tk, kdim)

        def body_tn(a_ref, b_ref, o_ref):
            @pl.when(pl.program_id(2) == 0)
            def _():
                o_ref[...] = jnp.zeros_like(o_ref)

            o_ref[...] += _dot_tn(_mx(a_ref[...]), _mx(b_ref[...]))

        return pl.pallas_call(
            body_tn, name=name, grid=(m // tm, n // tn, kdim // tk),
            in_specs=[pl.BlockSpec((tk, tm), lambda i, j, k: (k, i)), pl.BlockSpec((tk, tn), lambda i, j, k: (k, j))],
            out_specs=pl.BlockSpec((tm, tn), lambda i, j, k: (i, j)),
            out_shape=jax.ShapeDtypeStruct((m, n), F32),
            compiler_params=_params(("parallel", "parallel", "arbitrary")),
        )(a, b)

    m, kdim = a.shape
    n = b.shape[1] if mode == "nn" else b.shape[0]
    tm, tn = min(tm, m), min(tn, n)
    has_add = add is not None

    def body(*refs):
        a_ref, b_ref = refs[0], refs[1]
        o_ref = refs[-1]
        av, bv = _mx(a_ref[...]), _mx(b_ref[...])
        acc = _dot(av, bv) if mode == "nn" else _dot_nt(av, bv)
        if has_add:
            acc = acc + add_scale * refs[2][...]
        o_ref[...] = acc.astype(out_dtype)

    b_spec = (pl.BlockSpec((kdim, tn), lambda i, j: (0, j)) if mode == "nn"
              else pl.BlockSpec((tn, kdim), lambda i, j: (j, 0)))
    in_specs = [pl.BlockSpec((tm, kdim), lambda i, j: (i, 0)), b_spec]
    args = [a, b]
    if has_add:
        in_specs.append(pl.BlockSpec((tm, tn), lambda i, j: (i, j)))
        args.append(add)
    return pl.pallas_call(
        body, name=name, grid=(m // tm, n // tn), in_specs=in_specs,
        out_specs=pl.BlockSpec((tm, tn), lambda i, j: (i, j)),
        out_shape=jax.ShapeDtypeStruct((m, n), out_dtype),
        compiler_params=_params(("parallel", "parallel")),
    )(*args)


def _shift_down(blk, halo, s):
    if s == 0:
        return blk
    t = blk.shape[0]
    r = pltpu.roll(blk, s, 0)
    hr = pltpu.roll(halo, s, 0)
    row8 = lax.broadcasted_iota(jnp.int32, hr.shape, 0)
    head = jnp.where(row8 < s, hr, r[:SUBLANES])
    return jnp.concatenate([head, r[SUBLANES:]], axis=0) if t > SUBLANES else head


def _shift_up(blk, halo, s):
    if s == 0:
        return blk
    t = blk.shape[0]
    r = pltpu.roll(blk, t - s, 0)
    hr = pltpu.roll(halo, SUBLANES - s, 0)
    row8 = lax.broadcasted_iota(jnp.int32, hr.shape, 0)
    tail = jnp.where(row8 >= SUBLANES - s, hr, r[t - SUBLANES:])
    return jnp.concatenate([r[:t - SUBLANES], tail], axis=0) if t > SUBLANES else tail


def _scan_down(a, u):
    t = a.shape[0]
    row = lax.broadcasted_iota(jnp.int32, a.shape, 0)
    d = 1
    while d < t:
        keep = row >= d
        a_sh = jnp.where(keep, pltpu.roll(a, d, 0), 1.0)
        u_sh = jnp.where(keep, pltpu.roll(u, d, 0), 0.0)
        u = a * u_sh + u
        a = a * a_sh
        d *= 2
    return a, u


def _scan_up(a, u):
    t = a.shape[0]
    row = lax.broadcasted_iota(jnp.int32, a.shape, 0)
    d = 1
    while d < t:
        keep = row < t - d
        a_sh = jnp.where(keep, pltpu.roll(a, t - d, 0), 1.0)
        u_sh = jnp.where(keep, pltpu.roll(u, t - d, 0), 0.0)
        u = a * u_sh + u
        a = a * a_sh
        d *= 2
    return a, u


def _conv4(blk, halo, cw, cb):
    out = cb + blk * cw[3:4]
    for k in range(3):
        out = out + _shift_down(blk, halo, 3 - k) * cw[k:k + 1]
    return out


RG_T = 512


def _rg_gates(xc, wa, ba, wx, bx, lam):
    xcb = _mx(xc)
    r = _sigmoid(_dot(xcb, wa) + ba)
    ig = _sigmoid(_dot(xcb, wx) + bx)
    sp = _softplus(-lam)
    la = (-RG_C * r) * sp
    a = jnp.exp(la)
    mult = jnp.sqrt(_neg_expm1(2.0 * la))
    return r, ig, sp, a, mult


def _rglru_fwd(proj0, cw8, cb, wa, ba, wx, bx, lam):
    t, w = RG_T, RNN_W
    nb = SEQ // t

    def body(x_ref, halo_ref, cw_ref, cb_ref, wa_ref, ba_ref, wx_ref, bx_ref, lam_ref, xc_ref, h_ref, carry):
        i = pl.program_id(0)

        @pl.when(i == 0)
        def _():
            carry[...] = jnp.zeros_like(carry)

        blk = x_ref[...]
        halo = jnp.where(i > 0, halo_ref[...], 0.0)
        xc = _conv4(blk, halo, cw_ref[...], cb_ref[...])
        _, ig, _, a, mult = _rg_gates(xc, wa_ref[...], ba_ref[...], wx_ref[...], bx_ref[...], lam_ref[...])
        u = mult * (ig * xc)
        big_a, big_u = _scan_down(a, u)
        h = big_a * carry[SUBLANES - 1:SUBLANES, :] + big_u
        carry[...] = h[t - SUBLANES:]
        xc_ref[...] = xc
        h_ref[...] = h

    vec = pl.BlockSpec((1, w), lambda i: (0, 0))
    mat = pl.BlockSpec((w, w), lambda i: (0, 0))
    return pl.pallas_call(
        body, name="rglru_fwd", grid=(nb,),
        in_specs=[pl.BlockSpec((t, w), lambda i: (i, 0)),
                  pl.BlockSpec((SUBLANES, w), lambda i: (jnp.maximum(i * (t // SUBLANES) - 1, 0), 0)),
                  pl.BlockSpec((SUBLANES, w), lambda i: (0, 0)), vec, mat, vec, mat, vec, vec],
        out_specs=[pl.BlockSpec((t, w), lambda i: (i, 0)), pl.BlockSpec((t, w), lambda i: (i, 0))],
        out_shape=[jax.ShapeDtypeStruct((SEQ, w), F32), jax.ShapeDtypeStruct((SEQ, w), F32)],
        scratch_shapes=[pltpu.VMEM((SUBLANES, w), F32)],
        compiler_params=_params(("arbitrary",)),
    )(proj0, proj0, cw8, cb, wa, ba, wx, bx, lam)


def _rglru_bwd(dh, xc, h, proj0, cw8, wa, ba, wx, bx, lam):
    t, w = RG_T, RNN_W
    nb = SEQ // t
    tb = t // SUBLANES

    def body(dh_ref, xc_ref, h_ref, hh_ref, x_ref, xh_ref, cw_ref, wa_ref, ba_ref, wx_ref, bx_ref, lam_ref,
             dx_ref, dwa_ref, dwx_ref, dvec_ref, gcarry, dxc_next):
        i = pl.program_id(0)
        rev = nb - 1 - i

        @pl.when(i == 0)
        def _():
            gcarry[...] = jnp.zeros_like(gcarry)
            dxc_next[...] = jnp.zeros_like(dxc_next)
            dwa_ref[...] = jnp.zeros_like(dwa_ref)
            dwx_ref[...] = jnp.zeros_like(dwx_ref)
            dvec_ref[...] = jnp.zeros_like(dvec_ref)

        xc = xc_ref[...]
        wa_v, wx_v = wa_ref[...], wx_ref[...]
        lam_v = lam_ref[...]
        r, ig, sp, a, mult = _rg_gates(xc, wa_v, ba_ref[...], wx_v, bx_ref[...], lam_v)
        dhv = dh_ref[...]
        big_a, big_u = _scan_up(a, a * dhv)
        gg = big_a * gcarry[0:1, :] + big_u
        g = dhv + _shift_up(gg, gcarry[...], 1)
        gcarry[...] = gg[:SUBLANES]
        hhalo = jnp.where(rev > 0, hh_ref[...], 0.0)
        da = g * _shift_down(h_ref[...], hhalo, 1)
        d_mult = g * (ig * xc)
        d_i = g * (mult * xc)
        dxc = g * (mult * ig)
        d_la = da * a - d_mult * (a * a) / mult
        d_r = d_la * (-RG_C * sp)
        d_sp = jnp.sum(d_la * (-RG_C * r), axis=0, keepdims=True)
        d_pa = d_r * r * (1.0 - r)
        d_px = d_i * ig * (1.0 - ig)
        d_pab, d_pxb = _mx(d_pa), _mx(d_px)
        dxc = dxc + _dot_nt(d_pab, wa_v) + _dot_nt(d_pxb, wx_v)
        xcb = _mx(xc)
        dwa_ref[...] += _dot_tn(xcb, d_pab)
        dwx_ref[...] += _dot_tn(xcb, d_pxb)
        dvec_ref[0:1, :] += jnp.sum(d_pa, axis=0, keepdims=True)
        dvec_ref[1:2, :] += jnp.sum(d_px, axis=0, keepdims=True)
        dvec_ref[2:3, :] += d_sp * (-_sigmoid(-lam_v))
        dvec_ref[3:4, :] += jnp.sum(dxc, axis=0, keepdims=True)
        xblk = x_ref[...]
        xhalo = jnp.where(rev > 0, xh_ref[...], 0.0)
        cw = cw_ref[...]
        dx = dxc * cw[3:4]
        nxt = dxc_next[...]
        for k in range(4):
            dvec_ref[4 + k:5 + k, :] += jnp.sum(dxc * _shift_down(xblk, xhalo, 3 - k), axis=0, keepdims=True)
            if k < 3:
                dx = dx + _shift_up(dxc, nxt, 3 - k) * cw[k:k + 1]
        dxc_next[...] = dxc[:SUBLANES]
        dx_ref[...] = dx

    blk = pl.BlockSpec((t, w), lambda i: (nb - 1 - i, 0))
    halo = pl.BlockSpec((SUBLANES, w), lambda i: (jnp.maximum((nb - 1 - i) * tb - 1, 0), 0))
    vec = pl.BlockSpec((1, w), lambda i: (0, 0))
    mat = pl.BlockSpec((w, w), lambda i: (0, 0))
    return pl.pallas_call(
        body, name="rglru_bwd", grid=(nb,),
        in_specs=[blk, blk, blk, halo, blk, halo, pl.BlockSpec((SUBLANES, w), lambda i: (0, 0)), mat, vec, mat, vec, vec],
        out_specs=[blk, mat, mat, pl.BlockSpec((16, w), lambda i: (0, 0))],
        out_shape=[jax.ShapeDtypeStruct((SEQ, w), F32), jax.ShapeDtypeStruct((w, w), F32),
                   jax.ShapeDtypeStruct((w, w), F32), jax.ShapeDtypeStruct((16, w), F32)],
        scratch_shapes=[pltpu.VMEM((SUBLANES, w), F32), pltpu.VMEM((SUBLANES, w), F32)],
        compiler_params=_params(("arbitrary",)),
    )(dh, xc, h, h, proj0, proj0, cw8, wa, ba, wx, bx, lam)


MLA_T = 512


def _rope(v, c, sa, sb):
    return v * c + pltpu.roll(v, LANES - 16, 1) * sa + pltpu.roll(v, 16, 1) * sb


def _rope_t(dv, c, sa, sb):
    return dv * c + pltpu.roll(dv * sa, 16, 1) + pltpu.roll(dv * sb, LANES - 16, 1)


def _rms(v, g, eps=1e-6):
    rs = lax.rsqrt(jnp.mean(v * v, axis=-1, keepdims=True) + eps)
    return v * rs * g, rs


def _mla_norm_fwd(proj0, q_norm, kv_norm, tc, tsa, tsb):
    t = MLA_T

    def body(cq_ref, ck_ref, qn_ref, kn_ref, c_ref, sa_ref, sb_ref, oq_ref, ok_ref, okr_ref):
        oq_ref[...] = _mx(_rms(cq_ref[...], qn_ref[...])[0])
        ck = ck_ref[...]
        ok_ref[...] = _mx(_rms(ck[:, :LANES], kn_ref[...])[0])
        okr_ref[...] = _rope(ck[:, LANES:], c_ref[...], sa_ref[...], sb_ref[...])

    tab = pl.BlockSpec((t, LANES), lambda i: (i, 0))
    return pl.pallas_call(
        body, name="mla_norm_fwd", grid=(SEQ // t,),
        in_specs=[pl.BlockSpec((t, 256), lambda i: (i, 6)), pl.BlockSpec((t, 256), lambda i: (i, 7)),
                  pl.BlockSpec((1, 256), lambda i: (0, 0)), pl.BlockSpec((1, LANES), lambda i: (0, 0)), tab, tab, tab],
        out_specs=[pl.BlockSpec((t, 256), lambda i: (i, 0)), tab, tab],
        out_shape=[jax.ShapeDtypeStruct((SEQ, 256), MXU_DTYPE), jax.ShapeDtypeStruct((SEQ, LANES), MXU_DTYPE),
                   jax.ShapeDtypeStruct((SEQ, LANES), F32)],
        compiler_params=_params(("parallel",)),
    )(proj0, proj0, q_norm, kv_norm, tc, tsa, tsb)


def _mla_assemble(qraw, kvraw, kr, tc, tsa, tsb):
    t = MLA_T

    def body(q_ref, k_ref, v_ref, kr_ref, c_ref, sa_ref, sb_ref, oq_ref, ok_ref, ov_ref):
        c, sa, sb, krv = c_ref[...], sa_ref[...], sb_ref[...], kr_ref[...]
        for hd in range(MLA_HEADS):
            sl = slice(hd * LANES, (hd + 1) * LANES)
            oq_ref[:, sl] = _mx(_rope(q_ref[:, sl], c, sa, sb))
            ok_ref[:, sl] = _mx(k_ref[:, sl] + krv)
        ov_ref[...] = _mx(v_ref[...])

    tab = pl.BlockSpec((t, LANES), lambda i: (i, 0))
    wide = pl.BlockSpec((t, 1024), lambda i: (i, 0))
    return pl.pallas_call(
        body, name="mla_assemble", grid=(SEQ // t,),
        in_specs=[wide, wide, pl.BlockSpec((t, 512), lambda i: (i, 2)), tab, tab, tab, tab],
        out_specs=[wide, wide, pl.BlockSpec((t, 512), lambda i: (i, 0))],
        out_shape=[jax.ShapeDtypeStruct((SEQ, 1024), MXU_DTYPE), jax.ShapeDtypeStruct((SEQ, 1024), MXU_DTYPE),
                   jax.ShapeDtypeStruct((SEQ, 512), MXU_DTYPE)],
        compiler_params=_params(("parallel",)),
    )(qraw, kvraw, kvraw, kr, tc, tsa, tsb)


ATT_T = 512


def _flash_fwd(q, k, v):
    t = ATT_T
    nb = SEQ // t

    def body(q_ref, k_ref, v_ref, o_ref, lse_ref, m_sc, l_sc, acc_sc):
        qi, ki = pl.program_id(1), pl.program_id(2)

        @pl.when(ki == 0)
        def _():
            m_sc[...] = jnp.full_like(m_sc, -jnp.inf)
            l_sc[...] = jnp.zeros_like(l_sc)
            acc_sc[...] = jnp.zeros_like(acc_sc)

        @pl.when(ki <= qi)
        def _():
            row = qi * t + lax.broadcasted_iota(jnp.int32, (t, t), 0)
            col = ki * t + lax.broadcasted_iota(jnp.int32, (t, t), 1)
            causal = col <= row
            vv = v_ref[...]
            lane_v = lax.broadcasted_iota(jnp.int32, vv.shape, 1)
            lane_o = lax.broadcasted_iota(jnp.int32, (t, LANES), 1)
            pv = jnp.zeros((t, LANES), F32)
            alphas = []
            for hd in range(2):
                sl = slice(hd * LANES, (hd + 1) * LANES)
                s = _dot_nt(q_ref[:, sl], k_ref[:, sl]) * ATT_SCALE
                s = jnp.where(causal, s, -jnp.inf)
                m_prev = m_sc[hd]
                m_new = jnp.maximum(m_prev, jnp.max(s, axis=1, keepdims=True))
                p = jnp.exp(s - m_new[:, :1])
                alpha = jnp.exp(m_prev - m_new)
                l_sc[hd] = alpha * l_sc[hd] + jnp.sum(p, axis=1, keepdims=True)
                m_sc[hd] = m_new
                alphas.append(alpha)
                vh = jnp.where((lane_v >= hd * 64) & (lane_v < (hd + 1) * 64), vv, jnp.zeros_like(vv))
                pv = pv + _dot(_mx(p), vh)
            acc_sc[...] = acc_sc[...] * jnp.where(lane_o < 64, alphas[0], alphas[1]) + pv

        @pl.when(ki == qi)
        def _():
            lane_o = lax.broadcasted_iota(jnp.int32, (t, LANES), 1)
            first = lane_o < 64
            l_pair = jnp.where(first, l_sc[0], l_sc[1])
            o_ref[...] = acc_sc[...] / l_pair
            lse_ref[0] = jnp.where(first, m_sc[0], m_sc[1]) + jnp.log(l_pair)

    return pl.pallas_call(
        body, name="flash_fwd", grid=(4, nb, nb),
        in_specs=[pl.BlockSpec((t, 256), lambda p, qi, ki: (qi, p)),
                  pl.BlockSpec((t, 256), lambda p, qi, ki: (jnp.minimum(ki, qi), p)),
                  pl.BlockSpec((t, LANES), lambda p, qi, ki: (jnp.minimum(ki, qi), p))],
        out_specs=[pl.BlockSpec((t, LANES), lambda p, qi, ki: (qi, p)),
                   pl.BlockSpec((1, t, LANES), lambda p, qi, ki: (p, qi, 0))],
        out_shape=[jax.ShapeDtypeStruct((SEQ, 512), F32), jax.ShapeDtypeStruct((4, SEQ, LANES), F32)],
        scratch_shapes=[pltpu.VMEM((2, t, LANES), F32), pltpu.VMEM((2, t, LANES), F32), pltpu.VMEM((t, LANES), F32)],
        compiler_params=_params(("parallel", "arbitrary", "arbitrary")),
    )(q, k, v)


def _flash_bwd(q, k, v, o, do, lse):
    t = ATT_T
    nb = SEQ // t

    def body(q_ref, k_ref, v_ref, o_ref, do_ref, lse_ref, dq_ref, dk_ref, dv_ref):
        ki, qi = pl.program_id(1), pl.program_id(2)

        @pl.when((ki == 0) & (qi == 0))
        def _():
            dq_ref[...] = jnp.zeros_like(dq_ref)

        @pl.when(qi == 0)
        def _():
            dk_ref[...] = jnp.zeros_like(dk_ref)
            dv_ref[...] = jnp.zeros_like(dv_ref)

        @pl.when(qi >= ki)
        def _():
            row = qi * t + lax.broadcasted_iota(jnp.int32, (t, t), 0)
            col = ki * t + lax.broadcasted_iota(jnp.int32, (t, t), 1)
            causal = col <= row
            dov, ov, vv, lse_v = do_ref[...], o_ref[...], v_ref[...], lse_ref[0]
            lane = lax.broadcasted_iota(jnp.int32, (t, LANES), 1)
            prod = dov * ov
            qrows = pl.ds(pl.multiple_of(qi * t, t), t)
            dv_acc = jnp.zeros((t, LANES), F32)
            for hd in range(2):
                sl = slice(hd * LANES, (hd + 1) * LANES)
                mine = (lane >= hd * 64) & (lane < (hd + 1) * 64)
                qh, kh = q_ref[:, sl], k_ref[:, sl]
                s = _dot_nt(qh, kh) * ATT_SCALE
                p = jnp.where(causal, jnp.exp(s - lse_v[:, hd * 64:hd * 64 + 1]), 0.0)
                do_h = jnp.where(mine, dov, 0.0)
                delta = jnp.sum(jnp.where(mine, prod, 0.0), axis=1, keepdims=True)
                dp = _dot_nt(_mx(do_h), vv)
                ds = _mx(p * (dp - delta) * ATT_SCALE)
                dv_acc = dv_acc + jnp.where(mine, _dot_tn(_mx(p), _mx(dov)), 0.0)
                dk_ref[:, sl] += _dot_tn(ds, qh)
                dq_ref[qrows, sl] += _dot(ds, kh)
            dv_ref[...] += dv_acc

    qmap = lambda p, ki, qi: (jnp.maximum(qi, ki), p)
    return pl.pallas_call(
        body, name="flash_bwd", grid=(4, nb, nb),
        in_specs=[pl.BlockSpec((t, 256), qmap), pl.BlockSpec((t, 256), lambda p, ki, qi: (ki, p)),
                  pl.BlockSpec((t, LANES), lambda p, ki, qi: (ki, p)), pl.BlockSpec((t, LANES), qmap),
                  pl.BlockSpec((t, LANES), qmap),
                  pl.BlockSpec((1, t, LANES), lambda p, ki, qi: (p, jnp.maximum(qi, ki), 0))],
        out_specs=[pl.BlockSpec((SEQ, 256), lambda p, ki, qi: (0, p)),
                   pl.BlockSpec((t, 256), lambda p, ki, qi: (ki, p)),
                   pl.BlockSpec((t, LANES), lambda p, ki, qi: (ki, p))],
        out_shape=[jax.ShapeDtypeStruct((SEQ, 1024), F32), jax.ShapeDtypeStruct((SEQ, 1024), F32),
                   jax.ShapeDtypeStruct((SEQ, 512), F32)],
        compiler_params=_params(("parallel", "arbitrary", "arbitrary")),
    )(q, k, v, o, do, lse)


def _mla_bwd_rope(dq, dk, dv, tc, tsa, tsb):
    t = MLA_T

    def body(dq_ref, dk_ref, dv_ref, c_ref, sa_ref, sb_ref, oq_ref, okv_ref, okr_ref):
        c, sa, sb = c_ref[...], sa_ref[...], sb_ref[...]
        lane = lax.broadcasted_iota(jnp.int32, (t, LANES), 1)
        dkr = jnp.zeros((t, LANES), F32)
        for hd in range(MLA_HEADS):
            sl = slice(hd * LANES, (hd + 1) * LANES)
            oq_ref[:, sl] = _mx(_rope_t(dq_ref[:, sl], c, sa, sb))
            dkh = dk_ref[:, sl]
            okv_ref[:, sl] = _mx(dkh)
            dkr = dkr + dkh
        okv_ref[:, 1024:] = _mx(dv_ref[...])
        dkr = jnp.where((lane >= 64) & (lane < 96), dkr, 0.0)
        okr_ref[...] = _rope_t(dkr, c, sa, sb)

    tab = pl.BlockSpec((t, LANES), lambda i: (i, 0))
    wide = pl.BlockSpec((t, 1024), lambda i: (i, 0))
    return pl.pallas_call(
        body, name="mla_bwd_rope", grid=(SEQ // t,),
        in_specs=[wide, wide, pl.BlockSpec((t, 512), lambda i: (i, 0)), tab, tab, tab],
        out_specs=[wide, pl.BlockSpec((t, 1536), lambda i: (i, 0)), tab],
        out_shape=[jax.ShapeDtypeStruct((SEQ, 1024), MXU_DTYPE), jax.ShapeDtypeStruct((SEQ, 1536), MXU_DTYPE),
                   jax.ShapeDtypeStruct((SEQ, LANES), F32)],
        compiler_params=_params(("parallel",)),
    )(dq, dk, dv, tc, tsa, tsb)


def _rms_bwd(v, g, dy, eps=1e-6):
    rs = lax.rsqrt(jnp.mean(v * v, axis=-1, keepdims=True) + eps)
    xh = v * rs
    dxh = dy * g
    dv = rs * (dxh - xh * jnp.mean(dxh * xh, axis=-1, keepdims=True))
    return dv, jnp.sum(dy * xh, axis=0, keepdims=True)


def _mla_norm_bwd(proj0, dqn, dkn, dkr, q_norm, kv_norm):
    t = MLA_T

    def body(cq_ref, ck_ref, dqn_ref, dkn_ref, dkr_ref, qn_ref, kn_ref, o_ref, dgq_ref, dgk_ref):
        @pl.when(pl.program_id(0) == 0)
        def _():
            dgq_ref[...] = jnp.zeros_like(dgq_ref)
            dgk_ref[...] = jnp.zeros_like(dgk_ref)

        dcq, dgq = _rms_bwd(cq_ref[...], qn_ref[...], dqn_ref[...])
        dck, dgk = _rms_bwd(ck_ref[:, :LANES], kn_ref[...], dkn_ref[...])
        o_ref[:, :256] = dcq
        o_ref[:, 256:384] = dck
        o_ref[:, 384:] = dkr_ref[...]
        dgq_ref[0:1, :] += dgq
        dgk_ref[0:1, :] += dgk

    tab = pl.BlockSpec((t, LANES), lambda i: (i, 0))
    return pl.pallas_call(
        body, name="mla_norm_bwd", grid=(SEQ // t,),
        in_specs=[pl.BlockSpec((t, 256), lambda i: (i, 6)), pl.BlockSpec((t, 256), lambda i: (i, 7)),
                  pl.BlockSpec((t, 256), lambda i: (i, 0)), tab, tab,
                  pl.BlockSpec((1, 256), lambda i: (0, 0)), pl.BlockSpec((1, LANES), lambda i: (0, 0))],
        out_specs=[pl.BlockSpec((t, 512), lambda i: (i, 0)), pl.BlockSpec((SUBLANES, 256), lambda i: (0, 0)),
                   pl.BlockSpec((SUBLANES, LANES), lambda i: (0, 0))],
        out_shape=[jax.ShapeDtypeStruct((SEQ, 512), F32), jax.ShapeDtypeStruct((SUBLANES, 256), F32),
                   jax.ShapeDtypeStruct((SUBLANES, LANES), F32)],
        compiler_params=_params(("arbitrary",)),
    )(proj0, proj0, dqn, dkn, dkr, q_norm, kv_norm)


LN_T = 512


def _ln(v, g, b, eps=1e-5):
    mu = jnp.mean(v, axis=-1, keepdims=True)
    xc = v - mu
    rs = lax.rsqrt(jnp.mean(xc * xc, axis=-1, keepdims=True) + eps)
    return xc * rs * g + b


def _ln_bwd(v, g, dy, eps=1e-5):
    mu = jnp.mean(v, axis=-1, keepdims=True)
    xc = v - mu
    rs = lax.rsqrt(jnp.mean(xc * xc, axis=-1, keepdims=True) + eps)
    xh = xc * rs
    dxh = dy * g
    dv = rs * (dxh - jnp.mean(dxh, axis=-1, keepdims=True) - xh * jnp.mean(dxh * xh, axis=-1, keepdims=True))
    return dv, jnp.sum(dy * xh, axis=0, keepdims=True), jnp.sum(dy, axis=0, keepdims=True)


def _l0_out(h, o, proj0, x, w_out, g, b):
    t = LN_T

    def body(h_ref, o_ref, ga_ref, gb_ref, x_ref, w_ref, g_ref, b_ref, y_ref, v_ref, x1_ref):
        y = _mx(jnp.concatenate([h_ref[...] * _silu(ga_ref[...]), o_ref[...] * _silu(gb_ref[...])], axis=1))
        v = DN_ALPHA * x_ref[...] + _dot(y, w_ref[...])
        y_ref[...] = y
        v_ref[...] = v
        x1_ref[...] = _ln(v, g_ref[...], b_ref[...])

    half = pl.BlockSpec((t, 512), lambda i: (i, 0))
    full = pl.BlockSpec((t, D_MODEL), lambda i: (i, 0))
    vec = pl.BlockSpec((1, D_MODEL), lambda i: (0, 0))
    return pl.pallas_call(
        body, name="l0_out", grid=(SEQ // t,),
        in_specs=[half, half, pl.BlockSpec((t, 512), lambda i: (i, 1)), pl.BlockSpec((t, 512), lambda i: (i, 2)), full,
                  pl.BlockSpec((D_MODEL, D_MODEL), lambda i: (0, 0)), vec, vec],
        out_specs=[full, full, full],
        out_shape=[jax.ShapeDtypeStruct((SEQ, D_MODEL), MXU_DTYPE), jax.ShapeDtypeStruct((SEQ, D_MODEL), F32),
                   jax.ShapeDtypeStruct((SEQ, D_MODEL), F32)],
        compiler_params=_params(("parallel",)),
    )(h, o, proj0, proj0, x, w_out, g, b)


def _ln_bwd_call(v, dy, g):
    t = LN_T

    def body(v_ref, dy_ref, g_ref, dv_ref, dgb_ref):
        @pl.when(pl.program_id(0) == 0)
        def _():
            dgb_ref[...] = jnp.zeros_like(dgb_ref)

        dv, dg, db = _ln_bwd(v_ref[...], g_ref[...], dy_ref[...])
        dv_ref[...] = dv
        dgb_ref[0:1, :] += dg
        dgb_ref[1:2, :] += db

    full = pl.BlockSpec((t, D_MODEL), lambda i: (i, 0))
    return pl.pallas_call(
        body, name="ln_bwd", grid=(SEQ // t,),
        in_specs=[full, full, pl.BlockSpec((1, D_MODEL), lambda i: (0, 0))],
        out_specs=[full, pl.BlockSpec((SUBLANES, D_MODEL), lambda i: (0, 0))],
        out_shape=[jax.ShapeDtypeStruct((SEQ, D_MODEL), F32), jax.ShapeDtypeStruct((SUBLANES, D_MODEL), F32)],
        compiler_params=_params(("arbitrary",)),
    )(v, dy, g)


def _gate_bwd(dy, h, o, proj0):
    t = LN_T

    def body(dya_ref, dyb_ref, h_ref, o_ref, ga_ref, gb_ref, dh_ref, do_ref, dg_ref):
        ga, gb, dya, dyb = ga_ref[...], gb_ref[...], dya_ref[...], dyb_ref[...]
        dh_ref[...] = dya * _silu(ga)
        do_ref[...] = dyb * _silu(gb)
        dg_ref[:, :512] = dya * h_ref[...] * _dsilu(ga)
        dg_ref[:, 512:] = dyb * o_ref[...] * _dsilu(gb)

    half = pl.BlockSpec((t, 512), lambda i: (i, 0))
    half1 = pl.BlockSpec((t, 512), lambda i: (i, 1))
    full = pl.BlockSpec((t, 1024), lambda i: (i, 0))
    return pl.pallas_call(
        body, name="gate_bwd", grid=(SEQ // t,),
        in_specs=[half, half1, half, half, half1, pl.BlockSpec((t, 512), lambda i: (i, 2))],
        out_specs=[half, half, full],
        out_shape=[jax.ShapeDtypeStruct((SEQ, 512), F32), jax.ShapeDtypeStruct((SEQ, 512), F32),
                   jax.ShapeDtypeStruct((SEQ, 1024), F32)],
        compiler_params=_params(("parallel",)),
    )(dy, dy, h, o, proj0, proj0)


CONV_T = 512
CONV_CB = 1024


def _ssd_conv_fwd(xbc, cw8, cb):
    t, cbk = CONV_T, CONV_CB
    tb = t // SUBLANES

    def body(x_ref, halo_ref, cw_ref, cb_ref, pre_ref, act_ref):
        halo = jnp.where(pl.program_id(1) > 0, halo_ref[...], 0.0)
        pre = _conv4(x_ref[...], halo, cw_ref[...], cb_ref[...])
        pre_ref[...] = pre
        act_ref[...] = _silu(pre)

    blk = pl.BlockSpec((t, cbk), lambda j, i: (i, j))
    return pl.pallas_call(
        body, name="ssd_conv_fwd", grid=(SSD_CONV // cbk, SEQ // t),
        in_specs=[blk, pl.BlockSpec((SUBLANES, cbk), lambda j, i: (jnp.maximum(i * tb - 1, 0), j)),
                  pl.BlockSpec((SUBLANES, cbk), lambda j, i: (0, j)), pl.BlockSpec((1, cbk), lambda j, i: (0, j))],
        out_specs=[blk, blk],
        out_shape=[jax.ShapeDtypeStruct((SEQ, SSD_CONV), F32), jax.ShapeDtypeStruct((SEQ, SSD_CONV), F32)],
        compiler_params=_params(("parallel", "parallel")),
    )(xbc, xbc, cw8, cb)


def _ssd_conv_bwd(dact, pre, xbc, cw8):
    t, cbk = CONV_T, CONV_CB
    tb = t // SUBLANES
    nb = SEQ // t

    def body(da_ref, dan_ref, pre_ref, pren_ref, x_ref, xh_ref, cw_ref, dx_ref, dcw_ref):
        i = pl.program_id(1)

        @pl.when(i == 0)
        def _():
            dcw_ref[...] = jnp.zeros_like(dcw_ref)

        dpre = da_ref[...] * _dsilu(pre_ref[...])
        dpre_next = jnp.where(i < nb - 1, dan_ref[...] * _dsilu(pren_ref[...]), 0.0)
        xblk = x_ref[...]
        xhalo = jnp.where(i > 0, xh_ref[...], 0.0)
        cw = cw_ref[...]
        dx = dpre * cw[3:4]
        for k in range(4):
            dcw_ref[k:k + 1, :] += jnp.sum(dpre * _shift_down(xblk, xhalo, 3 - k), axis=0, keepdims=True)
            if k < 3:
                dx = dx + _shift_up(dpre, dpre_next, 3 - k) * cw[k:k + 1]
        dcw_ref[4:5, :] += jnp.sum(dpre, axis=0, keepdims=True)
        dx_ref[...] = dx

    blk = pl.BlockSpec((t, cbk), lambda j, i: (i, j))
    nxt = pl.BlockSpec((SUBLANES, cbk), lambda j, i: (jnp.minimum((i + 1) * tb, SEQ // SUBLANES - 1), j))
    prv = pl.BlockSpec((SUBLANES, cbk), lambda j, i: (jnp.maximum(i * tb - 1, 0), j))
    acc = pl.BlockSpec((SUBLANES, cbk), lambda j, i: (0, j))
    return pl.pallas_call(
        body, name="ssd_conv_bwd", grid=(SSD_CONV // cbk, nb),
        in_specs=[blk, nxt, blk, nxt, blk, prv, acc],
        out_specs=[blk, acc],
        out_shape=[jax.ShapeDtypeStruct((SEQ, SSD_CONV), F32), jax.ShapeDtypeStruct((SUBLANES, SSD_CONV), F32)],
        compiler_params=_params(("parallel", "arbitrary")),
    )(dact, dact, pre, pre, xbc, xbc, cw8)


def _ssd_common(dt_raw, bias, alog, tril, expand, xs):
    lane = lax.broadcasted_iota(jnp.int32, dt_raw.shape, 1)
    dt = jnp.where(lane < SSD_HEADS, _softplus(dt_raw + bias), 0.0)
    a_neg = -jnp.exp(alog)
    cs = _dot_hi(tril, dt * a_neg)
    dt_x = _dot_hi(dt, expand)
    cs_x = _dot_hi(cs, expand)
    last = cs_x[SSD_L - 1:SSD_L, :]
    return dt, a_neg, cs, dt_x, cs_x, xs * dt_x, jnp.exp(last - cs_x), jnp.exp(cs_x), jnp.exp(last)


def _ssd_decay(cs, cs_t, hh, causal):
    seg = cs[:, hh:hh + 1] - cs_t[hh:hh + 1, :]
    return jnp.where(causal, jnp.exp(jnp.where(causal, seg, 0.0)), 0.0)


def _ssd_scan_fwd(act, dt_raw, bias, alog, d_x, tril, expand):
    nc = SEQ // SSD_L
    gw = SSD_INNER // SSD_GROUPS

    def body(act_ref, dt_ref, bias_ref, alog_ref, dx_ref, tril_ref, e_ref, y_ref, hp_ref, h_sc):
        @pl.when(pl.program_id(0) == 0)
        def _():
            h_sc[...] = jnp.zeros_like(h_sc)

        xs = act_ref[:, :SSD_INNER]
        _, _, cs, _, _, xdt, ds_x, ecs_x, elast = _ssd_common(
            dt_ref[...], bias_ref[...], alog_ref[...], tril_ref[...], e_ref[...], xs)
        cs_t = cs.T
        causal = (lax.broadcasted_iota(jnp.int32, (SSD_L, SSD_L), 0)
                  >= lax.broadcasted_iota(jnp.int32, (SSD_L, SSD_L), 1))
        lane = lax.broadcasted_iota(jnp.int32, (SSD_L, LANES), 1)
        xdt_b = _mx(xdt)
        xds_b = _mx(xdt * ds_x)
        hp_ref[0] = h_sc[...]
        for g in range(SSD_GROUPS):
            gs = slice(g * gw, (g + 1) * gw)
            bg = _mx(act_ref[:, SSD_INNER + g * SSD_N:SSD_INNER + (g + 1) * SSD_N])
            cg = _mx(act_ref[:, SSD_INNER + 512 + g * SSD_N:SSD_INNER + 512 + (g + 1) * SSD_N])
            cb = _dot_nt(cg, bg)
            hprev = h_sc[:, gs]
            yoff = _dot(cg, _mx(hprev)) * ecs_x[:, gs]
            h_sc[:, gs] = hprev * elast[:, gs] + _dot_tn(bg, xds_b[:, gs])
            for pr in range(4):
                ps = slice(g * gw + pr * LANES, g * gw + (pr + 1) * LANES)
                xp = xdt_b[:, ps]
                ydiag = jnp.zeros((SSD_L, LANES), F32)
                for j in range(2):
                    dm = _ssd_decay(cs, cs_t, g * 8 + pr * 2 + j, causal)
                    mine = (lane >= j * 64) & (lane < (j + 1) * 64)
                    ydiag = ydiag + _dot(_mx(cb * dm), jnp.where(mine, xp, jnp.zeros_like(xp)))
                y_ref[:, ps] = ydiag + yoff[:, pr * LANES:(pr + 1) * LANES] + dx_ref[:, ps] * xs[:, ps]

    const = lambda shape: pl.BlockSpec(shape, lambda c: (0, 0))
    return pl.pallas_call(
        body, name="ssd_scan_fwd", grid=(nc,),
        in_specs=[pl.BlockSpec((SSD_L, SSD_CONV), lambda c: (c, 0)), pl.BlockSpec((SSD_L, LANES), lambda c: (c, 0)),
                  const((1, LANES)), const((1, LANES)), const((1, SSD_INNER)), const((SSD_L, SSD_L)),
                  const((LANES, SSD_INNER))],
        out_specs=[pl.BlockSpec((SSD_L, SSD_INNER), lambda c: (c, 0)),
                   pl.BlockSpec((1, SSD_N, SSD_INNER), lambda c: (c, 0, 0))],
        out_shape=[jax.ShapeDtypeStruct((SEQ, SSD_INNER), F32), jax.ShapeDtypeStruct((nc, SSD_N, SSD_INNER), F32)],
        scratch_shapes=[pltpu.VMEM((SSD_N, SSD_INNER), F32)],
        compiler_params=_params(("arbitrary",)),
    )(act, dt_raw, bias, alog, d_x, tril, expand)


def _ssd_scan_bwd(dy, act, dt_raw, hprev_all, bias, alog, d_x, tril, expand, expand_t):
    nc = SEQ // SSD_L
    gw = SSD_INNER // SSD_GROUPS

    def body(dy_ref, act_ref, dt_ref, hp_ref, bias_ref, alog_ref, dx_ref, tril_ref, e_ref, et_ref,
             dact_ref, ddt_ref, dvec_ref, dh_sc, dd_sc):
        i = pl.program_id(0)

        @pl.when(i == 0)
        def _():
            dh_sc[...] = jnp.zeros_like(dh_sc)
            dd_sc[...] = jnp.zeros_like(dd_sc)
            dvec_ref[...] = jnp.zeros_like(dvec_ref)

        xs = act_ref[:, :SSD_INNER]
        dt_raw_v, bias_v = dt_ref[...], bias_ref[...]
        dt, a_neg, cs, dt_x, _, xdt, ds_x, ecs_x, elast = _ssd_common(
            dt_raw_v, bias_v, alog_ref[...], tril_ref[...], e_ref[...], xs)
        cs_t = cs.T
        rowi = lax.broadcasted_iota(jnp.int32, (SSD_L, SSD_L), 0)
        coli = lax.broadcasted_iota(jnp.int32, (SSD_L, SSD_L), 1)
        causal = rowi >= coli
        lane = lax.broadcasted_iota(jnp.int32, (SSD_L, LANES), 1)
        row_g = lax.broadcasted_iota(jnp.int32, (SSD_L, gw), 0)
        dyv = dy_ref[...]
        dd_sc[0:1, :] += jnp.sum(dyv * xs, axis=0, keepdims=True)
        xdt_b = _mx(xdt)
        xds = xdt * ds_x
        xds_b = _mx(xds)
        dy_b = _mx(dyv)
        dye_b = _mx(dyv * ecs_x)
        dcs = jnp.zeros((SSD_L, LANES), F32)
        dcs_t = jnp.zeros((LANES, SSD_L), F32)
        dcs_parts = []
        dxdt_parts = []
        for g in range(SSD_GROUPS):
            gs = slice(g * gw, (g + 1) * gw)
            bcol = slice(SSD_INNER + g * SSD_N, SSD_INNER + (g + 1) * SSD_N)
            ccol = slice(SSD_INNER + 512 + g * SSD_N, SSD_INNER + 512 + (g + 1) * SSD_N)
            bg, cg = _mx(act_ref[:, bcol]), _mx(act_ref[:, ccol])
            cb = _dot_nt(cg, bg)
            hp = hp_ref[0, :, gs]
            hp_b = _mx(hp)
            dh = dh_sc[:, gs]
            dh_b = _mx(dh)
            yoff = _dot(cg, hp_b) * ecs_x[:, gs]
            bdh = _dot(bg, dh_b)
            tt = xds[:, gs] * bdh
            last_row = (jnp.sum(tt, axis=0, keepdims=True)
                        + jnp.sum(dh * hp, axis=0, keepdims=True) * elast[:, gs])
            dcs_parts.append(dyv[:, gs] * yoff - tt + jnp.where(row_g == SSD_L - 1, last_row, 0.0))
            dc_g = _dot_nt(dye_b[:, gs], hp_b)
            db_g = _dot_nt(xds_b[:, gs], dh_b)
            dh_sc[:, gs] = _dot_tn(cg, dye_b[:, gs]) + dh * elast[:, gs]
            wsum = jnp.zeros((SSD_L, SSD_L), F32)
            dxdt_g = []
            for pr in range(4):
                ps = slice(g * gw + pr * LANES, g * gw + (pr + 1) * LANES)
                xp, dyp = xdt_b[:, ps], dy_b[:, ps]
                dxp = jnp.zeros((SSD_L, LANES), F32)
                for j in range(2):
                    hh = g * 8 + pr * 2 + j
                    dm = _ssd_decay(cs, cs_t, hh, causal)
                    mine = (lane >= j * 64) & (lane < (j + 1) * 64)
                    dy_h = jnp.where(mine, dyp, jnp.zeros_like(dyp))
                    wd = _dot_nt(dy_h, xp) * dm
                    wsum = wsum + wd
                    gmat = wd * cb
                    dcs = dcs + jnp.where(lane == hh, jnp.sum(gmat, axis=1, keepdims=True), 0.0)
                    dcs_t = dcs_t - jnp.where(rowi == hh, jnp.sum(gmat, axis=0, keepdims=True), 0.0)
                    dxp = dxp + _dot_tn(_mx(cb * dm), dy_h)
                dxdt_g.append(dxp)
            dxdt_parts.append(jnp.concatenate(dxdt_g, axis=1) + bdh * ds_x[:, gs])
            ws_b = _mx(wsum)
            dact_ref[:, ccol] = dc_g + _dot(ws_b, bg)
            dact_ref[:, bcol] = db_g + _dot_tn(ws_b, cg)
        dxdt = jnp.concatenate(dxdt_parts, axis=1)
        dcs_x = jnp.concatenate(dcs_parts, axis=1)
        et = et_ref[...]
        dcs_tot = dcs + dcs_t.T + _dot_hi(dcs_x, et)
        da_dt = _dot_hi((coli >= rowi).astype(F32), dcs_tot)
        ddt = da_dt * a_neg + _dot_hi(dxdt * xs, et)
        ddt_raw = ddt * _sigmoid(dt_raw_v + bias_v)
        ddt_ref[...] = ddt_raw
        dvec_ref[0:1, :] += jnp.sum(ddt_raw, axis=0, keepdims=True)
        dvec_ref[1:2, :] += jnp.sum(da_dt * dt, axis=0, keepdims=True) * a_neg
        dact_ref[:, :SSD_INNER] = dyv * dx_ref[...] + dxdt * dt_x

        @pl.when(i == nc - 1)
        def _():
            dvec_ref[2:3, :] = _dot_hi(dd_sc[...], et)[0:1, :]

    const = lambda shape: pl.BlockSpec(shape, lambda c: (0, 0))
    rev = lambda c: (nc - 1 - c, 0)
    return pl.pallas_call(
        body, name="ssd_scan_bwd", grid=(nc,),
        in_specs=[pl.BlockSpec((SSD_L, SSD_INNER), rev), pl.BlockSpec((SSD_L, SSD_CONV), rev),
                  pl.BlockSpec((SSD_L, LANES), rev),
                  pl.BlockSpec((1, SSD_N, SSD_INNER), lambda c: (nc - 1 - c, 0, 0)),
                  const((1, LANES)), const((1, LANES)), const((1, SSD_INNER)), const((SSD_L, SSD_L)),
                  const((LANES, SSD_INNER)), const((SSD_INNER, LANES))],
        out_specs=[pl.BlockSpec((SSD_L, SSD_CONV), rev), pl.BlockSpec((SSD_L, LANES), rev), const((SUBLANES, LANES))],
        out_shape=[jax.ShapeDtypeStruct((SEQ, SSD_CONV), F32), jax.ShapeDtypeStruct((SEQ, LANES), F32),
                   jax.ShapeDtypeStruct((SUBLANES, LANES), F32)],
        scratch_shapes=[pltpu.VMEM((SSD_N, SSD_INNER), F32), pltpu.VMEM((SUBLANES, SSD_INNER), F32)],
        compiler_params=_params(("arbitrary",)),
    )(dy, act, dt_raw, hprev_all, bias, alog, d_x, tril, expand, expand_t)


L1_T = 256


def _gated_norm(y, z, nw):
    y2 = y * _silu(z)
    gw = SSD_INNER // SSD_GROUPS
    outs, xhs, rss = [], [], []
    for g in range(SSD_GROUPS):
        gs = slice(g * gw, (g + 1) * gw)
        v = y2[:, gs]
        rs = lax.rsqrt(jnp.mean(v * v, axis=-1, keepdims=True) + 1e-6)
        xhs.append(v * rs)
        rss.append(rs)
        outs.append(v * rs * nw[:, gs])
    return outs, xhs, rss


def _l1_out(y, z, nw, w_out, x1, g, b, target):
    t = L1_T

    def body(y_ref, z_ref, nw_ref, w_ref, x1_ref, g_ref, b_ref, tg_ref, yn_ref, dv_ref, dgb_ref, loss_ref):
        @pl.when(pl.program_id(0) == 0)
        def _():
            dgb_ref[...] = jnp.zeros_like(dgb_ref)
            loss_ref[...] = jnp.zeros_like(loss_ref)

        outs, _, _ = _gated_norm(y_ref[...], z_ref[...], nw_ref[...])
        yn = _mx(jnp.concatenate(outs, axis=1))
        yn_ref[...] = yn
        v = DN_ALPHA * x1_ref[...] + _dot(yn, w_ref[...])
        gv = g_ref[...]
        err = _ln(v, gv, b_ref[...]) - tg_ref[...]
        rowsum = jnp.sum(err * err, axis=1, keepdims=True)
        loss_ref[...] += 0.5 * jnp.sum(rowsum, axis=0, keepdims=True) / D_MODEL
        dv, dg, db = _ln_bwd(v, gv, err / D_MODEL)
        dv_ref[...] = dv
        dgb_ref[0:1, :] += dg
        dgb_ref[1:2, :] += db

    wide = pl.BlockSpec((t, SSD_INNER), lambda i: (i, 0))
    full = pl.BlockSpec((t, D_MODEL), lambda i: (i, 0))
    vec = pl.BlockSpec((1, D_MODEL), lambda i: (0, 0))
    return pl.pallas_call(
        body, name="l1_out", grid=(SEQ // t,),
        in_specs=[wide, wide, pl.BlockSpec((1, SSD_INNER), lambda i: (0, 0)),
                  pl.BlockSpec((SSD_INNER, D_MODEL), lambda i: (0, 0)), full, vec, vec, full],
        out_specs=[wide, full, pl.BlockSpec((SUBLANES, D_MODEL), lambda i: (0, 0)),
                   pl.BlockSpec((SUBLANES, LANES), lambda i: (0, 0))],
        out_shape=[jax.ShapeDtypeStruct((SEQ, SSD_INNER), MXU_DTYPE), jax.ShapeDtypeStruct((SEQ, D_MODEL), F32),
                   jax.ShapeDtypeStruct((SUBLANES, D_MODEL), F32), jax.ShapeDtypeStruct((SUBLANES, LANES), F32)],
        compiler_params=_params(("arbitrary",)),
    )(y, z, nw, w_out, x1, g, b, target)


def _l1_gate_bwd(dyn, y, z, nw):
    t = L1_T
    gw = SSD_INNER // SSD_GROUPS

    def body(dyn_ref, y_ref, z_ref, nw_ref, dy_ref, dz_ref, dnw_ref):
        @pl.when(pl.program_id(0) == 0)
        def _():
            dnw_ref[...] = jnp.zeros_like(dnw_ref)

        yv, zv, nwv = y_ref[...], z_ref[...], nw_ref[...]
        _, xhs, rss = _gated_norm(yv, zv, nwv)
        sz, dsz = _silu(zv), _dsilu(zv)
        for g in range(SSD_GROUPS):
            gs = slice(g * gw, (g + 1) * gw)
            d_out = dyn_ref[:, gs]
            xh = xhs[g]
            dnw_ref[0:1, gs] += jnp.sum(d_out * xh, axis=0, keepdims=True)
            dxh = d_out * nwv[:, gs]
            dy2 = rss[g] * (dxh - xh * jnp.mean(dxh * xh, axis=-1, keepdims=True))
            dy_ref[:, gs] = dy2 * sz[:, gs]
            dz_ref[:, gs] = dy2 * yv[:, gs] * dsz[:, gs]

    wide = pl.BlockSpec((t, SSD_INNER), lambda i: (i, 0))
    return pl.pallas_call(
        body, name="l1_gate_bwd", grid=(SEQ // t,),
        in_specs=[wide, wide, wide, pl.BlockSpec((1, SSD_INNER), lambda i: (0, 0))],
        out_specs=[wide, wide, pl.BlockSpec((SUBLANES, SSD_INNER), lambda i: (0, 0))],
        out_shape=[jax.ShapeDtypeStruct((SEQ, SSD_INNER), F32), jax.ShapeDtypeStruct((SEQ, SSD_INNER), F32),
                   jax.ShapeDtypeStruct((SUBLANES, SSD_INNER), F32)],
        compiler_params=_params(("arbitrary",)),
    )(dyn, y, z, nw)


MESH = pl.DeviceIdType.MESH
ANY = pl.BlockSpec(memory_space=pl.ANY)


def _flip(v, bit):
    return 1 - v if bit else v


def _all_gather(blocks, name):
    n = len(blocks)

    def body(*refs):
        x_refs, out_refs = refs[:n], refs[n:2 * n]
        send_sems, recv_sems, local_sems = refs[2 * n:]
        mx, my, mc = lax.axis_index("x"), lax.axis_index("y"), lax.axis_index("c")
        me, sibling = (mx, my, mc), (mx, my, 1 - mc)
        chips = [(1 - mx, my), (mx, 1 - my), (1 - mx, 1 - my)]

        def copy(a, k, block, to, own=False):
            px, py, pc = block
            slot = out_refs[a].at[4 * px + 2 * py + pc]
            return pltpu.make_async_remote_copy(
                src_ref=x_refs[a] if own else slot, dst_ref=slot,
                send_sem=send_sems.at[7 * a + k], recv_sem=recv_sems.at[7 * a + k], device_id=to, device_id_type=MESH)

        mine = [pltpu.make_async_copy(x_refs[a], out_refs[a].at[4 * mx + 2 * my + mc], local_sems.at[a])
                for a in range(n)]
        first = []
        for a in range(n):
            mine[a].start()
            first.append(copy(a, 0, me, sibling, own=True))
            first += [copy(a, 1 + j, me, (*chip, mc), own=True) for j, chip in enumerate(chips)]
        for cp in first:
            cp.start()
        passed = []
        for j, chip in enumerate(chips):
            for a in range(n):
                copy(a, 1 + j, (*chip, mc), me).wait_recv()
                fwd = copy(a, 4 + j, (*chip, mc), sibling)
                fwd.start()
                passed.append(fwd)
        for a in range(n):
            copy(a, 0, sibling, me).wait_recv()
            for j, chip in enumerate(chips):
                copy(a, 4 + j, (*chip, 1 - mc), me).wait_recv()
        for cp in first + passed:
            cp.wait_send()
        for cp in mine:
            cp.wait()

    return pl.pallas_call(
        body, name=name, in_specs=[ANY] * n, out_specs=[ANY] * n,
        out_shape=[jax.ShapeDtypeStruct((N_DEV,) + b.shape, b.dtype) for b in blocks],
        scratch_shapes=[pltpu.SemaphoreType.DMA((7 * n,)), pltpu.SemaphoreType.DMA((7 * n,)),
                        pltpu.SemaphoreType.DMA((n,))],
    )(*blocks)


def _exchange(scatter, bcast, name):
    ns, n = len(scatter), len(scatter) + len(bcast)

    def body(*refs):
        in_refs, out_refs = refs[:n], refs[n:2 * n]
        send_sems, recv_sems, local_sems = refs[2 * n:]
        mx, my, mc = lax.axis_index("x"), lax.axis_index("y"), lax.axis_index("c")
        me = 4 * mx + 2 * my + mc

        def src(a, slot):
            return in_refs[a].at[slot] if a < ns else in_refs[a]

        local = [pltpu.make_async_copy(src(a, me), out_refs[a].at[me], local_sems.at[a]) for a in range(n)]
        for cp in local:
            cp.start()
        copies = []
        for k in range(1, N_DEV):
            px, py, pc = _flip(mx, (k >> 2) & 1), _flip(my, (k >> 1) & 1), _flip(mc, k & 1)
            for a in range(n):
                cp = pltpu.make_async_remote_copy(
                    src_ref=src(a, 4 * px + 2 * py + pc), dst_ref=out_refs[a].at[me],
                    send_sem=send_sems.at[7 * a + k - 1], recv_sem=recv_sems.at[7 * a + k - 1],
                    device_id=(px, py, pc), device_id_type=MESH)
                cp.start()
                copies.append(cp)
        for cp in copies:
            cp.wait()
        for cp in local:
            cp.wait()

    return pl.pallas_call(
        body, name=name, in_specs=[ANY] * n, out_specs=[ANY] * n,
        out_shape=[jax.ShapeDtypeStruct(a.shape, a.dtype) for a in scatter]
        + [jax.ShapeDtypeStruct((N_DEV,) + a.shape, a.dtype) for a in bcast],
        scratch_shapes=[pltpu.SemaphoreType.DMA((7 * n,)), pltpu.SemaphoreType.DMA((7 * n,)),
                        pltpu.SemaphoreType.DMA((n,))],
    )(*scatter, *bcast)


def _segments(col_map, width):
    segs = []
    for lo, hi, arr, alo in col_map:
        for s in range(N_DEV):
            a, b = max(lo, s * width), min(hi, (s + 1) * width)
            if a < b:
                segs.append((s, a - s * width, b - a, arr, alo + a - lo))
    return segs


COPY_ROWS = 256


def _unshard(g8, col_map, widths, name):
    _, r, w = g8.shape
    rb = min(r, COPY_ROWS)
    segs = _segments(col_map, w)

    def body(g_ref, *o_refs):
        for o_ref in o_refs:
            o_ref[...] = jnp.zeros_like(o_ref)
        for s, llo, n, arr, alo in segs:
            o_refs[arr][:, alo:alo + n] = g_ref[s, :, llo:llo + n]

    return pl.pallas_call(
        body, name=name, grid=(r // rb,),
        in_specs=[pl.BlockSpec((N_DEV, rb, w), lambda i: (0, i, 0))],
        out_specs=[pl.BlockSpec((rb, n), lambda i: (i, 0)) for n in widths],
        out_shape=[jax.ShapeDtypeStruct((r, n), g8.dtype) for n in widths],
        compiler_params=_params(("parallel",)),
    )(g8)


def _reshard(srcs, col_map, w, dtype, name):
    r = srcs[0].shape[0]
    rb = min(r, COPY_ROWS)
    segs = _segments(col_map, w)

    def body(*refs):
        o_ref = refs[-1]
        for s, llo, n, arr, alo in segs:
            o_ref[s, :, llo:llo + n] = refs[arr][:, alo:alo + n].astype(dtype)

    return pl.pallas_call(
        body, name=name, grid=(r // rb,),
        in_specs=[pl.BlockSpec((rb, a.shape[1]), lambda i: (i, 0)) for a in srcs],
        out_specs=pl.BlockSpec((N_DEV, rb, w), lambda i: (0, i, 0)),
        out_shape=jax.ShapeDtypeStruct((N_DEV, r, w), dtype),
        compiler_params=_params(("parallel",)),
    )(*srcs)


def _adamw(parts, w, m, v, name):
    r, c = w.shape
    tr = COPY_ROWS if r % COPY_ROWS == 0 else r

    def body(p_ref, w_ref, m_ref, v_ref, g_ref, d_ref, mo_ref, vo_ref):
        g = p_ref[0].astype(F32)
        for s in range(1, N_DEV):
            g = g + p_ref[s].astype(F32)
        mn = ADAM_B1 * m_ref[...] + (1.0 - ADAM_B1) * g
        vn = ADAM_B2 * v_ref[...] + (1.0 - ADAM_B2) * (g * g)
        m_hat = mn / (1.0 - ADAM_B1 ** ADAM_STEP)
        v_hat = vn / (1.0 - ADAM_B2 ** ADAM_STEP)
        g_ref[...] = g
        d_ref[...] = -ADAM_LR * (m_hat / (jnp.sqrt(v_hat) + ADAM_EPS) + ADAM_WD * w_ref[...])
        mo_ref[...] = mn
        vo_ref[...] = vn

    blk = pl.BlockSpec((tr, c), lambda i: (i, 0))
    out = jax.ShapeDtypeStruct((r, c), F32)
    return pl.pallas_call(
        body, name=name, grid=(r // tr,),
        in_specs=[pl.BlockSpec((N_DEV, tr, c), lambda i: (0, i, 0)), blk, blk, blk],
        out_specs=[blk, blk, blk, blk], out_shape=[out, out, out, out],
        compiler_params=_params(("parallel",)),
    )(parts, w, m, v)


BIG = (("ab_w_in", (1024, 244), 1), ("ssd_w_in", (1024, 644), 1), ("ab_w_out", (128, 1024), 0),
       ("ssd_w_out", (256, 1024), 0), ("mla_w_uq", (256, 96), 1), ("mla_w_ukv", (128, 128), 1))
SMALL = (("ab_conv_w", (4, 64), 1), ("ssd_conv_w", (4, 384), 1), ("ssd_conv_b", (384,), 0),
         ("ssd_norm", (256,), 0), ("ssd_ln_g", (128,), 0), ("ssd_ln_b", (128,), 0))
REPL = (("ab_conv_b", (512,)), ("ab_gate_a_w", (8, 64, 64)), ("ab_gate_a_b", (512,)), ("ab_gate_x_w", (8, 64, 64)),
        ("ab_gate_x_b", (512,)), ("ab_lambda", (512,)), ("mla_q_norm", (256,)), ("mla_kv_norm", (128,)),
        ("ab_ln_g", (1024,)), ("ab_ln_b", (1024,)), ("ssd_dt_bias", (32,)), ("ssd_a_log", (32,)), ("ssd_d", (32,)))
SMALL_ROWS = 24
REPL_ROWS = 552

MAP_W0 = ((0, 1920, 0, 0), (1920, 1952, 0, 1984))
MAP_W1 = ((0, 2048, 0, 0), (2048, 5120, 1, 0), (5120, 5152, 2, 0))
MAP_WQ = tuple((96 * hd, 96 * hd + 96, 0, 128 * hd) for hd in range(8))
MAP_WKV = (tuple((128 * hd, 128 * hd + 64, 0, 128 * hd) for hd in range(8))
           + tuple((128 * hd + 64, 128 * hd + 128, 0, 1024 + 64 * hd) for hd in range(8)))
MAP_G0 = ((0, 512, 0, 0), (512, 1536, 1, 0), (1536, 1920, 2, 0), (1920, 1952, 2, 448))


def _pack_rows(pieces, rows, dtype):
    flat = jnp.concatenate([p.reshape(-1).astype(dtype) for p in pieces])
    return jnp.pad(flat, (0, rows * LANES - flat.shape[0])).reshape(rows, LANES)


def _unshard_flat(flat8, off, shape, axis):
    n = math.prod(shape)
    sh = flat8[:, off:off + n].reshape((N_DEV,) + shape)
    if axis == 0:
        return sh.reshape((N_DEV * shape[0],) + shape[1:])
    return jnp.moveaxis(sh, 0, len(shape) - 1).reshape(shape[:-1] + (N_DEV * shape[-1],))


def _to_shards(full, shape, axis):
    if axis == 0:
        return full.reshape(N_DEV, -1)
    sh = full.reshape(shape[:-1] + (N_DEV, shape[-1]))
    return jnp.moveaxis(sh, len(shape) - 1, 0).reshape(N_DEV, -1)


def _pad128(v):
    flat = v.reshape(-1)
    return jnp.pad(flat, (0, (-flat.shape[0]) % LANES))


def _block_diag(w):
    eye = jnp.eye(8, dtype=w.dtype)
    return (eye[:, None, :, None] * w[:, :, None, :]).reshape(RNN_W, RNN_W)


def _block_diag_t(d):
    d4 = d.reshape(8, 64, 8, 64)
    return jnp.stack([d4[hd, :, hd, :] for hd in range(8)])


def _row(v, width=None):
    v = v.reshape(1, -1)
    return v if width is None else jnp.pad(v, ((0, 0), (0, width - v.shape[1])))


def _taps8(w):
    return jnp.pad(w, ((0, 4), (0, 0)))


def kernel(x, positions, ab_w_in, ab_conv_w, ab_conv_b, ab_gate_a_w, ab_gate_a_b, ab_gate_x_w, ab_gate_x_b, ab_lambda, mla_q_norm, mla_kv_norm, mla_w_uq, mla_w_ukv, ab_w_out, ab_ln_g, ab_ln_b, ssd_w_in, ssd_conv_w, ssd_conv_b, ssd_dt_bias, ssd_a_log, ssd_d, ssd_norm, ssd_w_out, ssd_ln_g, ssd_ln_b, loss_target, m_ab_w_in, m_ab_conv_w, m_ab_conv_b, m_ab_gate_a_w, m_ab_gate_a_b, m_ab_gate_x_w, m_ab_gate_x_b, m_ab_lambda, m_mla_q_norm, m_mla_kv_norm, m_mla_w_uq, m_mla_w_ukv, m_ab_w_out, m_ab_ln_g, m_ab_ln_b, m_ssd_w_in, m_ssd_conv_w, m_ssd_conv_b, m_ssd_dt_bias, m_ssd_a_log, m_ssd_d, m_ssd_norm, m_ssd_w_out, m_ssd_ln_g, m_ssd_ln_b, v_ab_w_in, v_ab_conv_w, v_ab_conv_b, v_ab_gate_a_w, v_ab_gate_a_b, v_ab_gate_x_w, v_ab_gate_x_b, v_ab_lambda, v_mla_q_norm, v_mla_kv_norm, v_mla_w_uq, v_mla_w_ukv, v_ab_w_out, v_ab_ln_g, v_ab_ln_b, v_ssd_w_in, v_ssd_conv_w, v_ssd_conv_b, v_ssd_dt_bias, v_ssd_a_log, v_ssd_d, v_ssd_norm, v_ssd_w_out, v_ssd_ln_g, v_ssd_ln_b):
    args = dict(locals())
    w_in = {n: args[n][0] for n, *_ in BIG + SMALL + REPL}
    m_in = {n: args["m_" + n][0] for n, *_ in BIG + SMALL + REPL}
    v_in = {n: args["v_" + n][0] for n, *_ in BIG + SMALL + REPL}

    gathered = _all_gather([w_in[n].astype(MXU_DTYPE) for n, *_ in BIG]
                           + [_pack_rows([w_in[n] for n, *_ in SMALL], SMALL_ROWS, F32)], "gather_params")
    g8 = dict(zip([n for n, *_ in BIG], gathered))
    small8 = gathered[-1].reshape(N_DEV, -1)
    wts = {"wo0": g8["ab_w_out"].reshape(1024, 1024), "wo1": g8["ssd_w_out"].reshape(2048, 1024)}
    wts["w0p"], = _unshard(g8["ab_w_in"], MAP_W0, (2048,), "unshard_w0")
    wts["w1z"], wts["w1x"], wts["w1d"] = _unshard(g8["ssd_w_in"], MAP_W1, (2048, 3072, 128), "unshard_w1")
    wts["wq"], = _unshard(g8["mla_w_uq"], MAP_WQ, (1024,), "unshard_wq")
    wts["wkv"], = _unshard(g8["mla_w_ukv"], MAP_WKV, (1536,), "unshard_wkv")
    small, off = {}, 0
    for n, shape, axis in SMALL:
        small[n] = _unshard_flat(small8, off, shape, axis)
        off += math.prod(shape)

    pc, loss_part, grad_x = _local_step(x[0], positions[0], loss_target[0], wts, small, w_in)

    bf = MXU_DTYPE
    send = [_reshard([pc["g_rnn"], pc["g_gate"], pc["g_tail"]], MAP_G0, 244, bf, "reshard_w0"),
            _reshard([pc["g_z"], pc["g_xbc"], pc["g_dt"]], MAP_W1, 644, bf, "reshard_w1"),
            pc["g_wo0"].astype(bf).reshape(N_DEV, 128, 1024), pc["g_wo1"].astype(bf).reshape(N_DEV, 256, 1024),
            _reshard([pc["g_wq"]], MAP_WQ, 96, bf, "reshard_wq"), _reshard([pc["g_wkv"]], MAP_WKV, 128, bf, "reshard_wkv")]
    small_send = jnp.concatenate([_to_shards(pc[n], shape, axis) for n, shape, axis in SMALL], axis=1)
    small_send = jnp.pad(small_send, ((0, 0), (0, SMALL_ROWS * LANES - small_send.shape[1]))).reshape(N_DEV, SMALL_ROWS, LANES)
    repl_part = _pack_rows([_pad128(pc[n]) for n, _ in REPL], REPL_ROWS, F32)
    recv = _exchange(send + [small_send], [repl_part], "exchange_grads")

    outs = {}
    kinds = ("grad", "delta", "new_m", "new_v")
    for (n, *_), parts in zip(BIG, recv):
        for kind, res in zip(kinds, _adamw(parts, w_in[n], m_in[n], v_in[n], "adamw_" + n)):
            outs[kind, n] = res[None]
    packs = [_pack_rows([src[n] for n, *_ in SMALL], SMALL_ROWS, F32) for src in (w_in, m_in, v_in)]
    res_s = [r.reshape(-1) for r in _adamw(recv[-2], *packs, "adamw_small")]
    packs = [_pack_rows([_pad128(src[n]) for n, _ in REPL], REPL_ROWS, F32) for src in (w_in, m_in, v_in)]
    res_r = [r.reshape(-1) for r in _adamw(recv[-1], *packs, "adamw_repl")]
    off = 0
    for n, shape, _ in SMALL:
        size = math.prod(shape)
        for kind, flat in zip(kinds, res_s):
            outs[kind, n] = flat[off:off + size].reshape(args[n].shape)
        off += size
    off = 0
    for n, shape in REPL:
        size = math.prod(shape)
        for kind, flat in zip(("grad", "delta", "new_m", "new_v"), res_r):
            outs[kind, n] = flat[off:off + size].reshape(args[n].shape)
        off += size + (-size) % LANES

    loss = lax.psum(loss_part, ("x", "y", "c"))
    order = ["ab_w_in", "ab_conv_w", "ab_conv_b", "ab_gate_a_w", "ab_gate_a_b", "ab_gate_x_w", "ab_gate_x_b",
             "ab_lambda", "mla_q_norm", "mla_kv_norm", "mla_w_uq", "mla_w_ukv", "ab_w_out", "ab_ln_g", "ab_ln_b",
             "ssd_w_in", "ssd_conv_w", "ssd_conv_b", "ssd_dt_bias", "ssd_a_log", "ssd_d", "ssd_norm", "ssd_w_out",
             "ssd_ln_g", "ssd_ln_b"]
    return (loss, grad_x[None], *[outs[kind, n] for kind in ("grad", "delta", "new_m", "new_v") for n in order])


def _local_step(x, pos, target, wts, full, rep):
    bf = MXU_DTYPE
    inv_freq = 10000.0 ** (-jnp.arange(0, 32, 2, dtype=F32) / 32)
    ang = pos.astype(F32)[:, None] * inv_freq
    cos, sin = jnp.cos(ang), jnp.sin(ang)
    zeros = lambda n: jnp.zeros((SEQ, n), F32)
    tc = jnp.concatenate([jnp.ones((SEQ, 64), F32), cos, cos, zeros(32)], axis=1)
    tsa = jnp.concatenate([zeros(64), -sin, zeros(48)], axis=1)
    tsb = jnp.concatenate([zeros(80), sin, zeros(32)], axis=1)

    w0p, wq, wkv, w1z, w1x, w1d = (wts[k] for k in ("w0p", "wq", "wkv", "w1z", "w1x", "w1d"))
    wo0, wo1 = wts["wo0"], wts["wo1"]
    wa, wxg = _block_diag(rep["ab_gate_a_w"]).astype(bf), _block_diag(rep["ab_gate_x_w"]).astype(bf)
    cw0, cb0 = _taps8(full["ab_conv_w"]), _row(rep["ab_conv_b"])
    ba, bx, lam = _row(rep["ab_gate_a_b"]), _row(rep["ab_gate_x_b"]), _row(rep["ab_lambda"])
    qn_w, kn_w = _row(rep["mla_q_norm"]), _row(rep["mla_kv_norm"])
    g0, b0 = _row(rep["ab_ln_g"]), _row(rep["ab_ln_b"])
    cw1, cb1 = _taps8(full["ssd_conv_w"]), _row(full["ssd_conv_b"])
    dt_bias, a_log = _row(rep["ssd_dt_bias"], LANES), _row(rep["ssd_a_log"], LANES)
    d_x = _row(jnp.repeat(rep["ssd_d"], SSD_P))
    nw, g1, b1 = _row(full["ssd_norm"]), _row(full["ssd_ln_g"]), _row(full["ssd_ln_b"])
    tril = jnp.tril(jnp.ones((SSD_L, SSD_L), F32))
    expand = (jnp.arange(LANES)[:, None] == jnp.arange(SSD_INNER)[None, :] // SSD_P).astype(F32)
    expand_t = expand.T

    proj0 = _mm(x, w0p, "nn", name="l0_in")
    xc, h = _rglru_fwd(proj0, cw0, cb0, wa, ba, wxg, bx, lam)
    qn, kn, kr = _mla_norm_fwd(proj0, qn_w, kn_w, tc, tsa, tsb)
    qraw = _mm(qn, wq, "nn", name="mla_q")
    kvraw = _mm(kn, wkv, "nn", name="mla_kv", tn=512)
    qc, kc, vc = _mla_assemble(qraw, kvraw, kr, tc, tsa, tsb)
    o, lse = _flash_fwd(qc, kc, vc)
    y0, v0, x1 = _l0_out(h, o, proj0, x, wo0, g0, b0)

    z = _mm(x1, w1z, "nn", name="l1_in_z")
    xbc = _mm(x1, w1x, "nn", name="l1_in_xbc")
    dt_raw = _mm(x1, w1d, "nn", name="l1_in_dt")
    pre, act = _ssd_conv_fwd(xbc, cw1, cb1)
    ys, hprev = _ssd_scan_fwd(act, dt_raw, dt_bias, a_log, d_x, tril, expand)
    yn, dv1, dgb1, loss8 = _l1_out(ys, z, nw, wo1, x1, g1, b1, target)

    g_wo1 = _mm(yn, dv1, "tn", name="l1_dwout")
    dyn = _mm(dv1, wo1, "nt", name="l1_dyn", tn=1024)
    dys, dz, dnw = _l1_gate_bwd(dyn, ys, z, nw)
    dact, ddt_raw, dvec1 = _ssd_scan_bwd(dys, act, dt_raw, hprev, dt_bias, a_log, d_x, tril, expand, expand_t)
    dxbc, dcw1 = _ssd_conv_bwd(dact, pre, xbc, cw1)
    g_z, g_xbc = _mm(x1, dz, "tn", name="l1_dw_z"), _mm(x1, dxbc, "tn", name="l1_dw_xbc")
    g_dt = _mm(x1, ddt_raw, "tn", name="l1_dw_dt")
    dx1 = _mm(dz, w1z, "nt", name="l1_dx_z", add=dv1, add_scale=DN_ALPHA)
    dx1 = _mm(dxbc, w1x, "nt", name="l1_dx_xbc", add=dx1)
    dx1 = _mm(ddt_raw, w1d, "nt", name="l1_dx_dt", add=dx1)

    dv0, dgb0 = _ln_bwd_call(v0, dx1, g0)
    g_wo0 = _mm(y0, dv0, "tn", name="l0_dwout")
    dy0 = _mm(dv0, wo0, "nt", name="l0_dy")
    dh, do, dgate = _gate_bwd(dy0, h, o, proj0)
    dq, dk, dvv = _flash_bwd(qc, kc, vc, o, do, lse)
    dqraw, dkvraw, dkr = _mla_bwd_rope(dq, dk, dvv, tc, tsa, tsb)
    g_wq = _mm(qn, dqraw, "tn", name="mla_dwq", tm=256)
    g_wkv = _mm(kn, dkvraw, "tn", name="mla_dwkv", tm=128, tn=512)
    dqn = _mm(dqraw, wq, "nt", name="mla_dqn", tn=256)
    dkn = _mm(dkvraw, wkv, "nt", name="mla_dkn", tn=128)
    dtail, dqnw, dknw = _mla_norm_bwd(proj0, dqn, dkn, dkr, qn_w, kn_w)
    dxr, g_wa, g_wx, dvec0 = _rglru_bwd(dh, xc, h, proj0, cw0, wa, ba, wxg, bx, lam)
    g_tail = _mm(x, dtail, "tn", name="l0_dw_tail")
    g_rnn, g_gate = _mm(x, dxr, "tn", name="l0_dw_rnn"), _mm(x, dgate, "tn", name="l0_dw_gate")
    dx = _mm(dxr, w0p[:, :512], "nt", name="l0_dx_rnn", add=dv0, add_scale=DN_ALPHA)
    dx = _mm(dgate, w0p[:, 512:1536], "nt", name="l0_dx_gate", add=dx)
    dx = _mm(dtail, w0p[:, 1536:], "nt", name="l0_dx_tail", add=dx)

    grads = {
        "g_rnn": g_rnn, "g_gate": g_gate, "g_tail": g_tail, "g_z": g_z, "g_xbc": g_xbc, "g_dt": g_dt,
        "g_wo0": g_wo0, "g_wo1": g_wo1, "g_wq": g_wq, "g_wkv": g_wkv,
        "ab_conv_w": dvec0[4:8], "ssd_conv_w": dcw1[0:4], "ssd_conv_b": dcw1[4], "ssd_norm": dnw[0],
        "ssd_ln_g": dgb1[0], "ssd_ln_b": dgb1[1],
        "ab_conv_b": dvec0[3], "ab_gate_a_w": _block_diag_t(g_wa), "ab_gate_a_b": dvec0[0],
        "ab_gate_x_w": _block_diag_t(g_wx), "ab_gate_x_b": dvec0[1], "ab_lambda": dvec0[2],
        "mla_q_norm": dqnw[0], "mla_kv_norm": dknw[0], "ab_ln_g": dgb0[0], "ab_ln_b": dgb0[1],
        "ssd_dt_bias": dvec1[0, :32], "ssd_a_log": dvec1[1, :32], "ssd_d": dvec1[2, :32],
    }
    return grads, loss8[0, 0], dx
```

```python
import math

import jax
import jax.numpy as jnp
from jax import lax
from jax.experimental import pallas as pl
from jax.experimental.pallas import tpu as pltpu

F32 = jnp.float32
MXU_DTYPE = jnp.bfloat16

N_DEV = 8
SEQ = 4096
D_MODEL = 1024
DN_ALPHA = 4.0 ** 0.25
RNN_W = 512
MLA_HEADS = 8
ATT_SCALE = 96.0 ** -0.5
ATT_C = ATT_SCALE * math.log2(math.e)
RG_C = 8.0
SSD_INNER = 2048
SSD_HEADS = 32
SSD_P = 64
SSD_GROUPS = 4
SSD_N = 128
SSD_L = 128
SSD_CONV = 3072
LANES = 128
SUBLANES = 8
VMEM_LIMIT = 56 * 1024 * 1024

ADAM_LR, ADAM_B1, ADAM_B2, ADAM_EPS, ADAM_WD, ADAM_STEP = 0.001, 0.9, 0.999, 1e-08, 0.01, 10

HIGHEST = lax.Precision.HIGHEST


def _params(sem, limit=VMEM_LIMIT):
    return pltpu.CompilerParams(dimension_semantics=sem, vmem_limit_bytes=limit)


def _dot(a, b):
    return lax.dot_general(a, b, (((1,), (0,)), ((), ())), preferred_element_type=F32)


def _dot_nt(a, b):
    return lax.dot_general(a, b, (((1,), (1,)), ((), ())), preferred_element_type=F32)


def _dot_tn(a, b):
    return lax.dot_general(a, b, (((0,), (0,)), ((), ())), preferred_element_type=F32)


def _dot_hi(a, b):
    return lax.dot_general(a, b, (((1,), (0,)), ((), ())), precision=HIGHEST, preferred_element_type=F32)


def _mx(v):
    return v.astype(MXU_DTYPE)


def _sigmoid(v):
    return 1.0 / (1.0 + jnp.exp(-v))


def _log1p_pos(e):
    poly = e * (1.0 - e * (0.5 - e * (1.0 / 3.0 - e * 0.25)))
    return jnp.where(e < 0.01, poly, jnp.log(1.0 + e))


def _softplus(v):
    return jnp.maximum(v, 0.0) + _log1p_pos(jnp.exp(-jnp.abs(v)))


def _neg_expm1(v):
    poly = -v * (1.0 + v * (0.5 + v * (1.0 / 6.0 + v * (1.0 / 24.0 + v * (1.0 / 120.0)))))
    return jnp.where(jnp.abs(v) < 0.1, poly, 1.0 - jnp.exp(v))


def _silu(v):
    return v * _sigmoid(v)


def _dsilu(v):
    s = _sigmoid(v)
    return s * (1.0 + v * (1.0 - s))


def _mm(a, b, mode, *, name, add=None, add_scale=1.0, out_dtype=F32, tm=None, tn=1024, tk=512):
    if mode == "tn":
        kdim, m = a.shape
        n = b.shape[1]
        tm, tn, tk = min(tm or 1024, m), min(tn, n), min(tk, kdim)

        def body_tn(a_ref, b_ref, o_ref):
            @pl.when(pl.program_id(2) == 0)
            def _():
                o_ref[...] = jnp.zeros_like(o_ref)

            o_ref[...] += _dot_tn(_mx(a_ref[...]), _mx(b_ref[...]))

        return pl.pallas_call(
            body_tn, name=name, grid=(m // tm, n // tn, kdim // tk),
            in_specs=[pl.BlockSpec((tk, tm), lambda i, j, k: (k, i)), pl.BlockSpec((tk, tn), lambda i, j, k: (k, j))],
            out_specs=pl.BlockSpec((tm, tn), lambda i, j, k: (i, j)),
            out_shape=jax.ShapeDtypeStruct((m, n), F32),
            compiler_params=_params(("parallel", "parallel", "arbitrary")),
        )(a, b)

    m, kdim = a.shape
    n = b.shape[1] if mode == "nn" else b.shape[0]
    tm, tn = min(tm or 512, m), min(tn, n)
    has_add = add is not None

    def body(*refs):
        a_ref, b_ref = refs[0], refs[1]
        o_ref = refs[-1]
        av, bv = _mx(a_ref[...]), _mx(b_ref[...])
        acc = _dot(av, bv) if mode == "nn" else _dot_nt(av, bv)
        if has_add:
            acc = acc + add_scale * refs[2][...]
        o_ref[...] = acc.astype(out_dtype)

    b_spec = (pl.BlockSpec((kdim, tn), lambda i, j: (0, j)) if mode == "nn"
              else pl.BlockSpec((tn, kdim), lambda i, j: (j, 0)))
    in_specs = [pl.BlockSpec((tm, kdim), lambda i, j: (i, 0)), b_spec]
    args = [a, b]
    if has_add:
        in_specs.append(pl.BlockSpec((tm, tn), lambda i, j: (i, j)))
        args.append(add)
    return pl.pallas_call(
        body, name=name, grid=(m // tm, n // tn), in_specs=in_specs,
        out_specs=pl.BlockSpec((tm, tn), lambda i, j: (i, j)),
        out_shape=jax.ShapeDtypeStruct((m, n), out_dtype),
        compiler_params=_params(("parallel", "parallel")),
    )(*args)


def _shift_down(blk, halo, s):
    if s == 0:
        return blk
    t = blk.shape[0]
    r = pltpu.roll(blk, s, 0)
    hr = pltpu.roll(halo, s, 0)
    row8 = lax.broadcasted_iota(jnp.int32, hr.shape, 0)
    head = jnp.where(row8 < s, hr, r[:SUBLANES])
    return jnp.concatenate([head, r[SUBLANES:]], axis=0) if t > SUBLANES else head


def _shift_up(blk, halo, s):
    if s == 0:
        return blk
    t = blk.shape[0]
    r = pltpu.roll(blk, t - s, 0)
    hr = pltpu.roll(halo, SUBLANES - s, 0)
    row8 = lax.broadcasted_iota(jnp.int32, hr.shape, 0)
    tail = jnp.where(row8 >= SUBLANES - s, hr, r[t - SUBLANES:])
    return jnp.concatenate([r[:t - SUBLANES], tail], axis=0) if t > SUBLANES else tail


def _scan_down(a, u):
    t = a.shape[0]
    row = lax.broadcasted_iota(jnp.int32, a.shape, 0)
    d = 1
    while d < t:
        keep = row >= d
        a_sh = jnp.where(keep, pltpu.roll(a, d, 0), 1.0)
        u_sh = jnp.where(keep, pltpu.roll(u, d, 0), 0.0)
        u = a * u_sh + u
        a = a * a_sh
        d *= 2
    return a, u


def _scan_up(a, u):
    t = a.shape[0]
    row = lax.broadcasted_iota(jnp.int32, a.shape, 0)
    d = 1
    while d < t:
        keep = row < t - d
        a_sh = jnp.where(keep, pltpu.roll(a, t - d, 0), 1.0)
        u_sh = jnp.where(keep, pltpu.roll(u, t - d, 0), 0.0)
        u = a * u_sh + u
        a = a * a_sh
        d *= 2
    return a, u


def _conv4(blk, halo, cw, cb):
    out = cb + blk * cw[3:4]
    for k in range(3):
        out = out + _shift_down(blk, halo, 3 - k) * cw[k:k + 1]
    return out


RG_T = 512


def _rg_gates(xc, wa, ba, wx, bx, lam):
    xcb = _mx(xc)
    r = _sigmoid(_dot(xcb, wa) + ba)
    ig = _sigmoid(_dot(xcb, wx) + bx)
    sp = _softplus(-lam)
    la = (-RG_C * r) * sp
    a = jnp.exp(la)
    mult = jnp.sqrt(_neg_expm1(2.0 * la))
    return r, ig, sp, a, mult


def _rglru_fwd(proj0, cw8, cb, wa, ba, wx, bx, lam):
    t, w = RG_T, RNN_W
    nb = SEQ // t

    def body(x_ref, halo_ref, cw_ref, cb_ref, wa_ref, ba_ref, wx_ref, bx_ref, lam_ref, xc_ref, h_ref, carry):
        i = pl.program_id(0)

        @pl.when(i == 0)
        def _():
            carry[...] = jnp.zeros_like(carry)

        blk = x_ref[...]
        halo = jnp.where(i > 0, halo_ref[...], 0.0)
        xc = _conv4(blk, halo, cw_ref[...], cb_ref[...])
        _, ig, _, a, mult = _rg_gates(xc, wa_ref[...], ba_ref[...], wx_ref[...], bx_ref[...], lam_ref[...])
        u = mult * (ig * xc)
        big_a, big_u = _scan_down(a, u)
        h = big_a * carry[SUBLANES - 1:SUBLANES, :] + big_u
        carry[...] = h[t - SUBLANES:]
        xc_ref[...] = xc
        h_ref[...] = h

    vec = pl.BlockSpec((1, w), lambda i: (0, 0))
    mat = pl.BlockSpec((w, w), lambda i: (0, 0))
    return pl.pallas_call(
        body, name="rglru_fwd", grid=(nb,),
        in_specs=[pl.BlockSpec((t, w), lambda i: (i, 0)),
                  pl.BlockSpec((SUBLANES, w), lambda i: (jnp.maximum(i * (t // SUBLANES) - 1, 0), 0)),
                  pl.BlockSpec((SUBLANES, w), lambda i: (0, 0)), vec, mat, vec, mat, vec, vec],
        out_specs=[pl.BlockSpec((t, w), lambda i: (i, 0)), pl.BlockSpec((t, w), lambda i: (i, 0))],
        out_shape=[jax.ShapeDtypeStruct((SEQ, w), F32), jax.ShapeDtypeStruct((SEQ, w), F32)],
        scratch_shapes=[pltpu.VMEM((SUBLANES, w), F32)],
        compiler_params=_params(("arbitrary",)),
    )(proj0, proj0, cw8, cb, wa, ba, wx, bx, lam)


def _rglru_bwd(dh, xc, h, proj0, cw8, wa, ba, wx, bx, lam):
    t, w = RG_T, RNN_W
    nb = SEQ // t
    tb = t // SUBLANES

    def body(dh_ref, xc_ref, h_ref, hh_ref, x_ref, xh_ref, cw_ref, wa_ref, ba_ref, wx_ref, bx_ref, lam_ref,
             dx_ref, dwa_ref, dwx_ref, dvec_ref, gcarry, dxc_next):
        i = pl.program_id(0)
        rev = nb - 1 - i

        @pl.when(i == 0)
        def _():
            gcarry[...] = jnp.zeros_like(gcarry)
            dxc_next[...] = jnp.zeros_like(dxc_next)
            dwa_ref[...] = jnp.zeros_like(dwa_ref)
            dwx_ref[...] = jnp.zeros_like(dwx_ref)
            dvec_ref[...] = jnp.zeros_like(dvec_ref)

        xc = xc_ref[...]
        wa_v, wx_v = wa_ref[...], wx_ref[...]
        lam_v = lam_ref[...]
        r, ig, sp, a, mult = _rg_gates(xc, wa_v, ba_ref[...], wx_v, bx_ref[...], lam_v)
        dhv = dh_ref[...]
        big_a, big_u = _scan_up(a, a * dhv)
        gg = big_a * gcarry[0:1, :] + big_u
        g = dhv + _shift_up(gg, gcarry[...], 1)
        gcarry[...] = gg[:SUBLANES]
        hhalo = jnp.where(rev > 0, hh_ref[...], 0.0)
        da = g * _shift_down(h_ref[...], hhalo, 1)
        d_mult = g * (ig * xc)
        d_i = g * (mult * xc)
        dxc = g * (mult * ig)
        d_la = da * a - d_mult * (a * a) / mult
        d_r = d_la * (-RG_C * sp)
        d_sp = jnp.sum(d_la * (-RG_C * r), axis=0, keepdims=True)
        d_pa = d_r * r * (1.0 - r)
        d_px = d_i * ig * (1.0 - ig)
        d_pab, d_pxb = _mx(d_pa), _mx(d_px)
        dxc = dxc + _dot_nt(d_pab, wa_v) + _dot_nt(d_pxb, wx_v)
        xcb = _mx(xc)
        dwa_ref[...] += _dot_tn(xcb, d_pab)
        dwx_ref[...] += _dot_tn(xcb, d_pxb)
        dvec_ref[0:1, :] += jnp.sum(d_pa, axis=0, keepdims=True)
        dvec_ref[1:2, :] += jnp.sum(d_px, axis=0, keepdims=True)
        dvec_ref[2:3, :] += d_sp * (-_sigmoid(-lam_v))
        dvec_ref[3:4, :] += jnp.sum(dxc, axis=0, keepdims=True)
        xblk = x_ref[...]
        xhalo = jnp.where(rev > 0, xh_ref[...], 0.0)
        cw = cw_ref[...]
        dx = dxc * cw[3:4]
        nxt = dxc_next[...]
        for k in range(4):
            dvec_ref[4 + k:5 + k, :] += jnp.sum(dxc * _shift_down(xblk, xhalo, 3 - k), axis=0, keepdims=True)
            if k < 3:
                dx = dx + _shift_up(dxc, nxt, 3 - k) * cw[k:k + 1]
        dxc_next[...] = dxc[:SUBLANES]
        dx_ref[...] = _mx(dx)

    blk = pl.BlockSpec((t, w), lambda i: (nb - 1 - i, 0))
    halo = pl.BlockSpec((SUBLANES, w), lambda i: (jnp.maximum((nb - 1 - i) * tb - 1, 0), 0))
    vec = pl.BlockSpec((1, w), lambda i: (0, 0))
    mat = pl.BlockSpec((w, w), lambda i: (0, 0))
    return pl.pallas_call(
        body, name="rglru_bwd", grid=(nb,),
        in_specs=[blk, blk, blk, halo, blk, halo, pl.BlockSpec((SUBLANES, w), lambda i: (0, 0)), mat, vec, mat, vec, vec],
        out_specs=[blk, mat, mat, pl.BlockSpec((16, w), lambda i: (0, 0))],
        out_shape=[jax.ShapeDtypeStruct((SEQ, w), MXU_DTYPE), jax.ShapeDtypeStruct((w, w), F32),
                   jax.ShapeDtypeStruct((w, w), F32), jax.ShapeDtypeStruct((16, w), F32)],
        scratch_shapes=[pltpu.VMEM((SUBLANES, w), F32), pltpu.VMEM((SUBLANES, w), F32)],
        compiler_params=_params(("arbitrary",)),
    )(dh, xc, h, h, proj0, proj0, cw8, wa, ba, wx, bx, lam)


MLA_T = 512


def _rope(v, c, sa, sb):
    return v * c + pltpu.roll(v, LANES - 16, 1) * sa + pltpu.roll(v, 16, 1) * sb


def _rope_t(dv, c, sa, sb):
    return dv * c + pltpu.roll(dv * sa, 16, 1) + pltpu.roll(dv * sb, LANES - 16, 1)


def _rms(v, g, eps=1e-6):
    rs = lax.rsqrt(jnp.mean(v * v, axis=-1, keepdims=True) + eps)
    return v * rs * g, rs


def _mla_norm_fwd(proj0, q_norm, kv_norm, tc, tsa, tsb):
    t = MLA_T

    def body(cq_ref, ck_ref, qn_ref, kn_ref, c_ref, sa_ref, sb_ref, oq_ref, ok_ref, okr_ref):
        oq_ref[...] = _mx(_rms(cq_ref[...], qn_ref[...])[0])
        ck = ck_ref[...]
        ok_ref[...] = _mx(_rms(ck[:, :LANES], kn_ref[...])[0])
        okr_ref[...] = _rope(ck[:, LANES:], c_ref[...], sa_ref[...], sb_ref[...])

    tab = pl.BlockSpec((t, LANES), lambda i: (i, 0))
    return pl.pallas_call(
        body, name="mla_norm_fwd", grid=(SEQ // t,),
        in_specs=[pl.BlockSpec((t, 256), lambda i: (i, 6)), pl.BlockSpec((t, 256), lambda i: (i, 7)),
                  pl.BlockSpec((1, 256), lambda i: (0, 0)), pl.BlockSpec((1, LANES), lambda i: (0, 0)), tab, tab, tab],
        out_specs=[pl.BlockSpec((t, 256), lambda i: (i, 0)), tab, tab],
        out_shape=[jax.ShapeDtypeStruct((SEQ, 256), MXU_DTYPE), jax.ShapeDtypeStruct((SEQ, LANES), MXU_DTYPE),
                   jax.ShapeDtypeStruct((SEQ, LANES), F32)],
        compiler_params=_params(("parallel",)),
    )(proj0, proj0, q_norm, kv_norm, tc, tsa, tsb)


def _mla_assemble(qraw, kvraw, kr, tc, tsa, tsb):
    t = MLA_T

    def body(q_ref, k_ref, v_ref, kr_ref, c_ref, sa_ref, sb_ref, oq_ref, ok_ref, ov_ref):
        c, sa, sb, krv = c_ref[...], sa_ref[...], sb_ref[...], kr_ref[...]
        for hd in range(MLA_HEADS):
            sl = slice(hd * LANES, (hd + 1) * LANES)
            oq_ref[:, sl] = _mx(_rope(q_ref[:, sl], c, sa, sb))
            ok_ref[:, sl] = _mx(k_ref[:, sl] + krv)
        ov_ref[...] = _mx(v_ref[...])

    tab = pl.BlockSpec((t, LANES), lambda i: (i, 0))
    wide = pl.BlockSpec((t, 1024), lambda i: (i, 0))
    return pl.pallas_call(
        body, name="mla_assemble", grid=(SEQ // t,),
        in_specs=[wide, wide, pl.BlockSpec((t, 512), lambda i: (i, 2)), tab, tab, tab, tab],
        out_specs=[wide, wide, pl.BlockSpec((t, 512), lambda i: (i, 0))],
        out_shape=[jax.ShapeDtypeStruct((SEQ, 1024), MXU_DTYPE), jax.ShapeDtypeStruct((SEQ, 1024), MXU_DTYPE),
                   jax.ShapeDtypeStruct((SEQ, 512), MXU_DTYPE)],
        compiler_params=_params(("parallel",)),
    )(qraw, kvraw, kvraw, kr, tc, tsa, tsb)


ATT_T = 512


def _flash_fwd(q, k, v):
    t = ATT_T
    nb = SEQ // t

    steps = [(qi, ki) for qi in range(nb) for ki in range(qi + 1)]
    qi_tab = jnp.asarray([s[0] for s in steps], jnp.int32)
    ki_tab = jnp.asarray([s[1] for s in steps], jnp.int32)

    def body(qi_ref, ki_ref, q_ref, k_ref, v_ref, o_ref, lse_ref, m_sc, acc_sc):
        step = pl.program_id(1)
        qi, ki = qi_ref[step], ki_ref[step]

        @pl.when(ki == 0)
        def _():
            m_sc[...] = jnp.full_like(m_sc, -jnp.inf)
            acc_sc[...] = jnp.zeros_like(acc_sc)

        def update(diagonal):
            vv = v_ref[...]
            lane_v = lax.broadcasted_iota(jnp.int32, vv.shape, 1)
            for hd in range(2):
                sl = slice(hd * LANES, (hd + 1) * LANES)
                s = _dot_nt(q_ref[:, sl], k_ref[:, sl])
                if diagonal:
                    s = jnp.where(lax.broadcasted_iota(jnp.int32, (t, t), 1)
                                  <= lax.broadcasted_iota(jnp.int32, (t, t), 0), s, -jnp.inf)
                m_prev = m_sc[hd]
                m_new = jnp.maximum(m_prev, jnp.max(s, axis=1, keepdims=True))
                p = jnp.exp2((s - m_new[:, :1]) * ATT_C)
                m_sc[hd] = m_new
                vh = jnp.where((lane_v >= hd * 64) & (lane_v < (hd + 1) * 64), vv, jnp.ones_like(vv))
                acc_sc[hd] = acc_sc[hd] * jnp.exp2((m_prev - m_new) * ATT_C) + _dot(_mx(p), vh)

        @pl.when(ki < qi)
        def _():
            update(False)

        @pl.when(ki == qi)
        def _():
            update(True)
            first = lax.broadcasted_iota(jnp.int32, (t, LANES), 1) < 64
            a0, a1 = acc_sc[0], acc_sc[1]
            l0, l1 = pltpu.roll(a0, 64, 1), pltpu.roll(a1, 64, 1)
            o_ref[...] = jnp.where(first, a0 / l0, a1 / l1)
            lse_ref[0] = jnp.where(first, m_sc[0] * ATT_SCALE + jnp.log(l0), m_sc[1] * ATT_SCALE + jnp.log(l1))

    grid_spec = pltpu.PrefetchScalarGridSpec(
        num_scalar_prefetch=2, grid=(4, len(steps)),
        in_specs=[pl.BlockSpec((t, 256), lambda p, s, qt, kt: (qt[s], p)),
                  pl.BlockSpec((t, 256), lambda p, s, qt, kt: (kt[s], p)),
                  pl.BlockSpec((t, LANES), lambda p, s, qt, kt: (kt[s], p))],
        out_specs=[pl.BlockSpec((t, LANES), lambda p, s, qt, kt: (qt[s], p)),
                   pl.BlockSpec((1, t, LANES), lambda p, s, qt, kt: (p, qt[s], 0))],
        scratch_shapes=[pltpu.VMEM((2, t, LANES), F32), pltpu.VMEM((2, t, LANES), F32)])
    return pl.pallas_call(
        body, name="flash_fwd", grid_spec=grid_spec,
        out_shape=[jax.ShapeDtypeStruct((SEQ, 512), F32), jax.ShapeDtypeStruct((4, SEQ, LANES), F32)],
        compiler_params=_params(("parallel", "arbitrary")),
    )(qi_tab, ki_tab, q, k, v)


def _flash_bwd(q, k, v, o, do, lse):
    t = ATT_T
    nb = SEQ // t

    steps = [(qi, ki) for ki in range(nb) for qi in range(ki, nb)]
    qi_tab = jnp.asarray([s[0] for s in steps], jnp.int32)
    ki_tab = jnp.asarray([s[1] for s in steps], jnp.int32)
    log2e = math.log2(math.e)

    def body(qi_ref, ki_ref, q_ref, k_ref, v_ref, o_ref, do_ref, lse_ref, dq_ref, dk_ref, dv_ref):
        step = pl.program_id(1)
        qi, ki = qi_ref[step], ki_ref[step]

        @pl.when(step == 0)
        def _():
            dq_ref[...] = jnp.zeros_like(dq_ref)

        @pl.when(qi == ki)
        def _():
            dk_ref[...] = jnp.zeros_like(dk_ref)
            dv_ref[...] = jnp.zeros_like(dv_ref)

        def update(diagonal):
            dov, ov, vv = do_ref[...], o_ref[...], v_ref[...]
            lse2 = lse_ref[0] * log2e
            lane = lax.broadcasted_iota(jnp.int32, (t, LANES), 1)
            prod = dov * ov
            qrows = pl.ds(pl.multiple_of(qi * t, t), t)
            dv_acc = jnp.zeros((t, LANES), F32)
            for hd in range(2):
                sl = slice(hd * LANES, (hd + 1) * LANES)
                mine = (lane >= hd * 64) & (lane < (hd + 1) * 64)
                qh, kh = q_ref[:, sl], k_ref[:, sl]
                p = jnp.exp2(_dot_nt(qh, kh) * ATT_C - lse2[:, hd * 64:hd * 64 + 1])
                if diagonal:
                    p = jnp.where(lax.broadcasted_iota(jnp.int32, (t, t), 1)
                                  <= lax.broadcasted_iota(jnp.int32, (t, t), 0), p, 0.0)
                do_h = jnp.where(mine, dov, 0.0)
                delta = jnp.sum(jnp.where(mine, prod, 0.0), axis=1, keepdims=True)
                dp = _dot_nt(_mx(do_h), vv)
                ds = _mx(p * (dp - delta) * ATT_SCALE)
                dv_acc = dv_acc + jnp.where(mine, _dot_tn(_mx(p), _mx(dov)), 0.0)
                dk_ref[:, sl] += _dot_tn(ds, qh)
                dq_ref[qrows, sl] += _dot(ds, kh)
            dv_ref[...] += dv_acc

        @pl.when(qi > ki)
        def _():
            update(False)

        @pl.when(qi == ki)
        def _():
            update(True)

    qmap = lambda p, s, qt, kt: (qt[s], p)
    kmap = lambda p, s, qt, kt: (kt[s], p)
    grid_spec = pltpu.PrefetchScalarGridSpec(
        num_scalar_prefetch=2, grid=(4, len(steps)),
        in_specs=[pl.BlockSpec((t, 256), qmap), pl.BlockSpec((t, 256), kmap), pl.BlockSpec((t, LANES), kmap),
                  pl.BlockSpec((t, LANES), qmap), pl.BlockSpec((t, LANES), qmap),
                  pl.BlockSpec((1, t, LANES), lambda p, s, qt, kt: (p, qt[s], 0))],
        out_specs=[pl.BlockSpec((SEQ, 256), lambda p, s, qt, kt: (0, p)), pl.BlockSpec((t, 256), kmap),
                   pl.BlockSpec((t, LANES), kmap)])
    return pl.pallas_call(
        body, name="flash_bwd", grid_spec=grid_spec,
        out_shape=[jax.ShapeDtypeStruct((SEQ, 1024), F32), jax.ShapeDtypeStruct((SEQ, 1024), F32),
                   jax.ShapeDtypeStruct((SEQ, 512), F32)],
        compiler_params=_params(("parallel", "arbitrary")),
    )(qi_tab, ki_tab, q, k, v, o, do, lse)


def _mla_bwd_rope(dq, dk, dv, tc, tsa, tsb):
    t = MLA_T

    def body(dq_ref, dk_ref, dv_ref, c_ref, sa_ref, sb_ref, oq_ref, okv_ref, okr_ref):
        c, sa, sb = c_ref[...], sa_ref[...], sb_ref[...]
        lane = lax.broadcasted_iota(jnp.int32, (t, LANES), 1)
        dkr = jnp.zeros((t, LANES), F32)
        for hd in range(MLA_HEADS):
            sl = slice(hd * LANES, (hd + 1) * LANES)
            oq_ref[:, sl] = _mx(_rope_t(dq_ref[:, sl], c, sa, sb))
            dkh = dk_ref[:, sl]
            okv_ref[:, sl] = _mx(dkh)
            dkr = dkr + dkh
        okv_ref[:, 1024:] = _mx(dv_ref[...])
        dkr = jnp.where((lane >= 64) & (lane < 96), dkr, 0.0)
        okr_ref[...] = _rope_t(dkr, c, sa, sb)

    tab = pl.BlockSpec((t, LANES), lambda i: (i, 0))
    wide = pl.BlockSpec((t, 1024), lambda i: (i, 0))
    return pl.pallas_call(
        body, name="mla_bwd_rope", grid=(SEQ // t,),
        in_specs=[wide, wide, pl.BlockSpec((t, 512), lambda i: (i, 0)), tab, tab, tab],
        out_specs=[wide, pl.BlockSpec((t, 1536), lambda i: (i, 0)), tab],
        out_shape=[jax.ShapeDtypeStruct((SEQ, 1024), MXU_DTYPE), jax.ShapeDtypeStruct((SEQ, 1536), MXU_DTYPE),
                   jax.ShapeDtypeStruct((SEQ, LANES), F32)],
        compiler_params=_params(("parallel",)),
    )(dq, dk, dv, tc, tsa, tsb)


def _rms_bwd(v, g, dy, eps=1e-6):
    rs = lax.rsqrt(jnp.mean(v * v, axis=-1, keepdims=True) + eps)
    xh = v * rs
    dxh = dy * g
    dv = rs * (dxh - xh * jnp.mean(dxh * xh, axis=-1, keepdims=True))
    return dv, jnp.sum(dy * xh, axis=0, keepdims=True)


def _mla_norm_bwd(proj0, dqn, dkn, dkr, q_norm, kv_norm):
    t = MLA_T

    def body(cq_ref, ck_ref, dqn_ref, dkn_ref, dkr_ref, qn_ref, kn_ref, o_ref, dgq_ref, dgk_ref):
        @pl.when(pl.program_id(0) == 0)
        def _():
            dgq_ref[...] = jnp.zeros_like(dgq_ref)
            dgk_ref[...] = jnp.zeros_like(dgk_ref)

        dcq, dgq = _rms_bwd(cq_ref[...], qn_ref[...], dqn_ref[...])
        dck, dgk = _rms_bwd(ck_ref[:, :LANES], kn_ref[...], dkn_ref[...])
        o_ref[:, :256] = _mx(dcq)
        o_ref[:, 256:384] = _mx(dck)
        o_ref[:, 384:] = _mx(dkr_ref[...])
        dgq_ref[0:1, :] += dgq
        dgk_ref[0:1, :] += dgk

    tab = pl.BlockSpec((t, LANES), lambda i: (i, 0))
    return pl.pallas_call(
        body, name="mla_norm_bwd", grid=(SEQ // t,),
        in_specs=[pl.BlockSpec((t, 256), lambda i: (i, 6)), pl.BlockSpec((t, 256), lambda i: (i, 7)),
                  pl.BlockSpec((t, 256), lambda i: (i, 0)), tab, tab,
                  pl.BlockSpec((1, 256), lambda i: (0, 0)), pl.BlockSpec((1, LANES), lambda i: (0, 0))],
        out_specs=[pl.BlockSpec((t, 512), lambda i: (i, 0)), pl.BlockSpec((SUBLANES, 256), lambda i: (0, 0)),
                   pl.BlockSpec((SUBLANES, LANES), lambda i: (0, 0))],
        out_shape=[jax.ShapeDtypeStruct((SEQ, 512), MXU_DTYPE), jax.ShapeDtypeStruct((SUBLANES, 256), F32),
                   jax.ShapeDtypeStruct((SUBLANES, LANES), F32)],
        compiler_params=_params(("arbitrary",)),
    )(proj0, proj0, dqn, dkn, dkr, q_norm, kv_norm)


LN_T = 512


def _ln(v, g, b, eps=1e-5):
    mu = jnp.mean(v, axis=-1, keepdims=True)
    xc = v - mu
    rs = lax.rsqrt(jnp.mean(xc * xc, axis=-1, keepdims=True) + eps)
    return xc * rs * g + b


def _ln_bwd(v, g, dy, eps=1e-5):
    mu = jnp.mean(v, axis=-1, keepdims=True)
    xc = v - mu
    rs = lax.rsqrt(jnp.mean(xc * xc, axis=-1, keepdims=True) + eps)
    xh = xc * rs
    dxh = dy * g
    dv = rs * (dxh - jnp.mean(dxh, axis=-1, keepdims=True) - xh * jnp.mean(dxh * xh, axis=-1, keepdims=True))
    return dv, jnp.sum(dy * xh, axis=0, keepdims=True), jnp.sum(dy, axis=0, keepdims=True)


def _l0_out(h, o, proj0, x, w_out, g, b):
    t = LN_T

    def body(h_ref, o_ref, ga_ref, gb_ref, x_ref, w_ref, g_ref, b_ref, y_ref, v_ref, x1_ref, x1b_ref):
        y = _mx(jnp.concatenate([h_ref[...] * _silu(ga_ref[...]), o_ref[...] * _silu(gb_ref[...])], axis=1))
        v = DN_ALPHA * x_ref[...] + _dot(y, w_ref[...])
        y_ref[...] = y
        v_ref[...] = v
        x1 = _ln(v, g_ref[...], b_ref[...])
        x1_ref[...] = x1
        x1b_ref[...] = _mx(x1)

    half = pl.BlockSpec((t, 512), lambda i: (i, 0))
    full = pl.BlockSpec((t, D_MODEL), lambda i: (i, 0))
    vec = pl.BlockSpec((1, D_MODEL), lambda i: (0, 0))
    return pl.pallas_call(
        body, name="l0_out", grid=(SEQ // t,),
        in_specs=[half, half, pl.BlockSpec((t, 512), lambda i: (i, 1)), pl.BlockSpec((t, 512), lambda i: (i, 2)), full,
                  pl.BlockSpec((D_MODEL, D_MODEL), lambda i: (0, 0)), vec, vec],
        out_specs=[full, full, full, full],
        out_shape=[jax.ShapeDtypeStruct((SEQ, D_MODEL), MXU_DTYPE), jax.ShapeDtypeStruct((SEQ, D_MODEL), F32),
                   jax.ShapeDtypeStruct((SEQ, D_MODEL), F32), jax.ShapeDtypeStruct((SEQ, D_MODEL), MXU_DTYPE)],
        compiler_params=_params(("parallel",)),
    )(h, o, proj0, proj0, x, w_out, g, b)


def _ln_bwd_call(v, dy, g):
    t = LN_T

    def body(v_ref, dy_ref, g_ref, dv_ref, dgb_ref):
        @pl.when(pl.program_id(0) == 0)
        def _():
            dgb_ref[...] = jnp.zeros_like(dgb_ref)

        dv, dg, db = _ln_bwd(v_ref[...], g_ref[...], dy_ref[...])
        dv_ref[...] = dv
        dgb_ref[0:1, :] += dg
        dgb_ref[1:2, :] += db

    full = pl.BlockSpec((t, D_MODEL), lambda i: (i, 0))
    return pl.pallas_call(
        body, name="ln_bwd", grid=(SEQ // t,),
        in_specs=[full, full, pl.BlockSpec((1, D_MODEL), lambda i: (0, 0))],
        out_specs=[full, pl.BlockSpec((SUBLANES, D_MODEL), lambda i: (0, 0))],
        out_shape=[jax.ShapeDtypeStruct((SEQ, D_MODEL), F32), jax.ShapeDtypeStruct((SUBLANES, D_MODEL), F32)],
        compiler_params=_params(("arbitrary",)),
    )(v, dy, g)


def _gate_bwd(dy, h, o, proj0):
    t = LN_T

    def body(dya_ref, dyb_ref, h_ref, o_ref, ga_ref, gb_ref, dh_ref, do_ref, dg_ref):
        ga, gb, dya, dyb = ga_ref[...], gb_ref[...], dya_ref[...], dyb_ref[...]
        dh_ref[...] = dya * _silu(ga)
        do_ref[...] = dyb * _silu(gb)
        dg_ref[:, :512] = _mx(dya * h_ref[...] * _dsilu(ga))
        dg_ref[:, 512:] = _mx(dyb * o_ref[...] * _dsilu(gb))

    half = pl.BlockSpec((t, 512), lambda i: (i, 0))
    half1 = pl.BlockSpec((t, 512), lambda i: (i, 1))
    full = pl.BlockSpec((t, 1024), lambda i: (i, 0))
    return pl.pallas_call(
        body, name="gate_bwd", grid=(SEQ // t,),
        in_specs=[half, half1, half, half, half1, pl.BlockSpec((t, 512), lambda i: (i, 2))],
        out_specs=[half, half, full],
        out_shape=[jax.ShapeDtypeStruct((SEQ, 512), F32), jax.ShapeDtypeStruct((SEQ, 512), F32),
                   jax.ShapeDtypeStruct((SEQ, 1024), MXU_DTYPE)],
        compiler_params=_params(("parallel",)),
    )(dy, dy, h, o, proj0, proj0)


CONV_T = 512
CONV_CB = 1024


def _ssd_conv_fwd(xbc, cw8, cb):
    t, cbk = CONV_T, CONV_CB
    tb = t // SUBLANES

    def body(x_ref, halo_ref, cw_ref, cb_ref, pre_ref, act_ref):
        halo = jnp.where(pl.program_id(1) > 0, halo_ref[...], 0.0)
        pre = _conv4(x_ref[...], halo, cw_ref[...], cb_ref[...])
        pre_ref[...] = pre
        act_ref[...] = _silu(pre)

    blk = pl.BlockSpec((t, cbk), lambda j, i: (i, j))
    return pl.pallas_call(
        body, name="ssd_conv_fwd", grid=(SSD_CONV // cbk, SEQ // t),
        in_specs=[blk, pl.BlockSpec((SUBLANES, cbk), lambda j, i: (jnp.maximum(i * tb - 1, 0), j)),
                  pl.BlockSpec((SUBLANES, cbk), lambda j, i: (0, j)), pl.BlockSpec((1, cbk), lambda j, i: (0, j))],
        out_specs=[blk, blk],
        out_shape=[jax.ShapeDtypeStruct((SEQ, SSD_CONV), F32), jax.ShapeDtypeStruct((SEQ, SSD_CONV), F32)],
        compiler_params=_params(("parallel", "parallel")),
    )(xbc, xbc, cw8, cb)


def _ssd_conv_bwd(dact, pre, xbc, cw8):
    t, cbk = CONV_T, CONV_CB
    tb = t // SUBLANES
    nb = SEQ // t

    def body(da_ref, dan_ref, pre_ref, pren_ref, x_ref, xh_ref, cw_ref, dx_ref, dcw_ref):
        i = pl.program_id(1)

        @pl.when(i == 0)
        def _():
            dcw_ref[...] = jnp.zeros_like(dcw_ref)

        dpre = da_ref[...] * _dsilu(pre_ref[...])
        dpre_next = jnp.where(i < nb - 1, dan_ref[...] * _dsilu(pren_ref[...]), 0.0)
        xblk = x_ref[...]
        xhalo = jnp.where(i > 0, xh_ref[...], 0.0)
        cw = cw_ref[...]
        dx = dpre * cw[3:4]
        for k in range(4):
            dcw_ref[k:k + 1, :] += jnp.sum(dpre * _shift_down(xblk, xhalo, 3 - k), axis=0, keepdims=True)
            if k < 3:
                dx = dx + _shift_up(dpre, dpre_next, 3 - k) * cw[k:k + 1]
        dcw_ref[4:5, :] += jnp.sum(dpre, axis=0, keepdims=True)
        dx_ref[...] = _mx(dx)

    blk = pl.BlockSpec((t, cbk), lambda j, i: (i, j))
    nxt = pl.BlockSpec((SUBLANES, cbk), lambda j, i: (jnp.minimum((i + 1) * tb, SEQ // SUBLANES - 1), j))
    prv = pl.BlockSpec((SUBLANES, cbk), lambda j, i: (jnp.maximum(i * tb - 1, 0), j))
    acc = pl.BlockSpec((SUBLANES, cbk), lambda j, i: (0, j))
    return pl.pallas_call(
        body, name="ssd_conv_bwd", grid=(SSD_CONV // cbk, nb),
        in_specs=[blk, nxt, blk, nxt, blk, prv, acc],
        out_specs=[blk, acc],
        out_shape=[jax.ShapeDtypeStruct((SEQ, SSD_CONV), MXU_DTYPE), jax.ShapeDtypeStruct((SUBLANES, SSD_CONV), F32)],
        compiler_params=_params(("parallel", "arbitrary")),
    )(dact, dact, pre, pre, xbc, xbc, cw8)


def _ssd_common(dt_raw, bias, alog, tril, xs):
    lane = lax.broadcasted_iota(jnp.int32, dt_raw.shape, 1)
    dt = jnp.where(lane < SSD_HEADS, _softplus(dt_raw + bias), 0.0)
    a_neg = -jnp.exp(alog)
    cs = _dot_hi(tril, dt * a_neg)
    dt_x = _expand_heads(dt)
    ecs_x = _expand_heads(jnp.exp(cs))
    ds_x = _expand_heads(jnp.exp(cs[SSD_L - 1:SSD_L, :] - cs))
    return dt, a_neg, cs, dt_x, None, xs * dt_x, ds_x, ecs_x, ecs_x[SSD_L - 1:SSD_L, :]


def _expand_heads(v):
    lane = lax.broadcasted_iota(jnp.int32, v.shape, 1)
    tiles = [jnp.where(lane < SSD_P, v[:, 2 * pr:2 * pr + 1], v[:, 2 * pr + 1:2 * pr + 2])
             for pr in range(SSD_HEADS // 2)]
    return jnp.concatenate(tiles, axis=1)


def _fold_heads(v, expand_t):
    hi = v.astype(jnp.bfloat16)
    lo = (v - hi.astype(F32)).astype(jnp.bfloat16)
    return _dot(hi, expand_t) + _dot(lo, expand_t)


def _ssd_decay(cs, cs_t, hh, causal):
    seg = cs[:, hh:hh + 1] - cs_t[hh:hh + 1, :]
    return jnp.where(causal, jnp.exp(jnp.where(causal, seg, 0.0)), 0.0)


def _ssd_scan_fwd(act, dt_raw, bias, alog, d_x, tril):
    nc = SEQ // SSD_L
    gw = SSD_INNER // SSD_GROUPS

    def body(act_ref, dt_ref, bias_ref, alog_ref, dx_ref, tril_ref, y_ref, hp_ref, h_sc):
        @pl.when(pl.program_id(0) == 0)
        def _():
            h_sc[...] = jnp.zeros_like(h_sc)

        xs = act_ref[:, :SSD_INNER]
        _, _, cs, _, _, xdt, ds_x, ecs_x, elast = _ssd_common(
            dt_ref[...], bias_ref[...], alog_ref[...], tril_ref[...], xs)
        cs_t = cs.T
        causal = (lax.broadcasted_iota(jnp.int32, (SSD_L, SSD_L), 0)
                  >= lax.broadcasted_iota(jnp.int32, (SSD_L, SSD_L), 1))
        lane = lax.broadcasted_iota(jnp.int32, (SSD_L, LANES), 1)
        xdt_b = _mx(xdt)
        xds_b = _mx(xdt * ds_x)
        hp_ref[0] = h_sc[...]
        for g in range(SSD_GROUPS):
            gs = slice(g * gw, (g + 1) * gw)
            bg = _mx(act_ref[:, SSD_INNER + g * SSD_N:SSD_INNER + (g + 1) * SSD_N])
            cg = _mx(act_ref[:, SSD_INNER + 512 + g * SSD_N:SSD_INNER + 512 + (g + 1) * SSD_N])
            cb = _dot_nt(cg, bg)
            hprev = h_sc[:, gs]
            yoff = _dot(cg, _mx(hprev)) * ecs_x[:, gs]
            h_sc[:, gs] = hprev * elast[:, gs] + _dot_tn(bg, xds_b[:, gs])
            for pr in range(4):
                ps = slice(g * gw + pr * LANES, g * gw + (pr + 1) * LANES)
                xp = xdt_b[:, ps]
                ydiag = jnp.zeros((SSD_L, LANES), F32)
                for j in range(2):
                    dm = _ssd_decay(cs, cs_t, g * 8 + pr * 2 + j, causal)
                    mine = (lane >= j * 64) & (lane < (j + 1) * 64)
                    ydiag = ydiag + _dot(_mx(cb * dm), jnp.where(mine, xp, jnp.zeros_like(xp)))
                y_ref[:, ps] = ydiag + yoff[:, pr * LANES:(pr + 1) * LANES] + dx_ref[:, ps] * xs[:, ps]

    const = lambda shape: pl.BlockSpec(shape, lambda c: (0, 0))
    return pl.pallas_call(
        body, name="ssd_scan_fwd", grid=(nc,),
        in_specs=[pl.BlockSpec((SSD_L, SSD_CONV), lambda c: (c, 0)), pl.BlockSpec((SSD_L, LANES), lambda c: (c, 0)),
                  const((1, LANES)), const((1, LANES)), const((1, SSD_INNER)), const((SSD_L, SSD_L))],
        out_specs=[pl.BlockSpec((SSD_L, SSD_INNER), lambda c: (c, 0)),
                   pl.BlockSpec((1, SSD_N, SSD_INNER), lambda c: (c, 0, 0))],
        out_shape=[jax.ShapeDtypeStruct((SEQ, SSD_INNER), F32), jax.ShapeDtypeStruct((nc, SSD_N, SSD_INNER), F32)],
        scratch_shapes=[pltpu.VMEM((SSD_N, SSD_INNER), F32)],
        compiler_params=_params(("arbitrary",)),
    )(act, dt_raw, bias, alog, d_x, tril)


def _ssd_scan_bwd(dy, act, dt_raw, hprev_all, bias, alog, d_x, tril, expand_t):
    nc = SEQ // SSD_L
    gw = SSD_INNER // SSD_GROUPS

    def body(dy_ref, act_ref, dt_ref, hp_ref, bias_ref, alog_ref, dx_ref, tril_ref, et_ref,
             dact_ref, ddt_ref, dvec_ref, dh_sc, dd_sc):
        i = pl.program_id(0)

        @pl.when(i == 0)
        def _():
            dh_sc[...] = jnp.zeros_like(dh_sc)
            dd_sc[...] = jnp.zeros_like(dd_sc)
            dvec_ref[...] = jnp.zeros_like(dvec_ref)

        xs = act_ref[:, :SSD_INNER]
        dt_raw_v, bias_v = dt_ref[...], bias_ref[...]
        dt, a_neg, cs, dt_x, _, xdt, ds_x, ecs_x, elast = _ssd_common(
            dt_raw_v, bias_v, alog_ref[...], tril_ref[...], xs)
        cs_t = cs.T
        rowi = lax.broadcasted_iota(jnp.int32, (SSD_L, SSD_L), 0)
        coli = lax.broadcasted_iota(jnp.int32, (SSD_L, SSD_L), 1)
        causal = rowi >= coli
        lane = lax.broadcasted_iota(jnp.int32, (SSD_L, LANES), 1)
        row_g = lax.broadcasted_iota(jnp.int32, (SSD_L, gw), 0)
        dyv = dy_ref[...]
        dd_sc[0:1, :] += jnp.sum(dyv * xs, axis=0, keepdims=True)
        xdt_b = _mx(xdt)
        xds = xdt * ds_x
        xds_b = _mx(xds)
        dy_b = _mx(dyv)
        dye_b = _mx(dyv * ecs_x)
        dcs = jnp.zeros((SSD_L, LANES), F32)
        dcs_t = jnp.zeros((LANES, SSD_L), F32)
        dcs_parts = []
        dxdt_parts = []
        for g in range(SSD_GROUPS):
            gs = slice(g * gw, (g + 1) * gw)
            bcol = slice(SSD_INNER + g * SSD_N, SSD_INNER + (g + 1) * SSD_N)
            ccol = slice(SSD_INNER + 512 + g * SSD_N, SSD_INNER + 512 + (g + 1) * SSD_N)
            bg, cg = _mx(act_ref[:, bcol]), _mx(act_ref[:, ccol])
            cb = _dot_nt(cg, bg)
            hp = hp_ref[0, :, gs]
            hp_b = _mx(hp)
            dh = dh_sc[:, gs]
            dh_b = _mx(dh)
            yoff = _dot(cg, hp_b) * ecs_x[:, gs]
            bdh = _dot(bg, dh_b)
            tt = xds[:, gs] * bdh
            last_row = (jnp.sum(tt, axis=0, keepdims=True)
                        + jnp.sum(dh * hp, axis=0, keepdims=True) * elast[:, gs])
            dcs_parts.append(dyv[:, gs] * yoff - tt + jnp.where(row_g == SSD_L - 1, last_row, 0.0))
            dc_g = _dot_nt(dye_b[:, gs], hp_b)
            db_g = _dot_nt(xds_b[:, gs], dh_b)
            dh_sc[:, gs] = _dot_tn(cg, dye_b[:, gs]) + dh * elast[:, gs]
            wsum = jnp.zeros((SSD_L, SSD_L), F32)
            dxdt_g = []
            for pr in range(4):
                ps = slice(g * gw + pr * LANES, g * gw + (pr + 1) * LANES)
                xp, dyp = xdt_b[:, ps], dy_b[:, ps]
                dxp = jnp.zeros((SSD_L, LANES), F32)
                for j in range(2):
                    hh = g * 8 + pr * 2 + j
                    dm = _ssd_decay(cs, cs_t, hh, causal)
                    mine = (lane >= j * 64) & (lane < (j + 1) * 64)
                    dy_h = jnp.where(mine, dyp, jnp.zeros_like(dyp))
                    wd = _dot_nt(dy_h, xp) * dm
                    wsum = wsum + wd
                    gmat = wd * cb
                    dcs = dcs + jnp.where(lane == hh, jnp.sum(gmat, axis=1, keepdims=True), 0.0)
                    dcs_t = dcs_t - jnp.where(rowi == hh, jnp.sum(gmat, axis=0, keepdims=True), 0.0)
                    dxp = dxp + _dot_tn(_mx(cb * dm), dy_h)
                dxdt_g.append(dxp)
            dxdt_parts.append(jnp.concatenate(dxdt_g, axis=1) + bdh * ds_x[:, gs])
            ws_b = _mx(wsum)
            dact_ref[:, ccol] = dc_g + _dot(ws_b, bg)
            dact_ref[:, bcol] = db_g + _dot_tn(ws_b, cg)
        dxdt = jnp.concatenate(dxdt_parts, axis=1)
        dcs_x = jnp.concatenate(dcs_parts, axis=1)
        et = et_ref[...]
        dcs_tot = dcs + dcs_t.T + _fold_heads(dcs_x, et)
        da_dt = _dot_hi((coli >= rowi).astype(F32), dcs_tot)
        ddt = da_dt * a_neg + _fold_heads(dxdt * xs, et)
        ddt_raw = ddt * _sigmoid(dt_raw_v + bias_v)
        ddt_ref[...] = ddt_raw
        dvec_ref[0:1, :] += jnp.sum(ddt_raw, axis=0, keepdims=True)
        dvec_ref[1:2, :] += jnp.sum(da_dt * dt, axis=0, keepdims=True) * a_neg
        dact_ref[:, :SSD_INNER] = dyv * dx_ref[...] + dxdt * dt_x

        @pl.when(i == nc - 1)
        def _():
            dvec_ref[2:3, :] = _fold_heads(dd_sc[...], et)[0:1, :]

    const = lambda shape: pl.BlockSpec(shape, lambda c: (0, 0))
    rev = lambda c: (nc - 1 - c, 0)
    return pl.pallas_call(
        body, name="ssd_scan_bwd", grid=(nc,),
        in_specs=[pl.BlockSpec((SSD_L, SSD_INNER), rev), pl.BlockSpec((SSD_L, SSD_CONV), rev),
                  pl.BlockSpec((SSD_L, LANES), rev),
                  pl.BlockSpec((1, SSD_N, SSD_INNER), lambda c: (nc - 1 - c, 0, 0)),
                  const((1, LANES)), const((1, LANES)), const((1, SSD_INNER)), const((SSD_L, SSD_L)),
                  const((SSD_INNER, LANES))],
        out_specs=[pl.BlockSpec((SSD_L, SSD_CONV), rev), pl.BlockSpec((SSD_L, LANES), rev), const((SUBLANES, LANES))],
        out_shape=[jax.ShapeDtypeStruct((SEQ, SSD_CONV), F32), jax.ShapeDtypeStruct((SEQ, LANES), F32),
                   jax.ShapeDtypeStruct((SUBLANES, LANES), F32)],
        scratch_shapes=[pltpu.VMEM((SSD_N, SSD_INNER), F32), pltpu.VMEM((SUBLANES, SSD_INNER), F32)],
        compiler_params=_params(("arbitrary",)),
    )(dy, act, dt_raw, hprev_all, bias, alog, d_x, tril, expand_t)


L1_T = 256


def _gated_norm(y, z, nw):
    y2 = y * _silu(z)
    gw = SSD_INNER // SSD_GROUPS
    outs, xhs, rss = [], [], []
    for g in range(SSD_GROUPS):
        gs = slice(g * gw, (g + 1) * gw)
        v = y2[:, gs]
        rs = lax.rsqrt(jnp.mean(v * v, axis=-1, keepdims=True) + 1e-6)
        xhs.append(v * rs)
        rss.append(rs)
        outs.append(v * rs * nw[:, gs])
    return outs, xhs, rss


def _l1_out(y, z, nw, w_out, x1, g, b, target):
    t = L1_T

    def body(y_ref, z_ref, nw_ref, w_ref, x1_ref, g_ref, b_ref, tg_ref, yn_ref, dv_ref, dgb_ref, loss_ref):
        @pl.when(pl.program_id(0) == 0)
        def _():
            dgb_ref[...] = jnp.zeros_like(dgb_ref)
            loss_ref[...] = jnp.zeros_like(loss_ref)

        outs, _, _ = _gated_norm(y_ref[...], z_ref[...], nw_ref[...])
        yn = _mx(jnp.concatenate(outs, axis=1))
        yn_ref[...] = yn
        v = DN_ALPHA * x1_ref[...] + _dot(yn, w_ref[...])
        gv = g_ref[...]
        err = _ln(v, gv, b_ref[...]) - tg_ref[...]
        rowsum = jnp.sum(err * err, axis=1, keepdims=True)
        loss_ref[...] += 0.5 * jnp.sum(rowsum, axis=0, keepdims=True) / D_MODEL
        dv, dg, db = _ln_bwd(v, gv, err / D_MODEL)
        dv_ref[...] = dv
        dgb_ref[0:1, :] += dg
        dgb_ref[1:2, :] += db

    wide = pl.BlockSpec((t, SSD_INNER), lambda i: (i, 0))
    full = pl.BlockSpec((t, D_MODEL), lambda i: (i, 0))
    vec = pl.BlockSpec((1, D_MODEL), lambda i: (0, 0))
    return pl.pallas_call(
        body, name="l1_out", grid=(SEQ // t,),
        in_specs=[wide, wide, pl.BlockSpec((1, SSD_INNER), lambda i: (0, 0)),
                  pl.BlockSpec((SSD_INNER, D_MODEL), lambda i: (0, 0)), full, vec, vec, full],
        out_specs=[wide, full, pl.BlockSpec((SUBLANES, D_MODEL), lambda i: (0, 0)),
                   pl.BlockSpec((SUBLANES, LANES), lambda i: (0, 0))],
        out_shape=[jax.ShapeDtypeStruct((SEQ, SSD_INNER), MXU_DTYPE), jax.ShapeDtypeStruct((SEQ, D_MODEL), F32),
                   jax.ShapeDtypeStruct((SUBLANES, D_MODEL), F32), jax.ShapeDtypeStruct((SUBLANES, LANES), F32)],
        compiler_params=_params(("arbitrary",)),
    )(y, z, nw, w_out, x1, g, b, target)


def _l1_gate_bwd(dyn, y, z, nw):
    t = L1_T
    gw = SSD_INNER // SSD_GROUPS

    def body(dyn_ref, y_ref, z_ref, nw_ref, dy_ref, dz_ref, dnw_ref):
        @pl.when(pl.program_id(0) == 0)
        def _():
            dnw_ref[...] = jnp.zeros_like(dnw_ref)

        yv, zv, nwv = y_ref[...], z_ref[...], nw_ref[...]
        _, xhs, rss = _gated_norm(yv, zv, nwv)
        sz, dsz = _silu(zv), _dsilu(zv)
        for g in range(SSD_GROUPS):
            gs = slice(g * gw, (g + 1) * gw)
            d_out = dyn_ref[:, gs]
            xh = xhs[g]
            dnw_ref[0:1, gs] += jnp.sum(d_out * xh, axis=0, keepdims=True)
            dxh = d_out * nwv[:, gs]
            dy2 = rss[g] * (dxh - xh * jnp.mean(dxh * xh, axis=-1, keepdims=True))
            dy_ref[:, gs] = dy2 * sz[:, gs]
            dz_ref[:, gs] = _mx(dy2 * yv[:, gs] * dsz[:, gs])

    wide = pl.BlockSpec((t, SSD_INNER), lambda i: (i, 0))
    return pl.pallas_call(
        body, name="l1_gate_bwd", grid=(SEQ // t,),
        in_specs=[wide, wide, wide, pl.BlockSpec((1, SSD_INNER), lambda i: (0, 0))],
        out_specs=[wide, wide, pl.BlockSpec((SUBLANES, SSD_INNER), lambda i: (0, 0))],
        out_shape=[jax.ShapeDtypeStruct((SEQ, SSD_INNER), F32), jax.ShapeDtypeStruct((SEQ, SSD_INNER), MXU_DTYPE),
                   jax.ShapeDtypeStruct((SUBLANES, SSD_INNER), F32)],
        compiler_params=_params(("arbitrary",)),
    )(dyn, y, z, nw)


MESH = pl.DeviceIdType.MESH
ANY = pl.BlockSpec(memory_space=pl.ANY)


def _flip(v, bit):
    return 1 - v if bit else v


def _all_gather(blocks, name):
    n = len(blocks)

    def body(*refs):
        x_refs, out_refs = refs[:n], refs[n:2 * n]
        send_sems, recv_sems, local_sems = refs[2 * n:]
        mx, my, mc = lax.axis_index("x"), lax.axis_index("y"), lax.axis_index("c")
        me, sibling = (mx, my, mc), (mx, my, 1 - mc)
        chips = [(1 - mx, my), (mx, 1 - my), (1 - mx, 1 - my)]

        def copy(a, k, block, to, own=False):
            px, py, pc = block
            slot = out_refs[a].at[4 * px + 2 * py + pc]
            return pltpu.make_async_remote_copy(
                src_ref=x_refs[a] if own else slot, dst_ref=slot,
                send_sem=send_sems.at[7 * a + k], recv_sem=recv_sems.at[7 * a + k], device_id=to, device_id_type=MESH)

        mine = [pltpu.make_async_copy(x_refs[a], out_refs[a].at[4 * mx + 2 * my + mc], local_sems.at[a])
                for a in range(n)]
        first = []
        for a in range(n):
            mine[a].start()
            first.append(copy(a, 0, me, sibling, own=True))
            first += [copy(a, 1 + j, me, (*chip, mc), own=True) for j, chip in enumerate(chips)]
        for cp in first:
            cp.start()
        passed = []
        for j, chip in enumerate(chips):
            for a in range(n):
                copy(a, 1 + j, (*chip, mc), me).wait_recv()
                fwd = copy(a, 4 + j, (*chip, mc), sibling)
                fwd.start()
                passed.append(fwd)
        for a in range(n):
            copy(a, 0, sibling, me).wait_recv()
            for j, chip in enumerate(chips):
                copy(a, 4 + j, (*chip, 1 - mc), me).wait_recv()
        for cp in first + passed:
            cp.wait_send()
        for cp in mine:
            cp.wait()

    return pl.pallas_call(
        body, name=name, in_specs=[ANY] * n, out_specs=[ANY] * n,
        out_shape=[jax.ShapeDtypeStruct((N_DEV,) + b.shape, b.dtype) for b in blocks],
        scratch_shapes=[pltpu.SemaphoreType.DMA((7 * n,)), pltpu.SemaphoreType.DMA((7 * n,)),
                        pltpu.SemaphoreType.DMA((n,))],
    )(*blocks)


def _exchange(scatter, bcast, name):
    ns, n = len(scatter), len(scatter) + len(bcast)

    def body(*refs):
        in_refs, out_refs = refs[:n], refs[n:2 * n]
        send_sems, recv_sems, local_sems = refs[2 * n:]
        mx, my, mc = lax.axis_index("x"), lax.axis_index("y"), lax.axis_index("c")
        me = 4 * mx + 2 * my + mc

        def src(a, slot):
            return in_refs[a].at[slot] if a < ns else in_refs[a]

        local = [pltpu.make_async_copy(src(a, me), out_refs[a].at[me], local_sems.at[a]) for a in range(n)]
        for cp in local:
            cp.start()
        copies = []
        for k in range(1, N_DEV):
            px, py, pc = _flip(mx, (k >> 2) & 1), _flip(my, (k >> 1) & 1), _flip(mc, k & 1)
            for a in range(n):
                cp = pltpu.make_async_remote_copy(
                    src_ref=src(a, 4 * px + 2 * py + pc), dst_ref=out_refs[a].at[me],
                    send_sem=send_sems.at[7 * a + k - 1], recv_sem=recv_sems.at[7 * a + k - 1],
                    device_id=(px, py, pc), device_id_type=MESH)
                cp.start()
                copies.append(cp)
        for cp in copies:
            cp.wait()
        for cp in local:
            cp.wait()

    return pl.pallas_call(
        body, name=name, in_specs=[ANY] * n, out_specs=[ANY] * n,
        out_shape=[jax.ShapeDtypeStruct(a.shape, a.dtype) for a in scatter]
        + [jax.ShapeDtypeStruct((N_DEV,) + a.shape, a.dtype) for a in bcast],
        scratch_shapes=[pltpu.SemaphoreType.DMA((7 * n,)), pltpu.SemaphoreType.DMA((7 * n,)),
                        pltpu.SemaphoreType.DMA((n,))],
    )(*scatter, *bcast)


def _segments(col_map, width):
    segs = []
    for lo, hi, arr, alo in col_map:
        for s in range(N_DEV):
            a, b = max(lo, s * width), min(hi, (s + 1) * width)
            if a < b:
                segs.append((s, a - s * width, b - a, arr, alo + a - lo))
    return segs


COPY_ROWS = 256


def _unshard(g8, col_map, widths, name):
    _, r, w = g8.shape
    rb = min(r, COPY_ROWS)
    segs = _segments(col_map, w)

    def body(g_ref, *o_refs):
        for o_ref in o_refs:
            o_ref[...] = jnp.zeros_like(o_ref)
        for s, llo, n, arr, alo in segs:
            o_refs[arr][:, alo:alo + n] = g_ref[s, :, llo:llo + n]

    return pl.pallas_call(
        body, name=name, grid=(r // rb,),
        in_specs=[pl.BlockSpec((N_DEV, rb, w), lambda i: (0, i, 0))],
        out_specs=[pl.BlockSpec((rb, n), lambda i: (i, 0)) for n in widths],
        out_shape=[jax.ShapeDtypeStruct((r, n), g8.dtype) for n in widths],
        compiler_params=_params(("parallel",)),
    )(g8)


def _reshard(srcs, col_map, w, dtype, name):
    r = srcs[0].shape[0]
    rb = min(r, COPY_ROWS)
    segs = _segments(col_map, w)

    def body(*refs):
        o_ref = refs[-1]
        for s, llo, n, arr, alo in segs:
            o_ref[s, :, llo:llo + n] = refs[arr][:, alo:alo + n].astype(dtype)

    return pl.pallas_call(
        body, name=name, grid=(r // rb,),
        in_specs=[pl.BlockSpec((rb, a.shape[1]), lambda i: (i, 0)) for a in srcs],
        out_specs=pl.BlockSpec((N_DEV, rb, w), lambda i: (0, i, 0)),
        out_shape=jax.ShapeDtypeStruct((N_DEV, r, w), dtype),
        compiler_params=_params(("parallel",)),
    )(*srcs)


def _adamw(parts, w, m, v, name):
    r, c = w.shape
    tr = COPY_ROWS if r % COPY_ROWS == 0 else r

    def body(p_ref, w_ref, m_ref, v_ref, g_ref, d_ref, mo_ref, vo_ref):
        g = p_ref[0].astype(F32)
        for s in range(1, N_DEV):
            g = g + p_ref[s].astype(F32)
        mn = ADAM_B1 * m_ref[...] + (1.0 - ADAM_B1) * g
        vn = ADAM_B2 * v_ref[...] + (1.0 - ADAM_B2) * (g * g)
        m_hat = mn / (1.0 - ADAM_B1 ** ADAM_STEP)
        v_hat = vn / (1.0 - ADAM_B2 ** ADAM_STEP)
        g_ref[...] = g
        d_ref[...] = -ADAM_LR * (m_hat / (jnp.sqrt(v_hat) + ADAM_EPS) + ADAM_WD * w_ref[...])
        mo_ref[...] = mn
        vo_ref[...] = vn

    blk = pl.BlockSpec((tr, c), lambda i: (i, 0))
    out = jax.ShapeDtypeStruct((r, c), F32)
    return pl.pallas_call(
        body, name=name, grid=(r // tr,),
        in_specs=[pl.BlockSpec((N_DEV, tr, c), lambda i: (0, i, 0)), blk, blk, blk],
        out_specs=[blk, blk, blk, blk], out_shape=[out, out, out, out],
        compiler_params=_params(("parallel",)),
    )(parts, w, m, v)


BIG = (("ab_w_in", (1024, 244), 1), ("ssd_w_in", (1024, 644), 1), ("ab_w_out", (128, 1024), 0),
       ("ssd_w_out", (256, 1024), 0), ("mla_w_uq", (256, 96), 1), ("mla_w_ukv", (128, 128), 1))
SMALL = (("ab_conv_w", (4, 64), 1), ("ssd_conv_w", (4, 384), 1), ("ssd_conv_b", (384,), 0),
         ("ssd_norm", (256,), 0), ("ssd_ln_g", (128,), 0), ("ssd_ln_b", (128,), 0))
REPL = (("ab_conv_b", (512,)), ("ab_gate_a_w", (8, 64, 64)), ("ab_gate_a_b", (512,)), ("ab_gate_x_w", (8, 64, 64)),
        ("ab_gate_x_b", (512,)), ("ab_lambda", (512,)), ("mla_q_norm", (256,)), ("mla_kv_norm", (128,)),
        ("ab_ln_g", (1024,)), ("ab_ln_b", (1024,)), ("ssd_dt_bias", (32,)), ("ssd_a_log", (32,)), ("ssd_d", (32,)))
SMALL_ROWS = 24
REPL_ROWS = 552

MAP_W0 = ((0, 1920, 0, 0), (1920, 1952, 0, 1984))
MAP_W1 = ((0, 2048, 0, 0), (2048, 5120, 1, 0), (5120, 5152, 2, 0))
MAP_WQ = tuple((96 * hd, 96 * hd + 96, 0, 128 * hd) for hd in range(8))
MAP_WKV = (tuple((128 * hd, 128 * hd + 64, 0, 128 * hd) for hd in range(8))
           + tuple((128 * hd + 64, 128 * hd + 128, 0, 1024 + 64 * hd) for hd in range(8)))
MAP_G0 = ((0, 512, 0, 0), (512, 1536, 1, 0), (1536, 1920, 2, 0), (1920, 1952, 2, 448))


def _pack_rows(pieces, rows, dtype):
    flat = jnp.concatenate([p.reshape(-1).astype(dtype) for p in pieces])
    return jnp.pad(flat, (0, rows * LANES - flat.shape[0])).reshape(rows, LANES)


def _unshard_flat(flat8, off, shape, axis):
    n = math.prod(shape)
    sh = flat8[:, off:off + n].reshape((N_DEV,) + shape)
    if axis == 0:
        return sh.reshape((N_DEV * shape[0],) + shape[1:])
    return jnp.moveaxis(sh, 0, len(shape) - 1).reshape(shape[:-1] + (N_DEV * shape[-1],))


def _to_shards(full, shape, axis):
    if axis == 0:
        return full.reshape(N_DEV, -1)
    sh = full.reshape(shape[:-1] + (N_DEV, shape[-1]))
    return jnp.moveaxis(sh, len(shape) - 1, 0).reshape(N_DEV, -1)


def _pad128(v):
    flat = v.reshape(-1)
    return jnp.pad(flat, (0, (-flat.shape[0]) % LANES))


def _block_diag(w):
    eye = jnp.eye(8, dtype=w.dtype)
    return (eye[:, None, :, None] * w[:, :, None, :]).reshape(RNN_W, RNN_W)


def _block_diag_t(d):
    d4 = d.reshape(8, 64, 8, 64)
    return jnp.stack([d4[hd, :, hd, :] for hd in range(8)])


def _row(v, width=None):
    v = v.reshape(1, -1)
    return v if width is None else jnp.pad(v, ((0, 0), (0, width - v.shape[1])))


def _taps8(w):
    return jnp.pad(w, ((0, 4), (0, 0)))


def kernel(x, positions, ab_w_in, ab_conv_w, ab_conv_b, ab_gate_a_w, ab_gate_a_b, ab_gate_x_w, ab_gate_x_b, ab_lambda, mla_q_norm, mla_kv_norm, mla_w_uq, mla_w_ukv, ab_w_out, ab_ln_g, ab_ln_b, ssd_w_in, ssd_conv_w, ssd_conv_b, ssd_dt_bias, ssd_a_log, ssd_d, ssd_norm, ssd_w_out, ssd_ln_g, ssd_ln_b, loss_target, m_ab_w_in, m_ab_conv_w, m_ab_conv_b, m_ab_gate_a_w, m_ab_gate_a_b, m_ab_gate_x_w, m_ab_gate_x_b, m_ab_lambda, m_mla_q_norm, m_mla_kv_norm, m_mla_w_uq, m_mla_w_ukv, m_ab_w_out, m_ab_ln_g, m_ab_ln_b, m_ssd_w_in, m_ssd_conv_w, m_ssd_conv_b, m_ssd_dt_bias, m_ssd_a_log, m_ssd_d, m_ssd_norm, m_ssd_w_out, m_ssd_ln_g, m_ssd_ln_b, v_ab_w_in, v_ab_conv_w, v_ab_conv_b, v_ab_gate_a_w, v_ab_gate_a_b, v_ab_gate_x_w, v_ab_gate_x_b, v_ab_lambda, v_mla_q_norm, v_mla_kv_norm, v_mla_w_uq, v_mla_w_ukv, v_ab_w_out, v_ab_ln_g, v_ab_ln_b, v_ssd_w_in, v_ssd_conv_w, v_ssd_conv_b, v_ssd_dt_bias, v_ssd_a_log, v_ssd_d, v_ssd_norm, v_ssd_w_out, v_ssd_ln_g, v_ssd_ln_b):
    args = dict(locals())
    w_in = {n: args[n][0] for n, *_ in BIG + SMALL + REPL}
    m_in = {n: args["m_" + n][0] for n, *_ in BIG + SMALL + REPL}
    v_in = {n: args["v_" + n][0] for n, *_ in BIG + SMALL + REPL}

    gathered = _all_gather([w_in[n].astype(MXU_DTYPE) for n, *_ in BIG]
                           + [_pack_rows([w_in[n] for n, *_ in SMALL], SMALL_ROWS, F32)], "gather_params")
    g8 = dict(zip([n for n, *_ in BIG], gathered))
    small8 = gathered[-1].reshape(N_DEV, -1)
    wts = {"wo0": g8["ab_w_out"].reshape(1024, 1024), "wo1": g8["ssd_w_out"].reshape(2048, 1024)}
    wts["w0p"], = _unshard(g8["ab_w_in"], MAP_W0, (2048,), "unshard_w0")
    wts["w1z"], wts["w1x"], wts["w1d"] = _unshard(g8["ssd_w_in"], MAP_W1, (2048, 3072, 128), "unshard_w1")
    wts["wq"], = _unshard(g8["mla_w_uq"], MAP_WQ, (1024,), "unshard_wq")
    wts["wkv"], = _unshard(g8["mla_w_ukv"], MAP_WKV, (1536,), "unshard_wkv")
    small, off = {}, 0
    for n, shape, axis in SMALL:
        small[n] = _unshard_flat(small8, off, shape, axis)
        off += math.prod(shape)

    pc, loss_part, grad_x = _local_step(x[0], positions[0], loss_target[0], wts, small, w_in)

    bf = MXU_DTYPE
    send = [_reshard([pc["g_rnn"], pc["g_gate"], pc["g_tail"]], MAP_G0, 244, bf, "reshard_w0"),
            _reshard([pc["g_z"], pc["g_xbc"], pc["g_dt"]], MAP_W1, 644, bf, "reshard_w1"),
            pc["g_wo0"].astype(bf).reshape(N_DEV, 128, 1024), pc["g_wo1"].astype(bf).reshape(N_DEV, 256, 1024),
            _reshard([pc["g_wq"]], MAP_WQ, 96, bf, "reshard_wq"), _reshard([pc["g_wkv"]], MAP_WKV, 128, bf, "reshard_wkv")]
    small_send = jnp.concatenate([_to_shards(pc[n], shape, axis) for n, shape, axis in SMALL], axis=1)
    small_send = jnp.pad(small_send, ((0, 0), (0, SMALL_ROWS * LANES - small_send.shape[1]))).reshape(N_DEV, SMALL_ROWS, LANES)
    repl_part = _pack_rows([_pad128(pc[n]) for n, _ in REPL], REPL_ROWS, F32)
    recv = _exchange(send + [small_send], [repl_part], "exchange_grads")

    outs = {}
    kinds = ("grad", "delta", "new_m", "new_v")
    for (n, *_), parts in zip(BIG, recv):
        for kind, res in zip(kinds, _adamw(parts, w_in[n], m_in[n], v_in[n], "adamw_" + n)):
            outs[kind, n] = res[None]
    packs = [_pack_rows([src[n] for n, *_ in SMALL], SMALL_ROWS, F32) for src in (w_in, m_in, v_in)]
    res_s = [r.reshape(-1) for r in _adamw(recv[-2], *packs, "adamw_small")]
    packs = [_pack_rows([_pad128(src[n]) for n, _ in REPL], REPL_ROWS, F32) for src in (w_in, m_in, v_in)]
    res_r = [r.reshape(-1) for r in _adamw(recv[-1], *packs, "adamw_repl")]
    off = 0
    for n, shape, _ in SMALL:
        size = math.prod(shape)
        for kind, flat in zip(kinds, res_s):
            outs[kind, n] = flat[off:off + size].reshape(args[n].shape)
        off += size
    off = 0
    for n, shape in REPL:
        size = math.prod(shape)
        for kind, flat in zip(("grad", "delta", "new_m", "new_v"), res_r):
            outs[kind, n] = flat[off:off + size].reshape(args[n].shape)
        off += size + (-size) % LANES

    loss = lax.psum(loss_part, ("x", "y", "c"))
    order = ["ab_w_in", "ab_conv_w", "ab_conv_b", "ab_gate_a_w", "ab_gate_a_b", "ab_gate_x_w", "ab_gate_x_b",
             "ab_lambda", "mla_q_norm", "mla_kv_norm", "mla_w_uq", "mla_w_ukv", "ab_w_out", "ab_ln_g", "ab_ln_b",
             "ssd_w_in", "ssd_conv_w", "ssd_conv_b", "ssd_dt_bias", "ssd_a_log", "ssd_d", "ssd_norm", "ssd_w_out",
             "ssd_ln_g", "ssd_ln_b"]
    return (loss, grad_x[None], *[outs[kind, n] for kind in ("grad", "delta", "new_m", "new_v") for n in order])


def _local_step(x, pos, target, wts, full, rep):
    bf = MXU_DTYPE
    inv_freq = 10000.0 ** (-jnp.arange(0, 32, 2, dtype=F32) / 32)
    ang = pos.astype(F32)[:, None] * inv_freq
    cos, sin = jnp.cos(ang), jnp.sin(ang)
    zeros = lambda n: jnp.zeros((SEQ, n), F32)
    tc = jnp.concatenate([jnp.ones((SEQ, 64), F32), cos, cos, zeros(32)], axis=1)
    tsa = jnp.concatenate([zeros(64), -sin, zeros(48)], axis=1)
    tsb = jnp.concatenate([zeros(80), sin, zeros(32)], axis=1)

    w0p, wq, wkv, w1z, w1x, w1d = (wts[k] for k in ("w0p", "wq", "wkv", "w1z", "w1x", "w1d"))
    wo0, wo1 = wts["wo0"], wts["wo1"]
    wa, wxg = _block_diag(rep["ab_gate_a_w"]).astype(bf), _block_diag(rep["ab_gate_x_w"]).astype(bf)
    cw0, cb0 = _taps8(full["ab_conv_w"]), _row(rep["ab_conv_b"])
    ba, bx, lam = _row(rep["ab_gate_a_b"]), _row(rep["ab_gate_x_b"]), _row(rep["ab_lambda"])
    qn_w, kn_w = _row(rep["mla_q_norm"]), _row(rep["mla_kv_norm"])
    g0, b0 = _row(rep["ab_ln_g"]), _row(rep["ab_ln_b"])
    cw1, cb1 = _taps8(full["ssd_conv_w"]), _row(full["ssd_conv_b"])
    dt_bias, a_log = _row(rep["ssd_dt_bias"], LANES), _row(rep["ssd_a_log"], LANES)
    d_x = _row(jnp.repeat(rep["ssd_d"], SSD_P))
    nw, g1, b1 = _row(full["ssd_norm"]), _row(full["ssd_ln_g"]), _row(full["ssd_ln_b"])
    tril = jnp.tril(jnp.ones((SSD_L, SSD_L), F32))
    expand_t = (jnp.arange(SSD_INNER)[:, None] // SSD_P == jnp.arange(LANES)[None, :]).astype(jnp.bfloat16)

    xb = x.astype(bf)
    proj0 = _mm(xb, w0p, "nn", name="l0_in")
    xc, h = _rglru_fwd(proj0, cw0, cb0, wa, ba, wxg, bx, lam)
    qn, kn, kr = _mla_norm_fwd(proj0, qn_w, kn_w, tc, tsa, tsb)
    qraw = _mm(qn, wq, "nn", name="mla_q")
    kvraw = _mm(kn, wkv, "nn", name="mla_kv", tn=512)
    qc, kc, vc = _mla_assemble(qraw, kvraw, kr, tc, tsa, tsb)
    o, lse = _flash_fwd(qc, kc, vc)
    y0, v0, x1, x1b = _l0_out(h, o, proj0, x, wo0, g0, b0)

    z = _mm(x1b, w1z, "nn", name="l1_in_z")
    xbc = _mm(x1b, w1x, "nn", name="l1_in_xbc")
    dt_raw = _mm(x1b, w1d, "nn", name="l1_in_dt")
    pre, act = _ssd_conv_fwd(xbc, cw1, cb1)
    ys, hprev = _ssd_scan_fwd(act, dt_raw, dt_bias, a_log, d_x, tril)
    yn, dv1, dgb1, loss8 = _l1_out(ys, z, nw, wo1, x1, g1, b1, target)

    g_wo1 = _mm(yn, dv1, "tn", name="l1_dwout")
    dyn = _mm(dv1, wo1, "nt", name="l1_dyn", tn=1024)
    dys, dz, dnw = _l1_gate_bwd(dyn, ys, z, nw)
    dact, ddt_raw, dvec1 = _ssd_scan_bwd(dys, act, dt_raw, hprev, dt_bias, a_log, d_x, tril, expand_t)
    dxbc, dcw1 = _ssd_conv_bwd(dact, pre, xbc, cw1)
    g_z, g_xbc = _mm(x1b, dz, "tn", name="l1_dw_z"), _mm(x1b, dxbc, "tn", name="l1_dw_xbc")
    g_dt = _mm(x1b, ddt_raw, "tn", name="l1_dw_dt")
    dx1 = _mm(dz, w1z, "nt", name="l1_dx_z", add=dv1, add_scale=DN_ALPHA)
    dx1 = _mm(dxbc, w1x, "nt", name="l1_dx_xbc", add=dx1)
    dx1 = _mm(ddt_raw, w1d, "nt", name="l1_dx_dt", add=dx1)

    dv0, dgb0 = _ln_bwd_call(v0, dx1, g0)
    g_wo0 = _mm(y0, dv0, "tn", name="l0_dwout")
    dy0 = _mm(dv0, wo0, "nt", name="l0_dy")
    dh, do, dgate = _gate_bwd(dy0, h, o, proj0)
    dq, dk, dvv = _flash_bwd(qc, kc, vc, o, do, lse)
    dqraw, dkvraw, dkr = _mla_bwd_rope(dq, dk, dvv, tc, tsa, tsb)
    g_wq = _mm(qn, dqraw, "tn", name="mla_dwq", tm=256)
    g_wkv = _mm(kn, dkvraw, "tn", name="mla_dwkv", tm=128, tn=512)
    dqn = _mm(dqraw, wq, "nt", name="mla_dqn", tn=256)
    dkn = _mm(dkvraw, wkv, "nt", name="mla_dkn", tn=128)
    dtail, dqnw, dknw = _mla_norm_bwd(proj0, dqn, dkn, dkr, qn_w, kn_w)
    dxr, g_wa, g_wx, dvec0 = _rglru_bwd(dh, xc, h, proj0, cw0, wa, ba, wxg, bx, lam)
    g_tail = _mm(xb, dtail, "tn", name="l0_dw_tail")
    g_rnn, g_gate = _mm(xb, dxr, "tn", name="l0_dw_rnn"), _mm(xb, dgate, "tn", name="l0_dw_gate")
    dx = _mm(dxr, w0p[:, :512], "nt", name="l0_dx_rnn", add=dv0, add_scale=DN_ALPHA)
    dx = _mm(dgate, w0p[:, 512:1536], "nt", name="l0_dx_gate", add=dx)
    dx = _mm(dtail, w0p[:, 1536:], "nt", name="l0_dx_tail", add=dx)

    grads = {
        "g_rnn": g_rnn, "g_gate": g_gate, "g_tail": g_tail, "g_z": g_z, "g_xbc": g_xbc, "g_dt": g_dt,
        "g_wo0": g_wo0, "g_wo1": g_wo1, "g_wq": g_wq, "g_wkv": g_wkv,
        "ab_conv_w": dvec0[4:8], "ssd_conv_w": dcw1[0:4], "ssd_conv_b": dcw1[4], "ssd_norm": dnw[0],
        "ssd_ln_g": dgb1[0], "ssd_ln_b": dgb1[1],
        "ab_conv_b": dvec0[3], "ab_gate_a_w": _block_diag_t(g_wa), "ab_gate_a_b": dvec0[0],
        "ab_gate_x_w": _block_diag_t(g_wx), "ab_gate_x_b": dvec0[1], "ab_lambda": dvec0[2],
        "mla_q_norm": dqnw[0], "mla_kv_norm": dknw[0], "ab_ln_g": dgb0[0], "ab_ln_b": dgb0[1],
        "ssd_dt_bias": dvec1[0, :32], "ssd_a_log": dvec1[1, :32], "ssd_d": dvec1[2, :32],
    }
    return grads, loss8[0, 0], dx
```

```python
import math

import jax
import jax.numpy as jnp
from jax import lax
from jax.experimental import pallas as pl
from jax.experimental.pallas import tpu as pltpu

F32 = jnp.float32
MXU_DTYPE = jnp.bfloat16

N_DEV = 8
SEQ = 4096
D_MODEL = 1024
DN_ALPHA = 4.0 ** 0.25
RNN_W = 512
MLA_HEADS = 8
ATT_SCALE = 96.0 ** -0.5
ATT_C = ATT_SCALE * math.log2(math.e)
RG_C = 8.0
SSD_INNER = 2048
SSD_HEADS = 32
SSD_P = 64
SSD_GROUPS = 4
SSD_N = 128
SSD_L = 128
SSD_CONV = 3072
LANES = 128
SUBLANES = 8
VMEM_LIMIT = 56 * 1024 * 1024

ADAM_LR, ADAM_B1, ADAM_B2, ADAM_EPS, ADAM_WD, ADAM_STEP = 0.001, 0.9, 0.999, 1e-08, 0.01, 10

HIGHEST = lax.Precision.HIGHEST


def _params(sem, limit=VMEM_LIMIT):
    return pltpu.CompilerParams(dimension_semantics=sem, vmem_limit_bytes=limit)


def _dot(a, b):
    return lax.dot_general(a, b, (((1,), (0,)), ((), ())), preferred_element_type=F32)


def _dot_nt(a, b):
    return lax.dot_general(a, b, (((1,), (1,)), ((), ())), preferred_element_type=F32)


def _dot_tn(a, b):
    return lax.dot_general(a, b, (((0,), (0,)), ((), ())), preferred_element_type=F32)


def _dot_hi(a, b):
    return lax.dot_general(a, b, (((1,), (0,)), ((), ())), precision=HIGHEST, preferred_element_type=F32)


def _mx(v):
    return v.astype(MXU_DTYPE)


def _sigmoid(v):
    return 1.0 / (1.0 + jnp.exp(-v))


def _log1p_pos(e):
    poly = e * (1.0 - e * (0.5 - e * (1.0 / 3.0 - e * 0.25)))
    return jnp.where(e < 0.01, poly, jnp.log(1.0 + e))


def _softplus(v):
    return jnp.maximum(v, 0.0) + _log1p_pos(jnp.exp(-jnp.abs(v)))


def _neg_expm1(v):
    poly = -v * (1.0 + v * (0.5 + v * (1.0 / 6.0 + v * (1.0 / 24.0 + v * (1.0 / 120.0)))))
    return jnp.where(jnp.abs(v) < 0.1, poly, 1.0 - jnp.exp(v))


def _silu(v):
    return v * _sigmoid(v)


def _dsilu(v):
    s = _sigmoid(v)
    return s * (1.0 + v * (1.0 - s))


def _mm(a, b, mode, *, name, add=None, add_scale=1.0, out_dtype=F32, tm=None, tn=1024, tk=512):
    if mode == "tn":
        kdim, m = a.shape
        n = b.shape[1]
        tm, tn, tk = min(tm or 1024, m), min(tn, n), min(tk, kdim)

        def body_tn(a_ref, b_ref, o_ref):
            @pl.when(pl.program_id(2) == 0)
            def _():
                o_ref[...] = jnp.zeros_like(o_ref)

            o_ref[...] += _dot_tn(_mx(a_ref[...]), _mx(b_ref[...]))

        return pl.pallas_call(
            body_tn, name=name, grid=(m // tm, n // tn, kdim // tk),
            in_specs=[pl.BlockSpec((tk, tm), lambda i, j, k: (k, i)), pl.BlockSpec((tk, tn), lambda i, j, k: (k, j))],
            out_specs=pl.BlockSpec((tm, tn), lambda i, j, k: (i, j)),
            out_shape=jax.ShapeDtypeStruct((m, n), F32),
            compiler_params=_params(("parallel", "parallel", "arbitrary")),
        )(a, b)

    m, kdim = a.shape
    n = b.shape[1] if mode == "nn" else b.shape[0]
    tm, tn = min(tm or 512, m), min(tn, n)
    has_add = add is not None

    def body(*refs):
        a_ref, b_ref = refs[0], refs[1]
        o_ref = refs[-1]
        av, bv = _mx(a_ref[...]), _mx(b_ref[...])
        acc = _dot(av, bv) if mode == "nn" else _dot_nt(av, bv)
        if has_add:
            acc = acc + add_scale * refs[2][...]
        o_ref[...] = acc.astype(out_dtype)

    b_spec = (pl.BlockSpec((kdim, tn), lambda i, j: (0, j)) if mode == "nn"
              else pl.BlockSpec((tn, kdim), lambda i, j: (j, 0)))
    in_specs = [pl.BlockSpec((tm, kdim), lambda i, j: (i, 0)), b_spec]
    args = [a, b]
    if has_add:
        in_specs.append(pl.BlockSpec((tm, tn), lambda i, j: (i, j)))
        args.append(add)
    return pl.pallas_call(
        body, name=name, grid=(m // tm, n // tn), in_specs=in_specs,
        out_specs=pl.BlockSpec((tm, tn), lambda i, j: (i, j)),
        out_shape=jax.ShapeDtypeStruct((m, n), out_dtype),
        compiler_params=_params(("parallel", "parallel")),
    )(*args)


def _shift_down(blk, halo, s):
    if s == 0:
        return blk
    t = blk.shape[0]
    r = pltpu.roll(blk, s, 0)
    hr = pltpu.roll(halo, s, 0)
    row8 = lax.broadcasted_iota(jnp.int32, hr.shape, 0)
    head = jnp.where(row8 < s, hr, r[:SUBLANES])
    return jnp.concatenate([head, r[SUBLANES:]], axis=0) if t > SUBLANES else head


def _shift_up(blk, halo, s):
    if s == 0:
        return blk
    t = blk.shape[0]
    r = pltpu.roll(blk, t - s, 0)
    hr = pltpu.roll(halo, SUBLANES - s, 0)
    row8 = lax.broadcasted_iota(jnp.int32, hr.shape, 0)
    tail = jnp.where(row8 >= SUBLANES - s, hr, r[t - SUBLANES:])
    return jnp.concatenate([r[:t - SUBLANES], tail], axis=0) if t > SUBLANES else tail


def _scan_down(a, u):
    t = a.shape[0]
    row = lax.broadcasted_iota(jnp.int32, a.shape, 0)
    d = 1
    while d < t:
        keep = row >= d
        a_sh = jnp.where(keep, pltpu.roll(a, d, 0), 1.0)
        u_sh = jnp.where(keep, pltpu.roll(u, d, 0), 0.0)
        u = a * u_sh + u
        a = a * a_sh
        d *= 2
    return a, u


def _scan_up(a, u):
    t = a.shape[0]
    row = lax.broadcasted_iota(jnp.int32, a.shape, 0)
    d = 1
    while d < t:
        keep = row < t - d
        a_sh = jnp.where(keep, pltpu.roll(a, t - d, 0), 1.0)
        u_sh = jnp.where(keep, pltpu.roll(u, t - d, 0), 0.0)
        u = a * u_sh + u
        a = a * a_sh
        d *= 2
    return a, u


def _conv4(blk, halo, cw, cb):
    out = cb + blk * cw[3:4]
    for k in range(3):
        out = out + _shift_down(blk, halo, 3 - k) * cw[k:k + 1]
    return out


RG_T = 512


def _rg_gates(xc, wa, ba, wx, bx, lam):
    xcb = _mx(xc)
    r = _sigmoid(_dot(xcb, wa) + ba)
    ig = _sigmoid(_dot(xcb, wx) + bx)
    sp = _softplus(-lam)
    la = (-RG_C * r) * sp
    a = jnp.exp(la)
    mult = jnp.sqrt(_neg_expm1(2.0 * la))
    return r, ig, sp, a, mult


def _rglru_fwd(proj0, cw8, cb, wa, ba, wx, bx, lam):
    t, w = RG_T, RNN_W
    nb = SEQ // t

    def body(x_ref, halo_ref, cw_ref, cb_ref, wa_ref, ba_ref, wx_ref, bx_ref, lam_ref, xc_ref, h_ref, carry):
        i = pl.program_id(0)

        @pl.when(i == 0)
        def _():
            carry[...] = jnp.zeros_like(carry)

        blk = x_ref[...]
        halo = jnp.where(i > 0, halo_ref[...], 0.0)
        xc = _conv4(blk, halo, cw_ref[...], cb_ref[...])
        _, ig, _, a, mult = _rg_gates(xc, wa_ref[...], ba_ref[...], wx_ref[...], bx_ref[...], lam_ref[...])
        u = mult * (ig * xc)
        big_a, big_u = _scan_down(a, u)
        h = big_a * carry[SUBLANES - 1:SUBLANES, :] + big_u
        carry[...] = h[t - SUBLANES:]
        xc_ref[...] = xc
        h_ref[...] = h

    vec = pl.BlockSpec((1, w), lambda i: (0, 0))
    mat = pl.BlockSpec((w, w), lambda i: (0, 0))
    return pl.pallas_call(
        body, name="rglru_fwd", grid=(nb,),
        in_specs=[pl.BlockSpec((t, w), lambda i: (i, 0)),
                  pl.BlockSpec((SUBLANES, w), lambda i: (jnp.maximum(i * (t // SUBLANES) - 1, 0), 0)),
                  pl.BlockSpec((SUBLANES, w), lambda i: (0, 0)), vec, mat, vec, mat, vec, vec],
        out_specs=[pl.BlockSpec((t, w), lambda i: (i, 0)), pl.BlockSpec((t, w), lambda i: (i, 0))],
        out_shape=[jax.ShapeDtypeStruct((SEQ, w), F32), jax.ShapeDtypeStruct((SEQ, w), F32)],
        scratch_shapes=[pltpu.VMEM((SUBLANES, w), F32)],
        compiler_params=_params(("arbitrary",)),
    )(proj0, proj0, cw8, cb, wa, ba, wx, bx, lam)


def _rglru_bwd(dh, xc, h, proj0, cw8, wa, ba, wx, bx, lam):
    t, w = RG_T, RNN_W
    nb = SEQ // t
    tb = t // SUBLANES

    def body(dh_ref, xc_ref, h_ref, hh_ref, x_ref, xh_ref, cw_ref, wa_ref, ba_ref, wx_ref, bx_ref, lam_ref,
             dx_ref, dwa_ref, dwx_ref, dvec_ref, gcarry, dxc_next):
        i = pl.program_id(0)
        rev = nb - 1 - i

        @pl.when(i == 0)
        def _():
            gcarry[...] = jnp.zeros_like(gcarry)
            dxc_next[...] = jnp.zeros_like(dxc_next)
            dwa_ref[...] = jnp.zeros_like(dwa_ref)
            dwx_ref[...] = jnp.zeros_like(dwx_ref)
            dvec_ref[...] = jnp.zeros_like(dvec_ref)

        xc = xc_ref[...]
        wa_v, wx_v = wa_ref[...], wx_ref[...]
        lam_v = lam_ref[...]
        r, ig, sp, a, mult = _rg_gates(xc, wa_v, ba_ref[...], wx_v, bx_ref[...], lam_v)
        dhv = dh_ref[...]
        big_a, big_u = _scan_up(a, a * dhv)
        gg = big_a * gcarry[0:1, :] + big_u
        g = dhv + _shift_up(gg, gcarry[...], 1)
        gcarry[...] = gg[:SUBLANES]
        hhalo = jnp.where(rev > 0, hh_ref[...], 0.0)
        da = g * _shift_down(h_ref[...], hhalo, 1)
        d_mult = g * (ig * xc)
        d_i = g * (mult * xc)
        dxc = g * (mult * ig)
        d_la = da * a - d_mult * (a * a) / mult
        d_r = d_la * (-RG_C * sp)
        d_sp = jnp.sum(d_la * (-RG_C * r), axis=0, keepdims=True)
        d_pa = d_r * r * (1.0 - r)
        d_px = d_i * ig * (1.0 - ig)
        d_pab, d_pxb = _mx(d_pa), _mx(d_px)
        dxc = dxc + _dot_nt(d_pab, wa_v) + _dot_nt(d_pxb, wx_v)
        xcb = _mx(xc)
        dwa_ref[...] += _dot_tn(xcb, d_pab)
        dwx_ref[...] += _dot_tn(xcb, d_pxb)
        dvec_ref[0:1, :] += jnp.sum(d_pa, axis=0, keepdims=True)
        dvec_ref[1:2, :] += jnp.sum(d_px, axis=0, keepdims=True)
        dvec_ref[2:3, :] += d_sp * (-_sigmoid(-lam_v))
        dvec_ref[3:4, :] += jnp.sum(dxc, axis=0, keepdims=True)
        xblk = x_ref[...]
        xhalo = jnp.where(rev > 0, xh_ref[...], 0.0)
        cw = cw_ref[...]
        dx = dxc * cw[3:4]
        nxt = dxc_next[...]
        for k in range(4):
            dvec_ref[4 + k:5 + k, :] += jnp.sum(dxc * _shift_down(xblk, xhalo, 3 - k), axis=0, keepdims=True)
            if k < 3:
                dx = dx + _shift_up(dxc, nxt, 3 - k) * cw[k:k + 1]
        dxc_next[...] = dxc[:SUBLANES]
        dx_ref[...] = _mx(dx)

    blk = pl.BlockSpec((t, w), lambda i: (nb - 1 - i, 0))
    halo = pl.BlockSpec((SUBLANES, w), lambda i: (jnp.maximum((nb - 1 - i) * tb - 1, 0), 0))
    vec = pl.BlockSpec((1, w), lambda i: (0, 0))
    mat = pl.BlockSpec((w, w), lambda i: (0, 0))
    return pl.pallas_call(
        body, name="rglru_bwd", grid=(nb,),
        in_specs=[blk, blk, blk, halo, blk, halo, pl.BlockSpec((SUBLANES, w), lambda i: (0, 0)), mat, vec, mat, vec, vec],
        out_specs=[blk, mat, mat, pl.BlockSpec((16, w), lambda i: (0, 0))],
        out_shape=[jax.ShapeDtypeStruct((SEQ, w), MXU_DTYPE), jax.ShapeDtypeStruct((w, w), F32),
                   jax.ShapeDtypeStruct((w, w), F32), jax.ShapeDtypeStruct((16, w), F32)],
        scratch_shapes=[pltpu.VMEM((SUBLANES, w), F32), pltpu.VMEM((SUBLANES, w), F32)],
        compiler_params=_params(("arbitrary",)),
    )(dh, xc, h, h, proj0, proj0, cw8, wa, ba, wx, bx, lam)


MLA_T = 512


def _rope(v, c, sa, sb):
    return v * c + pltpu.roll(v, LANES - 16, 1) * sa + pltpu.roll(v, 16, 1) * sb


def _rope_t(dv, c, sa, sb):
    return dv * c + pltpu.roll(dv * sa, 16, 1) + pltpu.roll(dv * sb, LANES - 16, 1)


def _rms(v, g, eps=1e-6):
    rs = lax.rsqrt(jnp.mean(v * v, axis=-1, keepdims=True) + eps)
    return v * rs * g, rs


def _mla_norm_fwd(proj0, q_norm, kv_norm, tc, tsa, tsb):
    t = MLA_T

    def body(cq_ref, ck_ref, qn_ref, kn_ref, c_ref, sa_ref, sb_ref, oq_ref, ok_ref, okr_ref):
        oq_ref[...] = _mx(_rms(cq_ref[...], qn_ref[...])[0])
        ck = ck_ref[...]
        ok_ref[...] = _mx(_rms(ck[:, :LANES], kn_ref[...])[0])
        okr_ref[...] = _rope(ck[:, LANES:], c_ref[...], sa_ref[...], sb_ref[...])

    tab = pl.BlockSpec((t, LANES), lambda i: (i, 0))
    return pl.pallas_call(
        body, name="mla_norm_fwd", grid=(SEQ // t,),
        in_specs=[pl.BlockSpec((t, 256), lambda i: (i, 6)), pl.BlockSpec((t, 256), lambda i: (i, 7)),
                  pl.BlockSpec((1, 256), lambda i: (0, 0)), pl.BlockSpec((1, LANES), lambda i: (0, 0)), tab, tab, tab],
        out_specs=[pl.BlockSpec((t, 256), lambda i: (i, 0)), tab, tab],
        out_shape=[jax.ShapeDtypeStruct((SEQ, 256), MXU_DTYPE), jax.ShapeDtypeStruct((SEQ, LANES), MXU_DTYPE),
                   jax.ShapeDtypeStruct((SEQ, LANES), F32)],
        compiler_params=_params(("parallel",)),
    )(proj0, proj0, q_norm, kv_norm, tc, tsa, tsb)


def _mla_assemble(qraw, kvraw, kr, tc, tsa, tsb):
    t = MLA_T

    def body(q_ref, k_ref, v_ref, kr_ref, c_ref, sa_ref, sb_ref, oq_ref, ok_ref, ov_ref):
        c, sa, sb, krv = c_ref[...], sa_ref[...], sb_ref[...], kr_ref[...]
        for hd in range(MLA_HEADS):
            sl = slice(hd * LANES, (hd + 1) * LANES)
            oq_ref[:, sl] = _mx(_rope(q_ref[:, sl], c, sa, sb))
            ok_ref[:, sl] = _mx(k_ref[:, sl] + krv)
        ov_ref[...] = _mx(v_ref[...])

    tab = pl.BlockSpec((t, LANES), lambda i: (i, 0))
    wide = pl.BlockSpec((t, 1024), lambda i: (i, 0))
    return pl.pallas_call(
        body, name="mla_assemble", grid=(SEQ // t,),
        in_specs=[wide, wide, pl.BlockSpec((t, 512), lambda i: (i, 2)), tab, tab, tab, tab],
        out_specs=[wide, wide, pl.BlockSpec((t, 512), lambda i: (i, 0))],
        out_shape=[jax.ShapeDtypeStruct((SEQ, 1024), MXU_DTYPE), jax.ShapeDtypeStruct((SEQ, 1024), MXU_DTYPE),
                   jax.ShapeDtypeStruct((SEQ, 512), MXU_DTYPE)],
        compiler_params=_params(("parallel",)),
    )(qraw, kvraw, kvraw, kr, tc, tsa, tsb)


ATT_T = 512


def _flash_fwd(q, k, v, bcast=()):
    t = ATT_T
    nb = SEQ // t

    steps = [(qi, ki) for qi in range(nb) for ki in range(qi + 1)]
    qi_tab = jnp.asarray([s[0] for s in steps], jnp.int32)
    ki_tab = jnp.asarray([s[1] for s in steps], jnp.int32)

    nx = len(bcast)

    def body(qi_ref, ki_ref, q_ref, k_ref, v_ref, *rest):
        x_refs, (o_ref, lse_ref), g_refs = rest[:nx], rest[nx:nx + 2], rest[nx + 2:2 * nx + 2]
        m_sc, acc_sc = rest[2 * nx + 2:2 * nx + 4]
        step = pl.program_id(1)
        qi, ki = qi_ref[step], ki_ref[step]
        if nx:
            copies = _peer_copies(x_refs, g_refs, rest[2 * nx + 4:], 0)

            @pl.when((pl.program_id(0) == 0) & (step == 0))
            def _():
                for cp in copies:
                    cp.start()

        @pl.when(ki == 0)
        def _():
            m_sc[...] = jnp.full_like(m_sc, -jnp.inf)
            acc_sc[...] = jnp.zeros_like(acc_sc)

        def update(diagonal):
            vv = v_ref[...]
            lane_v = lax.broadcasted_iota(jnp.int32, vv.shape, 1)
            for hd in range(2):
                sl = slice(hd * LANES, (hd + 1) * LANES)
                s = _dot_nt(q_ref[:, sl], k_ref[:, sl])
                if diagonal:
                    s = jnp.where(lax.broadcasted_iota(jnp.int32, (t, t), 1)
                                  <= lax.broadcasted_iota(jnp.int32, (t, t), 0), s, -jnp.inf)
                m_prev = m_sc[hd]
                m_new = jnp.maximum(m_prev, jnp.max(s, axis=1, keepdims=True))
                p = jnp.exp2((s - m_new[:, :1]) * ATT_C)
                m_sc[hd] = m_new
                vh = jnp.where((lane_v >= hd * 64) & (lane_v < (hd + 1) * 64), vv, jnp.ones_like(vv))
                acc_sc[hd] = acc_sc[hd] * jnp.exp2((m_prev - m_new) * ATT_C) + _dot(_mx(p), vh)

        @pl.when(ki < qi)
        def _():
            update(False)

        @pl.when(ki == qi)
        def _():
            update(True)
            first = lax.broadcasted_iota(jnp.int32, (t, LANES), 1) < 64
            a0, a1 = acc_sc[0], acc_sc[1]
            l0, l1 = pltpu.roll(a0, 64, 1), pltpu.roll(a1, 64, 1)
            o_ref[...] = jnp.where(first, a0 / l0, a1 / l1)
            lse_ref[0] = jnp.where(first, m_sc[0] * ATT_SCALE + jnp.log(l0), m_sc[1] * ATT_SCALE + jnp.log(l1))

        if nx:
            @pl.when((pl.program_id(0) == 3) & (step == len(steps) - 1))
            def _():
                for cp in copies:
                    cp.wait()

    grid_spec = pltpu.PrefetchScalarGridSpec(
        num_scalar_prefetch=2, grid=(4, len(steps)),
        in_specs=[pl.BlockSpec((t, 256), lambda p, s, qt, kt: (qt[s], p)),
                  pl.BlockSpec((t, 256), lambda p, s, qt, kt: (kt[s], p)),
                  pl.BlockSpec((t, LANES), lambda p, s, qt, kt: (kt[s], p))] + [ANY] * nx,
        out_specs=[pl.BlockSpec((t, LANES), lambda p, s, qt, kt: (qt[s], p)),
                   pl.BlockSpec((1, t, LANES), lambda p, s, qt, kt: (p, qt[s], 0))] + [ANY] * nx,
        scratch_shapes=[pltpu.VMEM((2, t, LANES), F32), pltpu.VMEM((2, t, LANES), F32)]
        + (_exchange_sems(nx) if nx else []))
    res = pl.pallas_call(
        body, name="flash_fwd", grid_spec=grid_spec,
        out_shape=[jax.ShapeDtypeStruct((SEQ, 512), F32), jax.ShapeDtypeStruct((4, SEQ, LANES), F32)]
        + _exchange_shapes([], bcast),
        compiler_params=_params(("arbitrary", "arbitrary")),
    )(qi_tab, ki_tab, q, k, v, *bcast)
    return res[0], res[1], res[2:]


def _flash_bwd(q, k, v, o, do, lse, scatter=()):
    t = ATT_T
    nb = SEQ // t

    steps = [(qi, ki) for ki in range(nb) for qi in range(ki, nb)]
    qi_tab = jnp.asarray([s[0] for s in steps], jnp.int32)
    ki_tab = jnp.asarray([s[1] for s in steps], jnp.int32)
    log2e = math.log2(math.e)

    nx = len(scatter)

    def body(qi_ref, ki_ref, q_ref, k_ref, v_ref, o_ref, do_ref, lse_ref, *rest):
        x_refs, (dq_ref, dk_ref, dv_ref), g_refs = rest[:nx], rest[nx:nx + 3], rest[nx + 3:2 * nx + 3]
        step = pl.program_id(1)
        qi, ki = qi_ref[step], ki_ref[step]
        if nx:
            copies = _peer_copies(x_refs, g_refs, rest[2 * nx + 3:], nx)

            @pl.when((pl.program_id(0) == 0) & (step == 0))
            def _():
                for cp in copies:
                    cp.start()

        @pl.when(step == 0)
        def _():
            dq_ref[...] = jnp.zeros_like(dq_ref)

        @pl.when(qi == ki)
        def _():
            dk_ref[...] = jnp.zeros_like(dk_ref)
            dv_ref[...] = jnp.zeros_like(dv_ref)

        def update(diagonal):
            dov, ov, vv = do_ref[...], o_ref[...], v_ref[...]
            lse2 = lse_ref[0] * log2e
            lane = lax.broadcasted_iota(jnp.int32, (t, LANES), 1)
            prod = dov * ov
            qrows = pl.ds(pl.multiple_of(qi * t, t), t)
            dv_acc = jnp.zeros((t, LANES), F32)
            for hd in range(2):
                sl = slice(hd * LANES, (hd + 1) * LANES)
                mine = (lane >= hd * 64) & (lane < (hd + 1) * 64)
                qh, kh = q_ref[:, sl], k_ref[:, sl]
                p = jnp.exp2(_dot_nt(qh, kh) * ATT_C - lse2[:, hd * 64:hd * 64 + 1])
                if diagonal:
                    p = jnp.where(lax.broadcasted_iota(jnp.int32, (t, t), 1)
                                  <= lax.broadcasted_iota(jnp.int32, (t, t), 0), p, 0.0)
                do_h = jnp.where(mine, dov, 0.0)
                delta = jnp.sum(jnp.where(mine, prod, 0.0), axis=1, keepdims=True)
                dp = _dot_nt(_mx(do_h), vv)
                ds = _mx(p * (dp - delta) * ATT_SCALE)
                dv_acc = dv_acc + jnp.where(mine, _dot_tn(_mx(p), _mx(dov)), 0.0)
                dk_ref[:, sl] += _dot_tn(ds, qh)
                dq_ref[qrows, sl] += _dot(ds, kh)
            dv_ref[...] += dv_acc

        @pl.when(qi > ki)
        def _():
            update(False)

        @pl.when(qi == ki)
        def _():
            update(True)

        if nx:
            @pl.when((pl.program_id(0) == 3) & (step == len(steps) - 1))
            def _():
                for cp in copies:
                    cp.wait()

    qmap = lambda p, s, qt, kt: (qt[s], p)
    kmap = lambda p, s, qt, kt: (kt[s], p)
    grid_spec = pltpu.PrefetchScalarGridSpec(
        num_scalar_prefetch=2, grid=(4, len(steps)),
        in_specs=[pl.BlockSpec((t, 256), qmap), pl.BlockSpec((t, 256), kmap), pl.BlockSpec((t, LANES), kmap),
                  pl.BlockSpec((t, LANES), qmap), pl.BlockSpec((t, LANES), qmap),
                  pl.BlockSpec((1, t, LANES), lambda p, s, qt, kt: (p, qt[s], 0))] + [ANY] * nx,
        out_specs=[pl.BlockSpec((SEQ, 256), lambda p, s, qt, kt: (0, p)), pl.BlockSpec((t, 256), kmap),
                   pl.BlockSpec((t, LANES), kmap)] + [ANY] * nx,
        scratch_shapes=_exchange_sems(nx) if nx else [])
    res = pl.pallas_call(
        body, name="flash_bwd", grid_spec=grid_spec,
        out_shape=[jax.ShapeDtypeStruct((SEQ, 1024), F32), jax.ShapeDtypeStruct((SEQ, 1024), F32),
                   jax.ShapeDtypeStruct((SEQ, 512), F32)] + _exchange_shapes(scatter, []),
        compiler_params=_params(("arbitrary", "arbitrary")),
    )(qi_tab, ki_tab, q, k, v, o, do, lse, *scatter)
    return res[0], res[1], res[2], res[3:]


def _mla_bwd_rope(dq, dk, dv, tc, tsa, tsb):
    t = MLA_T

    def body(dq_ref, dk_ref, dv_ref, c_ref, sa_ref, sb_ref, oq_ref, okv_ref, okr_ref):
        c, sa, sb = c_ref[...], sa_ref[...], sb_ref[...]
        lane = lax.broadcasted_iota(jnp.int32, (t, LANES), 1)
        dkr = jnp.zeros((t, LANES), F32)
        for hd in range(MLA_HEADS):
            sl = slice(hd * LANES, (hd + 1) * LANES)
            oq_ref[:, sl] = _mx(_rope_t(dq_ref[:, sl], c, sa, sb))
            dkh = dk_ref[:, sl]
            okv_ref[:, sl] = _mx(dkh)
            dkr = dkr + dkh
        okv_ref[:, 1024:] = _mx(dv_ref[...])
        dkr = jnp.where((lane >= 64) & (lane < 96), dkr, 0.0)
        okr_ref[...] = _rope_t(dkr, c, sa, sb)

    tab = pl.BlockSpec((t, LANES), lambda i: (i, 0))
    wide = pl.BlockSpec((t, 1024), lambda i: (i, 0))
    return pl.pallas_call(
        body, name="mla_bwd_rope", grid=(SEQ // t,),
        in_specs=[wide, wide, pl.BlockSpec((t, 512), lambda i: (i, 0)), tab, tab, tab],
        out_specs=[wide, pl.BlockSpec((t, 1536), lambda i: (i, 0)), tab],
        out_shape=[jax.ShapeDtypeStruct((SEQ, 1024), MXU_DTYPE), jax.ShapeDtypeStruct((SEQ, 1536), MXU_DTYPE),
                   jax.ShapeDtypeStruct((SEQ, LANES), F32)],
        compiler_params=_params(("parallel",)),
    )(dq, dk, dv, tc, tsa, tsb)


def _rms_bwd(v, g, dy, eps=1e-6):
    rs = lax.rsqrt(jnp.mean(v * v, axis=-1, keepdims=True) + eps)
    xh = v * rs
    dxh = dy * g
    dv = rs * (dxh - xh * jnp.mean(dxh * xh, axis=-1, keepdims=True))
    return dv, jnp.sum(dy * xh, axis=0, keepdims=True)


def _mla_norm_bwd(proj0, dqn, dkn, dkr, q_norm, kv_norm):
    t = MLA_T

    def body(cq_ref, ck_ref, dqn_ref, dkn_ref, dkr_ref, qn_ref, kn_ref, o_ref, dgq_ref, dgk_ref):
        @pl.when(pl.program_id(0) == 0)
        def _():
            dgq_ref[...] = jnp.zeros_like(dgq_ref)
            dgk_ref[...] = jnp.zeros_like(dgk_ref)

        dcq, dgq = _rms_bwd(cq_ref[...], qn_ref[...], dqn_ref[...])
        dck, dgk = _rms_bwd(ck_ref[:, :LANES], kn_ref[...], dkn_ref[...])
        o_ref[:, :256] = _mx(dcq)
        o_ref[:, 256:384] = _mx(dck)
        o_ref[:, 384:] = _mx(dkr_ref[...])
        dgq_ref[0:1, :] += dgq
        dgk_ref[0:1, :] += dgk

    tab = pl.BlockSpec((t, LANES), lambda i: (i, 0))
    return pl.pallas_call(
        body, name="mla_norm_bwd", grid=(SEQ // t,),
        in_specs=[pl.BlockSpec((t, 256), lambda i: (i, 6)), pl.BlockSpec((t, 256), lambda i: (i, 7)),
                  pl.BlockSpec((t, 256), lambda i: (i, 0)), tab, tab,
                  pl.BlockSpec((1, 256), lambda i: (0, 0)), pl.BlockSpec((1, LANES), lambda i: (0, 0))],
        out_specs=[pl.BlockSpec((t, 512), lambda i: (i, 0)), pl.BlockSpec((SUBLANES, 256), lambda i: (0, 0)),
                   pl.BlockSpec((SUBLANES, LANES), lambda i: (0, 0))],
        out_shape=[jax.ShapeDtypeStruct((SEQ, 512), MXU_DTYPE), jax.ShapeDtypeStruct((SUBLANES, 256), F32),
                   jax.ShapeDtypeStruct((SUBLANES, LANES), F32)],
        compiler_params=_params(("arbitrary",)),
    )(proj0, proj0, dqn, dkn, dkr, q_norm, kv_norm)


LN_T = 512


def _ln(v, g, b, eps=1e-5):
    mu = jnp.mean(v, axis=-1, keepdims=True)
    xc = v - mu
    rs = lax.rsqrt(jnp.mean(xc * xc, axis=-1, keepdims=True) + eps)
    return xc * rs * g + b


def _ln_bwd(v, g, dy, eps=1e-5):
    mu = jnp.mean(v, axis=-1, keepdims=True)
    xc = v - mu
    rs = lax.rsqrt(jnp.mean(xc * xc, axis=-1, keepdims=True) + eps)
    xh = xc * rs
    dxh = dy * g
    dv = rs * (dxh - jnp.mean(dxh, axis=-1, keepdims=True) - xh * jnp.mean(dxh * xh, axis=-1, keepdims=True))
    return dv, jnp.sum(dy * xh, axis=0, keepdims=True), jnp.sum(dy, axis=0, keepdims=True)


def _l0_out(h, o, proj0, x, w_out, g, b):
    t = LN_T

    def body(h_ref, o_ref, ga_ref, gb_ref, x_ref, w_ref, g_ref, b_ref, y_ref, v_ref, x1_ref, x1b_ref):
        y = _mx(jnp.concatenate([h_ref[...] * _silu(ga_ref[...]), o_ref[...] * _silu(gb_ref[...])], axis=1))
        v = DN_ALPHA * x_ref[...] + _dot(y, w_ref[...])
        y_ref[...] = y
        v_ref[...] = v
        x1 = _ln(v, g_ref[...], b_ref[...])
        x1_ref[...] = x1
        x1b_ref[...] = _mx(x1)

    half = pl.BlockSpec((t, 512), lambda i: (i, 0))
    full = pl.BlockSpec((t, D_MODEL), lambda i: (i, 0))
    vec = pl.BlockSpec((1, D_MODEL), lambda i: (0, 0))
    return pl.pallas_call(
        body, name="l0_out", grid=(SEQ // t,),
        in_specs=[half, half, pl.BlockSpec((t, 512), lambda i: (i, 1)), pl.BlockSpec((t, 512), lambda i: (i, 2)), full,
                  pl.BlockSpec((D_MODEL, D_MODEL), lambda i: (0, 0)), vec, vec],
        out_specs=[full, full, full, full],
        out_shape=[jax.ShapeDtypeStruct((SEQ, D_MODEL), MXU_DTYPE), jax.ShapeDtypeStruct((SEQ, D_MODEL), F32),
                   jax.ShapeDtypeStruct((SEQ, D_MODEL), F32), jax.ShapeDtypeStruct((SEQ, D_MODEL), MXU_DTYPE)],
        compiler_params=_params(("parallel",)),
    )(h, o, proj0, proj0, x, w_out, g, b)


def _ln_bwd_call(v, dy, g):
    t = LN_T

    def body(v_ref, dy_ref, g_ref, dv_ref, dgb_ref):
        @pl.when(pl.program_id(0) == 0)
        def _():
            dgb_ref[...] = jnp.zeros_like(dgb_ref)

        dv, dg, db = _ln_bwd(v_ref[...], g_ref[...], dy_ref[...])
        dv_ref[...] = dv
        dgb_ref[0:1, :] += dg
        dgb_ref[1:2, :] += db

    full = pl.BlockSpec((t, D_MODEL), lambda i: (i, 0))
    return pl.pallas_call(
        body, name="ln_bwd", grid=(SEQ // t,),
        in_specs=[full, full, pl.BlockSpec((1, D_MODEL), lambda i: (0, 0))],
        out_specs=[full, pl.BlockSpec((SUBLANES, D_MODEL), lambda i: (0, 0))],
        out_shape=[jax.ShapeDtypeStruct((SEQ, D_MODEL), F32), jax.ShapeDtypeStruct((SUBLANES, D_MODEL), F32)],
        compiler_params=_params(("arbitrary",)),
    )(v, dy, g)


def _gate_bwd(dy, h, o, proj0):
    t = LN_T

    def body(dya_ref, dyb_ref, h_ref, o_ref, ga_ref, gb_ref, dh_ref, do_ref, dg_ref):
        ga, gb, dya, dyb = ga_ref[...], gb_ref[...], dya_ref[...], dyb_ref[...]
        dh_ref[...] = dya * _silu(ga)
        do_ref[...] = dyb * _silu(gb)
        dg_ref[:, :512] = _mx(dya * h_ref[...] * _dsilu(ga))
        dg_ref[:, 512:] = _mx(dyb * o_ref[...] * _dsilu(gb))

    half = pl.BlockSpec((t, 512), lambda i: (i, 0))
    half1 = pl.BlockSpec((t, 512), lambda i: (i, 1))
    full = pl.BlockSpec((t, 1024), lambda i: (i, 0))
    return pl.pallas_call(
        body, name="gate_bwd", grid=(SEQ // t,),
        in_specs=[half, half1, half, half, half1, pl.BlockSpec((t, 512), lambda i: (i, 2))],
        out_specs=[half, half, full],
        out_shape=[jax.ShapeDtypeStruct((SEQ, 512), F32), jax.ShapeDtypeStruct((SEQ, 512), F32),
                   jax.ShapeDtypeStruct((SEQ, 1024), MXU_DTYPE)],
        compiler_params=_params(("parallel",)),
    )(dy, dy, h, o, proj0, proj0)


CONV_T = 512
CONV_CB = 1024


def _ssd_conv_fwd(xbc, cw8, cb):
    t, cbk = CONV_T, CONV_CB
    tb = t // SUBLANES

    def body(x_ref, halo_ref, cw_ref, cb_ref, pre_ref, act_ref):
        halo = jnp.where(pl.program_id(1) > 0, halo_ref[...], 0.0)
        pre = _conv4(x_ref[...], halo, cw_ref[...], cb_ref[...])
        pre_ref[...] = pre
        act_ref[...] = _silu(pre)

    blk = pl.BlockSpec((t, cbk), lambda j, i: (i, j))
    return pl.pallas_call(
        body, name="ssd_conv_fwd", grid=(SSD_CONV // cbk, SEQ // t),
        in_specs=[blk, pl.BlockSpec((SUBLANES, cbk), lambda j, i: (jnp.maximum(i * tb - 1, 0), j)),
                  pl.BlockSpec((SUBLANES, cbk), lambda j, i: (0, j)), pl.BlockSpec((1, cbk), lambda j, i: (0, j))],
        out_specs=[blk, blk],
        out_shape=[jax.ShapeDtypeStruct((SEQ, SSD_CONV), F32), jax.ShapeDtypeStruct((SEQ, SSD_CONV), F32)],
        compiler_params=_params(("parallel", "parallel")),
    )(xbc, xbc, cw8, cb)


def _ssd_conv_bwd(dact, pre, xbc, cw8):
    t, cbk = CONV_T, CONV_CB
    tb = t // SUBLANES
    nb = SEQ // t

    def body(da_ref, dan_ref, pre_ref, pren_ref, x_ref, xh_ref, cw_ref, dx_ref, dcw_ref):
        i = pl.program_id(1)

        @pl.when(i == 0)
        def _():
            dcw_ref[...] = jnp.zeros_like(dcw_ref)

        dpre = da_ref[...] * _dsilu(pre_ref[...])
        dpre_next = jnp.where(i < nb - 1, dan_ref[...] * _dsilu(pren_ref[...]), 0.0)
        xblk = x_ref[...]
        xhalo = jnp.where(i > 0, xh_ref[...], 0.0)
        cw = cw_ref[...]
        dx = dpre * cw[3:4]
        for k in range(4):
            dcw_ref[k:k + 1, :] += jnp.sum(dpre * _shift_down(xblk, xhalo, 3 - k), axis=0, keepdims=True)
            if k < 3:
                dx = dx + _shift_up(dpre, dpre_next, 3 - k) * cw[k:k + 1]
        dcw_ref[4:5, :] += jnp.sum(dpre, axis=0, keepdims=True)
        dx_ref[...] = _mx(dx)

    blk = pl.BlockSpec((t, cbk), lambda j, i: (i, j))
    nxt = pl.BlockSpec((SUBLANES, cbk), lambda j, i: (jnp.minimum((i + 1) * tb, SEQ // SUBLANES - 1), j))
    prv = pl.BlockSpec((SUBLANES, cbk), lambda j, i: (jnp.maximum(i * tb - 1, 0), j))
    acc = pl.BlockSpec((SUBLANES, cbk), lambda j, i: (0, j))
    return pl.pallas_call(
        body, name="ssd_conv_bwd", grid=(SSD_CONV // cbk, nb),
        in_specs=[blk, nxt, blk, nxt, blk, prv, acc],
        out_specs=[blk, acc],
        out_shape=[jax.ShapeDtypeStruct((SEQ, SSD_CONV), MXU_DTYPE), jax.ShapeDtypeStruct((SUBLANES, SSD_CONV), F32)],
        compiler_params=_params(("parallel", "arbitrary")),
    )(dact, dact, pre, pre, xbc, xbc, cw8)


def _ssd_common(dt_raw, bias, alog, tril, xs):
    lane = lax.broadcasted_iota(jnp.int32, dt_raw.shape, 1)
    dt = jnp.where(lane < SSD_HEADS, _softplus(dt_raw + bias), 0.0)
    a_neg = -jnp.exp(alog)
    cs = _dot_hi(tril, dt * a_neg)
    dt_x = _expand_heads(dt)
    ecs_x = _expand_heads(jnp.exp(cs))
    ds_x = _expand_heads(jnp.exp(cs[SSD_L - 1:SSD_L, :] - cs))
    return dt, a_neg, cs, dt_x, None, xs * dt_x, ds_x, ecs_x, ecs_x[SSD_L - 1:SSD_L, :]


def _expand_heads(v):
    lane = lax.broadcasted_iota(jnp.int32, v.shape, 1)
    tiles = [jnp.where(lane < SSD_P, v[:, 2 * pr:2 * pr + 1], v[:, 2 * pr + 1:2 * pr + 2])
             for pr in range(SSD_HEADS // 2)]
    return jnp.concatenate(tiles, axis=1)


def _fold_heads(v, expand_t):
    hi = v.astype(jnp.bfloat16)
    lo = (v - hi.astype(F32)).astype(jnp.bfloat16)
    return _dot(hi, expand_t) + _dot(lo, expand_t)


def _ssd_decay(cs, cs_t, hh, causal):
    seg = cs[:, hh:hh + 1] - cs_t[hh:hh + 1, :]
    return jnp.where(causal, jnp.exp(jnp.where(causal, seg, 0.0)), 0.0)


def _ssd_scan_fwd(act, dt_raw, bias, alog, d_x, tril):
    nc = SEQ // SSD_L
    gw = SSD_INNER // SSD_GROUPS

    def body(act_ref, dt_ref, bias_ref, alog_ref, dx_ref, tril_ref, y_ref, hp_ref, h_sc):
        @pl.when(pl.program_id(0) == 0)
        def _():
            h_sc[...] = jnp.zeros_like(h_sc)

        xs = act_ref[:, :SSD_INNER]
        _, _, cs, _, _, xdt, ds_x, ecs_x, elast = _ssd_common(
            dt_ref[...], bias_ref[...], alog_ref[...], tril_ref[...], xs)
        cs_t = cs.T
        causal = (lax.broadcasted_iota(jnp.int32, (SSD_L, SSD_L), 0)
                  >= lax.broadcasted_iota(jnp.int32, (SSD_L, SSD_L), 1))
        lane = lax.broadcasted_iota(jnp.int32, (SSD_L, LANES), 1)
        xdt_b = _mx(xdt)
        xds_b = _mx(xdt * ds_x)
        hp_ref[0] = h_sc[...]
        for g in range(SSD_GROUPS):
            gs = slice(g * gw, (g + 1) * gw)
            bg = _mx(act_ref[:, SSD_INNER + g * SSD_N:SSD_INNER + (g + 1) * SSD_N])
            cg = _mx(act_ref[:, SSD_INNER + 512 + g * SSD_N:SSD_INNER + 512 + (g + 1) * SSD_N])
            cb = _dot_nt(cg, bg)
            hprev = h_sc[:, gs]
            yoff = _dot(cg, _mx(hprev)) * ecs_x[:, gs]
            h_sc[:, gs] = hprev * elast[:, gs] + _dot_tn(bg, xds_b[:, gs])
            for pr in range(4):
                ps = slice(g * gw + pr * LANES, g * gw + (pr + 1) * LANES)
                xp = xdt_b[:, ps]
                ydiag = jnp.zeros((SSD_L, LANES), F32)
                for j in range(2):
                    dm = _ssd_decay(cs, cs_t, g * 8 + pr * 2 + j, causal)
                    mine = (lane >= j * 64) & (lane < (j + 1) * 64)
                    ydiag = ydiag + _dot(_mx(cb * dm), jnp.where(mine, xp, jnp.zeros_like(xp)))
                y_ref[:, ps] = ydiag + yoff[:, pr * LANES:(pr + 1) * LANES] + dx_ref[:, ps] * xs[:, ps]

    const = lambda shape: pl.BlockSpec(shape, lambda c: (0, 0))
    return pl.pallas_call(
        body, name="ssd_scan_fwd", grid=(nc,),
        in_specs=[pl.BlockSpec((SSD_L, SSD_CONV), lambda c: (c, 0)), pl.BlockSpec((SSD_L, LANES), lambda c: (c, 0)),
                  const((1, LANES)), const((1, LANES)), const((1, SSD_INNER)), const((SSD_L, SSD_L))],
        out_specs=[pl.BlockSpec((SSD_L, SSD_INNER), lambda c: (c, 0)),
                   pl.BlockSpec((1, SSD_N, SSD_INNER), lambda c: (c, 0, 0))],
        out_shape=[jax.ShapeDtypeStruct((SEQ, SSD_INNER), F32), jax.ShapeDtypeStruct((nc, SSD_N, SSD_INNER), F32)],
        scratch_shapes=[pltpu.VMEM((SSD_N, SSD_INNER), F32)],
        compiler_params=_params(("arbitrary",)),
    )(act, dt_raw, bias, alog, d_x, tril)


def _ssd_scan_bwd(dy, act, dt_raw, hprev_all, bias, alog, d_x, tril, expand_t):
    nc = SEQ // SSD_L
    gw = SSD_INNER // SSD_GROUPS

    def body(dy_ref, act_ref, dt_ref, hp_ref, bias_ref, alog_ref, dx_ref, tril_ref, et_ref,
             dact_ref, ddt_ref, dvec_ref, dh_sc, dd_sc):
        i = pl.program_id(0)

        @pl.when(i == 0)
        def _():
            dh_sc[...] = jnp.zeros_like(dh_sc)
            dd_sc[...] = jnp.zeros_like(dd_sc)
            dvec_ref[...] = jnp.zeros_like(dvec_ref)

        xs = act_ref[:, :SSD_INNER]
        dt_raw_v, bias_v = dt_ref[...], bias_ref[...]
        dt, a_neg, cs, dt_x, _, xdt, ds_x, ecs_x, elast = _ssd_common(
            dt_raw_v, bias_v, alog_ref[...], tril_ref[...], xs)
        cs_t = cs.T
        rowi = lax.broadcasted_iota(jnp.int32, (SSD_L, SSD_L), 0)
        coli = lax.broadcasted_iota(jnp.int32, (SSD_L, SSD_L), 1)
        causal = rowi >= coli
        lane = lax.broadcasted_iota(jnp.int32, (SSD_L, LANES), 1)
        row_g = lax.broadcasted_iota(jnp.int32, (SSD_L, gw), 0)
        dyv = dy_ref[...]
        dd_sc[0:1, :] += jnp.sum(dyv * xs, axis=0, keepdims=True)
        xdt_b = _mx(xdt)
        xds = xdt * ds_x
        xds_b = _mx(xds)
        dy_b = _mx(dyv)
        dye_b = _mx(dyv * ecs_x)
        dcs = jnp.zeros((SSD_L, LANES), F32)
        dcs_t = jnp.zeros((LANES, SSD_L), F32)
        dcs_parts = []
        dxdt_parts = []
        for g in range(SSD_GROUPS):
            gs = slice(g * gw, (g + 1) * gw)
            bcol = slice(SSD_INNER + g * SSD_N, SSD_INNER + (g + 1) * SSD_N)
            ccol = slice(SSD_INNER + 512 + g * SSD_N, SSD_INNER + 512 + (g + 1) * SSD_N)
            bg, cg = _mx(act_ref[:, bcol]), _mx(act_ref[:, ccol])
            cb = _dot_nt(cg, bg)
            hp = hp_ref[0, :, gs]
            hp_b = _mx(hp)
            dh = dh_sc[:, gs]
            dh_b = _mx(dh)
            yoff = _dot(cg, hp_b) * ecs_x[:, gs]
            bdh = _dot(bg, dh_b)
            tt = xds[:, gs] * bdh
            last_row = (jnp.sum(tt, axis=0, keepdims=True)
                        + jnp.sum(dh * hp, axis=0, keepdims=True) * elast[:, gs])
            dcs_parts.append(dyv[:, gs] * yoff - tt + jnp.where(row_g == SSD_L - 1, last_row, 0.0))
            dc_g = _dot_nt(dye_b[:, gs], hp_b)
            db_g = _dot_nt(xds_b[:, gs], dh_b)
            dh_sc[:, gs] = _dot_tn(cg, dye_b[:, gs]) + dh * elast[:, gs]
            wsum = jnp.zeros((SSD_L, SSD_L), F32)
            dxdt_g = []
            for pr in range(4):
                ps = slice(g * gw + pr * LANES, g * gw + (pr + 1) * LANES)
                xp, dyp = xdt_b[:, ps], dy_b[:, ps]
                dxp = jnp.zeros((SSD_L, LANES), F32)
                for j in range(2):
                    hh = g * 8 + pr * 2 + j
                    dm = _ssd_decay(cs, cs_t, hh, causal)
                    mine = (lane >= j * 64) & (lane < (j + 1) * 64)
                    dy_h = jnp.where(mine, dyp, jnp.zeros_like(dyp))
                    wd = _dot_nt(dy_h, xp) * dm
                    wsum = wsum + wd
                    gmat = wd * cb
                    dcs = dcs + jnp.where(lane == hh, jnp.sum(gmat, axis=1, keepdims=True), 0.0)
                    dcs_t = dcs_t - jnp.where(rowi == hh, jnp.sum(gmat, axis=0, keepdims=True), 0.0)
                    dxp = dxp + _dot_tn(_mx(cb * dm), dy_h)
                dxdt_g.append(dxp)
            dxdt_parts.append(jnp.concatenate(dxdt_g, axis=1) + bdh * ds_x[:, gs])
            ws_b = _mx(wsum)
            dact_ref[:, ccol] = dc_g + _dot(ws_b, bg)
            dact_ref[:, bcol] = db_g + _dot_tn(ws_b, cg)
        dxdt = jnp.concatenate(dxdt_parts, axis=1)
        dcs_x = jnp.concatenate(dcs_parts, axis=1)
        et = et_ref[...]
        dcs_tot = dcs + dcs_t.T + _fold_heads(dcs_x, et)
        da_dt = _dot_hi((coli >= rowi).astype(F32), dcs_tot)
        ddt = da_dt * a_neg + _fold_heads(dxdt * xs, et)
        ddt_raw = ddt * _sigmoid(dt_raw_v + bias_v)
        ddt_ref[...] = ddt_raw
        dvec_ref[0:1, :] += jnp.sum(ddt_raw, axis=0, keepdims=True)
        dvec_ref[1:2, :] += jnp.sum(da_dt * dt, axis=0, keepdims=True) * a_neg
        dact_ref[:, :SSD_INNER] = dyv * dx_ref[...] + dxdt * dt_x

        @pl.when(i == nc - 1)
        def _():
            dvec_ref[2:3, :] = _fold_heads(dd_sc[...], et)[0:1, :]

    const = lambda shape: pl.BlockSpec(shape, lambda c: (0, 0))
    rev = lambda c: (nc - 1 - c, 0)
    return pl.pallas_call(
        body, name="ssd_scan_bwd", grid=(nc,),
        in_specs=[pl.BlockSpec((SSD_L, SSD_INNER), rev), pl.BlockSpec((SSD_L, SSD_CONV), rev),
                  pl.BlockSpec((SSD_L, LANES), rev),
                  pl.BlockSpec((1, SSD_N, SSD_INNER), lambda c: (nc - 1 - c, 0, 0)),
                  const((1, LANES)), const((1, LANES)), const((1, SSD_INNER)), const((SSD_L, SSD_L)),
                  const((SSD_INNER, LANES))],
        out_specs=[pl.BlockSpec((SSD_L, SSD_CONV), rev), pl.BlockSpec((SSD_L, LANES), rev), const((SUBLANES, LANES))],
        out_shape=[jax.ShapeDtypeStruct((SEQ, SSD_CONV), F32), jax.ShapeDtypeStruct((SEQ, LANES), F32),
                   jax.ShapeDtypeStruct((SUBLANES, LANES), F32)],
        scratch_shapes=[pltpu.VMEM((SSD_N, SSD_INNER), F32), pltpu.VMEM((SUBLANES, SSD_INNER), F32)],
        compiler_params=_params(("arbitrary",)),
    )(dy, act, dt_raw, hprev_all, bias, alog, d_x, tril, expand_t)


L1_T = 256


def _gated_norm(y, z, nw):
    y2 = y * _silu(z)
    gw = SSD_INNER // SSD_GROUPS
    outs, xhs, rss = [], [], []
    for g in range(SSD_GROUPS):
        gs = slice(g * gw, (g + 1) * gw)
        v = y2[:, gs]
        rs = lax.rsqrt(jnp.mean(v * v, axis=-1, keepdims=True) + 1e-6)
        xhs.append(v * rs)
        rss.append(rs)
        outs.append(v * rs * nw[:, gs])
    return outs, xhs, rss


def _l1_out(y, z, nw, w_out, x1, g, b, target):
    t = L1_T

    def body(y_ref, z_ref, nw_ref, w_ref, x1_ref, g_ref, b_ref, tg_ref, yn_ref, dv_ref, dgb_ref, loss_ref):
        @pl.when(pl.program_id(0) == 0)
        def _():
            dgb_ref[...] = jnp.zeros_like(dgb_ref)
            loss_ref[...] = jnp.zeros_like(loss_ref)

        outs, _, _ = _gated_norm(y_ref[...], z_ref[...], nw_ref[...])
        yn = _mx(jnp.concatenate(outs, axis=1))
        yn_ref[...] = yn
        v = DN_ALPHA * x1_ref[...] + _dot(yn, w_ref[...])
        gv = g_ref[...]
        err = _ln(v, gv, b_ref[...]) - tg_ref[...]
        rowsum = jnp.sum(err * err, axis=1, keepdims=True)
        loss_ref[...] += 0.5 * jnp.sum(rowsum, axis=0, keepdims=True) / D_MODEL
        dv, dg, db = _ln_bwd(v, gv, err / D_MODEL)
        dv_ref[...] = dv
        dgb_ref[0:1, :] += dg
        dgb_ref[1:2, :] += db

    wide = pl.BlockSpec((t, SSD_INNER), lambda i: (i, 0))
    full = pl.BlockSpec((t, D_MODEL), lambda i: (i, 0))
    vec = pl.BlockSpec((1, D_MODEL), lambda i: (0, 0))
    return pl.pallas_call(
        body, name="l1_out", grid=(SEQ // t,),
        in_specs=[wide, wide, pl.BlockSpec((1, SSD_INNER), lambda i: (0, 0)),
                  pl.BlockSpec((SSD_INNER, D_MODEL), lambda i: (0, 0)), full, vec, vec, full],
        out_specs=[wide, full, pl.BlockSpec((SUBLANES, D_MODEL), lambda i: (0, 0)),
                   pl.BlockSpec((SUBLANES, LANES), lambda i: (0, 0))],
        out_shape=[jax.ShapeDtypeStruct((SEQ, SSD_INNER), MXU_DTYPE), jax.ShapeDtypeStruct((SEQ, D_MODEL), F32),
                   jax.ShapeDtypeStruct((SUBLANES, D_MODEL), F32), jax.ShapeDtypeStruct((SUBLANES, LANES), F32)],
        compiler_params=_params(("arbitrary",)),
    )(y, z, nw, w_out, x1, g, b, target)


def _l1_gate_bwd(dyn, y, z, nw):
    t = L1_T
    gw = SSD_INNER // SSD_GROUPS

    def body(dyn_ref, y_ref, z_ref, nw_ref, dy_ref, dz_ref, dnw_ref):
        @pl.when(pl.program_id(0) == 0)
        def _():
            dnw_ref[...] = jnp.zeros_like(dnw_ref)

        yv, zv, nwv = y_ref[...], z_ref[...], nw_ref[...]
        _, xhs, rss = _gated_norm(yv, zv, nwv)
        sz, dsz = _silu(zv), _dsilu(zv)
        for g in range(SSD_GROUPS):
            gs = slice(g * gw, (g + 1) * gw)
            d_out = dyn_ref[:, gs]
            xh = xhs[g]
            dnw_ref[0:1, gs] += jnp.sum(d_out * xh, axis=0, keepdims=True)
            dxh = d_out * nwv[:, gs]
            dy2 = rss[g] * (dxh - xh * jnp.mean(dxh * xh, axis=-1, keepdims=True))
            dy_ref[:, gs] = dy2 * sz[:, gs]
            dz_ref[:, gs] = _mx(dy2 * yv[:, gs] * dsz[:, gs])

    wide = pl.BlockSpec((t, SSD_INNER), lambda i: (i, 0))
    return pl.pallas_call(
        body, name="l1_gate_bwd", grid=(SEQ // t,),
        in_specs=[wide, wide, wide, pl.BlockSpec((1, SSD_INNER), lambda i: (0, 0))],
        out_specs=[wide, wide, pl.BlockSpec((SUBLANES, SSD_INNER), lambda i: (0, 0))],
        out_shape=[jax.ShapeDtypeStruct((SEQ, SSD_INNER), F32), jax.ShapeDtypeStruct((SEQ, SSD_INNER), MXU_DTYPE),
                   jax.ShapeDtypeStruct((SUBLANES, SSD_INNER), F32)],
        compiler_params=_params(("arbitrary",)),
    )(dyn, y, z, nw)


MESH = pl.DeviceIdType.MESH
ANY = pl.BlockSpec(memory_space=pl.ANY)


def _flip(v, bit):
    return 1 - v if bit else v


def _all_gather(blocks, name):
    n = len(blocks)

    def body(*refs):
        x_refs, out_refs = refs[:n], refs[n:2 * n]
        send_sems, recv_sems, local_sems = refs[2 * n:]
        mx, my, mc = lax.axis_index("x"), lax.axis_index("y"), lax.axis_index("c")
        me, sibling = (mx, my, mc), (mx, my, 1 - mc)
        chips = [(1 - mx, my), (mx, 1 - my), (1 - mx, 1 - my)]

        def copy(a, k, block, to, own=False):
            px, py, pc = block
            slot = out_refs[a].at[4 * px + 2 * py + pc]
            return pltpu.make_async_remote_copy(
                src_ref=x_refs[a] if own else slot, dst_ref=slot,
                send_sem=send_sems.at[7 * a + k], recv_sem=recv_sems.at[7 * a + k], device_id=to, device_id_type=MESH)

        mine = [pltpu.make_async_copy(x_refs[a], out_refs[a].at[4 * mx + 2 * my + mc], local_sems.at[a])
                for a in range(n)]
        first = []
        for a in range(n):
            mine[a].start()
            first.append(copy(a, 0, me, sibling, own=True))
            first += [copy(a, 1 + j, me, (*chip, mc), own=True) for j, chip in enumerate(chips)]
        for cp in first:
            cp.start()
        passed = []
        for j, chip in enumerate(chips):
            for a in range(n):
                copy(a, 1 + j, (*chip, mc), me).wait_recv()
                fwd = copy(a, 4 + j, (*chip, mc), sibling)
                fwd.start()
                passed.append(fwd)
        for a in range(n):
            copy(a, 0, sibling, me).wait_recv()
            for j, chip in enumerate(chips):
                copy(a, 4 + j, (*chip, 1 - mc), me).wait_recv()
        for cp in first + passed:
            cp.wait_send()
        for cp in mine:
            cp.wait()

    return pl.pallas_call(
        body, name=name, in_specs=[ANY] * n, out_specs=[ANY] * n,
        out_shape=[jax.ShapeDtypeStruct((N_DEV,) + b.shape, b.dtype) for b in blocks],
        scratch_shapes=[pltpu.SemaphoreType.DMA((7 * n,)), pltpu.SemaphoreType.DMA((7 * n,)),
                        pltpu.SemaphoreType.DMA((n,))],
    )(*blocks)


def _exchange(scatter, bcast, name):
    n = len(scatter) + len(bcast)

    def body(*refs):
        copies = _peer_copies(refs[:n], refs[n:2 * n], refs[2 * n:], len(scatter))
        for cp in copies:
            cp.start()
        for cp in copies:
            cp.wait()

    return pl.pallas_call(
        body, name=name, in_specs=[ANY] * n, out_specs=[ANY] * n,
        out_shape=_exchange_shapes(scatter, bcast), scratch_shapes=_exchange_sems(n),
    )(*scatter, *bcast)


def _exchange_shapes(scatter, bcast):
    return ([jax.ShapeDtypeStruct(a.shape, a.dtype) for a in scatter]
            + [jax.ShapeDtypeStruct((N_DEV,) + a.shape, a.dtype) for a in bcast])


def _exchange_sems(n):
    return [pltpu.SemaphoreType.DMA((7 * n,)), pltpu.SemaphoreType.DMA((7 * n,)), pltpu.SemaphoreType.DMA((n,))]


def _peer_copies(in_refs, out_refs, sems, n_scatter):
    send_sems, recv_sems, local_sems = sems
    n = len(in_refs)
    mx, my, mc = lax.axis_index("x"), lax.axis_index("y"), lax.axis_index("c")
    me = 4 * mx + 2 * my + mc

    def src(a, slot):
        return in_refs[a].at[slot] if a < n_scatter else in_refs[a]

    copies = [pltpu.make_async_copy(src(a, me), out_refs[a].at[me], local_sems.at[a]) for a in range(n)]
    for k in range(1, N_DEV):
        px, py, pc = _flip(mx, (k >> 2) & 1), _flip(my, (k >> 1) & 1), _flip(mc, k & 1)
        for a in range(n):
            copies.append(pltpu.make_async_remote_copy(
                src_ref=src(a, 4 * px + 2 * py + pc), dst_ref=out_refs[a].at[me],
                send_sem=send_sems.at[7 * a + k - 1], recv_sem=recv_sems.at[7 * a + k - 1],
                device_id=(px, py, pc), device_id_type=MESH))
    return copies


def _segments(col_map, width):
    segs = []
    for lo, hi, arr, alo in col_map:
        for s in range(N_DEV):
            a, b = max(lo, s * width), min(hi, (s + 1) * width)
            if a < b:
                segs.append((s, a - s * width, b - a, arr, alo + a - lo))
    return segs


COPY_ROWS = 256


def _unshard(g8, col_map, widths, name):
    _, r, w = g8.shape
    rb = min(r, COPY_ROWS)
    segs = _segments(col_map, w)

    def body(g_ref, *o_refs):
        for o_ref in o_refs:
            o_ref[...] = jnp.zeros_like(o_ref)
        for s, llo, n, arr, alo in segs:
            o_refs[arr][:, alo:alo + n] = g_ref[s, :, llo:llo + n]

    return pl.pallas_call(
        body, name=name, grid=(r // rb,),
        in_specs=[pl.BlockSpec((N_DEV, rb, w), lambda i: (0, i, 0))],
        out_specs=[pl.BlockSpec((rb, n), lambda i: (i, 0)) for n in widths],
        out_shape=[jax.ShapeDtypeStruct((r, n), g8.dtype) for n in widths],
        compiler_params=_params(("parallel",)),
    )(g8)


def _reshard(srcs, col_map, w, dtype, name):
    r = srcs[0].shape[0]
    rb = min(r, COPY_ROWS)
    segs = _segments(col_map, w)

    def body(*refs):
        o_ref = refs[-1]
        for s, llo, n, arr, alo in segs:
            o_ref[s, :, llo:llo + n] = refs[arr][:, alo:alo + n].astype(dtype)

    return pl.pallas_call(
        body, name=name, grid=(r // rb,),
        in_specs=[pl.BlockSpec((rb, a.shape[1]), lambda i: (i, 0)) for a in srcs],
        out_specs=pl.BlockSpec((N_DEV, rb, w), lambda i: (0, i, 0)),
        out_shape=jax.ShapeDtypeStruct((N_DEV, r, w), dtype),
        compiler_params=_params(("parallel",)),
    )(*srcs)


def _adamw(parts, w, m, v, name):
    r, c = w.shape
    tr = COPY_ROWS if r % COPY_ROWS == 0 else r

    def body(p_ref, w_ref, m_ref, v_ref, g_ref, d_ref, mo_ref, vo_ref):
        g = p_ref[0].astype(F32)
        for s in range(1, N_DEV):
            g = g + p_ref[s].astype(F32)
        mn = ADAM_B1 * m_ref[...] + (1.0 - ADAM_B1) * g
        vn = ADAM_B2 * v_ref[...] + (1.0 - ADAM_B2) * (g * g)
        m_hat = mn / (1.0 - ADAM_B1 ** ADAM_STEP)
        v_hat = vn / (1.0 - ADAM_B2 ** ADAM_STEP)
        g_ref[...] = g
        d_ref[...] = -ADAM_LR * (m_hat / (jnp.sqrt(v_hat) + ADAM_EPS) + ADAM_WD * w_ref[...])
        mo_ref[...] = mn
        vo_ref[...] = vn

    blk = pl.BlockSpec((tr, c), lambda i: (i, 0))
    out = jax.ShapeDtypeStruct((r, c), F32)
    return pl.pallas_call(
        body, name=name, grid=(r // tr,),
        in_specs=[pl.BlockSpec((N_DEV, tr, c), lambda i: (0, i, 0)), blk, blk, blk],
        out_specs=[blk, blk, blk, blk], out_shape=[out, out, out, out],
        compiler_params=_params(("parallel",)),
    )(parts, w, m, v)


BIG = (("ab_w_in", (1024, 244), 1), ("ssd_w_in", (1024, 644), 1), ("ab_w_out", (128, 1024), 0),
       ("ssd_w_out", (256, 1024), 0), ("mla_w_uq", (256, 96), 1), ("mla_w_ukv", (128, 128), 1))
SMALL = (("ab_conv_w", (4, 64), 1), ("ssd_conv_w", (4, 384), 1), ("ssd_conv_b", (384,), 0),
         ("ssd_norm", (256,), 0), ("ssd_ln_g", (128,), 0), ("ssd_ln_b", (128,), 0))
REPL = (("ab_conv_b", (512,)), ("ab_gate_a_w", (8, 64, 64)), ("ab_gate_a_b", (512,)), ("ab_gate_x_w", (8, 64, 64)),
        ("ab_gate_x_b", (512,)), ("ab_lambda", (512,)), ("mla_q_norm", (256,)), ("mla_kv_norm", (128,)),
        ("ab_ln_g", (1024,)), ("ab_ln_b", (1024,)), ("ssd_dt_bias", (32,)), ("ssd_a_log", (32,)), ("ssd_d", (32,)))
SMALL_ROWS = 24
REPL_ROWS = 552

MAP_W0 = ((0, 1920, 0, 0), (1920, 1952, 0, 1984))
MAP_W1 = ((0, 2048, 0, 0), (2048, 5120, 1, 0), (5120, 5152, 2, 0))
MAP_WQ = tuple((96 * hd, 96 * hd + 96, 0, 128 * hd) for hd in range(8))
MAP_WKV = (tuple((128 * hd, 128 * hd + 64, 0, 128 * hd) for hd in range(8))
           + tuple((128 * hd + 64, 128 * hd + 128, 0, 1024 + 64 * hd) for hd in range(8)))
MAP_G0 = ((0, 512, 0, 0), (512, 1536, 1, 0), (1536, 1920, 2, 0), (1920, 1952, 2, 448))


def _pack_rows(pieces, rows, dtype):
    flat = jnp.concatenate([p.reshape(-1).astype(dtype) for p in pieces])
    return jnp.pad(flat, (0, rows * LANES - flat.shape[0])).reshape(rows, LANES)


def _unshard_flat(flat8, off, shape, axis):
    n = math.prod(shape)
    sh = flat8[:, off:off + n].reshape((N_DEV,) + shape)
    if axis == 0:
        return sh.reshape((N_DEV * shape[0],) + shape[1:])
    return jnp.moveaxis(sh, 0, len(shape) - 1).reshape(shape[:-1] + (N_DEV * shape[-1],))


def _to_shards(full, shape, axis):
    if axis == 0:
        return full.reshape(N_DEV, -1)
    sh = full.reshape(shape[:-1] + (N_DEV, shape[-1]))
    return jnp.moveaxis(sh, len(shape) - 1, 0).reshape(N_DEV, -1)


def _pad128(v):
    flat = v.reshape(-1)
    return jnp.pad(flat, (0, (-flat.shape[0]) % LANES))


def _block_diag(w):
    eye = jnp.eye(8, dtype=w.dtype)
    return (eye[:, None, :, None] * w[:, :, None, :]).reshape(RNN_W, RNN_W)


def _block_diag_t(d):
    d4 = d.reshape(8, 64, 8, 64)
    return jnp.stack([d4[hd, :, hd, :] for hd in range(8)])


def _row(v, width=None):
    v = v.reshape(1, -1)
    return v if width is None else jnp.pad(v, ((0, 0), (0, width - v.shape[1])))


def _taps8(w):
    return jnp.pad(w, ((0, 4), (0, 0)))


def kernel(x, positions, ab_w_in, ab_conv_w, ab_conv_b, ab_gate_a_w, ab_gate_a_b, ab_gate_x_w, ab_gate_x_b, ab_lambda, mla_q_norm, mla_kv_norm, mla_w_uq, mla_w_ukv, ab_w_out, ab_ln_g, ab_ln_b, ssd_w_in, ssd_conv_w, ssd_conv_b, ssd_dt_bias, ssd_a_log, ssd_d, ssd_norm, ssd_w_out, ssd_ln_g, ssd_ln_b, loss_target, m_ab_w_in, m_ab_conv_w, m_ab_conv_b, m_ab_gate_a_w, m_ab_gate_a_b, m_ab_gate_x_w, m_ab_gate_x_b, m_ab_lambda, m_mla_q_norm, m_mla_kv_norm, m_mla_w_uq, m_mla_w_ukv, m_ab_w_out, m_ab_ln_g, m_ab_ln_b, m_ssd_w_in, m_ssd_conv_w, m_ssd_conv_b, m_ssd_dt_bias, m_ssd_a_log, m_ssd_d, m_ssd_norm, m_ssd_w_out, m_ssd_ln_g, m_ssd_ln_b, v_ab_w_in, v_ab_conv_w, v_ab_conv_b, v_ab_gate_a_w, v_ab_gate_a_b, v_ab_gate_x_w, v_ab_gate_x_b, v_ab_lambda, v_mla_q_norm, v_mla_kv_norm, v_mla_w_uq, v_mla_w_ukv, v_ab_w_out, v_ab_ln_g, v_ab_ln_b, v_ssd_w_in, v_ssd_conv_w, v_ssd_conv_b, v_ssd_dt_bias, v_ssd_a_log, v_ssd_d, v_ssd_norm, v_ssd_w_out, v_ssd_ln_g, v_ssd_ln_b):
    args = dict(locals())
    w_in = {n: args[n][0] for n, *_ in BIG + SMALL + REPL}
    m_in = {n: args["m_" + n][0] for n, *_ in BIG + SMALL + REPL}
    v_in = {n: args["v_" + n][0] for n, *_ in BIG + SMALL + REPL}

    bf = MXU_DTYPE
    l0_names = ("ab_w_in", "ab_w_out", "mla_w_uq", "mla_w_ukv")
    gathered = _all_gather([w_in[n].astype(bf) for n in l0_names]
                           + [_pack_rows([w_in[n] for n, *_ in SMALL], SMALL_ROWS, F32)], "gather_params")
    g8 = dict(zip(l0_names, gathered))
    small8 = gathered[-1].reshape(N_DEV, -1)
    wts = {"wo0": g8["ab_w_out"].reshape(1024, 1024)}
    wts["w0p"], = _unshard(g8["ab_w_in"], MAP_W0, (2048,), "unshard_w0")
    wts["wq"], = _unshard(g8["mla_w_uq"], MAP_WQ, (1024,), "unshard_wq")
    wts["wkv"], = _unshard(g8["mla_w_ukv"], MAP_WKV, (1536,), "unshard_wkv")
    small, off = {}, 0
    for n, shape, axis in SMALL:
        small[n] = _unshard_flat(small8, off, shape, axis)
        off += math.prod(shape)

    pc, recv_l1, loss_part, grad_x = _local_step(
        x[0], positions[0], loss_target[0], wts, small, w_in, [w_in["ssd_w_in"].astype(bf), w_in["ssd_w_out"].astype(bf)])

    send = [_reshard([pc["g_rnn"], pc["g_gate"], pc["g_tail"]], MAP_G0, 244, bf, "reshard_w0"),
            pc["g_wo0"].astype(bf).reshape(N_DEV, 128, 1024),
            _reshard([pc["g_wq"]], MAP_WQ, 96, bf, "reshard_wq"), _reshard([pc["g_wkv"]], MAP_WKV, 128, bf, "reshard_wkv")]
    small_send = jnp.concatenate([_to_shards(pc[n], shape, axis) for n, shape, axis in SMALL], axis=1)
    small_send = jnp.pad(small_send, ((0, 0), (0, SMALL_ROWS * LANES - small_send.shape[1]))).reshape(N_DEV, SMALL_ROWS, LANES)
    repl_part = _pack_rows([_pad128(pc[n]) for n, _ in REPL], REPL_ROWS, F32)
    recv = _exchange(send + [small_send], [repl_part], "exchange_grads")

    outs = {}
    kinds = ("grad", "delta", "new_m", "new_v")
    for n, parts in zip(l0_names + ("ssd_w_in", "ssd_w_out"), list(recv[:4]) + list(recv_l1)):
        for kind, res in zip(kinds, _adamw(parts, w_in[n], m_in[n], v_in[n], "adamw_" + n)):
            outs[kind, n] = res[None]
    packs = [_pack_rows([src[n] for n, *_ in SMALL], SMALL_ROWS, F32) for src in (w_in, m_in, v_in)]
    res_s = [r.reshape(-1) for r in _adamw(recv[-2], *packs, "adamw_small")]
    packs = [_pack_rows([_pad128(src[n]) for n, _ in REPL], REPL_ROWS, F32) for src in (w_in, m_in, v_in)]
    res_r = [r.reshape(-1) for r in _adamw(recv[-1], *packs, "adamw_repl")]
    off = 0
    for n, shape, _ in SMALL:
        size = math.prod(shape)
        for kind, flat in zip(kinds, res_s):
            outs[kind, n] = flat[off:off + size].reshape(args[n].shape)
        off += size
    off = 0
    for n, shape in REPL:
        size = math.prod(shape)
        for kind, flat in zip(("grad", "delta", "new_m", "new_v"), res_r):
            outs[kind, n] = flat[off:off + size].reshape(args[n].shape)
        off += size + (-size) % LANES

    loss = lax.psum(loss_part, ("x", "y", "c"))
    order = ["ab_w_in", "ab_conv_w", "ab_conv_b", "ab_gate_a_w", "ab_gate_a_b", "ab_gate_x_w", "ab_gate_x_b",
             "ab_lambda", "mla_q_norm", "mla_kv_norm", "mla_w_uq", "mla_w_ukv", "ab_w_out", "ab_ln_g", "ab_ln_b",
             "ssd_w_in", "ssd_conv_w", "ssd_conv_b", "ssd_dt_bias", "ssd_a_log", "ssd_d", "ssd_norm", "ssd_w_out",
             "ssd_ln_g", "ssd_ln_b"]
    return (loss, grad_x[None], *[outs[kind, n] for kind in ("grad", "delta", "new_m", "new_v") for n in order])


def _local_step(x, pos, target, wts, full, rep, l1_blocks):
    bf = MXU_DTYPE
    inv_freq = 10000.0 ** (-jnp.arange(0, 32, 2, dtype=F32) / 32)
    ang = pos.astype(F32)[:, None] * inv_freq
    cos, sin = jnp.cos(ang), jnp.sin(ang)
    zeros = lambda n: jnp.zeros((SEQ, n), F32)
    tc = jnp.concatenate([jnp.ones((SEQ, 64), F32), cos, cos, zeros(32)], axis=1)
    tsa = jnp.concatenate([zeros(64), -sin, zeros(48)], axis=1)
    tsb = jnp.concatenate([zeros(80), sin, zeros(32)], axis=1)

    w0p, wq, wkv, wo0 = (wts[k] for k in ("w0p", "wq", "wkv", "wo0"))
    wa, wxg = _block_diag(rep["ab_gate_a_w"]).astype(bf), _block_diag(rep["ab_gate_x_w"]).astype(bf)
    cw0, cb0 = _taps8(full["ab_conv_w"]), _row(rep["ab_conv_b"])
    ba, bx, lam = _row(rep["ab_gate_a_b"]), _row(rep["ab_gate_x_b"]), _row(rep["ab_lambda"])
    qn_w, kn_w = _row(rep["mla_q_norm"]), _row(rep["mla_kv_norm"])
    g0, b0 = _row(rep["ab_ln_g"]), _row(rep["ab_ln_b"])
    cw1, cb1 = _taps8(full["ssd_conv_w"]), _row(full["ssd_conv_b"])
    dt_bias, a_log = _row(rep["ssd_dt_bias"], LANES), _row(rep["ssd_a_log"], LANES)
    d_x = _row(jnp.repeat(rep["ssd_d"], SSD_P))
    nw, g1, b1 = _row(full["ssd_norm"]), _row(full["ssd_ln_g"]), _row(full["ssd_ln_b"])
    tril = jnp.tril(jnp.ones((SSD_L, SSD_L), F32))
    expand_t = (jnp.arange(SSD_INNER)[:, None] // SSD_P == jnp.arange(LANES)[None, :]).astype(jnp.bfloat16)

    xb = x.astype(bf)
    proj0 = _mm(xb, w0p, "nn", name="l0_in")
    xc, h = _rglru_fwd(proj0, cw0, cb0, wa, ba, wxg, bx, lam)
    qn, kn, kr = _mla_norm_fwd(proj0, qn_w, kn_w, tc, tsa, tsb)
    qraw = _mm(qn, wq, "nn", name="mla_q")
    kvraw = _mm(kn, wkv, "nn", name="mla_kv", tn=512)
    qc, kc, vc = _mla_assemble(qraw, kvraw, kr, tc, tsa, tsb)
    o, lse, (w1_8, wo1_8) = _flash_fwd(qc, kc, vc, bcast=l1_blocks)
    w1z, w1x, w1d = _unshard(w1_8, MAP_W1, (2048, 3072, 128), "unshard_w1")
    wo1 = wo1_8.reshape(SSD_INNER, D_MODEL)
    y0, v0, x1, x1b = _l0_out(h, o, proj0, x, wo0, g0, b0)

    z = _mm(x1b, w1z, "nn", name="l1_in_z")
    xbc = _mm(x1b, w1x, "nn", name="l1_in_xbc")
    dt_raw = _mm(x1b, w1d, "nn", name="l1_in_dt")
    pre, act = _ssd_conv_fwd(xbc, cw1, cb1)
    ys, hprev = _ssd_scan_fwd(act, dt_raw, dt_bias, a_log, d_x, tril)
    yn, dv1, dgb1, loss8 = _l1_out(ys, z, nw, wo1, x1, g1, b1, target)

    g_wo1 = _mm(yn, dv1, "tn", name="l1_dwout")
    dyn = _mm(dv1, wo1, "nt", name="l1_dyn", tn=1024)
    dys, dz, dnw = _l1_gate_bwd(dyn, ys, z, nw)
    dact, ddt_raw, dvec1 = _ssd_scan_bwd(dys, act, dt_raw, hprev, dt_bias, a_log, d_x, tril, expand_t)
    dxbc, dcw1 = _ssd_conv_bwd(dact, pre, xbc, cw1)
    g_z, g_xbc = _mm(x1b, dz, "tn", name="l1_dw_z"), _mm(x1b, dxbc, "tn", name="l1_dw_xbc")
    g_dt = _mm(x1b, ddt_raw, "tn", name="l1_dw_dt")
    dx1 = _mm(dz, w1z, "nt", name="l1_dx_z", add=dv1, add_scale=DN_ALPHA)
    dx1 = _mm(dxbc, w1x, "nt", name="l1_dx_xbc", add=dx1)
    dx1 = _mm(ddt_raw, w1d, "nt", name="l1_dx_dt", add=dx1)

    dv0, dgb0 = _ln_bwd_call(v0, dx1, g0)
    g_wo0 = _mm(y0, dv0, "tn", name="l0_dwout")
    dy0 = _mm(dv0, wo0, "nt", name="l0_dy")
    dh, do, dgate = _gate_bwd(dy0, h, o, proj0)
    send_l1 = [_reshard([g_z, g_xbc, g_dt], MAP_W1, 644, bf, "reshard_w1"), g_wo1.astype(bf).reshape(N_DEV, 256, D_MODEL)]
    dq, dk, dvv, recv_l1 = _flash_bwd(qc, kc, vc, o, do, lse, scatter=send_l1)
    dqraw, dkvraw, dkr = _mla_bwd_rope(dq, dk, dvv, tc, tsa, tsb)
    g_wq = _mm(qn, dqraw, "tn", name="mla_dwq", tm=256)
    g_wkv = _mm(kn, dkvraw, "tn", name="mla_dwkv", tm=128, tn=512)
    dqn = _mm(dqraw, wq, "nt", name="mla_dqn", tn=256)
    dkn = _mm(dkvraw, wkv, "nt", name="mla_dkn", tn=128)
    dtail, dqnw, dknw = _mla_norm_bwd(proj0, dqn, dkn, dkr, qn_w, kn_w)
    dxr, g_wa, g_wx, dvec0 = _rglru_bwd(dh, xc, h, proj0, cw0, wa, ba, wxg, bx, lam)
    g_tail = _mm(xb, dtail, "tn", name="l0_dw_tail")
    g_rnn, g_gate = _mm(xb, dxr, "tn", name="l0_dw_rnn"), _mm(xb, dgate, "tn", name="l0_dw_gate")
    dx = _mm(dxr, w0p[:, :512], "nt", name="l0_dx_rnn", add=dv0, add_scale=DN_ALPHA)
    dx = _mm(dgate, w0p[:, 512:1536], "nt", name="l0_dx_gate", add=dx)
    dx = _mm(dtail, w0p[:, 1536:], "nt", name="l0_dx_tail", add=dx)

    grads = {
        "g_rnn": g_rnn, "g_gate": g_gate, "g_tail": g_tail, "g_wo0": g_wo0, "g_wq": g_wq, "g_wkv": g_wkv,
        "ab_conv_w": dvec0[4:8], "ssd_conv_w": dcw1[0:4], "ssd_conv_b": dcw1[4], "ssd_norm": dnw[0],
        "ssd_ln_g": dgb1[0], "ssd_ln_b": dgb1[1],
        "ab_conv_b": dvec0[3], "ab_gate_a_w": _block_diag_t(g_wa), "ab_gate_a_b": dvec0[0],
        "ab_gate_x_w": _block_diag_t(g_wx), "ab_gate_x_b": dvec0[1], "ab_lambda": dvec0[2],
        "mla_q_norm": dqnw[0], "mla_kv_norm": dknw[0], "ab_ln_g": dgb0[0], "ab_ln_b": dgb0[1],
        "ssd_dt_bias": dvec1[0, :32], "ssd_a_log": dvec1[1, :32], "ssd_d": dvec1[2, :32],
    }
    return grads, recv_l1, loss8[0, 0], dx
```

```python
import math

import jax
import jax.numpy as jnp
from jax import lax
from jax.experimental import pallas as pl
from jax.experimental.pallas import tpu as pltpu

F32 = jnp.float32
MXU_DTYPE = jnp.bfloat16

N_DEV = 8
SEQ = 4096
D_MODEL = 1024
DN_ALPHA = 4.0 ** 0.25
RNN_W = 512
MLA_HEADS = 8
ATT_SCALE = 96.0 ** -0.5
ATT_C = ATT_SCALE * math.log2(math.e)
RG_C = 8.0
SSD_INNER = 2048
SSD_HEADS = 32
SSD_P = 64
SSD_GROUPS = 4
SSD_N = 128
SSD_L = 128
SSD_CONV = 3072
LANES = 128
SUBLANES = 8
VMEM_LIMIT = 56 * 1024 * 1024

ADAM_LR, ADAM_B1, ADAM_B2, ADAM_EPS, ADAM_WD, ADAM_STEP = 0.001, 0.9, 0.999, 1e-08, 0.01, 10

HIGHEST = lax.Precision.HIGHEST


def _params(sem, limit=VMEM_LIMIT):
    return pltpu.CompilerParams(dimension_semantics=sem, vmem_limit_bytes=limit)


def _dot(a, b):
    return lax.dot_general(a, b, (((1,), (0,)), ((), ())), preferred_element_type=F32)


def _dot_nt(a, b):
    return lax.dot_general(a, b, (((1,), (1,)), ((), ())), preferred_element_type=F32)


def _dot_tn(a, b):
    return lax.dot_general(a, b, (((0,), (0,)), ((), ())), preferred_element_type=F32)


def _dot_hi(a, b):
    return lax.dot_general(a, b, (((1,), (0,)), ((), ())), precision=HIGHEST, preferred_element_type=F32)


def _mx(v):
    return v.astype(MXU_DTYPE)


def _sigmoid(v):
    return 1.0 / (1.0 + jnp.exp(-v))


def _log1p_pos(e):
    poly = e * (1.0 - e * (0.5 - e * (1.0 / 3.0 - e * 0.25)))
    return jnp.where(e < 0.01, poly, jnp.log(1.0 + e))


def _softplus(v):
    return jnp.maximum(v, 0.0) + _log1p_pos(jnp.exp(-jnp.abs(v)))


def _neg_expm1(v):
    poly = -v * (1.0 + v * (0.5 + v * (1.0 / 6.0 + v * (1.0 / 24.0 + v * (1.0 / 120.0)))))
    return jnp.where(jnp.abs(v) < 0.1, poly, 1.0 - jnp.exp(v))


def _silu(v):
    return v * _sigmoid(v)


def _dsilu(v):
    s = _sigmoid(v)
    return s * (1.0 + v * (1.0 - s))


def _mm(a, b, mode, *, name, add=None, add_scale=1.0, out_dtype=F32, tm=None, tn=1024, tk=512, b_col=0):
    if mode == "tn":
        kdim, m = a.shape
        n = b.shape[1]
        tm, tn, tk = min(tm or 1024, m), min(tn, n), min(tk, kdim)

        def body_tn(a_ref, b_ref, o_ref):
            @pl.when(pl.program_id(2) == 0)
            def _():
                o_ref[...] = jnp.zeros_like(o_ref)

            o_ref[...] += _dot_tn(_mx(a_ref[...]), _mx(b_ref[...]))

        return pl.pallas_call(
            body_tn, name=name, grid=(m // tm, n // tn, kdim // tk),
            in_specs=[pl.BlockSpec((tk, tm), lambda i, j, k: (k, i)), pl.BlockSpec((tk, tn), lambda i, j, k: (k, j))],
            out_specs=pl.BlockSpec((tm, tn), lambda i, j, k: (i, j)),
            out_shape=jax.ShapeDtypeStruct((m, n), F32),
            compiler_params=_params(("parallel", "parallel", "arbitrary")),
        )(a, b)

    m, kdim = a.shape
    n = b.shape[1] if mode == "nn" else b.shape[0]
    tm, tn = min(tm or 1024, m), min(tn, n)
    has_add = add is not None

    def body(*refs):
        a_ref, b_ref = refs[0], refs[1]
        o_ref = refs[-1]
        av, bv = _mx(a_ref[...]), _mx(b_ref[...])
        acc = _dot(av, bv) if mode == "nn" else _dot_nt(av, bv)
        if has_add:
            acc = acc + add_scale * refs[2][...]
        o_ref[...] = acc.astype(out_dtype)

    b_spec = (pl.BlockSpec((kdim, tn), lambda i, j: (0, j)) if mode == "nn"
              else pl.BlockSpec((tn, kdim), lambda i, j: (j, b_col)))
    in_specs = [pl.BlockSpec((tm, kdim), lambda i, j: (i, 0)), b_spec]
    args = [a, b]
    if has_add:
        in_specs.append(pl.BlockSpec((tm, tn), lambda i, j: (i, j)))
        args.append(add)
    return pl.pallas_call(
        body, name=name, grid=(m // tm, n // tn), in_specs=in_specs,
        out_specs=pl.BlockSpec((tm, tn), lambda i, j: (i, j)),
        out_shape=jax.ShapeDtypeStruct((m, n), out_dtype),
        compiler_params=_params(("parallel", "parallel")),
    )(*args)


def _shift_down(blk, halo, s):
    if s == 0:
        return blk
    t = blk.shape[0]
    r = pltpu.roll(blk, s, 0)
    hr = pltpu.roll(halo, s, 0)
    row8 = lax.broadcasted_iota(jnp.int32, hr.shape, 0)
    head = jnp.where(row8 < s, hr, r[:SUBLANES])
    return jnp.concatenate([head, r[SUBLANES:]], axis=0) if t > SUBLANES else head


def _shift_up(blk, halo, s):
    if s == 0:
        return blk
    t = blk.shape[0]
    r = pltpu.roll(blk, t - s, 0)
    hr = pltpu.roll(halo, SUBLANES - s, 0)
    row8 = lax.broadcasted_iota(jnp.int32, hr.shape, 0)
    tail = jnp.where(row8 >= SUBLANES - s, hr, r[t - SUBLANES:])
    return jnp.concatenate([r[:t - SUBLANES], tail], axis=0) if t > SUBLANES else tail


def _scan_down(a, u):
    t = a.shape[0]
    row = lax.broadcasted_iota(jnp.int32, a.shape, 0)
    d = 1
    while d < t:
        keep = row >= d
        a_sh = jnp.where(keep, pltpu.roll(a, d, 0), 1.0)
        u_sh = jnp.where(keep, pltpu.roll(u, d, 0), 0.0)
        u = a * u_sh + u
        a = a * a_sh
        d *= 2
    return a, u


def _scan_up(a, u):
    t = a.shape[0]
    row = lax.broadcasted_iota(jnp.int32, a.shape, 0)
    d = 1
    while d < t:
        keep = row < t - d
        a_sh = jnp.where(keep, pltpu.roll(a, t - d, 0), 1.0)
        u_sh = jnp.where(keep, pltpu.roll(u, t - d, 0), 0.0)
        u = a * u_sh + u
        a = a * a_sh
        d *= 2
    return a, u


def _conv4(blk, halo, cw, cb):
    out = cb + blk * cw[3:4]
    for k in range(3):
        out = out + _shift_down(blk, halo, 3 - k) * cw[k:k + 1]
    return out


RG_T = 512
P0_RNN = 2


def _rg_gates(xc, wa, ba, wx, bx, lam):
    xcb = _mx(xc)
    r = _sigmoid(_dot(xcb, wa) + ba)
    ig = _sigmoid(_dot(xcb, wx) + bx)
    sp = _softplus(-lam)
    la = (-RG_C * r) * sp
    a = jnp.exp(la)
    mult = jnp.sqrt(_neg_expm1(2.0 * la))
    return r, ig, sp, a, mult


def _rglru_fwd(proj0, cw8, cb, wa, ba, wx, bx, lam):
    t, w = RG_T, RNN_W
    nb = SEQ // t

    def body(x_ref, halo_ref, cw_ref, cb_ref, wa_ref, ba_ref, wx_ref, bx_ref, lam_ref, xc_ref, h_ref, carry):
        i = pl.program_id(0)

        @pl.when(i == 0)
        def _():
            carry[...] = jnp.zeros_like(carry)

        blk = x_ref[...]
        halo = jnp.where(i > 0, halo_ref[...], 0.0)
        xc = _conv4(blk, halo, cw_ref[...], cb_ref[...])
        _, ig, _, a, mult = _rg_gates(xc, wa_ref[...], ba_ref[...], wx_ref[...], bx_ref[...], lam_ref[...])
        u = mult * (ig * xc)
        big_a, big_u = _scan_down(a, u)
        h = big_a * carry[SUBLANES - 1:SUBLANES, :] + big_u
        carry[...] = h[t - SUBLANES:]
        xc_ref[...] = xc
        h_ref[...] = h

    vec = pl.BlockSpec((1, w), lambda i: (0, 0))
    mat = pl.BlockSpec((w, w), lambda i: (0, 0))
    return pl.pallas_call(
        body, name="rglru_fwd", grid=(nb,),
        in_specs=[pl.BlockSpec((t, w), lambda i: (i, P0_RNN)),
                  pl.BlockSpec((SUBLANES, w), lambda i: (jnp.maximum(i * (t // SUBLANES) - 1, 0), P0_RNN)),
                  pl.BlockSpec((SUBLANES, w), lambda i: (0, 0)), vec, mat, vec, mat, vec, vec],
        out_specs=[pl.BlockSpec((t, w), lambda i: (i, 0)), pl.BlockSpec((t, w), lambda i: (i, 0))],
        out_shape=[jax.ShapeDtypeStruct((SEQ, w), F32), jax.ShapeDtypeStruct((SEQ, w), F32)],
        scratch_shapes=[pltpu.VMEM((SUBLANES, w), F32)],
        compiler_params=_params(("arbitrary",)),
    )(proj0, proj0, cw8, cb, wa, ba, wx, bx, lam)


def _rglru_bwd(dh, xc, h, proj0, cw8, wa, ba, wx, bx, lam):
    t, w = RG_T, RNN_W
    nb = SEQ // t
    tb = t // SUBLANES

    def body(dh_ref, xc_ref, h_ref, hh_ref, x_ref, cw_ref, wa_ref, ba_ref, wx_ref, bx_ref, lam_ref,
             dx_ref, dwa_ref, dwx_ref, dvec_ref, gcarry, dxc_next):
        i = pl.program_id(0)
        rev = nb - 1 - i

        @pl.when(i == 0)
        def _():
            gcarry[...] = jnp.zeros_like(gcarry)
            dxc_next[...] = jnp.zeros_like(dxc_next)
            dwa_ref[...] = jnp.zeros_like(dwa_ref)
            dwx_ref[...] = jnp.zeros_like(dwx_ref)
            dvec_ref[...] = jnp.zeros_like(dvec_ref)

        xc = xc_ref[...]
        wa_v, wx_v = wa_ref[...], wx_ref[...]
        lam_v = lam_ref[...]
        r, ig, sp, a, mult = _rg_gates(xc, wa_v, ba_ref[...], wx_v, bx_ref[...], lam_v)
        dhv = dh_ref[...]
        big_a, big_u = _scan_up(a, a * dhv)
        gg = big_a * gcarry[0:1, :] + big_u
        g = dhv + _shift_up(gg, gcarry[...], 1)
        gcarry[...] = gg[:SUBLANES]
        hhalo = jnp.where(rev > 0, hh_ref[...], 0.0)
        da = g * _shift_down(h_ref[...], hhalo, 1)
        d_mult = g * (ig * xc)
        d_i = g * (mult * xc)
        dxc = g * (mult * ig)
        d_la = da * a - d_mult * (a * a) / mult
        d_r = d_la * (-RG_C * sp)
        d_sp = jnp.sum(d_la * (-RG_C * r), axis=0, keepdims=True)
        d_pa = d_r * r * (1.0 - r)
        d_px = d_i * ig * (1.0 - ig)
        d_pab, d_pxb = _mx(d_pa), _mx(d_px)
        dxc = dxc + _dot_nt(d_pab, wa_v) + _dot_nt(d_pxb, wx_v)
        xcb = _mx(xc)
        dwa_ref[...] += _dot_tn(xcb, d_pab)
        dwx_ref[...] += _dot_tn(xcb, d_pxb)
        dvec_ref[0:1, :] += jnp.sum(d_pa, axis=0, keepdims=True)
        dvec_ref[1:2, :] += jnp.sum(d_px, axis=0, keepdims=True)
        dvec_ref[2:3, :] += d_sp * (-_sigmoid(-lam_v))
        dvec_ref[3:4, :] += jnp.sum(dxc, axis=0, keepdims=True)
        xblk = x_ref[...]
        cw = cw_ref[...]
        dx = dxc * cw[3:4]
        nxt = dxc_next[...]
        dvec_ref[7:8, :] += jnp.sum(dxc * xblk, axis=0, keepdims=True)
        for k in range(3):
            up = _shift_up(dxc, nxt, 3 - k)
            dvec_ref[4 + k:5 + k, :] += jnp.sum(up * xblk, axis=0, keepdims=True)
            dx = dx + up * cw[k:k + 1]
        dxc_next[...] = dxc[:SUBLANES]
        dx_ref[...] = _mx(dx)

    blk = pl.BlockSpec((t, w), lambda i: (nb - 1 - i, 0))
    halo = pl.BlockSpec((SUBLANES, w), lambda i: (jnp.maximum((nb - 1 - i) * tb - 1, 0), 0))
    vec = pl.BlockSpec((1, w), lambda i: (0, 0))
    mat = pl.BlockSpec((w, w), lambda i: (0, 0))
    return pl.pallas_call(
        body, name="rglru_bwd", grid=(nb,),
        in_specs=[blk, blk, blk, halo, pl.BlockSpec((t, w), lambda i: (nb - 1 - i, P0_RNN)),
                  pl.BlockSpec((SUBLANES, w), lambda i: (0, 0)), mat, vec, mat, vec, vec],
        out_specs=[blk, mat, mat, pl.BlockSpec((16, w), lambda i: (0, 0))],
        out_shape=[jax.ShapeDtypeStruct((SEQ, w), MXU_DTYPE), jax.ShapeDtypeStruct((w, w), F32),
                   jax.ShapeDtypeStruct((w, w), F32), jax.ShapeDtypeStruct((16, w), F32)],
        scratch_shapes=[pltpu.VMEM((SUBLANES, w), F32), pltpu.VMEM((SUBLANES, w), F32)],
        compiler_params=_params(("arbitrary",)),
    )(dh, xc, h, h, proj0, cw8, wa, ba, wx, bx, lam)


MLA_T = 512


def _rope(v, c, sa, sb):
    return v * c + pltpu.roll(v, LANES - 16, 1) * sa + pltpu.roll(v, 16, 1) * sb


def _rope_t(dv, c, sa, sb):
    return dv * c + pltpu.roll(dv * sa, 16, 1) + pltpu.roll(dv * sb, LANES - 16, 1)


def _rms(v, g, eps=1e-6):
    rs = lax.rsqrt(jnp.mean(v * v, axis=-1, keepdims=True) + eps)
    return v * rs * g, rs


def _mla_norm_fwd(proj0, q_norm, kv_norm, tc, tsa, tsb):
    t = MLA_T

    def body(cq_ref, ck_ref, qn_ref, kn_ref, c_ref, sa_ref, sb_ref, oq_ref, ok_ref, okr_ref):
        oq_ref[...] = _mx(_rms(cq_ref[...], qn_ref[...])[0])
        ck = ck_ref[...]
        ok_ref[...] = _mx(_rms(ck[:, :LANES], kn_ref[...])[0])
        okr_ref[...] = _rope(ck[:, LANES:], c_ref[...], sa_ref[...], sb_ref[...])

    tab = pl.BlockSpec((t, LANES), lambda i: (i, 0))
    return pl.pallas_call(
        body, name="mla_norm_fwd", grid=(SEQ // t,),
        in_specs=[pl.BlockSpec((t, 256), lambda i: (i, 6)), pl.BlockSpec((t, 256), lambda i: (i, 7)),
                  pl.BlockSpec((1, 256), lambda i: (0, 0)), pl.BlockSpec((1, LANES), lambda i: (0, 0)), tab, tab, tab],
        out_specs=[pl.BlockSpec((t, 256), lambda i: (i, 0)), tab, tab],
        out_shape=[jax.ShapeDtypeStruct((SEQ, 256), MXU_DTYPE), jax.ShapeDtypeStruct((SEQ, LANES), MXU_DTYPE),
                   jax.ShapeDtypeStruct((SEQ, LANES), F32)],
        compiler_params=_params(("parallel",)),
    )(proj0, proj0, q_norm, kv_norm, tc, tsa, tsb)


def _mla_assemble(qraw, kvraw, kr, tc, tsa, tsb):
    t = MLA_T

    def body(q_ref, k_ref, v_ref, kr_ref, c_ref, sa_ref, sb_ref, oq_ref, ok_ref, ov_ref):
        c, sa, sb, krv = c_ref[...], sa_ref[...], sb_ref[...], kr_ref[...]
        for hd in range(MLA_HEADS):
            sl = slice(hd * LANES, (hd + 1) * LANES)
            oq_ref[:, sl] = _mx(_rope(q_ref[:, sl], c, sa, sb))
            ok_ref[:, sl] = _mx(k_ref[:, sl] + krv)
        ov_ref[...] = _mx(v_ref[...])

    tab = pl.BlockSpec((t, LANES), lambda i: (i, 0))
    wide = pl.BlockSpec((t, 1024), lambda i: (i, 0))
    return pl.pallas_call(
        body, name="mla_assemble", grid=(SEQ // t,),
        in_specs=[wide, wide, pl.BlockSpec((t, 512), lambda i: (i, 2)), tab, tab, tab, tab],
        out_specs=[wide, wide, pl.BlockSpec((t, 512), lambda i: (i, 0))],
        out_shape=[jax.ShapeDtypeStruct((SEQ, 1024), MXU_DTYPE), jax.ShapeDtypeStruct((SEQ, 1024), MXU_DTYPE),
                   jax.ShapeDtypeStruct((SEQ, 512), MXU_DTYPE)],
        compiler_params=_params(("parallel",)),
    )(qraw, kvraw, kvraw, kr, tc, tsa, tsb)


ATT_T = 512


def _flash_fwd(q, k, v, bcast=()):
    t = ATT_T
    nb = SEQ // t

    steps = [(qi, ki) for qi in range(nb) for ki in range(qi + 1)]
    qi_tab = jnp.asarray([s[0] for s in steps], jnp.int32)
    ki_tab = jnp.asarray([s[1] for s in steps], jnp.int32)

    nx = len(bcast)

    def body(qi_ref, ki_ref, q_ref, k_ref, v_ref, *rest):
        x_refs, (o_ref, lse_ref), g_refs = rest[:nx], rest[nx:nx + 2], rest[nx + 2:2 * nx + 2]
        m_sc, acc_sc = rest[2 * nx + 2:2 * nx + 4]
        step = pl.program_id(1)
        qi, ki = qi_ref[step], ki_ref[step]
        if nx:
            copies = _peer_copies(x_refs, g_refs, rest[2 * nx + 4:], 0)

            @pl.when((pl.program_id(0) == 0) & (step == 0))
            def _():
                for cp in copies:
                    cp.start()

        @pl.when(ki == 0)
        def _():
            m_sc[...] = jnp.full_like(m_sc, -jnp.inf)
            acc_sc[...] = jnp.zeros_like(acc_sc)

        def update(diagonal):
            vv = v_ref[...]
            lane_v = lax.broadcasted_iota(jnp.int32, vv.shape, 1)
            for hd in range(2):
                sl = slice(hd * LANES, (hd + 1) * LANES)
                s = _dot_nt(q_ref[:, sl], k_ref[:, sl])
                if diagonal:
                    s = jnp.where(lax.broadcasted_iota(jnp.int32, (t, t), 1)
                                  <= lax.broadcasted_iota(jnp.int32, (t, t), 0), s, -jnp.inf)
                m_prev = m_sc[hd]
                m_new = jnp.maximum(m_prev, jnp.max(s, axis=1, keepdims=True))
                p = jnp.exp2((s - m_new[:, :1]) * ATT_C)
                m_sc[hd] = m_new
                vh = jnp.where((lane_v >= hd * 64) & (lane_v < (hd + 1) * 64), vv, jnp.ones_like(vv))
                acc_sc[hd] = acc_sc[hd] * jnp.exp2((m_prev - m_new) * ATT_C) + _dot(_mx(p), vh)

        @pl.when(ki < qi)
        def _():
            update(False)

        @pl.when(ki == qi)
        def _():
            update(True)
            first = lax.broadcasted_iota(jnp.int32, (t, LANES), 1) < 64
            a0, a1 = acc_sc[0], acc_sc[1]
            l0, l1 = pltpu.roll(a0, 64, 1), pltpu.roll(a1, 64, 1)
            o_ref[...] = jnp.where(first, a0 / l0, a1 / l1)
            lse_ref[0] = jnp.where(first, m_sc[0] * ATT_SCALE + jnp.log(l0), m_sc[1] * ATT_SCALE + jnp.log(l1))

        if nx:
            @pl.when((pl.program_id(0) == 3) & (step == len(steps) - 1))
            def _():
                for cp in copies:
                    cp.wait()

    grid_spec = pltpu.PrefetchScalarGridSpec(
        num_scalar_prefetch=2, grid=(4, len(steps)),
        in_specs=[pl.BlockSpec((t, 256), lambda p, s, qt, kt: (qt[s], p)),
                  pl.BlockSpec((t, 256), lambda p, s, qt, kt: (kt[s], p)),
                  pl.BlockSpec((t, LANES), lambda p, s, qt, kt: (kt[s], p))] + [ANY] * nx,
        out_specs=[pl.BlockSpec((t, LANES), lambda p, s, qt, kt: (qt[s], p)),
                   pl.BlockSpec((1, t, LANES), lambda p, s, qt, kt: (p, qt[s], 0))] + [ANY] * nx,
        scratch_shapes=[pltpu.VMEM((2, t, LANES), F32), pltpu.VMEM((2, t, LANES), F32)]
        + (_exchange_sems(nx) if nx else []))
    res = pl.pallas_call(
        body, name="flash_fwd", grid_spec=grid_spec,
        out_shape=[jax.ShapeDtypeStruct((SEQ, 512), F32), jax.ShapeDtypeStruct((4, SEQ, LANES), F32)]
        + _exchange_shapes([], bcast),
        compiler_params=_params(("arbitrary", "arbitrary")),
    )(qi_tab, ki_tab, q, k, v, *bcast)
    return res[0], res[1], res[2:]


def _flash_bwd(q, k, v, o, do, lse, scatter=()):
    t = ATT_T
    nb = SEQ // t

    steps = [(qi, ki) for ki in range(nb) for qi in range(ki, nb)]
    qi_tab = jnp.asarray([s[0] for s in steps], jnp.int32)
    ki_tab = jnp.asarray([s[1] for s in steps], jnp.int32)
    log2e = math.log2(math.e)

    nx = len(scatter)

    def body(qi_ref, ki_ref, q_ref, k_ref, v_ref, o_ref, do_ref, lse_ref, *rest):
        x_refs, (dq_ref, dk_ref, dv_ref), g_refs = rest[:nx], rest[nx:nx + 3], rest[nx + 3:2 * nx + 3]
        step = pl.program_id(1)
        qi, ki = qi_ref[step], ki_ref[step]
        if nx:
            copies = _peer_copies(x_refs, g_refs, rest[2 * nx + 3:], nx)

            @pl.when((pl.program_id(0) == 0) & (step == 0))
            def _():
                for cp in copies:
                    cp.start()

        @pl.when(step == 0)
        def _():
            dq_ref[...] = jnp.zeros_like(dq_ref)

        @pl.when(qi == ki)
        def _():
            dk_ref[...] = jnp.zeros_like(dk_ref)
            dv_ref[...] = jnp.zeros_like(dv_ref)

        def update(diagonal):
            dov, ov, vv = do_ref[...], o_ref[...], v_ref[...]
            lse2 = lse_ref[0] * log2e
            lane = lax.broadcasted_iota(jnp.int32, (t, LANES), 1)
            prod = dov * ov
            qrows = pl.ds(pl.multiple_of(qi * t, t), t)
            dv_acc = jnp.zeros((t, LANES), F32)
            for hd in range(2):
                sl = slice(hd * LANES, (hd + 1) * LANES)
                mine = (lane >= hd * 64) & (lane < (hd + 1) * 64)
                qh, kh = q_ref[:, sl], k_ref[:, sl]
                p = jnp.exp2(_dot_nt(qh, kh) * ATT_C - lse2[:, hd * 64:hd * 64 + 1])
                if diagonal:
                    p = jnp.where(lax.broadcasted_iota(jnp.int32, (t, t), 1)
                                  <= lax.broadcasted_iota(jnp.int32, (t, t), 0), p, 0.0)
                do_h = jnp.where(mine, dov, 0.0)
                delta = jnp.sum(jnp.where(mine, prod, 0.0), axis=1, keepdims=True)
                dp = _dot_nt(_mx(do_h), vv)
                ds = _mx(p * (dp - delta) * ATT_SCALE)
                dv_acc = dv_acc + jnp.where(mine, _dot_tn(_mx(p), _mx(dov)), 0.0)
                dk_ref[:, sl] += _dot_tn(ds, qh)
                dq_ref[qrows, sl] += _dot(ds, kh)
            dv_ref[...] += dv_acc

        @pl.when(qi > ki)
        def _():
            update(False)

        @pl.when(qi == ki)
        def _():
            update(True)

        if nx:
            @pl.when((pl.program_id(0) == 3) & (step == len(steps) - 1))
            def _():
                for cp in copies:
                    cp.wait()

    qmap = lambda p, s, qt, kt: (qt[s], p)
    kmap = lambda p, s, qt, kt: (kt[s], p)
    grid_spec = pltpu.PrefetchScalarGridSpec(
        num_scalar_prefetch=2, grid=(4, len(steps)),
        in_specs=[pl.BlockSpec((t, 256), qmap), pl.BlockSpec((t, 256), kmap), pl.BlockSpec((t, LANES), kmap),
                  pl.BlockSpec((t, LANES), qmap), pl.BlockSpec((t, LANES), qmap),
                  pl.BlockSpec((1, t, LANES), lambda p, s, qt, kt: (p, qt[s], 0))] + [ANY] * nx,
        out_specs=[pl.BlockSpec((SEQ, 256), lambda p, s, qt, kt: (0, p)), pl.BlockSpec((t, 256), kmap),
                   pl.BlockSpec((t, LANES), kmap)] + [ANY] * nx,
        scratch_shapes=_exchange_sems(nx) if nx else [])
    res = pl.pallas_call(
        body, name="flash_bwd", grid_spec=grid_spec,
        out_shape=[jax.ShapeDtypeStruct((SEQ, 1024), F32), jax.ShapeDtypeStruct((SEQ, 1024), F32),
                   jax.ShapeDtypeStruct((SEQ, 512), F32)] + _exchange_shapes(scatter, []),
        compiler_params=_params(("arbitrary", "arbitrary")),
    )(qi_tab, ki_tab, q, k, v, o, do, lse, *scatter)
    return res[0], res[1], res[2], res[3:]


def _mla_bwd_rope(dq, dk, dv, tc, tsa, tsb):
    t = MLA_T

    def body(dq_ref, dk_ref, dv_ref, c_ref, sa_ref, sb_ref, oq_ref, okv_ref, okr_ref):
        c, sa, sb = c_ref[...], sa_ref[...], sb_ref[...]
        lane = lax.broadcasted_iota(jnp.int32, (t, LANES), 1)
        dkr = jnp.zeros((t, LANES), F32)
        for hd in range(MLA_HEADS):
            sl = slice(hd * LANES, (hd + 1) * LANES)
            oq_ref[:, sl] = _mx(_rope_t(dq_ref[:, sl], c, sa, sb))
            dkh = dk_ref[:, sl]
            okv_ref[:, sl] = _mx(dkh)
            dkr = dkr + dkh
        okv_ref[:, 1024:] = _mx(dv_ref[...])
        dkr = jnp.where((lane >= 64) & (lane < 96), dkr, 0.0)
        okr_ref[...] = _rope_t(dkr, c, sa, sb)

    tab = pl.BlockSpec((t, LANES), lambda i: (i, 0))
    wide = pl.BlockSpec((t, 1024), lambda i: (i, 0))
    return pl.pallas_call(
        body, name="mla_bwd_rope", grid=(SEQ // t,),
        in_specs=[wide, wide, pl.BlockSpec((t, 512), lambda i: (i, 0)), tab, tab, tab],
        out_specs=[wide, pl.BlockSpec((t, 1536), lambda i: (i, 0)), tab],
        out_shape=[jax.ShapeDtypeStruct((SEQ, 1024), MXU_DTYPE), jax.ShapeDtypeStruct((SEQ, 1536), MXU_DTYPE),
                   jax.ShapeDtypeStruct((SEQ, LANES), F32)],
        compiler_params=_params(("parallel",)),
    )(dq, dk, dv, tc, tsa, tsb)


def _rms_bwd(v, g, dy, eps=1e-6):
    rs = lax.rsqrt(jnp.mean(v * v, axis=-1, keepdims=True) + eps)
    xh = v * rs
    dxh = dy * g
    dv = rs * (dxh - xh * jnp.mean(dxh * xh, axis=-1, keepdims=True))
    return dv, jnp.sum(dy * xh, axis=0, keepdims=True)


def _mla_norm_bwd(proj0, dqn, dkn, dkr, q_norm, kv_norm):
    t = MLA_T

    def body(cq_ref, ck_ref, dqn_ref, dkn_ref, dkr_ref, qn_ref, kn_ref, o_ref, dgq_ref, dgk_ref):
        @pl.when(pl.program_id(0) == 0)
        def _():
            dgq_ref[...] = jnp.zeros_like(dgq_ref)
            dgk_ref[...] = jnp.zeros_like(dgk_ref)

        dcq, dgq = _rms_bwd(cq_ref[...], qn_ref[...], dqn_ref[...])
        dck, dgk = _rms_bwd(ck_ref[:, :LANES], kn_ref[...], dkn_ref[...])
        o_ref[:, :256] = _mx(dcq)
        o_ref[:, 256:384] = _mx(dck)
        o_ref[:, 384:] = _mx(dkr_ref[...])
        dgq_ref[0:1, :] += dgq
        dgk_ref[0:1, :] += dgk

    tab = pl.BlockSpec((t, LANES), lambda i: (i, 0))
    return pl.pallas_call(
        body, name="mla_norm_bwd", grid=(SEQ // t,),
        in_specs=[pl.BlockSpec((t, 256), lambda i: (i, 6)), pl.BlockSpec((t, 256), lambda i: (i, 7)),
                  pl.BlockSpec((t, 256), lambda i: (i, 0)), tab, tab,
                  pl.BlockSpec((1, 256), lambda i: (0, 0)), pl.BlockSpec((1, LANES), lambda i: (0, 0))],
        out_specs=[pl.BlockSpec((t, 512), lambda i: (i, 0)), pl.BlockSpec((SUBLANES, 256), lambda i: (0, 0)),
                   pl.BlockSpec((SUBLANES, LANES), lambda i: (0, 0))],
        out_shape=[jax.ShapeDtypeStruct((SEQ, 512), MXU_DTYPE), jax.ShapeDtypeStruct((SUBLANES, 256), F32),
                   jax.ShapeDtypeStruct((SUBLANES, LANES), F32)],
        compiler_params=_params(("arbitrary",)),
    )(proj0, proj0, dqn, dkn, dkr, q_norm, kv_norm)


LN_T = 512


def _ln(v, g, b, eps=1e-5):
    mu = jnp.mean(v, axis=-1, keepdims=True)
    xc = v - mu
    rs = lax.rsqrt(jnp.mean(xc * xc, axis=-1, keepdims=True) + eps)
    return xc * rs * g + b


def _ln_bwd(v, g, dy, eps=1e-5):
    mu = jnp.mean(v, axis=-1, keepdims=True)
    xc = v - mu
    rs = lax.rsqrt(jnp.mean(xc * xc, axis=-1, keepdims=True) + eps)
    xh = xc * rs
    dxh = dy * g
    dv = rs * (dxh - jnp.mean(dxh, axis=-1, keepdims=True) - xh * jnp.mean(dxh * xh, axis=-1, keepdims=True))
    return dv, jnp.sum(dy * xh, axis=0, keepdims=True), jnp.sum(dy, axis=0, keepdims=True)


def _l0_out(h, o, proj0, x, w_out, g, b):
    t = LN_T

    def body(h_ref, o_ref, ga_ref, gb_ref, x_ref, w_ref, g_ref, b_ref, y_ref, v_ref, x1_ref, x1b_ref):
        y = _mx(jnp.concatenate([h_ref[...] * _silu(ga_ref[...]), o_ref[...] * _silu(gb_ref[...])], axis=1))
        v = DN_ALPHA * x_ref[...] + _dot(y, w_ref[...])
        y_ref[...] = y
        v_ref[...] = v
        x1 = _ln(v, g_ref[...], b_ref[...])
        x1_ref[...] = x1
        x1b_ref[...] = _mx(x1)

    half = pl.BlockSpec((t, 512), lambda i: (i, 0))
    full = pl.BlockSpec((t, D_MODEL), lambda i: (i, 0))
    vec = pl.BlockSpec((1, D_MODEL), lambda i: (0, 0))
    return pl.pallas_call(
        body, name="l0_out", grid=(SEQ // t,),
        in_specs=[half, half, pl.BlockSpec((t, 512), lambda i: (i, 0)), pl.BlockSpec((t, 512), lambda i: (i, 1)), full,
                  pl.BlockSpec((D_MODEL, D_MODEL), lambda i: (0, 0)), vec, vec],
        out_specs=[full, full, full, full],
        out_shape=[jax.ShapeDtypeStruct((SEQ, D_MODEL), MXU_DTYPE), jax.ShapeDtypeStruct((SEQ, D_MODEL), F32),
                   jax.ShapeDtypeStruct((SEQ, D_MODEL), F32), jax.ShapeDtypeStruct((SEQ, D_MODEL), MXU_DTYPE)],
        compiler_params=_params(("parallel",)),
    )(h, o, proj0, proj0, x, w_out, g, b)


def _ln_bwd_call(v, dy, g):
    t = LN_T

    def body(v_ref, dy_ref, g_ref, dv_ref, dgb_ref):
        @pl.when(pl.program_id(0) == 0)
        def _():
            dgb_ref[...] = jnp.zeros_like(dgb_ref)

        dv, dg, db = _ln_bwd(v_ref[...], g_ref[...], dy_ref[...])
        dv_ref[...] = dv
        dgb_ref[0:1, :] += dg
        dgb_ref[1:2, :] += db

    full = pl.BlockSpec((t, D_MODEL), lambda i: (i, 0))
    return pl.pallas_call(
        body, name="ln_bwd", grid=(SEQ // t,),
        in_specs=[full, full, pl.BlockSpec((1, D_MODEL), lambda i: (0, 0))],
        out_specs=[full, pl.BlockSpec((SUBLANES, D_MODEL), lambda i: (0, 0))],
        out_shape=[jax.ShapeDtypeStruct((SEQ, D_MODEL), F32), jax.ShapeDtypeStruct((SUBLANES, D_MODEL), F32)],
        compiler_params=_params(("arbitrary",)),
    )(v, dy, g)


def _gate_bwd(dy, h, o, proj0):
    t = LN_T

    def body(dya_ref, dyb_ref, h_ref, o_ref, ga_ref, gb_ref, dh_ref, do_ref, dg_ref):
        ga, gb, dya, dyb = ga_ref[...], gb_ref[...], dya_ref[...], dyb_ref[...]
        dh_ref[...] = dya * _silu(ga)
        do_ref[...] = dyb * _silu(gb)
        dg_ref[:, :512] = _mx(dya * h_ref[...] * _dsilu(ga))
        dg_ref[:, 512:] = _mx(dyb * o_ref[...] * _dsilu(gb))

    half = pl.BlockSpec((t, 512), lambda i: (i, 0))
    half1 = pl.BlockSpec((t, 512), lambda i: (i, 1))
    full = pl.BlockSpec((t, 1024), lambda i: (i, 0))
    return pl.pallas_call(
        body, name="gate_bwd", grid=(SEQ // t,),
        in_specs=[half, half1, half, half, half, half1],
        out_specs=[half, half, full],
        out_shape=[jax.ShapeDtypeStruct((SEQ, 512), F32), jax.ShapeDtypeStruct((SEQ, 512), F32),
                   jax.ShapeDtypeStruct((SEQ, 1024), MXU_DTYPE)],
        compiler_params=_params(("parallel",)),
    )(dy, dy, h, o, proj0, proj0)


CONV_T = 512
CONV_CB = 1024


def _ssd_conv_fwd(xbc, cw8, cb):
    t, cbk = CONV_T, CONV_CB
    tb = t // SUBLANES

    def body(x_ref, halo_ref, cw_ref, cb_ref, pre_ref, act_ref):
        halo = jnp.where(pl.program_id(1) > 0, halo_ref[...], 0.0)
        pre = _conv4(x_ref[...], halo, cw_ref[...], cb_ref[...])
        pre_ref[...] = pre
        act_ref[...] = _silu(pre)

    blk = pl.BlockSpec((t, cbk), lambda j, i: (i, j))
    return pl.pallas_call(
        body, name="ssd_conv_fwd", grid=(SSD_CONV // cbk, SEQ // t),
        in_specs=[blk, pl.BlockSpec((SUBLANES, cbk), lambda j, i: (jnp.maximum(i * tb - 1, 0), j)),
                  pl.BlockSpec((SUBLANES, cbk), lambda j, i: (0, j)), pl.BlockSpec((1, cbk), lambda j, i: (0, j))],
        out_specs=[blk, blk],
        out_shape=[jax.ShapeDtypeStruct((SEQ, SSD_CONV), F32), jax.ShapeDtypeStruct((SEQ, SSD_CONV), F32)],
        compiler_params=_params(("parallel", "parallel")),
    )(xbc, xbc, cw8, cb)


def _ssd_conv_bwd(dact, pre, xbc, cw8):
    t, cbk = CONV_T, CONV_CB
    tb = t // SUBLANES
    nb = SEQ // t

    def body(da_ref, dan_ref, pre_ref, pren_ref, x_ref, cw_ref, dx_ref, dcw_ref):
        i = pl.program_id(1)

        @pl.when(i == 0)
        def _():
            dcw_ref[...] = jnp.zeros_like(dcw_ref)

        dpre = da_ref[...] * _dsilu(pre_ref[...])
        dpre_next = jnp.where(i < nb - 1, dan_ref[...] * _dsilu(pren_ref[...]), 0.0)
        xblk = x_ref[...]
        cw = cw_ref[...]
        dx = dpre * cw[3:4]
        dcw_ref[3:4, :] += jnp.sum(dpre * xblk, axis=0, keepdims=True)
        for k in range(3):
            up = _shift_up(dpre, dpre_next, 3 - k)
            dcw_ref[k:k + 1, :] += jnp.sum(up * xblk, axis=0, keepdims=True)
            dx = dx + up * cw[k:k + 1]
        dcw_ref[4:5, :] += jnp.sum(dpre, axis=0, keepdims=True)
        dx_ref[...] = _mx(dx)

    blk = pl.BlockSpec((t, cbk), lambda j, i: (i, j))
    nxt = pl.BlockSpec((SUBLANES, cbk), lambda j, i: (jnp.minimum((i + 1) * tb, SEQ // SUBLANES - 1), j))
    acc = pl.BlockSpec((SUBLANES, cbk), lambda j, i: (0, j))
    return pl.pallas_call(
        body, name="ssd_conv_bwd", grid=(SSD_CONV // cbk, nb),
        in_specs=[blk, nxt, blk, nxt, blk, acc],
        out_specs=[blk, acc],
        out_shape=[jax.ShapeDtypeStruct((SEQ, SSD_CONV), MXU_DTYPE), jax.ShapeDtypeStruct((SUBLANES, SSD_CONV), F32)],
        compiler_params=_params(("parallel", "arbitrary")),
    )(dact, dact, pre, pre, xbc, cw8)


def _ssd_common(dt_raw, bias, alog, tril, xs):
    lane = lax.broadcasted_iota(jnp.int32, dt_raw.shape, 1)
    dt = jnp.where(lane < SSD_HEADS, _softplus(dt_raw + bias), 0.0)
    a_neg = -jnp.exp(alog)
    cs = _dot_hi(tril, dt * a_neg)
    dt_x = _expand_heads(dt)
    ecs_x = _expand_heads(jnp.exp(cs))
    ds_x = _expand_heads(jnp.exp(cs[SSD_L - 1:SSD_L, :] - cs))
    return dt, a_neg, cs, dt_x, None, xs * dt_x, ds_x, ecs_x, ecs_x[SSD_L - 1:SSD_L, :]


def _expand_heads(v):
    lane = lax.broadcasted_iota(jnp.int32, v.shape, 1)
    tiles = [jnp.where(lane < SSD_P, v[:, 2 * pr:2 * pr + 1], v[:, 2 * pr + 1:2 * pr + 2])
             for pr in range(SSD_HEADS // 2)]
    return jnp.concatenate(tiles, axis=1)


def _fold_heads(v, expand_t):
    hi = v.astype(jnp.bfloat16)
    lo = (v - hi.astype(F32)).astype(jnp.bfloat16)
    return _dot(hi, expand_t) + _dot(lo, expand_t)


def _ssd_decay(cs, cs_t, hh, causal):
    seg = cs[:, hh:hh + 1] - cs_t[hh:hh + 1, :]
    return jnp.where(causal, jnp.exp(jnp.where(causal, seg, 0.0)), 0.0)


def _ssd_scan_fwd(act, dt_raw, bias, alog, d_x, tril):
    nc = SEQ // SSD_L
    gw = SSD_INNER // SSD_GROUPS

    def body(act_ref, dt_ref, bias_ref, alog_ref, dx_ref, tril_ref, y_ref, hp_ref, h_sc):
        @pl.when(pl.program_id(0) == 0)
        def _():
            h_sc[...] = jnp.zeros_like(h_sc)

        xs = act_ref[:, :SSD_INNER]
        _, _, cs, _, _, xdt, ds_x, ecs_x, elast = _ssd_common(
            dt_ref[...], bias_ref[...], alog_ref[...], tril_ref[...], xs)
        cs_t = cs.T
        causal = (lax.broadcasted_iota(jnp.int32, (SSD_L, SSD_L), 0)
                  >= lax.broadcasted_iota(jnp.int32, (SSD_L, SSD_L), 1))
        lane = lax.broadcasted_iota(jnp.int32, (SSD_L, LANES), 1)
        xdt_b = _mx(xdt)
        xds_b = _mx(xdt * ds_x)
        hp_ref[0] = h_sc[...]
        for g in range(SSD_GROUPS):
            gs = slice(g * gw, (g + 1) * gw)
            bg = _mx(act_ref[:, SSD_INNER + g * SSD_N:SSD_INNER + (g + 1) * SSD_N])
            cg = _mx(act_ref[:, SSD_INNER + 512 + g * SSD_N:SSD_INNER + 512 + (g + 1) * SSD_N])
            cb = _dot_nt(cg, bg)
            hprev = h_sc[:, gs]
            yoff = _dot(cg, _mx(hprev)) * ecs_x[:, gs]
            h_sc[:, gs] = hprev * elast[:, gs] + _dot_tn(bg, xds_b[:, gs])
            for pr in range(4):
                ps = slice(g * gw + pr * LANES, g * gw + (pr + 1) * LANES)
                xp = xdt_b[:, ps]
                ydiag = jnp.zeros((SSD_L, LANES), F32)
                for j in range(2):
                    dm = _ssd_decay(cs, cs_t, g * 8 + pr * 2 + j, causal)
                    mine = (lane >= j * 64) & (lane < (j + 1) * 64)
                    ydiag = ydiag + _dot(_mx(cb * dm), jnp.where(mine, xp, jnp.zeros_like(xp)))
                y_ref[:, ps] = ydiag + yoff[:, pr * LANES:(pr + 1) * LANES] + dx_ref[:, ps] * xs[:, ps]

    const = lambda shape: pl.BlockSpec(shape, lambda c: (0, 0))
    return pl.pallas_call(
        body, name="ssd_scan_fwd", grid=(nc,),
        in_specs=[pl.BlockSpec((SSD_L, SSD_CONV), lambda c: (c, 0)), pl.BlockSpec((SSD_L, LANES), lambda c: (c, 0)),
                  const((1, LANES)), const((1, LANES)), const((1, SSD_INNER)), const((SSD_L, SSD_L))],
        out_specs=[pl.BlockSpec((SSD_L, SSD_INNER), lambda c: (c, 0)),
                   pl.BlockSpec((1, SSD_N, SSD_INNER), lambda c: (c, 0, 0))],
        out_shape=[jax.ShapeDtypeStruct((SEQ, SSD_INNER), F32), jax.ShapeDtypeStruct((nc, SSD_N, SSD_INNER), F32)],
        scratch_shapes=[pltpu.VMEM((SSD_N, SSD_INNER), F32)],
        compiler_params=_params(("arbitrary",)),
    )(act, dt_raw, bias, alog, d_x, tril)


def _ssd_scan_bwd(dy, act, dt_raw, hprev_all, bias, alog, d_x, tril, expand_t):
    nc = SEQ // SSD_L
    gw = SSD_INNER // SSD_GROUPS

    def body(dy_ref, act_ref, dt_ref, hp_ref, bias_ref, alog_ref, dx_ref, tril_ref, et_ref,
             dact_ref, ddt_ref, dvec_ref, dh_sc, dd_sc):
        i = pl.program_id(0)

        @pl.when(i == 0)
        def _():
            dh_sc[...] = jnp.zeros_like(dh_sc)
            dd_sc[...] = jnp.zeros_like(dd_sc)
            dvec_ref[...] = jnp.zeros_like(dvec_ref)

        xs = act_ref[:, :SSD_INNER]
        dt_raw_v, bias_v = dt_ref[...], bias_ref[...]
        dt, a_neg, cs, dt_x, _, xdt, ds_x, ecs_x, elast = _ssd_common(
            dt_raw_v, bias_v, alog_ref[...], tril_ref[...], xs)
        cs_t = cs.T
        rowi = lax.broadcasted_iota(jnp.int32, (SSD_L, SSD_L), 0)
        coli = lax.broadcasted_iota(jnp.int32, (SSD_L, SSD_L), 1)
        causal = rowi >= coli
        lane = lax.broadcasted_iota(jnp.int32, (SSD_L, LANES), 1)
        row_g = lax.broadcasted_iota(jnp.int32, (SSD_L, gw), 0)
        dyv = dy_ref[...]
        dd_sc[0:1, :] += jnp.sum(dyv * xs, axis=0, keepdims=True)
        xdt_b = _mx(xdt)
        xds = xdt * ds_x
        xds_b = _mx(xds)
        dy_b = _mx(dyv)
        dye_b = _mx(dyv * ecs_x)
        dcs = jnp.zeros((SSD_L, LANES), F32)
        dcs_t = jnp.zeros((LANES, SSD_L), F32)
        dcs_parts = []
        dxdt_parts = []
        for g in range(SSD_GROUPS):
            gs = slice(g * gw, (g + 1) * gw)
            bcol = slice(SSD_INNER + g * SSD_N, SSD_INNER + (g + 1) * SSD_N)
            ccol = slice(SSD_INNER + 512 + g * SSD_N, SSD_INNER + 512 + (g + 1) * SSD_N)
            bg, cg = _mx(act_ref[:, bcol]), _mx(act_ref[:, ccol])
            cb = _dot_nt(cg, bg)
            hp = hp_ref[0, :, gs]
            hp_b = _mx(hp)
            dh = dh_sc[:, gs]
            dh_b = _mx(dh)
            yoff = _dot(cg, hp_b) * ecs_x[:, gs]
            bdh = _dot(bg, dh_b)
            tt = xds[:, gs] * bdh
            last_row = (jnp.sum(tt, axis=0, keepdims=True)
                        + jnp.sum(dh * hp, axis=0, keepdims=True) * elast[:, gs])
            dcs_parts.append(dyv[:, gs] * yoff - tt + jnp.where(row_g == SSD_L - 1, last_row, 0.0))
            dc_g = _dot_nt(dye_b[:, gs], hp_b)
            db_g = _dot_nt(xds_b[:, gs], dh_b)
            dh_sc[:, gs] = _dot_tn(cg, dye_b[:, gs]) + dh * elast[:, gs]
            wsum = jnp.zeros((SSD_L, SSD_L), F32)
            dxdt_g = []
            for pr in range(4):
                ps = slice(g * gw + pr * LANES, g * gw + (pr + 1) * LANES)
                xp, dyp = xdt_b[:, ps], dy_b[:, ps]
                dxp = jnp.zeros((SSD_L, LANES), F32)
                for j in range(2):
                    hh = g * 8 + pr * 2 + j
                    dm = _ssd_decay(cs, cs_t, hh, causal)
                    mine = (lane >= j * 64) & (lane < (j + 1) * 64)
                    dy_h = jnp.where(mine, dyp, jnp.zeros_like(dyp))
                    wd = _dot_nt(dy_h, xp) * dm
                    wsum = wsum + wd
                    gmat = wd * cb
                    dcs = dcs + jnp.where(lane == hh, jnp.sum(gmat, axis=1, keepdims=True), 0.0)
                    dcs_t = dcs_t - jnp.where(rowi == hh, jnp.sum(gmat, axis=0, keepdims=True), 0.0)
                    dxp = dxp + _dot_tn(_mx(cb * dm), dy_h)
                dxdt_g.append(dxp)
            dxdt_parts.append(jnp.concatenate(dxdt_g, axis=1) + bdh * ds_x[:, gs])
            ws_b = _mx(wsum)
            dact_ref[:, ccol] = dc_g + _dot(ws_b, bg)
            dact_ref[:, bcol] = db_g + _dot_tn(ws_b, cg)
        dxdt = jnp.concatenate(dxdt_parts, axis=1)
        dcs_x = jnp.concatenate(dcs_parts, axis=1)
        et = et_ref[...]
        dcs_tot = dcs + dcs_t.T + _fold_heads(dcs_x, et)
        da_dt = _dot_hi((coli >= rowi).astype(F32), dcs_tot)
        ddt = da_dt * a_neg + _fold_heads(dxdt * xs, et)
        ddt_raw = ddt * _sigmoid(dt_raw_v + bias_v)
        ddt_ref[...] = ddt_raw
        dvec_ref[0:1, :] += jnp.sum(ddt_raw, axis=0, keepdims=True)
        dvec_ref[1:2, :] += jnp.sum(da_dt * dt, axis=0, keepdims=True) * a_neg
        dact_ref[:, :SSD_INNER] = dyv * dx_ref[...] + dxdt * dt_x

        @pl.when(i == nc - 1)
        def _():
            dvec_ref[2:3, :] = _fold_heads(dd_sc[...], et)[0:1, :]

    const = lambda shape: pl.BlockSpec(shape, lambda c: (0, 0))
    rev = lambda c: (nc - 1 - c, 0)
    return pl.pallas_call(
        body, name="ssd_scan_bwd", grid=(nc,),
        in_specs=[pl.BlockSpec((SSD_L, SSD_INNER), rev), pl.BlockSpec((SSD_L, SSD_CONV), rev),
                  pl.BlockSpec((SSD_L, LANES), rev),
                  pl.BlockSpec((1, SSD_N, SSD_INNER), lambda c: (nc - 1 - c, 0, 0)),
                  const((1, LANES)), const((1, LANES)), const((1, SSD_INNER)), const((SSD_L, SSD_L)),
                  const((SSD_INNER, LANES))],
        out_specs=[pl.BlockSpec((SSD_L, SSD_CONV), rev), pl.BlockSpec((SSD_L, LANES), rev), const((SUBLANES, LANES))],
        out_shape=[jax.ShapeDtypeStruct((SEQ, SSD_CONV), F32), jax.ShapeDtypeStruct((SEQ, LANES), F32),
                   jax.ShapeDtypeStruct((SUBLANES, LANES), F32)],
        scratch_shapes=[pltpu.VMEM((SSD_N, SSD_INNER), F32), pltpu.VMEM((SUBLANES, SSD_INNER), F32)],
        compiler_params=_params(("arbitrary",)),
    )(dy, act, dt_raw, hprev_all, bias, alog, d_x, tril, expand_t)


L1_T = 256


def _gated_norm(y, z, nw):
    y2 = y * _silu(z)
    gw = SSD_INNER // SSD_GROUPS
    outs, xhs, rss = [], [], []
    for g in range(SSD_GROUPS):
        gs = slice(g * gw, (g + 1) * gw)
        v = y2[:, gs]
        rs = lax.rsqrt(jnp.mean(v * v, axis=-1, keepdims=True) + 1e-6)
        xhs.append(v * rs)
        rss.append(rs)
        outs.append(v * rs * nw[:, gs])
    return outs, xhs, rss


def _l1_out(y, z, nw, w_out, x1, g, b, target):
    t = L1_T

    def body(y_ref, z_ref, nw_ref, w_ref, x1_ref, g_ref, b_ref, tg_ref, yn_ref, dv_ref, dgb_ref, loss_ref):
        @pl.when(pl.program_id(0) == 0)
        def _():
            dgb_ref[...] = jnp.zeros_like(dgb_ref)
            loss_ref[...] = jnp.zeros_like(loss_ref)

        outs, _, _ = _gated_norm(y_ref[...], z_ref[...], nw_ref[...])
        yn = _mx(jnp.concatenate(outs, axis=1))
        yn_ref[...] = yn
        v = DN_ALPHA * x1_ref[...] + _dot(yn, w_ref[...])
        gv = g_ref[...]
        err = _ln(v, gv, b_ref[...]) - tg_ref[...]
        rowsum = jnp.sum(err * err, axis=1, keepdims=True)
        loss_ref[...] += 0.5 * jnp.sum(rowsum, axis=0, keepdims=True) / D_MODEL
        dv, dg, db = _ln_bwd(v, gv, err / D_MODEL)
        dv_ref[...] = dv
        dgb_ref[0:1, :] += dg
        dgb_ref[1:2, :] += db

    wide = pl.BlockSpec((t, SSD_INNER), lambda i: (i, 0))
    full = pl.BlockSpec((t, D_MODEL), lambda i: (i, 0))
    vec = pl.BlockSpec((1, D_MODEL), lambda i: (0, 0))
    return pl.pallas_call(
        body, name="l1_out", grid=(SEQ // t,),
        in_specs=[wide, wide, pl.BlockSpec((1, SSD_INNER), lambda i: (0, 0)),
                  pl.BlockSpec((SSD_INNER, D_MODEL), lambda i: (0, 0)), full, vec, vec, full],
        out_specs=[wide, full, pl.BlockSpec((SUBLANES, D_MODEL), lambda i: (0, 0)),
                   pl.BlockSpec((SUBLANES, LANES), lambda i: (0, 0))],
        out_shape=[jax.ShapeDtypeStruct((SEQ, SSD_INNER), MXU_DTYPE), jax.ShapeDtypeStruct((SEQ, D_MODEL), F32),
                   jax.ShapeDtypeStruct((SUBLANES, D_MODEL), F32), jax.ShapeDtypeStruct((SUBLANES, LANES), F32)],
        compiler_params=_params(("arbitrary",)),
    )(y, z, nw, w_out, x1, g, b, target)


def _l1_gate_bwd(dyn, y, z, nw):
    t = L1_T
    gw = SSD_INNER // SSD_GROUPS

    def body(dyn_ref, y_ref, z_ref, nw_ref, dy_ref, dz_ref, dnw_ref):
        @pl.when(pl.program_id(0) == 0)
        def _():
            dnw_ref[...] = jnp.zeros_like(dnw_ref)

        yv, zv, nwv = y_ref[...], z_ref[...], nw_ref[...]
        _, xhs, rss = _gated_norm(yv, zv, nwv)
        sz, dsz = _silu(zv), _dsilu(zv)
        for g in range(SSD_GROUPS):
            gs = slice(g * gw, (g + 1) * gw)
            d_out = dyn_ref[:, gs]
            xh = xhs[g]
            dnw_ref[0:1, gs] += jnp.sum(d_out * xh, axis=0, keepdims=True)
            dxh = d_out * nwv[:, gs]
            dy2 = rss[g] * (dxh - xh * jnp.mean(dxh * xh, axis=-1, keepdims=True))
            dy_ref[:, gs] = dy2 * sz[:, gs]
            dz_ref[:, gs] = _mx(dy2 * yv[:, gs] * dsz[:, gs])

    wide = pl.BlockSpec((t, SSD_INNER), lambda i: (i, 0))
    return pl.pallas_call(
        body, name="l1_gate_bwd", grid=(SEQ // t,),
        in_specs=[wide, wide, wide, pl.BlockSpec((1, SSD_INNER), lambda i: (0, 0))],
        out_specs=[wide, wide, pl.BlockSpec((SUBLANES, SSD_INNER), lambda i: (0, 0))],
        out_shape=[jax.ShapeDtypeStruct((SEQ, SSD_INNER), F32), jax.ShapeDtypeStruct((SEQ, SSD_INNER), MXU_DTYPE),
                   jax.ShapeDtypeStruct((SUBLANES, SSD_INNER), F32)],
        compiler_params=_params(("arbitrary",)),
    )(dyn, y, z, nw)


MESH = pl.DeviceIdType.MESH
ANY = pl.BlockSpec(memory_space=pl.ANY)


def _flip(v, bit):
    return 1 - v if bit else v


def _all_gather(blocks, name):
    n = len(blocks)

    def body(*refs):
        x_refs, out_refs = refs[:n], refs[n:2 * n]
        send_sems, recv_sems, local_sems = refs[2 * n:]
        mx, my, mc = lax.axis_index("x"), lax.axis_index("y"), lax.axis_index("c")
        me, sibling = (mx, my, mc), (mx, my, 1 - mc)
        chips = [(1 - mx, my), (mx, 1 - my), (1 - mx, 1 - my)]

        def copy(a, k, block, to, own=False):
            px, py, pc = block
            slot = out_refs[a].at[4 * px + 2 * py + pc]
            return pltpu.make_async_remote_copy(
                src_ref=x_refs[a] if own else slot, dst_ref=slot,
                send_sem=send_sems.at[7 * a + k], recv_sem=recv_sems.at[7 * a + k], device_id=to, device_id_type=MESH)

        mine = [pltpu.make_async_copy(x_refs[a], out_refs[a].at[4 * mx + 2 * my + mc], local_sems.at[a])
                for a in range(n)]
        first = []
        for a in range(n):
            mine[a].start()
            first.append(copy(a, 0, me, sibling, own=True))
            first += [copy(a, 1 + j, me, (*chip, mc), own=True) for j, chip in enumerate(chips)]
        for cp in first:
            cp.start()
        passed = []
        for j, chip in enumerate(chips):
            for a in range(n):
                copy(a, 1 + j, (*chip, mc), me).wait_recv()
                fwd = copy(a, 4 + j, (*chip, mc), sibling)
                fwd.start()
                passed.append(fwd)
        for a in range(n):
            copy(a, 0, sibling, me).wait_recv()
            for j, chip in enumerate(chips):
                copy(a, 4 + j, (*chip, 1 - mc), me).wait_recv()
        for cp in first + passed:
            cp.wait_send()
        for cp in mine:
            cp.wait()

    return pl.pallas_call(
        body, name=name, in_specs=[ANY] * n, out_specs=[ANY] * n,
        out_shape=[jax.ShapeDtypeStruct((N_DEV,) + b.shape, b.dtype) for b in blocks],
        scratch_shapes=[pltpu.SemaphoreType.DMA((7 * n,)), pltpu.SemaphoreType.DMA((7 * n,)),
                        pltpu.SemaphoreType.DMA((n,))],
    )(*blocks)


def _exchange(scatter, bcast, name):
    n = len(scatter) + len(bcast)

    def body(*refs):
        copies = _peer_copies(refs[:n], refs[n:2 * n], refs[2 * n:], len(scatter))
        for cp in copies:
            cp.start()
        for cp in copies:
            cp.wait()

    return pl.pallas_call(
        body, name=name, in_specs=[ANY] * n, out_specs=[ANY] * n,
        out_shape=_exchange_shapes(scatter, bcast), scratch_shapes=_exchange_sems(n),
    )(*scatter, *bcast)


def _exchange_shapes(scatter, bcast):
    return ([jax.ShapeDtypeStruct(a.shape, a.dtype) for a in scatter]
            + [jax.ShapeDtypeStruct((N_DEV,) + a.shape, a.dtype) for a in bcast])


def _exchange_sems(n):
    return [pltpu.SemaphoreType.DMA((7 * n,)), pltpu.SemaphoreType.DMA((7 * n,)), pltpu.SemaphoreType.DMA((n,))]


def _peer_copies(in_refs, out_refs, sems, n_scatter):
    send_sems, recv_sems, local_sems = sems
    n = len(in_refs)
    mx, my, mc = lax.axis_index("x"), lax.axis_index("y"), lax.axis_index("c")
    me = 4 * mx + 2 * my + mc

    def src(a, slot):
        return in_refs[a].at[slot] if a < n_scatter else in_refs[a]

    copies = [pltpu.make_async_copy(src(a, me), out_refs[a].at[me], local_sems.at[a]) for a in range(n)]
    for k in range(1, N_DEV):
        px, py, pc = _flip(mx, (k >> 2) & 1), _flip(my, (k >> 1) & 1), _flip(mc, k & 1)
        for a in range(n):
            copies.append(pltpu.make_async_remote_copy(
                src_ref=src(a, 4 * px + 2 * py + pc), dst_ref=out_refs[a].at[me],
                send_sem=send_sems.at[7 * a + k - 1], recv_sem=recv_sems.at[7 * a + k - 1],
                device_id=(px, py, pc), device_id_type=MESH))
    return copies


def _segments(col_map, width):
    segs = []
    for lo, hi, arr, alo in col_map:
        for s in range(N_DEV):
            a, b = max(lo, s * width), min(hi, (s + 1) * width)
            if a < b:
                segs.append((s, a - s * width, b - a, arr, alo + a - lo))
    return segs


COPY_ROWS = 256


def _unshard(g8, col_map, widths, name):
    _, r, w = g8.shape
    rb = min(r, COPY_ROWS)
    segs = _segments(col_map, w)

    def body(g_ref, *o_refs):
        for o_ref in o_refs:
            o_ref[...] = jnp.zeros_like(o_ref)
        for s, llo, n, arr, alo in segs:
            o_refs[arr][:, alo:alo + n] = g_ref[s, :, llo:llo + n]

    return pl.pallas_call(
        body, name=name, grid=(r // rb,),
        in_specs=[pl.BlockSpec((N_DEV, rb, w), lambda i: (0, i, 0))],
        out_specs=[pl.BlockSpec((rb, n), lambda i: (i, 0)) for n in widths],
        out_shape=[jax.ShapeDtypeStruct((r, n), g8.dtype) for n in widths],
        compiler_params=_params(("parallel",)),
    )(g8)


def _reshard(srcs, col_map, w, dtype, name):
    r = srcs[0].shape[0]
    rb = min(r, COPY_ROWS)
    segs = _segments(col_map, w)

    def body(*refs):
        o_ref = refs[-1]
        for s, llo, n, arr, alo in segs:
            o_ref[s, :, llo:llo + n] = refs[arr][:, alo:alo + n].astype(dtype)

    return pl.pallas_call(
        body, name=name, grid=(r // rb,),
        in_specs=[pl.BlockSpec((rb, a.shape[1]), lambda i: (i, 0)) for a in srcs],
        out_specs=pl.BlockSpec((N_DEV, rb, w), lambda i: (0, i, 0)),
        out_shape=jax.ShapeDtypeStruct((N_DEV, r, w), dtype),
        compiler_params=_params(("parallel",)),
    )(*srcs)


def _adamw(parts, w, m, v, name):
    r, c = w.shape
    tr = COPY_ROWS if r % COPY_ROWS == 0 else r

    def body(p_ref, w_ref, m_ref, v_ref, g_ref, d_ref, mo_ref, vo_ref):
        g = p_ref[0].astype(F32)
        for s in range(1, N_DEV):
            g = g + p_ref[s].astype(F32)
        g_ref[...] = g
        d_ref[...], mo_ref[...], vo_ref[...] = _adamw_math(g, w_ref[...], m_ref[...], v_ref[...])

    blk = pl.BlockSpec((tr, c), lambda i: (i, 0))
    out = jax.ShapeDtypeStruct((r, c), F32)
    return pl.pallas_call(
        body, name=name, grid=(r // tr,),
        in_specs=[pl.BlockSpec((N_DEV, tr, c), lambda i: (0, i, 0)), blk, blk, blk],
        out_specs=[blk, blk, blk, blk], out_shape=[out, out, out, out],
        compiler_params=_params(("parallel",)),
    )(parts, w, m, v)


def _adamw_math(g, w, m, v):
    mn = ADAM_B1 * m + (1.0 - ADAM_B1) * g
    vn = ADAM_B2 * v + (1.0 - ADAM_B2) * (g * g)
    m_hat = mn / (1.0 - ADAM_B1 ** ADAM_STEP)
    v_hat = vn / (1.0 - ADAM_B2 ** ADAM_STEP)
    return -ADAM_LR * (m_hat / (jnp.sqrt(v_hat) + ADAM_EPS) + ADAM_WD * w), mn, vn


SMALL = (("ab_conv_w", 0, 4, 64), ("ssd_conv_w", 4, 4, 384), ("ssd_conv_b", 8, 1, 384), ("ssd_norm", 9, 1, 256),
         ("ssd_ln_g", 10, 1, 128), ("ssd_ln_b", 11, 1, 128))
VECS = (("ab_conv_b", 512), ("ab_gate_a_b", 512), ("ab_gate_x_b", 512), ("ab_lambda", 512), ("mla_q_norm", 256),
        ("mla_kv_norm", 128), ("ab_ln_g", 1024), ("ab_ln_b", 1024), ("ssd_dt_bias", 32), ("ssd_a_log", 32),
        ("ssd_d", 32))
GATES = ("ab_gate_a_w", "ab_gate_x_w")
SMALL_NAMES = tuple(n for n, *_ in SMALL) + tuple(n for n, _ in VECS) + GATES
VMEM_WHOLE = pl.BlockSpec(memory_space=pltpu.VMEM)


def _view2d(name, a):
    if name in GATES:
        return a.reshape(RNN_W, 64)
    return a[0] if a.ndim == 3 else a


def _unshard_small(g):
    widths = (512, 3072, 3072, 2048, 1024, 1024)

    def body(*refs):
        ins, outs = refs[:6], refs[6:]
        outs[0][...] = jnp.zeros_like(outs[0])
        outs[1][...] = jnp.zeros_like(outs[1])
        for (_, _, nr, c), i_ref, o_ref in zip(SMALL, ins, outs):
            for j in range(N_DEV):
                o_ref[0:nr, j * c:(j + 1) * c] = i_ref[j]

    return pl.pallas_call(
        body, name="unshard_small", in_specs=[VMEM_WHOLE] * 6, out_specs=[VMEM_WHOLE] * 6,
        out_shape=[jax.ShapeDtypeStruct((SUBLANES if nr == 4 else 1, w), F32) for (_, _, nr, _), w in zip(SMALL, widths)],
    )(*g)


def _prep_repl(ga, gx, dt_bias, a_log, d):
    def body(ga_ref, gx_ref, b_ref, al_ref, d_ref, wa_ref, wx_ref, b128_ref, al128_ref, dx_ref):
        wa_ref[...] = jnp.zeros_like(wa_ref)
        wx_ref[...] = jnp.zeros_like(wx_ref)
        for hd in range(8):
            hs = slice(hd * 64, (hd + 1) * 64)
            wa_ref[hs, hs] = _mx(ga_ref[hs, :])
            wx_ref[hs, hs] = _mx(gx_ref[hs, :])
        b128_ref[...] = jnp.zeros_like(b128_ref)
        al128_ref[...] = jnp.zeros_like(al128_ref)
        b128_ref[:, 0:SSD_HEADS] = b_ref[...]
        al128_ref[:, 0:SSD_HEADS] = al_ref[...]
        dv = d_ref[...]
        for hd in range(SSD_HEADS):
            dx_ref[:, hd * SSD_P:(hd + 1) * SSD_P] = jnp.broadcast_to(dv[:, hd:hd + 1], (1, SSD_P))

    return pl.pallas_call(
        body, name="prep_repl", in_specs=[VMEM_WHOLE] * 5, out_specs=[VMEM_WHOLE] * 5,
        out_shape=[jax.ShapeDtypeStruct((RNN_W, RNN_W), MXU_DTYPE), jax.ShapeDtypeStruct((RNN_W, RNN_W), MXU_DTYPE),
                   jax.ShapeDtypeStruct((1, LANES), F32), jax.ShapeDtypeStruct((1, LANES), F32),
                   jax.ShapeDtypeStruct((1, SSD_INNER), F32)],
    )(ga, gx, dt_bias, a_log, d)


def _pack_small(dvec0, g_wa, g_wx, dqnw, dknw, dgb0, dvec1, dcw1, dnw, dgb1):
    def body(dvec0_ref, gwa_ref, gwx_ref, dqn_ref, dkn_ref, dgb0_ref, dvec1_ref, dcw1_ref, dnw_ref, dgb1_ref,
             sm_ref, vec_ref, ga_ref, gx_ref):
        sm_ref[...] = jnp.zeros_like(sm_ref)
        vec_ref[...] = jnp.zeros_like(vec_ref)
        sharded = ((dvec0_ref, 4), (dcw1_ref, 0), (dcw1_ref, 4), (dnw_ref, 0), (dgb1_ref, 0), (dgb1_ref, 1))
        for (_, r0, nr, c), (src, sr) in zip(SMALL, sharded):
            for j in range(N_DEV):
                sm_ref[j, r0:r0 + nr, 0:c] = src[sr:sr + nr, j * c:(j + 1) * c]
        vectors = ((dvec0_ref, 3), (dvec0_ref, 0), (dvec0_ref, 1), (dvec0_ref, 2), (dqn_ref, 0), (dkn_ref, 0),
                   (dgb0_ref, 0), (dgb0_ref, 1), (dvec1_ref, 0), (dvec1_ref, 1), (dvec1_ref, 2))
        for row, ((_, c), (src, sr)) in enumerate(zip(VECS, vectors)):
            vec_ref[row:row + 1, 0:c] = src[sr:sr + 1, 0:c]
        for hd in range(8):
            hs = slice(hd * 64, (hd + 1) * 64)
            ga_ref[hs, :] = gwa_ref[hs, hs]
            gx_ref[hs, :] = gwx_ref[hs, hs]

    return pl.pallas_call(
        body, name="pack_small", in_specs=[VMEM_WHOLE] * 10, out_specs=[VMEM_WHOLE] * 4,
        out_shape=[jax.ShapeDtypeStruct((N_DEV, 16, 384), F32), jax.ShapeDtypeStruct((16, 1024), F32),
                   jax.ShapeDtypeStruct((RNN_W, 64), F32), jax.ShapeDtypeStruct((RNN_W, 64), F32)],
    )(dvec0, g_wa, g_wx, dqnw, dknw, dgb0, dvec1, dcw1, dnw, dgb1)


def _adamw_small(recv_sm, recv_vec, recv_ga, recv_gx, wmv):
    plan = ([(0, r0, nr, c) for _, r0, nr, c in SMALL] + [(1, row, 1, c) for row, (_, c) in enumerate(VECS)]
            + [(2, 0, RNN_W, 64), (3, 0, RNN_W, 64)])
    n = len(plan)

    def body(*refs):
        recv, ins, outs = refs[:4], refs[4:4 + 3 * n], refs[4 + 3 * n:]
        for i, (src, r0, nr, c) in enumerate(plan):
            g = recv[src][0, r0:r0 + nr, 0:c]
            for s in range(1, N_DEV):
                g = g + recv[src][s, r0:r0 + nr, 0:c]
            w_ref, m_ref, v_ref = ins[3 * i:3 * i + 3]
            outs[4 * i][...] = g
            outs[4 * i + 1][...], outs[4 * i + 2][...], outs[4 * i + 3][...] = _adamw_math(
                g, w_ref[...], m_ref[...], v_ref[...])

    flat = [a for t in wmv for a in t]
    return pl.pallas_call(
        body, name="adamw_small", in_specs=[VMEM_WHOLE] * (4 + 3 * n), out_specs=[VMEM_WHOLE] * (4 * n),
        out_shape=[jax.ShapeDtypeStruct(t[0].shape, F32) for t in wmv for _ in range(4)],
    )(recv_sm, recv_vec, recv_ga, recv_gx, *flat)


BIG_L0 = ("ab_w_in", "ab_w_out", "mla_w_uq", "mla_w_ukv")
BIG_L1 = ("ssd_w_in", "ssd_w_out")

MAP_W0 = ((0, 512, 0, 1024), (512, 1536, 0, 0), (1536, 1920, 0, 1536), (1920, 1952, 0, 1984))
MAP_W1 = ((0, 2048, 0, 0), (2048, 5120, 1, 0), (5120, 5152, 2, 0))
MAP_WQ = tuple((96 * hd, 96 * hd + 96, 0, 128 * hd) for hd in range(8))
MAP_WKV = (tuple((128 * hd, 128 * hd + 64, 0, 128 * hd) for hd in range(8))
           + tuple((128 * hd + 64, 128 * hd + 128, 0, 1024 + 64 * hd) for hd in range(8)))
MAP_G0 = ((0, 512, 0, 0), (512, 1536, 1, 0), (1536, 1920, 2, 0), (1920, 1952, 2, 448))


def kernel(x, positions, ab_w_in, ab_conv_w, ab_conv_b, ab_gate_a_w, ab_gate_a_b, ab_gate_x_w, ab_gate_x_b, ab_lambda, mla_q_norm, mla_kv_norm, mla_w_uq, mla_w_ukv, ab_w_out, ab_ln_g, ab_ln_b, ssd_w_in, ssd_conv_w, ssd_conv_b, ssd_dt_bias, ssd_a_log, ssd_d, ssd_norm, ssd_w_out, ssd_ln_g, ssd_ln_b, loss_target, m_ab_w_in, m_ab_conv_w, m_ab_conv_b, m_ab_gate_a_w, m_ab_gate_a_b, m_ab_gate_x_w, m_ab_gate_x_b, m_ab_lambda, m_mla_q_norm, m_mla_kv_norm, m_mla_w_uq, m_mla_w_ukv, m_ab_w_out, m_ab_ln_g, m_ab_ln_b, m_ssd_w_in, m_ssd_conv_w, m_ssd_conv_b, m_ssd_dt_bias, m_ssd_a_log, m_ssd_d, m_ssd_norm, m_ssd_w_out, m_ssd_ln_g, m_ssd_ln_b, v_ab_w_in, v_ab_conv_w, v_ab_conv_b, v_ab_gate_a_w, v_ab_gate_a_b, v_ab_gate_x_w, v_ab_gate_x_b, v_ab_lambda, v_mla_q_norm, v_mla_kv_norm, v_mla_w_uq, v_mla_w_ukv, v_ab_w_out, v_ab_ln_g, v_ab_ln_b, v_ssd_w_in, v_ssd_conv_w, v_ssd_conv_b, v_ssd_dt_bias, v_ssd_a_log, v_ssd_d, v_ssd_norm, v_ssd_w_out, v_ssd_ln_g, v_ssd_ln_b):
    args = dict(locals())
    bf = MXU_DTYPE
    big = {n: [args[pre + n][0] for pre in ("", "m_", "v_")] for n in BIG_L0 + BIG_L1}
    sml = {n: [_view2d(n, args[pre + n]) for pre in ("", "m_", "v_")] for n in SMALL_NAMES}

    gathered = _all_gather([big[n][0].astype(bf) for n in BIG_L0] + [sml[n][0] for n, *_ in SMALL], "gather_params")
    g8 = dict(zip(BIG_L0, gathered))
    p = {"wo0": g8["ab_w_out"].reshape(D_MODEL, D_MODEL)}
    p["w0p"], = _unshard(g8["ab_w_in"], MAP_W0, (2048,), "unshard_w0")
    p["wq"], = _unshard(g8["mla_w_uq"], MAP_WQ, (1024,), "unshard_wq")
    p["wkv"], = _unshard(g8["mla_w_ukv"], MAP_WKV, (1536,), "unshard_wkv")
    p["cw0"], p["cw1"], p["cb1"], p["nw"], p["g1"], p["b1"] = _unshard_small(gathered[len(BIG_L0):])
    p["wa"], p["wx"], p["dt_bias"], p["a_log"], p["d_x"] = _prep_repl(
        sml["ab_gate_a_w"][0], sml["ab_gate_x_w"][0], sml["ssd_dt_bias"][0], sml["ssd_a_log"][0], sml["ssd_d"][0])
    for key, n in (("cb0", "ab_conv_b"), ("ba", "ab_gate_a_b"), ("bx", "ab_gate_x_b"), ("lam", "ab_lambda"),
                   ("qn_w", "mla_q_norm"), ("kn_w", "mla_kv_norm"), ("g0", "ab_ln_g"), ("b0", "ab_ln_b")):
        p[key] = sml[n][0]

    acc, recv_l1, loss_part, grad_x = _local_step(
        x[0], positions[0], loss_target[0], p, [big[n][0].astype(bf) for n in BIG_L1])

    send = [_reshard([acc["g_rnn"], acc["g_gate"], acc["g_tail"]], MAP_G0, 244, bf, "reshard_w0"),
            acc["g_wo0"].astype(bf).reshape(N_DEV, 128, D_MODEL),
            _reshard([acc["g_wq"]], MAP_WQ, 96, bf, "reshard_wq"), _reshard([acc["g_wkv"]], MAP_WKV, 128, bf, "reshard_wkv")]
    sm_slots, vec_rows, ga, gx = _pack_small(*(acc[k] for k in (
        "dvec0", "g_wa", "g_wx", "dqnw", "dknw", "dgb0", "dvec1", "dcw1", "dnw", "dgb1")))
    recv = _exchange(send + [sm_slots], [vec_rows, ga, gx], "exchange_grads")

    outs = {}
    kinds = ("grad", "delta", "new_m", "new_v")
    for n, parts in zip(BIG_L0 + BIG_L1, list(recv[:4]) + list(recv_l1)):
        for kind, res in zip(kinds, _adamw(parts, *big[n], "adamw_" + n)):
            outs[kind, n] = res[None]
    res = _adamw_small(*recv[4:], [sml[n] for n in SMALL_NAMES])
    for i, n in enumerate(SMALL_NAMES):
        for k, kind in enumerate(kinds):
            outs[kind, n] = res[4 * i + k].reshape(args[n].shape)

    loss = lax.psum(loss_part, ("x", "y", "c"))
    order = ["ab_w_in", "ab_conv_w", "ab_conv_b", "ab_gate_a_w", "ab_gate_a_b", "ab_gate_x_w", "ab_gate_x_b",
             "ab_lambda", "mla_q_norm", "mla_kv_norm", "mla_w_uq", "mla_w_ukv", "ab_w_out", "ab_ln_g", "ab_ln_b",
             "ssd_w_in", "ssd_conv_w", "ssd_conv_b", "ssd_dt_bias", "ssd_a_log", "ssd_d", "ssd_norm", "ssd_w_out",
             "ssd_ln_g", "ssd_ln_b"]
    return (loss, grad_x[None], *[outs[kind, n] for kind in ("grad", "delta", "new_m", "new_v") for n in order])


def _local_step(x, pos, target, p, l1_blocks):
    bf = MXU_DTYPE
    inv_freq = 10000.0 ** (-jnp.arange(0, 32, 2, dtype=F32) / 32)
    ang = pos.astype(F32)[:, None] * inv_freq
    cos, sin = jnp.cos(ang), jnp.sin(ang)
    zeros = lambda n: jnp.zeros((SEQ, n), F32)
    tc = jnp.concatenate([jnp.ones((SEQ, 64), F32), cos, cos, zeros(32)], axis=1)
    tsa = jnp.concatenate([zeros(64), -sin, zeros(48)], axis=1)
    tsb = jnp.concatenate([zeros(80), sin, zeros(32)], axis=1)

    w0p, wq, wkv, wo0, wa, wxg = (p[k] for k in ("w0p", "wq", "wkv", "wo0", "wa", "wx"))
    cw0, cb0, ba, bx, lam = (p[k] for k in ("cw0", "cb0", "ba", "bx", "lam"))
    qn_w, kn_w, g0, b0 = (p[k] for k in ("qn_w", "kn_w", "g0", "b0"))
    cw1, cb1, dt_bias, a_log, d_x, nw, g1, b1 = (p[k] for k in ("cw1", "cb1", "dt_bias", "a_log", "d_x", "nw", "g1", "b1"))
    tril = jnp.tril(jnp.ones((SSD_L, SSD_L), F32))
    expand_t = (jnp.arange(SSD_INNER)[:, None] // SSD_P == jnp.arange(LANES)[None, :]).astype(jnp.bfloat16)

    xb = x.astype(bf)
    proj0 = _mm(xb, w0p, "nn", name="l0_in")
    xc, h = _rglru_fwd(proj0, cw0, cb0, wa, ba, wxg, bx, lam)
    qn, kn, kr = _mla_norm_fwd(proj0, qn_w, kn_w, tc, tsa, tsb)
    qraw = _mm(qn, wq, "nn", name="mla_q")
    kvraw = _mm(kn, wkv, "nn", name="mla_kv", tn=512)
    qc, kc, vc = _mla_assemble(qraw, kvraw, kr, tc, tsa, tsb)
    o, lse, (w1_8, wo1_8) = _flash_fwd(qc, kc, vc, bcast=l1_blocks)
    w1z, w1x, w1d = _unshard(w1_8, MAP_W1, (2048, 3072, 128), "unshard_w1")
    wo1 = wo1_8.reshape(SSD_INNER, D_MODEL)
    y0, v0, x1, x1b = _l0_out(h, o, proj0, x, wo0, g0, b0)

    z = _mm(x1b, w1z, "nn", name="l1_in_z")
    xbc = _mm(x1b, w1x, "nn", name="l1_in_xbc")
    dt_raw = _mm(x1b, w1d, "nn", name="l1_in_dt")
    pre, act = _ssd_conv_fwd(xbc, cw1, cb1)
    ys, hprev = _ssd_scan_fwd(act, dt_raw, dt_bias, a_log, d_x, tril)
    yn, dv1, dgb1, loss8 = _l1_out(ys, z, nw, wo1, x1, g1, b1, target)

    g_wo1 = _mm(yn, dv1, "tn", name="l1_dwout")
    dyn = _mm(dv1, wo1, "nt", name="l1_dyn", tn=1024)
    dys, dz, dnw = _l1_gate_bwd(dyn, ys, z, nw)
    dact, ddt_raw, dvec1 = _ssd_scan_bwd(dys, act, dt_raw, hprev, dt_bias, a_log, d_x, tril, expand_t)
    dxbc, dcw1 = _ssd_conv_bwd(dact, pre, xbc, cw1)
    g_z, g_xbc = _mm(x1b, dz, "tn", name="l1_dw_z"), _mm(x1b, dxbc, "tn", name="l1_dw_xbc")
    g_dt = _mm(x1b, ddt_raw, "tn", name="l1_dw_dt")
    dx1 = _mm(dz, w1z, "nt", name="l1_dx_z", add=dv1, add_scale=DN_ALPHA)
    dx1 = _mm(dxbc, w1x, "nt", name="l1_dx_xbc", add=dx1)
    dx1 = _mm(ddt_raw, w1d, "nt", name="l1_dx_dt", add=dx1)

    dv0, dgb0 = _ln_bwd_call(v0, dx1, g0)
    g_wo0 = _mm(y0, dv0, "tn", name="l0_dwout")
    dy0 = _mm(dv0, wo0, "nt", name="l0_dy")
    dh, do, dgate = _gate_bwd(dy0, h, o, proj0)
    send_l1 = [_reshard([g_z, g_xbc, g_dt], MAP_W1, 644, bf, "reshard_w1"), g_wo1.astype(bf).reshape(N_DEV, 256, D_MODEL)]
    dq, dk, dvv, recv_l1 = _flash_bwd(qc, kc, vc, o, do, lse, scatter=send_l1)
    dqraw, dkvraw, dkr = _mla_bwd_rope(dq, dk, dvv, tc, tsa, tsb)
    g_wq = _mm(qn, dqraw, "tn", name="mla_dwq", tm=256)
    g_wkv = _mm(kn, dkvraw, "tn", name="mla_dwkv", tm=128, tn=512)
    dqn = _mm(dqraw, wq, "nt", name="mla_dqn", tn=256)
    dkn = _mm(dkvraw, wkv, "nt", name="mla_dkn", tn=128)
    dtail, dqnw, dknw = _mla_norm_bwd(proj0, dqn, dkn, dkr, qn_w, kn_w)
    dxr, g_wa, g_wx, dvec0 = _rglru_bwd(dh, xc, h, proj0, cw0, wa, ba, wxg, bx, lam)
    g_tail = _mm(xb, dtail, "tn", name="l0_dw_tail")
    g_rnn, g_gate = _mm(xb, dxr, "tn", name="l0_dw_rnn"), _mm(xb, dgate, "tn", name="l0_dw_gate")
    dx = _mm(dxr, w0p, "nt", name="l0_dx_rnn", add=dv0, add_scale=DN_ALPHA, b_col=P0_RNN)
    dx = _mm(dgate, w0p, "nt", name="l0_dx_gate", add=dx, b_col=0)
    dx = _mm(dtail, w0p, "nt", name="l0_dx_tail", add=dx, b_col=3)

    acc = {"g_rnn": g_rnn, "g_gate": g_gate, "g_tail": g_tail, "g_wo0": g_wo0, "g_wq": g_wq, "g_wkv": g_wkv,
           "dvec0": dvec0, "g_wa": g_wa, "g_wx": g_wx, "dqnw": dqnw, "dknw": dknw, "dgb0": dgb0, "dvec1": dvec1,
           "dcw1": dcw1, "dnw": dnw, "dgb1": dgb1}
    return acc, recv_l1, loss8[0, 0], dx
```

```python
import math

import jax
import jax.numpy as jnp
from jax import lax
from jax.experimental import pallas as pl
from jax.experimental.pallas import tpu as pltpu

F32 = jnp.float32
MXU_DTYPE = jnp.bfloat16

N_DEV = 8
SEQ = 4096
D_MODEL = 1024
DN_ALPHA = 4.0 ** 0.25
RNN_W = 512
MLA_HEADS = 8
ATT_SCALE = 96.0 ** -0.5
ATT_C = ATT_SCALE * math.log2(math.e)
RG_C = 8.0
SSD_INNER = 2048
SSD_HEADS = 32
SSD_P = 64
SSD_GROUPS = 4
SSD_N = 128
SSD_L = 128
SSD_CONV = 3072
LANES = 128
SUBLANES = 8
VMEM_LIMIT = 56 * 1024 * 1024

ADAM_LR, ADAM_B1, ADAM_B2, ADAM_EPS, ADAM_WD, ADAM_STEP = 0.001, 0.9, 0.999, 1e-08, 0.01, 10

HIGHEST = lax.Precision.HIGHEST


def _params(sem, limit=VMEM_LIMIT):
    return pltpu.CompilerParams(dimension_semantics=sem, vmem_limit_bytes=limit)


def _dot(a, b):
    return lax.dot_general(a, b, (((1,), (0,)), ((), ())), preferred_element_type=F32)


def _dot_nt(a, b):
    return lax.dot_general(a, b, (((1,), (1,)), ((), ())), preferred_element_type=F32)


def _dot_tn(a, b):
    return lax.dot_general(a, b, (((0,), (0,)), ((), ())), preferred_element_type=F32)


def _dot_hi(a, b):
    return lax.dot_general(a, b, (((1,), (0,)), ((), ())), precision=HIGHEST, preferred_element_type=F32)


def _mx(v):
    return v.astype(MXU_DTYPE)


def _sigmoid(v):
    return 1.0 / (1.0 + jnp.exp(-v))


def _log1p_pos(e):
    poly = e * (1.0 - e * (0.5 - e * (1.0 / 3.0 - e * 0.25)))
    return jnp.where(e < 0.01, poly, jnp.log(1.0 + e))


def _softplus(v):
    return jnp.maximum(v, 0.0) + _log1p_pos(jnp.exp(-jnp.abs(v)))


def _neg_expm1(v):
    poly = -v * (1.0 + v * (0.5 + v * (1.0 / 6.0 + v * (1.0 / 24.0 + v * (1.0 / 120.0)))))
    return jnp.where(jnp.abs(v) < 0.1, poly, 1.0 - jnp.exp(v))


def _silu(v):
    return v * _sigmoid(v)


def _dsilu(v):
    s = _sigmoid(v)
    return s * (1.0 + v * (1.0 - s))


def _mm(a, b, mode, *, name, add=None, add_scale=1.0, out_dtype=F32, tm=None, tn=1024, tk=512, b_col=0):
    if mode == "tn":
        kdim, m = a.shape
        n = b.shape[1]
        tm, tn, tk = min(tm or 1024, m), min(tn, n), min(tk, kdim)

        def body_tn(a_ref, b_ref, o_ref):
            @pl.when(pl.program_id(2) == 0)
            def _():
                o_ref[...] = jnp.zeros_like(o_ref)

            o_ref[...] += _dot_tn(_mx(a_ref[...]), _mx(b_ref[...]))

        return pl.pallas_call(
            body_tn, name=name, grid=(m // tm, n // tn, kdim // tk),
            in_specs=[pl.BlockSpec((tk, tm), lambda i, j, k: (k, i)), pl.BlockSpec((tk, tn), lambda i, j, k: (k, j))],
            out_specs=pl.BlockSpec((tm, tn), lambda i, j, k: (i, j)),
            out_shape=jax.ShapeDtypeStruct((m, n), F32),
            compiler_params=_params(("parallel", "parallel", "arbitrary")),
        )(a, b)

    m, kdim = a.shape
    n = b.shape[1] if mode == "nn" else b.shape[0]
    tm, tn = min(tm or 1024, m), min(tn, n)
    has_add = add is not None

    def body(*refs):
        a_ref, b_ref = refs[0], refs[1]
        o_ref = refs[-1]
        av, bv = _mx(a_ref[...]), _mx(b_ref[...])
        acc = _dot(av, bv) if mode == "nn" else _dot_nt(av, bv)
        if has_add:
            acc = acc + add_scale * refs[2][...]
        o_ref[...] = acc.astype(out_dtype)

    b_spec = (pl.BlockSpec((kdim, tn), lambda i, j: (0, j)) if mode == "nn"
              else pl.BlockSpec((tn, kdim), lambda i, j: (j, b_col)))
    in_specs = [pl.BlockSpec((tm, kdim), lambda i, j: (i, 0)), b_spec]
    args = [a, b]
    if has_add:
        in_specs.append(pl.BlockSpec((tm, tn), lambda i, j: (i, j)))
        args.append(add)
    return pl.pallas_call(
        body, name=name, grid=(m // tm, n // tn), in_specs=in_specs,
        out_specs=pl.BlockSpec((tm, tn), lambda i, j: (i, j)),
        out_shape=jax.ShapeDtypeStruct((m, n), out_dtype),
        compiler_params=_params(("parallel", "parallel")),
    )(*args)


def _shift_down(blk, halo, s):
    if s == 0:
        return blk
    t = blk.shape[0]
    r = pltpu.roll(blk, s, 0)
    hr = pltpu.roll(halo, s, 0)
    row8 = lax.broadcasted_iota(jnp.int32, hr.shape, 0)
    head = jnp.where(row8 < s, hr, r[:SUBLANES])
    return jnp.concatenate([head, r[SUBLANES:]], axis=0) if t > SUBLANES else head


def _shift_up(blk, halo, s):
    if s == 0:
        return blk
    t = blk.shape[0]
    r = pltpu.roll(blk, t - s, 0)
    hr = pltpu.roll(halo, SUBLANES - s, 0)
    row8 = lax.broadcasted_iota(jnp.int32, hr.shape, 0)
    tail = jnp.where(row8 >= SUBLANES - s, hr, r[t - SUBLANES:])
    return jnp.concatenate([r[:t - SUBLANES], tail], axis=0) if t > SUBLANES else tail


def _scan_down(a, u):
    t = a.shape[0]
    row = lax.broadcasted_iota(jnp.int32, a.shape, 0)
    d = 1
    while d < t:
        keep = row >= d
        a_sh = jnp.where(keep, pltpu.roll(a, d, 0), 1.0)
        u_sh = jnp.where(keep, pltpu.roll(u, d, 0), 0.0)
        u = a * u_sh + u
        a = a * a_sh
        d *= 2
    return a, u


def _scan_up(a, u):
    t = a.shape[0]
    row = lax.broadcasted_iota(jnp.int32, a.shape, 0)
    d = 1
    while d < t:
        keep = row < t - d
        a_sh = jnp.where(keep, pltpu.roll(a, t - d, 0), 1.0)
        u_sh = jnp.where(keep, pltpu.roll(u, t - d, 0), 0.0)
        u = a * u_sh + u
        a = a * a_sh
        d *= 2
    return a, u


def _conv4(blk, halo, cw, cb):
    out = cb + blk * cw[3:4]
    for k in range(3):
        out = out + _shift_down(blk, halo, 3 - k) * cw[k:k + 1]
    return out


RG_T = 512
P0_RNN = 2


def _rg_gates(xc, wa, ba, wx, bx, lam):
    xcb = _mx(xc)
    r = _sigmoid(_dot(xcb, wa) + ba)
    ig = _sigmoid(_dot(xcb, wx) + bx)
    sp = _softplus(-lam)
    la = (-RG_C * r) * sp
    a = jnp.exp(la)
    mult = jnp.sqrt(_neg_expm1(2.0 * la))
    return r, ig, sp, a, mult


def _rglru_fwd(proj0, cw8, cb, wa, ba, wx, bx, lam):
    t, w = RG_T, RNN_W
    nb = SEQ // t

    def body(x_ref, halo_ref, cw_ref, cb_ref, wa_ref, ba_ref, wx_ref, bx_ref, lam_ref, xc_ref, h_ref, carry):
        i = pl.program_id(0)

        @pl.when(i == 0)
        def _():
            carry[...] = jnp.zeros_like(carry)

        blk = x_ref[...]
        halo = jnp.where(i > 0, halo_ref[...], 0.0)
        xc = _conv4(blk, halo, cw_ref[...], cb_ref[...])
        _, ig, _, a, mult = _rg_gates(xc, wa_ref[...], ba_ref[...], wx_ref[...], bx_ref[...], lam_ref[...])
        u = mult * (ig * xc)
        big_a, big_u = _scan_down(a, u)
        h = big_a * carry[SUBLANES - 1:SUBLANES, :] + big_u
        carry[...] = h[t - SUBLANES:]
        xc_ref[...] = xc
        h_ref[...] = h

    vec = pl.BlockSpec((1, w), lambda i: (0, 0))
    mat = pl.BlockSpec((w, w), lambda i: (0, 0))
    return pl.pallas_call(
        body, name="rglru_fwd", grid=(nb,),
        in_specs=[pl.BlockSpec((t, w), lambda i: (i, P0_RNN)),
                  pl.BlockSpec((SUBLANES, w), lambda i: (jnp.maximum(i * (t // SUBLANES) - 1, 0), P0_RNN)),
                  pl.BlockSpec((SUBLANES, w), lambda i: (0, 0)), vec, mat, vec, mat, vec, vec],
        out_specs=[pl.BlockSpec((t, w), lambda i: (i, 0)), pl.BlockSpec((t, w), lambda i: (i, 0))],
        out_shape=[jax.ShapeDtypeStruct((SEQ, w), F32), jax.ShapeDtypeStruct((SEQ, w), F32)],
        scratch_shapes=[pltpu.VMEM((SUBLANES, w), F32)],
        compiler_params=_params(("arbitrary",)),
    )(proj0, proj0, cw8, cb, wa, ba, wx, bx, lam)


def _rglru_bwd(dh, xc, h, proj0, cw8, wa, ba, wx, bx, lam):
    t, w = RG_T, RNN_W
    nb = SEQ // t
    tb = t // SUBLANES

    def body(dh_ref, xc_ref, h_ref, hh_ref, x_ref, cw_ref, wa_ref, ba_ref, wx_ref, bx_ref, lam_ref,
             dx_ref, dwa_ref, dwx_ref, dvec_ref, gcarry, dxc_next):
        i = pl.program_id(0)
        rev = nb - 1 - i

        @pl.when(i == 0)
        def _():
            gcarry[...] = jnp.zeros_like(gcarry)
            dxc_next[...] = jnp.zeros_like(dxc_next)
            dwa_ref[...] = jnp.zeros_like(dwa_ref)
            dwx_ref[...] = jnp.zeros_like(dwx_ref)
            dvec_ref[...] = jnp.zeros_like(dvec_ref)

        xc = xc_ref[...]
        wa_v, wx_v = wa_ref[...], wx_ref[...]
        lam_v = lam_ref[...]
        r, ig, sp, a, mult = _rg_gates(xc, wa_v, ba_ref[...], wx_v, bx_ref[...], lam_v)
        dhv = dh_ref[...]
        big_a, big_u = _scan_up(a, a * dhv)
        gg = big_a * gcarry[0:1, :] + big_u
        g = dhv + _shift_up(gg, gcarry[...], 1)
        gcarry[...] = gg[:SUBLANES]
        hhalo = jnp.where(rev > 0, hh_ref[...], 0.0)
        da = g * _shift_down(h_ref[...], hhalo, 1)
        d_mult = g * (ig * xc)
        d_i = g * (mult * xc)
        dxc = g * (mult * ig)
        d_la = da * a - d_mult * (a * a) / mult
        d_r = d_la * (-RG_C * sp)
        d_sp = jnp.sum(d_la * (-RG_C * r), axis=0, keepdims=True)
        d_pa = d_r * r * (1.0 - r)
        d_px = d_i * ig * (1.0 - ig)
        d_pab, d_pxb = _mx(d_pa), _mx(d_px)
        dxc = dxc + _dot_nt(d_pab, wa_v) + _dot_nt(d_pxb, wx_v)
        xcb = _mx(xc)
        dwa_ref[...] += _dot_tn(xcb, d_pab)
        dwx_ref[...] += _dot_tn(xcb, d_pxb)
        dvec_ref[0:1, :] += jnp.sum(d_pa, axis=0, keepdims=True)
        dvec_ref[1:2, :] += jnp.sum(d_px, axis=0, keepdims=True)
        dvec_ref[2:3, :] += d_sp * (-_sigmoid(-lam_v))
        dvec_ref[3:4, :] += jnp.sum(dxc, axis=0, keepdims=True)
        xblk = x_ref[...]
        cw = cw_ref[...]
        dx = dxc * cw[3:4]
        nxt = dxc_next[...]
        dvec_ref[7:8, :] += jnp.sum(dxc * xblk, axis=0, keepdims=True)
        for k in range(3):
            up = _shift_up(dxc, nxt, 3 - k)
            dvec_ref[4 + k:5 + k, :] += jnp.sum(up * xblk, axis=0, keepdims=True)
            dx = dx + up * cw[k:k + 1]
        dxc_next[...] = dxc[:SUBLANES]
        dx_ref[...] = _mx(dx)

    blk = pl.BlockSpec((t, w), lambda i: (nb - 1 - i, 0))
    halo = pl.BlockSpec((SUBLANES, w), lambda i: (jnp.maximum((nb - 1 - i) * tb - 1, 0), 0))
    vec = pl.BlockSpec((1, w), lambda i: (0, 0))
    mat = pl.BlockSpec((w, w), lambda i: (0, 0))
    return pl.pallas_call(
        body, name="rglru_bwd", grid=(nb,),
        in_specs=[blk, blk, blk, halo, pl.BlockSpec((t, w), lambda i: (nb - 1 - i, P0_RNN)),
                  pl.BlockSpec((SUBLANES, w), lambda i: (0, 0)), mat, vec, mat, vec, vec],
        out_specs=[blk, mat, mat, pl.BlockSpec((16, w), lambda i: (0, 0))],
        out_shape=[jax.ShapeDtypeStruct((SEQ, w), MXU_DTYPE), jax.ShapeDtypeStruct((w, w), F32),
                   jax.ShapeDtypeStruct((w, w), F32), jax.ShapeDtypeStruct((16, w), F32)],
        scratch_shapes=[pltpu.VMEM((SUBLANES, w), F32), pltpu.VMEM((SUBLANES, w), F32)],
        compiler_params=_params(("arbitrary",)),
    )(dh, xc, h, h, proj0, cw8, wa, ba, wx, bx, lam)


MLA_T = 512


def _rope(v, c, sa, sb):
    return v * c + pltpu.roll(v, LANES - 16, 1) * sa + pltpu.roll(v, 16, 1) * sb


def _rope_t(dv, c, sa, sb):
    return dv * c + pltpu.roll(dv * sa, 16, 1) + pltpu.roll(dv * sb, LANES - 16, 1)


def _rms(v, g, eps=1e-6):
    rs = lax.rsqrt(jnp.mean(v * v, axis=-1, keepdims=True) + eps)
    return v * rs * g, rs


def _mla_fwd(proj0, q_norm, kv_norm, wq, wkv, tc, tsa, tsb):
    t = MLA_T

    def body(cq_ref, ck_ref, qn_ref, kn_ref, wq_ref, wkv_ref, c_ref, sa_ref, sb_ref,
             oqn_ref, okn_ref, oq_ref, ok_ref, ov_ref):
        c, sa, sb = c_ref[...], sa_ref[...], sb_ref[...]
        ck = ck_ref[...]
        qn = _mx(_rms(cq_ref[...], qn_ref[...])[0])
        kn = _mx(_rms(ck[:, :LANES], kn_ref[...])[0])
        oqn_ref[...] = qn
        okn_ref[...] = kn
        krv = _rope(ck[:, LANES:], c, sa, sb)
        qraw = _dot(qn, wq_ref[...])
        kvraw = _dot(kn, wkv_ref[...])
        for hd in range(MLA_HEADS):
            sl = slice(hd * LANES, (hd + 1) * LANES)
            oq_ref[:, sl] = _mx(_rope(qraw[:, sl], c, sa, sb))
            ok_ref[:, sl] = _mx(kvraw[:, sl] + krv)
        ov_ref[...] = _mx(kvraw[:, 1024:])

    tab = pl.BlockSpec((t, LANES), lambda i: (i, 0))
    wide = pl.BlockSpec((t, 1024), lambda i: (i, 0))
    const = lambda shape: pl.BlockSpec(shape, lambda i: (0, 0))
    return pl.pallas_call(
        body, name="mla_fwd", grid=(SEQ // t,),
        in_specs=[pl.BlockSpec((t, 256), lambda i: (i, 6)), pl.BlockSpec((t, 256), lambda i: (i, 7)),
                  const((1, 256)), const((1, LANES)), const((256, 1024)), const((LANES, 1536)), tab, tab, tab],
        out_specs=[pl.BlockSpec((t, 256), lambda i: (i, 0)), tab, wide, wide, pl.BlockSpec((t, 512), lambda i: (i, 0))],
        out_shape=[jax.ShapeDtypeStruct((SEQ, 256), MXU_DTYPE), jax.ShapeDtypeStruct((SEQ, LANES), MXU_DTYPE),
                   jax.ShapeDtypeStruct((SEQ, 1024), MXU_DTYPE), jax.ShapeDtypeStruct((SEQ, 1024), MXU_DTYPE),
                   jax.ShapeDtypeStruct((SEQ, 512), MXU_DTYPE)],
        compiler_params=_params(("parallel",)),
    )(proj0, proj0, q_norm, kv_norm, wq, wkv, tc, tsa, tsb)


ATT_T = 512


def _flash_fwd(q, k, v, bcast=()):
    t = ATT_T
    nb = SEQ // t

    steps = [(qi, ki) for qi in range(nb) for ki in range(qi + 1)]
    qi_tab = jnp.asarray([s[0] for s in steps], jnp.int32)
    ki_tab = jnp.asarray([s[1] for s in steps], jnp.int32)

    nx = len(bcast)

    def body(qi_ref, ki_ref, q_ref, k_ref, v_ref, *rest):
        x_refs, (o_ref, lse_ref), g_refs = rest[:nx], rest[nx:nx + 2], rest[nx + 2:2 * nx + 2]
        m_sc, acc_sc = rest[2 * nx + 2:2 * nx + 4]
        step = pl.program_id(1)
        qi, ki = qi_ref[step], ki_ref[step]
        if nx:
            copies = _peer_copies(x_refs, g_refs, rest[2 * nx + 4:], 0)

            @pl.when((pl.program_id(0) == 0) & (step == 0))
            def _():
                for cp in copies:
                    cp.start()

        @pl.when(ki == 0)
        def _():
            m_sc[...] = jnp.full_like(m_sc, -jnp.inf)
            acc_sc[...] = jnp.zeros_like(acc_sc)

        def update(diagonal):
            vv = v_ref[...]
            lane_v = lax.broadcasted_iota(jnp.int32, vv.shape, 1)
            m_prev = [m_sc[0], m_sc[1]]
            a_prev = [acc_sc[0], acc_sc[1]]
            m_out, a_out = [], []
            for hd in range(2):
                sl = slice(hd * LANES, (hd + 1) * LANES)
                s = _dot_nt(q_ref[:, sl], k_ref[:, sl])
                if diagonal:
                    s = jnp.where(lax.broadcasted_iota(jnp.int32, (t, t), 1)
                                  <= lax.broadcasted_iota(jnp.int32, (t, t), 0), s, -jnp.inf)
                m_new = jnp.maximum(m_prev[hd], jnp.max(s, axis=1, keepdims=True))
                p = jnp.exp2((s - m_new[:, :1]) * ATT_C)
                vh = jnp.where((lane_v >= hd * 64) & (lane_v < (hd + 1) * 64), vv, jnp.ones_like(vv))
                m_out.append(m_new)
                a_out.append(a_prev[hd] * jnp.exp2((m_prev[hd] - m_new) * ATT_C) + _dot(_mx(p), vh))
            for hd in range(2):
                m_sc[hd] = m_out[hd]
                acc_sc[hd] = a_out[hd]

        @pl.when(ki < qi)
        def _():
            update(False)

        @pl.when(ki == qi)
        def _():
            update(True)
            first = lax.broadcasted_iota(jnp.int32, (t, LANES), 1) < 64
            a0, a1 = acc_sc[0], acc_sc[1]
            l0, l1 = pltpu.roll(a0, 64, 1), pltpu.roll(a1, 64, 1)
            o_ref[...] = jnp.where(first, a0 / l0, a1 / l1)
            lse_ref[0] = jnp.where(first, m_sc[0] * ATT_SCALE + jnp.log(l0), m_sc[1] * ATT_SCALE + jnp.log(l1))

        if nx:
            @pl.when((pl.program_id(0) == 3) & (step == len(steps) - 1))
            def _():
                for cp in copies:
                    cp.wait()

    grid_spec = pltpu.PrefetchScalarGridSpec(
        num_scalar_prefetch=2, grid=(4, len(steps)),
        in_specs=[pl.BlockSpec((t, 256), lambda p, s, qt, kt: (qt[s], p)),
                  pl.BlockSpec((t, 256), lambda p, s, qt, kt: (kt[s], p)),
                  pl.BlockSpec((t, LANES), lambda p, s, qt, kt: (kt[s], p))] + [ANY] * nx,
        out_specs=[pl.BlockSpec((t, LANES), lambda p, s, qt, kt: (qt[s], p)),
                   pl.BlockSpec((1, t, LANES), lambda p, s, qt, kt: (p, qt[s], 0))] + [ANY] * nx,
        scratch_shapes=[pltpu.VMEM((2, t, LANES), F32), pltpu.VMEM((2, t, LANES), F32)]
        + (_exchange_sems(nx) if nx else []))
    res = pl.pallas_call(
        body, name="flash_fwd", grid_spec=grid_spec,
        out_shape=[jax.ShapeDtypeStruct((SEQ, 512), F32), jax.ShapeDtypeStruct((4, SEQ, LANES), F32)]
        + _exchange_shapes([], bcast),
        compiler_params=_params(("arbitrary", "arbitrary")),
    )(qi_tab, ki_tab, q, k, v, *bcast)
    return res[0], res[1], res[2:]


def _flash_bwd(q, k, v, o, do, lse, scatter=()):
    t = ATT_T
    nb = SEQ // t

    steps = [(qi, ki) for ki in range(nb) for qi in range(ki, nb)]
    qi_tab = jnp.asarray([s[0] for s in steps], jnp.int32)
    ki_tab = jnp.asarray([s[1] for s in steps], jnp.int32)
    log2e = math.log2(math.e)

    nx = len(scatter)

    def body(qi_ref, ki_ref, q_ref, k_ref, v_ref, o_ref, do_ref, lse_ref, *rest):
        x_refs, (dq_ref, dk_ref, dv_ref), g_refs = rest[:nx], rest[nx:nx + 3], rest[nx + 3:2 * nx + 3]
        step = pl.program_id(1)
        qi, ki = qi_ref[step], ki_ref[step]
        if nx:
            copies = _peer_copies(x_refs, g_refs, rest[2 * nx + 3:], nx)

            @pl.when((pl.program_id(0) == 0) & (step == 0))
            def _():
                for cp in copies:
                    cp.start()

        @pl.when(step == 0)
        def _():
            dq_ref[...] = jnp.zeros_like(dq_ref)

        @pl.when(qi == ki)
        def _():
            dk_ref[...] = jnp.zeros_like(dk_ref)
            dv_ref[...] = jnp.zeros_like(dv_ref)

        def update(diagonal):
            dov, ov, vv = do_ref[...], o_ref[...], v_ref[...]
            lse2 = lse_ref[0] * log2e
            lane = lax.broadcasted_iota(jnp.int32, (t, LANES), 1)
            prod = dov * ov
            qrows = pl.ds(pl.multiple_of(qi * t, t), t)
            dv_acc = jnp.zeros((t, LANES), F32)
            dk_new, dq_new = [], []
            for hd in range(2):
                sl = slice(hd * LANES, (hd + 1) * LANES)
                mine = (lane >= hd * 64) & (lane < (hd + 1) * 64)
                qh, kh = q_ref[:, sl], k_ref[:, sl]
                p = jnp.exp2(_dot_nt(qh, kh) * ATT_C - lse2[:, hd * 64:hd * 64 + 1])
                if diagonal:
                    p = jnp.where(lax.broadcasted_iota(jnp.int32, (t, t), 1)
                                  <= lax.broadcasted_iota(jnp.int32, (t, t), 0), p, 0.0)
                do_h = jnp.where(mine, dov, 0.0)
                delta = jnp.sum(jnp.where(mine, prod, 0.0), axis=1, keepdims=True)
                dp = _dot_nt(_mx(do_h), vv)
                ds = _mx(p * (dp - delta) * ATT_SCALE)
                dv_acc = dv_acc + jnp.where(mine, _dot_tn(_mx(p), _mx(dov)), 0.0)
                dk_new.append(_dot_tn(ds, qh))
                dq_new.append(_dot(ds, kh))
            for hd in range(2):
                sl = slice(hd * LANES, (hd + 1) * LANES)
                dk_ref[:, sl] += dk_new[hd]
                dq_ref[qrows, sl] += dq_new[hd]
            dv_ref[...] += dv_acc

        @pl.when(qi > ki)
        def _():
            update(False)

        @pl.when(qi == ki)
        def _():
            update(True)

        if nx:
            @pl.when((pl.program_id(0) == 3) & (step == len(steps) - 1))
            def _():
                for cp in copies:
                    cp.wait()

    qmap = lambda p, s, qt, kt: (qt[s], p)
    kmap = lambda p, s, qt, kt: (kt[s], p)
    grid_spec = pltpu.PrefetchScalarGridSpec(
        num_scalar_prefetch=2, grid=(4, len(steps)),
        in_specs=[pl.BlockSpec((t, 256), qmap), pl.BlockSpec((t, 256), kmap), pl.BlockSpec((t, LANES), kmap),
                  pl.BlockSpec((t, LANES), qmap), pl.BlockSpec((t, LANES), qmap),
                  pl.BlockSpec((1, t, LANES), lambda p, s, qt, kt: (p, qt[s], 0))] + [ANY] * nx,
        out_specs=[pl.BlockSpec((SEQ, 256), lambda p, s, qt, kt: (0, p)), pl.BlockSpec((t, 256), kmap),
                   pl.BlockSpec((t, LANES), kmap)] + [ANY] * nx,
        scratch_shapes=_exchange_sems(nx) if nx else [])
    res = pl.pallas_call(
        body, name="flash_bwd", grid_spec=grid_spec,
        out_shape=[jax.ShapeDtypeStruct((SEQ, 1024), F32), jax.ShapeDtypeStruct((SEQ, 1024), F32),
                   jax.ShapeDtypeStruct((SEQ, 512), F32)] + _exchange_shapes(scatter, []),
        compiler_params=_params(("arbitrary", "arbitrary")),
    )(qi_tab, ki_tab, q, k, v, o, do, lse, *scatter)
    return res[0], res[1], res[2], res[3:]


def _rms_bwd(v, g, dy, eps=1e-6):
    rs = lax.rsqrt(jnp.mean(v * v, axis=-1, keepdims=True) + eps)
    xh = v * rs
    dxh = dy * g
    dv = rs * (dxh - xh * jnp.mean(dxh * xh, axis=-1, keepdims=True))
    return dv, jnp.sum(dy * xh, axis=0, keepdims=True)


def _mla_bwd(dq, dk, dv, proj0, q_norm, kv_norm, wq, wkv, tc, tsa, tsb):
    t = MLA_T

    def body(dq_ref, dk_ref, dv_ref, cq_ref, ck_ref, qn_ref, kn_ref, wq_ref, wkv_ref, c_ref, sa_ref, sb_ref,
             oq_ref, okv_ref, o_ref, dgq_ref, dgk_ref):
        @pl.when(pl.program_id(0) == 0)
        def _():
            dgq_ref[...] = jnp.zeros_like(dgq_ref)
            dgk_ref[...] = jnp.zeros_like(dgk_ref)

        c, sa, sb = c_ref[...], sa_ref[...], sb_ref[...]
        lane = lax.broadcasted_iota(jnp.int32, (t, LANES), 1)
        dkr = jnp.zeros((t, LANES), F32)
        for hd in range(MLA_HEADS):
            sl = slice(hd * LANES, (hd + 1) * LANES)
            oq_ref[:, sl] = _mx(_rope_t(dq_ref[:, sl], c, sa, sb))
            dkh = dk_ref[:, sl]
            okv_ref[:, sl] = _mx(dkh)
            dkr = dkr + dkh
        okv_ref[:, 1024:] = _mx(dv_ref[...])
        dkr = _rope_t(jnp.where((lane >= 64) & (lane < 96), dkr, 0.0), c, sa, sb)
        dqn = _dot_nt(oq_ref[...], wq_ref[...])
        dkn = _dot_nt(okv_ref[...], wkv_ref[...])
        dcq, dgq = _rms_bwd(cq_ref[...], qn_ref[...], dqn)
        dck, dgk = _rms_bwd(ck_ref[:, :LANES], kn_ref[...], dkn)
        o_ref[:, :256] = _mx(dcq)
        o_ref[:, 256:384] = _mx(dck)
        o_ref[:, 384:] = _mx(dkr)
        dgq_ref[0:1, :] += dgq
        dgk_ref[0:1, :] += dgk

    tab = pl.BlockSpec((t, LANES), lambda i: (i, 0))
    wide = pl.BlockSpec((t, 1024), lambda i: (i, 0))
    const = lambda shape: pl.BlockSpec(shape, lambda i: (0, 0))
    return pl.pallas_call(
        body, name="mla_bwd", grid=(SEQ // t,),
        in_specs=[wide, wide, pl.BlockSpec((t, 512), lambda i: (i, 0)),
                  pl.BlockSpec((t, 256), lambda i: (i, 6)), pl.BlockSpec((t, 256), lambda i: (i, 7)),
                  const((1, 256)), const((1, LANES)), const((256, 1024)), const((LANES, 1536)), tab, tab, tab],
        out_specs=[wide, pl.BlockSpec((t, 1536), lambda i: (i, 0)), pl.BlockSpec((t, 512), lambda i: (i, 0)),
                   const((SUBLANES, 256)), const((SUBLANES, LANES))],
        out_shape=[jax.ShapeDtypeStruct((SEQ, 1024), MXU_DTYPE), jax.ShapeDtypeStruct((SEQ, 1536), MXU_DTYPE),
                   jax.ShapeDtypeStruct((SEQ, 512), MXU_DTYPE), jax.ShapeDtypeStruct((SUBLANES, 256), F32),
                   jax.ShapeDtypeStruct((SUBLANES, LANES), F32)],
        compiler_params=_params(("arbitrary",)),
    )(dq, dk, dv, proj0, proj0, q_norm, kv_norm, wq, wkv, tc, tsa, tsb)


LN_T = 512


def _ln(v, g, b, eps=1e-5):
    mu = jnp.mean(v, axis=-1, keepdims=True)
    xc = v - mu
    rs = lax.rsqrt(jnp.mean(xc * xc, axis=-1, keepdims=True) + eps)
    return xc * rs * g + b


def _ln_bwd(v, g, dy, eps=1e-5):
    mu = jnp.mean(v, axis=-1, keepdims=True)
    xc = v - mu
    rs = lax.rsqrt(jnp.mean(xc * xc, axis=-1, keepdims=True) + eps)
    xh = xc * rs
    dxh = dy * g
    dv = rs * (dxh - jnp.mean(dxh, axis=-1, keepdims=True) - xh * jnp.mean(dxh * xh, axis=-1, keepdims=True))
    return dv, jnp.sum(dy * xh, axis=0, keepdims=True), jnp.sum(dy, axis=0, keepdims=True)


def _l0_out(h, o, proj0, x, w_out, g, b):
    t = LN_T

    def body(h_ref, o_ref, ga_ref, gb_ref, x_ref, w_ref, g_ref, b_ref, y_ref, v_ref, x1_ref, x1b_ref):
        y = _mx(jnp.concatenate([h_ref[...] * _silu(ga_ref[...]), o_ref[...] * _silu(gb_ref[...])], axis=1))
        v = DN_ALPHA * x_ref[...] + _dot(y, w_ref[...])
        y_ref[...] = y
        v_ref[...] = v
        x1 = _ln(v, g_ref[...], b_ref[...])
        x1_ref[...] = x1
        x1b_ref[...] = _mx(x1)

    half = pl.BlockSpec((t, 512), lambda i: (i, 0))
    full = pl.BlockSpec((t, D_MODEL), lambda i: (i, 0))
    vec = pl.BlockSpec((1, D_MODEL), lambda i: (0, 0))
    return pl.pallas_call(
        body, name="l0_out", grid=(SEQ // t,),
        in_specs=[half, half, pl.BlockSpec((t, 512), lambda i: (i, 0)), pl.BlockSpec((t, 512), lambda i: (i, 1)), full,
                  pl.BlockSpec((D_MODEL, D_MODEL), lambda i: (0, 0)), vec, vec],
        out_specs=[full, full, full, full],
        out_shape=[jax.ShapeDtypeStruct((SEQ, D_MODEL), MXU_DTYPE), jax.ShapeDtypeStruct((SEQ, D_MODEL), F32),
                   jax.ShapeDtypeStruct((SEQ, D_MODEL), F32), jax.ShapeDtypeStruct((SEQ, D_MODEL), MXU_DTYPE)],
        compiler_params=_params(("parallel",)),
    )(h, o, proj0, proj0, x, w_out, g, b)


def _ln_bwd_call(v, dy, ddt, w1d, g):
    t = LN_T

    def body(v_ref, dy_ref, ddt_ref, w_ref, g_ref, dv_ref, dgb_ref):
        @pl.when(pl.program_id(0) == 0)
        def _():
            dgb_ref[...] = jnp.zeros_like(dgb_ref)

        dy_v = dy_ref[...] + _dot_nt(_mx(ddt_ref[...]), w_ref[...])
        dv, dg, db = _ln_bwd(v_ref[...], g_ref[...], dy_v)
        dv_ref[...] = dv
        dgb_ref[0:1, :] += dg
        dgb_ref[1:2, :] += db

    full = pl.BlockSpec((t, D_MODEL), lambda i: (i, 0))
    return pl.pallas_call(
        body, name="ln_bwd", grid=(SEQ // t,),
        in_specs=[full, full, pl.BlockSpec((t, LANES), lambda i: (i, 0)), pl.BlockSpec((D_MODEL, LANES), lambda i: (0, 0)),
                  pl.BlockSpec((1, D_MODEL), lambda i: (0, 0))],
        out_specs=[full, pl.BlockSpec((SUBLANES, D_MODEL), lambda i: (0, 0))],
        out_shape=[jax.ShapeDtypeStruct((SEQ, D_MODEL), F32), jax.ShapeDtypeStruct((SUBLANES, D_MODEL), F32)],
        compiler_params=_params(("arbitrary",)),
    )(v, dy, ddt, w1d, g)


def _gate_bwd(dv0, w_out, h, o, proj0):
    t = LN_T

    def body(dv_ref, w_ref, h_ref, o_ref, ga_ref, gb_ref, dh_ref, do_ref, dg_ref):
        dy = _dot_nt(_mx(dv_ref[...]), w_ref[...])
        ga, gb, dya, dyb = ga_ref[...], gb_ref[...], dy[:, :512], dy[:, 512:]
        dh_ref[...] = dya * _silu(ga)
        do_ref[...] = dyb * _silu(gb)
        dg_ref[:, :512] = _mx(dya * h_ref[...] * _dsilu(ga))
        dg_ref[:, 512:] = _mx(dyb * o_ref[...] * _dsilu(gb))

    half = pl.BlockSpec((t, 512), lambda i: (i, 0))
    half1 = pl.BlockSpec((t, 512), lambda i: (i, 1))
    full = pl.BlockSpec((t, 1024), lambda i: (i, 0))
    return pl.pallas_call(
        body, name="gate_bwd", grid=(SEQ // t,),
        in_specs=[full, pl.BlockSpec((D_MODEL, D_MODEL), lambda i: (0, 0)), half, half, half, half1],
        out_specs=[half, half, full],
        out_shape=[jax.ShapeDtypeStruct((SEQ, 512), F32), jax.ShapeDtypeStruct((SEQ, 512), F32),
                   jax.ShapeDtypeStruct((SEQ, 1024), MXU_DTYPE)],
        compiler_params=_params(("parallel",)),
    )(dv0, w_out, h, o, proj0, proj0)


CONV_T = 512
CONV_CB = 1024


def _ssd_conv_fwd(xbc, cw8, cb):
    t, cbk = CONV_T, CONV_CB
    tb = t // SUBLANES

    def body(x_ref, halo_ref, cw_ref, cb_ref, pre_ref, act_ref):
        halo = jnp.where(pl.program_id(1) > 0, halo_ref[...], 0.0)
        pre = _conv4(x_ref[...], halo, cw_ref[...], cb_ref[...])
        pre_ref[...] = pre
        act_ref[...] = _silu(pre)

    blk = pl.BlockSpec((t, cbk), lambda j, i: (i, j))
    return pl.pallas_call(
        body, name="ssd_conv_fwd", grid=(SSD_CONV // cbk, SEQ // t),
        in_specs=[blk, pl.BlockSpec((SUBLANES, cbk), lambda j, i: (jnp.maximum(i * tb - 1, 0), j)),
                  pl.BlockSpec((SUBLANES, cbk), lambda j, i: (0, j)), pl.BlockSpec((1, cbk), lambda j, i: (0, j))],
        out_specs=[blk, blk],
        out_shape=[jax.ShapeDtypeStruct((SEQ, SSD_CONV), F32), jax.ShapeDtypeStruct((SEQ, SSD_CONV), F32)],
        compiler_params=_params(("parallel", "parallel")),
    )(xbc, xbc, cw8, cb)


def _ssd_conv_bwd(dact, pre, xbc, cw8):
    t, cbk = CONV_T, CONV_CB
    tb = t // SUBLANES
    nb = SEQ // t

    def body(da_ref, dan_ref, pre_ref, pren_ref, x_ref, cw_ref, dx_ref, dcw_ref):
        i = pl.program_id(1)

        @pl.when(i == 0)
        def _():
            dcw_ref[...] = jnp.zeros_like(dcw_ref)

        dpre = da_ref[...] * _dsilu(pre_ref[...])
        dpre_next = jnp.where(i < nb - 1, dan_ref[...] * _dsilu(pren_ref[...]), 0.0)
        xblk = x_ref[...]
        cw = cw_ref[...]
        dx = dpre * cw[3:4]
        dcw_ref[3:4, :] += jnp.sum(dpre * xblk, axis=0, keepdims=True)
        for k in range(3):
            up = _shift_up(dpre, dpre_next, 3 - k)
            dcw_ref[k:k + 1, :] += jnp.sum(up * xblk, axis=0, keepdims=True)
            dx = dx + up * cw[k:k + 1]
        dcw_ref[4:5, :] += jnp.sum(dpre, axis=0, keepdims=True)
        dx_ref[...] = _mx(dx)

    blk = pl.BlockSpec((t, cbk), lambda j, i: (i, j))
    nxt = pl.BlockSpec((SUBLANES, cbk), lambda j, i: (jnp.minimum((i + 1) * tb, SEQ // SUBLANES - 1), j))
    acc = pl.BlockSpec((SUBLANES, cbk), lambda j, i: (0, j))
    return pl.pallas_call(
        body, name="ssd_conv_bwd", grid=(SSD_CONV // cbk, nb),
        in_specs=[blk, nxt, blk, nxt, blk, acc],
        out_specs=[blk, acc],
        out_shape=[jax.ShapeDtypeStruct((SEQ, SSD_CONV), MXU_DTYPE), jax.ShapeDtypeStruct((SUBLANES, SSD_CONV), F32)],
        compiler_params=_params(("parallel", "arbitrary")),
    )(dact, dact, pre, pre, xbc, cw8)


def _ssd_common(dt_raw, bias, alog, tril, expand_t, xs):
    lane = lax.broadcasted_iota(jnp.int32, dt_raw.shape, 1)
    dt = jnp.where(lane < SSD_HEADS, _softplus(dt_raw + bias), 0.0)
    a_neg = -jnp.exp(alog)
    cs = _dot_hi(tril, dt * a_neg)
    dt_x = _expand_heads(dt, expand_t)
    ecs_x = _expand_heads(jnp.exp(cs), expand_t)
    ds_x = _expand_heads(jnp.exp(cs[SSD_L - 1:SSD_L, :] - cs), expand_t)
    return dt, a_neg, cs, dt_x, None, xs * dt_x, ds_x, ecs_x, ecs_x[SSD_L - 1:SSD_L, :]


def _expand_heads(v, expand_t):
    hi = v.astype(jnp.bfloat16)
    lo = (v - hi.astype(F32)).astype(jnp.bfloat16)
    return _dot_nt(hi, expand_t) + _dot_nt(lo, expand_t)


def _fold_heads(v, expand_t):
    hi = v.astype(jnp.bfloat16)
    lo = (v - hi.astype(F32)).astype(jnp.bfloat16)
    return _dot(hi, expand_t) + _dot(lo, expand_t)


def _ssd_decay(cs, cs_t, hh, causal):
    seg = cs[:, hh:hh + 1] - cs_t[hh:hh + 1, :]
    return jnp.where(causal, jnp.exp(jnp.where(causal, seg, 0.0)), 0.0)


def _ssd_scan_fwd(act, dt_raw, bias, alog, d_x, tril, expand_t):
    nc = SEQ // SSD_L
    gw = SSD_INNER // SSD_GROUPS

    def body(act_ref, dt_ref, bias_ref, alog_ref, dx_ref, tril_ref, et_ref, y_ref, hp_ref, h_sc):
        @pl.when(pl.program_id(0) == 0)
        def _():
            h_sc[...] = jnp.zeros_like(h_sc)

        xs = act_ref[:, :SSD_INNER]
        _, _, cs, _, _, xdt, ds_x, ecs_x, elast = _ssd_common(
            dt_ref[...], bias_ref[...], alog_ref[...], tril_ref[...], et_ref[...], xs)
        cs_t = cs.T
        causal = (lax.broadcasted_iota(jnp.int32, (SSD_L, SSD_L), 0)
                  >= lax.broadcasted_iota(jnp.int32, (SSD_L, SSD_L), 1))
        lane = lax.broadcasted_iota(jnp.int32, (SSD_L, LANES), 1)
        xdt_b = _mx(xdt)
        xds_b = _mx(xdt * ds_x)
        hp_ref[0] = h_sc[...]
        for g in range(SSD_GROUPS):
            gs = slice(g * gw, (g + 1) * gw)
            bg = _mx(act_ref[:, SSD_INNER + g * SSD_N:SSD_INNER + (g + 1) * SSD_N])
            cg = _mx(act_ref[:, SSD_INNER + 512 + g * SSD_N:SSD_INNER + 512 + (g + 1) * SSD_N])
            cb = _dot_nt(cg, bg)
            hprev = h_sc[:, gs]
            yoff = _dot(cg, _mx(hprev)) * ecs_x[:, gs]
            h_sc[:, gs] = hprev * elast[:, gs] + _dot_tn(bg, xds_b[:, gs])
            for pr in range(4):
                ps = slice(g * gw + pr * LANES, g * gw + (pr + 1) * LANES)
                xp = xdt_b[:, ps]
                ydiag = jnp.zeros((SSD_L, LANES), F32)
                for j in range(2):
                    dm = _ssd_decay(cs, cs_t, g * 8 + pr * 2 + j, causal)
                    mine = (lane >= j * 64) & (lane < (j + 1) * 64)
                    ydiag = ydiag + _dot(_mx(cb * dm), jnp.where(mine, xp, jnp.zeros_like(xp)))
                y_ref[:, ps] = ydiag + yoff[:, pr * LANES:(pr + 1) * LANES] + dx_ref[:, ps] * xs[:, ps]

    const = lambda shape: pl.BlockSpec(shape, lambda c: (0, 0))
    return pl.pallas_call(
        body, name="ssd_scan_fwd", grid=(nc,),
        in_specs=[pl.BlockSpec((SSD_L, SSD_CONV), lambda c: (c, 0)), pl.BlockSpec((SSD_L, LANES), lambda c: (c, 0)),
                  const((1, LANES)), const((1, LANES)), const((1, SSD_INNER)), const((SSD_L, SSD_L)),
                  const((SSD_INNER, LANES))],
        out_specs=[pl.BlockSpec((SSD_L, SSD_INNER), lambda c: (c, 0)),
                   pl.BlockSpec((1, SSD_N, SSD_INNER), lambda c: (c, 0, 0))],
        out_shape=[jax.ShapeDtypeStruct((SEQ, SSD_INNER), F32), jax.ShapeDtypeStruct((nc, SSD_N, SSD_INNER), F32)],
        scratch_shapes=[pltpu.VMEM((SSD_N, SSD_INNER), F32)],
        compiler_params=_params(("arbitrary",)),
    )(act, dt_raw, bias, alog, d_x, tril, expand_t)


def _ssd_scan_bwd(dy, act, dt_raw, hprev_all, bias, alog, d_x, tril, expand_t):
    nc = SEQ // SSD_L
    gw = SSD_INNER // SSD_GROUPS

    def body(dy_ref, act_ref, dt_ref, hp_ref, bias_ref, alog_ref, dx_ref, tril_ref, et_ref,
             dact_ref, ddt_ref, dvec_ref, dh_sc, dd_sc):
        i = pl.program_id(0)

        @pl.when(i == 0)
        def _():
            dh_sc[...] = jnp.zeros_like(dh_sc)
            dd_sc[...] = jnp.zeros_like(dd_sc)
            dvec_ref[...] = jnp.zeros_like(dvec_ref)

        xs = act_ref[:, :SSD_INNER]
        dt_raw_v, bias_v = dt_ref[...], bias_ref[...]
        dt, a_neg, cs, dt_x, _, xdt, ds_x, ecs_x, elast = _ssd_common(
            dt_raw_v, bias_v, alog_ref[...], tril_ref[...], et_ref[...], xs)
        cs_t = cs.T
        rowi = lax.broadcasted_iota(jnp.int32, (SSD_L, SSD_L), 0)
        coli = lax.broadcasted_iota(jnp.int32, (SSD_L, SSD_L), 1)
        causal = rowi >= coli
        lane = lax.broadcasted_iota(jnp.int32, (SSD_L, LANES), 1)
        row_g = lax.broadcasted_iota(jnp.int32, (SSD_L, gw), 0)
        dyv = dy_ref[...]
        dd_sc[0:1, :] += jnp.sum(dyv * xs, axis=0, keepdims=True)
        xdt_b = _mx(xdt)
        xds = xdt * ds_x
        xds_b = _mx(xds)
        dy_b = _mx(dyv)
        dye_b = _mx(dyv * ecs_x)
        dcs = jnp.zeros((SSD_L, LANES), F32)
        dcs_t = jnp.zeros((LANES, SSD_L), F32)
        dcs_parts = []
        dxdt_parts = []
        for g in range(SSD_GROUPS):
            gs = slice(g * gw, (g + 1) * gw)
            bcol = slice(SSD_INNER + g * SSD_N, SSD_INNER + (g + 1) * SSD_N)
            ccol = slice(SSD_INNER + 512 + g * SSD_N, SSD_INNER + 512 + (g + 1) * SSD_N)
            bg, cg = _mx(act_ref[:, bcol]), _mx(act_ref[:, ccol])
            cb = _dot_nt(cg, bg)
            hp = hp_ref[0, :, gs]
            hp_b = _mx(hp)
            dh = dh_sc[:, gs]
            dh_b = _mx(dh)
            yoff = _dot(cg, hp_b) * ecs_x[:, gs]
            bdh = _dot(bg, dh_b)
            tt = xds[:, gs] * bdh
            last_row = (jnp.sum(tt, axis=0, keepdims=True)
                        + jnp.sum(dh * hp, axis=0, keepdims=True) * elast[:, gs])
            dcs_parts.append(dyv[:, gs] * yoff - tt + jnp.where(row_g == SSD_L - 1, last_row, 0.0))
            dc_g = _dot_nt(dye_b[:, gs], hp_b)
            db_g = _dot_nt(xds_b[:, gs], dh_b)
            dh_sc[:, gs] = _dot_tn(cg, dye_b[:, gs]) + dh * elast[:, gs]
            wsum = jnp.zeros((SSD_L, SSD_L), F32)
            dxdt_g = []
            for pr in range(4):
                ps = slice(g * gw + pr * LANES, g * gw + (pr + 1) * LANES)
                xp, dyp = xdt_b[:, ps], dy_b[:, ps]
                dxp = jnp.zeros((SSD_L, LANES), F32)
                for j in range(2):
                    hh = g * 8 + pr * 2 + j
                    dm = _ssd_decay(cs, cs_t, hh, causal)
                    mine = (lane >= j * 64) & (lane < (j + 1) * 64)
                    dy_h = jnp.where(mine, dyp, jnp.zeros_like(dyp))
                    wd = _dot_nt(dy_h, xp) * dm
                    wsum = wsum + wd
                    gmat = wd * cb
                    dcs = dcs + jnp.where(lane == hh, jnp.sum(gmat, axis=1, keepdims=True), 0.0)
                    dcs_t = dcs_t - jnp.where(rowi == hh, jnp.sum(gmat, axis=0, keepdims=True), 0.0)
                    dxp = dxp + _dot_tn(_mx(cb * dm), dy_h)
                dxdt_g.append(dxp)
            dxdt_parts.append(jnp.concatenate(dxdt_g, axis=1) + bdh * ds_x[:, gs])
            ws_b = _mx(wsum)
            dact_ref[:, ccol] = dc_g + _dot(ws_b, bg)
            dact_ref[:, bcol] = db_g + _dot_tn(ws_b, cg)
        dxdt = jnp.concatenate(dxdt_parts, axis=1)
        dcs_x = jnp.concatenate(dcs_parts, axis=1)
        et = et_ref[...]
        dcs_tot = dcs + dcs_t.T + _fold_heads(dcs_x, et)
        da_dt = _dot_hi((coli >= rowi).astype(F32), dcs_tot)
        ddt = da_dt * a_neg + _fold_heads(dxdt * xs, et)
        ddt_raw = ddt * _sigmoid(dt_raw_v + bias_v)
        ddt_ref[...] = ddt_raw
        dvec_ref[0:1, :] += jnp.sum(ddt_raw, axis=0, keepdims=True)
        dvec_ref[1:2, :] += jnp.sum(da_dt * dt, axis=0, keepdims=True) * a_neg
        dact_ref[:, :SSD_INNER] = dyv * dx_ref[...] + dxdt * dt_x

        @pl.when(i == nc - 1)
        def _():
            dvec_ref[2:3, :] = _fold_heads(dd_sc[...], et)[0:1, :]

    const = lambda shape: pl.BlockSpec(shape, lambda c: (0, 0))
    rev = lambda c: (nc - 1 - c, 0)
    return pl.pallas_call(
        body, name="ssd_scan_bwd", grid=(nc,),
        in_specs=[pl.BlockSpec((SSD_L, SSD_INNER), rev), pl.BlockSpec((SSD_L, SSD_CONV), rev),
                  pl.BlockSpec((SSD_L, LANES), rev),
                  pl.BlockSpec((1, SSD_N, SSD_INNER), lambda c: (nc - 1 - c, 0, 0)),
                  const((1, LANES)), const((1, LANES)), const((1, SSD_INNER)), const((SSD_L, SSD_L)),
                  const((SSD_INNER, LANES))],
        out_specs=[pl.BlockSpec((SSD_L, SSD_CONV), rev), pl.BlockSpec((SSD_L, LANES), rev), const((SUBLANES, LANES))],
        out_shape=[jax.ShapeDtypeStruct((SEQ, SSD_CONV), F32), jax.ShapeDtypeStruct((SEQ, LANES), F32),
                   jax.ShapeDtypeStruct((SUBLANES, LANES), F32)],
        scratch_shapes=[pltpu.VMEM((SSD_N, SSD_INNER), F32), pltpu.VMEM((SUBLANES, SSD_INNER), F32)],
        compiler_params=_params(("arbitrary",)),
    )(dy, act, dt_raw, hprev_all, bias, alog, d_x, tril, expand_t)


L1_T = 256


def _gated_norm(y, z, nw):
    y2 = y * _silu(z)
    gw = SSD_INNER // SSD_GROUPS
    outs, xhs, rss = [], [], []
    for g in range(SSD_GROUPS):
        gs = slice(g * gw, (g + 1) * gw)
        v = y2[:, gs]
        rs = lax.rsqrt(jnp.mean(v * v, axis=-1, keepdims=True) + 1e-6)
        xhs.append(v * rs)
        rss.append(rs)
        outs.append(v * rs * nw[:, gs])
    return outs, xhs, rss


def _l1_out(y, z, nw, w_out, x1, g, b, target):
    t = L1_T

    def body(y_ref, z_ref, nw_ref, w_ref, x1_ref, g_ref, b_ref, tg_ref, yn_ref, dv_ref, dgb_ref, loss_ref):
        @pl.when(pl.program_id(0) == 0)
        def _():
            dgb_ref[...] = jnp.zeros_like(dgb_ref)
            loss_ref[...] = jnp.zeros_like(loss_ref)

        outs, _, _ = _gated_norm(y_ref[...], z_ref[...], nw_ref[...])
        yn = _mx(jnp.concatenate(outs, axis=1))
        yn_ref[...] = yn
        v = DN_ALPHA * x1_ref[...] + _dot(yn, w_ref[...])
        gv = g_ref[...]
        err = _ln(v, gv, b_ref[...]) - tg_ref[...]
        rowsum = jnp.sum(err * err, axis=1, keepdims=True)
        loss_ref[...] += 0.5 * jnp.sum(rowsum, axis=0, keepdims=True) / D_MODEL
        dv, dg, db = _ln_bwd(v, gv, err / D_MODEL)
        dv_ref[...] = dv
        dgb_ref[0:1, :] += dg
        dgb_ref[1:2, :] += db

    wide = pl.BlockSpec((t, SSD_INNER), lambda i: (i, 0))
    full = pl.BlockSpec((t, D_MODEL), lambda i: (i, 0))
    vec = pl.BlockSpec((1, D_MODEL), lambda i: (0, 0))
    return pl.pallas_call(
        body, name="l1_out", grid=(SEQ // t,),
        in_specs=[wide, wide, pl.BlockSpec((1, SSD_INNER), lambda i: (0, 0)),
                  pl.BlockSpec((SSD_INNER, D_MODEL), lambda i: (0, 0)), full, vec, vec, full],
        out_specs=[wide, full, pl.BlockSpec((SUBLANES, D_MODEL), lambda i: (0, 0)),
                   pl.BlockSpec((SUBLANES, LANES), lambda i: (0, 0))],
        out_shape=[jax.ShapeDtypeStruct((SEQ, SSD_INNER), MXU_DTYPE), jax.ShapeDtypeStruct((SEQ, D_MODEL), F32),
                   jax.ShapeDtypeStruct((SUBLANES, D_MODEL), F32), jax.ShapeDtypeStruct((SUBLANES, LANES), F32)],
        compiler_params=_params(("arbitrary",)),
    )(y, z, nw, w_out, x1, g, b, target)


def _l1_gate_bwd(dv1, w_out, y, z, nw):
    t = L1_T
    gw = SSD_INNER // SSD_GROUPS

    def body(dv_ref, w_ref, y_ref, z_ref, nw_ref, dy_ref, dz_ref, dnw_ref):
        @pl.when(pl.program_id(0) == 0)
        def _():
            dnw_ref[...] = jnp.zeros_like(dnw_ref)

        dyn = _dot_nt(_mx(dv_ref[...]), w_ref[...])
        yv, zv, nwv = y_ref[...], z_ref[...], nw_ref[...]
        _, xhs, rss = _gated_norm(yv, zv, nwv)
        sz, dsz = _silu(zv), _dsilu(zv)
        for g in range(SSD_GROUPS):
            gs = slice(g * gw, (g + 1) * gw)
            d_out = dyn[:, gs]
            xh = xhs[g]
            dnw_ref[0:1, gs] += jnp.sum(d_out * xh, axis=0, keepdims=True)
            dxh = d_out * nwv[:, gs]
            dy2 = rss[g] * (dxh - xh * jnp.mean(dxh * xh, axis=-1, keepdims=True))
            dy_ref[:, gs] = dy2 * sz[:, gs]
            dz_ref[:, gs] = _mx(dy2 * yv[:, gs] * dsz[:, gs])

    wide = pl.BlockSpec((t, SSD_INNER), lambda i: (i, 0))
    return pl.pallas_call(
        body, name="l1_gate_bwd", grid=(SEQ // t,),
        in_specs=[pl.BlockSpec((t, D_MODEL), lambda i: (i, 0)), pl.BlockSpec((SSD_INNER, D_MODEL), lambda i: (0, 0)),
                  wide, wide, pl.BlockSpec((1, SSD_INNER), lambda i: (0, 0))],
        out_specs=[wide, wide, pl.BlockSpec((SUBLANES, SSD_INNER), lambda i: (0, 0))],
        out_shape=[jax.ShapeDtypeStruct((SEQ, SSD_INNER), F32), jax.ShapeDtypeStruct((SEQ, SSD_INNER), MXU_DTYPE),
                   jax.ShapeDtypeStruct((SUBLANES, SSD_INNER), F32)],
        compiler_params=_params(("arbitrary",)),
    )(dv1, w_out, y, z, nw)


MESH = pl.DeviceIdType.MESH
ANY = pl.BlockSpec(memory_space=pl.ANY)


def _flip(v, bit):
    return 1 - v if bit else v


def _all_gather(blocks, name):
    n = len(blocks)

    def body(*refs):
        x_refs, out_refs = refs[:n], refs[n:2 * n]
        send_sems, recv_sems, local_sems = refs[2 * n:]
        mx, my, mc = lax.axis_index("x"), lax.axis_index("y"), lax.axis_index("c")
        me, sibling = (mx, my, mc), (mx, my, 1 - mc)
        chips = [(1 - mx, my), (mx, 1 - my), (1 - mx, 1 - my)]

        def copy(a, k, block, to, own=False):
            px, py, pc = block
            slot = out_refs[a].at[4 * px + 2 * py + pc]
            return pltpu.make_async_remote_copy(
                src_ref=x_refs[a] if own else slot, dst_ref=slot,
                send_sem=send_sems.at[7 * a + k], recv_sem=recv_sems.at[7 * a + k], device_id=to, device_id_type=MESH)

        mine = [pltpu.make_async_copy(x_refs[a], out_refs[a].at[4 * mx + 2 * my + mc], local_sems.at[a])
                for a in range(n)]
        first = []
        for a in range(n):
            mine[a].start()
            first.append(copy(a, 0, me, sibling, own=True))
            first += [copy(a, 1 + j, me, (*chip, mc), own=True) for j, chip in enumerate(chips)]
        for cp in first:
            cp.start()
        passed = []
        for j, chip in enumerate(chips):
            for a in range(n):
                copy(a, 1 + j, (*chip, mc), me).wait_recv()
                fwd = copy(a, 4 + j, (*chip, mc), sibling)
                fwd.start()
                passed.append(fwd)
        for a in range(n):
            copy(a, 0, sibling, me).wait_recv()
            for j, chip in enumerate(chips):
                copy(a, 4 + j, (*chip, 1 - mc), me).wait_recv()
        for cp in first + passed:
            cp.wait_send()
        for cp in mine:
            cp.wait()

    return pl.pallas_call(
        body, name=name, in_specs=[ANY] * n, out_specs=[ANY] * n,
        out_shape=[jax.ShapeDtypeStruct((N_DEV,) + b.shape, b.dtype) for b in blocks],
        scratch_shapes=[pltpu.SemaphoreType.DMA((7 * n,)), pltpu.SemaphoreType.DMA((7 * n,)),
                        pltpu.SemaphoreType.DMA((n,))],
    )(*blocks)


def _exchange(scatter, bcast, name):
    n = len(scatter) + len(bcast)

    def body(*refs):
        copies = _peer_copies(refs[:n], refs[n:2 * n], refs[2 * n:], len(scatter))
        for cp in copies:
            cp.start()
        for cp in copies:
            cp.wait()

    return pl.pallas_call(
        body, name=name, in_specs=[ANY] * n, out_specs=[ANY] * n,
        out_shape=_exchange_shapes(scatter, bcast), scratch_shapes=_exchange_sems(n),
    )(*scatter, *bcast)


def _exchange_shapes(scatter, bcast):
    return ([jax.ShapeDtypeStruct(a.shape, a.dtype) for a in scatter]
            + [jax.ShapeDtypeStruct((N_DEV,) + a.shape, a.dtype) for a in bcast])


def _exchange_sems(n):
    return [pltpu.SemaphoreType.DMA((7 * n,)), pltpu.SemaphoreType.DMA((7 * n,)), pltpu.SemaphoreType.DMA((n,))]


def _peer_copies(in_refs, out_refs, sems, n_scatter):
    send_sems, recv_sems, local_sems = sems
    n = len(in_refs)
    mx, my, mc = lax.axis_index("x"), lax.axis_index("y"), lax.axis_index("c")
    me = 4 * mx + 2 * my + mc

    def src(a, slot):
        return in_refs[a].at[slot] if a < n_scatter else in_refs[a]

    copies = [pltpu.make_async_copy(src(a, me), out_refs[a].at[me], local_sems.at[a]) for a in range(n)]
    for k in range(1, N_DEV):
        px, py, pc = _flip(mx, (k >> 2) & 1), _flip(my, (k >> 1) & 1), _flip(mc, k & 1)
        for a in range(n):
            copies.append(pltpu.make_async_remote_copy(
                src_ref=src(a, 4 * px + 2 * py + pc), dst_ref=out_refs[a].at[me],
                send_sem=send_sems.at[7 * a + k - 1], recv_sem=recv_sems.at[7 * a + k - 1],
                device_id=(px, py, pc), device_id_type=MESH))
    return copies


def _segments(col_map, width):
    segs = []
    for lo, hi, arr, alo in col_map:
        for s in range(N_DEV):
            a, b = max(lo, s * width), min(hi, (s + 1) * width)
            if a < b:
                segs.append((s, a - s * width, b - a, arr, alo + a - lo))
    return segs


COPY_ROWS = 256


def _unshard(g8, col_map, widths, name):
    _, r, w = g8.shape
    rb = min(r, COPY_ROWS)
    segs = _segments(col_map, w)

    def body(g_ref, *o_refs):
        for o_ref in o_refs:
            o_ref[...] = jnp.zeros_like(o_ref)
        for s, llo, n, arr, alo in segs:
            o_refs[arr][:, alo:alo + n] = g_ref[s, :, llo:llo + n]

    return pl.pallas_call(
        body, name=name, grid=(r // rb,),
        in_specs=[pl.BlockSpec((N_DEV, rb, w), lambda i: (0, i, 0))],
        out_specs=[pl.BlockSpec((rb, n), lambda i: (i, 0)) for n in widths],
        out_shape=[jax.ShapeDtypeStruct((r, n), g8.dtype) for n in widths],
        compiler_params=_params(("parallel",)),
    )(g8)


def _reshard(srcs, col_map, w, dtype, name):
    r = srcs[0].shape[0]
    rb = min(r, COPY_ROWS)
    segs = _segments(col_map, w)

    def body(*refs):
        o_ref = refs[-1]
        for s, llo, n, arr, alo in segs:
            o_ref[s, :, llo:llo + n] = refs[arr][:, alo:alo + n].astype(dtype)

    return pl.pallas_call(
        body, name=name, grid=(r // rb,),
        in_specs=[pl.BlockSpec((rb, a.shape[1]), lambda i: (i, 0)) for a in srcs],
        out_specs=pl.BlockSpec((N_DEV, rb, w), lambda i: (0, i, 0)),
        out_shape=jax.ShapeDtypeStruct((N_DEV, r, w), dtype),
        compiler_params=_params(("parallel",)),
    )(*srcs)


def _adamw(parts, w, m, v, name):
    r, c = w.shape
    tr = COPY_ROWS if r % COPY_ROWS == 0 else r

    def body(p_ref, w_ref, m_ref, v_ref, g_ref, d_ref, mo_ref, vo_ref):
        g = p_ref[0].astype(F32)
        for s in range(1, N_DEV):
            g = g + p_ref[s].astype(F32)
        g_ref[...] = g
        d_ref[...], mo_ref[...], vo_ref[...] = _adamw_math(g, w_ref[...], m_ref[...], v_ref[...])

    blk = pl.BlockSpec((tr, c), lambda i: (i, 0))
    out = jax.ShapeDtypeStruct((r, c), F32)
    return pl.pallas_call(
        body, name=name, grid=(r // tr,),
        in_specs=[pl.BlockSpec((N_DEV, tr, c), lambda i: (0, i, 0)), blk, blk, blk],
        out_specs=[blk, blk, blk, blk], out_shape=[out, out, out, out],
        compiler_params=_params(("parallel",)),
    )(parts, w, m, v)


def _adamw_math(g, w, m, v):
    mn = ADAM_B1 * m + (1.0 - ADAM_B1) * g
    vn = ADAM_B2 * v + (1.0 - ADAM_B2) * (g * g)
    m_hat = mn / (1.0 - ADAM_B1 ** ADAM_STEP)
    v_hat = vn / (1.0 - ADAM_B2 ** ADAM_STEP)
    return -ADAM_LR * (m_hat / (jnp.sqrt(v_hat) + ADAM_EPS) + ADAM_WD * w), mn, vn


SMALL = (("ab_conv_w", 0, 4, 64), ("ssd_conv_w", 4, 4, 384), ("ssd_conv_b", 8, 1, 384), ("ssd_norm", 9, 1, 256),
         ("ssd_ln_g", 10, 1, 128), ("ssd_ln_b", 11, 1, 128))
VECS = (("ab_conv_b", 512), ("ab_gate_a_b", 512), ("ab_gate_x_b", 512), ("ab_lambda", 512), ("mla_q_norm", 256),
        ("mla_kv_norm", 128), ("ab_ln_g", 1024), ("ab_ln_b", 1024), ("ssd_dt_bias", 32), ("ssd_a_log", 32),
        ("ssd_d", 32))
GATES = ("ab_gate_a_w", "ab_gate_x_w")
SMALL_NAMES = tuple(n for n, *_ in SMALL) + tuple(n for n, _ in VECS) + GATES
VMEM_WHOLE = pl.BlockSpec(memory_space=pltpu.VMEM)


def _view2d(name, a):
    if name in GATES:
        return a.reshape(RNN_W, 64)
    return a[0] if a.ndim == 3 else a


def _unshard_small(g):
    widths = (512, 3072, 3072, 2048, 1024, 1024)

    def body(*refs):
        ins, outs = refs[:6], refs[6:]
        outs[0][...] = jnp.zeros_like(outs[0])
        outs[1][...] = jnp.zeros_like(outs[1])
        for (_, _, nr, c), i_ref, o_ref in zip(SMALL, ins, outs):
            for j in range(N_DEV):
                o_ref[0:nr, j * c:(j + 1) * c] = i_ref[j]

    return pl.pallas_call(
        body, name="unshard_small", in_specs=[VMEM_WHOLE] * 6, out_specs=[VMEM_WHOLE] * 6,
        out_shape=[jax.ShapeDtypeStruct((SUBLANES if nr == 4 else 1, w), F32) for (_, _, nr, _), w in zip(SMALL, widths)],
    )(*g)


def _prep_repl(ga, gx, dt_bias, a_log, d):
    def body(ga_ref, gx_ref, b_ref, al_ref, d_ref, wa_ref, wx_ref, b128_ref, al128_ref, dx_ref):
        wa_ref[...] = jnp.zeros_like(wa_ref)
        wx_ref[...] = jnp.zeros_like(wx_ref)
        for hd in range(8):
            hs = slice(hd * 64, (hd + 1) * 64)
            wa_ref[hs, hs] = _mx(ga_ref[hs, :])
            wx_ref[hs, hs] = _mx(gx_ref[hs, :])
        b128_ref[...] = jnp.zeros_like(b128_ref)
        al128_ref[...] = jnp.zeros_like(al128_ref)
        b128_ref[:, 0:SSD_HEADS] = b_ref[...]
        al128_ref[:, 0:SSD_HEADS] = al_ref[...]
        dv = d_ref[...]
        for hd in range(SSD_HEADS):
            dx_ref[:, hd * SSD_P:(hd + 1) * SSD_P] = jnp.broadcast_to(dv[:, hd:hd + 1], (1, SSD_P))

    return pl.pallas_call(
        body, name="prep_repl", in_specs=[VMEM_WHOLE] * 5, out_specs=[VMEM_WHOLE] * 5,
        out_shape=[jax.ShapeDtypeStruct((RNN_W, RNN_W), MXU_DTYPE), jax.ShapeDtypeStruct((RNN_W, RNN_W), MXU_DTYPE),
                   jax.ShapeDtypeStruct((1, LANES), F32), jax.ShapeDtypeStruct((1, LANES), F32),
                   jax.ShapeDtypeStruct((1, SSD_INNER), F32)],
    )(ga, gx, dt_bias, a_log, d)


def _pack_small(dvec0, g_wa, g_wx, dqnw, dknw, dgb0, dvec1, dcw1, dnw, dgb1):
    def body(dvec0_ref, gwa_ref, gwx_ref, dqn_ref, dkn_ref, dgb0_ref, dvec1_ref, dcw1_ref, dnw_ref, dgb1_ref,
             sm_ref, vec_ref, ga_ref, gx_ref):
        sm_ref[...] = jnp.zeros_like(sm_ref)
        vec_ref[...] = jnp.zeros_like(vec_ref)
        sharded = ((dvec0_ref, 4), (dcw1_ref, 0), (dcw1_ref, 4), (dnw_ref, 0), (dgb1_ref, 0), (dgb1_ref, 1))
        for (_, r0, nr, c), (src, sr) in zip(SMALL, sharded):
            for j in range(N_DEV):
                sm_ref[j, r0:r0 + nr, 0:c] = src[sr:sr + nr, j * c:(j + 1) * c]
        vectors = ((dvec0_ref, 3), (dvec0_ref, 0), (dvec0_ref, 1), (dvec0_ref, 2), (dqn_ref, 0), (dkn_ref, 0),
                   (dgb0_ref, 0), (dgb0_ref, 1), (dvec1_ref, 0), (dvec1_ref, 1), (dvec1_ref, 2))
        for row, ((_, c), (src, sr)) in enumerate(zip(VECS, vectors)):
            vec_ref[row:row + 1, 0:c] = src[sr:sr + 1, 0:c]
        for hd in range(8):
            hs = slice(hd * 64, (hd + 1) * 64)
            ga_ref[hs, :] = gwa_ref[hs, hs]
            gx_ref[hs, :] = gwx_ref[hs, hs]

    return pl.pallas_call(
        body, name="pack_small", in_specs=[VMEM_WHOLE] * 10, out_specs=[VMEM_WHOLE] * 4,
        out_shape=[jax.ShapeDtypeStruct((N_DEV, 16, 384), F32), jax.ShapeDtypeStruct((16, 1024), F32),
                   jax.ShapeDtypeStruct((RNN_W, 64), F32), jax.ShapeDtypeStruct((RNN_W, 64), F32)],
    )(dvec0, g_wa, g_wx, dqnw, dknw, dgb0, dvec1, dcw1, dnw, dgb1)


def _adamw_small(recv_sm, recv_vec, recv_ga, recv_gx, wmv):
    plan = ([(0, r0, nr, c) for _, r0, nr, c in SMALL] + [(1, row, 1, c) for row, (_, c) in enumerate(VECS)]
            + [(2, 0, RNN_W, 64), (3, 0, RNN_W, 64)])
    n = len(plan)

    def body(*refs):
        recv, ins, outs = refs[:4], refs[4:4 + 3 * n], refs[4 + 3 * n:]
        for i, (src, r0, nr, c) in enumerate(plan):
            g = recv[src][0, r0:r0 + nr, 0:c]
            for s in range(1, N_DEV):
                g = g + recv[src][s, r0:r0 + nr, 0:c]
            w_ref, m_ref, v_ref = ins[3 * i:3 * i + 3]
            outs[4 * i][...] = g
            outs[4 * i + 1][...], outs[4 * i + 2][...], outs[4 * i + 3][...] = _adamw_math(
                g, w_ref[...], m_ref[...], v_ref[...])

    flat = [a for t in wmv for a in t]
    return pl.pallas_call(
        body, name="adamw_small", in_specs=[VMEM_WHOLE] * (4 + 3 * n), out_specs=[VMEM_WHOLE] * (4 * n),
        out_shape=[jax.ShapeDtypeStruct(t[0].shape, F32) for t in wmv for _ in range(4)],
    )(recv_sm, recv_vec, recv_ga, recv_gx, *flat)


BIG_L0 = ("ab_w_in", "ab_w_out", "mla_w_uq", "mla_w_ukv")
BIG_L1 = ("ssd_w_in", "ssd_w_out")

MAP_W0 = ((0, 512, 0, 1024), (512, 1536, 0, 0), (1536, 1920, 0, 1536), (1920, 1952, 0, 1984))
MAP_W1 = ((0, 2048, 0, 0), (2048, 5120, 1, 0), (5120, 5152, 2, 0))
MAP_WQ = tuple((96 * hd, 96 * hd + 96, 0, 128 * hd) for hd in range(8))
MAP_WKV = (tuple((128 * hd, 128 * hd + 64, 0, 128 * hd) for hd in range(8))
           + tuple((128 * hd + 64, 128 * hd + 128, 0, 1024 + 64 * hd) for hd in range(8)))
MAP_G0 = ((0, 512, 0, 0), (512, 1536, 1, 0), (1536, 1920, 2, 0), (1920, 1952, 2, 448))


def kernel(x, positions, ab_w_in, ab_conv_w, ab_conv_b, ab_gate_a_w, ab_gate_a_b, ab_gate_x_w, ab_gate_x_b, ab_lambda, mla_q_norm, mla_kv_norm, mla_w_uq, mla_w_ukv, ab_w_out, ab_ln_g, ab_ln_b, ssd_w_in, ssd_conv_w, ssd_conv_b, ssd_dt_bias, ssd_a_log, ssd_d, ssd_norm, ssd_w_out, ssd_ln_g, ssd_ln_b, loss_target, m_ab_w_in, m_ab_conv_w, m_ab_conv_b, m_ab_gate_a_w, m_ab_gate_a_b, m_ab_gate_x_w, m_ab_gate_x_b, m_ab_lambda, m_mla_q_norm, m_mla_kv_norm, m_mla_w_uq, m_mla_w_ukv, m_ab_w_out, m_ab_ln_g, m_ab_ln_b, m_ssd_w_in, m_ssd_conv_w, m_ssd_conv_b, m_ssd_dt_bias, m_ssd_a_log, m_ssd_d, m_ssd_norm, m_ssd_w_out, m_ssd_ln_g, m_ssd_ln_b, v_ab_w_in, v_ab_conv_w, v_ab_conv_b, v_ab_gate_a_w, v_ab_gate_a_b, v_ab_gate_x_w, v_ab_gate_x_b, v_ab_lambda, v_mla_q_norm, v_mla_kv_norm, v_mla_w_uq, v_mla_w_ukv, v_ab_w_out, v_ab_ln_g, v_ab_ln_b, v_ssd_w_in, v_ssd_conv_w, v_ssd_conv_b, v_ssd_dt_bias, v_ssd_a_log, v_ssd_d, v_ssd_norm, v_ssd_w_out, v_ssd_ln_g, v_ssd_ln_b):
    args = dict(locals())
    bf = MXU_DTYPE
    big = {n: [args[pre + n][0] for pre in ("", "m_", "v_")] for n in BIG_L0 + BIG_L1}
    sml = {n: [_view2d(n, args[pre + n]) for pre in ("", "m_", "v_")] for n in SMALL_NAMES}

    gathered = _all_gather([big[n][0].astype(bf) for n in BIG_L0] + [sml[n][0] for n, *_ in SMALL], "gather_params")
    g8 = dict(zip(BIG_L0, gathered))
    p = {"wo0": g8["ab_w_out"].reshape(D_MODEL, D_MODEL)}
    p["w0p"], = _unshard(g8["ab_w_in"], MAP_W0, (2048,), "unshard_w0")
    p["wq"], = _unshard(g8["mla_w_uq"], MAP_WQ, (1024,), "unshard_wq")
    p["wkv"], = _unshard(g8["mla_w_ukv"], MAP_WKV, (1536,), "unshard_wkv")
    p["cw0"], p["cw1"], p["cb1"], p["nw"], p["g1"], p["b1"] = _unshard_small(gathered[len(BIG_L0):])
    p["wa"], p["wx"], p["dt_bias"], p["a_log"], p["d_x"] = _prep_repl(
        sml["ab_gate_a_w"][0], sml["ab_gate_x_w"][0], sml["ssd_dt_bias"][0], sml["ssd_a_log"][0], sml["ssd_d"][0])
    for key, n in (("cb0", "ab_conv_b"), ("ba", "ab_gate_a_b"), ("bx", "ab_gate_x_b"), ("lam", "ab_lambda"),
                   ("qn_w", "mla_q_norm"), ("kn_w", "mla_kv_norm"), ("g0", "ab_ln_g"), ("b0", "ab_ln_b")):
        p[key] = sml[n][0]

    acc, recv_l1, loss_part, grad_x = _local_step(
        x[0], positions[0], loss_target[0], p, [big[n][0].astype(bf) for n in BIG_L1])

    send = [_reshard([acc["g_rnn"], acc["g_gate"], acc["g_tail"]], MAP_G0, 244, bf, "reshard_w0"),
            acc["g_wo0"].astype(bf).reshape(N_DEV, 128, D_MODEL),
            _reshard([acc["g_wq"]], MAP_WQ, 96, bf, "reshard_wq"), _reshard([acc["g_wkv"]], MAP_WKV, 128, bf, "reshard_wkv")]
    sm_slots, vec_rows, ga, gx = _pack_small(*(acc[k] for k in (
        "dvec0", "g_wa", "g_wx", "dqnw", "dknw", "dgb0", "dvec1", "dcw1", "dnw", "dgb1")))
    recv = _exchange(send + [sm_slots], [vec_rows, ga, gx], "exchange_grads")

    outs = {}
    kinds = ("grad", "delta", "new_m", "new_v")
    for n, parts in zip(BIG_L0 + BIG_L1, list(recv[:4]) + list(recv_l1)):
        for kind, res in zip(kinds, _adamw(parts, *big[n], "adamw_" + n)):
            outs[kind, n] = res[None]
    res = _adamw_small(*recv[4:], [sml[n] for n in SMALL_NAMES])
    for i, n in enumerate(SMALL_NAMES):
        for k, kind in enumerate(kinds):
            outs[kind, n] = res[4 * i + k].reshape(args[n].shape)

    loss = lax.psum(loss_part, ("x", "y", "c"))
    order = ["ab_w_in", "ab_conv_w", "ab_conv_b", "ab_gate_a_w", "ab_gate_a_b", "ab_gate_x_w", "ab_gate_x_b",
             "ab_lambda", "mla_q_norm", "mla_kv_norm", "mla_w_uq", "mla_w_ukv", "ab_w_out", "ab_ln_g", "ab_ln_b",
             "ssd_w_in", "ssd_conv_w", "ssd_conv_b", "ssd_dt_bias", "ssd_a_log", "ssd_d", "ssd_norm", "ssd_w_out",
             "ssd_ln_g", "ssd_ln_b"]
    return (loss, grad_x[None], *[outs[kind, n] for kind in ("grad", "delta", "new_m", "new_v") for n in order])


def _local_step(x, pos, target, p, l1_blocks):
    bf = MXU_DTYPE
    inv_freq = 10000.0 ** (-jnp.arange(0, 32, 2, dtype=F32) / 32)
    ang = pos.astype(F32)[:, None] * inv_freq
    cos, sin = jnp.cos(ang), jnp.sin(ang)
    zeros = lambda n: jnp.zeros((SEQ, n), F32)
    tc = jnp.concatenate([jnp.ones((SEQ, 64), F32), cos, cos, zeros(32)], axis=1)
    tsa = jnp.concatenate([zeros(64), -sin, zeros(48)], axis=1)
    tsb = jnp.concatenate([zeros(80), sin, zeros(32)], axis=1)

    w0p, wq, wkv, wo0, wa, wxg = (p[k] for k in ("w0p", "wq", "wkv", "wo0", "wa", "wx"))
    cw0, cb0, ba, bx, lam = (p[k] for k in ("cw0", "cb0", "ba", "bx", "lam"))
    qn_w, kn_w, g0, b0 = (p[k] for k in ("qn_w", "kn_w", "g0", "b0"))
    cw1, cb1, dt_bias, a_log, d_x, nw, g1, b1 = (p[k] for k in ("cw1", "cb1", "dt_bias", "a_log", "d_x", "nw", "g1", "b1"))
    tril = jnp.tril(jnp.ones((SSD_L, SSD_L), F32))
    expand_t = (jnp.arange(SSD_INNER)[:, None] // SSD_P == jnp.arange(LANES)[None, :]).astype(jnp.bfloat16)

    xb = x.astype(bf)
    proj0 = _mm(xb, w0p, "nn", name="l0_in")
    xc, h = _rglru_fwd(proj0, cw0, cb0, wa, ba, wxg, bx, lam)
    qn, kn, qc, kc, vc = _mla_fwd(proj0, qn_w, kn_w, wq, wkv, tc, tsa, tsb)
    o, lse, (w1_8, wo1_8) = _flash_fwd(qc, kc, vc, bcast=l1_blocks)
    w1z, w1x, w1d = _unshard(w1_8, MAP_W1, (2048, 3072, 128), "unshard_w1")
    wo1 = wo1_8.reshape(SSD_INNER, D_MODEL)
    y0, v0, x1, x1b = _l0_out(h, o, proj0, x, wo0, g0, b0)

    z = _mm(x1b, w1z, "nn", name="l1_in_z")
    xbc = _mm(x1b, w1x, "nn", name="l1_in_xbc")
    dt_raw = _mm(x1b, w1d, "nn", name="l1_in_dt")
    pre, act = _ssd_conv_fwd(xbc, cw1, cb1)
    ys, hprev = _ssd_scan_fwd(act, dt_raw, dt_bias, a_log, d_x, tril, expand_t)
    yn, dv1, dgb1, loss8 = _l1_out(ys, z, nw, wo1, x1, g1, b1, target)

    g_wo1 = _mm(yn, dv1, "tn", name="l1_dwout")
    dys, dz, dnw = _l1_gate_bwd(dv1, wo1, ys, z, nw)
    dact, ddt_raw, dvec1 = _ssd_scan_bwd(dys, act, dt_raw, hprev, dt_bias, a_log, d_x, tril, expand_t)
    dxbc, dcw1 = _ssd_conv_bwd(dact, pre, xbc, cw1)
    g_z, g_xbc = _mm(x1b, dz, "tn", name="l1_dw_z"), _mm(x1b, dxbc, "tn", name="l1_dw_xbc")
    g_dt = _mm(x1b, ddt_raw, "tn", name="l1_dw_dt")
    dx1 = _mm(dz, w1z, "nt", name="l1_dx_z", add=dv1, add_scale=DN_ALPHA)
    dx1 = _mm(dxbc, w1x, "nt", name="l1_dx_xbc", add=dx1)

    dv0, dgb0 = _ln_bwd_call(v0, dx1, ddt_raw, w1d, g0)
    g_wo0 = _mm(y0, dv0, "tn", name="l0_dwout")
    dh, do, dgate = _gate_bwd(dv0, wo0, h, o, proj0)
    send_l1 = [_reshard([g_z, g_xbc, g_dt], MAP_W1, 644, bf, "reshard_w1"), g_wo1.astype(bf).reshape(N_DEV, 256, D_MODEL)]
    dq, dk, dvv, recv_l1 = _flash_bwd(qc, kc, vc, o, do, lse, scatter=send_l1)
    dqraw, dkvraw, dtail, dqnw, dknw = _mla_bwd(dq, dk, dvv, proj0, qn_w, kn_w, wq, wkv, tc, tsa, tsb)
    g_wq = _mm(qn, dqraw, "tn", name="mla_dwq", tm=256)
    g_wkv = _mm(kn, dkvraw, "tn", name="mla_dwkv", tm=128, tn=512)
    dxr, g_wa, g_wx, dvec0 = _rglru_bwd(dh, xc, h, proj0, cw0, wa, ba, wxg, bx, lam)
    g_tail = _mm(xb, dtail, "tn", name="l0_dw_tail")
    g_rnn, g_gate = _mm(xb, dxr, "tn", name="l0_dw_rnn"), _mm(xb, dgate, "tn", name="l0_dw_gate")
    dx = _mm(dxr, w0p, "nt", name="l0_dx_rnn", add=dv0, add_scale=DN_ALPHA, b_col=P0_RNN)
    dx = _mm(dgate, w0p, "nt", name="l0_dx_gate", add=dx, b_col=0)
    dx = _mm(dtail, w0p, "nt", name="l0_dx_tail", add=dx, b_col=3)

    acc = {"g_rnn": g_rnn, "g_gate": g_gate, "g_tail": g_tail, "g_wo0": g_wo0, "g_wq": g_wq, "g_wkv": g_wkv,
           "dvec0": dvec0, "g_wa": g_wa, "g_wx": g_wx, "dqnw": dqnw, "dknw": dknw, "dgb0": dgb0, "dvec1": dvec1,
           "dcw1": dcw1, "dnw": dnw, "dgb1": dgb1}
    return acc, recv_l1, loss8[0, 0], dx
```

```python
import math

import jax
import jax.numpy as jnp
from jax import lax
from jax.experimental import pallas as pl
from jax.experimental.pallas import tpu as pltpu

F32 = jnp.float32
MXU_DTYPE = jnp.bfloat16

N_DEV = 8
SEQ = 4096
D_MODEL = 1024
DN_ALPHA = 4.0 ** 0.25
RNN_W = 512
MLA_HEADS = 8
ATT_SCALE = 96.0 ** -0.5
ATT_C = ATT_SCALE * math.log2(math.e)
RG_C = 8.0
SSD_INNER = 2048
SSD_HEADS = 32
SSD_P = 64
SSD_GROUPS = 4
SSD_N = 128
SSD_L = 128
SSD_CONV = 3072
LANES = 128
SUBLANES = 8
VMEM_LIMIT = 56 * 1024 * 1024

ADAM_LR, ADAM_B1, ADAM_B2, ADAM_EPS, ADAM_WD, ADAM_STEP = 0.001, 0.9, 0.999, 1e-08, 0.01, 10

HIGHEST = lax.Precision.HIGHEST


def _params(sem, limit=VMEM_LIMIT):
    return pltpu.CompilerParams(dimension_semantics=sem, vmem_limit_bytes=limit)


def _dot(a, b):
    return lax.dot_general(a, b, (((1,), (0,)), ((), ())), preferred_element_type=F32)


def _dot_nt(a, b):
    return lax.dot_general(a, b, (((1,), (1,)), ((), ())), preferred_element_type=F32)


def _dot_tn(a, b):
    return lax.dot_general(a, b, (((0,), (0,)), ((), ())), preferred_element_type=F32)


def _dot_hi(a, b):
    return lax.dot_general(a, b, (((1,), (0,)), ((), ())), precision=HIGHEST, preferred_element_type=F32)


def _mx(v):
    return v.astype(MXU_DTYPE)


def _sigmoid(v):
    return 1.0 / (1.0 + jnp.exp(-v))


def _log1p_pos(e):
    poly = e * (1.0 - e * (0.5 - e * (1.0 / 3.0 - e * 0.25)))
    return jnp.where(e < 0.01, poly, jnp.log(1.0 + e))


def _softplus(v):
    return jnp.maximum(v, 0.0) + _log1p_pos(jnp.exp(-jnp.abs(v)))


def _neg_expm1(v):
    poly = -v * (1.0 + v * (0.5 + v * (1.0 / 6.0 + v * (1.0 / 24.0 + v * (1.0 / 120.0)))))
    return jnp.where(jnp.abs(v) < 0.1, poly, 1.0 - jnp.exp(v))


def _silu(v):
    return v * _sigmoid(v)


def _dsilu(v):
    s = _sigmoid(v)
    return s * (1.0 + v * (1.0 - s))


def _mm(a, b, mode, *, name, add=None, add_scale=1.0, out_dtype=F32, tm=None, tn=1024, tk=512, b_col=0):
    if mode == "tn":
        kdim, m = a.shape
        n = b.shape[1]
        tm, tn, tk = min(tm or 1024, m), min(tn, n), min(tk, kdim)

        def body_tn(a_ref, b_ref, o_ref):
            @pl.when(pl.program_id(2) == 0)
            def _():
                o_ref[...] = jnp.zeros_like(o_ref)

            o_ref[...] += _dot_tn(_mx(a_ref[...]), _mx(b_ref[...]))

        return pl.pallas_call(
            body_tn, name=name, grid=(m // tm, n // tn, kdim // tk),
            in_specs=[pl.BlockSpec((tk, tm), lambda i, j, k: (k, i)), pl.BlockSpec((tk, tn), lambda i, j, k: (k, j))],
            out_specs=pl.BlockSpec((tm, tn), lambda i, j, k: (i, j)),
            out_shape=jax.ShapeDtypeStruct((m, n), F32),
            compiler_params=_params(("parallel", "parallel", "arbitrary")),
        )(a, b)

    m, kdim = a.shape
    n = b.shape[1] if mode == "nn" else b.shape[0]
    tm, tn = min(tm or 1024, m), min(tn, n)
    has_add = add is not None

    def body(*refs):
        a_ref, b_ref = refs[0], refs[1]
        o_ref = refs[-1]
        av, bv = _mx(a_ref[...]), _mx(b_ref[...])
        acc = _dot(av, bv) if mode == "nn" else _dot_nt(av, bv)
        if has_add:
            acc = acc + add_scale * refs[2][...]
        o_ref[...] = acc.astype(out_dtype)

    b_spec = (pl.BlockSpec((kdim, tn), lambda i, j: (0, j)) if mode == "nn"
              else pl.BlockSpec((tn, kdim), lambda i, j: (j, b_col)))
    in_specs = [pl.BlockSpec((tm, kdim), lambda i, j: (i, 0)), b_spec]
    args = [a, b]
    if has_add:
        in_specs.append(pl.BlockSpec((tm, tn), lambda i, j: (i, j)))
        args.append(add)
    return pl.pallas_call(
        body, name=name, grid=(m // tm, n // tn), in_specs=in_specs,
        out_specs=pl.BlockSpec((tm, tn), lambda i, j: (i, j)),
        out_shape=jax.ShapeDtypeStruct((m, n), out_dtype),
        compiler_params=_params(("parallel", "parallel")),
    )(*args)


def _shift_down(blk, halo, s):
    if s == 0:
        return blk
    t = blk.shape[0]
    r = pltpu.roll(blk, s, 0)
    hr = pltpu.roll(halo, s, 0)
    row8 = lax.broadcasted_iota(jnp.int32, hr.shape, 0)
    head = jnp.where(row8 < s, hr, r[:SUBLANES])
    return jnp.concatenate([head, r[SUBLANES:]], axis=0) if t > SUBLANES else head


def _shift_up(blk, halo, s):
    if s == 0:
        return blk
    t = blk.shape[0]
    r = pltpu.roll(blk, t - s, 0)
    hr = pltpu.roll(halo, SUBLANES - s, 0)
    row8 = lax.broadcasted_iota(jnp.int32, hr.shape, 0)
    tail = jnp.where(row8 >= SUBLANES - s, hr, r[t - SUBLANES:])
    return jnp.concatenate([r[:t - SUBLANES], tail], axis=0) if t > SUBLANES else tail


def _scan_down(a, u):
    t = a.shape[0]
    row = lax.broadcasted_iota(jnp.int32, a.shape, 0)
    d = 1
    while d < t:
        keep = row >= d
        a_sh = jnp.where(keep, pltpu.roll(a, d, 0), 1.0)
        u_sh = jnp.where(keep, pltpu.roll(u, d, 0), 0.0)
        u = a * u_sh + u
        a = a * a_sh
        d *= 2
    return a, u


def _scan_up(a, u):
    t = a.shape[0]
    row = lax.broadcasted_iota(jnp.int32, a.shape, 0)
    d = 1
    while d < t:
        keep = row < t - d
        a_sh = jnp.where(keep, pltpu.roll(a, t - d, 0), 1.0)
        u_sh = jnp.where(keep, pltpu.roll(u, t - d, 0), 0.0)
        u = a * u_sh + u
        a = a * a_sh
        d *= 2
    return a, u


def _conv4(blk, halo, cw, cb):
    out = cb + blk * cw[3:4]
    for k in range(3):
        out = out + _shift_down(blk, halo, 3 - k) * cw[k:k + 1]
    return out


RG_T = 512
P0_RNN = 2


def _rg_gates(xc, wa, ba, wx, bx, lam):
    xcb = _mx(xc)
    r = _sigmoid(_dot(xcb, wa) + ba)
    ig = _sigmoid(_dot(xcb, wx) + bx)
    sp = _softplus(-lam)
    la = (-RG_C * r) * sp
    a = jnp.exp(la)
    mult = jnp.sqrt(_neg_expm1(2.0 * la))
    return r, ig, sp, a, mult


def _rglru_fwd(proj0, cw8, cb, wa, ba, wx, bx, lam):
    t, w = RG_T, RNN_W
    nb = SEQ // t

    def body(x_ref, halo_ref, cw_ref, cb_ref, wa_ref, ba_ref, wx_ref, bx_ref, lam_ref, xc_ref, h_ref, carry):
        i = pl.program_id(0)

        @pl.when(i == 0)
        def _():
            carry[...] = jnp.zeros_like(carry)

        blk = x_ref[...]
        halo = jnp.where(i > 0, halo_ref[...], 0.0)
        xc = _conv4(blk, halo, cw_ref[...], cb_ref[...])
        _, ig, _, a, mult = _rg_gates(xc, wa_ref[...], ba_ref[...], wx_ref[...], bx_ref[...], lam_ref[...])
        u = mult * (ig * xc)
        big_a, big_u = _scan_down(a, u)
        h = big_a * carry[SUBLANES - 1:SUBLANES, :] + big_u
        carry[...] = h[t - SUBLANES:]
        xc_ref[...] = xc
        h_ref[...] = h

    vec = pl.BlockSpec((1, w), lambda i: (0, 0))
    mat = pl.BlockSpec((w, w), lambda i: (0, 0))
    return pl.pallas_call(
        body, name="rglru_fwd", grid=(nb,),
        in_specs=[pl.BlockSpec((t, w), lambda i: (i, P0_RNN)),
                  pl.BlockSpec((SUBLANES, w), lambda i: (jnp.maximum(i * (t // SUBLANES) - 1, 0), P0_RNN)),
                  pl.BlockSpec((SUBLANES, w), lambda i: (0, 0)), vec, mat, vec, mat, vec, vec],
        out_specs=[pl.BlockSpec((t, w), lambda i: (i, 0)), pl.BlockSpec((t, w), lambda i: (i, 0))],
        out_shape=[jax.ShapeDtypeStruct((SEQ, w), F32), jax.ShapeDtypeStruct((SEQ, w), F32)],
        scratch_shapes=[pltpu.VMEM((SUBLANES, w), F32)],
        compiler_params=_params(("arbitrary",)),
    )(proj0, proj0, cw8, cb, wa, ba, wx, bx, lam)


def _rglru_bwd(dh, xc, h, proj0, cw8, wa, ba, wx, bx, lam):
    t, w = RG_T, RNN_W
    nb = SEQ // t
    tb = t // SUBLANES

    def body(dh_ref, xc_ref, h_ref, hh_ref, x_ref, cw_ref, wa_ref, ba_ref, wx_ref, bx_ref, lam_ref,
             dx_ref, dwa_ref, dwx_ref, dvec_ref, gcarry, dxc_next):
        i = pl.program_id(0)
        rev = nb - 1 - i

        @pl.when(i == 0)
        def _():
            gcarry[...] = jnp.zeros_like(gcarry)
            dxc_next[...] = jnp.zeros_like(dxc_next)
            dwa_ref[...] = jnp.zeros_like(dwa_ref)
            dwx_ref[...] = jnp.zeros_like(dwx_ref)
            dvec_ref[...] = jnp.zeros_like(dvec_ref)

        xc = xc_ref[...]
        wa_v, wx_v = wa_ref[...], wx_ref[...]
        lam_v = lam_ref[...]
        r, ig, sp, a, mult = _rg_gates(xc, wa_v, ba_ref[...], wx_v, bx_ref[...], lam_v)
        dhv = dh_ref[...]
        big_a, big_u = _scan_up(a, a * dhv)
        gg = big_a * gcarry[0:1, :] + big_u
        g = dhv + _shift_up(gg, gcarry[...], 1)
        gcarry[...] = gg[:SUBLANES]
        hhalo = jnp.where(rev > 0, hh_ref[...], 0.0)
        da = g * _shift_down(h_ref[...], hhalo, 1)
        d_mult = g * (ig * xc)
        d_i = g * (mult * xc)
        dxc = g * (mult * ig)
        d_la = da * a - d_mult * (a * a) / mult
        d_r = d_la * (-RG_C * sp)
        d_sp = jnp.sum(d_la * (-RG_C * r), axis=0, keepdims=True)
        d_pa = d_r * r * (1.0 - r)
        d_px = d_i * ig * (1.0 - ig)
        d_pab, d_pxb = _mx(d_pa), _mx(d_px)
        dxc = dxc + _dot_nt(d_pab, wa_v) + _dot_nt(d_pxb, wx_v)
        xcb = _mx(xc)
        dwa_ref[...] += _dot_tn(xcb, d_pab)
        dwx_ref[...] += _dot_tn(xcb, d_pxb)
        dvec_ref[0:1, :] += jnp.sum(d_pa, axis=0, keepdims=True)
        dvec_ref[1:2, :] += jnp.sum(d_px, axis=0, keepdims=True)
        dvec_ref[2:3, :] += d_sp * (-_sigmoid(-lam_v))
        dvec_ref[3:4, :] += jnp.sum(dxc, axis=0, keepdims=True)
        xblk = x_ref[...]
        cw = cw_ref[...]
        dx = dxc * cw[3:4]
        nxt = dxc_next[...]
        dvec_ref[7:8, :] += jnp.sum(dxc * xblk, axis=0, keepdims=True)
        for k in range(3):
            up = _shift_up(dxc, nxt, 3 - k)
            dvec_ref[4 + k:5 + k, :] += jnp.sum(up * xblk, axis=0, keepdims=True)
            dx = dx + up * cw[k:k + 1]
        dxc_next[...] = dxc[:SUBLANES]
        dx_ref[...] = _mx(dx)

    blk = pl.BlockSpec((t, w), lambda i: (nb - 1 - i, 0))
    halo = pl.BlockSpec((SUBLANES, w), lambda i: (jnp.maximum((nb - 1 - i) * tb - 1, 0), 0))
    vec = pl.BlockSpec((1, w), lambda i: (0, 0))
    mat = pl.BlockSpec((w, w), lambda i: (0, 0))
    return pl.pallas_call(
        body, name="rglru_bwd", grid=(nb,),
        in_specs=[blk, blk, blk, halo, pl.BlockSpec((t, w), lambda i: (nb - 1 - i, P0_RNN)),
                  pl.BlockSpec((SUBLANES, w), lambda i: (0, 0)), mat, vec, mat, vec, vec],
        out_specs=[blk, mat, mat, pl.BlockSpec((16, w), lambda i: (0, 0))],
        out_shape=[jax.ShapeDtypeStruct((SEQ, w), MXU_DTYPE), jax.ShapeDtypeStruct((w, w), F32),
                   jax.ShapeDtypeStruct((w, w), F32), jax.ShapeDtypeStruct((16, w), F32)],
        scratch_shapes=[pltpu.VMEM((SUBLANES, w), F32), pltpu.VMEM((SUBLANES, w), F32)],
        compiler_params=_params(("arbitrary",)),
    )(dh, xc, h, h, proj0, cw8, wa, ba, wx, bx, lam)


MLA_T = 512


def _rope(v, c, sa, sb):
    return v * c + pltpu.roll(v, LANES - 16, 1) * sa + pltpu.roll(v, 16, 1) * sb


def _rope_t(dv, c, sa, sb):
    return dv * c + pltpu.roll(dv * sa, 16, 1) + pltpu.roll(dv * sb, LANES - 16, 1)


def _rms(v, g, eps=1e-6):
    rs = lax.rsqrt(jnp.mean(v * v, axis=-1, keepdims=True) + eps)
    return v * rs * g, rs


def _mla_fwd(proj0, q_norm, kv_norm, wq, wkv, tc, tsa, tsb):
    t = MLA_T

    def body(cq_ref, ck_ref, qn_ref, kn_ref, wq_ref, wkv_ref, c_ref, sa_ref, sb_ref,
             oqn_ref, okn_ref, oq_ref, ok_ref, ov_ref):
        c, sa, sb = c_ref[...], sa_ref[...], sb_ref[...]
        ck = ck_ref[...]
        qn = _mx(_rms(cq_ref[...], qn_ref[...])[0])
        kn = _mx(_rms(ck[:, :LANES], kn_ref[...])[0])
        oqn_ref[...] = qn
        okn_ref[...] = kn
        krv = _rope(ck[:, LANES:], c, sa, sb)
        qraw = _dot(qn, wq_ref[...])
        kvraw = _dot(kn, wkv_ref[...])
        for hd in range(MLA_HEADS):
            sl = slice(hd * LANES, (hd + 1) * LANES)
            oq_ref[:, sl] = _mx(_rope(qraw[:, sl], c, sa, sb))
            ok_ref[:, sl] = _mx(kvraw[:, sl] + krv)
        ov_ref[...] = _mx(kvraw[:, 1024:])

    tab = pl.BlockSpec((t, LANES), lambda i: (i, 0))
    wide = pl.BlockSpec((t, 1024), lambda i: (i, 0))
    const = lambda shape: pl.BlockSpec(shape, lambda i: (0, 0))
    return pl.pallas_call(
        body, name="mla_fwd", grid=(SEQ // t,),
        in_specs=[pl.BlockSpec((t, 256), lambda i: (i, 6)), pl.BlockSpec((t, 256), lambda i: (i, 7)),
                  const((1, 256)), const((1, LANES)), const((256, 1024)), const((LANES, 1536)), tab, tab, tab],
        out_specs=[pl.BlockSpec((t, 256), lambda i: (i, 0)), tab, wide, wide, pl.BlockSpec((t, 512), lambda i: (i, 0))],
        out_shape=[jax.ShapeDtypeStruct((SEQ, 256), MXU_DTYPE), jax.ShapeDtypeStruct((SEQ, LANES), MXU_DTYPE),
                   jax.ShapeDtypeStruct((SEQ, 1024), MXU_DTYPE), jax.ShapeDtypeStruct((SEQ, 1024), MXU_DTYPE),
                   jax.ShapeDtypeStruct((SEQ, 512), MXU_DTYPE)],
        compiler_params=_params(("parallel",)),
    )(proj0, proj0, q_norm, kv_norm, wq, wkv, tc, tsa, tsb)


ATT_T = 1024


def _flash_fwd(q, k, v, bcast=()):
    t = ATT_T
    nb = SEQ // t

    steps = [(qi, ki) for qi in range(nb) for ki in range(qi + 1)]
    qi_tab = jnp.asarray([s[0] for s in steps], jnp.int32)
    ki_tab = jnp.asarray([s[1] for s in steps], jnp.int32)

    nx = len(bcast)

    def body(qi_ref, ki_ref, q_ref, k_ref, v_ref, *rest):
        x_refs, (o_ref, lse_ref), g_refs = rest[:nx], rest[nx:nx + 2], rest[nx + 2:2 * nx + 2]
        m_sc, acc_sc = rest[2 * nx + 2:2 * nx + 4]
        step = pl.program_id(1)
        qi, ki = qi_ref[step], ki_ref[step]
        if nx:
            copies = _peer_copies(x_refs, g_refs, rest[2 * nx + 4:], 0)

            @pl.when((pl.program_id(0) == 0) & (step == 0))
            def _():
                for cp in copies:
                    cp.start()

        @pl.when(ki == 0)
        def _():
            m_sc[...] = jnp.full_like(m_sc, -jnp.inf)
            acc_sc[...] = jnp.zeros_like(acc_sc)

        def update(diagonal):
            vv = v_ref[...]
            lane_v = lax.broadcasted_iota(jnp.int32, vv.shape, 1)
            for hd in range(2):
                sl = slice(hd * LANES, (hd + 1) * LANES)
                s = _dot_nt(q_ref[:, sl], k_ref[:, sl])
                if diagonal:
                    s = jnp.where(lax.broadcasted_iota(jnp.int32, (t, t), 1)
                                  <= lax.broadcasted_iota(jnp.int32, (t, t), 0), s, -jnp.inf)
                m_prev = m_sc[hd]
                m_new = jnp.maximum(m_prev, jnp.max(s, axis=1, keepdims=True))
                p = jnp.exp2((s - m_new[:, :1]) * ATT_C)
                m_sc[hd] = m_new
                vh = jnp.where((lane_v >= hd * 64) & (lane_v < (hd + 1) * 64), vv, jnp.ones_like(vv))
                acc_sc[hd] = acc_sc[hd] * jnp.exp2((m_prev - m_new) * ATT_C) + _dot(_mx(p), vh)

        @pl.when(ki < qi)
        def _():
            update(False)

        @pl.when(ki == qi)
        def _():
            update(True)
            first = lax.broadcasted_iota(jnp.int32, (t, LANES), 1) < 64
            a0, a1 = acc_sc[0], acc_sc[1]
            l0, l1 = pltpu.roll(a0, 64, 1), pltpu.roll(a1, 64, 1)
            o_ref[...] = jnp.where(first, a0 / l0, a1 / l1)
            lse_ref[0] = jnp.where(first, m_sc[0] * ATT_SCALE + jnp.log(l0), m_sc[1] * ATT_SCALE + jnp.log(l1))

        if nx:
            @pl.when((pl.program_id(0) == 3) & (step == len(steps) - 1))
            def _():
                for cp in copies:
                    cp.wait()

    grid_spec = pltpu.PrefetchScalarGridSpec(
        num_scalar_prefetch=2, grid=(4, len(steps)),
        in_specs=[pl.BlockSpec((t, 256), lambda p, s, qt, kt: (qt[s], p)),
                  pl.BlockSpec((t, 256), lambda p, s, qt, kt: (kt[s], p)),
                  pl.BlockSpec((t, LANES), lambda p, s, qt, kt: (kt[s], p))] + [ANY] * nx,
        out_specs=[pl.BlockSpec((t, LANES), lambda p, s, qt, kt: (qt[s], p)),
                   pl.BlockSpec((1, t, LANES), lambda p, s, qt, kt: (p, qt[s], 0))] + [ANY] * nx,
        scratch_shapes=[pltpu.VMEM((2, t, LANES), F32), pltpu.VMEM((2, t, LANES), F32)]
        + (_exchange_sems(nx) if nx else []))
    res = pl.pallas_call(
        body, name="flash_fwd", grid_spec=grid_spec,
        out_shape=[jax.ShapeDtypeStruct((SEQ, 512), F32), jax.ShapeDtypeStruct((4, SEQ, LANES), F32)]
        + _exchange_shapes([], bcast),
        compiler_params=_params(("arbitrary", "arbitrary")),
    )(qi_tab, ki_tab, q, k, v, *bcast)
    return res[0], res[1], res[2:]


def _flash_bwd(q, k, v, o, do, lse, scatter=()):
    t = ATT_T
    nb = SEQ // t

    steps = [(qi, ki) for ki in range(nb) for qi in range(ki, nb)]
    qi_tab = jnp.asarray([s[0] for s in steps], jnp.int32)
    ki_tab = jnp.asarray([s[1] for s in steps], jnp.int32)
    log2e = math.log2(math.e)

    nx = len(scatter)

    def body(qi_ref, ki_ref, q_ref, k_ref, v_ref, o_ref, do_ref, lse_ref, *rest):
        x_refs, (dq_ref, dk_ref, dv_ref), g_refs = rest[:nx], rest[nx:nx + 3], rest[nx + 3:2 * nx + 3]
        step = pl.program_id(1)
        qi, ki = qi_ref[step], ki_ref[step]
        if nx:
            copies = _peer_copies(x_refs, g_refs, rest[2 * nx + 3:], nx)

            @pl.when((pl.program_id(0) == 0) & (step == 0))
            def _():
                for cp in copies:
                    cp.start()

        @pl.when(step == 0)
        def _():
            dq_ref[...] = jnp.zeros_like(dq_ref)

        @pl.when(qi == ki)
        def _():
            dk_ref[...] = jnp.zeros_like(dk_ref)
            dv_ref[...] = jnp.zeros_like(dv_ref)

        def update(diagonal):
            dov, ov, vv = do_ref[...], o_ref[...], v_ref[...]
            lse2 = lse_ref[0] * log2e
            lane = lax.broadcasted_iota(jnp.int32, (t, LANES), 1)
            prod = dov * ov
            qrows = pl.ds(pl.multiple_of(qi * t, t), t)
            dv_acc = jnp.zeros((t, LANES), F32)
            dk_new, dq_new = [], []
            for hd in range(2):
                sl = slice(hd * LANES, (hd + 1) * LANES)
                mine = (lane >= hd * 64) & (lane < (hd + 1) * 64)
                qh, kh = q_ref[:, sl], k_ref[:, sl]
                p = jnp.exp2(_dot_nt(qh, kh) * ATT_C - lse2[:, hd * 64:hd * 64 + 1])
                if diagonal:
                    p = jnp.where(lax.broadcasted_iota(jnp.int32, (t, t), 1)
                                  <= lax.broadcasted_iota(jnp.int32, (t, t), 0), p, 0.0)
                do_h = jnp.where(mine, dov, 0.0)
                delta = jnp.sum(jnp.where(mine, prod, 0.0), axis=1, keepdims=True)
                dp = _dot_nt(_mx(do_h), vv)
                ds = _mx(p * (dp - delta) * ATT_SCALE)
                dv_acc = dv_acc + jnp.where(mine, _dot_tn(_mx(p), _mx(dov)), 0.0)
                dk_new.append(_dot_tn(ds, qh))
                dq_new.append(_dot(ds, kh))
            for hd in range(2):
                sl = slice(hd * LANES, (hd + 1) * LANES)
                dk_ref[:, sl] += dk_new[hd]
                dq_ref[qrows, sl] += dq_new[hd]
            dv_ref[...] += dv_acc

        @pl.when(qi > ki)
        def _():
            update(False)

        @pl.when(qi == ki)
        def _():
            update(True)

        if nx:
            @pl.when((pl.program_id(0) == 3) & (step == len(steps) - 1))
            def _():
                for cp in copies:
                    cp.wait()

    qmap = lambda p, s, qt, kt: (qt[s], p)
    kmap = lambda p, s, qt, kt: (kt[s], p)
    grid_spec = pltpu.PrefetchScalarGridSpec(
        num_scalar_prefetch=2, grid=(4, len(steps)),
        in_specs=[pl.BlockSpec((t, 256), qmap), pl.BlockSpec((t, 256), kmap), pl.BlockSpec((t, LANES), kmap),
                  pl.BlockSpec((t, LANES), qmap), pl.BlockSpec((t, LANES), qmap),
                  pl.BlockSpec((1, t, LANES), lambda p, s, qt, kt: (p, qt[s], 0))] + [ANY] * nx,
        out_specs=[pl.BlockSpec((SEQ, 256), lambda p, s, qt, kt: (0, p)), pl.BlockSpec((t, 256), kmap),
                   pl.BlockSpec((t, LANES), kmap)] + [ANY] * nx,
        scratch_shapes=_exchange_sems(nx) if nx else [])
    res = pl.pallas_call(
        body, name="flash_bwd", grid_spec=grid_spec,
        out_shape=[jax.ShapeDtypeStruct((SEQ, 1024), F32), jax.ShapeDtypeStruct((SEQ, 1024), F32),
                   jax.ShapeDtypeStruct((SEQ, 512), F32)] + _exchange_shapes(scatter, []),
        compiler_params=_params(("arbitrary", "arbitrary")),
    )(qi_tab, ki_tab, q, k, v, o, do, lse, *scatter)
    return res[0], res[1], res[2], res[3:]


def _rms_bwd(v, g, dy, eps=1e-6):
    rs = lax.rsqrt(jnp.mean(v * v, axis=-1, keepdims=True) + eps)
    xh = v * rs
    dxh = dy * g
    dv = rs * (dxh - xh * jnp.mean(dxh * xh, axis=-1, keepdims=True))
    return dv, jnp.sum(dy * xh, axis=0, keepdims=True)


def _mla_bwd(dq, dk, dv, proj0, q_norm, kv_norm, wq, wkv, tc, tsa, tsb):
    t = MLA_T

    def body(dq_ref, dk_ref, dv_ref, cq_ref, ck_ref, qn_ref, kn_ref, wq_ref, wkv_ref, c_ref, sa_ref, sb_ref,
             oq_ref, okv_ref, o_ref, dgq_ref, dgk_ref):
        @pl.when(pl.program_id(0) == 0)
        def _():
            dgq_ref[...] = jnp.zeros_like(dgq_ref)
            dgk_ref[...] = jnp.zeros_like(dgk_ref)

        c, sa, sb = c_ref[...], sa_ref[...], sb_ref[...]
        lane = lax.broadcasted_iota(jnp.int32, (t, LANES), 1)
        dkr = jnp.zeros((t, LANES), F32)
        for hd in range(MLA_HEADS):
            sl = slice(hd * LANES, (hd + 1) * LANES)
            oq_ref[:, sl] = _mx(_rope_t(dq_ref[:, sl], c, sa, sb))
            dkh = dk_ref[:, sl]
            okv_ref[:, sl] = _mx(dkh)
            dkr = dkr + dkh
        okv_ref[:, 1024:] = _mx(dv_ref[...])
        dkr = _rope_t(jnp.where((lane >= 64) & (lane < 96), dkr, 0.0), c, sa, sb)
        dqn = _dot_nt(oq_ref[...], wq_ref[...])
        dkn = _dot_nt(okv_ref[...], wkv_ref[...])
        dcq, dgq = _rms_bwd(cq_ref[...], qn_ref[...], dqn)
        dck, dgk = _rms_bwd(ck_ref[:, :LANES], kn_ref[...], dkn)
        o_ref[:, :256] = _mx(dcq)
        o_ref[:, 256:384] = _mx(dck)
        o_ref[:, 384:] = _mx(dkr)
        dgq_ref[0:1, :] += dgq
        dgk_ref[0:1, :] += dgk

    tab = pl.BlockSpec((t, LANES), lambda i: (i, 0))
    wide = pl.BlockSpec((t, 1024), lambda i: (i, 0))
    const = lambda shape: pl.BlockSpec(shape, lambda i: (0, 0))
    return pl.pallas_call(
        body, name="mla_bwd", grid=(SEQ // t,),
        in_specs=[wide, wide, pl.BlockSpec((t, 512), lambda i: (i, 0)),
                  pl.BlockSpec((t, 256), lambda i: (i, 6)), pl.BlockSpec((t, 256), lambda i: (i, 7)),
                  const((1, 256)), const((1, LANES)), const((256, 1024)), const((LANES, 1536)), tab, tab, tab],
        out_specs=[wide, pl.BlockSpec((t, 1536), lambda i: (i, 0)), pl.BlockSpec((t, 512), lambda i: (i, 0)),
                   const((SUBLANES, 256)), const((SUBLANES, LANES))],
        out_shape=[jax.ShapeDtypeStruct((SEQ, 1024), MXU_DTYPE), jax.ShapeDtypeStruct((SEQ, 1536), MXU_DTYPE),
                   jax.ShapeDtypeStruct((SEQ, 512), MXU_DTYPE), jax.ShapeDtypeStruct((SUBLANES, 256), F32),
                   jax.ShapeDtypeStruct((SUBLANES, LANES), F32)],
        compiler_params=_params(("arbitrary",)),
    )(dq, dk, dv, proj0, proj0, q_norm, kv_norm, wq, wkv, tc, tsa, tsb)


LN_T = 512


def _ln(v, g, b, eps=1e-5):
    mu = jnp.mean(v, axis=-1, keepdims=True)
    xc = v - mu
    rs = lax.rsqrt(jnp.mean(xc * xc, axis=-1, keepdims=True) + eps)
    return xc * rs * g + b


def _ln_bwd(v, g, dy, eps=1e-5):
    mu = jnp.mean(v, axis=-1, keepdims=True)
    xc = v - mu
    rs = lax.rsqrt(jnp.mean(xc * xc, axis=-1, keepdims=True) + eps)
    xh = xc * rs
    dxh = dy * g
    dv = rs * (dxh - jnp.mean(dxh, axis=-1, keepdims=True) - xh * jnp.mean(dxh * xh, axis=-1, keepdims=True))
    return dv, jnp.sum(dy * xh, axis=0, keepdims=True), jnp.sum(dy, axis=0, keepdims=True)


def _l0_out(h, o, proj0, x, w_out, g, b):
    t = LN_T

    def body(h_ref, o_ref, ga_ref, gb_ref, x_ref, w_ref, g_ref, b_ref, y_ref, v_ref, x1_ref, x1b_ref):
        y = _mx(jnp.concatenate([h_ref[...] * _silu(ga_ref[...]), o_ref[...] * _silu(gb_ref[...])], axis=1))
        v = DN_ALPHA * x_ref[...] + _dot(y, w_ref[...])
        y_ref[...] = y
        v_ref[...] = v
        x1 = _ln(v, g_ref[...], b_ref[...])
        x1_ref[...] = x1
        x1b_ref[...] = _mx(x1)

    half = pl.BlockSpec((t, 512), lambda i: (i, 0))
    full = pl.BlockSpec((t, D_MODEL), lambda i: (i, 0))
    vec = pl.BlockSpec((1, D_MODEL), lambda i: (0, 0))
    return pl.pallas_call(
        body, name="l0_out", grid=(SEQ // t,),
        in_specs=[half, half, pl.BlockSpec((t, 512), lambda i: (i, 0)), pl.BlockSpec((t, 512), lambda i: (i, 1)), full,
                  pl.BlockSpec((D_MODEL, D_MODEL), lambda i: (0, 0)), vec, vec],
        out_specs=[full, full, full, full],
        out_shape=[jax.ShapeDtypeStruct((SEQ, D_MODEL), MXU_DTYPE), jax.ShapeDtypeStruct((SEQ, D_MODEL), F32),
                   jax.ShapeDtypeStruct((SEQ, D_MODEL), F32), jax.ShapeDtypeStruct((SEQ, D_MODEL), MXU_DTYPE)],
        compiler_params=_params(("parallel",)),
    )(h, o, proj0, proj0, x, w_out, g, b)


def _ln_bwd_call(v, dy, ddt, w1d, g):
    t = LN_T

    def body(v_ref, dy_ref, ddt_ref, w_ref, g_ref, dv_ref, dgb_ref):
        @pl.when(pl.program_id(0) == 0)
        def _():
            dgb_ref[...] = jnp.zeros_like(dgb_ref)

        dy_v = dy_ref[...] + _dot_nt(_mx(ddt_ref[...]), w_ref[...])
        dv, dg, db = _ln_bwd(v_ref[...], g_ref[...], dy_v)
        dv_ref[...] = dv
        dgb_ref[0:1, :] += dg
        dgb_ref[1:2, :] += db

    full = pl.BlockSpec((t, D_MODEL), lambda i: (i, 0))
    return pl.pallas_call(
        body, name="ln_bwd", grid=(SEQ // t,),
        in_specs=[full, full, pl.BlockSpec((t, LANES), lambda i: (i, 0)), pl.BlockSpec((D_MODEL, LANES), lambda i: (0, 0)),
                  pl.BlockSpec((1, D_MODEL), lambda i: (0, 0))],
        out_specs=[full, pl.BlockSpec((SUBLANES, D_MODEL), lambda i: (0, 0))],
        out_shape=[jax.ShapeDtypeStruct((SEQ, D_MODEL), F32), jax.ShapeDtypeStruct((SUBLANES, D_MODEL), F32)],
        compiler_params=_params(("arbitrary",)),
    )(v, dy, ddt, w1d, g)


def _gate_bwd(dv0, w_out, h, o, proj0):
    t = LN_T

    def body(dv_ref, w_ref, h_ref, o_ref, ga_ref, gb_ref, dh_ref, do_ref, dg_ref):
        dy = _dot_nt(_mx(dv_ref[...]), w_ref[...])
        ga, gb, dya, dyb = ga_ref[...], gb_ref[...], dy[:, :512], dy[:, 512:]
        dh_ref[...] = dya * _silu(ga)
        do_ref[...] = dyb * _silu(gb)
        dg_ref[:, :512] = _mx(dya * h_ref[...] * _dsilu(ga))
        dg_ref[:, 512:] = _mx(dyb * o_ref[...] * _dsilu(gb))

    half = pl.BlockSpec((t, 512), lambda i: (i, 0))
    half1 = pl.BlockSpec((t, 512), lambda i: (i, 1))
    full = pl.BlockSpec((t, 1024), lambda i: (i, 0))
    return pl.pallas_call(
        body, name="gate_bwd", grid=(SEQ // t,),
        in_specs=[full, pl.BlockSpec((D_MODEL, D_MODEL), lambda i: (0, 0)), half, half, half, half1],
        out_specs=[half, half, full],
        out_shape=[jax.ShapeDtypeStruct((SEQ, 512), F32), jax.ShapeDtypeStruct((SEQ, 512), F32),
                   jax.ShapeDtypeStruct((SEQ, 1024), MXU_DTYPE)],
        compiler_params=_params(("parallel",)),
    )(dv0, w_out, h, o, proj0, proj0)


CONV_T = 512
CONV_CB = 1024


def _ssd_conv_fwd(xbc, cw8, cb):
    t, cbk = CONV_T, CONV_CB
    tb = t // SUBLANES

    def body(x_ref, halo_ref, cw_ref, cb_ref, pre_ref, act_ref):
        halo = jnp.where(pl.program_id(1) > 0, halo_ref[...], 0.0)
        pre = _conv4(x_ref[...], halo, cw_ref[...], cb_ref[...])
        pre_ref[...] = pre
        act_ref[...] = _silu(pre)

    blk = pl.BlockSpec((t, cbk), lambda j, i: (i, j))
    return pl.pallas_call(
        body, name="ssd_conv_fwd", grid=(SSD_CONV // cbk, SEQ // t),
        in_specs=[blk, pl.BlockSpec((SUBLANES, cbk), lambda j, i: (jnp.maximum(i * tb - 1, 0), j)),
                  pl.BlockSpec((SUBLANES, cbk), lambda j, i: (0, j)), pl.BlockSpec((1, cbk), lambda j, i: (0, j))],
        out_specs=[blk, blk],
        out_shape=[jax.ShapeDtypeStruct((SEQ, SSD_CONV), F32), jax.ShapeDtypeStruct((SEQ, SSD_CONV), F32)],
        compiler_params=_params(("parallel", "parallel")),
    )(xbc, xbc, cw8, cb)


def _ssd_conv_bwd(dact, pre, xbc, cw8):
    t, cbk = CONV_T, CONV_CB
    tb = t // SUBLANES
    nb = SEQ // t

    def body(da_ref, dan_ref, pre_ref, pren_ref, x_ref, cw_ref, dx_ref, dcw_ref):
        i = pl.program_id(1)

        @pl.when(i == 0)
        def _():
            dcw_ref[...] = jnp.zeros_like(dcw_ref)

        dpre = da_ref[...] * _dsilu(pre_ref[...])
        dpre_next = jnp.where(i < nb - 1, dan_ref[...] * _dsilu(pren_ref[...]), 0.0)
        xblk = x_ref[...]
        cw = cw_ref[...]
        dx = dpre * cw[3:4]
        dcw_ref[3:4, :] += jnp.sum(dpre * xblk, axis=0, keepdims=True)
        for k in range(3):
            up = _shift_up(dpre, dpre_next, 3 - k)
            dcw_ref[k:k + 1, :] += jnp.sum(up * xblk, axis=0, keepdims=True)
            dx = dx + up * cw[k:k + 1]
        dcw_ref[4:5, :] += jnp.sum(dpre, axis=0, keepdims=True)
        dx_ref[...] = _mx(dx)

    blk = pl.BlockSpec((t, cbk), lambda j, i: (i, j))
    nxt = pl.BlockSpec((SUBLANES, cbk), lambda j, i: (jnp.minimum((i + 1) * tb, SEQ // SUBLANES - 1), j))
    acc = pl.BlockSpec((SUBLANES, cbk), lambda j, i: (0, j))
    return pl.pallas_call(
        body, name="ssd_conv_bwd", grid=(SSD_CONV // cbk, nb),
        in_specs=[blk, nxt, blk, nxt, blk, acc],
        out_specs=[blk, acc],
        out_shape=[jax.ShapeDtypeStruct((SEQ, SSD_CONV), MXU_DTYPE), jax.ShapeDtypeStruct((SUBLANES, SSD_CONV), F32)],
        compiler_params=_params(("parallel", "arbitrary")),
    )(dact, dact, pre, pre, xbc, cw8)


def _ssd_common(dt_raw, bias, alog, tril, expand_t, xs):
    lane = lax.broadcasted_iota(jnp.int32, dt_raw.shape, 1)
    dt = jnp.where(lane < SSD_HEADS, _softplus(dt_raw + bias), 0.0)
    a_neg = -jnp.exp(alog)
    cs = _dot_hi(tril, dt * a_neg)
    dt_x = _expand_heads(dt, expand_t)
    ecs_x = _expand_heads(jnp.exp(cs), expand_t)
    ds_x = _expand_heads(jnp.exp(cs[SSD_L - 1:SSD_L, :] - cs), expand_t)
    return dt, a_neg, cs, dt_x, None, xs * dt_x, ds_x, ecs_x, ecs_x[SSD_L - 1:SSD_L, :]


def _expand_heads(v, expand_t):
    hi = v.astype(jnp.bfloat16)
    lo = (v - hi.astype(F32)).astype(jnp.bfloat16)
    return _dot_nt(hi, expand_t) + _dot_nt(lo, expand_t)


def _fold_heads(v, expand_t):
    hi = v.astype(jnp.bfloat16)
    lo = (v - hi.astype(F32)).astype(jnp.bfloat16)
    return _dot(hi, expand_t) + _dot(lo, expand_t)


def _ssd_decay(cs, cs_t, hh, causal):
    seg = cs[:, hh:hh + 1] - cs_t[hh:hh + 1, :]
    return jnp.where(causal, jnp.exp(jnp.where(causal, seg, 0.0)), 0.0)


def _ssd_scan_fwd(act, dt_raw, bias, alog, d_x, tril, expand_t):
    nc = SEQ // SSD_L
    gw = SSD_INNER // SSD_GROUPS

    def body(act_ref, dt_ref, bias_ref, alog_ref, dx_ref, tril_ref, et_ref, y_ref, hp_ref, h_sc):
        @pl.when(pl.program_id(0) == 0)
        def _():
            h_sc[...] = jnp.zeros_like(h_sc)

        xs = act_ref[:, :SSD_INNER]
        _, _, cs, _, _, xdt, ds_x, ecs_x, elast = _ssd_common(
            dt_ref[...], bias_ref[...], alog_ref[...], tril_ref[...], et_ref[...], xs)
        cs_t = cs.T
        causal = (lax.broadcasted_iota(jnp.int32, (SSD_L, SSD_L), 0)
                  >= lax.broadcasted_iota(jnp.int32, (SSD_L, SSD_L), 1))
        lane = lax.broadcasted_iota(jnp.int32, (SSD_L, LANES), 1)
        xdt_b = _mx(xdt)
        xds_b = _mx(xdt * ds_x)
        hp_ref[0] = h_sc[...]
        for g in range(SSD_GROUPS):
            gs = slice(g * gw, (g + 1) * gw)
            bg = _mx(act_ref[:, SSD_INNER + g * SSD_N:SSD_INNER + (g + 1) * SSD_N])
            cg = _mx(act_ref[:, SSD_INNER + 512 + g * SSD_N:SSD_INNER + 512 + (g + 1) * SSD_N])
            cb = _dot_nt(cg, bg)
            hprev = h_sc[:, gs]
            yoff = _dot(cg, _mx(hprev)) * ecs_x[:, gs]
            h_sc[:, gs] = hprev * elast[:, gs] + _dot_tn(bg, xds_b[:, gs])
            for pr in range(4):
                ps = slice(g * gw + pr * LANES, g * gw + (pr + 1) * LANES)
                xp = xdt_b[:, ps]
                ydiag = jnp.zeros((SSD_L, LANES), F32)
                for j in range(2):
                    dm = _ssd_decay(cs, cs_t, g * 8 + pr * 2 + j, causal)
                    mine = (lane >= j * 64) & (lane < (j + 1) * 64)
                    ydiag = ydiag + _dot(_mx(cb * dm), jnp.where(mine, xp, jnp.zeros_like(xp)))
                y_ref[:, ps] = ydiag + yoff[:, pr * LANES:(pr + 1) * LANES] + dx_ref[:, ps] * xs[:, ps]

    const = lambda shape: pl.BlockSpec(shape, lambda c: (0, 0))
    return pl.pallas_call(
        body, name="ssd_scan_fwd", grid=(nc,),
        in_specs=[pl.BlockSpec((SSD_L, SSD_CONV), lambda c: (c, 0)), pl.BlockSpec((SSD_L, LANES), lambda c: (c, 0)),
                  const((1, LANES)), const((1, LANES)), const((1, SSD_INNER)), const((SSD_L, SSD_L)),
                  const((SSD_INNER, LANES))],
        out_specs=[pl.BlockSpec((SSD_L, SSD_INNER), lambda c: (c, 0)),
                   pl.BlockSpec((1, SSD_N, SSD_INNER), lambda c: (c, 0, 0))],
        out_shape=[jax.ShapeDtypeStruct((SEQ, SSD_INNER), F32), jax.ShapeDtypeStruct((nc, SSD_N, SSD_INNER), F32)],
        scratch_shapes=[pltpu.VMEM((SSD_N, SSD_INNER), F32)],
        compiler_params=_params(("arbitrary",)),
    )(act, dt_raw, bias, alog, d_x, tril, expand_t)


def _ssd_scan_bwd(dy, act, dt_raw, hprev_all, bias, alog, d_x, tril, expand_t):
    nc = SEQ // SSD_L
    gw = SSD_INNER // SSD_GROUPS

    def body(dy_ref, act_ref, dt_ref, hp_ref, bias_ref, alog_ref, dx_ref, tril_ref, et_ref,
             dact_ref, ddt_ref, dvec_ref, dh_sc, dd_sc):
        i = pl.program_id(0)

        @pl.when(i == 0)
        def _():
            dh_sc[...] = jnp.zeros_like(dh_sc)
            dd_sc[...] = jnp.zeros_like(dd_sc)
            dvec_ref[...] = jnp.zeros_like(dvec_ref)

        xs = act_ref[:, :SSD_INNER]
        dt_raw_v, bias_v = dt_ref[...], bias_ref[...]
        dt, a_neg, cs, dt_x, _, xdt, ds_x, ecs_x, elast = _ssd_common(
            dt_raw_v, bias_v, alog_ref[...], tril_ref[...], et_ref[...], xs)
        cs_t = cs.T
        rowi = lax.broadcasted_iota(jnp.int32, (SSD_L, SSD_L), 0)
        coli = lax.broadcasted_iota(jnp.int32, (SSD_L, SSD_L), 1)
        causal = rowi >= coli
        lane = lax.broadcasted_iota(jnp.int32, (SSD_L, LANES), 1)
        row_g = lax.broadcasted_iota(jnp.int32, (SSD_L, gw), 0)
        dyv = dy_ref[...]
        dd_sc[0:1, :] += jnp.sum(dyv * xs, axis=0, keepdims=True)
        xdt_b = _mx(xdt)
        xds = xdt * ds_x
        xds_b = _mx(xds)
        dy_b = _mx(dyv)
        dye_b = _mx(dyv * ecs_x)
        dcs = jnp.zeros((SSD_L, LANES), F32)
        dcs_t = jnp.zeros((LANES, SSD_L), F32)
        dcs_parts = []
        dxdt_parts = []
        for g in range(SSD_GROUPS):
            gs = slice(g * gw, (g + 1) * gw)
            bcol = slice(SSD_INNER + g * SSD_N, SSD_INNER + (g + 1) * SSD_N)
            ccol = slice(SSD_INNER + 512 + g * SSD_N, SSD_INNER + 512 + (g + 1) * SSD_N)
            bg, cg = _mx(act_ref[:, bcol]), _mx(act_ref[:, ccol])
            cb = _dot_nt(cg, bg)
            hp = hp_ref[0, :, gs]
            hp_b = _mx(hp)
            dh = dh_sc[:, gs]
            dh_b = _mx(dh)
            yoff = _dot(cg, hp_b) * ecs_x[:, gs]
            bdh = _dot(bg, dh_b)
            tt = xds[:, gs] * bdh
            last_row = (jnp.sum(tt, axis=0, keepdims=True)
                        + jnp.sum(dh * hp, axis=0, keepdims=True) * elast[:, gs])
            dcs_parts.append(dyv[:, gs] * yoff - tt + jnp.where(row_g == SSD_L - 1, last_row, 0.0))
            dc_g = _dot_nt(dye_b[:, gs], hp_b)
            db_g = _dot_nt(xds_b[:, gs], dh_b)
            dh_sc[:, gs] = _dot_tn(cg, dye_b[:, gs]) + dh * elast[:, gs]
            wsum = jnp.zeros((SSD_L, SSD_L), F32)
            dxdt_g = []
            for pr in range(4):
                ps = slice(g * gw + pr * LANES, g * gw + (pr + 1) * LANES)
                xp, dyp = xdt_b[:, ps], dy_b[:, ps]
                dxp = jnp.zeros((SSD_L, LANES), F32)
                for j in range(2):
                    hh = g * 8 + pr * 2 + j
                    dm = _ssd_decay(cs, cs_t, hh, causal)
                    mine = (lane >= j * 64) & (lane < (j + 1) * 64)
                    dy_h = jnp.where(mine, dyp, jnp.zeros_like(dyp))
                    wd = _dot_nt(dy_h, xp) * dm
                    wsum = wsum + wd
                    gmat = wd * cb
                    dcs = dcs + jnp.where(lane == hh, jnp.sum(gmat, axis=1, keepdims=True), 0.0)
                    dcs_t = dcs_t - jnp.where(rowi == hh, jnp.sum(gmat, axis=0, keepdims=True), 0.0)
                    dxp = dxp + _dot_tn(_mx(cb * dm), dy_h)
                dxdt_g.append(dxp)
            dxdt_parts.append(jnp.concatenate(dxdt_g, axis=1) + bdh * ds_x[:, gs])
            ws_b = _mx(wsum)
            dact_ref[:, ccol] = dc_g + _dot(ws_b, bg)
            dact_ref[:, bcol] = db_g + _dot_tn(ws_b, cg)
        dxdt = jnp.concatenate(dxdt_parts, axis=1)
        dcs_x = jnp.concatenate(dcs_parts, axis=1)
        et = et_ref[...]
        dcs_tot = dcs + dcs_t.T + _fold_heads(dcs_x, et)
        da_dt = _dot_hi((coli >= rowi).astype(F32), dcs_tot)
        ddt = da_dt * a_neg + _fold_heads(dxdt * xs, et)
        ddt_raw = ddt * _sigmoid(dt_raw_v + bias_v)
        ddt_ref[...] = ddt_raw
        dvec_ref[0:1, :] += jnp.sum(ddt_raw, axis=0, keepdims=True)
        dvec_ref[1:2, :] += jnp.sum(da_dt * dt, axis=0, keepdims=True) * a_neg
        dact_ref[:, :SSD_INNER] = dyv * dx_ref[...] + dxdt * dt_x

        @pl.when(i == nc - 1)
        def _():
            dvec_ref[2:3, :] = _fold_heads(dd_sc[...], et)[0:1, :]

    const = lambda shape: pl.BlockSpec(shape, lambda c: (0, 0))
    rev = lambda c: (nc - 1 - c, 0)
    return pl.pallas_call(
        body, name="ssd_scan_bwd", grid=(nc,),
        in_specs=[pl.BlockSpec((SSD_L, SSD_INNER), rev), pl.BlockSpec((SSD_L, SSD_CONV), rev),
                  pl.BlockSpec((SSD_L, LANES), rev),
                  pl.BlockSpec((1, SSD_N, SSD_INNER), lambda c: (nc - 1 - c, 0, 0)),
                  const((1, LANES)), const((1, LANES)), const((1, SSD_INNER)), const((SSD_L, SSD_L)),
                  const((SSD_INNER, LANES))],
        out_specs=[pl.BlockSpec((SSD_L, SSD_CONV), rev), pl.BlockSpec((SSD_L, LANES), rev), const((SUBLANES, LANES))],
        out_shape=[jax.ShapeDtypeStruct((SEQ, SSD_CONV), F32), jax.ShapeDtypeStruct((SEQ, LANES), F32),
                   jax.ShapeDtypeStruct((SUBLANES, LANES), F32)],
        scratch_shapes=[pltpu.VMEM((SSD_N, SSD_INNER), F32), pltpu.VMEM((SUBLANES, SSD_INNER), F32)],
        compiler_params=_params(("arbitrary",)),
    )(dy, act, dt_raw, hprev_all, bias, alog, d_x, tril, expand_t)


L1_T = 256


def _gated_norm(y, z, nw):
    y2 = y * _silu(z)
    gw = SSD_INNER // SSD_GROUPS
    outs, xhs, rss = [], [], []
    for g in range(SSD_GROUPS):
        gs = slice(g * gw, (g + 1) * gw)
        v = y2[:, gs]
        rs = lax.rsqrt(jnp.mean(v * v, axis=-1, keepdims=True) + 1e-6)
        xhs.append(v * rs)
        rss.append(rs)
        outs.append(v * rs * nw[:, gs])
    return outs, xhs, rss


def _l1_out(y, z, nw, w_out, x1, g, b, target):
    t = L1_T

    def body(y_ref, z_ref, nw_ref, w_ref, x1_ref, g_ref, b_ref, tg_ref, yn_ref, dv_ref, dgb_ref, loss_ref):
        @pl.when(pl.program_id(0) == 0)
        def _():
            dgb_ref[...] = jnp.zeros_like(dgb_ref)
            loss_ref[...] = jnp.zeros_like(loss_ref)

        outs, _, _ = _gated_norm(y_ref[...], z_ref[...], nw_ref[...])
        yn = _mx(jnp.concatenate(outs, axis=1))
        yn_ref[...] = yn
        v = DN_ALPHA * x1_ref[...] + _dot(yn, w_ref[...])
        gv = g_ref[...]
        err = _ln(v, gv, b_ref[...]) - tg_ref[...]
        rowsum = jnp.sum(err * err, axis=1, keepdims=True)
        loss_ref[...] += 0.5 * jnp.sum(rowsum, axis=0, keepdims=True) / D_MODEL
        dv, dg, db = _ln_bwd(v, gv, err / D_MODEL)
        dv_ref[...] = dv
        dgb_ref[0:1, :] += dg
        dgb_ref[1:2, :] += db

    wide = pl.BlockSpec((t, SSD_INNER), lambda i: (i, 0))
    full = pl.BlockSpec((t, D_MODEL), lambda i: (i, 0))
    vec = pl.BlockSpec((1, D_MODEL), lambda i: (0, 0))
    return pl.pallas_call(
        body, name="l1_out", grid=(SEQ // t,),
        in_specs=[wide, wide, pl.BlockSpec((1, SSD_INNER), lambda i: (0, 0)),
                  pl.BlockSpec((SSD_INNER, D_MODEL), lambda i: (0, 0)), full, vec, vec, full],
        out_specs=[wide, full, pl.BlockSpec((SUBLANES, D_MODEL), lambda i: (0, 0)),
                   pl.BlockSpec((SUBLANES, LANES), lambda i: (0, 0))],
        out_shape=[jax.ShapeDtypeStruct((SEQ, SSD_INNER), MXU_DTYPE), jax.ShapeDtypeStruct((SEQ, D_MODEL), F32),
                   jax.ShapeDtypeStruct((SUBLANES, D_MODEL), F32), jax.ShapeDtypeStruct((SUBLANES, LANES), F32)],
        compiler_params=_params(("arbitrary",)),
    )(y, z, nw, w_out, x1, g, b, target)


def _l1_gate_bwd(dv1, w_out, y, z, nw):
    t = L1_T
    gw = SSD_INNER // SSD_GROUPS

    def body(dv_ref, w_ref, y_ref, z_ref, nw_ref, dy_ref, dz_ref, dnw_ref):
        @pl.when(pl.program_id(0) == 0)
        def _():
            dnw_ref[...] = jnp.zeros_like(dnw_ref)

        dyn = _dot_nt(_mx(dv_ref[...]), w_ref[...])
        yv, zv, nwv = y_ref[...], z_ref[...], nw_ref[...]
        _, xhs, rss = _gated_norm(yv, zv, nwv)
        sz, dsz = _silu(zv), _dsilu(zv)
        for g in range(SSD_GROUPS):
            gs = slice(g * gw, (g + 1) * gw)
            d_out = dyn[:, gs]
            xh = xhs[g]
            dnw_ref[0:1, gs] += jnp.sum(d_out * xh, axis=0, keepdims=True)
            dxh = d_out * nwv[:, gs]
            dy2 = rss[g] * (dxh - xh * jnp.mean(dxh * xh, axis=-1, keepdims=True))
            dy_ref[:, gs] = dy2 * sz[:, gs]
            dz_ref[:, gs] = _mx(dy2 * yv[:, gs] * dsz[:, gs])

    wide = pl.BlockSpec((t, SSD_INNER), lambda i: (i, 0))
    return pl.pallas_call(
        body, name="l1_gate_bwd", grid=(SEQ // t,),
        in_specs=[pl.BlockSpec((t, D_MODEL), lambda i: (i, 0)), pl.BlockSpec((SSD_INNER, D_MODEL), lambda i: (0, 0)),
                  wide, wide, pl.BlockSpec((1, SSD_INNER), lambda i: (0, 0))],
        out_specs=[wide, wide, pl.BlockSpec((SUBLANES, SSD_INNER), lambda i: (0, 0))],
        out_shape=[jax.ShapeDtypeStruct((SEQ, SSD_INNER), F32), jax.ShapeDtypeStruct((SEQ, SSD_INNER), MXU_DTYPE),
                   jax.ShapeDtypeStruct((SUBLANES, SSD_INNER), F32)],
        compiler_params=_params(("arbitrary",)),
    )(dv1, w_out, y, z, nw)


MESH = pl.DeviceIdType.MESH
ANY = pl.BlockSpec(memory_space=pl.ANY)


def _flip(v, bit):
    return 1 - v if bit else v


def _all_gather(blocks, name):
    n = len(blocks)

    def body(*refs):
        x_refs, out_refs = refs[:n], refs[n:2 * n]
        send_sems, recv_sems, local_sems = refs[2 * n:]
        mx, my, mc = lax.axis_index("x"), lax.axis_index("y"), lax.axis_index("c")
        me, sibling = (mx, my, mc), (mx, my, 1 - mc)
        chips = [(1 - mx, my), (mx, 1 - my), (1 - mx, 1 - my)]

        def copy(a, k, block, to, own=False):
            px, py, pc = block
            slot = out_refs[a].at[4 * px + 2 * py + pc]
            return pltpu.make_async_remote_copy(
                src_ref=x_refs[a] if own else slot, dst_ref=slot,
                send_sem=send_sems.at[7 * a + k], recv_sem=recv_sems.at[7 * a + k], device_id=to, device_id_type=MESH)

        mine = [pltpu.make_async_copy(x_refs[a], out_refs[a].at[4 * mx + 2 * my + mc], local_sems.at[a])
                for a in range(n)]
        first = []
        for a in range(n):
            mine[a].start()
            first.append(copy(a, 0, me, sibling, own=True))
            first += [copy(a, 1 + j, me, (*chip, mc), own=True) for j, chip in enumerate(chips)]
        for cp in first:
            cp.start()
        passed = []
        for j, chip in enumerate(chips):
            for a in range(n):
                copy(a, 1 + j, (*chip, mc), me).wait_recv()
                fwd = copy(a, 4 + j, (*chip, mc), sibling)
                fwd.start()
                passed.append(fwd)
        for a in range(n):
            copy(a, 0, sibling, me).wait_recv()
            for j, chip in enumerate(chips):
                copy(a, 4 + j, (*chip, 1 - mc), me).wait_recv()
        for cp in first + passed:
            cp.wait_send()
        for cp in mine:
            cp.wait()

    return pl.pallas_call(
        body, name=name, in_specs=[ANY] * n, out_specs=[ANY] * n,
        out_shape=[jax.ShapeDtypeStruct((N_DEV,) + b.shape, b.dtype) for b in blocks],
        scratch_shapes=[pltpu.SemaphoreType.DMA((7 * n,)), pltpu.SemaphoreType.DMA((7 * n,)),
                        pltpu.SemaphoreType.DMA((n,))],
    )(*blocks)


def _exchange(scatter, bcast, name):
    n = len(scatter) + len(bcast)

    def body(*refs):
        copies = _peer_copies(refs[:n], refs[n:2 * n], refs[2 * n:], len(scatter))
        for cp in copies:
            cp.start()
        for cp in copies:
            cp.wait()

    return pl.pallas_call(
        body, name=name, in_specs=[ANY] * n, out_specs=[ANY] * n,
        out_shape=_exchange_shapes(scatter, bcast), scratch_shapes=_exchange_sems(n),
    )(*scatter, *bcast)


def _exchange_shapes(scatter, bcast):
    return ([jax.ShapeDtypeStruct(a.shape, a.dtype) for a in scatter]
            + [jax.ShapeDtypeStruct((N_DEV,) + a.shape, a.dtype) for a in bcast])


def _exchange_sems(n):
    return [pltpu.SemaphoreType.DMA((7 * n,)), pltpu.SemaphoreType.DMA((7 * n,)), pltpu.SemaphoreType.DMA((n,))]


def _peer_copies(in_refs, out_refs, sems, n_scatter):
    send_sems, recv_sems, local_sems = sems
    n = len(in_refs)
    mx, my, mc = lax.axis_index("x"), lax.axis_index("y"), lax.axis_index("c")
    me = 4 * mx + 2 * my + mc

    def src(a, slot):
        return in_refs[a].at[slot] if a < n_scatter else in_refs[a]

    copies = [pltpu.make_async_copy(src(a, me), out_refs[a].at[me], local_sems.at[a]) for a in range(n)]
    for k in range(1, N_DEV):
        px, py, pc = _flip(mx, (k >> 2) & 1), _flip(my, (k >> 1) & 1), _flip(mc, k & 1)
        for a in range(n):
            copies.append(pltpu.make_async_remote_copy(
                src_ref=src(a, 4 * px + 2 * py + pc), dst_ref=out_refs[a].at[me],
                send_sem=send_sems.at[7 * a + k - 1], recv_sem=recv_sems.at[7 * a + k - 1],
                device_id=(px, py, pc), device_id_type=MESH))
    return copies


def _segments(col_map, width):
    segs = []
    for lo, hi, arr, alo in col_map:
        for s in range(N_DEV):
            a, b = max(lo, s * width), min(hi, (s + 1) * width)
            if a < b:
                segs.append((s, a - s * width, b - a, arr, alo + a - lo))
    return segs


COPY_ROWS = 256


def _unshard(g8, col_map, widths, name):
    _, r, w = g8.shape
    rb = min(r, COPY_ROWS)
    segs = _segments(col_map, w)

    def body(g_ref, *o_refs):
        for o_ref in o_refs:
            o_ref[...] = jnp.zeros_like(o_ref)
        for s, llo, n, arr, alo in segs:
            o_refs[arr][:, alo:alo + n] = g_ref[s, :, llo:llo + n]

    return pl.pallas_call(
        body, name=name, grid=(r // rb,),
        in_specs=[pl.BlockSpec((N_DEV, rb, w), lambda i: (0, i, 0))],
        out_specs=[pl.BlockSpec((rb, n), lambda i: (i, 0)) for n in widths],
        out_shape=[jax.ShapeDtypeStruct((r, n), g8.dtype) for n in widths],
        compiler_params=_params(("parallel",)),
    )(g8)


def _reshard(srcs, col_map, w, dtype, name):
    r = srcs[0].shape[0]
    rb = min(r, COPY_ROWS)
    segs = _segments(col_map, w)

    def body(*refs):
        o_ref = refs[-1]
        for s, llo, n, arr, alo in segs:
            o_ref[s, :, llo:llo + n] = refs[arr][:, alo:alo + n].astype(dtype)

    return pl.pallas_call(
        body, name=name, grid=(r // rb,),
        in_specs=[pl.BlockSpec((rb, a.shape[1]), lambda i: (i, 0)) for a in srcs],
        out_specs=pl.BlockSpec((N_DEV, rb, w), lambda i: (0, i, 0)),
        out_shape=jax.ShapeDtypeStruct((N_DEV, r, w), dtype),
        compiler_params=_params(("parallel",)),
    )(*srcs)


def _adamw(parts, w, m, v, name):
    r, c = w.shape
    tr = COPY_ROWS if r % COPY_ROWS == 0 else r

    def body(p_ref, w_ref, m_ref, v_ref, g_ref, d_ref, mo_ref, vo_ref):
        g = p_ref[0].astype(F32)
        for s in range(1, N_DEV):
            g = g + p_ref[s].astype(F32)
        g_ref[...] = g
        d_ref[...], mo_ref[...], vo_ref[...] = _adamw_math(g, w_ref[...], m_ref[...], v_ref[...])

    blk = pl.BlockSpec((tr, c), lambda i: (i, 0))
    out = jax.ShapeDtypeStruct((r, c), F32)
    return pl.pallas_call(
        body, name=name, grid=(r // tr,),
        in_specs=[pl.BlockSpec((N_DEV, tr, c), lambda i: (0, i, 0)), blk, blk, blk],
        out_specs=[blk, blk, blk, blk], out_shape=[out, out, out, out],
        compiler_params=_params(("parallel",)),
    )(parts, w, m, v)


def _adamw_math(g, w, m, v):
    mn = ADAM_B1 * m + (1.0 - ADAM_B1) * g
    vn = ADAM_B2 * v + (1.0 - ADAM_B2) * (g * g)
    m_hat = mn / (1.0 - ADAM_B1 ** ADAM_STEP)
    v_hat = vn / (1.0 - ADAM_B2 ** ADAM_STEP)
    return -ADAM_LR * (m_hat / (jnp.sqrt(v_hat) + ADAM_EPS) + ADAM_WD * w), mn, vn


SMALL = (("ab_conv_w", 0, 4, 64), ("ssd_conv_w", 4, 4, 384), ("ssd_conv_b", 8, 1, 384), ("ssd_norm", 9, 1, 256),
         ("ssd_ln_g", 10, 1, 128), ("ssd_ln_b", 11, 1, 128))
VECS = (("ab_conv_b", 512), ("ab_gate_a_b", 512), ("ab_gate_x_b", 512), ("ab_lambda", 512), ("mla_q_norm", 256),
        ("mla_kv_norm", 128), ("ab_ln_g", 1024), ("ab_ln_b", 1024), ("ssd_dt_bias", 32), ("ssd_a_log", 32),
        ("ssd_d", 32))
GATES = ("ab_gate_a_w", "ab_gate_x_w")
SMALL_NAMES = tuple(n for n, *_ in SMALL) + tuple(n for n, _ in VECS) + GATES
VMEM_WHOLE = pl.BlockSpec(memory_space=pltpu.VMEM)


def _view2d(name, a):
    if name in GATES:
        return a.reshape(RNN_W, 64)
    return a[0] if a.ndim == 3 else a


def _unshard_small(g):
    widths = (512, 3072, 3072, 2048, 1024, 1024)

    def body(*refs):
        ins, outs = refs[:6], refs[6:]
        outs[0][...] = jnp.zeros_like(outs[0])
        outs[1][...] = jnp.zeros_like(outs[1])
        for (_, _, nr, c), i_ref, o_ref in zip(SMALL, ins, outs):
            for j in range(N_DEV):
                o_ref[0:nr, j * c:(j + 1) * c] = i_ref[j]

    return pl.pallas_call(
        body, name="unshard_small", in_specs=[VMEM_WHOLE] * 6, out_specs=[VMEM_WHOLE] * 6,
        out_shape=[jax.ShapeDtypeStruct((SUBLANES if nr == 4 else 1, w), F32) for (_, _, nr, _), w in zip(SMALL, widths)],
    )(*g)


def _prep_repl(ga, gx, dt_bias, a_log, d):
    def body(ga_ref, gx_ref, b_ref, al_ref, d_ref, wa_ref, wx_ref, b128_ref, al128_ref, dx_ref):
        wa_ref[...] = jnp.zeros_like(wa_ref)
        wx_ref[...] = jnp.zeros_like(wx_ref)
        for hd in range(8):
            hs = slice(hd * 64, (hd + 1) * 64)
            wa_ref[hs, hs] = _mx(ga_ref[hs, :])
            wx_ref[hs, hs] = _mx(gx_ref[hs, :])
        b128_ref[...] = jnp.zeros_like(b128_ref)
        al128_ref[...] = jnp.zeros_like(al128_ref)
        b128_ref[:, 0:SSD_HEADS] = b_ref[...]
        al128_ref[:, 0:SSD_HEADS] = al_ref[...]
        dv = d_ref[...]
        for hd in range(SSD_HEADS):
            dx_ref[:, hd * SSD_P:(hd + 1) * SSD_P] = jnp.broadcast_to(dv[:, hd:hd + 1], (1, SSD_P))

    return pl.pallas_call(
        body, name="prep_repl", in_specs=[VMEM_WHOLE] * 5, out_specs=[VMEM_WHOLE] * 5,
        out_shape=[jax.ShapeDtypeStruct((RNN_W, RNN_W), MXU_DTYPE), jax.ShapeDtypeStruct((RNN_W, RNN_W), MXU_DTYPE),
                   jax.ShapeDtypeStruct((1, LANES), F32), jax.ShapeDtypeStruct((1, LANES), F32),
                   jax.ShapeDtypeStruct((1, SSD_INNER), F32)],
    )(ga, gx, dt_bias, a_log, d)


def _pack_small(dvec0, g_wa, g_wx, dqnw, dknw, dgb0, dvec1, dcw1, dnw, dgb1):
    def body(dvec0_ref, gwa_ref, gwx_ref, dqn_ref, dkn_ref, dgb0_ref, dvec1_ref, dcw1_ref, dnw_ref, dgb1_ref,
             sm_ref, vec_ref, ga_ref, gx_ref):
        sm_ref[...] = jnp.zeros_like(sm_ref)
        vec_ref[...] = jnp.zeros_like(vec_ref)
        sharded = ((dvec0_ref, 4), (dcw1_ref, 0), (dcw1_ref, 4), (dnw_ref, 0), (dgb1_ref, 0), (dgb1_ref, 1))
        for (_, r0, nr, c), (src, sr) in zip(SMALL, sharded):
            for j in range(N_DEV):
                sm_ref[j, r0:r0 + nr, 0:c] = src[sr:sr + nr, j * c:(j + 1) * c]
        vectors = ((dvec0_ref, 3), (dvec0_ref, 0), (dvec0_ref, 1), (dvec0_ref, 2), (dqn_ref, 0), (dkn_ref, 0),
                   (dgb0_ref, 0), (dgb0_ref, 1), (dvec1_ref, 0), (dvec1_ref, 1), (dvec1_ref, 2))
        for row, ((_, c), (src, sr)) in enumerate(zip(VECS, vectors)):
            vec_ref[row:row + 1, 0:c] = src[sr:sr + 1, 0:c]
        for hd in range(8):
            hs = slice(hd * 64, (hd + 1) * 64)
            ga_ref[hs, :] = gwa_ref[hs, hs]
            gx_ref[hs, :] = gwx_ref[hs, hs]

    return pl.pallas_call(
        body, name="pack_small", in_specs=[VMEM_WHOLE] * 10, out_specs=[VMEM_WHOLE] * 4,
        out_shape=[jax.ShapeDtypeStruct((N_DEV, 16, 384), F32), jax.ShapeDtypeStruct((16, 1024), F32),
                   jax.ShapeDtypeStruct((RNN_W, 64), F32), jax.ShapeDtypeStruct((RNN_W, 64), F32)],
    )(dvec0, g_wa, g_wx, dqnw, dknw, dgb0, dvec1, dcw1, dnw, dgb1)


def _adamw_small(recv_sm, recv_vec, recv_ga, recv_gx, wmv):
    plan = ([(0, r0, nr, c) for _, r0, nr, c in SMALL] + [(1, row, 1, c) for row, (_, c) in enumerate(VECS)]
            + [(2, 0, RNN_W, 64), (3, 0, RNN_W, 64)])
    n = len(plan)

    def body(*refs):
        recv, ins, outs = refs[:4], refs[4:4 + 3 * n], refs[4 + 3 * n:]
        for i, (src, r0, nr, c) in enumerate(plan):
            g = recv[src][0, r0:r0 + nr, 0:c]
            for s in range(1, N_DEV):
                g = g + recv[src][s, r0:r0 + nr, 0:c]
            w_ref, m_ref, v_ref = ins[3 * i:3 * i + 3]
            outs[4 * i][...] = g
            outs[4 * i + 1][...], outs[4 * i + 2][...], outs[4 * i + 3][...] = _adamw_math(
                g, w_ref[...], m_ref[...], v_ref[...])

    flat = [a for t in wmv for a in t]
    return pl.pallas_call(
        body, name="adamw_small", in_specs=[VMEM_WHOLE] * (4 + 3 * n), out_specs=[VMEM_WHOLE] * (4 * n),
        out_shape=[jax.ShapeDtypeStruct(t[0].shape, F32) for t in wmv for _ in range(4)],
    )(recv_sm, recv_vec, recv_ga, recv_gx, *flat)


BIG_L0 = ("ab_w_in", "ab_w_out", "mla_w_uq", "mla_w_ukv")
BIG_L1 = ("ssd_w_in", "ssd_w_out")

MAP_W0 = ((0, 512, 0, 1024), (512, 1536, 0, 0), (1536, 1920, 0, 1536), (1920, 1952, 0, 1984))
MAP_W1 = ((0, 2048, 0, 0), (2048, 5120, 1, 0), (5120, 5152, 2, 0))
MAP_WQ = tuple((96 * hd, 96 * hd + 96, 0, 128 * hd) for hd in range(8))
MAP_WKV = (tuple((128 * hd, 128 * hd + 64, 0, 128 * hd) for hd in range(8))
           + tuple((128 * hd + 64, 128 * hd + 128, 0, 1024 + 64 * hd) for hd in range(8)))
MAP_G0 = ((0, 512, 0, 0), (512, 1536, 1, 0), (1536, 1920, 2, 0), (1920, 1952, 2, 448))


def kernel(x, positions, ab_w_in, ab_conv_w, ab_conv_b, ab_gate_a_w, ab_gate_a_b, ab_gate_x_w, ab_gate_x_b, ab_lambda, mla_q_norm, mla_kv_norm, mla_w_uq, mla_w_ukv, ab_w_out, ab_ln_g, ab_ln_b, ssd_w_in, ssd_conv_w, ssd_conv_b, ssd_dt_bias, ssd_a_log, ssd_d, ssd_norm, ssd_w_out, ssd_ln_g, ssd_ln_b, loss_target, m_ab_w_in, m_ab_conv_w, m_ab_conv_b, m_ab_gate_a_w, m_ab_gate_a_b, m_ab_gate_x_w, m_ab_gate_x_b, m_ab_lambda, m_mla_q_norm, m_mla_kv_norm, m_mla_w_uq, m_mla_w_ukv, m_ab_w_out, m_ab_ln_g, m_ab_ln_b, m_ssd_w_in, m_ssd_conv_w, m_ssd_conv_b, m_ssd_dt_bias, m_ssd_a_log, m_ssd_d, m_ssd_norm, m_ssd_w_out, m_ssd_ln_g, m_ssd_ln_b, v_ab_w_in, v_ab_conv_w, v_ab_conv_b, v_ab_gate_a_w, v_ab_gate_a_b, v_ab_gate_x_w, v_ab_gate_x_b, v_ab_lambda, v_mla_q_norm, v_mla_kv_norm, v_mla_w_uq, v_mla_w_ukv, v_ab_w_out, v_ab_ln_g, v_ab_ln_b, v_ssd_w_in, v_ssd_conv_w, v_ssd_conv_b, v_ssd_dt_bias, v_ssd_a_log, v_ssd_d, v_ssd_norm, v_ssd_w_out, v_ssd_ln_g, v_ssd_ln_b):
    args = dict(locals())
    bf = MXU_DTYPE
    big = {n: [args[pre + n][0] for pre in ("", "m_", "v_")] for n in BIG_L0 + BIG_L1}
    sml = {n: [_view2d(n, args[pre + n]) for pre in ("", "m_", "v_")] for n in SMALL_NAMES}

    gathered = _all_gather([big[n][0].astype(bf) for n in BIG_L0] + [sml[n][0] for n, *_ in SMALL], "gather_params")
    g8 = dict(zip(BIG_L0, gathered))
    p = {"wo0": g8["ab_w_out"].reshape(D_MODEL, D_MODEL)}
    p["w0p"], = _unshard(g8["ab_w_in"], MAP_W0, (2048,), "unshard_w0")
    p["wq"], = _unshard(g8["mla_w_uq"], MAP_WQ, (1024,), "unshard_wq")
    p["wkv"], = _unshard(g8["mla_w_ukv"], MAP_WKV, (1536,), "unshard_wkv")
    p["cw0"], p["cw1"], p["cb1"], p["nw"], p["g1"], p["b1"] = _unshard_small(gathered[len(BIG_L0):])
    p["wa"], p["wx"], p["dt_bias"], p["a_log"], p["d_x"] = _prep_repl(
        sml["ab_gate_a_w"][0], sml["ab_gate_x_w"][0], sml["ssd_dt_bias"][0], sml["ssd_a_log"][0], sml["ssd_d"][0])
    for key, n in (("cb0", "ab_conv_b"), ("ba", "ab_gate_a_b"), ("bx", "ab_gate_x_b"), ("lam", "ab_lambda"),
                   ("qn_w", "mla_q_norm"), ("kn_w", "mla_kv_norm"), ("g0", "ab_ln_g"), ("b0", "ab_ln_b")):
        p[key] = sml[n][0]

    acc, recv_l1, loss_part, grad_x = _local_step(
        x[0], positions[0], loss_target[0], p, [big[n][0].astype(bf) for n in BIG_L1])

    send = [_reshard([acc["g_rnn"], acc["g_gate"], acc["g_tail"]], MAP_G0, 244, bf, "reshard_w0"),
            acc["g_wo0"].astype(bf).reshape(N_DEV, 128, D_MODEL),
            _reshard([acc["g_wq"]], MAP_WQ, 96, bf, "reshard_wq"), _reshard([acc["g_wkv"]], MAP_WKV, 128, bf, "reshard_wkv")]
    sm_slots, vec_rows, ga, gx = _pack_small(*(acc[k] for k in (
        "dvec0", "g_wa", "g_wx", "dqnw", "dknw", "dgb0", "dvec1", "dcw1", "dnw", "dgb1")))
    recv = _exchange(send + [sm_slots], [vec_rows, ga, gx], "exchange_grads")

    outs = {}
    kinds = ("grad", "delta", "new_m", "new_v")
    for n, parts in zip(BIG_L0 + BIG_L1, list(recv[:4]) + list(recv_l1)):
        for kind, res in zip(kinds, _adamw(parts, *big[n], "adamw_" + n)):
            outs[kind, n] = res[None]
    res = _adamw_small(*recv[4:], [sml[n] for n in SMALL_NAMES])
    for i, n in enumerate(SMALL_NAMES):
        for k, kind in enumerate(kinds):
            outs[kind, n] = res[4 * i + k].reshape(args[n].shape)

    loss = lax.psum(loss_part, ("x", "y", "c"))
    order = ["ab_w_in", "ab_conv_w", "ab_conv_b", "ab_gate_a_w", "ab_gate_a_b", "ab_gate_x_w", "ab_gate_x_b",
             "ab_lambda", "mla_q_norm", "mla_kv_norm", "mla_w_uq", "mla_w_ukv", "ab_w_out", "ab_ln_g", "ab_ln_b",
             "ssd_w_in", "ssd_conv_w", "ssd_conv_b", "ssd_dt_bias", "ssd_a_log", "ssd_d", "ssd_norm", "ssd_w_out",
             "ssd_ln_g", "ssd_ln_b"]
    return (loss, grad_x[None], *[outs[kind, n] for kind in ("grad", "delta", "new_m", "new_v") for n in order])


def _local_step(x, pos, target, p, l1_blocks):
    bf = MXU_DTYPE
    inv_freq = 10000.0 ** (-jnp.arange(0, 32, 2, dtype=F32) / 32)
    ang = pos.astype(F32)[:, None] * inv_freq
    cos, sin = jnp.cos(ang), jnp.sin(ang)
    zeros = lambda n: jnp.zeros((SEQ, n), F32)
    tc = jnp.concatenate([jnp.ones((SEQ, 64), F32), cos, cos, zeros(32)], axis=1)
    tsa = jnp.concatenate([zeros(64), -sin, zeros(48)], axis=1)
    tsb = jnp.concatenate([zeros(80), sin, zeros(32)], axis=1)

    w0p, wq, wkv, wo0, wa, wxg = (p[k] for k in ("w0p", "wq", "wkv", "wo0", "wa", "wx"))
    cw0, cb0, ba, bx, lam = (p[k] for k in ("cw0", "cb0", "ba", "bx", "lam"))
    qn_w, kn_w, g0, b0 = (p[k] for k in ("qn_w", "kn_w", "g0", "b0"))
    cw1, cb1, dt_bias, a_log, d_x, nw, g1, b1 = (p[k] for k in ("cw1", "cb1", "dt_bias", "a_log", "d_x", "nw", "g1", "b1"))
    tril = jnp.tril(jnp.ones((SSD_L, SSD_L), F32))
    expand_t = (jnp.arange(SSD_INNER)[:, None] // SSD_P == jnp.arange(LANES)[None, :]).astype(jnp.bfloat16)

    xb = x.astype(bf)
    proj0 = _mm(xb, w0p, "nn", name="l0_in")
    xc, h = _rglru_fwd(proj0, cw0, cb0, wa, ba, wxg, bx, lam)
    qn, kn, qc, kc, vc = _mla_fwd(proj0, qn_w, kn_w, wq, wkv, tc, tsa, tsb)
    o, lse, (w1_8, wo1_8) = _flash_fwd(qc, kc, vc, bcast=l1_blocks)
    w1z, w1x, w1d = _unshard(w1_8, MAP_W1, (2048, 3072, 128), "unshard_w1")
    wo1 = wo1_8.reshape(SSD_INNER, D_MODEL)
    y0, v0, x1, x1b = _l0_out(h, o, proj0, x, wo0, g0, b0)

    z = _mm(x1b, w1z, "nn", name="l1_in_z")
    xbc = _mm(x1b, w1x, "nn", name="l1_in_xbc")
    dt_raw = _mm(x1b, w1d, "nn", name="l1_in_dt")
    pre, act = _ssd_conv_fwd(xbc, cw1, cb1)
    ys, hprev = _ssd_scan_fwd(act, dt_raw, dt_bias, a_log, d_x, tril, expand_t)
    yn, dv1, dgb1, loss8 = _l1_out(ys, z, nw, wo1, x1, g1, b1, target)

    g_wo1 = _mm(yn, dv1, "tn", name="l1_dwout")
    dys, dz, dnw = _l1_gate_bwd(dv1, wo1, ys, z, nw)
    dact, ddt_raw, dvec1 = _ssd_scan_bwd(dys, act, dt_raw, hprev, dt_bias, a_log, d_x, tril, expand_t)
    dxbc, dcw1 = _ssd_conv_bwd(dact, pre, xbc, cw1)
    g_z, g_xbc = _mm(x1b, dz, "tn", name="l1_dw_z"), _mm(x1b, dxbc, "tn", name="l1_dw_xbc")
    g_dt = _mm(x1b, ddt_raw, "tn", name="l1_dw_dt")
    dx1 = _mm(dz, w1z, "nt", name="l1_dx_z", add=dv1, add_scale=DN_ALPHA)
    dx1 = _mm(dxbc, w1x, "nt", name="l1_dx_xbc", add=dx1)

    dv0, dgb0 = _ln_bwd_call(v0, dx1, ddt_raw, w1d, g0)
    g_wo0 = _mm(y0, dv0, "tn", name="l0_dwout")
    dh, do, dgate = _gate_bwd(dv0, wo0, h, o, proj0)
    send_l1 = [_reshard([g_z, g_xbc, g_dt], MAP_W1, 644, bf, "reshard_w1"), g_wo1.astype(bf).reshape(N_DEV, 256, D_MODEL)]
    dq, dk, dvv, recv_l1 = _flash_bwd(qc, kc, vc, o, do, lse, scatter=send_l1)
    dqraw, dkvraw, dtail, dqnw, dknw = _mla_bwd(dq, dk, dvv, proj0, qn_w, kn_w, wq, wkv, tc, tsa, tsb)
    g_wq = _mm(qn, dqraw, "tn", name="mla_dwq", tm=256)
    g_wkv = _mm(kn, dkvraw, "tn", name="mla_dwkv", tm=128, tn=512)
    dxr, g_wa, g_wx, dvec0 = _rglru_bwd(dh, xc, h, proj0, cw0, wa, ba, wxg, bx, lam)
    g_tail = _mm(xb, dtail, "tn", name="l0_dw_tail")
    g_rnn, g_gate = _mm(xb, dxr, "tn", name="l0_dw_rnn"), _mm(xb, dgate, "tn", name="l0_dw_gate")
    dx = _mm(dxr, w0p, "nt", name="l0_dx_rnn", add=dv0, add_scale=DN_ALPHA, b_col=P0_RNN)
    dx = _mm(dgate, w0p, "nt", name="l0_dx_gate", add=dx, b_col=0)
    dx = _mm(dtail, w0p, "nt", name="l0_dx_tail", add=dx, b_col=3)

    acc = {"g_rnn": g_rnn, "g_gate": g_gate, "g_tail": g_tail, "g_wo0": g_wo0, "g_wq": g_wq, "g_wkv": g_wkv,
           "dvec0": dvec0, "g_wa": g_wa, "g_wx": g_wx, "dqnw": dqnw, "dknw": dknw, "dgb0": dgb0, "dvec1": dvec1,
           "dcw1": dcw1, "dnw": dnw, "dgb1": dgb1}
    return acc, recv_l1, loss8[0, 0], dx
```

```python
import math

import jax
import jax.numpy as jnp
from jax import lax
from jax.experimental import pallas as pl
from jax.experimental.pallas import tpu as pltpu

F32 = jnp.float32
MXU_DTYPE = jnp.bfloat16

N_DEV = 8
SEQ = 4096
D_MODEL = 1024
DN_ALPHA = 4.0 ** 0.25
RNN_W = 512
MLA_HEADS = 8
ATT_SCALE = 96.0 ** -0.5
ATT_C = ATT_SCALE * math.log2(math.e)
RG_C = 8.0
SSD_INNER = 2048
SSD_HEADS = 32
SSD_P = 64
SSD_GROUPS = 4
SSD_N = 128
SSD_L = 128
SSD_CONV = 3072
LANES = 128
SUBLANES = 8
VMEM_LIMIT = 56 * 1024 * 1024

ADAM_LR, ADAM_B1, ADAM_B2, ADAM_EPS, ADAM_WD, ADAM_STEP = 0.001, 0.9, 0.999, 1e-08, 0.01, 10

HIGHEST = lax.Precision.HIGHEST


def _params(sem, limit=VMEM_LIMIT):
    return pltpu.CompilerParams(dimension_semantics=sem, vmem_limit_bytes=limit)


def _dot(a, b):
    return lax.dot_general(a, b, (((1,), (0,)), ((), ())), preferred_element_type=F32)


def _dot_nt(a, b):
    return lax.dot_general(a, b, (((1,), (1,)), ((), ())), preferred_element_type=F32)


def _dot_tn(a, b):
    return lax.dot_general(a, b, (((0,), (0,)), ((), ())), preferred_element_type=F32)


def _dot_hi(a, b):
    return lax.dot_general(a, b, (((1,), (0,)), ((), ())), precision=HIGHEST, preferred_element_type=F32)


def _mx(v):
    return v.astype(MXU_DTYPE)


def _sigmoid(v):
    return 1.0 / (1.0 + jnp.exp(-v))


def _log1p_pos(e):
    poly = e * (1.0 - e * (0.5 - e * (1.0 / 3.0 - e * 0.25)))
    return jnp.where(e < 0.01, poly, jnp.log(1.0 + e))


def _softplus(v):
    return jnp.maximum(v, 0.0) + _log1p_pos(jnp.exp(-jnp.abs(v)))


def _neg_expm1(v):
    poly = -v * (1.0 + v * (0.5 + v * (1.0 / 6.0 + v * (1.0 / 24.0 + v * (1.0 / 120.0)))))
    return jnp.where(jnp.abs(v) < 0.1, poly, 1.0 - jnp.exp(v))


def _silu(v):
    return v * _sigmoid(v)


def _dsilu(v):
    s = _sigmoid(v)
    return s * (1.0 + v * (1.0 - s))


def _mm(a, b, mode, *, name, add=None, add_scale=1.0, out_dtype=F32, tm=None, tn=1024, tk=512, b_col=0):
    if mode == "tn":
        kdim, m = a.shape
        n = b.shape[1]
        tm, tn, tk = min(tm or 1024, m), min(tn, n), min(tk, kdim)

        def body_tn(a_ref, b_ref, o_ref):
            @pl.when(pl.program_id(2) == 0)
            def _():
                o_ref[...] = jnp.zeros_like(o_ref)

            o_ref[...] += _dot_tn(_mx(a_ref[...]), _mx(b_ref[...]))

        return pl.pallas_call(
            body_tn, name=name, grid=(m // tm, n // tn, kdim // tk),
            in_specs=[pl.BlockSpec((tk, tm), lambda i, j, k: (k, i)), pl.BlockSpec((tk, tn), lambda i, j, k: (k, j))],
            out_specs=pl.BlockSpec((tm, tn), lambda i, j, k: (i, j)),
            out_shape=jax.ShapeDtypeStruct((m, n), F32),
            compiler_params=_params(("parallel", "parallel", "arbitrary")),
        )(a, b)

    m, kdim = a.shape
    n = b.shape[1] if mode == "nn" else b.shape[0]
    tm, tn = min(tm or 1024, m), min(tn, n)
    has_add = add is not None

    def body(*refs):
        a_ref, b_ref = refs[0], refs[1]
        o_ref = refs[-1]
        av, bv = _mx(a_ref[...]), _mx(b_ref[...])
        acc = _dot(av, bv) if mode == "nn" else _dot_nt(av, bv)
        if has_add:
            acc = acc + add_scale * refs[2][...]
        o_ref[...] = acc.astype(out_dtype)

    b_spec = (pl.BlockSpec((kdim, tn), lambda i, j: (0, j)) if mode == "nn"
              else pl.BlockSpec((tn, kdim), lambda i, j: (j, b_col)))
    in_specs = [pl.BlockSpec((tm, kdim), lambda i, j: (i, 0)), b_spec]
    args = [a, b]
    if has_add:
        in_specs.append(pl.BlockSpec((tm, tn), lambda i, j: (i, j)))
        args.append(add)
    return pl.pallas_call(
        body, name=name, grid=(m // tm, n // tn), in_specs=in_specs,
        out_specs=pl.BlockSpec((tm, tn), lambda i, j: (i, j)),
        out_shape=jax.ShapeDtypeStruct((m, n), out_dtype),
        compiler_params=_params(("parallel", "parallel")),
    )(*args)


def _shift_down(blk, halo, s):
    if s == 0:
        return blk
    t = blk.shape[0]
    r = pltpu.roll(blk, s, 0)
    hr = pltpu.roll(halo, s, 0)
    row8 = lax.broadcasted_iota(jnp.int32, hr.shape, 0)
    head = jnp.where(row8 < s, hr, r[:SUBLANES])
    return jnp.concatenate([head, r[SUBLANES:]], axis=0) if t > SUBLANES else head


def _shift_up(blk, halo, s):
    if s == 0:
        return blk
    t = blk.shape[0]
    r = pltpu.roll(blk, t - s, 0)
    hr = pltpu.roll(halo, SUBLANES - s, 0)
    row8 = lax.broadcasted_iota(jnp.int32, hr.shape, 0)
    tail = jnp.where(row8 >= SUBLANES - s, hr, r[t - SUBLANES:])
    return jnp.concatenate([r[:t - SUBLANES], tail], axis=0) if t > SUBLANES else tail


def _scan_down(a, u):
    t = a.shape[0]
    row = lax.broadcasted_iota(jnp.int32, a.shape, 0)
    d = 1
    while d < t:
        keep = row >= d
        a_sh = jnp.where(keep, pltpu.roll(a, d, 0), 1.0)
        u_sh = jnp.where(keep, pltpu.roll(u, d, 0), 0.0)
        u = a * u_sh + u
        a = a * a_sh
        d *= 2
    return a, u


def _scan_up(a, u):
    t = a.shape[0]
    row = lax.broadcasted_iota(jnp.int32, a.shape, 0)
    d = 1
    while d < t:
        keep = row < t - d
        a_sh = jnp.where(keep, pltpu.roll(a, t - d, 0), 1.0)
        u_sh = jnp.where(keep, pltpu.roll(u, t - d, 0), 0.0)
        u = a * u_sh + u
        a = a * a_sh
        d *= 2
    return a, u


def _conv4(blk, halo, cw, cb):
    out = cb + blk * cw[3:4]
    for k in range(3):
        out = out + _shift_down(blk, halo, 3 - k) * cw[k:k + 1]
    return out


RG_T = 512
P0_RNN = 2


def _rg_gates(xc, wa, ba, wx, bx, lam):
    xcb = _mx(xc)
    r = _sigmoid(_dot(xcb, wa) + ba)
    ig = _sigmoid(_dot(xcb, wx) + bx)
    sp = _softplus(-lam)
    la = (-RG_C * r) * sp
    a = jnp.exp(la)
    mult = jnp.sqrt(_neg_expm1(2.0 * la))
    return r, ig, sp, a, mult


def _rglru_fwd(proj0, cw8, cb, wa, ba, wx, bx, lam):
    t, w = RG_T, RNN_W
    nb = SEQ // t

    def body(x_ref, halo_ref, cw_ref, cb_ref, wa_ref, ba_ref, wx_ref, bx_ref, lam_ref, xc_ref, h_ref, carry):
        i = pl.program_id(0)

        @pl.when(i == 0)
        def _():
            carry[...] = jnp.zeros_like(carry)

        blk = x_ref[...]
        halo = jnp.where(i > 0, halo_ref[...], 0.0)
        xc = _conv4(blk, halo, cw_ref[...], cb_ref[...])
        _, ig, _, a, mult = _rg_gates(xc, wa_ref[...], ba_ref[...], wx_ref[...], bx_ref[...], lam_ref[...])
        u = mult * (ig * xc)
        big_a, big_u = _scan_down(a, u)
        h = big_a * carry[SUBLANES - 1:SUBLANES, :] + big_u
        carry[...] = h[t - SUBLANES:]
        xc_ref[...] = xc
        h_ref[...] = h

    vec = pl.BlockSpec((1, w), lambda i: (0, 0))
    mat = pl.BlockSpec((w, w), lambda i: (0, 0))
    return pl.pallas_call(
        body, name="rglru_fwd", grid=(nb,),
        in_specs=[pl.BlockSpec((t, w), lambda i: (i, P0_RNN)),
                  pl.BlockSpec((SUBLANES, w), lambda i: (jnp.maximum(i * (t // SUBLANES) - 1, 0), P0_RNN)),
                  pl.BlockSpec((SUBLANES, w), lambda i: (0, 0)), vec, mat, vec, mat, vec, vec],
        out_specs=[pl.BlockSpec((t, w), lambda i: (i, 0)), pl.BlockSpec((t, w), lambda i: (i, 0))],
        out_shape=[jax.ShapeDtypeStruct((SEQ, w), F32), jax.ShapeDtypeStruct((SEQ, w), F32)],
        scratch_shapes=[pltpu.VMEM((SUBLANES, w), F32)],
        compiler_params=_params(("arbitrary",)),
    )(proj0, proj0, cw8, cb, wa, ba, wx, bx, lam)


def _rglru_bwd(dh, xc, h, proj0, cw8, wa, ba, wx, bx, lam):
    t, w = RG_T, RNN_W
    nb = SEQ // t
    tb = t // SUBLANES

    def body(dh_ref, xc_ref, h_ref, hh_ref, x_ref, cw_ref, wa_ref, ba_ref, wx_ref, bx_ref, lam_ref,
             dx_ref, dwa_ref, dwx_ref, dvec_ref, gcarry, dxc_next):
        i = pl.program_id(0)
        rev = nb - 1 - i

        @pl.when(i == 0)
        def _():
            gcarry[...] = jnp.zeros_like(gcarry)
            dxc_next[...] = jnp.zeros_like(dxc_next)
            dwa_ref[...] = jnp.zeros_like(dwa_ref)
            dwx_ref[...] = jnp.zeros_like(dwx_ref)
            dvec_ref[...] = jnp.zeros_like(dvec_ref)

        xc = xc_ref[...]
        wa_v, wx_v = wa_ref[...], wx_ref[...]
        lam_v = lam_ref[...]
        r, ig, sp, a, mult = _rg_gates(xc, wa_v, ba_ref[...], wx_v, bx_ref[...], lam_v)
        dhv = dh_ref[...]
        big_a, big_u = _scan_up(a, a * dhv)
        gg = big_a * gcarry[0:1, :] + big_u
        g = dhv + _shift_up(gg, gcarry[...], 1)
        gcarry[...] = gg[:SUBLANES]
        hhalo = jnp.where(rev > 0, hh_ref[...], 0.0)
        da = g * _shift_down(h_ref[...], hhalo, 1)
        d_mult = g * (ig * xc)
        d_i = g * (mult * xc)
        dxc = g * (mult * ig)
        d_la = da * a - d_mult * (a * a) / mult
        d_r = d_la * (-RG_C * sp)
        d_sp = jnp.sum(d_la * (-RG_C * r), axis=0, keepdims=True)
        d_pa = d_r * r * (1.0 - r)
        d_px = d_i * ig * (1.0 - ig)
        d_pab, d_pxb = _mx(d_pa), _mx(d_px)
        dxc = dxc + _dot_nt(d_pab, wa_v) + _dot_nt(d_pxb, wx_v)
        xcb = _mx(xc)
        dwa_ref[...] += _dot_tn(xcb, d_pab)
        dwx_ref[...] += _dot_tn(xcb, d_pxb)
        dvec_ref[0:1, :] += jnp.sum(d_pa, axis=0, keepdims=True)
        dvec_ref[1:2, :] += jnp.sum(d_px, axis=0, keepdims=True)
        dvec_ref[2:3, :] += d_sp * (-_sigmoid(-lam_v))
        dvec_ref[3:4, :] += jnp.sum(dxc, axis=0, keepdims=True)
        xblk = x_ref[...]
        cw = cw_ref[...]
        dx = dxc * cw[3:4]
        nxt = dxc_next[...]
        dvec_ref[7:8, :] += jnp.sum(dxc * xblk, axis=0, keepdims=True)
        for k in range(3):
            up = _shift_up(dxc, nxt, 3 - k)
            dvec_ref[4 + k:5 + k, :] += jnp.sum(up * xblk, axis=0, keepdims=True)
            dx = dx + up * cw[k:k + 1]
        dxc_next[...] = dxc[:SUBLANES]
        dx_ref[...] = _mx(dx)

    blk = pl.BlockSpec((t, w), lambda i: (nb - 1 - i, 0))
    halo = pl.BlockSpec((SUBLANES, w), lambda i: (jnp.maximum((nb - 1 - i) * tb - 1, 0), 0))
    vec = pl.BlockSpec((1, w), lambda i: (0, 0))
    mat = pl.BlockSpec((w, w), lambda i: (0, 0))
    return pl.pallas_call(
        body, name="rglru_bwd", grid=(nb,),
        in_specs=[blk, blk, blk, halo, pl.BlockSpec((t, w), lambda i: (nb - 1 - i, P0_RNN)),
                  pl.BlockSpec((SUBLANES, w), lambda i: (0, 0)), mat, vec, mat, vec, vec],
        out_specs=[blk, mat, mat, pl.BlockSpec((16, w), lambda i: (0, 0))],
        out_shape=[jax.ShapeDtypeStruct((SEQ, w), MXU_DTYPE), jax.ShapeDtypeStruct((w, w), F32),
                   jax.ShapeDtypeStruct((w, w), F32), jax.ShapeDtypeStruct((16, w), F32)],
        scratch_shapes=[pltpu.VMEM((SUBLANES, w), F32), pltpu.VMEM((SUBLANES, w), F32)],
        compiler_params=_params(("arbitrary",)),
    )(dh, xc, h, h, proj0, cw8, wa, ba, wx, bx, lam)


MLA_T = 512


def _rope(v, c, sa, sb):
    return v * c + pltpu.roll(v, LANES - 16, 1) * sa + pltpu.roll(v, 16, 1) * sb


def _rope_t(dv, c, sa, sb):
    return dv * c + pltpu.roll(dv * sa, 16, 1) + pltpu.roll(dv * sb, LANES - 16, 1)


def _rms(v, g, eps=1e-6):
    rs = lax.rsqrt(jnp.mean(v * v, axis=-1, keepdims=True) + eps)
    return v * rs * g, rs


def _mla_fwd(proj0, q_norm, kv_norm, wq, wkv, tc, tsa, tsb):
    t = MLA_T

    def body(cq_ref, ck_ref, qn_ref, kn_ref, wq_ref, wkv_ref, c_ref, sa_ref, sb_ref,
             oqn_ref, okn_ref, oq_ref, ok_ref, ov_ref):
        c, sa, sb = c_ref[...], sa_ref[...], sb_ref[...]
        ck = ck_ref[...]
        qn = _mx(_rms(cq_ref[...], qn_ref[...])[0])
        kn = _mx(_rms(ck[:, :LANES], kn_ref[...])[0])
        oqn_ref[...] = qn
        okn_ref[...] = kn
        krv = _rope(ck[:, LANES:], c, sa, sb)
        qraw = _dot(qn, wq_ref[...])
        kvraw = _dot(kn, wkv_ref[...])
        for hd in range(MLA_HEADS):
            sl = slice(hd * LANES, (hd + 1) * LANES)
            oq_ref[:, sl] = _mx(_rope(qraw[:, sl], c, sa, sb))
            ok_ref[:, sl] = _mx(kvraw[:, sl] + krv)
        ov_ref[...] = _mx(kvraw[:, 1024:])

    tab = pl.BlockSpec((t, LANES), lambda i: (i, 0))
    wide = pl.BlockSpec((t, 1024), lambda i: (i, 0))
    const = lambda shape: pl.BlockSpec(shape, lambda i: (0, 0))
    return pl.pallas_call(
        body, name="mla_fwd", grid=(SEQ // t,),
        in_specs=[pl.BlockSpec((t, 256), lambda i: (i, 6)), pl.BlockSpec((t, 256), lambda i: (i, 7)),
                  const((1, 256)), const((1, LANES)), const((256, 1024)), const((LANES, 1536)), tab, tab, tab],
        out_specs=[pl.BlockSpec((t, 256), lambda i: (i, 0)), tab, wide, wide, pl.BlockSpec((t, 512), lambda i: (i, 0))],
        out_shape=[jax.ShapeDtypeStruct((SEQ, 256), MXU_DTYPE), jax.ShapeDtypeStruct((SEQ, LANES), MXU_DTYPE),
                   jax.ShapeDtypeStruct((SEQ, 1024), MXU_DTYPE), jax.ShapeDtypeStruct((SEQ, 1024), MXU_DTYPE),
                   jax.ShapeDtypeStruct((SEQ, 512), MXU_DTYPE)],
        compiler_params=_params(("parallel",)),
    )(proj0, proj0, q_norm, kv_norm, wq, wkv, tc, tsa, tsb)


ATT_T = 1024


def _flash_fwd(q, k, v, bcast=()):
    t = ATT_T
    nb = SEQ // t

    steps = [(qi, ki) for qi in range(nb) for ki in range(qi + 1)]
    qi_tab = jnp.asarray([s[0] for s in steps], jnp.int32)
    ki_tab = jnp.asarray([s[1] for s in steps], jnp.int32)

    nx = len(bcast)

    def body(qi_ref, ki_ref, q_ref, k_ref, v_ref, *rest):
        x_refs, (o_ref, lse_ref), g_refs = rest[:nx], rest[nx:nx + 2], rest[nx + 2:2 * nx + 2]
        m_sc, acc_sc = rest[2 * nx + 2:2 * nx + 4]
        step = pl.program_id(1)
        qi, ki = qi_ref[step], ki_ref[step]
        if nx:
            copies = _peer_copies(x_refs, g_refs, rest[2 * nx + 4:], [])

            @pl.when((pl.program_id(0) == 0) & (step == 0))
            def _():
                for cp in copies:
                    cp.start()

        @pl.when(ki == 0)
        def _():
            m_sc[...] = jnp.full_like(m_sc, -jnp.inf)
            acc_sc[...] = jnp.zeros_like(acc_sc)

        def update(diagonal):
            vv = v_ref[...]
            lane_v = lax.broadcasted_iota(jnp.int32, vv.shape, 1)
            for hd in range(2):
                sl = slice(hd * LANES, (hd + 1) * LANES)
                s = _dot_nt(q_ref[:, sl], k_ref[:, sl])
                if diagonal:
                    s = jnp.where(lax.broadcasted_iota(jnp.int32, (t, t), 1)
                                  <= lax.broadcasted_iota(jnp.int32, (t, t), 0), s, -jnp.inf)
                m_prev = m_sc[hd]
                m_new = jnp.maximum(m_prev, jnp.max(s, axis=1, keepdims=True))
                p = jnp.exp2((s - m_new[:, :1]) * ATT_C)
                m_sc[hd] = m_new
                vh = jnp.where((lane_v >= hd * 64) & (lane_v < (hd + 1) * 64), vv, jnp.ones_like(vv))
                acc_sc[hd] = acc_sc[hd] * jnp.exp2((m_prev - m_new) * ATT_C) + _dot(_mx(p), vh)

        @pl.when(ki < qi)
        def _():
            update(False)

        @pl.when(ki == qi)
        def _():
            update(True)
            first = lax.broadcasted_iota(jnp.int32, (t, LANES), 1) < 64
            a0, a1 = acc_sc[0], acc_sc[1]
            l0, l1 = pltpu.roll(a0, 64, 1), pltpu.roll(a1, 64, 1)
            o_ref[...] = jnp.where(first, a0 / l0, a1 / l1)
            lse_ref[0] = jnp.where(first, m_sc[0] * ATT_SCALE + jnp.log(l0), m_sc[1] * ATT_SCALE + jnp.log(l1))

        if nx:
            @pl.when((pl.program_id(0) == 3) & (step == len(steps) - 1))
            def _():
                for cp in copies:
                    cp.wait()

    grid_spec = pltpu.PrefetchScalarGridSpec(
        num_scalar_prefetch=2, grid=(4, len(steps)),
        in_specs=[pl.BlockSpec((t, 256), lambda p, s, qt, kt: (qt[s], p)),
                  pl.BlockSpec((t, 256), lambda p, s, qt, kt: (kt[s], p)),
                  pl.BlockSpec((t, LANES), lambda p, s, qt, kt: (kt[s], p))] + [ANY] * nx,
        out_specs=[pl.BlockSpec((t, LANES), lambda p, s, qt, kt: (qt[s], p)),
                   pl.BlockSpec((1, t, LANES), lambda p, s, qt, kt: (p, qt[s], 0))] + [ANY] * nx,
        scratch_shapes=[pltpu.VMEM((2, t, LANES), F32), pltpu.VMEM((2, t, LANES), F32)]
        + (_exchange_sems(nx) if nx else []))
    res = pl.pallas_call(
        body, name="flash_fwd", grid_spec=grid_spec,
        out_shape=[jax.ShapeDtypeStruct((SEQ, 512), F32), jax.ShapeDtypeStruct((4, SEQ, LANES), F32)]
        + _exchange_shapes([], bcast),
        compiler_params=_params(("arbitrary", "arbitrary")),
    )(qi_tab, ki_tab, q, k, v, *bcast)
    return res[0], res[1], res[2:]


def _flash_bwd(q, k, v, o, do, lse, scatter=()):
    t = ATT_T
    nb = SEQ // t

    steps = [(qi, ki) for ki in range(nb) for qi in range(ki, nb)]
    qi_tab = jnp.asarray([s[0] for s in steps], jnp.int32)
    ki_tab = jnp.asarray([s[1] for s in steps], jnp.int32)
    log2e = math.log2(math.e)

    sc_arrays, sc_ranges = _scatter_args(scatter)
    nx = len(sc_arrays)

    def body(qi_ref, ki_ref, q_ref, k_ref, v_ref, o_ref, do_ref, lse_ref, *rest):
        x_refs, (dq_ref, dk_ref, dv_ref), g_refs = rest[:nx], rest[nx:nx + 3], rest[nx + 3:2 * nx + 3]
        step = pl.program_id(1)
        qi, ki = qi_ref[step], ki_ref[step]
        if nx:
            copies = _peer_copies(x_refs, g_refs, rest[2 * nx + 3:], sc_ranges)

            @pl.when((pl.program_id(0) == 0) & (step == 0))
            def _():
                for cp in copies:
                    cp.start()

        @pl.when(step == 0)
        def _():
            dq_ref[...] = jnp.zeros_like(dq_ref)

        @pl.when(qi == ki)
        def _():
            dk_ref[...] = jnp.zeros_like(dk_ref)
            dv_ref[...] = jnp.zeros_like(dv_ref)

        def update(diagonal):
            dov, ov, vv = do_ref[...], o_ref[...], v_ref[...]
            lse2 = lse_ref[0] * log2e
            lane = lax.broadcasted_iota(jnp.int32, (t, LANES), 1)
            prod = dov * ov
            qrows = pl.ds(pl.multiple_of(qi * t, t), t)
            dv_acc = jnp.zeros((t, LANES), F32)
            dk_new, dq_new = [], []
            for hd in range(2):
                sl = slice(hd * LANES, (hd + 1) * LANES)
                mine = (lane >= hd * 64) & (lane < (hd + 1) * 64)
                qh, kh = q_ref[:, sl], k_ref[:, sl]
                p = jnp.exp2(_dot_nt(qh, kh) * ATT_C - lse2[:, hd * 64:hd * 64 + 1])
                if diagonal:
                    p = jnp.where(lax.broadcasted_iota(jnp.int32, (t, t), 1)
                                  <= lax.broadcasted_iota(jnp.int32, (t, t), 0), p, 0.0)
                do_h = jnp.where(mine, dov, 0.0)
                delta = jnp.sum(jnp.where(mine, prod, 0.0), axis=1, keepdims=True)
                dp = _dot_nt(_mx(do_h), vv)
                ds = _mx(p * (dp - delta) * ATT_SCALE)
                dv_acc = dv_acc + jnp.where(mine, _dot_tn(_mx(p), _mx(dov)), 0.0)
                dk_new.append(_dot_tn(ds, qh))
                dq_new.append(_dot(ds, kh))
            for hd in range(2):
                sl = slice(hd * LANES, (hd + 1) * LANES)
                dk_ref[:, sl] += dk_new[hd]
                dq_ref[qrows, sl] += dq_new[hd]
            dv_ref[...] += dv_acc

        @pl.when(qi > ki)
        def _():
            update(False)

        @pl.when(qi == ki)
        def _():
            update(True)

        if nx:
            @pl.when((pl.program_id(0) == 3) & (step == len(steps) - 1))
            def _():
                for cp in copies:
                    cp.wait()

    qmap = lambda p, s, qt, kt: (qt[s], p)
    kmap = lambda p, s, qt, kt: (kt[s], p)
    grid_spec = pltpu.PrefetchScalarGridSpec(
        num_scalar_prefetch=2, grid=(4, len(steps)),
        in_specs=[pl.BlockSpec((t, 256), qmap), pl.BlockSpec((t, 256), kmap), pl.BlockSpec((t, LANES), kmap),
                  pl.BlockSpec((t, LANES), qmap), pl.BlockSpec((t, LANES), qmap),
                  pl.BlockSpec((1, t, LANES), lambda p, s, qt, kt: (p, qt[s], 0))] + [ANY] * nx,
        out_specs=[pl.BlockSpec((SEQ, 256), lambda p, s, qt, kt: (0, p)), pl.BlockSpec((t, 256), kmap),
                   pl.BlockSpec((t, LANES), kmap)] + [ANY] * nx,
        scratch_shapes=_exchange_sems(nx) if nx else [])
    res = pl.pallas_call(
        body, name="flash_bwd", grid_spec=grid_spec,
        out_shape=[jax.ShapeDtypeStruct((SEQ, 1024), F32), jax.ShapeDtypeStruct((SEQ, 1024), F32),
                   jax.ShapeDtypeStruct((SEQ, 512), F32)] + _exchange_shapes(sc_arrays, []),
        compiler_params=_params(("arbitrary", "arbitrary")),
    )(qi_tab, ki_tab, q, k, v, o, do, lse, *sc_arrays)
    return res[0], res[1], res[2], res[3:]


def _rms_bwd(v, g, dy, eps=1e-6):
    rs = lax.rsqrt(jnp.mean(v * v, axis=-1, keepdims=True) + eps)
    xh = v * rs
    dxh = dy * g
    dv = rs * (dxh - xh * jnp.mean(dxh * xh, axis=-1, keepdims=True))
    return dv, jnp.sum(dy * xh, axis=0, keepdims=True)


def _mla_bwd(dq, dk, dv, proj0, q_norm, kv_norm, wq, wkv, tc, tsa, tsb):
    t = MLA_T

    def body(dq_ref, dk_ref, dv_ref, cq_ref, ck_ref, qn_ref, kn_ref, wq_ref, wkv_ref, c_ref, sa_ref, sb_ref,
             oq_ref, okv_ref, o_ref, dgq_ref, dgk_ref):
        @pl.when(pl.program_id(0) == 0)
        def _():
            dgq_ref[...] = jnp.zeros_like(dgq_ref)
            dgk_ref[...] = jnp.zeros_like(dgk_ref)

        c, sa, sb = c_ref[...], sa_ref[...], sb_ref[...]
        lane = lax.broadcasted_iota(jnp.int32, (t, LANES), 1)
        dkr = jnp.zeros((t, LANES), F32)
        for hd in range(MLA_HEADS):
            sl = slice(hd * LANES, (hd + 1) * LANES)
            oq_ref[:, sl] = _mx(_rope_t(dq_ref[:, sl], c, sa, sb))
            dkh = dk_ref[:, sl]
            okv_ref[:, sl] = _mx(dkh)
            dkr = dkr + dkh
        okv_ref[:, 1024:] = _mx(dv_ref[...])
        dkr = _rope_t(jnp.where((lane >= 64) & (lane < 96), dkr, 0.0), c, sa, sb)
        dqn = _dot_nt(oq_ref[...], wq_ref[...])
        dkn = _dot_nt(okv_ref[...], wkv_ref[...])
        dcq, dgq = _rms_bwd(cq_ref[...], qn_ref[...], dqn)
        dck, dgk = _rms_bwd(ck_ref[:, :LANES], kn_ref[...], dkn)
        o_ref[:, :256] = _mx(dcq)
        o_ref[:, 256:384] = _mx(dck)
        o_ref[:, 384:] = _mx(dkr)
        dgq_ref[0:1, :] += dgq
        dgk_ref[0:1, :] += dgk

    tab = pl.BlockSpec((t, LANES), lambda i: (i, 0))
    wide = pl.BlockSpec((t, 1024), lambda i: (i, 0))
    const = lambda shape: pl.BlockSpec(shape, lambda i: (0, 0))
    return pl.pallas_call(
        body, name="mla_bwd", grid=(SEQ // t,),
        in_specs=[wide, wide, pl.BlockSpec((t, 512), lambda i: (i, 0)),
                  pl.BlockSpec((t, 256), lambda i: (i, 6)), pl.BlockSpec((t, 256), lambda i: (i, 7)),
                  const((1, 256)), const((1, LANES)), const((256, 1024)), const((LANES, 1536)), tab, tab, tab],
        out_specs=[wide, pl.BlockSpec((t, 1536), lambda i: (i, 0)), pl.BlockSpec((t, 512), lambda i: (i, 0)),
                   const((SUBLANES, 256)), const((SUBLANES, LANES))],
        out_shape=[jax.ShapeDtypeStruct((SEQ, 1024), MXU_DTYPE), jax.ShapeDtypeStruct((SEQ, 1536), MXU_DTYPE),
                   jax.ShapeDtypeStruct((SEQ, 512), MXU_DTYPE), jax.ShapeDtypeStruct((SUBLANES, 256), F32),
                   jax.ShapeDtypeStruct((SUBLANES, LANES), F32)],
        compiler_params=_params(("arbitrary",)),
    )(dq, dk, dv, proj0, proj0, q_norm, kv_norm, wq, wkv, tc, tsa, tsb)


LN_T = 512


def _ln(v, g, b, eps=1e-5):
    mu = jnp.mean(v, axis=-1, keepdims=True)
    xc = v - mu
    rs = lax.rsqrt(jnp.mean(xc * xc, axis=-1, keepdims=True) + eps)
    return xc * rs * g + b


def _ln_bwd(v, g, dy, eps=1e-5):
    mu = jnp.mean(v, axis=-1, keepdims=True)
    xc = v - mu
    rs = lax.rsqrt(jnp.mean(xc * xc, axis=-1, keepdims=True) + eps)
    xh = xc * rs
    dxh = dy * g
    dv = rs * (dxh - jnp.mean(dxh, axis=-1, keepdims=True) - xh * jnp.mean(dxh * xh, axis=-1, keepdims=True))
    return dv, jnp.sum(dy * xh, axis=0, keepdims=True), jnp.sum(dy, axis=0, keepdims=True)


def _l0_out(h, o, proj0, x, w_out, g, b):
    t = LN_T

    def body(h_ref, o_ref, ga_ref, gb_ref, x_ref, w_ref, g_ref, b_ref, y_ref, v_ref, x1_ref, x1b_ref):
        y = _mx(jnp.concatenate([h_ref[...] * _silu(ga_ref[...]), o_ref[...] * _silu(gb_ref[...])], axis=1))
        v = DN_ALPHA * x_ref[...] + _dot(y, w_ref[...])
        y_ref[...] = y
        v_ref[...] = v
        x1 = _ln(v, g_ref[...], b_ref[...])
        x1_ref[...] = x1
        x1b_ref[...] = _mx(x1)

    half = pl.BlockSpec((t, 512), lambda i: (i, 0))
    full = pl.BlockSpec((t, D_MODEL), lambda i: (i, 0))
    vec = pl.BlockSpec((1, D_MODEL), lambda i: (0, 0))
    return pl.pallas_call(
        body, name="l0_out", grid=(SEQ // t,),
        in_specs=[half, half, pl.BlockSpec((t, 512), lambda i: (i, 0)), pl.BlockSpec((t, 512), lambda i: (i, 1)), full,
                  pl.BlockSpec((D_MODEL, D_MODEL), lambda i: (0, 0)), vec, vec],
        out_specs=[full, full, full, full],
        out_shape=[jax.ShapeDtypeStruct((SEQ, D_MODEL), MXU_DTYPE), jax.ShapeDtypeStruct((SEQ, D_MODEL), F32),
                   jax.ShapeDtypeStruct((SEQ, D_MODEL), F32), jax.ShapeDtypeStruct((SEQ, D_MODEL), MXU_DTYPE)],
        compiler_params=_params(("parallel",)),
    )(h, o, proj0, proj0, x, w_out, g, b)


def _ln_bwd_call(v, dy, ddt, w1d, g):
    t = LN_T

    def body(v_ref, dy_ref, ddt_ref, w_ref, g_ref, dv_ref, dgb_ref):
        @pl.when(pl.program_id(0) == 0)
        def _():
            dgb_ref[...] = jnp.zeros_like(dgb_ref)

        dy_v = dy_ref[...] + _dot_nt(_mx(ddt_ref[...]), w_ref[...])
        dv, dg, db = _ln_bwd(v_ref[...], g_ref[...], dy_v)
        dv_ref[...] = dv
        dgb_ref[0:1, :] += dg
        dgb_ref[1:2, :] += db

    full = pl.BlockSpec((t, D_MODEL), lambda i: (i, 0))
    return pl.pallas_call(
        body, name="ln_bwd", grid=(SEQ // t,),
        in_specs=[full, full, pl.BlockSpec((t, LANES), lambda i: (i, 0)), pl.BlockSpec((D_MODEL, LANES), lambda i: (0, 0)),
                  pl.BlockSpec((1, D_MODEL), lambda i: (0, 0))],
        out_specs=[full, pl.BlockSpec((SUBLANES, D_MODEL), lambda i: (0, 0))],
        out_shape=[jax.ShapeDtypeStruct((SEQ, D_MODEL), F32), jax.ShapeDtypeStruct((SUBLANES, D_MODEL), F32)],
        compiler_params=_params(("arbitrary",)),
    )(v, dy, ddt, w1d, g)


def _gate_bwd(dv0, w_out, h, o, proj0):
    t = LN_T

    def body(dv_ref, w_ref, h_ref, o_ref, ga_ref, gb_ref, dh_ref, do_ref, dg_ref):
        dy = _dot_nt(_mx(dv_ref[...]), w_ref[...])
        ga, gb, dya, dyb = ga_ref[...], gb_ref[...], dy[:, :512], dy[:, 512:]
        dh_ref[...] = dya * _silu(ga)
        do_ref[...] = dyb * _silu(gb)
        dg_ref[:, :512] = _mx(dya * h_ref[...] * _dsilu(ga))
        dg_ref[:, 512:] = _mx(dyb * o_ref[...] * _dsilu(gb))

    half = pl.BlockSpec((t, 512), lambda i: (i, 0))
    half1 = pl.BlockSpec((t, 512), lambda i: (i, 1))
    full = pl.BlockSpec((t, 1024), lambda i: (i, 0))
    return pl.pallas_call(
        body, name="gate_bwd", grid=(SEQ // t,),
        in_specs=[full, pl.BlockSpec((D_MODEL, D_MODEL), lambda i: (0, 0)), half, half, half, half1],
        out_specs=[half, half, full],
        out_shape=[jax.ShapeDtypeStruct((SEQ, 512), F32), jax.ShapeDtypeStruct((SEQ, 512), F32),
                   jax.ShapeDtypeStruct((SEQ, 1024), MXU_DTYPE)],
        compiler_params=_params(("parallel",)),
    )(dv0, w_out, h, o, proj0, proj0)


CONV_T = 512
CONV_CB = 1024


def _ssd_conv_fwd(xbc, cw8, cb):
    t, cbk = CONV_T, CONV_CB
    tb = t // SUBLANES

    def body(x_ref, halo_ref, cw_ref, cb_ref, pre_ref, act_ref):
        halo = jnp.where(pl.program_id(1) > 0, halo_ref[...], 0.0)
        pre = _conv4(x_ref[...], halo, cw_ref[...], cb_ref[...])
        pre_ref[...] = pre
        act_ref[...] = _silu(pre)

    blk = pl.BlockSpec((t, cbk), lambda j, i: (i, j))
    return pl.pallas_call(
        body, name="ssd_conv_fwd", grid=(SSD_CONV // cbk, SEQ // t),
        in_specs=[blk, pl.BlockSpec((SUBLANES, cbk), lambda j, i: (jnp.maximum(i * tb - 1, 0), j)),
                  pl.BlockSpec((SUBLANES, cbk), lambda j, i: (0, j)), pl.BlockSpec((1, cbk), lambda j, i: (0, j))],
        out_specs=[blk, blk],
        out_shape=[jax.ShapeDtypeStruct((SEQ, SSD_CONV), F32), jax.ShapeDtypeStruct((SEQ, SSD_CONV), F32)],
        compiler_params=_params(("parallel", "parallel")),
    )(xbc, xbc, cw8, cb)


def _ssd_conv_bwd(dact, pre, xbc, cw8):
    t, cbk = CONV_T, CONV_CB
    tb = t // SUBLANES
    nb = SEQ // t

    def body(da_ref, dan_ref, pre_ref, pren_ref, x_ref, cw_ref, dx_ref, dcw_ref):
        i = pl.program_id(1)

        @pl.when(i == 0)
        def _():
            dcw_ref[...] = jnp.zeros_like(dcw_ref)

        dpre = da_ref[...] * _dsilu(pre_ref[...])
        dpre_next = jnp.where(i < nb - 1, dan_ref[...] * _dsilu(pren_ref[...]), 0.0)
        xblk = x_ref[...]
        cw = cw_ref[...]
        dx = dpre * cw[3:4]
        dcw_ref[3:4, :] += jnp.sum(dpre * xblk, axis=0, keepdims=True)
        for k in range(3):
            up = _shift_up(dpre, dpre_next, 3 - k)
            dcw_ref[k:k + 1, :] += jnp.sum(up * xblk, axis=0, keepdims=True)
            dx = dx + up * cw[k:k + 1]
        dcw_ref[4:5, :] += jnp.sum(dpre, axis=0, keepdims=True)
        dx_ref[...] = _mx(dx)

    blk = pl.BlockSpec((t, cbk), lambda j, i: (i, j))
    nxt = pl.BlockSpec((SUBLANES, cbk), lambda j, i: (jnp.minimum((i + 1) * tb, SEQ // SUBLANES - 1), j))
    acc = pl.BlockSpec((SUBLANES, cbk), lambda j, i: (0, j))
    return pl.pallas_call(
        body, name="ssd_conv_bwd", grid=(SSD_CONV // cbk, nb),
        in_specs=[blk, nxt, blk, nxt, blk, acc],
        out_specs=[blk, acc],
        out_shape=[jax.ShapeDtypeStruct((SEQ, SSD_CONV), MXU_DTYPE), jax.ShapeDtypeStruct((SUBLANES, SSD_CONV), F32)],
        compiler_params=_params(("parallel", "arbitrary")),
    )(dact, dact, pre, pre, xbc, cw8)


def _ssd_common(dt_raw, bias, alog, tril, expand_t, xs):
    lane = lax.broadcasted_iota(jnp.int32, dt_raw.shape, 1)
    dt = jnp.where(lane < SSD_HEADS, _softplus(dt_raw + bias), 0.0)
    a_neg = -jnp.exp(alog)
    cs = _dot_hi(tril, dt * a_neg)
    dt_x = _expand_heads(dt, expand_t)
    ecs_x = _expand_heads(jnp.exp(cs), expand_t)
    ds_x = _expand_heads(jnp.exp(cs[SSD_L - 1:SSD_L, :] - cs), expand_t)
    return dt, a_neg, cs, dt_x, None, xs * dt_x, ds_x, ecs_x, ecs_x[SSD_L - 1:SSD_L, :]


def _expand_heads(v, expand_t):
    hi = v.astype(jnp.bfloat16)
    lo = (v - hi.astype(F32)).astype(jnp.bfloat16)
    return _dot_nt(hi, expand_t) + _dot_nt(lo, expand_t)


def _fold_heads(v, expand_t):
    hi = v.astype(jnp.bfloat16)
    lo = (v - hi.astype(F32)).astype(jnp.bfloat16)
    return _dot(hi, expand_t) + _dot(lo, expand_t)


def _ssd_decay(cs, cs_t, hh, causal):
    seg = cs[:, hh:hh + 1] - cs_t[hh:hh + 1, :]
    return jnp.where(causal, jnp.exp(jnp.where(causal, seg, 0.0)), 0.0)


def _ssd_scan_fwd(act, dt_raw, bias, alog, d_x, tril, expand_t):
    nc = SEQ // SSD_L
    gw = SSD_INNER // SSD_GROUPS

    def body(act_ref, dt_ref, bias_ref, alog_ref, dx_ref, tril_ref, et_ref, y_ref, hp_ref, h_sc):
        @pl.when(pl.program_id(0) == 0)
        def _():
            h_sc[...] = jnp.zeros_like(h_sc)

        xs = act_ref[:, :SSD_INNER]
        _, _, cs, _, _, xdt, ds_x, ecs_x, elast = _ssd_common(
            dt_ref[...], bias_ref[...], alog_ref[...], tril_ref[...], et_ref[...], xs)
        cs_t = cs.T
        causal = (lax.broadcasted_iota(jnp.int32, (SSD_L, SSD_L), 0)
                  >= lax.broadcasted_iota(jnp.int32, (SSD_L, SSD_L), 1))
        lane = lax.broadcasted_iota(jnp.int32, (SSD_L, LANES), 1)
        xdt_b = _mx(xdt)
        xds_b = _mx(xdt * ds_x)
        hp_ref[0] = h_sc[...]
        for g in range(SSD_GROUPS):
            gs = slice(g * gw, (g + 1) * gw)
            bg = _mx(act_ref[:, SSD_INNER + g * SSD_N:SSD_INNER + (g + 1) * SSD_N])
            cg = _mx(act_ref[:, SSD_INNER + 512 + g * SSD_N:SSD_INNER + 512 + (g + 1) * SSD_N])
            cb = _dot_nt(cg, bg)
            hprev = h_sc[:, gs]
            yoff = _dot(cg, _mx(hprev)) * ecs_x[:, gs]
            h_sc[:, gs] = hprev * elast[:, gs] + _dot_tn(bg, xds_b[:, gs])
            for pr in range(4):
                ps = slice(g * gw + pr * LANES, g * gw + (pr + 1) * LANES)
                xp = xdt_b[:, ps]
                ydiag = jnp.zeros((SSD_L, LANES), F32)
                for j in range(2):
                    dm = _ssd_decay(cs, cs_t, g * 8 + pr * 2 + j, causal)
                    mine = (lane >= j * 64) & (lane < (j + 1) * 64)
                    ydiag = ydiag + _dot(_mx(cb * dm), jnp.where(mine, xp, jnp.zeros_like(xp)))
                y_ref[:, ps] = ydiag + yoff[:, pr * LANES:(pr + 1) * LANES] + dx_ref[:, ps] * xs[:, ps]

    const = lambda shape: pl.BlockSpec(shape, lambda c: (0, 0))
    return pl.pallas_call(
        body, name="ssd_scan_fwd", grid=(nc,),
        in_specs=[pl.BlockSpec((SSD_L, SSD_CONV), lambda c: (c, 0)), pl.BlockSpec((SSD_L, LANES), lambda c: (c, 0)),
                  const((1, LANES)), const((1, LANES)), const((1, SSD_INNER)), const((SSD_L, SSD_L)),
                  const((SSD_INNER, LANES))],
        out_specs=[pl.BlockSpec((SSD_L, SSD_INNER), lambda c: (c, 0)),
                   pl.BlockSpec((1, SSD_N, SSD_INNER), lambda c: (c, 0, 0))],
        out_shape=[jax.ShapeDtypeStruct((SEQ, SSD_INNER), F32), jax.ShapeDtypeStruct((nc, SSD_N, SSD_INNER), F32)],
        scratch_shapes=[pltpu.VMEM((SSD_N, SSD_INNER), F32)],
        compiler_params=_params(("arbitrary",)),
    )(act, dt_raw, bias, alog, d_x, tril, expand_t)


def _ssd_scan_bwd(dy, act, dt_raw, hprev_all, bias, alog, d_x, tril, expand_t):
    nc = SEQ // SSD_L
    gw = SSD_INNER // SSD_GROUPS

    def body(dy_ref, act_ref, dt_ref, hp_ref, bias_ref, alog_ref, dx_ref, tril_ref, et_ref,
             dact_ref, ddt_ref, dvec_ref, dh_sc, dd_sc):
        i = pl.program_id(0)

        @pl.when(i == 0)
        def _():
            dh_sc[...] = jnp.zeros_like(dh_sc)
            dd_sc[...] = jnp.zeros_like(dd_sc)
            dvec_ref[...] = jnp.zeros_like(dvec_ref)

        xs = act_ref[:, :SSD_INNER]
        dt_raw_v, bias_v = dt_ref[...], bias_ref[...]
        dt, a_neg, cs, dt_x, _, xdt, ds_x, ecs_x, elast = _ssd_common(
            dt_raw_v, bias_v, alog_ref[...], tril_ref[...], et_ref[...], xs)
        cs_t = cs.T
        rowi = lax.broadcasted_iota(jnp.int32, (SSD_L, SSD_L), 0)
        coli = lax.broadcasted_iota(jnp.int32, (SSD_L, SSD_L), 1)
        causal = rowi >= coli
        lane = lax.broadcasted_iota(jnp.int32, (SSD_L, LANES), 1)
        row_g = lax.broadcasted_iota(jnp.int32, (SSD_L, gw), 0)
        dyv = dy_ref[...]
        dd_sc[0:1, :] += jnp.sum(dyv * xs, axis=0, keepdims=True)
        xdt_b = _mx(xdt)
        xds = xdt * ds_x
        xds_b = _mx(xds)
        dy_b = _mx(dyv)
        dye_b = _mx(dyv * ecs_x)
        dcs = jnp.zeros((SSD_L, LANES), F32)
        dcs_t = jnp.zeros((LANES, SSD_L), F32)
        dcs_parts = []
        dxdt_parts = []
        for g in range(SSD_GROUPS):
            gs = slice(g * gw, (g + 1) * gw)
            bcol = slice(SSD_INNER + g * SSD_N, SSD_INNER + (g + 1) * SSD_N)
            ccol = slice(SSD_INNER + 512 + g * SSD_N, SSD_INNER + 512 + (g + 1) * SSD_N)
            bg, cg = _mx(act_ref[:, bcol]), _mx(act_ref[:, ccol])
            cb = _dot_nt(cg, bg)
            hp = hp_ref[0, :, gs]
            hp_b = _mx(hp)
            dh = dh_sc[:, gs]
            dh_b = _mx(dh)
            yoff = _dot(cg, hp_b) * ecs_x[:, gs]
            bdh = _dot(bg, dh_b)
            tt = xds[:, gs] * bdh
            last_row = (jnp.sum(tt, axis=0, keepdims=True)
                        + jnp.sum(dh * hp, axis=0, keepdims=True) * elast[:, gs])
            dcs_parts.append(dyv[:, gs] * yoff - tt + jnp.where(row_g == SSD_L - 1, last_row, 0.0))
            dc_g = _dot_nt(dye_b[:, gs], hp_b)
            db_g = _dot_nt(xds_b[:, gs], dh_b)
            dh_sc[:, gs] = _dot_tn(cg, dye_b[:, gs]) + dh * elast[:, gs]
            wsum = jnp.zeros((SSD_L, SSD_L), F32)
            dxdt_g = []
            for pr in range(4):
                ps = slice(g * gw + pr * LANES, g * gw + (pr + 1) * LANES)
                xp, dyp = xdt_b[:, ps], dy_b[:, ps]
                dxp = jnp.zeros((SSD_L, LANES), F32)
                for j in range(2):
                    hh = g * 8 + pr * 2 + j
                    dm = _ssd_decay(cs, cs_t, hh, causal)
                    mine = (lane >= j * 64) & (lane < (j + 1) * 64)
                    dy_h = jnp.where(mine, dyp, jnp.zeros_like(dyp))
                    wd = _dot_nt(dy_h, xp) * dm
                    wsum = wsum + wd
                    gmat = wd * cb
                    dcs = dcs + jnp.where(lane == hh, jnp.sum(gmat, axis=1, keepdims=True), 0.0)
                    dcs_t = dcs_t - jnp.where(rowi == hh, jnp.sum(gmat, axis=0, keepdims=True), 0.0)
                    dxp = dxp + _dot_tn(_mx(cb * dm), dy_h)
                dxdt_g.append(dxp)
            dxdt_parts.append(jnp.concatenate(dxdt_g, axis=1) + bdh * ds_x[:, gs])
            ws_b = _mx(wsum)
            dact_ref[:, ccol] = dc_g + _dot(ws_b, bg)
            dact_ref[:, bcol] = db_g + _dot_tn(ws_b, cg)
        dxdt = jnp.concatenate(dxdt_parts, axis=1)
        dcs_x = jnp.concatenate(dcs_parts, axis=1)
        et = et_ref[...]
        dcs_tot = dcs + dcs_t.T + _fold_heads(dcs_x, et)
        da_dt = _dot_hi((coli >= rowi).astype(F32), dcs_tot)
        ddt = da_dt * a_neg + _fold_heads(dxdt * xs, et)
        ddt_raw = ddt * _sigmoid(dt_raw_v + bias_v)
        ddt_ref[...] = ddt_raw
        dvec_ref[0:1, :] += jnp.sum(ddt_raw, axis=0, keepdims=True)
        dvec_ref[1:2, :] += jnp.sum(da_dt * dt, axis=0, keepdims=True) * a_neg
        dact_ref[:, :SSD_INNER] = dyv * dx_ref[...] + dxdt * dt_x

        @pl.when(i == nc - 1)
        def _():
            dvec_ref[2:3, :] = _fold_heads(dd_sc[...], et)[0:1, :]

    const = lambda shape: pl.BlockSpec(shape, lambda c: (0, 0))
    rev = lambda c: (nc - 1 - c, 0)
    return pl.pallas_call(
        body, name="ssd_scan_bwd", grid=(nc,),
        in_specs=[pl.BlockSpec((SSD_L, SSD_INNER), rev), pl.BlockSpec((SSD_L, SSD_CONV), rev),
                  pl.BlockSpec((SSD_L, LANES), rev),
                  pl.BlockSpec((1, SSD_N, SSD_INNER), lambda c: (nc - 1 - c, 0, 0)),
                  const((1, LANES)), const((1, LANES)), const((1, SSD_INNER)), const((SSD_L, SSD_L)),
                  const((SSD_INNER, LANES))],
        out_specs=[pl.BlockSpec((SSD_L, SSD_CONV), rev), pl.BlockSpec((SSD_L, LANES), rev), const((SUBLANES, LANES))],
        out_shape=[jax.ShapeDtypeStruct((SEQ, SSD_CONV), F32), jax.ShapeDtypeStruct((SEQ, LANES), F32),
                   jax.ShapeDtypeStruct((SUBLANES, LANES), F32)],
        scratch_shapes=[pltpu.VMEM((SSD_N, SSD_INNER), F32), pltpu.VMEM((SUBLANES, SSD_INNER), F32)],
        compiler_params=_params(("arbitrary",)),
    )(dy, act, dt_raw, hprev_all, bias, alog, d_x, tril, expand_t)


L1_T = 256


def _gated_norm(y, z, nw):
    y2 = y * _silu(z)
    gw = SSD_INNER // SSD_GROUPS
    outs, xhs, rss = [], [], []
    for g in range(SSD_GROUPS):
        gs = slice(g * gw, (g + 1) * gw)
        v = y2[:, gs]
        rs = lax.rsqrt(jnp.mean(v * v, axis=-1, keepdims=True) + 1e-6)
        xhs.append(v * rs)
        rss.append(rs)
        outs.append(v * rs * nw[:, gs])
    return outs, xhs, rss


def _l1_out(y, z, nw, w_out, x1, g, b, target):
    t = L1_T

    def body(y_ref, z_ref, nw_ref, w_ref, x1_ref, g_ref, b_ref, tg_ref, yn_ref, dv_ref, dgb_ref, loss_ref):
        @pl.when(pl.program_id(0) == 0)
        def _():
            dgb_ref[...] = jnp.zeros_like(dgb_ref)
            loss_ref[...] = jnp.zeros_like(loss_ref)

        outs, _, _ = _gated_norm(y_ref[...], z_ref[...], nw_ref[...])
        yn = _mx(jnp.concatenate(outs, axis=1))
        yn_ref[...] = yn
        v = DN_ALPHA * x1_ref[...] + _dot(yn, w_ref[...])
        gv = g_ref[...]
        err = _ln(v, gv, b_ref[...]) - tg_ref[...]
        rowsum = jnp.sum(err * err, axis=1, keepdims=True)
        loss_ref[...] += 0.5 * jnp.sum(rowsum, axis=0, keepdims=True) / D_MODEL
        dv, dg, db = _ln_bwd(v, gv, err / D_MODEL)
        dv_ref[...] = dv
        dgb_ref[0:1, :] += dg
        dgb_ref[1:2, :] += db

    wide = pl.BlockSpec((t, SSD_INNER), lambda i: (i, 0))
    full = pl.BlockSpec((t, D_MODEL), lambda i: (i, 0))
    vec = pl.BlockSpec((1, D_MODEL), lambda i: (0, 0))
    return pl.pallas_call(
        body, name="l1_out", grid=(SEQ // t,),
        in_specs=[wide, wide, pl.BlockSpec((1, SSD_INNER), lambda i: (0, 0)),
                  pl.BlockSpec((SSD_INNER, D_MODEL), lambda i: (0, 0)), full, vec, vec, full],
        out_specs=[wide, full, pl.BlockSpec((SUBLANES, D_MODEL), lambda i: (0, 0)),
                   pl.BlockSpec((SUBLANES, LANES), lambda i: (0, 0))],
        out_shape=[jax.ShapeDtypeStruct((SEQ, SSD_INNER), MXU_DTYPE), jax.ShapeDtypeStruct((SEQ, D_MODEL), F32),
                   jax.ShapeDtypeStruct((SUBLANES, D_MODEL), F32), jax.ShapeDtypeStruct((SUBLANES, LANES), F32)],
        compiler_params=_params(("arbitrary",)),
    )(y, z, nw, w_out, x1, g, b, target)


def _l1_gate_bwd(dv1, w_out, y, z, nw):
    t = L1_T
    gw = SSD_INNER // SSD_GROUPS

    def body(dv_ref, w_ref, y_ref, z_ref, nw_ref, dy_ref, dz_ref, dnw_ref):
        @pl.when(pl.program_id(0) == 0)
        def _():
            dnw_ref[...] = jnp.zeros_like(dnw_ref)

        dyn = _dot_nt(_mx(dv_ref[...]), w_ref[...])
        yv, zv, nwv = y_ref[...], z_ref[...], nw_ref[...]
        _, xhs, rss = _gated_norm(yv, zv, nwv)
        sz, dsz = _silu(zv), _dsilu(zv)
        for g in range(SSD_GROUPS):
            gs = slice(g * gw, (g + 1) * gw)
            d_out = dyn[:, gs]
            xh = xhs[g]
            dnw_ref[0:1, gs] += jnp.sum(d_out * xh, axis=0, keepdims=True)
            dxh = d_out * nwv[:, gs]
            dy2 = rss[g] * (dxh - xh * jnp.mean(dxh * xh, axis=-1, keepdims=True))
            dy_ref[:, gs] = dy2 * sz[:, gs]
            dz_ref[:, gs] = _mx(dy2 * yv[:, gs] * dsz[:, gs])

    wide = pl.BlockSpec((t, SSD_INNER), lambda i: (i, 0))
    return pl.pallas_call(
        body, name="l1_gate_bwd", grid=(SEQ // t,),
        in_specs=[pl.BlockSpec((t, D_MODEL), lambda i: (i, 0)), pl.BlockSpec((SSD_INNER, D_MODEL), lambda i: (0, 0)),
                  wide, wide, pl.BlockSpec((1, SSD_INNER), lambda i: (0, 0))],
        out_specs=[wide, wide, pl.BlockSpec((SUBLANES, SSD_INNER), lambda i: (0, 0))],
        out_shape=[jax.ShapeDtypeStruct((SEQ, SSD_INNER), F32), jax.ShapeDtypeStruct((SEQ, SSD_INNER), MXU_DTYPE),
                   jax.ShapeDtypeStruct((SUBLANES, SSD_INNER), F32)],
        compiler_params=_params(("arbitrary",)),
    )(dv1, w_out, y, z, nw)


MESH = pl.DeviceIdType.MESH
ANY = pl.BlockSpec(memory_space=pl.ANY)


def _flip(v, bit):
    return 1 - v if bit else v


def _all_gather(blocks, name):
    n = len(blocks)

    def body(*refs):
        x_refs, out_refs = refs[:n], refs[n:2 * n]
        send_sems, recv_sems, local_sems = refs[2 * n:]
        mx, my, mc = lax.axis_index("x"), lax.axis_index("y"), lax.axis_index("c")
        me, sibling = (mx, my, mc), (mx, my, 1 - mc)
        chips = [(1 - mx, my), (mx, 1 - my), (1 - mx, 1 - my)]

        def copy(a, k, block, to, own=False):
            px, py, pc = block
            slot = out_refs[a].at[4 * px + 2 * py + pc]
            return pltpu.make_async_remote_copy(
                src_ref=x_refs[a] if own else slot, dst_ref=slot,
                send_sem=send_sems.at[7 * a + k], recv_sem=recv_sems.at[7 * a + k], device_id=to, device_id_type=MESH)

        mine = [pltpu.make_async_copy(x_refs[a], out_refs[a].at[4 * mx + 2 * my + mc], local_sems.at[a])
                for a in range(n)]
        first = []
        for a in range(n):
            mine[a].start()
            first.append(copy(a, 0, me, sibling, own=True))
            first += [copy(a, 1 + j, me, (*chip, mc), own=True) for j, chip in enumerate(chips)]
        for cp in first:
            cp.start()
        passed = []
        for j, chip in enumerate(chips):
            for a in range(n):
                copy(a, 1 + j, (*chip, mc), me).wait_recv()
                fwd = copy(a, 4 + j, (*chip, mc), sibling)
                fwd.start()
                passed.append(fwd)
        for a in range(n):
            copy(a, 0, sibling, me).wait_recv()
            for j, chip in enumerate(chips):
                copy(a, 4 + j, (*chip, 1 - mc), me).wait_recv()
        for cp in first + passed:
            cp.wait_send()
        for cp in mine:
            cp.wait()

    return pl.pallas_call(
        body, name=name, in_specs=[ANY] * n, out_specs=[ANY] * n,
        out_shape=[jax.ShapeDtypeStruct((N_DEV,) + b.shape, b.dtype) for b in blocks],
        scratch_shapes=[pltpu.SemaphoreType.DMA((7 * n,)), pltpu.SemaphoreType.DMA((7 * n,)),
                        pltpu.SemaphoreType.DMA((n,))],
    )(*blocks)


def _exchange(scatter, bcast, name):
    arrays, ranges = _scatter_args(scatter)
    n = len(arrays) + len(bcast)

    def body(*refs):
        copies = _peer_copies(refs[:n], refs[n:2 * n], refs[2 * n:], ranges)
        for cp in copies:
            cp.start()
        for cp in copies:
            cp.wait()

    return pl.pallas_call(
        body, name=name, in_specs=[ANY] * n, out_specs=[ANY] * n,
        out_shape=_exchange_shapes(arrays, bcast), scratch_shapes=_exchange_sems(n),
    )(*arrays, *bcast)


def _scatter_args(scatter):
    arrays = [s[0] if isinstance(s, tuple) else s for s in scatter]
    ranges = [s[1] if isinstance(s, tuple) else (0, N_DEV) for s in scatter]
    return arrays, ranges


def _exchange_shapes(scatter, bcast):
    return ([jax.ShapeDtypeStruct((N_DEV,) + a.shape[1:], a.dtype) for a in scatter]
            + [jax.ShapeDtypeStruct((N_DEV,) + a.shape, a.dtype) for a in bcast])


def _exchange_sems(n):
    return [pltpu.SemaphoreType.DMA((7 * n,)), pltpu.SemaphoreType.DMA((7 * n,)), pltpu.SemaphoreType.DMA((n,))]


class _GuardedCopy:
    def __init__(self, copy, send=None, recv=None, local=False):
        self.copy, self.send, self.recv, self.local = copy, send, recv, local

    @staticmethod
    def _run(pred, fn):
        if pred is None:
            fn()
        else:
            pl.when(pred)(fn)

    def start(self):
        self._run(self.send, self.copy.start)

    def wait(self):
        if self.local:
            self._run(self.send, self.copy.wait)
        else:
            self._run(self.send, self.copy.wait_send)
            self._run(self.recv, self.copy.wait_recv)


def _peer_copies(in_refs, out_refs, sems, ranges):
    send_sems, recv_sems, local_sems = sems
    n, ns = len(in_refs), len(ranges)
    mx, my, mc = lax.axis_index("x"), lax.axis_index("y"), lax.axis_index("c")
    me = 4 * mx + 2 * my + mc

    def src(a, slot):
        return in_refs[a].at[slot - ranges[a][0]] if a < ns else in_refs[a]

    def member(a, dev):
        if a >= ns or ranges[a] == (0, N_DEV):
            return None
        return (dev >= ranges[a][0]) & (dev < ranges[a][1])

    copies = [_GuardedCopy(pltpu.make_async_copy(src(a, me), out_refs[a].at[me], local_sems.at[a]),
                           send=member(a, me), local=True) for a in range(n)]
    for k in range(1, N_DEV):
        px, py, pc = _flip(mx, (k >> 2) & 1), _flip(my, (k >> 1) & 1), _flip(mc, k & 1)
        peer = 4 * px + 2 * py + pc
        for a in range(n):
            copies.append(_GuardedCopy(pltpu.make_async_remote_copy(
                src_ref=src(a, peer), dst_ref=out_refs[a].at[me],
                send_sem=send_sems.at[7 * a + k - 1], recv_sem=recv_sems.at[7 * a + k - 1],
                device_id=(px, py, pc), device_id_type=MESH), send=member(a, peer), recv=member(a, me)))
    return copies


def _segments(col_map, width):
    segs = []
    for lo, hi, arr, alo in col_map:
        for s in range(N_DEV):
            a, b = max(lo, s * width), min(hi, (s + 1) * width)
            if a < b:
                segs.append((s, a - s * width, b - a, arr, alo + a - lo))
    return segs


COPY_ROWS = 256


def _unshard(g8, col_map, widths, name):
    _, r, w = g8.shape
    rb = min(r, COPY_ROWS)
    segs = _segments(col_map, w)

    def body(g_ref, *o_refs):
        for o_ref in o_refs:
            o_ref[...] = jnp.zeros_like(o_ref)
        for s, llo, n, arr, alo in segs:
            o_refs[arr][:, alo:alo + n] = g_ref[s, :, llo:llo + n]

    return pl.pallas_call(
        body, name=name, grid=(r // rb,),
        in_specs=[pl.BlockSpec((N_DEV, rb, w), lambda i: (0, i, 0))],
        out_specs=[pl.BlockSpec((rb, n), lambda i: (i, 0)) for n in widths],
        out_shape=[jax.ShapeDtypeStruct((r, n), g8.dtype) for n in widths],
        compiler_params=_params(("parallel",)),
    )(g8)


def _reshard(srcs, col_map, w, dtype, name, shards=(0, N_DEV)):
    r = srcs[0].shape[0]
    rb = min(r, COPY_ROWS)
    lo, hi = shards
    segs = [sg for sg in _segments(col_map, w) if lo <= sg[0] < hi]

    def body(*refs):
        o_ref = refs[-1]
        for s, llo, n, arr, alo in segs:
            o_ref[s - lo, :, llo:llo + n] = refs[arr][:, alo:alo + n].astype(dtype)

    return pl.pallas_call(
        body, name=name, grid=(r // rb,),
        in_specs=[pl.BlockSpec((rb, a.shape[1]), lambda i: (i, 0)) for a in srcs],
        out_specs=pl.BlockSpec((hi - lo, rb, w), lambda i: (0, i, 0)),
        out_shape=jax.ShapeDtypeStruct((hi - lo, r, w), dtype),
        compiler_params=_params(("parallel",)),
    )(*srcs)


def _adamw(parts, w, m, v, name):
    r, c = w.shape
    tr = COPY_ROWS if r % COPY_ROWS == 0 else r

    def body(p_ref, w_ref, m_ref, v_ref, g_ref, d_ref, mo_ref, vo_ref):
        g = p_ref[0].astype(F32)
        for s in range(1, N_DEV):
            g = g + p_ref[s].astype(F32)
        g_ref[...] = g
        d_ref[...], mo_ref[...], vo_ref[...] = _adamw_math(g, w_ref[...], m_ref[...], v_ref[...])

    blk = pl.BlockSpec((tr, c), lambda i: (i, 0))
    out = jax.ShapeDtypeStruct((r, c), F32)
    return pl.pallas_call(
        body, name=name, grid=(r // tr,),
        in_specs=[pl.BlockSpec((N_DEV, tr, c), lambda i: (0, i, 0)), blk, blk, blk],
        out_specs=[blk, blk, blk, blk], out_shape=[out, out, out, out],
        compiler_params=_params(("parallel",)),
    )(parts, w, m, v)


def _adamw_math(g, w, m, v):
    mn = ADAM_B1 * m + (1.0 - ADAM_B1) * g
    vn = ADAM_B2 * v + (1.0 - ADAM_B2) * (g * g)
    m_hat = mn / (1.0 - ADAM_B1 ** ADAM_STEP)
    v_hat = vn / (1.0 - ADAM_B2 ** ADAM_STEP)
    return -ADAM_LR * (m_hat / (jnp.sqrt(v_hat) + ADAM_EPS) + ADAM_WD * w), mn, vn


SMALL = (("ab_conv_w", 0, 4, 64), ("ssd_conv_w", 4, 4, 384), ("ssd_conv_b", 8, 1, 384), ("ssd_norm", 9, 1, 256),
         ("ssd_ln_g", 10, 1, 128), ("ssd_ln_b", 11, 1, 128))
VECS = (("ab_conv_b", 512), ("ab_gate_a_b", 512), ("ab_gate_x_b", 512), ("ab_lambda", 512), ("mla_q_norm", 256),
        ("mla_kv_norm", 128), ("ab_ln_g", 1024), ("ab_ln_b", 1024), ("ssd_dt_bias", 32), ("ssd_a_log", 32),
        ("ssd_d", 32))
GATES = ("ab_gate_a_w", "ab_gate_x_w")
SMALL_NAMES = tuple(n for n, *_ in SMALL) + tuple(n for n, _ in VECS) + GATES
VMEM_WHOLE = pl.BlockSpec(memory_space=pltpu.VMEM)


def _view2d(name, a):
    if name in GATES:
        return a.reshape(RNN_W, 64)
    return a[0] if a.ndim == 3 else a


def _unshard_small(g):
    widths = (512, 3072, 3072, 2048, 1024, 1024)

    def body(*refs):
        ins, outs = refs[:6], refs[6:]
        outs[0][...] = jnp.zeros_like(outs[0])
        outs[1][...] = jnp.zeros_like(outs[1])
        for (_, _, nr, c), i_ref, o_ref in zip(SMALL, ins, outs):
            for j in range(N_DEV):
                o_ref[0:nr, j * c:(j + 1) * c] = i_ref[j]

    return pl.pallas_call(
        body, name="unshard_small", in_specs=[VMEM_WHOLE] * 6, out_specs=[VMEM_WHOLE] * 6,
        out_shape=[jax.ShapeDtypeStruct((SUBLANES if nr == 4 else 1, w), F32) for (_, _, nr, _), w in zip(SMALL, widths)],
    )(*g)


def _prep_repl(ga, gx, dt_bias, a_log, d):
    def body(ga_ref, gx_ref, b_ref, al_ref, d_ref, wa_ref, wx_ref, b128_ref, al128_ref, dx_ref):
        wa_ref[...] = jnp.zeros_like(wa_ref)
        wx_ref[...] = jnp.zeros_like(wx_ref)
        for hd in range(8):
            hs = slice(hd * 64, (hd + 1) * 64)
            wa_ref[hs, hs] = _mx(ga_ref[hs, :])
            wx_ref[hs, hs] = _mx(gx_ref[hs, :])
        b128_ref[...] = jnp.zeros_like(b128_ref)
        al128_ref[...] = jnp.zeros_like(al128_ref)
        b128_ref[:, 0:SSD_HEADS] = b_ref[...]
        al128_ref[:, 0:SSD_HEADS] = al_ref[...]
        dv = d_ref[...]
        for hd in range(SSD_HEADS):
            dx_ref[:, hd * SSD_P:(hd + 1) * SSD_P] = jnp.broadcast_to(dv[:, hd:hd + 1], (1, SSD_P))

    return pl.pallas_call(
        body, name="prep_repl", in_specs=[VMEM_WHOLE] * 5, out_specs=[VMEM_WHOLE] * 5,
        out_shape=[jax.ShapeDtypeStruct((RNN_W, RNN_W), MXU_DTYPE), jax.ShapeDtypeStruct((RNN_W, RNN_W), MXU_DTYPE),
                   jax.ShapeDtypeStruct((1, LANES), F32), jax.ShapeDtypeStruct((1, LANES), F32),
                   jax.ShapeDtypeStruct((1, SSD_INNER), F32)],
    )(ga, gx, dt_bias, a_log, d)


def _pack_small(dvec0, g_wa, g_wx, dqnw, dknw, dgb0, dvec1, dcw1, dnw, dgb1):
    def body(dvec0_ref, gwa_ref, gwx_ref, dqn_ref, dkn_ref, dgb0_ref, dvec1_ref, dcw1_ref, dnw_ref, dgb1_ref,
             sm_ref, vec_ref, ga_ref, gx_ref):
        sm_ref[...] = jnp.zeros_like(sm_ref)
        vec_ref[...] = jnp.zeros_like(vec_ref)
        sharded = ((dvec0_ref, 4), (dcw1_ref, 0), (dcw1_ref, 4), (dnw_ref, 0), (dgb1_ref, 0), (dgb1_ref, 1))
        for (_, r0, nr, c), (src, sr) in zip(SMALL, sharded):
            for j in range(N_DEV):
                sm_ref[j, r0:r0 + nr, 0:c] = src[sr:sr + nr, j * c:(j + 1) * c]
        vectors = ((dvec0_ref, 3), (dvec0_ref, 0), (dvec0_ref, 1), (dvec0_ref, 2), (dqn_ref, 0), (dkn_ref, 0),
                   (dgb0_ref, 0), (dgb0_ref, 1), (dvec1_ref, 0), (dvec1_ref, 1), (dvec1_ref, 2))
        for row, ((_, c), (src, sr)) in enumerate(zip(VECS, vectors)):
            vec_ref[row:row + 1, 0:c] = src[sr:sr + 1, 0:c]
        for hd in range(8):
            hs = slice(hd * 64, (hd + 1) * 64)
            ga_ref[hs, :] = gwa_ref[hs, hs]
            gx_ref[hs, :] = gwx_ref[hs, hs]

    return pl.pallas_call(
        body, name="pack_small", in_specs=[VMEM_WHOLE] * 10, out_specs=[VMEM_WHOLE] * 4,
        out_shape=[jax.ShapeDtypeStruct((N_DEV, 16, 384), F32), jax.ShapeDtypeStruct((16, 1024), F32),
                   jax.ShapeDtypeStruct((RNN_W, 64), F32), jax.ShapeDtypeStruct((RNN_W, 64), F32)],
    )(dvec0, g_wa, g_wx, dqnw, dknw, dgb0, dvec1, dcw1, dnw, dgb1)


def _adamw_small(recv_sm, recv_vec, recv_ga, recv_gx, wmv):
    plan = ([(0, r0, nr, c) for _, r0, nr, c in SMALL] + [(1, row, 1, c) for row, (_, c) in enumerate(VECS)]
            + [(2, 0, RNN_W, 64), (3, 0, RNN_W, 64)])
    n = len(plan)

    def body(*refs):
        recv, ins, outs = refs[:4], refs[4:4 + 3 * n], refs[4 + 3 * n:]
        for i, (src, r0, nr, c) in enumerate(plan):
            g = recv[src][0, r0:r0 + nr, 0:c]
            for s in range(1, N_DEV):
                g = g + recv[src][s, r0:r0 + nr, 0:c]
            w_ref, m_ref, v_ref = ins[3 * i:3 * i + 3]
            outs[4 * i][...] = g
            outs[4 * i + 1][...], outs[4 * i + 2][...], outs[4 * i + 3][...] = _adamw_math(
                g, w_ref[...], m_ref[...], v_ref[...])

    flat = [a for t in wmv for a in t]
    return pl.pallas_call(
        body, name="adamw_small", in_specs=[VMEM_WHOLE] * (4 + 3 * n), out_specs=[VMEM_WHOLE] * (4 * n),
        out_shape=[jax.ShapeDtypeStruct(t[0].shape, F32) for t in wmv for _ in range(4)],
    )(recv_sm, recv_vec, recv_ga, recv_gx, *flat)


BIG_L0 = ("ab_w_in", "ab_w_out", "mla_w_uq", "mla_w_ukv")
BIG_L1 = ("ssd_w_in", "ssd_w_out")

MAP_W0 = ((0, 512, 0, 1024), (512, 1536, 0, 0), (1536, 1920, 0, 1536), (1920, 1952, 0, 1984))
MAP_W1 = ((0, 2048, 0, 0), (2048, 5120, 1, 0), (5120, 5152, 2, 0))
MAP_WQ = tuple((96 * hd, 96 * hd + 96, 0, 128 * hd) for hd in range(8))
MAP_WKV = (tuple((128 * hd, 128 * hd + 64, 0, 128 * hd) for hd in range(8))
           + tuple((128 * hd + 64, 128 * hd + 128, 0, 1024 + 64 * hd) for hd in range(8)))
MAP_G0 = ((0, 512, 0, 0), (512, 1536, 1, 0), (1536, 1920, 2, 0), (1920, 1952, 2, 448))
W0_EARLY, W0_LATE = (0, 6), (6, 8)


def kernel(x, positions, ab_w_in, ab_conv_w, ab_conv_b, ab_gate_a_w, ab_gate_a_b, ab_gate_x_w, ab_gate_x_b, ab_lambda, mla_q_norm, mla_kv_norm, mla_w_uq, mla_w_ukv, ab_w_out, ab_ln_g, ab_ln_b, ssd_w_in, ssd_conv_w, ssd_conv_b, ssd_dt_bias, ssd_a_log, ssd_d, ssd_norm, ssd_w_out, ssd_ln_g, ssd_ln_b, loss_target, m_ab_w_in, m_ab_conv_w, m_ab_conv_b, m_ab_gate_a_w, m_ab_gate_a_b, m_ab_gate_x_w, m_ab_gate_x_b, m_ab_lambda, m_mla_q_norm, m_mla_kv_norm, m_mla_w_uq, m_mla_w_ukv, m_ab_w_out, m_ab_ln_g, m_ab_ln_b, m_ssd_w_in, m_ssd_conv_w, m_ssd_conv_b, m_ssd_dt_bias, m_ssd_a_log, m_ssd_d, m_ssd_norm, m_ssd_w_out, m_ssd_ln_g, m_ssd_ln_b, v_ab_w_in, v_ab_conv_w, v_ab_conv_b, v_ab_gate_a_w, v_ab_gate_a_b, v_ab_gate_x_w, v_ab_gate_x_b, v_ab_lambda, v_mla_q_norm, v_mla_kv_norm, v_mla_w_uq, v_mla_w_ukv, v_ab_w_out, v_ab_ln_g, v_ab_ln_b, v_ssd_w_in, v_ssd_conv_w, v_ssd_conv_b, v_ssd_dt_bias, v_ssd_a_log, v_ssd_d, v_ssd_norm, v_ssd_w_out, v_ssd_ln_g, v_ssd_ln_b):
    args = dict(locals())
    bf = MXU_DTYPE
    big = {n: [args[pre + n][0] for pre in ("", "m_", "v_")] for n in BIG_L0 + BIG_L1}
    sml = {n: [_view2d(n, args[pre + n]) for pre in ("", "m_", "v_")] for n in SMALL_NAMES}

    gathered = _all_gather([big[n][0].astype(bf) for n in BIG_L0] + [sml[n][0] for n, *_ in SMALL], "gather_params")
    g8 = dict(zip(BIG_L0, gathered))
    p = {"wo0": g8["ab_w_out"].reshape(D_MODEL, D_MODEL)}
    p["w0p"], = _unshard(g8["ab_w_in"], MAP_W0, (2048,), "unshard_w0")
    p["wq"], = _unshard(g8["mla_w_uq"], MAP_WQ, (1024,), "unshard_wq")
    p["wkv"], = _unshard(g8["mla_w_ukv"], MAP_WKV, (1536,), "unshard_wkv")
    p["cw0"], p["cw1"], p["cb1"], p["nw"], p["g1"], p["b1"] = _unshard_small(gathered[len(BIG_L0):])
    p["wa"], p["wx"], p["dt_bias"], p["a_log"], p["d_x"] = _prep_repl(
        sml["ab_gate_a_w"][0], sml["ab_gate_x_w"][0], sml["ssd_dt_bias"][0], sml["ssd_a_log"][0], sml["ssd_d"][0])
    for key, n in (("cb0", "ab_conv_b"), ("ba", "ab_gate_a_b"), ("bx", "ab_gate_x_b"), ("lam", "ab_lambda"),
                   ("qn_w", "mla_q_norm"), ("kn_w", "mla_kv_norm"), ("g0", "ab_ln_g"), ("b0", "ab_ln_b")):
        p[key] = sml[n][0]

    acc, recv_early, loss_part, grad_x = _local_step(
        x[0], positions[0], loss_target[0], p, [big[n][0].astype(bf) for n in BIG_L1])

    send = [(_reshard([acc["g_rnn"], acc["g_gate"], acc["g_tail"]], MAP_G0, 244, bf, "reshard_w0_late", shards=W0_LATE),
             W0_LATE),
            _reshard([acc["g_wq"]], MAP_WQ, 96, bf, "reshard_wq"), _reshard([acc["g_wkv"]], MAP_WKV, 128, bf, "reshard_wkv")]
    sm_slots, vec_rows, ga, gx = _pack_small(*(acc[k] for k in (
        "dvec0", "g_wa", "g_wx", "dqnw", "dknw", "dgb0", "dvec1", "dcw1", "dnw", "dgb1")))
    recv = _exchange(send + [sm_slots], [vec_rows, ga, gx], "exchange_grads")
    me = 4 * lax.axis_index("x") + 2 * lax.axis_index("y") + lax.axis_index("c")
    parts = {"ssd_w_in": recv_early[0], "ssd_w_out": recv_early[1], "ab_w_out": recv_early[2],
             "ab_w_in": jnp.where(me >= W0_LATE[0], recv[0], recv_early[3]), "mla_w_uq": recv[1], "mla_w_ukv": recv[2]}

    outs = {}
    kinds = ("grad", "delta", "new_m", "new_v")
    for n in BIG_L0 + BIG_L1:
        for kind, res in zip(kinds, _adamw(parts[n], *big[n], "adamw_" + n)):
            outs[kind, n] = res[None]
    res = _adamw_small(*recv[3:], [sml[n] for n in SMALL_NAMES])
    for i, n in enumerate(SMALL_NAMES):
        for k, kind in enumerate(kinds):
            outs[kind, n] = res[4 * i + k].reshape(args[n].shape)

    loss = lax.psum(loss_part, ("x", "y", "c"))
    order = ["ab_w_in", "ab_conv_w", "ab_conv_b", "ab_gate_a_w", "ab_gate_a_b", "ab_gate_x_w", "ab_gate_x_b",
             "ab_lambda", "mla_q_norm", "mla_kv_norm", "mla_w_uq", "mla_w_ukv", "ab_w_out", "ab_ln_g", "ab_ln_b",
             "ssd_w_in", "ssd_conv_w", "ssd_conv_b", "ssd_dt_bias", "ssd_a_log", "ssd_d", "ssd_norm", "ssd_w_out",
             "ssd_ln_g", "ssd_ln_b"]
    return (loss, grad_x[None], *[outs[kind, n] for kind in ("grad", "delta", "new_m", "new_v") for n in order])


def _local_step(x, pos, target, p, l1_blocks):
    bf = MXU_DTYPE
    inv_freq = 10000.0 ** (-jnp.arange(0, 32, 2, dtype=F32) / 32)
    ang = pos.astype(F32)[:, None] * inv_freq
    cos, sin = jnp.cos(ang), jnp.sin(ang)
    zeros = lambda n: jnp.zeros((SEQ, n), F32)
    tc = jnp.concatenate([jnp.ones((SEQ, 64), F32), cos, cos, zeros(32)], axis=1)
    tsa = jnp.concatenate([zeros(64), -sin, zeros(48)], axis=1)
    tsb = jnp.concatenate([zeros(80), sin, zeros(32)], axis=1)

    w0p, wq, wkv, wo0, wa, wxg = (p[k] for k in ("w0p", "wq", "wkv", "wo0", "wa", "wx"))
    cw0, cb0, ba, bx, lam = (p[k] for k in ("cw0", "cb0", "ba", "bx", "lam"))
    qn_w, kn_w, g0, b0 = (p[k] for k in ("qn_w", "kn_w", "g0", "b0"))
    cw1, cb1, dt_bias, a_log, d_x, nw, g1, b1 = (p[k] for k in ("cw1", "cb1", "dt_bias", "a_log", "d_x", "nw", "g1", "b1"))
    tril = jnp.tril(jnp.ones((SSD_L, SSD_L), F32))
    expand_t = (jnp.arange(SSD_INNER)[:, None] // SSD_P == jnp.arange(LANES)[None, :]).astype(jnp.bfloat16)

    xb = x.astype(bf)
    proj0 = _mm(xb, w0p, "nn", name="l0_in")
    xc, h = _rglru_fwd(proj0, cw0, cb0, wa, ba, wxg, bx, lam)
    qn, kn, qc, kc, vc = _mla_fwd(proj0, qn_w, kn_w, wq, wkv, tc, tsa, tsb)
    o, lse, (w1_8, wo1_8) = _flash_fwd(qc, kc, vc, bcast=l1_blocks)
    w1z, w1x, w1d = _unshard(w1_8, MAP_W1, (2048, 3072, 128), "unshard_w1")
    wo1 = wo1_8.reshape(SSD_INNER, D_MODEL)
    y0, v0, x1, x1b = _l0_out(h, o, proj0, x, wo0, g0, b0)

    z = _mm(x1b, w1z, "nn", name="l1_in_z")
    xbc = _mm(x1b, w1x, "nn", name="l1_in_xbc")
    dt_raw = _mm(x1b, w1d, "nn", name="l1_in_dt")
    pre, act = _ssd_conv_fwd(xbc, cw1, cb1)
    ys, hprev = _ssd_scan_fwd(act, dt_raw, dt_bias, a_log, d_x, tril, expand_t)
    yn, dv1, dgb1, loss8 = _l1_out(ys, z, nw, wo1, x1, g1, b1, target)

    g_wo1 = _mm(yn, dv1, "tn", name="l1_dwout")
    dys, dz, dnw = _l1_gate_bwd(dv1, wo1, ys, z, nw)
    dact, ddt_raw, dvec1 = _ssd_scan_bwd(dys, act, dt_raw, hprev, dt_bias, a_log, d_x, tril, expand_t)
    dxbc, dcw1 = _ssd_conv_bwd(dact, pre, xbc, cw1)
    g_z, g_xbc = _mm(x1b, dz, "tn", name="l1_dw_z"), _mm(x1b, dxbc, "tn", name="l1_dw_xbc")
    g_dt = _mm(x1b, ddt_raw, "tn", name="l1_dw_dt")
    dx1 = _mm(dz, w1z, "nt", name="l1_dx_z", add=dv1, add_scale=DN_ALPHA)
    dx1 = _mm(dxbc, w1x, "nt", name="l1_dx_xbc", add=dx1)

    dv0, dgb0 = _ln_bwd_call(v0, dx1, ddt_raw, w1d, g0)
    g_wo0 = _mm(y0, dv0, "tn", name="l0_dwout")
    dh, do, dgate = _gate_bwd(dv0, wo0, h, o, proj0)
    dxr, g_wa, g_wx, dvec0 = _rglru_bwd(dh, xc, h, proj0, cw0, wa, ba, wxg, bx, lam)
    g_rnn, g_gate = _mm(xb, dxr, "tn", name="l0_dw_rnn"), _mm(xb, dgate, "tn", name="l0_dw_gate")
    early = [_reshard([g_z, g_xbc, g_dt], MAP_W1, 644, bf, "reshard_w1"), g_wo1.astype(bf).reshape(N_DEV, 256, D_MODEL),
             g_wo0.astype(bf).reshape(N_DEV, 128, D_MODEL),
             (_reshard([g_rnn, g_gate], MAP_G0, 244, bf, "reshard_w0_early", shards=W0_EARLY), W0_EARLY)]
    dq, dk, dvv, recv_early = _flash_bwd(qc, kc, vc, o, do, lse, scatter=early)
    dqraw, dkvraw, dtail, dqnw, dknw = _mla_bwd(dq, dk, dvv, proj0, qn_w, kn_w, wq, wkv, tc, tsa, tsb)
    g_wq = _mm(qn, dqraw, "tn", name="mla_dwq", tm=256)
    g_wkv = _mm(kn, dkvraw, "tn", name="mla_dwkv", tm=128, tn=512)
    g_tail = _mm(xb, dtail, "tn", name="l0_dw_tail")
    dx = _mm(dxr, w0p, "nt", name="l0_dx_rnn", add=dv0, add_scale=DN_ALPHA, b_col=P0_RNN)
    dx = _mm(dgate, w0p, "nt", name="l0_dx_gate", add=dx, b_col=0)
    dx = _mm(dtail, w0p, "nt", name="l0_dx_tail", add=dx, b_col=3)

    acc = {"g_rnn": g_rnn, "g_gate": g_gate, "g_tail": g_tail, "g_wq": g_wq, "g_wkv": g_wkv,
           "dvec0": dvec0, "g_wa": g_wa, "g_wx": g_wx, "dqnw": dqnw, "dknw": dknw, "dgb0": dgb0, "dvec1": dvec1,
           "dcw1": dcw1, "dnw": dnw, "dgb1": dgb1}
    return acc, recv_early, loss8[0, 0], dx
```

```python
import math

import jax
import jax.numpy as jnp
from jax import lax
from jax.experimental import pallas as pl
from jax.experimental.pallas import tpu as pltpu

F32 = jnp.float32
MXU_DTYPE = jnp.bfloat16

N_DEV = 8
SEQ = 4096
D_MODEL = 1024
DN_ALPHA = 4.0 ** 0.25
RNN_W = 512
MLA_HEADS = 8
ATT_SCALE = 96.0 ** -0.5
ATT_C = ATT_SCALE * math.log2(math.e)
RG_C = 8.0
SSD_INNER = 2048
SSD_HEADS = 32
SSD_P = 64
SSD_GROUPS = 4
SSD_N = 128
SSD_L = 128
SSD_CONV = 3072
LANES = 128
SUBLANES = 8
VMEM_LIMIT = 56 * 1024 * 1024

ADAM_LR, ADAM_B1, ADAM_B2, ADAM_EPS, ADAM_WD, ADAM_STEP = 0.001, 0.9, 0.999, 1e-08, 0.01, 10

HIGHEST = lax.Precision.HIGHEST


def _params(sem, limit=VMEM_LIMIT):
    return pltpu.CompilerParams(dimension_semantics=sem, vmem_limit_bytes=limit)


def _dot(a, b):
    return lax.dot_general(a, b, (((1,), (0,)), ((), ())), preferred_element_type=F32)


def _dot_nt(a, b):
    return lax.dot_general(a, b, (((1,), (1,)), ((), ())), preferred_element_type=F32)


def _dot_tn(a, b):
    return lax.dot_general(a, b, (((0,), (0,)), ((), ())), preferred_element_type=F32)


def _dot_hi(a, b):
    return lax.dot_general(a, b, (((1,), (0,)), ((), ())), precision=HIGHEST, preferred_element_type=F32)


def _mx(v):
    return v.astype(MXU_DTYPE)


def _sigmoid(v):
    return 1.0 / (1.0 + jnp.exp(-v))


def _log1p_pos(e):
    poly = e * (1.0 - e * (0.5 - e * (1.0 / 3.0 - e * 0.25)))
    return jnp.where(e < 0.01, poly, jnp.log(1.0 + e))


def _softplus(v):
    return jnp.maximum(v, 0.0) + _log1p_pos(jnp.exp(-jnp.abs(v)))


def _neg_expm1(v):
    poly = -v * (1.0 + v * (0.5 + v * (1.0 / 6.0 + v * (1.0 / 24.0 + v * (1.0 / 120.0)))))
    return jnp.where(jnp.abs(v) < 0.1, poly, 1.0 - jnp.exp(v))


def _silu(v):
    return v * _sigmoid(v)


def _dsilu(v):
    s = _sigmoid(v)
    return s * (1.0 + v * (1.0 - s))


def _mm(a, b, mode, *, name, add=None, add_scale=1.0, out_dtype=F32, tm=None, tn=1024, tk=512):
    if mode == "tn":
        kdim, m = a.shape
        n = b.shape[1]
        tm, tn, tk = min(tm or 1024, m), min(tn, n), min(tk, kdim)

        def body_tn(a_ref, b_ref, o_ref):
            @pl.when(pl.program_id(2) == 0)
            def _():
                o_ref[...] = jnp.zeros_like(o_ref)

            o_ref[...] += _dot_tn(_mx(a_ref[...]), _mx(b_ref[...]))

        return pl.pallas_call(
            body_tn, name=name, grid=(m // tm, n // tn, kdim // tk),
            in_specs=[pl.BlockSpec((tk, tm), lambda i, j, k: (k, i)), pl.BlockSpec((tk, tn), lambda i, j, k: (k, j))],
            out_specs=pl.BlockSpec((tm, tn), lambda i, j, k: (i, j)),
            out_shape=jax.ShapeDtypeStruct((m, n), F32),
            compiler_params=_params(("parallel", "parallel", "arbitrary")),
        )(a, b)

    m, kdim = a.shape
    n = b.shape[1] if mode == "nn" else b.shape[0]
    tm, tn = min(tm or 1024, m), min(tn, n)
    has_add = add is not None

    def body(*refs):
        a_ref, b_ref = refs[0], refs[1]
        o_ref = refs[-1]
        av, bv = _mx(a_ref[...]), _mx(b_ref[...])
        acc = _dot(av, bv) if mode == "nn" else _dot_nt(av, bv)
        if has_add:
            acc = acc + add_scale * refs[2][...]
        o_ref[...] = acc.astype(out_dtype)

    b_spec = (pl.BlockSpec((kdim, tn), lambda i, j: (0, j)) if mode == "nn"
              else pl.BlockSpec((tn, kdim), lambda i, j: (j, 0)))
    in_specs = [pl.BlockSpec((tm, kdim), lambda i, j: (i, 0)), b_spec]
    args = [a, b]
    if has_add:
        in_specs.append(pl.BlockSpec((tm, tn), lambda i, j: (i, j)))
        args.append(add)
    return pl.pallas_call(
        body, name=name, grid=(m // tm, n // tn), in_specs=in_specs,
        out_specs=pl.BlockSpec((tm, tn), lambda i, j: (i, j)),
        out_shape=jax.ShapeDtypeStruct((m, n), out_dtype),
        compiler_params=_params(("parallel", "parallel")),
    )(*args)


def _shift_down(blk, halo, s):
    if s == 0:
        return blk
    t = blk.shape[0]
    r = pltpu.roll(blk, s, 0)
    hr = pltpu.roll(halo, s, 0)
    row8 = lax.broadcasted_iota(jnp.int32, hr.shape, 0)
    head = jnp.where(row8 < s, hr, r[:SUBLANES])
    return jnp.concatenate([head, r[SUBLANES:]], axis=0) if t > SUBLANES else head


def _shift_up(blk, halo, s):
    if s == 0:
        return blk
    t = blk.shape[0]
    r = pltpu.roll(blk, t - s, 0)
    hr = pltpu.roll(halo, SUBLANES - s, 0)
    row8 = lax.broadcasted_iota(jnp.int32, hr.shape, 0)
    tail = jnp.where(row8 >= SUBLANES - s, hr, r[t - SUBLANES:])
    return jnp.concatenate([r[:t - SUBLANES], tail], axis=0) if t > SUBLANES else tail


def _scan_down(a, u):
    t = a.shape[0]
    row = lax.broadcasted_iota(jnp.int32, a.shape, 0)
    d = 1
    while d < t:
        keep = row >= d
        a_sh = jnp.where(keep, pltpu.roll(a, d, 0), 1.0)
        u_sh = jnp.where(keep, pltpu.roll(u, d, 0), 0.0)
        u = a * u_sh + u
        a = a * a_sh
        d *= 2
    return a, u


def _scan_up(a, u):
    t = a.shape[0]
    row = lax.broadcasted_iota(jnp.int32, a.shape, 0)
    d = 1
    while d < t:
        keep = row < t - d
        a_sh = jnp.where(keep, pltpu.roll(a, t - d, 0), 1.0)
        u_sh = jnp.where(keep, pltpu.roll(u, t - d, 0), 0.0)
        u = a * u_sh + u
        a = a * a_sh
        d *= 2
    return a, u


def _conv4(blk, halo, cw, cb):
    out = cb + blk * cw[3:4]
    for k in range(3):
        out = out + _shift_down(blk, halo, 3 - k) * cw[k:k + 1]
    return out


RG_T = 512
P0_RNN = 2


def _rg_gates(xc, wa, ba, wx, bx, lam):
    xcb = _mx(xc)
    r = _sigmoid(_dot(xcb, wa) + ba)
    ig = _sigmoid(_dot(xcb, wx) + bx)
    sp = _softplus(-lam)
    la = (-RG_C * r) * sp
    a = jnp.exp(la)
    mult = jnp.sqrt(_neg_expm1(2.0 * la))
    return r, ig, sp, a, mult


def _rglru_fwd(proj0, cw8, cb, wa, ba, wx, bx, lam):
    t, w = RG_T, RNN_W
    nb = SEQ // t

    def body(x_ref, halo_ref, cw_ref, cb_ref, wa_ref, ba_ref, wx_ref, bx_ref, lam_ref, xc_ref, h_ref, carry):
        i = pl.program_id(0)

        @pl.when(i == 0)
        def _():
            carry[...] = jnp.zeros_like(carry)

        blk = x_ref[...]
        halo = jnp.where(i > 0, halo_ref[...], 0.0)
        xc = _conv4(blk, halo, cw_ref[...], cb_ref[...])
        _, ig, _, a, mult = _rg_gates(xc, wa_ref[...], ba_ref[...], wx_ref[...], bx_ref[...], lam_ref[...])
        u = mult * (ig * xc)
        big_a, big_u = _scan_down(a, u)
        h = big_a * carry[SUBLANES - 1:SUBLANES, :] + big_u
        carry[...] = h[t - SUBLANES:]
        xc_ref[...] = xc
        h_ref[...] = h

    vec = pl.BlockSpec((1, w), lambda i: (0, 0))
    mat = pl.BlockSpec((w, w), lambda i: (0, 0))
    return pl.pallas_call(
        body, name="rglru_fwd", grid=(nb,),
        in_specs=[pl.BlockSpec((t, w), lambda i: (i, P0_RNN)),
                  pl.BlockSpec((SUBLANES, w), lambda i: (jnp.maximum(i * (t // SUBLANES) - 1, 0), P0_RNN)),
                  pl.BlockSpec((SUBLANES, w), lambda i: (0, 0)), vec, mat, vec, mat, vec, vec],
        out_specs=[pl.BlockSpec((t, w), lambda i: (i, 0)), pl.BlockSpec((t, w), lambda i: (i, 0))],
        out_shape=[jax.ShapeDtypeStruct((SEQ, w), F32), jax.ShapeDtypeStruct((SEQ, w), F32)],
        scratch_shapes=[pltpu.VMEM((SUBLANES, w), F32)],
        compiler_params=_params(("arbitrary",)),
    )(proj0, proj0, cw8, cb, wa, ba, wx, bx, lam)


def _rglru_bwd(dh, xc, h, proj0, cw8, wa, ba, wx, bx, lam):
    t, w = RG_T, RNN_W
    nb = SEQ // t
    tb = t // SUBLANES

    def body(dh_ref, xc_ref, h_ref, hh_ref, x_ref, cw_ref, wa_ref, ba_ref, wx_ref, bx_ref, lam_ref,
             dx_ref, dwa_ref, dwx_ref, dvec_ref, gcarry, dxc_next):
        i = pl.program_id(0)
        rev = nb - 1 - i

        @pl.when(i == 0)
        def _():
            gcarry[...] = jnp.zeros_like(gcarry)
            dxc_next[...] = jnp.zeros_like(dxc_next)
            dwa_ref[...] = jnp.zeros_like(dwa_ref)
            dwx_ref[...] = jnp.zeros_like(dwx_ref)
            dvec_ref[...] = jnp.zeros_like(dvec_ref)

        xc = xc_ref[...]
        wa_v, wx_v = wa_ref[...], wx_ref[...]
        lam_v = lam_ref[...]
        r, ig, sp, a, mult = _rg_gates(xc, wa_v, ba_ref[...], wx_v, bx_ref[...], lam_v)
        dhv = dh_ref[...]
        big_a, big_u = _scan_up(a, a * dhv)
        gg = big_a * gcarry[0:1, :] + big_u
        g = dhv + _shift_up(gg, gcarry[...], 1)
        gcarry[...] = gg[:SUBLANES]
        hhalo = jnp.where(rev > 0, hh_ref[...], 0.0)
        da = g * _shift_down(h_ref[...], hhalo, 1)
        d_mult = g * (ig * xc)
        d_i = g * (mult * xc)
        dxc = g * (mult * ig)
        d_la = da * a - d_mult * (a * a) / mult
        d_r = d_la * (-RG_C * sp)
        d_sp = jnp.sum(d_la * (-RG_C * r), axis=0, keepdims=True)
        d_pa = d_r * r * (1.0 - r)
        d_px = d_i * ig * (1.0 - ig)
        d_pab, d_pxb = _mx(d_pa), _mx(d_px)
        dxc = dxc + _dot_nt(d_pab, wa_v) + _dot_nt(d_pxb, wx_v)
        xcb = _mx(xc)
        dwa_ref[...] += _dot_tn(xcb, d_pab)
        dwx_ref[...] += _dot_tn(xcb, d_pxb)
        dvec_ref[0:1, :] += jnp.sum(d_pa, axis=0, keepdims=True)
        dvec_ref[1:2, :] += jnp.sum(d_px, axis=0, keepdims=True)
        dvec_ref[2:3, :] += d_sp * (-_sigmoid(-lam_v))
        dvec_ref[3:4, :] += jnp.sum(dxc, axis=0, keepdims=True)
        xblk = x_ref[...]
        cw = cw_ref[...]
        dx = dxc * cw[3:4]
        nxt = dxc_next[...]
        dvec_ref[7:8, :] += jnp.sum(dxc * xblk, axis=0, keepdims=True)
        for k in range(3):
            up = _shift_up(dxc, nxt, 3 - k)
            dvec_ref[4 + k:5 + k, :] += jnp.sum(up * xblk, axis=0, keepdims=True)
            dx = dx + up * cw[k:k + 1]
        dxc_next[...] = dxc[:SUBLANES]
        dx_ref[...] = _mx(dx)

    blk = pl.BlockSpec((t, w), lambda i: (nb - 1 - i, 0))
    halo = pl.BlockSpec((SUBLANES, w), lambda i: (jnp.maximum((nb - 1 - i) * tb - 1, 0), 0))
    vec = pl.BlockSpec((1, w), lambda i: (0, 0))
    mat = pl.BlockSpec((w, w), lambda i: (0, 0))
    return pl.pallas_call(
        body, name="rglru_bwd", grid=(nb,),
        in_specs=[blk, blk, blk, halo, pl.BlockSpec((t, w), lambda i: (nb - 1 - i, P0_RNN)),
                  pl.BlockSpec((SUBLANES, w), lambda i: (0, 0)), mat, vec, mat, vec, vec],
        out_specs=[blk, mat, mat, pl.BlockSpec((16, w), lambda i: (0, 0))],
        out_shape=[jax.ShapeDtypeStruct((SEQ, w), MXU_DTYPE), jax.ShapeDtypeStruct((w, w), F32),
                   jax.ShapeDtypeStruct((w, w), F32), jax.ShapeDtypeStruct((16, w), F32)],
        scratch_shapes=[pltpu.VMEM((SUBLANES, w), F32), pltpu.VMEM((SUBLANES, w), F32)],
        compiler_params=_params(("arbitrary",)),
    )(dh, xc, h, h, proj0, cw8, wa, ba, wx, bx, lam)


MLA_T = 512


def _rope(v, c, sa, sb):
    return v * c + pltpu.roll(v, LANES - 16, 1) * sa + pltpu.roll(v, 16, 1) * sb


def _rope_t(dv, c, sa, sb):
    return dv * c + pltpu.roll(dv * sa, 16, 1) + pltpu.roll(dv * sb, LANES - 16, 1)


def _rms(v, g, eps=1e-6):
    rs = lax.rsqrt(jnp.mean(v * v, axis=-1, keepdims=True) + eps)
    return v * rs * g, rs


def _mla_fwd(proj0, q_norm, kv_norm, wq, wkv, tc, tsa, tsb):
    t = MLA_T

    def body(cq_ref, ck_ref, qn_ref, kn_ref, wq_ref, wkv_ref, c_ref, sa_ref, sb_ref,
             oqn_ref, okn_ref, oq_ref, ok_ref, ov_ref):
        c, sa, sb = c_ref[...], sa_ref[...], sb_ref[...]
        ck = ck_ref[...]
        qn = _mx(_rms(cq_ref[...], qn_ref[...])[0])
        kn = _mx(_rms(ck[:, :LANES], kn_ref[...])[0])
        oqn_ref[...] = qn
        okn_ref[...] = kn
        krv = _rope(ck[:, LANES:], c, sa, sb)
        qraw = _dot(qn, wq_ref[...])
        kvraw = _dot(kn, wkv_ref[...])
        for hd in range(MLA_HEADS):
            sl = slice(hd * LANES, (hd + 1) * LANES)
            oq_ref[:, sl] = _mx(_rope(qraw[:, sl], c, sa, sb))
            ok_ref[:, sl] = _mx(kvraw[:, sl] + krv)
        ov_ref[...] = _mx(kvraw[:, 1024:])

    tab = pl.BlockSpec((t, LANES), lambda i: (i, 0))
    wide = pl.BlockSpec((t, 1024), lambda i: (i, 0))
    const = lambda shape: pl.BlockSpec(shape, lambda i: (0, 0))
    return pl.pallas_call(
        body, name="mla_fwd", grid=(SEQ // t,),
        in_specs=[pl.BlockSpec((t, 256), lambda i: (i, 6)), pl.BlockSpec((t, 256), lambda i: (i, 7)),
                  const((1, 256)), const((1, LANES)), const((256, 1024)), const((LANES, 1536)), tab, tab, tab],
        out_specs=[pl.BlockSpec((t, 256), lambda i: (i, 0)), tab, wide, wide, pl.BlockSpec((t, 512), lambda i: (i, 0))],
        out_shape=[jax.ShapeDtypeStruct((SEQ, 256), MXU_DTYPE), jax.ShapeDtypeStruct((SEQ, LANES), MXU_DTYPE),
                   jax.ShapeDtypeStruct((SEQ, 1024), MXU_DTYPE), jax.ShapeDtypeStruct((SEQ, 1024), MXU_DTYPE),
                   jax.ShapeDtypeStruct((SEQ, 512), MXU_DTYPE)],
        compiler_params=_params(("parallel",)),
    )(proj0, proj0, q_norm, kv_norm, wq, wkv, tc, tsa, tsb)


ATT_T = 1024


def _flash_fwd(q, k, v, bcast=()):
    t = ATT_T
    nb = SEQ // t

    steps = [(qi, ki) for qi in range(nb) for ki in range(qi + 1)]
    qi_tab = jnp.asarray([s[0] for s in steps], jnp.int32)
    ki_tab = jnp.asarray([s[1] for s in steps], jnp.int32)

    nx = len(bcast)

    def body(qi_ref, ki_ref, q_ref, k_ref, v_ref, *rest):
        x_refs, (o_ref, lse_ref), g_refs = rest[:nx], rest[nx:nx + 2], rest[nx + 2:2 * nx + 2]
        m_sc, acc_sc = rest[2 * nx + 2:2 * nx + 4]
        step = pl.program_id(1)
        qi, ki = qi_ref[step], ki_ref[step]
        if nx:
            copies = _peer_copies(x_refs, g_refs, rest[2 * nx + 4:], [])

            @pl.when((pl.program_id(0) == 0) & (step == 0))
            def _():
                for cp in copies:
                    cp.start()

        @pl.when(ki == 0)
        def _():
            m_sc[...] = jnp.full_like(m_sc, -jnp.inf)
            acc_sc[...] = jnp.zeros_like(acc_sc)

        def update(diagonal):
            vv = v_ref[...]
            lane_v = lax.broadcasted_iota(jnp.int32, vv.shape, 1)
            for hd in range(2):
                sl = slice(hd * LANES, (hd + 1) * LANES)
                s = _dot_nt(q_ref[:, sl], k_ref[:, sl])
                if diagonal:
                    s = jnp.where(lax.broadcasted_iota(jnp.int32, (t, t), 1)
                                  <= lax.broadcasted_iota(jnp.int32, (t, t), 0), s, -jnp.inf)
                m_prev = m_sc[hd]
                m_new = jnp.maximum(m_prev, jnp.max(s, axis=1, keepdims=True))
                p = jnp.exp2((s - m_new[:, :1]) * ATT_C)
                m_sc[hd] = m_new
                vh = jnp.where((lane_v >= hd * 64) & (lane_v < (hd + 1) * 64), vv, jnp.ones_like(vv))
                acc_sc[hd] = acc_sc[hd] * jnp.exp2((m_prev - m_new) * ATT_C) + _dot(_mx(p), vh)

        @pl.when(ki < qi)
        def _():
            update(False)

        @pl.when(ki == qi)
        def _():
            update(True)
            first = lax.broadcasted_iota(jnp.int32, (t, LANES), 1) < 64
            a0, a1 = acc_sc[0], acc_sc[1]
            l0, l1 = pltpu.roll(a0, 64, 1), pltpu.roll(a1, 64, 1)
            o_ref[...] = jnp.where(first, a0 / l0, a1 / l1)
            lse_ref[0] = jnp.where(first, m_sc[0] * ATT_SCALE + jnp.log(l0), m_sc[1] * ATT_SCALE + jnp.log(l1))

        if nx:
            @pl.when((pl.program_id(0) == 3) & (step == len(steps) - 1))
            def _():
                for cp in copies:
                    cp.wait()

    grid_spec = pltpu.PrefetchScalarGridSpec(
        num_scalar_prefetch=2, grid=(4, len(steps)),
        in_specs=[pl.BlockSpec((t, 256), lambda p, s, qt, kt: (qt[s], p)),
                  pl.BlockSpec((t, 256), lambda p, s, qt, kt: (kt[s], p)),
                  pl.BlockSpec((t, LANES), lambda p, s, qt, kt: (kt[s], p))] + [ANY] * nx,
        out_specs=[pl.BlockSpec((t, LANES), lambda p, s, qt, kt: (qt[s], p)),
                   pl.BlockSpec((1, t, LANES), lambda p, s, qt, kt: (p, qt[s], 0))] + [ANY] * nx,
        scratch_shapes=[pltpu.VMEM((2, t, LANES), F32), pltpu.VMEM((2, t, LANES), F32)]
        + (_exchange_sems(nx) if nx else []))
    res = pl.pallas_call(
        body, name="flash_fwd", grid_spec=grid_spec,
        out_shape=[jax.ShapeDtypeStruct((SEQ, 512), F32), jax.ShapeDtypeStruct((4, SEQ, LANES), F32)]
        + _exchange_shapes([], bcast),
        compiler_params=_params(("arbitrary", "arbitrary")),
    )(qi_tab, ki_tab, q, k, v, *bcast)
    return res[0], res[1], res[2:]


def _flash_bwd(q, k, v, o, do, lse, scatter=()):
    t = ATT_T
    nb = SEQ // t

    steps = [(qi, ki) for ki in range(nb) for qi in range(ki, nb)]
    qi_tab = jnp.asarray([s[0] for s in steps], jnp.int32)
    ki_tab = jnp.asarray([s[1] for s in steps], jnp.int32)
    log2e = math.log2(math.e)

    sc_arrays, sc_ranges = _scatter_args(scatter)
    nx = len(sc_arrays)

    def body(qi_ref, ki_ref, q_ref, k_ref, v_ref, o_ref, do_ref, lse_ref, *rest):
        x_refs, (dq_ref, dk_ref, dv_ref), g_refs = rest[:nx], rest[nx:nx + 3], rest[nx + 3:2 * nx + 3]
        step = pl.program_id(1)
        qi, ki = qi_ref[step], ki_ref[step]
        if nx:
            copies = _peer_copies(x_refs, g_refs, rest[2 * nx + 3:], sc_ranges)

            @pl.when((pl.program_id(0) == 0) & (step == 0))
            def _():
                for cp in copies:
                    cp.start()

        @pl.when(step == 0)
        def _():
            dq_ref[...] = jnp.zeros_like(dq_ref)

        @pl.when(qi == ki)
        def _():
            dk_ref[...] = jnp.zeros_like(dk_ref)
            dv_ref[...] = jnp.zeros_like(dv_ref)

        def update(diagonal):
            dov, ov, vv = do_ref[...], o_ref[...], v_ref[...]
            lse2 = lse_ref[0] * log2e
            lane = lax.broadcasted_iota(jnp.int32, (t, LANES), 1)
            prod = dov * ov
            qrows = pl.ds(pl.multiple_of(qi * t, t), t)
            dv_acc = jnp.zeros((t, LANES), F32)
            dk_new, dq_new = [], []
            for hd in range(2):
                sl = slice(hd * LANES, (hd + 1) * LANES)
                mine = (lane >= hd * 64) & (lane < (hd + 1) * 64)
                qh, kh = q_ref[:, sl], k_ref[:, sl]
                p = jnp.exp2(_dot_nt(qh, kh) * ATT_C - lse2[:, hd * 64:hd * 64 + 1])
                if diagonal:
                    p = jnp.where(lax.broadcasted_iota(jnp.int32, (t, t), 1)
                                  <= lax.broadcasted_iota(jnp.int32, (t, t), 0), p, 0.0)
                do_h = jnp.where(mine, dov, 0.0)
                delta = jnp.sum(jnp.where(mine, prod, 0.0), axis=1, keepdims=True)
                dp = _dot_nt(_mx(do_h), vv)
                ds = _mx(p * (dp - delta) * ATT_SCALE)
                dv_acc = dv_acc + jnp.where(mine, _dot_tn(_mx(p), _mx(dov)), 0.0)
                dk_new.append(_dot_tn(ds, qh))
                dq_new.append(_dot(ds, kh))
            for hd in range(2):
                sl = slice(hd * LANES, (hd + 1) * LANES)
                dk_ref[:, sl] += dk_new[hd]
                dq_ref[qrows, sl] += dq_new[hd]
            dv_ref[...] += dv_acc

        @pl.when(qi > ki)
        def _():
            update(False)

        @pl.when(qi == ki)
        def _():
            update(True)

        if nx:
            @pl.when((pl.program_id(0) == 3) & (step == len(steps) - 1))
            def _():
                for cp in copies:
                    cp.wait()

    qmap = lambda p, s, qt, kt: (qt[s], p)
    kmap = lambda p, s, qt, kt: (kt[s], p)
    grid_spec = pltpu.PrefetchScalarGridSpec(
        num_scalar_prefetch=2, grid=(4, len(steps)),
        in_specs=[pl.BlockSpec((t, 256), qmap), pl.BlockSpec((t, 256), kmap), pl.BlockSpec((t, LANES), kmap),
                  pl.BlockSpec((t, LANES), qmap), pl.BlockSpec((t, LANES), qmap),
                  pl.BlockSpec((1, t, LANES), lambda p, s, qt, kt: (p, qt[s], 0))] + [ANY] * nx,
        out_specs=[pl.BlockSpec((SEQ, 256), lambda p, s, qt, kt: (0, p)), pl.BlockSpec((t, 256), kmap),
                   pl.BlockSpec((t, LANES), kmap)] + [ANY] * nx,
        scratch_shapes=_exchange_sems(nx) if nx else [])
    res = pl.pallas_call(
        body, name="flash_bwd", grid_spec=grid_spec,
        out_shape=[jax.ShapeDtypeStruct((SEQ, 1024), F32), jax.ShapeDtypeStruct((SEQ, 1024), F32),
                   jax.ShapeDtypeStruct((SEQ, 512), F32)] + _exchange_shapes(sc_arrays, []),
        compiler_params=_params(("arbitrary", "arbitrary")),
    )(qi_tab, ki_tab, q, k, v, o, do, lse, *sc_arrays)
    return res[0], res[1], res[2], res[3:]


def _rms_bwd(v, g, dy, eps=1e-6):
    rs = lax.rsqrt(jnp.mean(v * v, axis=-1, keepdims=True) + eps)
    xh = v * rs
    dxh = dy * g
    dv = rs * (dxh - xh * jnp.mean(dxh * xh, axis=-1, keepdims=True))
    return dv, jnp.sum(dy * xh, axis=0, keepdims=True)


def _mla_bwd(dq, dk, dv, proj0, qlat, klat, q_norm, kv_norm, wq, wkv, tc, tsa, tsb):
    t = MLA_T

    def body(dq_ref, dk_ref, dv_ref, cq_ref, ck_ref, ql_ref, kl_ref, qn_ref, kn_ref, wq_ref, wkv_ref,
             c_ref, sa_ref, sb_ref, o_ref, gwq_ref, gwkv_ref, dgq_ref, dgk_ref, oq_ref, okv_ref):
        @pl.when(pl.program_id(0) == 0)
        def _():
            dgq_ref[...] = jnp.zeros_like(dgq_ref)
            dgk_ref[...] = jnp.zeros_like(dgk_ref)
            gwq_ref[...] = jnp.zeros_like(gwq_ref)
            gwkv_ref[...] = jnp.zeros_like(gwkv_ref)

        c, sa, sb = c_ref[...], sa_ref[...], sb_ref[...]
        lane = lax.broadcasted_iota(jnp.int32, (t, LANES), 1)
        dkr = jnp.zeros((t, LANES), F32)
        for hd in range(MLA_HEADS):
            sl = slice(hd * LANES, (hd + 1) * LANES)
            oq_ref[:, sl] = _mx(_rope_t(dq_ref[:, sl], c, sa, sb))
            dkh = dk_ref[:, sl]
            okv_ref[:, sl] = _mx(dkh)
            dkr = dkr + dkh
        okv_ref[:, 1024:] = _mx(dv_ref[...])
        dkr = _rope_t(jnp.where((lane >= 64) & (lane < 96), dkr, 0.0), c, sa, sb)
        dqraw, dkvraw = oq_ref[...], okv_ref[...]
        gwq_ref[...] += _dot_tn(ql_ref[...], dqraw)
        gwkv_ref[...] += _dot_tn(kl_ref[...], dkvraw)
        dqn = _dot_nt(dqraw, wq_ref[...])
        dkn = _dot_nt(dkvraw, wkv_ref[...])
        dcq, dgq = _rms_bwd(cq_ref[...], qn_ref[...], dqn)
        dck, dgk = _rms_bwd(ck_ref[:, :LANES], kn_ref[...], dkn)
        o_ref[:, :256] = _mx(dcq)
        o_ref[:, 256:384] = _mx(dck)
        o_ref[:, 384:] = _mx(dkr)
        dgq_ref[0:1, :] += dgq
        dgk_ref[0:1, :] += dgk

    tab = pl.BlockSpec((t, LANES), lambda i: (i, 0))
    wide = pl.BlockSpec((t, 1024), lambda i: (i, 0))
    const = lambda shape: pl.BlockSpec(shape, lambda i: (0, 0))
    return pl.pallas_call(
        body, name="mla_bwd", grid=(SEQ // t,),
        in_specs=[wide, wide, pl.BlockSpec((t, 512), lambda i: (i, 0)),
                  pl.BlockSpec((t, 256), lambda i: (i, 6)), pl.BlockSpec((t, 256), lambda i: (i, 7)),
                  pl.BlockSpec((t, 256), lambda i: (i, 0)), tab,
                  const((1, 256)), const((1, LANES)), const((256, 1024)), const((LANES, 1536)), tab, tab, tab],
        out_specs=[pl.BlockSpec((t, 512), lambda i: (i, 0)), const((256, 1024)), const((LANES, 1536)),
                   const((SUBLANES, 256)), const((SUBLANES, LANES))],
        out_shape=[jax.ShapeDtypeStruct((SEQ, 512), MXU_DTYPE), jax.ShapeDtypeStruct((256, 1024), F32),
                   jax.ShapeDtypeStruct((LANES, 1536), F32), jax.ShapeDtypeStruct((SUBLANES, 256), F32),
                   jax.ShapeDtypeStruct((SUBLANES, LANES), F32)],
        scratch_shapes=[pltpu.VMEM((t, 1024), MXU_DTYPE), pltpu.VMEM((t, 1536), MXU_DTYPE)],
        compiler_params=_params(("arbitrary",)),
    )(dq, dk, dv, proj0, proj0, qlat, klat, q_norm, kv_norm, wq, wkv, tc, tsa, tsb)


LN_T = 512


def _ln(v, g, b, eps=1e-5):
    mu = jnp.mean(v, axis=-1, keepdims=True)
    xc = v - mu
    rs = lax.rsqrt(jnp.mean(xc * xc, axis=-1, keepdims=True) + eps)
    return xc * rs * g + b


def _ln_bwd(v, g, dy, eps=1e-5):
    mu = jnp.mean(v, axis=-1, keepdims=True)
    xc = v - mu
    rs = lax.rsqrt(jnp.mean(xc * xc, axis=-1, keepdims=True) + eps)
    xh = xc * rs
    dxh = dy * g
    dv = rs * (dxh - jnp.mean(dxh, axis=-1, keepdims=True) - xh * jnp.mean(dxh * xh, axis=-1, keepdims=True))
    return dv, jnp.sum(dy * xh, axis=0, keepdims=True), jnp.sum(dy, axis=0, keepdims=True)


def _l0_out(h, o, proj0, x, w_out, g, b):
    t = LN_T

    def body(h_ref, o_ref, ga_ref, gb_ref, x_ref, w_ref, g_ref, b_ref, y_ref, v_ref, x1_ref, x1b_ref):
        y = _mx(jnp.concatenate([h_ref[...] * _silu(ga_ref[...]), o_ref[...] * _silu(gb_ref[...])], axis=1))
        v = DN_ALPHA * x_ref[...] + _dot(y, w_ref[...])
        y_ref[...] = y
        v_ref[...] = v
        x1 = _ln(v, g_ref[...], b_ref[...])
        x1_ref[...] = x1
        x1b_ref[...] = _mx(x1)

    half = pl.BlockSpec((t, 512), lambda i: (i, 0))
    full = pl.BlockSpec((t, D_MODEL), lambda i: (i, 0))
    vec = pl.BlockSpec((1, D_MODEL), lambda i: (0, 0))
    return pl.pallas_call(
        body, name="l0_out", grid=(SEQ // t,),
        in_specs=[half, half, pl.BlockSpec((t, 512), lambda i: (i, 0)), pl.BlockSpec((t, 512), lambda i: (i, 1)), full,
                  pl.BlockSpec((D_MODEL, D_MODEL), lambda i: (0, 0)), vec, vec],
        out_specs=[full, full, full, full],
        out_shape=[jax.ShapeDtypeStruct((SEQ, D_MODEL), MXU_DTYPE), jax.ShapeDtypeStruct((SEQ, D_MODEL), F32),
                   jax.ShapeDtypeStruct((SEQ, D_MODEL), F32), jax.ShapeDtypeStruct((SEQ, D_MODEL), MXU_DTYPE)],
        compiler_params=_params(("parallel",)),
    )(h, o, proj0, proj0, x, w_out, g, b)


def _ln_bwd_call(v, dy, ddt, w1d, g):
    t = LN_T

    def body(v_ref, dy_ref, ddt_ref, w_ref, g_ref, dv_ref, dgb_ref):
        @pl.when(pl.program_id(0) == 0)
        def _():
            dgb_ref[...] = jnp.zeros_like(dgb_ref)

        dy_v = dy_ref[...] + _dot_nt(_mx(ddt_ref[...]), w_ref[...])
        dv, dg, db = _ln_bwd(v_ref[...], g_ref[...], dy_v)
        dv_ref[...] = dv
        dgb_ref[0:1, :] += dg
        dgb_ref[1:2, :] += db

    full = pl.BlockSpec((t, D_MODEL), lambda i: (i, 0))
    return pl.pallas_call(
        body, name="ln_bwd", grid=(SEQ // t,),
        in_specs=[full, full, pl.BlockSpec((t, LANES), lambda i: (i, 0)), pl.BlockSpec((D_MODEL, LANES), lambda i: (0, 0)),
                  pl.BlockSpec((1, D_MODEL), lambda i: (0, 0))],
        out_specs=[full, pl.BlockSpec((SUBLANES, D_MODEL), lambda i: (0, 0))],
        out_shape=[jax.ShapeDtypeStruct((SEQ, D_MODEL), F32), jax.ShapeDtypeStruct((SUBLANES, D_MODEL), F32)],
        compiler_params=_params(("arbitrary",)),
    )(v, dy, ddt, w1d, g)


def _gate_bwd(dv0, w_out, h, o, proj0):
    t = LN_T

    def body(dv_ref, w_ref, h_ref, o_ref, ga_ref, gb_ref, dh_ref, do_ref, dg_ref):
        dy = _dot_nt(_mx(dv_ref[...]), w_ref[...])
        ga, gb, dya, dyb = ga_ref[...], gb_ref[...], dy[:, :512], dy[:, 512:]
        dh_ref[...] = dya * _silu(ga)
        do_ref[...] = dyb * _silu(gb)
        dg_ref[:, :512] = _mx(dya * h_ref[...] * _dsilu(ga))
        dg_ref[:, 512:] = _mx(dyb * o_ref[...] * _dsilu(gb))

    half = pl.BlockSpec((t, 512), lambda i: (i, 0))
    half1 = pl.BlockSpec((t, 512), lambda i: (i, 1))
    full = pl.BlockSpec((t, 1024), lambda i: (i, 0))
    return pl.pallas_call(
        body, name="gate_bwd", grid=(SEQ // t,),
        in_specs=[full, pl.BlockSpec((D_MODEL, D_MODEL), lambda i: (0, 0)), half, half, half, half1],
        out_specs=[half, half, full],
        out_shape=[jax.ShapeDtypeStruct((SEQ, 512), F32), jax.ShapeDtypeStruct((SEQ, 512), F32),
                   jax.ShapeDtypeStruct((SEQ, 1024), MXU_DTYPE)],
        compiler_params=_params(("parallel",)),
    )(dv0, w_out, h, o, proj0, proj0)


CONV_T = 512
CONV_CB = 1024


def _ssd_conv_fwd(xbc, cw8, cb):
    t, cbk = CONV_T, CONV_CB
    tb = t // SUBLANES

    def body(x_ref, halo_ref, cw_ref, cb_ref, pre_ref, act_ref):
        halo = jnp.where(pl.program_id(1) > 0, halo_ref[...], 0.0)
        pre = _conv4(x_ref[...], halo, cw_ref[...], cb_ref[...])
        pre_ref[...] = pre
        act_ref[...] = _silu(pre)

    blk = pl.BlockSpec((t, cbk), lambda j, i: (i, j))
    return pl.pallas_call(
        body, name="ssd_conv_fwd", grid=(SSD_CONV // cbk, SEQ // t),
        in_specs=[blk, pl.BlockSpec((SUBLANES, cbk), lambda j, i: (jnp.maximum(i * tb - 1, 0), j)),
                  pl.BlockSpec((SUBLANES, cbk), lambda j, i: (0, j)), pl.BlockSpec((1, cbk), lambda j, i: (0, j))],
        out_specs=[blk, blk],
        out_shape=[jax.ShapeDtypeStruct((SEQ, SSD_CONV), F32), jax.ShapeDtypeStruct((SEQ, SSD_CONV), F32)],
        compiler_params=_params(("parallel", "parallel")),
    )(xbc, xbc, cw8, cb)


def _ssd_conv_bwd(dact, pre, xbc, cw8):
    t, cbk = CONV_T, CONV_CB
    tb = t // SUBLANES
    nb = SEQ // t

    def body(da_ref, dan_ref, pre_ref, pren_ref, x_ref, cw_ref, dx_ref, dcw_ref):
        i = pl.program_id(1)

        @pl.when(i == 0)
        def _():
            dcw_ref[...] = jnp.zeros_like(dcw_ref)

        dpre = da_ref[...] * _dsilu(pre_ref[...])
        dpre_next = jnp.where(i < nb - 1, dan_ref[...] * _dsilu(pren_ref[...]), 0.0)
        xblk = x_ref[...]
        cw = cw_ref[...]
        dx = dpre * cw[3:4]
        dcw_ref[3:4, :] += jnp.sum(dpre * xblk, axis=0, keepdims=True)
        for k in range(3):
            up = _shift_up(dpre, dpre_next, 3 - k)
            dcw_ref[k:k + 1, :] += jnp.sum(up * xblk, axis=0, keepdims=True)
            dx = dx + up * cw[k:k + 1]
        dcw_ref[4:5, :] += jnp.sum(dpre, axis=0, keepdims=True)
        dx_ref[...] = _mx(dx)

    blk = pl.BlockSpec((t, cbk), lambda j, i: (i, j))
    nxt = pl.BlockSpec((SUBLANES, cbk), lambda j, i: (jnp.minimum((i + 1) * tb, SEQ // SUBLANES - 1), j))
    acc = pl.BlockSpec((SUBLANES, cbk), lambda j, i: (0, j))
    return pl.pallas_call(
        body, name="ssd_conv_bwd", grid=(SSD_CONV // cbk, nb),
        in_specs=[blk, nxt, blk, nxt, blk, acc],
        out_specs=[blk, acc],
        out_shape=[jax.ShapeDtypeStruct((SEQ, SSD_CONV), MXU_DTYPE), jax.ShapeDtypeStruct((SUBLANES, SSD_CONV), F32)],
        compiler_params=_params(("parallel", "arbitrary")),
    )(dact, dact, pre, pre, xbc, cw8)


def _ssd_common(dt_raw, bias, alog, tril, expand_t, xs):
    lane = lax.broadcasted_iota(jnp.int32, dt_raw.shape, 1)
    dt = jnp.where(lane < SSD_HEADS, _softplus(dt_raw + bias), 0.0)
    a_neg = -jnp.exp(alog)
    cs = _dot_hi(tril, dt * a_neg)
    dt_x = _expand_heads(dt, expand_t)
    ecs_x = _expand_heads(jnp.exp(cs), expand_t)
    ds_x = _expand_heads(jnp.exp(cs[SSD_L - 1:SSD_L, :] - cs), expand_t)
    return dt, a_neg, cs, dt_x, None, xs * dt_x, ds_x, ecs_x, ecs_x[SSD_L - 1:SSD_L, :]


def _expand_heads(v, expand_t):
    hi = v.astype(jnp.bfloat16)
    lo = (v - hi.astype(F32)).astype(jnp.bfloat16)
    return _dot_nt(hi, expand_t) + _dot_nt(lo, expand_t)


def _fold_heads(v, expand_t):
    hi = v.astype(jnp.bfloat16)
    lo = (v - hi.astype(F32)).astype(jnp.bfloat16)
    return _dot(hi, expand_t) + _dot(lo, expand_t)


def _ssd_decay(cs, cs_t, hh, causal):
    seg = cs[:, hh:hh + 1] - cs_t[hh:hh + 1, :]
    return jnp.where(causal, jnp.exp(jnp.where(causal, seg, 0.0)), 0.0)


def _ssd_scan_fwd(act, dt_raw, bias, alog, d_x, tril, expand_t):
    nc = SEQ // SSD_L
    gw = SSD_INNER // SSD_GROUPS

    def body(act_ref, dt_ref, bias_ref, alog_ref, dx_ref, tril_ref, et_ref, y_ref, hp_ref, h_sc):
        @pl.when(pl.program_id(0) == 0)
        def _():
            h_sc[...] = jnp.zeros_like(h_sc)

        xs = act_ref[:, :SSD_INNER]
        _, _, cs, _, _, xdt, ds_x, ecs_x, elast = _ssd_common(
            dt_ref[...], bias_ref[...], alog_ref[...], tril_ref[...], et_ref[...], xs)
        cs_t = cs.T
        causal = (lax.broadcasted_iota(jnp.int32, (SSD_L, SSD_L), 0)
                  >= lax.broadcasted_iota(jnp.int32, (SSD_L, SSD_L), 1))
        lane = lax.broadcasted_iota(jnp.int32, (SSD_L, LANES), 1)
        xdt_b = _mx(xdt)
        xds_b = _mx(xdt * ds_x)
        hp_ref[0] = h_sc[...]
        for g in range(SSD_GROUPS):
            gs = slice(g * gw, (g + 1) * gw)
            bg = _mx(act_ref[:, SSD_INNER + g * SSD_N:SSD_INNER + (g + 1) * SSD_N])
            cg = _mx(act_ref[:, SSD_INNER + 512 + g * SSD_N:SSD_INNER + 512 + (g + 1) * SSD_N])
            cb = _dot_nt(cg, bg)
            hprev = h_sc[:, gs]
            yoff = _dot(cg, _mx(hprev)) * ecs_x[:, gs]
            h_sc[:, gs] = hprev * elast[:, gs] + _dot_tn(bg, xds_b[:, gs])
            for pr in range(4):
                ps = slice(g * gw + pr * LANES, g * gw + (pr + 1) * LANES)
                xp = xdt_b[:, ps]
                ydiag = jnp.zeros((SSD_L, LANES), F32)
                for j in range(2):
                    dm = _ssd_decay(cs, cs_t, g * 8 + pr * 2 + j, causal)
                    mine = (lane >= j * 64) & (lane < (j + 1) * 64)
                    ydiag = ydiag + _dot(_mx(cb * dm), jnp.where(mine, xp, jnp.zeros_like(xp)))
                y_ref[:, ps] = ydiag + yoff[:, pr * LANES:(pr + 1) * LANES] + dx_ref[:, ps] * xs[:, ps]

    const = lambda shape: pl.BlockSpec(shape, lambda c: (0, 0))
    return pl.pallas_call(
        body, name="ssd_scan_fwd", grid=(nc,),
        in_specs=[pl.BlockSpec((SSD_L, SSD_CONV), lambda c: (c, 0)), pl.BlockSpec((SSD_L, LANES), lambda c: (c, 0)),
                  const((1, LANES)), const((1, LANES)), const((1, SSD_INNER)), const((SSD_L, SSD_L)),
                  const((SSD_INNER, LANES))],
        out_specs=[pl.BlockSpec((SSD_L, SSD_INNER), lambda c: (c, 0)),
                   pl.BlockSpec((1, SSD_N, SSD_INNER), lambda c: (c, 0, 0))],
        out_shape=[jax.ShapeDtypeStruct((SEQ, SSD_INNER), F32), jax.ShapeDtypeStruct((nc, SSD_N, SSD_INNER), F32)],
        scratch_shapes=[pltpu.VMEM((SSD_N, SSD_INNER), F32)],
        compiler_params=_params(("arbitrary",)),
    )(act, dt_raw, bias, alog, d_x, tril, expand_t)


def _ssd_scan_bwd(dy, act, dt_raw, hprev_all, bias, alog, d_x, tril, expand_t):
    nc = SEQ // SSD_L
    gw = SSD_INNER // SSD_GROUPS

    def body(dy_ref, act_ref, dt_ref, hp_ref, bias_ref, alog_ref, dx_ref, tril_ref, et_ref,
             dact_ref, ddt_ref, dvec_ref, dh_sc, dd_sc):
        i = pl.program_id(0)

        @pl.when(i == 0)
        def _():
            dh_sc[...] = jnp.zeros_like(dh_sc)
            dd_sc[...] = jnp.zeros_like(dd_sc)
            dvec_ref[...] = jnp.zeros_like(dvec_ref)

        xs = act_ref[:, :SSD_INNER]
        dt_raw_v, bias_v = dt_ref[...], bias_ref[...]
        dt, a_neg, cs, dt_x, _, xdt, ds_x, ecs_x, elast = _ssd_common(
            dt_raw_v, bias_v, alog_ref[...], tril_ref[...], et_ref[...], xs)
        cs_t = cs.T
        rowi = lax.broadcasted_iota(jnp.int32, (SSD_L, SSD_L), 0)
        coli = lax.broadcasted_iota(jnp.int32, (SSD_L, SSD_L), 1)
        causal = rowi >= coli
        lane = lax.broadcasted_iota(jnp.int32, (SSD_L, LANES), 1)
        row_g = lax.broadcasted_iota(jnp.int32, (SSD_L, gw), 0)
        dyv = dy_ref[...]
        dd_sc[0:1, :] += jnp.sum(dyv * xs, axis=0, keepdims=True)
        xdt_b = _mx(xdt)
        xds = xdt * ds_x
        xds_b = _mx(xds)
        dy_b = _mx(dyv)
        dye_b = _mx(dyv * ecs_x)
        dcs = jnp.zeros((SSD_L, LANES), F32)
        dcs_t = jnp.zeros((LANES, SSD_L), F32)
        dcs_parts = []
        dxdt_parts = []
        for g in range(SSD_GROUPS):
            gs = slice(g * gw, (g + 1) * gw)
            bcol = slice(SSD_INNER + g * SSD_N, SSD_INNER + (g + 1) * SSD_N)
            ccol = slice(SSD_INNER + 512 + g * SSD_N, SSD_INNER + 512 + (g + 1) * SSD_N)
            bg, cg = _mx(act_ref[:, bcol]), _mx(act_ref[:, ccol])
            cb = _dot_nt(cg, bg)
            hp = hp_ref[0, :, gs]
            hp_b = _mx(hp)
            dh = dh_sc[:, gs]
            dh_b = _mx(dh)
            yoff = _dot(cg, hp_b) * ecs_x[:, gs]
            bdh = _dot(bg, dh_b)
            tt = xds[:, gs] * bdh
            last_row = (jnp.sum(tt, axis=0, keepdims=True)
                        + jnp.sum(dh * hp, axis=0, keepdims=True) * elast[:, gs])
            dcs_parts.append(dyv[:, gs] * yoff - tt + jnp.where(row_g == SSD_L - 1, last_row, 0.0))
            dc_g = _dot_nt(dye_b[:, gs], hp_b)
            db_g = _dot_nt(xds_b[:, gs], dh_b)
            dh_sc[:, gs] = _dot_tn(cg, dye_b[:, gs]) + dh * elast[:, gs]
            wsum = jnp.zeros((SSD_L, SSD_L), F32)
            dxdt_g = []
            for pr in range(4):
                ps = slice(g * gw + pr * LANES, g * gw + (pr + 1) * LANES)
                xp, dyp = xdt_b[:, ps], dy_b[:, ps]
                dxp = jnp.zeros((SSD_L, LANES), F32)
                for j in range(2):
                    hh = g * 8 + pr * 2 + j
                    dm = _ssd_decay(cs, cs_t, hh, causal)
                    mine = (lane >= j * 64) & (lane < (j + 1) * 64)
                    dy_h = jnp.where(mine, dyp, jnp.zeros_like(dyp))
                    wd = _dot_nt(dy_h, xp) * dm
                    wsum = wsum + wd
                    gmat = wd * cb
                    dcs = dcs + jnp.where(lane == hh, jnp.sum(gmat, axis=1, keepdims=True), 0.0)
                    dcs_t = dcs_t - jnp.where(rowi == hh, jnp.sum(gmat, axis=0, keepdims=True), 0.0)
                    dxp = dxp + _dot_tn(_mx(cb * dm), dy_h)
                dxdt_g.append(dxp)
            dxdt_parts.append(jnp.concatenate(dxdt_g, axis=1) + bdh * ds_x[:, gs])
            ws_b = _mx(wsum)
            dact_ref[:, ccol] = dc_g + _dot(ws_b, bg)
            dact_ref[:, bcol] = db_g + _dot_tn(ws_b, cg)
        dxdt = jnp.concatenate(dxdt_parts, axis=1)
        dcs_x = jnp.concatenate(dcs_parts, axis=1)
        et = et_ref[...]
        dcs_tot = dcs + dcs_t.T + _fold_heads(dcs_x, et)
        da_dt = _dot_hi((coli >= rowi).astype(F32), dcs_tot)
        ddt = da_dt * a_neg + _fold_heads(dxdt * xs, et)
        ddt_raw = ddt * _sigmoid(dt_raw_v + bias_v)
        ddt_ref[...] = ddt_raw
        dvec_ref[0:1, :] += jnp.sum(ddt_raw, axis=0, keepdims=True)
        dvec_ref[1:2, :] += jnp.sum(da_dt * dt, axis=0, keepdims=True) * a_neg
        dact_ref[:, :SSD_INNER] = dyv * dx_ref[...] + dxdt * dt_x

        @pl.when(i == nc - 1)
        def _():
            dvec_ref[2:3, :] = _fold_heads(dd_sc[...], et)[0:1, :]

    const = lambda shape: pl.BlockSpec(shape, lambda c: (0, 0))
    rev = lambda c: (nc - 1 - c, 0)
    return pl.pallas_call(
        body, name="ssd_scan_bwd", grid=(nc,),
        in_specs=[pl.BlockSpec((SSD_L, SSD_INNER), rev), pl.BlockSpec((SSD_L, SSD_CONV), rev),
                  pl.BlockSpec((SSD_L, LANES), rev),
                  pl.BlockSpec((1, SSD_N, SSD_INNER), lambda c: (nc - 1 - c, 0, 0)),
                  const((1, LANES)), const((1, LANES)), const((1, SSD_INNER)), const((SSD_L, SSD_L)),
                  const((SSD_INNER, LANES))],
        out_specs=[pl.BlockSpec((SSD_L, SSD_CONV), rev), pl.BlockSpec((SSD_L, LANES), rev), const((SUBLANES, LANES))],
        out_shape=[jax.ShapeDtypeStruct((SEQ, SSD_CONV), F32), jax.ShapeDtypeStruct((SEQ, LANES), F32),
                   jax.ShapeDtypeStruct((SUBLANES, LANES), F32)],
        scratch_shapes=[pltpu.VMEM((SSD_N, SSD_INNER), F32), pltpu.VMEM((SUBLANES, SSD_INNER), F32)],
        compiler_params=_params(("arbitrary",)),
    )(dy, act, dt_raw, hprev_all, bias, alog, d_x, tril, expand_t)


L1_T = 512


def _gated_norm(y, z, nw):
    y2 = y * _silu(z)
    gw = SSD_INNER // SSD_GROUPS
    outs, xhs, rss = [], [], []
    for g in range(SSD_GROUPS):
        gs = slice(g * gw, (g + 1) * gw)
        v = y2[:, gs]
        rs = lax.rsqrt(jnp.mean(v * v, axis=-1, keepdims=True) + 1e-6)
        xhs.append(v * rs)
        rss.append(rs)
        outs.append(v * rs * nw[:, gs])
    return outs, xhs, rss


def _l1_out(y, z, nw, w_out, x1, g, b, target):
    t = L1_T

    def body(y_ref, z_ref, nw_ref, w_ref, x1_ref, g_ref, b_ref, tg_ref, yn_ref, dv_ref, dgb_ref, loss_ref):
        @pl.when(pl.program_id(0) == 0)
        def _():
            dgb_ref[...] = jnp.zeros_like(dgb_ref)
            loss_ref[...] = jnp.zeros_like(loss_ref)

        outs, _, _ = _gated_norm(y_ref[...], z_ref[...], nw_ref[...])
        yn = _mx(jnp.concatenate(outs, axis=1))
        yn_ref[...] = yn
        v = DN_ALPHA * x1_ref[...] + _dot(yn, w_ref[...])
        gv = g_ref[...]
        err = _ln(v, gv, b_ref[...]) - tg_ref[...]
        rowsum = jnp.sum(err * err, axis=1, keepdims=True)
        loss_ref[...] += 0.5 * jnp.sum(rowsum, axis=0, keepdims=True) / D_MODEL
        dv, dg, db = _ln_bwd(v, gv, err / D_MODEL)
        dv_ref[...] = dv
        dgb_ref[0:1, :] += dg
        dgb_ref[1:2, :] += db

    wide = pl.BlockSpec((t, SSD_INNER), lambda i: (i, 0))
    full = pl.BlockSpec((t, D_MODEL), lambda i: (i, 0))
    vec = pl.BlockSpec((1, D_MODEL), lambda i: (0, 0))
    return pl.pallas_call(
        body, name="l1_out", grid=(SEQ // t,),
        in_specs=[wide, wide, pl.BlockSpec((1, SSD_INNER), lambda i: (0, 0)),
                  pl.BlockSpec((SSD_INNER, D_MODEL), lambda i: (0, 0)), full, vec, vec, full],
        out_specs=[wide, full, pl.BlockSpec((SUBLANES, D_MODEL), lambda i: (0, 0)),
                   pl.BlockSpec((SUBLANES, LANES), lambda i: (0, 0))],
        out_shape=[jax.ShapeDtypeStruct((SEQ, SSD_INNER), MXU_DTYPE), jax.ShapeDtypeStruct((SEQ, D_MODEL), F32),
                   jax.ShapeDtypeStruct((SUBLANES, D_MODEL), F32), jax.ShapeDtypeStruct((SUBLANES, LANES), F32)],
        compiler_params=_params(("arbitrary",)),
    )(y, z, nw, w_out, x1, g, b, target)


def _l1_gate_bwd(dv1, w_out, y, z, nw):
    t = L1_T
    gw = SSD_INNER // SSD_GROUPS

    def body(dv_ref, w_ref, y_ref, z_ref, nw_ref, dy_ref, dz_ref, dnw_ref):
        @pl.when(pl.program_id(0) == 0)
        def _():
            dnw_ref[...] = jnp.zeros_like(dnw_ref)

        dyn = _dot_nt(_mx(dv_ref[...]), w_ref[...])
        yv, zv, nwv = y_ref[...], z_ref[...], nw_ref[...]
        _, xhs, rss = _gated_norm(yv, zv, nwv)
        sz, dsz = _silu(zv), _dsilu(zv)
        for g in range(SSD_GROUPS):
            gs = slice(g * gw, (g + 1) * gw)
            d_out = dyn[:, gs]
            xh = xhs[g]
            dnw_ref[0:1, gs] += jnp.sum(d_out * xh, axis=0, keepdims=True)
            dxh = d_out * nwv[:, gs]
            dy2 = rss[g] * (dxh - xh * jnp.mean(dxh * xh, axis=-1, keepdims=True))
            dy_ref[:, gs] = dy2 * sz[:, gs]
            dz_ref[:, gs] = _mx(dy2 * yv[:, gs] * dsz[:, gs])

    wide = pl.BlockSpec((t, SSD_INNER), lambda i: (i, 0))
    return pl.pallas_call(
        body, name="l1_gate_bwd", grid=(SEQ // t,),
        in_specs=[pl.BlockSpec((t, D_MODEL), lambda i: (i, 0)), pl.BlockSpec((SSD_INNER, D_MODEL), lambda i: (0, 0)),
                  wide, wide, pl.BlockSpec((1, SSD_INNER), lambda i: (0, 0))],
        out_specs=[wide, wide, pl.BlockSpec((SUBLANES, SSD_INNER), lambda i: (0, 0))],
        out_shape=[jax.ShapeDtypeStruct((SEQ, SSD_INNER), F32), jax.ShapeDtypeStruct((SEQ, SSD_INNER), MXU_DTYPE),
                   jax.ShapeDtypeStruct((SUBLANES, SSD_INNER), F32)],
        compiler_params=_params(("arbitrary",)),
    )(dv1, w_out, y, z, nw)


MESH = pl.DeviceIdType.MESH
ANY = pl.BlockSpec(memory_space=pl.ANY)


def _flip(v, bit):
    return 1 - v if bit else v


def _all_gather(blocks, name):
    n = len(blocks)

    def body(*refs):
        x_refs, out_refs = refs[:n], refs[n:2 * n]
        send_sems, recv_sems, local_sems = refs[2 * n:]
        mx, my, mc = lax.axis_index("x"), lax.axis_index("y"), lax.axis_index("c")
        me, sibling = (mx, my, mc), (mx, my, 1 - mc)
        chips = [(1 - mx, my), (mx, 1 - my), (1 - mx, 1 - my)]

        def copy(a, k, block, to, own=False):
            px, py, pc = block
            slot = out_refs[a].at[4 * px + 2 * py + pc]
            return pltpu.make_async_remote_copy(
                src_ref=x_refs[a] if own else slot, dst_ref=slot,
                send_sem=send_sems.at[7 * a + k], recv_sem=recv_sems.at[7 * a + k], device_id=to, device_id_type=MESH)

        mine = [pltpu.make_async_copy(x_refs[a], out_refs[a].at[4 * mx + 2 * my + mc], local_sems.at[a])
                for a in range(n)]
        first = []
        for a in range(n):
            mine[a].start()
            first.append(copy(a, 0, me, sibling, own=True))
            first += [copy(a, 1 + j, me, (*chip, mc), own=True) for j, chip in enumerate(chips)]
        for cp in first:
            cp.start()
        passed = []
        for j, chip in enumerate(chips):
            for a in range(n):
                copy(a, 1 + j, (*chip, mc), me).wait_recv()
                fwd = copy(a, 4 + j, (*chip, mc), sibling)
                fwd.start()
                passed.append(fwd)
        for a in range(n):
            copy(a, 0, sibling, me).wait_recv()
            for j, chip in enumerate(chips):
                copy(a, 4 + j, (*chip, 1 - mc), me).wait_recv()
        for cp in first + passed:
            cp.wait_send()
        for cp in mine:
            cp.wait()

    return pl.pallas_call(
        body, name=name, in_specs=[ANY] * n, out_specs=[ANY] * n,
        out_shape=[jax.ShapeDtypeStruct((N_DEV,) + b.shape, b.dtype) for b in blocks],
        scratch_shapes=[pltpu.SemaphoreType.DMA((7 * n,)), pltpu.SemaphoreType.DMA((7 * n,)),
                        pltpu.SemaphoreType.DMA((n,))],
    )(*blocks)


def _l0_dx(dxr, dgate, dtail, w0p, dv0, scatter=(), bcast=()):
    arrays, ranges = _scatter_args(scatter)
    n = len(arrays) + len(bcast)
    tm = 1024
    steps = SEQ // tm

    def body(dxr_ref, dg_ref, dt_ref, w_ref, dv_ref, *rest):
        o_ref = rest[n]
        i = pl.program_id(0)
        if n:
            copies = _peer_copies(rest[:n], rest[n + 1:2 * n + 1], rest[2 * n + 1:], ranges)

            @pl.when(i == 0)
            def _():
                for cp in copies:
                    cp.start()

        o_ref[...] = (DN_ALPHA * dv_ref[...] + _dot_nt(dg_ref[...], w_ref[:, 0:1024])
                      + _dot_nt(dxr_ref[...], w_ref[:, 1024:1536]) + _dot_nt(dt_ref[...], w_ref[:, 1536:2048]))

        if n:
            @pl.when(i == steps - 1)
            def _():
                for cp in copies:
                    cp.wait()

    rows = lambda w: pl.BlockSpec((tm, w), lambda i: (i, 0))
    res = pl.pallas_call(
        body, name="l0_dx", grid=(steps,),
        in_specs=[rows(512), rows(1024), rows(512), pl.BlockSpec((D_MODEL, 2048), lambda i: (0, 0)), rows(D_MODEL)]
        + [ANY] * n,
        out_specs=[rows(D_MODEL)] + [ANY] * n,
        out_shape=[jax.ShapeDtypeStruct((SEQ, D_MODEL), F32)] + _exchange_shapes(arrays, bcast),
        scratch_shapes=_exchange_sems(n) if n else [],
        compiler_params=_params(("arbitrary",)),
    )(dxr, dgate, dtail, w0p, dv0, *arrays, *bcast)
    return res[0], res[1:]


def _scatter_args(scatter):
    arrays = [s[0] if isinstance(s, tuple) else s for s in scatter]
    ranges = [s[1] if isinstance(s, tuple) else (0, N_DEV) for s in scatter]
    return arrays, ranges


def _exchange_shapes(scatter, bcast):
    return ([jax.ShapeDtypeStruct((N_DEV,) + a.shape[1:], a.dtype) for a in scatter]
            + [jax.ShapeDtypeStruct((N_DEV,) + a.shape, a.dtype) for a in bcast])


def _exchange_sems(n):
    return [pltpu.SemaphoreType.DMA((7 * n,)), pltpu.SemaphoreType.DMA((7 * n,)), pltpu.SemaphoreType.DMA((n,))]


class _GuardedCopy:
    def __init__(self, copy, send=None, recv=None, local=False):
        self.copy, self.send, self.recv, self.local = copy, send, recv, local

    @staticmethod
    def _run(pred, fn):
        if pred is None:
            fn()
        else:
            pl.when(pred)(fn)

    def start(self):
        self._run(self.send, self.copy.start)

    def wait(self):
        if self.local:
            self._run(self.send, self.copy.wait)
        else:
            self._run(self.send, self.copy.wait_send)
            self._run(self.recv, self.copy.wait_recv)


def _peer_copies(in_refs, out_refs, sems, ranges):
    send_sems, recv_sems, local_sems = sems
    n, ns = len(in_refs), len(ranges)
    mx, my, mc = lax.axis_index("x"), lax.axis_index("y"), lax.axis_index("c")
    me = 4 * mx + 2 * my + mc

    def src(a, slot):
        return in_refs[a].at[slot - ranges[a][0]] if a < ns else in_refs[a]

    def member(a, dev):
        if a >= ns or ranges[a] == (0, N_DEV):
            return None
        return (dev >= ranges[a][0]) & (dev < ranges[a][1])

    copies = [_GuardedCopy(pltpu.make_async_copy(src(a, me), out_refs[a].at[me], local_sems.at[a]),
                           send=member(a, me), local=True) for a in range(n)]
    for k in range(1, N_DEV):
        px, py, pc = _flip(mx, (k >> 2) & 1), _flip(my, (k >> 1) & 1), _flip(mc, k & 1)
        peer = 4 * px + 2 * py + pc
        for a in range(n):
            copies.append(_GuardedCopy(pltpu.make_async_remote_copy(
                src_ref=src(a, peer), dst_ref=out_refs[a].at[me],
                send_sem=send_sems.at[7 * a + k - 1], recv_sem=recv_sems.at[7 * a + k - 1],
                device_id=(px, py, pc), device_id_type=MESH), send=member(a, peer), recv=member(a, me)))
    return copies


def _segments(col_map, width):
    segs = []
    for lo, hi, arr, alo in col_map:
        for s in range(N_DEV):
            a, b = max(lo, s * width), min(hi, (s + 1) * width)
            if a < b:
                segs.append((s, a - s * width, b - a, arr, alo + a - lo))
    return segs


COPY_ROWS = 256


def _unshard(g8, col_map, widths, name):
    _, r, w = g8.shape
    rb = min(r, COPY_ROWS)
    segs = _segments(col_map, w)

    def body(g_ref, *o_refs):
        for o_ref in o_refs:
            o_ref[...] = jnp.zeros_like(o_ref)
        for s, llo, n, arr, alo in segs:
            o_refs[arr][:, alo:alo + n] = g_ref[s, :, llo:llo + n]

    return pl.pallas_call(
        body, name=name, grid=(r // rb,),
        in_specs=[pl.BlockSpec((N_DEV, rb, w), lambda i: (0, i, 0))],
        out_specs=[pl.BlockSpec((rb, n), lambda i: (i, 0)) for n in widths],
        out_shape=[jax.ShapeDtypeStruct((r, n), g8.dtype) for n in widths],
        compiler_params=_params(("parallel",)),
    )(g8)


def _reshard(srcs, col_map, w, dtype, name, shards=(0, N_DEV)):
    r = srcs[0].shape[0]
    rb = min(r, COPY_ROWS)
    lo, hi = shards
    segs = [sg for sg in _segments(col_map, w) if lo <= sg[0] < hi]

    def body(*refs):
        o_ref = refs[-1]
        for s, llo, n, arr, alo in segs:
            o_ref[s - lo, :, llo:llo + n] = refs[arr][:, alo:alo + n].astype(dtype)

    return pl.pallas_call(
        body, name=name, grid=(r // rb,),
        in_specs=[pl.BlockSpec((rb, a.shape[1]), lambda i: (i, 0)) for a in srcs],
        out_specs=pl.BlockSpec((hi - lo, rb, w), lambda i: (0, i, 0)),
        out_shape=jax.ShapeDtypeStruct((hi - lo, r, w), dtype),
        compiler_params=_params(("parallel",)),
    )(*srcs)


def _adamw(parts, w, m, v, name):
    r, c = w.shape
    tr = COPY_ROWS if r % COPY_ROWS == 0 else r

    def body(p_ref, w_ref, m_ref, v_ref, g_ref, d_ref, mo_ref, vo_ref):
        g = p_ref[0].astype(F32)
        for s in range(1, N_DEV):
            g = g + p_ref[s].astype(F32)
        g_ref[...] = g
        d_ref[...], mo_ref[...], vo_ref[...] = _adamw_math(g, w_ref[...], m_ref[...], v_ref[...])

    blk = pl.BlockSpec((tr, c), lambda i: (i, 0))
    out = jax.ShapeDtypeStruct((r, c), F32)
    return pl.pallas_call(
        body, name=name, grid=(r // tr,),
        in_specs=[pl.BlockSpec((N_DEV, tr, c), lambda i: (0, i, 0)), blk, blk, blk],
        out_specs=[blk, blk, blk, blk], out_shape=[out, out, out, out],
        compiler_params=_params(("parallel",)),
    )(parts, w, m, v)


def _adamw_math(g, w, m, v):
    mn = ADAM_B1 * m + (1.0 - ADAM_B1) * g
    vn = ADAM_B2 * v + (1.0 - ADAM_B2) * (g * g)
    m_hat = mn / (1.0 - ADAM_B1 ** ADAM_STEP)
    v_hat = vn / (1.0 - ADAM_B2 ** ADAM_STEP)
    return -ADAM_LR * (m_hat / (jnp.sqrt(v_hat) + ADAM_EPS) + ADAM_WD * w), mn, vn


SMALL = (("ab_conv_w", 0, 4, 64), ("ssd_conv_w", 4, 4, 384), ("ssd_conv_b", 8, 1, 384), ("ssd_norm", 9, 1, 256),
         ("ssd_ln_g", 10, 1, 128), ("ssd_ln_b", 11, 1, 128))
VECS = (("ab_conv_b", 512), ("ab_gate_a_b", 512), ("ab_gate_x_b", 512), ("ab_lambda", 512), ("mla_q_norm", 256),
        ("mla_kv_norm", 128), ("ab_ln_g", 1024), ("ab_ln_b", 1024), ("ssd_dt_bias", 32), ("ssd_a_log", 32),
        ("ssd_d", 32))
GATES = ("ab_gate_a_w", "ab_gate_x_w")
SMALL_NAMES = tuple(n for n, *_ in SMALL) + tuple(n for n, _ in VECS) + GATES
VMEM_WHOLE = pl.BlockSpec(memory_space=pltpu.VMEM)


def _view2d(name, a):
    if name in GATES:
        return a.reshape(RNN_W, 64)
    return a[0] if a.ndim == 3 else a


def _unshard_small(g):
    widths = (512, 3072, 3072, 2048, 1024, 1024)

    def body(*refs):
        ins, outs = refs[:6], refs[6:]
        outs[0][...] = jnp.zeros_like(outs[0])
        outs[1][...] = jnp.zeros_like(outs[1])
        for (_, _, nr, c), i_ref, o_ref in zip(SMALL, ins, outs):
            for j in range(N_DEV):
                o_ref[0:nr, j * c:(j + 1) * c] = i_ref[j]

    return pl.pallas_call(
        body, name="unshard_small", in_specs=[VMEM_WHOLE] * 6, out_specs=[VMEM_WHOLE] * 6,
        out_shape=[jax.ShapeDtypeStruct((SUBLANES if nr == 4 else 1, w), F32) for (_, _, nr, _), w in zip(SMALL, widths)],
    )(*g)


def _prep_repl(ga, gx, dt_bias, a_log, d):
    def body(ga_ref, gx_ref, b_ref, al_ref, d_ref, wa_ref, wx_ref, b128_ref, al128_ref, dx_ref):
        wa_ref[...] = jnp.zeros_like(wa_ref)
        wx_ref[...] = jnp.zeros_like(wx_ref)
        for hd in range(8):
            hs = slice(hd * 64, (hd + 1) * 64)
            wa_ref[hs, hs] = _mx(ga_ref[hs, :])
            wx_ref[hs, hs] = _mx(gx_ref[hs, :])
        b128_ref[...] = jnp.zeros_like(b128_ref)
        al128_ref[...] = jnp.zeros_like(al128_ref)
        b128_ref[:, 0:SSD_HEADS] = b_ref[...]
        al128_ref[:, 0:SSD_HEADS] = al_ref[...]
        dv = d_ref[...]
        for hd in range(SSD_HEADS):
            dx_ref[:, hd * SSD_P:(hd + 1) * SSD_P] = jnp.broadcast_to(dv[:, hd:hd + 1], (1, SSD_P))

    return pl.pallas_call(
        body, name="prep_repl", in_specs=[VMEM_WHOLE] * 5, out_specs=[VMEM_WHOLE] * 5,
        out_shape=[jax.ShapeDtypeStruct((RNN_W, RNN_W), MXU_DTYPE), jax.ShapeDtypeStruct((RNN_W, RNN_W), MXU_DTYPE),
                   jax.ShapeDtypeStruct((1, LANES), F32), jax.ShapeDtypeStruct((1, LANES), F32),
                   jax.ShapeDtypeStruct((1, SSD_INNER), F32)],
    )(ga, gx, dt_bias, a_log, d)


def _pack_small(dvec0, g_wa, g_wx, dqnw, dknw, dgb0, dvec1, dcw1, dnw, dgb1):
    def body(dvec0_ref, gwa_ref, gwx_ref, dqn_ref, dkn_ref, dgb0_ref, dvec1_ref, dcw1_ref, dnw_ref, dgb1_ref,
             sm_ref, vec_ref, gg_ref):
        sm_ref[...] = jnp.zeros_like(sm_ref)
        vec_ref[...] = jnp.zeros_like(vec_ref)
        sharded = ((dvec0_ref, 4), (dcw1_ref, 0), (dcw1_ref, 4), (dnw_ref, 0), (dgb1_ref, 0), (dgb1_ref, 1))
        for (_, r0, nr, c), (src, sr) in zip(SMALL, sharded):
            for j in range(N_DEV):
                sm_ref[j, r0:r0 + nr, 0:c] = src[sr:sr + nr, j * c:(j + 1) * c]
        vectors = ((dvec0_ref, 3), (dvec0_ref, 0), (dvec0_ref, 1), (dvec0_ref, 2), (dqn_ref, 0), (dkn_ref, 0),
                   (dgb0_ref, 0), (dgb0_ref, 1), (dvec1_ref, 0), (dvec1_ref, 1), (dvec1_ref, 2))
        for row, ((_, c), (src, sr)) in enumerate(zip(VECS, vectors)):
            vec_ref[row:row + 1, 0:c] = src[sr:sr + 1, 0:c]
        for hd in range(8):
            hs = slice(hd * 64, (hd + 1) * 64)
            gg_ref[hs, 0:64] = _mx(gwa_ref[hs, hs])
            gg_ref[hs, 64:128] = _mx(gwx_ref[hs, hs])

    return pl.pallas_call(
        body, name="pack_small", in_specs=[VMEM_WHOLE] * 10, out_specs=[VMEM_WHOLE] * 3,
        out_shape=[jax.ShapeDtypeStruct((N_DEV, 16, 384), F32), jax.ShapeDtypeStruct((16, 1024), F32),
                   jax.ShapeDtypeStruct((RNN_W, LANES), MXU_DTYPE)],
    )(dvec0, g_wa, g_wx, dqnw, dknw, dgb0, dvec1, dcw1, dnw, dgb1)


def _adamw_small(recv_sm, recv_vec, recv_gg, wmv):
    plan = ([(0, r0, nr, c) for _, r0, nr, c in SMALL] + [(1, row, 1, c) for row, (_, c) in enumerate(VECS)]
            + [(2, 0, RNN_W, 0), (2, 0, RNN_W, 64)])
    n = len(plan)

    def body(*refs):
        recv, ins, outs = refs[:3], refs[3:3 + 3 * n], refs[3 + 3 * n:]
        for i, (src, r0, nr, c) in enumerate(plan):
            cols = slice(c, c + 64) if src == 2 else slice(0, c)
            g = recv[src][0, r0:r0 + nr, cols].astype(F32)
            for s in range(1, N_DEV):
                g = g + recv[src][s, r0:r0 + nr, cols].astype(F32)
            w_ref, m_ref, v_ref = ins[3 * i:3 * i + 3]
            outs[4 * i][...] = g
            outs[4 * i + 1][...], outs[4 * i + 2][...], outs[4 * i + 3][...] = _adamw_math(
                g, w_ref[...], m_ref[...], v_ref[...])

    flat = [a for t in wmv for a in t]
    return pl.pallas_call(
        body, name="adamw_small", in_specs=[VMEM_WHOLE] * (3 + 3 * n), out_specs=[VMEM_WHOLE] * (4 * n),
        out_shape=[jax.ShapeDtypeStruct(t[0].shape, F32) for t in wmv for _ in range(4)],
    )(recv_sm, recv_vec, recv_gg, *flat)


BIG_L0 = ("ab_w_in", "ab_w_out", "mla_w_uq", "mla_w_ukv")
BIG_L1 = ("ssd_w_in", "ssd_w_out")

MAP_W0 = ((0, 512, 0, 1024), (512, 1536, 0, 0), (1536, 1920, 0, 1536), (1920, 1952, 0, 1984))
MAP_W1 = ((0, 2048, 0, 0), (2048, 5120, 1, 0), (5120, 5152, 2, 0))
MAP_WQ = tuple((96 * hd, 96 * hd + 96, 0, 128 * hd) for hd in range(8))
MAP_WKV = (tuple((128 * hd, 128 * hd + 64, 0, 128 * hd) for hd in range(8))
           + tuple((128 * hd + 64, 128 * hd + 128, 0, 1024 + 64 * hd) for hd in range(8)))
MAP_G0 = ((0, 512, 0, 0), (512, 1536, 1, 0), (1536, 1920, 2, 0), (1920, 1952, 2, 448))
W0_EARLY, W0_LATE = (0, 6), (6, 8)


def kernel(x, positions, ab_w_in, ab_conv_w, ab_conv_b, ab_gate_a_w, ab_gate_a_b, ab_gate_x_w, ab_gate_x_b, ab_lambda, mla_q_norm, mla_kv_norm, mla_w_uq, mla_w_ukv, ab_w_out, ab_ln_g, ab_ln_b, ssd_w_in, ssd_conv_w, ssd_conv_b, ssd_dt_bias, ssd_a_log, ssd_d, ssd_norm, ssd_w_out, ssd_ln_g, ssd_ln_b, loss_target, m_ab_w_in, m_ab_conv_w, m_ab_conv_b, m_ab_gate_a_w, m_ab_gate_a_b, m_ab_gate_x_w, m_ab_gate_x_b, m_ab_lambda, m_mla_q_norm, m_mla_kv_norm, m_mla_w_uq, m_mla_w_ukv, m_ab_w_out, m_ab_ln_g, m_ab_ln_b, m_ssd_w_in, m_ssd_conv_w, m_ssd_conv_b, m_ssd_dt_bias, m_ssd_a_log, m_ssd_d, m_ssd_norm, m_ssd_w_out, m_ssd_ln_g, m_ssd_ln_b, v_ab_w_in, v_ab_conv_w, v_ab_conv_b, v_ab_gate_a_w, v_ab_gate_a_b, v_ab_gate_x_w, v_ab_gate_x_b, v_ab_lambda, v_mla_q_norm, v_mla_kv_norm, v_mla_w_uq, v_mla_w_ukv, v_ab_w_out, v_ab_ln_g, v_ab_ln_b, v_ssd_w_in, v_ssd_conv_w, v_ssd_conv_b, v_ssd_dt_bias, v_ssd_a_log, v_ssd_d, v_ssd_norm, v_ssd_w_out, v_ssd_ln_g, v_ssd_ln_b):
    args = dict(locals())
    bf = MXU_DTYPE
    big = {n: [args[pre + n][0] for pre in ("", "m_", "v_")] for n in BIG_L0 + BIG_L1}
    sml = {n: [_view2d(n, args[pre + n]) for pre in ("", "m_", "v_")] for n in SMALL_NAMES}

    gathered = _all_gather([big[n][0].astype(bf) for n in BIG_L0] + [sml[n][0] for n, *_ in SMALL], "gather_params")
    g8 = dict(zip(BIG_L0, gathered))
    p = {"wo0": g8["ab_w_out"].reshape(D_MODEL, D_MODEL)}
    p["w0p"], = _unshard(g8["ab_w_in"], MAP_W0, (2048,), "unshard_w0")
    p["wq"], = _unshard(g8["mla_w_uq"], MAP_WQ, (1024,), "unshard_wq")
    p["wkv"], = _unshard(g8["mla_w_ukv"], MAP_WKV, (1536,), "unshard_wkv")
    p["cw0"], p["cw1"], p["cb1"], p["nw"], p["g1"], p["b1"] = _unshard_small(gathered[len(BIG_L0):])
    p["wa"], p["wx"], p["dt_bias"], p["a_log"], p["d_x"] = _prep_repl(
        sml["ab_gate_a_w"][0], sml["ab_gate_x_w"][0], sml["ssd_dt_bias"][0], sml["ssd_a_log"][0], sml["ssd_d"][0])
    for key, n in (("cb0", "ab_conv_b"), ("ba", "ab_gate_a_b"), ("bx", "ab_gate_x_b"), ("lam", "ab_lambda"),
                   ("qn_w", "mla_q_norm"), ("kn_w", "mla_kv_norm"), ("g0", "ab_ln_g"), ("b0", "ab_ln_b")):
        p[key] = sml[n][0]

    _, recv_early, recv, loss_part, grad_x = _local_step(
        x[0], positions[0], loss_target[0], p, [big[n][0].astype(bf) for n in BIG_L1])

    me = 4 * lax.axis_index("x") + 2 * lax.axis_index("y") + lax.axis_index("c")
    parts = {"ssd_w_in": recv_early[0], "ssd_w_out": recv_early[1], "ab_w_out": recv_early[2],
             "ab_w_in": jnp.where(me >= W0_LATE[0], recv[0], recv_early[3]), "mla_w_uq": recv[1], "mla_w_ukv": recv[2]}

    outs = {}
    kinds = ("grad", "delta", "new_m", "new_v")
    for n in BIG_L0 + BIG_L1:
        for kind, res in zip(kinds, _adamw(parts[n], *big[n], "adamw_" + n)):
            outs[kind, n] = res[None]
    res = _adamw_small(*recv[3:], [sml[n] for n in SMALL_NAMES])
    for i, n in enumerate(SMALL_NAMES):
        for k, kind in enumerate(kinds):
            outs[kind, n] = res[4 * i + k].reshape(args[n].shape)

    loss = lax.psum(loss_part, ("x", "y", "c"))
    order = ["ab_w_in", "ab_conv_w", "ab_conv_b", "ab_gate_a_w", "ab_gate_a_b", "ab_gate_x_w", "ab_gate_x_b",
             "ab_lambda", "mla_q_norm", "mla_kv_norm", "mla_w_uq", "mla_w_ukv", "ab_w_out", "ab_ln_g", "ab_ln_b",
             "ssd_w_in", "ssd_conv_w", "ssd_conv_b", "ssd_dt_bias", "ssd_a_log", "ssd_d", "ssd_norm", "ssd_w_out",
             "ssd_ln_g", "ssd_ln_b"]
    return (loss, grad_x[None], *[outs[kind, n] for kind in ("grad", "delta", "new_m", "new_v") for n in order])


def _local_step(x, pos, target, p, l1_blocks):
    bf = MXU_DTYPE
    inv_freq = 10000.0 ** (-jnp.arange(0, 32, 2, dtype=F32) / 32)
    ang = pos.astype(F32)[:, None] * inv_freq
    cos, sin = jnp.cos(ang), jnp.sin(ang)
    zeros = lambda n: jnp.zeros((SEQ, n), F32)
    tc = jnp.concatenate([jnp.ones((SEQ, 64), F32), cos, cos, zeros(32)], axis=1)
    tsa = jnp.concatenate([zeros(64), -sin, zeros(48)], axis=1)
    tsb = jnp.concatenate([zeros(80), sin, zeros(32)], axis=1)

    w0p, wq, wkv, wo0, wa, wxg = (p[k] for k in ("w0p", "wq", "wkv", "wo0", "wa", "wx"))
    cw0, cb0, ba, bx, lam = (p[k] for k in ("cw0", "cb0", "ba", "bx", "lam"))
    qn_w, kn_w, g0, b0 = (p[k] for k in ("qn_w", "kn_w", "g0", "b0"))
    cw1, cb1, dt_bias, a_log, d_x, nw, g1, b1 = (p[k] for k in ("cw1", "cb1", "dt_bias", "a_log", "d_x", "nw", "g1", "b1"))
    tril = jnp.tril(jnp.ones((SSD_L, SSD_L), F32))
    expand_t = (jnp.arange(SSD_INNER)[:, None] // SSD_P == jnp.arange(LANES)[None, :]).astype(jnp.bfloat16)

    xb = x.astype(bf)
    proj0 = _mm(xb, w0p, "nn", name="l0_in")
    xc, h = _rglru_fwd(proj0, cw0, cb0, wa, ba, wxg, bx, lam)
    qn, kn, qc, kc, vc = _mla_fwd(proj0, qn_w, kn_w, wq, wkv, tc, tsa, tsb)
    o, lse, (w1_8, wo1_8) = _flash_fwd(qc, kc, vc, bcast=l1_blocks)
    w1z, w1x, w1d = _unshard(w1_8, MAP_W1, (2048, 3072, 128), "unshard_w1")
    wo1 = wo1_8.reshape(SSD_INNER, D_MODEL)
    y0, v0, x1, x1b = _l0_out(h, o, proj0, x, wo0, g0, b0)

    z = _mm(x1b, w1z, "nn", name="l1_in_z")
    xbc = _mm(x1b, w1x, "nn", name="l1_in_xbc")
    dt_raw = _mm(x1b, w1d, "nn", name="l1_in_dt")
    pre, act = _ssd_conv_fwd(xbc, cw1, cb1)
    ys, hprev = _ssd_scan_fwd(act, dt_raw, dt_bias, a_log, d_x, tril, expand_t)
    yn, dv1, dgb1, loss8 = _l1_out(ys, z, nw, wo1, x1, g1, b1, target)

    g_wo1 = _mm(yn, dv1, "tn", name="l1_dwout")
    dys, dz, dnw = _l1_gate_bwd(dv1, wo1, ys, z, nw)
    dact, ddt_raw, dvec1 = _ssd_scan_bwd(dys, act, dt_raw, hprev, dt_bias, a_log, d_x, tril, expand_t)
    dxbc, dcw1 = _ssd_conv_bwd(dact, pre, xbc, cw1)
    g_z, g_xbc = _mm(x1b, dz, "tn", name="l1_dw_z"), _mm(x1b, dxbc, "tn", name="l1_dw_xbc")
    g_dt = _mm(x1b, ddt_raw, "tn", name="l1_dw_dt")
    dx1 = _mm(dz, w1z, "nt", name="l1_dx_z", add=dv1, add_scale=DN_ALPHA)
    dx1 = _mm(dxbc, w1x, "nt", name="l1_dx_xbc", add=dx1)

    dv0, dgb0 = _ln_bwd_call(v0, dx1, ddt_raw, w1d, g0)
    g_wo0 = _mm(y0, dv0, "tn", name="l0_dwout")
    dh, do, dgate = _gate_bwd(dv0, wo0, h, o, proj0)
    dxr, g_wa, g_wx, dvec0 = _rglru_bwd(dh, xc, h, proj0, cw0, wa, ba, wxg, bx, lam)
    g_rnn, g_gate = _mm(xb, dxr, "tn", name="l0_dw_rnn"), _mm(xb, dgate, "tn", name="l0_dw_gate")
    early = [_reshard([g_z, g_xbc, g_dt], MAP_W1, 644, bf, "reshard_w1"), g_wo1.astype(bf).reshape(N_DEV, 256, D_MODEL),
             g_wo0.astype(bf).reshape(N_DEV, 128, D_MODEL),
             (_reshard([g_rnn, g_gate], MAP_G0, 244, bf, "reshard_w0_early", shards=W0_EARLY), W0_EARLY)]
    dq, dk, dvv, recv_early = _flash_bwd(qc, kc, vc, o, do, lse, scatter=early)
    dtail, g_wq, g_wkv, dqnw, dknw = _mla_bwd(dq, dk, dvv, proj0, qn, kn, qn_w, kn_w, wq, wkv, tc, tsa, tsb)
    g_tail = _mm(xb, dtail, "tn", name="l0_dw_tail")

    acc = {"g_rnn": g_rnn, "g_gate": g_gate, "g_tail": g_tail, "g_wq": g_wq, "g_wkv": g_wkv,
           "dvec0": dvec0, "g_wa": g_wa, "g_wx": g_wx, "dqnw": dqnw, "dknw": dknw, "dgb0": dgb0, "dvec1": dvec1,
           "dcw1": dcw1, "dnw": dnw, "dgb1": dgb1}
    late = [(_reshard([g_rnn, g_gate, g_tail], MAP_G0, 244, bf, "reshard_w0_late", shards=W0_LATE), W0_LATE),
            _reshard([g_wq], MAP_WQ, 96, bf, "reshard_wq"), _reshard([g_wkv], MAP_WKV, 128, bf, "reshard_wkv")]
    sm_slots, vec_rows, gates = _pack_small(dvec0, g_wa, g_wx, dqnw, dknw, dgb0, dvec1, dcw1, dnw, dgb1)
    dx, recv_late = _l0_dx(dxr, dgate, dtail, w0p, dv0, scatter=late + [sm_slots], bcast=[vec_rows, gates])
    return acc, recv_early, recv_late, loss8[0, 0], dx
```

```python
import math

import jax
import jax.numpy as jnp
from jax import lax
from jax.experimental import pallas as pl
from jax.experimental.pallas import tpu as pltpu

F32 = jnp.float32
MXU_DTYPE = jnp.bfloat16

N_DEV = 8
SEQ = 4096
D_MODEL = 1024
DN_ALPHA = 4.0 ** 0.25
RNN_W = 512
MLA_HEADS = 8
ATT_SCALE = 96.0 ** -0.5
ATT_C = ATT_SCALE * math.log2(math.e)
RG_C = 8.0
SSD_INNER = 2048
SSD_HEADS = 32
SSD_P = 64
SSD_GROUPS = 4
SSD_N = 128
SSD_L = 128
SSD_CONV = 3072
LANES = 128
SUBLANES = 8
VMEM_LIMIT = 56 * 1024 * 1024

ADAM_LR, ADAM_B1, ADAM_B2, ADAM_EPS, ADAM_WD, ADAM_STEP = 0.001, 0.9, 0.999, 1e-08, 0.01, 10

HIGHEST = lax.Precision.HIGHEST


def _params(sem, limit=VMEM_LIMIT):
    return pltpu.CompilerParams(dimension_semantics=sem, vmem_limit_bytes=limit)


def _dot(a, b):
    return lax.dot_general(a, b, (((1,), (0,)), ((), ())), preferred_element_type=F32)


def _dot_nt(a, b):
    return lax.dot_general(a, b, (((1,), (1,)), ((), ())), preferred_element_type=F32)


def _dot_tn(a, b):
    return lax.dot_general(a, b, (((0,), (0,)), ((), ())), preferred_element_type=F32)


def _dot_hi(a, b):
    return lax.dot_general(a, b, (((1,), (0,)), ((), ())), precision=HIGHEST, preferred_element_type=F32)


def _mx(v):
    return v.astype(MXU_DTYPE)


def _sigmoid(v):
    return 1.0 / (1.0 + jnp.exp(-v))


def _log1p_pos(e):
    poly = e * (1.0 - e * (0.5 - e * (1.0 / 3.0 - e * 0.25)))
    return jnp.where(e < 0.01, poly, jnp.log(1.0 + e))


def _softplus(v):
    return jnp.maximum(v, 0.0) + _log1p_pos(jnp.exp(-jnp.abs(v)))


def _neg_expm1(v):
    poly = -v * (1.0 + v * (0.5 + v * (1.0 / 6.0 + v * (1.0 / 24.0 + v * (1.0 / 120.0)))))
    return jnp.where(jnp.abs(v) < 0.1, poly, 1.0 - jnp.exp(v))


def _silu(v):
    return v * _sigmoid(v)


def _dsilu(v):
    s = _sigmoid(v)
    return s * (1.0 + v * (1.0 - s))


def _mm(a, b, mode, *, name, add=None, add_scale=1.0, out_dtype=F32, tm=None, tn=1024, tk=512):
    if mode == "tn":
        kdim, m = a.shape
        n = b.shape[1]
        tm, tn, tk = min(tm or 1024, m), min(tn, n), min(tk, kdim)

        def body_tn(a_ref, b_ref, o_ref):
            @pl.when(pl.program_id(2) == 0)
            def _():
                o_ref[...] = jnp.zeros_like(o_ref)

            o_ref[...] += _dot_tn(_mx(a_ref[...]), _mx(b_ref[...]))

        return pl.pallas_call(
            body_tn, name=name, grid=(m // tm, n // tn, kdim // tk),
            in_specs=[pl.BlockSpec((tk, tm), lambda i, j, k: (k, i)), pl.BlockSpec((tk, tn), lambda i, j, k: (k, j))],
            out_specs=pl.BlockSpec((tm, tn), lambda i, j, k: (i, j)),
            out_shape=jax.ShapeDtypeStruct((m, n), F32),
            compiler_params=_params(("parallel", "parallel", "arbitrary")),
        )(a, b)

    m, kdim = a.shape
    n = b.shape[1] if mode == "nn" else b.shape[0]
    tm, tn = min(tm or 1024, m), min(tn, n)
    has_add = add is not None

    def body(*refs):
        a_ref, b_ref = refs[0], refs[1]
        o_ref = refs[-1]
        av, bv = _mx(a_ref[...]), _mx(b_ref[...])
        acc = _dot(av, bv) if mode == "nn" else _dot_nt(av, bv)
        if has_add:
            acc = acc + add_scale * refs[2][...]
        o_ref[...] = acc.astype(out_dtype)

    b_spec = (pl.BlockSpec((kdim, tn), lambda i, j: (0, j)) if mode == "nn"
              else pl.BlockSpec((tn, kdim), lambda i, j: (j, 0)))
    in_specs = [pl.BlockSpec((tm, kdim), lambda i, j: (i, 0)), b_spec]
    args = [a, b]
    if has_add:
        in_specs.append(pl.BlockSpec((tm, tn), lambda i, j: (i, j)))
        args.append(add)
    return pl.pallas_call(
        body, name=name, grid=(m // tm, n // tn), in_specs=in_specs,
        out_specs=pl.BlockSpec((tm, tn), lambda i, j: (i, j)),
        out_shape=jax.ShapeDtypeStruct((m, n), out_dtype),
        compiler_params=_params(("parallel", "parallel")),
    )(*args)


def _shift_down(blk, halo, s):
    if s == 0:
        return blk
    t = blk.shape[0]
    r = pltpu.roll(blk, s, 0)
    hr = pltpu.roll(halo, s, 0)
    row8 = lax.broadcasted_iota(jnp.int32, hr.shape, 0)
    head = jnp.where(row8 < s, hr, r[:SUBLANES])
    return jnp.concatenate([head, r[SUBLANES:]], axis=0) if t > SUBLANES else head


def _shift_up(blk, halo, s):
    if s == 0:
        return blk
    t = blk.shape[0]
    r = pltpu.roll(blk, t - s, 0)
    hr = pltpu.roll(halo, SUBLANES - s, 0)
    row8 = lax.broadcasted_iota(jnp.int32, hr.shape, 0)
    tail = jnp.where(row8 >= SUBLANES - s, hr, r[t - SUBLANES:])
    return jnp.concatenate([r[:t - SUBLANES], tail], axis=0) if t > SUBLANES else tail


def _scan_down(a, u):
    t = a.shape[0]
    row = lax.broadcasted_iota(jnp.int32, a.shape, 0)
    d = 1
    while d < t:
        keep = row >= d
        a_sh = jnp.where(keep, pltpu.roll(a, d, 0), 1.0)
        u_sh = jnp.where(keep, pltpu.roll(u, d, 0), 0.0)
        u = a * u_sh + u
        a = a * a_sh
        d *= 2
    return a, u


def _scan_up(a, u):
    t = a.shape[0]
    row = lax.broadcasted_iota(jnp.int32, a.shape, 0)
    d = 1
    while d < t:
        keep = row < t - d
        a_sh = jnp.where(keep, pltpu.roll(a, t - d, 0), 1.0)
        u_sh = jnp.where(keep, pltpu.roll(u, t - d, 0), 0.0)
        u = a * u_sh + u
        a = a * a_sh
        d *= 2
    return a, u


def _conv4(blk, halo, cw, cb):
    out = cb + blk * cw[3:4]
    for k in range(3):
        out = out + _shift_down(blk, halo, 3 - k) * cw[k:k + 1]
    return out


RG_T = 512
P0_RNN = 2


def _rg_gates(xc, wa, ba, wx, bx, lam):
    xcb = _mx(xc)
    r = _sigmoid(_dot(xcb, wa) + ba)
    ig = _sigmoid(_dot(xcb, wx) + bx)
    sp = _softplus(-lam)
    la = (-RG_C * r) * sp
    a = jnp.exp(la)
    mult = jnp.sqrt(_neg_expm1(2.0 * la))
    return r, ig, sp, a, mult


def _rglru_fwd(proj0, cw8, cb, wa, ba, wx, bx, lam):
    t, w = RG_T, RNN_W
    nb = SEQ // t

    def body(x_ref, halo_ref, cw_ref, cb_ref, wa_ref, ba_ref, wx_ref, bx_ref, lam_ref, xc_ref, h_ref, carry):
        i = pl.program_id(0)

        @pl.when(i == 0)
        def _():
            carry[...] = jnp.zeros_like(carry)

        blk = x_ref[...]
        halo = jnp.where(i > 0, halo_ref[...], 0.0)
        xc = _conv4(blk, halo, cw_ref[...], cb_ref[...])
        _, ig, _, a, mult = _rg_gates(xc, wa_ref[...], ba_ref[...], wx_ref[...], bx_ref[...], lam_ref[...])
        u = mult * (ig * xc)
        big_a, big_u = _scan_down(a, u)
        h = big_a * carry[SUBLANES - 1:SUBLANES, :] + big_u
        carry[...] = h[t - SUBLANES:]
        xc_ref[...] = xc
        h_ref[...] = h

    vec = pl.BlockSpec((1, w), lambda i: (0, 0))
    mat = pl.BlockSpec((w, w), lambda i: (0, 0))
    return pl.pallas_call(
        body, name="rglru_fwd", grid=(nb,),
        in_specs=[pl.BlockSpec((t, w), lambda i: (i, P0_RNN)),
                  pl.BlockSpec((SUBLANES, w), lambda i: (jnp.maximum(i * (t // SUBLANES) - 1, 0), P0_RNN)),
                  pl.BlockSpec((SUBLANES, w), lambda i: (0, 0)), vec, mat, vec, mat, vec, vec],
        out_specs=[pl.BlockSpec((t, w), lambda i: (i, 0)), pl.BlockSpec((t, w), lambda i: (i, 0))],
        out_shape=[jax.ShapeDtypeStruct((SEQ, w), F32), jax.ShapeDtypeStruct((SEQ, w), F32)],
        scratch_shapes=[pltpu.VMEM((SUBLANES, w), F32)],
        compiler_params=_params(("arbitrary",)),
    )(proj0, proj0, cw8, cb, wa, ba, wx, bx, lam)


def _rglru_bwd(dh, xc, h, proj0, cw8, wa, ba, wx, bx, lam):
    t, w = RG_T, RNN_W
    nb = SEQ // t
    tb = t // SUBLANES

    def body(dh_ref, xc_ref, h_ref, hh_ref, x_ref, cw_ref, wa_ref, ba_ref, wx_ref, bx_ref, lam_ref,
             dx_ref, dwa_ref, dwx_ref, dvec_ref, gcarry, dxc_next):
        i = pl.program_id(0)
        rev = nb - 1 - i

        @pl.when(i == 0)
        def _():
            gcarry[...] = jnp.zeros_like(gcarry)
            dxc_next[...] = jnp.zeros_like(dxc_next)
            dwa_ref[...] = jnp.zeros_like(dwa_ref)
            dwx_ref[...] = jnp.zeros_like(dwx_ref)
            dvec_ref[...] = jnp.zeros_like(dvec_ref)

        xc = xc_ref[...]
        wa_v, wx_v = wa_ref[...], wx_ref[...]
        lam_v = lam_ref[...]
        r, ig, sp, a, mult = _rg_gates(xc, wa_v, ba_ref[...], wx_v, bx_ref[...], lam_v)
        dhv = dh_ref[...]
        big_a, big_u = _scan_up(a, a * dhv)
        gg = big_a * gcarry[0:1, :] + big_u
        g = dhv + _shift_up(gg, gcarry[...], 1)
        gcarry[...] = gg[:SUBLANES]
        hhalo = jnp.where(rev > 0, hh_ref[...], 0.0)
        da = g * _shift_down(h_ref[...], hhalo, 1)
        d_mult = g * (ig * xc)
        d_i = g * (mult * xc)
        dxc = g * (mult * ig)
        d_la = da * a - d_mult * (a * a) / mult
        d_r = d_la * (-RG_C * sp)
        d_sp = jnp.sum(d_la * (-RG_C * r), axis=0, keepdims=True)
        d_pa = d_r * r * (1.0 - r)
        d_px = d_i * ig * (1.0 - ig)
        d_pab, d_pxb = _mx(d_pa), _mx(d_px)
        dxc = dxc + _dot_nt(d_pab, wa_v) + _dot_nt(d_pxb, wx_v)
        xcb = _mx(xc)
        dwa_ref[...] += _dot_tn(xcb, d_pab)
        dwx_ref[...] += _dot_tn(xcb, d_pxb)
        dvec_ref[0:1, :] += jnp.sum(d_pa, axis=0, keepdims=True)
        dvec_ref[1:2, :] += jnp.sum(d_px, axis=0, keepdims=True)
        dvec_ref[2:3, :] += d_sp * (-_sigmoid(-lam_v))
        dvec_ref[3:4, :] += jnp.sum(dxc, axis=0, keepdims=True)
        xblk = x_ref[...]
        cw = cw_ref[...]
        dx = dxc * cw[3:4]
        nxt = dxc_next[...]
        dvec_ref[7:8, :] += jnp.sum(dxc * xblk, axis=0, keepdims=True)
        for k in range(3):
            up = _shift_up(dxc, nxt, 3 - k)
            dvec_ref[4 + k:5 + k, :] += jnp.sum(up * xblk, axis=0, keepdims=True)
            dx = dx + up * cw[k:k + 1]
        dxc_next[...] = dxc[:SUBLANES]
        dx_ref[...] = _mx(dx)

    blk = pl.BlockSpec((t, w), lambda i: (nb - 1 - i, 0))
    halo = pl.BlockSpec((SUBLANES, w), lambda i: (jnp.maximum((nb - 1 - i) * tb - 1, 0), 0))
    vec = pl.BlockSpec((1, w), lambda i: (0, 0))
    mat = pl.BlockSpec((w, w), lambda i: (0, 0))
    return pl.pallas_call(
        body, name="rglru_bwd", grid=(nb,),
        in_specs=[blk, blk, blk, halo, pl.BlockSpec((t, w), lambda i: (nb - 1 - i, P0_RNN)),
                  pl.BlockSpec((SUBLANES, w), lambda i: (0, 0)), mat, vec, mat, vec, vec],
        out_specs=[blk, mat, mat, pl.BlockSpec((16, w), lambda i: (0, 0))],
        out_shape=[jax.ShapeDtypeStruct((SEQ, w), MXU_DTYPE), jax.ShapeDtypeStruct((w, w), F32),
                   jax.ShapeDtypeStruct((w, w), F32), jax.ShapeDtypeStruct((16, w), F32)],
        scratch_shapes=[pltpu.VMEM((SUBLANES, w), F32), pltpu.VMEM((SUBLANES, w), F32)],
        compiler_params=_params(("arbitrary",)),
    )(dh, xc, h, h, proj0, cw8, wa, ba, wx, bx, lam)


MLA_T = 512


def _rope(v, c, sa, sb):
    return v * c + pltpu.roll(v, LANES - 16, 1) * sa + pltpu.roll(v, 16, 1) * sb


def _rope_t(dv, c, sa, sb):
    return dv * c + pltpu.roll(dv * sa, 16, 1) + pltpu.roll(dv * sb, LANES - 16, 1)


def _rms(v, g, eps=1e-6):
    rs = lax.rsqrt(jnp.mean(v * v, axis=-1, keepdims=True) + eps)
    return v * rs * g, rs


def _mla_fwd(proj0, q_norm, kv_norm, wq, wkv, tc, tsa, tsb):
    t = MLA_T

    def body(cq_ref, ck_ref, qn_ref, kn_ref, wq_ref, wkv_ref, c_ref, sa_ref, sb_ref,
             oqn_ref, okn_ref, oq_ref, ok_ref, ov_ref):
        c, sa, sb = c_ref[...], sa_ref[...], sb_ref[...]
        ck = ck_ref[...]
        qn = _mx(_rms(cq_ref[...], qn_ref[...])[0])
        kn = _mx(_rms(ck[:, :LANES], kn_ref[...])[0])
        oqn_ref[...] = qn
        okn_ref[...] = kn
        krv = _rope(ck[:, LANES:], c, sa, sb)
        qraw = _dot(qn, wq_ref[...])
        kvraw = _dot(kn, wkv_ref[...])
        for hd in range(MLA_HEADS):
            sl = slice(hd * LANES, (hd + 1) * LANES)
            oq_ref[:, sl] = _mx(_rope(qraw[:, sl], c, sa, sb))
            ok_ref[:, sl] = _mx(kvraw[:, sl] + krv)
        ov_ref[...] = _mx(kvraw[:, 1024:])

    tab = pl.BlockSpec((t, LANES), lambda i: (i, 0))
    wide = pl.BlockSpec((t, 1024), lambda i: (i, 0))
    const = lambda shape: pl.BlockSpec(shape, lambda i: (0, 0))
    return pl.pallas_call(
        body, name="mla_fwd", grid=(SEQ // t,),
        in_specs=[pl.BlockSpec((t, 256), lambda i: (i, 6)), pl.BlockSpec((t, 256), lambda i: (i, 7)),
                  const((1, 256)), const((1, LANES)), const((256, 1024)), const((LANES, 1536)), tab, tab, tab],
        out_specs=[pl.BlockSpec((t, 256), lambda i: (i, 0)), tab, wide, wide, pl.BlockSpec((t, 512), lambda i: (i, 0))],
        out_shape=[jax.ShapeDtypeStruct((SEQ, 256), MXU_DTYPE), jax.ShapeDtypeStruct((SEQ, LANES), MXU_DTYPE),
                   jax.ShapeDtypeStruct((SEQ, 1024), MXU_DTYPE), jax.ShapeDtypeStruct((SEQ, 1024), MXU_DTYPE),
                   jax.ShapeDtypeStruct((SEQ, 512), MXU_DTYPE)],
        compiler_params=_params(("parallel",)),
    )(proj0, proj0, q_norm, kv_norm, wq, wkv, tc, tsa, tsb)


ATT_T = 1024


def _flash_fwd(q, k, v, bcast=()):
    t = ATT_T
    nb = SEQ // t

    steps = [(qi, ki) for qi in range(nb) for ki in range(qi + 1)]
    qi_tab = jnp.asarray([s[0] for s in steps], jnp.int32)
    ki_tab = jnp.asarray([s[1] for s in steps], jnp.int32)

    nx = len(bcast)

    def body(qi_ref, ki_ref, q_ref, k_ref, v_ref, *rest):
        x_refs, (o_ref, lse_ref), g_refs = rest[:nx], rest[nx:nx + 2], rest[nx + 2:2 * nx + 2]
        m_sc, acc_sc = rest[2 * nx + 2:2 * nx + 4]
        step = pl.program_id(1)
        qi, ki = qi_ref[step], ki_ref[step]
        if nx:
            copies = _peer_copies(x_refs, g_refs, rest[2 * nx + 4:], [])

            @pl.when((pl.program_id(0) == 0) & (step == 0))
            def _():
                for cp in copies:
                    cp.start()

        @pl.when(ki == 0)
        def _():
            m_sc[...] = jnp.full_like(m_sc, -jnp.inf)
            acc_sc[...] = jnp.zeros_like(acc_sc)

        def update(diagonal):
            vv = v_ref[...]
            lane_v = lax.broadcasted_iota(jnp.int32, vv.shape, 1)
            for hd in range(2):
                sl = slice(hd * LANES, (hd + 1) * LANES)
                s = _dot_nt(q_ref[:, sl], k_ref[:, sl])
                if diagonal:
                    s = jnp.where(lax.broadcasted_iota(jnp.int32, (t, t), 1)
                                  <= lax.broadcasted_iota(jnp.int32, (t, t), 0), s, -jnp.inf)
                m_prev = m_sc[hd]
                m_new = jnp.maximum(m_prev, jnp.max(s, axis=1, keepdims=True))
                p = jnp.exp2((s - m_new[:, :1]) * ATT_C)
                m_sc[hd] = m_new
                vh = jnp.where((lane_v >= hd * 64) & (lane_v < (hd + 1) * 64), vv, jnp.ones_like(vv))
                acc_sc[hd] = acc_sc[hd] * jnp.exp2((m_prev - m_new) * ATT_C) + _dot(_mx(p), vh)

        @pl.when(ki < qi)
        def _():
            update(False)

        @pl.when(ki == qi)
        def _():
            update(True)
            first = lax.broadcasted_iota(jnp.int32, (t, LANES), 1) < 64
            a0, a1 = acc_sc[0], acc_sc[1]
            l0, l1 = pltpu.roll(a0, 64, 1), pltpu.roll(a1, 64, 1)
            o_ref[...] = jnp.where(first, a0 / l0, a1 / l1)
            lse_ref[0] = jnp.where(first, m_sc[0] * ATT_SCALE + jnp.log(l0), m_sc[1] * ATT_SCALE + jnp.log(l1))

        if nx:
            @pl.when((pl.program_id(0) == 3) & (step == len(steps) - 1))
            def _():
                for cp in copies:
                    cp.wait()

    grid_spec = pltpu.PrefetchScalarGridSpec(
        num_scalar_prefetch=2, grid=(4, len(steps)),
        in_specs=[pl.BlockSpec((t, 256), lambda p, s, qt, kt: (qt[s], p)),
                  pl.BlockSpec((t, 256), lambda p, s, qt, kt: (kt[s], p)),
                  pl.BlockSpec((t, LANES), lambda p, s, qt, kt: (kt[s], p))] + [ANY] * nx,
        out_specs=[pl.BlockSpec((t, LANES), lambda p, s, qt, kt: (qt[s], p)),
                   pl.BlockSpec((1, t, LANES), lambda p, s, qt, kt: (p, qt[s], 0))] + [ANY] * nx,
        scratch_shapes=[pltpu.VMEM((2, t, LANES), F32), pltpu.VMEM((2, t, LANES), F32)]
        + (_exchange_sems(nx) if nx else []))
    res = pl.pallas_call(
        body, name="flash_fwd", grid_spec=grid_spec,
        out_shape=[jax.ShapeDtypeStruct((SEQ, 512), F32), jax.ShapeDtypeStruct((4, SEQ, LANES), F32)]
        + _exchange_shapes([], bcast),
        compiler_params=_params(("arbitrary", "arbitrary")),
    )(qi_tab, ki_tab, q, k, v, *bcast)
    return res[0], res[1], res[2:]


def _flash_bwd(q, k, v, o, do, lse, scatter=()):
    t = ATT_T
    nb = SEQ // t

    steps = [(qi, ki) for ki in range(nb) for qi in range(ki, nb)]
    qi_tab = jnp.asarray([s[0] for s in steps], jnp.int32)
    ki_tab = jnp.asarray([s[1] for s in steps], jnp.int32)
    log2e = math.log2(math.e)

    sc_arrays, sc_ranges = _scatter_args(scatter)
    nx = len(sc_arrays)

    def body(qi_ref, ki_ref, q_ref, k_ref, v_ref, o_ref, do_ref, lse_ref, *rest):
        x_refs, (dq_ref, dk_ref, dv_ref), g_refs = rest[:nx], rest[nx:nx + 3], rest[nx + 3:2 * nx + 3]
        step = pl.program_id(1)
        qi, ki = qi_ref[step], ki_ref[step]
        if nx:
            copies = _peer_copies(x_refs, g_refs, rest[2 * nx + 3:], sc_ranges)

            @pl.when((pl.program_id(0) == 0) & (step == 0))
            def _():
                for cp in copies:
                    cp.start()

        @pl.when(step == 0)
        def _():
            dq_ref[...] = jnp.zeros_like(dq_ref)

        @pl.when(qi == ki)
        def _():
            dk_ref[...] = jnp.zeros_like(dk_ref)
            dv_ref[...] = jnp.zeros_like(dv_ref)

        def update(diagonal):
            dov, ov, vv = do_ref[...], o_ref[...], v_ref[...]
            lse2 = lse_ref[0] * log2e
            lane = lax.broadcasted_iota(jnp.int32, (t, LANES), 1)
            prod = dov * ov
            qrows = pl.ds(pl.multiple_of(qi * t, t), t)
            dv_acc = jnp.zeros((t, LANES), F32)
            dk_new, dq_new = [], []
            for hd in range(2):
                sl = slice(hd * LANES, (hd + 1) * LANES)
                mine = (lane >= hd * 64) & (lane < (hd + 1) * 64)
                qh, kh = q_ref[:, sl], k_ref[:, sl]
                p = jnp.exp2(_dot_nt(qh, kh) * ATT_C - lse2[:, hd * 64:hd * 64 + 1])
                if diagonal:
                    p = jnp.where(lax.broadcasted_iota(jnp.int32, (t, t), 1)
                                  <= lax.broadcasted_iota(jnp.int32, (t, t), 0), p, 0.0)
                do_h = jnp.where(mine, dov, 0.0)
                delta = jnp.sum(jnp.where(mine, prod, 0.0), axis=1, keepdims=True)
                dp = _dot_nt(_mx(do_h), vv)
                ds = _mx(p * (dp - delta) * ATT_SCALE)
                dv_acc = dv_acc + jnp.where(mine, _dot_tn(_mx(p), _mx(dov)), 0.0)
                dk_new.append(_dot_tn(ds, qh))
                dq_new.append(_dot(ds, kh))
            for hd in range(2):
                sl = slice(hd * LANES, (hd + 1) * LANES)
                dk_ref[:, sl] += dk_new[hd]
                dq_ref[qrows, sl] += dq_new[hd]
            dv_ref[...] += dv_acc

        @pl.when(qi > ki)
        def _():
            update(False)

        @pl.when(qi == ki)
        def _():
            update(True)

        if nx:
            @pl.when((pl.program_id(0) == 3) & (step == len(steps) - 1))
            def _():
                for cp in copies:
                    cp.wait()

    qmap = lambda p, s, qt, kt: (qt[s], p)
    kmap = lambda p, s, qt, kt: (kt[s], p)
    grid_spec = pltpu.PrefetchScalarGridSpec(
        num_scalar_prefetch=2, grid=(4, len(steps)),
        in_specs=[pl.BlockSpec((t, 256), qmap), pl.BlockSpec((t, 256), kmap), pl.BlockSpec((t, LANES), kmap),
                  pl.BlockSpec((t, LANES), qmap), pl.BlockSpec((t, LANES), qmap),
                  pl.BlockSpec((1, t, LANES), lambda p, s, qt, kt: (p, qt[s], 0))] + [ANY] * nx,
        out_specs=[pl.BlockSpec((SEQ, 256), lambda p, s, qt, kt: (0, p)), pl.BlockSpec((t, 256), kmap),
                   pl.BlockSpec((t, LANES), kmap)] + [ANY] * nx,
        scratch_shapes=_exchange_sems(nx) if nx else [])
    res = pl.pallas_call(
        body, name="flash_bwd", grid_spec=grid_spec,
        out_shape=[jax.ShapeDtypeStruct((SEQ, 1024), F32), jax.ShapeDtypeStruct((SEQ, 1024), F32),
                   jax.ShapeDtypeStruct((SEQ, 512), F32)] + _exchange_shapes(sc_arrays, []),
        compiler_params=_params(("arbitrary", "arbitrary")),
    )(qi_tab, ki_tab, q, k, v, o, do, lse, *sc_arrays)
    return res[0], res[1], res[2], res[3:]


def _rms_bwd(v, g, dy, eps=1e-6):
    rs = lax.rsqrt(jnp.mean(v * v, axis=-1, keepdims=True) + eps)
    xh = v * rs
    dxh = dy * g
    dv = rs * (dxh - xh * jnp.mean(dxh * xh, axis=-1, keepdims=True))
    return dv, jnp.sum(dy * xh, axis=0, keepdims=True)


def _mla_bwd(dq, dk, dv, proj0, qlat, klat, q_norm, kv_norm, wq, wkv, tc, tsa, tsb):
    t = MLA_T

    def body(dq_ref, dk_ref, dv_ref, cq_ref, ck_ref, ql_ref, kl_ref, qn_ref, kn_ref, wq_ref, wkv_ref,
             c_ref, sa_ref, sb_ref, o_ref, gwq_ref, gwkv_ref, dgq_ref, dgk_ref, oq_ref, okv_ref):
        @pl.when(pl.program_id(0) == 0)
        def _():
            dgq_ref[...] = jnp.zeros_like(dgq_ref)
            dgk_ref[...] = jnp.zeros_like(dgk_ref)
            gwq_ref[...] = jnp.zeros_like(gwq_ref)
            gwkv_ref[...] = jnp.zeros_like(gwkv_ref)

        c, sa, sb = c_ref[...], sa_ref[...], sb_ref[...]
        lane = lax.broadcasted_iota(jnp.int32, (t, LANES), 1)
        dkr = jnp.zeros((t, LANES), F32)
        for hd in range(MLA_HEADS):
            sl = slice(hd * LANES, (hd + 1) * LANES)
            oq_ref[:, sl] = _mx(_rope_t(dq_ref[:, sl], c, sa, sb))
            dkh = dk_ref[:, sl]
            okv_ref[:, sl] = _mx(dkh)
            dkr = dkr + dkh
        okv_ref[:, 1024:] = _mx(dv_ref[...])
        dkr = _rope_t(jnp.where((lane >= 64) & (lane < 96), dkr, 0.0), c, sa, sb)
        dqraw, dkvraw = oq_ref[...], okv_ref[...]
        gwq_ref[...] += _dot_tn(ql_ref[...], dqraw)
        gwkv_ref[...] += _dot_tn(kl_ref[...], dkvraw)
        dqn = _dot_nt(dqraw, wq_ref[...])
        dkn = _dot_nt(dkvraw, wkv_ref[...])
        dcq, dgq = _rms_bwd(cq_ref[...], qn_ref[...], dqn)
        dck, dgk = _rms_bwd(ck_ref[:, :LANES], kn_ref[...], dkn)
        o_ref[:, :256] = _mx(dcq)
        o_ref[:, 256:384] = _mx(dck)
        o_ref[:, 384:] = _mx(dkr)
        dgq_ref[0:1, :] += dgq
        dgk_ref[0:1, :] += dgk

    tab = pl.BlockSpec((t, LANES), lambda i: (i, 0))
    wide = pl.BlockSpec((t, 1024), lambda i: (i, 0))
    const = lambda shape: pl.BlockSpec(shape, lambda i: (0, 0))
    return pl.pallas_call(
        body, name="mla_bwd", grid=(SEQ // t,),
        in_specs=[wide, wide, pl.BlockSpec((t, 512), lambda i: (i, 0)),
                  pl.BlockSpec((t, 256), lambda i: (i, 6)), pl.BlockSpec((t, 256), lambda i: (i, 7)),
                  pl.BlockSpec((t, 256), lambda i: (i, 0)), tab,
                  const((1, 256)), const((1, LANES)), const((256, 1024)), const((LANES, 1536)), tab, tab, tab],
        out_specs=[pl.BlockSpec((t, 512), lambda i: (i, 0)), const((256, 1024)), const((LANES, 1536)),
                   const((SUBLANES, 256)), const((SUBLANES, LANES))],
        out_shape=[jax.ShapeDtypeStruct((SEQ, 512), MXU_DTYPE), jax.ShapeDtypeStruct((256, 1024), F32),
                   jax.ShapeDtypeStruct((LANES, 1536), F32), jax.ShapeDtypeStruct((SUBLANES, 256), F32),
                   jax.ShapeDtypeStruct((SUBLANES, LANES), F32)],
        scratch_shapes=[pltpu.VMEM((t, 1024), MXU_DTYPE), pltpu.VMEM((t, 1536), MXU_DTYPE)],
        compiler_params=_params(("arbitrary",)),
    )(dq, dk, dv, proj0, proj0, qlat, klat, q_norm, kv_norm, wq, wkv, tc, tsa, tsb)


LN_T = 512


def _ln(v, g, b, eps=1e-5):
    mu = jnp.mean(v, axis=-1, keepdims=True)
    xc = v - mu
    rs = lax.rsqrt(jnp.mean(xc * xc, axis=-1, keepdims=True) + eps)
    return xc * rs * g + b


def _ln_bwd(v, g, dy, eps=1e-5):
    mu = jnp.mean(v, axis=-1, keepdims=True)
    xc = v - mu
    rs = lax.rsqrt(jnp.mean(xc * xc, axis=-1, keepdims=True) + eps)
    xh = xc * rs
    dxh = dy * g
    dv = rs * (dxh - jnp.mean(dxh, axis=-1, keepdims=True) - xh * jnp.mean(dxh * xh, axis=-1, keepdims=True))
    return dv, jnp.sum(dy * xh, axis=0, keepdims=True), jnp.sum(dy, axis=0, keepdims=True)


def _l0_out(h, o, proj0, x, w_out, g, b):
    t = LN_T

    def body(h_ref, o_ref, ga_ref, gb_ref, x_ref, w_ref, g_ref, b_ref, y_ref, v_ref, x1_ref, x1b_ref):
        y = _mx(jnp.concatenate([h_ref[...] * _silu(ga_ref[...]), o_ref[...] * _silu(gb_ref[...])], axis=1))
        v = DN_ALPHA * x_ref[...] + _dot(y, w_ref[...])
        y_ref[...] = y
        v_ref[...] = v
        x1 = _ln(v, g_ref[...], b_ref[...])
        x1_ref[...] = x1
        x1b_ref[...] = _mx(x1)

    half = pl.BlockSpec((t, 512), lambda i: (i, 0))
    full = pl.BlockSpec((t, D_MODEL), lambda i: (i, 0))
    vec = pl.BlockSpec((1, D_MODEL), lambda i: (0, 0))
    return pl.pallas_call(
        body, name="l0_out", grid=(SEQ // t,),
        in_specs=[half, half, pl.BlockSpec((t, 512), lambda i: (i, 0)), pl.BlockSpec((t, 512), lambda i: (i, 1)), full,
                  pl.BlockSpec((D_MODEL, D_MODEL), lambda i: (0, 0)), vec, vec],
        out_specs=[full, full, full, full],
        out_shape=[jax.ShapeDtypeStruct((SEQ, D_MODEL), MXU_DTYPE), jax.ShapeDtypeStruct((SEQ, D_MODEL), F32),
                   jax.ShapeDtypeStruct((SEQ, D_MODEL), F32), jax.ShapeDtypeStruct((SEQ, D_MODEL), MXU_DTYPE)],
        compiler_params=_params(("parallel",)),
    )(h, o, proj0, proj0, x, w_out, g, b)


def _ln_bwd_call(v, dy, ddt, w1d, g):
    t = LN_T

    def body(v_ref, dy_ref, ddt_ref, w_ref, g_ref, dv_ref, dgb_ref):
        @pl.when(pl.program_id(0) == 0)
        def _():
            dgb_ref[...] = jnp.zeros_like(dgb_ref)

        dy_v = dy_ref[...] + _dot_nt(_mx(ddt_ref[...]), w_ref[...])
        dv, dg, db = _ln_bwd(v_ref[...], g_ref[...], dy_v)
        dv_ref[...] = dv
        dgb_ref[0:1, :] += dg
        dgb_ref[1:2, :] += db

    full = pl.BlockSpec((t, D_MODEL), lambda i: (i, 0))
    return pl.pallas_call(
        body, name="ln_bwd", grid=(SEQ // t,),
        in_specs=[full, full, pl.BlockSpec((t, LANES), lambda i: (i, 0)), pl.BlockSpec((D_MODEL, LANES), lambda i: (0, 0)),
                  pl.BlockSpec((1, D_MODEL), lambda i: (0, 0))],
        out_specs=[full, pl.BlockSpec((SUBLANES, D_MODEL), lambda i: (0, 0))],
        out_shape=[jax.ShapeDtypeStruct((SEQ, D_MODEL), F32), jax.ShapeDtypeStruct((SUBLANES, D_MODEL), F32)],
        compiler_params=_params(("arbitrary",)),
    )(v, dy, ddt, w1d, g)


def _gate_bwd(dv0, w_out, h, o, proj0):
    t = LN_T

    def body(dv_ref, w_ref, h_ref, o_ref, ga_ref, gb_ref, dh_ref, do_ref, dg_ref):
        dy = _dot_nt(_mx(dv_ref[...]), w_ref[...])
        ga, gb, dya, dyb = ga_ref[...], gb_ref[...], dy[:, :512], dy[:, 512:]
        dh_ref[...] = dya * _silu(ga)
        do_ref[...] = dyb * _silu(gb)
        dg_ref[:, :512] = _mx(dya * h_ref[...] * _dsilu(ga))
        dg_ref[:, 512:] = _mx(dyb * o_ref[...] * _dsilu(gb))

    half = pl.BlockSpec((t, 512), lambda i: (i, 0))
    half1 = pl.BlockSpec((t, 512), lambda i: (i, 1))
    full = pl.BlockSpec((t, 1024), lambda i: (i, 0))
    return pl.pallas_call(
        body, name="gate_bwd", grid=(SEQ // t,),
        in_specs=[full, pl.BlockSpec((D_MODEL, D_MODEL), lambda i: (0, 0)), half, half, half, half1],
        out_specs=[half, half, full],
        out_shape=[jax.ShapeDtypeStruct((SEQ, 512), F32), jax.ShapeDtypeStruct((SEQ, 512), F32),
                   jax.ShapeDtypeStruct((SEQ, 1024), MXU_DTYPE)],
        compiler_params=_params(("parallel",)),
    )(dv0, w_out, h, o, proj0, proj0)


CONV_T = 512
CONV_CB = 1024


def _ssd_conv_fwd(xbc, cw8, cb):
    t, cbk = CONV_T, CONV_CB
    tb = t // SUBLANES

    def body(x_ref, halo_ref, cw_ref, cb_ref, pre_ref, act_ref):
        halo = jnp.where(pl.program_id(1) > 0, halo_ref[...], 0.0)
        pre = _conv4(x_ref[...], halo, cw_ref[...], cb_ref[...])
        pre_ref[...] = pre
        act_ref[...] = _silu(pre)

    blk = pl.BlockSpec((t, cbk), lambda j, i: (i, j))
    return pl.pallas_call(
        body, name="ssd_conv_fwd", grid=(SSD_CONV // cbk, SEQ // t),
        in_specs=[blk, pl.BlockSpec((SUBLANES, cbk), lambda j, i: (jnp.maximum(i * tb - 1, 0), j)),
                  pl.BlockSpec((SUBLANES, cbk), lambda j, i: (0, j)), pl.BlockSpec((1, cbk), lambda j, i: (0, j))],
        out_specs=[blk, blk],
        out_shape=[jax.ShapeDtypeStruct((SEQ, SSD_CONV), F32), jax.ShapeDtypeStruct((SEQ, SSD_CONV), F32)],
        compiler_params=_params(("parallel", "parallel")),
    )(xbc, xbc, cw8, cb)


def _ssd_conv_bwd(dact, pre, xbc, cw8):
    t, cbk = CONV_T, CONV_CB
    tb = t // SUBLANES
    nb = SEQ // t

    def body(da_ref, dan_ref, pre_ref, pren_ref, x_ref, cw_ref, dx_ref, dcw_ref):
        i = pl.program_id(1)

        @pl.when(i == 0)
        def _():
            dcw_ref[...] = jnp.zeros_like(dcw_ref)

        dpre = da_ref[...] * _dsilu(pre_ref[...])
        dpre_next = jnp.where(i < nb - 1, dan_ref[...] * _dsilu(pren_ref[...]), 0.0)
        xblk = x_ref[...]
        cw = cw_ref[...]
        dx = dpre * cw[3:4]
        dcw_ref[3:4, :] += jnp.sum(dpre * xblk, axis=0, keepdims=True)
        for k in range(3):
            up = _shift_up(dpre, dpre_next, 3 - k)
            dcw_ref[k:k + 1, :] += jnp.sum(up * xblk, axis=0, keepdims=True)
            dx = dx + up * cw[k:k + 1]
        dcw_ref[4:5, :] += jnp.sum(dpre, axis=0, keepdims=True)
        dx_ref[...] = _mx(dx)

    blk = pl.BlockSpec((t, cbk), lambda j, i: (i, j))
    nxt = pl.BlockSpec((SUBLANES, cbk), lambda j, i: (jnp.minimum((i + 1) * tb, SEQ // SUBLANES - 1), j))
    acc = pl.BlockSpec((SUBLANES, cbk), lambda j, i: (0, j))
    return pl.pallas_call(
        body, name="ssd_conv_bwd", grid=(SSD_CONV // cbk, nb),
        in_specs=[blk, nxt, blk, nxt, blk, acc],
        out_specs=[blk, acc],
        out_shape=[jax.ShapeDtypeStruct((SEQ, SSD_CONV), MXU_DTYPE), jax.ShapeDtypeStruct((SUBLANES, SSD_CONV), F32)],
        compiler_params=_params(("parallel", "arbitrary")),
    )(dact, dact, pre, pre, xbc, cw8)


def _ssd_common(dt_raw, bias, alog, tril, expand_t, xs):
    lane = lax.broadcasted_iota(jnp.int32, dt_raw.shape, 1)
    dt = jnp.where(lane < SSD_HEADS, _softplus(dt_raw + bias), 0.0)
    a_neg = -jnp.exp(alog)
    cs = _dot_hi(tril, dt * a_neg)
    dt_x = _expand_heads(dt, expand_t)
    ecs_x = _expand_heads(jnp.exp(cs), expand_t)
    ds_x = _expand_heads(jnp.exp(cs[SSD_L - 1:SSD_L, :] - cs), expand_t)
    return dt, a_neg, cs, dt_x, None, xs * dt_x, ds_x, ecs_x, ecs_x[SSD_L - 1:SSD_L, :]


def _expand_heads(v, expand_t):
    hi = v.astype(jnp.bfloat16)
    lo = (v - hi.astype(F32)).astype(jnp.bfloat16)
    return _dot_nt(hi, expand_t) + _dot_nt(lo, expand_t)


def _fold_heads(v, expand_t):
    hi = v.astype(jnp.bfloat16)
    lo = (v - hi.astype(F32)).astype(jnp.bfloat16)
    return _dot(hi, expand_t) + _dot(lo, expand_t)


def _ssd_decay(cs, cs_t, hh, causal):
    seg = cs[:, hh:hh + 1] - cs_t[hh:hh + 1, :]
    return jnp.where(causal, jnp.exp(jnp.where(causal, seg, 0.0)), 0.0)


def _ssd_scan_fwd(act, dt_raw, bias, alog, d_x, tril, expand_t):
    nc = SEQ // SSD_L
    gw = SSD_INNER // SSD_GROUPS

    def body(act_ref, dt_ref, bias_ref, alog_ref, dx_ref, tril_ref, et_ref, y_ref, hp_ref, h_sc):
        @pl.when(pl.program_id(0) == 0)
        def _():
            h_sc[...] = jnp.zeros_like(h_sc)

        xs = act_ref[:, :SSD_INNER]
        _, _, cs, _, _, xdt, ds_x, ecs_x, elast = _ssd_common(
            dt_ref[...], bias_ref[...], alog_ref[...], tril_ref[...], et_ref[...], xs)
        cs_t = cs.T
        causal = (lax.broadcasted_iota(jnp.int32, (SSD_L, SSD_L), 0)
                  >= lax.broadcasted_iota(jnp.int32, (SSD_L, SSD_L), 1))
        lane = lax.broadcasted_iota(jnp.int32, (SSD_L, LANES), 1)
        xdt_b = _mx(xdt)
        xds_b = _mx(xdt * ds_x)
        hp_ref[0] = h_sc[...]
        for g in range(SSD_GROUPS):
            gs = slice(g * gw, (g + 1) * gw)
            bg = _mx(act_ref[:, SSD_INNER + g * SSD_N:SSD_INNER + (g + 1) * SSD_N])
            cg = _mx(act_ref[:, SSD_INNER + 512 + g * SSD_N:SSD_INNER + 512 + (g + 1) * SSD_N])
            cb = _dot_nt(cg, bg)
            hprev = h_sc[:, gs]
            yoff = _dot(cg, _mx(hprev)) * ecs_x[:, gs]
            h_sc[:, gs] = hprev * elast[:, gs] + _dot_tn(bg, xds_b[:, gs])
            for pr in range(4):
                ps = slice(g * gw + pr * LANES, g * gw + (pr + 1) * LANES)
                xp = xdt_b[:, ps]
                ydiag = jnp.zeros((SSD_L, LANES), F32)
                for j in range(2):
                    dm = _ssd_decay(cs, cs_t, g * 8 + pr * 2 + j, causal)
                    mine = (lane >= j * 64) & (lane < (j + 1) * 64)
                    ydiag = ydiag + _dot(_mx(cb * dm), jnp.where(mine, xp, jnp.zeros_like(xp)))
                y_ref[:, ps] = ydiag + yoff[:, pr * LANES:(pr + 1) * LANES] + dx_ref[:, ps] * xs[:, ps]

    const = lambda shape: pl.BlockSpec(shape, lambda c: (0, 0))
    return pl.pallas_call(
        body, name="ssd_scan_fwd", grid=(nc,),
        in_specs=[pl.BlockSpec((SSD_L, SSD_CONV), lambda c: (c, 0)), pl.BlockSpec((SSD_L, LANES), lambda c: (c, 0)),
                  const((1, LANES)), const((1, LANES)), const((1, SSD_INNER)), const((SSD_L, SSD_L)),
                  const((SSD_INNER, LANES))],
        out_specs=[pl.BlockSpec((SSD_L, SSD_INNER), lambda c: (c, 0)),
                   pl.BlockSpec((1, SSD_N, SSD_INNER), lambda c: (c, 0, 0))],
        out_shape=[jax.ShapeDtypeStruct((SEQ, SSD_INNER), F32), jax.ShapeDtypeStruct((nc, SSD_N, SSD_INNER), F32)],
        scratch_shapes=[pltpu.VMEM((SSD_N, SSD_INNER), F32)],
        compiler_params=_params(("arbitrary",)),
    )(act, dt_raw, bias, alog, d_x, tril, expand_t)


def _ssd_scan_bwd(dy, act, dt_raw, hprev_all, bias, alog, d_x, tril, expand_t):
    nc = SEQ // SSD_L
    gw = SSD_INNER // SSD_GROUPS

    def body(dy_ref, act_ref, dt_ref, hp_ref, bias_ref, alog_ref, dx_ref, tril_ref, et_ref,
             dact_ref, ddt_ref, dvec_ref, dh_sc, dd_sc):
        i = pl.program_id(0)

        @pl.when(i == 0)
        def _():
            dh_sc[...] = jnp.zeros_like(dh_sc)
            dd_sc[...] = jnp.zeros_like(dd_sc)
            dvec_ref[...] = jnp.zeros_like(dvec_ref)

        xs = act_ref[:, :SSD_INNER]
        dt_raw_v, bias_v = dt_ref[...], bias_ref[...]
        dt, a_neg, cs, dt_x, _, xdt, ds_x, ecs_x, elast = _ssd_common(
            dt_raw_v, bias_v, alog_ref[...], tril_ref[...], et_ref[...], xs)
        cs_t = cs.T
        rowi = lax.broadcasted_iota(jnp.int32, (SSD_L, SSD_L), 0)
        coli = lax.broadcasted_iota(jnp.int32, (SSD_L, SSD_L), 1)
        causal = rowi >= coli
        lane = lax.broadcasted_iota(jnp.int32, (SSD_L, LANES), 1)
        row_g = lax.broadcasted_iota(jnp.int32, (SSD_L, gw), 0)
        dyv = dy_ref[...]
        dd_sc[0:1, :] += jnp.sum(dyv * xs, axis=0, keepdims=True)
        xdt_b = _mx(xdt)
        xds = xdt * ds_x
        xds_b = _mx(xds)
        dy_b = _mx(dyv)
        dye_b = _mx(dyv * ecs_x)
        dcs = jnp.zeros((SSD_L, LANES), F32)
        dcs_t = jnp.zeros((LANES, SSD_L), F32)
        dcs_parts = []
        dxdt_parts = []
        for g in range(SSD_GROUPS):
            gs = slice(g * gw, (g + 1) * gw)
            bcol = slice(SSD_INNER + g * SSD_N, SSD_INNER + (g + 1) * SSD_N)
            ccol = slice(SSD_INNER + 512 + g * SSD_N, SSD_INNER + 512 + (g + 1) * SSD_N)
            bg, cg = _mx(act_ref[:, bcol]), _mx(act_ref[:, ccol])
            cb = _dot_nt(cg, bg)
            hp = hp_ref[0, :, gs]
            hp_b = _mx(hp)
            dh = dh_sc[:, gs]
            dh_b = _mx(dh)
            yoff = _dot(cg, hp_b) * ecs_x[:, gs]
            bdh = _dot(bg, dh_b)
            tt = xds[:, gs] * bdh
            last_row = (jnp.sum(tt, axis=0, keepdims=True)
                        + jnp.sum(dh * hp, axis=0, keepdims=True) * elast[:, gs])
            dcs_parts.append(dyv[:, gs] * yoff - tt + jnp.where(row_g == SSD_L - 1, last_row, 0.0))
            dc_g = _dot_nt(dye_b[:, gs], hp_b)
            db_g = _dot_nt(xds_b[:, gs], dh_b)
            dh_sc[:, gs] = _dot_tn(cg, dye_b[:, gs]) + dh * elast[:, gs]
            wsum = jnp.zeros((SSD_L, SSD_L), F32)
            dxdt_g = []
            for pr in range(4):
                ps = slice(g * gw + pr * LANES, g * gw + (pr + 1) * LANES)
                xp, dyp = xdt_b[:, ps], dy_b[:, ps]
                dxp = jnp.zeros((SSD_L, LANES), F32)
                for j in range(2):
                    hh = g * 8 + pr * 2 + j
                    dm = _ssd_decay(cs, cs_t, hh, causal)
                    mine = (lane >= j * 64) & (lane < (j + 1) * 64)
                    dy_h = jnp.where(mine, dyp, jnp.zeros_like(dyp))
                    wd = _dot_nt(dy_h, xp) * dm
                    wsum = wsum + wd
                    gmat = wd * cb
                    dcs = dcs + jnp.where(lane == hh, jnp.sum(gmat, axis=1, keepdims=True), 0.0)
                    dcs_t = dcs_t - jnp.where(rowi == hh, jnp.sum(gmat, axis=0, keepdims=True), 0.0)
                    dxp = dxp + _dot_tn(_mx(cb * dm), dy_h)
                dxdt_g.append(dxp)
            dxdt_parts.append(jnp.concatenate(dxdt_g, axis=1) + bdh * ds_x[:, gs])
            ws_b = _mx(wsum)
            dact_ref[:, ccol] = dc_g + _dot(ws_b, bg)
            dact_ref[:, bcol] = db_g + _dot_tn(ws_b, cg)
        dxdt = jnp.concatenate(dxdt_parts, axis=1)
        dcs_x = jnp.concatenate(dcs_parts, axis=1)
        et = et_ref[...]
        dcs_tot = dcs + dcs_t.T + _fold_heads(dcs_x, et)
        da_dt = _dot_hi((coli >= rowi).astype(F32), dcs_tot)
        ddt = da_dt * a_neg + _fold_heads(dxdt * xs, et)
        ddt_raw = ddt * _sigmoid(dt_raw_v + bias_v)
        ddt_ref[...] = ddt_raw
        dvec_ref[0:1, :] += jnp.sum(ddt_raw, axis=0, keepdims=True)
        dvec_ref[1:2, :] += jnp.sum(da_dt * dt, axis=0, keepdims=True) * a_neg
        dact_ref[:, :SSD_INNER] = dyv * dx_ref[...] + dxdt * dt_x

        @pl.when(i == nc - 1)
        def _():
            dvec_ref[2:3, :] = _fold_heads(dd_sc[...], et)[0:1, :]

    const = lambda shape: pl.BlockSpec(shape, lambda c: (0, 0))
    rev = lambda c: (nc - 1 - c, 0)
    return pl.pallas_call(
        body, name="ssd_scan_bwd", grid=(nc,),
        in_specs=[pl.BlockSpec((SSD_L, SSD_INNER), rev), pl.BlockSpec((SSD_L, SSD_CONV), rev),
                  pl.BlockSpec((SSD_L, LANES), rev),
                  pl.BlockSpec((1, SSD_N, SSD_INNER), lambda c: (nc - 1 - c, 0, 0)),
                  const((1, LANES)), const((1, LANES)), const((1, SSD_INNER)), const((SSD_L, SSD_L)),
                  const((SSD_INNER, LANES))],
        out_specs=[pl.BlockSpec((SSD_L, SSD_CONV), rev), pl.BlockSpec((SSD_L, LANES), rev), const((SUBLANES, LANES))],
        out_shape=[jax.ShapeDtypeStruct((SEQ, SSD_CONV), F32), jax.ShapeDtypeStruct((SEQ, LANES), F32),
                   jax.ShapeDtypeStruct((SUBLANES, LANES), F32)],
        scratch_shapes=[pltpu.VMEM((SSD_N, SSD_INNER), F32), pltpu.VMEM((SUBLANES, SSD_INNER), F32)],
        compiler_params=_params(("arbitrary",)),
    )(dy, act, dt_raw, hprev_all, bias, alog, d_x, tril, expand_t)


L1_T = 512


def _gated_norm(y, z, nw):
    y2 = y * _silu(z)
    gw = SSD_INNER // SSD_GROUPS
    outs, xhs, rss = [], [], []
    for g in range(SSD_GROUPS):
        gs = slice(g * gw, (g + 1) * gw)
        v = y2[:, gs]
        rs = lax.rsqrt(jnp.mean(v * v, axis=-1, keepdims=True) + 1e-6)
        xhs.append(v * rs)
        rss.append(rs)
        outs.append(v * rs * nw[:, gs])
    return outs, xhs, rss


def _l1_out(y, z, nw, w_out, x1, g, b, target):
    t = L1_T

    def body(y_ref, z_ref, nw_ref, w_ref, x1_ref, g_ref, b_ref, tg_ref, yn_ref, dv_ref, dgb_ref, loss_ref):
        @pl.when(pl.program_id(0) == 0)
        def _():
            dgb_ref[...] = jnp.zeros_like(dgb_ref)
            loss_ref[...] = jnp.zeros_like(loss_ref)

        outs, _, _ = _gated_norm(y_ref[...], z_ref[...], nw_ref[...])
        yn = _mx(jnp.concatenate(outs, axis=1))
        yn_ref[...] = yn
        v = DN_ALPHA * x1_ref[...] + _dot(yn, w_ref[...])
        gv = g_ref[...]
        err = _ln(v, gv, b_ref[...]) - tg_ref[...]
        rowsum = jnp.sum(err * err, axis=1, keepdims=True)
        loss_ref[...] += 0.5 * jnp.sum(rowsum, axis=0, keepdims=True) / D_MODEL
        dv, dg, db = _ln_bwd(v, gv, err / D_MODEL)
        dv_ref[...] = dv
        dgb_ref[0:1, :] += dg
        dgb_ref[1:2, :] += db

    wide = pl.BlockSpec((t, SSD_INNER), lambda i: (i, 0))
    full = pl.BlockSpec((t, D_MODEL), lambda i: (i, 0))
    vec = pl.BlockSpec((1, D_MODEL), lambda i: (0, 0))
    return pl.pallas_call(
        body, name="l1_out", grid=(SEQ // t,),
        in_specs=[wide, wide, pl.BlockSpec((1, SSD_INNER), lambda i: (0, 0)),
                  pl.BlockSpec((SSD_INNER, D_MODEL), lambda i: (0, 0)), full, vec, vec, full],
        out_specs=[wide, full, pl.BlockSpec((SUBLANES, D_MODEL), lambda i: (0, 0)),
                   pl.BlockSpec((SUBLANES, LANES), lambda i: (0, 0))],
        out_shape=[jax.ShapeDtypeStruct((SEQ, SSD_INNER), MXU_DTYPE), jax.ShapeDtypeStruct((SEQ, D_MODEL), F32),
                   jax.ShapeDtypeStruct((SUBLANES, D_MODEL), F32), jax.ShapeDtypeStruct((SUBLANES, LANES), F32)],
        compiler_params=_params(("arbitrary",)),
    )(y, z, nw, w_out, x1, g, b, target)


def _l1_gate_bwd(dv1, w_out, y, z, nw):
    t = L1_T
    gw = SSD_INNER // SSD_GROUPS

    def body(dv_ref, w_ref, y_ref, z_ref, nw_ref, dy_ref, dz_ref, dnw_ref):
        @pl.when(pl.program_id(0) == 0)
        def _():
            dnw_ref[...] = jnp.zeros_like(dnw_ref)

        dyn = _dot_nt(_mx(dv_ref[...]), w_ref[...])
        yv, zv, nwv = y_ref[...], z_ref[...], nw_ref[...]
        _, xhs, rss = _gated_norm(yv, zv, nwv)
        sz, dsz = _silu(zv), _dsilu(zv)
        for g in range(SSD_GROUPS):
            gs = slice(g * gw, (g + 1) * gw)
            d_out = dyn[:, gs]
            xh = xhs[g]
            dnw_ref[0:1, gs] += jnp.sum(d_out * xh, axis=0, keepdims=True)
            dxh = d_out * nwv[:, gs]
            dy2 = rss[g] * (dxh - xh * jnp.mean(dxh * xh, axis=-1, keepdims=True))
            dy_ref[:, gs] = dy2 * sz[:, gs]
            dz_ref[:, gs] = _mx(dy2 * yv[:, gs] * dsz[:, gs])

    wide = pl.BlockSpec((t, SSD_INNER), lambda i: (i, 0))
    return pl.pallas_call(
        body, name="l1_gate_bwd", grid=(SEQ // t,),
        in_specs=[pl.BlockSpec((t, D_MODEL), lambda i: (i, 0)), pl.BlockSpec((SSD_INNER, D_MODEL), lambda i: (0, 0)),
                  wide, wide, pl.BlockSpec((1, SSD_INNER), lambda i: (0, 0))],
        out_specs=[wide, wide, pl.BlockSpec((SUBLANES, SSD_INNER), lambda i: (0, 0))],
        out_shape=[jax.ShapeDtypeStruct((SEQ, SSD_INNER), F32), jax.ShapeDtypeStruct((SEQ, SSD_INNER), MXU_DTYPE),
                   jax.ShapeDtypeStruct((SUBLANES, SSD_INNER), F32)],
        compiler_params=_params(("arbitrary",)),
    )(dv1, w_out, y, z, nw)


MESH = pl.DeviceIdType.MESH
ANY = pl.BlockSpec(memory_space=pl.ANY)


def _flip(v, bit):
    return 1 - v if bit else v


def _all_gather(blocks, name):
    n = len(blocks)

    def body(*refs):
        x_refs, out_refs = refs[:n], refs[n:2 * n]
        send_sems, recv_sems, local_sems = refs[2 * n:]
        mx, my, mc = lax.axis_index("x"), lax.axis_index("y"), lax.axis_index("c")
        me, sibling = (mx, my, mc), (mx, my, 1 - mc)
        chips = [(1 - mx, my), (mx, 1 - my), (1 - mx, 1 - my)]

        def copy(a, k, block, to, own=False):
            px, py, pc = block
            slot = out_refs[a].at[4 * px + 2 * py + pc]
            return pltpu.make_async_remote_copy(
                src_ref=x_refs[a] if own else slot, dst_ref=slot,
                send_sem=send_sems.at[7 * a + k], recv_sem=recv_sems.at[7 * a + k], device_id=to, device_id_type=MESH)

        mine = [pltpu.make_async_copy(x_refs[a], out_refs[a].at[4 * mx + 2 * my + mc], local_sems.at[a])
                for a in range(n)]
        first = []
        for a in range(n):
            mine[a].start()
            first.append(copy(a, 0, me, sibling, own=True))
            first += [copy(a, 1 + j, me, (*chip, mc), own=True) for j, chip in enumerate(chips)]
        for cp in first:
            cp.start()
        passed = []
        for j, chip in enumerate(chips):
            for a in range(n):
                copy(a, 1 + j, (*chip, mc), me).wait_recv()
                fwd = copy(a, 4 + j, (*chip, mc), sibling)
                fwd.start()
                passed.append(fwd)
        for a in range(n):
            copy(a, 0, sibling, me).wait_recv()
            for j, chip in enumerate(chips):
                copy(a, 4 + j, (*chip, 1 - mc), me).wait_recv()
        for cp in first + passed:
            cp.wait_send()
        for cp in mine:
            cp.wait()

    return pl.pallas_call(
        body, name=name, in_specs=[ANY] * n, out_specs=[ANY] * n,
        out_shape=[jax.ShapeDtypeStruct((N_DEV,) + b.shape, b.dtype) for b in blocks],
        scratch_shapes=[pltpu.SemaphoreType.DMA((7 * n,)), pltpu.SemaphoreType.DMA((7 * n,)),
                        pltpu.SemaphoreType.DMA((n,))],
    )(*blocks)


def _l0_in(x, w0p, bcast=()):
    n = len(bcast)
    tm, tn = 1024, 1024
    gi, gj = SEQ // tm, 2048 // tn

    def body(x_ref, w_ref, *rest):
        o_ref, xb_ref = rest[n], rest[n + 1]
        i, j = pl.program_id(0), pl.program_id(1)
        if n:
            copies = _peer_copies(rest[:n], rest[n + 2:2 * n + 2], rest[2 * n + 2:], [])

            @pl.when((i == 0) & (j == 0))
            def _():
                for cp in copies:
                    cp.start()

        xb = _mx(x_ref[...])
        xb_ref[...] = xb
        o_ref[...] = _dot(xb, w_ref[...])

        if n:
            @pl.when((i == gi - 1) & (j == gj - 1))
            def _():
                for cp in copies:
                    cp.wait()

    res = pl.pallas_call(
        body, name="l0_in", grid=(gi, gj),
        in_specs=[pl.BlockSpec((tm, D_MODEL), lambda i, j: (i, 0)), pl.BlockSpec((D_MODEL, tn), lambda i, j: (0, j))]
        + [ANY] * n,
        out_specs=[pl.BlockSpec((tm, tn), lambda i, j: (i, j)), pl.BlockSpec((tm, D_MODEL), lambda i, j: (i, 0))]
        + [ANY] * n,
        out_shape=[jax.ShapeDtypeStruct((SEQ, 2048), F32), jax.ShapeDtypeStruct((SEQ, D_MODEL), MXU_DTYPE)]
        + _exchange_shapes([], bcast),
        scratch_shapes=_exchange_sems(n) if n else [],
        compiler_params=_params(("arbitrary", "arbitrary")),
    )(x, w0p, *bcast)
    return res[0], res[1], res[2:]


def _l0_dx(dxr, dgate, dtail, w0p, dv0, scatter=(), bcast=()):
    arrays, ranges = _scatter_args(scatter)
    n = len(arrays) + len(bcast)
    tm = 1024
    steps = SEQ // tm

    def body(dxr_ref, dg_ref, dt_ref, w_ref, dv_ref, *rest):
        o_ref = rest[n]
        i = pl.program_id(0)
        if n:
            copies = _peer_copies(rest[:n], rest[n + 1:2 * n + 1], rest[2 * n + 1:], ranges)

            @pl.when(i == 0)
            def _():
                for cp in copies:
                    cp.start()

        o_ref[...] = (DN_ALPHA * dv_ref[...] + _dot_nt(dg_ref[...], w_ref[:, 0:1024])
                      + _dot_nt(dxr_ref[...], w_ref[:, 1024:1536]) + _dot_nt(dt_ref[...], w_ref[:, 1536:2048]))

        if n:
            @pl.when(i == steps - 1)
            def _():
                for cp in copies:
                    cp.wait()

    rows = lambda w: pl.BlockSpec((tm, w), lambda i: (i, 0))
    res = pl.pallas_call(
        body, name="l0_dx", grid=(steps,),
        in_specs=[rows(512), rows(1024), rows(512), pl.BlockSpec((D_MODEL, 2048), lambda i: (0, 0)), rows(D_MODEL)]
        + [ANY] * n,
        out_specs=[rows(D_MODEL)] + [ANY] * n,
        out_shape=[jax.ShapeDtypeStruct((SEQ, D_MODEL), F32)] + _exchange_shapes(arrays, bcast),
        scratch_shapes=_exchange_sems(n) if n else [],
        compiler_params=_params(("arbitrary",)),
    )(dxr, dgate, dtail, w0p, dv0, *arrays, *bcast)
    return res[0], res[1:]


def _scatter_args(scatter):
    arrays = [s[0] if isinstance(s, tuple) else s for s in scatter]
    ranges = [s[1] if isinstance(s, tuple) else (0, N_DEV) for s in scatter]
    return arrays, ranges


def _exchange_shapes(scatter, bcast):
    return ([jax.ShapeDtypeStruct((N_DEV,) + a.shape[1:], a.dtype) for a in scatter]
            + [jax.ShapeDtypeStruct((N_DEV,) + a.shape, a.dtype) for a in bcast])


def _exchange_sems(n):
    return [pltpu.SemaphoreType.DMA((7 * n,)), pltpu.SemaphoreType.DMA((7 * n,)), pltpu.SemaphoreType.DMA((n,))]


class _GuardedCopy:
    def __init__(self, copy, send=None, recv=None, local=False):
        self.copy, self.send, self.recv, self.local = copy, send, recv, local

    @staticmethod
    def _run(pred, fn):
        if pred is None:
            fn()
        else:
            pl.when(pred)(fn)

    def start(self):
        self._run(self.send, self.copy.start)

    def wait(self):
        if self.local:
            self._run(self.send, self.copy.wait)
        else:
            self._run(self.send, self.copy.wait_send)
            self._run(self.recv, self.copy.wait_recv)


def _peer_copies(in_refs, out_refs, sems, ranges):
    send_sems, recv_sems, local_sems = sems
    n, ns = len(in_refs), len(ranges)
    mx, my, mc = lax.axis_index("x"), lax.axis_index("y"), lax.axis_index("c")
    me = 4 * mx + 2 * my + mc

    def src(a, slot):
        return in_refs[a].at[slot - ranges[a][0]] if a < ns else in_refs[a]

    def member(a, dev):
        if a >= ns or ranges[a] == (0, N_DEV):
            return None
        return (dev >= ranges[a][0]) & (dev < ranges[a][1])

    copies = [_GuardedCopy(pltpu.make_async_copy(src(a, me), out_refs[a].at[me], local_sems.at[a]),
                           send=member(a, me), local=True) for a in range(n)]
    for k in range(1, N_DEV):
        px, py, pc = _flip(mx, (k >> 2) & 1), _flip(my, (k >> 1) & 1), _flip(mc, k & 1)
        peer = 4 * px + 2 * py + pc
        for a in range(n):
            copies.append(_GuardedCopy(pltpu.make_async_remote_copy(
                src_ref=src(a, peer), dst_ref=out_refs[a].at[me],
                send_sem=send_sems.at[7 * a + k - 1], recv_sem=recv_sems.at[7 * a + k - 1],
                device_id=(px, py, pc), device_id_type=MESH), send=member(a, peer), recv=member(a, me)))
    return copies


def _segments(col_map, width):
    segs = []
    for lo, hi, arr, alo in col_map:
        for s in range(N_DEV):
            a, b = max(lo, s * width), min(hi, (s + 1) * width)
            if a < b:
                segs.append((s, a - s * width, b - a, arr, alo + a - lo))
    return segs


COPY_ROWS = 256


def _unshard(g8, col_map, widths, name):
    _, r, w = g8.shape
    rb = min(r, COPY_ROWS)
    segs = _segments(col_map, w)

    def body(g_ref, *o_refs):
        for o_ref in o_refs:
            o_ref[...] = jnp.zeros_like(o_ref)
        for s, llo, n, arr, alo in segs:
            o_refs[arr][:, alo:alo + n] = g_ref[s, :, llo:llo + n]

    return pl.pallas_call(
        body, name=name, grid=(r // rb,),
        in_specs=[pl.BlockSpec((N_DEV, rb, w), lambda i: (0, i, 0))],
        out_specs=[pl.BlockSpec((rb, n), lambda i: (i, 0)) for n in widths],
        out_shape=[jax.ShapeDtypeStruct((r, n), g8.dtype) for n in widths],
        compiler_params=_params(("parallel",)),
    )(g8)


def _reshard(srcs, col_map, w, dtype, name, shards=(0, N_DEV)):
    r = srcs[0].shape[0]
    rb = min(r, COPY_ROWS)
    lo, hi = shards
    segs = [sg for sg in _segments(col_map, w) if lo <= sg[0] < hi]

    def body(*refs):
        o_ref = refs[-1]
        for s, llo, n, arr, alo in segs:
            o_ref[s - lo, :, llo:llo + n] = refs[arr][:, alo:alo + n].astype(dtype)

    return pl.pallas_call(
        body, name=name, grid=(r // rb,),
        in_specs=[pl.BlockSpec((rb, a.shape[1]), lambda i: (i, 0)) for a in srcs],
        out_specs=pl.BlockSpec((hi - lo, rb, w), lambda i: (0, i, 0)),
        out_shape=jax.ShapeDtypeStruct((hi - lo, r, w), dtype),
        compiler_params=_params(("parallel",)),
    )(*srcs)


def _adamw(parts, w, m, v, name):
    r, c = w.shape
    tr = COPY_ROWS if r % COPY_ROWS == 0 else r

    def body(p_ref, w_ref, m_ref, v_ref, g_ref, d_ref, mo_ref, vo_ref):
        g = p_ref[0].astype(F32)
        for s in range(1, N_DEV):
            g = g + p_ref[s].astype(F32)
        g_ref[...] = g
        d_ref[...], mo_ref[...], vo_ref[...] = _adamw_math(g, w_ref[...], m_ref[...], v_ref[...])

    blk = pl.BlockSpec((tr, c), lambda i: (i, 0))
    out = jax.ShapeDtypeStruct((r, c), F32)
    return pl.pallas_call(
        body, name=name, grid=(r // tr,),
        in_specs=[pl.BlockSpec((N_DEV, tr, c), lambda i: (0, i, 0)), blk, blk, blk],
        out_specs=[blk, blk, blk, blk], out_shape=[out, out, out, out],
        compiler_params=_params(("parallel",)),
    )(parts, w, m, v)


def _adamw_math(g, w, m, v):
    mn = ADAM_B1 * m + (1.0 - ADAM_B1) * g
    vn = ADAM_B2 * v + (1.0 - ADAM_B2) * (g * g)
    m_hat = mn / (1.0 - ADAM_B1 ** ADAM_STEP)
    v_hat = vn / (1.0 - ADAM_B2 ** ADAM_STEP)
    return -ADAM_LR * (m_hat / (jnp.sqrt(v_hat) + ADAM_EPS) + ADAM_WD * w), mn, vn


SMALL = (("ab_conv_w", 0, 4, 64), ("ssd_conv_w", 4, 4, 384), ("ssd_conv_b", 8, 1, 384), ("ssd_norm", 9, 1, 256),
         ("ssd_ln_g", 10, 1, 128), ("ssd_ln_b", 11, 1, 128))
VECS = (("ab_conv_b", 512), ("ab_gate_a_b", 512), ("ab_gate_x_b", 512), ("ab_lambda", 512), ("mla_q_norm", 256),
        ("mla_kv_norm", 128), ("ab_ln_g", 1024), ("ab_ln_b", 1024), ("ssd_dt_bias", 32), ("ssd_a_log", 32),
        ("ssd_d", 32))
GATES = ("ab_gate_a_w", "ab_gate_x_w")
SMALL_NAMES = tuple(n for n, *_ in SMALL) + tuple(n for n, _ in VECS) + GATES
VMEM_WHOLE = pl.BlockSpec(memory_space=pltpu.VMEM)


def _view2d(name, a):
    if name in GATES:
        return a.reshape(RNN_W, 64)
    return a[0] if a.ndim == 3 else a


def _unshard_small(g):
    widths = (512, 3072, 3072, 2048, 1024, 1024)

    def body(*refs):
        ins, outs = refs[:6], refs[6:]
        outs[0][...] = jnp.zeros_like(outs[0])
        outs[1][...] = jnp.zeros_like(outs[1])
        for (_, _, nr, c), i_ref, o_ref in zip(SMALL, ins, outs):
            for j in range(N_DEV):
                o_ref[0:nr, j * c:(j + 1) * c] = i_ref[j]

    return pl.pallas_call(
        body, name="unshard_small", in_specs=[VMEM_WHOLE] * 6, out_specs=[VMEM_WHOLE] * 6,
        out_shape=[jax.ShapeDtypeStruct((SUBLANES if nr == 4 else 1, w), F32) for (_, _, nr, _), w in zip(SMALL, widths)],
    )(*g)


def _prep_repl(ga, gx, dt_bias, a_log, d):
    def body(ga_ref, gx_ref, b_ref, al_ref, d_ref, wa_ref, wx_ref, b128_ref, al128_ref, dx_ref):
        wa_ref[...] = jnp.zeros_like(wa_ref)
        wx_ref[...] = jnp.zeros_like(wx_ref)
        for hd in range(8):
            hs = slice(hd * 64, (hd + 1) * 64)
            wa_ref[hs, hs] = _mx(ga_ref[hs, :])
            wx_ref[hs, hs] = _mx(gx_ref[hs, :])
        b128_ref[...] = jnp.zeros_like(b128_ref)
        al128_ref[...] = jnp.zeros_like(al128_ref)
        b128_ref[:, 0:SSD_HEADS] = b_ref[...]
        al128_ref[:, 0:SSD_HEADS] = al_ref[...]
        dv = d_ref[...]
        for hd in range(SSD_HEADS):
            dx_ref[:, hd * SSD_P:(hd + 1) * SSD_P] = jnp.broadcast_to(dv[:, hd:hd + 1], (1, SSD_P))

    return pl.pallas_call(
        body, name="prep_repl", in_specs=[VMEM_WHOLE] * 5, out_specs=[VMEM_WHOLE] * 5,
        out_shape=[jax.ShapeDtypeStruct((RNN_W, RNN_W), MXU_DTYPE), jax.ShapeDtypeStruct((RNN_W, RNN_W), MXU_DTYPE),
                   jax.ShapeDtypeStruct((1, LANES), F32), jax.ShapeDtypeStruct((1, LANES), F32),
                   jax.ShapeDtypeStruct((1, SSD_INNER), F32)],
    )(ga, gx, dt_bias, a_log, d)


LOSS_ROW = 11


def _pack_small(dvec0, g_wa, g_wx, dqnw, dknw, dgb0, dvec1, dcw1, dnw, dgb1, loss8):
    def body(dvec0_ref, gwa_ref, gwx_ref, dqn_ref, dkn_ref, dgb0_ref, dvec1_ref, dcw1_ref, dnw_ref, dgb1_ref,
             loss_ref, sm_ref, vec_ref, gg_ref):
        sm_ref[...] = jnp.zeros_like(sm_ref)
        vec_ref[...] = jnp.zeros_like(vec_ref)
        sharded = ((dvec0_ref, 4), (dcw1_ref, 0), (dcw1_ref, 4), (dnw_ref, 0), (dgb1_ref, 0), (dgb1_ref, 1))
        for (_, r0, nr, c), (src, sr) in zip(SMALL, sharded):
            for j in range(N_DEV):
                sm_ref[j, r0:r0 + nr, 0:c] = src[sr:sr + nr, j * c:(j + 1) * c]
        vectors = ((dvec0_ref, 3), (dvec0_ref, 0), (dvec0_ref, 1), (dvec0_ref, 2), (dqn_ref, 0), (dkn_ref, 0),
                   (dgb0_ref, 0), (dgb0_ref, 1), (dvec1_ref, 0), (dvec1_ref, 1), (dvec1_ref, 2))
        for row, ((_, c), (src, sr)) in enumerate(zip(VECS, vectors)):
            vec_ref[row:row + 1, 0:c] = src[sr:sr + 1, 0:c]
        vec_ref[LOSS_ROW:LOSS_ROW + 1, 0:LANES] = loss_ref[0:1, :]
        for hd in range(8):
            hs = slice(hd * 64, (hd + 1) * 64)
            gg_ref[hs, 0:64] = _mx(gwa_ref[hs, hs])
            gg_ref[hs, 64:128] = _mx(gwx_ref[hs, hs])

    return pl.pallas_call(
        body, name="pack_small", in_specs=[VMEM_WHOLE] * 11, out_specs=[VMEM_WHOLE] * 3,
        out_shape=[jax.ShapeDtypeStruct((N_DEV, 16, 384), F32), jax.ShapeDtypeStruct((16, 1024), F32),
                   jax.ShapeDtypeStruct((RNN_W, LANES), MXU_DTYPE)],
    )(dvec0, g_wa, g_wx, dqnw, dknw, dgb0, dvec1, dcw1, dnw, dgb1, loss8)


def _adamw_small(recv_sm, recv_vec, recv_gg, wmv):
    plan = ([(0, r0, nr, c) for _, r0, nr, c in SMALL] + [(1, row, 1, c) for row, (_, c) in enumerate(VECS)]
            + [(2, 0, RNN_W, 0), (2, 0, RNN_W, 64)])
    n = len(plan)

    def body(*refs):
        recv, ins, outs = refs[:3], refs[3:3 + 3 * n], refs[3 + 3 * n:]
        for i, (src, r0, nr, c) in enumerate(plan):
            cols = slice(c, c + 64) if src == 2 else slice(0, c)
            g = recv[src][0, r0:r0 + nr, cols].astype(F32)
            for s in range(1, N_DEV):
                g = g + recv[src][s, r0:r0 + nr, cols].astype(F32)
            w_ref, m_ref, v_ref = ins[3 * i:3 * i + 3]
            outs[4 * i][...] = g
            outs[4 * i + 1][...], outs[4 * i + 2][...], outs[4 * i + 3][...] = _adamw_math(
                g, w_ref[...], m_ref[...], v_ref[...])
        loss = recv[1][0, LOSS_ROW:LOSS_ROW + 1, 0:LANES]
        for s in range(1, N_DEV):
            loss = loss + recv[1][s, LOSS_ROW:LOSS_ROW + 1, 0:LANES]
        outs[4 * n][...] = loss

    flat = [a for t in wmv for a in t]
    return pl.pallas_call(
        body, name="adamw_small", in_specs=[VMEM_WHOLE] * (3 + 3 * n), out_specs=[VMEM_WHOLE] * (4 * n + 1),
        out_shape=[jax.ShapeDtypeStruct(t[0].shape, F32) for t in wmv for _ in range(4)]
        + [jax.ShapeDtypeStruct((1, LANES), F32)],
    )(recv_sm, recv_vec, recv_gg, *flat)


BIG_L0 = ("ab_w_in", "ab_w_out", "mla_w_uq", "mla_w_ukv")
BIG_L1 = ("ssd_w_in", "ssd_w_out")

MAP_W0 = ((0, 512, 0, 1024), (512, 1536, 0, 0), (1536, 1920, 0, 1536), (1920, 1952, 0, 1984))
MAP_W1 = ((0, 2048, 0, 0), (2048, 5120, 1, 0), (5120, 5152, 2, 0))
MAP_WQ = tuple((96 * hd, 96 * hd + 96, 0, 128 * hd) for hd in range(8))
MAP_WKV = (tuple((128 * hd, 128 * hd + 64, 0, 128 * hd) for hd in range(8))
           + tuple((128 * hd + 64, 128 * hd + 128, 0, 1024 + 64 * hd) for hd in range(8)))
MAP_G0 = ((0, 512, 0, 0), (512, 1536, 1, 0), (1536, 1920, 2, 0), (1920, 1952, 2, 448))
W0_EARLY, W0_LATE = (0, 6), (6, 8)


def kernel(x, positions, ab_w_in, ab_conv_w, ab_conv_b, ab_gate_a_w, ab_gate_a_b, ab_gate_x_w, ab_gate_x_b, ab_lambda, mla_q_norm, mla_kv_norm, mla_w_uq, mla_w_ukv, ab_w_out, ab_ln_g, ab_ln_b, ssd_w_in, ssd_conv_w, ssd_conv_b, ssd_dt_bias, ssd_a_log, ssd_d, ssd_norm, ssd_w_out, ssd_ln_g, ssd_ln_b, loss_target, m_ab_w_in, m_ab_conv_w, m_ab_conv_b, m_ab_gate_a_w, m_ab_gate_a_b, m_ab_gate_x_w, m_ab_gate_x_b, m_ab_lambda, m_mla_q_norm, m_mla_kv_norm, m_mla_w_uq, m_mla_w_ukv, m_ab_w_out, m_ab_ln_g, m_ab_ln_b, m_ssd_w_in, m_ssd_conv_w, m_ssd_conv_b, m_ssd_dt_bias, m_ssd_a_log, m_ssd_d, m_ssd_norm, m_ssd_w_out, m_ssd_ln_g, m_ssd_ln_b, v_ab_w_in, v_ab_conv_w, v_ab_conv_b, v_ab_gate_a_w, v_ab_gate_a_b, v_ab_gate_x_w, v_ab_gate_x_b, v_ab_lambda, v_mla_q_norm, v_mla_kv_norm, v_mla_w_uq, v_mla_w_ukv, v_ab_w_out, v_ab_ln_g, v_ab_ln_b, v_ssd_w_in, v_ssd_conv_w, v_ssd_conv_b, v_ssd_dt_bias, v_ssd_a_log, v_ssd_d, v_ssd_norm, v_ssd_w_out, v_ssd_ln_g, v_ssd_ln_b):
    args = dict(locals())
    bf = MXU_DTYPE
    big = {n: [args[pre + n][0] for pre in ("", "m_", "v_")] for n in BIG_L0 + BIG_L1}
    sml = {n: [_view2d(n, args[pre + n]) for pre in ("", "m_", "v_")] for n in SMALL_NAMES}

    w0_8, cw0_8 = _all_gather([big["ab_w_in"][0].astype(bf), sml["ab_conv_w"][0]], "gather_params")
    p = {"cw0_8": cw0_8, "l0_blocks": [big[n][0].astype(bf) for n in BIG_L0[1:]] + [sml[n][0] for n, *_ in SMALL[1:]]}
    p["w0p"], = _unshard(w0_8, MAP_W0, (2048,), "unshard_w0")
    p["wa"], p["wx"], p["dt_bias"], p["a_log"], p["d_x"] = _prep_repl(
        sml["ab_gate_a_w"][0], sml["ab_gate_x_w"][0], sml["ssd_dt_bias"][0], sml["ssd_a_log"][0], sml["ssd_d"][0])
    for key, n in (("cb0", "ab_conv_b"), ("ba", "ab_gate_a_b"), ("bx", "ab_gate_x_b"), ("lam", "ab_lambda"),
                   ("qn_w", "mla_q_norm"), ("kn_w", "mla_kv_norm"), ("g0", "ab_ln_g"), ("b0", "ab_ln_b")):
        p[key] = sml[n][0]

    _, recv_early, recv, _, grad_x = _local_step(
        x[0], positions[0], loss_target[0], p, [big[n][0].astype(bf) for n in BIG_L1])

    me = 4 * lax.axis_index("x") + 2 * lax.axis_index("y") + lax.axis_index("c")
    parts = {"ssd_w_in": recv_early[0], "ssd_w_out": recv_early[1], "ab_w_out": recv_early[2],
             "ab_w_in": jnp.where(me >= W0_LATE[0], recv[0], recv_early[3]), "mla_w_uq": recv[1], "mla_w_ukv": recv[2]}

    outs = {}
    kinds = ("grad", "delta", "new_m", "new_v")
    for n in BIG_L0 + BIG_L1:
        for kind, res in zip(kinds, _adamw(parts[n], *big[n], "adamw_" + n)):
            outs[kind, n] = res[None]
    res = _adamw_small(*recv[3:], [sml[n] for n in SMALL_NAMES])
    for i, n in enumerate(SMALL_NAMES):
        for k, kind in enumerate(kinds):
            outs[kind, n] = res[4 * i + k].reshape(args[n].shape)

    loss = res[4 * len(SMALL_NAMES)][0, 0]
    order = ["ab_w_in", "ab_conv_w", "ab_conv_b", "ab_gate_a_w", "ab_gate_a_b", "ab_gate_x_w", "ab_gate_x_b",
             "ab_lambda", "mla_q_norm", "mla_kv_norm", "mla_w_uq", "mla_w_ukv", "ab_w_out", "ab_ln_g", "ab_ln_b",
             "ssd_w_in", "ssd_conv_w", "ssd_conv_b", "ssd_dt_bias", "ssd_a_log", "ssd_d", "ssd_norm", "ssd_w_out",
             "ssd_ln_g", "ssd_ln_b"]
    return (loss, grad_x[None], *[outs[kind, n] for kind in ("grad", "delta", "new_m", "new_v") for n in order])


def _local_step(x, pos, target, p, l1_blocks):
    bf = MXU_DTYPE
    inv_freq = 10000.0 ** (-jnp.arange(0, 32, 2, dtype=F32) / 32)
    ang = pos.astype(F32)[:, None] * inv_freq
    cos, sin = jnp.cos(ang), jnp.sin(ang)
    zeros = lambda n: jnp.zeros((SEQ, n), F32)
    tc = jnp.concatenate([jnp.ones((SEQ, 64), F32), cos, cos, zeros(32)], axis=1)
    tsa = jnp.concatenate([zeros(64), -sin, zeros(48)], axis=1)
    tsb = jnp.concatenate([zeros(80), sin, zeros(32)], axis=1)

    w0p, wa, wxg = (p[k] for k in ("w0p", "wa", "wx"))
    cb0, ba, bx, lam = (p[k] for k in ("cb0", "ba", "bx", "lam"))
    qn_w, kn_w, g0, b0 = (p[k] for k in ("qn_w", "kn_w", "g0", "b0"))
    dt_bias, a_log, d_x = (p[k] for k in ("dt_bias", "a_log", "d_x"))
    tril = jnp.tril(jnp.ones((SSD_L, SSD_L), F32))
    expand_t = (jnp.arange(SSD_INNER)[:, None] // SSD_P == jnp.arange(LANES)[None, :]).astype(jnp.bfloat16)

    proj0, xb, l0_8 = _l0_in(x, w0p, bcast=p["l0_blocks"])
    wo0 = l0_8[0].reshape(D_MODEL, D_MODEL)
    wq, = _unshard(l0_8[1], MAP_WQ, (1024,), "unshard_wq")
    wkv, = _unshard(l0_8[2], MAP_WKV, (1536,), "unshard_wkv")
    cw0, cw1, cb1, nw, g1, b1 = _unshard_small([p["cw0_8"]] + list(l0_8[3:]))
    xc, h = _rglru_fwd(proj0, cw0, cb0, wa, ba, wxg, bx, lam)
    qn, kn, qc, kc, vc = _mla_fwd(proj0, qn_w, kn_w, wq, wkv, tc, tsa, tsb)
    o, lse, (w1_8, wo1_8) = _flash_fwd(qc, kc, vc, bcast=l1_blocks)
    w1z, w1x, w1d = _unshard(w1_8, MAP_W1, (2048, 3072, 128), "unshard_w1")
    wo1 = wo1_8.reshape(SSD_INNER, D_MODEL)
    y0, v0, x1, x1b = _l0_out(h, o, proj0, x, wo0, g0, b0)

    z = _mm(x1b, w1z, "nn", name="l1_in_z")
    xbc = _mm(x1b, w1x, "nn", name="l1_in_xbc")
    dt_raw = _mm(x1b, w1d, "nn", name="l1_in_dt")
    pre, act = _ssd_conv_fwd(xbc, cw1, cb1)
    ys, hprev = _ssd_scan_fwd(act, dt_raw, dt_bias, a_log, d_x, tril, expand_t)
    yn, dv1, dgb1, loss8 = _l1_out(ys, z, nw, wo1, x1, g1, b1, target)

    g_wo1 = _mm(yn, dv1, "tn", name="l1_dwout")
    dys, dz, dnw = _l1_gate_bwd(dv1, wo1, ys, z, nw)
    dact, ddt_raw, dvec1 = _ssd_scan_bwd(dys, act, dt_raw, hprev, dt_bias, a_log, d_x, tril, expand_t)
    dxbc, dcw1 = _ssd_conv_bwd(dact, pre, xbc, cw1)
    g_z, g_xbc = _mm(x1b, dz, "tn", name="l1_dw_z"), _mm(x1b, dxbc, "tn", name="l1_dw_xbc")
    g_dt = _mm(x1b, ddt_raw, "tn", name="l1_dw_dt")
    dx1 = _mm(dz, w1z, "nt", name="l1_dx_z", add=dv1, add_scale=DN_ALPHA)
    dx1 = _mm(dxbc, w1x, "nt", name="l1_dx_xbc", add=dx1)

    dv0, dgb0 = _ln_bwd_call(v0, dx1, ddt_raw, w1d, g0)
    g_wo0 = _mm(y0, dv0, "tn", name="l0_dwout")
    dh, do, dgate = _gate_bwd(dv0, wo0, h, o, proj0)
    dxr, g_wa, g_wx, dvec0 = _rglru_bwd(dh, xc, h, proj0, cw0, wa, ba, wxg, bx, lam)
    g_rnn, g_gate = _mm(xb, dxr, "tn", name="l0_dw_rnn"), _mm(xb, dgate, "tn", name="l0_dw_gate")
    early = [_reshard([g_z, g_xbc, g_dt], MAP_W1, 644, bf, "reshard_w1"), g_wo1.astype(bf).reshape(N_DEV, 256, D_MODEL),
             g_wo0.astype(bf).reshape(N_DEV, 128, D_MODEL),
             (_reshard([g_rnn, g_gate], MAP_G0, 244, bf, "reshard_w0_early", shards=W0_EARLY), W0_EARLY)]
    dq, dk, dvv, recv_early = _flash_bwd(qc, kc, vc, o, do, lse, scatter=early)
    dtail, g_wq, g_wkv, dqnw, dknw = _mla_bwd(dq, dk, dvv, proj0, qn, kn, qn_w, kn_w, wq, wkv, tc, tsa, tsb)
    g_tail = _mm(xb, dtail, "tn", name="l0_dw_tail")

    acc = {"g_rnn": g_rnn, "g_gate": g_gate, "g_tail": g_tail, "g_wq": g_wq, "g_wkv": g_wkv,
           "dvec0": dvec0, "g_wa": g_wa, "g_wx": g_wx, "dqnw": dqnw, "dknw": dknw, "dgb0": dgb0, "dvec1": dvec1,
           "dcw1": dcw1, "dnw": dnw, "dgb1": dgb1}
    late = [(_reshard([g_rnn, g_gate, g_tail], MAP_G0, 244, bf, "reshard_w0_late", shards=W0_LATE), W0_LATE),
            _reshard([g_wq], MAP_WQ, 96, bf, "reshard_wq"), _reshard([g_wkv], MAP_WKV, 128, bf, "reshard_wkv")]
    sm_slots, vec_rows, gates = _pack_small(dvec0, g_wa, g_wx, dqnw, dknw, dgb0, dvec1, dcw1, dnw, dgb1, loss8)
    dx, recv_late = _l0_dx(dxr, dgate, dtail, w0p, dv0, scatter=late + [sm_slots], bcast=[vec_rows, gates])
    return acc, recv_early, recv_late, loss8[0, 0], dx
```

```python
import math

import jax
import jax.numpy as jnp
from jax import lax
from jax.experimental import pallas as pl
from jax.experimental.pallas import tpu as pltpu

F32 = jnp.float32
MXU_DTYPE = jnp.bfloat16

N_DEV = 8
SEQ = 4096
D_MODEL = 1024
DN_ALPHA = 4.0 ** 0.25
RNN_W = 512
MLA_HEADS = 8
ATT_SCALE = 96.0 ** -0.5
ATT_C = ATT_SCALE * math.log2(math.e)
RG_C = 8.0
SSD_INNER = 2048
SSD_HEADS = 32
SSD_P = 64
SSD_GROUPS = 4
SSD_N = 128
SSD_L = 128
SSD_CONV = 3072
LANES = 128
SUBLANES = 8
VMEM_LIMIT = 56 * 1024 * 1024

ADAM_LR, ADAM_B1, ADAM_B2, ADAM_EPS, ADAM_WD, ADAM_STEP = 0.001, 0.9, 0.999, 1e-08, 0.01, 10

HIGHEST = lax.Precision.HIGHEST


def _params(sem, limit=VMEM_LIMIT):
    return pltpu.CompilerParams(dimension_semantics=sem, vmem_limit_bytes=limit)


def _dot(a, b):
    return lax.dot_general(a, b, (((1,), (0,)), ((), ())), preferred_element_type=F32)


def _dot_nt(a, b):
    return lax.dot_general(a, b, (((1,), (1,)), ((), ())), preferred_element_type=F32)


def _dot_tn(a, b):
    return lax.dot_general(a, b, (((0,), (0,)), ((), ())), preferred_element_type=F32)


def _dot_hi(a, b):
    return lax.dot_general(a, b, (((1,), (0,)), ((), ())), precision=HIGHEST, preferred_element_type=F32)


def _mx(v):
    return v.astype(MXU_DTYPE)


def _sigmoid(v):
    return 1.0 / (1.0 + jnp.exp(-v))


def _log1p_pos(e):
    poly = e * (1.0 - e * (0.5 - e * (1.0 / 3.0 - e * 0.25)))
    return jnp.where(e < 0.01, poly, jnp.log(1.0 + e))


def _softplus(v):
    return jnp.maximum(v, 0.0) + _log1p_pos(jnp.exp(-jnp.abs(v)))


def _neg_expm1(v):
    poly = -v * (1.0 + v * (0.5 + v * (1.0 / 6.0 + v * (1.0 / 24.0 + v * (1.0 / 120.0)))))
    return jnp.where(jnp.abs(v) < 0.1, poly, 1.0 - jnp.exp(v))


def _silu(v):
    return v * _sigmoid(v)


def _dsilu(v):
    s = _sigmoid(v)
    return s * (1.0 + v * (1.0 - s))


def _mm(a, b, mode, *, name, add=None, add_scale=1.0, out_dtype=F32, tm=None, tn=1024, tk=512):
    if mode == "tn":
        kdim, m = a.shape
        n = b.shape[1]
        tm, tn, tk = min(tm or 1024, m), min(tn, n), min(tk, kdim)

        def body_tn(a_ref, b_ref, o_ref):
            @pl.when(pl.program_id(2) == 0)
            def _():
                o_ref[...] = jnp.zeros_like(o_ref)

            o_ref[...] += _dot_tn(_mx(a_ref[...]), _mx(b_ref[...]))

        return pl.pallas_call(
            body_tn, name=name, grid=(m // tm, n // tn, kdim // tk),
            in_specs=[pl.BlockSpec((tk, tm), lambda i, j, k: (k, i)), pl.BlockSpec((tk, tn), lambda i, j, k: (k, j))],
            out_specs=pl.BlockSpec((tm, tn), lambda i, j, k: (i, j)),
            out_shape=jax.ShapeDtypeStruct((m, n), F32),
            compiler_params=_params(("parallel", "parallel", "arbitrary")),
        )(a, b)

    m, kdim = a.shape
    n = b.shape[1] if mode == "nn" else b.shape[0]
    tm, tn = min(tm or 1024, m), min(tn, n)
    has_add = add is not None

    def body(*refs):
        a_ref, b_ref = refs[0], refs[1]
        o_ref = refs[-1]
        av, bv = _mx(a_ref[...]), _mx(b_ref[...])
        acc = _dot(av, bv) if mode == "nn" else _dot_nt(av, bv)
        if has_add:
            acc = acc + add_scale * refs[2][...]
        o_ref[...] = acc.astype(out_dtype)

    b_spec = (pl.BlockSpec((kdim, tn), lambda i, j: (0, j)) if mode == "nn"
              else pl.BlockSpec((tn, kdim), lambda i, j: (j, 0)))
    in_specs = [pl.BlockSpec((tm, kdim), lambda i, j: (i, 0)), b_spec]
    args = [a, b]
    if has_add:
        in_specs.append(pl.BlockSpec((tm, tn), lambda i, j: (i, j)))
        args.append(add)
    return pl.pallas_call(
        body, name=name, grid=(m // tm, n // tn), in_specs=in_specs,
        out_specs=pl.BlockSpec((tm, tn), lambda i, j: (i, j)),
        out_shape=jax.ShapeDtypeStruct((m, n), out_dtype),
        compiler_params=_params(("parallel", "parallel")),
    )(*args)


def _shift_down(blk, halo, s):
    if s == 0:
        return blk
    t = blk.shape[0]
    r = pltpu.roll(blk, s, 0)
    hr = pltpu.roll(halo, s, 0)
    row8 = lax.broadcasted_iota(jnp.int32, hr.shape, 0)
    head = jnp.where(row8 < s, hr, r[:SUBLANES])
    return jnp.concatenate([head, r[SUBLANES:]], axis=0) if t > SUBLANES else head


def _shift_up(blk, halo, s):
    if s == 0:
        return blk
    t = blk.shape[0]
    r = pltpu.roll(blk, t - s, 0)
    hr = pltpu.roll(halo, SUBLANES - s, 0)
    row8 = lax.broadcasted_iota(jnp.int32, hr.shape, 0)
    tail = jnp.where(row8 >= SUBLANES - s, hr, r[t - SUBLANES:])
    return jnp.concatenate([r[:t - SUBLANES], tail], axis=0) if t > SUBLANES else tail


def _scan_down(a, u):
    t = a.shape[0]
    row = lax.broadcasted_iota(jnp.int32, a.shape, 0)
    d = 1
    while d < t:
        keep = row >= d
        a_sh = jnp.where(keep, pltpu.roll(a, d, 0), 1.0)
        u_sh = jnp.where(keep, pltpu.roll(u, d, 0), 0.0)
        u = a * u_sh + u
        a = a * a_sh
        d *= 2
    return a, u


def _scan_up(a, u):
    t = a.shape[0]
    row = lax.broadcasted_iota(jnp.int32, a.shape, 0)
    d = 1
    while d < t:
        keep = row < t - d
        a_sh = jnp.where(keep, pltpu.roll(a, t - d, 0), 1.0)
        u_sh = jnp.where(keep, pltpu.roll(u, t - d, 0), 0.0)
        u = a * u_sh + u
        a = a * a_sh
        d *= 2
    return a, u


def _conv4(blk, halo, cw, cb):
    out = cb + blk * cw[3:4]
    for k in range(3):
        out = out + _shift_down(blk, halo, 3 - k) * cw[k:k + 1]
    return out


RG_T = 512
P0_RNN = 2


def _rg_gates(xc, wa, ba, wx, bx, lam):
    xcb = _mx(xc)
    r = _sigmoid(_dot(xcb, wa) + ba)
    ig = _sigmoid(_dot(xcb, wx) + bx)
    sp = _softplus(-lam)
    la = (-RG_C * r) * sp
    a = jnp.exp(la)
    mult = jnp.sqrt(_neg_expm1(2.0 * la))
    return r, ig, sp, a, mult


def _rglru_fwd(proj0, cw8, cb, wa, ba, wx, bx, lam):
    t, w = RG_T, RNN_W
    nb = SEQ // t

    def body(x_ref, halo_ref, cw_ref, cb_ref, wa_ref, ba_ref, wx_ref, bx_ref, lam_ref, xc_ref, h_ref, carry):
        i = pl.program_id(0)

        @pl.when(i == 0)
        def _():
            carry[...] = jnp.zeros_like(carry)

        blk = x_ref[...]
        halo = jnp.where(i > 0, halo_ref[...], 0.0)
        xc = _conv4(blk, halo, cw_ref[...], cb_ref[...])
        _, ig, _, a, mult = _rg_gates(xc, wa_ref[...], ba_ref[...], wx_ref[...], bx_ref[...], lam_ref[...])
        u = mult * (ig * xc)
        big_a, big_u = _scan_down(a, u)
        h = big_a * carry[SUBLANES - 1:SUBLANES, :] + big_u
        carry[...] = h[t - SUBLANES:]
        xc_ref[...] = xc
        h_ref[...] = h

    vec = pl.BlockSpec((1, w), lambda i: (0, 0))
    mat = pl.BlockSpec((w, w), lambda i: (0, 0))
    return pl.pallas_call(
        body, name="rglru_fwd", grid=(nb,),
        in_specs=[pl.BlockSpec((t, w), lambda i: (i, P0_RNN)),
                  pl.BlockSpec((SUBLANES, w), lambda i: (jnp.maximum(i * (t // SUBLANES) - 1, 0), P0_RNN)),
                  pl.BlockSpec((SUBLANES, w), lambda i: (0, 0)), vec, mat, vec, mat, vec, vec],
        out_specs=[pl.BlockSpec((t, w), lambda i: (i, 0)), pl.BlockSpec((t, w), lambda i: (i, 0))],
        out_shape=[jax.ShapeDtypeStruct((SEQ, w), F32), jax.ShapeDtypeStruct((SEQ, w), F32)],
        scratch_shapes=[pltpu.VMEM((SUBLANES, w), F32)],
        compiler_params=_params(("arbitrary",)),
    )(proj0, proj0, cw8, cb, wa, ba, wx, bx, lam)


def _rglru_bwd(dh, xc, h, proj0, cw8, wa, ba, wx, bx, lam):
    t, w = RG_T, RNN_W
    nb = SEQ // t
    tb = t // SUBLANES

    def body(dh_ref, xc_ref, h_ref, hh_ref, x_ref, cw_ref, wa_ref, ba_ref, wx_ref, bx_ref, lam_ref,
             dx_ref, dwa_ref, dwx_ref, dvec_ref, gcarry, dxc_next):
        i = pl.program_id(0)
        rev = nb - 1 - i

        @pl.when(i == 0)
        def _():
            gcarry[...] = jnp.zeros_like(gcarry)
            dxc_next[...] = jnp.zeros_like(dxc_next)
            dwa_ref[...] = jnp.zeros_like(dwa_ref)
            dwx_ref[...] = jnp.zeros_like(dwx_ref)
            dvec_ref[...] = jnp.zeros_like(dvec_ref)

        xc = xc_ref[...]
        wa_v, wx_v = wa_ref[...], wx_ref[...]
        lam_v = lam_ref[...]
        r, ig, sp, a, mult = _rg_gates(xc, wa_v, ba_ref[...], wx_v, bx_ref[...], lam_v)
        dhv = dh_ref[...]
        big_a, big_u = _scan_up(a, a * dhv)
        gg = big_a * gcarry[0:1, :] + big_u
        g = dhv + _shift_up(gg, gcarry[...], 1)
        gcarry[...] = gg[:SUBLANES]
        hhalo = jnp.where(rev > 0, hh_ref[...], 0.0)
        da = g * _shift_down(h_ref[...], hhalo, 1)
        d_mult = g * (ig * xc)
        d_i = g * (mult * xc)
        dxc = g * (mult * ig)
        d_la = da * a - d_mult * (a * a) / mult
        d_r = d_la * (-RG_C * sp)
        d_sp = jnp.sum(d_la * (-RG_C * r), axis=0, keepdims=True)
        d_pa = d_r * r * (1.0 - r)
        d_px = d_i * ig * (1.0 - ig)
        d_pab, d_pxb = _mx(d_pa), _mx(d_px)
        dxc = dxc + _dot_nt(d_pab, wa_v) + _dot_nt(d_pxb, wx_v)
        xcb = _mx(xc)
        dwa_ref[...] += _dot_tn(xcb, d_pab)
        dwx_ref[...] += _dot_tn(xcb, d_pxb)
        dvec_ref[0:1, :] += jnp.sum(d_pa, axis=0, keepdims=True)
        dvec_ref[1:2, :] += jnp.sum(d_px, axis=0, keepdims=True)
        dvec_ref[2:3, :] += d_sp * (-_sigmoid(-lam_v))
        dvec_ref[3:4, :] += jnp.sum(dxc, axis=0, keepdims=True)
        xblk = x_ref[...]
        cw = cw_ref[...]
        dx = dxc * cw[3:4]
        nxt = dxc_next[...]
        dvec_ref[7:8, :] += jnp.sum(dxc * xblk, axis=0, keepdims=True)
        for k in range(3):
            up = _shift_up(dxc, nxt, 3 - k)
            dvec_ref[4 + k:5 + k, :] += jnp.sum(up * xblk, axis=0, keepdims=True)
            dx = dx + up * cw[k:k + 1]
        dxc_next[...] = dxc[:SUBLANES]
        dx_ref[...] = _mx(dx)

    blk = pl.BlockSpec((t, w), lambda i: (nb - 1 - i, 0))
    halo = pl.BlockSpec((SUBLANES, w), lambda i: (jnp.maximum((nb - 1 - i) * tb - 1, 0), 0))
    vec = pl.BlockSpec((1, w), lambda i: (0, 0))
    mat = pl.BlockSpec((w, w), lambda i: (0, 0))
    return pl.pallas_call(
        body, name="rglru_bwd", grid=(nb,),
        in_specs=[blk, blk, blk, halo, pl.BlockSpec((t, w), lambda i: (nb - 1 - i, P0_RNN)),
                  pl.BlockSpec((SUBLANES, w), lambda i: (0, 0)), mat, vec, mat, vec, vec],
        out_specs=[blk, mat, mat, pl.BlockSpec((16, w), lambda i: (0, 0))],
        out_shape=[jax.ShapeDtypeStruct((SEQ, w), MXU_DTYPE), jax.ShapeDtypeStruct((w, w), F32),
                   jax.ShapeDtypeStruct((w, w), F32), jax.ShapeDtypeStruct((16, w), F32)],
        scratch_shapes=[pltpu.VMEM((SUBLANES, w), F32), pltpu.VMEM((SUBLANES, w), F32)],
        compiler_params=_params(("arbitrary",)),
    )(dh, xc, h, h, proj0, cw8, wa, ba, wx, bx, lam)


MLA_T = 512


def _rope(v, c, sa, sb):
    return v * c + pltpu.roll(v, LANES - 16, 1) * sa + pltpu.roll(v, 16, 1) * sb


def _rope_t(dv, c, sa, sb):
    return dv * c + pltpu.roll(dv * sa, 16, 1) + pltpu.roll(dv * sb, LANES - 16, 1)


def _rms(v, g, eps=1e-6):
    rs = lax.rsqrt(jnp.mean(v * v, axis=-1, keepdims=True) + eps)
    return v * rs * g, rs


def _mla_fwd(proj0, q_norm, kv_norm, wq, wkv, tc, tsa, tsb):
    t = MLA_T

    def body(cq_ref, ck_ref, qn_ref, kn_ref, wq_ref, wkv_ref, c_ref, sa_ref, sb_ref,
             oqn_ref, okn_ref, oq_ref, ok_ref, ov_ref):
        c, sa, sb = c_ref[...], sa_ref[...], sb_ref[...]
        ck = ck_ref[...]
        qn = _mx(_rms(cq_ref[...], qn_ref[...])[0])
        kn = _mx(_rms(ck[:, :LANES], kn_ref[...])[0])
        oqn_ref[...] = qn
        okn_ref[...] = kn
        krv = _rope(ck[:, LANES:], c, sa, sb)
        qraw = _dot(qn, wq_ref[...])
        kvraw = _dot(kn, wkv_ref[...])
        for hd in range(MLA_HEADS):
            sl = slice(hd * LANES, (hd + 1) * LANES)
            oq_ref[:, sl] = _mx(_rope(qraw[:, sl], c, sa, sb))
            ok_ref[:, sl] = _mx(kvraw[:, sl] + krv)
        ov_ref[...] = _mx(kvraw[:, 1024:])

    tab = pl.BlockSpec((t, LANES), lambda i: (i, 0))
    wide = pl.BlockSpec((t, 1024), lambda i: (i, 0))
    const = lambda shape: pl.BlockSpec(shape, lambda i: (0, 0))
    return pl.pallas_call(
        body, name="mla_fwd", grid=(SEQ // t,),
        in_specs=[pl.BlockSpec((t, 256), lambda i: (i, 6)), pl.BlockSpec((t, 256), lambda i: (i, 7)),
                  const((1, 256)), const((1, LANES)), const((256, 1024)), const((LANES, 1536)), tab, tab, tab],
        out_specs=[pl.BlockSpec((t, 256), lambda i: (i, 0)), tab, wide, wide, pl.BlockSpec((t, 512), lambda i: (i, 0))],
        out_shape=[jax.ShapeDtypeStruct((SEQ, 256), MXU_DTYPE), jax.ShapeDtypeStruct((SEQ, LANES), MXU_DTYPE),
                   jax.ShapeDtypeStruct((SEQ, 1024), MXU_DTYPE), jax.ShapeDtypeStruct((SEQ, 1024), MXU_DTYPE),
                   jax.ShapeDtypeStruct((SEQ, 512), MXU_DTYPE)],
        compiler_params=_params(("parallel",)),
    )(proj0, proj0, q_norm, kv_norm, wq, wkv, tc, tsa, tsb)


ATT_T = 1024


def _flash_fwd(q, k, v, bcast=()):
    t = ATT_T
    nb = SEQ // t

    steps = [(qi, ki) for qi in range(nb) for ki in range(qi + 1)]
    qi_tab = jnp.asarray([s[0] for s in steps], jnp.int32)
    ki_tab = jnp.asarray([s[1] for s in steps], jnp.int32)

    nx = len(bcast)

    def body(qi_ref, ki_ref, q_ref, k_ref, v_ref, *rest):
        x_refs, (o_ref, lse_ref), g_refs = rest[:nx], rest[nx:nx + 2], rest[nx + 2:2 * nx + 2]
        m_sc, acc_sc = rest[2 * nx + 2:2 * nx + 4]
        step = pl.program_id(1)
        qi, ki = qi_ref[step], ki_ref[step]
        if nx:
            copies = _peer_copies(x_refs, g_refs, rest[2 * nx + 4:], [])

            @pl.when((pl.program_id(0) == 0) & (step == 0))
            def _():
                for cp in copies:
                    cp.start()

        @pl.when(ki == 0)
        def _():
            m_sc[...] = jnp.full_like(m_sc, -jnp.inf)
            acc_sc[...] = jnp.zeros_like(acc_sc)

        def update(diagonal):
            vv = v_ref[...]
            lane_v = lax.broadcasted_iota(jnp.int32, vv.shape, 1)
            for hd in range(2):
                sl = slice(hd * LANES, (hd + 1) * LANES)
                s = _dot_nt(q_ref[:, sl], k_ref[:, sl])
                if diagonal:
                    s = jnp.where(lax.broadcasted_iota(jnp.int32, (t, t), 1)
                                  <= lax.broadcasted_iota(jnp.int32, (t, t), 0), s, -jnp.inf)
                m_prev = m_sc[hd]
                m_new = jnp.maximum(m_prev, jnp.max(s, axis=1, keepdims=True))
                p = jnp.exp2((s - m_new[:, :1]) * ATT_C)
                m_sc[hd] = m_new
                vh = jnp.where((lane_v >= hd * 64) & (lane_v < (hd + 1) * 64), vv, jnp.ones_like(vv))
                acc_sc[hd] = acc_sc[hd] * jnp.exp2((m_prev - m_new) * ATT_C) + _dot(_mx(p), vh)

        @pl.when(ki < qi)
        def _():
            update(False)

        @pl.when(ki == qi)
        def _():
            update(True)
            first = lax.broadcasted_iota(jnp.int32, (t, LANES), 1) < 64
            a0, a1 = acc_sc[0], acc_sc[1]
            l0, l1 = pltpu.roll(a0, 64, 1), pltpu.roll(a1, 64, 1)
            o_ref[...] = jnp.where(first, a0 / l0, a1 / l1)
            lse_ref[0] = jnp.where(first, m_sc[0] * ATT_SCALE + jnp.log(l0), m_sc[1] * ATT_SCALE + jnp.log(l1))

        if nx:
            @pl.when((pl.program_id(0) == 3) & (step == len(steps) - 1))
            def _():
                for cp in copies:
                    cp.wait()

    grid_spec = pltpu.PrefetchScalarGridSpec(
        num_scalar_prefetch=2, grid=(4, len(steps)),
        in_specs=[pl.BlockSpec((t, 256), lambda p, s, qt, kt: (qt[s], p)),
                  pl.BlockSpec((t, 256), lambda p, s, qt, kt: (kt[s], p)),
                  pl.BlockSpec((t, LANES), lambda p, s, qt, kt: (kt[s], p))] + [ANY] * nx,
        out_specs=[pl.BlockSpec((t, LANES), lambda p, s, qt, kt: (qt[s], p)),
                   pl.BlockSpec((1, t, LANES), lambda p, s, qt, kt: (p, qt[s], 0))] + [ANY] * nx,
        scratch_shapes=[pltpu.VMEM((2, t, LANES), F32), pltpu.VMEM((2, t, LANES), F32)]
        + (_exchange_sems(nx) if nx else []))
    res = pl.pallas_call(
        body, name="flash_fwd", grid_spec=grid_spec,
        out_shape=[jax.ShapeDtypeStruct((SEQ, 512), F32), jax.ShapeDtypeStruct((4, SEQ, LANES), F32)]
        + _exchange_shapes([], bcast),
        compiler_params=_params(("arbitrary", "arbitrary")),
    )(qi_tab, ki_tab, q, k, v, *bcast)
    return res[0], res[1], res[2:]


def _flash_bwd(q, k, v, o, do, lse, scatter=()):
    t = ATT_T
    nb = SEQ // t

    steps = [(qi, ki) for ki in range(nb) for qi in range(ki, nb)]
    qi_tab = jnp.asarray([s[0] for s in steps], jnp.int32)
    ki_tab = jnp.asarray([s[1] for s in steps], jnp.int32)
    log2e = math.log2(math.e)

    sc_arrays, sc_ranges = _scatter_args(scatter)
    nx = len(sc_arrays)

    def body(qi_ref, ki_ref, q_ref, k_ref, v_ref, o_ref, do_ref, lse_ref, *rest):
        x_refs, (dq_ref, dk_ref, dv_ref), g_refs = rest[:nx], rest[nx:nx + 3], rest[nx + 3:2 * nx + 3]
        dkt_sc, dvt_sc = rest[2 * nx + 3:2 * nx + 5]
        step = pl.program_id(1)
        qi, ki = qi_ref[step], ki_ref[step]
        if nx:
            copies = _peer_copies(x_refs, g_refs, rest[2 * nx + 5:], sc_ranges)

            @pl.when((pl.program_id(0) == 0) & (step == 0))
            def _():
                for cp in copies:
                    cp.start()

        @pl.when(step == 0)
        def _():
            dq_ref[...] = jnp.zeros_like(dq_ref)

        @pl.when(qi == ki)
        def _():
            dkt_sc[...] = jnp.zeros_like(dkt_sc)
            dvt_sc[...] = jnp.zeros_like(dvt_sc)

        def update(diagonal):
            dov, ov, vv = do_ref[...], o_ref[...], v_ref[...]
            lse2 = lse_ref[0] * log2e
            lane = lax.broadcasted_iota(jnp.int32, (t, LANES), 1)
            row_t = lax.broadcasted_iota(jnp.int32, (LANES, t), 0)
            prod = dov * ov
            do_b = _mx(dov)
            qrows = pl.ds(pl.multiple_of(qi * t, t), t)
            dvt_acc = jnp.zeros((LANES, t), F32)
            dkt_new, dq_new = [], []
            for hd in range(2):
                sl = slice(hd * LANES, (hd + 1) * LANES)
                mine = (lane >= hd * 64) & (lane < (hd + 1) * 64)
                qh, kh = q_ref[:, sl], k_ref[:, sl]
                p = jnp.exp2(_dot_nt(qh, kh) * ATT_C - lse2[:, hd * 64:hd * 64 + 1])
                if diagonal:
                    p = jnp.where(lax.broadcasted_iota(jnp.int32, (t, t), 1)
                                  <= lax.broadcasted_iota(jnp.int32, (t, t), 0), p, 0.0)
                do_h = jnp.where(mine, dov, 0.0)
                delta = jnp.sum(jnp.where(mine, prod, 0.0), axis=1, keepdims=True)
                dp = _dot_nt(_mx(do_h), vv)
                ds = _mx(p * (dp - delta) * ATT_SCALE)
                dvt_acc = dvt_acc + jnp.where((row_t >= hd * 64) & (row_t < (hd + 1) * 64), _dot_tn(do_b, _mx(p)), 0.0)
                dkt_new.append(_dot_tn(qh, ds))
                dq_new.append(_dot(ds, kh))
            for hd in range(2):
                sl = slice(hd * LANES, (hd + 1) * LANES)
                dkt_sc[sl, :] += dkt_new[hd]
                dq_ref[qrows, sl] += dq_new[hd]
            dvt_sc[...] += dvt_acc

        @pl.when(qi > ki)
        def _():
            update(False)

        @pl.when(qi == ki)
        def _():
            update(True)

        @pl.when(qi == nb - 1)
        def _():
            dk_ref[...] = dkt_sc[...].T
            dv_ref[...] = dvt_sc[...].T

        if nx:
            @pl.when((pl.program_id(0) == 3) & (step == len(steps) - 1))
            def _():
                for cp in copies:
                    cp.wait()

    qmap = lambda p, s, qt, kt: (qt[s], p)
    kmap = lambda p, s, qt, kt: (kt[s], p)
    grid_spec = pltpu.PrefetchScalarGridSpec(
        num_scalar_prefetch=2, grid=(4, len(steps)),
        in_specs=[pl.BlockSpec((t, 256), qmap), pl.BlockSpec((t, 256), kmap), pl.BlockSpec((t, LANES), kmap),
                  pl.BlockSpec((t, LANES), qmap), pl.BlockSpec((t, LANES), qmap),
                  pl.BlockSpec((1, t, LANES), lambda p, s, qt, kt: (p, qt[s], 0))] + [ANY] * nx,
        out_specs=[pl.BlockSpec((SEQ, 256), lambda p, s, qt, kt: (0, p)), pl.BlockSpec((t, 256), kmap),
                   pl.BlockSpec((t, LANES), kmap)] + [ANY] * nx,
        scratch_shapes=[pltpu.VMEM((256, t), F32), pltpu.VMEM((LANES, t), F32)] + (_exchange_sems(nx) if nx else []))
    res = pl.pallas_call(
        body, name="flash_bwd", grid_spec=grid_spec,
        out_shape=[jax.ShapeDtypeStruct((SEQ, 1024), F32), jax.ShapeDtypeStruct((SEQ, 1024), F32),
                   jax.ShapeDtypeStruct((SEQ, 512), F32)] + _exchange_shapes(sc_arrays, []),
        compiler_params=_params(("arbitrary", "arbitrary")),
    )(qi_tab, ki_tab, q, k, v, o, do, lse, *sc_arrays)
    return res[0], res[1], res[2], res[3:]


def _rms_bwd(v, g, dy, eps=1e-6):
    rs = lax.rsqrt(jnp.mean(v * v, axis=-1, keepdims=True) + eps)
    xh = v * rs
    dxh = dy * g
    dv = rs * (dxh - xh * jnp.mean(dxh * xh, axis=-1, keepdims=True))
    return dv, jnp.sum(dy * xh, axis=0, keepdims=True)


def _mla_bwd(dq, dk, dv, proj0, qlat, klat, q_norm, kv_norm, wq, wkv, tc, tsa, tsb):
    t = MLA_T

    def body(dq_ref, dk_ref, dv_ref, cq_ref, ck_ref, ql_ref, kl_ref, qn_ref, kn_ref, wq_ref, wkv_ref,
             c_ref, sa_ref, sb_ref, o_ref, gwq_ref, gwkv_ref, dgq_ref, dgk_ref, oq_ref, okv_ref):
        @pl.when(pl.program_id(0) == 0)
        def _():
            dgq_ref[...] = jnp.zeros_like(dgq_ref)
            dgk_ref[...] = jnp.zeros_like(dgk_ref)
            gwq_ref[...] = jnp.zeros_like(gwq_ref)
            gwkv_ref[...] = jnp.zeros_like(gwkv_ref)

        c, sa, sb = c_ref[...], sa_ref[...], sb_ref[...]
        lane = lax.broadcasted_iota(jnp.int32, (t, LANES), 1)
        dkr = jnp.zeros((t, LANES), F32)
        for hd in range(MLA_HEADS):
            sl = slice(hd * LANES, (hd + 1) * LANES)
            oq_ref[:, sl] = _mx(_rope_t(dq_ref[:, sl], c, sa, sb))
            dkh = dk_ref[:, sl]
            okv_ref[:, sl] = _mx(dkh)
            dkr = dkr + dkh
        okv_ref[:, 1024:] = _mx(dv_ref[...])
        dkr = _rope_t(jnp.where((lane >= 64) & (lane < 96), dkr, 0.0), c, sa, sb)
        dqraw, dkvraw = oq_ref[...], okv_ref[...]
        gwq_ref[...] += _dot_tn(ql_ref[...], dqraw)
        gwkv_ref[...] += _dot_tn(kl_ref[...], dkvraw)
        dqn = _dot_nt(dqraw, wq_ref[...])
        dkn = _dot_nt(dkvraw, wkv_ref[...])
        dcq, dgq = _rms_bwd(cq_ref[...], qn_ref[...], dqn)
        dck, dgk = _rms_bwd(ck_ref[:, :LANES], kn_ref[...], dkn)
        o_ref[:, :256] = _mx(dcq)
        o_ref[:, 256:384] = _mx(dck)
        o_ref[:, 384:] = _mx(dkr)
        dgq_ref[0:1, :] += dgq
        dgk_ref[0:1, :] += dgk

    tab = pl.BlockSpec((t, LANES), lambda i: (i, 0))
    wide = pl.BlockSpec((t, 1024), lambda i: (i, 0))
    const = lambda shape: pl.BlockSpec(shape, lambda i: (0, 0))
    return pl.pallas_call(
        body, name="mla_bwd", grid=(SEQ // t,),
        in_specs=[wide, wide, pl.BlockSpec((t, 512), lambda i: (i, 0)),
                  pl.BlockSpec((t, 256), lambda i: (i, 6)), pl.BlockSpec((t, 256), lambda i: (i, 7)),
                  pl.BlockSpec((t, 256), lambda i: (i, 0)), tab,
                  const((1, 256)), const((1, LANES)), const((256, 1024)), const((LANES, 1536)), tab, tab, tab],
        out_specs=[pl.BlockSpec((t, 512), lambda i: (i, 0)), const((256, 1024)), const((LANES, 1536)),
                   const((SUBLANES, 256)), const((SUBLANES, LANES))],
        out_shape=[jax.ShapeDtypeStruct((SEQ, 512), MXU_DTYPE), jax.ShapeDtypeStruct((256, 1024), F32),
                   jax.ShapeDtypeStruct((LANES, 1536), F32), jax.ShapeDtypeStruct((SUBLANES, 256), F32),
                   jax.ShapeDtypeStruct((SUBLANES, LANES), F32)],
        scratch_shapes=[pltpu.VMEM((t, 1024), MXU_DTYPE), pltpu.VMEM((t, 1536), MXU_DTYPE)],
        compiler_params=_params(("arbitrary",)),
    )(dq, dk, dv, proj0, proj0, qlat, klat, q_norm, kv_norm, wq, wkv, tc, tsa, tsb)


LN_T = 512


def _ln(v, g, b, eps=1e-5):
    mu = jnp.mean(v, axis=-1, keepdims=True)
    xc = v - mu
    rs = lax.rsqrt(jnp.mean(xc * xc, axis=-1, keepdims=True) + eps)
    return xc * rs * g + b


def _ln_bwd(v, g, dy, eps=1e-5):
    mu = jnp.mean(v, axis=-1, keepdims=True)
    xc = v - mu
    rs = lax.rsqrt(jnp.mean(xc * xc, axis=-1, keepdims=True) + eps)
    xh = xc * rs
    dxh = dy * g
    dv = rs * (dxh - jnp.mean(dxh, axis=-1, keepdims=True) - xh * jnp.mean(dxh * xh, axis=-1, keepdims=True))
    return dv, jnp.sum(dy * xh, axis=0, keepdims=True), jnp.sum(dy, axis=0, keepdims=True)


def _l0_out(h, o, proj0, x, w_out, g, b):
    t = LN_T

    def body(h_ref, o_ref, ga_ref, gb_ref, x_ref, w_ref, g_ref, b_ref, y_ref, v_ref, x1_ref, x1b_ref):
        y = _mx(jnp.concatenate([h_ref[...] * _silu(ga_ref[...]), o_ref[...] * _silu(gb_ref[...])], axis=1))
        v = DN_ALPHA * x_ref[...] + _dot(y, w_ref[...])
        y_ref[...] = y
        v_ref[...] = v
        x1 = _ln(v, g_ref[...], b_ref[...])
        x1_ref[...] = x1
        x1b_ref[...] = _mx(x1)

    half = pl.BlockSpec((t, 512), lambda i: (i, 0))
    full = pl.BlockSpec((t, D_MODEL), lambda i: (i, 0))
    vec = pl.BlockSpec((1, D_MODEL), lambda i: (0, 0))
    return pl.pallas_call(
        body, name="l0_out", grid=(SEQ // t,),
        in_specs=[half, half, pl.BlockSpec((t, 512), lambda i: (i, 0)), pl.BlockSpec((t, 512), lambda i: (i, 1)), full,
                  pl.BlockSpec((D_MODEL, D_MODEL), lambda i: (0, 0)), vec, vec],
        out_specs=[full, full, full, full],
        out_shape=[jax.ShapeDtypeStruct((SEQ, D_MODEL), MXU_DTYPE), jax.ShapeDtypeStruct((SEQ, D_MODEL), F32),
                   jax.ShapeDtypeStruct((SEQ, D_MODEL), F32), jax.ShapeDtypeStruct((SEQ, D_MODEL), MXU_DTYPE)],
        compiler_params=_params(("parallel",)),
    )(h, o, proj0, proj0, x, w_out, g, b)


def _ln_bwd_call(v, dy, ddt, w1d, g):
    t = LN_T

    def body(v_ref, dy_ref, ddt_ref, w_ref, g_ref, dv_ref, dgb_ref):
        @pl.when(pl.program_id(0) == 0)
        def _():
            dgb_ref[...] = jnp.zeros_like(dgb_ref)

        dy_v = dy_ref[...] + _dot_nt(_mx(ddt_ref[...]), w_ref[...])
        dv, dg, db = _ln_bwd(v_ref[...], g_ref[...], dy_v)
        dv_ref[...] = dv
        dgb_ref[0:1, :] += dg
        dgb_ref[1:2, :] += db

    full = pl.BlockSpec((t, D_MODEL), lambda i: (i, 0))
    return pl.pallas_call(
        body, name="ln_bwd", grid=(SEQ // t,),
        in_specs=[full, full, pl.BlockSpec((t, LANES), lambda i: (i, 0)), pl.BlockSpec((D_MODEL, LANES), lambda i: (0, 0)),
                  pl.BlockSpec((1, D_MODEL), lambda i: (0, 0))],
        out_specs=[full, pl.BlockSpec((SUBLANES, D_MODEL), lambda i: (0, 0))],
        out_shape=[jax.ShapeDtypeStruct((SEQ, D_MODEL), F32), jax.ShapeDtypeStruct((SUBLANES, D_MODEL), F32)],
        compiler_params=_params(("arbitrary",)),
    )(v, dy, ddt, w1d, g)


def _gate_bwd(dv0, w_out, h, o, proj0):
    t = LN_T

    def body(dv_ref, w_ref, h_ref, o_ref, ga_ref, gb_ref, dh_ref, do_ref, dg_ref):
        dy = _dot_nt(_mx(dv_ref[...]), w_ref[...])
        ga, gb, dya, dyb = ga_ref[...], gb_ref[...], dy[:, :512], dy[:, 512:]
        dh_ref[...] = dya * _silu(ga)
        do_ref[...] = dyb * _silu(gb)
        dg_ref[:, :512] = _mx(dya * h_ref[...] * _dsilu(ga))
        dg_ref[:, 512:] = _mx(dyb * o_ref[...] * _dsilu(gb))

    half = pl.BlockSpec((t, 512), lambda i: (i, 0))
    half1 = pl.BlockSpec((t, 512), lambda i: (i, 1))
    full = pl.BlockSpec((t, 1024), lambda i: (i, 0))
    return pl.pallas_call(
        body, name="gate_bwd", grid=(SEQ // t,),
        in_specs=[full, pl.BlockSpec((D_MODEL, D_MODEL), lambda i: (0, 0)), half, half, half, half1],
        out_specs=[half, half, full],
        out_shape=[jax.ShapeDtypeStruct((SEQ, 512), F32), jax.ShapeDtypeStruct((SEQ, 512), F32),
                   jax.ShapeDtypeStruct((SEQ, 1024), MXU_DTYPE)],
        compiler_params=_params(("parallel",)),
    )(dv0, w_out, h, o, proj0, proj0)


CONV_T = 1024
CONV_CB = 1024


def _ssd_conv_fwd(xbc, cw8, cb):
    t, cbk = CONV_T, CONV_CB
    tb = t // SUBLANES

    def body(x_ref, halo_ref, cw_ref, cb_ref, pre_ref, act_ref):
        halo = jnp.where(pl.program_id(1) > 0, halo_ref[...], 0.0)
        pre = _conv4(x_ref[...], halo, cw_ref[...], cb_ref[...])
        pre_ref[...] = pre
        act_ref[...] = _silu(pre)

    blk = pl.BlockSpec((t, cbk), lambda j, i: (i, j))
    return pl.pallas_call(
        body, name="ssd_conv_fwd", grid=(SSD_CONV // cbk, SEQ // t),
        in_specs=[blk, pl.BlockSpec((SUBLANES, cbk), lambda j, i: (jnp.maximum(i * tb - 1, 0), j)),
                  pl.BlockSpec((SUBLANES, cbk), lambda j, i: (0, j)), pl.BlockSpec((1, cbk), lambda j, i: (0, j))],
        out_specs=[blk, blk],
        out_shape=[jax.ShapeDtypeStruct((SEQ, SSD_CONV), F32), jax.ShapeDtypeStruct((SEQ, SSD_CONV), F32)],
        compiler_params=_params(("parallel", "parallel")),
    )(xbc, xbc, cw8, cb)


def _ssd_conv_bwd(dact, pre, xbc, cw8):
    t, cbk = CONV_T, CONV_CB
    tb = t // SUBLANES
    nb = SEQ // t

    def body(da_ref, dan_ref, pre_ref, pren_ref, x_ref, cw_ref, dx_ref, dcw_ref):
        i = pl.program_id(1)

        @pl.when(i == 0)
        def _():
            dcw_ref[...] = jnp.zeros_like(dcw_ref)

        dpre = da_ref[...] * _dsilu(pre_ref[...])
        dpre_next = jnp.where(i < nb - 1, dan_ref[...] * _dsilu(pren_ref[...]), 0.0)
        xblk = x_ref[...]
        cw = cw_ref[...]
        dx = dpre * cw[3:4]
        dcw_ref[3:4, :] += jnp.sum(dpre * xblk, axis=0, keepdims=True)
        for k in range(3):
            up = _shift_up(dpre, dpre_next, 3 - k)
            dcw_ref[k:k + 1, :] += jnp.sum(up * xblk, axis=0, keepdims=True)
            dx = dx + up * cw[k:k + 1]
        dcw_ref[4:5, :] += jnp.sum(dpre, axis=0, keepdims=True)
        dx_ref[...] = _mx(dx)

    blk = pl.BlockSpec((t, cbk), lambda j, i: (i, j))
    nxt = pl.BlockSpec((SUBLANES, cbk), lambda j, i: (jnp.minimum((i + 1) * tb, SEQ // SUBLANES - 1), j))
    acc = pl.BlockSpec((SUBLANES, cbk), lambda j, i: (0, j))
    return pl.pallas_call(
        body, name="ssd_conv_bwd", grid=(SSD_CONV // cbk, nb),
        in_specs=[blk, nxt, blk, nxt, blk, acc],
        out_specs=[blk, acc],
        out_shape=[jax.ShapeDtypeStruct((SEQ, SSD_CONV), MXU_DTYPE), jax.ShapeDtypeStruct((SUBLANES, SSD_CONV), F32)],
        compiler_params=_params(("parallel", "arbitrary")),
    )(dact, dact, pre, pre, xbc, cw8)


def _ssd_common(dt_raw, bias, alog, tril, expand_t, xs):
    lane = lax.broadcasted_iota(jnp.int32, dt_raw.shape, 1)
    dt = jnp.where(lane < SSD_HEADS, _softplus(dt_raw + bias), 0.0)
    a_neg = -jnp.exp(alog)
    cs = _dot_hi(tril, dt * a_neg)
    dt_x = _expand_heads(dt, expand_t)
    ecs_x = _expand_heads(jnp.exp(cs), expand_t)
    ds_x = _expand_heads(jnp.exp(cs[SSD_L - 1:SSD_L, :] - cs), expand_t)
    return dt, a_neg, cs, dt_x, None, xs * dt_x, ds_x, ecs_x, ecs_x[SSD_L - 1:SSD_L, :]


def _expand_heads(v, expand_t):
    hi = v.astype(jnp.bfloat16)
    lo = (v - hi.astype(F32)).astype(jnp.bfloat16)
    return _dot_nt(hi, expand_t) + _dot_nt(lo, expand_t)


def _fold_heads(v, expand_t):
    hi = v.astype(jnp.bfloat16)
    lo = (v - hi.astype(F32)).astype(jnp.bfloat16)
    return _dot(hi, expand_t) + _dot(lo, expand_t)


def _ssd_decay(cs, cs_t, hh, causal):
    seg = cs[:, hh:hh + 1] - cs_t[hh:hh + 1, :]
    return jnp.where(causal, jnp.exp(jnp.where(causal, seg, 0.0)), 0.0)


def _ssd_scan_fwd(act, dt_raw, bias, alog, d_x, tril, expand_t):
    nc = SEQ // SSD_L
    gw = SSD_INNER // SSD_GROUPS

    def body(act_ref, dt_ref, bias_ref, alog_ref, dx_ref, tril_ref, et_ref, y_ref, hp_ref, h_sc):
        @pl.when(pl.program_id(0) == 0)
        def _():
            h_sc[...] = jnp.zeros_like(h_sc)

        xs = act_ref[:, :SSD_INNER]
        _, _, cs, _, _, xdt, ds_x, ecs_x, elast = _ssd_common(
            dt_ref[...], bias_ref[...], alog_ref[...], tril_ref[...], et_ref[...], xs)
        cs_t = cs.T
        causal = (lax.broadcasted_iota(jnp.int32, (SSD_L, SSD_L), 0)
                  >= lax.broadcasted_iota(jnp.int32, (SSD_L, SSD_L), 1))
        lane = lax.broadcasted_iota(jnp.int32, (SSD_L, LANES), 1)
        xdt_b = _mx(xdt)
        xds_b = _mx(xdt * ds_x)
        hp_ref[0] = h_sc[...]
        for g in range(SSD_GROUPS):
            gs = slice(g * gw, (g + 1) * gw)
            bg = _mx(act_ref[:, SSD_INNER + g * SSD_N:SSD_INNER + (g + 1) * SSD_N])
            cg = _mx(act_ref[:, SSD_INNER + 512 + g * SSD_N:SSD_INNER + 512 + (g + 1) * SSD_N])
            cb = _dot_nt(cg, bg)
            hprev = h_sc[:, gs]
            yoff = _dot(cg, _mx(hprev)) * ecs_x[:, gs]
            h_sc[:, gs] = hprev * elast[:, gs] + _dot_tn(bg, xds_b[:, gs])
            for pr in range(4):
                ps = slice(g * gw + pr * LANES, g * gw + (pr + 1) * LANES)
                xp = xdt_b[:, ps]
                ydiag = jnp.zeros((SSD_L, LANES), F32)
                for j in range(2):
                    dm = _ssd_decay(cs, cs_t, g * 8 + pr * 2 + j, causal)
                    mine = (lane >= j * 64) & (lane < (j + 1) * 64)
                    ydiag = ydiag + _dot(_mx(cb * dm), jnp.where(mine, xp, jnp.zeros_like(xp)))
                y_ref[:, ps] = ydiag + yoff[:, pr * LANES:(pr + 1) * LANES] + dx_ref[:, ps] * xs[:, ps]

    const = lambda shape: pl.BlockSpec(shape, lambda c: (0, 0))
    return pl.pallas_call(
        body, name="ssd_scan_fwd", grid=(nc,),
        in_specs=[pl.BlockSpec((SSD_L, SSD_CONV), lambda c: (c, 0)), pl.BlockSpec((SSD_L, LANES), lambda c: (c, 0)),
                  const((1, LANES)), const((1, LANES)), const((1, SSD_INNER)), const((SSD_L, SSD_L)),
                  const((SSD_INNER, LANES))],
        out_specs=[pl.BlockSpec((SSD_L, SSD_INNER), lambda c: (c, 0)),
                   pl.BlockSpec((1, SSD_N, SSD_INNER), lambda c: (c, 0, 0))],
        out_shape=[jax.ShapeDtypeStruct((SEQ, SSD_INNER), F32), jax.ShapeDtypeStruct((nc, SSD_N, SSD_INNER), F32)],
        scratch_shapes=[pltpu.VMEM((SSD_N, SSD_INNER), F32)],
        compiler_params=_params(("arbitrary",)),
    )(act, dt_raw, bias, alog, d_x, tril, expand_t)


def _ssd_scan_bwd(dy, act, dt_raw, hprev_all, bias, alog, d_x, tril, expand_t):
    nc = SEQ // SSD_L
    gw = SSD_INNER // SSD_GROUPS

    def body(dy_ref, act_ref, dt_ref, hp_ref, bias_ref, alog_ref, dx_ref, tril_ref, et_ref,
             dact_ref, ddt_ref, dvec_ref, dh_sc, dd_sc):
        i = pl.program_id(0)

        @pl.when(i == 0)
        def _():
            dh_sc[...] = jnp.zeros_like(dh_sc)
            dd_sc[...] = jnp.zeros_like(dd_sc)
            dvec_ref[...] = jnp.zeros_like(dvec_ref)

        xs = act_ref[:, :SSD_INNER]
        dt_raw_v, bias_v = dt_ref[...], bias_ref[...]
        dt, a_neg, cs, dt_x, _, xdt, ds_x, ecs_x, elast = _ssd_common(
            dt_raw_v, bias_v, alog_ref[...], tril_ref[...], et_ref[...], xs)
        cs_t = cs.T
        rowi = lax.broadcasted_iota(jnp.int32, (SSD_L, SSD_L), 0)
        coli = lax.broadcasted_iota(jnp.int32, (SSD_L, SSD_L), 1)
        causal = rowi >= coli
        lane = lax.broadcasted_iota(jnp.int32, (SSD_L, LANES), 1)
        row_g = lax.broadcasted_iota(jnp.int32, (SSD_L, gw), 0)
        dyv = dy_ref[...]
        dd_sc[0:1, :] += jnp.sum(dyv * xs, axis=0, keepdims=True)
        xdt_b = _mx(xdt)
        xds = xdt * ds_x
        xds_b = _mx(xds)
        dy_b = _mx(dyv)
        dye_b = _mx(dyv * ecs_x)
        dcs = jnp.zeros((SSD_L, LANES), F32)
        dcs_t = jnp.zeros((LANES, SSD_L), F32)
        dcs_parts = []
        dxdt_parts = []
        for g in range(SSD_GROUPS):
            gs = slice(g * gw, (g + 1) * gw)
            bcol = slice(SSD_INNER + g * SSD_N, SSD_INNER + (g + 1) * SSD_N)
            ccol = slice(SSD_INNER + 512 + g * SSD_N, SSD_INNER + 512 + (g + 1) * SSD_N)
            bg, cg = _mx(act_ref[:, bcol]), _mx(act_ref[:, ccol])
            cb = _dot_nt(cg, bg)
            hp = hp_ref[0, :, gs]
            hp_b = _mx(hp)
            dh = dh_sc[:, gs]
            dh_b = _mx(dh)
            yoff = _dot(cg, hp_b) * ecs_x[:, gs]
            bdh = _dot(bg, dh_b)
            tt = xds[:, gs] * bdh
            last_row = (jnp.sum(tt, axis=0, keepdims=True)
                        + jnp.sum(dh * hp, axis=0, keepdims=True) * elast[:, gs])
            dcs_parts.append(dyv[:, gs] * yoff - tt + jnp.where(row_g == SSD_L - 1, last_row, 0.0))
            dc_g = _dot_nt(dye_b[:, gs], hp_b)
            db_g = _dot_nt(xds_b[:, gs], dh_b)
            dh_sc[:, gs] = _dot_tn(cg, dye_b[:, gs]) + dh * elast[:, gs]
            wsum = jnp.zeros((SSD_L, SSD_L), F32)
            dxdt_g = []
            for pr in range(4):
                ps = slice(g * gw + pr * LANES, g * gw + (pr + 1) * LANES)
                xp, dyp = xdt_b[:, ps], dy_b[:, ps]
                dxp = jnp.zeros((SSD_L, LANES), F32)
                for j in range(2):
                    hh = g * 8 + pr * 2 + j
                    dm = _ssd_decay(cs, cs_t, hh, causal)
                    mine = (lane >= j * 64) & (lane < (j + 1) * 64)
                    dy_h = jnp.where(mine, dyp, jnp.zeros_like(dyp))
                    wd = _dot_nt(dy_h, xp) * dm
                    wsum = wsum + wd
                    gmat = wd * cb
                    dcs = dcs + jnp.where(lane == hh, jnp.sum(gmat, axis=1, keepdims=True), 0.0)
                    dcs_t = dcs_t - jnp.where(rowi == hh, jnp.sum(gmat, axis=0, keepdims=True), 0.0)
                    dxp = dxp + _dot_tn(_mx(cb * dm), dy_h)
                dxdt_g.append(dxp)
            dxdt_parts.append(jnp.concatenate(dxdt_g, axis=1) + bdh * ds_x[:, gs])
            ws_b = _mx(wsum)
            dact_ref[:, ccol] = dc_g + _dot(ws_b, bg)
            dact_ref[:, bcol] = db_g + _dot_tn(ws_b, cg)
        dxdt = jnp.concatenate(dxdt_parts, axis=1)
        dcs_x = jnp.concatenate(dcs_parts, axis=1)
        et = et_ref[...]
        dcs_tot = dcs + dcs_t.T + _fold_heads(dcs_x, et)
        da_dt = _dot_hi((coli >= rowi).astype(F32), dcs_tot)
        ddt = da_dt * a_neg + _fold_heads(dxdt * xs, et)
        ddt_raw = ddt * _sigmoid(dt_raw_v + bias_v)
        ddt_ref[...] = ddt_raw
        dvec_ref[0:1, :] += jnp.sum(ddt_raw, axis=0, keepdims=True)
        dvec_ref[1:2, :] += jnp.sum(da_dt * dt, axis=0, keepdims=True) * a_neg
        dact_ref[:, :SSD_INNER] = dyv * dx_ref[...] + dxdt * dt_x

        @pl.when(i == nc - 1)
        def _():
            dvec_ref[2:3, :] = _fold_heads(dd_sc[...], et)[0:1, :]

    const = lambda shape: pl.BlockSpec(shape, lambda c: (0, 0))
    rev = lambda c: (nc - 1 - c, 0)
    return pl.pallas_call(
        body, name="ssd_scan_bwd", grid=(nc,),
        in_specs=[pl.BlockSpec((SSD_L, SSD_INNER), rev), pl.BlockSpec((SSD_L, SSD_CONV), rev),
                  pl.BlockSpec((SSD_L, LANES), rev),
                  pl.BlockSpec((1, SSD_N, SSD_INNER), lambda c: (nc - 1 - c, 0, 0)),
                  const((1, LANES)), const((1, LANES)), const((1, SSD_INNER)), const((SSD_L, SSD_L)),
                  const((SSD_INNER, LANES))],
        out_specs=[pl.BlockSpec((SSD_L, SSD_CONV), rev), pl.BlockSpec((SSD_L, LANES), rev), const((SUBLANES, LANES))],
        out_shape=[jax.ShapeDtypeStruct((SEQ, SSD_CONV), F32), jax.ShapeDtypeStruct((SEQ, LANES), F32),
                   jax.ShapeDtypeStruct((SUBLANES, LANES), F32)],
        scratch_shapes=[pltpu.VMEM((SSD_N, SSD_INNER), F32), pltpu.VMEM((SUBLANES, SSD_INNER), F32)],
        compiler_params=_params(("arbitrary",)),
    )(dy, act, dt_raw, hprev_all, bias, alog, d_x, tril, expand_t)


L1_T = 512


def _gated_norm(y, z, nw):
    y2 = y * _silu(z)
    gw = SSD_INNER // SSD_GROUPS
    outs, xhs, rss = [], [], []
    for g in range(SSD_GROUPS):
        gs = slice(g * gw, (g + 1) * gw)
        v = y2[:, gs]
        rs = lax.rsqrt(jnp.mean(v * v, axis=-1, keepdims=True) + 1e-6)
        xhs.append(v * rs)
        rss.append(rs)
        outs.append(v * rs * nw[:, gs])
    return outs, xhs, rss


def _l1_out(y, z, nw, w_out, x1, g, b, target):
    t = L1_T

    def body(y_ref, z_ref, nw_ref, w_ref, x1_ref, g_ref, b_ref, tg_ref, yn_ref, dv_ref, dgb_ref, loss_ref):
        @pl.when(pl.program_id(0) == 0)
        def _():
            dgb_ref[...] = jnp.zeros_like(dgb_ref)
            loss_ref[...] = jnp.zeros_like(loss_ref)

        outs, _, _ = _gated_norm(y_ref[...], z_ref[...], nw_ref[...])
        yn = _mx(jnp.concatenate(outs, axis=1))
        yn_ref[...] = yn
        v = DN_ALPHA * x1_ref[...] + _dot(yn, w_ref[...])
        gv = g_ref[...]
        err = _ln(v, gv, b_ref[...]) - tg_ref[...]
        rowsum = jnp.sum(err * err, axis=1, keepdims=True)
        loss_ref[...] += 0.5 * jnp.sum(rowsum, axis=0, keepdims=True) / D_MODEL
        dv, dg, db = _ln_bwd(v, gv, err / D_MODEL)
        dv_ref[...] = dv
        dgb_ref[0:1, :] += dg
        dgb_ref[1:2, :] += db

    wide = pl.BlockSpec((t, SSD_INNER), lambda i: (i, 0))
    full = pl.BlockSpec((t, D_MODEL), lambda i: (i, 0))
    vec = pl.BlockSpec((1, D_MODEL), lambda i: (0, 0))
    return pl.pallas_call(
        body, name="l1_out", grid=(SEQ // t,),
        in_specs=[wide, wide, pl.BlockSpec((1, SSD_INNER), lambda i: (0, 0)),
                  pl.BlockSpec((SSD_INNER, D_MODEL), lambda i: (0, 0)), full, vec, vec, full],
        out_specs=[wide, full, pl.BlockSpec((SUBLANES, D_MODEL), lambda i: (0, 0)),
                   pl.BlockSpec((SUBLANES, LANES), lambda i: (0, 0))],
        out_shape=[jax.ShapeDtypeStruct((SEQ, SSD_INNER), MXU_DTYPE), jax.ShapeDtypeStruct((SEQ, D_MODEL), F32),
                   jax.ShapeDtypeStruct((SUBLANES, D_MODEL), F32), jax.ShapeDtypeStruct((SUBLANES, LANES), F32)],
        compiler_params=_params(("arbitrary",)),
    )(y, z, nw, w_out, x1, g, b, target)


def _l1_gate_bwd(dv1, w_out, y, z, nw):
    t = L1_T
    gw = SSD_INNER // SSD_GROUPS

    def body(dv_ref, w_ref, y_ref, z_ref, nw_ref, dy_ref, dz_ref, dnw_ref):
        @pl.when(pl.program_id(0) == 0)
        def _():
            dnw_ref[...] = jnp.zeros_like(dnw_ref)

        dyn = _dot_nt(_mx(dv_ref[...]), w_ref[...])
        yv, zv, nwv = y_ref[...], z_ref[...], nw_ref[...]
        _, xhs, rss = _gated_norm(yv, zv, nwv)
        sz, dsz = _silu(zv), _dsilu(zv)
        for g in range(SSD_GROUPS):
            gs = slice(g * gw, (g + 1) * gw)
            d_out = dyn[:, gs]
            xh = xhs[g]
            dnw_ref[0:1, gs] += jnp.sum(d_out * xh, axis=0, keepdims=True)
            dxh = d_out * nwv[:, gs]
            dy2 = rss[g] * (dxh - xh * jnp.mean(dxh * xh, axis=-1, keepdims=True))
            dy_ref[:, gs] = dy2 * sz[:, gs]
            dz_ref[:, gs] = _mx(dy2 * yv[:, gs] * dsz[:, gs])

    wide = pl.BlockSpec((t, SSD_INNER), lambda i: (i, 0))
    return pl.pallas_call(
        body, name="l1_gate_bwd", grid=(SEQ // t,),
        in_specs=[pl.BlockSpec((t, D_MODEL), lambda i: (i, 0)), pl.BlockSpec((SSD_INNER, D_MODEL), lambda i: (0, 0)),
                  wide, wide, pl.BlockSpec((1, SSD_INNER), lambda i: (0, 0))],
        out_specs=[wide, wide, pl.BlockSpec((SUBLANES, SSD_INNER), lambda i: (0, 0))],
        out_shape=[jax.ShapeDtypeStruct((SEQ, SSD_INNER), F32), jax.ShapeDtypeStruct((SEQ, SSD_INNER), MXU_DTYPE),
                   jax.ShapeDtypeStruct((SUBLANES, SSD_INNER), F32)],
        compiler_params=_params(("arbitrary",)),
    )(dv1, w_out, y, z, nw)


MESH = pl.DeviceIdType.MESH
ANY = pl.BlockSpec(memory_space=pl.ANY)


def _flip(v, bit):
    return 1 - v if bit else v


def _all_gather(blocks, name):
    n = len(blocks)

    def body(*refs):
        x_refs, out_refs = refs[:n], refs[n:2 * n]
        send_sems, recv_sems, local_sems = refs[2 * n:]
        mx, my, mc = lax.axis_index("x"), lax.axis_index("y"), lax.axis_index("c")
        me, sibling = (mx, my, mc), (mx, my, 1 - mc)
        chips = [(1 - mx, my), (mx, 1 - my), (1 - mx, 1 - my)]

        def copy(a, k, block, to, own=False):
            px, py, pc = block
            slot = out_refs[a].at[4 * px + 2 * py + pc]
            return pltpu.make_async_remote_copy(
                src_ref=x_refs[a] if own else slot, dst_ref=slot,
                send_sem=send_sems.at[7 * a + k], recv_sem=recv_sems.at[7 * a + k], device_id=to, device_id_type=MESH)

        mine = [pltpu.make_async_copy(x_refs[a], out_refs[a].at[4 * mx + 2 * my + mc], local_sems.at[a])
                for a in range(n)]
        first = []
        for a in range(n):
            mine[a].start()
            first.append(copy(a, 0, me, sibling, own=True))
            first += [copy(a, 1 + j, me, (*chip, mc), own=True) for j, chip in enumerate(chips)]
        for cp in first:
            cp.start()
        passed = []
        for j, chip in enumerate(chips):
            for a in range(n):
                copy(a, 1 + j, (*chip, mc), me).wait_recv()
                fwd = copy(a, 4 + j, (*chip, mc), sibling)
                fwd.start()
                passed.append(fwd)
        for a in range(n):
            copy(a, 0, sibling, me).wait_recv()
            for j, chip in enumerate(chips):
                copy(a, 4 + j, (*chip, 1 - mc), me).wait_recv()
        for cp in first + passed:
            cp.wait_send()
        for cp in mine:
            cp.wait()

    return pl.pallas_call(
        body, name=name, in_specs=[ANY] * n, out_specs=[ANY] * n,
        out_shape=[jax.ShapeDtypeStruct((N_DEV,) + b.shape, b.dtype) for b in blocks],
        scratch_shapes=[pltpu.SemaphoreType.DMA((7 * n,)), pltpu.SemaphoreType.DMA((7 * n,)),
                        pltpu.SemaphoreType.DMA((n,))],
    )(*blocks)


def _l0_in(x, w0p, bcast=()):
    n = len(bcast)
    tm, tn = 1024, 1024
    gi, gj = SEQ // tm, 2048 // tn

    def body(x_ref, w_ref, *rest):
        o_ref, xb_ref = rest[n], rest[n + 1]
        i, j = pl.program_id(0), pl.program_id(1)
        if n:
            copies = _peer_copies(rest[:n], rest[n + 2:2 * n + 2], rest[2 * n + 2:], [])

            @pl.when((i == 0) & (j == 0))
            def _():
                for cp in copies:
                    cp.start()

        xb = _mx(x_ref[...])
        xb_ref[...] = xb
        o_ref[...] = _dot(xb, w_ref[...])

        if n:
            @pl.when((i == gi - 1) & (j == gj - 1))
            def _():
                for cp in copies:
                    cp.wait()

    res = pl.pallas_call(
        body, name="l0_in", grid=(gi, gj),
        in_specs=[pl.BlockSpec((tm, D_MODEL), lambda i, j: (i, 0)), pl.BlockSpec((D_MODEL, tn), lambda i, j: (0, j))]
        + [ANY] * n,
        out_specs=[pl.BlockSpec((tm, tn), lambda i, j: (i, j)), pl.BlockSpec((tm, D_MODEL), lambda i, j: (i, 0))]
        + [ANY] * n,
        out_shape=[jax.ShapeDtypeStruct((SEQ, 2048), F32), jax.ShapeDtypeStruct((SEQ, D_MODEL), MXU_DTYPE)]
        + _exchange_shapes([], bcast),
        scratch_shapes=_exchange_sems(n) if n else [],
        compiler_params=_params(("arbitrary", "arbitrary")),
    )(x, w0p, *bcast)
    return res[0], res[1], res[2:]


def _l0_dx(dxr, dgate, dtail, w0p, dv0, scatter=(), bcast=()):
    arrays, ranges = _scatter_args(scatter)
    n = len(arrays) + len(bcast)
    tm = 1024
    steps = SEQ // tm

    def body(dxr_ref, dg_ref, dt_ref, w_ref, dv_ref, *rest):
        o_ref = rest[n]
        i = pl.program_id(0)
        if n:
            copies = _peer_copies(rest[:n], rest[n + 1:2 * n + 1], rest[2 * n + 1:], ranges)

            @pl.when(i == 0)
            def _():
                for cp in copies:
                    cp.start()

        o_ref[...] = (DN_ALPHA * dv_ref[...] + _dot_nt(dg_ref[...], w_ref[:, 0:1024])
                      + _dot_nt(dxr_ref[...], w_ref[:, 1024:1536]) + _dot_nt(dt_ref[...], w_ref[:, 1536:2048]))

        if n:
            @pl.when(i == steps - 1)
            def _():
                for cp in copies:
                    cp.wait()

    rows = lambda w: pl.BlockSpec((tm, w), lambda i: (i, 0))
    res = pl.pallas_call(
        body, name="l0_dx", grid=(steps,),
        in_specs=[rows(512), rows(1024), rows(512), pl.BlockSpec((D_MODEL, 2048), lambda i: (0, 0)), rows(D_MODEL)]
        + [ANY] * n,
        out_specs=[rows(D_MODEL)] + [ANY] * n,
        out_shape=[jax.ShapeDtypeStruct((SEQ, D_MODEL), F32)] + _exchange_shapes(arrays, bcast),
        scratch_shapes=_exchange_sems(n) if n else [],
        compiler_params=_params(("arbitrary",)),
    )(dxr, dgate, dtail, w0p, dv0, *arrays, *bcast)
    return res[0], res[1:]


def _scatter_args(scatter):
    arrays = [s[0] if isinstance(s, tuple) else s for s in scatter]
    ranges = [s[1] if isinstance(s, tuple) else (0, N_DEV) for s in scatter]
    return arrays, ranges


def _exchange_shapes(scatter, bcast):
    return ([jax.ShapeDtypeStruct((N_DEV,) + a.shape[1:], a.dtype) for a in scatter]
            + [jax.ShapeDtypeStruct((N_DEV,) + a.shape, a.dtype) for a in bcast])


def _exchange_sems(n):
    return [pltpu.SemaphoreType.DMA((7 * n,)), pltpu.SemaphoreType.DMA((7 * n,)), pltpu.SemaphoreType.DMA((n,))]


class _GuardedCopy:
    def __init__(self, copy, send=None, recv=None, local=False):
        self.copy, self.send, self.recv, self.local = copy, send, recv, local

    @staticmethod
    def _run(pred, fn):
        if pred is None:
            fn()
        else:
            pl.when(pred)(fn)

    def start(self):
        self._run(self.send, self.copy.start)

    def wait(self):
        if self.local:
            self._run(self.send, self.copy.wait)
        else:
            self._run(self.send, self.copy.wait_send)
            self._run(self.recv, self.copy.wait_recv)


def _peer_copies(in_refs, out_refs, sems, ranges):
    send_sems, recv_sems, local_sems = sems
    n, ns = len(in_refs), len(ranges)
    mx, my, mc = lax.axis_index("x"), lax.axis_index("y"), lax.axis_index("c")
    me = 4 * mx + 2 * my + mc

    def src(a, slot):
        return in_refs[a].at[slot - ranges[a][0]] if a < ns else in_refs[a]

    def member(a, dev):
        if a >= ns or ranges[a] == (0, N_DEV):
            return None
        return (dev >= ranges[a][0]) & (dev < ranges[a][1])

    copies = [_GuardedCopy(pltpu.make_async_copy(src(a, me), out_refs[a].at[me], local_sems.at[a]),
                           send=member(a, me), local=True) for a in range(n)]
    for k in range(1, N_DEV):
        px, py, pc = _flip(mx, (k >> 2) & 1), _flip(my, (k >> 1) & 1), _flip(mc, k & 1)
        peer = 4 * px + 2 * py + pc
        for a in range(n):
            copies.append(_GuardedCopy(pltpu.make_async_remote_copy(
                src_ref=src(a, peer), dst_ref=out_refs[a].at[me],
                send_sem=send_sems.at[7 * a + k - 1], recv_sem=recv_sems.at[7 * a + k - 1],
                device_id=(px, py, pc), device_id_type=MESH), send=member(a, peer), recv=member(a, me)))
    return copies


def _segments(col_map, width):
    segs = []
    for lo, hi, arr, alo in col_map:
        for s in range(N_DEV):
            a, b = max(lo, s * width), min(hi, (s + 1) * width)
            if a < b:
                segs.append((s, a - s * width, b - a, arr, alo + a - lo))
    return segs


COPY_ROWS = 256


def _unshard(g8, col_map, widths, name):
    _, r, w = g8.shape
    rb = min(r, COPY_ROWS)
    segs = _segments(col_map, w)

    def body(g_ref, *o_refs):
        for o_ref in o_refs:
            o_ref[...] = jnp.zeros_like(o_ref)
        for s, llo, n, arr, alo in segs:
            o_refs[arr][:, alo:alo + n] = g_ref[s, :, llo:llo + n]

    return pl.pallas_call(
        body, name=name, grid=(r // rb,),
        in_specs=[pl.BlockSpec((N_DEV, rb, w), lambda i: (0, i, 0))],
        out_specs=[pl.BlockSpec((rb, n), lambda i: (i, 0)) for n in widths],
        out_shape=[jax.ShapeDtypeStruct((r, n), g8.dtype) for n in widths],
        compiler_params=_params(("parallel",)),
    )(g8)


def _reshard(srcs, col_map, w, dtype, name, shards=(0, N_DEV)):
    r = srcs[0].shape[0]
    rb = min(r, COPY_ROWS)
    lo, hi = shards
    segs = [sg for sg in _segments(col_map, w) if lo <= sg[0] < hi]

    def body(*refs):
        o_ref = refs[-1]
        for s, llo, n, arr, alo in segs:
            o_ref[s - lo, :, llo:llo + n] = refs[arr][:, alo:alo + n].astype(dtype)

    return pl.pallas_call(
        body, name=name, grid=(r // rb,),
        in_specs=[pl.BlockSpec((rb, a.shape[1]), lambda i: (i, 0)) for a in srcs],
        out_specs=pl.BlockSpec((hi - lo, rb, w), lambda i: (0, i, 0)),
        out_shape=jax.ShapeDtypeStruct((hi - lo, r, w), dtype),
        compiler_params=_params(("parallel",)),
    )(*srcs)


def _adamw(parts, w, m, v, name):
    r, c = w.shape
    tr = COPY_ROWS if r % COPY_ROWS == 0 else r

    def body(p_ref, w_ref, m_ref, v_ref, g_ref, d_ref, mo_ref, vo_ref):
        g = p_ref[0].astype(F32)
        for s in range(1, N_DEV):
            g = g + p_ref[s].astype(F32)
        g_ref[...] = g
        d_ref[...], mo_ref[...], vo_ref[...] = _adamw_math(g, w_ref[...], m_ref[...], v_ref[...])

    blk = pl.BlockSpec((tr, c), lambda i: (i, 0))
    out = jax.ShapeDtypeStruct((r, c), F32)
    return pl.pallas_call(
        body, name=name, grid=(r // tr,),
        in_specs=[pl.BlockSpec((N_DEV, tr, c), lambda i: (0, i, 0)), blk, blk, blk],
        out_specs=[blk, blk, blk, blk], out_shape=[out, out, out, out],
        compiler_params=_params(("parallel",)),
    )(parts, w, m, v)


def _adamw_math(g, w, m, v):
    mn = ADAM_B1 * m + (1.0 - ADAM_B1) * g
    vn = ADAM_B2 * v + (1.0 - ADAM_B2) * (g * g)
    m_hat = mn / (1.0 - ADAM_B1 ** ADAM_STEP)
    v_hat = vn / (1.0 - ADAM_B2 ** ADAM_STEP)
    return -ADAM_LR * (m_hat / (jnp.sqrt(v_hat) + ADAM_EPS) + ADAM_WD * w), mn, vn


SMALL = (("ab_conv_w", 0, 4, 64), ("ssd_conv_w", 4, 4, 384), ("ssd_conv_b", 8, 1, 384), ("ssd_norm", 9, 1, 256),
         ("ssd_ln_g", 10, 1, 128), ("ssd_ln_b", 11, 1, 128))
VECS = (("ab_conv_b", 512), ("ab_gate_a_b", 512), ("ab_gate_x_b", 512), ("ab_lambda", 512), ("mla_q_norm", 256),
        ("mla_kv_norm", 128), ("ab_ln_g", 1024), ("ab_ln_b", 1024), ("ssd_dt_bias", 32), ("ssd_a_log", 32),
        ("ssd_d", 32))
GATES = ("ab_gate_a_w", "ab_gate_x_w")
SMALL_NAMES = tuple(n for n, *_ in SMALL) + tuple(n for n, _ in VECS) + GATES
VMEM_WHOLE = pl.BlockSpec(memory_space=pltpu.VMEM)


def _view2d(name, a):
    if name in GATES:
        return a.reshape(RNN_W, 64)
    return a[0] if a.ndim == 3 else a


def _unshard_small(g):
    widths = (512, 3072, 3072, 2048, 1024, 1024)

    def body(*refs):
        ins, outs = refs[:6], refs[6:]
        outs[0][...] = jnp.zeros_like(outs[0])
        outs[1][...] = jnp.zeros_like(outs[1])
        for (_, _, nr, c), i_ref, o_ref in zip(SMALL, ins, outs):
            for j in range(N_DEV):
                o_ref[0:nr, j * c:(j + 1) * c] = i_ref[j]

    return pl.pallas_call(
        body, name="unshard_small", in_specs=[VMEM_WHOLE] * 6, out_specs=[VMEM_WHOLE] * 6,
        out_shape=[jax.ShapeDtypeStruct((SUBLANES if nr == 4 else 1, w), F32) for (_, _, nr, _), w in zip(SMALL, widths)],
    )(*g)


def _prep_repl(ga, gx, dt_bias, a_log, d):
    def body(ga_ref, gx_ref, b_ref, al_ref, d_ref, wa_ref, wx_ref, b128_ref, al128_ref, dx_ref):
        wa_ref[...] = jnp.zeros_like(wa_ref)
        wx_ref[...] = jnp.zeros_like(wx_ref)
        for hd in range(8):
            hs = slice(hd * 64, (hd + 1) * 64)
            wa_ref[hs, hs] = _mx(ga_ref[hs, :])
            wx_ref[hs, hs] = _mx(gx_ref[hs, :])
        b128_ref[...] = jnp.zeros_like(b128_ref)
        al128_ref[...] = jnp.zeros_like(al128_ref)
        b128_ref[:, 0:SSD_HEADS] = b_ref[...]
        al128_ref[:, 0:SSD_HEADS] = al_ref[...]
        dv = d_ref[...]
        for hd in range(SSD_HEADS):
            dx_ref[:, hd * SSD_P:(hd + 1) * SSD_P] = jnp.broadcast_to(dv[:, hd:hd + 1], (1, SSD_P))

    return pl.pallas_call(
        body, name="prep_repl", in_specs=[VMEM_WHOLE] * 5, out_specs=[VMEM_WHOLE] * 5,
        out_shape=[jax.ShapeDtypeStruct((RNN_W, RNN_W), MXU_DTYPE), jax.ShapeDtypeStruct((RNN_W, RNN_W), MXU_DTYPE),
                   jax.ShapeDtypeStruct((1, LANES), F32), jax.ShapeDtypeStruct((1, LANES), F32),
                   jax.ShapeDtypeStruct((1, SSD_INNER), F32)],
    )(ga, gx, dt_bias, a_log, d)


LOSS_ROW = 11


def _pack_small(dvec0, g_wa, g_wx, dqnw, dknw, dgb0, dvec1, dcw1, dnw, dgb1, loss8):
    def body(dvec0_ref, gwa_ref, gwx_ref, dqn_ref, dkn_ref, dgb0_ref, dvec1_ref, dcw1_ref, dnw_ref, dgb1_ref,
             loss_ref, sm_ref, vec_ref, gg_ref):
        sm_ref[...] = jnp.zeros_like(sm_ref)
        vec_ref[...] = jnp.zeros_like(vec_ref)
        sharded = ((dvec0_ref, 4), (dcw1_ref, 0), (dcw1_ref, 4), (dnw_ref, 0), (dgb1_ref, 0), (dgb1_ref, 1))
        for (_, r0, nr, c), (src, sr) in zip(SMALL, sharded):
            for j in range(N_DEV):
                sm_ref[j, r0:r0 + nr, 0:c] = src[sr:sr + nr, j * c:(j + 1) * c]
        vectors = ((dvec0_ref, 3), (dvec0_ref, 0), (dvec0_ref, 1), (dvec0_ref, 2), (dqn_ref, 0), (dkn_ref, 0),
                   (dgb0_ref, 0), (dgb0_ref, 1), (dvec1_ref, 0), (dvec1_ref, 1), (dvec1_ref, 2))
        for row, ((_, c), (src, sr)) in enumerate(zip(VECS, vectors)):
            vec_ref[row:row + 1, 0:c] = src[sr:sr + 1, 0:c]
        vec_ref[LOSS_ROW:LOSS_ROW + 1, 0:LANES] = loss_ref[0:1, :]
        for hd in range(8):
            hs = slice(hd * 64, (hd + 1) * 64)
            gg_ref[hs, 0:64] = _mx(gwa_ref[hs, hs])
            gg_ref[hs, 64:128] = _mx(gwx_ref[hs, hs])

    return pl.pallas_call(
        body, name="pack_small", in_specs=[VMEM_WHOLE] * 11, out_specs=[VMEM_WHOLE] * 3,
        out_shape=[jax.ShapeDtypeStruct((N_DEV, 16, 384), F32), jax.ShapeDtypeStruct((16, 1024), F32),
                   jax.ShapeDtypeStruct((RNN_W, LANES), MXU_DTYPE)],
    )(dvec0, g_wa, g_wx, dqnw, dknw, dgb0, dvec1, dcw1, dnw, dgb1, loss8)


def _adamw_small(recv_sm, recv_vec, recv_gg, wmv):
    plan = ([(0, r0, nr, c) for _, r0, nr, c in SMALL] + [(1, row, 1, c) for row, (_, c) in enumerate(VECS)]
            + [(2, 0, RNN_W, 0), (2, 0, RNN_W, 64)])
    n = len(plan)

    def body(*refs):
        recv, ins, outs = refs[:3], refs[3:3 + 3 * n], refs[3 + 3 * n:]
        for i, (src, r0, nr, c) in enumerate(plan):
            cols = slice(c, c + 64) if src == 2 else slice(0, c)
            g = recv[src][0, r0:r0 + nr, cols].astype(F32)
            for s in range(1, N_DEV):
                g = g + recv[src][s, r0:r0 + nr, cols].astype(F32)
            w_ref, m_ref, v_ref = ins[3 * i:3 * i + 3]
            outs[4 * i][...] = g
            outs[4 * i + 1][...], outs[4 * i + 2][...], outs[4 * i + 3][...] = _adamw_math(
                g, w_ref[...], m_ref[...], v_ref[...])
        loss = recv[1][0, LOSS_ROW:LOSS_ROW + 1, 0:LANES]
        for s in range(1, N_DEV):
            loss = loss + recv[1][s, LOSS_ROW:LOSS_ROW + 1, 0:LANES]
        outs[4 * n][...] = loss

    flat = [a for t in wmv for a in t]
    return pl.pallas_call(
        body, name="adamw_small", in_specs=[VMEM_WHOLE] * (3 + 3 * n), out_specs=[VMEM_WHOLE] * (4 * n + 1),
        out_shape=[jax.ShapeDtypeStruct(t[0].shape, F32) for t in wmv for _ in range(4)]
        + [jax.ShapeDtypeStruct((1, LANES), F32)],
    )(recv_sm, recv_vec, recv_gg, *flat)


BIG_L0 = ("ab_w_in", "ab_w_out", "mla_w_uq", "mla_w_ukv")
BIG_L1 = ("ssd_w_in", "ssd_w_out")

MAP_W0 = ((0, 512, 0, 1024), (512, 1536, 0, 0), (1536, 1920, 0, 1536), (1920, 1952, 0, 1984))
MAP_W1 = ((0, 2048, 0, 0), (2048, 5120, 1, 0), (5120, 5152, 2, 0))
MAP_WQ = tuple((96 * hd, 96 * hd + 96, 0, 128 * hd) for hd in range(8))
MAP_WKV = (tuple((128 * hd, 128 * hd + 64, 0, 128 * hd) for hd in range(8))
           + tuple((128 * hd + 64, 128 * hd + 128, 0, 1024 + 64 * hd) for hd in range(8)))
MAP_G0 = ((0, 512, 0, 0), (512, 1536, 1, 0), (1536, 1920, 2, 0), (1920, 1952, 2, 448))
W0_EARLY, W0_LATE = (0, 6), (6, 8)


def kernel(x, positions, ab_w_in, ab_conv_w, ab_conv_b, ab_gate_a_w, ab_gate_a_b, ab_gate_x_w, ab_gate_x_b, ab_lambda, mla_q_norm, mla_kv_norm, mla_w_uq, mla_w_ukv, ab_w_out, ab_ln_g, ab_ln_b, ssd_w_in, ssd_conv_w, ssd_conv_b, ssd_dt_bias, ssd_a_log, ssd_d, ssd_norm, ssd_w_out, ssd_ln_g, ssd_ln_b, loss_target, m_ab_w_in, m_ab_conv_w, m_ab_conv_b, m_ab_gate_a_w, m_ab_gate_a_b, m_ab_gate_x_w, m_ab_gate_x_b, m_ab_lambda, m_mla_q_norm, m_mla_kv_norm, m_mla_w_uq, m_mla_w_ukv, m_ab_w_out, m_ab_ln_g, m_ab_ln_b, m_ssd_w_in, m_ssd_conv_w, m_ssd_conv_b, m_ssd_dt_bias, m_ssd_a_log, m_ssd_d, m_ssd_norm, m_ssd_w_out, m_ssd_ln_g, m_ssd_ln_b, v_ab_w_in, v_ab_conv_w, v_ab_conv_b, v_ab_gate_a_w, v_ab_gate_a_b, v_ab_gate_x_w, v_ab_gate_x_b, v_ab_lambda, v_mla_q_norm, v_mla_kv_norm, v_mla_w_uq, v_mla_w_ukv, v_ab_w_out, v_ab_ln_g, v_ab_ln_b, v_ssd_w_in, v_ssd_conv_w, v_ssd_conv_b, v_ssd_dt_bias, v_ssd_a_log, v_ssd_d, v_ssd_norm, v_ssd_w_out, v_ssd_ln_g, v_ssd_ln_b):
    args = dict(locals())
    bf = MXU_DTYPE
    big = {n: [args[pre + n][0] for pre in ("", "m_", "v_")] for n in BIG_L0 + BIG_L1}
    sml = {n: [_view2d(n, args[pre + n]) for pre in ("", "m_", "v_")] for n in SMALL_NAMES}

    w0_8, cw0_8 = _all_gather([big["ab_w_in"][0].astype(bf), sml["ab_conv_w"][0]], "gather_params")
    p = {"cw0_8": cw0_8, "l0_blocks": [big[n][0].astype(bf) for n in BIG_L0[1:]] + [sml[n][0] for n, *_ in SMALL[1:]]}
    p["w0p"], = _unshard(w0_8, MAP_W0, (2048,), "unshard_w0")
    p["wa"], p["wx"], p["dt_bias"], p["a_log"], p["d_x"] = _prep_repl(
        sml["ab_gate_a_w"][0], sml["ab_gate_x_w"][0], sml["ssd_dt_bias"][0], sml["ssd_a_log"][0], sml["ssd_d"][0])
    for key, n in (("cb0", "ab_conv_b"), ("ba", "ab_gate_a_b"), ("bx", "ab_gate_x_b"), ("lam", "ab_lambda"),
                   ("qn_w", "mla_q_norm"), ("kn_w", "mla_kv_norm"), ("g0", "ab_ln_g"), ("b0", "ab_ln_b")):
        p[key] = sml[n][0]

    _, recv_early, recv, _, grad_x = _local_step(
        x[0], positions[0], loss_target[0], p, [big[n][0].astype(bf) for n in BIG_L1])

    me = 4 * lax.axis_index("x") + 2 * lax.axis_index("y") + lax.axis_index("c")
    parts = {"ssd_w_in": recv_early[0], "ssd_w_out": recv_early[1], "ab_w_out": recv_early[2],
             "ab_w_in": jnp.where(me >= W0_LATE[0], recv[0], recv_early[3]), "mla_w_uq": recv[1], "mla_w_ukv": recv[2]}

    outs = {}
    kinds = ("grad", "delta", "new_m", "new_v")
    for n in BIG_L0 + BIG_L1:
        for kind, res in zip(kinds, _adamw(parts[n], *big[n], "adamw_" + n)):
            outs[kind, n] = res[None]
    res = _adamw_small(*recv[3:], [sml[n] for n in SMALL_NAMES])
    for i, n in enumerate(SMALL_NAMES):
        for k, kind in enumerate(kinds):
            outs[kind, n] = res[4 * i + k].reshape(args[n].shape)

    loss = res[4 * len(SMALL_NAMES)][0, 0]
    order = ["ab_w_in", "ab_conv_w", "ab_conv_b", "ab_gate_a_w", "ab_gate_a_b", "ab_gate_x_w", "ab_gate_x_b",
             "ab_lambda", "mla_q_norm", "mla_kv_norm", "mla_w_uq", "mla_w_ukv", "ab_w_out", "ab_ln_g", "ab_ln_b",
             "ssd_w_in", "ssd_conv_w", "ssd_conv_b", "ssd_dt_bias", "ssd_a_log", "ssd_d", "ssd_norm", "ssd_w_out",
             "ssd_ln_g", "ssd_ln_b"]
    return (loss, grad_x[None], *[outs[kind, n] for kind in ("grad", "delta", "new_m", "new_v") for n in order])


def _local_step(x, pos, target, p, l1_blocks):
    bf = MXU_DTYPE
    inv_freq = 10000.0 ** (-jnp.arange(0, 32, 2, dtype=F32) / 32)
    ang = pos.astype(F32)[:, None] * inv_freq
    cos, sin = jnp.cos(ang), jnp.sin(ang)
    zeros = lambda n: jnp.zeros((SEQ, n), F32)
    tc = jnp.concatenate([jnp.ones((SEQ, 64), F32), cos, cos, zeros(32)], axis=1)
    tsa = jnp.concatenate([zeros(64), -sin, zeros(48)], axis=1)
    tsb = jnp.concatenate([zeros(80), sin, zeros(32)], axis=1)

    w0p, wa, wxg = (p[k] for k in ("w0p", "wa", "wx"))
    cb0, ba, bx, lam = (p[k] for k in ("cb0", "ba", "bx", "lam"))
    qn_w, kn_w, g0, b0 = (p[k] for k in ("qn_w", "kn_w", "g0", "b0"))
    dt_bias, a_log, d_x = (p[k] for k in ("dt_bias", "a_log", "d_x"))
    tril = jnp.tril(jnp.ones((SSD_L, SSD_L), F32))
    expand_t = (jnp.arange(SSD_INNER)[:, None] // SSD_P == jnp.arange(LANES)[None, :]).astype(jnp.bfloat16)

    proj0, xb, l0_8 = _l0_in(x, w0p, bcast=p["l0_blocks"])
    wo0 = l0_8[0].reshape(D_MODEL, D_MODEL)
    wq, = _unshard(l0_8[1], MAP_WQ, (1024,), "unshard_wq")
    wkv, = _unshard(l0_8[2], MAP_WKV, (1536,), "unshard_wkv")
    cw0, cw1, cb1, nw, g1, b1 = _unshard_small([p["cw0_8"]] + list(l0_8[3:]))
    xc, h = _rglru_fwd(proj0, cw0, cb0, wa, ba, wxg, bx, lam)
    qn, kn, qc, kc, vc = _mla_fwd(proj0, qn_w, kn_w, wq, wkv, tc, tsa, tsb)
    o, lse, (w1_8, wo1_8) = _flash_fwd(qc, kc, vc, bcast=l1_blocks)
    w1z, w1x, w1d = _unshard(w1_8, MAP_W1, (2048, 3072, 128), "unshard_w1")
    wo1 = wo1_8.reshape(SSD_INNER, D_MODEL)
    y0, v0, x1, x1b = _l0_out(h, o, proj0, x, wo0, g0, b0)

    z = _mm(x1b, w1z, "nn", name="l1_in_z")
    xbc = _mm(x1b, w1x, "nn", name="l1_in_xbc")
    dt_raw = _mm(x1b, w1d, "nn", name="l1_in_dt")
    pre, act = _ssd_conv_fwd(xbc, cw1, cb1)
    ys, hprev = _ssd_scan_fwd(act, dt_raw, dt_bias, a_log, d_x, tril, expand_t)
    yn, dv1, dgb1, loss8 = _l1_out(ys, z, nw, wo1, x1, g1, b1, target)

    g_wo1 = _mm(yn, dv1, "tn", name="l1_dwout")
    dys, dz, dnw = _l1_gate_bwd(dv1, wo1, ys, z, nw)
    dact, ddt_raw, dvec1 = _ssd_scan_bwd(dys, act, dt_raw, hprev, dt_bias, a_log, d_x, tril, expand_t)
    dxbc, dcw1 = _ssd_conv_bwd(dact, pre, xbc, cw1)
    g_z, g_xbc = _mm(x1b, dz, "tn", name="l1_dw_z"), _mm(x1b, dxbc, "tn", name="l1_dw_xbc")
    g_dt = _mm(x1b, ddt_raw, "tn", name="l1_dw_dt")
    dx1 = _mm(dz, w1z, "nt", name="l1_dx_z", add=dv1, add_scale=DN_ALPHA)
    dx1 = _mm(dxbc, w1x, "nt", name="l1_dx_xbc", add=dx1)

    dv0, dgb0 = _ln_bwd_call(v0, dx1, ddt_raw, w1d, g0)
    g_wo0 = _mm(y0, dv0, "tn", name="l0_dwout")
    dh, do, dgate = _gate_bwd(dv0, wo0, h, o, proj0)
    dxr, g_wa, g_wx, dvec0 = _rglru_bwd(dh, xc, h, proj0, cw0, wa, ba, wxg, bx, lam)
    g_rnn, g_gate = _mm(xb, dxr, "tn", name="l0_dw_rnn"), _mm(xb, dgate, "tn", name="l0_dw_gate")
    early = [_reshard([g_z, g_xbc, g_dt], MAP_W1, 644, bf, "reshard_w1"), g_wo1.astype(bf).reshape(N_DEV, 256, D_MODEL),
             g_wo0.astype(bf).reshape(N_DEV, 128, D_MODEL),
             (_reshard([g_rnn, g_gate], MAP_G0, 244, bf, "reshard_w0_early", shards=W0_EARLY), W0_EARLY)]
    dq, dk, dvv, recv_early = _flash_bwd(qc, kc, vc, o, do, lse, scatter=early)
    dtail, g_wq, g_wkv, dqnw, dknw = _mla_bwd(dq, dk, dvv, proj0, qn, kn, qn_w, kn_w, wq, wkv, tc, tsa, tsb)
    g_tail = _mm(xb, dtail, "tn", name="l0_dw_tail")

    acc = {"g_rnn": g_rnn, "g_gate": g_gate, "g_tail": g_tail, "g_wq": g_wq, "g_wkv": g_wkv,
           "dvec0": dvec0, "g_wa": g_wa, "g_wx": g_wx, "dqnw": dqnw, "dknw": dknw, "dgb0": dgb0, "dvec1": dvec1,
           "dcw1": dcw1, "dnw": dnw, "dgb1": dgb1}
    late = [(_reshard([g_rnn, g_gate, g_tail], MAP_G0, 244, bf, "reshard_w0_late", shards=W0_LATE), W0_LATE),
            _reshard([g_wq], MAP_WQ, 96, bf, "reshard_wq"), _reshard([g_wkv], MAP_WKV, 128, bf, "reshard_wkv")]
    sm_slots, vec_rows, gates = _pack_small(dvec0, g_wa, g_wx, dqnw, dknw, dgb0, dvec1, dcw1, dnw, dgb1, loss8)
    dx, recv_late = _l0_dx(dxr, dgate, dtail, w0p, dv0, scatter=late + [sm_slots], bcast=[vec_rows, gates])
    return acc, recv_early, recv_late, loss8[0, 0], dx
```

```python
import math

import jax
import jax.numpy as jnp
from jax import lax
from jax.experimental import pallas as pl
from jax.experimental.pallas import tpu as pltpu

F32 = jnp.float32
MXU_DTYPE = jnp.bfloat16

N_DEV = 8
SEQ = 4096
D_MODEL = 1024
DN_ALPHA = 4.0 ** 0.25
RNN_W = 512
MLA_HEADS = 8
ATT_SCALE = 96.0 ** -0.5
ATT_C = ATT_SCALE * math.log2(math.e)
RG_C = 8.0
SSD_INNER = 2048
SSD_HEADS = 32
SSD_P = 64
SSD_GROUPS = 4
SSD_N = 128
SSD_L = 128
SSD_CONV = 3072
LANES = 128
SUBLANES = 8
VMEM_LIMIT = 56 * 1024 * 1024

ADAM_LR, ADAM_B1, ADAM_B2, ADAM_EPS, ADAM_WD, ADAM_STEP = 0.001, 0.9, 0.999, 1e-08, 0.01, 10

HIGHEST = lax.Precision.HIGHEST


def _params(sem, limit=VMEM_LIMIT):
    return pltpu.CompilerParams(dimension_semantics=sem, vmem_limit_bytes=limit)


def _dot(a, b):
    return lax.dot_general(a, b, (((1,), (0,)), ((), ())), preferred_element_type=F32)


def _dot_nt(a, b):
    return lax.dot_general(a, b, (((1,), (1,)), ((), ())), preferred_element_type=F32)


def _dot_tn(a, b):
    return lax.dot_general(a, b, (((0,), (0,)), ((), ())), preferred_element_type=F32)


def _dot_hi(a, b):
    return lax.dot_general(a, b, (((1,), (0,)), ((), ())), precision=HIGHEST, preferred_element_type=F32)


def _mx(v):
    return v.astype(MXU_DTYPE)


def _sigmoid(v):
    return 1.0 / (1.0 + jnp.exp(-v))


def _log1p_pos(e):
    poly = e * (1.0 - e * (0.5 - e * (1.0 / 3.0 - e * 0.25)))
    return jnp.where(e < 0.01, poly, jnp.log(1.0 + e))


def _softplus(v):
    return jnp.maximum(v, 0.0) + _log1p_pos(jnp.exp(-jnp.abs(v)))


def _neg_expm1(v):
    poly = -v * (1.0 + v * (0.5 + v * (1.0 / 6.0 + v * (1.0 / 24.0 + v * (1.0 / 120.0)))))
    return jnp.where(jnp.abs(v) < 0.1, poly, 1.0 - jnp.exp(v))


def _silu(v):
    return v * _sigmoid(v)


def _dsilu(v):
    s = _sigmoid(v)
    return s * (1.0 + v * (1.0 - s))


def _dw(a, b, *, name, tm=1024, tn=1024, tk=512):
    kdim, m = a.shape
    n = b.shape[1]
    tm, tn, tk = min(tm, m), min(tn, n), min(tk, kdim)

    def body(a_ref, b_ref, o_ref):
        @pl.when(pl.program_id(2) == 0)
        def _():
            o_ref[...] = jnp.zeros_like(o_ref)

        o_ref[...] += _dot_tn(_mx(a_ref[...]), _mx(b_ref[...]))

    return pl.pallas_call(
        body, name=name, grid=(m // tm, n // tn, kdim // tk),
        in_specs=[pl.BlockSpec((tk, tm), lambda i, j, k: (k, i)), pl.BlockSpec((tk, tn), lambda i, j, k: (k, j))],
        out_specs=pl.BlockSpec((tm, tn), lambda i, j, k: (i, j)),
        out_shape=jax.ShapeDtypeStruct((m, n), F32),
        compiler_params=_params(("parallel", "parallel", "arbitrary")),
    )(a, b)


def _shift_down(blk, halo, s):
    if s == 0:
        return blk
    t = blk.shape[0]
    r = pltpu.roll(blk, s, 0)
    hr = pltpu.roll(halo, s, 0)
    row8 = lax.broadcasted_iota(jnp.int32, hr.shape, 0)
    head = jnp.where(row8 < s, hr, r[:SUBLANES])
    return jnp.concatenate([head, r[SUBLANES:]], axis=0) if t > SUBLANES else head


def _shift_up(blk, halo, s):
    if s == 0:
        return blk
    t = blk.shape[0]
    r = pltpu.roll(blk, t - s, 0)
    hr = pltpu.roll(halo, SUBLANES - s, 0)
    row8 = lax.broadcasted_iota(jnp.int32, hr.shape, 0)
    tail = jnp.where(row8 >= SUBLANES - s, hr, r[t - SUBLANES:])
    return jnp.concatenate([r[:t - SUBLANES], tail], axis=0) if t > SUBLANES else tail


def _scan_down(a, u):
    t = a.shape[0]
    row = lax.broadcasted_iota(jnp.int32, a.shape, 0)
    d = 1
    while d < t:
        keep = row >= d
        a_sh = jnp.where(keep, pltpu.roll(a, d, 0), 1.0)
        u_sh = jnp.where(keep, pltpu.roll(u, d, 0), 0.0)
        u = a * u_sh + u
        a = a * a_sh
        d *= 2
    return a, u


def _scan_up(a, u):
    t = a.shape[0]
    row = lax.broadcasted_iota(jnp.int32, a.shape, 0)
    d = 1
    while d < t:
        keep = row < t - d
        a_sh = jnp.where(keep, pltpu.roll(a, t - d, 0), 1.0)
        u_sh = jnp.where(keep, pltpu.roll(u, t - d, 0), 0.0)
        u = a * u_sh + u
        a = a * a_sh
        d *= 2
    return a, u


def _conv4(blk, halo, cw, cb):
    out = cb + blk * cw[3:4]
    for k in range(3):
        out = out + _shift_down(blk, halo, 3 - k) * cw[k:k + 1]
    return out


RG_T = 512
P0_RNN = 2


def _rg_gates(xc, wa, ba, wx, bx, lam):
    xcb = _mx(xc)
    r = _sigmoid(_dot(xcb, wa) + ba)
    ig = _sigmoid(_dot(xcb, wx) + bx)
    sp = _softplus(-lam)
    la = (-RG_C * r) * sp
    a = jnp.exp(la)
    mult = jnp.sqrt(_neg_expm1(2.0 * la))
    return r, ig, sp, a, mult


def _rglru_fwd(proj0, cw8, cb, wa, ba, wx, bx, lam):
    t, w = RG_T, RNN_W
    nb = SEQ // t

    def body(x_ref, halo_ref, cw_ref, cb_ref, wa_ref, ba_ref, wx_ref, bx_ref, lam_ref, xc_ref, h_ref, carry):
        i = pl.program_id(0)

        @pl.when(i == 0)
        def _():
            carry[...] = jnp.zeros_like(carry)

        blk = x_ref[...]
        halo = jnp.where(i > 0, halo_ref[...], 0.0)
        xc = _conv4(blk, halo, cw_ref[...], cb_ref[...])
        _, ig, _, a, mult = _rg_gates(xc, wa_ref[...], ba_ref[...], wx_ref[...], bx_ref[...], lam_ref[...])
        u = mult * (ig * xc)
        big_a, big_u = _scan_down(a, u)
        h = big_a * carry[SUBLANES - 1:SUBLANES, :] + big_u
        carry[...] = h[t - SUBLANES:]
        xc_ref[...] = xc
        h_ref[...] = h

    vec = pl.BlockSpec((1, w), lambda i: (0, 0))
    mat = pl.BlockSpec((w, w), lambda i: (0, 0))
    return pl.pallas_call(
        body, name="rglru_fwd", grid=(nb,),
        in_specs=[pl.BlockSpec((t, w), lambda i: (i, P0_RNN)),
                  pl.BlockSpec((SUBLANES, w), lambda i: (jnp.maximum(i * (t // SUBLANES) - 1, 0), P0_RNN)),
                  pl.BlockSpec((SUBLANES, w), lambda i: (0, 0)), vec, mat, vec, mat, vec, vec],
        out_specs=[pl.BlockSpec((t, w), lambda i: (i, 0)), pl.BlockSpec((t, w), lambda i: (i, 0))],
        out_shape=[jax.ShapeDtypeStruct((SEQ, w), F32), jax.ShapeDtypeStruct((SEQ, w), F32)],
        scratch_shapes=[pltpu.VMEM((SUBLANES, w), F32)],
        compiler_params=_params(("arbitrary",)),
    )(proj0, proj0, cw8, cb, wa, ba, wx, bx, lam)


def _rglru_bwd(dh, xc, h, proj0, cw8, wa, ba, wx, bx, lam):
    t, w = RG_T, RNN_W
    nb = SEQ // t
    tb = t // SUBLANES

    def body(dh_ref, xc_ref, h_ref, hh_ref, x_ref, cw_ref, wa_ref, ba_ref, wx_ref, bx_ref, lam_ref,
             dx_ref, dwa_ref, dwx_ref, dvec_ref, gcarry, dxc_next):
        i = pl.program_id(0)
        rev = nb - 1 - i

        @pl.when(i == 0)
        def _():
            gcarry[...] = jnp.zeros_like(gcarry)
            dxc_next[...] = jnp.zeros_like(dxc_next)
            dwa_ref[...] = jnp.zeros_like(dwa_ref)
            dwx_ref[...] = jnp.zeros_like(dwx_ref)
            dvec_ref[...] = jnp.zeros_like(dvec_ref)

        xc = xc_ref[...]
        wa_v, wx_v = wa_ref[...], wx_ref[...]
        lam_v = lam_ref[...]
        r, ig, sp, a, mult = _rg_gates(xc, wa_v, ba_ref[...], wx_v, bx_ref[...], lam_v)
        dhv = dh_ref[...]
        big_a, big_u = _scan_up(a, a * dhv)
        gg = big_a * gcarry[0:1, :] + big_u
        g = dhv + _shift_up(gg, gcarry[...], 1)
        gcarry[...] = gg[:SUBLANES]
        hhalo = jnp.where(rev > 0, hh_ref[...], 0.0)
        da = g * _shift_down(h_ref[...], hhalo, 1)
        d_mult = g * (ig * xc)
        d_i = g * (mult * xc)
        dxc = g * (mult * ig)
        d_la = da * a - d_mult * (a * a) / mult
        d_r = d_la * (-RG_C * sp)
        d_sp = jnp.sum(d_la * (-RG_C * r), axis=0, keepdims=True)
        d_pa = d_r * r * (1.0 - r)
        d_px = d_i * ig * (1.0 - ig)
        d_pab, d_pxb = _mx(d_pa), _mx(d_px)
        dxc = dxc + _dot_nt(d_pab, wa_v) + _dot_nt(d_pxb, wx_v)
        xcb = _mx(xc)
        dwa_ref[...] += _dot_tn(xcb, d_pab)
        dwx_ref[...] += _dot_tn(xcb, d_pxb)
        dvec_ref[0:1, :] += jnp.sum(d_pa, axis=0, keepdims=True)
        dvec_ref[1:2, :] += jnp.sum(d_px, axis=0, keepdims=True)
        dvec_ref[2:3, :] += d_sp * (-_sigmoid(-lam_v))
        dvec_ref[3:4, :] += jnp.sum(dxc, axis=0, keepdims=True)
        xblk = x_ref[...]
        cw = cw_ref[...]
        dx = dxc * cw[3:4]
        nxt = dxc_next[...]
        dvec_ref[7:8, :] += jnp.sum(dxc * xblk, axis=0, keepdims=True)
        for k in range(3):
            up = _shift_up(dxc, nxt, 3 - k)
            dvec_ref[4 + k:5 + k, :] += jnp.sum(up * xblk, axis=0, keepdims=True)
            dx = dx + up * cw[k:k + 1]
        dxc_next[...] = dxc[:SUBLANES]
        dx_ref[...] = _mx(dx)

    blk = pl.BlockSpec((t, w), lambda i: (nb - 1 - i, 0))
    halo = pl.BlockSpec((SUBLANES, w), lambda i: (jnp.maximum((nb - 1 - i) * tb - 1, 0), 0))
    vec = pl.BlockSpec((1, w), lambda i: (0, 0))
    mat = pl.BlockSpec((w, w), lambda i: (0, 0))
    return pl.pallas_call(
        body, name="rglru_bwd", grid=(nb,),
        in_specs=[blk, blk, blk, halo, pl.BlockSpec((t, w), lambda i: (nb - 1 - i, P0_RNN)),
                  pl.BlockSpec((SUBLANES, w), lambda i: (0, 0)), mat, vec, mat, vec, vec],
        out_specs=[blk, mat, mat, pl.BlockSpec((16, w), lambda i: (0, 0))],
        out_shape=[jax.ShapeDtypeStruct((SEQ, w), MXU_DTYPE), jax.ShapeDtypeStruct((w, w), F32),
                   jax.ShapeDtypeStruct((w, w), F32), jax.ShapeDtypeStruct((16, w), F32)],
        scratch_shapes=[pltpu.VMEM((SUBLANES, w), F32), pltpu.VMEM((SUBLANES, w), F32)],
        compiler_params=_params(("arbitrary",)),
    )(dh, xc, h, h, proj0, cw8, wa, ba, wx, bx, lam)


MLA_T = 512


def _rope(v, c, sa, sb):
    return v * c + pltpu.roll(v, LANES - 16, 1) * sa + pltpu.roll(v, 16, 1) * sb


def _rope_t(dv, c, sa, sb):
    return dv * c + pltpu.roll(dv * sa, 16, 1) + pltpu.roll(dv * sb, LANES - 16, 1)


def _rms(v, g, eps=1e-6):
    rs = lax.rsqrt(jnp.mean(v * v, axis=-1, keepdims=True) + eps)
    return v * rs * g, rs


def _mla_fwd(proj0, q_norm, kv_norm, wq, wkv, tc, tsa, tsb):
    t = MLA_T

    def body(cq_ref, ck_ref, qn_ref, kn_ref, wq_ref, wkv_ref, c_ref, sa_ref, sb_ref,
             oqn_ref, okn_ref, oq_ref, ok_ref, ov_ref):
        c, sa, sb = c_ref[...], sa_ref[...], sb_ref[...]
        ck = ck_ref[...]
        qn = _mx(_rms(cq_ref[...], qn_ref[...])[0])
        kn = _mx(_rms(ck[:, :LANES], kn_ref[...])[0])
        oqn_ref[...] = qn
        okn_ref[...] = kn
        krv = _rope(ck[:, LANES:], c, sa, sb)
        qraw = _dot(qn, wq_ref[...])
        kvraw = _dot(kn, wkv_ref[...])
        for hd in range(MLA_HEADS):
            sl = slice(hd * LANES, (hd + 1) * LANES)
            oq_ref[:, sl] = _mx(_rope(qraw[:, sl], c, sa, sb))
            ok_ref[:, sl] = _mx(kvraw[:, sl] + krv)
        ov_ref[...] = _mx(kvraw[:, 1024:])

    tab = pl.BlockSpec((t, LANES), lambda i: (i, 0))
    wide = pl.BlockSpec((t, 1024), lambda i: (i, 0))
    const = lambda shape: pl.BlockSpec(shape, lambda i: (0, 0))
    return pl.pallas_call(
        body, name="mla_fwd", grid=(SEQ // t,),
        in_specs=[pl.BlockSpec((t, 256), lambda i: (i, 6)), pl.BlockSpec((t, 256), lambda i: (i, 7)),
                  const((1, 256)), const((1, LANES)), const((256, 1024)), const((LANES, 1536)), tab, tab, tab],
        out_specs=[pl.BlockSpec((t, 256), lambda i: (i, 0)), tab, wide, wide, pl.BlockSpec((t, 512), lambda i: (i, 0))],
        out_shape=[jax.ShapeDtypeStruct((SEQ, 256), MXU_DTYPE), jax.ShapeDtypeStruct((SEQ, LANES), MXU_DTYPE),
                   jax.ShapeDtypeStruct((SEQ, 1024), MXU_DTYPE), jax.ShapeDtypeStruct((SEQ, 1024), MXU_DTYPE),
                   jax.ShapeDtypeStruct((SEQ, 512), MXU_DTYPE)],
        compiler_params=_params(("parallel",)),
    )(proj0, proj0, q_norm, kv_norm, wq, wkv, tc, tsa, tsb)


ATT_T = 1024


def _flash_fwd(q, k, v, bcast=()):
    t = ATT_T
    nb = SEQ // t

    steps = [(qi, ki) for qi in range(nb) for ki in range(qi + 1)]
    qi_tab = jnp.asarray([s[0] for s in steps], jnp.int32)
    ki_tab = jnp.asarray([s[1] for s in steps], jnp.int32)

    nx = len(bcast)

    def body(qi_ref, ki_ref, q_ref, k_ref, v_ref, *rest):
        x_refs, (o_ref, lse_ref), g_refs = rest[:nx], rest[nx:nx + 2], rest[nx + 2:2 * nx + 2]
        m_sc, acc_sc = rest[2 * nx + 2:2 * nx + 4]
        step = pl.program_id(1)
        qi, ki = qi_ref[step], ki_ref[step]
        if nx:
            copies = _peer_copies(x_refs, g_refs, rest[2 * nx + 4:], [])

            @pl.when((pl.program_id(0) == 0) & (step == 0))
            def _():
                for cp in copies:
                    cp.start()

        @pl.when(ki == 0)
        def _():
            m_sc[...] = jnp.full_like(m_sc, -jnp.inf)
            acc_sc[...] = jnp.zeros_like(acc_sc)

        def update(diagonal):
            vv = v_ref[...]
            lane_v = lax.broadcasted_iota(jnp.int32, vv.shape, 1)
            for hd in range(2):
                sl = slice(hd * LANES, (hd + 1) * LANES)
                s = _dot_nt(q_ref[:, sl], k_ref[:, sl])
                if diagonal:
                    s = jnp.where(lax.broadcasted_iota(jnp.int32, (t, t), 1)
                                  <= lax.broadcasted_iota(jnp.int32, (t, t), 0), s, -jnp.inf)
                m_prev = m_sc[hd]
                m_new = jnp.maximum(m_prev, jnp.max(s, axis=1, keepdims=True))
                p = jnp.exp2((s - m_new[:, :1]) * ATT_C)
                m_sc[hd] = m_new
                vh = jnp.where((lane_v >= hd * 64) & (lane_v < (hd + 1) * 64), vv, jnp.ones_like(vv))
                acc_sc[hd] = acc_sc[hd] * jnp.exp2((m_prev - m_new) * ATT_C) + _dot(_mx(p), vh)

        @pl.when(ki < qi)
        def _():
            update(False)

        @pl.when(ki == qi)
        def _():
            update(True)
            first = lax.broadcasted_iota(jnp.int32, (t, LANES), 1) < 64
            a0, a1 = acc_sc[0], acc_sc[1]
            l0, l1 = pltpu.roll(a0, 64, 1), pltpu.roll(a1, 64, 1)
            o_ref[...] = jnp.where(first, a0 / l0, a1 / l1)
            lse_ref[0] = jnp.where(first, m_sc[0] * ATT_SCALE + jnp.log(l0), m_sc[1] * ATT_SCALE + jnp.log(l1))

        if nx:
            @pl.when((pl.program_id(0) == 3) & (step == len(steps) - 1))
            def _():
                for cp in copies:
                    cp.wait()

    grid_spec = pltpu.PrefetchScalarGridSpec(
        num_scalar_prefetch=2, grid=(4, len(steps)),
        in_specs=[pl.BlockSpec((t, 256), lambda p, s, qt, kt: (qt[s], p)),
                  pl.BlockSpec((t, 256), lambda p, s, qt, kt: (kt[s], p)),
                  pl.BlockSpec((t, LANES), lambda p, s, qt, kt: (kt[s], p))] + [ANY] * nx,
        out_specs=[pl.BlockSpec((t, LANES), lambda p, s, qt, kt: (qt[s], p)),
                   pl.BlockSpec((1, t, LANES), lambda p, s, qt, kt: (p, qt[s], 0))] + [ANY] * nx,
        scratch_shapes=[pltpu.VMEM((2, t, LANES), F32), pltpu.VMEM((2, t, LANES), F32)]
        + (_exchange_sems(nx) if nx else []))
    res = pl.pallas_call(
        body, name="flash_fwd", grid_spec=grid_spec,
        out_shape=[jax.ShapeDtypeStruct((SEQ, 512), F32), jax.ShapeDtypeStruct((4, SEQ, LANES), F32)]
        + _exchange_shapes([], bcast),
        compiler_params=_params(("arbitrary", "arbitrary")),
    )(qi_tab, ki_tab, q, k, v, *bcast)
    return res[0], res[1], res[2:]


def _flash_bwd(q, k, v, o, do, lse, scatter=()):
    t = ATT_T
    nb = SEQ // t

    steps = [(qi, ki) for ki in range(nb) for qi in range(ki, nb)]
    qi_tab = jnp.asarray([s[0] for s in steps], jnp.int32)
    ki_tab = jnp.asarray([s[1] for s in steps], jnp.int32)
    log2e = math.log2(math.e)

    sc_arrays, sc_ranges = _scatter_args(scatter)
    nx = len(sc_arrays)

    def body(qi_ref, ki_ref, q_ref, k_ref, v_ref, o_ref, do_ref, lse_ref, *rest):
        x_refs, (dq_ref, dk_ref, dv_ref), g_refs = rest[:nx], rest[nx:nx + 3], rest[nx + 3:2 * nx + 3]
        dkt_sc, dvt_sc = rest[2 * nx + 3:2 * nx + 5]
        step = pl.program_id(1)
        qi, ki = qi_ref[step], ki_ref[step]
        if nx:
            copies = _peer_copies(x_refs, g_refs, rest[2 * nx + 5:], sc_ranges)

            @pl.when((pl.program_id(0) == 0) & (step == 0))
            def _():
                for cp in copies:
                    cp.start()

        @pl.when(step == 0)
        def _():
            dq_ref[...] = jnp.zeros_like(dq_ref)

        @pl.when(qi == ki)
        def _():
            dkt_sc[...] = jnp.zeros_like(dkt_sc)
            dvt_sc[...] = jnp.zeros_like(dvt_sc)

        def update(diagonal):
            dov, ov, vv = do_ref[...], o_ref[...], v_ref[...]
            lse2 = lse_ref[0] * log2e
            lane = lax.broadcasted_iota(jnp.int32, (t, LANES), 1)
            row_t = lax.broadcasted_iota(jnp.int32, (LANES, t), 0)
            prod = dov * ov
            do_b = _mx(dov)
            qrows = pl.ds(pl.multiple_of(qi * t, t), t)
            dvt_acc = jnp.zeros((LANES, t), F32)
            dkt_new, dq_new = [], []
            for hd in range(2):
                sl = slice(hd * LANES, (hd + 1) * LANES)
                mine = (lane >= hd * 64) & (lane < (hd + 1) * 64)
                qh, kh = q_ref[:, sl], k_ref[:, sl]
                p = jnp.exp2(_dot_nt(qh, kh) * ATT_C - lse2[:, hd * 64:hd * 64 + 1])
                if diagonal:
                    p = jnp.where(lax.broadcasted_iota(jnp.int32, (t, t), 1)
                                  <= lax.broadcasted_iota(jnp.int32, (t, t), 0), p, 0.0)
                do_h = jnp.where(mine, dov, 0.0)
                delta = jnp.sum(jnp.where(mine, prod, 0.0), axis=1, keepdims=True)
                dp = _dot_nt(_mx(do_h), vv)
                ds = _mx(p * (dp - delta) * ATT_SCALE)
                dvt_acc = dvt_acc + jnp.where((row_t >= hd * 64) & (row_t < (hd + 1) * 64), _dot_tn(do_b, _mx(p)), 0.0)
                dkt_new.append(_dot_tn(qh, ds))
                dq_new.append(_dot(ds, kh))
            for hd in range(2):
                sl = slice(hd * LANES, (hd + 1) * LANES)
                dkt_sc[sl, :] += dkt_new[hd]
                dq_ref[qrows, sl] += dq_new[hd]
            dvt_sc[...] += dvt_acc

        @pl.when(qi > ki)
        def _():
            update(False)

        @pl.when(qi == ki)
        def _():
            update(True)

        @pl.when(qi == nb - 1)
        def _():
            dk_ref[...] = dkt_sc[...].T
            dv_ref[...] = dvt_sc[...].T

        if nx:
            @pl.when((pl.program_id(0) == 3) & (step == len(steps) - 1))
            def _():
                for cp in copies:
                    cp.wait()

    qmap = lambda p, s, qt, kt: (qt[s], p)
    kmap = lambda p, s, qt, kt: (kt[s], p)
    grid_spec = pltpu.PrefetchScalarGridSpec(
        num_scalar_prefetch=2, grid=(4, len(steps)),
        in_specs=[pl.BlockSpec((t, 256), qmap), pl.BlockSpec((t, 256), kmap), pl.BlockSpec((t, LANES), kmap),
                  pl.BlockSpec((t, LANES), qmap), pl.BlockSpec((t, LANES), qmap),
                  pl.BlockSpec((1, t, LANES), lambda p, s, qt, kt: (p, qt[s], 0))] + [ANY] * nx,
        out_specs=[pl.BlockSpec((SEQ, 256), lambda p, s, qt, kt: (0, p)), pl.BlockSpec((t, 256), kmap),
                   pl.BlockSpec((t, LANES), kmap)] + [ANY] * nx,
        scratch_shapes=[pltpu.VMEM((256, t), F32), pltpu.VMEM((LANES, t), F32)] + (_exchange_sems(nx) if nx else []))
    res = pl.pallas_call(
        body, name="flash_bwd", grid_spec=grid_spec,
        out_shape=[jax.ShapeDtypeStruct((SEQ, 1024), F32), jax.ShapeDtypeStruct((SEQ, 1024), F32),
                   jax.ShapeDtypeStruct((SEQ, 512), F32)] + _exchange_shapes(sc_arrays, []),
        compiler_params=_params(("arbitrary", "arbitrary")),
    )(qi_tab, ki_tab, q, k, v, o, do, lse, *sc_arrays)
    return res[0], res[1], res[2], res[3:]


def _rms_bwd(v, g, dy, eps=1e-6):
    rs = lax.rsqrt(jnp.mean(v * v, axis=-1, keepdims=True) + eps)
    xh = v * rs
    dxh = dy * g
    dv = rs * (dxh - xh * jnp.mean(dxh * xh, axis=-1, keepdims=True))
    return dv, jnp.sum(dy * xh, axis=0, keepdims=True)


def _mla_bwd(dq, dk, dv, proj0, qlat, klat, q_norm, kv_norm, wq, wkv, tc, tsa, tsb):
    t = MLA_T

    def body(dq_ref, dk_ref, dv_ref, cq_ref, ck_ref, ql_ref, kl_ref, qn_ref, kn_ref, wq_ref, wkv_ref,
             c_ref, sa_ref, sb_ref, o_ref, gwq_ref, gwkv_ref, dgq_ref, dgk_ref, oq_ref, okv_ref):
        @pl.when(pl.program_id(0) == 0)
        def _():
            dgq_ref[...] = jnp.zeros_like(dgq_ref)
            dgk_ref[...] = jnp.zeros_like(dgk_ref)
            gwq_ref[...] = jnp.zeros_like(gwq_ref)
            gwkv_ref[...] = jnp.zeros_like(gwkv_ref)

        c, sa, sb = c_ref[...], sa_ref[...], sb_ref[...]
        lane = lax.broadcasted_iota(jnp.int32, (t, LANES), 1)
        dkr = jnp.zeros((t, LANES), F32)
        for hd in range(MLA_HEADS):
            sl = slice(hd * LANES, (hd + 1) * LANES)
            oq_ref[:, sl] = _mx(_rope_t(dq_ref[:, sl], c, sa, sb))
            dkh = dk_ref[:, sl]
            okv_ref[:, sl] = _mx(dkh)
            dkr = dkr + dkh
        okv_ref[:, 1024:] = _mx(dv_ref[...])
        dkr = _rope_t(jnp.where((lane >= 64) & (lane < 96), dkr, 0.0), c, sa, sb)
        dqraw, dkvraw = oq_ref[...], okv_ref[...]
        gwq_ref[...] += _dot_tn(ql_ref[...], dqraw)
        gwkv_ref[...] += _dot_tn(kl_ref[...], dkvraw)
        dqn = _dot_nt(dqraw, wq_ref[...])
        dkn = _dot_nt(dkvraw, wkv_ref[...])
        dcq, dgq = _rms_bwd(cq_ref[...], qn_ref[...], dqn)
        dck, dgk = _rms_bwd(ck_ref[:, :LANES], kn_ref[...], dkn)
        o_ref[:, :256] = _mx(dcq)
        o_ref[:, 256:384] = _mx(dck)
        o_ref[:, 384:] = _mx(dkr)
        dgq_ref[0:1, :] += dgq
        dgk_ref[0:1, :] += dgk

    tab = pl.BlockSpec((t, LANES), lambda i: (i, 0))
    wide = pl.BlockSpec((t, 1024), lambda i: (i, 0))
    const = lambda shape: pl.BlockSpec(shape, lambda i: (0, 0))
    return pl.pallas_call(
        body, name="mla_bwd", grid=(SEQ // t,),
        in_specs=[wide, wide, pl.BlockSpec((t, 512), lambda i: (i, 0)),
                  pl.BlockSpec((t, 256), lambda i: (i, 6)), pl.BlockSpec((t, 256), lambda i: (i, 7)),
                  pl.BlockSpec((t, 256), lambda i: (i, 0)), tab,
                  const((1, 256)), const((1, LANES)), const((256, 1024)), const((LANES, 1536)), tab, tab, tab],
        out_specs=[pl.BlockSpec((t, 512), lambda i: (i, 0)), const((256, 1024)), const((LANES, 1536)),
                   const((SUBLANES, 256)), const((SUBLANES, LANES))],
        out_shape=[jax.ShapeDtypeStruct((SEQ, 512), MXU_DTYPE), jax.ShapeDtypeStruct((256, 1024), F32),
                   jax.ShapeDtypeStruct((LANES, 1536), F32), jax.ShapeDtypeStruct((SUBLANES, 256), F32),
                   jax.ShapeDtypeStruct((SUBLANES, LANES), F32)],
        scratch_shapes=[pltpu.VMEM((t, 1024), MXU_DTYPE), pltpu.VMEM((t, 1536), MXU_DTYPE)],
        compiler_params=_params(("arbitrary",)),
    )(dq, dk, dv, proj0, proj0, qlat, klat, q_norm, kv_norm, wq, wkv, tc, tsa, tsb)


LN_T = 512


def _ln(v, g, b, eps=1e-5):
    mu = jnp.mean(v, axis=-1, keepdims=True)
    xc = v - mu
    rs = lax.rsqrt(jnp.mean(xc * xc, axis=-1, keepdims=True) + eps)
    return xc * rs * g + b


def _ln_bwd(v, g, dy, eps=1e-5):
    mu = jnp.mean(v, axis=-1, keepdims=True)
    xc = v - mu
    rs = lax.rsqrt(jnp.mean(xc * xc, axis=-1, keepdims=True) + eps)
    xh = xc * rs
    dxh = dy * g
    dv = rs * (dxh - jnp.mean(dxh, axis=-1, keepdims=True) - xh * jnp.mean(dxh * xh, axis=-1, keepdims=True))
    return dv, jnp.sum(dy * xh, axis=0, keepdims=True), jnp.sum(dy, axis=0, keepdims=True)


def _l0_out(h, o, proj0, x, w_out, g, b):
    t = LN_T

    def body(h_ref, o_ref, ga_ref, gb_ref, x_ref, w_ref, g_ref, b_ref, y_ref, v_ref, x1_ref, x1b_ref):
        y = _mx(jnp.concatenate([h_ref[...] * _silu(ga_ref[...]), o_ref[...] * _silu(gb_ref[...])], axis=1))
        v = DN_ALPHA * x_ref[...] + _dot(y, w_ref[...])
        y_ref[...] = y
        v_ref[...] = v
        x1 = _ln(v, g_ref[...], b_ref[...])
        x1_ref[...] = x1
        x1b_ref[...] = _mx(x1)

    half = pl.BlockSpec((t, 512), lambda i: (i, 0))
    full = pl.BlockSpec((t, D_MODEL), lambda i: (i, 0))
    vec = pl.BlockSpec((1, D_MODEL), lambda i: (0, 0))
    return pl.pallas_call(
        body, name="l0_out", grid=(SEQ // t,),
        in_specs=[half, half, pl.BlockSpec((t, 512), lambda i: (i, 0)), pl.BlockSpec((t, 512), lambda i: (i, 1)), full,
                  pl.BlockSpec((D_MODEL, D_MODEL), lambda i: (0, 0)), vec, vec],
        out_specs=[full, full, full, full],
        out_shape=[jax.ShapeDtypeStruct((SEQ, D_MODEL), MXU_DTYPE), jax.ShapeDtypeStruct((SEQ, D_MODEL), F32),
                   jax.ShapeDtypeStruct((SEQ, D_MODEL), F32), jax.ShapeDtypeStruct((SEQ, D_MODEL), MXU_DTYPE)],
        compiler_params=_params(("parallel",)),
    )(h, o, proj0, proj0, x, w_out, g, b)


def _l1_in(x1b, w1z, w1x, w1d):
    t = LN_T

    def body(x_ref, wz_ref, wx_ref, wd_ref, z_ref, xbc_ref, dt_ref):
        xv = x_ref[...]
        z_ref[...] = _dot(xv, wz_ref[...])
        xbc_ref[...] = _dot(xv, wx_ref[...])
        dt_ref[...] = _dot(xv, wd_ref[...])

    rows = lambda w: pl.BlockSpec((t, w), lambda i: (i, 0))
    const = lambda w: pl.BlockSpec((D_MODEL, w), lambda i: (0, 0))
    return pl.pallas_call(
        body, name="l1_in", grid=(SEQ // t,),
        in_specs=[rows(D_MODEL), const(SSD_INNER), const(SSD_CONV), const(LANES)],
        out_specs=[rows(SSD_INNER), rows(SSD_CONV), rows(LANES)],
        out_shape=[jax.ShapeDtypeStruct((SEQ, SSD_INNER), F32), jax.ShapeDtypeStruct((SEQ, SSD_CONV), F32),
                   jax.ShapeDtypeStruct((SEQ, LANES), F32)],
        compiler_params=_params(("parallel",)),
    )(x1b, w1z, w1x, w1d)


def _l1_dx_ln(dz, dxbc, ddt, dv1, v0, w1z, w1x, w1d, g):
    t = LN_T

    def body(dz_ref, dx_ref, ddt_ref, dv1_ref, v_ref, wz_ref, wx_ref, wd_ref, g_ref, dv_ref, dgb_ref):
        @pl.when(pl.program_id(0) == 0)
        def _():
            dgb_ref[...] = jnp.zeros_like(dgb_ref)

        dy = (DN_ALPHA * dv1_ref[...] + _dot_nt(dz_ref[...], wz_ref[...]) + _dot_nt(dx_ref[...], wx_ref[...])
              + _dot_nt(_mx(ddt_ref[...]), wd_ref[...]))
        dv, dg, db = _ln_bwd(v_ref[...], g_ref[...], dy)
        dv_ref[...] = dv
        dgb_ref[0:1, :] += dg
        dgb_ref[1:2, :] += db

    rows = lambda w: pl.BlockSpec((t, w), lambda i: (i, 0))
    const = lambda w: pl.BlockSpec((D_MODEL, w), lambda i: (0, 0))
    return pl.pallas_call(
        body, name="l1_dx_ln", grid=(SEQ // t,),
        in_specs=[rows(SSD_INNER), rows(SSD_CONV), rows(LANES), rows(D_MODEL), rows(D_MODEL),
                  const(SSD_INNER), const(SSD_CONV), const(LANES), pl.BlockSpec((1, D_MODEL), lambda i: (0, 0))],
        out_specs=[rows(D_MODEL), pl.BlockSpec((SUBLANES, D_MODEL), lambda i: (0, 0))],
        out_shape=[jax.ShapeDtypeStruct((SEQ, D_MODEL), F32), jax.ShapeDtypeStruct((SUBLANES, D_MODEL), F32)],
        compiler_params=_params(("arbitrary",)),
    )(dz, dxbc, ddt, dv1, v0, w1z, w1x, w1d, g)


def _gate_bwd(dv0, w_out, h, o, proj0):
    t = LN_T

    def body(dv_ref, w_ref, h_ref, o_ref, ga_ref, gb_ref, dh_ref, do_ref, dg_ref):
        dy = _dot_nt(_mx(dv_ref[...]), w_ref[...])
        ga, gb, dya, dyb = ga_ref[...], gb_ref[...], dy[:, :512], dy[:, 512:]
        dh_ref[...] = dya * _silu(ga)
        do_ref[...] = dyb * _silu(gb)
        dg_ref[:, :512] = _mx(dya * h_ref[...] * _dsilu(ga))
        dg_ref[:, 512:] = _mx(dyb * o_ref[...] * _dsilu(gb))

    half = pl.BlockSpec((t, 512), lambda i: (i, 0))
    half1 = pl.BlockSpec((t, 512), lambda i: (i, 1))
    full = pl.BlockSpec((t, 1024), lambda i: (i, 0))
    return pl.pallas_call(
        body, name="gate_bwd", grid=(SEQ // t,),
        in_specs=[full, pl.BlockSpec((D_MODEL, D_MODEL), lambda i: (0, 0)), half, half, half, half1],
        out_specs=[half, half, full],
        out_shape=[jax.ShapeDtypeStruct((SEQ, 512), F32), jax.ShapeDtypeStruct((SEQ, 512), F32),
                   jax.ShapeDtypeStruct((SEQ, 1024), MXU_DTYPE)],
        compiler_params=_params(("parallel",)),
    )(dv0, w_out, h, o, proj0, proj0)


CONV_T = 1024
CONV_CB = 1024


def _ssd_conv_fwd(xbc, cw8, cb):
    t, cbk = CONV_T, CONV_CB
    tb = t // SUBLANES

    def body(x_ref, halo_ref, cw_ref, cb_ref, pre_ref, act_ref):
        halo = jnp.where(pl.program_id(1) > 0, halo_ref[...], 0.0)
        pre = _conv4(x_ref[...], halo, cw_ref[...], cb_ref[...])
        pre_ref[...] = pre
        act_ref[...] = _silu(pre)

    blk = pl.BlockSpec((t, cbk), lambda j, i: (i, j))
    return pl.pallas_call(
        body, name="ssd_conv_fwd", grid=(SSD_CONV // cbk, SEQ // t),
        in_specs=[blk, pl.BlockSpec((SUBLANES, cbk), lambda j, i: (jnp.maximum(i * tb - 1, 0), j)),
                  pl.BlockSpec((SUBLANES, cbk), lambda j, i: (0, j)), pl.BlockSpec((1, cbk), lambda j, i: (0, j))],
        out_specs=[blk, blk],
        out_shape=[jax.ShapeDtypeStruct((SEQ, SSD_CONV), F32), jax.ShapeDtypeStruct((SEQ, SSD_CONV), F32)],
        compiler_params=_params(("parallel", "parallel")),
    )(xbc, xbc, cw8, cb)


def _ssd_conv_bwd(dact, pre, xbc, cw8):
    t, cbk = CONV_T, CONV_CB
    tb = t // SUBLANES
    nb = SEQ // t

    def body(da_ref, dan_ref, pre_ref, pren_ref, x_ref, cw_ref, dx_ref, dcw_ref):
        i = pl.program_id(1)

        @pl.when(i == 0)
        def _():
            dcw_ref[...] = jnp.zeros_like(dcw_ref)

        dpre = da_ref[...] * _dsilu(pre_ref[...])
        dpre_next = jnp.where(i < nb - 1, dan_ref[...] * _dsilu(pren_ref[...]), 0.0)
        xblk = x_ref[...]
        cw = cw_ref[...]
        dx = dpre * cw[3:4]
        dcw_ref[3:4, :] += jnp.sum(dpre * xblk, axis=0, keepdims=True)
        for k in range(3):
            up = _shift_up(dpre, dpre_next, 3 - k)
            dcw_ref[k:k + 1, :] += jnp.sum(up * xblk, axis=0, keepdims=True)
            dx = dx + up * cw[k:k + 1]
        dcw_ref[4:5, :] += jnp.sum(dpre, axis=0, keepdims=True)
        dx_ref[...] = _mx(dx)

    blk = pl.BlockSpec((t, cbk), lambda j, i: (i, j))
    nxt = pl.BlockSpec((SUBLANES, cbk), lambda j, i: (jnp.minimum((i + 1) * tb, SEQ // SUBLANES - 1), j))
    acc = pl.BlockSpec((SUBLANES, cbk), lambda j, i: (0, j))
    return pl.pallas_call(
        body, name="ssd_conv_bwd", grid=(SSD_CONV // cbk, nb),
        in_specs=[blk, nxt, blk, nxt, blk, acc],
        out_specs=[blk, acc],
        out_shape=[jax.ShapeDtypeStruct((SEQ, SSD_CONV), MXU_DTYPE), jax.ShapeDtypeStruct((SUBLANES, SSD_CONV), F32)],
        compiler_params=_params(("parallel", "arbitrary")),
    )(dact, dact, pre, pre, xbc, cw8)


def _ssd_common(dt_raw, bias, alog, tril, expand_t, xs):
    lane = lax.broadcasted_iota(jnp.int32, dt_raw.shape, 1)
    dt = jnp.where(lane < SSD_HEADS, _softplus(dt_raw + bias), 0.0)
    a_neg = -jnp.exp(alog)
    cs = _dot_hi(tril, dt * a_neg)
    dt_x = _expand_heads(dt, expand_t)
    ecs_x = _expand_heads(jnp.exp(cs), expand_t)
    ds_x = _expand_heads(jnp.exp(cs[SSD_L - 1:SSD_L, :] - cs), expand_t)
    return dt, a_neg, cs, dt_x, None, xs * dt_x, ds_x, ecs_x, ecs_x[SSD_L - 1:SSD_L, :]


def _expand_heads(v, expand_t):
    hi = v.astype(jnp.bfloat16)
    lo = (v - hi.astype(F32)).astype(jnp.bfloat16)
    return _dot_nt(hi, expand_t) + _dot_nt(lo, expand_t)


def _fold_heads(v, expand_t):
    hi = v.astype(jnp.bfloat16)
    lo = (v - hi.astype(F32)).astype(jnp.bfloat16)
    return _dot(hi, expand_t) + _dot(lo, expand_t)


def _ssd_decay(cs, cs_t, hh, causal):
    seg = cs[:, hh:hh + 1] - cs_t[hh:hh + 1, :]
    return jnp.where(causal, jnp.exp(jnp.where(causal, seg, 0.0)), 0.0)


def _ssd_scan_fwd(act, dt_raw, bias, alog, d_x, tril, expand_t):
    nc = SEQ // SSD_L
    gw = SSD_INNER // SSD_GROUPS

    def body(act_ref, dt_ref, bias_ref, alog_ref, dx_ref, tril_ref, et_ref, y_ref, hp_ref, h_sc):
        @pl.when(pl.program_id(0) == 0)
        def _():
            h_sc[...] = jnp.zeros_like(h_sc)

        xs = act_ref[:, :SSD_INNER]
        _, _, cs, _, _, xdt, ds_x, ecs_x, elast = _ssd_common(
            dt_ref[...], bias_ref[...], alog_ref[...], tril_ref[...], et_ref[...], xs)
        cs_t = cs.T
        causal = (lax.broadcasted_iota(jnp.int32, (SSD_L, SSD_L), 0)
                  >= lax.broadcasted_iota(jnp.int32, (SSD_L, SSD_L), 1))
        lane = lax.broadcasted_iota(jnp.int32, (SSD_L, LANES), 1)
        xdt_b = _mx(xdt)
        xds_b = _mx(xdt * ds_x)
        hp_ref[0] = h_sc[...]
        for g in range(SSD_GROUPS):
            gs = slice(g * gw, (g + 1) * gw)
            bg = _mx(act_ref[:, SSD_INNER + g * SSD_N:SSD_INNER + (g + 1) * SSD_N])
            cg = _mx(act_ref[:, SSD_INNER + 512 + g * SSD_N:SSD_INNER + 512 + (g + 1) * SSD_N])
            cb = _dot_nt(cg, bg)
            hprev = h_sc[:, gs]
            yoff = _dot(cg, _mx(hprev)) * ecs_x[:, gs]
            h_sc[:, gs] = hprev * elast[:, gs] + _dot_tn(bg, xds_b[:, gs])
            for pr in range(4):
                ps = slice(g * gw + pr * LANES, g * gw + (pr + 1) * LANES)
                xp = xdt_b[:, ps]
                ydiag = jnp.zeros((SSD_L, LANES), F32)
                for j in range(2):
                    dm = _ssd_decay(cs, cs_t, g * 8 + pr * 2 + j, causal)
                    mine = (lane >= j * 64) & (lane < (j + 1) * 64)
                    ydiag = ydiag + _dot(_mx(cb * dm), jnp.where(mine, xp, jnp.zeros_like(xp)))
                y_ref[:, ps] = ydiag + yoff[:, pr * LANES:(pr + 1) * LANES] + dx_ref[:, ps] * xs[:, ps]

    const = lambda shape: pl.BlockSpec(shape, lambda c: (0, 0))
    return pl.pallas_call(
        body, name="ssd_scan_fwd", grid=(nc,),
        in_specs=[pl.BlockSpec((SSD_L, SSD_CONV), lambda c: (c, 0)), pl.BlockSpec((SSD_L, LANES), lambda c: (c, 0)),
                  const((1, LANES)), const((1, LANES)), const((1, SSD_INNER)), const((SSD_L, SSD_L)),
                  const((SSD_INNER, LANES))],
        out_specs=[pl.BlockSpec((SSD_L, SSD_INNER), lambda c: (c, 0)),
                   pl.BlockSpec((1, SSD_N, SSD_INNER), lambda c: (c, 0, 0))],
        out_shape=[jax.ShapeDtypeStruct((SEQ, SSD_INNER), F32), jax.ShapeDtypeStruct((nc, SSD_N, SSD_INNER), F32)],
        scratch_shapes=[pltpu.VMEM((SSD_N, SSD_INNER), F32)],
        compiler_params=_params(("arbitrary",)),
    )(act, dt_raw, bias, alog, d_x, tril, expand_t)


def _ssd_scan_bwd(dy, act, dt_raw, hprev_all, bias, alog, d_x, tril, expand_t):
    nc = SEQ // SSD_L
    gw = SSD_INNER // SSD_GROUPS

    def body(dy_ref, act_ref, dt_ref, hp_ref, bias_ref, alog_ref, dx_ref, tril_ref, et_ref,
             dact_ref, ddt_ref, dvec_ref, dh_sc, dd_sc):
        i = pl.program_id(0)

        @pl.when(i == 0)
        def _():
            dh_sc[...] = jnp.zeros_like(dh_sc)
            dd_sc[...] = jnp.zeros_like(dd_sc)
            dvec_ref[...] = jnp.zeros_like(dvec_ref)

        xs = act_ref[:, :SSD_INNER]
        dt_raw_v, bias_v = dt_ref[...], bias_ref[...]
        dt, a_neg, cs, dt_x, _, xdt, ds_x, ecs_x, elast = _ssd_common(
            dt_raw_v, bias_v, alog_ref[...], tril_ref[...], et_ref[...], xs)
        cs_t = cs.T
        rowi = lax.broadcasted_iota(jnp.int32, (SSD_L, SSD_L), 0)
        coli = lax.broadcasted_iota(jnp.int32, (SSD_L, SSD_L), 1)
        causal = rowi >= coli
        lane = lax.broadcasted_iota(jnp.int32, (SSD_L, LANES), 1)
        row_g = lax.broadcasted_iota(jnp.int32, (SSD_L, gw), 0)
        dyv = dy_ref[...]
        dd_sc[0:1, :] += jnp.sum(dyv * xs, axis=0, keepdims=True)
        xdt_b = _mx(xdt)
        xds = xdt * ds_x
        xds_b = _mx(xds)
        dy_b = _mx(dyv)
        dye_b = _mx(dyv * ecs_x)
        dcs = jnp.zeros((SSD_L, LANES), F32)
        dcs_t = jnp.zeros((LANES, SSD_L), F32)
        dcs_parts = []
        dxdt_parts = []
        for g in range(SSD_GROUPS):
            gs = slice(g * gw, (g + 1) * gw)
            bcol = slice(SSD_INNER + g * SSD_N, SSD_INNER + (g + 1) * SSD_N)
            ccol = slice(SSD_INNER + 512 + g * SSD_N, SSD_INNER + 512 + (g + 1) * SSD_N)
            bg, cg = _mx(act_ref[:, bcol]), _mx(act_ref[:, ccol])
            cb = _dot_nt(cg, bg)
            hp = hp_ref[0, :, gs]
            hp_b = _mx(hp)
            dh = dh_sc[:, gs]
            dh_b = _mx(dh)
            yoff = _dot(cg, hp_b) * ecs_x[:, gs]
            bdh = _dot(bg, dh_b)
            tt = xds[:, gs] * bdh
            last_row = (jnp.sum(tt, axis=0, keepdims=True)
                        + jnp.sum(dh * hp, axis=0, keepdims=True) * elast[:, gs])
            dcs_parts.append(dyv[:, gs] * yoff - tt + jnp.where(row_g == SSD_L - 1, last_row, 0.0))
            dc_g = _dot_nt(dye_b[:, gs], hp_b)
            db_g = _dot_nt(xds_b[:, gs], dh_b)
            dh_sc[:, gs] = _dot_tn(cg, dye_b[:, gs]) + dh * elast[:, gs]
            wsum = jnp.zeros((SSD_L, SSD_L), F32)
            dxdt_g = []
            for pr in range(4):
                ps = slice(g * gw + pr * LANES, g * gw + (pr + 1) * LANES)
                xp, dyp = xdt_b[:, ps], dy_b[:, ps]
                dxp = jnp.zeros((SSD_L, LANES), F32)
                for j in range(2):
                    hh = g * 8 + pr * 2 + j
                    dm = _ssd_decay(cs, cs_t, hh, causal)
                    mine = (lane >= j * 64) & (lane < (j + 1) * 64)
                    dy_h = jnp.where(mine, dyp, jnp.zeros_like(dyp))
                    wd = _dot_nt(dy_h, xp) * dm
                    wsum = wsum + wd
                    gmat = wd * cb
                    dcs = dcs + jnp.where(lane == hh, jnp.sum(gmat, axis=1, keepdims=True), 0.0)
                    dcs_t = dcs_t - jnp.where(rowi == hh, jnp.sum(gmat, axis=0, keepdims=True), 0.0)
                    dxp = dxp + _dot_tn(_mx(cb * dm), dy_h)
                dxdt_g.append(dxp)
            dxdt_parts.append(jnp.concatenate(dxdt_g, axis=1) + bdh * ds_x[:, gs])
            ws_b = _mx(wsum)
            dact_ref[:, ccol] = dc_g + _dot(ws_b, bg)
            dact_ref[:, bcol] = db_g + _dot_tn(ws_b, cg)
        dxdt = jnp.concatenate(dxdt_parts, axis=1)
        dcs_x = jnp.concatenate(dcs_parts, axis=1)
        et = et_ref[...]
        dcs_tot = dcs + dcs_t.T + _fold_heads(dcs_x, et)
        da_dt = _dot_hi((coli >= rowi).astype(F32), dcs_tot)
        ddt = da_dt * a_neg + _fold_heads(dxdt * xs, et)
        ddt_raw = ddt * _sigmoid(dt_raw_v + bias_v)
        ddt_ref[...] = ddt_raw
        dvec_ref[0:1, :] += jnp.sum(ddt_raw, axis=0, keepdims=True)
        dvec_ref[1:2, :] += jnp.sum(da_dt * dt, axis=0, keepdims=True) * a_neg
        dact_ref[:, :SSD_INNER] = dyv * dx_ref[...] + dxdt * dt_x

        @pl.when(i == nc - 1)
        def _():
            dvec_ref[2:3, :] = _fold_heads(dd_sc[...], et)[0:1, :]

    const = lambda shape: pl.BlockSpec(shape, lambda c: (0, 0))
    rev = lambda c: (nc - 1 - c, 0)
    return pl.pallas_call(
        body, name="ssd_scan_bwd", grid=(nc,),
        in_specs=[pl.BlockSpec((SSD_L, SSD_INNER), rev), pl.BlockSpec((SSD_L, SSD_CONV), rev),
                  pl.BlockSpec((SSD_L, LANES), rev),
                  pl.BlockSpec((1, SSD_N, SSD_INNER), lambda c: (nc - 1 - c, 0, 0)),
                  const((1, LANES)), const((1, LANES)), const((1, SSD_INNER)), const((SSD_L, SSD_L)),
                  const((SSD_INNER, LANES))],
        out_specs=[pl.BlockSpec((SSD_L, SSD_CONV), rev), pl.BlockSpec((SSD_L, LANES), rev), const((SUBLANES, LANES))],
        out_shape=[jax.ShapeDtypeStruct((SEQ, SSD_CONV), F32), jax.ShapeDtypeStruct((SEQ, LANES), F32),
                   jax.ShapeDtypeStruct((SUBLANES, LANES), F32)],
        scratch_shapes=[pltpu.VMEM((SSD_N, SSD_INNER), F32), pltpu.VMEM((SUBLANES, SSD_INNER), F32)],
        compiler_params=_params(("arbitrary",)),
    )(dy, act, dt_raw, hprev_all, bias, alog, d_x, tril, expand_t)


L1_T = 512


def _gated_norm(y, z, nw):
    y2 = y * _silu(z)
    gw = SSD_INNER // SSD_GROUPS
    outs, xhs, rss = [], [], []
    for g in range(SSD_GROUPS):
        gs = slice(g * gw, (g + 1) * gw)
        v = y2[:, gs]
        rs = lax.rsqrt(jnp.mean(v * v, axis=-1, keepdims=True) + 1e-6)
        xhs.append(v * rs)
        rss.append(rs)
        outs.append(v * rs * nw[:, gs])
    return outs, xhs, rss


def _l1_out(y, z, nw, w_out, x1, g, b, target):
    t = L1_T

    def body(y_ref, z_ref, nw_ref, w_ref, x1_ref, g_ref, b_ref, tg_ref, yn_ref, dv_ref, dgb_ref, loss_ref):
        @pl.when(pl.program_id(0) == 0)
        def _():
            dgb_ref[...] = jnp.zeros_like(dgb_ref)
            loss_ref[...] = jnp.zeros_like(loss_ref)

        outs, _, _ = _gated_norm(y_ref[...], z_ref[...], nw_ref[...])
        yn = _mx(jnp.concatenate(outs, axis=1))
        yn_ref[...] = yn
        v = DN_ALPHA * x1_ref[...] + _dot(yn, w_ref[...])
        gv = g_ref[...]
        err = _ln(v, gv, b_ref[...]) - tg_ref[...]
        rowsum = jnp.sum(err * err, axis=1, keepdims=True)
        loss_ref[...] += 0.5 * jnp.sum(rowsum, axis=0, keepdims=True) / D_MODEL
        dv, dg, db = _ln_bwd(v, gv, err / D_MODEL)
        dv_ref[...] = dv
        dgb_ref[0:1, :] += dg
        dgb_ref[1:2, :] += db

    wide = pl.BlockSpec((t, SSD_INNER), lambda i: (i, 0))
    full = pl.BlockSpec((t, D_MODEL), lambda i: (i, 0))
    vec = pl.BlockSpec((1, D_MODEL), lambda i: (0, 0))
    return pl.pallas_call(
        body, name="l1_out", grid=(SEQ // t,),
        in_specs=[wide, wide, pl.BlockSpec((1, SSD_INNER), lambda i: (0, 0)),
                  pl.BlockSpec((SSD_INNER, D_MODEL), lambda i: (0, 0)), full, vec, vec, full],
        out_specs=[wide, full, pl.BlockSpec((SUBLANES, D_MODEL), lambda i: (0, 0)),
                   pl.BlockSpec((SUBLANES, LANES), lambda i: (0, 0))],
        out_shape=[jax.ShapeDtypeStruct((SEQ, SSD_INNER), MXU_DTYPE), jax.ShapeDtypeStruct((SEQ, D_MODEL), F32),
                   jax.ShapeDtypeStruct((SUBLANES, D_MODEL), F32), jax.ShapeDtypeStruct((SUBLANES, LANES), F32)],
        compiler_params=_params(("arbitrary",)),
    )(y, z, nw, w_out, x1, g, b, target)


def _l1_gate_bwd(dv1, w_out, y, z, nw):
    t = L1_T
    gw = SSD_INNER // SSD_GROUPS

    def body(dv_ref, w_ref, y_ref, z_ref, nw_ref, dy_ref, dz_ref, dnw_ref):
        @pl.when(pl.program_id(0) == 0)
        def _():
            dnw_ref[...] = jnp.zeros_like(dnw_ref)

        dyn = _dot_nt(_mx(dv_ref[...]), w_ref[...])
        yv, zv, nwv = y_ref[...], z_ref[...], nw_ref[...]
        _, xhs, rss = _gated_norm(yv, zv, nwv)
        sz, dsz = _silu(zv), _dsilu(zv)
        for g in range(SSD_GROUPS):
            gs = slice(g * gw, (g + 1) * gw)
            d_out = dyn[:, gs]
            xh = xhs[g]
            dnw_ref[0:1, gs] += jnp.sum(d_out * xh, axis=0, keepdims=True)
            dxh = d_out * nwv[:, gs]
            dy2 = rss[g] * (dxh - xh * jnp.mean(dxh * xh, axis=-1, keepdims=True))
            dy_ref[:, gs] = dy2 * sz[:, gs]
            dz_ref[:, gs] = _mx(dy2 * yv[:, gs] * dsz[:, gs])

    wide = pl.BlockSpec((t, SSD_INNER), lambda i: (i, 0))
    return pl.pallas_call(
        body, name="l1_gate_bwd", grid=(SEQ // t,),
        in_specs=[pl.BlockSpec((t, D_MODEL), lambda i: (i, 0)), pl.BlockSpec((SSD_INNER, D_MODEL), lambda i: (0, 0)),
                  wide, wide, pl.BlockSpec((1, SSD_INNER), lambda i: (0, 0))],
        out_specs=[wide, wide, pl.BlockSpec((SUBLANES, SSD_INNER), lambda i: (0, 0))],
        out_shape=[jax.ShapeDtypeStruct((SEQ, SSD_INNER), F32), jax.ShapeDtypeStruct((SEQ, SSD_INNER), MXU_DTYPE),
                   jax.ShapeDtypeStruct((SUBLANES, SSD_INNER), F32)],
        compiler_params=_params(("arbitrary",)),
    )(dv1, w_out, y, z, nw)


MESH = pl.DeviceIdType.MESH
ANY = pl.BlockSpec(memory_space=pl.ANY)


def _flip(v, bit):
    return 1 - v if bit else v


def _all_gather(blocks, name):
    n = len(blocks)

    def body(*refs):
        x_refs, out_refs = refs[:n], refs[n:2 * n]
        send_sems, recv_sems, local_sems = refs[2 * n:]
        mx, my, mc = lax.axis_index("x"), lax.axis_index("y"), lax.axis_index("c")
        me, sibling = (mx, my, mc), (mx, my, 1 - mc)
        chips = [(1 - mx, my), (mx, 1 - my), (1 - mx, 1 - my)]

        def copy(a, k, block, to, own=False):
            px, py, pc = block
            slot = out_refs[a].at[4 * px + 2 * py + pc]
            return pltpu.make_async_remote_copy(
                src_ref=x_refs[a] if own else slot, dst_ref=slot,
                send_sem=send_sems.at[7 * a + k], recv_sem=recv_sems.at[7 * a + k], device_id=to, device_id_type=MESH)

        mine = [pltpu.make_async_copy(x_refs[a], out_refs[a].at[4 * mx + 2 * my + mc], local_sems.at[a])
                for a in range(n)]
        first = []
        for a in range(n):
            mine[a].start()
            first.append(copy(a, 0, me, sibling, own=True))
            first += [copy(a, 1 + j, me, (*chip, mc), own=True) for j, chip in enumerate(chips)]
        for cp in first:
            cp.start()
        passed = []
        for j, chip in enumerate(chips):
            for a in range(n):
                copy(a, 1 + j, (*chip, mc), me).wait_recv()
                fwd = copy(a, 4 + j, (*chip, mc), sibling)
                fwd.start()
                passed.append(fwd)
        for a in range(n):
            copy(a, 0, sibling, me).wait_recv()
            for j, chip in enumerate(chips):
                copy(a, 4 + j, (*chip, 1 - mc), me).wait_recv()
        for cp in first + passed:
            cp.wait_send()
        for cp in mine:
            cp.wait()

    return pl.pallas_call(
        body, name=name, in_specs=[ANY] * n, out_specs=[ANY] * n,
        out_shape=[jax.ShapeDtypeStruct((N_DEV,) + b.shape, b.dtype) for b in blocks],
        scratch_shapes=[pltpu.SemaphoreType.DMA((7 * n,)), pltpu.SemaphoreType.DMA((7 * n,)),
                        pltpu.SemaphoreType.DMA((n,))],
    )(*blocks)


def _l0_in(x, w0p, bcast=()):
    n = len(bcast)
    tm, tn = 1024, 1024
    gi, gj = SEQ // tm, 2048 // tn

    def body(x_ref, w_ref, *rest):
        o_ref, xb_ref = rest[n], rest[n + 1]
        i, j = pl.program_id(0), pl.program_id(1)
        if n:
            copies = _peer_copies(rest[:n], rest[n + 2:2 * n + 2], rest[2 * n + 2:], [])

            @pl.when((i == 0) & (j == 0))
            def _():
                for cp in copies:
                    cp.start()

        xb = _mx(x_ref[...])
        xb_ref[...] = xb
        o_ref[...] = _dot(xb, w_ref[...])

        if n:
            @pl.when((i == gi - 1) & (j == gj - 1))
            def _():
                for cp in copies:
                    cp.wait()

    res = pl.pallas_call(
        body, name="l0_in", grid=(gi, gj),
        in_specs=[pl.BlockSpec((tm, D_MODEL), lambda i, j: (i, 0)), pl.BlockSpec((D_MODEL, tn), lambda i, j: (0, j))]
        + [ANY] * n,
        out_specs=[pl.BlockSpec((tm, tn), lambda i, j: (i, j)), pl.BlockSpec((tm, D_MODEL), lambda i, j: (i, 0))]
        + [ANY] * n,
        out_shape=[jax.ShapeDtypeStruct((SEQ, 2048), F32), jax.ShapeDtypeStruct((SEQ, D_MODEL), MXU_DTYPE)]
        + _exchange_shapes([], bcast),
        scratch_shapes=_exchange_sems(n) if n else [],
        compiler_params=_params(("arbitrary", "arbitrary")),
    )(x, w0p, *bcast)
    return res[0], res[1], res[2:]


def _l0_dx(dxr, dgate, dtail, w0p, dv0, scatter=(), bcast=()):
    arrays, ranges = _scatter_args(scatter)
    n = len(arrays) + len(bcast)
    tm = 1024
    steps = SEQ // tm

    def body(dxr_ref, dg_ref, dt_ref, w_ref, dv_ref, *rest):
        o_ref = rest[n]
        i = pl.program_id(0)
        if n:
            copies = _peer_copies(rest[:n], rest[n + 1:2 * n + 1], rest[2 * n + 1:], ranges)

            @pl.when(i == 0)
            def _():
                for cp in copies:
                    cp.start()

        o_ref[...] = (DN_ALPHA * dv_ref[...] + _dot_nt(dg_ref[...], w_ref[:, 0:1024])
                      + _dot_nt(dxr_ref[...], w_ref[:, 1024:1536]) + _dot_nt(dt_ref[...], w_ref[:, 1536:2048]))

        if n:
            @pl.when(i == steps - 1)
            def _():
                for cp in copies:
                    cp.wait()

    rows = lambda w: pl.BlockSpec((tm, w), lambda i: (i, 0))
    res = pl.pallas_call(
        body, name="l0_dx", grid=(steps,),
        in_specs=[rows(512), rows(1024), rows(512), pl.BlockSpec((D_MODEL, 2048), lambda i: (0, 0)), rows(D_MODEL)]
        + [ANY] * n,
        out_specs=[rows(D_MODEL)] + [ANY] * n,
        out_shape=[jax.ShapeDtypeStruct((SEQ, D_MODEL), F32)] + _exchange_shapes(arrays, bcast),
        scratch_shapes=_exchange_sems(n) if n else [],
        compiler_params=_params(("arbitrary",)),
    )(dxr, dgate, dtail, w0p, dv0, *arrays, *bcast)
    return res[0], res[1:]


def _scatter_args(scatter):
    arrays = [s[0] if isinstance(s, tuple) else s for s in scatter]
    ranges = [s[1] if isinstance(s, tuple) else (0, N_DEV) for s in scatter]
    return arrays, ranges


def _exchange_shapes(scatter, bcast):
    return ([jax.ShapeDtypeStruct((N_DEV,) + a.shape[1:], a.dtype) for a in scatter]
            + [jax.ShapeDtypeStruct((N_DEV,) + a.shape, a.dtype) for a in bcast])


def _exchange_sems(n):
    return [pltpu.SemaphoreType.DMA((7 * n,)), pltpu.SemaphoreType.DMA((7 * n,)), pltpu.SemaphoreType.DMA((n,))]


class _GuardedCopy:
    def __init__(self, copy, send=None, recv=None, local=False):
        self.copy, self.send, self.recv, self.local = copy, send, recv, local

    @staticmethod
    def _run(pred, fn):
        if pred is None:
            fn()
        else:
            pl.when(pred)(fn)

    def start(self):
        self._run(self.send, self.copy.start)

    def wait(self):
        if self.local:
            self._run(self.send, self.copy.wait)
        else:
            self._run(self.send, self.copy.wait_send)
            self._run(self.recv, self.copy.wait_recv)


def _peer_copies(in_refs, out_refs, sems, ranges):
    send_sems, recv_sems, local_sems = sems
    n, ns = len(in_refs), len(ranges)
    mx, my, mc = lax.axis_index("x"), lax.axis_index("y"), lax.axis_index("c")
    me = 4 * mx + 2 * my + mc

    def src(a, slot):
        return in_refs[a].at[slot - ranges[a][0]] if a < ns else in_refs[a]

    def member(a, dev):
        if a >= ns or ranges[a] == (0, N_DEV):
            return None
        return (dev >= ranges[a][0]) & (dev < ranges[a][1])

    copies = [_GuardedCopy(pltpu.make_async_copy(src(a, me), out_refs[a].at[me], local_sems.at[a]),
                           send=member(a, me), local=True) for a in range(n)]
    for k in range(1, N_DEV):
        px, py, pc = _flip(mx, (k >> 2) & 1), _flip(my, (k >> 1) & 1), _flip(mc, k & 1)
        peer = 4 * px + 2 * py + pc
        for a in range(n):
            copies.append(_GuardedCopy(pltpu.make_async_remote_copy(
                src_ref=src(a, peer), dst_ref=out_refs[a].at[me],
                send_sem=send_sems.at[7 * a + k - 1], recv_sem=recv_sems.at[7 * a + k - 1],
                device_id=(px, py, pc), device_id_type=MESH), send=member(a, peer), recv=member(a, me)))
    return copies


def _segments(col_map, width):
    segs = []
    for lo, hi, arr, alo in col_map:
        for s in range(N_DEV):
            a, b = max(lo, s * width), min(hi, (s + 1) * width)
            if a < b:
                segs.append((s, a - s * width, b - a, arr, alo + a - lo))
    return segs


COPY_ROWS = 256


def _unshard(g8, col_map, widths, name):
    _, r, w = g8.shape
    rb = min(r, COPY_ROWS)
    segs = _segments(col_map, w)

    def body(g_ref, *o_refs):
        for o_ref in o_refs:
            o_ref[...] = jnp.zeros_like(o_ref)
        for s, llo, n, arr, alo in segs:
            o_refs[arr][:, alo:alo + n] = g_ref[s, :, llo:llo + n]

    return pl.pallas_call(
        body, name=name, grid=(r // rb,),
        in_specs=[pl.BlockSpec((N_DEV, rb, w), lambda i: (0, i, 0))],
        out_specs=[pl.BlockSpec((rb, n), lambda i: (i, 0)) for n in widths],
        out_shape=[jax.ShapeDtypeStruct((r, n), g8.dtype) for n in widths],
        compiler_params=_params(("parallel",)),
    )(g8)


def _reshard(srcs, col_map, w, dtype, name, shards=(0, N_DEV)):
    r = srcs[0].shape[0]
    rb = min(r, COPY_ROWS)
    lo, hi = shards
    segs = [sg for sg in _segments(col_map, w) if lo <= sg[0] < hi]

    def body(*refs):
        o_ref = refs[-1]
        for s, llo, n, arr, alo in segs:
            o_ref[s - lo, :, llo:llo + n] = refs[arr][:, alo:alo + n].astype(dtype)

    return pl.pallas_call(
        body, name=name, grid=(r // rb,),
        in_specs=[pl.BlockSpec((rb, a.shape[1]), lambda i: (i, 0)) for a in srcs],
        out_specs=pl.BlockSpec((hi - lo, rb, w), lambda i: (0, i, 0)),
        out_shape=jax.ShapeDtypeStruct((hi - lo, r, w), dtype),
        compiler_params=_params(("parallel",)),
    )(*srcs)


def _adamw(parts, w, m, v, name):
    r, c = w.shape
    tr = COPY_ROWS if r % COPY_ROWS == 0 else r

    def body(p_ref, w_ref, m_ref, v_ref, g_ref, d_ref, mo_ref, vo_ref):
        g = p_ref[0].astype(F32)
        for s in range(1, N_DEV):
            g = g + p_ref[s].astype(F32)
        g_ref[...] = g
        d_ref[...], mo_ref[...], vo_ref[...] = _adamw_math(g, w_ref[...], m_ref[...], v_ref[...])

    blk = pl.BlockSpec((tr, c), lambda i: (i, 0))
    out = jax.ShapeDtypeStruct((r, c), F32)
    return pl.pallas_call(
        body, name=name, grid=(r // tr,),
        in_specs=[pl.BlockSpec((N_DEV, tr, c), lambda i: (0, i, 0)), blk, blk, blk],
        out_specs=[blk, blk, blk, blk], out_shape=[out, out, out, out],
        compiler_params=_params(("parallel",)),
    )(parts, w, m, v)


def _adamw_math(g, w, m, v):
    mn = ADAM_B1 * m + (1.0 - ADAM_B1) * g
    vn = ADAM_B2 * v + (1.0 - ADAM_B2) * (g * g)
    m_hat = mn / (1.0 - ADAM_B1 ** ADAM_STEP)
    v_hat = vn / (1.0 - ADAM_B2 ** ADAM_STEP)
    return -ADAM_LR * (m_hat / (jnp.sqrt(v_hat) + ADAM_EPS) + ADAM_WD * w), mn, vn


SMALL = (("ab_conv_w", 0, 4, 64), ("ssd_conv_w", 4, 4, 384), ("ssd_conv_b", 8, 1, 384), ("ssd_norm", 9, 1, 256),
         ("ssd_ln_g", 10, 1, 128), ("ssd_ln_b", 11, 1, 128))
VECS = (("ab_conv_b", 512), ("ab_gate_a_b", 512), ("ab_gate_x_b", 512), ("ab_lambda", 512), ("mla_q_norm", 256),
        ("mla_kv_norm", 128), ("ab_ln_g", 1024), ("ab_ln_b", 1024), ("ssd_dt_bias", 32), ("ssd_a_log", 32),
        ("ssd_d", 32))
GATES = ("ab_gate_a_w", "ab_gate_x_w")
SMALL_NAMES = tuple(n for n, *_ in SMALL) + tuple(n for n, _ in VECS) + GATES
VMEM_WHOLE = pl.BlockSpec(memory_space=pltpu.VMEM)


def _view2d(name, a):
    if name in GATES:
        return a.reshape(RNN_W, 64)
    return a[0] if a.ndim == 3 else a


def _unshard_small(g):
    widths = (512, 3072, 3072, 2048, 1024, 1024)

    def body(*refs):
        ins, outs = refs[:6], refs[6:]
        outs[0][...] = jnp.zeros_like(outs[0])
        outs[1][...] = jnp.zeros_like(outs[1])
        for (_, _, nr, c), i_ref, o_ref in zip(SMALL, ins, outs):
            for j in range(N_DEV):
                o_ref[0:nr, j * c:(j + 1) * c] = i_ref[j]

    return pl.pallas_call(
        body, name="unshard_small", in_specs=[VMEM_WHOLE] * 6, out_specs=[VMEM_WHOLE] * 6,
        out_shape=[jax.ShapeDtypeStruct((SUBLANES if nr == 4 else 1, w), F32) for (_, _, nr, _), w in zip(SMALL, widths)],
    )(*g)


def _prep_repl(ga, gx, dt_bias, a_log, d):
    def body(ga_ref, gx_ref, b_ref, al_ref, d_ref, wa_ref, wx_ref, b128_ref, al128_ref, dx_ref):
        wa_ref[...] = jnp.zeros_like(wa_ref)
        wx_ref[...] = jnp.zeros_like(wx_ref)
        for hd in range(8):
            hs = slice(hd * 64, (hd + 1) * 64)
            wa_ref[hs, hs] = _mx(ga_ref[hs, :])
            wx_ref[hs, hs] = _mx(gx_ref[hs, :])
        b128_ref[...] = jnp.zeros_like(b128_ref)
        al128_ref[...] = jnp.zeros_like(al128_ref)
        b128_ref[:, 0:SSD_HEADS] = b_ref[...]
        al128_ref[:, 0:SSD_HEADS] = al_ref[...]
        dv = d_ref[...]
        for hd in range(SSD_HEADS):
            dx_ref[:, hd * SSD_P:(hd + 1) * SSD_P] = jnp.broadcast_to(dv[:, hd:hd + 1], (1, SSD_P))

    return pl.pallas_call(
        body, name="prep_repl", in_specs=[VMEM_WHOLE] * 5, out_specs=[VMEM_WHOLE] * 5,
        out_shape=[jax.ShapeDtypeStruct((RNN_W, RNN_W), MXU_DTYPE), jax.ShapeDtypeStruct((RNN_W, RNN_W), MXU_DTYPE),
                   jax.ShapeDtypeStruct((1, LANES), F32), jax.ShapeDtypeStruct((1, LANES), F32),
                   jax.ShapeDtypeStruct((1, SSD_INNER), F32)],
    )(ga, gx, dt_bias, a_log, d)


LOSS_ROW = 11


def _pack_small(dvec0, g_wa, g_wx, dqnw, dknw, dgb0, dvec1, dcw1, dnw, dgb1, loss8):
    def body(dvec0_ref, gwa_ref, gwx_ref, dqn_ref, dkn_ref, dgb0_ref, dvec1_ref, dcw1_ref, dnw_ref, dgb1_ref,
             loss_ref, sm_ref, vec_ref, gg_ref):
        sm_ref[...] = jnp.zeros_like(sm_ref)
        vec_ref[...] = jnp.zeros_like(vec_ref)
        sharded = ((dvec0_ref, 4), (dcw1_ref, 0), (dcw1_ref, 4), (dnw_ref, 0), (dgb1_ref, 0), (dgb1_ref, 1))
        for (_, r0, nr, c), (src, sr) in zip(SMALL, sharded):
            for j in range(N_DEV):
                sm_ref[j, r0:r0 + nr, 0:c] = src[sr:sr + nr, j * c:(j + 1) * c]
        vectors = ((dvec0_ref, 3), (dvec0_ref, 0), (dvec0_ref, 1), (dvec0_ref, 2), (dqn_ref, 0), (dkn_ref, 0),
                   (dgb0_ref, 0), (dgb0_ref, 1), (dvec1_ref, 0), (dvec1_ref, 1), (dvec1_ref, 2))
        for row, ((_, c), (src, sr)) in enumerate(zip(VECS, vectors)):
            vec_ref[row:row + 1, 0:c] = src[sr:sr + 1, 0:c]
        vec_ref[LOSS_ROW:LOSS_ROW + 1, 0:LANES] = loss_ref[0:1, :]
        for hd in range(8):
            hs = slice(hd * 64, (hd + 1) * 64)
            gg_ref[hs, 0:64] = _mx(gwa_ref[hs, hs])
            gg_ref[hs, 64:128] = _mx(gwx_ref[hs, hs])

    return pl.pallas_call(
        body, name="pack_small", in_specs=[VMEM_WHOLE] * 11, out_specs=[VMEM_WHOLE] * 3,
        out_shape=[jax.ShapeDtypeStruct((N_DEV, 16, 384), F32), jax.ShapeDtypeStruct((16, 1024), F32),
                   jax.ShapeDtypeStruct((RNN_W, LANES), MXU_DTYPE)],
    )(dvec0, g_wa, g_wx, dqnw, dknw, dgb0, dvec1, dcw1, dnw, dgb1, loss8)


def _adamw_small(recv_sm, recv_vec, recv_gg, wmv):
    plan = ([(0, r0, nr, c) for _, r0, nr, c in SMALL] + [(1, row, 1, c) for row, (_, c) in enumerate(VECS)]
            + [(2, 0, RNN_W, 0), (2, 0, RNN_W, 64)])
    n = len(plan)

    def body(*refs):
        recv, ins, outs = refs[:3], refs[3:3 + 3 * n], refs[3 + 3 * n:]
        for i, (src, r0, nr, c) in enumerate(plan):
            cols = slice(c, c + 64) if src == 2 else slice(0, c)
            g = recv[src][0, r0:r0 + nr, cols].astype(F32)
            for s in range(1, N_DEV):
                g = g + recv[src][s, r0:r0 + nr, cols].astype(F32)
            w_ref, m_ref, v_ref = ins[3 * i:3 * i + 3]
            outs[4 * i][...] = g
            outs[4 * i + 1][...], outs[4 * i + 2][...], outs[4 * i + 3][...] = _adamw_math(
                g, w_ref[...], m_ref[...], v_ref[...])
        loss = recv[1][0, LOSS_ROW:LOSS_ROW + 1, 0:LANES]
        for s in range(1, N_DEV):
            loss = loss + recv[1][s, LOSS_ROW:LOSS_ROW + 1, 0:LANES]
        outs[4 * n][...] = loss

    flat = [a for t in wmv for a in t]
    return pl.pallas_call(
        body, name="adamw_small", in_specs=[VMEM_WHOLE] * (3 + 3 * n), out_specs=[VMEM_WHOLE] * (4 * n + 1),
        out_shape=[jax.ShapeDtypeStruct(t[0].shape, F32) for t in wmv for _ in range(4)]
        + [jax.ShapeDtypeStruct((1, LANES), F32)],
    )(recv_sm, recv_vec, recv_gg, *flat)


BIG_L0 = ("ab_w_in", "ab_w_out", "mla_w_uq", "mla_w_ukv")
BIG_L1 = ("ssd_w_in", "ssd_w_out")

MAP_W0 = ((0, 512, 0, 1024), (512, 1536, 0, 0), (1536, 1920, 0, 1536), (1920, 1952, 0, 1984))
MAP_W1 = ((0, 2048, 0, 0), (2048, 5120, 1, 0), (5120, 5152, 2, 0))
MAP_WQ = tuple((96 * hd, 96 * hd + 96, 0, 128 * hd) for hd in range(8))
MAP_WKV = (tuple((128 * hd, 128 * hd + 64, 0, 128 * hd) for hd in range(8))
           + tuple((128 * hd + 64, 128 * hd + 128, 0, 1024 + 64 * hd) for hd in range(8)))
MAP_G0 = ((0, 512, 0, 0), (512, 1536, 1, 0), (1536, 1920, 2, 0), (1920, 1952, 2, 448))
W0_EARLY, W0_LATE = (0, 6), (6, 8)


def kernel(x, positions, ab_w_in, ab_conv_w, ab_conv_b, ab_gate_a_w, ab_gate_a_b, ab_gate_x_w, ab_gate_x_b, ab_lambda, mla_q_norm, mla_kv_norm, mla_w_uq, mla_w_ukv, ab_w_out, ab_ln_g, ab_ln_b, ssd_w_in, ssd_conv_w, ssd_conv_b, ssd_dt_bias, ssd_a_log, ssd_d, ssd_norm, ssd_w_out, ssd_ln_g, ssd_ln_b, loss_target, m_ab_w_in, m_ab_conv_w, m_ab_conv_b, m_ab_gate_a_w, m_ab_gate_a_b, m_ab_gate_x_w, m_ab_gate_x_b, m_ab_lambda, m_mla_q_norm, m_mla_kv_norm, m_mla_w_uq, m_mla_w_ukv, m_ab_w_out, m_ab_ln_g, m_ab_ln_b, m_ssd_w_in, m_ssd_conv_w, m_ssd_conv_b, m_ssd_dt_bias, m_ssd_a_log, m_ssd_d, m_ssd_norm, m_ssd_w_out, m_ssd_ln_g, m_ssd_ln_b, v_ab_w_in, v_ab_conv_w, v_ab_conv_b, v_ab_gate_a_w, v_ab_gate_a_b, v_ab_gate_x_w, v_ab_gate_x_b, v_ab_lambda, v_mla_q_norm, v_mla_kv_norm, v_mla_w_uq, v_mla_w_ukv, v_ab_w_out, v_ab_ln_g, v_ab_ln_b, v_ssd_w_in, v_ssd_conv_w, v_ssd_conv_b, v_ssd_dt_bias, v_ssd_a_log, v_ssd_d, v_ssd_norm, v_ssd_w_out, v_ssd_ln_g, v_ssd_ln_b):
    args = dict(locals())
    bf = MXU_DTYPE
    big = {n: [args[pre + n][0] for pre in ("", "m_", "v_")] for n in BIG_L0 + BIG_L1}
    sml = {n: [_view2d(n, args[pre + n]) for pre in ("", "m_", "v_")] for n in SMALL_NAMES}

    w0_8, cw0_8 = _all_gather([big["ab_w_in"][0].astype(bf), sml["ab_conv_w"][0]], "gather_params")
    p = {"cw0_8": cw0_8, "l0_blocks": [big[n][0].astype(bf) for n in BIG_L0[1:]] + [sml[n][0] for n, *_ in SMALL[1:]]}
    p["w0p"], = _unshard(w0_8, MAP_W0, (2048,), "unshard_w0")
    p["wa"], p["wx"], p["dt_bias"], p["a_log"], p["d_x"] = _prep_repl(
        sml["ab_gate_a_w"][0], sml["ab_gate_x_w"][0], sml["ssd_dt_bias"][0], sml["ssd_a_log"][0], sml["ssd_d"][0])
    for key, n in (("cb0", "ab_conv_b"), ("ba", "ab_gate_a_b"), ("bx", "ab_gate_x_b"), ("lam", "ab_lambda"),
                   ("qn_w", "mla_q_norm"), ("kn_w", "mla_kv_norm"), ("g0", "ab_ln_g"), ("b0", "ab_ln_b")):
        p[key] = sml[n][0]

    _, recv_early, recv, _, grad_x = _local_step(
        x[0], positions[0], loss_target[0], p, [big[n][0].astype(bf) for n in BIG_L1])

    me = 4 * lax.axis_index("x") + 2 * lax.axis_index("y") + lax.axis_index("c")
    parts = {"ssd_w_in": recv_early[0], "ssd_w_out": recv_early[1], "ab_w_out": recv_early[2],
             "ab_w_in": jnp.where(me >= W0_LATE[0], recv[0], recv_early[3]), "mla_w_uq": recv[1], "mla_w_ukv": recv[2]}

    outs = {}
    kinds = ("grad", "delta", "new_m", "new_v")
    for n in BIG_L0 + BIG_L1:
        for kind, res in zip(kinds, _adamw(parts[n], *big[n], "adamw_" + n)):
            outs[kind, n] = res[None]
    res = _adamw_small(*recv[3:], [sml[n] for n in SMALL_NAMES])
    for i, n in enumerate(SMALL_NAMES):
        for k, kind in enumerate(kinds):
            outs[kind, n] = res[4 * i + k].reshape(args[n].shape)

    loss = res[4 * len(SMALL_NAMES)][0, 0]
    order = ["ab_w_in", "ab_conv_w", "ab_conv_b", "ab_gate_a_w", "ab_gate_a_b", "ab_gate_x_w", "ab_gate_x_b",
             "ab_lambda", "mla_q_norm", "mla_kv_norm", "mla_w_uq", "mla_w_ukv", "ab_w_out", "ab_ln_g", "ab_ln_b",
             "ssd_w_in", "ssd_conv_w", "ssd_conv_b", "ssd_dt_bias", "ssd_a_log", "ssd_d", "ssd_norm", "ssd_w_out",
             "ssd_ln_g", "ssd_ln_b"]
    return (loss, grad_x[None], *[outs[kind, n] for kind in ("grad", "delta", "new_m", "new_v") for n in order])


def _local_step(x, pos, target, p, l1_blocks):
    bf = MXU_DTYPE
    inv_freq = 10000.0 ** (-jnp.arange(0, 32, 2, dtype=F32) / 32)
    ang = pos.astype(F32)[:, None] * inv_freq
    cos, sin = jnp.cos(ang), jnp.sin(ang)
    zeros = lambda n: jnp.zeros((SEQ, n), F32)
    tc = jnp.concatenate([jnp.ones((SEQ, 64), F32), cos, cos, zeros(32)], axis=1)
    tsa = jnp.concatenate([zeros(64), -sin, zeros(48)], axis=1)
    tsb = jnp.concatenate([zeros(80), sin, zeros(32)], axis=1)

    w0p, wa, wxg = (p[k] for k in ("w0p", "wa", "wx"))
    cb0, ba, bx, lam = (p[k] for k in ("cb0", "ba", "bx", "lam"))
    qn_w, kn_w, g0, b0 = (p[k] for k in ("qn_w", "kn_w", "g0", "b0"))
    dt_bias, a_log, d_x = (p[k] for k in ("dt_bias", "a_log", "d_x"))
    tril = jnp.tril(jnp.ones((SSD_L, SSD_L), F32))
    expand_t = (jnp.arange(SSD_INNER)[:, None] // SSD_P == jnp.arange(LANES)[None, :]).astype(jnp.bfloat16)

    proj0, xb, l0_8 = _l0_in(x, w0p, bcast=p["l0_blocks"])
    wo0 = l0_8[0].reshape(D_MODEL, D_MODEL)
    wq, = _unshard(l0_8[1], MAP_WQ, (1024,), "unshard_wq")
    wkv, = _unshard(l0_8[2], MAP_WKV, (1536,), "unshard_wkv")
    cw0, cw1, cb1, nw, g1, b1 = _unshard_small([p["cw0_8"]] + list(l0_8[3:]))
    xc, h = _rglru_fwd(proj0, cw0, cb0, wa, ba, wxg, bx, lam)
    qn, kn, qc, kc, vc = _mla_fwd(proj0, qn_w, kn_w, wq, wkv, tc, tsa, tsb)
    o, lse, (w1_8, wo1_8) = _flash_fwd(qc, kc, vc, bcast=l1_blocks)
    w1z, w1x, w1d = _unshard(w1_8, MAP_W1, (2048, 3072, 128), "unshard_w1")
    wo1 = wo1_8.reshape(SSD_INNER, D_MODEL)
    y0, v0, x1, x1b = _l0_out(h, o, proj0, x, wo0, g0, b0)

    z, xbc, dt_raw = _l1_in(x1b, w1z, w1x, w1d)
    pre, act = _ssd_conv_fwd(xbc, cw1, cb1)
    ys, hprev = _ssd_scan_fwd(act, dt_raw, dt_bias, a_log, d_x, tril, expand_t)
    yn, dv1, dgb1, loss8 = _l1_out(ys, z, nw, wo1, x1, g1, b1, target)

    g_wo1 = _dw(yn, dv1, name="l1_dwout")
    dys, dz, dnw = _l1_gate_bwd(dv1, wo1, ys, z, nw)
    dact, ddt_raw, dvec1 = _ssd_scan_bwd(dys, act, dt_raw, hprev, dt_bias, a_log, d_x, tril, expand_t)
    dxbc, dcw1 = _ssd_conv_bwd(dact, pre, xbc, cw1)
    g_z, g_xbc = _dw(x1b, dz, name="l1_dw_z"), _dw(x1b, dxbc, name="l1_dw_xbc")
    g_dt = _dw(x1b, ddt_raw, name="l1_dw_dt")

    dv0, dgb0 = _l1_dx_ln(dz, dxbc, ddt_raw, dv1, v0, w1z, w1x, w1d, g0)
    g_wo0 = _dw(y0, dv0, name="l0_dwout")
    dh, do, dgate = _gate_bwd(dv0, wo0, h, o, proj0)
    dxr, g_wa, g_wx, dvec0 = _rglru_bwd(dh, xc, h, proj0, cw0, wa, ba, wxg, bx, lam)
    g_rnn, g_gate = _dw(xb, dxr, name="l0_dw_rnn"), _dw(xb, dgate, name="l0_dw_gate")
    early = [_reshard([g_z, g_xbc, g_dt], MAP_W1, 644, bf, "reshard_w1"), g_wo1.astype(bf).reshape(N_DEV, 256, D_MODEL),
             g_wo0.astype(bf).reshape(N_DEV, 128, D_MODEL),
             (_reshard([g_rnn, g_gate], MAP_G0, 244, bf, "reshard_w0_early", shards=W0_EARLY), W0_EARLY)]
    dq, dk, dvv, recv_early = _flash_bwd(qc, kc, vc, o, do, lse, scatter=early)
    dtail, g_wq, g_wkv, dqnw, dknw = _mla_bwd(dq, dk, dvv, proj0, qn, kn, qn_w, kn_w, wq, wkv, tc, tsa, tsb)
    g_tail = _dw(xb, dtail, name="l0_dw_tail")

    acc = {"g_rnn": g_rnn, "g_gate": g_gate, "g_tail": g_tail, "g_wq": g_wq, "g_wkv": g_wkv,
           "dvec0": dvec0, "g_wa": g_wa, "g_wx": g_wx, "dqnw": dqnw, "dknw": dknw, "dgb0": dgb0, "dvec1": dvec1,
           "dcw1": dcw1, "dnw": dnw, "dgb1": dgb1}
    late = [(_reshard([g_rnn, g_gate, g_tail], MAP_G0, 244, bf, "reshard_w0_late", shards=W0_LATE), W0_LATE),
            _reshard([g_wq], MAP_WQ, 96, bf, "reshard_wq"), _reshard([g_wkv], MAP_WKV, 128, bf, "reshard_wkv")]
    sm_slots, vec_rows, gates = _pack_small(dvec0, g_wa, g_wx, dqnw, dknw, dgb0, dvec1, dcw1, dnw, dgb1, loss8)
    dx, recv_late = _l0_dx(dxr, dgate, dtail, w0p, dv0, scatter=late + [sm_slots], bcast=[vec_rows, gates])
    return acc, recv_early, recv_late, loss8[0, 0], dx
```

```python
import math

import jax
import jax.numpy as jnp
from jax import lax
from jax.experimental import pallas as pl
from jax.experimental.pallas import tpu as pltpu

F32 = jnp.float32
MXU_DTYPE = jnp.bfloat16

N_DEV = 8
SEQ = 4096
D_MODEL = 1024
DN_ALPHA = 4.0 ** 0.25
RNN_W = 512
MLA_HEADS = 8
ATT_SCALE = 96.0 ** -0.5
ATT_C = ATT_SCALE * math.log2(math.e)
RG_C = 8.0
SSD_INNER = 2048
SSD_HEADS = 32
SSD_P = 64
SSD_GROUPS = 4
SSD_N = 128
SSD_L = 128
SSD_CONV = 3072
LANES = 128
SUBLANES = 8
VMEM_LIMIT = 56 * 1024 * 1024

ADAM_LR, ADAM_B1, ADAM_B2, ADAM_EPS, ADAM_WD, ADAM_STEP = 0.001, 0.9, 0.999, 1e-08, 0.01, 10

HIGHEST = lax.Precision.HIGHEST


def _params(sem, limit=VMEM_LIMIT):
    return pltpu.CompilerParams(dimension_semantics=sem, vmem_limit_bytes=limit)


def _dot(a, b):
    return lax.dot_general(a, b, (((1,), (0,)), ((), ())), preferred_element_type=F32)


def _dot_nt(a, b):
    return lax.dot_general(a, b, (((1,), (1,)), ((), ())), preferred_element_type=F32)


def _dot_tn(a, b):
    return lax.dot_general(a, b, (((0,), (0,)), ((), ())), preferred_element_type=F32)


def _dot_hi(a, b):
    return lax.dot_general(a, b, (((1,), (0,)), ((), ())), precision=HIGHEST, preferred_element_type=F32)


def _mx(v):
    return v.astype(MXU_DTYPE)


def _sigmoid(v):
    return 1.0 / (1.0 + jnp.exp(-v))


def _log1p_pos(e):
    poly = e * (1.0 - e * (0.5 - e * (1.0 / 3.0 - e * 0.25)))
    return jnp.where(e < 0.01, poly, jnp.log(1.0 + e))


def _softplus(v):
    return jnp.maximum(v, 0.0) + _log1p_pos(jnp.exp(-jnp.abs(v)))


def _neg_expm1(v):
    poly = -v * (1.0 + v * (0.5 + v * (1.0 / 6.0 + v * (1.0 / 24.0 + v * (1.0 / 120.0)))))
    return jnp.where(jnp.abs(v) < 0.1, poly, 1.0 - jnp.exp(v))


def _silu(v):
    return v * _sigmoid(v)


def _dsilu(v):
    s = _sigmoid(v)
    return s * (1.0 + v * (1.0 - s))


def _dw(a, b, *, name, tm=1024, tn=1024, tk=512):
    kdim, m = a.shape
    n = b.shape[1]
    tm, tn, tk = min(tm, m), min(tn, n), min(tk, kdim)

    def body(a_ref, b_ref, o_ref):
        @pl.when(pl.program_id(2) == 0)
        def _():
            o_ref[...] = jnp.zeros_like(o_ref)

        o_ref[...] += _dot_tn(_mx(a_ref[...]), _mx(b_ref[...]))

    return pl.pallas_call(
        body, name=name, grid=(m // tm, n // tn, kdim // tk),
        in_specs=[pl.BlockSpec((tk, tm), lambda i, j, k: (k, i)), pl.BlockSpec((tk, tn), lambda i, j, k: (k, j))],
        out_specs=pl.BlockSpec((tm, tn), lambda i, j, k: (i, j)),
        out_shape=jax.ShapeDtypeStruct((m, n), F32),
        compiler_params=_params(("parallel", "parallel", "arbitrary")),
    )(a, b)


def _shift_down(blk, halo, s):
    if s == 0:
        return blk
    t = blk.shape[0]
    r = pltpu.roll(blk, s, 0)
    hr = pltpu.roll(halo, s, 0)
    row8 = lax.broadcasted_iota(jnp.int32, hr.shape, 0)
    head = jnp.where(row8 < s, hr, r[:SUBLANES])
    return jnp.concatenate([head, r[SUBLANES:]], axis=0) if t > SUBLANES else head


def _shift_up(blk, halo, s):
    if s == 0:
        return blk
    t = blk.shape[0]
    r = pltpu.roll(blk, t - s, 0)
    hr = pltpu.roll(halo, SUBLANES - s, 0)
    row8 = lax.broadcasted_iota(jnp.int32, hr.shape, 0)
    tail = jnp.where(row8 >= SUBLANES - s, hr, r[t - SUBLANES:])
    return jnp.concatenate([r[:t - SUBLANES], tail], axis=0) if t > SUBLANES else tail


def _scan_down(a, u):
    t = a.shape[0]
    row = lax.broadcasted_iota(jnp.int32, a.shape, 0)
    d = 1
    while d < t:
        keep = row >= d
        a_sh = jnp.where(keep, pltpu.roll(a, d, 0), 1.0)
        u_sh = jnp.where(keep, pltpu.roll(u, d, 0), 0.0)
        u = a * u_sh + u
        a = a * a_sh
        d *= 2
    return a, u


def _scan_up(a, u):
    t = a.shape[0]
    row = lax.broadcasted_iota(jnp.int32, a.shape, 0)
    d = 1
    while d < t:
        keep = row < t - d
        a_sh = jnp.where(keep, pltpu.roll(a, t - d, 0), 1.0)
        u_sh = jnp.where(keep, pltpu.roll(u, t - d, 0), 0.0)
        u = a * u_sh + u
        a = a * a_sh
        d *= 2
    return a, u


def _conv4(blk, halo, cw, cb):
    out = cb + blk * cw[3:4]
    for k in range(3):
        out = out + _shift_down(blk, halo, 3 - k) * cw[k:k + 1]
    return out


RG_T = 512
P0_RNN = 2


def _rg_gates(xc, wa, ba, wx, bx, lam):
    xcb = _mx(xc)
    r = _sigmoid(_dot(xcb, wa) + ba)
    ig = _sigmoid(_dot(xcb, wx) + bx)
    sp = _softplus(-lam)
    la = (-RG_C * r) * sp
    a = jnp.exp(la)
    mult = jnp.sqrt(_neg_expm1(2.0 * la))
    return r, ig, sp, a, mult


def _rglru_fwd(proj0, cw8, cb, wa, ba, wx, bx, lam):
    t, w = RG_T, RNN_W
    nb = SEQ // t

    def body(x_ref, halo_ref, cw_ref, cb_ref, wa_ref, ba_ref, wx_ref, bx_ref, lam_ref, xc_ref, h_ref, carry):
        i = pl.program_id(0)

        @pl.when(i == 0)
        def _():
            carry[...] = jnp.zeros_like(carry)

        blk = x_ref[...]
        halo = jnp.where(i > 0, halo_ref[...], 0.0)
        xc = _conv4(blk, halo, cw_ref[...], cb_ref[...])
        _, ig, _, a, mult = _rg_gates(xc, wa_ref[...], ba_ref[...], wx_ref[...], bx_ref[...], lam_ref[...])
        u = mult * (ig * xc)
        big_a, big_u = _scan_down(a, u)
        h = big_a * carry[SUBLANES - 1:SUBLANES, :] + big_u
        carry[...] = h[t - SUBLANES:]
        xc_ref[...] = xc
        h_ref[...] = h

    vec = pl.BlockSpec((1, w), lambda i: (0, 0))
    mat = pl.BlockSpec((w, w), lambda i: (0, 0))
    return pl.pallas_call(
        body, name="rglru_fwd", grid=(nb,),
        in_specs=[pl.BlockSpec((t, w), lambda i: (i, P0_RNN)),
                  pl.BlockSpec((SUBLANES, w), lambda i: (jnp.maximum(i * (t // SUBLANES) - 1, 0), P0_RNN)),
                  pl.BlockSpec((SUBLANES, w), lambda i: (0, 0)), vec, mat, vec, mat, vec, vec],
        out_specs=[pl.BlockSpec((t, w), lambda i: (i, 0)), pl.BlockSpec((t, w), lambda i: (i, 0))],
        out_shape=[jax.ShapeDtypeStruct((SEQ, w), F32), jax.ShapeDtypeStruct((SEQ, w), F32)],
        scratch_shapes=[pltpu.VMEM((SUBLANES, w), F32)],
        compiler_params=_params(("arbitrary",)),
    )(proj0, proj0, cw8, cb, wa, ba, wx, bx, lam)


def _rglru_bwd(dh, xc, h, proj0, cw8, wa, ba, wx, bx, lam):
    t, w = RG_T, RNN_W
    nb = SEQ // t
    tb = t // SUBLANES

    def body(dh_ref, xc_ref, h_ref, hh_ref, x_ref, cw_ref, wa_ref, ba_ref, wx_ref, bx_ref, lam_ref,
             dx_ref, dwa_ref, dwx_ref, dvec_ref, gcarry, dxc_next):
        i = pl.program_id(0)
        rev = nb - 1 - i

        @pl.when(i == 0)
        def _():
            gcarry[...] = jnp.zeros_like(gcarry)
            dxc_next[...] = jnp.zeros_like(dxc_next)
            dwa_ref[...] = jnp.zeros_like(dwa_ref)
            dwx_ref[...] = jnp.zeros_like(dwx_ref)
            dvec_ref[...] = jnp.zeros_like(dvec_ref)

        xc = xc_ref[...]
        wa_v, wx_v = wa_ref[...], wx_ref[...]
        lam_v = lam_ref[...]
        r, ig, sp, a, mult = _rg_gates(xc, wa_v, ba_ref[...], wx_v, bx_ref[...], lam_v)
        dhv = dh_ref[...]
        big_a, big_u = _scan_up(a, a * dhv)
        gg = big_a * gcarry[0:1, :] + big_u
        g = dhv + _shift_up(gg, gcarry[...], 1)
        gcarry[...] = gg[:SUBLANES]
        hhalo = jnp.where(rev > 0, hh_ref[...], 0.0)
        da = g * _shift_down(h_ref[...], hhalo, 1)
        d_mult = g * (ig * xc)
        d_i = g * (mult * xc)
        dxc = g * (mult * ig)
        d_la = da * a - d_mult * (a * a) / mult
        d_r = d_la * (-RG_C * sp)
        d_sp = jnp.sum(d_la * (-RG_C * r), axis=0, keepdims=True)
        d_pa = d_r * r * (1.0 - r)
        d_px = d_i * ig * (1.0 - ig)
        d_pab, d_pxb = _mx(d_pa), _mx(d_px)
        dxc = dxc + _dot_nt(d_pab, wa_v) + _dot_nt(d_pxb, wx_v)
        xcb = _mx(xc)
        dwa_ref[...] += _dot_tn(xcb, d_pab)
        dwx_ref[...] += _dot_tn(xcb, d_pxb)
        dvec_ref[0:1, :] += jnp.sum(d_pa, axis=0, keepdims=True)
        dvec_ref[1:2, :] += jnp.sum(d_px, axis=0, keepdims=True)
        dvec_ref[2:3, :] += d_sp * (-_sigmoid(-lam_v))
        dvec_ref[3:4, :] += jnp.sum(dxc, axis=0, keepdims=True)
        xblk = x_ref[...]
        cw = cw_ref[...]
        dx = dxc * cw[3:4]
        nxt = dxc_next[...]
        dvec_ref[7:8, :] += jnp.sum(dxc * xblk, axis=0, keepdims=True)
        for k in range(3):
            up = _shift_up(dxc, nxt, 3 - k)
            dvec_ref[4 + k:5 + k, :] += jnp.sum(up * xblk, axis=0, keepdims=True)
            dx = dx + up * cw[k:k + 1]
        dxc_next[...] = dxc[:SUBLANES]
        dx_ref[...] = _mx(dx)

    blk = pl.BlockSpec((t, w), lambda i: (nb - 1 - i, 0))
    halo = pl.BlockSpec((SUBLANES, w), lambda i: (jnp.maximum((nb - 1 - i) * tb - 1, 0), 0))
    vec = pl.BlockSpec((1, w), lambda i: (0, 0))
    mat = pl.BlockSpec((w, w), lambda i: (0, 0))
    return pl.pallas_call(
        body, name="rglru_bwd", grid=(nb,),
        in_specs=[blk, blk, blk, halo, pl.BlockSpec((t, w), lambda i: (nb - 1 - i, P0_RNN)),
                  pl.BlockSpec((SUBLANES, w), lambda i: (0, 0)), mat, vec, mat, vec, vec],
        out_specs=[blk, mat, mat, pl.BlockSpec((16, w), lambda i: (0, 0))],
        out_shape=[jax.ShapeDtypeStruct((SEQ, w), MXU_DTYPE), jax.ShapeDtypeStruct((w, w), F32),
                   jax.ShapeDtypeStruct((w, w), F32), jax.ShapeDtypeStruct((16, w), F32)],
        scratch_shapes=[pltpu.VMEM((SUBLANES, w), F32), pltpu.VMEM((SUBLANES, w), F32)],
        compiler_params=_params(("arbitrary",)),
    )(dh, xc, h, h, proj0, cw8, wa, ba, wx, bx, lam)


MLA_T = 512


def _rope(v, c, sa, sb):
    return v * c + pltpu.roll(v, LANES - 16, 1) * sa + pltpu.roll(v, 16, 1) * sb


def _rope_t(dv, c, sa, sb):
    return dv * c + pltpu.roll(dv * sa, 16, 1) + pltpu.roll(dv * sb, LANES - 16, 1)


def _rms(v, g, eps=1e-6):
    rs = lax.rsqrt(jnp.mean(v * v, axis=-1, keepdims=True) + eps)
    return v * rs * g, rs


def _mla_fwd(proj0, q_norm, kv_norm, wq, wkv, tc, tsa, tsb):
    t = MLA_T

    def body(cq_ref, ck_ref, qn_ref, kn_ref, wq_ref, wkv_ref, c_ref, sa_ref, sb_ref,
             oqn_ref, okn_ref, oq_ref, ok_ref, ov_ref):
        c, sa, sb = c_ref[...], sa_ref[...], sb_ref[...]
        ck = ck_ref[...]
        qn = _mx(_rms(cq_ref[...], qn_ref[...])[0])
        kn = _mx(_rms(ck[:, :LANES], kn_ref[...])[0])
        oqn_ref[...] = qn
        okn_ref[...] = kn
        krv = _rope(ck[:, LANES:], c, sa, sb)
        qraw = _dot(qn, wq_ref[...])
        kvraw = _dot(kn, wkv_ref[...])
        for hd in range(MLA_HEADS):
            sl = slice(hd * LANES, (hd + 1) * LANES)
            oq_ref[:, sl] = _mx(_rope(qraw[:, sl], c, sa, sb))
            ok_ref[:, sl] = _mx(kvraw[:, sl] + krv)
        ov_ref[...] = _mx(kvraw[:, 1024:])

    tab = pl.BlockSpec((t, LANES), lambda i: (i, 0))
    wide = pl.BlockSpec((t, 1024), lambda i: (i, 0))
    const = lambda shape: pl.BlockSpec(shape, lambda i: (0, 0))
    return pl.pallas_call(
        body, name="mla_fwd", grid=(SEQ // t,),
        in_specs=[pl.BlockSpec((t, 256), lambda i: (i, 6)), pl.BlockSpec((t, 256), lambda i: (i, 7)),
                  const((1, 256)), const((1, LANES)), const((256, 1024)), const((LANES, 1536)), tab, tab, tab],
        out_specs=[pl.BlockSpec((t, 256), lambda i: (i, 0)), tab, wide, wide, pl.BlockSpec((t, 512), lambda i: (i, 0))],
        out_shape=[jax.ShapeDtypeStruct((SEQ, 256), MXU_DTYPE), jax.ShapeDtypeStruct((SEQ, LANES), MXU_DTYPE),
                   jax.ShapeDtypeStruct((SEQ, 1024), MXU_DTYPE), jax.ShapeDtypeStruct((SEQ, 1024), MXU_DTYPE),
                   jax.ShapeDtypeStruct((SEQ, 512), MXU_DTYPE)],
        compiler_params=_params(("parallel",)),
    )(proj0, proj0, q_norm, kv_norm, wq, wkv, tc, tsa, tsb)


ATT_T = 1024


def _flash_fwd(q, k, v, bcast=()):
    t = ATT_T
    nb = SEQ // t

    steps = [(qi, ki) for qi in range(nb) for ki in range(qi + 1)]
    qi_tab = jnp.asarray([s[0] for s in steps], jnp.int32)
    ki_tab = jnp.asarray([s[1] for s in steps], jnp.int32)

    nx = len(bcast)

    def body(qi_ref, ki_ref, q_ref, k_ref, v_ref, *rest):
        x_refs, (o_ref, lse_ref), g_refs = rest[:nx], rest[nx:nx + 2], rest[nx + 2:2 * nx + 2]
        m_sc, acc_sc = rest[2 * nx + 2:2 * nx + 4]
        step = pl.program_id(1)
        qi, ki = qi_ref[step], ki_ref[step]
        if nx:
            copies = _peer_copies(x_refs, g_refs, rest[2 * nx + 4:], [])

            @pl.when((pl.program_id(0) == 0) & (step == 0))
            def _():
                for cp in copies:
                    cp.start()

        @pl.when(ki == 0)
        def _():
            m_sc[...] = jnp.full_like(m_sc, -jnp.inf)
            acc_sc[...] = jnp.zeros_like(acc_sc)

        def update(diagonal):
            vv = v_ref[...]
            lane_v = lax.broadcasted_iota(jnp.int32, vv.shape, 1)
            for hd in range(2):
                sl = slice(hd * LANES, (hd + 1) * LANES)
                s = _dot_nt(q_ref[:, sl], k_ref[:, sl])
                if diagonal:
                    s = jnp.where(lax.broadcasted_iota(jnp.int32, (t, t), 1)
                                  <= lax.broadcasted_iota(jnp.int32, (t, t), 0), s, -jnp.inf)
                m_prev = m_sc[hd]
                m_new = jnp.maximum(m_prev, jnp.max(s, axis=1, keepdims=True))
                p = jnp.exp2((s - m_new[:, :1]) * ATT_C)
                m_sc[hd] = m_new
                vh = jnp.where((lane_v >= hd * 64) & (lane_v < (hd + 1) * 64), vv, jnp.ones_like(vv))
                acc_sc[hd] = acc_sc[hd] * jnp.exp2((m_prev - m_new) * ATT_C) + _dot(_mx(p), vh)

        @pl.when(ki < qi)
        def _():
            update(False)

        @pl.when(ki == qi)
        def _():
            update(True)
            first = lax.broadcasted_iota(jnp.int32, (t, LANES), 1) < 64
            a0, a1 = acc_sc[0], acc_sc[1]
            l0, l1 = pltpu.roll(a0, 64, 1), pltpu.roll(a1, 64, 1)
            o_ref[...] = jnp.where(first, a0 / l0, a1 / l1)
            lse_ref[0] = jnp.where(first, m_sc[0] * ATT_SCALE + jnp.log(l0), m_sc[1] * ATT_SCALE + jnp.log(l1))

        if nx:
            @pl.when((pl.program_id(0) == 3) & (step == len(steps) - 1))
            def _():
                for cp in copies:
                    cp.wait()

    grid_spec = pltpu.PrefetchScalarGridSpec(
        num_scalar_prefetch=2, grid=(4, len(steps)),
        in_specs=[pl.BlockSpec((t, 256), lambda p, s, qt, kt: (qt[s], p)),
                  pl.BlockSpec((t, 256), lambda p, s, qt, kt: (kt[s], p)),
                  pl.BlockSpec((t, LANES), lambda p, s, qt, kt: (kt[s], p))] + [ANY] * nx,
        out_specs=[pl.BlockSpec((t, LANES), lambda p, s, qt, kt: (qt[s], p)),
                   pl.BlockSpec((1, t, LANES), lambda p, s, qt, kt: (p, qt[s], 0))] + [ANY] * nx,
        scratch_shapes=[pltpu.VMEM((2, t, LANES), F32), pltpu.VMEM((2, t, LANES), F32)]
        + (_exchange_sems(nx) if nx else []))
    res = pl.pallas_call(
        body, name="flash_fwd", grid_spec=grid_spec,
        out_shape=[jax.ShapeDtypeStruct((SEQ, 512), F32), jax.ShapeDtypeStruct((4, SEQ, LANES), F32)]
        + _exchange_shapes([], bcast),
        compiler_params=_params(("arbitrary", "arbitrary")),
    )(qi_tab, ki_tab, q, k, v, *bcast)
    return res[0], res[1], res[2:]


def _flash_bwd(q, k, v, o, do, lse, scatter=()):
    t = ATT_T
    nb = SEQ // t

    steps = [(qi, ki) for ki in range(nb) for qi in range(ki, nb)]
    qi_tab = jnp.asarray([s[0] for s in steps], jnp.int32)
    ki_tab = jnp.asarray([s[1] for s in steps], jnp.int32)
    log2e = math.log2(math.e)

    sc_arrays, sc_ranges = _scatter_args(scatter)
    nx = len(sc_arrays)

    def body(qi_ref, ki_ref, q_ref, k_ref, v_ref, o_ref, do_ref, lse_ref, *rest):
        x_refs, (dq_ref, dk_ref, dv_ref), g_refs = rest[:nx], rest[nx:nx + 3], rest[nx + 3:2 * nx + 3]
        dkt_sc, dvt_sc = rest[2 * nx + 3:2 * nx + 5]
        step = pl.program_id(1)
        qi, ki = qi_ref[step], ki_ref[step]
        if nx:
            copies = _peer_copies(x_refs, g_refs, rest[2 * nx + 5:], sc_ranges)

            @pl.when((pl.program_id(0) == 0) & (step == 0))
            def _():
                for cp in copies:
                    cp.start()

        @pl.when(step == 0)
        def _():
            dq_ref[...] = jnp.zeros_like(dq_ref)

        @pl.when(qi == ki)
        def _():
            dkt_sc[...] = jnp.zeros_like(dkt_sc)
            dvt_sc[...] = jnp.zeros_like(dvt_sc)

        def update(diagonal):
            dov, ov, vv = do_ref[...], o_ref[...], v_ref[...]
            lse2 = lse_ref[0] * log2e
            lane = lax.broadcasted_iota(jnp.int32, (t, LANES), 1)
            row_t = lax.broadcasted_iota(jnp.int32, (LANES, t), 0)
            prod = dov * ov
            do_b = _mx(dov)
            qrows = pl.ds(pl.multiple_of(qi * t, t), t)
            dvt_acc = jnp.zeros((LANES, t), F32)
            dkt_new, dq_new = [], []
            for hd in range(2):
                sl = slice(hd * LANES, (hd + 1) * LANES)
                mine = (lane >= hd * 64) & (lane < (hd + 1) * 64)
                qh, kh = q_ref[:, sl], k_ref[:, sl]
                p = jnp.exp2(_dot_nt(qh, kh) * ATT_C - lse2[:, hd * 64:hd * 64 + 1])
                if diagonal:
                    p = jnp.where(lax.broadcasted_iota(jnp.int32, (t, t), 1)
                                  <= lax.broadcasted_iota(jnp.int32, (t, t), 0), p, 0.0)
                do_h = jnp.where(mine, dov, 0.0)
                delta = jnp.sum(jnp.where(mine, prod, 0.0), axis=1, keepdims=True)
                dp = _dot_nt(_mx(do_h), vv)
                ds = _mx(p * (dp - delta) * ATT_SCALE)
                dvt_acc = dvt_acc + jnp.where((row_t >= hd * 64) & (row_t < (hd + 1) * 64), _dot_tn(do_b, _mx(p)), 0.0)
                dkt_new.append(_dot_tn(qh, ds))
                dq_new.append(_dot(ds, kh))
            for hd in range(2):
                sl = slice(hd * LANES, (hd + 1) * LANES)
                dkt_sc[sl, :] += dkt_new[hd]
                dq_ref[qrows, sl] += dq_new[hd]
            dvt_sc[...] += dvt_acc

        @pl.when(qi > ki)
        def _():
            update(False)

        @pl.when(qi == ki)
        def _():
            update(True)

        @pl.when(qi == nb - 1)
        def _():
            dk_ref[...] = dkt_sc[...].T
            dv_ref[...] = dvt_sc[...].T

        if nx:
            @pl.when((pl.program_id(0) == 3) & (step == len(steps) - 1))
            def _():
                for cp in copies:
                    cp.wait()

    qmap = lambda p, s, qt, kt: (qt[s], p)
    kmap = lambda p, s, qt, kt: (kt[s], p)
    grid_spec = pltpu.PrefetchScalarGridSpec(
        num_scalar_prefetch=2, grid=(4, len(steps)),
        in_specs=[pl.BlockSpec((t, 256), qmap), pl.BlockSpec((t, 256), kmap), pl.BlockSpec((t, LANES), kmap),
                  pl.BlockSpec((t, LANES), qmap), pl.BlockSpec((t, LANES), qmap),
                  pl.BlockSpec((1, t, LANES), lambda p, s, qt, kt: (p, qt[s], 0))] + [ANY] * nx,
        out_specs=[pl.BlockSpec((SEQ, 256), lambda p, s, qt, kt: (0, p)), pl.BlockSpec((t, 256), kmap),
                   pl.BlockSpec((t, LANES), kmap)] + [ANY] * nx,
        scratch_shapes=[pltpu.VMEM((256, t), F32), pltpu.VMEM((LANES, t), F32)] + (_exchange_sems(nx) if nx else []))
    res = pl.pallas_call(
        body, name="flash_bwd", grid_spec=grid_spec,
        out_shape=[jax.ShapeDtypeStruct((SEQ, 1024), F32), jax.ShapeDtypeStruct((SEQ, 1024), F32),
                   jax.ShapeDtypeStruct((SEQ, 512), F32)] + _exchange_shapes(sc_arrays, []),
        compiler_params=_params(("arbitrary", "arbitrary")),
    )(qi_tab, ki_tab, q, k, v, o, do, lse, *sc_arrays)
    return res[0], res[1], res[2], res[3:]


def _rms_bwd(v, g, dy, eps=1e-6):
    rs = lax.rsqrt(jnp.mean(v * v, axis=-1, keepdims=True) + eps)
    xh = v * rs
    dxh = dy * g
    dv = rs * (dxh - xh * jnp.mean(dxh * xh, axis=-1, keepdims=True))
    return dv, jnp.sum(dy * xh, axis=0, keepdims=True)


def _mla_bwd(dq, dk, dv, proj0, qlat, klat, q_norm, kv_norm, wq, wkv, tc, tsa, tsb):
    t = MLA_T

    def body(dq_ref, dk_ref, dv_ref, cq_ref, ck_ref, ql_ref, kl_ref, qn_ref, kn_ref, wq_ref, wkv_ref,
             c_ref, sa_ref, sb_ref, o_ref, gwq_ref, gwkv_ref, dgq_ref, dgk_ref, oq_ref, okv_ref):
        @pl.when(pl.program_id(0) == 0)
        def _():
            dgq_ref[...] = jnp.zeros_like(dgq_ref)
            dgk_ref[...] = jnp.zeros_like(dgk_ref)
            gwq_ref[...] = jnp.zeros_like(gwq_ref)
            gwkv_ref[...] = jnp.zeros_like(gwkv_ref)

        c, sa, sb = c_ref[...], sa_ref[...], sb_ref[...]
        lane = lax.broadcasted_iota(jnp.int32, (t, LANES), 1)
        dkr = jnp.zeros((t, LANES), F32)
        for hd in range(MLA_HEADS):
            sl = slice(hd * LANES, (hd + 1) * LANES)
            oq_ref[:, sl] = _mx(_rope_t(dq_ref[:, sl], c, sa, sb))
            dkh = dk_ref[:, sl]
            okv_ref[:, sl] = _mx(dkh)
            dkr = dkr + dkh
        okv_ref[:, 1024:] = _mx(dv_ref[...])
        dkr = _rope_t(jnp.where((lane >= 64) & (lane < 96), dkr, 0.0), c, sa, sb)
        dqraw, dkvraw = oq_ref[...], okv_ref[...]
        gwq_ref[...] += _dot_tn(ql_ref[...], dqraw)
        gwkv_ref[...] += _dot_tn(kl_ref[...], dkvraw)
        dqn = _dot_nt(dqraw, wq_ref[...])
        dkn = _dot_nt(dkvraw, wkv_ref[...])
        dcq, dgq = _rms_bwd(cq_ref[...], qn_ref[...], dqn)
        dck, dgk = _rms_bwd(ck_ref[:, :LANES], kn_ref[...], dkn)
        o_ref[:, :256] = _mx(dcq)
        o_ref[:, 256:384] = _mx(dck)
        o_ref[:, 384:] = _mx(dkr)
        dgq_ref[0:1, :] += dgq
        dgk_ref[0:1, :] += dgk

    tab = pl.BlockSpec((t, LANES), lambda i: (i, 0))
    wide = pl.BlockSpec((t, 1024), lambda i: (i, 0))
    const = lambda shape: pl.BlockSpec(shape, lambda i: (0, 0))
    return pl.pallas_call(
        body, name="mla_bwd", grid=(SEQ // t,),
        in_specs=[wide, wide, pl.BlockSpec((t, 512), lambda i: (i, 0)),
                  pl.BlockSpec((t, 256), lambda i: (i, 6)), pl.BlockSpec((t, 256), lambda i: (i, 7)),
                  pl.BlockSpec((t, 256), lambda i: (i, 0)), tab,
                  const((1, 256)), const((1, LANES)), const((256, 1024)), const((LANES, 1536)), tab, tab, tab],
        out_specs=[pl.BlockSpec((t, 512), lambda i: (i, 0)), const((256, 1024)), const((LANES, 1536)),
                   const((SUBLANES, 256)), const((SUBLANES, LANES))],
        out_shape=[jax.ShapeDtypeStruct((SEQ, 512), MXU_DTYPE), jax.ShapeDtypeStruct((256, 1024), F32),
                   jax.ShapeDtypeStruct((LANES, 1536), F32), jax.ShapeDtypeStruct((SUBLANES, 256), F32),
                   jax.ShapeDtypeStruct((SUBLANES, LANES), F32)],
        scratch_shapes=[pltpu.VMEM((t, 1024), MXU_DTYPE), pltpu.VMEM((t, 1536), MXU_DTYPE)],
        compiler_params=_params(("arbitrary",)),
    )(dq, dk, dv, proj0, proj0, qlat, klat, q_norm, kv_norm, wq, wkv, tc, tsa, tsb)


LN_T = 512


def _ln(v, g, b, eps=1e-5):
    mu = jnp.mean(v, axis=-1, keepdims=True)
    xc = v - mu
    rs = lax.rsqrt(jnp.mean(xc * xc, axis=-1, keepdims=True) + eps)
    return xc * rs * g + b


def _ln_bwd(v, g, dy, eps=1e-5):
    mu = jnp.mean(v, axis=-1, keepdims=True)
    xc = v - mu
    rs = lax.rsqrt(jnp.mean(xc * xc, axis=-1, keepdims=True) + eps)
    xh = xc * rs
    dxh = dy * g
    dv = rs * (dxh - jnp.mean(dxh, axis=-1, keepdims=True) - xh * jnp.mean(dxh * xh, axis=-1, keepdims=True))
    return dv, jnp.sum(dy * xh, axis=0, keepdims=True), jnp.sum(dy, axis=0, keepdims=True)


def _l0_out(h, o, proj0, x, w_out, g, b):
    t = LN_T

    def body(h_ref, o_ref, ga_ref, gb_ref, x_ref, w_ref, g_ref, b_ref, y_ref, v_ref, x1_ref, x1b_ref):
        y = _mx(jnp.concatenate([h_ref[...] * _silu(ga_ref[...]), o_ref[...] * _silu(gb_ref[...])], axis=1))
        v = DN_ALPHA * x_ref[...] + _dot(y, w_ref[...])
        y_ref[...] = y
        v_ref[...] = v
        x1 = _ln(v, g_ref[...], b_ref[...])
        x1_ref[...] = x1
        x1b_ref[...] = _mx(x1)

    half = pl.BlockSpec((t, 512), lambda i: (i, 0))
    full = pl.BlockSpec((t, D_MODEL), lambda i: (i, 0))
    vec = pl.BlockSpec((1, D_MODEL), lambda i: (0, 0))
    return pl.pallas_call(
        body, name="l0_out", grid=(SEQ // t,),
        in_specs=[half, half, pl.BlockSpec((t, 512), lambda i: (i, 0)), pl.BlockSpec((t, 512), lambda i: (i, 1)), full,
                  pl.BlockSpec((D_MODEL, D_MODEL), lambda i: (0, 0)), vec, vec],
        out_specs=[full, full, full, full],
        out_shape=[jax.ShapeDtypeStruct((SEQ, D_MODEL), MXU_DTYPE), jax.ShapeDtypeStruct((SEQ, D_MODEL), F32),
                   jax.ShapeDtypeStruct((SEQ, D_MODEL), F32), jax.ShapeDtypeStruct((SEQ, D_MODEL), MXU_DTYPE)],
        compiler_params=_params(("parallel",)),
    )(h, o, proj0, proj0, x, w_out, g, b)


def _l1_in(x1b, w1z, w1d):
    t = 1024

    def body(x_ref, wz_ref, wd_ref, z_ref, dt_ref):
        xv = x_ref[...]
        z_ref[...] = _dot(xv, wz_ref[...])
        dt_ref[...] = _dot(xv, wd_ref[...])

    rows = lambda w: pl.BlockSpec((t, w), lambda i: (i, 0))
    const = lambda w: pl.BlockSpec((D_MODEL, w), lambda i: (0, 0))
    return pl.pallas_call(
        body, name="l1_in", grid=(SEQ // t,),
        in_specs=[rows(D_MODEL), const(SSD_INNER), const(LANES)],
        out_specs=[rows(SSD_INNER), rows(LANES)],
        out_shape=[jax.ShapeDtypeStruct((SEQ, SSD_INNER), F32), jax.ShapeDtypeStruct((SEQ, LANES), F32)],
        compiler_params=_params(("parallel",)),
    )(x1b, w1z, w1d)


def _l1_dx_ln(dz, dxbc, ddt, dv1, v0, w1z, w1x, w1d, g):
    t = LN_T

    def body(dz_ref, dx_ref, ddt_ref, dv1_ref, v_ref, wz_ref, wx_ref, wd_ref, g_ref, dv_ref, dgb_ref):
        @pl.when(pl.program_id(0) == 0)
        def _():
            dgb_ref[...] = jnp.zeros_like(dgb_ref)

        dy = (DN_ALPHA * dv1_ref[...] + _dot_nt(dz_ref[...], wz_ref[...]) + _dot_nt(dx_ref[...], wx_ref[...])
              + _dot_nt(_mx(ddt_ref[...]), wd_ref[...]))
        dv, dg, db = _ln_bwd(v_ref[...], g_ref[...], dy)
        dv_ref[...] = dv
        dgb_ref[0:1, :] += dg
        dgb_ref[1:2, :] += db

    rows = lambda w: pl.BlockSpec((t, w), lambda i: (i, 0))
    const = lambda w: pl.BlockSpec((D_MODEL, w), lambda i: (0, 0))
    return pl.pallas_call(
        body, name="l1_dx_ln", grid=(SEQ // t,),
        in_specs=[rows(SSD_INNER), rows(SSD_CONV), rows(LANES), rows(D_MODEL), rows(D_MODEL),
                  const(SSD_INNER), const(SSD_CONV), const(LANES), pl.BlockSpec((1, D_MODEL), lambda i: (0, 0))],
        out_specs=[rows(D_MODEL), pl.BlockSpec((SUBLANES, D_MODEL), lambda i: (0, 0))],
        out_shape=[jax.ShapeDtypeStruct((SEQ, D_MODEL), F32), jax.ShapeDtypeStruct((SUBLANES, D_MODEL), F32)],
        compiler_params=_params(("arbitrary",)),
    )(dz, dxbc, ddt, dv1, v0, w1z, w1x, w1d, g)


def _gate_bwd(dv0, w_out, h, o, proj0):
    t = LN_T

    def body(dv_ref, w_ref, h_ref, o_ref, ga_ref, gb_ref, dh_ref, do_ref, dg_ref):
        dy = _dot_nt(_mx(dv_ref[...]), w_ref[...])
        ga, gb, dya, dyb = ga_ref[...], gb_ref[...], dy[:, :512], dy[:, 512:]
        dh_ref[...] = dya * _silu(ga)
        do_ref[...] = dyb * _silu(gb)
        dg_ref[:, :512] = _mx(dya * h_ref[...] * _dsilu(ga))
        dg_ref[:, 512:] = _mx(dyb * o_ref[...] * _dsilu(gb))

    half = pl.BlockSpec((t, 512), lambda i: (i, 0))
    half1 = pl.BlockSpec((t, 512), lambda i: (i, 1))
    full = pl.BlockSpec((t, 1024), lambda i: (i, 0))
    return pl.pallas_call(
        body, name="gate_bwd", grid=(SEQ // t,),
        in_specs=[full, pl.BlockSpec((D_MODEL, D_MODEL), lambda i: (0, 0)), half, half, half, half1],
        out_specs=[half, half, full],
        out_shape=[jax.ShapeDtypeStruct((SEQ, 512), F32), jax.ShapeDtypeStruct((SEQ, 512), F32),
                   jax.ShapeDtypeStruct((SEQ, 1024), MXU_DTYPE)],
        compiler_params=_params(("parallel",)),
    )(dv0, w_out, h, o, proj0, proj0)


CONV_T = 1024
CONV_CB = 1024


def _ssd_conv_fwd(x1b, w1x, cw8, cb):
    t, cbk = CONV_T, CONV_CB

    def body(x_ref, w_ref, cw_ref, cb_ref, xbc_ref, pre_ref, act_ref, carry):
        xbc = _dot(x_ref[...], w_ref[...])
        halo = jnp.where(pl.program_id(1) > 0, carry[...], 0.0)
        pre = _conv4(xbc, halo, cw_ref[...], cb_ref[...])
        carry[...] = xbc[t - SUBLANES:]
        xbc_ref[...] = xbc
        pre_ref[...] = pre
        act_ref[...] = _silu(pre)

    blk = pl.BlockSpec((t, cbk), lambda j, i: (i, j))
    out = jax.ShapeDtypeStruct((SEQ, SSD_CONV), F32)
    return pl.pallas_call(
        body, name="ssd_conv_fwd", grid=(SSD_CONV // cbk, SEQ // t),
        in_specs=[pl.BlockSpec((t, D_MODEL), lambda j, i: (i, 0)), pl.BlockSpec((D_MODEL, cbk), lambda j, i: (0, j)),
                  pl.BlockSpec((SUBLANES, cbk), lambda j, i: (0, j)), pl.BlockSpec((1, cbk), lambda j, i: (0, j))],
        out_specs=[blk, blk, blk], out_shape=[out, out, out],
        scratch_shapes=[pltpu.VMEM((SUBLANES, cbk), F32)],
        compiler_params=_params(("parallel", "arbitrary")),
    )(x1b, w1x, cw8, cb)


def _ssd_conv_bwd(dact, pre, xbc, cw8, x1b):
    t, cbk = CONV_T, CONV_CB
    tb = t // SUBLANES
    nb = SEQ // t

    def body(da_ref, dan_ref, pre_ref, pren_ref, x_ref, cw_ref, x1_ref, dx_ref, dcw_ref, gw_ref):
        i = pl.program_id(1)

        @pl.when(i == 0)
        def _():
            dcw_ref[...] = jnp.zeros_like(dcw_ref)
            gw_ref[...] = jnp.zeros_like(gw_ref)

        dpre = da_ref[...] * _dsilu(pre_ref[...])
        dpre_next = jnp.where(i < nb - 1, dan_ref[...] * _dsilu(pren_ref[...]), 0.0)
        xblk = x_ref[...]
        cw = cw_ref[...]
        dx = dpre * cw[3:4]
        dcw_ref[3:4, :] += jnp.sum(dpre * xblk, axis=0, keepdims=True)
        for k in range(3):
            up = _shift_up(dpre, dpre_next, 3 - k)
            dcw_ref[k:k + 1, :] += jnp.sum(up * xblk, axis=0, keepdims=True)
            dx = dx + up * cw[k:k + 1]
        dcw_ref[4:5, :] += jnp.sum(dpre, axis=0, keepdims=True)
        dxb = _mx(dx)
        dx_ref[...] = dxb
        gw_ref[...] += _dot_tn(x1_ref[...], dxb)

    blk = pl.BlockSpec((t, cbk), lambda j, i: (i, j))
    nxt = pl.BlockSpec((SUBLANES, cbk), lambda j, i: (jnp.minimum((i + 1) * tb, SEQ // SUBLANES - 1), j))
    acc = pl.BlockSpec((SUBLANES, cbk), lambda j, i: (0, j))
    return pl.pallas_call(
        body, name="ssd_conv_bwd", grid=(SSD_CONV // cbk, nb),
        in_specs=[blk, nxt, blk, nxt, blk, acc, pl.BlockSpec((t, D_MODEL), lambda j, i: (i, 0))],
        out_specs=[blk, acc, pl.BlockSpec((D_MODEL, cbk), lambda j, i: (0, j))],
        out_shape=[jax.ShapeDtypeStruct((SEQ, SSD_CONV), MXU_DTYPE), jax.ShapeDtypeStruct((SUBLANES, SSD_CONV), F32),
                   jax.ShapeDtypeStruct((D_MODEL, SSD_CONV), F32)],
        compiler_params=_params(("parallel", "arbitrary")),
    )(dact, dact, pre, pre, xbc, cw8, x1b)


def _ssd_common(dt_raw, bias, alog, tril, expand_t, xs):
    lane = lax.broadcasted_iota(jnp.int32, dt_raw.shape, 1)
    dt = jnp.where(lane < SSD_HEADS, _softplus(dt_raw + bias), 0.0)
    a_neg = -jnp.exp(alog)
    cs = _dot_hi(tril, dt * a_neg)
    dt_x = _expand_heads(dt, expand_t)
    ecs_x = _expand_heads(jnp.exp(cs), expand_t)
    ds_x = _expand_heads(jnp.exp(cs[SSD_L - 1:SSD_L, :] - cs), expand_t)
    return dt, a_neg, cs, dt_x, None, xs * dt_x, ds_x, ecs_x, ecs_x[SSD_L - 1:SSD_L, :]


def _expand_heads(v, expand_t):
    hi = v.astype(jnp.bfloat16)
    lo = (v - hi.astype(F32)).astype(jnp.bfloat16)
    return _dot_nt(hi, expand_t) + _dot_nt(lo, expand_t)


def _fold_heads(v, expand_t):
    hi = v.astype(jnp.bfloat16)
    lo = (v - hi.astype(F32)).astype(jnp.bfloat16)
    return _dot(hi, expand_t) + _dot(lo, expand_t)


def _ssd_decay(cs, cs_t, hh, causal):
    seg = cs[:, hh:hh + 1] - cs_t[hh:hh + 1, :]
    return jnp.where(causal, jnp.exp(jnp.where(causal, seg, 0.0)), 0.0)


def _ssd_scan_fwd(act, dt_raw, bias, alog, d_x, tril, expand_t):
    nc = SEQ // SSD_L
    gw = SSD_INNER // SSD_GROUPS

    def body(act_ref, dt_ref, bias_ref, alog_ref, dx_ref, tril_ref, et_ref, y_ref, hp_ref, h_sc):
        @pl.when(pl.program_id(0) == 0)
        def _():
            h_sc[...] = jnp.zeros_like(h_sc)

        xs = act_ref[:, :SSD_INNER]
        _, _, cs, _, _, xdt, ds_x, ecs_x, elast = _ssd_common(
            dt_ref[...], bias_ref[...], alog_ref[...], tril_ref[...], et_ref[...], xs)
        cs_t = cs.T
        causal = (lax.broadcasted_iota(jnp.int32, (SSD_L, SSD_L), 0)
                  >= lax.broadcasted_iota(jnp.int32, (SSD_L, SSD_L), 1))
        lane = lax.broadcasted_iota(jnp.int32, (SSD_L, LANES), 1)
        xdt_b = _mx(xdt)
        xds_b = _mx(xdt * ds_x)
        hp_ref[0] = h_sc[...]
        for g in range(SSD_GROUPS):
            gs = slice(g * gw, (g + 1) * gw)
            bg = _mx(act_ref[:, SSD_INNER + g * SSD_N:SSD_INNER + (g + 1) * SSD_N])
            cg = _mx(act_ref[:, SSD_INNER + 512 + g * SSD_N:SSD_INNER + 512 + (g + 1) * SSD_N])
            cb = _dot_nt(cg, bg)
            hprev = h_sc[:, gs]
            yoff = _dot(cg, _mx(hprev)) * ecs_x[:, gs]
            h_sc[:, gs] = hprev * elast[:, gs] + _dot_tn(bg, xds_b[:, gs])
            for pr in range(4):
                ps = slice(g * gw + pr * LANES, g * gw + (pr + 1) * LANES)
                xp = xdt_b[:, ps]
                ydiag = jnp.zeros((SSD_L, LANES), F32)
                for j in range(2):
                    dm = _ssd_decay(cs, cs_t, g * 8 + pr * 2 + j, causal)
                    mine = (lane >= j * 64) & (lane < (j + 1) * 64)
                    ydiag = ydiag + _dot(_mx(cb * dm), jnp.where(mine, xp, jnp.zeros_like(xp)))
                y_ref[:, ps] = ydiag + yoff[:, pr * LANES:(pr + 1) * LANES] + dx_ref[:, ps] * xs[:, ps]

    const = lambda shape: pl.BlockSpec(shape, lambda c: (0, 0))
    return pl.pallas_call(
        body, name="ssd_scan_fwd", grid=(nc,),
        in_specs=[pl.BlockSpec((SSD_L, SSD_CONV), lambda c: (c, 0)), pl.BlockSpec((SSD_L, LANES), lambda c: (c, 0)),
                  const((1, LANES)), const((1, LANES)), const((1, SSD_INNER)), const((SSD_L, SSD_L)),
                  const((SSD_INNER, LANES))],
        out_specs=[pl.BlockSpec((SSD_L, SSD_INNER), lambda c: (c, 0)),
                   pl.BlockSpec((1, SSD_N, SSD_INNER), lambda c: (c, 0, 0))],
        out_shape=[jax.ShapeDtypeStruct((SEQ, SSD_INNER), F32), jax.ShapeDtypeStruct((nc, SSD_N, SSD_INNER), F32)],
        scratch_shapes=[pltpu.VMEM((SSD_N, SSD_INNER), F32)],
        compiler_params=_params(("arbitrary",)),
    )(act, dt_raw, bias, alog, d_x, tril, expand_t)


def _ssd_scan_bwd(dy, act, dt_raw, hprev_all, bias, alog, d_x, tril, expand_t):
    nc = SEQ // SSD_L
    gw = SSD_INNER // SSD_GROUPS

    def body(dy_ref, act_ref, dt_ref, hp_ref, bias_ref, alog_ref, dx_ref, tril_ref, et_ref,
             dact_ref, ddt_ref, dvec_ref, dh_sc, dd_sc):
        i = pl.program_id(0)

        @pl.when(i == 0)
        def _():
            dh_sc[...] = jnp.zeros_like(dh_sc)
            dd_sc[...] = jnp.zeros_like(dd_sc)
            dvec_ref[...] = jnp.zeros_like(dvec_ref)

        xs = act_ref[:, :SSD_INNER]
        dt_raw_v, bias_v = dt_ref[...], bias_ref[...]
        dt, a_neg, cs, dt_x, _, xdt, ds_x, ecs_x, elast = _ssd_common(
            dt_raw_v, bias_v, alog_ref[...], tril_ref[...], et_ref[...], xs)
        cs_t = cs.T
        rowi = lax.broadcasted_iota(jnp.int32, (SSD_L, SSD_L), 0)
        coli = lax.broadcasted_iota(jnp.int32, (SSD_L, SSD_L), 1)
        causal = rowi >= coli
        lane = lax.broadcasted_iota(jnp.int32, (SSD_L, LANES), 1)
        row_g = lax.broadcasted_iota(jnp.int32, (SSD_L, gw), 0)
        dyv = dy_ref[...]
        dd_sc[0:1, :] += jnp.sum(dyv * xs, axis=0, keepdims=True)
        xdt_b = _mx(xdt)
        xds = xdt * ds_x
        xds_b = _mx(xds)
        dy_b = _mx(dyv)
        dye_b = _mx(dyv * ecs_x)
        dcs = jnp.zeros((SSD_L, LANES), F32)
        dcs_t = jnp.zeros((LANES, SSD_L), F32)
        dcs_parts = []
        dxdt_parts = []
        for g in range(SSD_GROUPS):
            gs = slice(g * gw, (g + 1) * gw)
            bcol = slice(SSD_INNER + g * SSD_N, SSD_INNER + (g + 1) * SSD_N)
            ccol = slice(SSD_INNER + 512 + g * SSD_N, SSD_INNER + 512 + (g + 1) * SSD_N)
            bg, cg = _mx(act_ref[:, bcol]), _mx(act_ref[:, ccol])
            cb = _dot_nt(cg, bg)
            hp = hp_ref[0, :, gs]
            hp_b = _mx(hp)
            dh = dh_sc[:, gs]
            dh_b = _mx(dh)
            yoff = _dot(cg, hp_b) * ecs_x[:, gs]
            bdh = _dot(bg, dh_b)
            tt = xds[:, gs] * bdh
            last_row = (jnp.sum(tt, axis=0, keepdims=True)
                        + jnp.sum(dh * hp, axis=0, keepdims=True) * elast[:, gs])
            dcs_parts.append(dyv[:, gs] * yoff - tt + jnp.where(row_g == SSD_L - 1, last_row, 0.0))
            dc_g = _dot_nt(dye_b[:, gs], hp_b)
            db_g = _dot_nt(xds_b[:, gs], dh_b)
            dh_sc[:, gs] = _dot_tn(cg, dye_b[:, gs]) + dh * elast[:, gs]
            wsum = jnp.zeros((SSD_L, SSD_L), F32)
            dxdt_g = []
            for pr in range(4):
                ps = slice(g * gw + pr * LANES, g * gw + (pr + 1) * LANES)
                xp, dyp = xdt_b[:, ps], dy_b[:, ps]
                dxp = jnp.zeros((SSD_L, LANES), F32)
                for j in range(2):
                    hh = g * 8 + pr * 2 + j
                    dm = _ssd_decay(cs, cs_t, hh, causal)
                    mine = (lane >= j * 64) & (lane < (j + 1) * 64)
                    dy_h = jnp.where(mine, dyp, jnp.zeros_like(dyp))
                    wd = _dot_nt(dy_h, xp) * dm
                    wsum = wsum + wd
                    gmat = wd * cb
                    dcs = dcs + jnp.where(lane == hh, jnp.sum(gmat, axis=1, keepdims=True), 0.0)
                    dcs_t = dcs_t - jnp.where(rowi == hh, jnp.sum(gmat, axis=0, keepdims=True), 0.0)
                    dxp = dxp + _dot_tn(_mx(cb * dm), dy_h)
                dxdt_g.append(dxp)
            dxdt_parts.append(jnp.concatenate(dxdt_g, axis=1) + bdh * ds_x[:, gs])
            ws_b = _mx(wsum)
            dact_ref[:, ccol] = dc_g + _dot(ws_b, bg)
            dact_ref[:, bcol] = db_g + _dot_tn(ws_b, cg)
        dxdt = jnp.concatenate(dxdt_parts, axis=1)
        dcs_x = jnp.concatenate(dcs_parts, axis=1)
        et = et_ref[...]
        dcs_tot = dcs + dcs_t.T + _fold_heads(dcs_x, et)
        da_dt = _dot_hi((coli >= rowi).astype(F32), dcs_tot)
        ddt = da_dt * a_neg + _fold_heads(dxdt * xs, et)
        ddt_raw = ddt * _sigmoid(dt_raw_v + bias_v)
        ddt_ref[...] = ddt_raw
        dvec_ref[0:1, :] += jnp.sum(ddt_raw, axis=0, keepdims=True)
        dvec_ref[1:2, :] += jnp.sum(da_dt * dt, axis=0, keepdims=True) * a_neg
        dact_ref[:, :SSD_INNER] = dyv * dx_ref[...] + dxdt * dt_x

        @pl.when(i == nc - 1)
        def _():
            dvec_ref[2:3, :] = _fold_heads(dd_sc[...], et)[0:1, :]

    const = lambda shape: pl.BlockSpec(shape, lambda c: (0, 0))
    rev = lambda c: (nc - 1 - c, 0)
    return pl.pallas_call(
        body, name="ssd_scan_bwd", grid=(nc,),
        in_specs=[pl.BlockSpec((SSD_L, SSD_INNER), rev), pl.BlockSpec((SSD_L, SSD_CONV), rev),
                  pl.BlockSpec((SSD_L, LANES), rev),
                  pl.BlockSpec((1, SSD_N, SSD_INNER), lambda c: (nc - 1 - c, 0, 0)),
                  const((1, LANES)), const((1, LANES)), const((1, SSD_INNER)), const((SSD_L, SSD_L)),
                  const((SSD_INNER, LANES))],
        out_specs=[pl.BlockSpec((SSD_L, SSD_CONV), rev), pl.BlockSpec((SSD_L, LANES), rev), const((SUBLANES, LANES))],
        out_shape=[jax.ShapeDtypeStruct((SEQ, SSD_CONV), F32), jax.ShapeDtypeStruct((SEQ, LANES), F32),
                   jax.ShapeDtypeStruct((SUBLANES, LANES), F32)],
        scratch_shapes=[pltpu.VMEM((SSD_N, SSD_INNER), F32), pltpu.VMEM((SUBLANES, SSD_INNER), F32)],
        compiler_params=_params(("arbitrary",)),
    )(dy, act, dt_raw, hprev_all, bias, alog, d_x, tril, expand_t)


L1_T = 512


def _gated_norm(y, z, nw):
    y2 = y * _silu(z)
    gw = SSD_INNER // SSD_GROUPS
    outs, xhs, rss = [], [], []
    for g in range(SSD_GROUPS):
        gs = slice(g * gw, (g + 1) * gw)
        v = y2[:, gs]
        rs = lax.rsqrt(jnp.mean(v * v, axis=-1, keepdims=True) + 1e-6)
        xhs.append(v * rs)
        rss.append(rs)
        outs.append(v * rs * nw[:, gs])
    return outs, xhs, rss


def _l1_out(y, z, nw, w_out, x1, g, b, target):
    t = L1_T

    def body(y_ref, z_ref, nw_ref, w_ref, x1_ref, g_ref, b_ref, tg_ref, yn_ref, dv_ref, dgb_ref, loss_ref):
        @pl.when(pl.program_id(0) == 0)
        def _():
            dgb_ref[...] = jnp.zeros_like(dgb_ref)
            loss_ref[...] = jnp.zeros_like(loss_ref)

        outs, _, _ = _gated_norm(y_ref[...], z_ref[...], nw_ref[...])
        yn = _mx(jnp.concatenate(outs, axis=1))
        yn_ref[...] = yn
        v = DN_ALPHA * x1_ref[...] + _dot(yn, w_ref[...])
        gv = g_ref[...]
        err = _ln(v, gv, b_ref[...]) - tg_ref[...]
        rowsum = jnp.sum(err * err, axis=1, keepdims=True)
        loss_ref[...] += 0.5 * jnp.sum(rowsum, axis=0, keepdims=True) / D_MODEL
        dv, dg, db = _ln_bwd(v, gv, err / D_MODEL)
        dv_ref[...] = dv
        dgb_ref[0:1, :] += dg
        dgb_ref[1:2, :] += db

    wide = pl.BlockSpec((t, SSD_INNER), lambda i: (i, 0))
    full = pl.BlockSpec((t, D_MODEL), lambda i: (i, 0))
    vec = pl.BlockSpec((1, D_MODEL), lambda i: (0, 0))
    return pl.pallas_call(
        body, name="l1_out", grid=(SEQ // t,),
        in_specs=[wide, wide, pl.BlockSpec((1, SSD_INNER), lambda i: (0, 0)),
                  pl.BlockSpec((SSD_INNER, D_MODEL), lambda i: (0, 0)), full, vec, vec, full],
        out_specs=[wide, full, pl.BlockSpec((SUBLANES, D_MODEL), lambda i: (0, 0)),
                   pl.BlockSpec((SUBLANES, LANES), lambda i: (0, 0))],
        out_shape=[jax.ShapeDtypeStruct((SEQ, SSD_INNER), MXU_DTYPE), jax.ShapeDtypeStruct((SEQ, D_MODEL), F32),
                   jax.ShapeDtypeStruct((SUBLANES, D_MODEL), F32), jax.ShapeDtypeStruct((SUBLANES, LANES), F32)],
        compiler_params=_params(("arbitrary",)),
    )(y, z, nw, w_out, x1, g, b, target)


def _l1_gate_bwd(dv1, w_out, y, z, nw, x1b):
    t = 256
    gw = SSD_INNER // SSD_GROUPS

    def body(dv_ref, w_ref, y_ref, z_ref, nw_ref, x1_ref, dy_ref, dz_ref, dnw_ref, gw_ref):
        @pl.when(pl.program_id(0) == 0)
        def _():
            dnw_ref[...] = jnp.zeros_like(dnw_ref)
            gw_ref[...] = jnp.zeros_like(gw_ref)

        dyn = _dot_nt(_mx(dv_ref[...]), w_ref[...])
        yv, zv, nwv = y_ref[...], z_ref[...], nw_ref[...]
        _, xhs, rss = _gated_norm(yv, zv, nwv)
        sz, dsz = _silu(zv), _dsilu(zv)
        for g in range(SSD_GROUPS):
            gs = slice(g * gw, (g + 1) * gw)
            d_out = dyn[:, gs]
            xh = xhs[g]
            dnw_ref[0:1, gs] += jnp.sum(d_out * xh, axis=0, keepdims=True)
            dxh = d_out * nwv[:, gs]
            dy2 = rss[g] * (dxh - xh * jnp.mean(dxh * xh, axis=-1, keepdims=True))
            dy_ref[:, gs] = dy2 * sz[:, gs]
            dz_ref[:, gs] = _mx(dy2 * yv[:, gs] * dsz[:, gs])
        gw_ref[...] += _dot_tn(x1_ref[...], dz_ref[...])

    wide = pl.BlockSpec((t, SSD_INNER), lambda i: (i, 0))
    return pl.pallas_call(
        body, name="l1_gate_bwd", grid=(SEQ // t,),
        in_specs=[pl.BlockSpec((t, D_MODEL), lambda i: (i, 0)), pl.BlockSpec((SSD_INNER, D_MODEL), lambda i: (0, 0)),
                  wide, wide, pl.BlockSpec((1, SSD_INNER), lambda i: (0, 0)), pl.BlockSpec((t, D_MODEL), lambda i: (i, 0))],
        out_specs=[wide, wide, pl.BlockSpec((SUBLANES, SSD_INNER), lambda i: (0, 0)),
                   pl.BlockSpec((D_MODEL, SSD_INNER), lambda i: (0, 0))],
        out_shape=[jax.ShapeDtypeStruct((SEQ, SSD_INNER), F32), jax.ShapeDtypeStruct((SEQ, SSD_INNER), MXU_DTYPE),
                   jax.ShapeDtypeStruct((SUBLANES, SSD_INNER), F32), jax.ShapeDtypeStruct((D_MODEL, SSD_INNER), F32)],
        compiler_params=_params(("arbitrary",)),
    )(dv1, w_out, y, z, nw, x1b)


MESH = pl.DeviceIdType.MESH
ANY = pl.BlockSpec(memory_space=pl.ANY)


def _flip(v, bit):
    return 1 - v if bit else v


def _all_gather(blocks, name):
    n = len(blocks)

    def body(*refs):
        x_refs, out_refs = refs[:n], refs[n:2 * n]
        send_sems, recv_sems, local_sems = refs[2 * n:]
        mx, my, mc = lax.axis_index("x"), lax.axis_index("y"), lax.axis_index("c")
        me, sibling = (mx, my, mc), (mx, my, 1 - mc)
        chips = [(1 - mx, my), (mx, 1 - my), (1 - mx, 1 - my)]

        def copy(a, k, block, to, own=False):
            px, py, pc = block
            slot = out_refs[a].at[4 * px + 2 * py + pc]
            return pltpu.make_async_remote_copy(
                src_ref=x_refs[a] if own else slot, dst_ref=slot,
                send_sem=send_sems.at[7 * a + k], recv_sem=recv_sems.at[7 * a + k], device_id=to, device_id_type=MESH)

        mine = [pltpu.make_async_copy(x_refs[a], out_refs[a].at[4 * mx + 2 * my + mc], local_sems.at[a])
                for a in range(n)]
        first = []
        for a in range(n):
            mine[a].start()
            first.append(copy(a, 0, me, sibling, own=True))
            first += [copy(a, 1 + j, me, (*chip, mc), own=True) for j, chip in enumerate(chips)]
        for cp in first:
            cp.start()
        passed = []
        for j, chip in enumerate(chips):
            for a in range(n):
                copy(a, 1 + j, (*chip, mc), me).wait_recv()
                fwd = copy(a, 4 + j, (*chip, mc), sibling)
                fwd.start()
                passed.append(fwd)
        for a in range(n):
            copy(a, 0, sibling, me).wait_recv()
            for j, chip in enumerate(chips):
                copy(a, 4 + j, (*chip, 1 - mc), me).wait_recv()
        for cp in first + passed:
            cp.wait_send()
        for cp in mine:
            cp.wait()

    return pl.pallas_call(
        body, name=name, in_specs=[ANY] * n, out_specs=[ANY] * n,
        out_shape=[jax.ShapeDtypeStruct((N_DEV,) + b.shape, b.dtype) for b in blocks],
        scratch_shapes=[pltpu.SemaphoreType.DMA((7 * n,)), pltpu.SemaphoreType.DMA((7 * n,)),
                        pltpu.SemaphoreType.DMA((n,))],
    )(*blocks)


def _l0_in(x, w0p, bcast=()):
    n = len(bcast)
    tm, tn = 1024, 1024
    gi, gj = SEQ // tm, 2048 // tn

    def body(x_ref, w_ref, *rest):
        o_ref, xb_ref = rest[n], rest[n + 1]
        i, j = pl.program_id(0), pl.program_id(1)
        if n:
            copies = _peer_copies(rest[:n], rest[n + 2:2 * n + 2], rest[2 * n + 2:], [])

            @pl.when((i == 0) & (j == 0))
            def _():
                for cp in copies:
                    cp.start()

        xb = _mx(x_ref[...])
        xb_ref[...] = xb
        o_ref[...] = _dot(xb, w_ref[...])

        if n:
            @pl.when((i == gi - 1) & (j == gj - 1))
            def _():
                for cp in copies:
                    cp.wait()

    res = pl.pallas_call(
        body, name="l0_in", grid=(gi, gj),
        in_specs=[pl.BlockSpec((tm, D_MODEL), lambda i, j: (i, 0)), pl.BlockSpec((D_MODEL, tn), lambda i, j: (0, j))]
        + [ANY] * n,
        out_specs=[pl.BlockSpec((tm, tn), lambda i, j: (i, j)), pl.BlockSpec((tm, D_MODEL), lambda i, j: (i, 0))]
        + [ANY] * n,
        out_shape=[jax.ShapeDtypeStruct((SEQ, 2048), F32), jax.ShapeDtypeStruct((SEQ, D_MODEL), MXU_DTYPE)]
        + _exchange_shapes([], bcast),
        scratch_shapes=_exchange_sems(n) if n else [],
        compiler_params=_params(("arbitrary", "arbitrary")),
    )(x, w0p, *bcast)
    return res[0], res[1], res[2:]


def _l0_dx(dxr, dgate, dtail, w0p, dv0, scatter=(), bcast=()):
    arrays, ranges = _scatter_args(scatter)
    n = len(arrays) + len(bcast)
    tm = 1024
    steps = SEQ // tm

    def body(dxr_ref, dg_ref, dt_ref, w_ref, dv_ref, *rest):
        o_ref = rest[n]
        i = pl.program_id(0)
        if n:
            copies = _peer_copies(rest[:n], rest[n + 1:2 * n + 1], rest[2 * n + 1:], ranges)

            @pl.when(i == 0)
            def _():
                for cp in copies:
                    cp.start()

        o_ref[...] = (DN_ALPHA * dv_ref[...] + _dot_nt(dg_ref[...], w_ref[:, 0:1024])
                      + _dot_nt(dxr_ref[...], w_ref[:, 1024:1536]) + _dot_nt(dt_ref[...], w_ref[:, 1536:2048]))

        if n:
            @pl.when(i == steps - 1)
            def _():
                for cp in copies:
                    cp.wait()

    rows = lambda w: pl.BlockSpec((tm, w), lambda i: (i, 0))
    res = pl.pallas_call(
        body, name="l0_dx", grid=(steps,),
        in_specs=[rows(512), rows(1024), rows(512), pl.BlockSpec((D_MODEL, 2048), lambda i: (0, 0)), rows(D_MODEL)]
        + [ANY] * n,
        out_specs=[rows(D_MODEL)] + [ANY] * n,
        out_shape=[jax.ShapeDtypeStruct((SEQ, D_MODEL), F32)] + _exchange_shapes(arrays, bcast),
        scratch_shapes=_exchange_sems(n) if n else [],
        compiler_params=_params(("arbitrary",)),
    )(dxr, dgate, dtail, w0p, dv0, *arrays, *bcast)
    return res[0], res[1:]


def _scatter_args(scatter):
    arrays = [s[0] if isinstance(s, tuple) else s for s in scatter]
    ranges = [s[1] if isinstance(s, tuple) else (0, N_DEV) for s in scatter]
    return arrays, ranges


def _exchange_shapes(scatter, bcast):
    return ([jax.ShapeDtypeStruct((N_DEV,) + a.shape[1:], a.dtype) for a in scatter]
            + [jax.ShapeDtypeStruct((N_DEV,) + a.shape, a.dtype) for a in bcast])


def _exchange_sems(n):
    return [pltpu.SemaphoreType.DMA((7 * n,)), pltpu.SemaphoreType.DMA((7 * n,)), pltpu.SemaphoreType.DMA((n,))]


class _GuardedCopy:
    def __init__(self, copy, send=None, recv=None, local=False):
        self.copy, self.send, self.recv, self.local = copy, send, recv, local

    @staticmethod
    def _run(pred, fn):
        if pred is None:
            fn()
        else:
            pl.when(pred)(fn)

    def start(self):
        self._run(self.send, self.copy.start)

    def wait(self):
        if self.local:
            self._run(self.send, self.copy.wait)
        else:
            self._run(self.send, self.copy.wait_send)
            self._run(self.recv, self.copy.wait_recv)


def _peer_copies(in_refs, out_refs, sems, ranges):
    send_sems, recv_sems, local_sems = sems
    n, ns = len(in_refs), len(ranges)
    mx, my, mc = lax.axis_index("x"), lax.axis_index("y"), lax.axis_index("c")
    me = 4 * mx + 2 * my + mc

    def src(a, slot):
        return in_refs[a].at[slot - ranges[a][0]] if a < ns else in_refs[a]

    def member(a, dev):
        if a >= ns or ranges[a] == (0, N_DEV):
            return None
        return (dev >= ranges[a][0]) & (dev < ranges[a][1])

    copies = [_GuardedCopy(pltpu.make_async_copy(src(a, me), out_refs[a].at[me], local_sems.at[a]),
                           send=member(a, me), local=True) for a in range(n)]
    for k in range(1, N_DEV):
        px, py, pc = _flip(mx, (k >> 2) & 1), _flip(my, (k >> 1) & 1), _flip(mc, k & 1)
        peer = 4 * px + 2 * py + pc
        for a in range(n):
            copies.append(_GuardedCopy(pltpu.make_async_remote_copy(
                src_ref=src(a, peer), dst_ref=out_refs[a].at[me],
                send_sem=send_sems.at[7 * a + k - 1], recv_sem=recv_sems.at[7 * a + k - 1],
                device_id=(px, py, pc), device_id_type=MESH), send=member(a, peer), recv=member(a, me)))
    return copies


def _segments(col_map, width):
    segs = []
    for lo, hi, arr, alo in col_map:
        for s in range(N_DEV):
            a, b = max(lo, s * width), min(hi, (s + 1) * width)
            if a < b:
                segs.append((s, a - s * width, b - a, arr, alo + a - lo))
    return segs


COPY_ROWS = 256


def _unshard(g8, col_map, widths, name):
    _, r, w = g8.shape
    rb = min(r, COPY_ROWS)
    segs = _segments(col_map, w)

    def body(g_ref, *o_refs):
        for o_ref in o_refs:
            o_ref[...] = jnp.zeros_like(o_ref)
        for s, llo, n, arr, alo in segs:
            o_refs[arr][:, alo:alo + n] = g_ref[s, :, llo:llo + n]

    return pl.pallas_call(
        body, name=name, grid=(r // rb,),
        in_specs=[pl.BlockSpec((N_DEV, rb, w), lambda i: (0, i, 0))],
        out_specs=[pl.BlockSpec((rb, n), lambda i: (i, 0)) for n in widths],
        out_shape=[jax.ShapeDtypeStruct((r, n), g8.dtype) for n in widths],
        compiler_params=_params(("parallel",)),
    )(g8)


def _reshard(srcs, col_map, w, dtype, name, shards=(0, N_DEV)):
    r = srcs[0].shape[0]
    rb = min(r, COPY_ROWS)
    lo, hi = shards
    segs = [sg for sg in _segments(col_map, w) if lo <= sg[0] < hi]

    def body(*refs):
        o_ref = refs[-1]
        for s, llo, n, arr, alo in segs:
            o_ref[s - lo, :, llo:llo + n] = refs[arr][:, alo:alo + n].astype(dtype)

    return pl.pallas_call(
        body, name=name, grid=(r // rb,),
        in_specs=[pl.BlockSpec((rb, a.shape[1]), lambda i: (i, 0)) for a in srcs],
        out_specs=pl.BlockSpec((hi - lo, rb, w), lambda i: (0, i, 0)),
        out_shape=jax.ShapeDtypeStruct((hi - lo, r, w), dtype),
        compiler_params=_params(("parallel",)),
    )(*srcs)


def _adamw(parts, w, m, v, name):
    r, c = w.shape
    tr = COPY_ROWS if r % COPY_ROWS == 0 else r

    def body(p_ref, w_ref, m_ref, v_ref, g_ref, d_ref, mo_ref, vo_ref):
        g = p_ref[0].astype(F32)
        for s in range(1, N_DEV):
            g = g + p_ref[s].astype(F32)
        g_ref[...] = g
        d_ref[...], mo_ref[...], vo_ref[...] = _adamw_math(g, w_ref[...], m_ref[...], v_ref[...])

    blk = pl.BlockSpec((tr, c), lambda i: (i, 0))
    out = jax.ShapeDtypeStruct((r, c), F32)
    return pl.pallas_call(
        body, name=name, grid=(r // tr,),
        in_specs=[pl.BlockSpec((N_DEV, tr, c), lambda i: (0, i, 0)), blk, blk, blk],
        out_specs=[blk, blk, blk, blk], out_shape=[out, out, out, out],
        compiler_params=_params(("parallel",)),
    )(parts, w, m, v)


def _adamw_math(g, w, m, v):
    mn = ADAM_B1 * m + (1.0 - ADAM_B1) * g
    vn = ADAM_B2 * v + (1.0 - ADAM_B2) * (g * g)
    m_hat = mn / (1.0 - ADAM_B1 ** ADAM_STEP)
    v_hat = vn / (1.0 - ADAM_B2 ** ADAM_STEP)
    return -ADAM_LR * (m_hat / (jnp.sqrt(v_hat) + ADAM_EPS) + ADAM_WD * w), mn, vn


SMALL = (("ab_conv_w", 0, 4, 64), ("ssd_conv_w", 4, 4, 384), ("ssd_conv_b", 8, 1, 384), ("ssd_norm", 9, 1, 256),
         ("ssd_ln_g", 10, 1, 128), ("ssd_ln_b", 11, 1, 128))
VECS = (("ab_conv_b", 512), ("ab_gate_a_b", 512), ("ab_gate_x_b", 512), ("ab_lambda", 512), ("mla_q_norm", 256),
        ("mla_kv_norm", 128), ("ab_ln_g", 1024), ("ab_ln_b", 1024), ("ssd_dt_bias", 32), ("ssd_a_log", 32),
        ("ssd_d", 32))
GATES = ("ab_gate_a_w", "ab_gate_x_w")
SMALL_NAMES = tuple(n for n, *_ in SMALL) + tuple(n for n, _ in VECS) + GATES
VMEM_WHOLE = pl.BlockSpec(memory_space=pltpu.VMEM)


def _view2d(name, a):
    if name in GATES:
        return a.reshape(RNN_W, 64)
    return a[0] if a.ndim == 3 else a


def _unshard_small(g):
    widths = (512, 3072, 3072, 2048, 1024, 1024)

    def body(*refs):
        ins, outs = refs[:6], refs[6:]
        outs[0][...] = jnp.zeros_like(outs[0])
        outs[1][...] = jnp.zeros_like(outs[1])
        for (_, _, nr, c), i_ref, o_ref in zip(SMALL, ins, outs):
            for j in range(N_DEV):
                o_ref[0:nr, j * c:(j + 1) * c] = i_ref[j]

    return pl.pallas_call(
        body, name="unshard_small", in_specs=[VMEM_WHOLE] * 6, out_specs=[VMEM_WHOLE] * 6,
        out_shape=[jax.ShapeDtypeStruct((SUBLANES if nr == 4 else 1, w), F32) for (_, _, nr, _), w in zip(SMALL, widths)],
    )(*g)


def _prep_repl(ga, gx, dt_bias, a_log, d):
    def body(ga_ref, gx_ref, b_ref, al_ref, d_ref, wa_ref, wx_ref, b128_ref, al128_ref, dx_ref):
        wa_ref[...] = jnp.zeros_like(wa_ref)
        wx_ref[...] = jnp.zeros_like(wx_ref)
        for hd in range(8):
            hs = slice(hd * 64, (hd + 1) * 64)
            wa_ref[hs, hs] = _mx(ga_ref[hs, :])
            wx_ref[hs, hs] = _mx(gx_ref[hs, :])
        b128_ref[...] = jnp.zeros_like(b128_ref)
        al128_ref[...] = jnp.zeros_like(al128_ref)
        b128_ref[:, 0:SSD_HEADS] = b_ref[...]
        al128_ref[:, 0:SSD_HEADS] = al_ref[...]
        dv = d_ref[...]
        for hd in range(SSD_HEADS):
            dx_ref[:, hd * SSD_P:(hd + 1) * SSD_P] = jnp.broadcast_to(dv[:, hd:hd + 1], (1, SSD_P))

    return pl.pallas_call(
        body, name="prep_repl", in_specs=[VMEM_WHOLE] * 5, out_specs=[VMEM_WHOLE] * 5,
        out_shape=[jax.ShapeDtypeStruct((RNN_W, RNN_W), MXU_DTYPE), jax.ShapeDtypeStruct((RNN_W, RNN_W), MXU_DTYPE),
                   jax.ShapeDtypeStruct((1, LANES), F32), jax.ShapeDtypeStruct((1, LANES), F32),
                   jax.ShapeDtypeStruct((1, SSD_INNER), F32)],
    )(ga, gx, dt_bias, a_log, d)


LOSS_ROW = 11


def _pack_small(dvec0, g_wa, g_wx, dqnw, dknw, dgb0, dvec1, dcw1, dnw, dgb1, loss8):
    def body(dvec0_ref, gwa_ref, gwx_ref, dqn_ref, dkn_ref, dgb0_ref, dvec1_ref, dcw1_ref, dnw_ref, dgb1_ref,
             loss_ref, sm_ref, vec_ref, gg_ref):
        sm_ref[...] = jnp.zeros_like(sm_ref)
        vec_ref[...] = jnp.zeros_like(vec_ref)
        sharded = ((dvec0_ref, 4), (dcw1_ref, 0), (dcw1_ref, 4), (dnw_ref, 0), (dgb1_ref, 0), (dgb1_ref, 1))
        for (_, r0, nr, c), (src, sr) in zip(SMALL, sharded):
            for j in range(N_DEV):
                sm_ref[j, r0:r0 + nr, 0:c] = src[sr:sr + nr, j * c:(j + 1) * c]
        vectors = ((dvec0_ref, 3), (dvec0_ref, 0), (dvec0_ref, 1), (dvec0_ref, 2), (dqn_ref, 0), (dkn_ref, 0),
                   (dgb0_ref, 0), (dgb0_ref, 1), (dvec1_ref, 0), (dvec1_ref, 1), (dvec1_ref, 2))
        for row, ((_, c), (src, sr)) in enumerate(zip(VECS, vectors)):
            vec_ref[row:row + 1, 0:c] = src[sr:sr + 1, 0:c]
        vec_ref[LOSS_ROW:LOSS_ROW + 1, 0:LANES] = loss_ref[0:1, :]
        for hd in range(8):
            hs = slice(hd * 64, (hd + 1) * 64)
            gg_ref[hs, 0:64] = _mx(gwa_ref[hs, hs])
            gg_ref[hs, 64:128] = _mx(gwx_ref[hs, hs])

    return pl.pallas_call(
        body, name="pack_small", in_specs=[VMEM_WHOLE] * 11, out_specs=[VMEM_WHOLE] * 3,
        out_shape=[jax.ShapeDtypeStruct((N_DEV, 16, 384), F32), jax.ShapeDtypeStruct((16, 1024), F32),
                   jax.ShapeDtypeStruct((RNN_W, LANES), MXU_DTYPE)],
    )(dvec0, g_wa, g_wx, dqnw, dknw, dgb0, dvec1, dcw1, dnw, dgb1, loss8)


def _adamw_small(recv_sm, recv_vec, recv_gg, wmv):
    plan = ([(0, r0, nr, c) for _, r0, nr, c in SMALL] + [(1, row, 1, c) for row, (_, c) in enumerate(VECS)]
            + [(2, 0, RNN_W, 0), (2, 0, RNN_W, 64)])
    n = len(plan)

    def body(*refs):
        recv, ins, outs = refs[:3], refs[3:3 + 3 * n], refs[3 + 3 * n:]
        for i, (src, r0, nr, c) in enumerate(plan):
            cols = slice(c, c + 64) if src == 2 else slice(0, c)
            g = recv[src][0, r0:r0 + nr, cols].astype(F32)
            for s in range(1, N_DEV):
                g = g + recv[src][s, r0:r0 + nr, cols].astype(F32)
            w_ref, m_ref, v_ref = ins[3 * i:3 * i + 3]
            outs[4 * i][...] = g
            outs[4 * i + 1][...], outs[4 * i + 2][...], outs[4 * i + 3][...] = _adamw_math(
                g, w_ref[...], m_ref[...], v_ref[...])
        loss = recv[1][0, LOSS_ROW:LOSS_ROW + 1, 0:LANES]
        for s in range(1, N_DEV):
            loss = loss + recv[1][s, LOSS_ROW:LOSS_ROW + 1, 0:LANES]
        outs[4 * n][...] = loss

    flat = [a for t in wmv for a in t]
    return pl.pallas_call(
        body, name="adamw_small", in_specs=[VMEM_WHOLE] * (3 + 3 * n), out_specs=[VMEM_WHOLE] * (4 * n + 1),
        out_shape=[jax.ShapeDtypeStruct(t[0].shape, F32) for t in wmv for _ in range(4)]
        + [jax.ShapeDtypeStruct((1, LANES), F32)],
    )(recv_sm, recv_vec, recv_gg, *flat)


BIG_L0 = ("ab_w_in", "ab_w_out", "mla_w_uq", "mla_w_ukv")
BIG_L1 = ("ssd_w_in", "ssd_w_out")

MAP_W0 = ((0, 512, 0, 1024), (512, 1536, 0, 0), (1536, 1920, 0, 1536), (1920, 1952, 0, 1984))
MAP_W1 = ((0, 2048, 0, 0), (2048, 5120, 1, 0), (5120, 5152, 2, 0))
MAP_WQ = tuple((96 * hd, 96 * hd + 96, 0, 128 * hd) for hd in range(8))
MAP_WKV = (tuple((128 * hd, 128 * hd + 64, 0, 128 * hd) for hd in range(8))
           + tuple((128 * hd + 64, 128 * hd + 128, 0, 1024 + 64 * hd) for hd in range(8)))
MAP_G0 = ((0, 512, 0, 0), (512, 1536, 1, 0), (1536, 1920, 2, 0), (1920, 1952, 2, 448))
W0_EARLY, W0_LATE = (0, 6), (6, 8)


def kernel(x, positions, ab_w_in, ab_conv_w, ab_conv_b, ab_gate_a_w, ab_gate_a_b, ab_gate_x_w, ab_gate_x_b, ab_lambda, mla_q_norm, mla_kv_norm, mla_w_uq, mla_w_ukv, ab_w_out, ab_ln_g, ab_ln_b, ssd_w_in, ssd_conv_w, ssd_conv_b, ssd_dt_bias, ssd_a_log, ssd_d, ssd_norm, ssd_w_out, ssd_ln_g, ssd_ln_b, loss_target, m_ab_w_in, m_ab_conv_w, m_ab_conv_b, m_ab_gate_a_w, m_ab_gate_a_b, m_ab_gate_x_w, m_ab_gate_x_b, m_ab_lambda, m_mla_q_norm, m_mla_kv_norm, m_mla_w_uq, m_mla_w_ukv, m_ab_w_out, m_ab_ln_g, m_ab_ln_b, m_ssd_w_in, m_ssd_conv_w, m_ssd_conv_b, m_ssd_dt_bias, m_ssd_a_log, m_ssd_d, m_ssd_norm, m_ssd_w_out, m_ssd_ln_g, m_ssd_ln_b, v_ab_w_in, v_ab_conv_w, v_ab_conv_b, v_ab_gate_a_w, v_ab_gate_a_b, v_ab_gate_x_w, v_ab_gate_x_b, v_ab_lambda, v_mla_q_norm, v_mla_kv_norm, v_mla_w_uq, v_mla_w_ukv, v_ab_w_out, v_ab_ln_g, v_ab_ln_b, v_ssd_w_in, v_ssd_conv_w, v_ssd_conv_b, v_ssd_dt_bias, v_ssd_a_log, v_ssd_d, v_ssd_norm, v_ssd_w_out, v_ssd_ln_g, v_ssd_ln_b):
    args = dict(locals())
    bf = MXU_DTYPE
    big = {n: [args[pre + n][0] for pre in ("", "m_", "v_")] for n in BIG_L0 + BIG_L1}
    sml = {n: [_view2d(n, args[pre + n]) for pre in ("", "m_", "v_")] for n in SMALL_NAMES}

    w0_8, cw0_8 = _all_gather([big["ab_w_in"][0].astype(bf), sml["ab_conv_w"][0]], "gather_params")
    p = {"cw0_8": cw0_8, "l0_blocks": [big[n][0].astype(bf) for n in BIG_L0[1:]] + [sml[n][0] for n, *_ in SMALL[1:]]}
    p["w0p"], = _unshard(w0_8, MAP_W0, (2048,), "unshard_w0")
    p["wa"], p["wx"], p["dt_bias"], p["a_log"], p["d_x"] = _prep_repl(
        sml["ab_gate_a_w"][0], sml["ab_gate_x_w"][0], sml["ssd_dt_bias"][0], sml["ssd_a_log"][0], sml["ssd_d"][0])
    for key, n in (("cb0", "ab_conv_b"), ("ba", "ab_gate_a_b"), ("bx", "ab_gate_x_b"), ("lam", "ab_lambda"),
                   ("qn_w", "mla_q_norm"), ("kn_w", "mla_kv_norm"), ("g0", "ab_ln_g"), ("b0", "ab_ln_b")):
        p[key] = sml[n][0]

    _, recv_early, recv, _, grad_x = _local_step(
        x[0], positions[0], loss_target[0], p, [big[n][0].astype(bf) for n in BIG_L1])

    me = 4 * lax.axis_index("x") + 2 * lax.axis_index("y") + lax.axis_index("c")
    parts = {"ssd_w_in": recv_early[0], "ssd_w_out": recv_early[1], "ab_w_out": recv_early[2],
             "ab_w_in": jnp.where(me >= W0_LATE[0], recv[0], recv_early[3]), "mla_w_uq": recv[1], "mla_w_ukv": recv[2]}

    outs = {}
    kinds = ("grad", "delta", "new_m", "new_v")
    for n in BIG_L0 + BIG_L1:
        for kind, res in zip(kinds, _adamw(parts[n], *big[n], "adamw_" + n)):
            outs[kind, n] = res[None]
    res = _adamw_small(*recv[3:], [sml[n] for n in SMALL_NAMES])
    for i, n in enumerate(SMALL_NAMES):
        for k, kind in enumerate(kinds):
            outs[kind, n] = res[4 * i + k].reshape(args[n].shape)

    loss = res[4 * len(SMALL_NAMES)][0, 0]
    order = ["ab_w_in", "ab_conv_w", "ab_conv_b", "ab_gate_a_w", "ab_gate_a_b", "ab_gate_x_w", "ab_gate_x_b",
             "ab_lambda", "mla_q_norm", "mla_kv_norm", "mla_w_uq", "mla_w_ukv", "ab_w_out", "ab_ln_g", "ab_ln_b",
             "ssd_w_in", "ssd_conv_w", "ssd_conv_b", "ssd_dt_bias", "ssd_a_log", "ssd_d", "ssd_norm", "ssd_w_out",
             "ssd_ln_g", "ssd_ln_b"]
    return (loss, grad_x[None], *[outs[kind, n] for kind in ("grad", "delta", "new_m", "new_v") for n in order])


def _local_step(x, pos, target, p, l1_blocks):
    bf = MXU_DTYPE
    inv_freq = 10000.0 ** (-jnp.arange(0, 32, 2, dtype=F32) / 32)
    ang = pos.astype(F32)[:, None] * inv_freq
    cos, sin = jnp.cos(ang), jnp.sin(ang)
    zeros = lambda n: jnp.zeros((SEQ, n), F32)
    tc = jnp.concatenate([jnp.ones((SEQ, 64), F32), cos, cos, zeros(32)], axis=1)
    tsa = jnp.concatenate([zeros(64), -sin, zeros(48)], axis=1)
    tsb = jnp.concatenate([zeros(80), sin, zeros(32)], axis=1)

    w0p, wa, wxg = (p[k] for k in ("w0p", "wa", "wx"))
    cb0, ba, bx, lam = (p[k] for k in ("cb0", "ba", "bx", "lam"))
    qn_w, kn_w, g0, b0 = (p[k] for k in ("qn_w", "kn_w", "g0", "b0"))
    dt_bias, a_log, d_x = (p[k] for k in ("dt_bias", "a_log", "d_x"))
    tril = jnp.tril(jnp.ones((SSD_L, SSD_L), F32))
    expand_t = (jnp.arange(SSD_INNER)[:, None] // SSD_P == jnp.arange(LANES)[None, :]).astype(jnp.bfloat16)

    proj0, xb, l0_8 = _l0_in(x, w0p, bcast=p["l0_blocks"])
    wo0 = l0_8[0].reshape(D_MODEL, D_MODEL)
    wq, = _unshard(l0_8[1], MAP_WQ, (1024,), "unshard_wq")
    wkv, = _unshard(l0_8[2], MAP_WKV, (1536,), "unshard_wkv")
    cw0, cw1, cb1, nw, g1, b1 = _unshard_small([p["cw0_8"]] + list(l0_8[3:]))
    xc, h = _rglru_fwd(proj0, cw0, cb0, wa, ba, wxg, bx, lam)
    qn, kn, qc, kc, vc = _mla_fwd(proj0, qn_w, kn_w, wq, wkv, tc, tsa, tsb)
    o, lse, (w1_8, wo1_8) = _flash_fwd(qc, kc, vc, bcast=l1_blocks)
    w1z, w1x, w1d = _unshard(w1_8, MAP_W1, (2048, 3072, 128), "unshard_w1")
    wo1 = wo1_8.reshape(SSD_INNER, D_MODEL)
    y0, v0, x1, x1b = _l0_out(h, o, proj0, x, wo0, g0, b0)

    z, dt_raw = _l1_in(x1b, w1z, w1d)
    xbc, pre, act = _ssd_conv_fwd(x1b, w1x, cw1, cb1)
    ys, hprev = _ssd_scan_fwd(act, dt_raw, dt_bias, a_log, d_x, tril, expand_t)
    yn, dv1, dgb1, loss8 = _l1_out(ys, z, nw, wo1, x1, g1, b1, target)

    g_wo1 = _dw(yn, dv1, name="l1_dwout")
    dys, dz, dnw, g_z = _l1_gate_bwd(dv1, wo1, ys, z, nw, x1b)
    dact, ddt_raw, dvec1 = _ssd_scan_bwd(dys, act, dt_raw, hprev, dt_bias, a_log, d_x, tril, expand_t)
    dxbc, dcw1, g_xbc = _ssd_conv_bwd(dact, pre, xbc, cw1, x1b)
    g_dt = _dw(x1b, ddt_raw, name="l1_dw_dt")

    dv0, dgb0 = _l1_dx_ln(dz, dxbc, ddt_raw, dv1, v0, w1z, w1x, w1d, g0)
    g_wo0 = _dw(y0, dv0, name="l0_dwout")
    dh, do, dgate = _gate_bwd(dv0, wo0, h, o, proj0)
    dxr, g_wa, g_wx, dvec0 = _rglru_bwd(dh, xc, h, proj0, cw0, wa, ba, wxg, bx, lam)
    g_rnn, g_gate = _dw(xb, dxr, name="l0_dw_rnn"), _dw(xb, dgate, name="l0_dw_gate")
    early = [_reshard([g_z, g_xbc, g_dt], MAP_W1, 644, bf, "reshard_w1"), g_wo1.astype(bf).reshape(N_DEV, 256, D_MODEL),
             g_wo0.astype(bf).reshape(N_DEV, 128, D_MODEL),
             (_reshard([g_rnn, g_gate], MAP_G0, 244, bf, "reshard_w0_early", shards=W0_EARLY), W0_EARLY)]
    dq, dk, dvv, recv_early = _flash_bwd(qc, kc, vc, o, do, lse, scatter=early)
    dtail, g_wq, g_wkv, dqnw, dknw = _mla_bwd(dq, dk, dvv, proj0, qn, kn, qn_w, kn_w, wq, wkv, tc, tsa, tsb)
    g_tail = _dw(xb, dtail, name="l0_dw_tail")

    acc = {"g_rnn": g_rnn, "g_gate": g_gate, "g_tail": g_tail, "g_wq": g_wq, "g_wkv": g_wkv,
           "dvec0": dvec0, "g_wa": g_wa, "g_wx": g_wx, "dqnw": dqnw, "dknw": dknw, "dgb0": dgb0, "dvec1": dvec1,
           "dcw1": dcw1, "dnw": dnw, "dgb1": dgb1}
    late = [(_reshard([g_rnn, g_gate, g_tail], MAP_G0, 244, bf, "reshard_w0_late", shards=W0_LATE), W0_LATE),
            _reshard([g_wq], MAP_WQ, 96, bf, "reshard_wq"), _reshard([g_wkv], MAP_WKV, 128, bf, "reshard_wkv")]
    sm_slots, vec_rows, gates = _pack_small(dvec0, g_wa, g_wx, dqnw, dknw, dgb0, dvec1, dcw1, dnw, dgb1, loss8)
    dx, recv_late = _l0_dx(dxr, dgate, dtail, w0p, dv0, scatter=late + [sm_slots], bcast=[vec_rows, gates])
    return acc, recv_early, recv_late, loss8[0, 0], dx
```

```python
import math

import jax
import jax.numpy as jnp
from jax import lax
from jax.experimental import pallas as pl
from jax.experimental.pallas import tpu as pltpu

F32 = jnp.float32
MXU_DTYPE = jnp.bfloat16

N_DEV = 8
SEQ = 4096
D_MODEL = 1024
DN_ALPHA = 4.0 ** 0.25
RNN_W = 512
MLA_HEADS = 8
ATT_SCALE = 96.0 ** -0.5
ATT_C = ATT_SCALE * math.log2(math.e)
RG_C = 8.0
SSD_INNER = 2048
SSD_HEADS = 32
SSD_P = 64
SSD_GROUPS = 4
SSD_N = 128
SSD_L = 128
SSD_CONV = 3072
LANES = 128
SUBLANES = 8
VMEM_LIMIT = 56 * 1024 * 1024

ADAM_LR, ADAM_B1, ADAM_B2, ADAM_EPS, ADAM_WD, ADAM_STEP = 0.001, 0.9, 0.999, 1e-08, 0.01, 10

HIGHEST = lax.Precision.HIGHEST


def _params(sem, limit=VMEM_LIMIT):
    return pltpu.CompilerParams(dimension_semantics=sem, vmem_limit_bytes=limit)


def _dot(a, b):
    return lax.dot_general(a, b, (((1,), (0,)), ((), ())), preferred_element_type=F32)


def _dot_nt(a, b):
    return lax.dot_general(a, b, (((1,), (1,)), ((), ())), preferred_element_type=F32)


def _dot_tn(a, b):
    return lax.dot_general(a, b, (((0,), (0,)), ((), ())), preferred_element_type=F32)


def _dot_hi(a, b):
    return lax.dot_general(a, b, (((1,), (0,)), ((), ())), precision=HIGHEST, preferred_element_type=F32)


def _mx(v):
    return v.astype(MXU_DTYPE)


def _sigmoid(v):
    return 1.0 / (1.0 + jnp.exp(-v))


def _log1p_pos(e):
    poly = e * (1.0 - e * (0.5 - e * (1.0 / 3.0 - e * 0.25)))
    return jnp.where(e < 0.01, poly, jnp.log(1.0 + e))


def _softplus(v):
    return jnp.maximum(v, 0.0) + _log1p_pos(jnp.exp(-jnp.abs(v)))


def _neg_expm1(v):
    poly = -v * (1.0 + v * (0.5 + v * (1.0 / 6.0 + v * (1.0 / 24.0 + v * (1.0 / 120.0)))))
    return jnp.where(jnp.abs(v) < 0.1, poly, 1.0 - jnp.exp(v))


def _silu(v):
    return v * _sigmoid(v)


def _dsilu(v):
    s = _sigmoid(v)
    return s * (1.0 + v * (1.0 - s))


def _shift_down(blk, halo, s):
    if s == 0:
        return blk
    t = blk.shape[0]
    r = pltpu.roll(blk, s, 0)
    hr = pltpu.roll(halo, s, 0)
    row8 = lax.broadcasted_iota(jnp.int32, hr.shape, 0)
    head = jnp.where(row8 < s, hr, r[:SUBLANES])
    return jnp.concatenate([head, r[SUBLANES:]], axis=0) if t > SUBLANES else head


def _shift_up(blk, halo, s):
    if s == 0:
        return blk
    t = blk.shape[0]
    r = pltpu.roll(blk, t - s, 0)
    hr = pltpu.roll(halo, SUBLANES - s, 0)
    row8 = lax.broadcasted_iota(jnp.int32, hr.shape, 0)
    tail = jnp.where(row8 >= SUBLANES - s, hr, r[t - SUBLANES:])
    return jnp.concatenate([r[:t - SUBLANES], tail], axis=0) if t > SUBLANES else tail


def _scan_down(a, u):
    t = a.shape[0]
    row = lax.broadcasted_iota(jnp.int32, a.shape, 0)
    d = 1
    while d < t:
        keep = row >= d
        a_sh = jnp.where(keep, pltpu.roll(a, d, 0), 1.0)
        u_sh = jnp.where(keep, pltpu.roll(u, d, 0), 0.0)
        u = a * u_sh + u
        a = a * a_sh
        d *= 2
    return a, u


def _scan_up(a, u):
    t = a.shape[0]
    row = lax.broadcasted_iota(jnp.int32, a.shape, 0)
    d = 1
    while d < t:
        keep = row < t - d
        a_sh = jnp.where(keep, pltpu.roll(a, t - d, 0), 1.0)
        u_sh = jnp.where(keep, pltpu.roll(u, t - d, 0), 0.0)
        u = a * u_sh + u
        a = a * a_sh
        d *= 2
    return a, u


def _conv4(blk, halo, cw, cb):
    out = cb + blk * cw[3:4]
    for k in range(3):
        out = out + _shift_down(blk, halo, 3 - k) * cw[k:k + 1]
    return out


RG_T = 512
P0_RNN = 2


def _rg_gates(xc, wa, ba, wx, bx, lam):
    xcb = _mx(xc)
    r = _sigmoid(_dot(xcb, wa) + ba)
    ig = _sigmoid(_dot(xcb, wx) + bx)
    sp = _softplus(-lam)
    la = (-RG_C * r) * sp
    a = jnp.exp(la)
    mult = jnp.sqrt(_neg_expm1(2.0 * la))
    return r, ig, sp, a, mult


def _rglru_fwd(proj0, cw8, cb, wa, ba, wx, bx, lam):
    t, w = RG_T, RNN_W
    nb = SEQ // t

    def body(x_ref, halo_ref, cw_ref, cb_ref, wa_ref, ba_ref, wx_ref, bx_ref, lam_ref, xc_ref, h_ref, carry):
        i = pl.program_id(0)

        @pl.when(i == 0)
        def _():
            carry[...] = jnp.zeros_like(carry)

        blk = x_ref[...]
        halo = jnp.where(i > 0, halo_ref[...], 0.0)
        xc = _conv4(blk, halo, cw_ref[...], cb_ref[...])
        _, ig, _, a, mult = _rg_gates(xc, wa_ref[...], ba_ref[...], wx_ref[...], bx_ref[...], lam_ref[...])
        u = mult * (ig * xc)
        big_a, big_u = _scan_down(a, u)
        h = big_a * carry[SUBLANES - 1:SUBLANES, :] + big_u
        carry[...] = h[t - SUBLANES:]
        xc_ref[...] = xc
        h_ref[...] = h

    vec = pl.BlockSpec((1, w), lambda i: (0, 0))
    mat = pl.BlockSpec((w, w), lambda i: (0, 0))
    return pl.pallas_call(
        body, name="rglru_fwd", grid=(nb,),
        in_specs=[pl.BlockSpec((t, w), lambda i: (i, P0_RNN)),
                  pl.BlockSpec((SUBLANES, w), lambda i: (jnp.maximum(i * (t // SUBLANES) - 1, 0), P0_RNN)),
                  pl.BlockSpec((SUBLANES, w), lambda i: (0, 0)), vec, mat, vec, mat, vec, vec],
        out_specs=[pl.BlockSpec((t, w), lambda i: (i, 0)), pl.BlockSpec((t, w), lambda i: (i, 0))],
        out_shape=[jax.ShapeDtypeStruct((SEQ, w), F32), jax.ShapeDtypeStruct((SEQ, w), F32)],
        scratch_shapes=[pltpu.VMEM((SUBLANES, w), F32)],
        compiler_params=_params(("arbitrary",)),
    )(proj0, proj0, cw8, cb, wa, ba, wx, bx, lam)


def _rglru_bwd(dh, xc, h, proj0, cw8, wa, ba, wx, bx, lam, xb):
    t, w = RG_T, RNN_W
    nb = SEQ // t
    tb = t // SUBLANES

    def body(dh_ref, xc_ref, h_ref, hh_ref, x_ref, cw_ref, wa_ref, ba_ref, wx_ref, bx_ref, lam_ref, xb_ref,
             dx_ref, dwa_ref, dwx_ref, dvec_ref, gw_ref, gcarry, dxc_next):
        i = pl.program_id(0)
        rev = nb - 1 - i

        @pl.when(i == 0)
        def _():
            gcarry[...] = jnp.zeros_like(gcarry)
            dxc_next[...] = jnp.zeros_like(dxc_next)
            gw_ref[...] = jnp.zeros_like(gw_ref)
            dwa_ref[...] = jnp.zeros_like(dwa_ref)
            dwx_ref[...] = jnp.zeros_like(dwx_ref)
            dvec_ref[...] = jnp.zeros_like(dvec_ref)

        xc = xc_ref[...]
        wa_v, wx_v = wa_ref[...], wx_ref[...]
        lam_v = lam_ref[...]
        r, ig, sp, a, mult = _rg_gates(xc, wa_v, ba_ref[...], wx_v, bx_ref[...], lam_v)
        dhv = dh_ref[...]
        big_a, big_u = _scan_up(a, a * dhv)
        gg = big_a * gcarry[0:1, :] + big_u
        g = dhv + _shift_up(gg, gcarry[...], 1)
        gcarry[...] = gg[:SUBLANES]
        hhalo = jnp.where(rev > 0, hh_ref[...], 0.0)
        da = g * _shift_down(h_ref[...], hhalo, 1)
        d_mult = g * (ig * xc)
        d_i = g * (mult * xc)
        dxc = g * (mult * ig)
        d_la = da * a - d_mult * (a * a) / mult
        d_r = d_la * (-RG_C * sp)
        d_sp = jnp.sum(d_la * (-RG_C * r), axis=0, keepdims=True)
        d_pa = d_r * r * (1.0 - r)
        d_px = d_i * ig * (1.0 - ig)
        d_pab, d_pxb = _mx(d_pa), _mx(d_px)
        dxc = dxc + _dot_nt(d_pab, wa_v) + _dot_nt(d_pxb, wx_v)
        xcb = _mx(xc)
        dwa_ref[...] += _dot_tn(xcb, d_pab)
        dwx_ref[...] += _dot_tn(xcb, d_pxb)
        dvec_ref[0:1, :] += jnp.sum(d_pa, axis=0, keepdims=True)
        dvec_ref[1:2, :] += jnp.sum(d_px, axis=0, keepdims=True)
        dvec_ref[2:3, :] += d_sp * (-_sigmoid(-lam_v))
        dvec_ref[3:4, :] += jnp.sum(dxc, axis=0, keepdims=True)
        xblk = x_ref[...]
        cw = cw_ref[...]
        dx = dxc * cw[3:4]
        nxt = dxc_next[...]
        dvec_ref[7:8, :] += jnp.sum(dxc * xblk, axis=0, keepdims=True)
        for k in range(3):
            up = _shift_up(dxc, nxt, 3 - k)
            dvec_ref[4 + k:5 + k, :] += jnp.sum(up * xblk, axis=0, keepdims=True)
            dx = dx + up * cw[k:k + 1]
        dxc_next[...] = dxc[:SUBLANES]
        dxb = _mx(dx)
        dx_ref[...] = dxb
        gw_ref[...] += _dot_tn(xb_ref[...], dxb)

    blk = pl.BlockSpec((t, w), lambda i: (nb - 1 - i, 0))
    halo = pl.BlockSpec((SUBLANES, w), lambda i: (jnp.maximum((nb - 1 - i) * tb - 1, 0), 0))
    vec = pl.BlockSpec((1, w), lambda i: (0, 0))
    mat = pl.BlockSpec((w, w), lambda i: (0, 0))
    return pl.pallas_call(
        body, name="rglru_bwd", grid=(nb,),
        in_specs=[blk, blk, blk, halo, pl.BlockSpec((t, w), lambda i: (nb - 1 - i, P0_RNN)),
                  pl.BlockSpec((SUBLANES, w), lambda i: (0, 0)), mat, vec, mat, vec, vec,
                  pl.BlockSpec((t, D_MODEL), lambda i: (nb - 1 - i, 0))],
        out_specs=[blk, mat, mat, pl.BlockSpec((16, w), lambda i: (0, 0)), pl.BlockSpec((D_MODEL, w), lambda i: (0, 0))],
        out_shape=[jax.ShapeDtypeStruct((SEQ, w), MXU_DTYPE), jax.ShapeDtypeStruct((w, w), F32),
                   jax.ShapeDtypeStruct((w, w), F32), jax.ShapeDtypeStruct((16, w), F32),
                   jax.ShapeDtypeStruct((D_MODEL, w), F32)],
        scratch_shapes=[pltpu.VMEM((SUBLANES, w), F32), pltpu.VMEM((SUBLANES, w), F32)],
        compiler_params=_params(("arbitrary",)),
    )(dh, xc, h, h, proj0, cw8, wa, ba, wx, bx, lam, xb)


MLA_T = 512


def _rope(v, c, sa, sb):
    return v * c + pltpu.roll(v, LANES - 16, 1) * sa + pltpu.roll(v, 16, 1) * sb


def _rope_t(dv, c, sa, sb):
    return dv * c + pltpu.roll(dv * sa, 16, 1) + pltpu.roll(dv * sb, LANES - 16, 1)


def _rms(v, g, eps=1e-6):
    rs = lax.rsqrt(jnp.mean(v * v, axis=-1, keepdims=True) + eps)
    return v * rs * g, rs


def _mla_fwd(proj0, q_norm, kv_norm, wq, wkv, tc, tsa, tsb):
    t = MLA_T

    def body(cq_ref, ck_ref, qn_ref, kn_ref, wq_ref, wkv_ref, c_ref, sa_ref, sb_ref,
             oqn_ref, okn_ref, oq_ref, ok_ref, ov_ref):
        c, sa, sb = c_ref[...], sa_ref[...], sb_ref[...]
        ck = ck_ref[...]
        qn = _mx(_rms(cq_ref[...], qn_ref[...])[0])
        kn = _mx(_rms(ck[:, :LANES], kn_ref[...])[0])
        oqn_ref[...] = qn
        okn_ref[...] = kn
        krv = _rope(ck[:, LANES:], c, sa, sb)
        qraw = _dot(qn, wq_ref[...])
        kvraw = _dot(kn, wkv_ref[...])
        for hd in range(MLA_HEADS):
            sl = slice(hd * LANES, (hd + 1) * LANES)
            oq_ref[:, sl] = _mx(_rope(qraw[:, sl], c, sa, sb))
            ok_ref[:, sl] = _mx(kvraw[:, sl] + krv)
        ov_ref[...] = _mx(kvraw[:, 1024:])

    tab = pl.BlockSpec((t, LANES), lambda i: (i, 0))
    wide = pl.BlockSpec((t, 1024), lambda i: (i, 0))
    const = lambda shape: pl.BlockSpec(shape, lambda i: (0, 0))
    return pl.pallas_call(
        body, name="mla_fwd", grid=(SEQ // t,),
        in_specs=[pl.BlockSpec((t, 256), lambda i: (i, 6)), pl.BlockSpec((t, 256), lambda i: (i, 7)),
                  const((1, 256)), const((1, LANES)), const((256, 1024)), const((LANES, 1536)), tab, tab, tab],
        out_specs=[pl.BlockSpec((t, 256), lambda i: (i, 0)), tab, wide, wide, pl.BlockSpec((t, 512), lambda i: (i, 0))],
        out_shape=[jax.ShapeDtypeStruct((SEQ, 256), MXU_DTYPE), jax.ShapeDtypeStruct((SEQ, LANES), MXU_DTYPE),
                   jax.ShapeDtypeStruct((SEQ, 1024), MXU_DTYPE), jax.ShapeDtypeStruct((SEQ, 1024), MXU_DTYPE),
                   jax.ShapeDtypeStruct((SEQ, 512), MXU_DTYPE)],
        compiler_params=_params(("parallel",)),
    )(proj0, proj0, q_norm, kv_norm, wq, wkv, tc, tsa, tsb)


ATT_T = 1024


def _flash_fwd(q, k, v, bcast=()):
    t = ATT_T
    nb = SEQ // t

    steps = [(qi, ki) for qi in range(nb) for ki in range(qi + 1)]
    qi_tab = jnp.asarray([s[0] for s in steps], jnp.int32)
    ki_tab = jnp.asarray([s[1] for s in steps], jnp.int32)

    nx = len(bcast)

    def body(qi_ref, ki_ref, q_ref, k_ref, v_ref, *rest):
        x_refs, (o_ref, lse_ref), g_refs = rest[:nx], rest[nx:nx + 2], rest[nx + 2:2 * nx + 2]
        m_sc, acc_sc = rest[2 * nx + 2:2 * nx + 4]
        step = pl.program_id(1)
        qi, ki = qi_ref[step], ki_ref[step]
        if nx:
            copies = _peer_copies(x_refs, g_refs, rest[2 * nx + 4:], [])

            @pl.when((pl.program_id(0) == 0) & (step == 0))
            def _():
                for cp in copies:
                    cp.start()

        @pl.when(ki == 0)
        def _():
            m_sc[...] = jnp.full_like(m_sc, -jnp.inf)
            acc_sc[...] = jnp.zeros_like(acc_sc)

        def update(diagonal):
            vv = v_ref[...]
            lane_v = lax.broadcasted_iota(jnp.int32, vv.shape, 1)
            for hd in range(2):
                sl = slice(hd * LANES, (hd + 1) * LANES)
                s = _dot_nt(q_ref[:, sl], k_ref[:, sl])
                if diagonal:
                    s = jnp.where(lax.broadcasted_iota(jnp.int32, (t, t), 1)
                                  <= lax.broadcasted_iota(jnp.int32, (t, t), 0), s, -jnp.inf)
                m_prev = m_sc[hd]
                m_new = jnp.maximum(m_prev, jnp.max(s, axis=1, keepdims=True))
                p = jnp.exp2((s - m_new[:, :1]) * ATT_C)
                m_sc[hd] = m_new
                vh = jnp.where((lane_v >= hd * 64) & (lane_v < (hd + 1) * 64), vv, jnp.ones_like(vv))
                acc_sc[hd] = acc_sc[hd] * jnp.exp2((m_prev - m_new) * ATT_C) + _dot(_mx(p), vh)

        @pl.when(ki < qi)
        def _():
            update(False)

        @pl.when(ki == qi)
        def _():
            update(True)
            first = lax.broadcasted_iota(jnp.int32, (t, LANES), 1) < 64
            a0, a1 = acc_sc[0], acc_sc[1]
            l0, l1 = pltpu.roll(a0, 64, 1), pltpu.roll(a1, 64, 1)
            o_ref[...] = jnp.where(first, a0 / l0, a1 / l1)
            lse_ref[0] = jnp.where(first, m_sc[0] * ATT_SCALE + jnp.log(l0), m_sc[1] * ATT_SCALE + jnp.log(l1))

        if nx:
            @pl.when((pl.program_id(0) == 3) & (step == len(steps) - 1))
            def _():
                for cp in copies:
                    cp.wait()

    grid_spec = pltpu.PrefetchScalarGridSpec(
        num_scalar_prefetch=2, grid=(4, len(steps)),
        in_specs=[pl.BlockSpec((t, 256), lambda p, s, qt, kt: (qt[s], p)),
                  pl.BlockSpec((t, 256), lambda p, s, qt, kt: (kt[s], p)),
                  pl.BlockSpec((t, LANES), lambda p, s, qt, kt: (kt[s], p))] + [ANY] * nx,
        out_specs=[pl.BlockSpec((t, LANES), lambda p, s, qt, kt: (qt[s], p)),
                   pl.BlockSpec((1, t, LANES), lambda p, s, qt, kt: (p, qt[s], 0))] + [ANY] * nx,
        scratch_shapes=[pltpu.VMEM((2, t, LANES), F32), pltpu.VMEM((2, t, LANES), F32)]
        + (_exchange_sems(nx) if nx else []))
    res = pl.pallas_call(
        body, name="flash_fwd", grid_spec=grid_spec,
        out_shape=[jax.ShapeDtypeStruct((SEQ, 512), F32), jax.ShapeDtypeStruct((4, SEQ, LANES), F32)]
        + _exchange_shapes([], bcast),
        compiler_params=_params(("arbitrary", "arbitrary")),
    )(qi_tab, ki_tab, q, k, v, *bcast)
    return res[0], res[1], res[2:]


def _flash_bwd(q, k, v, o, do, lse, scatter=()):
    t = ATT_T
    nb = SEQ // t

    steps = [(qi, ki) for ki in range(nb) for qi in range(ki, nb)]
    qi_tab = jnp.asarray([s[0] for s in steps], jnp.int32)
    ki_tab = jnp.asarray([s[1] for s in steps], jnp.int32)
    log2e = math.log2(math.e)

    sc_arrays, sc_ranges = _scatter_args(scatter)
    nx = len(sc_arrays)

    def body(qi_ref, ki_ref, q_ref, k_ref, v_ref, o_ref, do_ref, lse_ref, *rest):
        x_refs, (dq_ref, dk_ref, dv_ref), g_refs = rest[:nx], rest[nx:nx + 3], rest[nx + 3:2 * nx + 3]
        dkt_sc, dvt_sc = rest[2 * nx + 3:2 * nx + 5]
        step = pl.program_id(1)
        qi, ki = qi_ref[step], ki_ref[step]
        if nx:
            copies = _peer_copies(x_refs, g_refs, rest[2 * nx + 5:], sc_ranges)

            @pl.when((pl.program_id(0) == 0) & (step == 0))
            def _():
                for cp in copies:
                    cp.start()

        @pl.when(step == 0)
        def _():
            dq_ref[...] = jnp.zeros_like(dq_ref)

        @pl.when(qi == ki)
        def _():
            dkt_sc[...] = jnp.zeros_like(dkt_sc)
            dvt_sc[...] = jnp.zeros_like(dvt_sc)

        def update(diagonal):
            dov, ov, vv = do_ref[...], o_ref[...], v_ref[...]
            lse2 = lse_ref[0] * log2e
            lane = lax.broadcasted_iota(jnp.int32, (t, LANES), 1)
            row_t = lax.broadcasted_iota(jnp.int32, (LANES, t), 0)
            prod = dov * ov
            do_b = _mx(dov)
            qrows = pl.ds(pl.multiple_of(qi * t, t), t)
            dvt_acc = jnp.zeros((LANES, t), F32)
            dkt_new, dq_new = [], []
            for hd in range(2):
                sl = slice(hd * LANES, (hd + 1) * LANES)
                mine = (lane >= hd * 64) & (lane < (hd + 1) * 64)
                qh, kh = q_ref[:, sl], k_ref[:, sl]
                p = jnp.exp2(_dot_nt(qh, kh) * ATT_C - lse2[:, hd * 64:hd * 64 + 1])
                if diagonal:
                    p = jnp.where(lax.broadcasted_iota(jnp.int32, (t, t), 1)
                                  <= lax.broadcasted_iota(jnp.int32, (t, t), 0), p, 0.0)
                do_h = jnp.where(mine, dov, 0.0)
                delta = jnp.sum(jnp.where(mine, prod, 0.0), axis=1, keepdims=True)
                dp = _dot_nt(_mx(do_h), vv)
                ds = _mx(p * (dp - delta) * ATT_SCALE)
                dvt_acc = dvt_acc + jnp.where((row_t >= hd * 64) & (row_t < (hd + 1) * 64), _dot_tn(do_b, _mx(p)), 0.0)
                dkt_new.append(_dot_tn(qh, ds))
                dq_new.append(_dot(ds, kh))
            for hd in range(2):
                sl = slice(hd * LANES, (hd + 1) * LANES)
                dkt_sc[sl, :] += dkt_new[hd]
                dq_ref[qrows, sl] += dq_new[hd]
            dvt_sc[...] += dvt_acc

        @pl.when(qi > ki)
        def _():
            update(False)

        @pl.when(qi == ki)
        def _():
            update(True)

        @pl.when(qi == nb - 1)
        def _():
            dk_ref[...] = dkt_sc[...].T
            dv_ref[...] = dvt_sc[...].T

        if nx:
            @pl.when((pl.program_id(0) == 3) & (step == len(steps) - 1))
            def _():
                for cp in copies:
                    cp.wait()

    qmap = lambda p, s, qt, kt: (qt[s], p)
    kmap = lambda p, s, qt, kt: (kt[s], p)
    grid_spec = pltpu.PrefetchScalarGridSpec(
        num_scalar_prefetch=2, grid=(4, len(steps)),
        in_specs=[pl.BlockSpec((t, 256), qmap), pl.BlockSpec((t, 256), kmap), pl.BlockSpec((t, LANES), kmap),
                  pl.BlockSpec((t, LANES), qmap), pl.BlockSpec((t, LANES), qmap),
                  pl.BlockSpec((1, t, LANES), lambda p, s, qt, kt: (p, qt[s], 0))] + [ANY] * nx,
        out_specs=[pl.BlockSpec((SEQ, 256), lambda p, s, qt, kt: (0, p)), pl.BlockSpec((t, 256), kmap),
                   pl.BlockSpec((t, LANES), kmap)] + [ANY] * nx,
        scratch_shapes=[pltpu.VMEM((256, t), F32), pltpu.VMEM((LANES, t), F32)] + (_exchange_sems(nx) if nx else []))
    res = pl.pallas_call(
        body, name="flash_bwd", grid_spec=grid_spec,
        out_shape=[jax.ShapeDtypeStruct((SEQ, 1024), F32), jax.ShapeDtypeStruct((SEQ, 1024), F32),
                   jax.ShapeDtypeStruct((SEQ, 512), F32)] + _exchange_shapes(sc_arrays, []),
        compiler_params=_params(("arbitrary", "arbitrary")),
    )(qi_tab, ki_tab, q, k, v, o, do, lse, *sc_arrays)
    return res[0], res[1], res[2], res[3:]


def _rms_bwd(v, g, dy, eps=1e-6):
    rs = lax.rsqrt(jnp.mean(v * v, axis=-1, keepdims=True) + eps)
    xh = v * rs
    dxh = dy * g
    dv = rs * (dxh - xh * jnp.mean(dxh * xh, axis=-1, keepdims=True))
    return dv, jnp.sum(dy * xh, axis=0, keepdims=True)


def _mla_bwd(dq, dk, dv, proj0, qlat, klat, q_norm, kv_norm, wq, wkv, tc, tsa, tsb, xb):
    t = MLA_T

    def body(dq_ref, dk_ref, dv_ref, cq_ref, ck_ref, ql_ref, kl_ref, qn_ref, kn_ref, wq_ref, wkv_ref,
             c_ref, sa_ref, sb_ref, xb_ref, o_ref, gwq_ref, gwkv_ref, dgq_ref, dgk_ref, gwt_ref, oq_ref, okv_ref):
        @pl.when(pl.program_id(0) == 0)
        def _():
            dgq_ref[...] = jnp.zeros_like(dgq_ref)
            dgk_ref[...] = jnp.zeros_like(dgk_ref)
            gwq_ref[...] = jnp.zeros_like(gwq_ref)
            gwkv_ref[...] = jnp.zeros_like(gwkv_ref)
            gwt_ref[...] = jnp.zeros_like(gwt_ref)

        c, sa, sb = c_ref[...], sa_ref[...], sb_ref[...]
        lane = lax.broadcasted_iota(jnp.int32, (t, LANES), 1)
        dkr = jnp.zeros((t, LANES), F32)
        for hd in range(MLA_HEADS):
            sl = slice(hd * LANES, (hd + 1) * LANES)
            oq_ref[:, sl] = _mx(_rope_t(dq_ref[:, sl], c, sa, sb))
            dkh = dk_ref[:, sl]
            okv_ref[:, sl] = _mx(dkh)
            dkr = dkr + dkh
        okv_ref[:, 1024:] = _mx(dv_ref[...])
        dkr = _rope_t(jnp.where((lane >= 64) & (lane < 96), dkr, 0.0), c, sa, sb)
        dqraw, dkvraw = oq_ref[...], okv_ref[...]
        gwq_ref[...] += _dot_tn(ql_ref[...], dqraw)
        gwkv_ref[...] += _dot_tn(kl_ref[...], dkvraw)
        dqn = _dot_nt(dqraw, wq_ref[...])
        dkn = _dot_nt(dkvraw, wkv_ref[...])
        dcq, dgq = _rms_bwd(cq_ref[...], qn_ref[...], dqn)
        dck, dgk = _rms_bwd(ck_ref[:, :LANES], kn_ref[...], dkn)
        o_ref[:, :256] = _mx(dcq)
        o_ref[:, 256:384] = _mx(dck)
        o_ref[:, 384:] = _mx(dkr)
        gwt_ref[...] += _dot_tn(xb_ref[...], o_ref[...])
        dgq_ref[0:1, :] += dgq
        dgk_ref[0:1, :] += dgk

    tab = pl.BlockSpec((t, LANES), lambda i: (i, 0))
    wide = pl.BlockSpec((t, 1024), lambda i: (i, 0))
    const = lambda shape: pl.BlockSpec(shape, lambda i: (0, 0))
    return pl.pallas_call(
        body, name="mla_bwd", grid=(SEQ // t,),
        in_specs=[wide, wide, pl.BlockSpec((t, 512), lambda i: (i, 0)),
                  pl.BlockSpec((t, 256), lambda i: (i, 6)), pl.BlockSpec((t, 256), lambda i: (i, 7)),
                  pl.BlockSpec((t, 256), lambda i: (i, 0)), tab,
                  const((1, 256)), const((1, LANES)), const((256, 1024)), const((LANES, 1536)), tab, tab, tab, wide],
        out_specs=[pl.BlockSpec((t, 512), lambda i: (i, 0)), const((256, 1024)), const((LANES, 1536)),
                   const((SUBLANES, 256)), const((SUBLANES, LANES)), const((D_MODEL, 512))],
        out_shape=[jax.ShapeDtypeStruct((SEQ, 512), MXU_DTYPE), jax.ShapeDtypeStruct((256, 1024), F32),
                   jax.ShapeDtypeStruct((LANES, 1536), F32), jax.ShapeDtypeStruct((SUBLANES, 256), F32),
                   jax.ShapeDtypeStruct((SUBLANES, LANES), F32), jax.ShapeDtypeStruct((D_MODEL, 512), F32)],
        scratch_shapes=[pltpu.VMEM((t, 1024), MXU_DTYPE), pltpu.VMEM((t, 1536), MXU_DTYPE)],
        compiler_params=_params(("arbitrary",)),
    )(dq, dk, dv, proj0, proj0, qlat, klat, q_norm, kv_norm, wq, wkv, tc, tsa, tsb, xb)


LN_T = 512


def _ln(v, g, b, eps=1e-5):
    mu = jnp.mean(v, axis=-1, keepdims=True)
    xc = v - mu
    rs = lax.rsqrt(jnp.mean(xc * xc, axis=-1, keepdims=True) + eps)
    return xc * rs * g + b


def _ln_bwd(v, g, dy, eps=1e-5):
    mu = jnp.mean(v, axis=-1, keepdims=True)
    xc = v - mu
    rs = lax.rsqrt(jnp.mean(xc * xc, axis=-1, keepdims=True) + eps)
    xh = xc * rs
    dxh = dy * g
    dv = rs * (dxh - jnp.mean(dxh, axis=-1, keepdims=True) - xh * jnp.mean(dxh * xh, axis=-1, keepdims=True))
    return dv, jnp.sum(dy * xh, axis=0, keepdims=True), jnp.sum(dy, axis=0, keepdims=True)


def _l0_out(h, o, proj0, x, w_out, g, b):
    t = LN_T

    def body(h_ref, o_ref, ga_ref, gb_ref, x_ref, w_ref, g_ref, b_ref, y_ref, v_ref, x1_ref, x1b_ref):
        y = _mx(jnp.concatenate([h_ref[...] * _silu(ga_ref[...]), o_ref[...] * _silu(gb_ref[...])], axis=1))
        v = DN_ALPHA * x_ref[...] + _dot(y, w_ref[...])
        y_ref[...] = y
        v_ref[...] = v
        x1 = _ln(v, g_ref[...], b_ref[...])
        x1_ref[...] = x1
        x1b_ref[...] = _mx(x1)

    half = pl.BlockSpec((t, 512), lambda i: (i, 0))
    full = pl.BlockSpec((t, D_MODEL), lambda i: (i, 0))
    vec = pl.BlockSpec((1, D_MODEL), lambda i: (0, 0))
    return pl.pallas_call(
        body, name="l0_out", grid=(SEQ // t,),
        in_specs=[half, half, pl.BlockSpec((t, 512), lambda i: (i, 0)), pl.BlockSpec((t, 512), lambda i: (i, 1)), full,
                  pl.BlockSpec((D_MODEL, D_MODEL), lambda i: (0, 0)), vec, vec],
        out_specs=[full, full, full, full],
        out_shape=[jax.ShapeDtypeStruct((SEQ, D_MODEL), MXU_DTYPE), jax.ShapeDtypeStruct((SEQ, D_MODEL), F32),
                   jax.ShapeDtypeStruct((SEQ, D_MODEL), F32), jax.ShapeDtypeStruct((SEQ, D_MODEL), MXU_DTYPE)],
        compiler_params=_params(("parallel",)),
    )(h, o, proj0, proj0, x, w_out, g, b)


def _l1_in(x1b, w1z, w1d):
    t = 1024

    def body(x_ref, wz_ref, wd_ref, z_ref, dt_ref):
        xv = x_ref[...]
        z_ref[...] = _dot(xv, wz_ref[...])
        dt_ref[...] = _dot(xv, wd_ref[...])

    rows = lambda w: pl.BlockSpec((t, w), lambda i: (i, 0))
    const = lambda w: pl.BlockSpec((D_MODEL, w), lambda i: (0, 0))
    return pl.pallas_call(
        body, name="l1_in", grid=(SEQ // t,),
        in_specs=[rows(D_MODEL), const(SSD_INNER), const(LANES)],
        out_specs=[rows(SSD_INNER), rows(LANES)],
        out_shape=[jax.ShapeDtypeStruct((SEQ, SSD_INNER), F32), jax.ShapeDtypeStruct((SEQ, LANES), F32)],
        compiler_params=_params(("parallel",)),
    )(x1b, w1z, w1d)


def _l1_dx_ln(dz, dxbc, ddt, dv1, v0, w1z, w1x, w1d, g):
    t = LN_T

    def body(dz_ref, dx_ref, ddt_ref, dv1_ref, v_ref, wz_ref, wx_ref, wd_ref, g_ref, dv_ref, dgb_ref):
        @pl.when(pl.program_id(0) == 0)
        def _():
            dgb_ref[...] = jnp.zeros_like(dgb_ref)

        dy = (DN_ALPHA * dv1_ref[...] + _dot_nt(dz_ref[...], wz_ref[...]) + _dot_nt(dx_ref[...], wx_ref[...])
              + _dot_nt(_mx(ddt_ref[...]), wd_ref[...]))
        dv, dg, db = _ln_bwd(v_ref[...], g_ref[...], dy)
        dv_ref[...] = dv
        dgb_ref[0:1, :] += dg
        dgb_ref[1:2, :] += db

    rows = lambda w: pl.BlockSpec((t, w), lambda i: (i, 0))
    const = lambda w: pl.BlockSpec((D_MODEL, w), lambda i: (0, 0))
    return pl.pallas_call(
        body, name="l1_dx_ln", grid=(SEQ // t,),
        in_specs=[rows(SSD_INNER), rows(SSD_CONV), rows(LANES), rows(D_MODEL), rows(D_MODEL),
                  const(SSD_INNER), const(SSD_CONV), const(LANES), pl.BlockSpec((1, D_MODEL), lambda i: (0, 0))],
        out_specs=[rows(D_MODEL), pl.BlockSpec((SUBLANES, D_MODEL), lambda i: (0, 0))],
        out_shape=[jax.ShapeDtypeStruct((SEQ, D_MODEL), F32), jax.ShapeDtypeStruct((SUBLANES, D_MODEL), F32)],
        compiler_params=_params(("arbitrary",)),
    )(dz, dxbc, ddt, dv1, v0, w1z, w1x, w1d, g)


def _gate_bwd(dv0, w_out, h, o, proj0, y0, xb):
    t = LN_T

    def body(dv_ref, w_ref, h_ref, o_ref, ga_ref, gb_ref, y0_ref, xb_ref, dh_ref, do_ref, dg_ref, gwo_ref, gwg_ref):
        @pl.when(pl.program_id(0) == 0)
        def _():
            gwo_ref[...] = jnp.zeros_like(gwo_ref)
            gwg_ref[...] = jnp.zeros_like(gwg_ref)

        dvb = _mx(dv_ref[...])
        dy = _dot_nt(dvb, w_ref[...])
        ga, gb, dya, dyb = ga_ref[...], gb_ref[...], dy[:, :512], dy[:, 512:]
        dh_ref[...] = dya * _silu(ga)
        do_ref[...] = dyb * _silu(gb)
        dg_ref[:, :512] = _mx(dya * h_ref[...] * _dsilu(ga))
        dg_ref[:, 512:] = _mx(dyb * o_ref[...] * _dsilu(gb))
        gwo_ref[...] += _dot_tn(y0_ref[...], dvb)
        gwg_ref[...] += _dot_tn(xb_ref[...], dg_ref[...])

    half = pl.BlockSpec((t, 512), lambda i: (i, 0))
    half1 = pl.BlockSpec((t, 512), lambda i: (i, 1))
    full = pl.BlockSpec((t, 1024), lambda i: (i, 0))
    square = pl.BlockSpec((D_MODEL, D_MODEL), lambda i: (0, 0))
    return pl.pallas_call(
        body, name="gate_bwd", grid=(SEQ // t,),
        in_specs=[full, square, half, half, half, half1, full, full],
        out_specs=[half, half, full, square, square],
        out_shape=[jax.ShapeDtypeStruct((SEQ, 512), F32), jax.ShapeDtypeStruct((SEQ, 512), F32),
                   jax.ShapeDtypeStruct((SEQ, 1024), MXU_DTYPE), jax.ShapeDtypeStruct((D_MODEL, D_MODEL), F32),
                   jax.ShapeDtypeStruct((D_MODEL, D_MODEL), F32)],
        compiler_params=_params(("arbitrary",)),
    )(dv0, w_out, h, o, proj0, proj0, y0, xb)


CONV_T = 1024
CONV_CB = 1024


def _ssd_conv_fwd(x1b, w1x, cw8, cb):
    t, cbk = CONV_T, CONV_CB

    def body(x_ref, w_ref, cw_ref, cb_ref, xbc_ref, pre_ref, act_ref, carry):
        xbc = _dot(x_ref[...], w_ref[...])
        halo = jnp.where(pl.program_id(1) > 0, carry[...], 0.0)
        pre = _conv4(xbc, halo, cw_ref[...], cb_ref[...])
        carry[...] = xbc[t - SUBLANES:]
        xbc_ref[...] = xbc
        pre_ref[...] = pre
        act_ref[...] = _silu(pre)

    blk = pl.BlockSpec((t, cbk), lambda j, i: (i, j))
    out = jax.ShapeDtypeStruct((SEQ, SSD_CONV), F32)
    return pl.pallas_call(
        body, name="ssd_conv_fwd", grid=(SSD_CONV // cbk, SEQ // t),
        in_specs=[pl.BlockSpec((t, D_MODEL), lambda j, i: (i, 0)), pl.BlockSpec((D_MODEL, cbk), lambda j, i: (0, j)),
                  pl.BlockSpec((SUBLANES, cbk), lambda j, i: (0, j)), pl.BlockSpec((1, cbk), lambda j, i: (0, j))],
        out_specs=[blk, blk, blk], out_shape=[out, out, out],
        scratch_shapes=[pltpu.VMEM((SUBLANES, cbk), F32)],
        compiler_params=_params(("parallel", "arbitrary")),
    )(x1b, w1x, cw8, cb)


def _ssd_conv_bwd(dact, pre, xbc, cw8, x1b):
    t, cbk = CONV_T, CONV_CB
    tb = t // SUBLANES
    nb = SEQ // t

    def body(da_ref, dan_ref, pre_ref, pren_ref, x_ref, cw_ref, x1_ref, dx_ref, dcw_ref, gw_ref):
        i = pl.program_id(1)

        @pl.when(i == 0)
        def _():
            dcw_ref[...] = jnp.zeros_like(dcw_ref)
            gw_ref[...] = jnp.zeros_like(gw_ref)

        dpre = da_ref[...] * _dsilu(pre_ref[...])
        dpre_next = jnp.where(i < nb - 1, dan_ref[...] * _dsilu(pren_ref[...]), 0.0)
        xblk = x_ref[...]
        cw = cw_ref[...]
        dx = dpre * cw[3:4]
        dcw_ref[3:4, :] += jnp.sum(dpre * xblk, axis=0, keepdims=True)
        for k in range(3):
            up = _shift_up(dpre, dpre_next, 3 - k)
            dcw_ref[k:k + 1, :] += jnp.sum(up * xblk, axis=0, keepdims=True)
            dx = dx + up * cw[k:k + 1]
        dcw_ref[4:5, :] += jnp.sum(dpre, axis=0, keepdims=True)
        dxb = _mx(dx)
        dx_ref[...] = dxb
        gw_ref[...] += _dot_tn(x1_ref[...], dxb)

    blk = pl.BlockSpec((t, cbk), lambda j, i: (i, j))
    nxt = pl.BlockSpec((SUBLANES, cbk), lambda j, i: (jnp.minimum((i + 1) * tb, SEQ // SUBLANES - 1), j))
    acc = pl.BlockSpec((SUBLANES, cbk), lambda j, i: (0, j))
    return pl.pallas_call(
        body, name="ssd_conv_bwd", grid=(SSD_CONV // cbk, nb),
        in_specs=[blk, nxt, blk, nxt, blk, acc, pl.BlockSpec((t, D_MODEL), lambda j, i: (i, 0))],
        out_specs=[blk, acc, pl.BlockSpec((D_MODEL, cbk), lambda j, i: (0, j))],
        out_shape=[jax.ShapeDtypeStruct((SEQ, SSD_CONV), MXU_DTYPE), jax.ShapeDtypeStruct((SUBLANES, SSD_CONV), F32),
                   jax.ShapeDtypeStruct((D_MODEL, SSD_CONV), F32)],
        compiler_params=_params(("parallel", "arbitrary")),
    )(dact, dact, pre, pre, xbc, cw8, x1b)


def _ssd_common(dt_raw, bias, alog, tril, expand_t, xs):
    lane = lax.broadcasted_iota(jnp.int32, dt_raw.shape, 1)
    dt = jnp.where(lane < SSD_HEADS, _softplus(dt_raw + bias), 0.0)
    a_neg = -jnp.exp(alog)
    cs = _dot_hi(tril, dt * a_neg)
    dt_x = _expand_heads(dt, expand_t)
    ecs_x = _expand_heads(jnp.exp(cs), expand_t)
    ds_x = _expand_heads(jnp.exp(cs[SSD_L - 1:SSD_L, :] - cs), expand_t)
    return dt, a_neg, cs, dt_x, None, xs * dt_x, ds_x, ecs_x, ecs_x[SSD_L - 1:SSD_L, :]


def _expand_heads(v, expand_t):
    hi = v.astype(jnp.bfloat16)
    lo = (v - hi.astype(F32)).astype(jnp.bfloat16)
    return _dot_nt(hi, expand_t) + _dot_nt(lo, expand_t)


def _fold_heads(v, expand_t):
    hi = v.astype(jnp.bfloat16)
    lo = (v - hi.astype(F32)).astype(jnp.bfloat16)
    return _dot(hi, expand_t) + _dot(lo, expand_t)


def _ssd_decay(cs, cs_t, hh, causal):
    seg = cs[:, hh:hh + 1] - cs_t[hh:hh + 1, :]
    return jnp.where(causal, jnp.exp(jnp.where(causal, seg, 0.0)), 0.0)


def _ssd_scan_fwd(act, dt_raw, bias, alog, d_x, tril, expand_t):
    nc = SEQ // SSD_L
    gw = SSD_INNER // SSD_GROUPS

    def body(act_ref, dt_ref, bias_ref, alog_ref, dx_ref, tril_ref, et_ref, y_ref, hp_ref, h_sc):
        @pl.when(pl.program_id(0) == 0)
        def _():
            h_sc[...] = jnp.zeros_like(h_sc)

        xs = act_ref[:, :SSD_INNER]
        _, _, cs, _, _, xdt, ds_x, ecs_x, elast = _ssd_common(
            dt_ref[...], bias_ref[...], alog_ref[...], tril_ref[...], et_ref[...], xs)
        cs_t = cs.T
        causal = (lax.broadcasted_iota(jnp.int32, (SSD_L, SSD_L), 0)
                  >= lax.broadcasted_iota(jnp.int32, (SSD_L, SSD_L), 1))
        lane = lax.broadcasted_iota(jnp.int32, (SSD_L, LANES), 1)
        xdt_b = _mx(xdt)
        xds_b = _mx(xdt * ds_x)
        hp_ref[0] = h_sc[...]
        for g in range(SSD_GROUPS):
            gs = slice(g * gw, (g + 1) * gw)
            bg = _mx(act_ref[:, SSD_INNER + g * SSD_N:SSD_INNER + (g + 1) * SSD_N])
            cg = _mx(act_ref[:, SSD_INNER + 512 + g * SSD_N:SSD_INNER + 512 + (g + 1) * SSD_N])
            cb = _dot_nt(cg, bg)
            hprev = h_sc[:, gs]
            yoff = _dot(cg, _mx(hprev)) * ecs_x[:, gs]
            h_sc[:, gs] = hprev * elast[:, gs] + _dot_tn(bg, xds_b[:, gs])
            for pr in range(4):
                ps = slice(g * gw + pr * LANES, g * gw + (pr + 1) * LANES)
                xp = xdt_b[:, ps]
                ydiag = jnp.zeros((SSD_L, LANES), F32)
                for j in range(2):
                    dm = _ssd_decay(cs, cs_t, g * 8 + pr * 2 + j, causal)
                    mine = (lane >= j * 64) & (lane < (j + 1) * 64)
                    ydiag = ydiag + _dot(_mx(cb * dm), jnp.where(mine, xp, jnp.zeros_like(xp)))
                y_ref[:, ps] = ydiag + yoff[:, pr * LANES:(pr + 1) * LANES] + dx_ref[:, ps] * xs[:, ps]

    const = lambda shape: pl.BlockSpec(shape, lambda c: (0, 0))
    return pl.pallas_call(
        body, name="ssd_scan_fwd", grid=(nc,),
        in_specs=[pl.BlockSpec((SSD_L, SSD_CONV), lambda c: (c, 0)), pl.BlockSpec((SSD_L, LANES), lambda c: (c, 0)),
                  const((1, LANES)), const((1, LANES)), const((1, SSD_INNER)), const((SSD_L, SSD_L)),
                  const((SSD_INNER, LANES))],
        out_specs=[pl.BlockSpec((SSD_L, SSD_INNER), lambda c: (c, 0)),
                   pl.BlockSpec((1, SSD_N, SSD_INNER), lambda c: (c, 0, 0))],
        out_shape=[jax.ShapeDtypeStruct((SEQ, SSD_INNER), F32), jax.ShapeDtypeStruct((nc, SSD_N, SSD_INNER), F32)],
        scratch_shapes=[pltpu.VMEM((SSD_N, SSD_INNER), F32)],
        compiler_params=_params(("arbitrary",)),
    )(act, dt_raw, bias, alog, d_x, tril, expand_t)


def _ssd_scan_bwd(dy, act, dt_raw, hprev_all, bias, alog, d_x, tril, expand_t, x1b):
    nc = SEQ // SSD_L
    gw = SSD_INNER // SSD_GROUPS

    def body(dy_ref, act_ref, dt_ref, hp_ref, bias_ref, alog_ref, dx_ref, tril_ref, et_ref, x1_ref,
             dact_ref, ddt_ref, dvec_ref, gdt_ref, dh_sc, dd_sc):
        i = pl.program_id(0)

        @pl.when(i == 0)
        def _():
            dh_sc[...] = jnp.zeros_like(dh_sc)
            dd_sc[...] = jnp.zeros_like(dd_sc)
            gdt_ref[...] = jnp.zeros_like(gdt_ref)
            dvec_ref[...] = jnp.zeros_like(dvec_ref)

        xs = act_ref[:, :SSD_INNER]
        dt_raw_v, bias_v = dt_ref[...], bias_ref[...]
        dt, a_neg, cs, dt_x, _, xdt, ds_x, ecs_x, elast = _ssd_common(
            dt_raw_v, bias_v, alog_ref[...], tril_ref[...], et_ref[...], xs)
        cs_t = cs.T
        rowi = lax.broadcasted_iota(jnp.int32, (SSD_L, SSD_L), 0)
        coli = lax.broadcasted_iota(jnp.int32, (SSD_L, SSD_L), 1)
        causal = rowi >= coli
        lane = lax.broadcasted_iota(jnp.int32, (SSD_L, LANES), 1)
        row_g = lax.broadcasted_iota(jnp.int32, (SSD_L, gw), 0)
        dyv = dy_ref[...]
        dd_sc[0:1, :] += jnp.sum(dyv * xs, axis=0, keepdims=True)
        xdt_b = _mx(xdt)
        xds = xdt * ds_x
        xds_b = _mx(xds)
        dy_b = _mx(dyv)
        dye_b = _mx(dyv * ecs_x)
        dcs = jnp.zeros((SSD_L, LANES), F32)
        dcs_t = jnp.zeros((LANES, SSD_L), F32)
        dcs_parts = []
        dxdt_parts = []
        for g in range(SSD_GROUPS):
            gs = slice(g * gw, (g + 1) * gw)
            bcol = slice(SSD_INNER + g * SSD_N, SSD_INNER + (g + 1) * SSD_N)
            ccol = slice(SSD_INNER + 512 + g * SSD_N, SSD_INNER + 512 + (g + 1) * SSD_N)
            bg, cg = _mx(act_ref[:, bcol]), _mx(act_ref[:, ccol])
            cb = _dot_nt(cg, bg)
            hp = hp_ref[0, :, gs]
            hp_b = _mx(hp)
            dh = dh_sc[:, gs]
            dh_b = _mx(dh)
            yoff = _dot(cg, hp_b) * ecs_x[:, gs]
            bdh = _dot(bg, dh_b)
            tt = xds[:, gs] * bdh
            last_row = (jnp.sum(tt, axis=0, keepdims=True)
                        + jnp.sum(dh * hp, axis=0, keepdims=True) * elast[:, gs])
            dcs_parts.append(dyv[:, gs] * yoff - tt + jnp.where(row_g == SSD_L - 1, last_row, 0.0))
            dc_g = _dot_nt(dye_b[:, gs], hp_b)
            db_g = _dot_nt(xds_b[:, gs], dh_b)
            dh_sc[:, gs] = _dot_tn(cg, dye_b[:, gs]) + dh * elast[:, gs]
            wsum = jnp.zeros((SSD_L, SSD_L), F32)
            dxdt_g = []
            for pr in range(4):
                ps = slice(g * gw + pr * LANES, g * gw + (pr + 1) * LANES)
                xp, dyp = xdt_b[:, ps], dy_b[:, ps]
                dxp = jnp.zeros((SSD_L, LANES), F32)
                for j in range(2):
                    hh = g * 8 + pr * 2 + j
                    dm = _ssd_decay(cs, cs_t, hh, causal)
                    mine = (lane >= j * 64) & (lane < (j + 1) * 64)
                    dy_h = jnp.where(mine, dyp, jnp.zeros_like(dyp))
                    wd = _dot_nt(dy_h, xp) * dm
                    wsum = wsum + wd
                    gmat = wd * cb
                    dcs = dcs + jnp.where(lane == hh, jnp.sum(gmat, axis=1, keepdims=True), 0.0)
                    dcs_t = dcs_t - jnp.where(rowi == hh, jnp.sum(gmat, axis=0, keepdims=True), 0.0)
                    dxp = dxp + _dot_tn(_mx(cb * dm), dy_h)
                dxdt_g.append(dxp)
            dxdt_parts.append(jnp.concatenate(dxdt_g, axis=1) + bdh * ds_x[:, gs])
            ws_b = _mx(wsum)
            dact_ref[:, ccol] = dc_g + _dot(ws_b, bg)
            dact_ref[:, bcol] = db_g + _dot_tn(ws_b, cg)
        dxdt = jnp.concatenate(dxdt_parts, axis=1)
        dcs_x = jnp.concatenate(dcs_parts, axis=1)
        et = et_ref[...]
        dcs_tot = dcs + dcs_t.T + _fold_heads(dcs_x, et)
        da_dt = _dot_hi((coli >= rowi).astype(F32), dcs_tot)
        ddt = da_dt * a_neg + _fold_heads(dxdt * xs, et)
        ddt_raw = ddt * _sigmoid(dt_raw_v + bias_v)
        ddt_ref[...] = ddt_raw
        gdt_ref[...] += _dot_tn(x1_ref[...], _mx(ddt_raw))
        dvec_ref[0:1, :] += jnp.sum(ddt_raw, axis=0, keepdims=True)
        dvec_ref[1:2, :] += jnp.sum(da_dt * dt, axis=0, keepdims=True) * a_neg
        dact_ref[:, :SSD_INNER] = dyv * dx_ref[...] + dxdt * dt_x

        @pl.when(i == nc - 1)
        def _():
            dvec_ref[2:3, :] = _fold_heads(dd_sc[...], et)[0:1, :]

    const = lambda shape: pl.BlockSpec(shape, lambda c: (0, 0))
    rev = lambda c: (nc - 1 - c, 0)
    return pl.pallas_call(
        body, name="ssd_scan_bwd", grid=(nc,),
        in_specs=[pl.BlockSpec((SSD_L, SSD_INNER), rev), pl.BlockSpec((SSD_L, SSD_CONV), rev),
                  pl.BlockSpec((SSD_L, LANES), rev),
                  pl.BlockSpec((1, SSD_N, SSD_INNER), lambda c: (nc - 1 - c, 0, 0)),
                  const((1, LANES)), const((1, LANES)), const((1, SSD_INNER)), const((SSD_L, SSD_L)),
                  const((SSD_INNER, LANES)), pl.BlockSpec((SSD_L, D_MODEL), rev)],
        out_specs=[pl.BlockSpec((SSD_L, SSD_CONV), rev), pl.BlockSpec((SSD_L, LANES), rev), const((SUBLANES, LANES)),
                   const((D_MODEL, LANES))],
        out_shape=[jax.ShapeDtypeStruct((SEQ, SSD_CONV), F32), jax.ShapeDtypeStruct((SEQ, LANES), F32),
                   jax.ShapeDtypeStruct((SUBLANES, LANES), F32), jax.ShapeDtypeStruct((D_MODEL, LANES), F32)],
        scratch_shapes=[pltpu.VMEM((SSD_N, SSD_INNER), F32), pltpu.VMEM((SUBLANES, SSD_INNER), F32)],
        compiler_params=_params(("arbitrary",)),
    )(dy, act, dt_raw, hprev_all, bias, alog, d_x, tril, expand_t, x1b)


L1_T = 256


def _gated_norm(y, z, nw):
    y2 = y * _silu(z)
    gw = SSD_INNER // SSD_GROUPS
    outs, xhs, rss = [], [], []
    for g in range(SSD_GROUPS):
        gs = slice(g * gw, (g + 1) * gw)
        v = y2[:, gs]
        rs = lax.rsqrt(jnp.mean(v * v, axis=-1, keepdims=True) + 1e-6)
        xhs.append(v * rs)
        rss.append(rs)
        outs.append(v * rs * nw[:, gs])
    return outs, xhs, rss


def _l1_out(y, z, nw, w_out, x1, g, b, target):
    t = L1_T

    def body(y_ref, z_ref, nw_ref, w_ref, x1_ref, g_ref, b_ref, tg_ref, dv_ref, dgb_ref, loss_ref, gw_ref):
        @pl.when(pl.program_id(0) == 0)
        def _():
            dgb_ref[...] = jnp.zeros_like(dgb_ref)
            loss_ref[...] = jnp.zeros_like(loss_ref)
            gw_ref[...] = jnp.zeros_like(gw_ref)

        outs, _, _ = _gated_norm(y_ref[...], z_ref[...], nw_ref[...])
        yn = _mx(jnp.concatenate(outs, axis=1))
        v = DN_ALPHA * x1_ref[...] + _dot(yn, w_ref[...])
        gv = g_ref[...]
        err = _ln(v, gv, b_ref[...]) - tg_ref[...]
        rowsum = jnp.sum(err * err, axis=1, keepdims=True)
        loss_ref[...] += 0.5 * jnp.sum(rowsum, axis=0, keepdims=True) / D_MODEL
        dv, dg, db = _ln_bwd(v, gv, err / D_MODEL)
        dv_ref[...] = dv
        dgb_ref[0:1, :] += dg
        dgb_ref[1:2, :] += db
        gw_ref[...] += _dot_tn(yn, _mx(dv))

    wide = pl.BlockSpec((t, SSD_INNER), lambda i: (i, 0))
    full = pl.BlockSpec((t, D_MODEL), lambda i: (i, 0))
    vec = pl.BlockSpec((1, D_MODEL), lambda i: (0, 0))
    return pl.pallas_call(
        body, name="l1_out", grid=(SEQ // t,),
        in_specs=[wide, wide, pl.BlockSpec((1, SSD_INNER), lambda i: (0, 0)),
                  pl.BlockSpec((SSD_INNER, D_MODEL), lambda i: (0, 0)), full, vec, vec, full],
        out_specs=[full, pl.BlockSpec((SUBLANES, D_MODEL), lambda i: (0, 0)),
                   pl.BlockSpec((SUBLANES, LANES), lambda i: (0, 0)), pl.BlockSpec((SSD_INNER, D_MODEL), lambda i: (0, 0))],
        out_shape=[jax.ShapeDtypeStruct((SEQ, D_MODEL), F32), jax.ShapeDtypeStruct((SUBLANES, D_MODEL), F32),
                   jax.ShapeDtypeStruct((SUBLANES, LANES), F32), jax.ShapeDtypeStruct((SSD_INNER, D_MODEL), F32)],
        compiler_params=_params(("arbitrary",)),
    )(y, z, nw, w_out, x1, g, b, target)


def _l1_gate_bwd(dv1, w_out, y, z, nw, x1b):
    t = L1_T
    gw = SSD_INNER // SSD_GROUPS

    def body(dv_ref, w_ref, y_ref, z_ref, nw_ref, x1_ref, dy_ref, dz_ref, dnw_ref, gw_ref):
        @pl.when(pl.program_id(0) == 0)
        def _():
            dnw_ref[...] = jnp.zeros_like(dnw_ref)
            gw_ref[...] = jnp.zeros_like(gw_ref)

        dyn = _dot_nt(_mx(dv_ref[...]), w_ref[...])
        yv, zv, nwv = y_ref[...], z_ref[...], nw_ref[...]
        _, xhs, rss = _gated_norm(yv, zv, nwv)
        sz, dsz = _silu(zv), _dsilu(zv)
        for g in range(SSD_GROUPS):
            gs = slice(g * gw, (g + 1) * gw)
            d_out = dyn[:, gs]
            xh = xhs[g]
            dnw_ref[0:1, gs] += jnp.sum(d_out * xh, axis=0, keepdims=True)
            dxh = d_out * nwv[:, gs]
            dy2 = rss[g] * (dxh - xh * jnp.mean(dxh * xh, axis=-1, keepdims=True))
            dy_ref[:, gs] = dy2 * sz[:, gs]
            dz_ref[:, gs] = _mx(dy2 * yv[:, gs] * dsz[:, gs])
        gw_ref[...] += _dot_tn(x1_ref[...], dz_ref[...])

    wide = pl.BlockSpec((t, SSD_INNER), lambda i: (i, 0))
    return pl.pallas_call(
        body, name="l1_gate_bwd", grid=(SEQ // t,),
        in_specs=[pl.BlockSpec((t, D_MODEL), lambda i: (i, 0)), pl.BlockSpec((SSD_INNER, D_MODEL), lambda i: (0, 0)),
                  wide, wide, pl.BlockSpec((1, SSD_INNER), lambda i: (0, 0)), pl.BlockSpec((t, D_MODEL), lambda i: (i, 0))],
        out_specs=[wide, wide, pl.BlockSpec((SUBLANES, SSD_INNER), lambda i: (0, 0)),
                   pl.BlockSpec((D_MODEL, SSD_INNER), lambda i: (0, 0))],
        out_shape=[jax.ShapeDtypeStruct((SEQ, SSD_INNER), F32), jax.ShapeDtypeStruct((SEQ, SSD_INNER), MXU_DTYPE),
                   jax.ShapeDtypeStruct((SUBLANES, SSD_INNER), F32), jax.ShapeDtypeStruct((D_MODEL, SSD_INNER), F32)],
        compiler_params=_params(("arbitrary",)),
    )(dv1, w_out, y, z, nw, x1b)


MESH = pl.DeviceIdType.MESH
ANY = pl.BlockSpec(memory_space=pl.ANY)


def _flip(v, bit):
    return 1 - v if bit else v


def _all_gather(blocks, name):
    n = len(blocks)

    def body(*refs):
        x_refs, out_refs = refs[:n], refs[n:2 * n]
        send_sems, recv_sems, local_sems = refs[2 * n:]
        mx, my, mc = lax.axis_index("x"), lax.axis_index("y"), lax.axis_index("c")
        me, sibling = (mx, my, mc), (mx, my, 1 - mc)
        chips = [(1 - mx, my), (mx, 1 - my), (1 - mx, 1 - my)]

        def copy(a, k, block, to, own=False):
            px, py, pc = block
            slot = out_refs[a].at[4 * px + 2 * py + pc]
            return pltpu.make_async_remote_copy(
                src_ref=x_refs[a] if own else slot, dst_ref=slot,
                send_sem=send_sems.at[7 * a + k], recv_sem=recv_sems.at[7 * a + k], device_id=to, device_id_type=MESH)

        mine = [pltpu.make_async_copy(x_refs[a], out_refs[a].at[4 * mx + 2 * my + mc], local_sems.at[a])
                for a in range(n)]
        first = []
        for a in range(n):
            mine[a].start()
            first.append(copy(a, 0, me, sibling, own=True))
            first += [copy(a, 1 + j, me, (*chip, mc), own=True) for j, chip in enumerate(chips)]
        for cp in first:
            cp.start()
        passed = []
        for j, chip in enumerate(chips):
            for a in range(n):
                copy(a, 1 + j, (*chip, mc), me).wait_recv()
                fwd = copy(a, 4 + j, (*chip, mc), sibling)
                fwd.start()
                passed.append(fwd)
        for a in range(n):
            copy(a, 0, sibling, me).wait_recv()
            for j, chip in enumerate(chips):
                copy(a, 4 + j, (*chip, 1 - mc), me).wait_recv()
        for cp in first + passed:
            cp.wait_send()
        for cp in mine:
            cp.wait()

    return pl.pallas_call(
        body, name=name, in_specs=[ANY] * n, out_specs=[ANY] * n,
        out_shape=[jax.ShapeDtypeStruct((N_DEV,) + b.shape, b.dtype) for b in blocks],
        scratch_shapes=[pltpu.SemaphoreType.DMA((7 * n,)), pltpu.SemaphoreType.DMA((7 * n,)),
                        pltpu.SemaphoreType.DMA((n,))],
    )(*blocks)


def _l0_in(x, w0p, bcast=()):
    n = len(bcast)
    tm, tn = 1024, 1024
    gi, gj = SEQ // tm, 2048 // tn

    def body(x_ref, w_ref, *rest):
        o_ref, xb_ref = rest[n], rest[n + 1]
        i, j = pl.program_id(0), pl.program_id(1)
        if n:
            copies = _peer_copies(rest[:n], rest[n + 2:2 * n + 2], rest[2 * n + 2:], [])

            @pl.when((i == 0) & (j == 0))
            def _():
                for cp in copies:
                    cp.start()

        xb = _mx(x_ref[...])
        xb_ref[...] = xb
        o_ref[...] = _dot(xb, w_ref[...])

        if n:
            @pl.when((i == gi - 1) & (j == gj - 1))
            def _():
                for cp in copies:
                    cp.wait()

    res = pl.pallas_call(
        body, name="l0_in", grid=(gi, gj),
        in_specs=[pl.BlockSpec((tm, D_MODEL), lambda i, j: (i, 0)), pl.BlockSpec((D_MODEL, tn), lambda i, j: (0, j))]
        + [ANY] * n,
        out_specs=[pl.BlockSpec((tm, tn), lambda i, j: (i, j)), pl.BlockSpec((tm, D_MODEL), lambda i, j: (i, 0))]
        + [ANY] * n,
        out_shape=[jax.ShapeDtypeStruct((SEQ, 2048), F32), jax.ShapeDtypeStruct((SEQ, D_MODEL), MXU_DTYPE)]
        + _exchange_shapes([], bcast),
        scratch_shapes=_exchange_sems(n) if n else [],
        compiler_params=_params(("arbitrary", "arbitrary")),
    )(x, w0p, *bcast)
    return res[0], res[1], res[2:]


def _l0_dx(dxr, dgate, dtail, w0p, dv0, scatter=(), bcast=()):
    arrays, ranges = _scatter_args(scatter)
    n = len(arrays) + len(bcast)
    tm = 1024
    steps = SEQ // tm

    def body(dxr_ref, dg_ref, dt_ref, w_ref, dv_ref, *rest):
        o_ref = rest[n]
        i = pl.program_id(0)
        if n:
            copies = _peer_copies(rest[:n], rest[n + 1:2 * n + 1], rest[2 * n + 1:], ranges)

            @pl.when(i == 0)
            def _():
                for cp in copies:
                    cp.start()

        o_ref[...] = (DN_ALPHA * dv_ref[...] + _dot_nt(dg_ref[...], w_ref[:, 0:1024])
                      + _dot_nt(dxr_ref[...], w_ref[:, 1024:1536]) + _dot_nt(dt_ref[...], w_ref[:, 1536:2048]))

        if n:
            @pl.when(i == steps - 1)
            def _():
                for cp in copies:
                    cp.wait()

    rows = lambda w: pl.BlockSpec((tm, w), lambda i: (i, 0))
    res = pl.pallas_call(
        body, name="l0_dx", grid=(steps,),
        in_specs=[rows(512), rows(1024), rows(512), pl.BlockSpec((D_MODEL, 2048), lambda i: (0, 0)), rows(D_MODEL)]
        + [ANY] * n,
        out_specs=[rows(D_MODEL)] + [ANY] * n,
        out_shape=[jax.ShapeDtypeStruct((SEQ, D_MODEL), F32)] + _exchange_shapes(arrays, bcast),
        scratch_shapes=_exchange_sems(n) if n else [],
        compiler_params=_params(("arbitrary",)),
    )(dxr, dgate, dtail, w0p, dv0, *arrays, *bcast)
    return res[0], res[1:]


def _scatter_args(scatter):
    arrays = [s[0] if isinstance(s, tuple) else s for s in scatter]
    ranges = [s[1] if isinstance(s, tuple) else (0, N_DEV) for s in scatter]
    return arrays, ranges


def _exchange_shapes(scatter, bcast):
    return ([jax.ShapeDtypeStruct((N_DEV,) + a.shape[1:], a.dtype) for a in scatter]
            + [jax.ShapeDtypeStruct((N_DEV,) + a.shape, a.dtype) for a in bcast])


def _exchange_sems(n):
    return [pltpu.SemaphoreType.DMA((7 * n,)), pltpu.SemaphoreType.DMA((7 * n,)), pltpu.SemaphoreType.DMA((n,))]


class _GuardedCopy:
    def __init__(self, copy, send=None, recv=None, local=False):
        self.copy, self.send, self.recv, self.local = copy, send, recv, local

    @staticmethod
    def _run(pred, fn):
        if pred is None:
            fn()
        else:
            pl.when(pred)(fn)

    def start(self):
        self._run(self.send, self.copy.start)

    def wait(self):
        if self.local:
            self._run(self.send, self.copy.wait)
        else:
            self._run(self.send, self.copy.wait_send)
            self._run(self.recv, self.copy.wait_recv)


def _peer_copies(in_refs, out_refs, sems, ranges):
    send_sems, recv_sems, local_sems = sems
    n, ns = len(in_refs), len(ranges)
    mx, my, mc = lax.axis_index("x"), lax.axis_index("y"), lax.axis_index("c")
    me = 4 * mx + 2 * my + mc

    def src(a, slot):
        return in_refs[a].at[slot - ranges[a][0]] if a < ns else in_refs[a]

    def member(a, dev):
        if a >= ns or ranges[a] == (0, N_DEV):
            return None
        return (dev >= ranges[a][0]) & (dev < ranges[a][1])

    copies = [_GuardedCopy(pltpu.make_async_copy(src(a, me), out_refs[a].at[me], local_sems.at[a]),
                           send=member(a, me), local=True) for a in range(n)]
    for k in range(1, N_DEV):
        px, py, pc = _flip(mx, (k >> 2) & 1), _flip(my, (k >> 1) & 1), _flip(mc, k & 1)
        peer = 4 * px + 2 * py + pc
        for a in range(n):
            copies.append(_GuardedCopy(pltpu.make_async_remote_copy(
                src_ref=src(a, peer), dst_ref=out_refs[a].at[me],
                send_sem=send_sems.at[7 * a + k - 1], recv_sem=recv_sems.at[7 * a + k - 1],
                device_id=(px, py, pc), device_id_type=MESH), send=member(a, peer), recv=member(a, me)))
    return copies


def _segments(col_map, width):
    segs = []
    for lo, hi, arr, alo in col_map:
        for s in range(N_DEV):
            a, b = max(lo, s * width), min(hi, (s + 1) * width)
            if a < b:
                segs.append((s, a - s * width, b - a, arr, alo + a - lo))
    return segs


COPY_ROWS = 256


def _unshard(g8, col_map, widths, name):
    _, r, w = g8.shape
    rb = min(r, COPY_ROWS)
    segs = _segments(col_map, w)

    def body(g_ref, *o_refs):
        for o_ref in o_refs:
            o_ref[...] = jnp.zeros_like(o_ref)
        for s, llo, n, arr, alo in segs:
            o_refs[arr][:, alo:alo + n] = g_ref[s, :, llo:llo + n]

    return pl.pallas_call(
        body, name=name, grid=(r // rb,),
        in_specs=[pl.BlockSpec((N_DEV, rb, w), lambda i: (0, i, 0))],
        out_specs=[pl.BlockSpec((rb, n), lambda i: (i, 0)) for n in widths],
        out_shape=[jax.ShapeDtypeStruct((r, n), g8.dtype) for n in widths],
        compiler_params=_params(("parallel",)),
    )(g8)


def _reshard(srcs, col_map, w, dtype, name, shards=(0, N_DEV)):
    r = srcs[0].shape[0]
    rb = min(r, COPY_ROWS)
    lo, hi = shards
    segs = [sg for sg in _segments(col_map, w) if lo <= sg[0] < hi]

    def body(*refs):
        o_ref = refs[-1]
        for s, llo, n, arr, alo in segs:
            o_ref[s - lo, :, llo:llo + n] = refs[arr][:, alo:alo + n].astype(dtype)

    return pl.pallas_call(
        body, name=name, grid=(r // rb,),
        in_specs=[pl.BlockSpec((rb, a.shape[1]), lambda i: (i, 0)) for a in srcs],
        out_specs=pl.BlockSpec((hi - lo, rb, w), lambda i: (0, i, 0)),
        out_shape=jax.ShapeDtypeStruct((hi - lo, r, w), dtype),
        compiler_params=_params(("parallel",)),
    )(*srcs)


def _adamw(parts, w, m, v, name):
    r, c = w.shape
    tr = COPY_ROWS if r % COPY_ROWS == 0 else r

    def body(p_ref, w_ref, m_ref, v_ref, g_ref, d_ref, mo_ref, vo_ref):
        g = p_ref[0].astype(F32)
        for s in range(1, N_DEV):
            g = g + p_ref[s].astype(F32)
        g_ref[...] = g
        d_ref[...], mo_ref[...], vo_ref[...] = _adamw_math(g, w_ref[...], m_ref[...], v_ref[...])

    blk = pl.BlockSpec((tr, c), lambda i: (i, 0))
    out = jax.ShapeDtypeStruct((r, c), F32)
    return pl.pallas_call(
        body, name=name, grid=(r // tr,),
        in_specs=[pl.BlockSpec((N_DEV, tr, c), lambda i: (0, i, 0)), blk, blk, blk],
        out_specs=[blk, blk, blk, blk], out_shape=[out, out, out, out],
        compiler_params=_params(("parallel",)),
    )(parts, w, m, v)


def _adamw_math(g, w, m, v):
    mn = ADAM_B1 * m + (1.0 - ADAM_B1) * g
    vn = ADAM_B2 * v + (1.0 - ADAM_B2) * (g * g)
    m_hat = mn / (1.0 - ADAM_B1 ** ADAM_STEP)
    v_hat = vn / (1.0 - ADAM_B2 ** ADAM_STEP)
    return -ADAM_LR * (m_hat / (jnp.sqrt(v_hat) + ADAM_EPS) + ADAM_WD * w), mn, vn


SMALL = (("ab_conv_w", 0, 4, 64), ("ssd_conv_w", 4, 4, 384), ("ssd_conv_b", 8, 1, 384), ("ssd_norm", 9, 1, 256),
         ("ssd_ln_g", 10, 1, 128), ("ssd_ln_b", 11, 1, 128))
VECS = (("ab_conv_b", 512), ("ab_gate_a_b", 512), ("ab_gate_x_b", 512), ("ab_lambda", 512), ("mla_q_norm", 256),
        ("mla_kv_norm", 128), ("ab_ln_g", 1024), ("ab_ln_b", 1024), ("ssd_dt_bias", 32), ("ssd_a_log", 32),
        ("ssd_d", 32))
GATES = ("ab_gate_a_w", "ab_gate_x_w")
SMALL_NAMES = tuple(n for n, *_ in SMALL) + tuple(n for n, _ in VECS) + GATES
VMEM_WHOLE = pl.BlockSpec(memory_space=pltpu.VMEM)


def _view2d(name, a):
    if name in GATES:
        return a.reshape(RNN_W, 64)
    return a[0] if a.ndim == 3 else a


def _unshard_small(g):
    widths = (512, 3072, 3072, 2048, 1024, 1024)

    def body(*refs):
        ins, outs = refs[:6], refs[6:]
        outs[0][...] = jnp.zeros_like(outs[0])
        outs[1][...] = jnp.zeros_like(outs[1])
        for (_, _, nr, c), i_ref, o_ref in zip(SMALL, ins, outs):
            for j in range(N_DEV):
                o_ref[0:nr, j * c:(j + 1) * c] = i_ref[j]

    return pl.pallas_call(
        body, name="unshard_small", in_specs=[VMEM_WHOLE] * 6, out_specs=[VMEM_WHOLE] * 6,
        out_shape=[jax.ShapeDtypeStruct((SUBLANES if nr == 4 else 1, w), F32) for (_, _, nr, _), w in zip(SMALL, widths)],
    )(*g)


def _prep_repl(ga, gx, dt_bias, a_log, d):
    def body(ga_ref, gx_ref, b_ref, al_ref, d_ref, wa_ref, wx_ref, b128_ref, al128_ref, dx_ref):
        wa_ref[...] = jnp.zeros_like(wa_ref)
        wx_ref[...] = jnp.zeros_like(wx_ref)
        for hd in range(8):
            hs = slice(hd * 64, (hd + 1) * 64)
            wa_ref[hs, hs] = _mx(ga_ref[hs, :])
            wx_ref[hs, hs] = _mx(gx_ref[hs, :])
        b128_ref[...] = jnp.zeros_like(b128_ref)
        al128_ref[...] = jnp.zeros_like(al128_ref)
        b128_ref[:, 0:SSD_HEADS] = b_ref[...]
        al128_ref[:, 0:SSD_HEADS] = al_ref[...]
        dv = d_ref[...]
        for hd in range(SSD_HEADS):
            dx_ref[:, hd * SSD_P:(hd + 1) * SSD_P] = jnp.broadcast_to(dv[:, hd:hd + 1], (1, SSD_P))

    return pl.pallas_call(
        body, name="prep_repl", in_specs=[VMEM_WHOLE] * 5, out_specs=[VMEM_WHOLE] * 5,
        out_shape=[jax.ShapeDtypeStruct((RNN_W, RNN_W), MXU_DTYPE), jax.ShapeDtypeStruct((RNN_W, RNN_W), MXU_DTYPE),
                   jax.ShapeDtypeStruct((1, LANES), F32), jax.ShapeDtypeStruct((1, LANES), F32),
                   jax.ShapeDtypeStruct((1, SSD_INNER), F32)],
    )(ga, gx, dt_bias, a_log, d)


LOSS_ROW = 11


def _pack_small(dvec0, g_wa, g_wx, dqnw, dknw, dgb0, dvec1, dcw1, dnw, dgb1, loss8):
    def body(dvec0_ref, gwa_ref, gwx_ref, dqn_ref, dkn_ref, dgb0_ref, dvec1_ref, dcw1_ref, dnw_ref, dgb1_ref,
             loss_ref, sm_ref, vec_ref, gg_ref):
        sm_ref[...] = jnp.zeros_like(sm_ref)
        vec_ref[...] = jnp.zeros_like(vec_ref)
        sharded = ((dvec0_ref, 4), (dcw1_ref, 0), (dcw1_ref, 4), (dnw_ref, 0), (dgb1_ref, 0), (dgb1_ref, 1))
        for (_, r0, nr, c), (src, sr) in zip(SMALL, sharded):
            for j in range(N_DEV):
                sm_ref[j, r0:r0 + nr, 0:c] = src[sr:sr + nr, j * c:(j + 1) * c]
        vectors = ((dvec0_ref, 3), (dvec0_ref, 0), (dvec0_ref, 1), (dvec0_ref, 2), (dqn_ref, 0), (dkn_ref, 0),
                   (dgb0_ref, 0), (dgb0_ref, 1), (dvec1_ref, 0), (dvec1_ref, 1), (dvec1_ref, 2))
        for row, ((_, c), (src, sr)) in enumerate(zip(VECS, vectors)):
            vec_ref[row:row + 1, 0:c] = src[sr:sr + 1, 0:c]
        vec_ref[LOSS_ROW:LOSS_ROW + 1, 0:LANES] = loss_ref[0:1, :]
        for hd in range(8):
            hs = slice(hd * 64, (hd + 1) * 64)
            gg_ref[hs, 0:64] = _mx(gwa_ref[hs, hs])
            gg_ref[hs, 64:128] = _mx(gwx_ref[hs, hs])

    return pl.pallas_call(
        body, name="pack_small", in_specs=[VMEM_WHOLE] * 11, out_specs=[VMEM_WHOLE] * 3,
        out_shape=[jax.ShapeDtypeStruct((N_DEV, 16, 384), F32), jax.ShapeDtypeStruct((16, 1024), F32),
                   jax.ShapeDtypeStruct((RNN_W, LANES), MXU_DTYPE)],
    )(dvec0, g_wa, g_wx, dqnw, dknw, dgb0, dvec1, dcw1, dnw, dgb1, loss8)


def _adamw_small(recv_sm, recv_vec, recv_gg, wmv):
    plan = ([(0, r0, nr, c) for _, r0, nr, c in SMALL] + [(1, row, 1, c) for row, (_, c) in enumerate(VECS)]
            + [(2, 0, RNN_W, 0), (2, 0, RNN_W, 64)])
    n = len(plan)

    def body(*refs):
        recv, ins, outs = refs[:3], refs[3:3 + 3 * n], refs[3 + 3 * n:]
        for i, (src, r0, nr, c) in enumerate(plan):
            cols = slice(c, c + 64) if src == 2 else slice(0, c)
            g = recv[src][0, r0:r0 + nr, cols].astype(F32)
            for s in range(1, N_DEV):
                g = g + recv[src][s, r0:r0 + nr, cols].astype(F32)
            w_ref, m_ref, v_ref = ins[3 * i:3 * i + 3]
            outs[4 * i][...] = g
            outs[4 * i + 1][...], outs[4 * i + 2][...], outs[4 * i + 3][...] = _adamw_math(
                g, w_ref[...], m_ref[...], v_ref[...])
        loss = recv[1][0, LOSS_ROW:LOSS_ROW + 1, 0:LANES]
        for s in range(1, N_DEV):
            loss = loss + recv[1][s, LOSS_ROW:LOSS_ROW + 1, 0:LANES]
        outs[4 * n][...] = loss

    flat = [a for t in wmv for a in t]
    return pl.pallas_call(
        body, name="adamw_small", in_specs=[VMEM_WHOLE] * (3 + 3 * n), out_specs=[VMEM_WHOLE] * (4 * n + 1),
        out_shape=[jax.ShapeDtypeStruct(t[0].shape, F32) for t in wmv for _ in range(4)]
        + [jax.ShapeDtypeStruct((1, LANES), F32)],
    )(recv_sm, recv_vec, recv_gg, *flat)


BIG_L0 = ("ab_w_in", "ab_w_out", "mla_w_uq", "mla_w_ukv")
BIG_L1 = ("ssd_w_in", "ssd_w_out")

MAP_W0 = ((0, 512, 0, 1024), (512, 1536, 0, 0), (1536, 1920, 0, 1536), (1920, 1952, 0, 1984))
MAP_W1 = ((0, 2048, 0, 0), (2048, 5120, 1, 0), (5120, 5152, 2, 0))
MAP_WQ = tuple((96 * hd, 96 * hd + 96, 0, 128 * hd) for hd in range(8))
MAP_WKV = (tuple((128 * hd, 128 * hd + 64, 0, 128 * hd) for hd in range(8))
           + tuple((128 * hd + 64, 128 * hd + 128, 0, 1024 + 64 * hd) for hd in range(8)))
MAP_G0 = ((0, 512, 0, 0), (512, 1536, 1, 0), (1536, 1920, 2, 0), (1920, 1952, 2, 448))
W0_EARLY, W0_LATE = (0, 6), (6, 8)


def kernel(x, positions, ab_w_in, ab_conv_w, ab_conv_b, ab_gate_a_w, ab_gate_a_b, ab_gate_x_w, ab_gate_x_b, ab_lambda, mla_q_norm, mla_kv_norm, mla_w_uq, mla_w_ukv, ab_w_out, ab_ln_g, ab_ln_b, ssd_w_in, ssd_conv_w, ssd_conv_b, ssd_dt_bias, ssd_a_log, ssd_d, ssd_norm, ssd_w_out, ssd_ln_g, ssd_ln_b, loss_target, m_ab_w_in, m_ab_conv_w, m_ab_conv_b, m_ab_gate_a_w, m_ab_gate_a_b, m_ab_gate_x_w, m_ab_gate_x_b, m_ab_lambda, m_mla_q_norm, m_mla_kv_norm, m_mla_w_uq, m_mla_w_ukv, m_ab_w_out, m_ab_ln_g, m_ab_ln_b, m_ssd_w_in, m_ssd_conv_w, m_ssd_conv_b, m_ssd_dt_bias, m_ssd_a_log, m_ssd_d, m_ssd_norm, m_ssd_w_out, m_ssd_ln_g, m_ssd_ln_b, v_ab_w_in, v_ab_conv_w, v_ab_conv_b, v_ab_gate_a_w, v_ab_gate_a_b, v_ab_gate_x_w, v_ab_gate_x_b, v_ab_lambda, v_mla_q_norm, v_mla_kv_norm, v_mla_w_uq, v_mla_w_ukv, v_ab_w_out, v_ab_ln_g, v_ab_ln_b, v_ssd_w_in, v_ssd_conv_w, v_ssd_conv_b, v_ssd_dt_bias, v_ssd_a_log, v_ssd_d, v_ssd_norm, v_ssd_w_out, v_ssd_ln_g, v_ssd_ln_b):
    args = dict(locals())
    bf = MXU_DTYPE
    big = {n: [args[pre + n][0] for pre in ("", "m_", "v_")] for n in BIG_L0 + BIG_L1}
    sml = {n: [_view2d(n, args[pre + n]) for pre in ("", "m_", "v_")] for n in SMALL_NAMES}

    w0_8, cw0_8 = _all_gather([big["ab_w_in"][0].astype(bf), sml["ab_conv_w"][0]], "gather_params")
    p = {"cw0_8": cw0_8, "l0_blocks": [big[n][0].astype(bf) for n in BIG_L0[1:]] + [sml[n][0] for n, *_ in SMALL[1:]]}
    p["w0p"], = _unshard(w0_8, MAP_W0, (2048,), "unshard_w0")
    p["wa"], p["wx"], p["dt_bias"], p["a_log"], p["d_x"] = _prep_repl(
        sml["ab_gate_a_w"][0], sml["ab_gate_x_w"][0], sml["ssd_dt_bias"][0], sml["ssd_a_log"][0], sml["ssd_d"][0])
    for key, n in (("cb0", "ab_conv_b"), ("ba", "ab_gate_a_b"), ("bx", "ab_gate_x_b"), ("lam", "ab_lambda"),
                   ("qn_w", "mla_q_norm"), ("kn_w", "mla_kv_norm"), ("g0", "ab_ln_g"), ("b0", "ab_ln_b")):
        p[key] = sml[n][0]

    _, recv_early, recv, _, grad_x = _local_step(
        x[0], positions[0], loss_target[0], p, [big[n][0].astype(bf) for n in BIG_L1])

    me = 4 * lax.axis_index("x") + 2 * lax.axis_index("y") + lax.axis_index("c")
    parts = {"ssd_w_in": recv_early[0], "ssd_w_out": recv_early[1], "ab_w_out": recv_early[2],
             "ab_w_in": jnp.where(me >= W0_LATE[0], recv[0], recv_early[3]), "mla_w_uq": recv[1], "mla_w_ukv": recv[2]}

    outs = {}
    kinds = ("grad", "delta", "new_m", "new_v")
    for n in BIG_L0 + BIG_L1:
        for kind, res in zip(kinds, _adamw(parts[n], *big[n], "adamw_" + n)):
            outs[kind, n] = res[None]
    res = _adamw_small(*recv[3:], [sml[n] for n in SMALL_NAMES])
    for i, n in enumerate(SMALL_NAMES):
        for k, kind in enumerate(kinds):
            outs[kind, n] = res[4 * i + k].reshape(args[n].shape)

    loss = res[4 * len(SMALL_NAMES)][0, 0]
    order = ["ab_w_in", "ab_conv_w", "ab_conv_b", "ab_gate_a_w", "ab_gate_a_b", "ab_gate_x_w", "ab_gate_x_b",
             "ab_lambda", "mla_q_norm", "mla_kv_norm", "mla_w_uq", "mla_w_ukv", "ab_w_out", "ab_ln_g", "ab_ln_b",
             "ssd_w_in", "ssd_conv_w", "ssd_conv_b", "ssd_dt_bias", "ssd_a_log", "ssd_d", "ssd_norm", "ssd_w_out",
             "ssd_ln_g", "ssd_ln_b"]
    return (loss, grad_x[None], *[outs[kind, n] for kind in ("grad", "delta", "new_m", "new_v") for n in order])


def _local_step(x, pos, target, p, l1_blocks):
    bf = MXU_DTYPE
    inv_freq = 10000.0 ** (-jnp.arange(0, 32, 2, dtype=F32) / 32)
    ang = pos.astype(F32)[:, None] * inv_freq
    cos, sin = jnp.cos(ang), jnp.sin(ang)
    zeros = lambda n: jnp.zeros((SEQ, n), F32)
    tc = jnp.concatenate([jnp.ones((SEQ, 64), F32), cos, cos, zeros(32)], axis=1)
    tsa = jnp.concatenate([zeros(64), -sin, zeros(48)], axis=1)
    tsb = jnp.concatenate([zeros(80), sin, zeros(32)], axis=1)

    w0p, wa, wxg = (p[k] for k in ("w0p", "wa", "wx"))
    cb0, ba, bx, lam = (p[k] for k in ("cb0", "ba", "bx", "lam"))
    qn_w, kn_w, g0, b0 = (p[k] for k in ("qn_w", "kn_w", "g0", "b0"))
    dt_bias, a_log, d_x = (p[k] for k in ("dt_bias", "a_log", "d_x"))
    tril = jnp.tril(jnp.ones((SSD_L, SSD_L), F32))
    expand_t = (jnp.arange(SSD_INNER)[:, None] // SSD_P == jnp.arange(LANES)[None, :]).astype(jnp.bfloat16)

    proj0, xb, l0_8 = _l0_in(x, w0p, bcast=p["l0_blocks"])
    wo0 = l0_8[0].reshape(D_MODEL, D_MODEL)
    wq, = _unshard(l0_8[1], MAP_WQ, (1024,), "unshard_wq")
    wkv, = _unshard(l0_8[2], MAP_WKV, (1536,), "unshard_wkv")
    cw0, cw1, cb1, nw, g1, b1 = _unshard_small([p["cw0_8"]] + list(l0_8[3:]))
    xc, h = _rglru_fwd(proj0, cw0, cb0, wa, ba, wxg, bx, lam)
    qn, kn, qc, kc, vc = _mla_fwd(proj0, qn_w, kn_w, wq, wkv, tc, tsa, tsb)
    o, lse, (w1_8, wo1_8) = _flash_fwd(qc, kc, vc, bcast=l1_blocks)
    w1z, w1x, w1d = _unshard(w1_8, MAP_W1, (2048, 3072, 128), "unshard_w1")
    wo1 = wo1_8.reshape(SSD_INNER, D_MODEL)
    y0, v0, x1, x1b = _l0_out(h, o, proj0, x, wo0, g0, b0)

    z, dt_raw = _l1_in(x1b, w1z, w1d)
    xbc, pre, act = _ssd_conv_fwd(x1b, w1x, cw1, cb1)
    ys, hprev = _ssd_scan_fwd(act, dt_raw, dt_bias, a_log, d_x, tril, expand_t)
    dv1, dgb1, loss8, g_wo1 = _l1_out(ys, z, nw, wo1, x1, g1, b1, target)

    dys, dz, dnw, g_z = _l1_gate_bwd(dv1, wo1, ys, z, nw, x1b)
    dact, ddt_raw, dvec1, g_dt = _ssd_scan_bwd(dys, act, dt_raw, hprev, dt_bias, a_log, d_x, tril, expand_t, x1b)
    dxbc, dcw1, g_xbc = _ssd_conv_bwd(dact, pre, xbc, cw1, x1b)

    dv0, dgb0 = _l1_dx_ln(dz, dxbc, ddt_raw, dv1, v0, w1z, w1x, w1d, g0)
    dh, do, dgate, g_wo0, g_gate = _gate_bwd(dv0, wo0, h, o, proj0, y0, xb)
    dxr, g_wa, g_wx, dvec0, g_rnn = _rglru_bwd(dh, xc, h, proj0, cw0, wa, ba, wxg, bx, lam, xb)
    early = [_reshard([g_z, g_xbc, g_dt], MAP_W1, 644, bf, "reshard_w1"), g_wo1.astype(bf).reshape(N_DEV, 256, D_MODEL),
             g_wo0.astype(bf).reshape(N_DEV, 128, D_MODEL),
             (_reshard([g_rnn, g_gate], MAP_G0, 244, bf, "reshard_w0_early", shards=W0_EARLY), W0_EARLY)]
    dq, dk, dvv, recv_early = _flash_bwd(qc, kc, vc, o, do, lse, scatter=early)
    dtail, g_wq, g_wkv, dqnw, dknw, g_tail = _mla_bwd(dq, dk, dvv, proj0, qn, kn, qn_w, kn_w, wq, wkv, tc, tsa, tsb, xb)

    acc = {"g_rnn": g_rnn, "g_gate": g_gate, "g_tail": g_tail, "g_wq": g_wq, "g_wkv": g_wkv,
           "dvec0": dvec0, "g_wa": g_wa, "g_wx": g_wx, "dqnw": dqnw, "dknw": dknw, "dgb0": dgb0, "dvec1": dvec1,
           "dcw1": dcw1, "dnw": dnw, "dgb1": dgb1}
    late = [(_reshard([g_rnn, g_gate, g_tail], MAP_G0, 244, bf, "reshard_w0_late", shards=W0_LATE), W0_LATE),
            _reshard([g_wq], MAP_WQ, 96, bf, "reshard_wq"), _reshard([g_wkv], MAP_WKV, 128, bf, "reshard_wkv")]
    sm_slots, vec_rows, gates = _pack_small(dvec0, g_wa, g_wx, dqnw, dknw, dgb0, dvec1, dcw1, dnw, dgb1, loss8)
    dx, recv_late = _l0_dx(dxr, dgate, dtail, w0p, dv0, scatter=late + [sm_slots], bcast=[vec_rows, gates])
    return acc, recv_early, recv_late, loss8[0, 0], dx
```

```python
import math

import jax
import jax.numpy as jnp
from jax import lax
from jax.experimental import pallas as pl
from jax.experimental.pallas import tpu as pltpu

F32 = jnp.float32
MXU_DTYPE = jnp.bfloat16

N_DEV = 8
SEQ = 4096
D_MODEL = 1024
DN_ALPHA = 4.0 ** 0.25
RNN_W = 512
MLA_HEADS = 8
ATT_SCALE = 96.0 ** -0.5
ATT_C = ATT_SCALE * math.log2(math.e)
RG_C = 8.0
SSD_INNER = 2048
SSD_HEADS = 32
SSD_P = 64
SSD_GROUPS = 4
SSD_N = 128
SSD_L = 128
SSD_CONV = 3072
LANES = 128
SUBLANES = 8
VMEM_LIMIT = 56 * 1024 * 1024

ADAM_LR, ADAM_B1, ADAM_B2, ADAM_EPS, ADAM_WD, ADAM_STEP = 0.001, 0.9, 0.999, 1e-08, 0.01, 10

HIGHEST = lax.Precision.HIGHEST


def _params(sem, limit=VMEM_LIMIT):
    return pltpu.CompilerParams(dimension_semantics=sem, vmem_limit_bytes=limit)


def _dot(a, b):
    return lax.dot_general(a, b, (((1,), (0,)), ((), ())), preferred_element_type=F32)


def _dot_nt(a, b):
    return lax.dot_general(a, b, (((1,), (1,)), ((), ())), preferred_element_type=F32)


def _dot_tn(a, b):
    return lax.dot_general(a, b, (((0,), (0,)), ((), ())), preferred_element_type=F32)


def _dot_hi(a, b):
    return lax.dot_general(a, b, (((1,), (0,)), ((), ())), precision=HIGHEST, preferred_element_type=F32)


def _mx(v):
    return v.astype(MXU_DTYPE)


def _sigmoid(v):
    return 1.0 / (1.0 + jnp.exp(-v))


def _log1p_pos(e):
    poly = e * (1.0 - e * (0.5 - e * (1.0 / 3.0 - e * 0.25)))
    return jnp.where(e < 0.01, poly, jnp.log(1.0 + e))


def _softplus(v):
    return jnp.maximum(v, 0.0) + _log1p_pos(jnp.exp(-jnp.abs(v)))


def _neg_expm1(v):
    poly = -v * (1.0 + v * (0.5 + v * (1.0 / 6.0 + v * (1.0 / 24.0 + v * (1.0 / 120.0)))))
    return jnp.where(jnp.abs(v) < 0.1, poly, 1.0 - jnp.exp(v))


def _silu(v):
    return v * _sigmoid(v)


def _dsilu(v):
    s = _sigmoid(v)
    return s * (1.0 + v * (1.0 - s))


def _shift_down(blk, halo, s):
    if s == 0:
        return blk
    t = blk.shape[0]
    r = pltpu.roll(blk, s, 0)
    hr = pltpu.roll(halo, s, 0)
    row8 = lax.broadcasted_iota(jnp.int32, hr.shape, 0)
    head = jnp.where(row8 < s, hr, r[:SUBLANES])
    return jnp.concatenate([head, r[SUBLANES:]], axis=0) if t > SUBLANES else head


def _shift_up(blk, halo, s):
    if s == 0:
        return blk
    t = blk.shape[0]
    r = pltpu.roll(blk, t - s, 0)
    hr = pltpu.roll(halo, SUBLANES - s, 0)
    row8 = lax.broadcasted_iota(jnp.int32, hr.shape, 0)
    tail = jnp.where(row8 >= SUBLANES - s, hr, r[t - SUBLANES:])
    return jnp.concatenate([r[:t - SUBLANES], tail], axis=0) if t > SUBLANES else tail


def _scan_down(a, u):
    t = a.shape[0]
    row = lax.broadcasted_iota(jnp.int32, a.shape, 0)
    d = 1
    while d < t:
        keep = row >= d
        a_sh = jnp.where(keep, pltpu.roll(a, d, 0), 1.0)
        u_sh = jnp.where(keep, pltpu.roll(u, d, 0), 0.0)
        u = a * u_sh + u
        a = a * a_sh
        d *= 2
    return a, u


def _scan_up(a, u):
    t = a.shape[0]
    row = lax.broadcasted_iota(jnp.int32, a.shape, 0)
    d = 1
    while d < t:
        keep = row < t - d
        a_sh = jnp.where(keep, pltpu.roll(a, t - d, 0), 1.0)
        u_sh = jnp.where(keep, pltpu.roll(u, t - d, 0), 0.0)
        u = a * u_sh + u
        a = a * a_sh
        d *= 2
    return a, u


def _conv4(blk, halo, cw, cb):
    out = cb + blk * cw[3:4]
    for k in range(3):
        out = out + _shift_down(blk, halo, 3 - k) * cw[k:k + 1]
    return out


RG_T = 512
P0_RNN = 2


def _rg_gates(xc, wa, ba, wx, bx, lam):
    xcb = _mx(xc)
    r = _sigmoid(_dot(xcb, wa) + ba)
    ig = _sigmoid(_dot(xcb, wx) + bx)
    sp = _softplus(-lam)
    la = (-RG_C * r) * sp
    a = jnp.exp(la)
    mult = jnp.sqrt(_neg_expm1(2.0 * la))
    return r, ig, sp, a, mult


def _rglru_fwd(proj0, cw8, cb, wa, ba, wx, bx, lam):
    t, w = RG_T, RNN_W
    nb = SEQ // t

    def body(x_ref, halo_ref, cw_ref, cb_ref, wa_ref, ba_ref, wx_ref, bx_ref, lam_ref, xc_ref, h_ref, carry):
        i = pl.program_id(0)

        @pl.when(i == 0)
        def _():
            carry[...] = jnp.zeros_like(carry)

        blk = x_ref[...]
        halo = jnp.where(i > 0, halo_ref[...], 0.0)
        xc = _conv4(blk, halo, cw_ref[...], cb_ref[...])
        _, ig, _, a, mult = _rg_gates(xc, wa_ref[...], ba_ref[...], wx_ref[...], bx_ref[...], lam_ref[...])
        u = mult * (ig * xc)
        big_a, big_u = _scan_down(a, u)
        h = big_a * carry[SUBLANES - 1:SUBLANES, :] + big_u
        carry[...] = h[t - SUBLANES:]
        xc_ref[...] = xc
        h_ref[...] = h

    vec = pl.BlockSpec((1, w), lambda i: (0, 0))
    mat = pl.BlockSpec((w, w), lambda i: (0, 0))
    return pl.pallas_call(
        body, name="rglru_fwd", grid=(nb,),
        in_specs=[pl.BlockSpec((t, w), lambda i: (i, P0_RNN)),
                  pl.BlockSpec((SUBLANES, w), lambda i: (jnp.maximum(i * (t // SUBLANES) - 1, 0), P0_RNN)),
                  pl.BlockSpec((SUBLANES, w), lambda i: (0, 0)), vec, mat, vec, mat, vec, vec],
        out_specs=[pl.BlockSpec((t, w), lambda i: (i, 0)), pl.BlockSpec((t, w), lambda i: (i, 0))],
        out_shape=[jax.ShapeDtypeStruct((SEQ, w), F32), jax.ShapeDtypeStruct((SEQ, w), F32)],
        scratch_shapes=[pltpu.VMEM((SUBLANES, w), F32)],
        compiler_params=_params(("arbitrary",)),
    )(proj0, proj0, cw8, cb, wa, ba, wx, bx, lam)


def _rglru_bwd(dh, xc, h, proj0, cw8, wa, ba, wx, bx, lam, xb):
    t, w = RG_T, RNN_W
    nb = SEQ // t
    tb = t // SUBLANES

    def body(dh_ref, xc_ref, h_ref, hh_ref, x_ref, cw_ref, wa_ref, ba_ref, wx_ref, bx_ref, lam_ref, xb_ref,
             dx_ref, dwa_ref, dwx_ref, dvec_ref, gw_ref, gcarry, dxc_next):
        i = pl.program_id(0)
        rev = nb - 1 - i

        @pl.when(i == 0)
        def _():
            gcarry[...] = jnp.zeros_like(gcarry)
            dxc_next[...] = jnp.zeros_like(dxc_next)
            gw_ref[...] = jnp.zeros_like(gw_ref)
            dwa_ref[...] = jnp.zeros_like(dwa_ref)
            dwx_ref[...] = jnp.zeros_like(dwx_ref)
            dvec_ref[...] = jnp.zeros_like(dvec_ref)

        xc = xc_ref[...]
        wa_v, wx_v = wa_ref[...], wx_ref[...]
        lam_v = lam_ref[...]
        r, ig, sp, a, mult = _rg_gates(xc, wa_v, ba_ref[...], wx_v, bx_ref[...], lam_v)
        dhv = dh_ref[...]
        big_a, big_u = _scan_up(a, a * dhv)
        gg = big_a * gcarry[0:1, :] + big_u
        g = dhv + _shift_up(gg, gcarry[...], 1)
        gcarry[...] = gg[:SUBLANES]
        hhalo = jnp.where(rev > 0, hh_ref[...], 0.0)
        da = g * _shift_down(h_ref[...], hhalo, 1)
        d_mult = g * (ig * xc)
        d_i = g * (mult * xc)
        dxc = g * (mult * ig)
        d_la = da * a - d_mult * (a * a) / mult
        d_r = d_la * (-RG_C * sp)
        d_sp = jnp.sum(d_la * (-RG_C * r), axis=0, keepdims=True)
        d_pa = d_r * r * (1.0 - r)
        d_px = d_i * ig * (1.0 - ig)
        d_pab, d_pxb = _mx(d_pa), _mx(d_px)
        dxc = dxc + _dot_nt(d_pab, wa_v) + _dot_nt(d_pxb, wx_v)
        xcb = _mx(xc)
        dwa_ref[...] += _dot_tn(xcb, d_pab)
        dwx_ref[...] += _dot_tn(xcb, d_pxb)
        dvec_ref[0:1, :] += jnp.sum(d_pa, axis=0, keepdims=True)
        dvec_ref[1:2, :] += jnp.sum(d_px, axis=0, keepdims=True)
        dvec_ref[2:3, :] += d_sp * (-_sigmoid(-lam_v))
        dvec_ref[3:4, :] += jnp.sum(dxc, axis=0, keepdims=True)
        xblk = x_ref[...]
        cw = cw_ref[...]
        dx = dxc * cw[3:4]
        nxt = dxc_next[...]
        dvec_ref[7:8, :] += jnp.sum(dxc * xblk, axis=0, keepdims=True)
        for k in range(3):
            up = _shift_up(dxc, nxt, 3 - k)
            dvec_ref[4 + k:5 + k, :] += jnp.sum(up * xblk, axis=0, keepdims=True)
            dx = dx + up * cw[k:k + 1]
        dxc_next[...] = dxc[:SUBLANES]
        dxb = _mx(dx)
        dx_ref[...] = dxb
        gw_ref[...] += _dot_tn(xb_ref[...], dxb)

    blk = pl.BlockSpec((t, w), lambda i: (nb - 1 - i, 0))
    halo = pl.BlockSpec((SUBLANES, w), lambda i: (jnp.maximum((nb - 1 - i) * tb - 1, 0), 0))
    vec = pl.BlockSpec((1, w), lambda i: (0, 0))
    mat = pl.BlockSpec((w, w), lambda i: (0, 0))
    return pl.pallas_call(
        body, name="rglru_bwd", grid=(nb,),
        in_specs=[blk, blk, blk, halo, pl.BlockSpec((t, w), lambda i: (nb - 1 - i, P0_RNN)),
                  pl.BlockSpec((SUBLANES, w), lambda i: (0, 0)), mat, vec, mat, vec, vec,
                  pl.BlockSpec((t, D_MODEL), lambda i: (nb - 1 - i, 0))],
        out_specs=[blk, mat, mat, pl.BlockSpec((16, w), lambda i: (0, 0)), pl.BlockSpec((D_MODEL, w), lambda i: (0, 0))],
        out_shape=[jax.ShapeDtypeStruct((SEQ, w), MXU_DTYPE), jax.ShapeDtypeStruct((w, w), F32),
                   jax.ShapeDtypeStruct((w, w), F32), jax.ShapeDtypeStruct((16, w), F32),
                   jax.ShapeDtypeStruct((D_MODEL, w), F32)],
        scratch_shapes=[pltpu.VMEM((SUBLANES, w), F32), pltpu.VMEM((SUBLANES, w), F32)],
        compiler_params=_params(("arbitrary",)),
    )(dh, xc, h, h, proj0, cw8, wa, ba, wx, bx, lam, xb)


MLA_T = 512


def _rope(v, c, sa, sb):
    return v * c + pltpu.roll(v, LANES - 16, 1) * sa + pltpu.roll(v, 16, 1) * sb


def _rope_t(dv, c, sa, sb):
    return dv * c + pltpu.roll(dv * sa, 16, 1) + pltpu.roll(dv * sb, LANES - 16, 1)


def _rms(v, g, eps=1e-6):
    rs = lax.rsqrt(jnp.mean(v * v, axis=-1, keepdims=True) + eps)
    return v * rs * g, rs


def _mla_fwd(proj0, q_norm, kv_norm, wq, wkv, tc, tsa, tsb):
    t = MLA_T

    def body(cq_ref, ck_ref, qn_ref, kn_ref, wq_ref, wkv_ref, c_ref, sa_ref, sb_ref,
             oqn_ref, okn_ref, oq_ref, ok_ref, ov_ref):
        c, sa, sb = c_ref[...], sa_ref[...], sb_ref[...]
        ck = ck_ref[...]
        qn = _mx(_rms(cq_ref[...], qn_ref[...])[0])
        kn = _mx(_rms(ck[:, :LANES], kn_ref[...])[0])
        oqn_ref[...] = qn
        okn_ref[...] = kn
        krv = _rope(ck[:, LANES:], c, sa, sb)
        qraw = _dot(qn, wq_ref[...])
        kvraw = _dot(kn, wkv_ref[...])
        for hd in range(MLA_HEADS):
            sl = slice(hd * LANES, (hd + 1) * LANES)
            oq_ref[:, sl] = _mx(_rope(qraw[:, sl], c, sa, sb))
            ok_ref[:, sl] = _mx(kvraw[:, sl] + krv)
        ov_ref[...] = _mx(kvraw[:, 1024:])

    tab = pl.BlockSpec((t, LANES), lambda i: (i, 0))
    wide = pl.BlockSpec((t, 1024), lambda i: (i, 0))
    const = lambda shape: pl.BlockSpec(shape, lambda i: (0, 0))
    return pl.pallas_call(
        body, name="mla_fwd", grid=(SEQ // t,),
        in_specs=[pl.BlockSpec((t, 256), lambda i: (i, 6)), pl.BlockSpec((t, 256), lambda i: (i, 7)),
                  const((1, 256)), const((1, LANES)), const((256, 1024)), const((LANES, 1536)), tab, tab, tab],
        out_specs=[pl.BlockSpec((t, 256), lambda i: (i, 0)), tab, wide, wide, pl.BlockSpec((t, 512), lambda i: (i, 0))],
        out_shape=[jax.ShapeDtypeStruct((SEQ, 256), MXU_DTYPE), jax.ShapeDtypeStruct((SEQ, LANES), MXU_DTYPE),
                   jax.ShapeDtypeStruct((SEQ, 1024), MXU_DTYPE), jax.ShapeDtypeStruct((SEQ, 1024), MXU_DTYPE),
                   jax.ShapeDtypeStruct((SEQ, 512), MXU_DTYPE)],
        compiler_params=_params(("parallel",)),
    )(proj0, proj0, q_norm, kv_norm, wq, wkv, tc, tsa, tsb)


ATT_T = 1024


def _flash_fwd(q, k, v, bcast=()):
    t = ATT_T
    nb = SEQ // t

    steps = [(qi, ki) for qi in range(nb) for ki in range(qi + 1)]
    qi_tab = jnp.asarray([s[0] for s in steps], jnp.int32)
    ki_tab = jnp.asarray([s[1] for s in steps], jnp.int32)

    nx = len(bcast)

    def body(qi_ref, ki_ref, q_ref, k_ref, v_ref, *rest):
        x_refs, (o_ref, lse_ref), g_refs = rest[:nx], rest[nx:nx + 2], rest[nx + 2:2 * nx + 2]
        m_sc, acc_sc = rest[2 * nx + 2:2 * nx + 4]
        step = pl.program_id(1)
        qi, ki = qi_ref[step], ki_ref[step]
        if nx:
            copies = _peer_copies(x_refs, g_refs, rest[2 * nx + 4:], [])

            @pl.when((pl.program_id(0) == 0) & (step == 0))
            def _():
                for cp in copies:
                    cp.start()

        @pl.when(ki == 0)
        def _():
            m_sc[...] = jnp.full_like(m_sc, -jnp.inf)
            acc_sc[...] = jnp.zeros_like(acc_sc)

        def update(diagonal):
            vv = v_ref[...]
            lane_v = lax.broadcasted_iota(jnp.int32, vv.shape, 1)
            for hd in range(2):
                sl = slice(hd * LANES, (hd + 1) * LANES)
                s = _dot_nt(q_ref[:, sl], k_ref[:, sl])
                if diagonal:
                    s = jnp.where(lax.broadcasted_iota(jnp.int32, (t, t), 1)
                                  <= lax.broadcasted_iota(jnp.int32, (t, t), 0), s, -jnp.inf)
                m_prev = m_sc[hd]
                m_new = jnp.maximum(m_prev, jnp.max(s, axis=1, keepdims=True))
                p = jnp.exp2((s - m_new[:, :1]) * ATT_C)
                m_sc[hd] = m_new
                vh = jnp.where((lane_v >= hd * 64) & (lane_v < (hd + 1) * 64), vv, jnp.ones_like(vv))
                acc_sc[hd] = acc_sc[hd] * jnp.exp2((m_prev - m_new) * ATT_C) + _dot(_mx(p), vh)

        @pl.when(ki < qi)
        def _():
            update(False)

        @pl.when(ki == qi)
        def _():
            update(True)
            first = lax.broadcasted_iota(jnp.int32, (t, LANES), 1) < 64
            a0, a1 = acc_sc[0], acc_sc[1]
            l0, l1 = pltpu.roll(a0, 64, 1), pltpu.roll(a1, 64, 1)
            o_ref[...] = jnp.where(first, a0 / l0, a1 / l1)
            lse_ref[0] = jnp.where(first, m_sc[0] * ATT_SCALE + jnp.log(l0), m_sc[1] * ATT_SCALE + jnp.log(l1))

        if nx:
            @pl.when((pl.program_id(0) == 3) & (step == len(steps) - 1))
            def _():
                for cp in copies:
                    cp.wait()

    grid_spec = pltpu.PrefetchScalarGridSpec(
        num_scalar_prefetch=2, grid=(4, len(steps)),
        in_specs=[pl.BlockSpec((t, 256), lambda p, s, qt, kt: (qt[s], p)),
                  pl.BlockSpec((t, 256), lambda p, s, qt, kt: (kt[s], p)),
                  pl.BlockSpec((t, LANES), lambda p, s, qt, kt: (kt[s], p))] + [ANY] * nx,
        out_specs=[pl.BlockSpec((t, LANES), lambda p, s, qt, kt: (qt[s], p)),
                   pl.BlockSpec((1, t, LANES), lambda p, s, qt, kt: (p, qt[s], 0))] + [ANY] * nx,
        scratch_shapes=[pltpu.VMEM((2, t, LANES), F32), pltpu.VMEM((2, t, LANES), F32)]
        + (_exchange_sems(nx) if nx else []))
    res = pl.pallas_call(
        body, name="flash_fwd", grid_spec=grid_spec,
        out_shape=[jax.ShapeDtypeStruct((SEQ, 512), F32), jax.ShapeDtypeStruct((4, SEQ, LANES), F32)]
        + _exchange_shapes([], bcast),
        compiler_params=_params(("arbitrary", "arbitrary")),
    )(qi_tab, ki_tab, q, k, v, *bcast)
    return res[0], res[1], res[2:]


def _flash_bwd(q, k, v, o, do, lse, scatter=()):
    t = ATT_T
    nb = SEQ // t

    steps = [(qi, ki) for ki in range(nb) for qi in range(ki, nb)]
    qi_tab = jnp.asarray([s[0] for s in steps], jnp.int32)
    ki_tab = jnp.asarray([s[1] for s in steps], jnp.int32)
    log2e = math.log2(math.e)

    sc_arrays, sc_ranges = _scatter_args(scatter)
    nx = len(sc_arrays)

    def body(qi_ref, ki_ref, q_ref, k_ref, v_ref, o_ref, do_ref, lse_ref, *rest):
        x_refs, (dq_ref, dk_ref, dv_ref), g_refs = rest[:nx], rest[nx:nx + 3], rest[nx + 3:2 * nx + 3]
        dkt_sc, dvt_sc = rest[2 * nx + 3:2 * nx + 5]
        step = pl.program_id(1)
        qi, ki = qi_ref[step], ki_ref[step]
        if nx:
            copies = _peer_copies(x_refs, g_refs, rest[2 * nx + 5:], sc_ranges)

            @pl.when((pl.program_id(0) == 0) & (step == 0))
            def _():
                for cp in copies:
                    cp.start()

        @pl.when(step == 0)
        def _():
            dq_ref[...] = jnp.zeros_like(dq_ref)

        @pl.when(qi == ki)
        def _():
            dkt_sc[...] = jnp.zeros_like(dkt_sc)
            dvt_sc[...] = jnp.zeros_like(dvt_sc)

        def update(diagonal):
            dov, ov, vv = do_ref[...], o_ref[...], v_ref[...]
            lse2 = lse_ref[0] * log2e
            lane = lax.broadcasted_iota(jnp.int32, (t, LANES), 1)
            row_t = lax.broadcasted_iota(jnp.int32, (LANES, t), 0)
            prod = dov * ov
            do_b = _mx(dov)
            qrows = pl.ds(pl.multiple_of(qi * t, t), t)
            dvt_acc = jnp.zeros((LANES, t), F32)
            dkt_new, dq_new = [], []
            for hd in range(2):
                sl = slice(hd * LANES, (hd + 1) * LANES)
                mine = (lane >= hd * 64) & (lane < (hd + 1) * 64)
                qh, kh = q_ref[:, sl], k_ref[:, sl]
                p = jnp.exp2(_dot_nt(qh, kh) * ATT_C - lse2[:, hd * 64:hd * 64 + 1])
                if diagonal:
                    p = jnp.where(lax.broadcasted_iota(jnp.int32, (t, t), 1)
                                  <= lax.broadcasted_iota(jnp.int32, (t, t), 0), p, 0.0)
                do_h = jnp.where(mine, dov, 0.0)
                delta = jnp.sum(jnp.where(mine, prod, 0.0), axis=1, keepdims=True)
                dp = _dot_nt(_mx(do_h), vv)
                ds = _mx(p * (dp - delta) * ATT_SCALE)
                dvt_acc = dvt_acc + jnp.where((row_t >= hd * 64) & (row_t < (hd + 1) * 64), _dot_tn(do_b, _mx(p)), 0.0)
                dkt_new.append(_dot_tn(qh, ds))
                dq_new.append(_dot(ds, kh))
            for hd in range(2):
                sl = slice(hd * LANES, (hd + 1) * LANES)
                dkt_sc[sl, :] += dkt_new[hd]
                dq_ref[qrows, sl] += dq_new[hd]
            dvt_sc[...] += dvt_acc

        @pl.when(qi > ki)
        def _():
            update(False)

        @pl.when(qi == ki)
        def _():
            update(True)

        @pl.when(qi == nb - 1)
        def _():
            dk_ref[...] = dkt_sc[...].T
            dv_ref[...] = dvt_sc[...].T

        if nx:
            @pl.when((pl.program_id(0) == 3) & (step == len(steps) - 1))
            def _():
                for cp in copies:
                    cp.wait()

    qmap = lambda p, s, qt, kt: (qt[s], p)
    kmap = lambda p, s, qt, kt: (kt[s], p)
    grid_spec = pltpu.PrefetchScalarGridSpec(
        num_scalar_prefetch=2, grid=(4, len(steps)),
        in_specs=[pl.BlockSpec((t, 256), qmap), pl.BlockSpec((t, 256), kmap), pl.BlockSpec((t, LANES), kmap),
                  pl.BlockSpec((t, LANES), qmap), pl.BlockSpec((t, LANES), qmap),
                  pl.BlockSpec((1, t, LANES), lambda p, s, qt, kt: (p, qt[s], 0))] + [ANY] * nx,
        out_specs=[pl.BlockSpec((SEQ, 256), lambda p, s, qt, kt: (0, p)), pl.BlockSpec((t, 256), kmap),
                   pl.BlockSpec((t, LANES), kmap)] + [ANY] * nx,
        scratch_shapes=[pltpu.VMEM((256, t), F32), pltpu.VMEM((LANES, t), F32)] + (_exchange_sems(nx) if nx else []))
    res = pl.pallas_call(
        body, name="flash_bwd", grid_spec=grid_spec,
        out_shape=[jax.ShapeDtypeStruct((SEQ, 1024), F32), jax.ShapeDtypeStruct((SEQ, 1024), F32),
                   jax.ShapeDtypeStruct((SEQ, 512), F32)] + _exchange_shapes(sc_arrays, []),
        compiler_params=_params(("arbitrary", "arbitrary")),
    )(qi_tab, ki_tab, q, k, v, o, do, lse, *sc_arrays)
    return res[0], res[1], res[2], res[3:]


def _rms_bwd(v, g, dy, eps=1e-6):
    rs = lax.rsqrt(jnp.mean(v * v, axis=-1, keepdims=True) + eps)
    xh = v * rs
    dxh = dy * g
    dv = rs * (dxh - xh * jnp.mean(dxh * xh, axis=-1, keepdims=True))
    return dv, jnp.sum(dy * xh, axis=0, keepdims=True)


def _mla_bwd(dq, dk, dv, proj0, qlat, klat, q_norm, kv_norm, wq, wkv, tc, tsa, tsb, xb):
    t = MLA_T

    def body(dq_ref, dk_ref, dv_ref, cq_ref, ck_ref, ql_ref, kl_ref, qn_ref, kn_ref, wq_ref, wkv_ref,
             c_ref, sa_ref, sb_ref, xb_ref, o_ref, gwq_ref, gwkv_ref, dgq_ref, dgk_ref, gwt_ref, oq_ref, okv_ref):
        @pl.when(pl.program_id(0) == 0)
        def _():
            dgq_ref[...] = jnp.zeros_like(dgq_ref)
            dgk_ref[...] = jnp.zeros_like(dgk_ref)
            gwq_ref[...] = jnp.zeros_like(gwq_ref)
            gwkv_ref[...] = jnp.zeros_like(gwkv_ref)
            gwt_ref[...] = jnp.zeros_like(gwt_ref)

        c, sa, sb = c_ref[...], sa_ref[...], sb_ref[...]
        lane = lax.broadcasted_iota(jnp.int32, (t, LANES), 1)
        dkr = jnp.zeros((t, LANES), F32)
        for hd in range(MLA_HEADS):
            sl = slice(hd * LANES, (hd + 1) * LANES)
            oq_ref[:, sl] = _mx(_rope_t(dq_ref[:, sl], c, sa, sb))
            dkh = dk_ref[:, sl]
            okv_ref[:, sl] = _mx(dkh)
            dkr = dkr + dkh
        okv_ref[:, 1024:] = _mx(dv_ref[...])
        dkr = _rope_t(jnp.where((lane >= 64) & (lane < 96), dkr, 0.0), c, sa, sb)
        dqraw, dkvraw = oq_ref[...], okv_ref[...]
        gwq_ref[...] += _dot_tn(ql_ref[...], dqraw)
        gwkv_ref[...] += _dot_tn(kl_ref[...], dkvraw)
        dqn = _dot_nt(dqraw, wq_ref[...])
        dkn = _dot_nt(dkvraw, wkv_ref[...])
        dcq, dgq = _rms_bwd(cq_ref[...], qn_ref[...], dqn)
        dck, dgk = _rms_bwd(ck_ref[:, :LANES], kn_ref[...], dkn)
        o_ref[:, :256] = _mx(dcq)
        o_ref[:, 256:384] = _mx(dck)
        o_ref[:, 384:] = _mx(dkr)
        gwt_ref[...] += _dot_tn(xb_ref[...], o_ref[...])
        dgq_ref[0:1, :] += dgq
        dgk_ref[0:1, :] += dgk

    tab = pl.BlockSpec((t, LANES), lambda i: (i, 0))
    wide = pl.BlockSpec((t, 1024), lambda i: (i, 0))
    const = lambda shape: pl.BlockSpec(shape, lambda i: (0, 0))
    return pl.pallas_call(
        body, name="mla_bwd", grid=(SEQ // t,),
        in_specs=[wide, wide, pl.BlockSpec((t, 512), lambda i: (i, 0)),
                  pl.BlockSpec((t, 256), lambda i: (i, 6)), pl.BlockSpec((t, 256), lambda i: (i, 7)),
                  pl.BlockSpec((t, 256), lambda i: (i, 0)), tab,
                  const((1, 256)), const((1, LANES)), const((256, 1024)), const((LANES, 1536)), tab, tab, tab, wide],
        out_specs=[pl.BlockSpec((t, 512), lambda i: (i, 0)), const((256, 1024)), const((LANES, 1536)),
                   const((SUBLANES, 256)), const((SUBLANES, LANES)), const((D_MODEL, 512))],
        out_shape=[jax.ShapeDtypeStruct((SEQ, 512), MXU_DTYPE), jax.ShapeDtypeStruct((256, 1024), F32),
                   jax.ShapeDtypeStruct((LANES, 1536), F32), jax.ShapeDtypeStruct((SUBLANES, 256), F32),
                   jax.ShapeDtypeStruct((SUBLANES, LANES), F32), jax.ShapeDtypeStruct((D_MODEL, 512), F32)],
        scratch_shapes=[pltpu.VMEM((t, 1024), MXU_DTYPE), pltpu.VMEM((t, 1536), MXU_DTYPE)],
        compiler_params=_params(("arbitrary",)),
    )(dq, dk, dv, proj0, proj0, qlat, klat, q_norm, kv_norm, wq, wkv, tc, tsa, tsb, xb)


LN_T = 512


def _ln(v, g, b, eps=1e-5):
    mu = jnp.mean(v, axis=-1, keepdims=True)
    xc = v - mu
    rs = lax.rsqrt(jnp.mean(xc * xc, axis=-1, keepdims=True) + eps)
    return xc * rs * g + b


def _ln_bwd(v, g, dy, eps=1e-5):
    mu = jnp.mean(v, axis=-1, keepdims=True)
    xc = v - mu
    rs = lax.rsqrt(jnp.mean(xc * xc, axis=-1, keepdims=True) + eps)
    xh = xc * rs
    dxh = dy * g
    dv = rs * (dxh - jnp.mean(dxh, axis=-1, keepdims=True) - xh * jnp.mean(dxh * xh, axis=-1, keepdims=True))
    return dv, jnp.sum(dy * xh, axis=0, keepdims=True), jnp.sum(dy, axis=0, keepdims=True)


def _l0_out(h, o, proj0, x, w_out, g, b):
    t = LN_T

    def body(h_ref, o_ref, ga_ref, gb_ref, x_ref, w_ref, g_ref, b_ref, y_ref, v_ref, x1_ref, x1b_ref):
        y = _mx(jnp.concatenate([h_ref[...] * _silu(ga_ref[...]), o_ref[...] * _silu(gb_ref[...])], axis=1))
        v = DN_ALPHA * x_ref[...] + _dot(y, w_ref[...])
        y_ref[...] = y
        v_ref[...] = v
        x1 = _ln(v, g_ref[...], b_ref[...])
        x1_ref[...] = x1
        x1b_ref[...] = _mx(x1)

    half = pl.BlockSpec((t, 512), lambda i: (i, 0))
    full = pl.BlockSpec((t, D_MODEL), lambda i: (i, 0))
    vec = pl.BlockSpec((1, D_MODEL), lambda i: (0, 0))
    return pl.pallas_call(
        body, name="l0_out", grid=(SEQ // t,),
        in_specs=[half, half, pl.BlockSpec((t, 512), lambda i: (i, 0)), pl.BlockSpec((t, 512), lambda i: (i, 1)), full,
                  pl.BlockSpec((D_MODEL, D_MODEL), lambda i: (0, 0)), vec, vec],
        out_specs=[full, full, full, full],
        out_shape=[jax.ShapeDtypeStruct((SEQ, D_MODEL), MXU_DTYPE), jax.ShapeDtypeStruct((SEQ, D_MODEL), F32),
                   jax.ShapeDtypeStruct((SEQ, D_MODEL), F32), jax.ShapeDtypeStruct((SEQ, D_MODEL), MXU_DTYPE)],
        compiler_params=_params(("parallel",)),
    )(h, o, proj0, proj0, x, w_out, g, b)


def _l1_in(x1b, w1z, w1d):
    t = 1024

    def body(x_ref, wz_ref, wd_ref, z_ref, dt_ref):
        xv = x_ref[...]
        z_ref[...] = _dot(xv, wz_ref[...])
        dt_ref[...] = _dot(xv, wd_ref[...])

    rows = lambda w: pl.BlockSpec((t, w), lambda i: (i, 0))
    const = lambda w: pl.BlockSpec((D_MODEL, w), lambda i: (0, 0))
    return pl.pallas_call(
        body, name="l1_in", grid=(SEQ // t,),
        in_specs=[rows(D_MODEL), const(SSD_INNER), const(LANES)],
        out_specs=[rows(SSD_INNER), rows(LANES)],
        out_shape=[jax.ShapeDtypeStruct((SEQ, SSD_INNER), F32), jax.ShapeDtypeStruct((SEQ, LANES), F32)],
        compiler_params=_params(("parallel",)),
    )(x1b, w1z, w1d)


def _l1_dx_ln(dz, dxbc, ddt, dv1, v0, w1z, w1x, w1d, g):
    t = LN_T

    def body(dz_ref, dx_ref, ddt_ref, dv1_ref, v_ref, wz_ref, wx_ref, wd_ref, g_ref, dv_ref, dgb_ref):
        @pl.when(pl.program_id(0) == 0)
        def _():
            dgb_ref[...] = jnp.zeros_like(dgb_ref)

        dy = (DN_ALPHA * dv1_ref[...] + _dot_nt(dz_ref[...], wz_ref[...]) + _dot_nt(dx_ref[...], wx_ref[...])
              + _dot_nt(_mx(ddt_ref[...]), wd_ref[...]))
        dv, dg, db = _ln_bwd(v_ref[...], g_ref[...], dy)
        dv_ref[...] = dv
        dgb_ref[0:1, :] += dg
        dgb_ref[1:2, :] += db

    rows = lambda w: pl.BlockSpec((t, w), lambda i: (i, 0))
    const = lambda w: pl.BlockSpec((D_MODEL, w), lambda i: (0, 0))
    return pl.pallas_call(
        body, name="l1_dx_ln", grid=(SEQ // t,),
        in_specs=[rows(SSD_INNER), rows(SSD_CONV), rows(LANES), rows(D_MODEL), rows(D_MODEL),
                  const(SSD_INNER), const(SSD_CONV), const(LANES), pl.BlockSpec((1, D_MODEL), lambda i: (0, 0))],
        out_specs=[rows(D_MODEL), pl.BlockSpec((SUBLANES, D_MODEL), lambda i: (0, 0))],
        out_shape=[jax.ShapeDtypeStruct((SEQ, D_MODEL), F32), jax.ShapeDtypeStruct((SUBLANES, D_MODEL), F32)],
        compiler_params=_params(("arbitrary",)),
    )(dz, dxbc, ddt, dv1, v0, w1z, w1x, w1d, g)


def _gate_bwd(dv0, w_out, h, o, proj0, y0, xb):
    t = LN_T

    def body(dv_ref, w_ref, h_ref, o_ref, ga_ref, gb_ref, y0_ref, xb_ref, dh_ref, do_ref, dg_ref, gwo_ref, gwg_ref):
        @pl.when(pl.program_id(0) == 0)
        def _():
            gwo_ref[...] = jnp.zeros_like(gwo_ref)
            gwg_ref[...] = jnp.zeros_like(gwg_ref)

        dvb = _mx(dv_ref[...])
        dy = _dot_nt(dvb, w_ref[...])
        ga, gb, dya, dyb = ga_ref[...], gb_ref[...], dy[:, :512], dy[:, 512:]
        dh_ref[...] = dya * _silu(ga)
        do_ref[...] = dyb * _silu(gb)
        dg_ref[:, :512] = _mx(dya * h_ref[...] * _dsilu(ga))
        dg_ref[:, 512:] = _mx(dyb * o_ref[...] * _dsilu(gb))
        gwo_ref[...] += _dot_tn(y0_ref[...], dvb)
        gwg_ref[...] += _dot_tn(xb_ref[...], dg_ref[...])

    half = pl.BlockSpec((t, 512), lambda i: (i, 0))
    half1 = pl.BlockSpec((t, 512), lambda i: (i, 1))
    full = pl.BlockSpec((t, 1024), lambda i: (i, 0))
    square = pl.BlockSpec((D_MODEL, D_MODEL), lambda i: (0, 0))
    return pl.pallas_call(
        body, name="gate_bwd", grid=(SEQ // t,),
        in_specs=[full, square, half, half, half, half1, full, full],
        out_specs=[half, half, full, square, square],
        out_shape=[jax.ShapeDtypeStruct((SEQ, 512), F32), jax.ShapeDtypeStruct((SEQ, 512), F32),
                   jax.ShapeDtypeStruct((SEQ, 1024), MXU_DTYPE), jax.ShapeDtypeStruct((D_MODEL, D_MODEL), F32),
                   jax.ShapeDtypeStruct((D_MODEL, D_MODEL), F32)],
        compiler_params=_params(("arbitrary",)),
    )(dv0, w_out, h, o, proj0, proj0, y0, xb)


CONV_T = 1024
CONV_CB = 1024


def _ssd_conv_fwd(x1b, w1x, cw8, cb):
    t, cbk = CONV_T, CONV_CB

    def body(x_ref, w_ref, cw_ref, cb_ref, xbc_ref, pre_ref, act_ref, carry):
        xbc = _dot(x_ref[...], w_ref[...])
        halo = jnp.where(pl.program_id(1) > 0, carry[...], 0.0)
        pre = _conv4(xbc, halo, cw_ref[...], cb_ref[...])
        carry[...] = xbc[t - SUBLANES:]
        xbc_ref[...] = xbc
        pre_ref[...] = pre
        act_ref[...] = _silu(pre)

    blk = pl.BlockSpec((t, cbk), lambda j, i: (i, j))
    out = jax.ShapeDtypeStruct((SEQ, SSD_CONV), F32)
    return pl.pallas_call(
        body, name="ssd_conv_fwd", grid=(SSD_CONV // cbk, SEQ // t),
        in_specs=[pl.BlockSpec((t, D_MODEL), lambda j, i: (i, 0)), pl.BlockSpec((D_MODEL, cbk), lambda j, i: (0, j)),
                  pl.BlockSpec((SUBLANES, cbk), lambda j, i: (0, j)), pl.BlockSpec((1, cbk), lambda j, i: (0, j))],
        out_specs=[blk, blk, blk], out_shape=[out, out, out],
        scratch_shapes=[pltpu.VMEM((SUBLANES, cbk), F32)],
        compiler_params=_params(("parallel", "arbitrary")),
    )(x1b, w1x, cw8, cb)


def _ssd_conv_bwd(dact, pre, xbc, cw8, x1b):
    t, cbk = CONV_T, CONV_CB
    tb = t // SUBLANES
    nb = SEQ // t

    def body(da_ref, dan_ref, pre_ref, pren_ref, x_ref, cw_ref, x1_ref, dx_ref, dcw_ref, gw_ref):
        i = pl.program_id(1)

        @pl.when(i == 0)
        def _():
            dcw_ref[...] = jnp.zeros_like(dcw_ref)
            gw_ref[...] = jnp.zeros_like(gw_ref)

        dpre = da_ref[...] * _dsilu(pre_ref[...])
        dpre_next = jnp.where(i < nb - 1, dan_ref[...] * _dsilu(pren_ref[...]), 0.0)
        xblk = x_ref[...]
        cw = cw_ref[...]
        dx = dpre * cw[3:4]
        dcw_ref[3:4, :] += jnp.sum(dpre * xblk, axis=0, keepdims=True)
        for k in range(3):
            up = _shift_up(dpre, dpre_next, 3 - k)
            dcw_ref[k:k + 1, :] += jnp.sum(up * xblk, axis=0, keepdims=True)
            dx = dx + up * cw[k:k + 1]
        dcw_ref[4:5, :] += jnp.sum(dpre, axis=0, keepdims=True)
        dxb = _mx(dx)
        dx_ref[...] = dxb
        gw_ref[...] += _dot_tn(x1_ref[...], dxb)

    blk = pl.BlockSpec((t, cbk), lambda j, i: (i, j))
    nxt = pl.BlockSpec((SUBLANES, cbk), lambda j, i: (jnp.minimum((i + 1) * tb, SEQ // SUBLANES - 1), j))
    acc = pl.BlockSpec((SUBLANES, cbk), lambda j, i: (0, j))
    return pl.pallas_call(
        body, name="ssd_conv_bwd", grid=(SSD_CONV // cbk, nb),
        in_specs=[blk, nxt, blk, nxt, blk, acc, pl.BlockSpec((t, D_MODEL), lambda j, i: (i, 0))],
        out_specs=[blk, acc, pl.BlockSpec((D_MODEL, cbk), lambda j, i: (0, j))],
        out_shape=[jax.ShapeDtypeStruct((SEQ, SSD_CONV), MXU_DTYPE), jax.ShapeDtypeStruct((SUBLANES, SSD_CONV), F32),
                   jax.ShapeDtypeStruct((D_MODEL, SSD_CONV), F32)],
        compiler_params=_params(("parallel", "arbitrary")),
    )(dact, dact, pre, pre, xbc, cw8, x1b)


def _ssd_common(dt_raw, bias, alog, tril, expand_t, xs):
    lane = lax.broadcasted_iota(jnp.int32, dt_raw.shape, 1)
    dt = jnp.where(lane < SSD_HEADS, _softplus(dt_raw + bias), 0.0)
    a_neg = -jnp.exp(alog)
    cs = _dot_hi(tril, dt * a_neg)
    dt_x = _expand_heads(dt, expand_t)
    ecs_x = _expand_heads(jnp.exp(cs), expand_t)
    ds_x = _expand_heads(jnp.exp(cs[SSD_L - 1:SSD_L, :] - cs), expand_t)
    return dt, a_neg, cs, dt_x, None, xs * dt_x, ds_x, ecs_x, ecs_x[SSD_L - 1:SSD_L, :]


def _expand_heads(v, expand_t):
    hi = v.astype(jnp.bfloat16)
    lo = (v - hi.astype(F32)).astype(jnp.bfloat16)
    return _dot_nt(hi, expand_t) + _dot_nt(lo, expand_t)


def _fold_heads(v, expand_t):
    hi = v.astype(jnp.bfloat16)
    lo = (v - hi.astype(F32)).astype(jnp.bfloat16)
    return _dot(hi, expand_t) + _dot(lo, expand_t)


def _ssd_decay(cs, cs_t, hh, causal):
    seg = cs[:, hh:hh + 1] - cs_t[hh:hh + 1, :]
    return jnp.where(causal, jnp.exp(jnp.where(causal, seg, 0.0)), 0.0)


def _ssd_scan_fwd(act, dt_raw, bias, alog, d_x, tril, expand_t):
    nc = SEQ // SSD_L
    gw = SSD_INNER // SSD_GROUPS

    def body(act_ref, dt_ref, bias_ref, alog_ref, dx_ref, tril_ref, et_ref, y_ref, hp_ref, h_sc):
        @pl.when(pl.program_id(0) == 0)
        def _():
            h_sc[...] = jnp.zeros_like(h_sc)

        xs = act_ref[:, :SSD_INNER]
        _, _, cs, _, _, xdt, ds_x, ecs_x, elast = _ssd_common(
            dt_ref[...], bias_ref[...], alog_ref[...], tril_ref[...], et_ref[...], xs)
        cs_t = cs.T
        causal = (lax.broadcasted_iota(jnp.int32, (SSD_L, SSD_L), 0)
                  >= lax.broadcasted_iota(jnp.int32, (SSD_L, SSD_L), 1))
        lane = lax.broadcasted_iota(jnp.int32, (SSD_L, LANES), 1)
        xdt_b = _mx(xdt)
        xds_b = _mx(xdt * ds_x)
        hp_ref[0] = h_sc[...]
        for g in range(SSD_GROUPS):
            gs = slice(g * gw, (g + 1) * gw)
            bg = _mx(act_ref[:, SSD_INNER + g * SSD_N:SSD_INNER + (g + 1) * SSD_N])
            cg = _mx(act_ref[:, SSD_INNER + 512 + g * SSD_N:SSD_INNER + 512 + (g + 1) * SSD_N])
            cb = _dot_nt(cg, bg)
            hprev = h_sc[:, gs]
            yoff = _dot(cg, _mx(hprev)) * ecs_x[:, gs]
            h_sc[:, gs] = hprev * elast[:, gs] + _dot_tn(bg, xds_b[:, gs])
            for pr in range(4):
                ps = slice(g * gw + pr * LANES, g * gw + (pr + 1) * LANES)
                xp = xdt_b[:, ps]
                ydiag = jnp.zeros((SSD_L, LANES), F32)
                for j in range(2):
                    dm = _ssd_decay(cs, cs_t, g * 8 + pr * 2 + j, causal)
                    mine = (lane >= j * 64) & (lane < (j + 1) * 64)
                    ydiag = ydiag + _dot(_mx(cb * dm), jnp.where(mine, xp, jnp.zeros_like(xp)))
                y_ref[:, ps] = ydiag + yoff[:, pr * LANES:(pr + 1) * LANES] + dx_ref[:, ps] * xs[:, ps]

    const = lambda shape: pl.BlockSpec(shape, lambda c: (0, 0))
    return pl.pallas_call(
        body, name="ssd_scan_fwd", grid=(nc,),
        in_specs=[pl.BlockSpec((SSD_L, SSD_CONV), lambda c: (c, 0)), pl.BlockSpec((SSD_L, LANES), lambda c: (c, 0)),
                  const((1, LANES)), const((1, LANES)), const((1, SSD_INNER)), const((SSD_L, SSD_L)),
                  const((SSD_INNER, LANES))],
        out_specs=[pl.BlockSpec((SSD_L, SSD_INNER), lambda c: (c, 0)),
                   pl.BlockSpec((1, SSD_N, SSD_INNER), lambda c: (c, 0, 0))],
        out_shape=[jax.ShapeDtypeStruct((SEQ, SSD_INNER), F32), jax.ShapeDtypeStruct((nc, SSD_N, SSD_INNER), F32)],
        scratch_shapes=[pltpu.VMEM((SSD_N, SSD_INNER), F32)],
        compiler_params=_params(("arbitrary",)),
    )(act, dt_raw, bias, alog, d_x, tril, expand_t)


def _ssd_scan_bwd(dy, act, dt_raw, hprev_all, bias, alog, d_x, tril, expand_t, x1b):
    nc = SEQ // SSD_L
    gw = SSD_INNER // SSD_GROUPS

    def body(dy_ref, act_ref, dt_ref, hp_ref, bias_ref, alog_ref, dx_ref, tril_ref, et_ref, x1_ref,
             dact_ref, ddt_ref, dvec_ref, gdt_ref, dh_sc, dd_sc, dcs_sc, dcst_sc):
        i = pl.program_id(0)

        @pl.when(i == 0)
        def _():
            dh_sc[...] = jnp.zeros_like(dh_sc)
            dd_sc[...] = jnp.zeros_like(dd_sc)
            gdt_ref[...] = jnp.zeros_like(gdt_ref)
            dvec_ref[...] = jnp.zeros_like(dvec_ref)

        xs = act_ref[:, :SSD_INNER]
        dt_raw_v, bias_v = dt_ref[...], bias_ref[...]
        dt, a_neg, cs, dt_x, _, xdt, ds_x, ecs_x, elast = _ssd_common(
            dt_raw_v, bias_v, alog_ref[...], tril_ref[...], et_ref[...], xs)
        cs_t = cs.T
        rowi = lax.broadcasted_iota(jnp.int32, (SSD_L, SSD_L), 0)
        coli = lax.broadcasted_iota(jnp.int32, (SSD_L, SSD_L), 1)
        causal = rowi >= coli
        lane = lax.broadcasted_iota(jnp.int32, (SSD_L, LANES), 1)
        row_g = lax.broadcasted_iota(jnp.int32, (SSD_L, gw), 0)
        dyv = dy_ref[...]
        dd_sc[0:1, :] += jnp.sum(dyv * xs, axis=0, keepdims=True)
        xdt_b = _mx(xdt)
        xds = xdt * ds_x
        xds_b = _mx(xds)
        dy_b = _mx(dyv)
        dye_b = _mx(dyv * ecs_x)
        dcs_sc[...] = jnp.zeros_like(dcs_sc)
        dcst_sc[...] = jnp.zeros_like(dcst_sc)
        dcs_parts = []
        dxdt_parts = []
        for g in range(SSD_GROUPS):
            gs = slice(g * gw, (g + 1) * gw)
            bcol = slice(SSD_INNER + g * SSD_N, SSD_INNER + (g + 1) * SSD_N)
            ccol = slice(SSD_INNER + 512 + g * SSD_N, SSD_INNER + 512 + (g + 1) * SSD_N)
            bg, cg = _mx(act_ref[:, bcol]), _mx(act_ref[:, ccol])
            cb = _dot_nt(cg, bg)
            hp = hp_ref[0, :, gs]
            hp_b = _mx(hp)
            dh = dh_sc[:, gs]
            dh_b = _mx(dh)
            yoff = _dot(cg, hp_b) * ecs_x[:, gs]
            bdh = _dot(bg, dh_b)
            tt = xds[:, gs] * bdh
            last_row = (jnp.sum(tt, axis=0, keepdims=True)
                        + jnp.sum(dh * hp, axis=0, keepdims=True) * elast[:, gs])
            dcs_parts.append(dyv[:, gs] * yoff - tt + jnp.where(row_g == SSD_L - 1, last_row, 0.0))
            dc_g = _dot_nt(dye_b[:, gs], hp_b)
            db_g = _dot_nt(xds_b[:, gs], dh_b)
            dh_sc[:, gs] = _dot_tn(cg, dye_b[:, gs]) + dh * elast[:, gs]
            wsum = jnp.zeros((SSD_L, SSD_L), F32)
            dxdt_g = []
            for pr in range(4):
                ps = slice(g * gw + pr * LANES, g * gw + (pr + 1) * LANES)
                xp, dyp = xdt_b[:, ps], dy_b[:, ps]
                dxp = jnp.zeros((SSD_L, LANES), F32)
                for j in range(2):
                    hh = g * 8 + pr * 2 + j
                    dm = _ssd_decay(cs, cs_t, hh, causal)
                    mine = (lane >= j * 64) & (lane < (j + 1) * 64)
                    dy_h = jnp.where(mine, dyp, jnp.zeros_like(dyp))
                    wd = _dot_nt(dy_h, xp) * dm
                    wsum = wsum + wd
                    gmat = wd * cb
                    dcs_sc[:, hh:hh + 1] = jnp.sum(gmat, axis=1, keepdims=True)
                    dcst_sc[hh:hh + 1, :] = -jnp.sum(gmat, axis=0, keepdims=True)
                    dxp = dxp + _dot_tn(_mx(cb * dm), dy_h)
                dxdt_g.append(dxp)
            dxdt_parts.append(jnp.concatenate(dxdt_g, axis=1) + bdh * ds_x[:, gs])
            ws_b = _mx(wsum)
            dact_ref[:, ccol] = dc_g + _dot(ws_b, bg)
            dact_ref[:, bcol] = db_g + _dot_tn(ws_b, cg)
        dxdt = jnp.concatenate(dxdt_parts, axis=1)
        dcs_x = jnp.concatenate(dcs_parts, axis=1)
        et = et_ref[...]
        dcs_tot = dcs_sc[...] + dcst_sc[...].T + _fold_heads(dcs_x, et)
        da_dt = _dot_hi((coli >= rowi).astype(F32), dcs_tot)
        ddt = da_dt * a_neg + _fold_heads(dxdt * xs, et)
        ddt_raw = ddt * _sigmoid(dt_raw_v + bias_v)
        ddt_ref[...] = ddt_raw
        gdt_ref[...] += _dot_tn(x1_ref[...], _mx(ddt_raw))
        dvec_ref[0:1, :] += jnp.sum(ddt_raw, axis=0, keepdims=True)
        dvec_ref[1:2, :] += jnp.sum(da_dt * dt, axis=0, keepdims=True) * a_neg
        dact_ref[:, :SSD_INNER] = dyv * dx_ref[...] + dxdt * dt_x

        @pl.when(i == nc - 1)
        def _():
            dvec_ref[2:3, :] = _fold_heads(dd_sc[...], et)[0:1, :]

    const = lambda shape: pl.BlockSpec(shape, lambda c: (0, 0))
    rev = lambda c: (nc - 1 - c, 0)
    return pl.pallas_call(
        body, name="ssd_scan_bwd", grid=(nc,),
        in_specs=[pl.BlockSpec((SSD_L, SSD_INNER), rev), pl.BlockSpec((SSD_L, SSD_CONV), rev),
                  pl.BlockSpec((SSD_L, LANES), rev),
                  pl.BlockSpec((1, SSD_N, SSD_INNER), lambda c: (nc - 1 - c, 0, 0)),
                  const((1, LANES)), const((1, LANES)), const((1, SSD_INNER)), const((SSD_L, SSD_L)),
                  const((SSD_INNER, LANES)), pl.BlockSpec((SSD_L, D_MODEL), rev)],
        out_specs=[pl.BlockSpec((SSD_L, SSD_CONV), rev), pl.BlockSpec((SSD_L, LANES), rev), const((SUBLANES, LANES)),
                   const((D_MODEL, LANES))],
        out_shape=[jax.ShapeDtypeStruct((SEQ, SSD_CONV), F32), jax.ShapeDtypeStruct((SEQ, LANES), F32),
                   jax.ShapeDtypeStruct((SUBLANES, LANES), F32), jax.ShapeDtypeStruct((D_MODEL, LANES), F32)],
        scratch_shapes=[pltpu.VMEM((SSD_N, SSD_INNER), F32), pltpu.VMEM((SUBLANES, SSD_INNER), F32),
                        pltpu.VMEM((SSD_L, LANES), F32), pltpu.VMEM((LANES, SSD_L), F32)],
        compiler_params=_params(("arbitrary",)),
    )(dy, act, dt_raw, hprev_all, bias, alog, d_x, tril, expand_t, x1b)


L1_T = 512


def _resident(shape):
    return pl.BlockSpec(shape, lambda i: (0, 0), pipeline_mode=pl.Buffered(1))


def _gated_norm(y, z, nw):
    y2 = y * _silu(z)
    gw = SSD_INNER // SSD_GROUPS
    outs, xhs, rss = [], [], []
    for g in range(SSD_GROUPS):
        gs = slice(g * gw, (g + 1) * gw)
        v = y2[:, gs]
        rs = lax.rsqrt(jnp.mean(v * v, axis=-1, keepdims=True) + 1e-6)
        xhs.append(v * rs)
        rss.append(rs)
        outs.append(v * rs * nw[:, gs])
    return outs, xhs, rss


def _l1_out(y, z, nw, w_out, x1, g, b, target):
    t = L1_T

    def body(y_ref, z_ref, nw_ref, w_ref, x1_ref, g_ref, b_ref, tg_ref, dv_ref, dgb_ref, loss_ref, gw_ref):
        @pl.when(pl.program_id(0) == 0)
        def _():
            dgb_ref[...] = jnp.zeros_like(dgb_ref)
            loss_ref[...] = jnp.zeros_like(loss_ref)
            gw_ref[...] = jnp.zeros_like(gw_ref)

        outs, _, _ = _gated_norm(y_ref[...], z_ref[...], nw_ref[...])
        yn = _mx(jnp.concatenate(outs, axis=1))
        v = DN_ALPHA * x1_ref[...] + _dot(yn, w_ref[...])
        gv = g_ref[...]
        err = _ln(v, gv, b_ref[...]) - tg_ref[...]
        rowsum = jnp.sum(err * err, axis=1, keepdims=True)
        loss_ref[...] += 0.5 * jnp.sum(rowsum, axis=0, keepdims=True) / D_MODEL
        dv, dg, db = _ln_bwd(v, gv, err / D_MODEL)
        dv_ref[...] = dv
        dgb_ref[0:1, :] += dg
        dgb_ref[1:2, :] += db
        gw_ref[...] += _dot_tn(yn, _mx(dv))

    wide = pl.BlockSpec((t, SSD_INNER), lambda i: (i, 0))
    full = pl.BlockSpec((t, D_MODEL), lambda i: (i, 0))
    vec = pl.BlockSpec((1, D_MODEL), lambda i: (0, 0))
    return pl.pallas_call(
        body, name="l1_out", grid=(SEQ // t,),
        in_specs=[wide, wide, pl.BlockSpec((1, SSD_INNER), lambda i: (0, 0)),
                  _resident((SSD_INNER, D_MODEL)), full, vec, vec, full],
        out_specs=[full, pl.BlockSpec((SUBLANES, D_MODEL), lambda i: (0, 0)),
                   pl.BlockSpec((SUBLANES, LANES), lambda i: (0, 0)), _resident((SSD_INNER, D_MODEL))],
        out_shape=[jax.ShapeDtypeStruct((SEQ, D_MODEL), F32), jax.ShapeDtypeStruct((SUBLANES, D_MODEL), F32),
                   jax.ShapeDtypeStruct((SUBLANES, LANES), F32), jax.ShapeDtypeStruct((SSD_INNER, D_MODEL), F32)],
        compiler_params=_params(("arbitrary",)),
    )(y, z, nw, w_out, x1, g, b, target)


def _l1_gate_bwd(dv1, w_out, y, z, nw, x1b):
    t = L1_T
    gw = SSD_INNER // SSD_GROUPS

    def body(dv_ref, w_ref, y_ref, z_ref, nw_ref, x1_ref, dy_ref, dz_ref, dnw_ref, gw_ref):
        @pl.when(pl.program_id(0) == 0)
        def _():
            dnw_ref[...] = jnp.zeros_like(dnw_ref)
            gw_ref[...] = jnp.zeros_like(gw_ref)

        dyn = _dot_nt(_mx(dv_ref[...]), w_ref[...])
        yv, zv, nwv = y_ref[...], z_ref[...], nw_ref[...]
        _, xhs, rss = _gated_norm(yv, zv, nwv)
        sz, dsz = _silu(zv), _dsilu(zv)
        for g in range(SSD_GROUPS):
            gs = slice(g * gw, (g + 1) * gw)
            d_out = dyn[:, gs]
            xh = xhs[g]
            dnw_ref[0:1, gs] += jnp.sum(d_out * xh, axis=0, keepdims=True)
            dxh = d_out * nwv[:, gs]
            dy2 = rss[g] * (dxh - xh * jnp.mean(dxh * xh, axis=-1, keepdims=True))
            dy_ref[:, gs] = dy2 * sz[:, gs]
            dz_ref[:, gs] = _mx(dy2 * yv[:, gs] * dsz[:, gs])
        gw_ref[...] += _dot_tn(x1_ref[...], dz_ref[...])

    wide = pl.BlockSpec((t, SSD_INNER), lambda i: (i, 0))
    return pl.pallas_call(
        body, name="l1_gate_bwd", grid=(SEQ // t,),
        in_specs=[pl.BlockSpec((t, D_MODEL), lambda i: (i, 0)), _resident((SSD_INNER, D_MODEL)),
                  wide, wide, pl.BlockSpec((1, SSD_INNER), lambda i: (0, 0)), pl.BlockSpec((t, D_MODEL), lambda i: (i, 0))],
        out_specs=[wide, wide, pl.BlockSpec((SUBLANES, SSD_INNER), lambda i: (0, 0)),
                   _resident((D_MODEL, SSD_INNER))],
        out_shape=[jax.ShapeDtypeStruct((SEQ, SSD_INNER), F32), jax.ShapeDtypeStruct((SEQ, SSD_INNER), MXU_DTYPE),
                   jax.ShapeDtypeStruct((SUBLANES, SSD_INNER), F32), jax.ShapeDtypeStruct((D_MODEL, SSD_INNER), F32)],
        compiler_params=_params(("arbitrary",)),
    )(dv1, w_out, y, z, nw, x1b)


MESH = pl.DeviceIdType.MESH
ANY = pl.BlockSpec(memory_space=pl.ANY)


def _flip(v, bit):
    return 1 - v if bit else v


def _all_gather(blocks, name):
    n = len(blocks)

    def body(*refs):
        x_refs, out_refs = refs[:n], refs[n:2 * n]
        send_sems, recv_sems, local_sems = refs[2 * n:]
        mx, my, mc = lax.axis_index("x"), lax.axis_index("y"), lax.axis_index("c")
        me, sibling = (mx, my, mc), (mx, my, 1 - mc)
        chips = [(1 - mx, my), (mx, 1 - my), (1 - mx, 1 - my)]

        def copy(a, k, block, to, own=False):
            px, py, pc = block
            slot = out_refs[a].at[4 * px + 2 * py + pc]
            return pltpu.make_async_remote_copy(
                src_ref=x_refs[a] if own else slot, dst_ref=slot,
                send_sem=send_sems.at[7 * a + k], recv_sem=recv_sems.at[7 * a + k], device_id=to, device_id_type=MESH)

        mine = [pltpu.make_async_copy(x_refs[a], out_refs[a].at[4 * mx + 2 * my + mc], local_sems.at[a])
                for a in range(n)]
        first = []
        for a in range(n):
            mine[a].start()
            first.append(copy(a, 0, me, sibling, own=True))
            first += [copy(a, 1 + j, me, (*chip, mc), own=True) for j, chip in enumerate(chips)]
        for cp in first:
            cp.start()
        passed = []
        for j, chip in enumerate(chips):
            for a in range(n):
                copy(a, 1 + j, (*chip, mc), me).wait_recv()
                fwd = copy(a, 4 + j, (*chip, mc), sibling)
                fwd.start()
                passed.append(fwd)
        for a in range(n):
            copy(a, 0, sibling, me).wait_recv()
            for j, chip in enumerate(chips):
                copy(a, 4 + j, (*chip, 1 - mc), me).wait_recv()
        for cp in first + passed:
            cp.wait_send()
        for cp in mine:
            cp.wait()

    return pl.pallas_call(
        body, name=name, in_specs=[ANY] * n, out_specs=[ANY] * n,
        out_shape=[jax.ShapeDtypeStruct((N_DEV,) + b.shape, b.dtype) for b in blocks],
        scratch_shapes=[pltpu.SemaphoreType.DMA((7 * n,)), pltpu.SemaphoreType.DMA((7 * n,)),
                        pltpu.SemaphoreType.DMA((n,))],
    )(*blocks)


def _l0_in(x, w0p, bcast=()):
    n = len(bcast)
    tm, tn = 1024, 1024
    gi, gj = SEQ // tm, 2048 // tn

    def body(x_ref, w_ref, *rest):
        o_ref, xb_ref = rest[n], rest[n + 1]
        i, j = pl.program_id(0), pl.program_id(1)
        if n:
            copies = _peer_copies(rest[:n], rest[n + 2:2 * n + 2], rest[2 * n + 2:], [])

            @pl.when((i == 0) & (j == 0))
            def _():
                for cp in copies:
                    cp.start()

        xb = _mx(x_ref[...])
        xb_ref[...] = xb
        o_ref[...] = _dot(xb, w_ref[...])

        if n:
            @pl.when((i == gi - 1) & (j == gj - 1))
            def _():
                for cp in copies:
                    cp.wait()

    res = pl.pallas_call(
        body, name="l0_in", grid=(gi, gj),
        in_specs=[pl.BlockSpec((tm, D_MODEL), lambda i, j: (i, 0)), pl.BlockSpec((D_MODEL, tn), lambda i, j: (0, j))]
        + [ANY] * n,
        out_specs=[pl.BlockSpec((tm, tn), lambda i, j: (i, j)), pl.BlockSpec((tm, D_MODEL), lambda i, j: (i, 0))]
        + [ANY] * n,
        out_shape=[jax.ShapeDtypeStruct((SEQ, 2048), F32), jax.ShapeDtypeStruct((SEQ, D_MODEL), MXU_DTYPE)]
        + _exchange_shapes([], bcast),
        scratch_shapes=_exchange_sems(n) if n else [],
        compiler_params=_params(("arbitrary", "arbitrary")),
    )(x, w0p, *bcast)
    return res[0], res[1], res[2:]


def _l0_dx(dxr, dgate, dtail, w0p, dv0, scatter=(), bcast=()):
    arrays, ranges = _scatter_args(scatter)
    n = len(arrays) + len(bcast)
    tm = 1024
    steps = SEQ // tm

    def body(dxr_ref, dg_ref, dt_ref, w_ref, dv_ref, *rest):
        o_ref = rest[n]
        i = pl.program_id(0)
        if n:
            copies = _peer_copies(rest[:n], rest[n + 1:2 * n + 1], rest[2 * n + 1:], ranges)

            @pl.when(i == 0)
            def _():
                for cp in copies:
                    cp.start()

        o_ref[...] = (DN_ALPHA * dv_ref[...] + _dot_nt(dg_ref[...], w_ref[:, 0:1024])
                      + _dot_nt(dxr_ref[...], w_ref[:, 1024:1536]) + _dot_nt(dt_ref[...], w_ref[:, 1536:2048]))

        if n:
            @pl.when(i == steps - 1)
            def _():
                for cp in copies:
                    cp.wait()

    rows = lambda w: pl.BlockSpec((tm, w), lambda i: (i, 0))
    res = pl.pallas_call(
        body, name="l0_dx", grid=(steps,),
        in_specs=[rows(512), rows(1024), rows(512), pl.BlockSpec((D_MODEL, 2048), lambda i: (0, 0)), rows(D_MODEL)]
        + [ANY] * n,
        out_specs=[rows(D_MODEL)] + [ANY] * n,
        out_shape=[jax.ShapeDtypeStruct((SEQ, D_MODEL), F32)] + _exchange_shapes(arrays, bcast),
        scratch_shapes=_exchange_sems(n) if n else [],
        compiler_params=_params(("arbitrary",)),
    )(dxr, dgate, dtail, w0p, dv0, *arrays, *bcast)
    return res[0], res[1:]


def _scatter_args(scatter):
    arrays = [s[0] if isinstance(s, tuple) else s for s in scatter]
    ranges = [s[1] if isinstance(s, tuple) else (0, N_DEV) for s in scatter]
    return arrays, ranges


def _exchange_shapes(scatter, bcast):
    return ([jax.ShapeDtypeStruct((N_DEV,) + a.shape[1:], a.dtype) for a in scatter]
            + [jax.ShapeDtypeStruct((N_DEV,) + a.shape, a.dtype) for a in bcast])


def _exchange_sems(n):
    return [pltpu.SemaphoreType.DMA((7 * n,)), pltpu.SemaphoreType.DMA((7 * n,)), pltpu.SemaphoreType.DMA((n,))]


class _GuardedCopy:
    def __init__(self, copy, send=None, recv=None, local=False):
        self.copy, self.send, self.recv, self.local = copy, send, recv, local

    @staticmethod
    def _run(pred, fn):
        if pred is None:
            fn()
        else:
            pl.when(pred)(fn)

    def start(self):
        self._run(self.send, self.copy.start)

    def wait(self):
        if self.local:
            self._run(self.send, self.copy.wait)
        else:
            self._run(self.send, self.copy.wait_send)
            self._run(self.recv, self.copy.wait_recv)


def _peer_copies(in_refs, out_refs, sems, ranges):
    send_sems, recv_sems, local_sems = sems
    n, ns = len(in_refs), len(ranges)
    mx, my, mc = lax.axis_index("x"), lax.axis_index("y"), lax.axis_index("c")
    me = 4 * mx + 2 * my + mc

    def src(a, slot):
        return in_refs[a].at[slot - ranges[a][0]] if a < ns else in_refs[a]

    def member(a, dev):
        if a >= ns or ranges[a] == (0, N_DEV):
            return None
        return (dev >= ranges[a][0]) & (dev < ranges[a][1])

    copies = [_GuardedCopy(pltpu.make_async_copy(src(a, me), out_refs[a].at[me], local_sems.at[a]),
                           send=member(a, me), local=True) for a in range(n)]
    for k in range(1, N_DEV):
        px, py, pc = _flip(mx, (k >> 2) & 1), _flip(my, (k >> 1) & 1), _flip(mc, k & 1)
        peer = 4 * px + 2 * py + pc
        for a in range(n):
            copies.append(_GuardedCopy(pltpu.make_async_remote_copy(
                src_ref=src(a, peer), dst_ref=out_refs[a].at[me],
                send_sem=send_sems.at[7 * a + k - 1], recv_sem=recv_sems.at[7 * a + k - 1],
                device_id=(px, py, pc), device_id_type=MESH), send=member(a, peer), recv=member(a, me)))
    return copies


def _segments(col_map, width):
    segs = []
    for lo, hi, arr, alo in col_map:
        for s in range(N_DEV):
            a, b = max(lo, s * width), min(hi, (s + 1) * width)
            if a < b:
                segs.append((s, a - s * width, b - a, arr, alo + a - lo))
    return segs


COPY_ROWS = 256


def _unshard(g8, col_map, widths, name):
    _, r, w = g8.shape
    rb = min(r, COPY_ROWS)
    segs = _segments(col_map, w)

    def body(g_ref, *o_refs):
        for o_ref in o_refs:
            o_ref[...] = jnp.zeros_like(o_ref)
        for s, llo, n, arr, alo in segs:
            o_refs[arr][:, alo:alo + n] = g_ref[s, :, llo:llo + n]

    return pl.pallas_call(
        body, name=name, grid=(r // rb,),
        in_specs=[pl.BlockSpec((N_DEV, rb, w), lambda i: (0, i, 0))],
        out_specs=[pl.BlockSpec((rb, n), lambda i: (i, 0)) for n in widths],
        out_shape=[jax.ShapeDtypeStruct((r, n), g8.dtype) for n in widths],
        compiler_params=_params(("parallel",)),
    )(g8)


def _reshard(srcs, col_map, w, dtype, name, shards=(0, N_DEV)):
    r = srcs[0].shape[0]
    rb = min(r, COPY_ROWS)
    lo, hi = shards
    segs = [sg for sg in _segments(col_map, w) if lo <= sg[0] < hi]

    def body(*refs):
        o_ref = refs[-1]
        for s, llo, n, arr, alo in segs:
            o_ref[s - lo, :, llo:llo + n] = refs[arr][:, alo:alo + n].astype(dtype)

    return pl.pallas_call(
        body, name=name, grid=(r // rb,),
        in_specs=[pl.BlockSpec((rb, a.shape[1]), lambda i: (i, 0)) for a in srcs],
        out_specs=pl.BlockSpec((hi - lo, rb, w), lambda i: (0, i, 0)),
        out_shape=jax.ShapeDtypeStruct((hi - lo, r, w), dtype),
        compiler_params=_params(("parallel",)),
    )(*srcs)


def _adamw(parts, w, m, v, name):
    r, c = w.shape
    tr = COPY_ROWS if r % COPY_ROWS == 0 else r

    def body(p_ref, w_ref, m_ref, v_ref, g_ref, d_ref, mo_ref, vo_ref):
        g = p_ref[0].astype(F32)
        for s in range(1, N_DEV):
            g = g + p_ref[s].astype(F32)
        g_ref[...] = g
        d_ref[...], mo_ref[...], vo_ref[...] = _adamw_math(g, w_ref[...], m_ref[...], v_ref[...])

    blk = pl.BlockSpec((tr, c), lambda i: (i, 0))
    out = jax.ShapeDtypeStruct((r, c), F32)
    return pl.pallas_call(
        body, name=name, grid=(r // tr,),
        in_specs=[pl.BlockSpec((N_DEV, tr, c), lambda i: (0, i, 0)), blk, blk, blk],
        out_specs=[blk, blk, blk, blk], out_shape=[out, out, out, out],
        compiler_params=_params(("parallel",)),
    )(parts, w, m, v)


def _adamw_math(g, w, m, v):
    mn = ADAM_B1 * m + (1.0 - ADAM_B1) * g
    vn = ADAM_B2 * v + (1.0 - ADAM_B2) * (g * g)
    m_hat = mn / (1.0 - ADAM_B1 ** ADAM_STEP)
    v_hat = vn / (1.0 - ADAM_B2 ** ADAM_STEP)
    return -ADAM_LR * (m_hat / (jnp.sqrt(v_hat) + ADAM_EPS) + ADAM_WD * w), mn, vn


SMALL = (("ab_conv_w", 0, 4, 64), ("ssd_conv_w", 4, 4, 384), ("ssd_conv_b", 8, 1, 384), ("ssd_norm", 9, 1, 256),
         ("ssd_ln_g", 10, 1, 128), ("ssd_ln_b", 11, 1, 128))
VECS = (("ab_conv_b", 512), ("ab_gate_a_b", 512), ("ab_gate_x_b", 512), ("ab_lambda", 512), ("mla_q_norm", 256),
        ("mla_kv_norm", 128), ("ab_ln_g", 1024), ("ab_ln_b", 1024), ("ssd_dt_bias", 32), ("ssd_a_log", 32),
        ("ssd_d", 32))
GATES = ("ab_gate_a_w", "ab_gate_x_w")
SMALL_NAMES = tuple(n for n, *_ in SMALL) + tuple(n for n, _ in VECS) + GATES
VMEM_WHOLE = pl.BlockSpec(memory_space=pltpu.VMEM)


def _view2d(name, a):
    if name in GATES:
        return a.reshape(RNN_W, 64)
    return a[0] if a.ndim == 3 else a


def _unshard_small(g):
    widths = (512, 3072, 3072, 2048, 1024, 1024)

    def body(*refs):
        ins, outs = refs[:6], refs[6:]
        outs[0][...] = jnp.zeros_like(outs[0])
        outs[1][...] = jnp.zeros_like(outs[1])
        for (_, _, nr, c), i_ref, o_ref in zip(SMALL, ins, outs):
            for j in range(N_DEV):
                o_ref[0:nr, j * c:(j + 1) * c] = i_ref[j]

    return pl.pallas_call(
        body, name="unshard_small", in_specs=[VMEM_WHOLE] * 6, out_specs=[VMEM_WHOLE] * 6,
        out_shape=[jax.ShapeDtypeStruct((SUBLANES if nr == 4 else 1, w), F32) for (_, _, nr, _), w in zip(SMALL, widths)],
    )(*g)


def _prep_repl(ga, gx, dt_bias, a_log, d):
    def body(ga_ref, gx_ref, b_ref, al_ref, d_ref, wa_ref, wx_ref, b128_ref, al128_ref, dx_ref):
        wa_ref[...] = jnp.zeros_like(wa_ref)
        wx_ref[...] = jnp.zeros_like(wx_ref)
        for hd in range(8):
            hs = slice(hd * 64, (hd + 1) * 64)
            wa_ref[hs, hs] = _mx(ga_ref[hs, :])
            wx_ref[hs, hs] = _mx(gx_ref[hs, :])
        b128_ref[...] = jnp.zeros_like(b128_ref)
        al128_ref[...] = jnp.zeros_like(al128_ref)
        b128_ref[:, 0:SSD_HEADS] = b_ref[...]
        al128_ref[:, 0:SSD_HEADS] = al_ref[...]
        dv = d_ref[...]
        for hd in range(SSD_HEADS):
            dx_ref[:, hd * SSD_P:(hd + 1) * SSD_P] = jnp.broadcast_to(dv[:, hd:hd + 1], (1, SSD_P))

    return pl.pallas_call(
        body, name="prep_repl", in_specs=[VMEM_WHOLE] * 5, out_specs=[VMEM_WHOLE] * 5,
        out_shape=[jax.ShapeDtypeStruct((RNN_W, RNN_W), MXU_DTYPE), jax.ShapeDtypeStruct((RNN_W, RNN_W), MXU_DTYPE),
                   jax.ShapeDtypeStruct((1, LANES), F32), jax.ShapeDtypeStruct((1, LANES), F32),
                   jax.ShapeDtypeStruct((1, SSD_INNER), F32)],
    )(ga, gx, dt_bias, a_log, d)


LOSS_ROW = 11


def _pack_small(dvec0, g_wa, g_wx, dqnw, dknw, dgb0, dvec1, dcw1, dnw, dgb1, loss8):
    def body(dvec0_ref, gwa_ref, gwx_ref, dqn_ref, dkn_ref, dgb0_ref, dvec1_ref, dcw1_ref, dnw_ref, dgb1_ref,
             loss_ref, sm_ref, vec_ref, gg_ref):
        sm_ref[...] = jnp.zeros_like(sm_ref)
        vec_ref[...] = jnp.zeros_like(vec_ref)
        sharded = ((dvec0_ref, 4), (dcw1_ref, 0), (dcw1_ref, 4), (dnw_ref, 0), (dgb1_ref, 0), (dgb1_ref, 1))
        for (_, r0, nr, c), (src, sr) in zip(SMALL, sharded):
            for j in range(N_DEV):
                sm_ref[j, r0:r0 + nr, 0:c] = src[sr:sr + nr, j * c:(j + 1) * c]
        vectors = ((dvec0_ref, 3), (dvec0_ref, 0), (dvec0_ref, 1), (dvec0_ref, 2), (dqn_ref, 0), (dkn_ref, 0),
                   (dgb0_ref, 0), (dgb0_ref, 1), (dvec1_ref, 0), (dvec1_ref, 1), (dvec1_ref, 2))
        for row, ((_, c), (src, sr)) in enumerate(zip(VECS, vectors)):
            vec_ref[row:row + 1, 0:c] = src[sr:sr + 1, 0:c]
        vec_ref[LOSS_ROW:LOSS_ROW + 1, 0:LANES] = loss_ref[0:1, :]
        for hd in range(8):
            hs = slice(hd * 64, (hd + 1) * 64)
            gg_ref[hs, 0:64] = _mx(gwa_ref[hs, hs])
            gg_ref[hs, 64:128] = _mx(gwx_ref[hs, hs])

    return pl.pallas_call(
        body, name="pack_small", in_specs=[VMEM_WHOLE] * 11, out_specs=[VMEM_WHOLE] * 3,
        out_shape=[jax.ShapeDtypeStruct((N_DEV, 16, 384), F32), jax.ShapeDtypeStruct((16, 1024), F32),
                   jax.ShapeDtypeStruct((RNN_W, LANES), MXU_DTYPE)],
    )(dvec0, g_wa, g_wx, dqnw, dknw, dgb0, dvec1, dcw1, dnw, dgb1, loss8)


def _adamw_small(recv_sm, recv_vec, recv_gg, wmv):
    plan = ([(0, r0, nr, c) for _, r0, nr, c in SMALL] + [(1, row, 1, c) for row, (_, c) in enumerate(VECS)]
            + [(2, 0, RNN_W, 0), (2, 0, RNN_W, 64)])
    n = len(plan)

    def body(*refs):
        recv, ins, outs = refs[:3], refs[3:3 + 3 * n], refs[3 + 3 * n:]
        for i, (src, r0, nr, c) in enumerate(plan):
            cols = slice(c, c + 64) if src == 2 else slice(0, c)
            g = recv[src][0, r0:r0 + nr, cols].astype(F32)
            for s in range(1, N_DEV):
                g = g + recv[src][s, r0:r0 + nr, cols].astype(F32)
            w_ref, m_ref, v_ref = ins[3 * i:3 * i + 3]
            outs[4 * i][...] = g
            outs[4 * i + 1][...], outs[4 * i + 2][...], outs[4 * i + 3][...] = _adamw_math(
                g, w_ref[...], m_ref[...], v_ref[...])
        loss = recv[1][0, LOSS_ROW:LOSS_ROW + 1, 0:LANES]
        for s in range(1, N_DEV):
            loss = loss + recv[1][s, LOSS_ROW:LOSS_ROW + 1, 0:LANES]
        outs[4 * n][...] = loss

    flat = [a for t in wmv for a in t]
    return pl.pallas_call(
        body, name="adamw_small", in_specs=[VMEM_WHOLE] * (3 + 3 * n), out_specs=[VMEM_WHOLE] * (4 * n + 1),
        out_shape=[jax.ShapeDtypeStruct(t[0].shape, F32) for t in wmv for _ in range(4)]
        + [jax.ShapeDtypeStruct((1, LANES), F32)],
    )(recv_sm, recv_vec, recv_gg, *flat)


BIG_L0 = ("ab_w_in", "ab_w_out", "mla_w_uq", "mla_w_ukv")
BIG_L1 = ("ssd_w_in", "ssd_w_out")

MAP_W0 = ((0, 512, 0, 1024), (512, 1536, 0, 0), (1536, 1920, 0, 1536), (1920, 1952, 0, 1984))
MAP_W1 = ((0, 2048, 0, 0), (2048, 5120, 1, 0), (5120, 5152, 2, 0))
MAP_WQ = tuple((96 * hd, 96 * hd + 96, 0, 128 * hd) for hd in range(8))
MAP_WKV = (tuple((128 * hd, 128 * hd + 64, 0, 128 * hd) for hd in range(8))
           + tuple((128 * hd + 64, 128 * hd + 128, 0, 1024 + 64 * hd) for hd in range(8)))
MAP_G0 = ((0, 512, 0, 0), (512, 1536, 1, 0), (1536, 1920, 2, 0), (1920, 1952, 2, 448))
W0_EARLY, W0_LATE = (0, 6), (6, 8)


def kernel(x, positions, ab_w_in, ab_conv_w, ab_conv_b, ab_gate_a_w, ab_gate_a_b, ab_gate_x_w, ab_gate_x_b, ab_lambda, mla_q_norm, mla_kv_norm, mla_w_uq, mla_w_ukv, ab_w_out, ab_ln_g, ab_ln_b, ssd_w_in, ssd_conv_w, ssd_conv_b, ssd_dt_bias, ssd_a_log, ssd_d, ssd_norm, ssd_w_out, ssd_ln_g, ssd_ln_b, loss_target, m_ab_w_in, m_ab_conv_w, m_ab_conv_b, m_ab_gate_a_w, m_ab_gate_a_b, m_ab_gate_x_w, m_ab_gate_x_b, m_ab_lambda, m_mla_q_norm, m_mla_kv_norm, m_mla_w_uq, m_mla_w_ukv, m_ab_w_out, m_ab_ln_g, m_ab_ln_b, m_ssd_w_in, m_ssd_conv_w, m_ssd_conv_b, m_ssd_dt_bias, m_ssd_a_log, m_ssd_d, m_ssd_norm, m_ssd_w_out, m_ssd_ln_g, m_ssd_ln_b, v_ab_w_in, v_ab_conv_w, v_ab_conv_b, v_ab_gate_a_w, v_ab_gate_a_b, v_ab_gate_x_w, v_ab_gate_x_b, v_ab_lambda, v_mla_q_norm, v_mla_kv_norm, v_mla_w_uq, v_mla_w_ukv, v_ab_w_out, v_ab_ln_g, v_ab_ln_b, v_ssd_w_in, v_ssd_conv_w, v_ssd_conv_b, v_ssd_dt_bias, v_ssd_a_log, v_ssd_d, v_ssd_norm, v_ssd_w_out, v_ssd_ln_g, v_ssd_ln_b):
    args = dict(locals())
    bf = MXU_DTYPE
    big = {n: [args[pre + n][0] for pre in ("", "m_", "v_")] for n in BIG_L0 + BIG_L1}
    sml = {n: [_view2d(n, args[pre + n]) for pre in ("", "m_", "v_")] for n in SMALL_NAMES}

    w0_8, cw0_8 = _all_gather([big["ab_w_in"][0].astype(bf), sml["ab_conv_w"][0]], "gather_params")
    p = {"cw0_8": cw0_8, "l0_blocks": [big[n][0].astype(bf) for n in BIG_L0[1:]] + [sml[n][0] for n, *_ in SMALL[1:]]}
    p["w0p"], = _unshard(w0_8, MAP_W0, (2048,), "unshard_w0")
    p["wa"], p["wx"], p["dt_bias"], p["a_log"], p["d_x"] = _prep_repl(
        sml["ab_gate_a_w"][0], sml["ab_gate_x_w"][0], sml["ssd_dt_bias"][0], sml["ssd_a_log"][0], sml["ssd_d"][0])
    for key, n in (("cb0", "ab_conv_b"), ("ba", "ab_gate_a_b"), ("bx", "ab_gate_x_b"), ("lam", "ab_lambda"),
                   ("qn_w", "mla_q_norm"), ("kn_w", "mla_kv_norm"), ("g0", "ab_ln_g"), ("b0", "ab_ln_b")):
        p[key] = sml[n][0]

    _, recv_early, recv, _, grad_x = _local_step(
        x[0], positions[0], loss_target[0], p, [big[n][0].astype(bf) for n in BIG_L1])

    me = 4 * lax.axis_index("x") + 2 * lax.axis_index("y") + lax.axis_index("c")
    parts = {"ssd_w_in": recv_early[0], "ssd_w_out": recv_early[1], "ab_w_out": recv_early[2],
             "ab_w_in": jnp.where(me >= W0_LATE[0], recv[0], recv_early[3]), "mla_w_uq": recv[1], "mla_w_ukv": recv[2]}

    outs = {}
    kinds = ("grad", "delta", "new_m", "new_v")
    for n in BIG_L0 + BIG_L1:
        for kind, res in zip(kinds, _adamw(parts[n], *big[n], "adamw_" + n)):
            outs[kind, n] = res[None]
    res = _adamw_small(*recv[3:], [sml[n] for n in SMALL_NAMES])
    for i, n in enumerate(SMALL_NAMES):
        for k, kind in enumerate(kinds):
            outs[kind, n] = res[4 * i + k].reshape(args[n].shape)

    loss = res[4 * len(SMALL_NAMES)][0, 0]
    order = ["ab_w_in", "ab_conv_w", "ab_conv_b", "ab_gate_a_w", "ab_gate_a_b", "ab_gate_x_w", "ab_gate_x_b",
             "ab_lambda", "mla_q_norm", "mla_kv_norm", "mla_w_uq", "mla_w_ukv", "ab_w_out", "ab_ln_g", "ab_ln_b",
             "ssd_w_in", "ssd_conv_w", "ssd_conv_b", "ssd_dt_bias", "ssd_a_log", "ssd_d", "ssd_norm", "ssd_w_out",
             "ssd_ln_g", "ssd_ln_b"]
    return (loss, grad_x[None], *[outs[kind, n] for kind in ("grad", "delta", "new_m", "new_v") for n in order])


def _local_step(x, pos, target, p, l1_blocks):
    bf = MXU_DTYPE
    inv_freq = 10000.0 ** (-jnp.arange(0, 32, 2, dtype=F32) / 32)
    ang = pos.astype(F32)[:, None] * inv_freq
    cos, sin = jnp.cos(ang), jnp.sin(ang)
    zeros = lambda n: jnp.zeros((SEQ, n), F32)
    tc = jnp.concatenate([jnp.ones((SEQ, 64), F32), cos, cos, zeros(32)], axis=1)
    tsa = jnp.concatenate([zeros(64), -sin, zeros(48)], axis=1)
    tsb = jnp.concatenate([zeros(80), sin, zeros(32)], axis=1)

    w0p, wa, wxg = (p[k] for k in ("w0p", "wa", "wx"))
    cb0, ba, bx, lam = (p[k] for k in ("cb0", "ba", "bx", "lam"))
    qn_w, kn_w, g0, b0 = (p[k] for k in ("qn_w", "kn_w", "g0", "b0"))
    dt_bias, a_log, d_x = (p[k] for k in ("dt_bias", "a_log", "d_x"))
    tril = jnp.tril(jnp.ones((SSD_L, SSD_L), F32))
    expand_t = (jnp.arange(SSD_INNER)[:, None] // SSD_P == jnp.arange(LANES)[None, :]).astype(jnp.bfloat16)

    proj0, xb, l0_8 = _l0_in(x, w0p, bcast=p["l0_blocks"])
    wo0 = l0_8[0].reshape(D_MODEL, D_MODEL)
    wq, = _unshard(l0_8[1], MAP_WQ, (1024,), "unshard_wq")
    wkv, = _unshard(l0_8[2], MAP_WKV, (1536,), "unshard_wkv")
    cw0, cw1, cb1, nw, g1, b1 = _unshard_small([p["cw0_8"]] + list(l0_8[3:]))
    xc, h = _rglru_fwd(proj0, cw0, cb0, wa, ba, wxg, bx, lam)
    qn, kn, qc, kc, vc = _mla_fwd(proj0, qn_w, kn_w, wq, wkv, tc, tsa, tsb)
    o, lse, (w1_8, wo1_8) = _flash_fwd(qc, kc, vc, bcast=l1_blocks)
    w1z, w1x, w1d = _unshard(w1_8, MAP_W1, (2048, 3072, 128), "unshard_w1")
    wo1 = wo1_8.reshape(SSD_INNER, D_MODEL)
    y0, v0, x1, x1b = _l0_out(h, o, proj0, x, wo0, g0, b0)

    z, dt_raw = _l1_in(x1b, w1z, w1d)
    xbc, pre, act = _ssd_conv_fwd(x1b, w1x, cw1, cb1)
    ys, hprev = _ssd_scan_fwd(act, dt_raw, dt_bias, a_log, d_x, tril, expand_t)
    dv1, dgb1, loss8, g_wo1 = _l1_out(ys, z, nw, wo1, x1, g1, b1, target)

    dys, dz, dnw, g_z = _l1_gate_bwd(dv1, wo1, ys, z, nw, x1b)
    dact, ddt_raw, dvec1, g_dt = _ssd_scan_bwd(dys, act, dt_raw, hprev, dt_bias, a_log, d_x, tril, expand_t, x1b)
    dxbc, dcw1, g_xbc = _ssd_conv_bwd(dact, pre, xbc, cw1, x1b)

    dv0, dgb0 = _l1_dx_ln(dz, dxbc, ddt_raw, dv1, v0, w1z, w1x, w1d, g0)
    dh, do, dgate, g_wo0, g_gate = _gate_bwd(dv0, wo0, h, o, proj0, y0, xb)
    dxr, g_wa, g_wx, dvec0, g_rnn = _rglru_bwd(dh, xc, h, proj0, cw0, wa, ba, wxg, bx, lam, xb)
    early = [_reshard([g_z, g_xbc, g_dt], MAP_W1, 644, bf, "reshard_w1"), g_wo1.astype(bf).reshape(N_DEV, 256, D_MODEL),
             g_wo0.astype(bf).reshape(N_DEV, 128, D_MODEL),
             (_reshard([g_rnn, g_gate], MAP_G0, 244, bf, "reshard_w0_early", shards=W0_EARLY), W0_EARLY)]
    dq, dk, dvv, recv_early = _flash_bwd(qc, kc, vc, o, do, lse, scatter=early)
    dtail, g_wq, g_wkv, dqnw, dknw, g_tail = _mla_bwd(dq, dk, dvv, proj0, qn, kn, qn_w, kn_w, wq, wkv, tc, tsa, tsb, xb)

    acc = {"g_rnn": g_rnn, "g_gate": g_gate, "g_tail": g_tail, "g_wq": g_wq, "g_wkv": g_wkv,
           "dvec0": dvec0, "g_wa": g_wa, "g_wx": g_wx, "dqnw": dqnw, "dknw": dknw, "dgb0": dgb0, "dvec1": dvec1,
           "dcw1": dcw1, "dnw": dnw, "dgb1": dgb1}
    late = [(_reshard([g_rnn, g_gate, g_tail], MAP_G0, 244, bf, "reshard_w0_late", shards=W0_LATE), W0_LATE),
            _reshard([g_wq], MAP_WQ, 96, bf, "reshard_wq"), _reshard([g_wkv], MAP_WKV, 128, bf, "reshard_wkv")]
    sm_slots, vec_rows, gates = _pack_small(dvec0, g_wa, g_wx, dqnw, dknw, dgb0, dvec1, dcw1, dnw, dgb1, loss8)
    dx, recv_late = _l0_dx(dxr, dgate, dtail, w0p, dv0, scatter=late + [sm_slots], bcast=[vec_rows, gates])
    return acc, recv_early, recv_late, loss8[0, 0], dx
```

```python
import math

import jax
import jax.numpy as jnp
from jax import lax
from jax.experimental import pallas as pl
from jax.experimental.pallas import tpu as pltpu

F32 = jnp.float32
MXU_DTYPE = jnp.bfloat16

N_DEV = 8
SEQ = 4096
D_MODEL = 1024
DN_ALPHA = 4.0 ** 0.25
RNN_W = 512
MLA_HEADS = 8
ATT_SCALE = 96.0 ** -0.5
ATT_C = ATT_SCALE * math.log2(math.e)
RG_C = 8.0
SSD_INNER = 2048
SSD_HEADS = 32
SSD_P = 64
SSD_GROUPS = 4
SSD_N = 128
SSD_L = 128
SSD_CONV = 3072
LANES = 128
SUBLANES = 8
VMEM_LIMIT = 56 * 1024 * 1024

ADAM_LR, ADAM_B1, ADAM_B2, ADAM_EPS, ADAM_WD, ADAM_STEP = 0.001, 0.9, 0.999, 1e-08, 0.01, 10

HIGHEST = lax.Precision.HIGHEST


def _params(sem, limit=VMEM_LIMIT):
    return pltpu.CompilerParams(dimension_semantics=sem, vmem_limit_bytes=limit)


def _dot(a, b):
    return lax.dot_general(a, b, (((1,), (0,)), ((), ())), preferred_element_type=F32)


def _dot_nt(a, b):
    return lax.dot_general(a, b, (((1,), (1,)), ((), ())), preferred_element_type=F32)


def _dot_tn(a, b):
    return lax.dot_general(a, b, (((0,), (0,)), ((), ())), preferred_element_type=F32)


def _dot_hi(a, b):
    return lax.dot_general(a, b, (((1,), (0,)), ((), ())), precision=HIGHEST, preferred_element_type=F32)


def _mx(v):
    return v.astype(MXU_DTYPE)


def _sigmoid(v):
    return 1.0 / (1.0 + jnp.exp(-v))


def _log1p_pos(e):
    poly = e * (1.0 - e * (0.5 - e * (1.0 / 3.0 - e * 0.25)))
    return jnp.where(e < 0.01, poly, jnp.log(1.0 + e))


def _softplus(v):
    return jnp.maximum(v, 0.0) + _log1p_pos(jnp.exp(-jnp.abs(v)))


def _neg_expm1(v):
    poly = -v * (1.0 + v * (0.5 + v * (1.0 / 6.0 + v * (1.0 / 24.0 + v * (1.0 / 120.0)))))
    return jnp.where(jnp.abs(v) < 0.1, poly, 1.0 - jnp.exp(v))


def _silu(v):
    return v * _sigmoid(v)


def _dsilu(v):
    s = _sigmoid(v)
    return s * (1.0 + v * (1.0 - s))


def _shift_down(blk, halo, s):
    if s == 0:
        return blk
    t = blk.shape[0]
    r = pltpu.roll(blk, s, 0)
    hr = pltpu.roll(halo, s, 0)
    row8 = lax.broadcasted_iota(jnp.int32, hr.shape, 0)
    head = jnp.where(row8 < s, hr, r[:SUBLANES])
    return jnp.concatenate([head, r[SUBLANES:]], axis=0) if t > SUBLANES else head


def _shift_up(blk, halo, s):
    if s == 0:
        return blk
    t = blk.shape[0]
    r = pltpu.roll(blk, t - s, 0)
    hr = pltpu.roll(halo, SUBLANES - s, 0)
    row8 = lax.broadcasted_iota(jnp.int32, hr.shape, 0)
    tail = jnp.where(row8 >= SUBLANES - s, hr, r[t - SUBLANES:])
    return jnp.concatenate([r[:t - SUBLANES], tail], axis=0) if t > SUBLANES else tail


def _scan_down(a, u):
    t = a.shape[0]
    row = lax.broadcasted_iota(jnp.int32, a.shape, 0)
    d = 1
    while d < t:
        keep = row >= d
        a_sh = jnp.where(keep, pltpu.roll(a, d, 0), 1.0)
        u_sh = jnp.where(keep, pltpu.roll(u, d, 0), 0.0)
        u = a * u_sh + u
        a = a * a_sh
        d *= 2
    return a, u


def _scan_up(a, u):
    t = a.shape[0]
    row = lax.broadcasted_iota(jnp.int32, a.shape, 0)
    d = 1
    while d < t:
        keep = row < t - d
        a_sh = jnp.where(keep, pltpu.roll(a, t - d, 0), 1.0)
        u_sh = jnp.where(keep, pltpu.roll(u, t - d, 0), 0.0)
        u = a * u_sh + u
        a = a * a_sh
        d *= 2
    return a, u


def _conv4(blk, halo, cw, cb):
    out = cb + blk * cw[3:4]
    for k in range(3):
        out = out + _shift_down(blk, halo, 3 - k) * cw[k:k + 1]
    return out


RG_T = 512
P0_RNN = 2


def _rg_gates(xc, wa, ba, wx, bx, lam):
    xcb = _mx(xc)
    r = _sigmoid(_dot(xcb, wa) + ba)
    ig = _sigmoid(_dot(xcb, wx) + bx)
    sp = _softplus(-lam)
    la = (-RG_C * r) * sp
    a = jnp.exp(la)
    mult = jnp.sqrt(_neg_expm1(2.0 * la))
    return r, ig, sp, a, mult


def _rglru_fwd(proj0, cw8, cb, wa, ba, wx, bx, lam, bcast=()):
    t, w = RG_T, RNN_W
    nb = SEQ // t
    n = len(bcast)

    def body(x_ref, halo_ref, cw_ref, cb_ref, wa_ref, ba_ref, wx_ref, bx_ref, lam_ref, *rest):
        xc_ref, h_ref, carry = rest[n], rest[n + 1], rest[2 * n + 2]
        i = pl.program_id(0)
        if n:
            copies = _peer_copies(rest[:n], rest[n + 2:2 * n + 2], rest[2 * n + 3:], [])

            @pl.when(i == 0)
            def _():
                for cp in copies:
                    cp.start()

        @pl.when(i == 0)
        def _():
            carry[...] = jnp.zeros_like(carry)

        blk = x_ref[...]
        halo = jnp.where(i > 0, halo_ref[...], 0.0)
        xc = _conv4(blk, halo, cw_ref[...], cb_ref[...])
        _, ig, _, a, mult = _rg_gates(xc, wa_ref[...], ba_ref[...], wx_ref[...], bx_ref[...], lam_ref[...])
        u = mult * (ig * xc)
        big_a, big_u = _scan_down(a, u)
        h = big_a * carry[SUBLANES - 1:SUBLANES, :] + big_u
        carry[...] = h[t - SUBLANES:]
        xc_ref[...] = xc
        h_ref[...] = h

        if n:
            @pl.when(i == nb - 1)
            def _():
                for cp in copies:
                    cp.wait()

    vec = pl.BlockSpec((1, w), lambda i: (0, 0))
    mat = pl.BlockSpec((w, w), lambda i: (0, 0))
    res = pl.pallas_call(
        body, name="rglru_fwd", grid=(nb,),
        in_specs=[pl.BlockSpec((t, w), lambda i: (i, P0_RNN)),
                  pl.BlockSpec((SUBLANES, w), lambda i: (jnp.maximum(i * (t // SUBLANES) - 1, 0), P0_RNN)),
                  pl.BlockSpec((SUBLANES, w), lambda i: (0, 0)), vec, mat, vec, mat, vec, vec] + [ANY] * n,
        out_specs=[pl.BlockSpec((t, w), lambda i: (i, 0)), pl.BlockSpec((t, w), lambda i: (i, 0))] + [ANY] * n,
        out_shape=[jax.ShapeDtypeStruct((SEQ, w), F32), jax.ShapeDtypeStruct((SEQ, w), F32)] + _exchange_shapes([], bcast),
        scratch_shapes=[pltpu.VMEM((SUBLANES, w), F32)] + (_exchange_sems(n) if n else []),
        compiler_params=_params(("arbitrary",)),
    )(proj0, proj0, cw8, cb, wa, ba, wx, bx, lam, *bcast)
    return res[0], res[1], res[2:]


def _rglru_bwd(dh, xc, h, proj0, cw8, wa, ba, wx, bx, lam, xb):
    t, w = RG_T, RNN_W
    nb = SEQ // t
    tb = t // SUBLANES

    def body(dh_ref, xc_ref, h_ref, hh_ref, x_ref, cw_ref, wa_ref, ba_ref, wx_ref, bx_ref, lam_ref, xb_ref,
             dx_ref, dwa_ref, dwx_ref, dvec_ref, gw_ref, gcarry, dxc_next):
        i = pl.program_id(0)
        rev = nb - 1 - i

        @pl.when(i == 0)
        def _():
            gcarry[...] = jnp.zeros_like(gcarry)
            dxc_next[...] = jnp.zeros_like(dxc_next)
            gw_ref[...] = jnp.zeros_like(gw_ref)
            dwa_ref[...] = jnp.zeros_like(dwa_ref)
            dwx_ref[...] = jnp.zeros_like(dwx_ref)
            dvec_ref[...] = jnp.zeros_like(dvec_ref)

        xc = xc_ref[...]
        wa_v, wx_v = wa_ref[...], wx_ref[...]
        lam_v = lam_ref[...]
        r, ig, sp, a, mult = _rg_gates(xc, wa_v, ba_ref[...], wx_v, bx_ref[...], lam_v)
        dhv = dh_ref[...]
        big_a, big_u = _scan_up(a, a * dhv)
        gg = big_a * gcarry[0:1, :] + big_u
        g = dhv + _shift_up(gg, gcarry[...], 1)
        gcarry[...] = gg[:SUBLANES]
        hhalo = jnp.where(rev > 0, hh_ref[...], 0.0)
        da = g * _shift_down(h_ref[...], hhalo, 1)
        d_mult = g * (ig * xc)
        d_i = g * (mult * xc)
        dxc = g * (mult * ig)
        d_la = da * a - d_mult * (a * a) / mult
        d_r = d_la * (-RG_C * sp)
        d_sp = jnp.sum(d_la * (-RG_C * r), axis=0, keepdims=True)
        d_pa = d_r * r * (1.0 - r)
        d_px = d_i * ig * (1.0 - ig)
        d_pab, d_pxb = _mx(d_pa), _mx(d_px)
        dxc = dxc + _dot_nt(d_pab, wa_v) + _dot_nt(d_pxb, wx_v)
        xcb = _mx(xc)
        dwa_ref[...] += _dot_tn(xcb, d_pab)
        dwx_ref[...] += _dot_tn(xcb, d_pxb)
        dvec_ref[0:1, :] += jnp.sum(d_pa, axis=0, keepdims=True)
        dvec_ref[1:2, :] += jnp.sum(d_px, axis=0, keepdims=True)
        dvec_ref[2:3, :] += d_sp * (-_sigmoid(-lam_v))
        dvec_ref[3:4, :] += jnp.sum(dxc, axis=0, keepdims=True)
        xblk = x_ref[...]
        cw = cw_ref[...]
        dx = dxc * cw[3:4]
        nxt = dxc_next[...]
        dvec_ref[7:8, :] += jnp.sum(dxc * xblk, axis=0, keepdims=True)
        for k in range(3):
            up = _shift_up(dxc, nxt, 3 - k)
            dvec_ref[4 + k:5 + k, :] += jnp.sum(up * xblk, axis=0, keepdims=True)
            dx = dx + up * cw[k:k + 1]
        dxc_next[...] = dxc[:SUBLANES]
        dxb = _mx(dx)
        dx_ref[...] = dxb
        gw_ref[...] += _dot_tn(xb_ref[...], dxb)

    blk = pl.BlockSpec((t, w), lambda i: (nb - 1 - i, 0))
    halo = pl.BlockSpec((SUBLANES, w), lambda i: (jnp.maximum((nb - 1 - i) * tb - 1, 0), 0))
    vec = pl.BlockSpec((1, w), lambda i: (0, 0))
    mat = pl.BlockSpec((w, w), lambda i: (0, 0))
    return pl.pallas_call(
        body, name="rglru_bwd", grid=(nb,),
        in_specs=[blk, blk, blk, halo, pl.BlockSpec((t, w), lambda i: (nb - 1 - i, P0_RNN)),
                  pl.BlockSpec((SUBLANES, w), lambda i: (0, 0)), mat, vec, mat, vec, vec,
                  pl.BlockSpec((t, D_MODEL), lambda i: (nb - 1 - i, 0))],
        out_specs=[blk, mat, mat, pl.BlockSpec((16, w), lambda i: (0, 0)), pl.BlockSpec((D_MODEL, w), lambda i: (0, 0))],
        out_shape=[jax.ShapeDtypeStruct((SEQ, w), MXU_DTYPE), jax.ShapeDtypeStruct((w, w), F32),
                   jax.ShapeDtypeStruct((w, w), F32), jax.ShapeDtypeStruct((16, w), F32),
                   jax.ShapeDtypeStruct((D_MODEL, w), F32)],
        scratch_shapes=[pltpu.VMEM((SUBLANES, w), F32), pltpu.VMEM((SUBLANES, w), F32)],
        compiler_params=_params(("arbitrary",)),
    )(dh, xc, h, h, proj0, cw8, wa, ba, wx, bx, lam, xb)


MLA_T = 512


def _rope(v, c, sa, sb):
    return v * c + pltpu.roll(v, LANES - 16, 1) * sa + pltpu.roll(v, 16, 1) * sb


def _rope_t(dv, c, sa, sb):
    return dv * c + pltpu.roll(dv * sa, 16, 1) + pltpu.roll(dv * sb, LANES - 16, 1)


def _rms(v, g, eps=1e-6):
    rs = lax.rsqrt(jnp.mean(v * v, axis=-1, keepdims=True) + eps)
    return v * rs * g, rs


def _mla_fwd(proj0, q_norm, kv_norm, wq, wkv, tc, tsa, tsb):
    t = MLA_T

    def body(cq_ref, ck_ref, qn_ref, kn_ref, wq_ref, wkv_ref, c_ref, sa_ref, sb_ref,
             oqn_ref, okn_ref, oq_ref, ok_ref, ov_ref):
        c, sa, sb = c_ref[...], sa_ref[...], sb_ref[...]
        ck = ck_ref[...]
        qn = _mx(_rms(cq_ref[...], qn_ref[...])[0])
        kn = _mx(_rms(ck[:, :LANES], kn_ref[...])[0])
        oqn_ref[...] = qn
        okn_ref[...] = kn
        krv = _rope(ck[:, LANES:], c, sa, sb)
        qraw = _dot(qn, wq_ref[...])
        kvraw = _dot(kn, wkv_ref[...])
        for hd in range(MLA_HEADS):
            sl = slice(hd * LANES, (hd + 1) * LANES)
            oq_ref[:, sl] = _mx(_rope(qraw[:, sl], c, sa, sb))
            ok_ref[:, sl] = _mx(kvraw[:, sl] + krv)
        ov_ref[...] = _mx(kvraw[:, 1024:])

    tab = pl.BlockSpec((t, LANES), lambda i: (i, 0))
    wide = pl.BlockSpec((t, 1024), lambda i: (i, 0))
    const = lambda shape: pl.BlockSpec(shape, lambda i: (0, 0))
    return pl.pallas_call(
        body, name="mla_fwd", grid=(SEQ // t,),
        in_specs=[pl.BlockSpec((t, 256), lambda i: (i, 6)), pl.BlockSpec((t, 256), lambda i: (i, 7)),
                  const((1, 256)), const((1, LANES)), const((256, 1024)), const((LANES, 1536)), tab, tab, tab],
        out_specs=[pl.BlockSpec((t, 256), lambda i: (i, 0)), tab, wide, wide, pl.BlockSpec((t, 512), lambda i: (i, 0))],
        out_shape=[jax.ShapeDtypeStruct((SEQ, 256), MXU_DTYPE), jax.ShapeDtypeStruct((SEQ, LANES), MXU_DTYPE),
                   jax.ShapeDtypeStruct((SEQ, 1024), MXU_DTYPE), jax.ShapeDtypeStruct((SEQ, 1024), MXU_DTYPE),
                   jax.ShapeDtypeStruct((SEQ, 512), MXU_DTYPE)],
        compiler_params=_params(("parallel",)),
    )(proj0, proj0, q_norm, kv_norm, wq, wkv, tc, tsa, tsb)


ATT_T = 1024


def _flash_fwd(q, k, v, bcast=()):
    t = ATT_T
    nb = SEQ // t

    steps = [(qi, ki) for qi in range(nb) for ki in range(qi + 1)]
    qi_tab = jnp.asarray([s[0] for s in steps], jnp.int32)
    ki_tab = jnp.asarray([s[1] for s in steps], jnp.int32)

    nx = len(bcast)

    def body(qi_ref, ki_ref, q_ref, k_ref, v_ref, *rest):
        x_refs, (o_ref, lse_ref), g_refs = rest[:nx], rest[nx:nx + 2], rest[nx + 2:2 * nx + 2]
        m_sc, acc_sc = rest[2 * nx + 2:2 * nx + 4]
        step = pl.program_id(1)
        qi, ki = qi_ref[step], ki_ref[step]
        if nx:
            copies = _peer_copies(x_refs, g_refs, rest[2 * nx + 4:], [])

            @pl.when((pl.program_id(0) == 0) & (step == 0))
            def _():
                for cp in copies:
                    cp.start()

        @pl.when(ki == 0)
        def _():
            m_sc[...] = jnp.full_like(m_sc, -jnp.inf)
            acc_sc[...] = jnp.zeros_like(acc_sc)

        def update(diagonal):
            vv = v_ref[...]
            lane_v = lax.broadcasted_iota(jnp.int32, vv.shape, 1)
            for hd in range(2):
                sl = slice(hd * LANES, (hd + 1) * LANES)
                s = _dot_nt(q_ref[:, sl], k_ref[:, sl])
                if diagonal:
                    s = jnp.where(lax.broadcasted_iota(jnp.int32, (t, t), 1)
                                  <= lax.broadcasted_iota(jnp.int32, (t, t), 0), s, -jnp.inf)
                m_prev = m_sc[hd]
                m_new = jnp.maximum(m_prev, jnp.max(s, axis=1, keepdims=True))
                p = jnp.exp2((s - m_new[:, :1]) * ATT_C)
                m_sc[hd] = m_new
                vh = jnp.where((lane_v >= hd * 64) & (lane_v < (hd + 1) * 64), vv, jnp.ones_like(vv))
                acc_sc[hd] = acc_sc[hd] * jnp.exp2((m_prev - m_new) * ATT_C) + _dot(_mx(p), vh)

        @pl.when(ki < qi)
        def _():
            update(False)

        @pl.when(ki == qi)
        def _():
            update(True)
            first = lax.broadcasted_iota(jnp.int32, (t, LANES), 1) < 64
            a0, a1 = acc_sc[0], acc_sc[1]
            l0, l1 = pltpu.roll(a0, 64, 1), pltpu.roll(a1, 64, 1)
            o_ref[...] = jnp.where(first, a0 / l0, a1 / l1)
            lse_ref[0] = jnp.where(first, m_sc[0] * ATT_SCALE + jnp.log(l0), m_sc[1] * ATT_SCALE + jnp.log(l1))

        if nx:
            @pl.when((pl.program_id(0) == 3) & (step == len(steps) - 1))
            def _():
                for cp in copies:
                    cp.wait()

    grid_spec = pltpu.PrefetchScalarGridSpec(
        num_scalar_prefetch=2, grid=(4, len(steps)),
        in_specs=[pl.BlockSpec((t, 256), lambda p, s, qt, kt: (qt[s], p)),
                  pl.BlockSpec((t, 256), lambda p, s, qt, kt: (kt[s], p)),
                  pl.BlockSpec((t, LANES), lambda p, s, qt, kt: (kt[s], p))] + [ANY] * nx,
        out_specs=[pl.BlockSpec((t, LANES), lambda p, s, qt, kt: (qt[s], p)),
                   pl.BlockSpec((1, t, LANES), lambda p, s, qt, kt: (p, qt[s], 0))] + [ANY] * nx,
        scratch_shapes=[pltpu.VMEM((2, t, LANES), F32), pltpu.VMEM((2, t, LANES), F32)]
        + (_exchange_sems(nx) if nx else []))
    res = pl.pallas_call(
        body, name="flash_fwd", grid_spec=grid_spec,
        out_shape=[jax.ShapeDtypeStruct((SEQ, 512), F32), jax.ShapeDtypeStruct((4, SEQ, LANES), F32)]
        + _exchange_shapes([], bcast),
        compiler_params=_params(("arbitrary", "arbitrary")),
    )(qi_tab, ki_tab, q, k, v, *bcast)
    return res[0], res[1], res[2:]


def _flash_bwd(q, k, v, o, do, lse, scatter=()):
    t = ATT_T
    nb = SEQ // t

    steps = [(qi, ki) for ki in range(nb) for qi in range(ki, nb)]
    qi_tab = jnp.asarray([s[0] for s in steps], jnp.int32)
    ki_tab = jnp.asarray([s[1] for s in steps], jnp.int32)
    log2e = math.log2(math.e)

    sc_arrays, sc_ranges = _scatter_args(scatter)
    nx = len(sc_arrays)

    def body(qi_ref, ki_ref, q_ref, k_ref, v_ref, o_ref, do_ref, lse_ref, *rest):
        x_refs, (dq_ref, dk_ref, dv_ref), g_refs = rest[:nx], rest[nx:nx + 3], rest[nx + 3:2 * nx + 3]
        dkt_sc, dvt_sc = rest[2 * nx + 3:2 * nx + 5]
        step = pl.program_id(1)
        qi, ki = qi_ref[step], ki_ref[step]
        if nx:
            copies = _peer_copies(x_refs, g_refs, rest[2 * nx + 5:], sc_ranges)

            @pl.when((pl.program_id(0) == 0) & (step == 0))
            def _():
                for cp in copies:
                    cp.start()

        @pl.when(step == 0)
        def _():
            dq_ref[...] = jnp.zeros_like(dq_ref)

        @pl.when(qi == ki)
        def _():
            dkt_sc[...] = jnp.zeros_like(dkt_sc)
            dvt_sc[...] = jnp.zeros_like(dvt_sc)

        def update(diagonal):
            dov, ov, vv = do_ref[...], o_ref[...], v_ref[...]
            lse2 = lse_ref[0] * log2e
            lane = lax.broadcasted_iota(jnp.int32, (t, LANES), 1)
            row_t = lax.broadcasted_iota(jnp.int32, (LANES, t), 0)
            prod = dov * ov
            do_b = _mx(dov)
            qrows = pl.ds(pl.multiple_of(qi * t, t), t)
            dvt_acc = jnp.zeros((LANES, t), F32)
            dkt_new, dq_new = [], []
            for hd in range(2):
                sl = slice(hd * LANES, (hd + 1) * LANES)
                mine = (lane >= hd * 64) & (lane < (hd + 1) * 64)
                qh, kh = q_ref[:, sl], k_ref[:, sl]
                p = jnp.exp2(_dot_nt(qh, kh) * ATT_C - lse2[:, hd * 64:hd * 64 + 1])
                if diagonal:
                    p = jnp.where(lax.broadcasted_iota(jnp.int32, (t, t), 1)
                                  <= lax.broadcasted_iota(jnp.int32, (t, t), 0), p, 0.0)
                do_h = jnp.where(mine, dov, 0.0)
                delta = jnp.sum(jnp.where(mine, prod, 0.0), axis=1, keepdims=True)
                dp = _dot_nt(_mx(do_h), vv)
                ds = _mx(p * (dp - delta) * ATT_SCALE)
                dvt_acc = dvt_acc + jnp.where((row_t >= hd * 64) & (row_t < (hd + 1) * 64), _dot_tn(do_b, _mx(p)), 0.0)
                dkt_new.append(_dot_tn(qh, ds))
                dq_new.append(_dot(ds, kh))
            for hd in range(2):
                sl = slice(hd * LANES, (hd + 1) * LANES)
                dkt_sc[sl, :] += dkt_new[hd]
                dq_ref[qrows, sl] += dq_new[hd]
            dvt_sc[...] += dvt_acc

        @pl.when(qi > ki)
        def _():
            update(False)

        @pl.when(qi == ki)
        def _():
            update(True)

        @pl.when(qi == nb - 1)
        def _():
            dk_ref[...] = dkt_sc[...].T
            dv_ref[...] = dvt_sc[...].T

        if nx:
            @pl.when((pl.program_id(0) == 3) & (step == len(steps) - 1))
            def _():
                for cp in copies:
                    cp.wait()

    qmap = lambda p, s, qt, kt: (qt[s], p)
    kmap = lambda p, s, qt, kt: (kt[s], p)
    grid_spec = pltpu.PrefetchScalarGridSpec(
        num_scalar_prefetch=2, grid=(4, len(steps)),
        in_specs=[pl.BlockSpec((t, 256), qmap), pl.BlockSpec((t, 256), kmap), pl.BlockSpec((t, LANES), kmap),
                  pl.BlockSpec((t, LANES), qmap), pl.BlockSpec((t, LANES), qmap),
                  pl.BlockSpec((1, t, LANES), lambda p, s, qt, kt: (p, qt[s], 0))] + [ANY] * nx,
        out_specs=[pl.BlockSpec((SEQ, 256), lambda p, s, qt, kt: (0, p)), pl.BlockSpec((t, 256), kmap),
                   pl.BlockSpec((t, LANES), kmap)] + [ANY] * nx,
        scratch_shapes=[pltpu.VMEM((256, t), F32), pltpu.VMEM((LANES, t), F32)] + (_exchange_sems(nx) if nx else []))
    res = pl.pallas_call(
        body, name="flash_bwd", grid_spec=grid_spec,
        out_shape=[jax.ShapeDtypeStruct((SEQ, 1024), F32), jax.ShapeDtypeStruct((SEQ, 1024), F32),
                   jax.ShapeDtypeStruct((SEQ, 512), F32)] + _exchange_shapes(sc_arrays, []),
        compiler_params=_params(("arbitrary", "arbitrary")),
    )(qi_tab, ki_tab, q, k, v, o, do, lse, *sc_arrays)
    return res[0], res[1], res[2], res[3:]


def _rms_bwd(v, g, dy, eps=1e-6):
    rs = lax.rsqrt(jnp.mean(v * v, axis=-1, keepdims=True) + eps)
    xh = v * rs
    dxh = dy * g
    dv = rs * (dxh - xh * jnp.mean(dxh * xh, axis=-1, keepdims=True))
    return dv, jnp.sum(dy * xh, axis=0, keepdims=True)


def _mla_bwd(dq, dk, dv, proj0, qlat, klat, q_norm, kv_norm, wq, wkv, tc, tsa, tsb, xb):
    t = MLA_T

    def body(dq_ref, dk_ref, dv_ref, cq_ref, ck_ref, ql_ref, kl_ref, qn_ref, kn_ref, wq_ref, wkv_ref,
             c_ref, sa_ref, sb_ref, xb_ref, o_ref, gwq_ref, gwkv_ref, dgq_ref, dgk_ref, gwt_ref, oq_ref, okv_ref):
        @pl.when(pl.program_id(0) == 0)
        def _():
            dgq_ref[...] = jnp.zeros_like(dgq_ref)
            dgk_ref[...] = jnp.zeros_like(dgk_ref)
            gwq_ref[...] = jnp.zeros_like(gwq_ref)
            gwkv_ref[...] = jnp.zeros_like(gwkv_ref)
            gwt_ref[...] = jnp.zeros_like(gwt_ref)

        c, sa, sb = c_ref[...], sa_ref[...], sb_ref[...]
        lane = lax.broadcasted_iota(jnp.int32, (t, LANES), 1)
        dkr = jnp.zeros((t, LANES), F32)
        for hd in range(MLA_HEADS):
            sl = slice(hd * LANES, (hd + 1) * LANES)
            oq_ref[:, sl] = _mx(_rope_t(dq_ref[:, sl], c, sa, sb))
            dkh = dk_ref[:, sl]
            okv_ref[:, sl] = _mx(dkh)
            dkr = dkr + dkh
        okv_ref[:, 1024:] = _mx(dv_ref[...])
        dkr = _rope_t(jnp.where((lane >= 64) & (lane < 96), dkr, 0.0), c, sa, sb)
        dqraw, dkvraw = oq_ref[...], okv_ref[...]
        gwq_ref[...] += _dot_tn(ql_ref[...], dqraw)
        gwkv_ref[...] += _dot_tn(kl_ref[...], dkvraw)
        dqn = _dot_nt(dqraw, wq_ref[...])
        dkn = _dot_nt(dkvraw, wkv_ref[...])
        dcq, dgq = _rms_bwd(cq_ref[...], qn_ref[...], dqn)
        dck, dgk = _rms_bwd(ck_ref[:, :LANES], kn_ref[...], dkn)
        o_ref[:, :256] = _mx(dcq)
        o_ref[:, 256:384] = _mx(dck)
        o_ref[:, 384:] = _mx(dkr)
        gwt_ref[...] += _dot_tn(xb_ref[...], o_ref[...])
        dgq_ref[0:1, :] += dgq
        dgk_ref[0:1, :] += dgk

    tab = pl.BlockSpec((t, LANES), lambda i: (i, 0))
    wide = pl.BlockSpec((t, 1024), lambda i: (i, 0))
    const = lambda shape: pl.BlockSpec(shape, lambda i: (0, 0))
    return pl.pallas_call(
        body, name="mla_bwd", grid=(SEQ // t,),
        in_specs=[wide, wide, pl.BlockSpec((t, 512), lambda i: (i, 0)),
                  pl.BlockSpec((t, 256), lambda i: (i, 6)), pl.BlockSpec((t, 256), lambda i: (i, 7)),
                  pl.BlockSpec((t, 256), lambda i: (i, 0)), tab,
                  const((1, 256)), const((1, LANES)), const((256, 1024)), const((LANES, 1536)), tab, tab, tab, wide],
        out_specs=[pl.BlockSpec((t, 512), lambda i: (i, 0)), const((256, 1024)), const((LANES, 1536)),
                   const((SUBLANES, 256)), const((SUBLANES, LANES)), const((D_MODEL, 512))],
        out_shape=[jax.ShapeDtypeStruct((SEQ, 512), MXU_DTYPE), jax.ShapeDtypeStruct((256, 1024), F32),
                   jax.ShapeDtypeStruct((LANES, 1536), F32), jax.ShapeDtypeStruct((SUBLANES, 256), F32),
                   jax.ShapeDtypeStruct((SUBLANES, LANES), F32), jax.ShapeDtypeStruct((D_MODEL, 512), F32)],
        scratch_shapes=[pltpu.VMEM((t, 1024), MXU_DTYPE), pltpu.VMEM((t, 1536), MXU_DTYPE)],
        compiler_params=_params(("arbitrary",)),
    )(dq, dk, dv, proj0, proj0, qlat, klat, q_norm, kv_norm, wq, wkv, tc, tsa, tsb, xb)


LN_T = 512


def _ln(v, g, b, eps=1e-5):
    mu = jnp.mean(v, axis=-1, keepdims=True)
    xc = v - mu
    rs = lax.rsqrt(jnp.mean(xc * xc, axis=-1, keepdims=True) + eps)
    return xc * rs * g + b


def _ln_bwd(v, g, dy, eps=1e-5):
    mu = jnp.mean(v, axis=-1, keepdims=True)
    xc = v - mu
    rs = lax.rsqrt(jnp.mean(xc * xc, axis=-1, keepdims=True) + eps)
    xh = xc * rs
    dxh = dy * g
    dv = rs * (dxh - jnp.mean(dxh, axis=-1, keepdims=True) - xh * jnp.mean(dxh * xh, axis=-1, keepdims=True))
    return dv, jnp.sum(dy * xh, axis=0, keepdims=True), jnp.sum(dy, axis=0, keepdims=True)


def _l0_out(h, o, proj0, x, w_out, g, b):
    t = LN_T

    def body(h_ref, o_ref, ga_ref, gb_ref, x_ref, w_ref, g_ref, b_ref, y_ref, v_ref, x1_ref, x1b_ref):
        y = _mx(jnp.concatenate([h_ref[...] * _silu(ga_ref[...]), o_ref[...] * _silu(gb_ref[...])], axis=1))
        v = DN_ALPHA * x_ref[...] + _dot(y, w_ref[...])
        y_ref[...] = y
        v_ref[...] = v
        x1 = _ln(v, g_ref[...], b_ref[...])
        x1_ref[...] = x1
        x1b_ref[...] = _mx(x1)

    half = pl.BlockSpec((t, 512), lambda i: (i, 0))
    full = pl.BlockSpec((t, D_MODEL), lambda i: (i, 0))
    vec = pl.BlockSpec((1, D_MODEL), lambda i: (0, 0))
    return pl.pallas_call(
        body, name="l0_out", grid=(SEQ // t,),
        in_specs=[half, half, pl.BlockSpec((t, 512), lambda i: (i, 0)), pl.BlockSpec((t, 512), lambda i: (i, 1)), full,
                  pl.BlockSpec((D_MODEL, D_MODEL), lambda i: (0, 0)), vec, vec],
        out_specs=[full, full, full, full],
        out_shape=[jax.ShapeDtypeStruct((SEQ, D_MODEL), MXU_DTYPE), jax.ShapeDtypeStruct((SEQ, D_MODEL), F32),
                   jax.ShapeDtypeStruct((SEQ, D_MODEL), F32), jax.ShapeDtypeStruct((SEQ, D_MODEL), MXU_DTYPE)],
        compiler_params=_params(("parallel",)),
    )(h, o, proj0, proj0, x, w_out, g, b)


def _l1_in(x1b, w1z, w1d):
    t = 1024

    def body(x_ref, wz_ref, wd_ref, z_ref, dt_ref):
        xv = x_ref[...]
        z_ref[...] = _dot(xv, wz_ref[...])
        dt_ref[...] = _dot(xv, wd_ref[...])

    rows = lambda w: pl.BlockSpec((t, w), lambda i: (i, 0))
    const = lambda w: pl.BlockSpec((D_MODEL, w), lambda i: (0, 0))
    return pl.pallas_call(
        body, name="l1_in", grid=(SEQ // t,),
        in_specs=[rows(D_MODEL), const(SSD_INNER), const(LANES)],
        out_specs=[rows(SSD_INNER), rows(LANES)],
        out_shape=[jax.ShapeDtypeStruct((SEQ, SSD_INNER), F32), jax.ShapeDtypeStruct((SEQ, LANES), F32)],
        compiler_params=_params(("parallel",)),
    )(x1b, w1z, w1d)


def _l1_dx_ln(dz, dxbc, ddt, dv1, v0, w1z, w1x, w1d, g):
    t = LN_T

    def body(dz_ref, dx_ref, ddt_ref, dv1_ref, v_ref, wz_ref, wx_ref, wd_ref, g_ref, dv_ref, dgb_ref):
        @pl.when(pl.program_id(0) == 0)
        def _():
            dgb_ref[...] = jnp.zeros_like(dgb_ref)

        dy = (DN_ALPHA * dv1_ref[...] + _dot_nt(dz_ref[...], wz_ref[...]) + _dot_nt(dx_ref[...], wx_ref[...])
              + _dot_nt(_mx(ddt_ref[...]), wd_ref[...]))
        dv, dg, db = _ln_bwd(v_ref[...], g_ref[...], dy)
        dv_ref[...] = dv
        dgb_ref[0:1, :] += dg
        dgb_ref[1:2, :] += db

    rows = lambda w: pl.BlockSpec((t, w), lambda i: (i, 0))
    const = lambda w: pl.BlockSpec((D_MODEL, w), lambda i: (0, 0))
    return pl.pallas_call(
        body, name="l1_dx_ln", grid=(SEQ // t,),
        in_specs=[rows(SSD_INNER), rows(SSD_CONV), rows(LANES), rows(D_MODEL), rows(D_MODEL),
                  const(SSD_INNER), const(SSD_CONV), const(LANES), pl.BlockSpec((1, D_MODEL), lambda i: (0, 0))],
        out_specs=[rows(D_MODEL), pl.BlockSpec((SUBLANES, D_MODEL), lambda i: (0, 0))],
        out_shape=[jax.ShapeDtypeStruct((SEQ, D_MODEL), F32), jax.ShapeDtypeStruct((SUBLANES, D_MODEL), F32)],
        compiler_params=_params(("arbitrary",)),
    )(dz, dxbc, ddt, dv1, v0, w1z, w1x, w1d, g)


def _gate_bwd(dv0, w_out, h, o, proj0, y0, xb):
    t = LN_T

    def body(dv_ref, w_ref, h_ref, o_ref, ga_ref, gb_ref, y0_ref, xb_ref, dh_ref, do_ref, dg_ref, gwo_ref, gwg_ref):
        @pl.when(pl.program_id(0) == 0)
        def _():
            gwo_ref[...] = jnp.zeros_like(gwo_ref)
            gwg_ref[...] = jnp.zeros_like(gwg_ref)

        dvb = _mx(dv_ref[...])
        dy = _dot_nt(dvb, w_ref[...])
        ga, gb, dya, dyb = ga_ref[...], gb_ref[...], dy[:, :512], dy[:, 512:]
        dh_ref[...] = dya * _silu(ga)
        do_ref[...] = dyb * _silu(gb)
        dg_ref[:, :512] = _mx(dya * h_ref[...] * _dsilu(ga))
        dg_ref[:, 512:] = _mx(dyb * o_ref[...] * _dsilu(gb))
        gwo_ref[...] += _dot_tn(y0_ref[...], dvb)
        gwg_ref[...] += _dot_tn(xb_ref[...], dg_ref[...])

    half = pl.BlockSpec((t, 512), lambda i: (i, 0))
    half1 = pl.BlockSpec((t, 512), lambda i: (i, 1))
    full = pl.BlockSpec((t, 1024), lambda i: (i, 0))
    square = pl.BlockSpec((D_MODEL, D_MODEL), lambda i: (0, 0))
    return pl.pallas_call(
        body, name="gate_bwd", grid=(SEQ // t,),
        in_specs=[full, square, half, half, half, half1, full, full],
        out_specs=[half, half, full, square, square],
        out_shape=[jax.ShapeDtypeStruct((SEQ, 512), F32), jax.ShapeDtypeStruct((SEQ, 512), F32),
                   jax.ShapeDtypeStruct((SEQ, 1024), MXU_DTYPE), jax.ShapeDtypeStruct((D_MODEL, D_MODEL), F32),
                   jax.ShapeDtypeStruct((D_MODEL, D_MODEL), F32)],
        compiler_params=_params(("arbitrary",)),
    )(dv0, w_out, h, o, proj0, proj0, y0, xb)


CONV_T = 1024
CONV_CB = 1024


def _ssd_conv_fwd(x1b, w1x, cw8, cb):
    t, cbk = CONV_T, CONV_CB

    def body(x_ref, w_ref, cw_ref, cb_ref, xbc_ref, pre_ref, act_ref, carry):
        xbc = _dot(x_ref[...], w_ref[...])
        halo = jnp.where(pl.program_id(1) > 0, carry[...], 0.0)
        pre = _conv4(xbc, halo, cw_ref[...], cb_ref[...])
        carry[...] = xbc[t - SUBLANES:]
        xbc_ref[...] = xbc
        pre_ref[...] = pre
        act_ref[...] = _silu(pre)

    blk = pl.BlockSpec((t, cbk), lambda j, i: (i, j))
    out = jax.ShapeDtypeStruct((SEQ, SSD_CONV), F32)
    return pl.pallas_call(
        body, name="ssd_conv_fwd", grid=(SSD_CONV // cbk, SEQ // t),
        in_specs=[pl.BlockSpec((t, D_MODEL), lambda j, i: (i, 0)), pl.BlockSpec((D_MODEL, cbk), lambda j, i: (0, j)),
                  pl.BlockSpec((SUBLANES, cbk), lambda j, i: (0, j)), pl.BlockSpec((1, cbk), lambda j, i: (0, j))],
        out_specs=[blk, blk, blk], out_shape=[out, out, out],
        scratch_shapes=[pltpu.VMEM((SUBLANES, cbk), F32)],
        compiler_params=_params(("parallel", "arbitrary")),
    )(x1b, w1x, cw8, cb)


def _ssd_conv_bwd(dact, pre, xbc, cw8, x1b):
    t, cbk = CONV_T, CONV_CB
    tb = t // SUBLANES
    nb = SEQ // t

    def body(da_ref, dan_ref, pre_ref, pren_ref, x_ref, cw_ref, x1_ref, dx_ref, dcw_ref, gw_ref):
        i = pl.program_id(1)

        @pl.when(i == 0)
        def _():
            dcw_ref[...] = jnp.zeros_like(dcw_ref)
            gw_ref[...] = jnp.zeros_like(gw_ref)

        dpre = da_ref[...] * _dsilu(pre_ref[...])
        dpre_next = jnp.where(i < nb - 1, dan_ref[...] * _dsilu(pren_ref[...]), 0.0)
        xblk = x_ref[...]
        cw = cw_ref[...]
        dx = dpre * cw[3:4]
        dcw_ref[3:4, :] += jnp.sum(dpre * xblk, axis=0, keepdims=True)
        for k in range(3):
            up = _shift_up(dpre, dpre_next, 3 - k)
            dcw_ref[k:k + 1, :] += jnp.sum(up * xblk, axis=0, keepdims=True)
            dx = dx + up * cw[k:k + 1]
        dcw_ref[4:5, :] += jnp.sum(dpre, axis=0, keepdims=True)
        dxb = _mx(dx)
        dx_ref[...] = dxb
        gw_ref[...] += _dot_tn(x1_ref[...], dxb)

    blk = pl.BlockSpec((t, cbk), lambda j, i: (i, j))
    nxt = pl.BlockSpec((SUBLANES, cbk), lambda j, i: (jnp.minimum((i + 1) * tb, SEQ // SUBLANES - 1), j))
    acc = pl.BlockSpec((SUBLANES, cbk), lambda j, i: (0, j))
    return pl.pallas_call(
        body, name="ssd_conv_bwd", grid=(SSD_CONV // cbk, nb),
        in_specs=[blk, nxt, blk, nxt, blk, acc, pl.BlockSpec((t, D_MODEL), lambda j, i: (i, 0))],
        out_specs=[blk, acc, pl.BlockSpec((D_MODEL, cbk), lambda j, i: (0, j))],
        out_shape=[jax.ShapeDtypeStruct((SEQ, SSD_CONV), MXU_DTYPE), jax.ShapeDtypeStruct((SUBLANES, SSD_CONV), F32),
                   jax.ShapeDtypeStruct((D_MODEL, SSD_CONV), F32)],
        compiler_params=_params(("parallel", "arbitrary")),
    )(dact, dact, pre, pre, xbc, cw8, x1b)


def _ssd_common(dt_raw, bias, alog, tril, expand_t, xs):
    lane = lax.broadcasted_iota(jnp.int32, dt_raw.shape, 1)
    dt = jnp.where(lane < SSD_HEADS, _softplus(dt_raw + bias), 0.0)
    a_neg = -jnp.exp(alog)
    cs = _dot_hi(tril, dt * a_neg)
    dt_x = _expand_heads(dt, expand_t)
    ecs_x = _expand_heads(jnp.exp(cs), expand_t)
    ds_x = _expand_heads(jnp.exp(cs[SSD_L - 1:SSD_L, :] - cs), expand_t)
    return dt, a_neg, cs, dt_x, None, xs * dt_x, ds_x, ecs_x, ecs_x[SSD_L - 1:SSD_L, :]


def _expand_heads(v, expand_t):
    hi = v.astype(jnp.bfloat16)
    lo = (v - hi.astype(F32)).astype(jnp.bfloat16)
    return _dot_nt(hi, expand_t) + _dot_nt(lo, expand_t)


def _fold_heads(v, expand_t):
    hi = v.astype(jnp.bfloat16)
    lo = (v - hi.astype(F32)).astype(jnp.bfloat16)
    return _dot(hi, expand_t) + _dot(lo, expand_t)


def _ssd_decay(cs, cs_t, hh, causal):
    seg = cs[:, hh:hh + 1] - cs_t[hh:hh + 1, :]
    return jnp.where(causal, jnp.exp(jnp.where(causal, seg, 0.0)), 0.0)


def _ssd_scan_fwd(act, dt_raw, bias, alog, d_x, tril, expand_t):
    nc = SEQ // SSD_L
    gw = SSD_INNER // SSD_GROUPS

    def body(act_ref, dt_ref, bias_ref, alog_ref, dx_ref, tril_ref, et_ref, y_ref, hp_ref, h_sc):
        @pl.when(pl.program_id(0) == 0)
        def _():
            h_sc[...] = jnp.zeros_like(h_sc)

        xs = act_ref[:, :SSD_INNER]
        _, _, cs, _, _, xdt, ds_x, ecs_x, elast = _ssd_common(
            dt_ref[...], bias_ref[...], alog_ref[...], tril_ref[...], et_ref[...], xs)
        cs_t = cs.T
        causal = (lax.broadcasted_iota(jnp.int32, (SSD_L, SSD_L), 0)
                  >= lax.broadcasted_iota(jnp.int32, (SSD_L, SSD_L), 1))
        lane = lax.broadcasted_iota(jnp.int32, (SSD_L, LANES), 1)
        xdt_b = _mx(xdt)
        xds_b = _mx(xdt * ds_x)
        hp_ref[0] = h_sc[...]
        for g in range(SSD_GROUPS):
            gs = slice(g * gw, (g + 1) * gw)
            bg = _mx(act_ref[:, SSD_INNER + g * SSD_N:SSD_INNER + (g + 1) * SSD_N])
            cg = _mx(act_ref[:, SSD_INNER + 512 + g * SSD_N:SSD_INNER + 512 + (g + 1) * SSD_N])
            cb = _dot_nt(cg, bg)
            hprev = h_sc[:, gs]
            yoff = _dot(cg, _mx(hprev)) * ecs_x[:, gs]
            h_sc[:, gs] = hprev * elast[:, gs] + _dot_tn(bg, xds_b[:, gs])
            for pr in range(4):
                ps = slice(g * gw + pr * LANES, g * gw + (pr + 1) * LANES)
                xp = xdt_b[:, ps]
                ydiag = jnp.zeros((SSD_L, LANES), F32)
                for j in range(2):
                    dm = _ssd_decay(cs, cs_t, g * 8 + pr * 2 + j, causal)
                    mine = (lane >= j * 64) & (lane < (j + 1) * 64)
                    ydiag = ydiag + _dot(_mx(cb * dm), jnp.where(mine, xp, jnp.zeros_like(xp)))
                y_ref[:, ps] = ydiag + yoff[:, pr * LANES:(pr + 1) * LANES] + dx_ref[:, ps] * xs[:, ps]

    const = lambda shape: pl.BlockSpec(shape, lambda c: (0, 0))
    return pl.pallas_call(
        body, name="ssd_scan_fwd", grid=(nc,),
        in_specs=[pl.BlockSpec((SSD_L, SSD_CONV), lambda c: (c, 0)), pl.BlockSpec((SSD_L, LANES), lambda c: (c, 0)),
                  const((1, LANES)), const((1, LANES)), const((1, SSD_INNER)), const((SSD_L, SSD_L)),
                  const((SSD_INNER, LANES))],
        out_specs=[pl.BlockSpec((SSD_L, SSD_INNER), lambda c: (c, 0)),
                   pl.BlockSpec((1, SSD_N, SSD_INNER), lambda c: (c, 0, 0))],
        out_shape=[jax.ShapeDtypeStruct((SEQ, SSD_INNER), F32), jax.ShapeDtypeStruct((nc, SSD_N, SSD_INNER), F32)],
        scratch_shapes=[pltpu.VMEM((SSD_N, SSD_INNER), F32)],
        compiler_params=_params(("arbitrary",)),
    )(act, dt_raw, bias, alog, d_x, tril, expand_t)


def _ssd_scan_bwd(dy, act, dt_raw, hprev_all, bias, alog, d_x, tril, expand_t, x1b, scatter=()):
    nc = SEQ // SSD_L
    gw = SSD_INNER // SSD_GROUPS
    sc_arrays, sc_ranges = _scatter_args(scatter)
    nx = len(sc_arrays)

    def body(dy_ref, act_ref, dt_ref, hp_ref, bias_ref, alog_ref, dx_ref, tril_ref, et_ref, x1_ref, *rest):
        dact_ref, ddt_ref, dvec_ref, gdt_ref = rest[nx:nx + 4]
        dh_sc, dd_sc, dcs_sc, dcst_sc = rest[2 * nx + 4:2 * nx + 8]
        i = pl.program_id(0)
        if nx:
            copies = _peer_copies(rest[:nx], rest[nx + 4:2 * nx + 4], rest[2 * nx + 8:], sc_ranges)

            @pl.when(i == 0)
            def _():
                for cp in copies:
                    cp.start()

        @pl.when(i == 0)
        def _():
            dh_sc[...] = jnp.zeros_like(dh_sc)
            dd_sc[...] = jnp.zeros_like(dd_sc)
            gdt_ref[...] = jnp.zeros_like(gdt_ref)
            dvec_ref[...] = jnp.zeros_like(dvec_ref)

        xs = act_ref[:, :SSD_INNER]
        dt_raw_v, bias_v = dt_ref[...], bias_ref[...]
        dt, a_neg, cs, dt_x, _, xdt, ds_x, ecs_x, elast = _ssd_common(
            dt_raw_v, bias_v, alog_ref[...], tril_ref[...], et_ref[...], xs)
        cs_t = cs.T
        rowi = lax.broadcasted_iota(jnp.int32, (SSD_L, SSD_L), 0)
        coli = lax.broadcasted_iota(jnp.int32, (SSD_L, SSD_L), 1)
        causal = rowi >= coli
        lane = lax.broadcasted_iota(jnp.int32, (SSD_L, LANES), 1)
        row_g = lax.broadcasted_iota(jnp.int32, (SSD_L, gw), 0)
        dyv = dy_ref[...]
        dd_sc[0:1, :] += jnp.sum(dyv * xs, axis=0, keepdims=True)
        xdt_b = _mx(xdt)
        xds = xdt * ds_x
        xds_b = _mx(xds)
        dy_b = _mx(dyv)
        dye_b = _mx(dyv * ecs_x)
        dcs_sc[...] = jnp.zeros_like(dcs_sc)
        dcst_sc[...] = jnp.zeros_like(dcst_sc)
        dcs_parts = []
        dxdt_parts = []
        for g in range(SSD_GROUPS):
            gs = slice(g * gw, (g + 1) * gw)
            bcol = slice(SSD_INNER + g * SSD_N, SSD_INNER + (g + 1) * SSD_N)
            ccol = slice(SSD_INNER + 512 + g * SSD_N, SSD_INNER + 512 + (g + 1) * SSD_N)
            bg, cg = _mx(act_ref[:, bcol]), _mx(act_ref[:, ccol])
            cb = _dot_nt(cg, bg)
            hp = hp_ref[0, :, gs]
            hp_b = _mx(hp)
            dh = dh_sc[:, gs]
            dh_b = _mx(dh)
            yoff = _dot(cg, hp_b) * ecs_x[:, gs]
            bdh = _dot(bg, dh_b)
            tt = xds[:, gs] * bdh
            last_row = (jnp.sum(tt, axis=0, keepdims=True)
                        + jnp.sum(dh * hp, axis=0, keepdims=True) * elast[:, gs])
            dcs_parts.append(dyv[:, gs] * yoff - tt + jnp.where(row_g == SSD_L - 1, last_row, 0.0))
            dc_g = _dot_nt(dye_b[:, gs], hp_b)
            db_g = _dot_nt(xds_b[:, gs], dh_b)
            dh_sc[:, gs] = _dot_tn(cg, dye_b[:, gs]) + dh * elast[:, gs]
            wsum = jnp.zeros((SSD_L, SSD_L), F32)
            dxdt_g = []
            for pr in range(4):
                ps = slice(g * gw + pr * LANES, g * gw + (pr + 1) * LANES)
                xp, dyp = xdt_b[:, ps], dy_b[:, ps]
                dxp = jnp.zeros((SSD_L, LANES), F32)
                for j in range(2):
                    hh = g * 8 + pr * 2 + j
                    dm = _ssd_decay(cs, cs_t, hh, causal)
                    mine = (lane >= j * 64) & (lane < (j + 1) * 64)
                    dy_h = jnp.where(mine, dyp, jnp.zeros_like(dyp))
                    wd = _dot_nt(dy_h, xp) * dm
                    wsum = wsum + wd
                    gmat = wd * cb
                    dcs_sc[:, hh:hh + 1] = jnp.sum(gmat, axis=1, keepdims=True)
                    dcst_sc[hh:hh + 1, :] = -jnp.sum(gmat, axis=0, keepdims=True)
                    dxp = dxp + _dot_tn(_mx(cb * dm), dy_h)
                dxdt_g.append(dxp)
            dxdt_parts.append(jnp.concatenate(dxdt_g, axis=1) + bdh * ds_x[:, gs])
            ws_b = _mx(wsum)
            dact_ref[:, ccol] = dc_g + _dot(ws_b, bg)
            dact_ref[:, bcol] = db_g + _dot_tn(ws_b, cg)
        dxdt = jnp.concatenate(dxdt_parts, axis=1)
        dcs_x = jnp.concatenate(dcs_parts, axis=1)
        et = et_ref[...]
        dcs_tot = dcs_sc[...] + dcst_sc[...].T + _fold_heads(dcs_x, et)
        da_dt = _dot_hi((coli >= rowi).astype(F32), dcs_tot)
        ddt = da_dt * a_neg + _fold_heads(dxdt * xs, et)
        ddt_raw = ddt * _sigmoid(dt_raw_v + bias_v)
        ddt_ref[...] = ddt_raw
        gdt_ref[...] += _dot_tn(x1_ref[...], _mx(ddt_raw))
        dvec_ref[0:1, :] += jnp.sum(ddt_raw, axis=0, keepdims=True)
        dvec_ref[1:2, :] += jnp.sum(da_dt * dt, axis=0, keepdims=True) * a_neg
        dact_ref[:, :SSD_INNER] = dyv * dx_ref[...] + dxdt * dt_x

        @pl.when(i == nc - 1)
        def _():
            dvec_ref[2:3, :] = _fold_heads(dd_sc[...], et)[0:1, :]
            if nx:
                for cp in copies:
                    cp.wait()

    const = lambda shape: pl.BlockSpec(shape, lambda c: (0, 0))
    rev = lambda c: (nc - 1 - c, 0)
    res = pl.pallas_call(
        body, name="ssd_scan_bwd", grid=(nc,),
        in_specs=[pl.BlockSpec((SSD_L, SSD_INNER), rev), pl.BlockSpec((SSD_L, SSD_CONV), rev),
                  pl.BlockSpec((SSD_L, LANES), rev),
                  pl.BlockSpec((1, SSD_N, SSD_INNER), lambda c: (nc - 1 - c, 0, 0)),
                  const((1, LANES)), const((1, LANES)), const((1, SSD_INNER)), const((SSD_L, SSD_L)),
                  const((SSD_INNER, LANES)), pl.BlockSpec((SSD_L, D_MODEL), rev)] + [ANY] * nx,
        out_specs=[pl.BlockSpec((SSD_L, SSD_CONV), rev), pl.BlockSpec((SSD_L, LANES), rev), const((SUBLANES, LANES)),
                   const((D_MODEL, LANES))] + [ANY] * nx,
        out_shape=[jax.ShapeDtypeStruct((SEQ, SSD_CONV), F32), jax.ShapeDtypeStruct((SEQ, LANES), F32),
                   jax.ShapeDtypeStruct((SUBLANES, LANES), F32), jax.ShapeDtypeStruct((D_MODEL, LANES), F32)]
        + _exchange_shapes(sc_arrays, []),
        scratch_shapes=[pltpu.VMEM((SSD_N, SSD_INNER), F32), pltpu.VMEM((SUBLANES, SSD_INNER), F32),
                        pltpu.VMEM((SSD_L, LANES), F32), pltpu.VMEM((LANES, SSD_L), F32)]
        + (_exchange_sems(nx) if nx else []),
        compiler_params=_params(("arbitrary",)),
    )(dy, act, dt_raw, hprev_all, bias, alog, d_x, tril, expand_t, x1b, *sc_arrays)
    return res[0], res[1], res[2], res[3], res[4:]


L1_T = 512


def _resident(shape):
    return pl.BlockSpec(shape, lambda i: (0, 0), pipeline_mode=pl.Buffered(1))


def _gated_norm(y, z, nw):
    y2 = y * _silu(z)
    gw = SSD_INNER // SSD_GROUPS
    outs, xhs, rss = [], [], []
    for g in range(SSD_GROUPS):
        gs = slice(g * gw, (g + 1) * gw)
        v = y2[:, gs]
        rs = lax.rsqrt(jnp.mean(v * v, axis=-1, keepdims=True) + 1e-6)
        xhs.append(v * rs)
        rss.append(rs)
        outs.append(v * rs * nw[:, gs])
    return outs, xhs, rss


def _l1_out(y, z, nw, w_out, x1, g, b, target):
    t = L1_T

    def body(y_ref, z_ref, nw_ref, w_ref, x1_ref, g_ref, b_ref, tg_ref, dv_ref, dgb_ref, loss_ref, gw_ref):
        @pl.when(pl.program_id(0) == 0)
        def _():
            dgb_ref[...] = jnp.zeros_like(dgb_ref)
            loss_ref[...] = jnp.zeros_like(loss_ref)
            gw_ref[...] = jnp.zeros_like(gw_ref)

        outs, _, _ = _gated_norm(y_ref[...], z_ref[...], nw_ref[...])
        yn = _mx(jnp.concatenate(outs, axis=1))
        v = DN_ALPHA * x1_ref[...] + _dot(yn, w_ref[...])
        gv = g_ref[...]
        err = _ln(v, gv, b_ref[...]) - tg_ref[...]
        rowsum = jnp.sum(err * err, axis=1, keepdims=True)
        loss_ref[...] += 0.5 * jnp.sum(rowsum, axis=0, keepdims=True) / D_MODEL
        dv, dg, db = _ln_bwd(v, gv, err / D_MODEL)
        dv_ref[...] = dv
        dgb_ref[0:1, :] += dg
        dgb_ref[1:2, :] += db
        gw_ref[...] += _dot_tn(yn, _mx(dv))

    wide = pl.BlockSpec((t, SSD_INNER), lambda i: (i, 0))
    full = pl.BlockSpec((t, D_MODEL), lambda i: (i, 0))
    vec = pl.BlockSpec((1, D_MODEL), lambda i: (0, 0))
    return pl.pallas_call(
        body, name="l1_out", grid=(SEQ // t,),
        in_specs=[wide, wide, pl.BlockSpec((1, SSD_INNER), lambda i: (0, 0)),
                  _resident((SSD_INNER, D_MODEL)), full, vec, vec, full],
        out_specs=[full, pl.BlockSpec((SUBLANES, D_MODEL), lambda i: (0, 0)),
                   pl.BlockSpec((SUBLANES, LANES), lambda i: (0, 0)), _resident((SSD_INNER, D_MODEL))],
        out_shape=[jax.ShapeDtypeStruct((SEQ, D_MODEL), F32), jax.ShapeDtypeStruct((SUBLANES, D_MODEL), F32),
                   jax.ShapeDtypeStruct((SUBLANES, LANES), F32), jax.ShapeDtypeStruct((SSD_INNER, D_MODEL), F32)],
        compiler_params=_params(("arbitrary",)),
    )(y, z, nw, w_out, x1, g, b, target)


def _l1_gate_bwd(dv1, w_out, y, z, nw, x1b):
    t = L1_T
    gw = SSD_INNER // SSD_GROUPS

    def body(dv_ref, w_ref, y_ref, z_ref, nw_ref, x1_ref, dy_ref, dz_ref, dnw_ref, gw_ref):
        @pl.when(pl.program_id(0) == 0)
        def _():
            dnw_ref[...] = jnp.zeros_like(dnw_ref)
            gw_ref[...] = jnp.zeros_like(gw_ref)

        dyn = _dot_nt(_mx(dv_ref[...]), w_ref[...])
        yv, zv, nwv = y_ref[...], z_ref[...], nw_ref[...]
        _, xhs, rss = _gated_norm(yv, zv, nwv)
        sz, dsz = _silu(zv), _dsilu(zv)
        for g in range(SSD_GROUPS):
            gs = slice(g * gw, (g + 1) * gw)
            d_out = dyn[:, gs]
            xh = xhs[g]
            dnw_ref[0:1, gs] += jnp.sum(d_out * xh, axis=0, keepdims=True)
            dxh = d_out * nwv[:, gs]
            dy2 = rss[g] * (dxh - xh * jnp.mean(dxh * xh, axis=-1, keepdims=True))
            dy_ref[:, gs] = dy2 * sz[:, gs]
            dz_ref[:, gs] = _mx(dy2 * yv[:, gs] * dsz[:, gs])
        gw_ref[...] += _dot_tn(x1_ref[...], dz_ref[...])

    wide = pl.BlockSpec((t, SSD_INNER), lambda i: (i, 0))
    return pl.pallas_call(
        body, name="l1_gate_bwd", grid=(SEQ // t,),
        in_specs=[pl.BlockSpec((t, D_MODEL), lambda i: (i, 0)), _resident((SSD_INNER, D_MODEL)),
                  wide, wide, pl.BlockSpec((1, SSD_INNER), lambda i: (0, 0)), pl.BlockSpec((t, D_MODEL), lambda i: (i, 0))],
        out_specs=[wide, wide, pl.BlockSpec((SUBLANES, SSD_INNER), lambda i: (0, 0)),
                   _resident((D_MODEL, SSD_INNER))],
        out_shape=[jax.ShapeDtypeStruct((SEQ, SSD_INNER), F32), jax.ShapeDtypeStruct((SEQ, SSD_INNER), MXU_DTYPE),
                   jax.ShapeDtypeStruct((SUBLANES, SSD_INNER), F32), jax.ShapeDtypeStruct((D_MODEL, SSD_INNER), F32)],
        compiler_params=_params(("arbitrary",)),
    )(dv1, w_out, y, z, nw, x1b)


MESH = pl.DeviceIdType.MESH
ANY = pl.BlockSpec(memory_space=pl.ANY)


def _flip(v, bit):
    return 1 - v if bit else v


def _all_gather(blocks, name):
    n = len(blocks)

    def body(*refs):
        x_refs, out_refs = refs[:n], refs[n:2 * n]
        send_sems, recv_sems, local_sems = refs[2 * n:]
        mx, my, mc = lax.axis_index("x"), lax.axis_index("y"), lax.axis_index("c")
        me, sibling = (mx, my, mc), (mx, my, 1 - mc)
        chips = [(1 - mx, my), (mx, 1 - my), (1 - mx, 1 - my)]

        def copy(a, k, block, to, own=False):
            px, py, pc = block
            slot = out_refs[a].at[4 * px + 2 * py + pc]
            return pltpu.make_async_remote_copy(
                src_ref=x_refs[a] if own else slot, dst_ref=slot,
                send_sem=send_sems.at[7 * a + k], recv_sem=recv_sems.at[7 * a + k], device_id=to, device_id_type=MESH)

        mine = [pltpu.make_async_copy(x_refs[a], out_refs[a].at[4 * mx + 2 * my + mc], local_sems.at[a])
                for a in range(n)]
        first = []
        for a in range(n):
            mine[a].start()
            first.append(copy(a, 0, me, sibling, own=True))
            first += [copy(a, 1 + j, me, (*chip, mc), own=True) for j, chip in enumerate(chips)]
        for cp in first:
            cp.start()
        passed = []
        for j, chip in enumerate(chips):
            for a in range(n):
                copy(a, 1 + j, (*chip, mc), me).wait_recv()
                fwd = copy(a, 4 + j, (*chip, mc), sibling)
                fwd.start()
                passed.append(fwd)
        for a in range(n):
            copy(a, 0, sibling, me).wait_recv()
            for j, chip in enumerate(chips):
                copy(a, 4 + j, (*chip, 1 - mc), me).wait_recv()
        for cp in first + passed:
            cp.wait_send()
        for cp in mine:
            cp.wait()

    return pl.pallas_call(
        body, name=name, in_specs=[ANY] * n, out_specs=[ANY] * n,
        out_shape=[jax.ShapeDtypeStruct((N_DEV,) + b.shape, b.dtype) for b in blocks],
        scratch_shapes=[pltpu.SemaphoreType.DMA((7 * n,)), pltpu.SemaphoreType.DMA((7 * n,)),
                        pltpu.SemaphoreType.DMA((n,))],
    )(*blocks)


def _l0_in(x, w0p, bcast=()):
    n = len(bcast)
    tm, tn = 1024, 1024
    gi, gj = SEQ // tm, 2048 // tn

    def body(x_ref, w_ref, *rest):
        o_ref, xb_ref = rest[n], rest[n + 1]
        i, j = pl.program_id(0), pl.program_id(1)
        if n:
            copies = _peer_copies(rest[:n], rest[n + 2:2 * n + 2], rest[2 * n + 2:], [])

            @pl.when((i == 0) & (j == 0))
            def _():
                for cp in copies:
                    cp.start()

        xb = _mx(x_ref[...])
        xb_ref[...] = xb
        o_ref[...] = _dot(xb, w_ref[...])

        if n:
            @pl.when((i == gi - 1) & (j == gj - 1))
            def _():
                for cp in copies:
                    cp.wait()

    res = pl.pallas_call(
        body, name="l0_in", grid=(gi, gj),
        in_specs=[pl.BlockSpec((tm, D_MODEL), lambda i, j: (i, 0)), pl.BlockSpec((D_MODEL, tn), lambda i, j: (0, j))]
        + [ANY] * n,
        out_specs=[pl.BlockSpec((tm, tn), lambda i, j: (i, j)), pl.BlockSpec((tm, D_MODEL), lambda i, j: (i, 0))]
        + [ANY] * n,
        out_shape=[jax.ShapeDtypeStruct((SEQ, 2048), F32), jax.ShapeDtypeStruct((SEQ, D_MODEL), MXU_DTYPE)]
        + _exchange_shapes([], bcast),
        scratch_shapes=_exchange_sems(n) if n else [],
        compiler_params=_params(("arbitrary", "arbitrary")),
    )(x, w0p, *bcast)
    return res[0], res[1], res[2:]


def _l0_dx(dxr, dgate, dtail, w0p, dv0, scatter=(), bcast=()):
    arrays, ranges = _scatter_args(scatter)
    n = len(arrays) + len(bcast)
    tm = 1024
    steps = SEQ // tm

    def body(dxr_ref, dg_ref, dt_ref, w_ref, dv_ref, *rest):
        o_ref = rest[n]
        i = pl.program_id(0)
        if n:
            copies = _peer_copies(rest[:n], rest[n + 1:2 * n + 1], rest[2 * n + 1:], ranges)

            @pl.when(i == 0)
            def _():
                for cp in copies:
                    cp.start()

        o_ref[...] = (DN_ALPHA * dv_ref[...] + _dot_nt(dg_ref[...], w_ref[:, 0:1024])
                      + _dot_nt(dxr_ref[...], w_ref[:, 1024:1536]) + _dot_nt(dt_ref[...], w_ref[:, 1536:2048]))

        if n:
            @pl.when(i == steps - 1)
            def _():
                for cp in copies:
                    cp.wait()

    rows = lambda w: pl.BlockSpec((tm, w), lambda i: (i, 0))
    res = pl.pallas_call(
        body, name="l0_dx", grid=(steps,),
        in_specs=[rows(512), rows(1024), rows(512), pl.BlockSpec((D_MODEL, 2048), lambda i: (0, 0)), rows(D_MODEL)]
        + [ANY] * n,
        out_specs=[rows(D_MODEL)] + [ANY] * n,
        out_shape=[jax.ShapeDtypeStruct((SEQ, D_MODEL), F32)] + _exchange_shapes(arrays, bcast),
        scratch_shapes=_exchange_sems(n) if n else [],
        compiler_params=_params(("arbitrary",)),
    )(dxr, dgate, dtail, w0p, dv0, *arrays, *bcast)
    return res[0], res[1:]


def _scatter_args(scatter):
    arrays = [s[0] if isinstance(s, tuple) else s for s in scatter]
    ranges = [s[1] if isinstance(s, tuple) else (0, N_DEV) for s in scatter]
    return arrays, ranges


def _exchange_shapes(scatter, bcast):
    return ([jax.ShapeDtypeStruct((N_DEV,) + a.shape[1:], a.dtype) for a in scatter]
            + [jax.ShapeDtypeStruct((N_DEV,) + a.shape, a.dtype) for a in bcast])


def _exchange_sems(n):
    return [pltpu.SemaphoreType.DMA((7 * n,)), pltpu.SemaphoreType.DMA((7 * n,)), pltpu.SemaphoreType.DMA((n,))]


class _GuardedCopy:
    def __init__(self, copy, send=None, recv=None, local=False):
        self.copy, self.send, self.recv, self.local = copy, send, recv, local

    @staticmethod
    def _run(pred, fn):
        if pred is None:
            fn()
        else:
            pl.when(pred)(fn)

    def start(self):
        self._run(self.send, self.copy.start)

    def wait(self):
        if self.local:
            self._run(self.send, self.copy.wait)
        else:
            self._run(self.send, self.copy.wait_send)
            self._run(self.recv, self.copy.wait_recv)


def _peer_copies(in_refs, out_refs, sems, ranges):
    send_sems, recv_sems, local_sems = sems
    n, ns = len(in_refs), len(ranges)
    mx, my, mc = lax.axis_index("x"), lax.axis_index("y"), lax.axis_index("c")
    me = 4 * mx + 2 * my + mc

    def src(a, slot):
        return in_refs[a].at[slot - ranges[a][0]] if a < ns else in_refs[a]

    def member(a, dev):
        if a >= ns or ranges[a] == (0, N_DEV):
            return None
        return (dev >= ranges[a][0]) & (dev < ranges[a][1])

    copies = [_GuardedCopy(pltpu.make_async_copy(src(a, me), out_refs[a].at[me], local_sems.at[a]),
                           send=member(a, me), local=True) for a in range(n)]
    for k in range(1, N_DEV):
        px, py, pc = _flip(mx, (k >> 2) & 1), _flip(my, (k >> 1) & 1), _flip(mc, k & 1)
        peer = 4 * px + 2 * py + pc
        for a in range(n):
            copies.append(_GuardedCopy(pltpu.make_async_remote_copy(
                src_ref=src(a, peer), dst_ref=out_refs[a].at[me],
                send_sem=send_sems.at[7 * a + k - 1], recv_sem=recv_sems.at[7 * a + k - 1],
                device_id=(px, py, pc), device_id_type=MESH), send=member(a, peer), recv=member(a, me)))
    return copies


def _segments(col_map, width):
    segs = []
    for lo, hi, arr, alo in col_map:
        for s in range(N_DEV):
            a, b = max(lo, s * width), min(hi, (s + 1) * width)
            if a < b:
                segs.append((s, a - s * width, b - a, arr, alo + a - lo))
    return segs


COPY_ROWS = 256


def _unshard(g8, col_map, widths, name):
    _, r, w = g8.shape
    rb = min(r, COPY_ROWS)
    segs = _segments(col_map, w)

    def body(g_ref, *o_refs):
        for o_ref in o_refs:
            o_ref[...] = jnp.zeros_like(o_ref)
        for s, llo, n, arr, alo in segs:
            o_refs[arr][:, alo:alo + n] = g_ref[s, :, llo:llo + n]

    return pl.pallas_call(
        body, name=name, grid=(r // rb,),
        in_specs=[pl.BlockSpec((N_DEV, rb, w), lambda i: (0, i, 0))],
        out_specs=[pl.BlockSpec((rb, n), lambda i: (i, 0)) for n in widths],
        out_shape=[jax.ShapeDtypeStruct((r, n), g8.dtype) for n in widths],
        compiler_params=_params(("parallel",)),
    )(g8)


def _reshard(srcs, col_map, w, dtype, name, shards=(0, N_DEV)):
    r = srcs[0].shape[0]
    rb = min(r, COPY_ROWS)
    lo, hi = shards
    segs = [sg for sg in _segments(col_map, w) if lo <= sg[0] < hi]

    def body(*refs):
        o_ref = refs[-1]
        for s, llo, n, arr, alo in segs:
            o_ref[s - lo, :, llo:llo + n] = refs[arr][:, alo:alo + n].astype(dtype)

    return pl.pallas_call(
        body, name=name, grid=(r // rb,),
        in_specs=[pl.BlockSpec((rb, a.shape[1]), lambda i: (i, 0)) for a in srcs],
        out_specs=pl.BlockSpec((hi - lo, rb, w), lambda i: (0, i, 0)),
        out_shape=jax.ShapeDtypeStruct((hi - lo, r, w), dtype),
        compiler_params=_params(("parallel",)),
    )(*srcs)


def _adamw(parts, w, m, v, name):
    r, c = w.shape
    tr = COPY_ROWS if r % COPY_ROWS == 0 else r

    def body(p_ref, w_ref, m_ref, v_ref, g_ref, d_ref, mo_ref, vo_ref):
        g = p_ref[0].astype(F32)
        for s in range(1, N_DEV):
            g = g + p_ref[s].astype(F32)
        g_ref[...] = g
        d_ref[...], mo_ref[...], vo_ref[...] = _adamw_math(g, w_ref[...], m_ref[...], v_ref[...])

    blk = pl.BlockSpec((tr, c), lambda i: (i, 0))
    out = jax.ShapeDtypeStruct((r, c), F32)
    return pl.pallas_call(
        body, name=name, grid=(r // tr,),
        in_specs=[pl.BlockSpec((N_DEV, tr, c), lambda i: (0, i, 0)), blk, blk, blk],
        out_specs=[blk, blk, blk, blk], out_shape=[out, out, out, out],
        compiler_params=_params(("parallel",)),
    )(parts, w, m, v)


def _adamw_math(g, w, m, v):
    mn = ADAM_B1 * m + (1.0 - ADAM_B1) * g
    vn = ADAM_B2 * v + (1.0 - ADAM_B2) * (g * g)
    m_hat = mn / (1.0 - ADAM_B1 ** ADAM_STEP)
    v_hat = vn / (1.0 - ADAM_B2 ** ADAM_STEP)
    return -ADAM_LR * (m_hat / (jnp.sqrt(v_hat) + ADAM_EPS) + ADAM_WD * w), mn, vn


SMALL = (("ab_conv_w", 0, 4, 64), ("ssd_conv_w", 4, 4, 384), ("ssd_conv_b", 8, 1, 384), ("ssd_norm", 9, 1, 256),
         ("ssd_ln_g", 10, 1, 128), ("ssd_ln_b", 11, 1, 128))
VECS = (("ab_conv_b", 512), ("ab_gate_a_b", 512), ("ab_gate_x_b", 512), ("ab_lambda", 512), ("mla_q_norm", 256),
        ("mla_kv_norm", 128), ("ab_ln_g", 1024), ("ab_ln_b", 1024), ("ssd_dt_bias", 32), ("ssd_a_log", 32),
        ("ssd_d", 32))
GATES = ("ab_gate_a_w", "ab_gate_x_w")
SMALL_NAMES = tuple(n for n, *_ in SMALL) + tuple(n for n, _ in VECS) + GATES
VMEM_WHOLE = pl.BlockSpec(memory_space=pltpu.VMEM)


def _view2d(name, a):
    if name in GATES:
        return a.reshape(RNN_W, 64)
    return a[0] if a.ndim == 3 else a


def _unshard_small(g):
    widths = (512, 3072, 3072, 2048, 1024, 1024)

    def body(*refs):
        ins, outs = refs[:6], refs[6:]
        outs[0][...] = jnp.zeros_like(outs[0])
        outs[1][...] = jnp.zeros_like(outs[1])
        for (_, _, nr, c), i_ref, o_ref in zip(SMALL, ins, outs):
            for j in range(N_DEV):
                o_ref[0:nr, j * c:(j + 1) * c] = i_ref[j]

    return pl.pallas_call(
        body, name="unshard_small", in_specs=[VMEM_WHOLE] * 6, out_specs=[VMEM_WHOLE] * 6,
        out_shape=[jax.ShapeDtypeStruct((SUBLANES if nr == 4 else 1, w), F32) for (_, _, nr, _), w in zip(SMALL, widths)],
    )(*g)


def _prep_repl(ga, gx, dt_bias, a_log, d):
    def body(ga_ref, gx_ref, b_ref, al_ref, d_ref, wa_ref, wx_ref, b128_ref, al128_ref, dx_ref):
        wa_ref[...] = jnp.zeros_like(wa_ref)
        wx_ref[...] = jnp.zeros_like(wx_ref)
        for hd in range(8):
            hs = slice(hd * 64, (hd + 1) * 64)
            wa_ref[hs, hs] = _mx(ga_ref[hs, :])
            wx_ref[hs, hs] = _mx(gx_ref[hs, :])
        b128_ref[...] = jnp.zeros_like(b128_ref)
        al128_ref[...] = jnp.zeros_like(al128_ref)
        b128_ref[:, 0:SSD_HEADS] = b_ref[...]
        al128_ref[:, 0:SSD_HEADS] = al_ref[...]
        dv = d_ref[...]
        for hd in range(SSD_HEADS):
            dx_ref[:, hd * SSD_P:(hd + 1) * SSD_P] = jnp.broadcast_to(dv[:, hd:hd + 1], (1, SSD_P))

    return pl.pallas_call(
        body, name="prep_repl", in_specs=[VMEM_WHOLE] * 5, out_specs=[VMEM_WHOLE] * 5,
        out_shape=[jax.ShapeDtypeStruct((RNN_W, RNN_W), MXU_DTYPE), jax.ShapeDtypeStruct((RNN_W, RNN_W), MXU_DTYPE),
                   jax.ShapeDtypeStruct((1, LANES), F32), jax.ShapeDtypeStruct((1, LANES), F32),
                   jax.ShapeDtypeStruct((1, SSD_INNER), F32)],
    )(ga, gx, dt_bias, a_log, d)


LOSS_ROW = 11


def _pack_small(dvec0, g_wa, g_wx, dqnw, dknw, dgb0, dvec1, dcw1, dnw, dgb1, loss8):
    def body(dvec0_ref, gwa_ref, gwx_ref, dqn_ref, dkn_ref, dgb0_ref, dvec1_ref, dcw1_ref, dnw_ref, dgb1_ref,
             loss_ref, sm_ref, vec_ref, gg_ref):
        sm_ref[...] = jnp.zeros_like(sm_ref)
        vec_ref[...] = jnp.zeros_like(vec_ref)
        sharded = ((dvec0_ref, 4), (dcw1_ref, 0), (dcw1_ref, 4), (dnw_ref, 0), (dgb1_ref, 0), (dgb1_ref, 1))
        for (_, r0, nr, c), (src, sr) in zip(SMALL, sharded):
            for j in range(N_DEV):
                sm_ref[j, r0:r0 + nr, 0:c] = src[sr:sr + nr, j * c:(j + 1) * c]
        vectors = ((dvec0_ref, 3), (dvec0_ref, 0), (dvec0_ref, 1), (dvec0_ref, 2), (dqn_ref, 0), (dkn_ref, 0),
                   (dgb0_ref, 0), (dgb0_ref, 1), (dvec1_ref, 0), (dvec1_ref, 1), (dvec1_ref, 2))
        for row, ((_, c), (src, sr)) in enumerate(zip(VECS, vectors)):
            vec_ref[row:row + 1, 0:c] = src[sr:sr + 1, 0:c]
        vec_ref[LOSS_ROW:LOSS_ROW + 1, 0:LANES] = loss_ref[0:1, :]
        for hd in range(8):
            hs = slice(hd * 64, (hd + 1) * 64)
            gg_ref[hs, 0:64] = _mx(gwa_ref[hs, hs])
            gg_ref[hs, 64:128] = _mx(gwx_ref[hs, hs])

    return pl.pallas_call(
        body, name="pack_small", in_specs=[VMEM_WHOLE] * 11, out_specs=[VMEM_WHOLE] * 3,
        out_shape=[jax.ShapeDtypeStruct((N_DEV, 16, 384), F32), jax.ShapeDtypeStruct((16, 1024), F32),
                   jax.ShapeDtypeStruct((RNN_W, LANES), MXU_DTYPE)],
    )(dvec0, g_wa, g_wx, dqnw, dknw, dgb0, dvec1, dcw1, dnw, dgb1, loss8)


def _adamw_small(recv_sm, recv_vec, recv_gg, wmv):
    plan = ([(0, r0, nr, c) for _, r0, nr, c in SMALL] + [(1, row, 1, c) for row, (_, c) in enumerate(VECS)]
            + [(2, 0, RNN_W, 0), (2, 0, RNN_W, 64)])
    n = len(plan)

    def body(*refs):
        recv, ins, outs = refs[:3], refs[3:3 + 3 * n], refs[3 + 3 * n:]
        for i, (src, r0, nr, c) in enumerate(plan):
            cols = slice(c, c + 64) if src == 2 else slice(0, c)
            g = recv[src][0, r0:r0 + nr, cols].astype(F32)
            for s in range(1, N_DEV):
                g = g + recv[src][s, r0:r0 + nr, cols].astype(F32)
            w_ref, m_ref, v_ref = ins[3 * i:3 * i + 3]
            outs[4 * i][...] = g
            outs[4 * i + 1][...], outs[4 * i + 2][...], outs[4 * i + 3][...] = _adamw_math(
                g, w_ref[...], m_ref[...], v_ref[...])
        loss = recv[1][0, LOSS_ROW:LOSS_ROW + 1, 0:LANES]
        for s in range(1, N_DEV):
            loss = loss + recv[1][s, LOSS_ROW:LOSS_ROW + 1, 0:LANES]
        outs[4 * n][...] = loss

    flat = [a for t in wmv for a in t]
    return pl.pallas_call(
        body, name="adamw_small", in_specs=[VMEM_WHOLE] * (3 + 3 * n), out_specs=[VMEM_WHOLE] * (4 * n + 1),
        out_shape=[jax.ShapeDtypeStruct(t[0].shape, F32) for t in wmv for _ in range(4)]
        + [jax.ShapeDtypeStruct((1, LANES), F32)],
    )(recv_sm, recv_vec, recv_gg, *flat)


BIG_L0 = ("ab_w_in", "ab_w_out", "mla_w_uq", "mla_w_ukv")
BIG_L1 = ("ssd_w_in", "ssd_w_out")

MAP_W0 = ((0, 512, 0, 1024), (512, 1536, 0, 0), (1536, 1920, 0, 1536), (1920, 1952, 0, 1984))
MAP_W1 = ((0, 2048, 0, 0), (2048, 5120, 1, 0), (5120, 5152, 2, 0))
MAP_WQ = tuple((96 * hd, 96 * hd + 96, 0, 128 * hd) for hd in range(8))
MAP_WKV = (tuple((128 * hd, 128 * hd + 64, 0, 128 * hd) for hd in range(8))
           + tuple((128 * hd + 64, 128 * hd + 128, 0, 1024 + 64 * hd) for hd in range(8)))
MAP_G0 = ((0, 512, 0, 0), (512, 1536, 1, 0), (1536, 1920, 2, 0), (1920, 1952, 2, 448))
W0_EARLY, W0_LATE = (0, 6), (6, 8)


def kernel(x, positions, ab_w_in, ab_conv_w, ab_conv_b, ab_gate_a_w, ab_gate_a_b, ab_gate_x_w, ab_gate_x_b, ab_lambda, mla_q_norm, mla_kv_norm, mla_w_uq, mla_w_ukv, ab_w_out, ab_ln_g, ab_ln_b, ssd_w_in, ssd_conv_w, ssd_conv_b, ssd_dt_bias, ssd_a_log, ssd_d, ssd_norm, ssd_w_out, ssd_ln_g, ssd_ln_b, loss_target, m_ab_w_in, m_ab_conv_w, m_ab_conv_b, m_ab_gate_a_w, m_ab_gate_a_b, m_ab_gate_x_w, m_ab_gate_x_b, m_ab_lambda, m_mla_q_norm, m_mla_kv_norm, m_mla_w_uq, m_mla_w_ukv, m_ab_w_out, m_ab_ln_g, m_ab_ln_b, m_ssd_w_in, m_ssd_conv_w, m_ssd_conv_b, m_ssd_dt_bias, m_ssd_a_log, m_ssd_d, m_ssd_norm, m_ssd_w_out, m_ssd_ln_g, m_ssd_ln_b, v_ab_w_in, v_ab_conv_w, v_ab_conv_b, v_ab_gate_a_w, v_ab_gate_a_b, v_ab_gate_x_w, v_ab_gate_x_b, v_ab_lambda, v_mla_q_norm, v_mla_kv_norm, v_mla_w_uq, v_mla_w_ukv, v_ab_w_out, v_ab_ln_g, v_ab_ln_b, v_ssd_w_in, v_ssd_conv_w, v_ssd_conv_b, v_ssd_dt_bias, v_ssd_a_log, v_ssd_d, v_ssd_norm, v_ssd_w_out, v_ssd_ln_g, v_ssd_ln_b):
    args = dict(locals())
    bf = MXU_DTYPE
    big = {n: [args[pre + n][0] for pre in ("", "m_", "v_")] for n in BIG_L0 + BIG_L1}
    sml = {n: [_view2d(n, args[pre + n]) for pre in ("", "m_", "v_")] for n in SMALL_NAMES}

    w0_8, cw0_8 = _all_gather([big["ab_w_in"][0].astype(bf), sml["ab_conv_w"][0]], "gather_params")
    p = {"cw0_8": cw0_8, "l0_blocks": [big[n][0].astype(bf) for n in BIG_L0[1:]] + [sml[n][0] for n, *_ in SMALL[1:]]}
    p["w0p"], = _unshard(w0_8, MAP_W0, (2048,), "unshard_w0")
    p["wa"], p["wx"], p["dt_bias"], p["a_log"], p["d_x"] = _prep_repl(
        sml["ab_gate_a_w"][0], sml["ab_gate_x_w"][0], sml["ssd_dt_bias"][0], sml["ssd_a_log"][0], sml["ssd_d"][0])
    for key, n in (("cb0", "ab_conv_b"), ("ba", "ab_gate_a_b"), ("bx", "ab_gate_x_b"), ("lam", "ab_lambda"),
                   ("qn_w", "mla_q_norm"), ("kn_w", "mla_kv_norm"), ("g0", "ab_ln_g"), ("b0", "ab_ln_b")):
        p[key] = sml[n][0]

    _, recv_early, recv, _, grad_x = _local_step(
        x[0], positions[0], loss_target[0], p, [big[n][0].astype(bf) for n in BIG_L1])

    me = 4 * lax.axis_index("x") + 2 * lax.axis_index("y") + lax.axis_index("c")
    parts = dict(recv_early, ab_w_in=jnp.where(me >= W0_LATE[0], recv[0], recv_early["ab_w_in"]),
                 mla_w_uq=recv[1], mla_w_ukv=recv[2])

    outs = {}
    kinds = ("grad", "delta", "new_m", "new_v")
    for n in BIG_L0 + BIG_L1:
        for kind, res in zip(kinds, _adamw(parts[n], *big[n], "adamw_" + n)):
            outs[kind, n] = res[None]
    res = _adamw_small(*recv[3:], [sml[n] for n in SMALL_NAMES])
    for i, n in enumerate(SMALL_NAMES):
        for k, kind in enumerate(kinds):
            outs[kind, n] = res[4 * i + k].reshape(args[n].shape)

    loss = res[4 * len(SMALL_NAMES)][0, 0]
    order = ["ab_w_in", "ab_conv_w", "ab_conv_b", "ab_gate_a_w", "ab_gate_a_b", "ab_gate_x_w", "ab_gate_x_b",
             "ab_lambda", "mla_q_norm", "mla_kv_norm", "mla_w_uq", "mla_w_ukv", "ab_w_out", "ab_ln_g", "ab_ln_b",
             "ssd_w_in", "ssd_conv_w", "ssd_conv_b", "ssd_dt_bias", "ssd_a_log", "ssd_d", "ssd_norm", "ssd_w_out",
             "ssd_ln_g", "ssd_ln_b"]
    return (loss, grad_x[None], *[outs[kind, n] for kind in ("grad", "delta", "new_m", "new_v") for n in order])


def _local_step(x, pos, target, p, l1_blocks):
    bf = MXU_DTYPE
    inv_freq = 10000.0 ** (-jnp.arange(0, 32, 2, dtype=F32) / 32)
    ang = pos.astype(F32)[:, None] * inv_freq
    cos, sin = jnp.cos(ang), jnp.sin(ang)
    zeros = lambda n: jnp.zeros((SEQ, n), F32)
    tc = jnp.concatenate([jnp.ones((SEQ, 64), F32), cos, cos, zeros(32)], axis=1)
    tsa = jnp.concatenate([zeros(64), -sin, zeros(48)], axis=1)
    tsb = jnp.concatenate([zeros(80), sin, zeros(32)], axis=1)

    w0p, wa, wxg = (p[k] for k in ("w0p", "wa", "wx"))
    cb0, ba, bx, lam = (p[k] for k in ("cb0", "ba", "bx", "lam"))
    qn_w, kn_w, g0, b0 = (p[k] for k in ("qn_w", "kn_w", "g0", "b0"))
    dt_bias, a_log, d_x = (p[k] for k in ("dt_bias", "a_log", "d_x"))
    tril = jnp.tril(jnp.ones((SSD_L, SSD_L), F32))
    expand_t = (jnp.arange(SSD_INNER)[:, None] // SSD_P == jnp.arange(LANES)[None, :]).astype(jnp.bfloat16)

    proj0, xb, l0_8 = _l0_in(x, w0p, bcast=p["l0_blocks"])
    wo0 = l0_8[0].reshape(D_MODEL, D_MODEL)
    wq, = _unshard(l0_8[1], MAP_WQ, (1024,), "unshard_wq")
    wkv, = _unshard(l0_8[2], MAP_WKV, (1536,), "unshard_wkv")
    cw0, cw1, cb1, nw, g1, b1 = _unshard_small([p["cw0_8"]] + list(l0_8[3:]))
    xc, h, (wo1_8,) = _rglru_fwd(proj0, cw0, cb0, wa, ba, wxg, bx, lam, bcast=l1_blocks[1:])
    qn, kn, qc, kc, vc = _mla_fwd(proj0, qn_w, kn_w, wq, wkv, tc, tsa, tsb)
    o, lse, (w1_8,) = _flash_fwd(qc, kc, vc, bcast=l1_blocks[:1])
    w1z, w1x, w1d = _unshard(w1_8, MAP_W1, (2048, 3072, 128), "unshard_w1")
    wo1 = wo1_8.reshape(SSD_INNER, D_MODEL)
    y0, v0, x1, x1b = _l0_out(h, o, proj0, x, wo0, g0, b0)

    z, dt_raw = _l1_in(x1b, w1z, w1d)
    xbc, pre, act = _ssd_conv_fwd(x1b, w1x, cw1, cb1)
    ys, hprev = _ssd_scan_fwd(act, dt_raw, dt_bias, a_log, d_x, tril, expand_t)
    dv1, dgb1, loss8, g_wo1 = _l1_out(ys, z, nw, wo1, x1, g1, b1, target)

    dys, dz, dnw, g_z = _l1_gate_bwd(dv1, wo1, ys, z, nw, x1b)
    dact, ddt_raw, dvec1, g_dt, (recv_wo1,) = _ssd_scan_bwd(
        dys, act, dt_raw, hprev, dt_bias, a_log, d_x, tril, expand_t, x1b,
        scatter=[g_wo1.astype(bf).reshape(N_DEV, 256, D_MODEL)])
    dxbc, dcw1, g_xbc = _ssd_conv_bwd(dact, pre, xbc, cw1, x1b)

    dv0, dgb0 = _l1_dx_ln(dz, dxbc, ddt_raw, dv1, v0, w1z, w1x, w1d, g0)
    dh, do, dgate, g_wo0, g_gate = _gate_bwd(dv0, wo0, h, o, proj0, y0, xb)
    dxr, g_wa, g_wx, dvec0, g_rnn = _rglru_bwd(dh, xc, h, proj0, cw0, wa, ba, wxg, bx, lam, xb)
    early = [_reshard([g_z, g_xbc, g_dt], MAP_W1, 644, bf, "reshard_w1"), g_wo0.astype(bf).reshape(N_DEV, 128, D_MODEL),
             (_reshard([g_rnn, g_gate], MAP_G0, 244, bf, "reshard_w0_early", shards=W0_EARLY), W0_EARLY)]
    dq, dk, dvv, (recv_w1, recv_wo0, recv_w0) = _flash_bwd(qc, kc, vc, o, do, lse, scatter=early)
    recv_early = {"ssd_w_in": recv_w1, "ssd_w_out": recv_wo1, "ab_w_out": recv_wo0, "ab_w_in": recv_w0}
    dtail, g_wq, g_wkv, dqnw, dknw, g_tail = _mla_bwd(dq, dk, dvv, proj0, qn, kn, qn_w, kn_w, wq, wkv, tc, tsa, tsb, xb)

    acc = {"g_rnn": g_rnn, "g_gate": g_gate, "g_tail": g_tail, "g_wq": g_wq, "g_wkv": g_wkv,
           "dvec0": dvec0, "g_wa": g_wa, "g_wx": g_wx, "dqnw": dqnw, "dknw": dknw, "dgb0": dgb0, "dvec1": dvec1,
           "dcw1": dcw1, "dnw": dnw, "dgb1": dgb1}
    late = [(_reshard([g_rnn, g_gate, g_tail], MAP_G0, 244, bf, "reshard_w0_late", shards=W0_LATE), W0_LATE),
            _reshard([g_wq], MAP_WQ, 96, bf, "reshard_wq"), _reshard([g_wkv], MAP_WKV, 128, bf, "reshard_wkv")]
    sm_slots, vec_rows, gates = _pack_small(dvec0, g_wa, g_wx, dqnw, dknw, dgb0, dvec1, dcw1, dnw, dgb1, loss8)
    dx, recv_late = _l0_dx(dxr, dgate, dtail, w0p, dv0, scatter=late + [sm_slots], bcast=[vec_rows, gates])
    return acc, recv_early, recv_late, loss8[0, 0], dx
```

```python
import math

import jax
import jax.numpy as jnp
from jax import lax
from jax.experimental import pallas as pl
from jax.experimental.pallas import tpu as pltpu

F32 = jnp.float32
MXU_DTYPE = jnp.bfloat16

N_DEV = 8
SEQ = 4096
D_MODEL = 1024
DN_ALPHA = 4.0 ** 0.25
RNN_W = 512
MLA_HEADS = 8
ATT_SCALE = 96.0 ** -0.5
ATT_C = ATT_SCALE * math.log2(math.e)
RG_C = 8.0
SSD_INNER = 2048
SSD_HEADS = 32
SSD_P = 64
SSD_GROUPS = 4
SSD_N = 128
SSD_L = 128
SSD_CONV = 3072
LANES = 128
SUBLANES = 8
VMEM_LIMIT = 56 * 1024 * 1024

ADAM_LR, ADAM_B1, ADAM_B2, ADAM_EPS, ADAM_WD, ADAM_STEP = 0.001, 0.9, 0.999, 1e-08, 0.01, 10

HIGHEST = lax.Precision.HIGHEST


def _params(sem, limit=VMEM_LIMIT):
    return pltpu.CompilerParams(dimension_semantics=sem, vmem_limit_bytes=limit)


def _dot(a, b):
    return lax.dot_general(a, b, (((1,), (0,)), ((), ())), preferred_element_type=F32)


def _dot_nt(a, b):
    return lax.dot_general(a, b, (((1,), (1,)), ((), ())), preferred_element_type=F32)


def _dot_tn(a, b):
    return lax.dot_general(a, b, (((0,), (0,)), ((), ())), preferred_element_type=F32)


def _dot_hi(a, b):
    return lax.dot_general(a, b, (((1,), (0,)), ((), ())), precision=HIGHEST, preferred_element_type=F32)


def _mx(v):
    return v.astype(MXU_DTYPE)


def _sigmoid(v):
    return 1.0 / (1.0 + jnp.exp(-v))


def _log1p_pos(e):
    poly = e * (1.0 - e * (0.5 - e * (1.0 / 3.0 - e * 0.25)))
    return jnp.where(e < 0.01, poly, jnp.log(1.0 + e))


def _softplus(v):
    return jnp.maximum(v, 0.0) + _log1p_pos(jnp.exp(-jnp.abs(v)))


def _neg_expm1(v):
    poly = -v * (1.0 + v * (0.5 + v * (1.0 / 6.0 + v * (1.0 / 24.0 + v * (1.0 / 120.0)))))
    return jnp.where(jnp.abs(v) < 0.1, poly, 1.0 - jnp.exp(v))


def _silu(v):
    return v * _sigmoid(v)


def _dsilu(v):
    s = _sigmoid(v)
    return s * (1.0 + v * (1.0 - s))


def _shift_down(blk, halo, s):
    if s == 0:
        return blk
    t = blk.shape[0]
    r = pltpu.roll(blk, s, 0)
    hr = pltpu.roll(halo, s, 0)
    row8 = lax.broadcasted_iota(jnp.int32, hr.shape, 0)
    head = jnp.where(row8 < s, hr, r[:SUBLANES])
    return jnp.concatenate([head, r[SUBLANES:]], axis=0) if t > SUBLANES else head


def _shift_up(blk, halo, s):
    if s == 0:
        return blk
    t = blk.shape[0]
    r = pltpu.roll(blk, t - s, 0)
    hr = pltpu.roll(halo, SUBLANES - s, 0)
    row8 = lax.broadcasted_iota(jnp.int32, hr.shape, 0)
    tail = jnp.where(row8 >= SUBLANES - s, hr, r[t - SUBLANES:])
    return jnp.concatenate([r[:t - SUBLANES], tail], axis=0) if t > SUBLANES else tail


def _scan_down(a, u):
    t = a.shape[0]
    row = lax.broadcasted_iota(jnp.int32, a.shape, 0)
    d = 1
    while d < t:
        keep = row >= d
        a_sh = jnp.where(keep, pltpu.roll(a, d, 0), 1.0)
        u_sh = jnp.where(keep, pltpu.roll(u, d, 0), 0.0)
        u = a * u_sh + u
        a = a * a_sh
        d *= 2
    return a, u


def _scan_up(a, u):
    t = a.shape[0]
    row = lax.broadcasted_iota(jnp.int32, a.shape, 0)
    d = 1
    while d < t:
        keep = row < t - d
        a_sh = jnp.where(keep, pltpu.roll(a, t - d, 0), 1.0)
        u_sh = jnp.where(keep, pltpu.roll(u, t - d, 0), 0.0)
        u = a * u_sh + u
        a = a * a_sh
        d *= 2
    return a, u


def _conv4(blk, halo, cw, cb):
    out = cb + blk * cw[3:4]
    for k in range(3):
        out = out + _shift_down(blk, halo, 3 - k) * cw[k:k + 1]
    return out


RG_T = 512
P0_RNN = 2


def _rg_gates(xc, wa, ba, wx, bx, lam):
    xcb = _mx(xc)
    r = _sigmoid(_dot(xcb, wa) + ba)
    ig = _sigmoid(_dot(xcb, wx) + bx)
    sp = _softplus(-lam)
    la = (-RG_C * r) * sp
    a = jnp.exp(la)
    mult = jnp.sqrt(_neg_expm1(2.0 * la))
    return r, ig, sp, a, mult


def _rglru_fwd(proj0, cw8, cb, wa, ba, wx, bx, lam):
    t, w = RG_T, RNN_W
    nb = SEQ // t

    def body(x_ref, halo_ref, cw_ref, cb_ref, wa_ref, ba_ref, wx_ref, bx_ref, lam_ref, xc_ref, h_ref, carry):
        i = pl.program_id(0)

        @pl.when(i == 0)
        def _():
            carry[...] = jnp.zeros_like(carry)

        blk = x_ref[...]
        halo = jnp.where(i > 0, halo_ref[...], 0.0)
        xc = _conv4(blk, halo, cw_ref[...], cb_ref[...])
        _, ig, _, a, mult = _rg_gates(xc, wa_ref[...], ba_ref[...], wx_ref[...], bx_ref[...], lam_ref[...])
        u = mult * (ig * xc)
        big_a, big_u = _scan_down(a, u)
        h = big_a * carry[SUBLANES - 1:SUBLANES, :] + big_u
        carry[...] = h[t - SUBLANES:]
        xc_ref[...] = xc
        h_ref[...] = h

    vec = pl.BlockSpec((1, w), lambda i: (0, 0))
    mat = pl.BlockSpec((w, w), lambda i: (0, 0))
    return pl.pallas_call(
        body, name="rglru_fwd", grid=(nb,),
        in_specs=[pl.BlockSpec((t, w), lambda i: (i, P0_RNN)),
                  pl.BlockSpec((SUBLANES, w), lambda i: (jnp.maximum(i * (t // SUBLANES) - 1, 0), P0_RNN)),
                  pl.BlockSpec((SUBLANES, w), lambda i: (0, 0)), vec, mat, vec, mat, vec, vec],
        out_specs=[pl.BlockSpec((t, w), lambda i: (i, 0)), pl.BlockSpec((t, w), lambda i: (i, 0))],
        out_shape=[jax.ShapeDtypeStruct((SEQ, w), F32), jax.ShapeDtypeStruct((SEQ, w), F32)],
        scratch_shapes=[pltpu.VMEM((SUBLANES, w), F32)],
        compiler_params=_params(("arbitrary",)),
    )(proj0, proj0, cw8, cb, wa, ba, wx, bx, lam)


def _rglru_bwd(dh, xc, h, proj0, cw8, wa, ba, wx, bx, lam, xb):
    t, w = RG_T, RNN_W
    nb = SEQ // t
    tb = t // SUBLANES

    def body(dh_ref, xc_ref, h_ref, hh_ref, x_ref, cw_ref, wa_ref, ba_ref, wx_ref, bx_ref, lam_ref, xb_ref,
             dx_ref, dwa_ref, dwx_ref, dvec_ref, gw_ref, gcarry, dxc_next):
        i = pl.program_id(0)
        rev = nb - 1 - i

        @pl.when(i == 0)
        def _():
            gcarry[...] = jnp.zeros_like(gcarry)
            dxc_next[...] = jnp.zeros_like(dxc_next)
            gw_ref[...] = jnp.zeros_like(gw_ref)
            dwa_ref[...] = jnp.zeros_like(dwa_ref)
            dwx_ref[...] = jnp.zeros_like(dwx_ref)
            dvec_ref[...] = jnp.zeros_like(dvec_ref)

        xc = xc_ref[...]
        wa_v, wx_v = wa_ref[...], wx_ref[...]
        lam_v = lam_ref[...]
        r, ig, sp, a, mult = _rg_gates(xc, wa_v, ba_ref[...], wx_v, bx_ref[...], lam_v)
        dhv = dh_ref[...]
        big_a, big_u = _scan_up(a, a * dhv)
        gg = big_a * gcarry[0:1, :] + big_u
        g = dhv + _shift_up(gg, gcarry[...], 1)
        gcarry[...] = gg[:SUBLANES]
        hhalo = jnp.where(rev > 0, hh_ref[...], 0.0)
        da = g * _shift_down(h_ref[...], hhalo, 1)
        d_mult = g * (ig * xc)
        d_i = g * (mult * xc)
        dxc = g * (mult * ig)
        d_la = da * a - d_mult * (a * a) / mult
        d_r = d_la * (-RG_C * sp)
        d_sp = jnp.sum(d_la * (-RG_C * r), axis=0, keepdims=True)
        d_pa = d_r * r * (1.0 - r)
        d_px = d_i * ig * (1.0 - ig)
        d_pab, d_pxb = _mx(d_pa), _mx(d_px)
        dxc = dxc + _dot_nt(d_pab, wa_v) + _dot_nt(d_pxb, wx_v)
        xcb = _mx(xc)
        dwa_ref[...] += _dot_tn(xcb, d_pab)
        dwx_ref[...] += _dot_tn(xcb, d_pxb)
        dvec_ref[0:1, :] += jnp.sum(d_pa, axis=0, keepdims=True)
        dvec_ref[1:2, :] += jnp.sum(d_px, axis=0, keepdims=True)
        dvec_ref[2:3, :] += d_sp * (-_sigmoid(-lam_v))
        dvec_ref[3:4, :] += jnp.sum(dxc, axis=0, keepdims=True)
        xblk = x_ref[...]
        cw = cw_ref[...]
        dx = dxc * cw[3:4]
        nxt = dxc_next[...]
        dvec_ref[7:8, :] += jnp.sum(dxc * xblk, axis=0, keepdims=True)
        for k in range(3):
            up = _shift_up(dxc, nxt, 3 - k)
            dvec_ref[4 + k:5 + k, :] += jnp.sum(up * xblk, axis=0, keepdims=True)
            dx = dx + up * cw[k:k + 1]
        dxc_next[...] = dxc[:SUBLANES]
        dxb = _mx(dx)
        dx_ref[...] = dxb
        gw_ref[...] += _dot_tn(xb_ref[...], dxb)

    blk = pl.BlockSpec((t, w), lambda i: (nb - 1 - i, 0))
    halo = pl.BlockSpec((SUBLANES, w), lambda i: (jnp.maximum((nb - 1 - i) * tb - 1, 0), 0))
    vec = pl.BlockSpec((1, w), lambda i: (0, 0))
    mat = pl.BlockSpec((w, w), lambda i: (0, 0))
    return pl.pallas_call(
        body, name="rglru_bwd", grid=(nb,),
        in_specs=[blk, blk, blk, halo, pl.BlockSpec((t, w), lambda i: (nb - 1 - i, P0_RNN)),
                  pl.BlockSpec((SUBLANES, w), lambda i: (0, 0)), mat, vec, mat, vec, vec,
                  pl.BlockSpec((t, D_MODEL), lambda i: (nb - 1 - i, 0))],
        out_specs=[blk, mat, mat, pl.BlockSpec((16, w), lambda i: (0, 0)), pl.BlockSpec((D_MODEL, w), lambda i: (0, 0))],
        out_shape=[jax.ShapeDtypeStruct((SEQ, w), MXU_DTYPE), jax.ShapeDtypeStruct((w, w), F32),
                   jax.ShapeDtypeStruct((w, w), F32), jax.ShapeDtypeStruct((16, w), F32),
                   jax.ShapeDtypeStruct((D_MODEL, w), F32)],
        scratch_shapes=[pltpu.VMEM((SUBLANES, w), F32), pltpu.VMEM((SUBLANES, w), F32)],
        compiler_params=_params(("arbitrary",)),
    )(dh, xc, h, h, proj0, cw8, wa, ba, wx, bx, lam, xb)


MLA_T = 512


def _rope(v, c, sa, sb):
    return v * c + pltpu.roll(v, LANES - 16, 1) * sa + pltpu.roll(v, 16, 1) * sb


def _rope_t(dv, c, sa, sb):
    return dv * c + pltpu.roll(dv * sa, 16, 1) + pltpu.roll(dv * sb, LANES - 16, 1)


def _rms(v, g, eps=1e-6):
    rs = lax.rsqrt(jnp.mean(v * v, axis=-1, keepdims=True) + eps)
    return v * rs * g, rs


def _mla_fwd(proj0, q_norm, kv_norm, wq, wkv, tc, tsa, tsb):
    t = MLA_T

    def body(cq_ref, ck_ref, qn_ref, kn_ref, wq_ref, wkv_ref, c_ref, sa_ref, sb_ref,
             oqn_ref, okn_ref, oq_ref, ok_ref, ov_ref):
        c, sa, sb = c_ref[...], sa_ref[...], sb_ref[...]
        ck = ck_ref[...]
        qn = _mx(_rms(cq_ref[...], qn_ref[...])[0])
        kn = _mx(_rms(ck[:, :LANES], kn_ref[...])[0])
        oqn_ref[...] = qn
        okn_ref[...] = kn
        krv = _rope(ck[:, LANES:], c, sa, sb)
        qraw = _dot(qn, wq_ref[...])
        kvraw = _dot(kn, wkv_ref[...])
        for hd in range(MLA_HEADS):
            sl = slice(hd * LANES, (hd + 1) * LANES)
            oq_ref[:, sl] = _mx(_rope(qraw[:, sl], c, sa, sb))
            ok_ref[:, sl] = _mx(kvraw[:, sl] + krv)
        ov_ref[...] = _mx(kvraw[:, 1024:])

    tab = pl.BlockSpec((t, LANES), lambda i: (i, 0))
    wide = pl.BlockSpec((t, 1024), lambda i: (i, 0))
    const = lambda shape: pl.BlockSpec(shape, lambda i: (0, 0))
    return pl.pallas_call(
        body, name="mla_fwd", grid=(SEQ // t,),
        in_specs=[pl.BlockSpec((t, 256), lambda i: (i, 6)), pl.BlockSpec((t, 256), lambda i: (i, 7)),
                  const((1, 256)), const((1, LANES)), const((256, 1024)), const((LANES, 1536)), tab, tab, tab],
        out_specs=[pl.BlockSpec((t, 256), lambda i: (i, 0)), tab, wide, wide, pl.BlockSpec((t, 512), lambda i: (i, 0))],
        out_shape=[jax.ShapeDtypeStruct((SEQ, 256), MXU_DTYPE), jax.ShapeDtypeStruct((SEQ, LANES), MXU_DTYPE),
                   jax.ShapeDtypeStruct((SEQ, 1024), MXU_DTYPE), jax.ShapeDtypeStruct((SEQ, 1024), MXU_DTYPE),
                   jax.ShapeDtypeStruct((SEQ, 512), MXU_DTYPE)],
        compiler_params=_params(("parallel",)),
    )(proj0, proj0, q_norm, kv_norm, wq, wkv, tc, tsa, tsb)


ATT_T = 1024


def _flash_fwd(q, k, v, bcast=()):
    t = ATT_T
    nb = SEQ // t

    steps = [(qi, ki) for qi in range(nb) for ki in range(qi + 1)]
    qi_tab = jnp.asarray([s[0] for s in steps], jnp.int32)
    ki_tab = jnp.asarray([s[1] for s in steps], jnp.int32)

    nx = len(bcast)

    def body(qi_ref, ki_ref, q_ref, k_ref, v_ref, *rest):
        x_refs, (o_ref, lse_ref), g_refs = rest[:nx], rest[nx:nx + 2], rest[nx + 2:2 * nx + 2]
        m_sc, acc_sc = rest[2 * nx + 2:2 * nx + 4]
        step = pl.program_id(1)
        qi, ki = qi_ref[step], ki_ref[step]
        if nx:
            copies = _peer_copies(x_refs, g_refs, rest[2 * nx + 4:], [])

            @pl.when((pl.program_id(0) == 0) & (step == 0))
            def _():
                for cp in copies:
                    cp.start()

        @pl.when(ki == 0)
        def _():
            m_sc[...] = jnp.full_like(m_sc, -jnp.inf)
            acc_sc[...] = jnp.zeros_like(acc_sc)

        def update(diagonal):
            vv = v_ref[...]
            lane_v = lax.broadcasted_iota(jnp.int32, vv.shape, 1)
            for hd in range(2):
                sl = slice(hd * LANES, (hd + 1) * LANES)
                s = _dot_nt(q_ref[:, sl], k_ref[:, sl])
                if diagonal:
                    s = jnp.where(lax.broadcasted_iota(jnp.int32, (t, t), 1)
                                  <= lax.broadcasted_iota(jnp.int32, (t, t), 0), s, -jnp.inf)
                m_prev = m_sc[hd]
                m_new = jnp.maximum(m_prev, jnp.max(s, axis=1, keepdims=True))
                p = jnp.exp2((s - m_new[:, :1]) * ATT_C)
                m_sc[hd] = m_new
                vh = jnp.where((lane_v >= hd * 64) & (lane_v < (hd + 1) * 64), vv, jnp.ones_like(vv))
                acc_sc[hd] = acc_sc[hd] * jnp.exp2((m_prev - m_new) * ATT_C) + _dot(_mx(p), vh)

        @pl.when(ki < qi)
        def _():
            update(False)

        @pl.when(ki == qi)
        def _():
            update(True)
            first = lax.broadcasted_iota(jnp.int32, (t, LANES), 1) < 64
            a0, a1 = acc_sc[0], acc_sc[1]
            l0, l1 = pltpu.roll(a0, 64, 1), pltpu.roll(a1, 64, 1)
            o_ref[...] = jnp.where(first, a0 / l0, a1 / l1)
            lse_ref[0] = jnp.where(first, m_sc[0] * ATT_SCALE + jnp.log(l0), m_sc[1] * ATT_SCALE + jnp.log(l1))

        if nx:
            @pl.when((pl.program_id(0) == 3) & (step == len(steps) - 1))
            def _():
                for cp in copies:
                    cp.wait()

    grid_spec = pltpu.PrefetchScalarGridSpec(
        num_scalar_prefetch=2, grid=(4, len(steps)),
        in_specs=[pl.BlockSpec((t, 256), lambda p, s, qt, kt: (qt[s], p)),
                  pl.BlockSpec((t, 256), lambda p, s, qt, kt: (kt[s], p)),
                  pl.BlockSpec((t, LANES), lambda p, s, qt, kt: (kt[s], p))] + [ANY] * nx,
        out_specs=[pl.BlockSpec((t, LANES), lambda p, s, qt, kt: (qt[s], p)),
                   pl.BlockSpec((1, t, LANES), lambda p, s, qt, kt: (p, qt[s], 0))] + [ANY] * nx,
        scratch_shapes=[pltpu.VMEM((2, t, LANES), F32), pltpu.VMEM((2, t, LANES), F32)]
        + (_exchange_sems(nx) if nx else []))
    res = pl.pallas_call(
        body, name="flash_fwd", grid_spec=grid_spec,
        out_shape=[jax.ShapeDtypeStruct((SEQ, 512), F32), jax.ShapeDtypeStruct((4, SEQ, LANES), F32)]
        + _exchange_shapes([], bcast),
        compiler_params=_params(("arbitrary", "arbitrary")),
    )(qi_tab, ki_tab, q, k, v, *bcast)
    return res[0], res[1], res[2:]


def _flash_bwd(q, k, v, o, do, lse, scatter=()):
    t = ATT_T
    nb = SEQ // t

    steps = [(qi, ki) for ki in range(nb) for qi in range(ki, nb)]
    qi_tab = jnp.asarray([s[0] for s in steps], jnp.int32)
    ki_tab = jnp.asarray([s[1] for s in steps], jnp.int32)
    log2e = math.log2(math.e)

    sc_arrays, sc_ranges = _scatter_args(scatter)
    nx = len(sc_arrays)

    def body(qi_ref, ki_ref, q_ref, k_ref, v_ref, o_ref, do_ref, lse_ref, *rest):
        x_refs, (dq_ref, dk_ref, dv_ref), g_refs = rest[:nx], rest[nx:nx + 3], rest[nx + 3:2 * nx + 3]
        dkt_sc, dvt_sc = rest[2 * nx + 3:2 * nx + 5]
        step = pl.program_id(1)
        qi, ki = qi_ref[step], ki_ref[step]
        if nx:
            copies = _peer_copies(x_refs, g_refs, rest[2 * nx + 5:], sc_ranges)

            @pl.when((pl.program_id(0) == 0) & (step == 0))
            def _():
                for cp in copies:
                    cp.start()

        @pl.when(step == 0)
        def _():
            dq_ref[...] = jnp.zeros_like(dq_ref)

        @pl.when(qi == ki)
        def _():
            dkt_sc[...] = jnp.zeros_like(dkt_sc)
            dvt_sc[...] = jnp.zeros_like(dvt_sc)

        def update(diagonal):
            dov, ov, vv = do_ref[...], o_ref[...], v_ref[...]
            lse2 = lse_ref[0] * log2e
            lane = lax.broadcasted_iota(jnp.int32, (t, LANES), 1)
            row_t = lax.broadcasted_iota(jnp.int32, (LANES, t), 0)
            prod = dov * ov
            do_b = _mx(dov)
            qrows = pl.ds(pl.multiple_of(qi * t, t), t)
            dvt_acc = jnp.zeros((LANES, t), F32)
            dkt_new, dq_new = [], []
            for hd in range(2):
                sl = slice(hd * LANES, (hd + 1) * LANES)
                mine = (lane >= hd * 64) & (lane < (hd + 1) * 64)
                qh, kh = q_ref[:, sl], k_ref[:, sl]
                p = jnp.exp2(_dot_nt(qh, kh) * ATT_C - lse2[:, hd * 64:hd * 64 + 1])
                if diagonal:
                    p = jnp.where(lax.broadcasted_iota(jnp.int32, (t, t), 1)
                                  <= lax.broadcasted_iota(jnp.int32, (t, t), 0), p, 0.0)
                do_h = jnp.where(mine, dov, 0.0)
                delta = jnp.sum(jnp.where(mine, prod, 0.0), axis=1, keepdims=True)
                dp = _dot_nt(_mx(do_h), vv)
                ds = _mx(p * (dp - delta) * ATT_SCALE)
                dvt_acc = dvt_acc + jnp.where((row_t >= hd * 64) & (row_t < (hd + 1) * 64), _dot_tn(do_b, _mx(p)), 0.0)
                dkt_new.append(_dot_tn(qh, ds))
                dq_new.append(_dot(ds, kh))
            for hd in range(2):
                sl = slice(hd * LANES, (hd + 1) * LANES)
                dkt_sc[sl, :] += dkt_new[hd]
                dq_ref[qrows, sl] += dq_new[hd]
            dvt_sc[...] += dvt_acc

        @pl.when(qi > ki)
        def _():
            update(False)

        @pl.when(qi == ki)
        def _():
            update(True)

        @pl.when(qi == nb - 1)
        def _():
            dk_ref[...] = dkt_sc[...].T
            dv_ref[...] = dvt_sc[...].T

        if nx:
            @pl.when((pl.program_id(0) == 3) & (step == len(steps) - 1))
            def _():
                for cp in copies:
                    cp.wait()

    qmap = lambda p, s, qt, kt: (qt[s], p)
    kmap = lambda p, s, qt, kt: (kt[s], p)
    grid_spec = pltpu.PrefetchScalarGridSpec(
        num_scalar_prefetch=2, grid=(4, len(steps)),
        in_specs=[pl.BlockSpec((t, 256), qmap), pl.BlockSpec((t, 256), kmap), pl.BlockSpec((t, LANES), kmap),
                  pl.BlockSpec((t, LANES), qmap), pl.BlockSpec((t, LANES), qmap),
                  pl.BlockSpec((1, t, LANES), lambda p, s, qt, kt: (p, qt[s], 0))] + [ANY] * nx,
        out_specs=[pl.BlockSpec((SEQ, 256), lambda p, s, qt, kt: (0, p)), pl.BlockSpec((t, 256), kmap),
                   pl.BlockSpec((t, LANES), kmap)] + [ANY] * nx,
        scratch_shapes=[pltpu.VMEM((256, t), F32), pltpu.VMEM((LANES, t), F32)] + (_exchange_sems(nx) if nx else []))
    res = pl.pallas_call(
        body, name="flash_bwd", grid_spec=grid_spec,
        out_shape=[jax.ShapeDtypeStruct((SEQ, 1024), F32), jax.ShapeDtypeStruct((SEQ, 1024), F32),
                   jax.ShapeDtypeStruct((SEQ, 512), F32)] + _exchange_shapes(sc_arrays, []),
        compiler_params=_params(("arbitrary", "arbitrary")),
    )(qi_tab, ki_tab, q, k, v, o, do, lse, *sc_arrays)
    return res[0], res[1], res[2], res[3:]


def _rms_bwd(v, g, dy, eps=1e-6):
    rs = lax.rsqrt(jnp.mean(v * v, axis=-1, keepdims=True) + eps)
    xh = v * rs
    dxh = dy * g
    dv = rs * (dxh - xh * jnp.mean(dxh * xh, axis=-1, keepdims=True))
    return dv, jnp.sum(dy * xh, axis=0, keepdims=True)


def _mla_bwd(dq, dk, dv, proj0, qlat, klat, q_norm, kv_norm, wq, wkv, tc, tsa, tsb, xb):
    t = MLA_T

    def body(dq_ref, dk_ref, dv_ref, cq_ref, ck_ref, ql_ref, kl_ref, qn_ref, kn_ref, wq_ref, wkv_ref,
             c_ref, sa_ref, sb_ref, xb_ref, o_ref, gwq_ref, gwkv_ref, dgq_ref, dgk_ref, gwt_ref, oq_ref, okv_ref):
        @pl.when(pl.program_id(0) == 0)
        def _():
            dgq_ref[...] = jnp.zeros_like(dgq_ref)
            dgk_ref[...] = jnp.zeros_like(dgk_ref)
            gwq_ref[...] = jnp.zeros_like(gwq_ref)
            gwkv_ref[...] = jnp.zeros_like(gwkv_ref)
            gwt_ref[...] = jnp.zeros_like(gwt_ref)

        c, sa, sb = c_ref[...], sa_ref[...], sb_ref[...]
        lane = lax.broadcasted_iota(jnp.int32, (t, LANES), 1)
        dkr = jnp.zeros((t, LANES), F32)
        for hd in range(MLA_HEADS):
            sl = slice(hd * LANES, (hd + 1) * LANES)
            oq_ref[:, sl] = _mx(_rope_t(dq_ref[:, sl], c, sa, sb))
            dkh = dk_ref[:, sl]
            okv_ref[:, sl] = _mx(dkh)
            dkr = dkr + dkh
        okv_ref[:, 1024:] = _mx(dv_ref[...])
        dkr = _rope_t(jnp.where((lane >= 64) & (lane < 96), dkr, 0.0), c, sa, sb)
        dqraw, dkvraw = oq_ref[...], okv_ref[...]
        gwq_ref[...] += _dot_tn(ql_ref[...], dqraw)
        gwkv_ref[...] += _dot_tn(kl_ref[...], dkvraw)
        dqn = _dot_nt(dqraw, wq_ref[...])
        dkn = _dot_nt(dkvraw, wkv_ref[...])
        dcq, dgq = _rms_bwd(cq_ref[...], qn_ref[...], dqn)
        dck, dgk = _rms_bwd(ck_ref[:, :LANES], kn_ref[...], dkn)
        o_ref[:, :256] = _mx(dcq)
        o_ref[:, 256:384] = _mx(dck)
        o_ref[:, 384:] = _mx(dkr)
        gwt_ref[...] += _dot_tn(xb_ref[...], o_ref[...])
        dgq_ref[0:1, :] += dgq
        dgk_ref[0:1, :] += dgk

    tab = pl.BlockSpec((t, LANES), lambda i: (i, 0))
    wide = pl.BlockSpec((t, 1024), lambda i: (i, 0))
    const = lambda shape: pl.BlockSpec(shape, lambda i: (0, 0))
    return pl.pallas_call(
        body, name="mla_bwd", grid=(SEQ // t,),
        in_specs=[wide, wide, pl.BlockSpec((t, 512), lambda i: (i, 0)),
                  pl.BlockSpec((t, 256), lambda i: (i, 6)), pl.BlockSpec((t, 256), lambda i: (i, 7)),
                  pl.BlockSpec((t, 256), lambda i: (i, 0)), tab,
                  const((1, 256)), const((1, LANES)), const((256, 1024)), const((LANES, 1536)), tab, tab, tab, wide],
        out_specs=[pl.BlockSpec((t, 512), lambda i: (i, 0)), const((256, 1024)), const((LANES, 1536)),
                   const((SUBLANES, 256)), const((SUBLANES, LANES)), const((D_MODEL, 512))],
        out_shape=[jax.ShapeDtypeStruct((SEQ, 512), MXU_DTYPE), jax.ShapeDtypeStruct((256, 1024), F32),
                   jax.ShapeDtypeStruct((LANES, 1536), F32), jax.ShapeDtypeStruct((SUBLANES, 256), F32),
                   jax.ShapeDtypeStruct((SUBLANES, LANES), F32), jax.ShapeDtypeStruct((D_MODEL, 512), F32)],
        scratch_shapes=[pltpu.VMEM((t, 1024), MXU_DTYPE), pltpu.VMEM((t, 1536), MXU_DTYPE)],
        compiler_params=_params(("arbitrary",)),
    )(dq, dk, dv, proj0, proj0, qlat, klat, q_norm, kv_norm, wq, wkv, tc, tsa, tsb, xb)


LN_T = 512


def _ln(v, g, b, eps=1e-5):
    mu = jnp.mean(v, axis=-1, keepdims=True)
    xc = v - mu
    rs = lax.rsqrt(jnp.mean(xc * xc, axis=-1, keepdims=True) + eps)
    return xc * rs * g + b


def _ln_bwd(v, g, dy, eps=1e-5):
    mu = jnp.mean(v, axis=-1, keepdims=True)
    xc = v - mu
    rs = lax.rsqrt(jnp.mean(xc * xc, axis=-1, keepdims=True) + eps)
    xh = xc * rs
    dxh = dy * g
    dv = rs * (dxh - jnp.mean(dxh, axis=-1, keepdims=True) - xh * jnp.mean(dxh * xh, axis=-1, keepdims=True))
    return dv, jnp.sum(dy * xh, axis=0, keepdims=True), jnp.sum(dy, axis=0, keepdims=True)


def _l0_out(h, o, proj0, x, w_out, g, b):
    t = LN_T

    def body(h_ref, o_ref, ga_ref, gb_ref, x_ref, w_ref, g_ref, b_ref, y_ref, v_ref, x1_ref, x1b_ref):
        y = _mx(jnp.concatenate([h_ref[...] * _silu(ga_ref[...]), o_ref[...] * _silu(gb_ref[...])], axis=1))
        v = DN_ALPHA * x_ref[...] + _dot(y, w_ref[...])
        y_ref[...] = y
        v_ref[...] = v
        x1 = _ln(v, g_ref[...], b_ref[...])
        x1_ref[...] = x1
        x1b_ref[...] = _mx(x1)

    half = pl.BlockSpec((t, 512), lambda i: (i, 0))
    full = pl.BlockSpec((t, D_MODEL), lambda i: (i, 0))
    vec = pl.BlockSpec((1, D_MODEL), lambda i: (0, 0))
    return pl.pallas_call(
        body, name="l0_out", grid=(SEQ // t,),
        in_specs=[half, half, pl.BlockSpec((t, 512), lambda i: (i, 0)), pl.BlockSpec((t, 512), lambda i: (i, 1)), full,
                  pl.BlockSpec((D_MODEL, D_MODEL), lambda i: (0, 0)), vec, vec],
        out_specs=[full, full, full, full],
        out_shape=[jax.ShapeDtypeStruct((SEQ, D_MODEL), MXU_DTYPE), jax.ShapeDtypeStruct((SEQ, D_MODEL), F32),
                   jax.ShapeDtypeStruct((SEQ, D_MODEL), F32), jax.ShapeDtypeStruct((SEQ, D_MODEL), MXU_DTYPE)],
        compiler_params=_params(("parallel",)),
    )(h, o, proj0, proj0, x, w_out, g, b)


def _l1_in(x1b, w1z, w1d):
    t = 1024

    def body(x_ref, wz_ref, wd_ref, z_ref, dt_ref):
        xv = x_ref[...]
        z_ref[...] = _dot(xv, wz_ref[...])
        dt_ref[...] = _dot(xv, wd_ref[...])

    rows = lambda w: pl.BlockSpec((t, w), lambda i: (i, 0))
    const = lambda w: pl.BlockSpec((D_MODEL, w), lambda i: (0, 0))
    return pl.pallas_call(
        body, name="l1_in", grid=(SEQ // t,),
        in_specs=[rows(D_MODEL), const(SSD_INNER), const(LANES)],
        out_specs=[rows(SSD_INNER), rows(LANES)],
        out_shape=[jax.ShapeDtypeStruct((SEQ, SSD_INNER), F32), jax.ShapeDtypeStruct((SEQ, LANES), F32)],
        compiler_params=_params(("parallel",)),
    )(x1b, w1z, w1d)


def _l1_dx_ln(dz, dxbc, ddt, dv1, v0, w1z, w1x, w1d, g):
    t = LN_T

    def body(dz_ref, dx_ref, ddt_ref, dv1_ref, v_ref, wz_ref, wx_ref, wd_ref, g_ref, dv_ref, dgb_ref):
        @pl.when(pl.program_id(0) == 0)
        def _():
            dgb_ref[...] = jnp.zeros_like(dgb_ref)

        dy = (DN_ALPHA * dv1_ref[...] + _dot_nt(dz_ref[...], wz_ref[...]) + _dot_nt(dx_ref[...], wx_ref[...])
              + _dot_nt(_mx(ddt_ref[...]), wd_ref[...]))
        dv, dg, db = _ln_bwd(v_ref[...], g_ref[...], dy)
        dv_ref[...] = dv
        dgb_ref[0:1, :] += dg
        dgb_ref[1:2, :] += db

    rows = lambda w: pl.BlockSpec((t, w), lambda i: (i, 0))
    const = lambda w: pl.BlockSpec((D_MODEL, w), lambda i: (0, 0))
    return pl.pallas_call(
        body, name="l1_dx_ln", grid=(SEQ // t,),
        in_specs=[rows(SSD_INNER), rows(SSD_CONV), rows(LANES), rows(D_MODEL), rows(D_MODEL),
                  const(SSD_INNER), const(SSD_CONV), const(LANES), pl.BlockSpec((1, D_MODEL), lambda i: (0, 0))],
        out_specs=[rows(D_MODEL), pl.BlockSpec((SUBLANES, D_MODEL), lambda i: (0, 0))],
        out_shape=[jax.ShapeDtypeStruct((SEQ, D_MODEL), F32), jax.ShapeDtypeStruct((SUBLANES, D_MODEL), F32)],
        compiler_params=_params(("arbitrary",)),
    )(dz, dxbc, ddt, dv1, v0, w1z, w1x, w1d, g)


def _gate_bwd(dv0, w_out, h, o, proj0, y0, xb):
    t = LN_T

    def body(dv_ref, w_ref, h_ref, o_ref, ga_ref, gb_ref, y0_ref, xb_ref, dh_ref, do_ref, dg_ref, gwo_ref, gwg_ref):
        @pl.when(pl.program_id(0) == 0)
        def _():
            gwo_ref[...] = jnp.zeros_like(gwo_ref)
            gwg_ref[...] = jnp.zeros_like(gwg_ref)

        dvb = _mx(dv_ref[...])
        dy = _dot_nt(dvb, w_ref[...])
        ga, gb, dya, dyb = ga_ref[...], gb_ref[...], dy[:, :512], dy[:, 512:]
        dh_ref[...] = dya * _silu(ga)
        do_ref[...] = dyb * _silu(gb)
        dg_ref[:, :512] = _mx(dya * h_ref[...] * _dsilu(ga))
        dg_ref[:, 512:] = _mx(dyb * o_ref[...] * _dsilu(gb))
        gwo_ref[...] += _dot_tn(y0_ref[...], dvb)
        gwg_ref[...] += _dot_tn(xb_ref[...], dg_ref[...])

    half = pl.BlockSpec((t, 512), lambda i: (i, 0))
    half1 = pl.BlockSpec((t, 512), lambda i: (i, 1))
    full = pl.BlockSpec((t, 1024), lambda i: (i, 0))
    square = pl.BlockSpec((D_MODEL, D_MODEL), lambda i: (0, 0))
    return pl.pallas_call(
        body, name="gate_bwd", grid=(SEQ // t,),
        in_specs=[full, square, half, half, half, half1, full, full],
        out_specs=[half, half, full, square, square],
        out_shape=[jax.ShapeDtypeStruct((SEQ, 512), F32), jax.ShapeDtypeStruct((SEQ, 512), F32),
                   jax.ShapeDtypeStruct((SEQ, 1024), MXU_DTYPE), jax.ShapeDtypeStruct((D_MODEL, D_MODEL), F32),
                   jax.ShapeDtypeStruct((D_MODEL, D_MODEL), F32)],
        compiler_params=_params(("arbitrary",)),
    )(dv0, w_out, h, o, proj0, proj0, y0, xb)


CONV_T = 1024
CONV_CB = 1024


def _ssd_conv_fwd(x1b, w1x, cw8, cb):
    t, cbk = CONV_T, CONV_CB

    def body(x_ref, w_ref, cw_ref, cb_ref, xbc_ref, pre_ref, act_ref, carry):
        xbc = _dot(x_ref[...], w_ref[...])
        halo = jnp.where(pl.program_id(1) > 0, carry[...], 0.0)
        pre = _conv4(xbc, halo, cw_ref[...], cb_ref[...])
        carry[...] = xbc[t - SUBLANES:]
        xbc_ref[...] = xbc
        pre_ref[...] = pre
        act_ref[...] = _silu(pre)

    blk = pl.BlockSpec((t, cbk), lambda j, i: (i, j))
    out = jax.ShapeDtypeStruct((SEQ, SSD_CONV), F32)
    return pl.pallas_call(
        body, name="ssd_conv_fwd", grid=(SSD_CONV // cbk, SEQ // t),
        in_specs=[pl.BlockSpec((t, D_MODEL), lambda j, i: (i, 0)), pl.BlockSpec((D_MODEL, cbk), lambda j, i: (0, j)),
                  pl.BlockSpec((SUBLANES, cbk), lambda j, i: (0, j)), pl.BlockSpec((1, cbk), lambda j, i: (0, j))],
        out_specs=[blk, blk, blk], out_shape=[out, out, out],
        scratch_shapes=[pltpu.VMEM((SUBLANES, cbk), F32)],
        compiler_params=_params(("parallel", "arbitrary")),
    )(x1b, w1x, cw8, cb)


def _ssd_conv_bwd(dact, pre, xbc, cw8, x1b):
    t, cbk = CONV_T, CONV_CB
    tb = t // SUBLANES
    nb = SEQ // t

    def body(da_ref, dan_ref, pre_ref, pren_ref, x_ref, cw_ref, x1_ref, dx_ref, dcw_ref, gw_ref):
        i = pl.program_id(1)

        @pl.when(i == 0)
        def _():
            dcw_ref[...] = jnp.zeros_like(dcw_ref)
            gw_ref[...] = jnp.zeros_like(gw_ref)

        dpre = da_ref[...] * _dsilu(pre_ref[...])
        dpre_next = jnp.where(i < nb - 1, dan_ref[...] * _dsilu(pren_ref[...]), 0.0)
        xblk = x_ref[...]
        cw = cw_ref[...]
        dx = dpre * cw[3:4]
        dcw_ref[3:4, :] += jnp.sum(dpre * xblk, axis=0, keepdims=True)
        for k in range(3):
            up = _shift_up(dpre, dpre_next, 3 - k)
            dcw_ref[k:k + 1, :] += jnp.sum(up * xblk, axis=0, keepdims=True)
            dx = dx + up * cw[k:k + 1]
        dcw_ref[4:5, :] += jnp.sum(dpre, axis=0, keepdims=True)
        dxb = _mx(dx)
        dx_ref[...] = dxb
        gw_ref[...] += _dot_tn(x1_ref[...], dxb)

    blk = pl.BlockSpec((t, cbk), lambda j, i: (i, j))
    nxt = pl.BlockSpec((SUBLANES, cbk), lambda j, i: (jnp.minimum((i + 1) * tb, SEQ // SUBLANES - 1), j))
    acc = pl.BlockSpec((SUBLANES, cbk), lambda j, i: (0, j))
    return pl.pallas_call(
        body, name="ssd_conv_bwd", grid=(SSD_CONV // cbk, nb),
        in_specs=[blk, nxt, blk, nxt, blk, acc, pl.BlockSpec((t, D_MODEL), lambda j, i: (i, 0))],
        out_specs=[blk, acc, pl.BlockSpec((D_MODEL, cbk), lambda j, i: (0, j))],
        out_shape=[jax.ShapeDtypeStruct((SEQ, SSD_CONV), MXU_DTYPE), jax.ShapeDtypeStruct((SUBLANES, SSD_CONV), F32),
                   jax.ShapeDtypeStruct((D_MODEL, SSD_CONV), F32)],
        compiler_params=_params(("parallel", "arbitrary")),
    )(dact, dact, pre, pre, xbc, cw8, x1b)


def _ssd_common(dt_raw, bias, alog, tril, expand_t, xs):
    lane = lax.broadcasted_iota(jnp.int32, dt_raw.shape, 1)
    dt = jnp.where(lane < SSD_HEADS, _softplus(dt_raw + bias), 0.0)
    a_neg = -jnp.exp(alog)
    cs = _dot_hi(tril, dt * a_neg)
    dt_x = _expand_heads(dt, expand_t)
    ecs_x = _expand_heads(jnp.exp(cs), expand_t)
    ds_x = _expand_heads(jnp.exp(cs[SSD_L - 1:SSD_L, :] - cs), expand_t)
    return dt, a_neg, cs, dt_x, None, xs * dt_x, ds_x, ecs_x, ecs_x[SSD_L - 1:SSD_L, :]


def _expand_heads(v, expand_t):
    hi = v.astype(jnp.bfloat16)
    lo = (v - hi.astype(F32)).astype(jnp.bfloat16)
    return _dot_nt(hi, expand_t) + _dot_nt(lo, expand_t)


def _fold_heads(v, expand_t):
    hi = v.astype(jnp.bfloat16)
    lo = (v - hi.astype(F32)).astype(jnp.bfloat16)
    return _dot(hi, expand_t) + _dot(lo, expand_t)


def _ssd_decay(cs, cs_t, hh, causal):
    seg = cs[:, hh:hh + 1] - cs_t[hh:hh + 1, :]
    return jnp.where(causal, jnp.exp(jnp.where(causal, seg, 0.0)), 0.0)


def _ssd_scan_fwd(act, dt_raw, bias, alog, d_x, tril, expand_t, bcast=()):
    nc = SEQ // SSD_L
    gw = SSD_INNER // SSD_GROUPS
    n = len(bcast)

    def body(act_ref, dt_ref, bias_ref, alog_ref, dx_ref, tril_ref, et_ref, *rest):
        y_ref, hp_ref, h_sc = rest[n], rest[n + 1], rest[2 * n + 2]
        if n:
            copies = _peer_copies(rest[:n], rest[n + 2:2 * n + 2], rest[2 * n + 3:], [])

            @pl.when(pl.program_id(0) == 0)
            def _():
                for cp in copies:
                    cp.start()

            @pl.when(pl.program_id(0) == nc - 1)
            def _():
                for cp in copies:
                    cp.wait()

        @pl.when(pl.program_id(0) == 0)
        def _():
            h_sc[...] = jnp.zeros_like(h_sc)

        xs = act_ref[:, :SSD_INNER]
        _, _, cs, _, _, xdt, ds_x, ecs_x, elast = _ssd_common(
            dt_ref[...], bias_ref[...], alog_ref[...], tril_ref[...], et_ref[...], xs)
        cs_t = cs.T
        causal = (lax.broadcasted_iota(jnp.int32, (SSD_L, SSD_L), 0)
                  >= lax.broadcasted_iota(jnp.int32, (SSD_L, SSD_L), 1))
        lane = lax.broadcasted_iota(jnp.int32, (SSD_L, LANES), 1)
        xdt_b = _mx(xdt)
        xds_b = _mx(xdt * ds_x)
        hp_ref[0] = h_sc[...]
        for g in range(SSD_GROUPS):
            gs = slice(g * gw, (g + 1) * gw)
            bg = _mx(act_ref[:, SSD_INNER + g * SSD_N:SSD_INNER + (g + 1) * SSD_N])
            cg = _mx(act_ref[:, SSD_INNER + 512 + g * SSD_N:SSD_INNER + 512 + (g + 1) * SSD_N])
            cb = _dot_nt(cg, bg)
            hprev = h_sc[:, gs]
            yoff = _dot(cg, _mx(hprev)) * ecs_x[:, gs]
            h_sc[:, gs] = hprev * elast[:, gs] + _dot_tn(bg, xds_b[:, gs])
            for pr in range(4):
                ps = slice(g * gw + pr * LANES, g * gw + (pr + 1) * LANES)
                xp = xdt_b[:, ps]
                ydiag = jnp.zeros((SSD_L, LANES), F32)
                for j in range(2):
                    dm = _ssd_decay(cs, cs_t, g * 8 + pr * 2 + j, causal)
                    mine = (lane >= j * 64) & (lane < (j + 1) * 64)
                    ydiag = ydiag + _dot(_mx(cb * dm), jnp.where(mine, xp, jnp.zeros_like(xp)))
                y_ref[:, ps] = ydiag + yoff[:, pr * LANES:(pr + 1) * LANES] + dx_ref[:, ps] * xs[:, ps]

    const = lambda shape: pl.BlockSpec(shape, lambda c: (0, 0))
    res = pl.pallas_call(
        body, name="ssd_scan_fwd", grid=(nc,),
        in_specs=[pl.BlockSpec((SSD_L, SSD_CONV), lambda c: (c, 0)), pl.BlockSpec((SSD_L, LANES), lambda c: (c, 0)),
                  const((1, LANES)), const((1, LANES)), const((1, SSD_INNER)), const((SSD_L, SSD_L)),
                  const((SSD_INNER, LANES))] + [ANY] * n,
        out_specs=[pl.BlockSpec((SSD_L, SSD_INNER), lambda c: (c, 0)),
                   pl.BlockSpec((1, SSD_N, SSD_INNER), lambda c: (c, 0, 0))] + [ANY] * n,
        out_shape=[jax.ShapeDtypeStruct((SEQ, SSD_INNER), F32), jax.ShapeDtypeStruct((nc, SSD_N, SSD_INNER), F32)]
        + _exchange_shapes([], bcast),
        scratch_shapes=[pltpu.VMEM((SSD_N, SSD_INNER), F32)] + (_exchange_sems(n) if n else []),
        compiler_params=_params(("arbitrary",)),
    )(act, dt_raw, bias, alog, d_x, tril, expand_t, *bcast)
    return res[0], res[1], res[2:]


def _ssd_scan_bwd(dy, act, dt_raw, hprev_all, bias, alog, d_x, tril, expand_t, x1b, scatter=()):
    nc = SEQ // SSD_L
    gw = SSD_INNER // SSD_GROUPS
    sc_arrays, sc_ranges = _scatter_args(scatter)
    nx = len(sc_arrays)

    def body(dy_ref, act_ref, dt_ref, hp_ref, bias_ref, alog_ref, dx_ref, tril_ref, et_ref, x1_ref, *rest):
        dact_ref, ddt_ref, dvec_ref, gdt_ref = rest[nx:nx + 4]
        dh_sc, dd_sc, dcs_sc, dcst_sc = rest[2 * nx + 4:2 * nx + 8]
        i = pl.program_id(0)
        if nx:
            copies = _peer_copies(rest[:nx], rest[nx + 4:2 * nx + 4], rest[2 * nx + 8:], sc_ranges)

            @pl.when(i == 0)
            def _():
                for cp in copies:
                    cp.start()

        @pl.when(i == 0)
        def _():
            dh_sc[...] = jnp.zeros_like(dh_sc)
            dd_sc[...] = jnp.zeros_like(dd_sc)
            gdt_ref[...] = jnp.zeros_like(gdt_ref)
            dvec_ref[...] = jnp.zeros_like(dvec_ref)

        xs = act_ref[:, :SSD_INNER]
        dt_raw_v, bias_v = dt_ref[...], bias_ref[...]
        dt, a_neg, cs, dt_x, _, xdt, ds_x, ecs_x, elast = _ssd_common(
            dt_raw_v, bias_v, alog_ref[...], tril_ref[...], et_ref[...], xs)
        cs_t = cs.T
        rowi = lax.broadcasted_iota(jnp.int32, (SSD_L, SSD_L), 0)
        coli = lax.broadcasted_iota(jnp.int32, (SSD_L, SSD_L), 1)
        causal = rowi >= coli
        lane = lax.broadcasted_iota(jnp.int32, (SSD_L, LANES), 1)
        row_g = lax.broadcasted_iota(jnp.int32, (SSD_L, gw), 0)
        dyv = dy_ref[...]
        dd_sc[0:1, :] += jnp.sum(dyv * xs, axis=0, keepdims=True)
        xdt_b = _mx(xdt)
        xds = xdt * ds_x
        xds_b = _mx(xds)
        dy_b = _mx(dyv)
        dye_b = _mx(dyv * ecs_x)
        dcs_sc[...] = jnp.zeros_like(dcs_sc)
        dcst_sc[...] = jnp.zeros_like(dcst_sc)
        dcs_parts = []
        dxdt_parts = []
        for g in range(SSD_GROUPS):
            gs = slice(g * gw, (g + 1) * gw)
            bcol = slice(SSD_INNER + g * SSD_N, SSD_INNER + (g + 1) * SSD_N)
            ccol = slice(SSD_INNER + 512 + g * SSD_N, SSD_INNER + 512 + (g + 1) * SSD_N)
            bg, cg = _mx(act_ref[:, bcol]), _mx(act_ref[:, ccol])
            cb = _dot_nt(cg, bg)
            hp = hp_ref[0, :, gs]
            hp_b = _mx(hp)
            dh = dh_sc[:, gs]
            dh_b = _mx(dh)
            yoff = _dot(cg, hp_b) * ecs_x[:, gs]
            bdh = _dot(bg, dh_b)
            tt = xds[:, gs] * bdh
            last_row = (jnp.sum(tt, axis=0, keepdims=True)
                        + jnp.sum(dh * hp, axis=0, keepdims=True) * elast[:, gs])
            dcs_parts.append(dyv[:, gs] * yoff - tt + jnp.where(row_g == SSD_L - 1, last_row, 0.0))
            dc_g = _dot_nt(dye_b[:, gs], hp_b)
            db_g = _dot_nt(xds_b[:, gs], dh_b)
            dh_sc[:, gs] = _dot_tn(cg, dye_b[:, gs]) + dh * elast[:, gs]
            wsum = jnp.zeros((SSD_L, SSD_L), F32)
            dxdt_g = []
            for pr in range(4):
                ps = slice(g * gw + pr * LANES, g * gw + (pr + 1) * LANES)
                xp, dyp = xdt_b[:, ps], dy_b[:, ps]
                dxp = jnp.zeros((SSD_L, LANES), F32)
                for j in range(2):
                    hh = g * 8 + pr * 2 + j
                    dm = _ssd_decay(cs, cs_t, hh, causal)
                    mine = (lane >= j * 64) & (lane < (j + 1) * 64)
                    dy_h = jnp.where(mine, dyp, jnp.zeros_like(dyp))
                    wd = _dot_nt(dy_h, xp) * dm
                    wsum = wsum + wd
                    gmat = wd * cb
                    dcs_sc[:, hh:hh + 1] = jnp.sum(gmat, axis=1, keepdims=True)
                    dcst_sc[hh:hh + 1, :] = -jnp.sum(gmat, axis=0, keepdims=True)
                    dxp = dxp + _dot_tn(_mx(cb * dm), dy_h)
                dxdt_g.append(dxp)
            dxdt_parts.append(jnp.concatenate(dxdt_g, axis=1) + bdh * ds_x[:, gs])
            ws_b = _mx(wsum)
            dact_ref[:, ccol] = dc_g + _dot(ws_b, bg)
            dact_ref[:, bcol] = db_g + _dot_tn(ws_b, cg)
        dxdt = jnp.concatenate(dxdt_parts, axis=1)
        dcs_x = jnp.concatenate(dcs_parts, axis=1)
        et = et_ref[...]
        dcs_tot = dcs_sc[...] + dcst_sc[...].T + _fold_heads(dcs_x, et)
        da_dt = _dot_hi((coli >= rowi).astype(F32), dcs_tot)
        ddt = da_dt * a_neg + _fold_heads(dxdt * xs, et)
        ddt_raw = ddt * _sigmoid(dt_raw_v + bias_v)
        ddt_ref[...] = ddt_raw
        gdt_ref[...] += _dot_tn(x1_ref[...], _mx(ddt_raw))
        dvec_ref[0:1, :] += jnp.sum(ddt_raw, axis=0, keepdims=True)
        dvec_ref[1:2, :] += jnp.sum(da_dt * dt, axis=0, keepdims=True) * a_neg
        dact_ref[:, :SSD_INNER] = dyv * dx_ref[...] + dxdt * dt_x

        @pl.when(i == nc - 1)
        def _():
            dvec_ref[2:3, :] = _fold_heads(dd_sc[...], et)[0:1, :]
            if nx:
                for cp in copies:
                    cp.wait()

    const = lambda shape: pl.BlockSpec(shape, lambda c: (0, 0))
    rev = lambda c: (nc - 1 - c, 0)
    res = pl.pallas_call(
        body, name="ssd_scan_bwd", grid=(nc,),
        in_specs=[pl.BlockSpec((SSD_L, SSD_INNER), rev), pl.BlockSpec((SSD_L, SSD_CONV), rev),
                  pl.BlockSpec((SSD_L, LANES), rev),
                  pl.BlockSpec((1, SSD_N, SSD_INNER), lambda c: (nc - 1 - c, 0, 0)),
                  const((1, LANES)), const((1, LANES)), const((1, SSD_INNER)), const((SSD_L, SSD_L)),
                  const((SSD_INNER, LANES)), pl.BlockSpec((SSD_L, D_MODEL), rev)] + [ANY] * nx,
        out_specs=[pl.BlockSpec((SSD_L, SSD_CONV), rev), pl.BlockSpec((SSD_L, LANES), rev), const((SUBLANES, LANES)),
                   const((D_MODEL, LANES))] + [ANY] * nx,
        out_shape=[jax.ShapeDtypeStruct((SEQ, SSD_CONV), F32), jax.ShapeDtypeStruct((SEQ, LANES), F32),
                   jax.ShapeDtypeStruct((SUBLANES, LANES), F32), jax.ShapeDtypeStruct((D_MODEL, LANES), F32)]
        + _exchange_shapes(sc_arrays, []),
        scratch_shapes=[pltpu.VMEM((SSD_N, SSD_INNER), F32), pltpu.VMEM((SUBLANES, SSD_INNER), F32),
                        pltpu.VMEM((SSD_L, LANES), F32), pltpu.VMEM((LANES, SSD_L), F32)]
        + (_exchange_sems(nx) if nx else []),
        compiler_params=_params(("arbitrary",)),
    )(dy, act, dt_raw, hprev_all, bias, alog, d_x, tril, expand_t, x1b, *sc_arrays)
    return res[0], res[1], res[2], res[3], res[4:]


L1_T = 512


def _resident(shape):
    return pl.BlockSpec(shape, lambda i: (0, 0), pipeline_mode=pl.Buffered(1))


def _gated_norm(y, z, nw):
    y2 = y * _silu(z)
    gw = SSD_INNER // SSD_GROUPS
    outs, xhs, rss = [], [], []
    for g in range(SSD_GROUPS):
        gs = slice(g * gw, (g + 1) * gw)
        v = y2[:, gs]
        rs = lax.rsqrt(jnp.mean(v * v, axis=-1, keepdims=True) + 1e-6)
        xhs.append(v * rs)
        rss.append(rs)
        outs.append(v * rs * nw[:, gs])
    return outs, xhs, rss


def _l1_out(y, z, nw, w_out, x1, g, b, target):
    t = L1_T

    def body(y_ref, z_ref, nw_ref, w_ref, x1_ref, g_ref, b_ref, tg_ref, dv_ref, dgb_ref, loss_ref, gw_ref):
        @pl.when(pl.program_id(0) == 0)
        def _():
            dgb_ref[...] = jnp.zeros_like(dgb_ref)
            loss_ref[...] = jnp.zeros_like(loss_ref)
            gw_ref[...] = jnp.zeros_like(gw_ref)

        outs, _, _ = _gated_norm(y_ref[...], z_ref[...], nw_ref[...])
        yn = _mx(jnp.concatenate(outs, axis=1))
        v = DN_ALPHA * x1_ref[...] + _dot(yn, w_ref[...])
        gv = g_ref[...]
        err = _ln(v, gv, b_ref[...]) - tg_ref[...]
        rowsum = jnp.sum(err * err, axis=1, keepdims=True)
        loss_ref[...] += 0.5 * jnp.sum(rowsum, axis=0, keepdims=True) / D_MODEL
        dv, dg, db = _ln_bwd(v, gv, err / D_MODEL)
        dv_ref[...] = dv
        dgb_ref[0:1, :] += dg
        dgb_ref[1:2, :] += db
        gw_ref[...] += _dot_tn(yn, _mx(dv))

    wide = pl.BlockSpec((t, SSD_INNER), lambda i: (i, 0))
    full = pl.BlockSpec((t, D_MODEL), lambda i: (i, 0))
    vec = pl.BlockSpec((1, D_MODEL), lambda i: (0, 0))
    return pl.pallas_call(
        body, name="l1_out", grid=(SEQ // t,),
        in_specs=[wide, wide, pl.BlockSpec((1, SSD_INNER), lambda i: (0, 0)),
                  _resident((SSD_INNER, D_MODEL)), full, vec, vec, full],
        out_specs=[full, pl.BlockSpec((SUBLANES, D_MODEL), lambda i: (0, 0)),
                   pl.BlockSpec((SUBLANES, LANES), lambda i: (0, 0)), _resident((SSD_INNER, D_MODEL))],
        out_shape=[jax.ShapeDtypeStruct((SEQ, D_MODEL), F32), jax.ShapeDtypeStruct((SUBLANES, D_MODEL), F32),
                   jax.ShapeDtypeStruct((SUBLANES, LANES), F32), jax.ShapeDtypeStruct((SSD_INNER, D_MODEL), F32)],
        compiler_params=_params(("arbitrary",)),
    )(y, z, nw, w_out, x1, g, b, target)


def _l1_gate_bwd(dv1, w_out, y, z, nw, x1b):
    t = L1_T
    gw = SSD_INNER // SSD_GROUPS

    def body(dv_ref, w_ref, y_ref, z_ref, nw_ref, x1_ref, dy_ref, dz_ref, dnw_ref, gw_ref):
        @pl.when(pl.program_id(0) == 0)
        def _():
            dnw_ref[...] = jnp.zeros_like(dnw_ref)
            gw_ref[...] = jnp.zeros_like(gw_ref)

        dyn = _dot_nt(_mx(dv_ref[...]), w_ref[...])
        yv, zv, nwv = y_ref[...], z_ref[...], nw_ref[...]
        _, xhs, rss = _gated_norm(yv, zv, nwv)
        sz, dsz = _silu(zv), _dsilu(zv)
        for g in range(SSD_GROUPS):
            gs = slice(g * gw, (g + 1) * gw)
            d_out = dyn[:, gs]
            xh = xhs[g]
            dnw_ref[0:1, gs] += jnp.sum(d_out * xh, axis=0, keepdims=True)
            dxh = d_out * nwv[:, gs]
            dy2 = rss[g] * (dxh - xh * jnp.mean(dxh * xh, axis=-1, keepdims=True))
            dy_ref[:, gs] = dy2 * sz[:, gs]
            dz_ref[:, gs] = _mx(dy2 * yv[:, gs] * dsz[:, gs])
        gw_ref[...] += _dot_tn(x1_ref[...], dz_ref[...])

    wide = pl.BlockSpec((t, SSD_INNER), lambda i: (i, 0))
    return pl.pallas_call(
        body, name="l1_gate_bwd", grid=(SEQ // t,),
        in_specs=[pl.BlockSpec((t, D_MODEL), lambda i: (i, 0)), _resident((SSD_INNER, D_MODEL)),
                  wide, wide, pl.BlockSpec((1, SSD_INNER), lambda i: (0, 0)), pl.BlockSpec((t, D_MODEL), lambda i: (i, 0))],
        out_specs=[wide, wide, pl.BlockSpec((SUBLANES, SSD_INNER), lambda i: (0, 0)),
                   _resident((D_MODEL, SSD_INNER))],
        out_shape=[jax.ShapeDtypeStruct((SEQ, SSD_INNER), F32), jax.ShapeDtypeStruct((SEQ, SSD_INNER), MXU_DTYPE),
                   jax.ShapeDtypeStruct((SUBLANES, SSD_INNER), F32), jax.ShapeDtypeStruct((D_MODEL, SSD_INNER), F32)],
        compiler_params=_params(("arbitrary",)),
    )(dv1, w_out, y, z, nw, x1b)


MESH = pl.DeviceIdType.MESH
ANY = pl.BlockSpec(memory_space=pl.ANY)


def _flip(v, bit):
    return 1 - v if bit else v


def _all_gather(blocks, name):
    n = len(blocks)

    def body(*refs):
        x_refs, out_refs = refs[:n], refs[n:2 * n]
        send_sems, recv_sems, local_sems = refs[2 * n:]
        mx, my, mc = lax.axis_index("x"), lax.axis_index("y"), lax.axis_index("c")
        me, sibling = (mx, my, mc), (mx, my, 1 - mc)
        chips = [(1 - mx, my), (mx, 1 - my), (1 - mx, 1 - my)]

        def copy(a, k, block, to, own=False):
            px, py, pc = block
            slot = out_refs[a].at[4 * px + 2 * py + pc]
            return pltpu.make_async_remote_copy(
                src_ref=x_refs[a] if own else slot, dst_ref=slot,
                send_sem=send_sems.at[7 * a + k], recv_sem=recv_sems.at[7 * a + k], device_id=to, device_id_type=MESH)

        mine = [pltpu.make_async_copy(x_refs[a], out_refs[a].at[4 * mx + 2 * my + mc], local_sems.at[a])
                for a in range(n)]
        first = []
        for a in range(n):
            mine[a].start()
            first.append(copy(a, 0, me, sibling, own=True))
            first += [copy(a, 1 + j, me, (*chip, mc), own=True) for j, chip in enumerate(chips)]
        for cp in first:
            cp.start()
        passed = []
        for j, chip in enumerate(chips):
            for a in range(n):
                copy(a, 1 + j, (*chip, mc), me).wait_recv()
                fwd = copy(a, 4 + j, (*chip, mc), sibling)
                fwd.start()
                passed.append(fwd)
        for a in range(n):
            copy(a, 0, sibling, me).wait_recv()
            for j, chip in enumerate(chips):
                copy(a, 4 + j, (*chip, 1 - mc), me).wait_recv()
        for cp in first + passed:
            cp.wait_send()
        for cp in mine:
            cp.wait()

    return pl.pallas_call(
        body, name=name, in_specs=[ANY] * n, out_specs=[ANY] * n,
        out_shape=[jax.ShapeDtypeStruct((N_DEV,) + b.shape, b.dtype) for b in blocks],
        scratch_shapes=[pltpu.SemaphoreType.DMA((7 * n,)), pltpu.SemaphoreType.DMA((7 * n,)),
                        pltpu.SemaphoreType.DMA((n,))],
    )(*blocks)


def _l0_in(x, w0p, bcast=()):
    n = len(bcast)
    tm, tn = 1024, 1024
    gi, gj = SEQ // tm, 2048 // tn

    def body(x_ref, w_ref, *rest):
        o_ref, xb_ref = rest[n], rest[n + 1]
        i, j = pl.program_id(0), pl.program_id(1)
        if n:
            copies = _peer_copies(rest[:n], rest[n + 2:2 * n + 2], rest[2 * n + 2:], [])

            @pl.when((i == 0) & (j == 0))
            def _():
                for cp in copies:
                    cp.start()

        xb = _mx(x_ref[...])
        xb_ref[...] = xb
        o_ref[...] = _dot(xb, w_ref[...])

        if n:
            @pl.when((i == gi - 1) & (j == gj - 1))
            def _():
                for cp in copies:
                    cp.wait()

    res = pl.pallas_call(
        body, name="l0_in", grid=(gi, gj),
        in_specs=[pl.BlockSpec((tm, D_MODEL), lambda i, j: (i, 0)), pl.BlockSpec((D_MODEL, tn), lambda i, j: (0, j))]
        + [ANY] * n,
        out_specs=[pl.BlockSpec((tm, tn), lambda i, j: (i, j)), pl.BlockSpec((tm, D_MODEL), lambda i, j: (i, 0))]
        + [ANY] * n,
        out_shape=[jax.ShapeDtypeStruct((SEQ, 2048), F32), jax.ShapeDtypeStruct((SEQ, D_MODEL), MXU_DTYPE)]
        + _exchange_shapes([], bcast),
        scratch_shapes=_exchange_sems(n) if n else [],
        compiler_params=_params(("arbitrary", "arbitrary")),
    )(x, w0p, *bcast)
    return res[0], res[1], res[2:]


def _l0_dx(dxr, dgate, dtail, w0p, dv0, scatter=(), bcast=()):
    arrays, ranges = _scatter_args(scatter)
    n = len(arrays) + len(bcast)
    tm = 1024
    steps = SEQ // tm

    def body(dxr_ref, dg_ref, dt_ref, w_ref, dv_ref, *rest):
        o_ref = rest[n]
        i = pl.program_id(0)
        if n:
            copies = _peer_copies(rest[:n], rest[n + 1:2 * n + 1], rest[2 * n + 1:], ranges)

            @pl.when(i == 0)
            def _():
                for cp in copies:
                    cp.start()

        o_ref[...] = (DN_ALPHA * dv_ref[...] + _dot_nt(dg_ref[...], w_ref[:, 0:1024])
                      + _dot_nt(dxr_ref[...], w_ref[:, 1024:1536]) + _dot_nt(dt_ref[...], w_ref[:, 1536:2048]))

        if n:
            @pl.when(i == steps - 1)
            def _():
                for cp in copies:
                    cp.wait()

    rows = lambda w: pl.BlockSpec((tm, w), lambda i: (i, 0))
    res = pl.pallas_call(
        body, name="l0_dx", grid=(steps,),
        in_specs=[rows(512), rows(1024), rows(512), pl.BlockSpec((D_MODEL, 2048), lambda i: (0, 0)), rows(D_MODEL)]
        + [ANY] * n,
        out_specs=[rows(D_MODEL)] + [ANY] * n,
        out_shape=[jax.ShapeDtypeStruct((SEQ, D_MODEL), F32)] + _exchange_shapes(arrays, bcast),
        scratch_shapes=_exchange_sems(n) if n else [],
        compiler_params=_params(("arbitrary",)),
    )(dxr, dgate, dtail, w0p, dv0, *arrays, *bcast)
    return res[0], res[1:]


def _scatter_args(scatter):
    arrays = [s[0] if isinstance(s, tuple) else s for s in scatter]
    ranges = [s[1] if isinstance(s, tuple) else (0, N_DEV) for s in scatter]
    return arrays, ranges


def _exchange_shapes(scatter, bcast):
    return ([jax.ShapeDtypeStruct((N_DEV,) + a.shape[1:], a.dtype) for a in scatter]
            + [jax.ShapeDtypeStruct((N_DEV,) + a.shape, a.dtype) for a in bcast])


def _exchange_sems(n):
    return [pltpu.SemaphoreType.DMA((7 * n,)), pltpu.SemaphoreType.DMA((7 * n,)), pltpu.SemaphoreType.DMA((n,))]


class _GuardedCopy:
    def __init__(self, copy, send=None, recv=None, local=False):
        self.copy, self.send, self.recv, self.local = copy, send, recv, local

    @staticmethod
    def _run(pred, fn):
        if pred is None:
            fn()
        else:
            pl.when(pred)(fn)

    def start(self):
        self._run(self.send, self.copy.start)

    def wait(self):
        if self.local:
            self._run(self.send, self.copy.wait)
        else:
            self._run(self.send, self.copy.wait_send)
            self._run(self.recv, self.copy.wait_recv)


def _peer_copies(in_refs, out_refs, sems, ranges):
    send_sems, recv_sems, local_sems = sems
    n, ns = len(in_refs), len(ranges)
    mx, my, mc = lax.axis_index("x"), lax.axis_index("y"), lax.axis_index("c")
    me = 4 * mx + 2 * my + mc

    def src(a, slot):
        return in_refs[a].at[slot - ranges[a][0]] if a < ns else in_refs[a]

    def member(a, dev):
        if a >= ns or ranges[a] == (0, N_DEV):
            return None
        return (dev >= ranges[a][0]) & (dev < ranges[a][1])

    copies = [_GuardedCopy(pltpu.make_async_copy(src(a, me), out_refs[a].at[me], local_sems.at[a]),
                           send=member(a, me), local=True) for a in range(n)]
    for k in range(1, N_DEV):
        px, py, pc = _flip(mx, (k >> 2) & 1), _flip(my, (k >> 1) & 1), _flip(mc, k & 1)
        peer = 4 * px + 2 * py + pc
        for a in range(n):
            copies.append(_GuardedCopy(pltpu.make_async_remote_copy(
                src_ref=src(a, peer), dst_ref=out_refs[a].at[me],
                send_sem=send_sems.at[7 * a + k - 1], recv_sem=recv_sems.at[7 * a + k - 1],
                device_id=(px, py, pc), device_id_type=MESH), send=member(a, peer), recv=member(a, me)))
    return copies


def _segments(col_map, width):
    segs = []
    for lo, hi, arr, alo in col_map:
        for s in range(N_DEV):
            a, b = max(lo, s * width), min(hi, (s + 1) * width)
            if a < b:
                segs.append((s, a - s * width, b - a, arr, alo + a - lo))
    return segs


COPY_ROWS = 256


def _unshard(g8, col_map, widths, name):
    _, r, w = g8.shape
    rb = min(r, COPY_ROWS)
    segs = _segments(col_map, w)

    def body(g_ref, *o_refs):
        for o_ref in o_refs:
            o_ref[...] = jnp.zeros_like(o_ref)
        for s, llo, n, arr, alo in segs:
            o_refs[arr][:, alo:alo + n] = g_ref[s, :, llo:llo + n]

    return pl.pallas_call(
        body, name=name, grid=(r // rb,),
        in_specs=[pl.BlockSpec((N_DEV, rb, w), lambda i: (0, i, 0))],
        out_specs=[pl.BlockSpec((rb, n), lambda i: (i, 0)) for n in widths],
        out_shape=[jax.ShapeDtypeStruct((r, n), g8.dtype) for n in widths],
        compiler_params=_params(("parallel",)),
    )(g8)


def _reshard(srcs, col_map, w, dtype, name, shards=(0, N_DEV)):
    r = srcs[0].shape[0]
    rb = min(r, COPY_ROWS)
    lo, hi = shards
    segs = [sg for sg in _segments(col_map, w) if lo <= sg[0] < hi]

    def body(*refs):
        o_ref = refs[-1]
        for s, llo, n, arr, alo in segs:
            o_ref[s - lo, :, llo:llo + n] = refs[arr][:, alo:alo + n].astype(dtype)

    return pl.pallas_call(
        body, name=name, grid=(r // rb,),
        in_specs=[pl.BlockSpec((rb, a.shape[1]), lambda i: (i, 0)) for a in srcs],
        out_specs=pl.BlockSpec((hi - lo, rb, w), lambda i: (0, i, 0)),
        out_shape=jax.ShapeDtypeStruct((hi - lo, r, w), dtype),
        compiler_params=_params(("parallel",)),
    )(*srcs)


def _adamw(parts, w, m, v, name):
    r, c = w.shape
    tr = COPY_ROWS if r % COPY_ROWS == 0 else r

    def body(p_ref, w_ref, m_ref, v_ref, g_ref, d_ref, mo_ref, vo_ref):
        g = p_ref[0].astype(F32)
        for s in range(1, N_DEV):
            g = g + p_ref[s].astype(F32)
        g_ref[...] = g
        d_ref[...], mo_ref[...], vo_ref[...] = _adamw_math(g, w_ref[...], m_ref[...], v_ref[...])

    blk = pl.BlockSpec((tr, c), lambda i: (i, 0))
    out = jax.ShapeDtypeStruct((r, c), F32)
    return pl.pallas_call(
        body, name=name, grid=(r // tr,),
        in_specs=[pl.BlockSpec((N_DEV, tr, c), lambda i: (0, i, 0)), blk, blk, blk],
        out_specs=[blk, blk, blk, blk], out_shape=[out, out, out, out],
        compiler_params=_params(("parallel",)),
    )(parts, w, m, v)


def _adamw_math(g, w, m, v):
    mn = ADAM_B1 * m + (1.0 - ADAM_B1) * g
    vn = ADAM_B2 * v + (1.0 - ADAM_B2) * (g * g)
    m_hat = mn / (1.0 - ADAM_B1 ** ADAM_STEP)
    v_hat = vn / (1.0 - ADAM_B2 ** ADAM_STEP)
    return -ADAM_LR * (m_hat / (jnp.sqrt(v_hat) + ADAM_EPS) + ADAM_WD * w), mn, vn


SMALL = (("ab_conv_w", 0, 4, 64), ("ssd_conv_w", 4, 4, 384), ("ssd_conv_b", 8, 1, 384), ("ssd_norm", 9, 1, 256),
         ("ssd_ln_g", 10, 1, 128), ("ssd_ln_b", 11, 1, 128))
VECS = (("ab_conv_b", 512), ("ab_gate_a_b", 512), ("ab_gate_x_b", 512), ("ab_lambda", 512), ("mla_q_norm", 256),
        ("mla_kv_norm", 128), ("ab_ln_g", 1024), ("ab_ln_b", 1024), ("ssd_dt_bias", 32), ("ssd_a_log", 32),
        ("ssd_d", 32))
GATES = ("ab_gate_a_w", "ab_gate_x_w")
SMALL_NAMES = tuple(n for n, *_ in SMALL) + tuple(n for n, _ in VECS) + GATES
VMEM_WHOLE = pl.BlockSpec(memory_space=pltpu.VMEM)


def _view2d(name, a):
    if name in GATES:
        return a.reshape(RNN_W, 64)
    return a[0] if a.ndim == 3 else a


def _unshard_small(g):
    widths = (512, 3072, 3072, 2048, 1024, 1024)

    def body(*refs):
        ins, outs = refs[:6], refs[6:]
        outs[0][...] = jnp.zeros_like(outs[0])
        outs[1][...] = jnp.zeros_like(outs[1])
        for (_, _, nr, c), i_ref, o_ref in zip(SMALL, ins, outs):
            for j in range(N_DEV):
                o_ref[0:nr, j * c:(j + 1) * c] = i_ref[j]

    return pl.pallas_call(
        body, name="unshard_small", in_specs=[VMEM_WHOLE] * 6, out_specs=[VMEM_WHOLE] * 6,
        out_shape=[jax.ShapeDtypeStruct((SUBLANES if nr == 4 else 1, w), F32) for (_, _, nr, _), w in zip(SMALL, widths)],
    )(*g)


def _prep_repl(ga, gx, dt_bias, a_log, d):
    def body(ga_ref, gx_ref, b_ref, al_ref, d_ref, wa_ref, wx_ref, b128_ref, al128_ref, dx_ref):
        wa_ref[...] = jnp.zeros_like(wa_ref)
        wx_ref[...] = jnp.zeros_like(wx_ref)
        for hd in range(8):
            hs = slice(hd * 64, (hd + 1) * 64)
            wa_ref[hs, hs] = _mx(ga_ref[hs, :])
            wx_ref[hs, hs] = _mx(gx_ref[hs, :])
        b128_ref[...] = jnp.zeros_like(b128_ref)
        al128_ref[...] = jnp.zeros_like(al128_ref)
        b128_ref[:, 0:SSD_HEADS] = b_ref[...]
        al128_ref[:, 0:SSD_HEADS] = al_ref[...]
        dv = d_ref[...]
        for hd in range(SSD_HEADS):
            dx_ref[:, hd * SSD_P:(hd + 1) * SSD_P] = jnp.broadcast_to(dv[:, hd:hd + 1], (1, SSD_P))

    return pl.pallas_call(
        body, name="prep_repl", in_specs=[VMEM_WHOLE] * 5, out_specs=[VMEM_WHOLE] * 5,
        out_shape=[jax.ShapeDtypeStruct((RNN_W, RNN_W), MXU_DTYPE), jax.ShapeDtypeStruct((RNN_W, RNN_W), MXU_DTYPE),
                   jax.ShapeDtypeStruct((1, LANES), F32), jax.ShapeDtypeStruct((1, LANES), F32),
                   jax.ShapeDtypeStruct((1, SSD_INNER), F32)],
    )(ga, gx, dt_bias, a_log, d)


LOSS_ROW = 11


def _pack_small(dvec0, g_wa, g_wx, dqnw, dknw, dgb0, dvec1, dcw1, dnw, dgb1, loss8):
    def body(dvec0_ref, gwa_ref, gwx_ref, dqn_ref, dkn_ref, dgb0_ref, dvec1_ref, dcw1_ref, dnw_ref, dgb1_ref,
             loss_ref, sm_ref, vec_ref, gg_ref):
        sm_ref[...] = jnp.zeros_like(sm_ref)
        vec_ref[...] = jnp.zeros_like(vec_ref)
        sharded = ((dvec0_ref, 4), (dcw1_ref, 0), (dcw1_ref, 4), (dnw_ref, 0), (dgb1_ref, 0), (dgb1_ref, 1))
        for (_, r0, nr, c), (src, sr) in zip(SMALL, sharded):
            for j in range(N_DEV):
                sm_ref[j, r0:r0 + nr, 0:c] = src[sr:sr + nr, j * c:(j + 1) * c]
        vectors = ((dvec0_ref, 3), (dvec0_ref, 0), (dvec0_ref, 1), (dvec0_ref, 2), (dqn_ref, 0), (dkn_ref, 0),
                   (dgb0_ref, 0), (dgb0_ref, 1), (dvec1_ref, 0), (dvec1_ref, 1), (dvec1_ref, 2))
        for row, ((_, c), (src, sr)) in enumerate(zip(VECS, vectors)):
            vec_ref[row:row + 1, 0:c] = src[sr:sr + 1, 0:c]
        vec_ref[LOSS_ROW:LOSS_ROW + 1, 0:LANES] = loss_ref[0:1, :]
        for hd in range(8):
            hs = slice(hd * 64, (hd + 1) * 64)
            gg_ref[hs, 0:64] = _mx(gwa_ref[hs, hs])
            gg_ref[hs, 64:128] = _mx(gwx_ref[hs, hs])

    return pl.pallas_call(
        body, name="pack_small", in_specs=[VMEM_WHOLE] * 11, out_specs=[VMEM_WHOLE] * 3,
        out_shape=[jax.ShapeDtypeStruct((N_DEV, 16, 384), F32), jax.ShapeDtypeStruct((16, 1024), F32),
                   jax.ShapeDtypeStruct((RNN_W, LANES), MXU_DTYPE)],
    )(dvec0, g_wa, g_wx, dqnw, dknw, dgb0, dvec1, dcw1, dnw, dgb1, loss8)


def _adamw_small(recv_sm, recv_vec, recv_gg, wmv):
    plan = ([(0, r0, nr, c) for _, r0, nr, c in SMALL] + [(1, row, 1, c) for row, (_, c) in enumerate(VECS)]
            + [(2, 0, RNN_W, 0), (2, 0, RNN_W, 64)])
    n = len(plan)

    def body(*refs):
        recv, ins, outs = refs[:3], refs[3:3 + 3 * n], refs[3 + 3 * n:]
        for i, (src, r0, nr, c) in enumerate(plan):
            cols = slice(c, c + 64) if src == 2 else slice(0, c)
            g = recv[src][0, r0:r0 + nr, cols].astype(F32)
            for s in range(1, N_DEV):
                g = g + recv[src][s, r0:r0 + nr, cols].astype(F32)
            w_ref, m_ref, v_ref = ins[3 * i:3 * i + 3]
            outs[4 * i][...] = g
            outs[4 * i + 1][...], outs[4 * i + 2][...], outs[4 * i + 3][...] = _adamw_math(
                g, w_ref[...], m_ref[...], v_ref[...])
        loss = recv[1][0, LOSS_ROW:LOSS_ROW + 1, 0:LANES]
        for s in range(1, N_DEV):
            loss = loss + recv[1][s, LOSS_ROW:LOSS_ROW + 1, 0:LANES]
        outs[4 * n][...] = loss

    flat = [a for t in wmv for a in t]
    return pl.pallas_call(
        body, name="adamw_small", in_specs=[VMEM_WHOLE] * (3 + 3 * n), out_specs=[VMEM_WHOLE] * (4 * n + 1),
        out_shape=[jax.ShapeDtypeStruct(t[0].shape, F32) for t in wmv for _ in range(4)]
        + [jax.ShapeDtypeStruct((1, LANES), F32)],
    )(recv_sm, recv_vec, recv_gg, *flat)


BIG_L0 = ("ab_w_in", "ab_w_out", "mla_w_uq", "mla_w_ukv")
BIG_L1 = ("ssd_w_in", "ssd_w_out")

MAP_W0 = ((0, 512, 0, 1024), (512, 1536, 0, 0), (1536, 1920, 0, 1536), (1920, 1952, 0, 1984))
MAP_W1 = ((0, 2048, 0, 0), (2048, 5120, 1, 0), (5120, 5152, 2, 0))
MAP_WQ = tuple((96 * hd, 96 * hd + 96, 0, 128 * hd) for hd in range(8))
MAP_WKV = (tuple((128 * hd, 128 * hd + 64, 0, 128 * hd) for hd in range(8))
           + tuple((128 * hd + 64, 128 * hd + 128, 0, 1024 + 64 * hd) for hd in range(8)))
MAP_G0 = ((0, 512, 0, 0), (512, 1536, 1, 0), (1536, 1920, 2, 0), (1920, 1952, 2, 448))
W0_EARLY, W0_LATE = (0, 6), (6, 8)


def kernel(x, positions, ab_w_in, ab_conv_w, ab_conv_b, ab_gate_a_w, ab_gate_a_b, ab_gate_x_w, ab_gate_x_b, ab_lambda, mla_q_norm, mla_kv_norm, mla_w_uq, mla_w_ukv, ab_w_out, ab_ln_g, ab_ln_b, ssd_w_in, ssd_conv_w, ssd_conv_b, ssd_dt_bias, ssd_a_log, ssd_d, ssd_norm, ssd_w_out, ssd_ln_g, ssd_ln_b, loss_target, m_ab_w_in, m_ab_conv_w, m_ab_conv_b, m_ab_gate_a_w, m_ab_gate_a_b, m_ab_gate_x_w, m_ab_gate_x_b, m_ab_lambda, m_mla_q_norm, m_mla_kv_norm, m_mla_w_uq, m_mla_w_ukv, m_ab_w_out, m_ab_ln_g, m_ab_ln_b, m_ssd_w_in, m_ssd_conv_w, m_ssd_conv_b, m_ssd_dt_bias, m_ssd_a_log, m_ssd_d, m_ssd_norm, m_ssd_w_out, m_ssd_ln_g, m_ssd_ln_b, v_ab_w_in, v_ab_conv_w, v_ab_conv_b, v_ab_gate_a_w, v_ab_gate_a_b, v_ab_gate_x_w, v_ab_gate_x_b, v_ab_lambda, v_mla_q_norm, v_mla_kv_norm, v_mla_w_uq, v_mla_w_ukv, v_ab_w_out, v_ab_ln_g, v_ab_ln_b, v_ssd_w_in, v_ssd_conv_w, v_ssd_conv_b, v_ssd_dt_bias, v_ssd_a_log, v_ssd_d, v_ssd_norm, v_ssd_w_out, v_ssd_ln_g, v_ssd_ln_b):
    args = dict(locals())
    bf = MXU_DTYPE
    big = {n: [args[pre + n][0] for pre in ("", "m_", "v_")] for n in BIG_L0 + BIG_L1}
    sml = {n: [_view2d(n, args[pre + n]) for pre in ("", "m_", "v_")] for n in SMALL_NAMES}

    w0_8, cw0_8 = _all_gather([big["ab_w_in"][0].astype(bf), sml["ab_conv_w"][0]], "gather_params")
    p = {"cw0_8": cw0_8, "l0_blocks": [big[n][0].astype(bf) for n in BIG_L0[1:]] + [sml[n][0] for n, *_ in SMALL[1:]]}
    p["w0p"], = _unshard(w0_8, MAP_W0, (2048,), "unshard_w0")
    p["wa"], p["wx"], p["dt_bias"], p["a_log"], p["d_x"] = _prep_repl(
        sml["ab_gate_a_w"][0], sml["ab_gate_x_w"][0], sml["ssd_dt_bias"][0], sml["ssd_a_log"][0], sml["ssd_d"][0])
    for key, n in (("cb0", "ab_conv_b"), ("ba", "ab_gate_a_b"), ("bx", "ab_gate_x_b"), ("lam", "ab_lambda"),
                   ("qn_w", "mla_q_norm"), ("kn_w", "mla_kv_norm"), ("g0", "ab_ln_g"), ("b0", "ab_ln_b")):
        p[key] = sml[n][0]

    _, recv_early, recv, _, grad_x = _local_step(
        x[0], positions[0], loss_target[0], p, [big[n][0].astype(bf) for n in BIG_L1])

    me = 4 * lax.axis_index("x") + 2 * lax.axis_index("y") + lax.axis_index("c")
    parts = dict(recv_early, ab_w_in=jnp.where(me >= W0_LATE[0], recv[0], recv_early["ab_w_in"]),
                 mla_w_uq=recv[1], mla_w_ukv=recv[2])

    outs = {}
    kinds = ("grad", "delta", "new_m", "new_v")
    for n in BIG_L0 + BIG_L1:
        for kind, res in zip(kinds, _adamw(parts[n], *big[n], "adamw_" + n)):
            outs[kind, n] = res[None]
    res = _adamw_small(*recv[3:], [sml[n] for n in SMALL_NAMES])
    for i, n in enumerate(SMALL_NAMES):
        for k, kind in enumerate(kinds):
            outs[kind, n] = res[4 * i + k].reshape(args[n].shape)

    loss = res[4 * len(SMALL_NAMES)][0, 0]
    order = ["ab_w_in", "ab_conv_w", "ab_conv_b", "ab_gate_a_w", "ab_gate_a_b", "ab_gate_x_w", "ab_gate_x_b",
             "ab_lambda", "mla_q_norm", "mla_kv_norm", "mla_w_uq", "mla_w_ukv", "ab_w_out", "ab_ln_g", "ab_ln_b",
             "ssd_w_in", "ssd_conv_w", "ssd_conv_b", "ssd_dt_bias", "ssd_a_log", "ssd_d", "ssd_norm", "ssd_w_out",
             "ssd_ln_g", "ssd_ln_b"]
    return (loss, grad_x[None], *[outs[kind, n] for kind in ("grad", "delta", "new_m", "new_v") for n in order])


def _local_step(x, pos, target, p, l1_blocks):
    bf = MXU_DTYPE
    inv_freq = 10000.0 ** (-jnp.arange(0, 32, 2, dtype=F32) / 32)
    ang = pos.astype(F32)[:, None] * inv_freq
    cos, sin = jnp.cos(ang), jnp.sin(ang)
    zeros = lambda n: jnp.zeros((SEQ, n), F32)
    tc = jnp.concatenate([jnp.ones((SEQ, 64), F32), cos, cos, zeros(32)], axis=1)
    tsa = jnp.concatenate([zeros(64), -sin, zeros(48)], axis=1)
    tsb = jnp.concatenate([zeros(80), sin, zeros(32)], axis=1)

    w0p, wa, wxg = (p[k] for k in ("w0p", "wa", "wx"))
    cb0, ba, bx, lam = (p[k] for k in ("cb0", "ba", "bx", "lam"))
    qn_w, kn_w, g0, b0 = (p[k] for k in ("qn_w", "kn_w", "g0", "b0"))
    dt_bias, a_log, d_x = (p[k] for k in ("dt_bias", "a_log", "d_x"))
    tril = jnp.tril(jnp.ones((SSD_L, SSD_L), F32))
    expand_t = (jnp.arange(SSD_INNER)[:, None] // SSD_P == jnp.arange(LANES)[None, :]).astype(jnp.bfloat16)

    proj0, xb, l0_8 = _l0_in(x, w0p, bcast=p["l0_blocks"])
    wo0 = l0_8[0].reshape(D_MODEL, D_MODEL)
    wq, = _unshard(l0_8[1], MAP_WQ, (1024,), "unshard_wq")
    wkv, = _unshard(l0_8[2], MAP_WKV, (1536,), "unshard_wkv")
    cw0, cw1, cb1, nw, g1, b1 = _unshard_small([p["cw0_8"]] + list(l0_8[3:]))
    xc, h = _rglru_fwd(proj0, cw0, cb0, wa, ba, wxg, bx, lam)
    qn, kn, qc, kc, vc = _mla_fwd(proj0, qn_w, kn_w, wq, wkv, tc, tsa, tsb)
    o, lse, (w1_8,) = _flash_fwd(qc, kc, vc, bcast=l1_blocks[:1])
    w1z, w1x, w1d = _unshard(w1_8, MAP_W1, (2048, 3072, 128), "unshard_w1")
    y0, v0, x1, x1b = _l0_out(h, o, proj0, x, wo0, g0, b0)

    z, dt_raw = _l1_in(x1b, w1z, w1d)
    xbc, pre, act = _ssd_conv_fwd(x1b, w1x, cw1, cb1)
    ys, hprev, (wo1_8,) = _ssd_scan_fwd(act, dt_raw, dt_bias, a_log, d_x, tril, expand_t, bcast=l1_blocks[1:])
    wo1 = wo1_8.reshape(SSD_INNER, D_MODEL)
    dv1, dgb1, loss8, g_wo1 = _l1_out(ys, z, nw, wo1, x1, g1, b1, target)

    dys, dz, dnw, g_z = _l1_gate_bwd(dv1, wo1, ys, z, nw, x1b)
    dact, ddt_raw, dvec1, g_dt, (recv_wo1,) = _ssd_scan_bwd(
        dys, act, dt_raw, hprev, dt_bias, a_log, d_x, tril, expand_t, x1b,
        scatter=[g_wo1.astype(bf).reshape(N_DEV, 256, D_MODEL)])
    dxbc, dcw1, g_xbc = _ssd_conv_bwd(dact, pre, xbc, cw1, x1b)

    dv0, dgb0 = _l1_dx_ln(dz, dxbc, ddt_raw, dv1, v0, w1z, w1x, w1d, g0)
    dh, do, dgate, g_wo0, g_gate = _gate_bwd(dv0, wo0, h, o, proj0, y0, xb)
    dxr, g_wa, g_wx, dvec0, g_rnn = _rglru_bwd(dh, xc, h, proj0, cw0, wa, ba, wxg, bx, lam, xb)
    early = [_reshard([g_z, g_xbc, g_dt], MAP_W1, 644, bf, "reshard_w1"), g_wo0.astype(bf).reshape(N_DEV, 128, D_MODEL),
             (_reshard([g_rnn, g_gate], MAP_G0, 244, bf, "reshard_w0_early", shards=W0_EARLY), W0_EARLY)]
    dq, dk, dvv, (recv_w1, recv_wo0, recv_w0) = _flash_bwd(qc, kc, vc, o, do, lse, scatter=early)
    recv_early = {"ssd_w_in": recv_w1, "ssd_w_out": recv_wo1, "ab_w_out": recv_wo0, "ab_w_in": recv_w0}
    dtail, g_wq, g_wkv, dqnw, dknw, g_tail = _mla_bwd(dq, dk, dvv, proj0, qn, kn, qn_w, kn_w, wq, wkv, tc, tsa, tsb, xb)

    acc = {"g_rnn": g_rnn, "g_gate": g_gate, "g_tail": g_tail, "g_wq": g_wq, "g_wkv": g_wkv,
           "dvec0": dvec0, "g_wa": g_wa, "g_wx": g_wx, "dqnw": dqnw, "dknw": dknw, "dgb0": dgb0, "dvec1": dvec1,
           "dcw1": dcw1, "dnw": dnw, "dgb1": dgb1}
    late = [(_reshard([g_rnn, g_gate, g_tail], MAP_G0, 244, bf, "reshard_w0_late", shards=W0_LATE), W0_LATE),
            _reshard([g_wq], MAP_WQ, 96, bf, "reshard_wq"), _reshard([g_wkv], MAP_WKV, 128, bf, "reshard_wkv")]
    sm_slots, vec_rows, gates = _pack_small(dvec0, g_wa, g_wx, dqnw, dknw, dgb0, dvec1, dcw1, dnw, dgb1, loss8)
    dx, recv_late = _l0_dx(dxr, dgate, dtail, w0p, dv0, scatter=late + [sm_slots], bcast=[vec_rows, gates])
    return acc, recv_early, recv_late, loss8[0, 0], dx
```

```python
import math

import jax
import jax.numpy as jnp
from jax import lax
from jax.experimental import pallas as pl
from jax.experimental.pallas import tpu as pltpu

F32 = jnp.float32
MXU_DTYPE = jnp.bfloat16

N_DEV = 8
SEQ = 4096
D_MODEL = 1024
DN_ALPHA = 4.0 ** 0.25
RNN_W = 512
MLA_HEADS = 8
ATT_SCALE = 96.0 ** -0.5
ATT_C = ATT_SCALE * math.log2(math.e)
RG_C = 8.0
SSD_INNER = 2048
SSD_HEADS = 32
SSD_P = 64
SSD_GROUPS = 4
SSD_N = 128
SSD_L = 128
SSD_CONV = 3072
LANES = 128
SUBLANES = 8
VMEM_LIMIT = 56 * 1024 * 1024

ADAM_LR, ADAM_B1, ADAM_B2, ADAM_EPS, ADAM_WD, ADAM_STEP = 0.001, 0.9, 0.999, 1e-08, 0.01, 10

HIGHEST = lax.Precision.HIGHEST


def _params(sem, limit=VMEM_LIMIT):
    return pltpu.CompilerParams(dimension_semantics=sem, vmem_limit_bytes=limit)


def _dot(a, b):
    return lax.dot_general(a, b, (((1,), (0,)), ((), ())), preferred_element_type=F32)


def _dot_nt(a, b):
    return lax.dot_general(a, b, (((1,), (1,)), ((), ())), preferred_element_type=F32)


def _dot_tn(a, b):
    return lax.dot_general(a, b, (((0,), (0,)), ((), ())), preferred_element_type=F32)


def _dot_hi(a, b):
    return lax.dot_general(a, b, (((1,), (0,)), ((), ())), precision=HIGHEST, preferred_element_type=F32)


def _mx(v):
    return v.astype(MXU_DTYPE)


def _sigmoid(v):
    return 1.0 / (1.0 + jnp.exp(-v))


def _log1p_pos(e):
    poly = e * (1.0 - e * (0.5 - e * (1.0 / 3.0 - e * 0.25)))
    return jnp.where(e < 0.01, poly, jnp.log(1.0 + e))


def _softplus(v):
    return jnp.maximum(v, 0.0) + _log1p_pos(jnp.exp(-jnp.abs(v)))


def _neg_expm1(v):
    poly = -v * (1.0 + v * (0.5 + v * (1.0 / 6.0 + v * (1.0 / 24.0 + v * (1.0 / 120.0)))))
    return jnp.where(jnp.abs(v) < 0.1, poly, 1.0 - jnp.exp(v))


def _silu(v):
    return v * _sigmoid(v)


def _dsilu(v):
    s = _sigmoid(v)
    return s * (1.0 + v * (1.0 - s))


def _shift_down(blk, halo, s):
    if s == 0:
        return blk
    t = blk.shape[0]
    r = pltpu.roll(blk, s, 0)
    hr = pltpu.roll(halo, s, 0)
    row8 = lax.broadcasted_iota(jnp.int32, hr.shape, 0)
    head = jnp.where(row8 < s, hr, r[:SUBLANES])
    return jnp.concatenate([head, r[SUBLANES:]], axis=0) if t > SUBLANES else head


def _shift_up(blk, halo, s):
    if s == 0:
        return blk
    t = blk.shape[0]
    r = pltpu.roll(blk, t - s, 0)
    hr = pltpu.roll(halo, SUBLANES - s, 0)
    row8 = lax.broadcasted_iota(jnp.int32, hr.shape, 0)
    tail = jnp.where(row8 >= SUBLANES - s, hr, r[t - SUBLANES:])
    return jnp.concatenate([r[:t - SUBLANES], tail], axis=0) if t > SUBLANES else tail


def _scan_down(a, u):
    t = a.shape[0]
    row = lax.broadcasted_iota(jnp.int32, a.shape, 0)
    d = 1
    while d < t:
        keep = row >= d
        a_sh = jnp.where(keep, pltpu.roll(a, d, 0), 1.0)
        u_sh = jnp.where(keep, pltpu.roll(u, d, 0), 0.0)
        u = a * u_sh + u
        a = a * a_sh
        d *= 2
    return a, u


def _scan_up(a, u):
    t = a.shape[0]
    row = lax.broadcasted_iota(jnp.int32, a.shape, 0)
    d = 1
    while d < t:
        keep = row < t - d
        a_sh = jnp.where(keep, pltpu.roll(a, t - d, 0), 1.0)
        u_sh = jnp.where(keep, pltpu.roll(u, t - d, 0), 0.0)
        u = a * u_sh + u
        a = a * a_sh
        d *= 2
    return a, u


def _conv4(blk, halo, cw, cb):
    out = cb + blk * cw[3:4]
    for k in range(3):
        out = out + _shift_down(blk, halo, 3 - k) * cw[k:k + 1]
    return out


RG_T = 512
P0_RNN = 2


def _rg_gates(xc, wa, ba, wx, bx, lam):
    xcb = _mx(xc)
    r = _sigmoid(_dot(xcb, wa) + ba)
    ig = _sigmoid(_dot(xcb, wx) + bx)
    sp = _softplus(-lam)
    la = (-RG_C * r) * sp
    a = jnp.exp(la)
    mult = jnp.sqrt(_neg_expm1(2.0 * la))
    return r, ig, sp, a, mult


def _rglru_fwd(proj0, cw8, cb, wa, ba, wx, bx, lam):
    t, w = RG_T, RNN_W
    nb = SEQ // t

    def body(x_ref, halo_ref, cw_ref, cb_ref, wa_ref, ba_ref, wx_ref, bx_ref, lam_ref, xc_ref, h_ref, carry):
        i = pl.program_id(0)

        @pl.when(i == 0)
        def _():
            carry[...] = jnp.zeros_like(carry)

        blk = x_ref[...]
        halo = jnp.where(i > 0, halo_ref[...], 0.0)
        xc = _conv4(blk, halo, cw_ref[...], cb_ref[...])
        _, ig, _, a, mult = _rg_gates(xc, wa_ref[...], ba_ref[...], wx_ref[...], bx_ref[...], lam_ref[...])
        u = mult * (ig * xc)
        big_a, big_u = _scan_down(a, u)
        h = big_a * carry[SUBLANES - 1:SUBLANES, :] + big_u
        carry[...] = h[t - SUBLANES:]
        xc_ref[...] = xc
        h_ref[...] = h

    vec = pl.BlockSpec((1, w), lambda i: (0, 0))
    mat = pl.BlockSpec((w, w), lambda i: (0, 0))
    return pl.pallas_call(
        body, name="rglru_fwd", grid=(nb,),
        in_specs=[pl.BlockSpec((t, w), lambda i: (i, P0_RNN)),
                  pl.BlockSpec((SUBLANES, w), lambda i: (jnp.maximum(i * (t // SUBLANES) - 1, 0), P0_RNN)),
                  pl.BlockSpec((SUBLANES, w), lambda i: (0, 0)), vec, mat, vec, mat, vec, vec],
        out_specs=[pl.BlockSpec((t, w), lambda i: (i, 0)), pl.BlockSpec((t, w), lambda i: (i, 0))],
        out_shape=[jax.ShapeDtypeStruct((SEQ, w), F32), jax.ShapeDtypeStruct((SEQ, w), F32)],
        scratch_shapes=[pltpu.VMEM((SUBLANES, w), F32)],
        compiler_params=_params(("arbitrary",)),
    )(proj0, proj0, cw8, cb, wa, ba, wx, bx, lam)


def _rglru_bwd(dh, xc, h, proj0, cw8, wa, ba, wx, bx, lam, xb):
    t, w = RG_T, RNN_W
    nb = SEQ // t
    tb = t // SUBLANES

    def body(dh_ref, xc_ref, h_ref, hh_ref, x_ref, cw_ref, wa_ref, ba_ref, wx_ref, bx_ref, lam_ref, xb_ref,
             dx_ref, dwa_ref, dwx_ref, dvec_ref, gw_ref, gcarry, dxc_next):
        i = pl.program_id(0)
        rev = nb - 1 - i

        @pl.when(i == 0)
        def _():
            gcarry[...] = jnp.zeros_like(gcarry)
            dxc_next[...] = jnp.zeros_like(dxc_next)
            gw_ref[...] = jnp.zeros_like(gw_ref)
            dwa_ref[...] = jnp.zeros_like(dwa_ref)
            dwx_ref[...] = jnp.zeros_like(dwx_ref)
            dvec_ref[...] = jnp.zeros_like(dvec_ref)

        xc = xc_ref[...]
        wa_v, wx_v = wa_ref[...], wx_ref[...]
        lam_v = lam_ref[...]
        r, ig, sp, a, mult = _rg_gates(xc, wa_v, ba_ref[...], wx_v, bx_ref[...], lam_v)
        dhv = dh_ref[...]
        big_a, big_u = _scan_up(a, a * dhv)
        gg = big_a * gcarry[0:1, :] + big_u
        g = dhv + _shift_up(gg, gcarry[...], 1)
        gcarry[...] = gg[:SUBLANES]
        hhalo = jnp.where(rev > 0, hh_ref[...], 0.0)
        da = g * _shift_down(h_ref[...], hhalo, 1)
        d_mult = g * (ig * xc)
        d_i = g * (mult * xc)
        dxc = g * (mult * ig)
        d_la = da * a - d_mult * (a * a) / mult
        d_r = d_la * (-RG_C * sp)
        d_sp = jnp.sum(d_la * (-RG_C * r), axis=0, keepdims=True)
        d_pa = d_r * r * (1.0 - r)
        d_px = d_i * ig * (1.0 - ig)
        d_pab, d_pxb = _mx(d_pa), _mx(d_px)
        dxc = dxc + _dot_nt(d_pab, wa_v) + _dot_nt(d_pxb, wx_v)
        xcb = _mx(xc)
        dwa_ref[...] += _dot_tn(xcb, d_pab)
        dwx_ref[...] += _dot_tn(xcb, d_pxb)
        dvec_ref[0:1, :] += jnp.sum(d_pa, axis=0, keepdims=True)
        dvec_ref[1:2, :] += jnp.sum(d_px, axis=0, keepdims=True)
        dvec_ref[2:3, :] += d_sp * (-_sigmoid(-lam_v))
        dvec_ref[3:4, :] += jnp.sum(dxc, axis=0, keepdims=True)
        xblk = x_ref[...]
        cw = cw_ref[...]
        dx = dxc * cw[3:4]
        nxt = dxc_next[...]
        dvec_ref[7:8, :] += jnp.sum(dxc * xblk, axis=0, keepdims=True)
        for k in range(3):
            up = _shift_up(dxc, nxt, 3 - k)
            dvec_ref[4 + k:5 + k, :] += jnp.sum(up * xblk, axis=0, keepdims=True)
            dx = dx + up * cw[k:k + 1]
        dxc_next[...] = dxc[:SUBLANES]
        dxb = _mx(dx)
        dx_ref[...] = dxb
        gw_ref[...] += _dot_tn(xb_ref[...], dxb)

    blk = pl.BlockSpec((t, w), lambda i: (nb - 1 - i, 0))
    halo = pl.BlockSpec((SUBLANES, w), lambda i: (jnp.maximum((nb - 1 - i) * tb - 1, 0), 0))
    vec = pl.BlockSpec((1, w), lambda i: (0, 0))
    mat = pl.BlockSpec((w, w), lambda i: (0, 0))
    return pl.pallas_call(
        body, name="rglru_bwd", grid=(nb,),
        in_specs=[blk, blk, blk, halo, pl.BlockSpec((t, w), lambda i: (nb - 1 - i, P0_RNN)),
                  pl.BlockSpec((SUBLANES, w), lambda i: (0, 0)), mat, vec, mat, vec, vec,
                  pl.BlockSpec((t, D_MODEL), lambda i: (nb - 1 - i, 0))],
        out_specs=[blk, mat, mat, pl.BlockSpec((16, w), lambda i: (0, 0)), pl.BlockSpec((D_MODEL, w), lambda i: (0, 0))],
        out_shape=[jax.ShapeDtypeStruct((SEQ, w), MXU_DTYPE), jax.ShapeDtypeStruct((w, w), F32),
                   jax.ShapeDtypeStruct((w, w), F32), jax.ShapeDtypeStruct((16, w), F32),
                   jax.ShapeDtypeStruct((D_MODEL, w), F32)],
        scratch_shapes=[pltpu.VMEM((SUBLANES, w), F32), pltpu.VMEM((SUBLANES, w), F32)],
        compiler_params=_params(("arbitrary",)),
    )(dh, xc, h, h, proj0, cw8, wa, ba, wx, bx, lam, xb)


MLA_T = 512


def _rope(v, c, sa, sb):
    return v * c + pltpu.roll(v, LANES - 16, 1) * sa + pltpu.roll(v, 16, 1) * sb


def _rope_t(dv, c, sa, sb):
    return dv * c + pltpu.roll(dv * sa, 16, 1) + pltpu.roll(dv * sb, LANES - 16, 1)


def _rms(v, g, eps=1e-6):
    rs = lax.rsqrt(jnp.mean(v * v, axis=-1, keepdims=True) + eps)
    return v * rs * g, rs


def _mla_fwd(proj0, q_norm, kv_norm, wq, wkv, tc, tsa, tsb):
    t = MLA_T

    def body(cq_ref, ck_ref, qn_ref, kn_ref, wq_ref, wkv_ref, c_ref, sa_ref, sb_ref,
             oqn_ref, okn_ref, oq_ref, ok_ref, ov_ref):
        c, sa, sb = c_ref[...], sa_ref[...], sb_ref[...]
        ck = ck_ref[...]
        qn = _mx(_rms(cq_ref[...], qn_ref[...])[0])
        kn = _mx(_rms(ck[:, :LANES], kn_ref[...])[0])
        oqn_ref[...] = qn
        okn_ref[...] = kn
        krv = _rope(ck[:, LANES:], c, sa, sb)
        qraw = _dot(qn, wq_ref[...])
        kvraw = _dot(kn, wkv_ref[...])
        for hd in range(MLA_HEADS):
            sl = slice(hd * LANES, (hd + 1) * LANES)
            oq_ref[:, sl] = _mx(_rope(qraw[:, sl], c, sa, sb))
            ok_ref[:, sl] = _mx(kvraw[:, sl] + krv)
        ov_ref[...] = _mx(kvraw[:, 1024:])

    tab = pl.BlockSpec((t, LANES), lambda i: (i, 0))
    wide = pl.BlockSpec((t, 1024), lambda i: (i, 0))
    const = lambda shape: pl.BlockSpec(shape, lambda i: (0, 0))
    return pl.pallas_call(
        body, name="mla_fwd", grid=(SEQ // t,),
        in_specs=[pl.BlockSpec((t, 256), lambda i: (i, 6)), pl.BlockSpec((t, 256), lambda i: (i, 7)),
                  const((1, 256)), const((1, LANES)), const((256, 1024)), const((LANES, 1536)), tab, tab, tab],
        out_specs=[pl.BlockSpec((t, 256), lambda i: (i, 0)), tab, wide, wide, pl.BlockSpec((t, 512), lambda i: (i, 0))],
        out_shape=[jax.ShapeDtypeStruct((SEQ, 256), MXU_DTYPE), jax.ShapeDtypeStruct((SEQ, LANES), MXU_DTYPE),
                   jax.ShapeDtypeStruct((SEQ, 1024), MXU_DTYPE), jax.ShapeDtypeStruct((SEQ, 1024), MXU_DTYPE),
                   jax.ShapeDtypeStruct((SEQ, 512), MXU_DTYPE)],
        compiler_params=_params(("parallel",)),
    )(proj0, proj0, q_norm, kv_norm, wq, wkv, tc, tsa, tsb)


ATT_T = 1024


def _flash_fwd(q, k, v, bcast=()):
    t = ATT_T
    nb = SEQ // t

    steps = [(qi, ki) for qi in range(nb) for ki in range(qi + 1)]
    qi_tab = jnp.asarray([s[0] for s in steps], jnp.int32)
    ki_tab = jnp.asarray([s[1] for s in steps], jnp.int32)

    nx = len(bcast)

    def body(qi_ref, ki_ref, q_ref, k_ref, v_ref, *rest):
        x_refs, (o_ref, lse_ref), g_refs = rest[:nx], rest[nx:nx + 2], rest[nx + 2:2 * nx + 2]
        m_sc, acc_sc = rest[2 * nx + 2:2 * nx + 4]
        step = pl.program_id(1)
        qi, ki = qi_ref[step], ki_ref[step]
        if nx:
            copies = _peer_copies(x_refs, g_refs, rest[2 * nx + 4:], [])

            @pl.when((pl.program_id(0) == 0) & (step == 0))
            def _():
                for cp in copies:
                    cp.start()

        @pl.when(ki == 0)
        def _():
            m_sc[...] = jnp.full_like(m_sc, -jnp.inf)
            acc_sc[...] = jnp.zeros_like(acc_sc)

        def update(diagonal):
            vv = v_ref[...]
            lane_v = lax.broadcasted_iota(jnp.int32, vv.shape, 1)
            for hd in range(2):
                sl = slice(hd * LANES, (hd + 1) * LANES)
                st = _dot_nt(k_ref[:, sl], q_ref[:, sl])
                if diagonal:
                    st = jnp.where(lax.broadcasted_iota(jnp.int32, (t, t), 0)
                                   <= lax.broadcasted_iota(jnp.int32, (t, t), 1), st, -jnp.inf)
                m_prev = m_sc[hd:hd + 1, :]
                m_new = jnp.maximum(m_prev, jnp.max(st, axis=0, keepdims=True))
                pt = jnp.exp2((st - m_new) * ATT_C)
                m_sc[hd:hd + 1, :] = m_new
                vh = jnp.where((lane_v >= hd * 64) & (lane_v < (hd + 1) * 64), vv, jnp.ones_like(vv))
                acc_sc[hd] = acc_sc[hd] * jnp.exp2((m_prev - m_new) * ATT_C) + _dot_tn(vh, _mx(pt))

        @pl.when(ki < qi)
        def _():
            update(False)

        @pl.when(ki == qi)
        def _():
            update(True)
            a0, a1 = acc_sc[0], acc_sc[1]
            l0, l1 = a0[64:65, :], a1[0:1, :]
            first = lax.broadcasted_iota(jnp.int32, (LANES, t), 0) < 64
            o_ref[...] = jnp.where(first, a0 / l0, a1 / l1).T
            lse_ref[0, 0:1, :] = m_sc[0:1, :] * ATT_SCALE + jnp.log(l0)
            lse_ref[0, 1:2, :] = m_sc[1:2, :] * ATT_SCALE + jnp.log(l1)
            lse_ref[0, 2:SUBLANES, :] = jnp.zeros((SUBLANES - 2, t), F32)

        if nx:
            @pl.when((pl.program_id(0) == 3) & (step == len(steps) - 1))
            def _():
                for cp in copies:
                    cp.wait()

    grid_spec = pltpu.PrefetchScalarGridSpec(
        num_scalar_prefetch=2, grid=(4, len(steps)),
        in_specs=[pl.BlockSpec((t, 256), lambda p, s, qt, kt: (qt[s], p)),
                  pl.BlockSpec((t, 256), lambda p, s, qt, kt: (kt[s], p)),
                  pl.BlockSpec((t, LANES), lambda p, s, qt, kt: (kt[s], p))] + [ANY] * nx,
        out_specs=[pl.BlockSpec((t, LANES), lambda p, s, qt, kt: (qt[s], p)),
                   pl.BlockSpec((1, SUBLANES, t), lambda p, s, qt, kt: (p, 0, qt[s]))] + [ANY] * nx,
        scratch_shapes=[pltpu.VMEM((SUBLANES, t), F32), pltpu.VMEM((2, LANES, t), F32)]
        + (_exchange_sems(nx) if nx else []))
    res = pl.pallas_call(
        body, name="flash_fwd", grid_spec=grid_spec,
        out_shape=[jax.ShapeDtypeStruct((SEQ, 512), F32), jax.ShapeDtypeStruct((4, SUBLANES, SEQ), F32)]
        + _exchange_shapes([], bcast),
        compiler_params=_params(("arbitrary", "arbitrary")),
    )(qi_tab, ki_tab, q, k, v, *bcast)
    return res[0], res[1], res[2:]


def _flash_bwd(q, k, v, o, do, lse, scatter=()):
    t = ATT_T
    nb = SEQ // t

    steps = [(qi, ki) for ki in range(nb) for qi in range(ki, nb)]
    qi_tab = jnp.asarray([s[0] for s in steps], jnp.int32)
    ki_tab = jnp.asarray([s[1] for s in steps], jnp.int32)
    log2e = math.log2(math.e)

    sc_arrays, sc_ranges = _scatter_args(scatter)
    nx = len(sc_arrays)

    def body(qi_ref, ki_ref, q_ref, k_ref, v_ref, o_ref, do_ref, lse_ref, *rest):
        x_refs, (dq_ref, dk_ref, dv_ref), g_refs = rest[:nx], rest[nx:nx + 3], rest[nx + 3:2 * nx + 3]
        dkt_sc, dvt_sc = rest[2 * nx + 3:2 * nx + 5]
        step = pl.program_id(1)
        qi, ki = qi_ref[step], ki_ref[step]
        if nx:
            copies = _peer_copies(x_refs, g_refs, rest[2 * nx + 5:], sc_ranges)

            @pl.when((pl.program_id(0) == 0) & (step == 0))
            def _():
                for cp in copies:
                    cp.start()

        @pl.when(step == 0)
        def _():
            dq_ref[...] = jnp.zeros_like(dq_ref)

        @pl.when(qi == ki)
        def _():
            dkt_sc[...] = jnp.zeros_like(dkt_sc)
            dvt_sc[...] = jnp.zeros_like(dvt_sc)

        def update(diagonal):
            dov, ov, vv = do_ref[...], o_ref[...], v_ref[...]
            lse2 = (lse_ref[0] * log2e).T
            lane = lax.broadcasted_iota(jnp.int32, (t, LANES), 1)
            row_t = lax.broadcasted_iota(jnp.int32, (LANES, t), 0)
            prod = dov * ov
            do_b = _mx(dov)
            qrows = pl.ds(pl.multiple_of(qi * t, t), t)
            dvt_acc = jnp.zeros((LANES, t), F32)
            dkt_new, dq_new = [], []
            for hd in range(2):
                sl = slice(hd * LANES, (hd + 1) * LANES)
                mine = (lane >= hd * 64) & (lane < (hd + 1) * 64)
                qh, kh = q_ref[:, sl], k_ref[:, sl]
                p = jnp.exp2(_dot_nt(qh, kh) * ATT_C - lse2[:, hd:hd + 1])
                if diagonal:
                    p = jnp.where(lax.broadcasted_iota(jnp.int32, (t, t), 1)
                                  <= lax.broadcasted_iota(jnp.int32, (t, t), 0), p, 0.0)
                do_h = jnp.where(mine, dov, 0.0)
                delta = jnp.sum(jnp.where(mine, prod, 0.0), axis=1, keepdims=True)
                dp = _dot_nt(_mx(do_h), vv)
                ds = _mx(p * (dp - delta) * ATT_SCALE)
                dvt_acc = dvt_acc + jnp.where((row_t >= hd * 64) & (row_t < (hd + 1) * 64), _dot_tn(do_b, _mx(p)), 0.0)
                dkt_new.append(_dot_tn(qh, ds))
                dq_new.append(_dot(ds, kh))
            for hd in range(2):
                sl = slice(hd * LANES, (hd + 1) * LANES)
                dkt_sc[sl, :] += dkt_new[hd]
                dq_ref[qrows, sl] += dq_new[hd]
            dvt_sc[...] += dvt_acc

        @pl.when(qi > ki)
        def _():
            update(False)

        @pl.when(qi == ki)
        def _():
            update(True)

        @pl.when(qi == nb - 1)
        def _():
            dk_ref[...] = dkt_sc[...].T
            dv_ref[...] = dvt_sc[...].T

        if nx:
            @pl.when((pl.program_id(0) == 3) & (step == len(steps) - 1))
            def _():
                for cp in copies:
                    cp.wait()

    qmap = lambda p, s, qt, kt: (qt[s], p)
    kmap = lambda p, s, qt, kt: (kt[s], p)
    grid_spec = pltpu.PrefetchScalarGridSpec(
        num_scalar_prefetch=2, grid=(4, len(steps)),
        in_specs=[pl.BlockSpec((t, 256), qmap), pl.BlockSpec((t, 256), kmap), pl.BlockSpec((t, LANES), kmap),
                  pl.BlockSpec((t, LANES), qmap), pl.BlockSpec((t, LANES), qmap),
                  pl.BlockSpec((1, SUBLANES, t), lambda p, s, qt, kt: (p, 0, qt[s]))] + [ANY] * nx,
        out_specs=[pl.BlockSpec((SEQ, 256), lambda p, s, qt, kt: (0, p)), pl.BlockSpec((t, 256), kmap),
                   pl.BlockSpec((t, LANES), kmap)] + [ANY] * nx,
        scratch_shapes=[pltpu.VMEM((256, t), F32), pltpu.VMEM((LANES, t), F32)] + (_exchange_sems(nx) if nx else []))
    res = pl.pallas_call(
        body, name="flash_bwd", grid_spec=grid_spec,
        out_shape=[jax.ShapeDtypeStruct((SEQ, 1024), F32), jax.ShapeDtypeStruct((SEQ, 1024), F32),
                   jax.ShapeDtypeStruct((SEQ, 512), F32)] + _exchange_shapes(sc_arrays, []),
        compiler_params=_params(("arbitrary", "arbitrary")),
    )(qi_tab, ki_tab, q, k, v, o, do, lse, *sc_arrays)
    return res[0], res[1], res[2], res[3:]


def _rms_bwd(v, g, dy, eps=1e-6):
    rs = lax.rsqrt(jnp.mean(v * v, axis=-1, keepdims=True) + eps)
    xh = v * rs
    dxh = dy * g
    dv = rs * (dxh - xh * jnp.mean(dxh * xh, axis=-1, keepdims=True))
    return dv, jnp.sum(dy * xh, axis=0, keepdims=True)


def _mla_bwd(dq, dk, dv, proj0, qlat, klat, q_norm, kv_norm, wq, wkv, tc, tsa, tsb, xb):
    t = MLA_T

    def body(dq_ref, dk_ref, dv_ref, cq_ref, ck_ref, ql_ref, kl_ref, qn_ref, kn_ref, wq_ref, wkv_ref,
             c_ref, sa_ref, sb_ref, xb_ref, o_ref, gwq_ref, gwkv_ref, dgq_ref, dgk_ref, gwt_ref, oq_ref, okv_ref):
        @pl.when(pl.program_id(0) == 0)
        def _():
            dgq_ref[...] = jnp.zeros_like(dgq_ref)
            dgk_ref[...] = jnp.zeros_like(dgk_ref)
            gwq_ref[...] = jnp.zeros_like(gwq_ref)
            gwkv_ref[...] = jnp.zeros_like(gwkv_ref)
            gwt_ref[...] = jnp.zeros_like(gwt_ref)

        c, sa, sb = c_ref[...], sa_ref[...], sb_ref[...]
        lane = lax.broadcasted_iota(jnp.int32, (t, LANES), 1)
        dkr = jnp.zeros((t, LANES), F32)
        for hd in range(MLA_HEADS):
            sl = slice(hd * LANES, (hd + 1) * LANES)
            oq_ref[:, sl] = _mx(_rope_t(dq_ref[:, sl], c, sa, sb))
            dkh = dk_ref[:, sl]
            okv_ref[:, sl] = _mx(dkh)
            dkr = dkr + dkh
        okv_ref[:, 1024:] = _mx(dv_ref[...])
        dkr = _rope_t(jnp.where((lane >= 64) & (lane < 96), dkr, 0.0), c, sa, sb)
        dqraw, dkvraw = oq_ref[...], okv_ref[...]
        gwq_ref[...] += _dot_tn(ql_ref[...], dqraw)
        gwkv_ref[...] += _dot_tn(kl_ref[...], dkvraw)
        dqn = _dot_nt(dqraw, wq_ref[...])
        dkn = _dot_nt(dkvraw, wkv_ref[...])
        dcq, dgq = _rms_bwd(cq_ref[...], qn_ref[...], dqn)
        dck, dgk = _rms_bwd(ck_ref[:, :LANES], kn_ref[...], dkn)
        o_ref[:, :256] = _mx(dcq)
        o_ref[:, 256:384] = _mx(dck)
        o_ref[:, 384:] = _mx(dkr)
        gwt_ref[...] += _dot_tn(xb_ref[...], o_ref[...])
        dgq_ref[0:1, :] += dgq
        dgk_ref[0:1, :] += dgk

    tab = pl.BlockSpec((t, LANES), lambda i: (i, 0))
    wide = pl.BlockSpec((t, 1024), lambda i: (i, 0))
    const = lambda shape: pl.BlockSpec(shape, lambda i: (0, 0))
    return pl.pallas_call(
        body, name="mla_bwd", grid=(SEQ // t,),
        in_specs=[wide, wide, pl.BlockSpec((t, 512), lambda i: (i, 0)),
                  pl.BlockSpec((t, 256), lambda i: (i, 6)), pl.BlockSpec((t, 256), lambda i: (i, 7)),
                  pl.BlockSpec((t, 256), lambda i: (i, 0)), tab,
                  const((1, 256)), const((1, LANES)), const((256, 1024)), const((LANES, 1536)), tab, tab, tab, wide],
        out_specs=[pl.BlockSpec((t, 512), lambda i: (i, 0)), const((256, 1024)), const((LANES, 1536)),
                   const((SUBLANES, 256)), const((SUBLANES, LANES)), const((D_MODEL, 512))],
        out_shape=[jax.ShapeDtypeStruct((SEQ, 512), MXU_DTYPE), jax.ShapeDtypeStruct((256, 1024), F32),
                   jax.ShapeDtypeStruct((LANES, 1536), F32), jax.ShapeDtypeStruct((SUBLANES, 256), F32),
                   jax.ShapeDtypeStruct((SUBLANES, LANES), F32), jax.ShapeDtypeStruct((D_MODEL, 512), F32)],
        scratch_shapes=[pltpu.VMEM((t, 1024), MXU_DTYPE), pltpu.VMEM((t, 1536), MXU_DTYPE)],
        compiler_params=_params(("arbitrary",)),
    )(dq, dk, dv, proj0, proj0, qlat, klat, q_norm, kv_norm, wq, wkv, tc, tsa, tsb, xb)


LN_T = 512


def _ln(v, g, b, eps=1e-5):
    mu = jnp.mean(v, axis=-1, keepdims=True)
    xc = v - mu
    rs = lax.rsqrt(jnp.mean(xc * xc, axis=-1, keepdims=True) + eps)
    return xc * rs * g + b


def _ln_bwd(v, g, dy, eps=1e-5):
    mu = jnp.mean(v, axis=-1, keepdims=True)
    xc = v - mu
    rs = lax.rsqrt(jnp.mean(xc * xc, axis=-1, keepdims=True) + eps)
    xh = xc * rs
    dxh = dy * g
    dv = rs * (dxh - jnp.mean(dxh, axis=-1, keepdims=True) - xh * jnp.mean(dxh * xh, axis=-1, keepdims=True))
    return dv, jnp.sum(dy * xh, axis=0, keepdims=True), jnp.sum(dy, axis=0, keepdims=True)


def _l0_out(h, o, proj0, x, w_out, g, b):
    t = LN_T

    def body(h_ref, o_ref, ga_ref, gb_ref, x_ref, w_ref, g_ref, b_ref, y_ref, v_ref, x1_ref, x1b_ref):
        y = _mx(jnp.concatenate([h_ref[...] * _silu(ga_ref[...]), o_ref[...] * _silu(gb_ref[...])], axis=1))
        v = DN_ALPHA * x_ref[...] + _dot(y, w_ref[...])
        y_ref[...] = y
        v_ref[...] = v
        x1 = _ln(v, g_ref[...], b_ref[...])
        x1_ref[...] = x1
        x1b_ref[...] = _mx(x1)

    half = pl.BlockSpec((t, 512), lambda i: (i, 0))
    full = pl.BlockSpec((t, D_MODEL), lambda i: (i, 0))
    vec = pl.BlockSpec((1, D_MODEL), lambda i: (0, 0))
    return pl.pallas_call(
        body, name="l0_out", grid=(SEQ // t,),
        in_specs=[half, half, pl.BlockSpec((t, 512), lambda i: (i, 0)), pl.BlockSpec((t, 512), lambda i: (i, 1)), full,
                  pl.BlockSpec((D_MODEL, D_MODEL), lambda i: (0, 0)), vec, vec],
        out_specs=[full, full, full, full],
        out_shape=[jax.ShapeDtypeStruct((SEQ, D_MODEL), MXU_DTYPE), jax.ShapeDtypeStruct((SEQ, D_MODEL), F32),
                   jax.ShapeDtypeStruct((SEQ, D_MODEL), F32), jax.ShapeDtypeStruct((SEQ, D_MODEL), MXU_DTYPE)],
        compiler_params=_params(("parallel",)),
    )(h, o, proj0, proj0, x, w_out, g, b)


def _l1_in(x1b, w1z, w1d):
    t = 1024

    def body(x_ref, wz_ref, wd_ref, z_ref, dt_ref):
        xv = x_ref[...]
        z_ref[...] = _dot(xv, wz_ref[...])
        dt_ref[...] = _dot(xv, wd_ref[...])

    rows = lambda w: pl.BlockSpec((t, w), lambda i: (i, 0))
    const = lambda w: pl.BlockSpec((D_MODEL, w), lambda i: (0, 0))
    return pl.pallas_call(
        body, name="l1_in", grid=(SEQ // t,),
        in_specs=[rows(D_MODEL), const(SSD_INNER), const(LANES)],
        out_specs=[rows(SSD_INNER), rows(LANES)],
        out_shape=[jax.ShapeDtypeStruct((SEQ, SSD_INNER), F32), jax.ShapeDtypeStruct((SEQ, LANES), F32)],
        compiler_params=_params(("parallel",)),
    )(x1b, w1z, w1d)


def _l1_dx_ln(dz, dxbc, ddt, dv1, v0, w1z, w1x, w1d, g):
    t = LN_T

    def body(dz_ref, dx_ref, ddt_ref, dv1_ref, v_ref, wz_ref, wx_ref, wd_ref, g_ref, dv_ref, dgb_ref):
        @pl.when(pl.program_id(0) == 0)
        def _():
            dgb_ref[...] = jnp.zeros_like(dgb_ref)

        dy = (DN_ALPHA * dv1_ref[...] + _dot_nt(dz_ref[...], wz_ref[...]) + _dot_nt(dx_ref[...], wx_ref[...])
              + _dot_nt(_mx(ddt_ref[...]), wd_ref[...]))
        dv, dg, db = _ln_bwd(v_ref[...], g_ref[...], dy)
        dv_ref[...] = dv
        dgb_ref[0:1, :] += dg
        dgb_ref[1:2, :] += db

    rows = lambda w: pl.BlockSpec((t, w), lambda i: (i, 0))
    const = lambda w: pl.BlockSpec((D_MODEL, w), lambda i: (0, 0))
    return pl.pallas_call(
        body, name="l1_dx_ln", grid=(SEQ // t,),
        in_specs=[rows(SSD_INNER), rows(SSD_CONV), rows(LANES), rows(D_MODEL), rows(D_MODEL),
                  const(SSD_INNER), const(SSD_CONV), const(LANES), pl.BlockSpec((1, D_MODEL), lambda i: (0, 0))],
        out_specs=[rows(D_MODEL), pl.BlockSpec((SUBLANES, D_MODEL), lambda i: (0, 0))],
        out_shape=[jax.ShapeDtypeStruct((SEQ, D_MODEL), F32), jax.ShapeDtypeStruct((SUBLANES, D_MODEL), F32)],
        compiler_params=_params(("arbitrary",)),
    )(dz, dxbc, ddt, dv1, v0, w1z, w1x, w1d, g)


def _gate_bwd(dv0, w_out, h, o, proj0, y0, xb):
    t = LN_T

    def body(dv_ref, w_ref, h_ref, o_ref, ga_ref, gb_ref, y0_ref, xb_ref, dh_ref, do_ref, dg_ref, gwo_ref, gwg_ref):
        @pl.when(pl.program_id(0) == 0)
        def _():
            gwo_ref[...] = jnp.zeros_like(gwo_ref)
            gwg_ref[...] = jnp.zeros_like(gwg_ref)

        dvb = _mx(dv_ref[...])
        dy = _dot_nt(dvb, w_ref[...])
        ga, gb, dya, dyb = ga_ref[...], gb_ref[...], dy[:, :512], dy[:, 512:]
        dh_ref[...] = dya * _silu(ga)
        do_ref[...] = dyb * _silu(gb)
        dg_ref[:, :512] = _mx(dya * h_ref[...] * _dsilu(ga))
        dg_ref[:, 512:] = _mx(dyb * o_ref[...] * _dsilu(gb))
        gwo_ref[...] += _dot_tn(y0_ref[...], dvb)
        gwg_ref[...] += _dot_tn(xb_ref[...], dg_ref[...])

    half = pl.BlockSpec((t, 512), lambda i: (i, 0))
    half1 = pl.BlockSpec((t, 512), lambda i: (i, 1))
    full = pl.BlockSpec((t, 1024), lambda i: (i, 0))
    square = pl.BlockSpec((D_MODEL, D_MODEL), lambda i: (0, 0))
    return pl.pallas_call(
        body, name="gate_bwd", grid=(SEQ // t,),
        in_specs=[full, square, half, half, half, half1, full, full],
        out_specs=[half, half, full, square, square],
        out_shape=[jax.ShapeDtypeStruct((SEQ, 512), F32), jax.ShapeDtypeStruct((SEQ, 512), F32),
                   jax.ShapeDtypeStruct((SEQ, 1024), MXU_DTYPE), jax.ShapeDtypeStruct((D_MODEL, D_MODEL), F32),
                   jax.ShapeDtypeStruct((D_MODEL, D_MODEL), F32)],
        compiler_params=_params(("arbitrary",)),
    )(dv0, w_out, h, o, proj0, proj0, y0, xb)


CONV_T = 1024
CONV_CB = 1024


def _ssd_conv_fwd(x1b, w1x, cw8, cb):
    t, cbk = CONV_T, CONV_CB

    def body(x_ref, w_ref, cw_ref, cb_ref, xbc_ref, pre_ref, act_ref, carry):
        xbc = _dot(x_ref[...], w_ref[...])
        halo = jnp.where(pl.program_id(1) > 0, carry[...], 0.0)
        pre = _conv4(xbc, halo, cw_ref[...], cb_ref[...])
        carry[...] = xbc[t - SUBLANES:]
        xbc_ref[...] = xbc
        pre_ref[...] = pre
        act_ref[...] = _silu(pre)

    blk = pl.BlockSpec((t, cbk), lambda j, i: (i, j))
    out = jax.ShapeDtypeStruct((SEQ, SSD_CONV), F32)
    return pl.pallas_call(
        body, name="ssd_conv_fwd", grid=(SSD_CONV // cbk, SEQ // t),
        in_specs=[pl.BlockSpec((t, D_MODEL), lambda j, i: (i, 0)), pl.BlockSpec((D_MODEL, cbk), lambda j, i: (0, j)),
                  pl.BlockSpec((SUBLANES, cbk), lambda j, i: (0, j)), pl.BlockSpec((1, cbk), lambda j, i: (0, j))],
        out_specs=[blk, blk, blk], out_shape=[out, out, out],
        scratch_shapes=[pltpu.VMEM((SUBLANES, cbk), F32)],
        compiler_params=_params(("parallel", "arbitrary")),
    )(x1b, w1x, cw8, cb)


def _ssd_conv_bwd(dact, pre, xbc, cw8, x1b):
    t, cbk = CONV_T, CONV_CB
    tb = t // SUBLANES
    nb = SEQ // t

    def body(da_ref, dan_ref, pre_ref, pren_ref, x_ref, cw_ref, x1_ref, dx_ref, dcw_ref, gw_ref):
        i = pl.program_id(1)

        @pl.when(i == 0)
        def _():
            dcw_ref[...] = jnp.zeros_like(dcw_ref)
            gw_ref[...] = jnp.zeros_like(gw_ref)

        dpre = da_ref[...] * _dsilu(pre_ref[...])
        dpre_next = jnp.where(i < nb - 1, dan_ref[...] * _dsilu(pren_ref[...]), 0.0)
        xblk = x_ref[...]
        cw = cw_ref[...]
        dx = dpre * cw[3:4]
        dcw_ref[3:4, :] += jnp.sum(dpre * xblk, axis=0, keepdims=True)
        for k in range(3):
            up = _shift_up(dpre, dpre_next, 3 - k)
            dcw_ref[k:k + 1, :] += jnp.sum(up * xblk, axis=0, keepdims=True)
            dx = dx + up * cw[k:k + 1]
        dcw_ref[4:5, :] += jnp.sum(dpre, axis=0, keepdims=True)
        dxb = _mx(dx)
        dx_ref[...] = dxb
        gw_ref[...] += _dot_tn(x1_ref[...], dxb)

    blk = pl.BlockSpec((t, cbk), lambda j, i: (i, j))
    nxt = pl.BlockSpec((SUBLANES, cbk), lambda j, i: (jnp.minimum((i + 1) * tb, SEQ // SUBLANES - 1), j))
    acc = pl.BlockSpec((SUBLANES, cbk), lambda j, i: (0, j))
    return pl.pallas_call(
        body, name="ssd_conv_bwd", grid=(SSD_CONV // cbk, nb),
        in_specs=[blk, nxt, blk, nxt, blk, acc, pl.BlockSpec((t, D_MODEL), lambda j, i: (i, 0))],
        out_specs=[blk, acc, pl.BlockSpec((D_MODEL, cbk), lambda j, i: (0, j))],
        out_shape=[jax.ShapeDtypeStruct((SEQ, SSD_CONV), MXU_DTYPE), jax.ShapeDtypeStruct((SUBLANES, SSD_CONV), F32),
                   jax.ShapeDtypeStruct((D_MODEL, SSD_CONV), F32)],
        compiler_params=_params(("parallel", "arbitrary")),
    )(dact, dact, pre, pre, xbc, cw8, x1b)


def _ssd_common(dt_raw, bias, alog, tril, expand_t, xs):
    lane = lax.broadcasted_iota(jnp.int32, dt_raw.shape, 1)
    dt = jnp.where(lane < SSD_HEADS, _softplus(dt_raw + bias), 0.0)
    a_neg = -jnp.exp(alog)
    cs = _dot_hi(tril, dt * a_neg)
    dt_x = _expand_heads(dt, expand_t)
    ecs_x = _expand_heads(jnp.exp(cs), expand_t)
    ds_x = _expand_heads(jnp.exp(cs[SSD_L - 1:SSD_L, :] - cs), expand_t)
    return dt, a_neg, cs, dt_x, None, xs * dt_x, ds_x, ecs_x, ecs_x[SSD_L - 1:SSD_L, :]


def _expand_heads(v, expand_t):
    hi = v.astype(jnp.bfloat16)
    lo = (v - hi.astype(F32)).astype(jnp.bfloat16)
    return _dot_nt(hi, expand_t) + _dot_nt(lo, expand_t)


def _fold_heads(v, expand_t):
    hi = v.astype(jnp.bfloat16)
    lo = (v - hi.astype(F32)).astype(jnp.bfloat16)
    return _dot(hi, expand_t) + _dot(lo, expand_t)


def _ssd_decay(cs, cs_t, hh, causal):
    seg = cs[:, hh:hh + 1] - cs_t[hh:hh + 1, :]
    return jnp.where(causal, jnp.exp(jnp.where(causal, seg, 0.0)), 0.0)


def _ssd_scan_fwd(act, dt_raw, bias, alog, d_x, tril, expand_t, bcast=()):
    nc = SEQ // SSD_L
    gw = SSD_INNER // SSD_GROUPS
    n = len(bcast)

    def body(act_ref, dt_ref, bias_ref, alog_ref, dx_ref, tril_ref, et_ref, *rest):
        y_ref, hp_ref, h_sc = rest[n], rest[n + 1], rest[2 * n + 2]
        if n:
            copies = _peer_copies(rest[:n], rest[n + 2:2 * n + 2], rest[2 * n + 3:], [])

            @pl.when(pl.program_id(0) == 0)
            def _():
                for cp in copies:
                    cp.start()

            @pl.when(pl.program_id(0) == nc - 1)
            def _():
                for cp in copies:
                    cp.wait()

        @pl.when(pl.program_id(0) == 0)
        def _():
            h_sc[...] = jnp.zeros_like(h_sc)

        xs = act_ref[:, :SSD_INNER]
        _, _, cs, _, _, xdt, ds_x, ecs_x, elast = _ssd_common(
            dt_ref[...], bias_ref[...], alog_ref[...], tril_ref[...], et_ref[...], xs)
        cs_t = cs.T
        causal = (lax.broadcasted_iota(jnp.int32, (SSD_L, SSD_L), 0)
                  >= lax.broadcasted_iota(jnp.int32, (SSD_L, SSD_L), 1))
        lane = lax.broadcasted_iota(jnp.int32, (SSD_L, LANES), 1)
        xdt_b = _mx(xdt)
        xds_b = _mx(xdt * ds_x)
        hp_ref[0] = h_sc[...]
        for g in range(SSD_GROUPS):
            gs = slice(g * gw, (g + 1) * gw)
            bg = _mx(act_ref[:, SSD_INNER + g * SSD_N:SSD_INNER + (g + 1) * SSD_N])
            cg = _mx(act_ref[:, SSD_INNER + 512 + g * SSD_N:SSD_INNER + 512 + (g + 1) * SSD_N])
            cb = _dot_nt(cg, bg)
            hprev = h_sc[:, gs]
            yoff = _dot(cg, _mx(hprev)) * ecs_x[:, gs]
            h_sc[:, gs] = hprev * elast[:, gs] + _dot_tn(bg, xds_b[:, gs])
            for pr in range(4):
                ps = slice(g * gw + pr * LANES, g * gw + (pr + 1) * LANES)
                xp = xdt_b[:, ps]
                ydiag = jnp.zeros((SSD_L, LANES), F32)
                for j in range(2):
                    dm = _ssd_decay(cs, cs_t, g * 8 + pr * 2 + j, causal)
                    mine = (lane >= j * 64) & (lane < (j + 1) * 64)
                    ydiag = ydiag + _dot(_mx(cb * dm), jnp.where(mine, xp, jnp.zeros_like(xp)))
                y_ref[:, ps] = ydiag + yoff[:, pr * LANES:(pr + 1) * LANES] + dx_ref[:, ps] * xs[:, ps]

    const = lambda shape: pl.BlockSpec(shape, lambda c: (0, 0))
    res = pl.pallas_call(
        body, name="ssd_scan_fwd", grid=(nc,),
        in_specs=[pl.BlockSpec((SSD_L, SSD_CONV), lambda c: (c, 0)), pl.BlockSpec((SSD_L, LANES), lambda c: (c, 0)),
                  const((1, LANES)), const((1, LANES)), const((1, SSD_INNER)), const((SSD_L, SSD_L)),
                  const((SSD_INNER, LANES))] + [ANY] * n,
        out_specs=[pl.BlockSpec((SSD_L, SSD_INNER), lambda c: (c, 0)),
                   pl.BlockSpec((1, SSD_N, SSD_INNER), lambda c: (c, 0, 0))] + [ANY] * n,
        out_shape=[jax.ShapeDtypeStruct((SEQ, SSD_INNER), F32), jax.ShapeDtypeStruct((nc, SSD_N, SSD_INNER), F32)]
        + _exchange_shapes([], bcast),
        scratch_shapes=[pltpu.VMEM((SSD_N, SSD_INNER), F32)] + (_exchange_sems(n) if n else []),
        compiler_params=_params(("arbitrary",)),
    )(act, dt_raw, bias, alog, d_x, tril, expand_t, *bcast)
    return res[0], res[1], res[2:]


def _ssd_scan_bwd(dy, act, dt_raw, hprev_all, bias, alog, d_x, tril, expand_t, x1b, scatter=()):
    nc = SEQ // SSD_L
    gw = SSD_INNER // SSD_GROUPS
    sc_arrays, sc_ranges = _scatter_args(scatter)
    nx = len(sc_arrays)

    def body(dy_ref, act_ref, dt_ref, hp_ref, bias_ref, alog_ref, dx_ref, tril_ref, et_ref, x1_ref, *rest):
        dact_ref, ddt_ref, dvec_ref, gdt_ref = rest[nx:nx + 4]
        dh_sc, dd_sc, dcs_sc, dcst_sc = rest[2 * nx + 4:2 * nx + 8]
        i = pl.program_id(0)
        if nx:
            copies = _peer_copies(rest[:nx], rest[nx + 4:2 * nx + 4], rest[2 * nx + 8:], sc_ranges)

            @pl.when(i == 0)
            def _():
                for cp in copies:
                    cp.start()

        @pl.when(i == 0)
        def _():
            dh_sc[...] = jnp.zeros_like(dh_sc)
            dd_sc[...] = jnp.zeros_like(dd_sc)
            gdt_ref[...] = jnp.zeros_like(gdt_ref)
            dvec_ref[...] = jnp.zeros_like(dvec_ref)

        xs = act_ref[:, :SSD_INNER]
        dt_raw_v, bias_v = dt_ref[...], bias_ref[...]
        dt, a_neg, cs, dt_x, _, xdt, ds_x, ecs_x, elast = _ssd_common(
            dt_raw_v, bias_v, alog_ref[...], tril_ref[...], et_ref[...], xs)
        cs_t = cs.T
        rowi = lax.broadcasted_iota(jnp.int32, (SSD_L, SSD_L), 0)
        coli = lax.broadcasted_iota(jnp.int32, (SSD_L, SSD_L), 1)
        causal = rowi >= coli
        lane = lax.broadcasted_iota(jnp.int32, (SSD_L, LANES), 1)
        row_g = lax.broadcasted_iota(jnp.int32, (SSD_L, gw), 0)
        dyv = dy_ref[...]
        dd_sc[0:1, :] += jnp.sum(dyv * xs, axis=0, keepdims=True)
        xdt_b = _mx(xdt)
        xds = xdt * ds_x
        xds_b = _mx(xds)
        dy_b = _mx(dyv)
        dye_b = _mx(dyv * ecs_x)
        dcs_sc[...] = jnp.zeros_like(dcs_sc)
        dcst_sc[...] = jnp.zeros_like(dcst_sc)
        dcs_parts = []
        dxdt_parts = []
        for g in range(SSD_GROUPS):
            gs = slice(g * gw, (g + 1) * gw)
            bcol = slice(SSD_INNER + g * SSD_N, SSD_INNER + (g + 1) * SSD_N)
            ccol = slice(SSD_INNER + 512 + g * SSD_N, SSD_INNER + 512 + (g + 1) * SSD_N)
            bg, cg = _mx(act_ref[:, bcol]), _mx(act_ref[:, ccol])
            cb = _dot_nt(cg, bg)
            hp = hp_ref[0, :, gs]
            hp_b = _mx(hp)
            dh = dh_sc[:, gs]
            dh_b = _mx(dh)
            yoff = _dot(cg, hp_b) * ecs_x[:, gs]
            bdh = _dot(bg, dh_b)
            tt = xds[:, gs] * bdh
            last_row = (jnp.sum(tt, axis=0, keepdims=True)
                        + jnp.sum(dh * hp, axis=0, keepdims=True) * elast[:, gs])
            dcs_parts.append(dyv[:, gs] * yoff - tt + jnp.where(row_g == SSD_L - 1, last_row, 0.0))
            dc_g = _dot_nt(dye_b[:, gs], hp_b)
            db_g = _dot_nt(xds_b[:, gs], dh_b)
            dh_sc[:, gs] = _dot_tn(cg, dye_b[:, gs]) + dh * elast[:, gs]
            wsum = jnp.zeros((SSD_L, SSD_L), F32)
            dxdt_g = []
            for pr in range(4):
                ps = slice(g * gw + pr * LANES, g * gw + (pr + 1) * LANES)
                xp, dyp = xdt_b[:, ps], dy_b[:, ps]
                dxp = jnp.zeros((SSD_L, LANES), F32)
                for j in range(2):
                    hh = g * 8 + pr * 2 + j
                    dm = _ssd_decay(cs, cs_t, hh, causal)
                    mine = (lane >= j * 64) & (lane < (j + 1) * 64)
                    dy_h = jnp.where(mine, dyp, jnp.zeros_like(dyp))
                    wd = _dot_nt(dy_h, xp) * dm
                    wsum = wsum + wd
                    gmat = wd * cb
                    dcs_sc[:, hh:hh + 1] = jnp.sum(gmat, axis=1, keepdims=True)
                    dcst_sc[hh:hh + 1, :] = -jnp.sum(gmat, axis=0, keepdims=True)
                    dxp = dxp + _dot_tn(_mx(cb * dm), dy_h)
                dxdt_g.append(dxp)
            dxdt_parts.append(jnp.concatenate(dxdt_g, axis=1) + bdh * ds_x[:, gs])
            ws_b = _mx(wsum)
            dact_ref[:, ccol] = dc_g + _dot(ws_b, bg)
            dact_ref[:, bcol] = db_g + _dot_tn(ws_b, cg)
        dxdt = jnp.concatenate(dxdt_parts, axis=1)
        dcs_x = jnp.concatenate(dcs_parts, axis=1)
        et = et_ref[...]
        dcs_tot = dcs_sc[...] + dcst_sc[...].T + _fold_heads(dcs_x, et)
        da_dt = _dot_hi((coli >= rowi).astype(F32), dcs_tot)
        ddt = da_dt * a_neg + _fold_heads(dxdt * xs, et)
        ddt_raw = ddt * _sigmoid(dt_raw_v + bias_v)
        ddt_ref[...] = ddt_raw
        gdt_ref[...] += _dot_tn(x1_ref[...], _mx(ddt_raw))
        dvec_ref[0:1, :] += jnp.sum(ddt_raw, axis=0, keepdims=True)
        dvec_ref[1:2, :] += jnp.sum(da_dt * dt, axis=0, keepdims=True) * a_neg
        dact_ref[:, :SSD_INNER] = dyv * dx_ref[...] + dxdt * dt_x

        @pl.when(i == nc - 1)
        def _():
            dvec_ref[2:3, :] = _fold_heads(dd_sc[...], et)[0:1, :]
            if nx:
                for cp in copies:
                    cp.wait()

    const = lambda shape: pl.BlockSpec(shape, lambda c: (0, 0))
    rev = lambda c: (nc - 1 - c, 0)
    res = pl.pallas_call(
        body, name="ssd_scan_bwd", grid=(nc,),
        in_specs=[pl.BlockSpec((SSD_L, SSD_INNER), rev), pl.BlockSpec((SSD_L, SSD_CONV), rev),
                  pl.BlockSpec((SSD_L, LANES), rev),
                  pl.BlockSpec((1, SSD_N, SSD_INNER), lambda c: (nc - 1 - c, 0, 0)),
                  const((1, LANES)), const((1, LANES)), const((1, SSD_INNER)), const((SSD_L, SSD_L)),
                  const((SSD_INNER, LANES)), pl.BlockSpec((SSD_L, D_MODEL), rev)] + [ANY] * nx,
        out_specs=[pl.BlockSpec((SSD_L, SSD_CONV), rev), pl.BlockSpec((SSD_L, LANES), rev), const((SUBLANES, LANES)),
                   const((D_MODEL, LANES))] + [ANY] * nx,
        out_shape=[jax.ShapeDtypeStruct((SEQ, SSD_CONV), F32), jax.ShapeDtypeStruct((SEQ, LANES), F32),
                   jax.ShapeDtypeStruct((SUBLANES, LANES), F32), jax.ShapeDtypeStruct((D_MODEL, LANES), F32)]
        + _exchange_shapes(sc_arrays, []),
        scratch_shapes=[pltpu.VMEM((SSD_N, SSD_INNER), F32), pltpu.VMEM((SUBLANES, SSD_INNER), F32),
                        pltpu.VMEM((SSD_L, LANES), F32), pltpu.VMEM((LANES, SSD_L), F32)]
        + (_exchange_sems(nx) if nx else []),
        compiler_params=_params(("arbitrary",)),
    )(dy, act, dt_raw, hprev_all, bias, alog, d_x, tril, expand_t, x1b, *sc_arrays)
    return res[0], res[1], res[2], res[3], res[4:]


L1_T = 512


def _resident(shape):
    return pl.BlockSpec(shape, lambda i: (0, 0), pipeline_mode=pl.Buffered(1))


def _gated_norm(y, z, nw):
    y2 = y * _silu(z)
    gw = SSD_INNER // SSD_GROUPS
    outs, xhs, rss = [], [], []
    for g in range(SSD_GROUPS):
        gs = slice(g * gw, (g + 1) * gw)
        v = y2[:, gs]
        rs = lax.rsqrt(jnp.mean(v * v, axis=-1, keepdims=True) + 1e-6)
        xhs.append(v * rs)
        rss.append(rs)
        outs.append(v * rs * nw[:, gs])
    return outs, xhs, rss


def _l1_out(y, z, nw, w_out, x1, g, b, target):
    t = L1_T

    def body(y_ref, z_ref, nw_ref, w_ref, x1_ref, g_ref, b_ref, tg_ref, dv_ref, dgb_ref, loss_ref, gw_ref):
        @pl.when(pl.program_id(0) == 0)
        def _():
            dgb_ref[...] = jnp.zeros_like(dgb_ref)
            loss_ref[...] = jnp.zeros_like(loss_ref)
            gw_ref[...] = jnp.zeros_like(gw_ref)

        outs, _, _ = _gated_norm(y_ref[...], z_ref[...], nw_ref[...])
        yn = _mx(jnp.concatenate(outs, axis=1))
        v = DN_ALPHA * x1_ref[...] + _dot(yn, w_ref[...])
        gv = g_ref[...]
        err = _ln(v, gv, b_ref[...]) - tg_ref[...]
        rowsum = jnp.sum(err * err, axis=1, keepdims=True)
        loss_ref[...] += 0.5 * jnp.sum(rowsum, axis=0, keepdims=True) / D_MODEL
        dv, dg, db = _ln_bwd(v, gv, err / D_MODEL)
        dv_ref[...] = dv
        dgb_ref[0:1, :] += dg
        dgb_ref[1:2, :] += db
        gw_ref[...] += _dot_tn(yn, _mx(dv))

    wide = pl.BlockSpec((t, SSD_INNER), lambda i: (i, 0))
    full = pl.BlockSpec((t, D_MODEL), lambda i: (i, 0))
    vec = pl.BlockSpec((1, D_MODEL), lambda i: (0, 0))
    return pl.pallas_call(
        body, name="l1_out", grid=(SEQ // t,),
        in_specs=[wide, wide, pl.BlockSpec((1, SSD_INNER), lambda i: (0, 0)),
                  _resident((SSD_INNER, D_MODEL)), full, vec, vec, full],
        out_specs=[full, pl.BlockSpec((SUBLANES, D_MODEL), lambda i: (0, 0)),
                   pl.BlockSpec((SUBLANES, LANES), lambda i: (0, 0)), _resident((SSD_INNER, D_MODEL))],
        out_shape=[jax.ShapeDtypeStruct((SEQ, D_MODEL), F32), jax.ShapeDtypeStruct((SUBLANES, D_MODEL), F32),
                   jax.ShapeDtypeStruct((SUBLANES, LANES), F32), jax.ShapeDtypeStruct((SSD_INNER, D_MODEL), F32)],
        compiler_params=_params(("arbitrary",)),
    )(y, z, nw, w_out, x1, g, b, target)


def _l1_gate_bwd(dv1, w_out, y, z, nw, x1b):
    t = L1_T
    gw = SSD_INNER // SSD_GROUPS

    def body(dv_ref, w_ref, y_ref, z_ref, nw_ref, x1_ref, dy_ref, dz_ref, dnw_ref, gw_ref):
        @pl.when(pl.program_id(0) == 0)
        def _():
            dnw_ref[...] = jnp.zeros_like(dnw_ref)
            gw_ref[...] = jnp.zeros_like(gw_ref)

        dyn = _dot_nt(_mx(dv_ref[...]), w_ref[...])
        yv, zv, nwv = y_ref[...], z_ref[...], nw_ref[...]
        _, xhs, rss = _gated_norm(yv, zv, nwv)
        sz, dsz = _silu(zv), _dsilu(zv)
        for g in range(SSD_GROUPS):
            gs = slice(g * gw, (g + 1) * gw)
            d_out = dyn[:, gs]
            xh = xhs[g]
            dnw_ref[0:1, gs] += jnp.sum(d_out * xh, axis=0, keepdims=True)
            dxh = d_out * nwv[:, gs]
            dy2 = rss[g] * (dxh - xh * jnp.mean(dxh * xh, axis=-1, keepdims=True))
            dy_ref[:, gs] = dy2 * sz[:, gs]
            dz_ref[:, gs] = _mx(dy2 * yv[:, gs] * dsz[:, gs])
        gw_ref[...] += _dot_tn(x1_ref[...], dz_ref[...])

    wide = pl.BlockSpec((t, SSD_INNER), lambda i: (i, 0))
    return pl.pallas_call(
        body, name="l1_gate_bwd", grid=(SEQ // t,),
        in_specs=[pl.BlockSpec((t, D_MODEL), lambda i: (i, 0)), _resident((SSD_INNER, D_MODEL)),
                  wide, wide, pl.BlockSpec((1, SSD_INNER), lambda i: (0, 0)), pl.BlockSpec((t, D_MODEL), lambda i: (i, 0))],
        out_specs=[wide, wide, pl.BlockSpec((SUBLANES, SSD_INNER), lambda i: (0, 0)),
                   _resident((D_MODEL, SSD_INNER))],
        out_shape=[jax.ShapeDtypeStruct((SEQ, SSD_INNER), F32), jax.ShapeDtypeStruct((SEQ, SSD_INNER), MXU_DTYPE),
                   jax.ShapeDtypeStruct((SUBLANES, SSD_INNER), F32), jax.ShapeDtypeStruct((D_MODEL, SSD_INNER), F32)],
        compiler_params=_params(("arbitrary",)),
    )(dv1, w_out, y, z, nw, x1b)


MESH = pl.DeviceIdType.MESH
ANY = pl.BlockSpec(memory_space=pl.ANY)


def _flip(v, bit):
    return 1 - v if bit else v


def _all_gather(blocks, name):
    n = len(blocks)

    def body(*refs):
        x_refs, out_refs = refs[:n], refs[n:2 * n]
        send_sems, recv_sems, local_sems = refs[2 * n:]
        mx, my, mc = lax.axis_index("x"), lax.axis_index("y"), lax.axis_index("c")
        me, sibling = (mx, my, mc), (mx, my, 1 - mc)
        chips = [(1 - mx, my), (mx, 1 - my), (1 - mx, 1 - my)]

        def copy(a, k, block, to, own=False):
            px, py, pc = block
            slot = out_refs[a].at[4 * px + 2 * py + pc]
            return pltpu.make_async_remote_copy(
                src_ref=x_refs[a] if own else slot, dst_ref=slot,
                send_sem=send_sems.at[7 * a + k], recv_sem=recv_sems.at[7 * a + k], device_id=to, device_id_type=MESH)

        mine = [pltpu.make_async_copy(x_refs[a], out_refs[a].at[4 * mx + 2 * my + mc], local_sems.at[a])
                for a in range(n)]
        first = []
        for a in range(n):
            mine[a].start()
            first.append(copy(a, 0, me, sibling, own=True))
            first += [copy(a, 1 + j, me, (*chip, mc), own=True) for j, chip in enumerate(chips)]
        for cp in first:
            cp.start()
        passed = []
        for j, chip in enumerate(chips):
            for a in range(n):
                copy(a, 1 + j, (*chip, mc), me).wait_recv()
                fwd = copy(a, 4 + j, (*chip, mc), sibling)
                fwd.start()
                passed.append(fwd)
        for a in range(n):
            copy(a, 0, sibling, me).wait_recv()
            for j, chip in enumerate(chips):
                copy(a, 4 + j, (*chip, 1 - mc), me).wait_recv()
        for cp in first + passed:
            cp.wait_send()
        for cp in mine:
            cp.wait()

    return pl.pallas_call(
        body, name=name, in_specs=[ANY] * n, out_specs=[ANY] * n,
        out_shape=[jax.ShapeDtypeStruct((N_DEV,) + b.shape, b.dtype) for b in blocks],
        scratch_shapes=[pltpu.SemaphoreType.DMA((7 * n,)), pltpu.SemaphoreType.DMA((7 * n,)),
                        pltpu.SemaphoreType.DMA((n,))],
    )(*blocks)


def _l0_in(x, w0p, bcast=()):
    n = len(bcast)
    tm, tn = 1024, 1024
    gi, gj = SEQ // tm, 2048 // tn

    def body(x_ref, w_ref, *rest):
        o_ref, xb_ref = rest[n], rest[n + 1]
        i, j = pl.program_id(0), pl.program_id(1)
        if n:
            copies = _peer_copies(rest[:n], rest[n + 2:2 * n + 2], rest[2 * n + 2:], [])

            @pl.when((i == 0) & (j == 0))
            def _():
                for cp in copies:
                    cp.start()

        xb = _mx(x_ref[...])
        xb_ref[...] = xb
        o_ref[...] = _dot(xb, w_ref[...])

        if n:
            @pl.when((i == gi - 1) & (j == gj - 1))
            def _():
                for cp in copies:
                    cp.wait()

    res = pl.pallas_call(
        body, name="l0_in", grid=(gi, gj),
        in_specs=[pl.BlockSpec((tm, D_MODEL), lambda i, j: (i, 0)), pl.BlockSpec((D_MODEL, tn), lambda i, j: (0, j))]
        + [ANY] * n,
        out_specs=[pl.BlockSpec((tm, tn), lambda i, j: (i, j)), pl.BlockSpec((tm, D_MODEL), lambda i, j: (i, 0))]
        + [ANY] * n,
        out_shape=[jax.ShapeDtypeStruct((SEQ, 2048), F32), jax.ShapeDtypeStruct((SEQ, D_MODEL), MXU_DTYPE)]
        + _exchange_shapes([], bcast),
        scratch_shapes=_exchange_sems(n) if n else [],
        compiler_params=_params(("arbitrary", "arbitrary")),
    )(x, w0p, *bcast)
    return res[0], res[1], res[2:]


def _l0_dx(dxr, dgate, dtail, w0p, dv0, scatter=(), bcast=()):
    arrays, ranges = _scatter_args(scatter)
    n = len(arrays) + len(bcast)
    tm = 1024
    steps = SEQ // tm

    def body(dxr_ref, dg_ref, dt_ref, w_ref, dv_ref, *rest):
        o_ref = rest[n]
        i = pl.program_id(0)
        if n:
            copies = _peer_copies(rest[:n], rest[n + 1:2 * n + 1], rest[2 * n + 1:], ranges)

            @pl.when(i == 0)
            def _():
                for cp in copies:
                    cp.start()

        o_ref[...] = (DN_ALPHA * dv_ref[...] + _dot_nt(dg_ref[...], w_ref[:, 0:1024])
                      + _dot_nt(dxr_ref[...], w_ref[:, 1024:1536]) + _dot_nt(dt_ref[...], w_ref[:, 1536:2048]))

        if n:
            @pl.when(i == steps - 1)
            def _():
                for cp in copies:
                    cp.wait()

    rows = lambda w: pl.BlockSpec((tm, w), lambda i: (i, 0))
    res = pl.pallas_call(
        body, name="l0_dx", grid=(steps,),
        in_specs=[rows(512), rows(1024), rows(512), pl.BlockSpec((D_MODEL, 2048), lambda i: (0, 0)), rows(D_MODEL)]
        + [ANY] * n,
        out_specs=[rows(D_MODEL)] + [ANY] * n,
        out_shape=[jax.ShapeDtypeStruct((SEQ, D_MODEL), F32)] + _exchange_shapes(arrays, bcast),
        scratch_shapes=_exchange_sems(n) if n else [],
        compiler_params=_params(("arbitrary",)),
    )(dxr, dgate, dtail, w0p, dv0, *arrays, *bcast)
    return res[0], res[1:]


def _scatter_args(scatter):
    arrays = [s[0] if isinstance(s, tuple) else s for s in scatter]
    ranges = [s[1] if isinstance(s, tuple) else (0, N_DEV) for s in scatter]
    return arrays, ranges


def _exchange_shapes(scatter, bcast):
    return ([jax.ShapeDtypeStruct((N_DEV,) + a.shape[1:], a.dtype) for a in scatter]
            + [jax.ShapeDtypeStruct((N_DEV,) + a.shape, a.dtype) for a in bcast])


def _exchange_sems(n):
    return [pltpu.SemaphoreType.DMA((7 * n,)), pltpu.SemaphoreType.DMA((7 * n,)), pltpu.SemaphoreType.DMA((n,))]


class _GuardedCopy:
    def __init__(self, copy, send=None, recv=None, local=False):
        self.copy, self.send, self.recv, self.local = copy, send, recv, local

    @staticmethod
    def _run(pred, fn):
        if pred is None:
            fn()
        else:
            pl.when(pred)(fn)

    def start(self):
        self._run(self.send, self.copy.start)

    def wait(self):
        if self.local:
            self._run(self.send, self.copy.wait)
        else:
            self._run(self.send, self.copy.wait_send)
            self._run(self.recv, self.copy.wait_recv)


def _peer_copies(in_refs, out_refs, sems, ranges):
    send_sems, recv_sems, local_sems = sems
    n, ns = len(in_refs), len(ranges)
    mx, my, mc = lax.axis_index("x"), lax.axis_index("y"), lax.axis_index("c")
    me = 4 * mx + 2 * my + mc

    def src(a, slot):
        return in_refs[a].at[slot - ranges[a][0]] if a < ns else in_refs[a]

    def member(a, dev):
        if a >= ns or ranges[a] == (0, N_DEV):
            return None
        return (dev >= ranges[a][0]) & (dev < ranges[a][1])

    copies = [_GuardedCopy(pltpu.make_async_copy(src(a, me), out_refs[a].at[me], local_sems.at[a]),
                           send=member(a, me), local=True) for a in range(n)]
    for k in range(1, N_DEV):
        px, py, pc = _flip(mx, (k >> 2) & 1), _flip(my, (k >> 1) & 1), _flip(mc, k & 1)
        peer = 4 * px + 2 * py + pc
        for a in range(n):
            copies.append(_GuardedCopy(pltpu.make_async_remote_copy(
                src_ref=src(a, peer), dst_ref=out_refs[a].at[me],
                send_sem=send_sems.at[7 * a + k - 1], recv_sem=recv_sems.at[7 * a + k - 1],
                device_id=(px, py, pc), device_id_type=MESH), send=member(a, peer), recv=member(a, me)))
    return copies


def _segments(col_map, width):
    segs = []
    for lo, hi, arr, alo in col_map:
        for s in range(N_DEV):
            a, b = max(lo, s * width), min(hi, (s + 1) * width)
            if a < b:
                segs.append((s, a - s * width, b - a, arr, alo + a - lo))
    return segs


COPY_ROWS = 256


def _unshard(g8, col_map, widths, name):
    _, r, w = g8.shape
    rb = min(r, COPY_ROWS)
    segs = _segments(col_map, w)

    def body(g_ref, *o_refs):
        for o_ref in o_refs:
            o_ref[...] = jnp.zeros_like(o_ref)
        for s, llo, n, arr, alo in segs:
            o_refs[arr][:, alo:alo + n] = g_ref[s, :, llo:llo + n]

    return pl.pallas_call(
        body, name=name, grid=(r // rb,),
        in_specs=[pl.BlockSpec((N_DEV, rb, w), lambda i: (0, i, 0))],
        out_specs=[pl.BlockSpec((rb, n), lambda i: (i, 0)) for n in widths],
        out_shape=[jax.ShapeDtypeStruct((r, n), g8.dtype) for n in widths],
        compiler_params=_params(("parallel",)),
    )(g8)


def _reshard(srcs, col_map, w, dtype, name, shards=(0, N_DEV)):
    r = srcs[0].shape[0]
    rb = min(r, COPY_ROWS)
    lo, hi = shards
    segs = [sg for sg in _segments(col_map, w) if lo <= sg[0] < hi]

    def body(*refs):
        o_ref = refs[-1]
        for s, llo, n, arr, alo in segs:
            o_ref[s - lo, :, llo:llo + n] = refs[arr][:, alo:alo + n].astype(dtype)

    return pl.pallas_call(
        body, name=name, grid=(r // rb,),
        in_specs=[pl.BlockSpec((rb, a.shape[1]), lambda i: (i, 0)) for a in srcs],
        out_specs=pl.BlockSpec((hi - lo, rb, w), lambda i: (0, i, 0)),
        out_shape=jax.ShapeDtypeStruct((hi - lo, r, w), dtype),
        compiler_params=_params(("parallel",)),
    )(*srcs)


def _adamw(parts, w, m, v, name):
    r, c = w.shape
    tr = COPY_ROWS if r % COPY_ROWS == 0 else r

    def body(p_ref, w_ref, m_ref, v_ref, g_ref, d_ref, mo_ref, vo_ref):
        g = p_ref[0].astype(F32)
        for s in range(1, N_DEV):
            g = g + p_ref[s].astype(F32)
        g_ref[...] = g
        d_ref[...], mo_ref[...], vo_ref[...] = _adamw_math(g, w_ref[...], m_ref[...], v_ref[...])

    blk = pl.BlockSpec((tr, c), lambda i: (i, 0))
    out = jax.ShapeDtypeStruct((r, c), F32)
    return pl.pallas_call(
        body, name=name, grid=(r // tr,),
        in_specs=[pl.BlockSpec((N_DEV, tr, c), lambda i: (0, i, 0)), blk, blk, blk],
        out_specs=[blk, blk, blk, blk], out_shape=[out, out, out, out],
        compiler_params=_params(("parallel",)),
    )(parts, w, m, v)


def _adamw_math(g, w, m, v):
    mn = ADAM_B1 * m + (1.0 - ADAM_B1) * g
    vn = ADAM_B2 * v + (1.0 - ADAM_B2) * (g * g)
    m_hat = mn / (1.0 - ADAM_B1 ** ADAM_STEP)
    v_hat = vn / (1.0 - ADAM_B2 ** ADAM_STEP)
    return -ADAM_LR * (m_hat / (jnp.sqrt(v_hat) + ADAM_EPS) + ADAM_WD * w), mn, vn


SMALL = (("ab_conv_w", 0, 4, 64), ("ssd_conv_w", 4, 4, 384), ("ssd_conv_b", 8, 1, 384), ("ssd_norm", 9, 1, 256),
         ("ssd_ln_g", 10, 1, 128), ("ssd_ln_b", 11, 1, 128))
VECS = (("ab_conv_b", 512), ("ab_gate_a_b", 512), ("ab_gate_x_b", 512), ("ab_lambda", 512), ("mla_q_norm", 256),
        ("mla_kv_norm", 128), ("ab_ln_g", 1024), ("ab_ln_b", 1024), ("ssd_dt_bias", 32), ("ssd_a_log", 32),
        ("ssd_d", 32))
GATES = ("ab_gate_a_w", "ab_gate_x_w")
SMALL_NAMES = tuple(n for n, *_ in SMALL) + tuple(n for n, _ in VECS) + GATES
VMEM_WHOLE = pl.BlockSpec(memory_space=pltpu.VMEM)


def _view2d(name, a):
    if name in GATES:
        return a.reshape(RNN_W, 64)
    return a[0] if a.ndim == 3 else a


def _unshard_small(g):
    widths = (512, 3072, 3072, 2048, 1024, 1024)

    def body(*refs):
        ins, outs = refs[:6], refs[6:]
        outs[0][...] = jnp.zeros_like(outs[0])
        outs[1][...] = jnp.zeros_like(outs[1])
        for (_, _, nr, c), i_ref, o_ref in zip(SMALL, ins, outs):
            for j in range(N_DEV):
                o_ref[0:nr, j * c:(j + 1) * c] = i_ref[j]

    return pl.pallas_call(
        body, name="unshard_small", in_specs=[VMEM_WHOLE] * 6, out_specs=[VMEM_WHOLE] * 6,
        out_shape=[jax.ShapeDtypeStruct((SUBLANES if nr == 4 else 1, w), F32) for (_, _, nr, _), w in zip(SMALL, widths)],
    )(*g)


def _prep_repl(ga, gx, dt_bias, a_log, d):
    def body(ga_ref, gx_ref, b_ref, al_ref, d_ref, wa_ref, wx_ref, b128_ref, al128_ref, dx_ref):
        wa_ref[...] = jnp.zeros_like(wa_ref)
        wx_ref[...] = jnp.zeros_like(wx_ref)
        for hd in range(8):
            hs = slice(hd * 64, (hd + 1) * 64)
            wa_ref[hs, hs] = _mx(ga_ref[hs, :])
            wx_ref[hs, hs] = _mx(gx_ref[hs, :])
        b128_ref[...] = jnp.zeros_like(b128_ref)
        al128_ref[...] = jnp.zeros_like(al128_ref)
        b128_ref[:, 0:SSD_HEADS] = b_ref[...]
        al128_ref[:, 0:SSD_HEADS] = al_ref[...]
        dv = d_ref[...]
        for hd in range(SSD_HEADS):
            dx_ref[:, hd * SSD_P:(hd + 1) * SSD_P] = jnp.broadcast_to(dv[:, hd:hd + 1], (1, SSD_P))

    return pl.pallas_call(
        body, name="prep_repl", in_specs=[VMEM_WHOLE] * 5, out_specs=[VMEM_WHOLE] * 5,
        out_shape=[jax.ShapeDtypeStruct((RNN_W, RNN_W), MXU_DTYPE), jax.ShapeDtypeStruct((RNN_W, RNN_W), MXU_DTYPE),
                   jax.ShapeDtypeStruct((1, LANES), F32), jax.ShapeDtypeStruct((1, LANES), F32),
                   jax.ShapeDtypeStruct((1, SSD_INNER), F32)],
    )(ga, gx, dt_bias, a_log, d)


LOSS_ROW = 11


def _pack_small(dvec0, g_wa, g_wx, dqnw, dknw, dgb0, dvec1, dcw1, dnw, dgb1, loss8):
    def body(dvec0_ref, gwa_ref, gwx_ref, dqn_ref, dkn_ref, dgb0_ref, dvec1_ref, dcw1_ref, dnw_ref, dgb1_ref,
             loss_ref, sm_ref, vec_ref, gg_ref):
        sm_ref[...] = jnp.zeros_like(sm_ref)
        vec_ref[...] = jnp.zeros_like(vec_ref)
        sharded = ((dvec0_ref, 4), (dcw1_ref, 0), (dcw1_ref, 4), (dnw_ref, 0), (dgb1_ref, 0), (dgb1_ref, 1))
        for (_, r0, nr, c), (src, sr) in zip(SMALL, sharded):
            for j in range(N_DEV):
                sm_ref[j, r0:r0 + nr, 0:c] = src[sr:sr + nr, j * c:(j + 1) * c]
        vectors = ((dvec0_ref, 3), (dvec0_ref, 0), (dvec0_ref, 1), (dvec0_ref, 2), (dqn_ref, 0), (dkn_ref, 0),
                   (dgb0_ref, 0), (dgb0_ref, 1), (dvec1_ref, 0), (dvec1_ref, 1), (dvec1_ref, 2))
        for row, ((_, c), (src, sr)) in enumerate(zip(VECS, vectors)):
            vec_ref[row:row + 1, 0:c] = src[sr:sr + 1, 0:c]
        vec_ref[LOSS_ROW:LOSS_ROW + 1, 0:LANES] = loss_ref[0:1, :]
        for hd in range(8):
            hs = slice(hd * 64, (hd + 1) * 64)
            gg_ref[hs, 0:64] = _mx(gwa_ref[hs, hs])
            gg_ref[hs, 64:128] = _mx(gwx_ref[hs, hs])

    return pl.pallas_call(
        body, name="pack_small", in_specs=[VMEM_WHOLE] * 11, out_specs=[VMEM_WHOLE] * 3,
        out_shape=[jax.ShapeDtypeStruct((N_DEV, 16, 384), F32), jax.ShapeDtypeStruct((16, 1024), F32),
                   jax.ShapeDtypeStruct((RNN_W, LANES), MXU_DTYPE)],
    )(dvec0, g_wa, g_wx, dqnw, dknw, dgb0, dvec1, dcw1, dnw, dgb1, loss8)


def _adamw_small(recv_sm, recv_vec, recv_gg, wmv):
    plan = ([(0, r0, nr, c) for _, r0, nr, c in SMALL] + [(1, row, 1, c) for row, (_, c) in enumerate(VECS)]
            + [(2, 0, RNN_W, 0), (2, 0, RNN_W, 64)])
    n = len(plan)

    def body(*refs):
        recv, ins, outs = refs[:3], refs[3:3 + 3 * n], refs[3 + 3 * n:]
        for i, (src, r0, nr, c) in enumerate(plan):
            cols = slice(c, c + 64) if src == 2 else slice(0, c)
            g = recv[src][0, r0:r0 + nr, cols].astype(F32)
            for s in range(1, N_DEV):
                g = g + recv[src][s, r0:r0 + nr, cols].astype(F32)
            w_ref, m_ref, v_ref = ins[3 * i:3 * i + 3]
            outs[4 * i][...] = g
            outs[4 * i + 1][...], outs[4 * i + 2][...], outs[4 * i + 3][...] = _adamw_math(
                g, w_ref[...], m_ref[...], v_ref[...])
        loss = recv[1][0, LOSS_ROW:LOSS_ROW + 1, 0:LANES]
        for s in range(1, N_DEV):
            loss = loss + recv[1][s, LOSS_ROW:LOSS_ROW + 1, 0:LANES]
        outs[4 * n][...] = loss

    flat = [a for t in wmv for a in t]
    return pl.pallas_call(
        body, name="adamw_small", in_specs=[VMEM_WHOLE] * (3 + 3 * n), out_specs=[VMEM_WHOLE] * (4 * n + 1),
        out_shape=[jax.ShapeDtypeStruct(t[0].shape, F32) for t in wmv for _ in range(4)]
        + [jax.ShapeDtypeStruct((1, LANES), F32)],
    )(recv_sm, recv_vec, recv_gg, *flat)


BIG_L0 = ("ab_w_in", "ab_w_out", "mla_w_uq", "mla_w_ukv")
BIG_L1 = ("ssd_w_in", "ssd_w_out")

MAP_W0 = ((0, 512, 0, 1024), (512, 1536, 0, 0), (1536, 1920, 0, 1536), (1920, 1952, 0, 1984))
MAP_W1 = ((0, 2048, 0, 0), (2048, 5120, 1, 0), (5120, 5152, 2, 0))
MAP_WQ = tuple((96 * hd, 96 * hd + 96, 0, 128 * hd) for hd in range(8))
MAP_WKV = (tuple((128 * hd, 128 * hd + 64, 0, 128 * hd) for hd in range(8))
           + tuple((128 * hd + 64, 128 * hd + 128, 0, 1024 + 64 * hd) for hd in range(8)))
MAP_G0 = ((0, 512, 0, 0), (512, 1536, 1, 0), (1536, 1920, 2, 0), (1920, 1952, 2, 448))
W0_EARLY, W0_LATE = (0, 6), (6, 8)


def kernel(x, positions, ab_w_in, ab_conv_w, ab_conv_b, ab_gate_a_w, ab_gate_a_b, ab_gate_x_w, ab_gate_x_b, ab_lambda, mla_q_norm, mla_kv_norm, mla_w_uq, mla_w_ukv, ab_w_out, ab_ln_g, ab_ln_b, ssd_w_in, ssd_conv_w, ssd_conv_b, ssd_dt_bias, ssd_a_log, ssd_d, ssd_norm, ssd_w_out, ssd_ln_g, ssd_ln_b, loss_target, m_ab_w_in, m_ab_conv_w, m_ab_conv_b, m_ab_gate_a_w, m_ab_gate_a_b, m_ab_gate_x_w, m_ab_gate_x_b, m_ab_lambda, m_mla_q_norm, m_mla_kv_norm, m_mla_w_uq, m_mla_w_ukv, m_ab_w_out, m_ab_ln_g, m_ab_ln_b, m_ssd_w_in, m_ssd_conv_w, m_ssd_conv_b, m_ssd_dt_bias, m_ssd_a_log, m_ssd_d, m_ssd_norm, m_ssd_w_out, m_ssd_ln_g, m_ssd_ln_b, v_ab_w_in, v_ab_conv_w, v_ab_conv_b, v_ab_gate_a_w, v_ab_gate_a_b, v_ab_gate_x_w, v_ab_gate_x_b, v_ab_lambda, v_mla_q_norm, v_mla_kv_norm, v_mla_w_uq, v_mla_w_ukv, v_ab_w_out, v_ab_ln_g, v_ab_ln_b, v_ssd_w_in, v_ssd_conv_w, v_ssd_conv_b, v_ssd_dt_bias, v_ssd_a_log, v_ssd_d, v_ssd_norm, v_ssd_w_out, v_ssd_ln_g, v_ssd_ln_b):
    args = dict(locals())
    bf = MXU_DTYPE
    big = {n: [args[pre + n][0] for pre in ("", "m_", "v_")] for n in BIG_L0 + BIG_L1}
    sml = {n: [_view2d(n, args[pre + n]) for pre in ("", "m_", "v_")] for n in SMALL_NAMES}

    w0_8, cw0_8 = _all_gather([big["ab_w_in"][0].astype(bf), sml["ab_conv_w"][0]], "gather_params")
    p = {"cw0_8": cw0_8, "l0_blocks": [big[n][0].astype(bf) for n in BIG_L0[1:]] + [sml[n][0] for n, *_ in SMALL[1:]]}
    p["w0p"], = _unshard(w0_8, MAP_W0, (2048,), "unshard_w0")
    p["wa"], p["wx"], p["dt_bias"], p["a_log"], p["d_x"] = _prep_repl(
        sml["ab_gate_a_w"][0], sml["ab_gate_x_w"][0], sml["ssd_dt_bias"][0], sml["ssd_a_log"][0], sml["ssd_d"][0])
    for key, n in (("cb0", "ab_conv_b"), ("ba", "ab_gate_a_b"), ("bx", "ab_gate_x_b"), ("lam", "ab_lambda"),
                   ("qn_w", "mla_q_norm"), ("kn_w", "mla_kv_norm"), ("g0", "ab_ln_g"), ("b0", "ab_ln_b")):
        p[key] = sml[n][0]

    _, recv_early, recv, _, grad_x = _local_step(
        x[0], positions[0], loss_target[0], p, [big[n][0].astype(bf) for n in BIG_L1])

    me = 4 * lax.axis_index("x") + 2 * lax.axis_index("y") + lax.axis_index("c")
    parts = dict(recv_early, ab_w_in=jnp.where(me >= W0_LATE[0], recv[0], recv_early["ab_w_in"]),
                 mla_w_uq=recv[1], mla_w_ukv=recv[2])

    outs = {}
    kinds = ("grad", "delta", "new_m", "new_v")
    for n in BIG_L0 + BIG_L1:
        for kind, res in zip(kinds, _adamw(parts[n], *big[n], "adamw_" + n)):
            outs[kind, n] = res[None]
    res = _adamw_small(*recv[3:], [sml[n] for n in SMALL_NAMES])
    for i, n in enumerate(SMALL_NAMES):
        for k, kind in enumerate(kinds):
            outs[kind, n] = res[4 * i + k].reshape(args[n].shape)

    loss = res[4 * len(SMALL_NAMES)][0, 0]
    order = ["ab_w_in", "ab_conv_w", "ab_conv_b", "ab_gate_a_w", "ab_gate_a_b", "ab_gate_x_w", "ab_gate_x_b",
             "ab_lambda", "mla_q_norm", "mla_kv_norm", "mla_w_uq", "mla_w_ukv", "ab_w_out", "ab_ln_g", "ab_ln_b",
             "ssd_w_in", "ssd_conv_w", "ssd_conv_b", "ssd_dt_bias", "ssd_a_log", "ssd_d", "ssd_norm", "ssd_w_out",
             "ssd_ln_g", "ssd_ln_b"]
    return (loss, grad_x[None], *[outs[kind, n] for kind in ("grad", "delta", "new_m", "new_v") for n in order])


def _local_step(x, pos, target, p, l1_blocks):
    bf = MXU_DTYPE
    inv_freq = 10000.0 ** (-jnp.arange(0, 32, 2, dtype=F32) / 32)
    ang = pos.astype(F32)[:, None] * inv_freq
    cos, sin = jnp.cos(ang), jnp.sin(ang)
    zeros = lambda n: jnp.zeros((SEQ, n), F32)
    tc = jnp.concatenate([jnp.ones((SEQ, 64), F32), cos, cos, zeros(32)], axis=1)
    tsa = jnp.concatenate([zeros(64), -sin, zeros(48)], axis=1)
    tsb = jnp.concatenate([zeros(80), sin, zeros(32)], axis=1)

    w0p, wa, wxg = (p[k] for k in ("w0p", "wa", "wx"))
    cb0, ba, bx, lam = (p[k] for k in ("cb0", "ba", "bx", "lam"))
    qn_w, kn_w, g0, b0 = (p[k] for k in ("qn_w", "kn_w", "g0", "b0"))
    dt_bias, a_log, d_x = (p[k] for k in ("dt_bias", "a_log", "d_x"))
    tril = jnp.tril(jnp.ones((SSD_L, SSD_L), F32))
    expand_t = (jnp.arange(SSD_INNER)[:, None] // SSD_P == jnp.arange(LANES)[None, :]).astype(jnp.bfloat16)

    proj0, xb, l0_8 = _l0_in(x, w0p, bcast=p["l0_blocks"])
    wo0 = l0_8[0].reshape(D_MODEL, D_MODEL)
    wq, = _unshard(l0_8[1], MAP_WQ, (1024,), "unshard_wq")
    wkv, = _unshard(l0_8[2], MAP_WKV, (1536,), "unshard_wkv")
    cw0, cw1, cb1, nw, g1, b1 = _unshard_small([p["cw0_8"]] + list(l0_8[3:]))
    xc, h = _rglru_fwd(proj0, cw0, cb0, wa, ba, wxg, bx, lam)
    qn, kn, qc, kc, vc = _mla_fwd(proj0, qn_w, kn_w, wq, wkv, tc, tsa, tsb)
    o, lse, (w1_8,) = _flash_fwd(qc, kc, vc, bcast=l1_blocks[:1])
    w1z, w1x, w1d = _unshard(w1_8, MAP_W1, (2048, 3072, 128), "unshard_w1")
    y0, v0, x1, x1b = _l0_out(h, o, proj0, x, wo0, g0, b0)

    z, dt_raw = _l1_in(x1b, w1z, w1d)
    xbc, pre, act = _ssd_conv_fwd(x1b, w1x, cw1, cb1)
    ys, hprev, (wo1_8,) = _ssd_scan_fwd(act, dt_raw, dt_bias, a_log, d_x, tril, expand_t, bcast=l1_blocks[1:])
    wo1 = wo1_8.reshape(SSD_INNER, D_MODEL)
    dv1, dgb1, loss8, g_wo1 = _l1_out(ys, z, nw, wo1, x1, g1, b1, target)

    dys, dz, dnw, g_z = _l1_gate_bwd(dv1, wo1, ys, z, nw, x1b)
    dact, ddt_raw, dvec1, g_dt, (recv_wo1,) = _ssd_scan_bwd(
        dys, act, dt_raw, hprev, dt_bias, a_log, d_x, tril, expand_t, x1b,
        scatter=[g_wo1.astype(bf).reshape(N_DEV, 256, D_MODEL)])
    dxbc, dcw1, g_xbc = _ssd_conv_bwd(dact, pre, xbc, cw1, x1b)

    dv0, dgb0 = _l1_dx_ln(dz, dxbc, ddt_raw, dv1, v0, w1z, w1x, w1d, g0)
    dh, do, dgate, g_wo0, g_gate = _gate_bwd(dv0, wo0, h, o, proj0, y0, xb)
    dxr, g_wa, g_wx, dvec0, g_rnn = _rglru_bwd(dh, xc, h, proj0, cw0, wa, ba, wxg, bx, lam, xb)
    early = [_reshard([g_z, g_xbc, g_dt], MAP_W1, 644, bf, "reshard_w1"), g_wo0.astype(bf).reshape(N_DEV, 128, D_MODEL),
             (_reshard([g_rnn, g_gate], MAP_G0, 244, bf, "reshard_w0_early", shards=W0_EARLY), W0_EARLY)]
    dq, dk, dvv, (recv_w1, recv_wo0, recv_w0) = _flash_bwd(qc, kc, vc, o, do, lse, scatter=early)
    recv_early = {"ssd_w_in": recv_w1, "ssd_w_out": recv_wo1, "ab_w_out": recv_wo0, "ab_w_in": recv_w0}
    dtail, g_wq, g_wkv, dqnw, dknw, g_tail = _mla_bwd(dq, dk, dvv, proj0, qn, kn, qn_w, kn_w, wq, wkv, tc, tsa, tsb, xb)

    acc = {"g_rnn": g_rnn, "g_gate": g_gate, "g_tail": g_tail, "g_wq": g_wq, "g_wkv": g_wkv,
           "dvec0": dvec0, "g_wa": g_wa, "g_wx": g_wx, "dqnw": dqnw, "dknw": dknw, "dgb0": dgb0, "dvec1": dvec1,
           "dcw1": dcw1, "dnw": dnw, "dgb1": dgb1}
    late = [(_reshard([g_rnn, g_gate, g_tail], MAP_G0, 244, bf, "reshard_w0_late", shards=W0_LATE), W0_LATE),
            _reshard([g_wq], MAP_WQ, 96, bf, "reshard_wq"), _reshard([g_wkv], MAP_WKV, 128, bf, "reshard_wkv")]
    sm_slots, vec_rows, gates = _pack_small(dvec0, g_wa, g_wx, dqnw, dknw, dgb0, dvec1, dcw1, dnw, dgb1, loss8)
    dx, recv_late = _l0_dx(dxr, dgate, dtail, w0p, dv0, scatter=late + [sm_slots], bcast=[vec_rows, gates])
    return acc, recv_early, recv_late, loss8[0, 0], dx
```

```python
import math

import jax
import jax.numpy as jnp
from jax import lax
from jax.experimental import pallas as pl
from jax.experimental.pallas import tpu as pltpu

F32 = jnp.float32
MXU_DTYPE = jnp.bfloat16

N_DEV = 8
SEQ = 4096
D_MODEL = 1024
DN_ALPHA = 4.0 ** 0.25
RNN_W = 512
MLA_HEADS = 8
ATT_SCALE = 96.0 ** -0.5
ATT_C = ATT_SCALE * math.log2(math.e)
RG_C = 8.0
SSD_INNER = 2048
SSD_HEADS = 32
SSD_P = 64
SSD_GROUPS = 4
SSD_N = 128
SSD_L = 128
SSD_CONV = 3072
LANES = 128
SUBLANES = 8
VMEM_LIMIT = 56 * 1024 * 1024

ADAM_LR, ADAM_B1, ADAM_B2, ADAM_EPS, ADAM_WD, ADAM_STEP = 0.001, 0.9, 0.999, 1e-08, 0.01, 10

HIGHEST = lax.Precision.HIGHEST


def _params(sem, limit=VMEM_LIMIT):
    return pltpu.CompilerParams(dimension_semantics=sem, vmem_limit_bytes=limit)


def _dot(a, b):
    return lax.dot_general(a, b, (((1,), (0,)), ((), ())), preferred_element_type=F32)


def _dot_nt(a, b):
    return lax.dot_general(a, b, (((1,), (1,)), ((), ())), preferred_element_type=F32)


def _dot_tn(a, b):
    return lax.dot_general(a, b, (((0,), (0,)), ((), ())), preferred_element_type=F32)


def _dot_hi(a, b):
    return lax.dot_general(a, b, (((1,), (0,)), ((), ())), precision=HIGHEST, preferred_element_type=F32)


def _mx(v):
    return v.astype(MXU_DTYPE)


def _sigmoid(v):
    return 1.0 / (1.0 + jnp.exp(-v))


def _log1p_pos(e):
    poly = e * (1.0 - e * (0.5 - e * (1.0 / 3.0 - e * 0.25)))
    return jnp.where(e < 0.01, poly, jnp.log(1.0 + e))


def _softplus(v):
    return jnp.maximum(v, 0.0) + _log1p_pos(jnp.exp(-jnp.abs(v)))


def _neg_expm1(v):
    poly = -v * (1.0 + v * (0.5 + v * (1.0 / 6.0 + v * (1.0 / 24.0 + v * (1.0 / 120.0)))))
    return jnp.where(jnp.abs(v) < 0.1, poly, 1.0 - jnp.exp(v))


def _silu(v):
    return v * _sigmoid(v)


def _dsilu(v):
    s = _sigmoid(v)
    return s * (1.0 + v * (1.0 - s))


def _shift_down(blk, halo, s):
    if s == 0:
        return blk
    t = blk.shape[0]
    r = pltpu.roll(blk, s, 0)
    hr = pltpu.roll(halo, s, 0)
    row8 = lax.broadcasted_iota(jnp.int32, hr.shape, 0)
    head = jnp.where(row8 < s, hr, r[:SUBLANES])
    return jnp.concatenate([head, r[SUBLANES:]], axis=0) if t > SUBLANES else head


def _shift_up(blk, halo, s):
    if s == 0:
        return blk
    t = blk.shape[0]
    r = pltpu.roll(blk, t - s, 0)
    hr = pltpu.roll(halo, SUBLANES - s, 0)
    row8 = lax.broadcasted_iota(jnp.int32, hr.shape, 0)
    tail = jnp.where(row8 >= SUBLANES - s, hr, r[t - SUBLANES:])
    return jnp.concatenate([r[:t - SUBLANES], tail], axis=0) if t > SUBLANES else tail


def _scan_down(a, u):
    t = a.shape[0]
    row = lax.broadcasted_iota(jnp.int32, a.shape, 0)
    d = 1
    while d < t:
        keep = row >= d
        a_sh = jnp.where(keep, pltpu.roll(a, d, 0), 1.0)
        u_sh = jnp.where(keep, pltpu.roll(u, d, 0), 0.0)
        u = a * u_sh + u
        a = a * a_sh
        d *= 2
    return a, u


def _scan_up(a, u):
    t = a.shape[0]
    row = lax.broadcasted_iota(jnp.int32, a.shape, 0)
    d = 1
    while d < t:
        keep = row < t - d
        a_sh = jnp.where(keep, pltpu.roll(a, t - d, 0), 1.0)
        u_sh = jnp.where(keep, pltpu.roll(u, t - d, 0), 0.0)
        u = a * u_sh + u
        a = a * a_sh
        d *= 2
    return a, u


def _conv4(blk, halo, cw, cb):
    out = cb + blk * cw[3:4]
    for k in range(3):
        out = out + _shift_down(blk, halo, 3 - k) * cw[k:k + 1]
    return out


RG_T = 512
P0_RNN = 2


def _rg_gates(xc, wa, ba, wx, bx, lam):
    xcb = _mx(xc)
    r = _sigmoid(_dot(xcb, wa) + ba)
    ig = _sigmoid(_dot(xcb, wx) + bx)
    sp = _softplus(-lam)
    la = (-RG_C * r) * sp
    a = jnp.exp(la)
    mult = jnp.sqrt(_neg_expm1(2.0 * la))
    return r, ig, sp, a, mult


def _rglru_fwd(proj0, cw8, cb, wa, ba, wx, bx, lam):
    t, w = RG_T, RNN_W
    nb = SEQ // t

    def body(x_ref, halo_ref, cw_ref, cb_ref, wa_ref, ba_ref, wx_ref, bx_ref, lam_ref, xc_ref, h_ref, carry):
        i = pl.program_id(0)

        @pl.when(i == 0)
        def _():
            carry[...] = jnp.zeros_like(carry)

        blk = x_ref[...]
        halo = jnp.where(i > 0, halo_ref[...], 0.0)
        xc = _conv4(blk, halo, cw_ref[...], cb_ref[...])
        _, ig, _, a, mult = _rg_gates(xc, wa_ref[...], ba_ref[...], wx_ref[...], bx_ref[...], lam_ref[...])
        u = mult * (ig * xc)
        big_a, big_u = _scan_down(a, u)
        h = big_a * carry[SUBLANES - 1:SUBLANES, :] + big_u
        carry[...] = h[t - SUBLANES:]
        xc_ref[...] = xc
        h_ref[...] = h

    vec = pl.BlockSpec((1, w), lambda i: (0, 0))
    mat = pl.BlockSpec((w, w), lambda i: (0, 0))
    return pl.pallas_call(
        body, name="rglru_fwd", grid=(nb,),
        in_specs=[pl.BlockSpec((t, w), lambda i: (i, P0_RNN)),
                  pl.BlockSpec((SUBLANES, w), lambda i: (jnp.maximum(i * (t // SUBLANES) - 1, 0), P0_RNN)),
                  pl.BlockSpec((SUBLANES, w), lambda i: (0, 0)), vec, mat, vec, mat, vec, vec],
        out_specs=[pl.BlockSpec((t, w), lambda i: (i, 0)), pl.BlockSpec((t, w), lambda i: (i, 0))],
        out_shape=[jax.ShapeDtypeStruct((SEQ, w), F32), jax.ShapeDtypeStruct((SEQ, w), F32)],
        scratch_shapes=[pltpu.VMEM((SUBLANES, w), F32)],
        compiler_params=_params(("arbitrary",)),
    )(proj0, proj0, cw8, cb, wa, ba, wx, bx, lam)


def _rglru_bwd(dh, xc, h, proj0, cw8, wa, ba, wx, bx, lam, xb):
    t, w = RG_T, RNN_W
    nb = SEQ // t
    tb = t // SUBLANES

    def body(dh_ref, xc_ref, h_ref, hh_ref, x_ref, cw_ref, wa_ref, ba_ref, wx_ref, bx_ref, lam_ref, xb_ref,
             dx_ref, dwa_ref, dwx_ref, dvec_ref, gw_ref, gcarry, dxc_next):
        i = pl.program_id(0)
        rev = nb - 1 - i

        @pl.when(i == 0)
        def _():
            gcarry[...] = jnp.zeros_like(gcarry)
            dxc_next[...] = jnp.zeros_like(dxc_next)
            gw_ref[...] = jnp.zeros_like(gw_ref)
            dwa_ref[...] = jnp.zeros_like(dwa_ref)
            dwx_ref[...] = jnp.zeros_like(dwx_ref)
            dvec_ref[...] = jnp.zeros_like(dvec_ref)

        xc = xc_ref[...]
        wa_v, wx_v = wa_ref[...], wx_ref[...]
        lam_v = lam_ref[...]
        r, ig, sp, a, mult = _rg_gates(xc, wa_v, ba_ref[...], wx_v, bx_ref[...], lam_v)
        dhv = dh_ref[...]
        big_a, big_u = _scan_up(a, a * dhv)
        gg = big_a * gcarry[0:1, :] + big_u
        g = dhv + _shift_up(gg, gcarry[...], 1)
        gcarry[...] = gg[:SUBLANES]
        hhalo = jnp.where(rev > 0, hh_ref[...], 0.0)
        da = g * _shift_down(h_ref[...], hhalo, 1)
        d_mult = g * (ig * xc)
        d_i = g * (mult * xc)
        dxc = g * (mult * ig)
        d_la = da * a - d_mult * (a * a) / mult
        d_r = d_la * (-RG_C * sp)
        d_sp = jnp.sum(d_la * (-RG_C * r), axis=0, keepdims=True)
        d_pa = d_r * r * (1.0 - r)
        d_px = d_i * ig * (1.0 - ig)
        d_pab, d_pxb = _mx(d_pa), _mx(d_px)
        dxc = dxc + _dot_nt(d_pab, wa_v) + _dot_nt(d_pxb, wx_v)
        xcb = _mx(xc)
        dwa_ref[...] += _dot_tn(xcb, d_pab)
        dwx_ref[...] += _dot_tn(xcb, d_pxb)
        dvec_ref[0:1, :] += jnp.sum(d_pa, axis=0, keepdims=True)
        dvec_ref[1:2, :] += jnp.sum(d_px, axis=0, keepdims=True)
        dvec_ref[2:3, :] += d_sp * (-_sigmoid(-lam_v))
        dvec_ref[3:4, :] += jnp.sum(dxc, axis=0, keepdims=True)
        xblk = x_ref[...]
        cw = cw_ref[...]
        dx = dxc * cw[3:4]
        nxt = dxc_next[...]
        dvec_ref[7:8, :] += jnp.sum(dxc * xblk, axis=0, keepdims=True)
        for k in range(3):
            up = _shift_up(dxc, nxt, 3 - k)
            dvec_ref[4 + k:5 + k, :] += jnp.sum(up * xblk, axis=0, keepdims=True)
            dx = dx + up * cw[k:k + 1]
        dxc_next[...] = dxc[:SUBLANES]
        dxb = _mx(dx)
        dx_ref[...] = dxb
        gw_ref[...] += _dot_tn(xb_ref[...], dxb)

    blk = pl.BlockSpec((t, w), lambda i: (nb - 1 - i, 0))
    halo = pl.BlockSpec((SUBLANES, w), lambda i: (jnp.maximum((nb - 1 - i) * tb - 1, 0), 0))
    vec = pl.BlockSpec((1, w), lambda i: (0, 0))
    mat = pl.BlockSpec((w, w), lambda i: (0, 0))
    return pl.pallas_call(
        body, name="rglru_bwd", grid=(nb,),
        in_specs=[blk, blk, blk, halo, pl.BlockSpec((t, w), lambda i: (nb - 1 - i, P0_RNN)),
                  pl.BlockSpec((SUBLANES, w), lambda i: (0, 0)), mat, vec, mat, vec, vec,
                  pl.BlockSpec((t, D_MODEL), lambda i: (nb - 1 - i, 0))],
        out_specs=[blk, mat, mat, pl.BlockSpec((16, w), lambda i: (0, 0)), pl.BlockSpec((D_MODEL, w), lambda i: (0, 0))],
        out_shape=[jax.ShapeDtypeStruct((SEQ, w), MXU_DTYPE), jax.ShapeDtypeStruct((w, w), F32),
                   jax.ShapeDtypeStruct((w, w), F32), jax.ShapeDtypeStruct((16, w), F32),
                   jax.ShapeDtypeStruct((D_MODEL, w), F32)],
        scratch_shapes=[pltpu.VMEM((SUBLANES, w), F32), pltpu.VMEM((SUBLANES, w), F32)],
        compiler_params=_params(("arbitrary",)),
    )(dh, xc, h, h, proj0, cw8, wa, ba, wx, bx, lam, xb)


MLA_T = 512


def _rope(v, c, sa, sb):
    return v * c + pltpu.roll(v, LANES - 16, 1) * sa + pltpu.roll(v, 16, 1) * sb


def _rope_t(dv, c, sa, sb):
    return dv * c + pltpu.roll(dv * sa, 16, 1) + pltpu.roll(dv * sb, LANES - 16, 1)


def _rms(v, g, eps=1e-6):
    rs = lax.rsqrt(jnp.mean(v * v, axis=-1, keepdims=True) + eps)
    return v * rs * g, rs


def _mla_fwd(proj0, q_norm, kv_norm, wq, wkv, tc, tsa, tsb):
    t = MLA_T

    def body(cq_ref, ck_ref, qn_ref, kn_ref, wq_ref, wkv_ref, c_ref, sa_ref, sb_ref,
             oqn_ref, okn_ref, oq_ref, ok_ref, ov_ref):
        c, sa, sb = c_ref[...], sa_ref[...], sb_ref[...]
        ck = ck_ref[...]
        qn = _mx(_rms(cq_ref[...], qn_ref[...])[0])
        kn = _mx(_rms(ck[:, :LANES], kn_ref[...])[0])
        oqn_ref[...] = qn
        okn_ref[...] = kn
        krv = _rope(ck[:, LANES:], c, sa, sb)
        qraw = _dot(qn, wq_ref[...])
        kvraw = _dot(kn, wkv_ref[...])
        for hd in range(MLA_HEADS):
            sl = slice(hd * LANES, (hd + 1) * LANES)
            oq_ref[:, sl] = _mx(_rope(qraw[:, sl], c, sa, sb))
            ok_ref[:, sl] = _mx(kvraw[:, sl] + krv)
        ov_ref[...] = _mx(kvraw[:, 1024:])

    tab = pl.BlockSpec((t, LANES), lambda i: (i, 0))
    wide = pl.BlockSpec((t, 1024), lambda i: (i, 0))
    const = lambda shape: pl.BlockSpec(shape, lambda i: (0, 0))
    return pl.pallas_call(
        body, name="mla_fwd", grid=(SEQ // t,),
        in_specs=[pl.BlockSpec((t, 256), lambda i: (i, 6)), pl.BlockSpec((t, 256), lambda i: (i, 7)),
                  const((1, 256)), const((1, LANES)), const((256, 1024)), const((LANES, 1536)), tab, tab, tab],
        out_specs=[pl.BlockSpec((t, 256), lambda i: (i, 0)), tab, wide, wide, pl.BlockSpec((t, 512), lambda i: (i, 0))],
        out_shape=[jax.ShapeDtypeStruct((SEQ, 256), MXU_DTYPE), jax.ShapeDtypeStruct((SEQ, LANES), MXU_DTYPE),
                   jax.ShapeDtypeStruct((SEQ, 1024), MXU_DTYPE), jax.ShapeDtypeStruct((SEQ, 1024), MXU_DTYPE),
                   jax.ShapeDtypeStruct((SEQ, 512), MXU_DTYPE)],
        compiler_params=_params(("parallel",)),
    )(proj0, proj0, q_norm, kv_norm, wq, wkv, tc, tsa, tsb)


ATT_T = 1024


def _flash_fwd(q, k, v, bcast=()):
    t = ATT_T
    nb = SEQ // t

    steps = [(qi, ki) for qi in range(nb) for ki in range(qi + 1)]
    qi_tab = jnp.asarray([s[0] for s in steps], jnp.int32)
    ki_tab = jnp.asarray([s[1] for s in steps], jnp.int32)

    nx = len(bcast)

    def body(qi_ref, ki_ref, q_ref, k_ref, v_ref, *rest):
        x_refs, (o_ref, lse_ref), g_refs = rest[:nx], rest[nx:nx + 2], rest[nx + 2:2 * nx + 2]
        m_sc, acc_sc, bias_sc = rest[2 * nx + 2:2 * nx + 5]
        step = pl.program_id(1)
        qi, ki = qi_ref[step], ki_ref[step]
        if nx:
            copies = _peer_copies(x_refs, g_refs, rest[2 * nx + 5:], [])

            @pl.when((pl.program_id(0) == 0) & (step == 0))
            def _():
                for cp in copies:
                    cp.start()

        @pl.when((pl.program_id(0) == 0) & (step == 0))
        def _():
            bias_sc[...] = jnp.where(lax.broadcasted_iota(jnp.int32, (t, t), 0)
                                     <= lax.broadcasted_iota(jnp.int32, (t, t), 1), 0.0, -jnp.inf)

        @pl.when(ki == 0)
        def _():
            m_sc[...] = jnp.full_like(m_sc, -jnp.inf)
            acc_sc[...] = jnp.zeros_like(acc_sc)

        def update(diagonal):
            vv = v_ref[...]
            lane_v = lax.broadcasted_iota(jnp.int32, vv.shape, 1)
            for hd in range(2):
                sl = slice(hd * LANES, (hd + 1) * LANES)
                st = _dot_nt(k_ref[:, sl], q_ref[:, sl])
                if diagonal:
                    st = st + bias_sc[...]
                m_prev = m_sc[hd:hd + 1, :]
                m_new = jnp.maximum(m_prev, jnp.max(st, axis=0, keepdims=True))
                pt = jnp.exp2((st - m_new) * ATT_C)
                m_sc[hd:hd + 1, :] = m_new
                vh = jnp.where((lane_v >= hd * 64) & (lane_v < (hd + 1) * 64), vv, jnp.ones_like(vv))
                acc_sc[hd] = acc_sc[hd] * jnp.exp2((m_prev - m_new) * ATT_C) + _dot_tn(vh, _mx(pt))

        @pl.when(ki < qi)
        def _():
            update(False)

        @pl.when(ki == qi)
        def _():
            update(True)
            a0, a1 = acc_sc[0], acc_sc[1]
            l0, l1 = a0[64:65, :], a1[0:1, :]
            first = lax.broadcasted_iota(jnp.int32, (LANES, t), 0) < 64
            o_ref[...] = jnp.where(first, a0 / l0, a1 / l1).T
            lse_ref[0, 0:1, :] = m_sc[0:1, :] * ATT_SCALE + jnp.log(l0)
            lse_ref[0, 1:2, :] = m_sc[1:2, :] * ATT_SCALE + jnp.log(l1)
            lse_ref[0, 2:SUBLANES, :] = jnp.zeros((SUBLANES - 2, t), F32)

        if nx:
            @pl.when((pl.program_id(0) == 3) & (step == len(steps) - 1))
            def _():
                for cp in copies:
                    cp.wait()

    grid_spec = pltpu.PrefetchScalarGridSpec(
        num_scalar_prefetch=2, grid=(4, len(steps)),
        in_specs=[pl.BlockSpec((t, 256), lambda p, s, qt, kt: (qt[s], p)),
                  pl.BlockSpec((t, 256), lambda p, s, qt, kt: (kt[s], p)),
                  pl.BlockSpec((t, LANES), lambda p, s, qt, kt: (kt[s], p))] + [ANY] * nx,
        out_specs=[pl.BlockSpec((t, LANES), lambda p, s, qt, kt: (qt[s], p)),
                   pl.BlockSpec((1, SUBLANES, t), lambda p, s, qt, kt: (p, 0, qt[s]))] + [ANY] * nx,
        scratch_shapes=[pltpu.VMEM((SUBLANES, t), F32), pltpu.VMEM((2, LANES, t), F32), pltpu.VMEM((t, t), F32)]
        + (_exchange_sems(nx) if nx else []))
    res = pl.pallas_call(
        body, name="flash_fwd", grid_spec=grid_spec,
        out_shape=[jax.ShapeDtypeStruct((SEQ, 512), F32), jax.ShapeDtypeStruct((4, SUBLANES, SEQ), F32)]
        + _exchange_shapes([], bcast),
        compiler_params=_params(("arbitrary", "arbitrary")),
    )(qi_tab, ki_tab, q, k, v, *bcast)
    return res[0], res[1], res[2:]


def _flash_bwd(q, k, v, o, do, lse, scatter=()):
    t = ATT_T
    nb = SEQ // t

    steps = [(qi, ki) for ki in range(nb) for qi in range(ki, nb)]
    qi_tab = jnp.asarray([s[0] for s in steps], jnp.int32)
    ki_tab = jnp.asarray([s[1] for s in steps], jnp.int32)
    log2e = math.log2(math.e)

    sc_arrays, sc_ranges = _scatter_args(scatter)
    nx = len(sc_arrays)

    def body(qi_ref, ki_ref, q_ref, k_ref, v_ref, o_ref, do_ref, lse_ref, *rest):
        x_refs, (dq_ref, dk_ref, dv_ref), g_refs = rest[:nx], rest[nx:nx + 3], rest[nx + 3:2 * nx + 3]
        dkt_sc, dvt_sc, bias_sc = rest[2 * nx + 3:2 * nx + 6]
        step = pl.program_id(1)
        qi, ki = qi_ref[step], ki_ref[step]
        if nx:
            copies = _peer_copies(x_refs, g_refs, rest[2 * nx + 6:], sc_ranges)

            @pl.when((pl.program_id(0) == 0) & (step == 0))
            def _():
                for cp in copies:
                    cp.start()

        @pl.when((pl.program_id(0) == 0) & (step == 0))
        def _():
            bias_sc[...] = jnp.where(lax.broadcasted_iota(jnp.int32, (t, t), 1)
                                     <= lax.broadcasted_iota(jnp.int32, (t, t), 0), 0.0, -jnp.inf)

        @pl.when(step == 0)
        def _():
            dq_ref[...] = jnp.zeros_like(dq_ref)

        @pl.when(qi == ki)
        def _():
            dkt_sc[...] = jnp.zeros_like(dkt_sc)
            dvt_sc[...] = jnp.zeros_like(dvt_sc)

        def update(diagonal):
            dov, ov, vv = do_ref[...], o_ref[...], v_ref[...]
            lse2 = (lse_ref[0] * log2e).T
            lane = lax.broadcasted_iota(jnp.int32, (t, LANES), 1)
            row_t = lax.broadcasted_iota(jnp.int32, (LANES, t), 0)
            prod = dov * ov
            do_b = _mx(dov)
            qrows = pl.ds(pl.multiple_of(qi * t, t), t)
            dvt_acc = jnp.zeros((LANES, t), F32)
            dkt_new, dq_new = [], []
            for hd in range(2):
                sl = slice(hd * LANES, (hd + 1) * LANES)
                mine = (lane >= hd * 64) & (lane < (hd + 1) * 64)
                qh, kh = q_ref[:, sl], k_ref[:, sl]
                e = _dot_nt(qh, kh) * ATT_C - lse2[:, hd:hd + 1]
                p = jnp.exp2(e + bias_sc[...] if diagonal else e)
                do_h = jnp.where(mine, dov, 0.0)
                delta = jnp.sum(jnp.where(mine, prod, 0.0), axis=1, keepdims=True)
                dp = _dot_nt(_mx(do_h), vv)
                ds = _mx(p * (dp - delta) * ATT_SCALE)
                dvt_acc = dvt_acc + jnp.where((row_t >= hd * 64) & (row_t < (hd + 1) * 64), _dot_tn(do_b, _mx(p)), 0.0)
                dkt_new.append(_dot_tn(qh, ds))
                dq_new.append(_dot(ds, kh))
            for hd in range(2):
                sl = slice(hd * LANES, (hd + 1) * LANES)
                dkt_sc[sl, :] += dkt_new[hd]
                dq_ref[qrows, sl] += dq_new[hd]
            dvt_sc[...] += dvt_acc

        @pl.when(qi > ki)
        def _():
            update(False)

        @pl.when(qi == ki)
        def _():
            update(True)

        @pl.when(qi == nb - 1)
        def _():
            dk_ref[...] = dkt_sc[...].T
            dv_ref[...] = dvt_sc[...].T

        if nx:
            @pl.when((pl.program_id(0) == 3) & (step == len(steps) - 1))
            def _():
                for cp in copies:
                    cp.wait()

    qmap = lambda p, s, qt, kt: (qt[s], p)
    kmap = lambda p, s, qt, kt: (kt[s], p)
    grid_spec = pltpu.PrefetchScalarGridSpec(
        num_scalar_prefetch=2, grid=(4, len(steps)),
        in_specs=[pl.BlockSpec((t, 256), qmap), pl.BlockSpec((t, 256), kmap), pl.BlockSpec((t, LANES), kmap),
                  pl.BlockSpec((t, LANES), qmap), pl.BlockSpec((t, LANES), qmap),
                  pl.BlockSpec((1, SUBLANES, t), lambda p, s, qt, kt: (p, 0, qt[s]))] + [ANY] * nx,
        out_specs=[pl.BlockSpec((SEQ, 256), lambda p, s, qt, kt: (0, p)), pl.BlockSpec((t, 256), kmap),
                   pl.BlockSpec((t, LANES), kmap)] + [ANY] * nx,
        scratch_shapes=[pltpu.VMEM((256, t), F32), pltpu.VMEM((LANES, t), F32), pltpu.VMEM((t, t), F32)]
        + (_exchange_sems(nx) if nx else []))
    res = pl.pallas_call(
        body, name="flash_bwd", grid_spec=grid_spec,
        out_shape=[jax.ShapeDtypeStruct((SEQ, 1024), F32), jax.ShapeDtypeStruct((SEQ, 1024), F32),
                   jax.ShapeDtypeStruct((SEQ, 512), F32)] + _exchange_shapes(sc_arrays, []),
        compiler_params=_params(("arbitrary", "arbitrary")),
    )(qi_tab, ki_tab, q, k, v, o, do, lse, *sc_arrays)
    return res[0], res[1], res[2], res[3:]


def _rms_bwd(v, g, dy, eps=1e-6):
    rs = lax.rsqrt(jnp.mean(v * v, axis=-1, keepdims=True) + eps)
    xh = v * rs
    dxh = dy * g
    dv = rs * (dxh - xh * jnp.mean(dxh * xh, axis=-1, keepdims=True))
    return dv, jnp.sum(dy * xh, axis=0, keepdims=True)


def _mla_bwd(dq, dk, dv, proj0, qlat, klat, q_norm, kv_norm, wq, wkv, tc, tsa, tsb, xb):
    t = MLA_T

    def body(dq_ref, dk_ref, dv_ref, cq_ref, ck_ref, ql_ref, kl_ref, qn_ref, kn_ref, wq_ref, wkv_ref,
             c_ref, sa_ref, sb_ref, xb_ref, o_ref, gwq_ref, gwkv_ref, dgq_ref, dgk_ref, gwt_ref, oq_ref, okv_ref):
        @pl.when(pl.program_id(0) == 0)
        def _():
            dgq_ref[...] = jnp.zeros_like(dgq_ref)
            dgk_ref[...] = jnp.zeros_like(dgk_ref)
            gwq_ref[...] = jnp.zeros_like(gwq_ref)
            gwkv_ref[...] = jnp.zeros_like(gwkv_ref)
            gwt_ref[...] = jnp.zeros_like(gwt_ref)

        c, sa, sb = c_ref[...], sa_ref[...], sb_ref[...]
        lane = lax.broadcasted_iota(jnp.int32, (t, LANES), 1)
        dkr = jnp.zeros((t, LANES), F32)
        for hd in range(MLA_HEADS):
            sl = slice(hd * LANES, (hd + 1) * LANES)
            oq_ref[:, sl] = _mx(_rope_t(dq_ref[:, sl], c, sa, sb))
            dkh = dk_ref[:, sl]
            okv_ref[:, sl] = _mx(dkh)
            dkr = dkr + dkh
        okv_ref[:, 1024:] = _mx(dv_ref[...])
        dkr = _rope_t(jnp.where((lane >= 64) & (lane < 96), dkr, 0.0), c, sa, sb)
        dqraw, dkvraw = oq_ref[...], okv_ref[...]
        gwq_ref[...] += _dot_tn(ql_ref[...], dqraw)
        gwkv_ref[...] += _dot_tn(kl_ref[...], dkvraw)
        dqn = _dot_nt(dqraw, wq_ref[...])
        dkn = _dot_nt(dkvraw, wkv_ref[...])
        dcq, dgq = _rms_bwd(cq_ref[...], qn_ref[...], dqn)
        dck, dgk = _rms_bwd(ck_ref[:, :LANES], kn_ref[...], dkn)
        o_ref[:, :256] = _mx(dcq)
        o_ref[:, 256:384] = _mx(dck)
        o_ref[:, 384:] = _mx(dkr)
        gwt_ref[...] += _dot_tn(xb_ref[...], o_ref[...])
        dgq_ref[0:1, :] += dgq
        dgk_ref[0:1, :] += dgk

    tab = pl.BlockSpec((t, LANES), lambda i: (i, 0))
    wide = pl.BlockSpec((t, 1024), lambda i: (i, 0))
    const = lambda shape: pl.BlockSpec(shape, lambda i: (0, 0))
    return pl.pallas_call(
        body, name="mla_bwd", grid=(SEQ // t,),
        in_specs=[wide, wide, pl.BlockSpec((t, 512), lambda i: (i, 0)),
                  pl.BlockSpec((t, 256), lambda i: (i, 6)), pl.BlockSpec((t, 256), lambda i: (i, 7)),
                  pl.BlockSpec((t, 256), lambda i: (i, 0)), tab,
                  const((1, 256)), const((1, LANES)), const((256, 1024)), const((LANES, 1536)), tab, tab, tab, wide],
        out_specs=[pl.BlockSpec((t, 512), lambda i: (i, 0)), const((256, 1024)), const((LANES, 1536)),
                   const((SUBLANES, 256)), const((SUBLANES, LANES)), const((D_MODEL, 512))],
        out_shape=[jax.ShapeDtypeStruct((SEQ, 512), MXU_DTYPE), jax.ShapeDtypeStruct((256, 1024), F32),
                   jax.ShapeDtypeStruct((LANES, 1536), F32), jax.ShapeDtypeStruct((SUBLANES, 256), F32),
                   jax.ShapeDtypeStruct((SUBLANES, LANES), F32), jax.ShapeDtypeStruct((D_MODEL, 512), F32)],
        scratch_shapes=[pltpu.VMEM((t, 1024), MXU_DTYPE), pltpu.VMEM((t, 1536), MXU_DTYPE)],
        compiler_params=_params(("arbitrary",)),
    )(dq, dk, dv, proj0, proj0, qlat, klat, q_norm, kv_norm, wq, wkv, tc, tsa, tsb, xb)


LN_T = 512


def _ln(v, g, b, eps=1e-5):
    mu = jnp.mean(v, axis=-1, keepdims=True)
    xc = v - mu
    rs = lax.rsqrt(jnp.mean(xc * xc, axis=-1, keepdims=True) + eps)
    return xc * rs * g + b


def _ln_bwd(v, g, dy, eps=1e-5):
    mu = jnp.mean(v, axis=-1, keepdims=True)
    xc = v - mu
    rs = lax.rsqrt(jnp.mean(xc * xc, axis=-1, keepdims=True) + eps)
    xh = xc * rs
    dxh = dy * g
    dv = rs * (dxh - jnp.mean(dxh, axis=-1, keepdims=True) - xh * jnp.mean(dxh * xh, axis=-1, keepdims=True))
    return dv, jnp.sum(dy * xh, axis=0, keepdims=True), jnp.sum(dy, axis=0, keepdims=True)


def _l0_out(h, o, proj0, x, w_out, g, b):
    t = LN_T

    def body(h_ref, o_ref, ga_ref, gb_ref, x_ref, w_ref, g_ref, b_ref, y_ref, v_ref, x1_ref, x1b_ref):
        y = _mx(jnp.concatenate([h_ref[...] * _silu(ga_ref[...]), o_ref[...] * _silu(gb_ref[...])], axis=1))
        v = DN_ALPHA * x_ref[...] + _dot(y, w_ref[...])
        y_ref[...] = y
        v_ref[...] = v
        x1 = _ln(v, g_ref[...], b_ref[...])
        x1_ref[...] = x1
        x1b_ref[...] = _mx(x1)

    half = pl.BlockSpec((t, 512), lambda i: (i, 0))
    full = pl.BlockSpec((t, D_MODEL), lambda i: (i, 0))
    vec = pl.BlockSpec((1, D_MODEL), lambda i: (0, 0))
    return pl.pallas_call(
        body, name="l0_out", grid=(SEQ // t,),
        in_specs=[half, half, pl.BlockSpec((t, 512), lambda i: (i, 0)), pl.BlockSpec((t, 512), lambda i: (i, 1)), full,
                  pl.BlockSpec((D_MODEL, D_MODEL), lambda i: (0, 0)), vec, vec],
        out_specs=[full, full, full, full],
        out_shape=[jax.ShapeDtypeStruct((SEQ, D_MODEL), MXU_DTYPE), jax.ShapeDtypeStruct((SEQ, D_MODEL), F32),
                   jax.ShapeDtypeStruct((SEQ, D_MODEL), F32), jax.ShapeDtypeStruct((SEQ, D_MODEL), MXU_DTYPE)],
        compiler_params=_params(("parallel",)),
    )(h, o, proj0, proj0, x, w_out, g, b)


def _l1_in(x1b, w1z, w1d):
    t = 1024

    def body(x_ref, wz_ref, wd_ref, z_ref, dt_ref):
        xv = x_ref[...]
        z_ref[...] = _dot(xv, wz_ref[...])
        dt_ref[...] = _dot(xv, wd_ref[...])

    rows = lambda w: pl.BlockSpec((t, w), lambda i: (i, 0))
    const = lambda w: pl.BlockSpec((D_MODEL, w), lambda i: (0, 0))
    return pl.pallas_call(
        body, name="l1_in", grid=(SEQ // t,),
        in_specs=[rows(D_MODEL), const(SSD_INNER), const(LANES)],
        out_specs=[rows(SSD_INNER), rows(LANES)],
        out_shape=[jax.ShapeDtypeStruct((SEQ, SSD_INNER), F32), jax.ShapeDtypeStruct((SEQ, LANES), F32)],
        compiler_params=_params(("parallel",)),
    )(x1b, w1z, w1d)


def _l1_dx_ln(dz, dxbc, ddt, dv1, v0, w1z, w1x, w1d, g):
    t = LN_T

    def body(dz_ref, dx_ref, ddt_ref, dv1_ref, v_ref, wz_ref, wx_ref, wd_ref, g_ref, dv_ref, dgb_ref):
        @pl.when(pl.program_id(0) == 0)
        def _():
            dgb_ref[...] = jnp.zeros_like(dgb_ref)

        dy = (DN_ALPHA * dv1_ref[...] + _dot_nt(dz_ref[...], wz_ref[...]) + _dot_nt(dx_ref[...], wx_ref[...])
              + _dot_nt(_mx(ddt_ref[...]), wd_ref[...]))
        dv, dg, db = _ln_bwd(v_ref[...], g_ref[...], dy)
        dv_ref[...] = dv
        dgb_ref[0:1, :] += dg
        dgb_ref[1:2, :] += db

    rows = lambda w: pl.BlockSpec((t, w), lambda i: (i, 0))
    const = lambda w: pl.BlockSpec((D_MODEL, w), lambda i: (0, 0))
    return pl.pallas_call(
        body, name="l1_dx_ln", grid=(SEQ // t,),
        in_specs=[rows(SSD_INNER), rows(SSD_CONV), rows(LANES), rows(D_MODEL), rows(D_MODEL),
                  const(SSD_INNER), const(SSD_CONV), const(LANES), pl.BlockSpec((1, D_MODEL), lambda i: (0, 0))],
        out_specs=[rows(D_MODEL), pl.BlockSpec((SUBLANES, D_MODEL), lambda i: (0, 0))],
        out_shape=[jax.ShapeDtypeStruct((SEQ, D_MODEL), F32), jax.ShapeDtypeStruct((SUBLANES, D_MODEL), F32)],
        compiler_params=_params(("arbitrary",)),
    )(dz, dxbc, ddt, dv1, v0, w1z, w1x, w1d, g)


def _gate_bwd(dv0, w_out, h, o, proj0, y0, xb):
    t = LN_T

    def body(dv_ref, w_ref, h_ref, o_ref, ga_ref, gb_ref, y0_ref, xb_ref, dh_ref, do_ref, dg_ref, gwo_ref, gwg_ref):
        @pl.when(pl.program_id(0) == 0)
        def _():
            gwo_ref[...] = jnp.zeros_like(gwo_ref)
            gwg_ref[...] = jnp.zeros_like(gwg_ref)

        dvb = _mx(dv_ref[...])
        dy = _dot_nt(dvb, w_ref[...])
        ga, gb, dya, dyb = ga_ref[...], gb_ref[...], dy[:, :512], dy[:, 512:]
        dh_ref[...] = dya * _silu(ga)
        do_ref[...] = dyb * _silu(gb)
        dg_ref[:, :512] = _mx(dya * h_ref[...] * _dsilu(ga))
        dg_ref[:, 512:] = _mx(dyb * o_ref[...] * _dsilu(gb))
        gwo_ref[...] += _dot_tn(y0_ref[...], dvb)
        gwg_ref[...] += _dot_tn(xb_ref[...], dg_ref[...])

    half = pl.BlockSpec((t, 512), lambda i: (i, 0))
    half1 = pl.BlockSpec((t, 512), lambda i: (i, 1))
    full = pl.BlockSpec((t, 1024), lambda i: (i, 0))
    square = pl.BlockSpec((D_MODEL, D_MODEL), lambda i: (0, 0))
    return pl.pallas_call(
        body, name="gate_bwd", grid=(SEQ // t,),
        in_specs=[full, square, half, half, half, half1, full, full],
        out_specs=[half, half, full, square, square],
        out_shape=[jax.ShapeDtypeStruct((SEQ, 512), F32), jax.ShapeDtypeStruct((SEQ, 512), F32),
                   jax.ShapeDtypeStruct((SEQ, 1024), MXU_DTYPE), jax.ShapeDtypeStruct((D_MODEL, D_MODEL), F32),
                   jax.ShapeDtypeStruct((D_MODEL, D_MODEL), F32)],
        compiler_params=_params(("arbitrary",)),
    )(dv0, w_out, h, o, proj0, proj0, y0, xb)


CONV_T = 1024
CONV_CB = 1024


def _ssd_conv_fwd(x1b, w1x, cw8, cb):
    t, cbk = CONV_T, CONV_CB

    def body(x_ref, w_ref, cw_ref, cb_ref, xbc_ref, pre_ref, act_ref, carry):
        xbc = _dot(x_ref[...], w_ref[...])
        halo = jnp.where(pl.program_id(1) > 0, carry[...], 0.0)
        pre = _conv4(xbc, halo, cw_ref[...], cb_ref[...])
        carry[...] = xbc[t - SUBLANES:]
        xbc_ref[...] = xbc
        pre_ref[...] = pre
        act_ref[...] = _silu(pre)

    blk = pl.BlockSpec((t, cbk), lambda j, i: (i, j))
    out = jax.ShapeDtypeStruct((SEQ, SSD_CONV), F32)
    return pl.pallas_call(
        body, name="ssd_conv_fwd", grid=(SSD_CONV // cbk, SEQ // t),
        in_specs=[pl.BlockSpec((t, D_MODEL), lambda j, i: (i, 0)), pl.BlockSpec((D_MODEL, cbk), lambda j, i: (0, j)),
                  pl.BlockSpec((SUBLANES, cbk), lambda j, i: (0, j)), pl.BlockSpec((1, cbk), lambda j, i: (0, j))],
        out_specs=[blk, blk, blk], out_shape=[out, out, out],
        scratch_shapes=[pltpu.VMEM((SUBLANES, cbk), F32)],
        compiler_params=_params(("parallel", "arbitrary")),
    )(x1b, w1x, cw8, cb)


def _ssd_conv_bwd(dact, pre, xbc, cw8, x1b):
    t, cbk = CONV_T, CONV_CB
    tb = t // SUBLANES
    nb = SEQ // t

    def body(da_ref, dan_ref, pre_ref, pren_ref, x_ref, cw_ref, x1_ref, dx_ref, dcw_ref, gw_ref):
        i = pl.program_id(1)

        @pl.when(i == 0)
        def _():
            dcw_ref[...] = jnp.zeros_like(dcw_ref)
            gw_ref[...] = jnp.zeros_like(gw_ref)

        dpre = da_ref[...] * _dsilu(pre_ref[...])
        dpre_next = jnp.where(i < nb - 1, dan_ref[...] * _dsilu(pren_ref[...]), 0.0)
        xblk = x_ref[...]
        cw = cw_ref[...]
        dx = dpre * cw[3:4]
        dcw_ref[3:4, :] += jnp.sum(dpre * xblk, axis=0, keepdims=True)
        for k in range(3):
            up = _shift_up(dpre, dpre_next, 3 - k)
            dcw_ref[k:k + 1, :] += jnp.sum(up * xblk, axis=0, keepdims=True)
            dx = dx + up * cw[k:k + 1]
        dcw_ref[4:5, :] += jnp.sum(dpre, axis=0, keepdims=True)
        dxb = _mx(dx)
        dx_ref[...] = dxb
        gw_ref[...] += _dot_tn(x1_ref[...], dxb)

    blk = pl.BlockSpec((t, cbk), lambda j, i: (i, j))
    nxt = pl.BlockSpec((SUBLANES, cbk), lambda j, i: (jnp.minimum((i + 1) * tb, SEQ // SUBLANES - 1), j))
    acc = pl.BlockSpec((SUBLANES, cbk), lambda j, i: (0, j))
    return pl.pallas_call(
        body, name="ssd_conv_bwd", grid=(SSD_CONV // cbk, nb),
        in_specs=[blk, nxt, blk, nxt, blk, acc, pl.BlockSpec((t, D_MODEL), lambda j, i: (i, 0))],
        out_specs=[blk, acc, pl.BlockSpec((D_MODEL, cbk), lambda j, i: (0, j))],
        out_shape=[jax.ShapeDtypeStruct((SEQ, SSD_CONV), MXU_DTYPE), jax.ShapeDtypeStruct((SUBLANES, SSD_CONV), F32),
                   jax.ShapeDtypeStruct((D_MODEL, SSD_CONV), F32)],
        compiler_params=_params(("parallel", "arbitrary")),
    )(dact, dact, pre, pre, xbc, cw8, x1b)


def _ssd_common(dt_raw, bias, alog, tril, expand_t, xs):
    lane = lax.broadcasted_iota(jnp.int32, dt_raw.shape, 1)
    dt = jnp.where(lane < SSD_HEADS, _softplus(dt_raw + bias), 0.0)
    a_neg = -jnp.exp(alog)
    cs = _dot_hi(tril, dt * a_neg)
    dt_x = _expand_heads(dt, expand_t)
    ecs_x = _expand_heads(jnp.exp(cs), expand_t)
    ds_x = _expand_heads(jnp.exp(cs[SSD_L - 1:SSD_L, :] - cs), expand_t)
    return dt, a_neg, cs, dt_x, None, xs * dt_x, ds_x, ecs_x, ecs_x[SSD_L - 1:SSD_L, :]


def _expand_heads(v, expand_t):
    hi = v.astype(jnp.bfloat16)
    lo = (v - hi.astype(F32)).astype(jnp.bfloat16)
    return _dot_nt(hi, expand_t) + _dot_nt(lo, expand_t)


def _fold_heads(v, expand_t):
    hi = v.astype(jnp.bfloat16)
    lo = (v - hi.astype(F32)).astype(jnp.bfloat16)
    return _dot(hi, expand_t) + _dot(lo, expand_t)


def _ssd_decay(cs, cs_t, hh, causal):
    seg = cs[:, hh:hh + 1] - cs_t[hh:hh + 1, :]
    return jnp.where(causal, jnp.exp(jnp.where(causal, seg, 0.0)), 0.0)


def _ssd_scan_fwd(act, dt_raw, bias, alog, d_x, tril, expand_t, bcast=()):
    nc = SEQ // SSD_L
    gw = SSD_INNER // SSD_GROUPS
    n = len(bcast)

    def body(act_ref, dt_ref, bias_ref, alog_ref, dx_ref, tril_ref, et_ref, *rest):
        y_ref, hp_ref, h_sc = rest[n], rest[n + 1], rest[2 * n + 2]
        if n:
            copies = _peer_copies(rest[:n], rest[n + 2:2 * n + 2], rest[2 * n + 3:], [])

            @pl.when(pl.program_id(0) == 0)
            def _():
                for cp in copies:
                    cp.start()

            @pl.when(pl.program_id(0) == nc - 1)
            def _():
                for cp in copies:
                    cp.wait()

        @pl.when(pl.program_id(0) == 0)
        def _():
            h_sc[...] = jnp.zeros_like(h_sc)

        xs = act_ref[:, :SSD_INNER]
        _, _, cs, _, _, xdt, ds_x, ecs_x, elast = _ssd_common(
            dt_ref[...], bias_ref[...], alog_ref[...], tril_ref[...], et_ref[...], xs)
        cs_t = cs.T
        causal = (lax.broadcasted_iota(jnp.int32, (SSD_L, SSD_L), 0)
                  >= lax.broadcasted_iota(jnp.int32, (SSD_L, SSD_L), 1))
        lane = lax.broadcasted_iota(jnp.int32, (SSD_L, LANES), 1)
        xdt_b = _mx(xdt)
        xds_b = _mx(xdt * ds_x)
        hp_ref[0] = h_sc[...]
        for g in range(SSD_GROUPS):
            gs = slice(g * gw, (g + 1) * gw)
            bg = _mx(act_ref[:, SSD_INNER + g * SSD_N:SSD_INNER + (g + 1) * SSD_N])
            cg = _mx(act_ref[:, SSD_INNER + 512 + g * SSD_N:SSD_INNER + 512 + (g + 1) * SSD_N])
            cb = _dot_nt(cg, bg)
            hprev = h_sc[:, gs]
            yoff = _dot(cg, _mx(hprev)) * ecs_x[:, gs]
            h_sc[:, gs] = hprev * elast[:, gs] + _dot_tn(bg, xds_b[:, gs])
            for pr in range(4):
                ps = slice(g * gw + pr * LANES, g * gw + (pr + 1) * LANES)
                xp = xdt_b[:, ps]
                ydiag = jnp.zeros((SSD_L, LANES), F32)
                for j in range(2):
                    dm = _ssd_decay(cs, cs_t, g * 8 + pr * 2 + j, causal)
                    mine = (lane >= j * 64) & (lane < (j + 1) * 64)
                    ydiag = ydiag + _dot(_mx(cb * dm), jnp.where(mine, xp, jnp.zeros_like(xp)))
                y_ref[:, ps] = ydiag + yoff[:, pr * LANES:(pr + 1) * LANES] + dx_ref[:, ps] * xs[:, ps]

    const = lambda shape: pl.BlockSpec(shape, lambda c: (0, 0))
    res = pl.pallas_call(
        body, name="ssd_scan_fwd", grid=(nc,),
        in_specs=[pl.BlockSpec((SSD_L, SSD_CONV), lambda c: (c, 0)), pl.BlockSpec((SSD_L, LANES), lambda c: (c, 0)),
                  const((1, LANES)), const((1, LANES)), const((1, SSD_INNER)), const((SSD_L, SSD_L)),
                  const((SSD_INNER, LANES))] + [ANY] * n,
        out_specs=[pl.BlockSpec((SSD_L, SSD_INNER), lambda c: (c, 0)),
                   pl.BlockSpec((1, SSD_N, SSD_INNER), lambda c: (c, 0, 0))] + [ANY] * n,
        out_shape=[jax.ShapeDtypeStruct((SEQ, SSD_INNER), F32), jax.ShapeDtypeStruct((nc, SSD_N, SSD_INNER), F32)]
        + _exchange_shapes([], bcast),
        scratch_shapes=[pltpu.VMEM((SSD_N, SSD_INNER), F32)] + (_exchange_sems(n) if n else []),
        compiler_params=_params(("arbitrary",)),
    )(act, dt_raw, bias, alog, d_x, tril, expand_t, *bcast)
    return res[0], res[1], res[2:]


def _ssd_scan_bwd(dy, act, dt_raw, hprev_all, bias, alog, d_x, tril, expand_t, x1b, scatter=()):
    nc = SEQ // SSD_L
    gw = SSD_INNER // SSD_GROUPS
    sc_arrays, sc_ranges = _scatter_args(scatter)
    nx = len(sc_arrays)

    def body(dy_ref, act_ref, dt_ref, hp_ref, bias_ref, alog_ref, dx_ref, tril_ref, et_ref, x1_ref, *rest):
        dact_ref, ddt_ref, dvec_ref, gdt_ref = rest[nx:nx + 4]
        dh_sc, dd_sc, dcs_sc, dcst_sc = rest[2 * nx + 4:2 * nx + 8]
        i = pl.program_id(0)
        if nx:
            copies = _peer_copies(rest[:nx], rest[nx + 4:2 * nx + 4], rest[2 * nx + 8:], sc_ranges)

            @pl.when(i == 0)
            def _():
                for cp in copies:
                    cp.start()

        @pl.when(i == 0)
        def _():
            dh_sc[...] = jnp.zeros_like(dh_sc)
            dd_sc[...] = jnp.zeros_like(dd_sc)
            gdt_ref[...] = jnp.zeros_like(gdt_ref)
            dvec_ref[...] = jnp.zeros_like(dvec_ref)

        xs = act_ref[:, :SSD_INNER]
        dt_raw_v, bias_v = dt_ref[...], bias_ref[...]
        dt, a_neg, cs, dt_x, _, xdt, ds_x, ecs_x, elast = _ssd_common(
            dt_raw_v, bias_v, alog_ref[...], tril_ref[...], et_ref[...], xs)
        cs_t = cs.T
        rowi = lax.broadcasted_iota(jnp.int32, (SSD_L, SSD_L), 0)
        coli = lax.broadcasted_iota(jnp.int32, (SSD_L, SSD_L), 1)
        causal = rowi >= coli
        lane = lax.broadcasted_iota(jnp.int32, (SSD_L, LANES), 1)
        row_g = lax.broadcasted_iota(jnp.int32, (SSD_L, gw), 0)
        dyv = dy_ref[...]
        dd_sc[0:1, :] += jnp.sum(dyv * xs, axis=0, keepdims=True)
        xdt_b = _mx(xdt)
        xds = xdt * ds_x
        xds_b = _mx(xds)
        dy_b = _mx(dyv)
        dye_b = _mx(dyv * ecs_x)
        dcs_sc[...] = jnp.zeros_like(dcs_sc)
        dcst_sc[...] = jnp.zeros_like(dcst_sc)
        dcs_parts = []
        dxdt_parts = []
        for g in range(SSD_GROUPS):
            gs = slice(g * gw, (g + 1) * gw)
            bcol = slice(SSD_INNER + g * SSD_N, SSD_INNER + (g + 1) * SSD_N)
            ccol = slice(SSD_INNER + 512 + g * SSD_N, SSD_INNER + 512 + (g + 1) * SSD_N)
            bg, cg = _mx(act_ref[:, bcol]), _mx(act_ref[:, ccol])
            cb = _dot_nt(cg, bg)
            hp = hp_ref[0, :, gs]
            hp_b = _mx(hp)
            dh = dh_sc[:, gs]
            dh_b = _mx(dh)
            yoff = _dot(cg, hp_b) * ecs_x[:, gs]
            bdh = _dot(bg, dh_b)
            tt = xds[:, gs] * bdh
            last_row = (jnp.sum(tt, axis=0, keepdims=True)
                        + jnp.sum(dh * hp, axis=0, keepdims=True) * elast[:, gs])
            dcs_parts.append(dyv[:, gs] * yoff - tt + jnp.where(row_g == SSD_L - 1, last_row, 0.0))
            dc_g = _dot_nt(dye_b[:, gs], hp_b)
            db_g = _dot_nt(xds_b[:, gs], dh_b)
            dh_sc[:, gs] = _dot_tn(cg, dye_b[:, gs]) + dh * elast[:, gs]
            wsum = jnp.zeros((SSD_L, SSD_L), F32)
            dxdt_g = []
            for pr in range(4):
                ps = slice(g * gw + pr * LANES, g * gw + (pr + 1) * LANES)
                xp, dyp = xdt_b[:, ps], dy_b[:, ps]
                dxp = jnp.zeros((SSD_L, LANES), F32)
                for j in range(2):
                    hh = g * 8 + pr * 2 + j
                    dm = _ssd_decay(cs, cs_t, hh, causal)
                    mine = (lane >= j * 64) & (lane < (j + 1) * 64)
                    dy_h = jnp.where(mine, dyp, jnp.zeros_like(dyp))
                    wd = _dot_nt(dy_h, xp) * dm
                    wsum = wsum + wd
                    gmat = wd * cb
                    dcs_sc[:, hh:hh + 1] = jnp.sum(gmat, axis=1, keepdims=True)
                    dcst_sc[hh:hh + 1, :] = -jnp.sum(gmat, axis=0, keepdims=True)
                    dxp = dxp + _dot_tn(_mx(cb * dm), dy_h)
                dxdt_g.append(dxp)
            dxdt_parts.append(jnp.concatenate(dxdt_g, axis=1) + bdh * ds_x[:, gs])
            ws_b = _mx(wsum)
            dact_ref[:, ccol] = dc_g + _dot(ws_b, bg)
            dact_ref[:, bcol] = db_g + _dot_tn(ws_b, cg)
        dxdt = jnp.concatenate(dxdt_parts, axis=1)
        dcs_x = jnp.concatenate(dcs_parts, axis=1)
        et = et_ref[...]
        dcs_tot = dcs_sc[...] + dcst_sc[...].T + _fold_heads(dcs_x, et)
        da_dt = _dot_hi((coli >= rowi).astype(F32), dcs_tot)
        ddt = da_dt * a_neg + _fold_heads(dxdt * xs, et)
        ddt_raw = ddt * _sigmoid(dt_raw_v + bias_v)
        ddt_ref[...] = ddt_raw
        gdt_ref[...] += _dot_tn(x1_ref[...], _mx(ddt_raw))
        dvec_ref[0:1, :] += jnp.sum(ddt_raw, axis=0, keepdims=True)
        dvec_ref[1:2, :] += jnp.sum(da_dt * dt, axis=0, keepdims=True) * a_neg
        dact_ref[:, :SSD_INNER] = dyv * dx_ref[...] + dxdt * dt_x

        @pl.when(i == nc - 1)
        def _():
            dvec_ref[2:3, :] = _fold_heads(dd_sc[...], et)[0:1, :]
            if nx:
                for cp in copies:
                    cp.wait()

    const = lambda shape: pl.BlockSpec(shape, lambda c: (0, 0))
    rev = lambda c: (nc - 1 - c, 0)
    res = pl.pallas_call(
        body, name="ssd_scan_bwd", grid=(nc,),
        in_specs=[pl.BlockSpec((SSD_L, SSD_INNER), rev), pl.BlockSpec((SSD_L, SSD_CONV), rev),
                  pl.BlockSpec((SSD_L, LANES), rev),
                  pl.BlockSpec((1, SSD_N, SSD_INNER), lambda c: (nc - 1 - c, 0, 0)),
                  const((1, LANES)), const((1, LANES)), const((1, SSD_INNER)), const((SSD_L, SSD_L)),
                  const((SSD_INNER, LANES)), pl.BlockSpec((SSD_L, D_MODEL), rev)] + [ANY] * nx,
        out_specs=[pl.BlockSpec((SSD_L, SSD_CONV), rev), pl.BlockSpec((SSD_L, LANES), rev), const((SUBLANES, LANES)),
                   const((D_MODEL, LANES))] + [ANY] * nx,
        out_shape=[jax.ShapeDtypeStruct((SEQ, SSD_CONV), F32), jax.ShapeDtypeStruct((SEQ, LANES), F32),
                   jax.ShapeDtypeStruct((SUBLANES, LANES), F32), jax.ShapeDtypeStruct((D_MODEL, LANES), F32)]
        + _exchange_shapes(sc_arrays, []),
        scratch_shapes=[pltpu.VMEM((SSD_N, SSD_INNER), F32), pltpu.VMEM((SUBLANES, SSD_INNER), F32),
                        pltpu.VMEM((SSD_L, LANES), F32), pltpu.VMEM((LANES, SSD_L), F32)]
        + (_exchange_sems(nx) if nx else []),
        compiler_params=_params(("arbitrary",)),
    )(dy, act, dt_raw, hprev_all, bias, alog, d_x, tril, expand_t, x1b, *sc_arrays)
    return res[0], res[1], res[2], res[3], res[4:]


L1_T = 512


def _resident(shape):
    return pl.BlockSpec(shape, lambda i: (0, 0), pipeline_mode=pl.Buffered(1))


def _gated_norm(y, z, nw):
    y2 = y * _silu(z)
    gw = SSD_INNER // SSD_GROUPS
    outs, xhs, rss = [], [], []
    for g in range(SSD_GROUPS):
        gs = slice(g * gw, (g + 1) * gw)
        v = y2[:, gs]
        rs = lax.rsqrt(jnp.mean(v * v, axis=-1, keepdims=True) + 1e-6)
        xhs.append(v * rs)
        rss.append(rs)
        outs.append(v * rs * nw[:, gs])
    return outs, xhs, rss


def _l1_out(y, z, nw, w_out, x1, g, b, target):
    t = L1_T

    def body(y_ref, z_ref, nw_ref, w_ref, x1_ref, g_ref, b_ref, tg_ref, dv_ref, dgb_ref, loss_ref, gw_ref):
        @pl.when(pl.program_id(0) == 0)
        def _():
            dgb_ref[...] = jnp.zeros_like(dgb_ref)
            loss_ref[...] = jnp.zeros_like(loss_ref)
            gw_ref[...] = jnp.zeros_like(gw_ref)

        outs, _, _ = _gated_norm(y_ref[...], z_ref[...], nw_ref[...])
        yn = _mx(jnp.concatenate(outs, axis=1))
        v = DN_ALPHA * x1_ref[...] + _dot(yn, w_ref[...])
        gv = g_ref[...]
        err = _ln(v, gv, b_ref[...]) - tg_ref[...]
        rowsum = jnp.sum(err * err, axis=1, keepdims=True)
        loss_ref[...] += 0.5 * jnp.sum(rowsum, axis=0, keepdims=True) / D_MODEL
        dv, dg, db = _ln_bwd(v, gv, err / D_MODEL)
        dv_ref[...] = dv
        dgb_ref[0:1, :] += dg
        dgb_ref[1:2, :] += db
        gw_ref[...] += _dot_tn(yn, _mx(dv))

    wide = pl.BlockSpec((t, SSD_INNER), lambda i: (i, 0))
    full = pl.BlockSpec((t, D_MODEL), lambda i: (i, 0))
    vec = pl.BlockSpec((1, D_MODEL), lambda i: (0, 0))
    return pl.pallas_call(
        body, name="l1_out", grid=(SEQ // t,),
        in_specs=[wide, wide, pl.BlockSpec((1, SSD_INNER), lambda i: (0, 0)),
                  _resident((SSD_INNER, D_MODEL)), full, vec, vec, full],
        out_specs=[full, pl.BlockSpec((SUBLANES, D_MODEL), lambda i: (0, 0)),
                   pl.BlockSpec((SUBLANES, LANES), lambda i: (0, 0)), _resident((SSD_INNER, D_MODEL))],
        out_shape=[jax.ShapeDtypeStruct((SEQ, D_MODEL), F32), jax.ShapeDtypeStruct((SUBLANES, D_MODEL), F32),
                   jax.ShapeDtypeStruct((SUBLANES, LANES), F32), jax.ShapeDtypeStruct((SSD_INNER, D_MODEL), F32)],
        compiler_params=_params(("arbitrary",)),
    )(y, z, nw, w_out, x1, g, b, target)


def _l1_gate_bwd(dv1, w_out, y, z, nw, x1b):
    t = L1_T
    gw = SSD_INNER // SSD_GROUPS

    def body(dv_ref, w_ref, y_ref, z_ref, nw_ref, x1_ref, dy_ref, dz_ref, dnw_ref, gw_ref):
        @pl.when(pl.program_id(0) == 0)
        def _():
            dnw_ref[...] = jnp.zeros_like(dnw_ref)
            gw_ref[...] = jnp.zeros_like(gw_ref)

        dyn = _dot_nt(_mx(dv_ref[...]), w_ref[...])
        yv, zv, nwv = y_ref[...], z_ref[...], nw_ref[...]
        _, xhs, rss = _gated_norm(yv, zv, nwv)
        sz, dsz = _silu(zv), _dsilu(zv)
        for g in range(SSD_GROUPS):
            gs = slice(g * gw, (g + 1) * gw)
            d_out = dyn[:, gs]
            xh = xhs[g]
            dnw_ref[0:1, gs] += jnp.sum(d_out * xh, axis=0, keepdims=True)
            dxh = d_out * nwv[:, gs]
            dy2 = rss[g] * (dxh - xh * jnp.mean(dxh * xh, axis=-1, keepdims=True))
            dy_ref[:, gs] = dy2 * sz[:, gs]
            dz_ref[:, gs] = _mx(dy2 * yv[:, gs] * dsz[:, gs])
        gw_ref[...] += _dot_tn(x1_ref[...], dz_ref[...])

    wide = pl.BlockSpec((t, SSD_INNER), lambda i: (i, 0))
    return pl.pallas_call(
        body, name="l1_gate_bwd", grid=(SEQ // t,),
        in_specs=[pl.BlockSpec((t, D_MODEL), lambda i: (i, 0)), _resident((SSD_INNER, D_MODEL)),
                  wide, wide, pl.BlockSpec((1, SSD_INNER), lambda i: (0, 0)), pl.BlockSpec((t, D_MODEL), lambda i: (i, 0))],
        out_specs=[wide, wide, pl.BlockSpec((SUBLANES, SSD_INNER), lambda i: (0, 0)),
                   _resident((D_MODEL, SSD_INNER))],
        out_shape=[jax.ShapeDtypeStruct((SEQ, SSD_INNER), F32), jax.ShapeDtypeStruct((SEQ, SSD_INNER), MXU_DTYPE),
                   jax.ShapeDtypeStruct((SUBLANES, SSD_INNER), F32), jax.ShapeDtypeStruct((D_MODEL, SSD_INNER), F32)],
        compiler_params=_params(("arbitrary",)),
    )(dv1, w_out, y, z, nw, x1b)


MESH = pl.DeviceIdType.MESH
ANY = pl.BlockSpec(memory_space=pl.ANY)


def _flip(v, bit):
    return 1 - v if bit else v


def _all_gather(blocks, name):
    n = len(blocks)

    def body(*refs):
        x_refs, out_refs = refs[:n], refs[n:2 * n]
        send_sems, recv_sems, local_sems = refs[2 * n:]
        mx, my, mc = lax.axis_index("x"), lax.axis_index("y"), lax.axis_index("c")
        me, sibling = (mx, my, mc), (mx, my, 1 - mc)
        chips = [(1 - mx, my), (mx, 1 - my), (1 - mx, 1 - my)]

        def copy(a, k, block, to, own=False):
            px, py, pc = block
            slot = out_refs[a].at[4 * px + 2 * py + pc]
            return pltpu.make_async_remote_copy(
                src_ref=x_refs[a] if own else slot, dst_ref=slot,
                send_sem=send_sems.at[7 * a + k], recv_sem=recv_sems.at[7 * a + k], device_id=to, device_id_type=MESH)

        mine = [pltpu.make_async_copy(x_refs[a], out_refs[a].at[4 * mx + 2 * my + mc], local_sems.at[a])
                for a in range(n)]
        first = []
        for a in range(n):
            mine[a].start()
            first.append(copy(a, 0, me, sibling, own=True))
            first += [copy(a, 1 + j, me, (*chip, mc), own=True) for j, chip in enumerate(chips)]
        for cp in first:
            cp.start()
        passed = []
        for j, chip in enumerate(chips):
            for a in range(n):
                copy(a, 1 + j, (*chip, mc), me).wait_recv()
                fwd = copy(a, 4 + j, (*chip, mc), sibling)
                fwd.start()
                passed.append(fwd)
        for a in range(n):
            copy(a, 0, sibling, me).wait_recv()
            for j, chip in enumerate(chips):
                copy(a, 4 + j, (*chip, 1 - mc), me).wait_recv()
        for cp in first + passed:
            cp.wait_send()
        for cp in mine:
            cp.wait()

    return pl.pallas_call(
        body, name=name, in_specs=[ANY] * n, out_specs=[ANY] * n,
        out_shape=[jax.ShapeDtypeStruct((N_DEV,) + b.shape, b.dtype) for b in blocks],
        scratch_shapes=[pltpu.SemaphoreType.DMA((7 * n,)), pltpu.SemaphoreType.DMA((7 * n,)),
                        pltpu.SemaphoreType.DMA((n,))],
    )(*blocks)


def _l0_in(x, w0p, bcast=()):
    n = len(bcast)
    tm, tn = 1024, 1024
    gi, gj = SEQ // tm, 2048 // tn

    def body(x_ref, w_ref, *rest):
        o_ref, xb_ref = rest[n], rest[n + 1]
        i, j = pl.program_id(0), pl.program_id(1)
        if n:
            copies = _peer_copies(rest[:n], rest[n + 2:2 * n + 2], rest[2 * n + 2:], [])

            @pl.when((i == 0) & (j == 0))
            def _():
                for cp in copies:
                    cp.start()

        xb = _mx(x_ref[...])
        xb_ref[...] = xb
        o_ref[...] = _dot(xb, w_ref[...])

        if n:
            @pl.when((i == gi - 1) & (j == gj - 1))
            def _():
                for cp in copies:
                    cp.wait()

    res = pl.pallas_call(
        body, name="l0_in", grid=(gi, gj),
        in_specs=[pl.BlockSpec((tm, D_MODEL), lambda i, j: (i, 0)), pl.BlockSpec((D_MODEL, tn), lambda i, j: (0, j))]
        + [ANY] * n,
        out_specs=[pl.BlockSpec((tm, tn), lambda i, j: (i, j)), pl.BlockSpec((tm, D_MODEL), lambda i, j: (i, 0))]
        + [ANY] * n,
        out_shape=[jax.ShapeDtypeStruct((SEQ, 2048), F32), jax.ShapeDtypeStruct((SEQ, D_MODEL), MXU_DTYPE)]
        + _exchange_shapes([], bcast),
        scratch_shapes=_exchange_sems(n) if n else [],
        compiler_params=_params(("arbitrary", "arbitrary")),
    )(x, w0p, *bcast)
    return res[0], res[1], res[2:]


def _l0_dx(dxr, dgate, dtail, w0p, dv0, scatter=(), bcast=()):
    arrays, ranges = _scatter_args(scatter)
    n = len(arrays) + len(bcast)
    tm = 1024
    steps = SEQ // tm

    def body(dxr_ref, dg_ref, dt_ref, w_ref, dv_ref, *rest):
        o_ref = rest[n]
        i = pl.program_id(0)
        if n:
            copies = _peer_copies(rest[:n], rest[n + 1:2 * n + 1], rest[2 * n + 1:], ranges)

            @pl.when(i == 0)
            def _():
                for cp in copies:
                    cp.start()

        o_ref[...] = (DN_ALPHA * dv_ref[...] + _dot_nt(dg_ref[...], w_ref[:, 0:1024])
                      + _dot_nt(dxr_ref[...], w_ref[:, 1024:1536]) + _dot_nt(dt_ref[...], w_ref[:, 1536:2048]))

        if n:
            @pl.when(i == steps - 1)
            def _():
                for cp in copies:
                    cp.wait()

    rows = lambda w: pl.BlockSpec((tm, w), lambda i: (i, 0))
    res = pl.pallas_call(
        body, name="l0_dx", grid=(steps,),
        in_specs=[rows(512), rows(1024), rows(512), pl.BlockSpec((D_MODEL, 2048), lambda i: (0, 0)), rows(D_MODEL)]
        + [ANY] * n,
        out_specs=[rows(D_MODEL)] + [ANY] * n,
        out_shape=[jax.ShapeDtypeStruct((SEQ, D_MODEL), F32)] + _exchange_shapes(arrays, bcast),
        scratch_shapes=_exchange_sems(n) if n else [],
        compiler_params=_params(("arbitrary",)),
    )(dxr, dgate, dtail, w0p, dv0, *arrays, *bcast)
    return res[0], res[1:]


def _scatter_args(scatter):
    arrays = [s[0] if isinstance(s, tuple) else s for s in scatter]
    ranges = [s[1] if isinstance(s, tuple) else (0, N_DEV) for s in scatter]
    return arrays, ranges


def _exchange_shapes(scatter, bcast):
    return ([jax.ShapeDtypeStruct((N_DEV,) + a.shape[1:], a.dtype) for a in scatter]
            + [jax.ShapeDtypeStruct((N_DEV,) + a.shape, a.dtype) for a in bcast])


def _exchange_sems(n):
    return [pltpu.SemaphoreType.DMA((7 * n,)), pltpu.SemaphoreType.DMA((7 * n,)), pltpu.SemaphoreType.DMA((n,))]


class _GuardedCopy:
    def __init__(self, copy, send=None, recv=None, local=False):
        self.copy, self.send, self.recv, self.local = copy, send, recv, local

    @staticmethod
    def _run(pred, fn):
        if pred is None:
            fn()
        else:
            pl.when(pred)(fn)

    def start(self):
        self._run(self.send, self.copy.start)

    def wait(self):
        if self.local:
            self._run(self.send, self.copy.wait)
        else:
            self._run(self.send, self.copy.wait_send)
            self._run(self.recv, self.copy.wait_recv)


def _peer_copies(in_refs, out_refs, sems, ranges):
    send_sems, recv_sems, local_sems = sems
    n, ns = len(in_refs), len(ranges)
    mx, my, mc = lax.axis_index("x"), lax.axis_index("y"), lax.axis_index("c")
    me = 4 * mx + 2 * my + mc

    def src(a, slot):
        return in_refs[a].at[slot - ranges[a][0]] if a < ns else in_refs[a]

    def member(a, dev):
        if a >= ns or ranges[a] == (0, N_DEV):
            return None
        return (dev >= ranges[a][0]) & (dev < ranges[a][1])

    copies = [_GuardedCopy(pltpu.make_async_copy(src(a, me), out_refs[a].at[me], local_sems.at[a]),
                           send=member(a, me), local=True) for a in range(n)]
    for k in range(1, N_DEV):
        px, py, pc = _flip(mx, (k >> 2) & 1), _flip(my, (k >> 1) & 1), _flip(mc, k & 1)
        peer = 4 * px + 2 * py + pc
        for a in range(n):
            copies.append(_GuardedCopy(pltpu.make_async_remote_copy(
                src_ref=src(a, peer), dst_ref=out_refs[a].at[me],
                send_sem=send_sems.at[7 * a + k - 1], recv_sem=recv_sems.at[7 * a + k - 1],
                device_id=(px, py, pc), device_id_type=MESH), send=member(a, peer), recv=member(a, me)))
    return copies


def _segments(col_map, width):
    segs = []
    for lo, hi, arr, alo in col_map:
        for s in range(N_DEV):
            a, b = max(lo, s * width), min(hi, (s + 1) * width)
            if a < b:
                segs.append((s, a - s * width, b - a, arr, alo + a - lo))
    return segs


COPY_ROWS = 256


def _unshard(g8, col_map, widths, name):
    _, r, w = g8.shape
    rb = min(r, COPY_ROWS)
    segs = _segments(col_map, w)

    def body(g_ref, *o_refs):
        for o_ref in o_refs:
            o_ref[...] = jnp.zeros_like(o_ref)
        for s, llo, n, arr, alo in segs:
            o_refs[arr][:, alo:alo + n] = g_ref[s, :, llo:llo + n]

    return pl.pallas_call(
        body, name=name, grid=(r // rb,),
        in_specs=[pl.BlockSpec((N_DEV, rb, w), lambda i: (0, i, 0))],
        out_specs=[pl.BlockSpec((rb, n), lambda i: (i, 0)) for n in widths],
        out_shape=[jax.ShapeDtypeStruct((r, n), g8.dtype) for n in widths],
        compiler_params=_params(("parallel",)),
    )(g8)


def _reshard(srcs, col_map, w, dtype, name, shards=(0, N_DEV)):
    r = srcs[0].shape[0]
    rb = min(r, COPY_ROWS)
    lo, hi = shards
    segs = [sg for sg in _segments(col_map, w) if lo <= sg[0] < hi]

    def body(*refs):
        o_ref = refs[-1]
        for s, llo, n, arr, alo in segs:
            o_ref[s - lo, :, llo:llo + n] = refs[arr][:, alo:alo + n].astype(dtype)

    return pl.pallas_call(
        body, name=name, grid=(r // rb,),
        in_specs=[pl.BlockSpec((rb, a.shape[1]), lambda i: (i, 0)) for a in srcs],
        out_specs=pl.BlockSpec((hi - lo, rb, w), lambda i: (0, i, 0)),
        out_shape=jax.ShapeDtypeStruct((hi - lo, r, w), dtype),
        compiler_params=_params(("parallel",)),
    )(*srcs)


def _adamw(parts, w, m, v, name):
    r, c = w.shape
    tr = COPY_ROWS if r % COPY_ROWS == 0 else r

    def body(p_ref, w_ref, m_ref, v_ref, g_ref, d_ref, mo_ref, vo_ref):
        g = p_ref[0].astype(F32)
        for s in range(1, N_DEV):
            g = g + p_ref[s].astype(F32)
        g_ref[...] = g
        d_ref[...], mo_ref[...], vo_ref[...] = _adamw_math(g, w_ref[...], m_ref[...], v_ref[...])

    blk = pl.BlockSpec((tr, c), lambda i: (i, 0))
    out = jax.ShapeDtypeStruct((r, c), F32)
    return pl.pallas_call(
        body, name=name, grid=(r // tr,),
        in_specs=[pl.BlockSpec((N_DEV, tr, c), lambda i: (0, i, 0)), blk, blk, blk],
        out_specs=[blk, blk, blk, blk], out_shape=[out, out, out, out],
        compiler_params=_params(("parallel",)),
    )(parts, w, m, v)


def _adamw_math(g, w, m, v):
    mn = ADAM_B1 * m + (1.0 - ADAM_B1) * g
    vn = ADAM_B2 * v + (1.0 - ADAM_B2) * (g * g)
    m_hat = mn / (1.0 - ADAM_B1 ** ADAM_STEP)
    v_hat = vn / (1.0 - ADAM_B2 ** ADAM_STEP)
    return -ADAM_LR * (m_hat / (jnp.sqrt(v_hat) + ADAM_EPS) + ADAM_WD * w), mn, vn


SMALL = (("ab_conv_w", 0, 4, 64), ("ssd_conv_w", 4, 4, 384), ("ssd_conv_b", 8, 1, 384), ("ssd_norm", 9, 1, 256),
         ("ssd_ln_g", 10, 1, 128), ("ssd_ln_b", 11, 1, 128))
VECS = (("ab_conv_b", 512), ("ab_gate_a_b", 512), ("ab_gate_x_b", 512), ("ab_lambda", 512), ("mla_q_norm", 256),
        ("mla_kv_norm", 128), ("ab_ln_g", 1024), ("ab_ln_b", 1024), ("ssd_dt_bias", 32), ("ssd_a_log", 32),
        ("ssd_d", 32))
GATES = ("ab_gate_a_w", "ab_gate_x_w")
SMALL_NAMES = tuple(n for n, *_ in SMALL) + tuple(n for n, _ in VECS) + GATES
VMEM_WHOLE = pl.BlockSpec(memory_space=pltpu.VMEM)


def _view2d(name, a):
    if name in GATES:
        return a.reshape(RNN_W, 64)
    return a[0] if a.ndim == 3 else a


def _unshard_small(g):
    widths = (512, 3072, 3072, 2048, 1024, 1024)

    def body(*refs):
        ins, outs = refs[:6], refs[6:]
        outs[0][...] = jnp.zeros_like(outs[0])
        outs[1][...] = jnp.zeros_like(outs[1])
        for (_, _, nr, c), i_ref, o_ref in zip(SMALL, ins, outs):
            for j in range(N_DEV):
                o_ref[0:nr, j * c:(j + 1) * c] = i_ref[j]

    return pl.pallas_call(
        body, name="unshard_small", in_specs=[VMEM_WHOLE] * 6, out_specs=[VMEM_WHOLE] * 6,
        out_shape=[jax.ShapeDtypeStruct((SUBLANES if nr == 4 else 1, w), F32) for (_, _, nr, _), w in zip(SMALL, widths)],
    )(*g)


def _prep_repl(ga, gx, dt_bias, a_log, d):
    def body(ga_ref, gx_ref, b_ref, al_ref, d_ref, wa_ref, wx_ref, b128_ref, al128_ref, dx_ref):
        wa_ref[...] = jnp.zeros_like(wa_ref)
        wx_ref[...] = jnp.zeros_like(wx_ref)
        for hd in range(8):
            hs = slice(hd * 64, (hd + 1) * 64)
            wa_ref[hs, hs] = _mx(ga_ref[hs, :])
            wx_ref[hs, hs] = _mx(gx_ref[hs, :])
        b128_ref[...] = jnp.zeros_like(b128_ref)
        al128_ref[...] = jnp.zeros_like(al128_ref)
        b128_ref[:, 0:SSD_HEADS] = b_ref[...]
        al128_ref[:, 0:SSD_HEADS] = al_ref[...]
        dv = d_ref[...]
        for hd in range(SSD_HEADS):
            dx_ref[:, hd * SSD_P:(hd + 1) * SSD_P] = jnp.broadcast_to(dv[:, hd:hd + 1], (1, SSD_P))

    return pl.pallas_call(
        body, name="prep_repl", in_specs=[VMEM_WHOLE] * 5, out_specs=[VMEM_WHOLE] * 5,
        out_shape=[jax.ShapeDtypeStruct((RNN_W, RNN_W), MXU_DTYPE), jax.ShapeDtypeStruct((RNN_W, RNN_W), MXU_DTYPE),
                   jax.ShapeDtypeStruct((1, LANES), F32), jax.ShapeDtypeStruct((1, LANES), F32),
                   jax.ShapeDtypeStruct((1, SSD_INNER), F32)],
    )(ga, gx, dt_bias, a_log, d)


LOSS_ROW = 11


def _pack_small(dvec0, g_wa, g_wx, dqnw, dknw, dgb0, dvec1, dcw1, dnw, dgb1, loss8):
    def body(dvec0_ref, gwa_ref, gwx_ref, dqn_ref, dkn_ref, dgb0_ref, dvec1_ref, dcw1_ref, dnw_ref, dgb1_ref,
             loss_ref, sm_ref, vec_ref, gg_ref):
        sm_ref[...] = jnp.zeros_like(sm_ref)
        vec_ref[...] = jnp.zeros_like(vec_ref)
        sharded = ((dvec0_ref, 4), (dcw1_ref, 0), (dcw1_ref, 4), (dnw_ref, 0), (dgb1_ref, 0), (dgb1_ref, 1))
        for (_, r0, nr, c), (src, sr) in zip(SMALL, sharded):
            for j in range(N_DEV):
                sm_ref[j, r0:r0 + nr, 0:c] = src[sr:sr + nr, j * c:(j + 1) * c]
        vectors = ((dvec0_ref, 3), (dvec0_ref, 0), (dvec0_ref, 1), (dvec0_ref, 2), (dqn_ref, 0), (dkn_ref, 0),
                   (dgb0_ref, 0), (dgb0_ref, 1), (dvec1_ref, 0), (dvec1_ref, 1), (dvec1_ref, 2))
        for row, ((_, c), (src, sr)) in enumerate(zip(VECS, vectors)):
            vec_ref[row:row + 1, 0:c] = src[sr:sr + 1, 0:c]
        vec_ref[LOSS_ROW:LOSS_ROW + 1, 0:LANES] = loss_ref[0:1, :]
        for hd in range(8):
            hs = slice(hd * 64, (hd + 1) * 64)
            gg_ref[hs, 0:64] = _mx(gwa_ref[hs, hs])
            gg_ref[hs, 64:128] = _mx(gwx_ref[hs, hs])

    return pl.pallas_call(
        body, name="pack_small", in_specs=[VMEM_WHOLE] * 11, out_specs=[VMEM_WHOLE] * 3,
        out_shape=[jax.ShapeDtypeStruct((N_DEV, 16, 384), F32), jax.ShapeDtypeStruct((16, 1024), F32),
                   jax.ShapeDtypeStruct((RNN_W, LANES), MXU_DTYPE)],
    )(dvec0, g_wa, g_wx, dqnw, dknw, dgb0, dvec1, dcw1, dnw, dgb1, loss8)


def _adamw_small(recv_sm, recv_vec, recv_gg, wmv):
    plan = ([(0, r0, nr, c) for _, r0, nr, c in SMALL] + [(1, row, 1, c) for row, (_, c) in enumerate(VECS)]
            + [(2, 0, RNN_W, 0), (2, 0, RNN_W, 64)])
    n = len(plan)

    def body(*refs):
        recv, ins, outs = refs[:3], refs[3:3 + 3 * n], refs[3 + 3 * n:]
        for i, (src, r0, nr, c) in enumerate(plan):
            cols = slice(c, c + 64) if src == 2 else slice(0, c)
            g = recv[src][0, r0:r0 + nr, cols].astype(F32)
            for s in range(1, N_DEV):
                g = g + recv[src][s, r0:r0 + nr, cols].astype(F32)
            w_ref, m_ref, v_ref = ins[3 * i:3 * i + 3]
            outs[4 * i][...] = g
            outs[4 * i + 1][...], outs[4 * i + 2][...], outs[4 * i + 3][...] = _adamw_math(
                g, w_ref[...], m_ref[...], v_ref[...])
        loss = recv[1][0, LOSS_ROW:LOSS_ROW + 1, 0:LANES]
        for s in range(1, N_DEV):
            loss = loss + recv[1][s, LOSS_ROW:LOSS_ROW + 1, 0:LANES]
        outs[4 * n][...] = loss

    flat = [a for t in wmv for a in t]
    return pl.pallas_call(
        body, name="adamw_small", in_specs=[VMEM_WHOLE] * (3 + 3 * n), out_specs=[VMEM_WHOLE] * (4 * n + 1),
        out_shape=[jax.ShapeDtypeStruct(t[0].shape, F32) for t in wmv for _ in range(4)]
        + [jax.ShapeDtypeStruct((1, LANES), F32)],
    )(recv_sm, recv_vec, recv_gg, *flat)


BIG_L0 = ("ab_w_in", "ab_w_out", "mla_w_uq", "mla_w_ukv")
BIG_L1 = ("ssd_w_in", "ssd_w_out")

MAP_W0 = ((0, 512, 0, 1024), (512, 1536, 0, 0), (1536, 1920, 0, 1536), (1920, 1952, 0, 1984))
MAP_W1 = ((0, 2048, 0, 0), (2048, 5120, 1, 0), (5120, 5152, 2, 0))
MAP_WQ = tuple((96 * hd, 96 * hd + 96, 0, 128 * hd) for hd in range(8))
MAP_WKV = (tuple((128 * hd, 128 * hd + 64, 0, 128 * hd) for hd in range(8))
           + tuple((128 * hd + 64, 128 * hd + 128, 0, 1024 + 64 * hd) for hd in range(8)))
MAP_G0 = ((0, 512, 0, 0), (512, 1536, 1, 0), (1536, 1920, 2, 0), (1920, 1952, 2, 448))
W0_EARLY, W0_LATE = (0, 6), (6, 8)


def kernel(x, positions, ab_w_in, ab_conv_w, ab_conv_b, ab_gate_a_w, ab_gate_a_b, ab_gate_x_w, ab_gate_x_b, ab_lambda, mla_q_norm, mla_kv_norm, mla_w_uq, mla_w_ukv, ab_w_out, ab_ln_g, ab_ln_b, ssd_w_in, ssd_conv_w, ssd_conv_b, ssd_dt_bias, ssd_a_log, ssd_d, ssd_norm, ssd_w_out, ssd_ln_g, ssd_ln_b, loss_target, m_ab_w_in, m_ab_conv_w, m_ab_conv_b, m_ab_gate_a_w, m_ab_gate_a_b, m_ab_gate_x_w, m_ab_gate_x_b, m_ab_lambda, m_mla_q_norm, m_mla_kv_norm, m_mla_w_uq, m_mla_w_ukv, m_ab_w_out, m_ab_ln_g, m_ab_ln_b, m_ssd_w_in, m_ssd_conv_w, m_ssd_conv_b, m_ssd_dt_bias, m_ssd_a_log, m_ssd_d, m_ssd_norm, m_ssd_w_out, m_ssd_ln_g, m_ssd_ln_b, v_ab_w_in, v_ab_conv_w, v_ab_conv_b, v_ab_gate_a_w, v_ab_gate_a_b, v_ab_gate_x_w, v_ab_gate_x_b, v_ab_lambda, v_mla_q_norm, v_mla_kv_norm, v_mla_w_uq, v_mla_w_ukv, v_ab_w_out, v_ab_ln_g, v_ab_ln_b, v_ssd_w_in, v_ssd_conv_w, v_ssd_conv_b, v_ssd_dt_bias, v_ssd_a_log, v_ssd_d, v_ssd_norm, v_ssd_w_out, v_ssd_ln_g, v_ssd_ln_b):
    args = dict(locals())
    bf = MXU_DTYPE
    big = {n: [args[pre + n][0] for pre in ("", "m_", "v_")] for n in BIG_L0 + BIG_L1}
    sml = {n: [_view2d(n, args[pre + n]) for pre in ("", "m_", "v_")] for n in SMALL_NAMES}

    w0_8, cw0_8 = _all_gather([big["ab_w_in"][0].astype(bf), sml["ab_conv_w"][0]], "gather_params")
    p = {"cw0_8": cw0_8, "l0_blocks": [big[n][0].astype(bf) for n in BIG_L0[1:]] + [sml[n][0] for n, *_ in SMALL[1:]]}
    p["w0p"], = _unshard(w0_8, MAP_W0, (2048,), "unshard_w0")
    p["wa"], p["wx"], p["dt_bias"], p["a_log"], p["d_x"] = _prep_repl(
        sml["ab_gate_a_w"][0], sml["ab_gate_x_w"][0], sml["ssd_dt_bias"][0], sml["ssd_a_log"][0], sml["ssd_d"][0])
    for key, n in (("cb0", "ab_conv_b"), ("ba", "ab_gate_a_b"), ("bx", "ab_gate_x_b"), ("lam", "ab_lambda"),
                   ("qn_w", "mla_q_norm"), ("kn_w", "mla_kv_norm"), ("g0", "ab_ln_g"), ("b0", "ab_ln_b")):
        p[key] = sml[n][0]

    _, recv_early, recv, _, grad_x = _local_step(
        x[0], positions[0], loss_target[0], p, [big[n][0].astype(bf) for n in BIG_L1])

    me = 4 * lax.axis_index("x") + 2 * lax.axis_index("y") + lax.axis_index("c")
    parts = dict(recv_early, ab_w_in=jnp.where(me >= W0_LATE[0], recv[0], recv_early["ab_w_in"]),
                 mla_w_uq=recv[1], mla_w_ukv=recv[2])

    outs = {}
    kinds = ("grad", "delta", "new_m", "new_v")
    for n in BIG_L0 + BIG_L1:
        for kind, res in zip(kinds, _adamw(parts[n], *big[n], "adamw_" + n)):
            outs[kind, n] = res[None]
    res = _adamw_small(*recv[3:], [sml[n] for n in SMALL_NAMES])
    for i, n in enumerate(SMALL_NAMES):
        for k, kind in enumerate(kinds):
            outs[kind, n] = res[4 * i + k].reshape(args[n].shape)

    loss = res[4 * len(SMALL_NAMES)][0, 0]
    order = ["ab_w_in", "ab_conv_w", "ab_conv_b", "ab_gate_a_w", "ab_gate_a_b", "ab_gate_x_w", "ab_gate_x_b",
             "ab_lambda", "mla_q_norm", "mla_kv_norm", "mla_w_uq", "mla_w_ukv", "ab_w_out", "ab_ln_g", "ab_ln_b",
             "ssd_w_in", "ssd_conv_w", "ssd_conv_b", "ssd_dt_bias", "ssd_a_log", "ssd_d", "ssd_norm", "ssd_w_out",
             "ssd_ln_g", "ssd_ln_b"]
    return (loss, grad_x[None], *[outs[kind, n] for kind in ("grad", "delta", "new_m", "new_v") for n in order])


def _local_step(x, pos, target, p, l1_blocks):
    bf = MXU_DTYPE
    inv_freq = 10000.0 ** (-jnp.arange(0, 32, 2, dtype=F32) / 32)
    ang = pos.astype(F32)[:, None] * inv_freq
    cos, sin = jnp.cos(ang), jnp.sin(ang)
    zeros = lambda n: jnp.zeros((SEQ, n), F32)
    tc = jnp.concatenate([jnp.ones((SEQ, 64), F32), cos, cos, zeros(32)], axis=1)
    tsa = jnp.concatenate([zeros(64), -sin, zeros(48)], axis=1)
    tsb = jnp.concatenate([zeros(80), sin, zeros(32)], axis=1)

    w0p, wa, wxg = (p[k] for k in ("w0p", "wa", "wx"))
    cb0, ba, bx, lam = (p[k] for k in ("cb0", "ba", "bx", "lam"))
    qn_w, kn_w, g0, b0 = (p[k] for k in ("qn_w", "kn_w", "g0", "b0"))
    dt_bias, a_log, d_x = (p[k] for k in ("dt_bias", "a_log", "d_x"))
    tril = jnp.tril(jnp.ones((SSD_L, SSD_L), F32))
    expand_t = (jnp.arange(SSD_INNER)[:, None] // SSD_P == jnp.arange(LANES)[None, :]).astype(jnp.bfloat16)

    proj0, xb, l0_8 = _l0_in(x, w0p, bcast=p["l0_blocks"])
    wo0 = l0_8[0].reshape(D_MODEL, D_MODEL)
    wq, = _unshard(l0_8[1], MAP_WQ, (1024,), "unshard_wq")
    wkv, = _unshard(l0_8[2], MAP_WKV, (1536,), "unshard_wkv")
    cw0, cw1, cb1, nw, g1, b1 = _unshard_small([p["cw0_8"]] + list(l0_8[3:]))
    xc, h = _rglru_fwd(proj0, cw0, cb0, wa, ba, wxg, bx, lam)
    qn, kn, qc, kc, vc = _mla_fwd(proj0, qn_w, kn_w, wq, wkv, tc, tsa, tsb)
    o, lse, (w1_8,) = _flash_fwd(qc, kc, vc, bcast=l1_blocks[:1])
    w1z, w1x, w1d = _unshard(w1_8, MAP_W1, (2048, 3072, 128), "unshard_w1")
    y0, v0, x1, x1b = _l0_out(h, o, proj0, x, wo0, g0, b0)

    z, dt_raw = _l1_in(x1b, w1z, w1d)
    xbc, pre, act = _ssd_conv_fwd(x1b, w1x, cw1, cb1)
    ys, hprev, (wo1_8,) = _ssd_scan_fwd(act, dt_raw, dt_bias, a_log, d_x, tril, expand_t, bcast=l1_blocks[1:])
    wo1 = wo1_8.reshape(SSD_INNER, D_MODEL)
    dv1, dgb1, loss8, g_wo1 = _l1_out(ys, z, nw, wo1, x1, g1, b1, target)

    dys, dz, dnw, g_z = _l1_gate_bwd(dv1, wo1, ys, z, nw, x1b)
    dact, ddt_raw, dvec1, g_dt, (recv_wo1,) = _ssd_scan_bwd(
        dys, act, dt_raw, hprev, dt_bias, a_log, d_x, tril, expand_t, x1b,
        scatter=[g_wo1.astype(bf).reshape(N_DEV, 256, D_MODEL)])
    dxbc, dcw1, g_xbc = _ssd_conv_bwd(dact, pre, xbc, cw1, x1b)

    dv0, dgb0 = _l1_dx_ln(dz, dxbc, ddt_raw, dv1, v0, w1z, w1x, w1d, g0)
    dh, do, dgate, g_wo0, g_gate = _gate_bwd(dv0, wo0, h, o, proj0, y0, xb)
    dxr, g_wa, g_wx, dvec0, g_rnn = _rglru_bwd(dh, xc, h, proj0, cw0, wa, ba, wxg, bx, lam, xb)
    early = [_reshard([g_z, g_xbc, g_dt], MAP_W1, 644, bf, "reshard_w1"), g_wo0.astype(bf).reshape(N_DEV, 128, D_MODEL),
             (_reshard([g_rnn, g_gate], MAP_G0, 244, bf, "reshard_w0_early", shards=W0_EARLY), W0_EARLY)]
    dq, dk, dvv, (recv_w1, recv_wo0, recv_w0) = _flash_bwd(qc, kc, vc, o, do, lse, scatter=early)
    recv_early = {"ssd_w_in": recv_w1, "ssd_w_out": recv_wo1, "ab_w_out": recv_wo0, "ab_w_in": recv_w0}
    dtail, g_wq, g_wkv, dqnw, dknw, g_tail = _mla_bwd(dq, dk, dvv, proj0, qn, kn, qn_w, kn_w, wq, wkv, tc, tsa, tsb, xb)

    acc = {"g_rnn": g_rnn, "g_gate": g_gate, "g_tail": g_tail, "g_wq": g_wq, "g_wkv": g_wkv,
           "dvec0": dvec0, "g_wa": g_wa, "g_wx": g_wx, "dqnw": dqnw, "dknw": dknw, "dgb0": dgb0, "dvec1": dvec1,
           "dcw1": dcw1, "dnw": dnw, "dgb1": dgb1}
    late = [(_reshard([g_rnn, g_gate, g_tail], MAP_G0, 244, bf, "reshard_w0_late", shards=W0_LATE), W0_LATE),
            _reshard([g_wq], MAP_WQ, 96, bf, "reshard_wq"), _reshard([g_wkv], MAP_WKV, 128, bf, "reshard_wkv")]
    sm_slots, vec_rows, gates = _pack_small(dvec0, g_wa, g_wx, dqnw, dknw, dgb0, dvec1, dcw1, dnw, dgb1, loss8)
    dx, recv_late = _l0_dx(dxr, dgate, dtail, w0p, dv0, scatter=late + [sm_slots], bcast=[vec_rows, gates])
    return acc, recv_early, recv_late, loss8[0, 0], dx
```

```python
import math

import jax
import jax.numpy as jnp
from jax import lax
from jax.experimental import pallas as pl
from jax.experimental.pallas import tpu as pltpu

F32 = jnp.float32
MXU_DTYPE = jnp.bfloat16

N_DEV = 8
SEQ = 4096
D_MODEL = 1024
DN_ALPHA = 4.0 ** 0.25
RNN_W = 512
MLA_HEADS = 8
ATT_SCALE = 96.0 ** -0.5
ATT_C = ATT_SCALE * math.log2(math.e)
RG_C = 8.0
SSD_INNER = 2048
SSD_HEADS = 32
SSD_P = 64
SSD_GROUPS = 4
SSD_N = 128
SSD_L = 128
SSD_CONV = 3072
LANES = 128
SUBLANES = 8
VMEM_LIMIT = 56 * 1024 * 1024

ADAM_LR, ADAM_B1, ADAM_B2, ADAM_EPS, ADAM_WD, ADAM_STEP = 0.001, 0.9, 0.999, 1e-08, 0.01, 10

HIGHEST = lax.Precision.HIGHEST


def _params(sem, limit=VMEM_LIMIT):
    return pltpu.CompilerParams(dimension_semantics=sem, vmem_limit_bytes=limit)


def _dot(a, b):
    return lax.dot_general(a, b, (((1,), (0,)), ((), ())), preferred_element_type=F32)


def _dot_nt(a, b):
    return lax.dot_general(a, b, (((1,), (1,)), ((), ())), preferred_element_type=F32)


def _dot_tn(a, b):
    return lax.dot_general(a, b, (((0,), (0,)), ((), ())), preferred_element_type=F32)


def _dot_hi(a, b):
    return lax.dot_general(a, b, (((1,), (0,)), ((), ())), precision=HIGHEST, preferred_element_type=F32)


def _mx(v):
    return v.astype(MXU_DTYPE)


def _sigmoid(v):
    return 1.0 / (1.0 + jnp.exp(-v))


def _log1p_pos(e):
    poly = e * (1.0 - e * (0.5 - e * (1.0 / 3.0 - e * 0.25)))
    return jnp.where(e < 0.01, poly, jnp.log(1.0 + e))


def _softplus(v):
    return jnp.maximum(v, 0.0) + _log1p_pos(jnp.exp(-jnp.abs(v)))


def _neg_expm1(v):
    poly = -v * (1.0 + v * (0.5 + v * (1.0 / 6.0 + v * (1.0 / 24.0 + v * (1.0 / 120.0)))))
    return jnp.where(jnp.abs(v) < 0.1, poly, 1.0 - jnp.exp(v))


def _silu(v):
    return v * _sigmoid(v)


def _dsilu(v):
    s = _sigmoid(v)
    return s * (1.0 + v * (1.0 - s))


def _shift_down(blk, halo, s):
    if s == 0:
        return blk
    t = blk.shape[0]
    r = pltpu.roll(blk, s, 0)
    hr = pltpu.roll(halo, s, 0)
    row8 = lax.broadcasted_iota(jnp.int32, hr.shape, 0)
    head = jnp.where(row8 < s, hr, r[:SUBLANES])
    return jnp.concatenate([head, r[SUBLANES:]], axis=0) if t > SUBLANES else head


def _shift_up(blk, halo, s):
    if s == 0:
        return blk
    t = blk.shape[0]
    r = pltpu.roll(blk, t - s, 0)
    hr = pltpu.roll(halo, SUBLANES - s, 0)
    row8 = lax.broadcasted_iota(jnp.int32, hr.shape, 0)
    tail = jnp.where(row8 >= SUBLANES - s, hr, r[t - SUBLANES:])
    return jnp.concatenate([r[:t - SUBLANES], tail], axis=0) if t > SUBLANES else tail


def _scan_down(a, u):
    t = a.shape[0]
    row = lax.broadcasted_iota(jnp.int32, a.shape, 0)
    d = 1
    while d < t:
        keep = row >= d
        a_sh = jnp.where(keep, pltpu.roll(a, d, 0), 1.0)
        u_sh = jnp.where(keep, pltpu.roll(u, d, 0), 0.0)
        u = a * u_sh + u
        a = a * a_sh
        d *= 2
    return a, u


def _scan_up(a, u):
    t = a.shape[0]
    row = lax.broadcasted_iota(jnp.int32, a.shape, 0)
    d = 1
    while d < t:
        keep = row < t - d
        a_sh = jnp.where(keep, pltpu.roll(a, t - d, 0), 1.0)
        u_sh = jnp.where(keep, pltpu.roll(u, t - d, 0), 0.0)
        u = a * u_sh + u
        a = a * a_sh
        d *= 2
    return a, u


def _conv4(blk, halo, cw, cb):
    out = cb + blk * cw[3:4]
    for k in range(3):
        out = out + _shift_down(blk, halo, 3 - k) * cw[k:k + 1]
    return out


RG_T = 512
P0_RNN = 2


def _rg_gates(xc, wa, ba, wx, bx, lam):
    xcb = _mx(xc)
    r = _sigmoid(_dot(xcb, wa) + ba)
    ig = _sigmoid(_dot(xcb, wx) + bx)
    sp = _softplus(-lam)
    la = (-RG_C * r) * sp
    a = jnp.exp(la)
    mult = jnp.sqrt(_neg_expm1(2.0 * la))
    return r, ig, sp, a, mult


def _rglru_fwd(proj0, cw8, cb, wa, ba, wx, bx, lam):
    t, w = RG_T, RNN_W
    nb = SEQ // t

    def body(x_ref, halo_ref, cw_ref, cb_ref, wa_ref, ba_ref, wx_ref, bx_ref, lam_ref, xc_ref, h_ref, carry):
        i = pl.program_id(0)

        @pl.when(i == 0)
        def _():
            carry[...] = jnp.zeros_like(carry)

        blk = x_ref[...]
        halo = jnp.where(i > 0, halo_ref[...], 0.0)
        xc = _conv4(blk, halo, cw_ref[...], cb_ref[...])
        _, ig, _, a, mult = _rg_gates(xc, wa_ref[...], ba_ref[...], wx_ref[...], bx_ref[...], lam_ref[...])
        u = mult * (ig * xc)
        big_a, big_u = _scan_down(a, u)
        h = big_a * carry[SUBLANES - 1:SUBLANES, :] + big_u
        carry[...] = h[t - SUBLANES:]
        xc_ref[...] = xc
        h_ref[...] = h

    vec = pl.BlockSpec((1, w), lambda i: (0, 0))
    mat = pl.BlockSpec((w, w), lambda i: (0, 0))
    return pl.pallas_call(
        body, name="rglru_fwd", grid=(nb,),
        in_specs=[pl.BlockSpec((t, w), lambda i: (i, P0_RNN)),
                  pl.BlockSpec((SUBLANES, w), lambda i: (jnp.maximum(i * (t // SUBLANES) - 1, 0), P0_RNN)),
                  pl.BlockSpec((SUBLANES, w), lambda i: (0, 0)), vec, mat, vec, mat, vec, vec],
        out_specs=[pl.BlockSpec((t, w), lambda i: (i, 0)), pl.BlockSpec((t, w), lambda i: (i, 0))],
        out_shape=[jax.ShapeDtypeStruct((SEQ, w), F32), jax.ShapeDtypeStruct((SEQ, w), F32)],
        scratch_shapes=[pltpu.VMEM((SUBLANES, w), F32)],
        compiler_params=_params(("arbitrary",)),
    )(proj0, proj0, cw8, cb, wa, ba, wx, bx, lam)


def _rglru_bwd(dh, xc, h, proj0, cw8, wa, ba, wx, bx, lam, xb):
    t, w = RG_T, RNN_W
    nb = SEQ // t
    tb = t // SUBLANES

    def body(dh_ref, xc_ref, h_ref, hh_ref, x_ref, cw_ref, wa_ref, ba_ref, wx_ref, bx_ref, lam_ref, xb_ref,
             dx_ref, dwa_ref, dwx_ref, dvec_ref, gw_ref, gcarry, dxc_next):
        i = pl.program_id(0)
        rev = nb - 1 - i

        @pl.when(i == 0)
        def _():
            gcarry[...] = jnp.zeros_like(gcarry)
            dxc_next[...] = jnp.zeros_like(dxc_next)
            gw_ref[...] = jnp.zeros_like(gw_ref)
            dwa_ref[...] = jnp.zeros_like(dwa_ref)
            dwx_ref[...] = jnp.zeros_like(dwx_ref)
            dvec_ref[...] = jnp.zeros_like(dvec_ref)

        xc = xc_ref[...]
        wa_v, wx_v = wa_ref[...], wx_ref[...]
        lam_v = lam_ref[...]
        r, ig, sp, a, mult = _rg_gates(xc, wa_v, ba_ref[...], wx_v, bx_ref[...], lam_v)
        dhv = dh_ref[...]
        big_a, big_u = _scan_up(a, a * dhv)
        gg = big_a * gcarry[0:1, :] + big_u
        g = dhv + _shift_up(gg, gcarry[...], 1)
        gcarry[...] = gg[:SUBLANES]
        hhalo = jnp.where(rev > 0, hh_ref[...], 0.0)
        da = g * _shift_down(h_ref[...], hhalo, 1)
        d_mult = g * (ig * xc)
        d_i = g * (mult * xc)
        dxc = g * (mult * ig)
        d_la = da * a - d_mult * (a * a) / mult
        d_r = d_la * (-RG_C * sp)
        d_sp = jnp.sum(d_la * (-RG_C * r), axis=0, keepdims=True)
        d_pa = d_r * r * (1.0 - r)
        d_px = d_i * ig * (1.0 - ig)
        d_pab, d_pxb = _mx(d_pa), _mx(d_px)
        dxc = dxc + _dot_nt(d_pab, wa_v) + _dot_nt(d_pxb, wx_v)
        xcb = _mx(xc)
        dwa_ref[...] += _dot_tn(xcb, d_pab)
        dwx_ref[...] += _dot_tn(xcb, d_pxb)
        dvec_ref[0:1, :] += jnp.sum(d_pa, axis=0, keepdims=True)
        dvec_ref[1:2, :] += jnp.sum(d_px, axis=0, keepdims=True)
        dvec_ref[2:3, :] += d_sp * (-_sigmoid(-lam_v))
        dvec_ref[3:4, :] += jnp.sum(dxc, axis=0, keepdims=True)
        xblk = x_ref[...]
        cw = cw_ref[...]
        dx = dxc * cw[3:4]
        nxt = dxc_next[...]
        dvec_ref[7:8, :] += jnp.sum(dxc * xblk, axis=0, keepdims=True)
        for k in range(3):
            up = _shift_up(dxc, nxt, 3 - k)
            dvec_ref[4 + k:5 + k, :] += jnp.sum(up * xblk, axis=0, keepdims=True)
            dx = dx + up * cw[k:k + 1]
        dxc_next[...] = dxc[:SUBLANES]
        dxb = _mx(dx)
        dx_ref[...] = dxb
        gw_ref[...] += _dot_tn(xb_ref[...], dxb)

    blk = pl.BlockSpec((t, w), lambda i: (nb - 1 - i, 0))
    halo = pl.BlockSpec((SUBLANES, w), lambda i: (jnp.maximum((nb - 1 - i) * tb - 1, 0), 0))
    vec = pl.BlockSpec((1, w), lambda i: (0, 0))
    mat = pl.BlockSpec((w, w), lambda i: (0, 0))
    return pl.pallas_call(
        body, name="rglru_bwd", grid=(nb,),
        in_specs=[blk, blk, blk, halo, pl.BlockSpec((t, w), lambda i: (nb - 1 - i, P0_RNN)),
                  pl.BlockSpec((SUBLANES, w), lambda i: (0, 0)), mat, vec, mat, vec, vec,
                  pl.BlockSpec((t, D_MODEL), lambda i: (nb - 1 - i, 0))],
        out_specs=[blk, mat, mat, pl.BlockSpec((16, w), lambda i: (0, 0)), pl.BlockSpec((D_MODEL, w), lambda i: (0, 0))],
        out_shape=[jax.ShapeDtypeStruct((SEQ, w), MXU_DTYPE), jax.ShapeDtypeStruct((w, w), F32),
                   jax.ShapeDtypeStruct((w, w), F32), jax.ShapeDtypeStruct((16, w), F32),
                   jax.ShapeDtypeStruct((D_MODEL, w), F32)],
        scratch_shapes=[pltpu.VMEM((SUBLANES, w), F32), pltpu.VMEM((SUBLANES, w), F32)],
        compiler_params=_params(("arbitrary",)),
    )(dh, xc, h, h, proj0, cw8, wa, ba, wx, bx, lam, xb)


MLA_T = 512


def _rope(v, c, sa, sb):
    return v * c + pltpu.roll(v, LANES - 16, 1) * sa + pltpu.roll(v, 16, 1) * sb


def _rope_t(dv, c, sa, sb):
    return dv * c + pltpu.roll(dv * sa, 16, 1) + pltpu.roll(dv * sb, LANES - 16, 1)


def _rms(v, g, eps=1e-6):
    rs = lax.rsqrt(jnp.mean(v * v, axis=-1, keepdims=True) + eps)
    return v * rs * g, rs


def _mla_fwd(proj0, q_norm, kv_norm, wq, wkv, tc, tsa, tsb):
    t = MLA_T

    def body(cq_ref, ck_ref, qn_ref, kn_ref, wq_ref, wkv_ref, c_ref, sa_ref, sb_ref,
             oqn_ref, okn_ref, oq_ref, ok_ref, ov_ref):
        c, sa, sb = c_ref[...], sa_ref[...], sb_ref[...]
        ck = ck_ref[...]
        qn = _mx(_rms(cq_ref[...], qn_ref[...])[0])
        kn = _mx(_rms(ck[:, :LANES], kn_ref[...])[0])
        oqn_ref[...] = qn
        okn_ref[...] = kn
        krv = _rope(ck[:, LANES:], c, sa, sb)
        qraw = _dot(qn, wq_ref[...])
        kvraw = _dot(kn, wkv_ref[...])
        for hd in range(MLA_HEADS):
            sl = slice(hd * LANES, (hd + 1) * LANES)
            oq_ref[:, sl] = _mx(_rope(qraw[:, sl], c, sa, sb))
            ok_ref[:, sl] = _mx(kvraw[:, sl] + krv)
        ov_ref[...] = _mx(kvraw[:, 1024:])

    tab = pl.BlockSpec((t, LANES), lambda i: (i, 0))
    wide = pl.BlockSpec((t, 1024), lambda i: (i, 0))
    const = lambda shape: pl.BlockSpec(shape, lambda i: (0, 0))
    return pl.pallas_call(
        body, name="mla_fwd", grid=(SEQ // t,),
        in_specs=[pl.BlockSpec((t, 256), lambda i: (i, 6)), pl.BlockSpec((t, 256), lambda i: (i, 7)),
                  const((1, 256)), const((1, LANES)), const((256, 1024)), const((LANES, 1536)), tab, tab, tab],
        out_specs=[pl.BlockSpec((t, 256), lambda i: (i, 0)), tab, wide, wide, pl.BlockSpec((t, 512), lambda i: (i, 0))],
        out_shape=[jax.ShapeDtypeStruct((SEQ, 256), MXU_DTYPE), jax.ShapeDtypeStruct((SEQ, LANES), MXU_DTYPE),
                   jax.ShapeDtypeStruct((SEQ, 1024), MXU_DTYPE), jax.ShapeDtypeStruct((SEQ, 1024), MXU_DTYPE),
                   jax.ShapeDtypeStruct((SEQ, 512), MXU_DTYPE)],
        compiler_params=_params(("parallel",)),
    )(proj0, proj0, q_norm, kv_norm, wq, wkv, tc, tsa, tsb)


ATT_T = 1024


def _flash_fwd(q, k, v, bcast=()):
    t = ATT_T
    nb = SEQ // t

    steps = [(qi, ki) for qi in range(nb) for ki in range(qi + 1)]
    qi_tab = jnp.asarray([s[0] for s in steps], jnp.int32)
    ki_tab = jnp.asarray([s[1] for s in steps], jnp.int32)

    nx = len(bcast)

    def body(qi_ref, ki_ref, q_ref, k_ref, v_ref, *rest):
        x_refs, (o_ref, lse_ref), g_refs = rest[:nx], rest[nx:nx + 2], rest[nx + 2:2 * nx + 2]
        m_sc, acc_sc = rest[2 * nx + 2:2 * nx + 4]
        step = pl.program_id(1)
        qi, ki = qi_ref[step], ki_ref[step]
        if nx:
            copies = _peer_copies(x_refs, g_refs, rest[2 * nx + 4:], [])

            @pl.when((pl.program_id(0) == 0) & (step == 0))
            def _():
                for cp in copies:
                    cp.start()

        @pl.when(ki == 0)
        def _():
            m_sc[...] = jnp.full_like(m_sc, -jnp.inf)
            acc_sc[...] = jnp.zeros_like(acc_sc)

        def update(diagonal):
            vv = v_ref[...]
            lane_v = lax.broadcasted_iota(jnp.int32, vv.shape, 1)
            for hd in range(2):
                sl = slice(hd * LANES, (hd + 1) * LANES)
                st = _dot_nt(k_ref[:, sl], q_ref[:, sl])
                if diagonal:
                    st = jnp.where(lax.broadcasted_iota(jnp.int32, (t, t), 0)
                                   <= lax.broadcasted_iota(jnp.int32, (t, t), 1), st, -jnp.inf)
                m_prev = m_sc[hd:hd + 1, :]
                m_new = jnp.maximum(m_prev, jnp.max(st, axis=0, keepdims=True))
                pt = jnp.exp2((st - m_new) * ATT_C)
                m_sc[hd:hd + 1, :] = m_new
                vh = jnp.where((lane_v >= hd * 64) & (lane_v < (hd + 1) * 64), vv, jnp.ones_like(vv))
                acc_sc[hd] = acc_sc[hd] * jnp.exp2((m_prev - m_new) * ATT_C) + _dot_tn(vh, _mx(pt))

        @pl.when(ki < qi)
        def _():
            update(False)

        @pl.when(ki == qi)
        def _():
            update(True)
            a0, a1 = acc_sc[0], acc_sc[1]
            l0, l1 = a0[64:65, :], a1[0:1, :]
            first = lax.broadcasted_iota(jnp.int32, (LANES, t), 0) < 64
            o_ref[...] = jnp.where(first, a0 / l0, a1 / l1).T
            lse_ref[0, 0:1, :] = m_sc[0:1, :] * ATT_SCALE + jnp.log(l0)
            lse_ref[0, 1:2, :] = m_sc[1:2, :] * ATT_SCALE + jnp.log(l1)
            lse_ref[0, 2:SUBLANES, :] = jnp.zeros((SUBLANES - 2, t), F32)

        if nx:
            @pl.when((pl.program_id(0) == 3) & (step == len(steps) - 1))
            def _():
                for cp in copies:
                    cp.wait()

    grid_spec = pltpu.PrefetchScalarGridSpec(
        num_scalar_prefetch=2, grid=(4, len(steps)),
        in_specs=[pl.BlockSpec((t, 256), lambda p, s, qt, kt: (qt[s], p)),
                  pl.BlockSpec((t, 256), lambda p, s, qt, kt: (kt[s], p)),
                  pl.BlockSpec((t, LANES), lambda p, s, qt, kt: (kt[s], p))] + [ANY] * nx,
        out_specs=[pl.BlockSpec((t, LANES), lambda p, s, qt, kt: (qt[s], p)),
                   pl.BlockSpec((1, SUBLANES, t), lambda p, s, qt, kt: (p, 0, qt[s]))] + [ANY] * nx,
        scratch_shapes=[pltpu.VMEM((SUBLANES, t), F32), pltpu.VMEM((2, LANES, t), F32)]
        + (_exchange_sems(nx) if nx else []))
    res = pl.pallas_call(
        body, name="flash_fwd", grid_spec=grid_spec,
        out_shape=[jax.ShapeDtypeStruct((SEQ, 512), F32), jax.ShapeDtypeStruct((4, SUBLANES, SEQ), F32)]
        + _exchange_shapes([], bcast),
        compiler_params=_params(("arbitrary", "arbitrary")),
    )(qi_tab, ki_tab, q, k, v, *bcast)
    return res[0], res[1], res[2:]


def _flash_bwd(q, k, v, o, do, lse, scatter=()):
    t = ATT_T
    nb = SEQ // t

    steps = [(qi, ki) for ki in range(nb) for qi in range(ki, nb)]
    qi_tab = jnp.asarray([s[0] for s in steps], jnp.int32)
    ki_tab = jnp.asarray([s[1] for s in steps], jnp.int32)
    log2e = math.log2(math.e)

    sc_arrays, sc_ranges = _scatter_args(scatter)
    nx = len(sc_arrays)

    def body(qi_ref, ki_ref, q_ref, k_ref, v_ref, o_ref, do_ref, lse_ref, *rest):
        x_refs, (dq_ref, dk_ref, dv_ref), g_refs = rest[:nx], rest[nx:nx + 3], rest[nx + 3:2 * nx + 3]
        dkt_sc, dvt_sc = rest[2 * nx + 3:2 * nx + 5]
        step = pl.program_id(1)
        qi, ki = qi_ref[step], ki_ref[step]
        if nx:
            copies = _peer_copies(x_refs, g_refs, rest[2 * nx + 5:], sc_ranges)

            @pl.when((pl.program_id(0) == 0) & (step == 0))
            def _():
                for cp in copies:
                    cp.start()

        @pl.when(step == 0)
        def _():
            dq_ref[...] = jnp.zeros_like(dq_ref)

        @pl.when(qi == ki)
        def _():
            dkt_sc[...] = jnp.zeros_like(dkt_sc)
            dvt_sc[...] = jnp.zeros_like(dvt_sc)

        def update(diagonal):
            dov, ov, vv = do_ref[...], o_ref[...], v_ref[...]
            lse2 = (lse_ref[0] * log2e).T
            lane = lax.broadcasted_iota(jnp.int32, (t, LANES), 1)
            row_t = lax.broadcasted_iota(jnp.int32, (LANES, t), 0)
            prod = dov * ov
            do_b = _mx(dov)
            qrows = pl.ds(pl.multiple_of(qi * t, t), t)
            dvt_acc = jnp.zeros((LANES, t), F32)
            dkt_new, dq_new = [], []
            for hd in range(2):
                sl = slice(hd * LANES, (hd + 1) * LANES)
                mine = (lane >= hd * 64) & (lane < (hd + 1) * 64)
                qh, kh = q_ref[:, sl], k_ref[:, sl]
                p = jnp.exp2(_dot_nt(qh, kh) * ATT_C - lse2[:, hd:hd + 1])
                if diagonal:
                    p = jnp.where(lax.broadcasted_iota(jnp.int32, (t, t), 1)
                                  <= lax.broadcasted_iota(jnp.int32, (t, t), 0), p, 0.0)
                do_h = jnp.where(mine, dov, 0.0)
                delta = jnp.sum(jnp.where(mine, prod, 0.0), axis=1, keepdims=True)
                dp = _dot_nt(_mx(do_h), vv)
                ds = _mx(p * (dp - delta) * ATT_SCALE)
                dvt_acc = dvt_acc + jnp.where((row_t >= hd * 64) & (row_t < (hd + 1) * 64), _dot_tn(do_b, _mx(p)), 0.0)
                dkt_new.append(_dot_tn(qh, ds))
                dq_new.append(_dot(ds, kh))
            for hd in range(2):
                sl = slice(hd * LANES, (hd + 1) * LANES)
                dkt_sc[sl, :] += dkt_new[hd]
                dq_ref[qrows, sl] += dq_new[hd]
            dvt_sc[...] += dvt_acc

        @pl.when(qi > ki)
        def _():
            update(False)

        @pl.when(qi == ki)
        def _():
            update(True)

        @pl.when(qi == nb - 1)
        def _():
            dk_ref[...] = dkt_sc[...].T
            dv_ref[...] = dvt_sc[...].T

        if nx:
            @pl.when((pl.program_id(0) == 3) & (step == len(steps) - 1))
            def _():
                for cp in copies:
                    cp.wait()

    qmap = lambda p, s, qt, kt: (qt[s], p)
    kmap = lambda p, s, qt, kt: (kt[s], p)
    grid_spec = pltpu.PrefetchScalarGridSpec(
        num_scalar_prefetch=2, grid=(4, len(steps)),
        in_specs=[pl.BlockSpec((t, 256), qmap), pl.BlockSpec((t, 256), kmap), pl.BlockSpec((t, LANES), kmap),
                  pl.BlockSpec((t, LANES), qmap), pl.BlockSpec((t, LANES), qmap),
                  pl.BlockSpec((1, SUBLANES, t), lambda p, s, qt, kt: (p, 0, qt[s]))] + [ANY] * nx,
        out_specs=[pl.BlockSpec((SEQ, 256), lambda p, s, qt, kt: (0, p)), pl.BlockSpec((t, 256), kmap),
                   pl.BlockSpec((t, LANES), kmap)] + [ANY] * nx,
        scratch_shapes=[pltpu.VMEM((256, t), F32), pltpu.VMEM((LANES, t), F32)] + (_exchange_sems(nx) if nx else []))
    res = pl.pallas_call(
        body, name="flash_bwd", grid_spec=grid_spec,
        out_shape=[jax.ShapeDtypeStruct((SEQ, 1024), F32), jax.ShapeDtypeStruct((SEQ, 1024), F32),
                   jax.ShapeDtypeStruct((SEQ, 512), F32)] + _exchange_shapes(sc_arrays, []),
        compiler_params=_params(("arbitrary", "arbitrary")),
    )(qi_tab, ki_tab, q, k, v, o, do, lse, *sc_arrays)
    return res[0], res[1], res[2], res[3:]


def _rms_bwd(v, g, dy, eps=1e-6):
    rs = lax.rsqrt(jnp.mean(v * v, axis=-1, keepdims=True) + eps)
    xh = v * rs
    dxh = dy * g
    dv = rs * (dxh - xh * jnp.mean(dxh * xh, axis=-1, keepdims=True))
    return dv, jnp.sum(dy * xh, axis=0, keepdims=True)


def _mla_bwd(dq, dk, dv, proj0, qlat, klat, q_norm, kv_norm, wq, wkv, tc, tsa, tsb, xb):
    t = MLA_T

    def body(dq_ref, dk_ref, dv_ref, cq_ref, ck_ref, ql_ref, kl_ref, qn_ref, kn_ref, wq_ref, wkv_ref,
             c_ref, sa_ref, sb_ref, xb_ref, o_ref, gwq_ref, gwkv_ref, dgq_ref, dgk_ref, gwt_ref, oq_ref, okv_ref):
        @pl.when(pl.program_id(0) == 0)
        def _():
            dgq_ref[...] = jnp.zeros_like(dgq_ref)
            dgk_ref[...] = jnp.zeros_like(dgk_ref)
            gwq_ref[...] = jnp.zeros_like(gwq_ref)
            gwkv_ref[...] = jnp.zeros_like(gwkv_ref)
            gwt_ref[...] = jnp.zeros_like(gwt_ref)

        c, sa, sb = c_ref[...], sa_ref[...], sb_ref[...]
        lane = lax.broadcasted_iota(jnp.int32, (t, LANES), 1)
        dkr = jnp.zeros((t, LANES), F32)
        for hd in range(MLA_HEADS):
            sl = slice(hd * LANES, (hd + 1) * LANES)
            oq_ref[:, sl] = _mx(_rope_t(dq_ref[:, sl], c, sa, sb))
            dkh = dk_ref[:, sl]
            okv_ref[:, sl] = _mx(dkh)
            dkr = dkr + dkh
        okv_ref[:, 1024:] = _mx(dv_ref[...])
        dkr = _rope_t(jnp.where((lane >= 64) & (lane < 96), dkr, 0.0), c, sa, sb)
        dqraw, dkvraw = oq_ref[...], okv_ref[...]
        gwq_ref[...] += _dot_tn(ql_ref[...], dqraw)
        gwkv_ref[...] += _dot_tn(kl_ref[...], dkvraw)
        dqn = _dot_nt(dqraw, wq_ref[...])
        dkn = _dot_nt(dkvraw, wkv_ref[...])
        dcq, dgq = _rms_bwd(cq_ref[...], qn_ref[...], dqn)
        dck, dgk = _rms_bwd(ck_ref[:, :LANES], kn_ref[...], dkn)
        o_ref[:, :256] = _mx(dcq)
        o_ref[:, 256:384] = _mx(dck)
        o_ref[:, 384:] = _mx(dkr)
        gwt_ref[...] += _dot_tn(xb_ref[...], o_ref[...])
        dgq_ref[0:1, :] += dgq
        dgk_ref[0:1, :] += dgk

    tab = pl.BlockSpec((t, LANES), lambda i: (i, 0))
    wide = pl.BlockSpec((t, 1024), lambda i: (i, 0))
    const = lambda shape: pl.BlockSpec(shape, lambda i: (0, 0))
    return pl.pallas_call(
        body, name="mla_bwd", grid=(SEQ // t,),
        in_specs=[wide, wide, pl.BlockSpec((t, 512), lambda i: (i, 0)),
                  pl.BlockSpec((t, 256), lambda i: (i, 6)), pl.BlockSpec((t, 256), lambda i: (i, 7)),
                  pl.BlockSpec((t, 256), lambda i: (i, 0)), tab,
                  const((1, 256)), const((1, LANES)), const((256, 1024)), const((LANES, 1536)), tab, tab, tab, wide],
        out_specs=[pl.BlockSpec((t, 512), lambda i: (i, 0)), const((256, 1024)), const((LANES, 1536)),
                   const((SUBLANES, 256)), const((SUBLANES, LANES)), const((D_MODEL, 512))],
        out_shape=[jax.ShapeDtypeStruct((SEQ, 512), MXU_DTYPE), jax.ShapeDtypeStruct((256, 1024), F32),
                   jax.ShapeDtypeStruct((LANES, 1536), F32), jax.ShapeDtypeStruct((SUBLANES, 256), F32),
                   jax.ShapeDtypeStruct((SUBLANES, LANES), F32), jax.ShapeDtypeStruct((D_MODEL, 512), F32)],
        scratch_shapes=[pltpu.VMEM((t, 1024), MXU_DTYPE), pltpu.VMEM((t, 1536), MXU_DTYPE)],
        compiler_params=_params(("arbitrary",)),
    )(dq, dk, dv, proj0, proj0, qlat, klat, q_norm, kv_norm, wq, wkv, tc, tsa, tsb, xb)


LN_T = 512


def _ln(v, g, b, eps=1e-5):
    mu = jnp.mean(v, axis=-1, keepdims=True)
    xc = v - mu
    rs = lax.rsqrt(jnp.mean(xc * xc, axis=-1, keepdims=True) + eps)
    return xc * rs * g + b


def _ln_bwd(v, g, dy, eps=1e-5):
    mu = jnp.mean(v, axis=-1, keepdims=True)
    xc = v - mu
    rs = lax.rsqrt(jnp.mean(xc * xc, axis=-1, keepdims=True) + eps)
    xh = xc * rs
    dxh = dy * g
    dv = rs * (dxh - jnp.mean(dxh, axis=-1, keepdims=True) - xh * jnp.mean(dxh * xh, axis=-1, keepdims=True))
    return dv, jnp.sum(dy * xh, axis=0, keepdims=True), jnp.sum(dy, axis=0, keepdims=True)


def _l0_out(h, o, proj0, x, w_out, g, b):
    t = LN_T

    def body(h_ref, o_ref, ga_ref, gb_ref, x_ref, w_ref, g_ref, b_ref, y_ref, v_ref, x1_ref, x1b_ref):
        y = _mx(jnp.concatenate([h_ref[...] * _silu(ga_ref[...]), o_ref[...] * _silu(gb_ref[...])], axis=1))
        v = DN_ALPHA * x_ref[...] + _dot(y, w_ref[...])
        y_ref[...] = y
        v_ref[...] = v
        x1 = _ln(v, g_ref[...], b_ref[...])
        x1_ref[...] = x1
        x1b_ref[...] = _mx(x1)

    half = pl.BlockSpec((t, 512), lambda i: (i, 0))
    full = pl.BlockSpec((t, D_MODEL), lambda i: (i, 0))
    vec = pl.BlockSpec((1, D_MODEL), lambda i: (0, 0))
    return pl.pallas_call(
        body, name="l0_out", grid=(SEQ // t,),
        in_specs=[half, half, pl.BlockSpec((t, 512), lambda i: (i, 0)), pl.BlockSpec((t, 512), lambda i: (i, 1)), full,
                  pl.BlockSpec((D_MODEL, D_MODEL), lambda i: (0, 0)), vec, vec],
        out_specs=[full, full, full, full],
        out_shape=[jax.ShapeDtypeStruct((SEQ, D_MODEL), MXU_DTYPE), jax.ShapeDtypeStruct((SEQ, D_MODEL), F32),
                   jax.ShapeDtypeStruct((SEQ, D_MODEL), F32), jax.ShapeDtypeStruct((SEQ, D_MODEL), MXU_DTYPE)],
        compiler_params=_params(("parallel",)),
    )(h, o, proj0, proj0, x, w_out, g, b)


def _l1_in(x1b, w1z, w1d):
    t = 1024

    def body(x_ref, wz_ref, wd_ref, z_ref, dt_ref):
        xv = x_ref[...]
        z_ref[...] = _dot(xv, wz_ref[...])
        dt_ref[...] = _dot(xv, wd_ref[...])

    rows = lambda w: pl.BlockSpec((t, w), lambda i: (i, 0))
    const = lambda w: pl.BlockSpec((D_MODEL, w), lambda i: (0, 0))
    return pl.pallas_call(
        body, name="l1_in", grid=(SEQ // t,),
        in_specs=[rows(D_MODEL), const(SSD_INNER), const(LANES)],
        out_specs=[rows(SSD_INNER), rows(LANES)],
        out_shape=[jax.ShapeDtypeStruct((SEQ, SSD_INNER), F32), jax.ShapeDtypeStruct((SEQ, LANES), F32)],
        compiler_params=_params(("parallel",)),
    )(x1b, w1z, w1d)


def _l1_dx_ln(dz, dxbc, ddt, dv1, v0, w1z, w1x, w1d, g):
    t = LN_T

    def body(dz_ref, dx_ref, ddt_ref, dv1_ref, v_ref, wz_ref, wx_ref, wd_ref, g_ref, dv_ref, dgb_ref):
        @pl.when(pl.program_id(0) == 0)
        def _():
            dgb_ref[...] = jnp.zeros_like(dgb_ref)

        dy = (DN_ALPHA * dv1_ref[...] + _dot_nt(dz_ref[...], wz_ref[...]) + _dot_nt(dx_ref[...], wx_ref[...])
              + _dot_nt(_mx(ddt_ref[...]), wd_ref[...]))
        dv, dg, db = _ln_bwd(v_ref[...], g_ref[...], dy)
        dv_ref[...] = dv
        dgb_ref[0:1, :] += dg
        dgb_ref[1:2, :] += db

    rows = lambda w: pl.BlockSpec((t, w), lambda i: (i, 0))
    const = lambda w: pl.BlockSpec((D_MODEL, w), lambda i: (0, 0))
    return pl.pallas_call(
        body, name="l1_dx_ln", grid=(SEQ // t,),
        in_specs=[rows(SSD_INNER), rows(SSD_CONV), rows(LANES), rows(D_MODEL), rows(D_MODEL),
                  const(SSD_INNER), const(SSD_CONV), const(LANES), pl.BlockSpec((1, D_MODEL), lambda i: (0, 0))],
        out_specs=[rows(D_MODEL), pl.BlockSpec((SUBLANES, D_MODEL), lambda i: (0, 0))],
        out_shape=[jax.ShapeDtypeStruct((SEQ, D_MODEL), F32), jax.ShapeDtypeStruct((SUBLANES, D_MODEL), F32)],
        compiler_params=_params(("arbitrary",)),
    )(dz, dxbc, ddt, dv1, v0, w1z, w1x, w1d, g)


def _gate_bwd(dv0, w_out, h, o, proj0, y0, xb):
    t = LN_T

    def body(dv_ref, w_ref, h_ref, o_ref, ga_ref, gb_ref, y0_ref, xb_ref, dh_ref, do_ref, dg_ref, gwo_ref, gwg_ref):
        @pl.when(pl.program_id(0) == 0)
        def _():
            gwo_ref[...] = jnp.zeros_like(gwo_ref)
            gwg_ref[...] = jnp.zeros_like(gwg_ref)

        dvb = _mx(dv_ref[...])
        dy = _dot_nt(dvb, w_ref[...])
        ga, gb, dya, dyb = ga_ref[...], gb_ref[...], dy[:, :512], dy[:, 512:]
        dh_ref[...] = dya * _silu(ga)
        do_ref[...] = dyb * _silu(gb)
        dg_ref[:, :512] = _mx(dya * h_ref[...] * _dsilu(ga))
        dg_ref[:, 512:] = _mx(dyb * o_ref[...] * _dsilu(gb))
        gwo_ref[...] += _dot_tn(y0_ref[...], dvb)
        gwg_ref[...] += _dot_tn(xb_ref[...], dg_ref[...])

    half = pl.BlockSpec((t, 512), lambda i: (i, 0))
    half1 = pl.BlockSpec((t, 512), lambda i: (i, 1))
    full = pl.BlockSpec((t, 1024), lambda i: (i, 0))
    square = pl.BlockSpec((D_MODEL, D_MODEL), lambda i: (0, 0))
    return pl.pallas_call(
        body, name="gate_bwd", grid=(SEQ // t,),
        in_specs=[full, square, half, half, half, half1, full, full],
        out_specs=[half, half, full, square, square],
        out_shape=[jax.ShapeDtypeStruct((SEQ, 512), F32), jax.ShapeDtypeStruct((SEQ, 512), F32),
                   jax.ShapeDtypeStruct((SEQ, 1024), MXU_DTYPE), jax.ShapeDtypeStruct((D_MODEL, D_MODEL), F32),
                   jax.ShapeDtypeStruct((D_MODEL, D_MODEL), F32)],
        compiler_params=_params(("arbitrary",)),
    )(dv0, w_out, h, o, proj0, proj0, y0, xb)


CONV_T = 1024
CONV_CB = 1024


def _ssd_conv_fwd(x1b, w1x, cw8, cb):
    t, cbk = CONV_T, CONV_CB

    def body(x_ref, w_ref, cw_ref, cb_ref, xbc_ref, pre_ref, act_ref, carry):
        xbc = _dot(x_ref[...], w_ref[...])
        halo = jnp.where(pl.program_id(1) > 0, carry[...], 0.0)
        pre = _conv4(xbc, halo, cw_ref[...], cb_ref[...])
        carry[...] = xbc[t - SUBLANES:]
        xbc_ref[...] = xbc
        pre_ref[...] = pre
        act_ref[...] = _silu(pre)

    blk = pl.BlockSpec((t, cbk), lambda j, i: (i, j))
    out = jax.ShapeDtypeStruct((SEQ, SSD_CONV), F32)
    return pl.pallas_call(
        body, name="ssd_conv_fwd", grid=(SSD_CONV // cbk, SEQ // t),
        in_specs=[pl.BlockSpec((t, D_MODEL), lambda j, i: (i, 0)), pl.BlockSpec((D_MODEL, cbk), lambda j, i: (0, j)),
                  pl.BlockSpec((SUBLANES, cbk), lambda j, i: (0, j)), pl.BlockSpec((1, cbk), lambda j, i: (0, j))],
        out_specs=[blk, blk, blk], out_shape=[out, out, out],
        scratch_shapes=[pltpu.VMEM((SUBLANES, cbk), F32)],
        compiler_params=_params(("parallel", "arbitrary")),
    )(x1b, w1x, cw8, cb)


def _ssd_conv_bwd(dact, pre, xbc, cw8, x1b):
    t, cbk = CONV_T, CONV_CB
    tb = t // SUBLANES
    nb = SEQ // t

    def body(da_ref, dan_ref, pre_ref, pren_ref, x_ref, cw_ref, x1_ref, dx_ref, dcw_ref, gw_ref):
        i = pl.program_id(1)

        @pl.when(i == 0)
        def _():
            dcw_ref[...] = jnp.zeros_like(dcw_ref)
            gw_ref[...] = jnp.zeros_like(gw_ref)

        dpre = da_ref[...] * _dsilu(pre_ref[...])
        dpre_next = jnp.where(i < nb - 1, dan_ref[...] * _dsilu(pren_ref[...]), 0.0)
        xblk = x_ref[...]
        cw = cw_ref[...]
        dx = dpre * cw[3:4]
        dcw_ref[3:4, :] += jnp.sum(dpre * xblk, axis=0, keepdims=True)
        for k in range(3):
            up = _shift_up(dpre, dpre_next, 3 - k)
            dcw_ref[k:k + 1, :] += jnp.sum(up * xblk, axis=0, keepdims=True)
            dx = dx + up * cw[k:k + 1]
        dcw_ref[4:5, :] += jnp.sum(dpre, axis=0, keepdims=True)
        dxb = _mx(dx)
        dx_ref[...] = dxb
        gw_ref[...] += _dot_tn(x1_ref[...], dxb)

    blk = pl.BlockSpec((t, cbk), lambda j, i: (i, j))
    nxt = pl.BlockSpec((SUBLANES, cbk), lambda j, i: (jnp.minimum((i + 1) * tb, SEQ // SUBLANES - 1), j))
    acc = pl.BlockSpec((SUBLANES, cbk), lambda j, i: (0, j))
    return pl.pallas_call(
        body, name="ssd_conv_bwd", grid=(SSD_CONV // cbk, nb),
        in_specs=[blk, nxt, blk, nxt, blk, acc, pl.BlockSpec((t, D_MODEL), lambda j, i: (i, 0))],
        out_specs=[blk, acc, pl.BlockSpec((D_MODEL, cbk), lambda j, i: (0, j))],
        out_shape=[jax.ShapeDtypeStruct((SEQ, SSD_CONV), MXU_DTYPE), jax.ShapeDtypeStruct((SUBLANES, SSD_CONV), F32),
                   jax.ShapeDtypeStruct((D_MODEL, SSD_CONV), F32)],
        compiler_params=_params(("parallel", "arbitrary")),
    )(dact, dact, pre, pre, xbc, cw8, x1b)


def _ssd_common(dt_raw, bias, alog, tril, expand_t, xs):
    lane = lax.broadcasted_iota(jnp.int32, dt_raw.shape, 1)
    dt = jnp.where(lane < SSD_HEADS, _softplus(dt_raw + bias), 0.0)
    a_neg = -jnp.exp(alog)
    cs = _dot_hi(tril, dt * a_neg)
    dt_x = _expand_heads(dt, expand_t)
    ecs_x = _expand_heads(jnp.exp(cs), expand_t)
    ds_x = _expand_heads(jnp.exp(cs[SSD_L - 1:SSD_L, :] - cs), expand_t)
    return dt, a_neg, cs, dt_x, None, xs * dt_x, ds_x, ecs_x, ecs_x[SSD_L - 1:SSD_L, :]


def _expand_heads(v, expand_t):
    hi = v.astype(jnp.bfloat16)
    lo = (v - hi.astype(F32)).astype(jnp.bfloat16)
    return _dot_nt(hi, expand_t) + _dot_nt(lo, expand_t)


def _fold_heads(v, expand_t):
    hi = v.astype(jnp.bfloat16)
    lo = (v - hi.astype(F32)).astype(jnp.bfloat16)
    return _dot(hi, expand_t) + _dot(lo, expand_t)


def _ssd_decay(cs, cs_t, hh, causal):
    seg = cs[:, hh:hh + 1] - cs_t[hh:hh + 1, :]
    return jnp.where(causal, jnp.exp(jnp.where(causal, seg, 0.0)), 0.0)


def _ssd_scan_fwd(act, dt_raw, bias, alog, d_x, tril, expand_t, bcast=()):
    nc = SEQ // SSD_L
    gw = SSD_INNER // SSD_GROUPS
    n = len(bcast)

    def body(act_ref, dt_ref, bias_ref, alog_ref, dx_ref, tril_ref, et_ref, *rest):
        y_ref, hp_ref, h_sc = rest[n], rest[n + 1], rest[2 * n + 2]
        if n:
            copies = _peer_copies(rest[:n], rest[n + 2:2 * n + 2], rest[2 * n + 3:], [])

            @pl.when(pl.program_id(0) == 0)
            def _():
                for cp in copies:
                    cp.start()

            @pl.when(pl.program_id(0) == nc - 1)
            def _():
                for cp in copies:
                    cp.wait()

        @pl.when(pl.program_id(0) == 0)
        def _():
            h_sc[...] = jnp.zeros_like(h_sc)

        xs = act_ref[:, :SSD_INNER]
        _, _, cs, _, _, xdt, ds_x, ecs_x, elast = _ssd_common(
            dt_ref[...], bias_ref[...], alog_ref[...], tril_ref[...], et_ref[...], xs)
        cs_t = cs.T
        causal = (lax.broadcasted_iota(jnp.int32, (SSD_L, SSD_L), 0)
                  >= lax.broadcasted_iota(jnp.int32, (SSD_L, SSD_L), 1))
        lane = lax.broadcasted_iota(jnp.int32, (SSD_L, LANES), 1)
        xdt_b = _mx(xdt)
        xds_b = _mx(xdt * ds_x)
        hp_ref[0] = h_sc[...]
        for g in range(SSD_GROUPS):
            gs = slice(g * gw, (g + 1) * gw)
            bg = _mx(act_ref[:, SSD_INNER + g * SSD_N:SSD_INNER + (g + 1) * SSD_N])
            cg = _mx(act_ref[:, SSD_INNER + 512 + g * SSD_N:SSD_INNER + 512 + (g + 1) * SSD_N])
            cb = _dot_nt(cg, bg)
            hprev = h_sc[:, gs]
            yoff = _dot(cg, _mx(hprev)) * ecs_x[:, gs]
            h_sc[:, gs] = hprev * elast[:, gs] + _dot_tn(bg, xds_b[:, gs])
            for pr in range(4):
                ps = slice(g * gw + pr * LANES, g * gw + (pr + 1) * LANES)
                xp = xdt_b[:, ps]
                ydiag = jnp.zeros((SSD_L, LANES), F32)
                for j in range(2):
                    dm = _ssd_decay(cs, cs_t, g * 8 + pr * 2 + j, causal)
                    mine = (lane >= j * 64) & (lane < (j + 1) * 64)
                    ydiag = ydiag + _dot(_mx(cb * dm), jnp.where(mine, xp, jnp.zeros_like(xp)))
                y_ref[:, ps] = ydiag + yoff[:, pr * LANES:(pr + 1) * LANES] + dx_ref[:, ps] * xs[:, ps]

    const = lambda shape: pl.BlockSpec(shape, lambda c: (0, 0))
    res = pl.pallas_call(
        body, name="ssd_scan_fwd", grid=(nc,),
        in_specs=[pl.BlockSpec((SSD_L, SSD_CONV), lambda c: (c, 0)), pl.BlockSpec((SSD_L, LANES), lambda c: (c, 0)),
                  const((1, LANES)), const((1, LANES)), const((1, SSD_INNER)), const((SSD_L, SSD_L)),
                  const((SSD_INNER, LANES))] + [ANY] * n,
        out_specs=[pl.BlockSpec((SSD_L, SSD_INNER), lambda c: (c, 0)),
                   pl.BlockSpec((1, SSD_N, SSD_INNER), lambda c: (c, 0, 0))] + [ANY] * n,
        out_shape=[jax.ShapeDtypeStruct((SEQ, SSD_INNER), F32), jax.ShapeDtypeStruct((nc, SSD_N, SSD_INNER), F32)]
        + _exchange_shapes([], bcast),
        scratch_shapes=[pltpu.VMEM((SSD_N, SSD_INNER), F32)] + (_exchange_sems(n) if n else []),
        compiler_params=_params(("arbitrary",)),
    )(act, dt_raw, bias, alog, d_x, tril, expand_t, *bcast)
    return res[0], res[1], res[2:]


def _ssd_scan_bwd(dy, act, dt_raw, hprev_all, bias, alog, d_x, tril, expand_t, x1b, scatter=()):
    nc = SEQ // SSD_L
    gw = SSD_INNER // SSD_GROUPS
    sc_arrays, sc_ranges = _scatter_args(scatter)
    nx = len(sc_arrays)

    def body(dy_ref, act_ref, dt_ref, hp_ref, bias_ref, alog_ref, dx_ref, tril_ref, et_ref, x1_ref, *rest):
        dact_ref, ddt_ref, dvec_ref, gdt_ref = rest[nx:nx + 4]
        dh_sc, dd_sc, dcs_sc, dcst_sc = rest[2 * nx + 4:2 * nx + 8]
        i = pl.program_id(0)
        if nx:
            copies = _peer_copies(rest[:nx], rest[nx + 4:2 * nx + 4], rest[2 * nx + 8:], sc_ranges)

            @pl.when(i == 0)
            def _():
                for cp in copies:
                    cp.start()

        @pl.when(i == 0)
        def _():
            dh_sc[...] = jnp.zeros_like(dh_sc)
            dd_sc[...] = jnp.zeros_like(dd_sc)
            gdt_ref[...] = jnp.zeros_like(gdt_ref)
            dvec_ref[...] = jnp.zeros_like(dvec_ref)

        xs = act_ref[:, :SSD_INNER]
        dt_raw_v, bias_v = dt_ref[...], bias_ref[...]
        dt, a_neg, cs, dt_x, _, xdt, ds_x, ecs_x, elast = _ssd_common(
            dt_raw_v, bias_v, alog_ref[...], tril_ref[...], et_ref[...], xs)
        cs_t = cs.T
        rowi = lax.broadcasted_iota(jnp.int32, (SSD_L, SSD_L), 0)
        coli = lax.broadcasted_iota(jnp.int32, (SSD_L, SSD_L), 1)
        causal = rowi >= coli
        lane = lax.broadcasted_iota(jnp.int32, (SSD_L, LANES), 1)
        row_g = lax.broadcasted_iota(jnp.int32, (SSD_L, gw), 0)
        dyv = dy_ref[...]
        dd_sc[0:1, :] += jnp.sum(dyv * xs, axis=0, keepdims=True)
        xdt_b = _mx(xdt)
        xds = xdt * ds_x
        xds_b = _mx(xds)
        dy_b = _mx(dyv)
        dye_b = _mx(dyv * ecs_x)
        dcs_sc[...] = jnp.zeros_like(dcs_sc)
        dcst_sc[...] = jnp.zeros_like(dcst_sc)
        dcs_parts = []
        dxdt_parts = []
        for g in range(SSD_GROUPS):
            gs = slice(g * gw, (g + 1) * gw)
            bcol = slice(SSD_INNER + g * SSD_N, SSD_INNER + (g + 1) * SSD_N)
            ccol = slice(SSD_INNER + 512 + g * SSD_N, SSD_INNER + 512 + (g + 1) * SSD_N)
            bg, cg = _mx(act_ref[:, bcol]), _mx(act_ref[:, ccol])
            cb = _dot_nt(cg, bg)
            hp = hp_ref[0, :, gs]
            hp_b = _mx(hp)
            dh = dh_sc[:, gs]
            dh_b = _mx(dh)
            yoff = _dot(cg, hp_b) * ecs_x[:, gs]
            bdh = _dot(bg, dh_b)
            tt = xds[:, gs] * bdh
            last_row = (jnp.sum(tt, axis=0, keepdims=True)
                        + jnp.sum(dh * hp, axis=0, keepdims=True) * elast[:, gs])
            dcs_parts.append(dyv[:, gs] * yoff - tt + jnp.where(row_g == SSD_L - 1, last_row, 0.0))
            dc_g = _dot_nt(dye_b[:, gs], hp_b)
            db_g = _dot_nt(xds_b[:, gs], dh_b)
            dh_sc[:, gs] = _dot_tn(cg, dye_b[:, gs]) + dh * elast[:, gs]
            wsum = jnp.zeros((SSD_L, SSD_L), F32)
            dxdt_g = []
            for pr in range(4):
                ps = slice(g * gw + pr * LANES, g * gw + (pr + 1) * LANES)
                xp, dyp = xdt_b[:, ps], dy_b[:, ps]
                dxp = jnp.zeros((SSD_L, LANES), F32)
                for j in range(2):
                    hh = g * 8 + pr * 2 + j
                    dm = _ssd_decay(cs, cs_t, hh, causal)
                    mine = (lane >= j * 64) & (lane < (j + 1) * 64)
                    dy_h = jnp.where(mine, dyp, jnp.zeros_like(dyp))
                    wd = _dot_nt(dy_h, xp) * dm
                    wsum = wsum + wd
                    gmat = wd * cb
                    dcs_sc[:, hh:hh + 1] = jnp.sum(gmat, axis=1, keepdims=True)
                    dcst_sc[hh:hh + 1, :] = -jnp.sum(gmat, axis=0, keepdims=True)
                    dxp = dxp + _dot_tn(_mx(cb * dm), dy_h)
                dxdt_g.append(dxp)
            dxdt_parts.append(jnp.concatenate(dxdt_g, axis=1) + bdh * ds_x[:, gs])
            ws_b = _mx(wsum)
            dact_ref[:, ccol] = dc_g + _dot(ws_b, bg)
            dact_ref[:, bcol] = db_g + _dot_tn(ws_b, cg)
        dxdt = jnp.concatenate(dxdt_parts, axis=1)
        dcs_x = jnp.concatenate(dcs_parts, axis=1)
        et = et_ref[...]
        dcs_tot = dcs_sc[...] + dcst_sc[...].T + _fold_heads(dcs_x, et)
        da_dt = _dot_hi((coli >= rowi).astype(F32), dcs_tot)
        ddt = da_dt * a_neg + _fold_heads(dxdt * xs, et)
        ddt_raw = ddt * _sigmoid(dt_raw_v + bias_v)
        ddt_ref[...] = ddt_raw
        gdt_ref[...] += _dot_tn(x1_ref[...], _mx(ddt_raw))
        dvec_ref[0:1, :] += jnp.sum(ddt_raw, axis=0, keepdims=True)
        dvec_ref[1:2, :] += jnp.sum(da_dt * dt, axis=0, keepdims=True) * a_neg
        dact_ref[:, :SSD_INNER] = dyv * dx_ref[...] + dxdt * dt_x

        @pl.when(i == nc - 1)
        def _():
            dvec_ref[2:3, :] = _fold_heads(dd_sc[...], et)[0:1, :]
            if nx:
                for cp in copies:
                    cp.wait()

    const = lambda shape: pl.BlockSpec(shape, lambda c: (0, 0))
    rev = lambda c: (nc - 1 - c, 0)
    res = pl.pallas_call(
        body, name="ssd_scan_bwd", grid=(nc,),
        in_specs=[pl.BlockSpec((SSD_L, SSD_INNER), rev), pl.BlockSpec((SSD_L, SSD_CONV), rev),
                  pl.BlockSpec((SSD_L, LANES), rev),
                  pl.BlockSpec((1, SSD_N, SSD_INNER), lambda c: (nc - 1 - c, 0, 0)),
                  const((1, LANES)), const((1, LANES)), const((1, SSD_INNER)), const((SSD_L, SSD_L)),
                  const((SSD_INNER, LANES)), pl.BlockSpec((SSD_L, D_MODEL), rev)] + [ANY] * nx,
        out_specs=[pl.BlockSpec((SSD_L, SSD_CONV), rev), pl.BlockSpec((SSD_L, LANES), rev), const((SUBLANES, LANES)),
                   const((D_MODEL, LANES))] + [ANY] * nx,
        out_shape=[jax.ShapeDtypeStruct((SEQ, SSD_CONV), F32), jax.ShapeDtypeStruct((SEQ, LANES), F32),
                   jax.ShapeDtypeStruct((SUBLANES, LANES), F32), jax.ShapeDtypeStruct((D_MODEL, LANES), F32)]
        + _exchange_shapes(sc_arrays, []),
        scratch_shapes=[pltpu.VMEM((SSD_N, SSD_INNER), F32), pltpu.VMEM((SUBLANES, SSD_INNER), F32),
                        pltpu.VMEM((SSD_L, LANES), F32), pltpu.VMEM((LANES, SSD_L), F32)]
        + (_exchange_sems(nx) if nx else []),
        compiler_params=_params(("arbitrary",)),
    )(dy, act, dt_raw, hprev_all, bias, alog, d_x, tril, expand_t, x1b, *sc_arrays)
    return res[0], res[1], res[2], res[3], res[4:]


L1_T = 512


def _resident(shape):
    return pl.BlockSpec(shape, lambda i: (0, 0), pipeline_mode=pl.Buffered(1))


def _gated_norm(y, z, nw):
    y2 = y * _silu(z)
    gw = SSD_INNER // SSD_GROUPS
    outs, xhs, rss = [], [], []
    for g in range(SSD_GROUPS):
        gs = slice(g * gw, (g + 1) * gw)
        v = y2[:, gs]
        rs = lax.rsqrt(jnp.mean(v * v, axis=-1, keepdims=True) + 1e-6)
        xhs.append(v * rs)
        rss.append(rs)
        outs.append(v * rs * nw[:, gs])
    return outs, xhs, rss


def _l1_out(y, z, nw, w_out, x1, g, b, target):
    t = L1_T

    def body(y_ref, z_ref, nw_ref, w_ref, x1_ref, g_ref, b_ref, tg_ref, dv_ref, dgb_ref, loss_ref, gw_ref):
        @pl.when(pl.program_id(0) == 0)
        def _():
            dgb_ref[...] = jnp.zeros_like(dgb_ref)
            loss_ref[...] = jnp.zeros_like(loss_ref)
            gw_ref[...] = jnp.zeros_like(gw_ref)

        outs, _, _ = _gated_norm(y_ref[...], z_ref[...], nw_ref[...])
        yn = _mx(jnp.concatenate(outs, axis=1))
        v = DN_ALPHA * x1_ref[...] + _dot(yn, w_ref[...])
        gv = g_ref[...]
        err = _ln(v, gv, b_ref[...]) - tg_ref[...]
        rowsum = jnp.sum(err * err, axis=1, keepdims=True)
        loss_ref[...] += 0.5 * jnp.sum(rowsum, axis=0, keepdims=True) / D_MODEL
        dv, dg, db = _ln_bwd(v, gv, err / D_MODEL)
        dv_ref[...] = dv
        dgb_ref[0:1, :] += dg
        dgb_ref[1:2, :] += db
        gw_ref[...] += _dot_tn(yn, _mx(dv))

    wide = pl.BlockSpec((t, SSD_INNER), lambda i: (i, 0))
    full = pl.BlockSpec((t, D_MODEL), lambda i: (i, 0))
    vec = pl.BlockSpec((1, D_MODEL), lambda i: (0, 0))
    return pl.pallas_call(
        body, name="l1_out", grid=(SEQ // t,),
        in_specs=[wide, wide, pl.BlockSpec((1, SSD_INNER), lambda i: (0, 0)),
                  _resident((SSD_INNER, D_MODEL)), full, vec, vec, full],
        out_specs=[full, pl.BlockSpec((SUBLANES, D_MODEL), lambda i: (0, 0)),
                   pl.BlockSpec((SUBLANES, LANES), lambda i: (0, 0)), _resident((SSD_INNER, D_MODEL))],
        out_shape=[jax.ShapeDtypeStruct((SEQ, D_MODEL), F32), jax.ShapeDtypeStruct((SUBLANES, D_MODEL), F32),
                   jax.ShapeDtypeStruct((SUBLANES, LANES), F32), jax.ShapeDtypeStruct((SSD_INNER, D_MODEL), F32)],
        compiler_params=_params(("arbitrary",)),
    )(y, z, nw, w_out, x1, g, b, target)


def _l1_gate_bwd(dv1, w_out, y, z, nw, x1b):
    t = L1_T
    gw = SSD_INNER // SSD_GROUPS

    def body(dv_ref, w_ref, y_ref, z_ref, nw_ref, x1_ref, dy_ref, dz_ref, dnw_ref, gw_ref):
        @pl.when(pl.program_id(0) == 0)
        def _():
            dnw_ref[...] = jnp.zeros_like(dnw_ref)
            gw_ref[...] = jnp.zeros_like(gw_ref)

        dyn = _dot_nt(_mx(dv_ref[...]), w_ref[...])
        yv, zv, nwv = y_ref[...], z_ref[...], nw_ref[...]
        _, xhs, rss = _gated_norm(yv, zv, nwv)
        sz, dsz = _silu(zv), _dsilu(zv)
        for g in range(SSD_GROUPS):
            gs = slice(g * gw, (g + 1) * gw)
            d_out = dyn[:, gs]
            xh = xhs[g]
            dnw_ref[0:1, gs] += jnp.sum(d_out * xh, axis=0, keepdims=True)
            dxh = d_out * nwv[:, gs]
            dy2 = rss[g] * (dxh - xh * jnp.mean(dxh * xh, axis=-1, keepdims=True))
            dy_ref[:, gs] = dy2 * sz[:, gs]
            dz_ref[:, gs] = _mx(dy2 * yv[:, gs] * dsz[:, gs])
        gw_ref[...] += _dot_tn(x1_ref[...], dz_ref[...])

    wide = pl.BlockSpec((t, SSD_INNER), lambda i: (i, 0))
    return pl.pallas_call(
        body, name="l1_gate_bwd", grid=(SEQ // t,),
        in_specs=[pl.BlockSpec((t, D_MODEL), lambda i: (i, 0)), _resident((SSD_INNER, D_MODEL)),
                  wide, wide, pl.BlockSpec((1, SSD_INNER), lambda i: (0, 0)), pl.BlockSpec((t, D_MODEL), lambda i: (i, 0))],
        out_specs=[wide, wide, pl.BlockSpec((SUBLANES, SSD_INNER), lambda i: (0, 0)),
                   _resident((D_MODEL, SSD_INNER))],
        out_shape=[jax.ShapeDtypeStruct((SEQ, SSD_INNER), F32), jax.ShapeDtypeStruct((SEQ, SSD_INNER), MXU_DTYPE),
                   jax.ShapeDtypeStruct((SUBLANES, SSD_INNER), F32), jax.ShapeDtypeStruct((D_MODEL, SSD_INNER), F32)],
        compiler_params=_params(("arbitrary",)),
    )(dv1, w_out, y, z, nw, x1b)


MESH = pl.DeviceIdType.MESH
ANY = pl.BlockSpec(memory_space=pl.ANY)


def _flip(v, bit):
    return 1 - v if bit else v


def _all_gather(blocks, name):
    n = len(blocks)

    def body(*refs):
        x_refs, out_refs = refs[:n], refs[n:2 * n]
        send_sems, recv_sems, local_sems = refs[2 * n:]
        mx, my, mc = lax.axis_index("x"), lax.axis_index("y"), lax.axis_index("c")
        me, sibling = (mx, my, mc), (mx, my, 1 - mc)
        chips = [(1 - mx, my), (mx, 1 - my), (1 - mx, 1 - my)]

        def copy(a, k, block, to, own=False):
            px, py, pc = block
            slot = out_refs[a].at[4 * px + 2 * py + pc]
            return pltpu.make_async_remote_copy(
                src_ref=x_refs[a] if own else slot, dst_ref=slot,
                send_sem=send_sems.at[7 * a + k], recv_sem=recv_sems.at[7 * a + k], device_id=to, device_id_type=MESH)

        mine = [pltpu.make_async_copy(x_refs[a], out_refs[a].at[4 * mx + 2 * my + mc], local_sems.at[a])
                for a in range(n)]
        first = []
        for a in range(n):
            mine[a].start()
            first.append(copy(a, 0, me, sibling, own=True))
            first += [copy(a, 1 + j, me, (*chip, mc), own=True) for j, chip in enumerate(chips)]
        for cp in first:
            cp.start()
        passed = []
        for j, chip in enumerate(chips):
            for a in range(n):
                copy(a, 1 + j, (*chip, mc), me).wait_recv()
                fwd = copy(a, 4 + j, (*chip, mc), sibling)
                fwd.start()
                passed.append(fwd)
        for a in range(n):
            copy(a, 0, sibling, me).wait_recv()
            for j, chip in enumerate(chips):
                copy(a, 4 + j, (*chip, 1 - mc), me).wait_recv()
        for cp in first + passed:
            cp.wait_send()
        for cp in mine:
            cp.wait()

    return pl.pallas_call(
        body, name=name, in_specs=[ANY] * n, out_specs=[ANY] * n,
        out_shape=[jax.ShapeDtypeStruct((N_DEV,) + b.shape, b.dtype) for b in blocks],
        scratch_shapes=[pltpu.SemaphoreType.DMA((7 * n,)), pltpu.SemaphoreType.DMA((7 * n,)),
                        pltpu.SemaphoreType.DMA((n,))],
    )(*blocks)


def _l0_in(x, w0p, bcast=()):
    n = len(bcast)
    tm, tn = 1024, 1024
    gi, gj = SEQ // tm, 2048 // tn

    def body(x_ref, w_ref, *rest):
        o_ref, xb_ref = rest[n], rest[n + 1]
        i, j = pl.program_id(0), pl.program_id(1)
        if n:
            copies = _peer_copies(rest[:n], rest[n + 2:2 * n + 2], rest[2 * n + 2:], [])

            @pl.when((i == 0) & (j == 0))
            def _():
                for cp in copies:
                    cp.start()

        xb = _mx(x_ref[...])
        xb_ref[...] = xb
        o_ref[...] = _dot(xb, w_ref[...])

        if n:
            @pl.when((i == gi - 1) & (j == gj - 1))
            def _():
                for cp in copies:
                    cp.wait()

    res = pl.pallas_call(
        body, name="l0_in", grid=(gi, gj),
        in_specs=[pl.BlockSpec((tm, D_MODEL), lambda i, j: (i, 0)), pl.BlockSpec((D_MODEL, tn), lambda i, j: (0, j))]
        + [ANY] * n,
        out_specs=[pl.BlockSpec((tm, tn), lambda i, j: (i, j)), pl.BlockSpec((tm, D_MODEL), lambda i, j: (i, 0))]
        + [ANY] * n,
        out_shape=[jax.ShapeDtypeStruct((SEQ, 2048), F32), jax.ShapeDtypeStruct((SEQ, D_MODEL), MXU_DTYPE)]
        + _exchange_shapes([], bcast),
        scratch_shapes=_exchange_sems(n) if n else [],
        compiler_params=_params(("arbitrary", "arbitrary")),
    )(x, w0p, *bcast)
    return res[0], res[1], res[2:]


def _l0_dx(dxr, dgate, dtail, w0p, dv0, scatter=(), bcast=()):
    arrays, ranges = _scatter_args(scatter)
    n = len(arrays) + len(bcast)
    tm = 1024
    steps = SEQ // tm

    def body(dxr_ref, dg_ref, dt_ref, w_ref, dv_ref, *rest):
        o_ref = rest[n]
        i = pl.program_id(0)
        if n:
            copies = _peer_copies(rest[:n], rest[n + 1:2 * n + 1], rest[2 * n + 1:], ranges)

            @pl.when(i == 0)
            def _():
                for cp in copies:
                    cp.start()

        o_ref[...] = (DN_ALPHA * dv_ref[...] + _dot_nt(dg_ref[...], w_ref[:, 0:1024])
                      + _dot_nt(dxr_ref[...], w_ref[:, 1024:1536]) + _dot_nt(dt_ref[...], w_ref[:, 1536:2048]))

        if n:
            @pl.when(i == steps - 1)
            def _():
                for cp in copies:
                    cp.wait()

    rows = lambda w: pl.BlockSpec((tm, w), lambda i: (i, 0))
    res = pl.pallas_call(
        body, name="l0_dx", grid=(steps,),
        in_specs=[rows(512), rows(1024), rows(512), pl.BlockSpec((D_MODEL, 2048), lambda i: (0, 0)), rows(D_MODEL)]
        + [ANY] * n,
        out_specs=[rows(D_MODEL)] + [ANY] * n,
        out_shape=[jax.ShapeDtypeStruct((SEQ, D_MODEL), F32)] + _exchange_shapes(arrays, bcast),
        scratch_shapes=_exchange_sems(n) if n else [],
        compiler_params=_params(("arbitrary",)),
    )(dxr, dgate, dtail, w0p, dv0, *arrays, *bcast)
    return res[0], res[1:]


def _scatter_args(scatter):
    arrays = [s[0] if isinstance(s, tuple) else s for s in scatter]
    ranges = [s[1] if isinstance(s, tuple) else (0, N_DEV) for s in scatter]
    return arrays, ranges


def _exchange_shapes(scatter, bcast):
    return ([jax.ShapeDtypeStruct((N_DEV,) + a.shape[1:], a.dtype) for a in scatter]
            + [jax.ShapeDtypeStruct((N_DEV,) + a.shape, a.dtype) for a in bcast])


def _exchange_sems(n):
    return [pltpu.SemaphoreType.DMA((7 * n,)), pltpu.SemaphoreType.DMA((7 * n,)), pltpu.SemaphoreType.DMA((n,))]


class _GuardedCopy:
    def __init__(self, copy, send=None, recv=None, local=False):
        self.copy, self.send, self.recv, self.local = copy, send, recv, local

    @staticmethod
    def _run(pred, fn):
        if pred is None:
            fn()
        else:
            pl.when(pred)(fn)

    def start(self):
        self._run(self.send, self.copy.start)

    def wait(self):
        if self.local:
            self._run(self.send, self.copy.wait)
        else:
            self._run(self.send, self.copy.wait_send)
            self._run(self.recv, self.copy.wait_recv)


def _peer_copies(in_refs, out_refs, sems, ranges):
    send_sems, recv_sems, local_sems = sems
    n, ns = len(in_refs), len(ranges)
    mx, my, mc = lax.axis_index("x"), lax.axis_index("y"), lax.axis_index("c")
    me = 4 * mx + 2 * my + mc

    def src(a, slot):
        return in_refs[a].at[slot - ranges[a][0]] if a < ns else in_refs[a]

    def member(a, dev):
        if a >= ns or ranges[a] == (0, N_DEV):
            return None
        return (dev >= ranges[a][0]) & (dev < ranges[a][1])

    copies = [_GuardedCopy(pltpu.make_async_copy(src(a, me), out_refs[a].at[me], local_sems.at[a]),
                           send=member(a, me), local=True) for a in range(n)]
    for k in range(1, N_DEV):
        px, py, pc = _flip(mx, (k >> 2) & 1), _flip(my, (k >> 1) & 1), _flip(mc, k & 1)
        peer = 4 * px + 2 * py + pc
        for a in range(n):
            copies.append(_GuardedCopy(pltpu.make_async_remote_copy(
                src_ref=src(a, peer), dst_ref=out_refs[a].at[me],
                send_sem=send_sems.at[7 * a + k - 1], recv_sem=recv_sems.at[7 * a + k - 1],
                device_id=(px, py, pc), device_id_type=MESH), send=member(a, peer), recv=member(a, me)))
    return copies


def _segments(col_map, width):
    segs = []
    for lo, hi, arr, alo in col_map:
        for s in range(N_DEV):
            a, b = max(lo, s * width), min(hi, (s + 1) * width)
            if a < b:
                segs.append((s, a - s * width, b - a, arr, alo + a - lo))
    return segs


COPY_ROWS = 256


def _unshard(g8, col_map, widths, name):
    _, r, w = g8.shape
    rb = min(r, COPY_ROWS)
    segs = _segments(col_map, w)

    def body(g_ref, *o_refs):
        for o_ref in o_refs:
            o_ref[...] = jnp.zeros_like(o_ref)
        for s, llo, n, arr, alo in segs:
            o_refs[arr][:, alo:alo + n] = g_ref[s, :, llo:llo + n]

    return pl.pallas_call(
        body, name=name, grid=(r // rb,),
        in_specs=[pl.BlockSpec((N_DEV, rb, w), lambda i: (0, i, 0))],
        out_specs=[pl.BlockSpec((rb, n), lambda i: (i, 0)) for n in widths],
        out_shape=[jax.ShapeDtypeStruct((r, n), g8.dtype) for n in widths],
        compiler_params=_params(("parallel",)),
    )(g8)


def _reshard(srcs, col_map, w, dtype, name, shards=(0, N_DEV)):
    r = srcs[0].shape[0]
    rb = min(r, COPY_ROWS)
    lo, hi = shards
    segs = [sg for sg in _segments(col_map, w) if lo <= sg[0] < hi]

    def body(*refs):
        o_ref = refs[-1]
        for s, llo, n, arr, alo in segs:
            o_ref[s - lo, :, llo:llo + n] = refs[arr][:, alo:alo + n].astype(dtype)

    return pl.pallas_call(
        body, name=name, grid=(r // rb,),
        in_specs=[pl.BlockSpec((rb, a.shape[1]), lambda i: (i, 0)) for a in srcs],
        out_specs=pl.BlockSpec((hi - lo, rb, w), lambda i: (0, i, 0)),
        out_shape=jax.ShapeDtypeStruct((hi - lo, r, w), dtype),
        compiler_params=_params(("parallel",)),
    )(*srcs)


def _adamw(parts, w, m, v, name):
    r, c = w.shape
    tr = COPY_ROWS if r % COPY_ROWS == 0 else r

    def body(p_ref, w_ref, m_ref, v_ref, g_ref, d_ref, mo_ref, vo_ref):
        g = p_ref[0].astype(F32)
        for s in range(1, N_DEV):
            g = g + p_ref[s].astype(F32)
        g_ref[...] = g
        d_ref[...], mo_ref[...], vo_ref[...] = _adamw_math(g, w_ref[...], m_ref[...], v_ref[...])

    blk = pl.BlockSpec((tr, c), lambda i: (i, 0))
    out = jax.ShapeDtypeStruct((r, c), F32)
    return pl.pallas_call(
        body, name=name, grid=(r // tr,),
        in_specs=[pl.BlockSpec((N_DEV, tr, c), lambda i: (0, i, 0)), blk, blk, blk],
        out_specs=[blk, blk, blk, blk], out_shape=[out, out, out, out],
        compiler_params=_params(("parallel",)),
    )(parts, w, m, v)


def _adamw_cols(parts, wt, mt, vt, name):
    _, r, c = parts.shape
    per = r // LANES
    assert per == SUBLANES, "one column must fill exactly one (8,128) tile of the [.,128] views"
    n = min(c, LANES)
    starts = list(range(0, c - n + 1, LANES)) + ([c - n] if c % n else [])

    def body(p_ref, w_ref, m_ref, v_ref, g_ref, d_ref, mo_ref, vo_ref, gt_sc, pad_sc):
        for lo in starts:
            g = p_ref[0, :, lo:lo + n].astype(F32)
            for s in range(1, N_DEV):
                g = g + p_ref[s, :, lo:lo + n].astype(F32)
            if n < LANES:
                pad_sc[...] = jnp.zeros_like(pad_sc)
                pad_sc[:, 0:n] = g
                g = pad_sc[...]
            gt = g.T
            for k in range(per):
                gt_sc[pl.ds(lo * per + k, n, stride=per), :] = gt[0:n, k * LANES:(k + 1) * LANES]
        g = gt_sc[...]
        d, mn, vn = _adamw_math(g, w_ref[...], m_ref[...], v_ref[...])
        for ref, val in ((g_ref, g), (d_ref, d), (mo_ref, mn), (vo_ref, vn)):
            ref[...] = val.reshape(c, 1, r)

    out = jax.ShapeDtypeStruct((c, 1, r), F32)
    return pl.pallas_call(
        body, name=name, out_shape=[out, out, out, out],
        scratch_shapes=[pltpu.VMEM((c * per, LANES), F32), pltpu.VMEM((r, LANES), F32)],
        compiler_params=pltpu.CompilerParams(vmem_limit_bytes=VMEM_LIMIT),
    )(parts, wt, mt, vt)


def _adamw_math(g, w, m, v):
    mn = ADAM_B1 * m + (1.0 - ADAM_B1) * g
    vn = ADAM_B2 * v + (1.0 - ADAM_B2) * (g * g)
    m_hat = mn / (1.0 - ADAM_B1 ** ADAM_STEP)
    v_hat = vn / (1.0 - ADAM_B2 ** ADAM_STEP)
    return -ADAM_LR * (m_hat / (jnp.sqrt(v_hat) + ADAM_EPS) + ADAM_WD * w), mn, vn


SMALL = (("ab_conv_w", 0, 4, 64), ("ssd_conv_w", 4, 4, 384), ("ssd_conv_b", 8, 1, 384), ("ssd_norm", 9, 1, 256),
         ("ssd_ln_g", 10, 1, 128), ("ssd_ln_b", 11, 1, 128))
VECS = (("ab_conv_b", 512), ("ab_gate_a_b", 512), ("ab_gate_x_b", 512), ("ab_lambda", 512), ("mla_q_norm", 256),
        ("mla_kv_norm", 128), ("ab_ln_g", 1024), ("ab_ln_b", 1024), ("ssd_dt_bias", 32), ("ssd_a_log", 32),
        ("ssd_d", 32))
GATES = ("ab_gate_a_w", "ab_gate_x_w")
SMALL_NAMES = tuple(n for n, *_ in SMALL) + tuple(n for n, _ in VECS) + GATES
VMEM_WHOLE = pl.BlockSpec(memory_space=pltpu.VMEM)


def _view2d(name, a):
    if name in GATES:
        return a.reshape(RNN_W, 64)
    return a[0] if a.ndim == 3 else a


def _unshard_small(g):
    widths = (512, 3072, 3072, 2048, 1024, 1024)

    def body(*refs):
        ins, outs = refs[:6], refs[6:]
        outs[0][...] = jnp.zeros_like(outs[0])
        outs[1][...] = jnp.zeros_like(outs[1])
        for (_, _, nr, c), i_ref, o_ref in zip(SMALL, ins, outs):
            for j in range(N_DEV):
                o_ref[0:nr, j * c:(j + 1) * c] = i_ref[j]

    return pl.pallas_call(
        body, name="unshard_small", in_specs=[VMEM_WHOLE] * 6, out_specs=[VMEM_WHOLE] * 6,
        out_shape=[jax.ShapeDtypeStruct((SUBLANES if nr == 4 else 1, w), F32) for (_, _, nr, _), w in zip(SMALL, widths)],
    )(*g)


def _prep_repl(ga, gx, dt_bias, a_log, d):
    def body(ga_ref, gx_ref, b_ref, al_ref, d_ref, wa_ref, wx_ref, b128_ref, al128_ref, dx_ref):
        wa_ref[...] = jnp.zeros_like(wa_ref)
        wx_ref[...] = jnp.zeros_like(wx_ref)
        for hd in range(8):
            hs = slice(hd * 64, (hd + 1) * 64)
            wa_ref[hs, hs] = _mx(ga_ref[hs, :])
            wx_ref[hs, hs] = _mx(gx_ref[hs, :])
        b128_ref[...] = jnp.zeros_like(b128_ref)
        al128_ref[...] = jnp.zeros_like(al128_ref)
        b128_ref[:, 0:SSD_HEADS] = b_ref[...]
        al128_ref[:, 0:SSD_HEADS] = al_ref[...]
        dv = d_ref[...]
        for hd in range(SSD_HEADS):
            dx_ref[:, hd * SSD_P:(hd + 1) * SSD_P] = jnp.broadcast_to(dv[:, hd:hd + 1], (1, SSD_P))

    return pl.pallas_call(
        body, name="prep_repl", in_specs=[VMEM_WHOLE] * 5, out_specs=[VMEM_WHOLE] * 5,
        out_shape=[jax.ShapeDtypeStruct((RNN_W, RNN_W), MXU_DTYPE), jax.ShapeDtypeStruct((RNN_W, RNN_W), MXU_DTYPE),
                   jax.ShapeDtypeStruct((1, LANES), F32), jax.ShapeDtypeStruct((1, LANES), F32),
                   jax.ShapeDtypeStruct((1, SSD_INNER), F32)],
    )(ga, gx, dt_bias, a_log, d)


LOSS_ROW = 11


def _pack_small(dvec0, g_wa, g_wx, dqnw, dknw, dgb0, dvec1, dcw1, dnw, dgb1, loss8):
    def body(dvec0_ref, gwa_ref, gwx_ref, dqn_ref, dkn_ref, dgb0_ref, dvec1_ref, dcw1_ref, dnw_ref, dgb1_ref,
             loss_ref, sm_ref, vec_ref, gg_ref):
        sm_ref[...] = jnp.zeros_like(sm_ref)
        vec_ref[...] = jnp.zeros_like(vec_ref)
        sharded = ((dvec0_ref, 4), (dcw1_ref, 0), (dcw1_ref, 4), (dnw_ref, 0), (dgb1_ref, 0), (dgb1_ref, 1))
        for (_, r0, nr, c), (src, sr) in zip(SMALL, sharded):
            for j in range(N_DEV):
                sm_ref[j, r0:r0 + nr, 0:c] = src[sr:sr + nr, j * c:(j + 1) * c]
        vectors = ((dvec0_ref, 3), (dvec0_ref, 0), (dvec0_ref, 1), (dvec0_ref, 2), (dqn_ref, 0), (dkn_ref, 0),
                   (dgb0_ref, 0), (dgb0_ref, 1), (dvec1_ref, 0), (dvec1_ref, 1), (dvec1_ref, 2))
        for row, ((_, c), (src, sr)) in enumerate(zip(VECS, vectors)):
            vec_ref[row:row + 1, 0:c] = src[sr:sr + 1, 0:c]
        vec_ref[LOSS_ROW:LOSS_ROW + 1, 0:LANES] = loss_ref[0:1, :]
        for hd in range(8):
            hs = slice(hd * 64, (hd + 1) * 64)
            gg_ref[hs, 0:64] = _mx(gwa_ref[hs, hs])
            gg_ref[hs, 64:128] = _mx(gwx_ref[hs, hs])

    return pl.pallas_call(
        body, name="pack_small", in_specs=[VMEM_WHOLE] * 11, out_specs=[VMEM_WHOLE] * 3,
        out_shape=[jax.ShapeDtypeStruct((N_DEV, 16, 384), F32), jax.ShapeDtypeStruct((16, 1024), F32),
                   jax.ShapeDtypeStruct((RNN_W, LANES), MXU_DTYPE)],
    )(dvec0, g_wa, g_wx, dqnw, dknw, dgb0, dvec1, dcw1, dnw, dgb1, loss8)


def _adamw_small(recv_sm, recv_vec, recv_gg, wmv):
    plan = ([(0, r0, nr, c) for _, r0, nr, c in SMALL] + [(1, row, 1, c) for row, (_, c) in enumerate(VECS)]
            + [(2, 0, RNN_W, 0), (2, 0, RNN_W, 64)])
    n = len(plan)

    def body(*refs):
        recv, ins, outs = refs[:3], refs[3:3 + 3 * n], refs[3 + 3 * n:]
        for i, (src, r0, nr, c) in enumerate(plan):
            cols = slice(c, c + 64) if src == 2 else slice(0, c)
            g = recv[src][0, r0:r0 + nr, cols].astype(F32)
            for s in range(1, N_DEV):
                g = g + recv[src][s, r0:r0 + nr, cols].astype(F32)
            w_ref, m_ref, v_ref = ins[3 * i:3 * i + 3]
            outs[4 * i][...] = g
            outs[4 * i + 1][...], outs[4 * i + 2][...], outs[4 * i + 3][...] = _adamw_math(
                g, w_ref[...], m_ref[...], v_ref[...])
        loss = recv[1][0, LOSS_ROW:LOSS_ROW + 1, 0:LANES]
        for s in range(1, N_DEV):
            loss = loss + recv[1][s, LOSS_ROW:LOSS_ROW + 1, 0:LANES]
        outs[4 * n][...] = loss

    flat = [a for t in wmv for a in t]
    return pl.pallas_call(
        body, name="adamw_small", in_specs=[VMEM_WHOLE] * (3 + 3 * n), out_specs=[VMEM_WHOLE] * (4 * n + 1),
        out_shape=[jax.ShapeDtypeStruct(t[0].shape, F32) for t in wmv for _ in range(4)]
        + [jax.ShapeDtypeStruct((1, LANES), F32)],
    )(recv_sm, recv_vec, recv_gg, *flat)


BIG_L0 = ("ab_w_in", "ab_w_out", "mla_w_uq", "mla_w_ukv")
BIG_L1 = ("ssd_w_in", "ssd_w_out")
COLUMN_SHARDED = ("ab_w_in", "ssd_w_in")

MAP_W0 = ((0, 512, 0, 1024), (512, 1536, 0, 0), (1536, 1920, 0, 1536), (1920, 1952, 0, 1984))
MAP_W1 = ((0, 2048, 0, 0), (2048, 5120, 1, 0), (5120, 5152, 2, 0))
MAP_WQ = tuple((96 * hd, 96 * hd + 96, 0, 128 * hd) for hd in range(8))
MAP_WKV = (tuple((128 * hd, 128 * hd + 64, 0, 128 * hd) for hd in range(8))
           + tuple((128 * hd + 64, 128 * hd + 128, 0, 1024 + 64 * hd) for hd in range(8)))
MAP_G0 = ((0, 512, 0, 0), (512, 1536, 1, 0), (1536, 1920, 2, 0), (1920, 1952, 2, 448))
W0_EARLY, W0_LATE = (0, 6), (6, 8)


def kernel(x, positions, ab_w_in, ab_conv_w, ab_conv_b, ab_gate_a_w, ab_gate_a_b, ab_gate_x_w, ab_gate_x_b, ab_lambda, mla_q_norm, mla_kv_norm, mla_w_uq, mla_w_ukv, ab_w_out, ab_ln_g, ab_ln_b, ssd_w_in, ssd_conv_w, ssd_conv_b, ssd_dt_bias, ssd_a_log, ssd_d, ssd_norm, ssd_w_out, ssd_ln_g, ssd_ln_b, loss_target, m_ab_w_in, m_ab_conv_w, m_ab_conv_b, m_ab_gate_a_w, m_ab_gate_a_b, m_ab_gate_x_w, m_ab_gate_x_b, m_ab_lambda, m_mla_q_norm, m_mla_kv_norm, m_mla_w_uq, m_mla_w_ukv, m_ab_w_out, m_ab_ln_g, m_ab_ln_b, m_ssd_w_in, m_ssd_conv_w, m_ssd_conv_b, m_ssd_dt_bias, m_ssd_a_log, m_ssd_d, m_ssd_norm, m_ssd_w_out, m_ssd_ln_g, m_ssd_ln_b, v_ab_w_in, v_ab_conv_w, v_ab_conv_b, v_ab_gate_a_w, v_ab_gate_a_b, v_ab_gate_x_w, v_ab_gate_x_b, v_ab_lambda, v_mla_q_norm, v_mla_kv_norm, v_mla_w_uq, v_mla_w_ukv, v_ab_w_out, v_ab_ln_g, v_ab_ln_b, v_ssd_w_in, v_ssd_conv_w, v_ssd_conv_b, v_ssd_dt_bias, v_ssd_a_log, v_ssd_d, v_ssd_norm, v_ssd_w_out, v_ssd_ln_g, v_ssd_ln_b):
    args = dict(locals())
    bf = MXU_DTYPE
    big = {n: [args[pre + n][0] for pre in ("", "m_", "v_")] for n in BIG_L0 + BIG_L1}
    sml = {n: [_view2d(n, args[pre + n]) for pre in ("", "m_", "v_")] for n in SMALL_NAMES}

    w0_8, cw0_8 = _all_gather([big["ab_w_in"][0].astype(bf), sml["ab_conv_w"][0]], "gather_params")
    p = {"cw0_8": cw0_8, "l0_blocks": [big[n][0].astype(bf) for n in BIG_L0[1:]] + [sml[n][0] for n, *_ in SMALL[1:]]}
    p["w0p"], = _unshard(w0_8, MAP_W0, (2048,), "unshard_w0")
    p["wa"], p["wx"], p["dt_bias"], p["a_log"], p["d_x"] = _prep_repl(
        sml["ab_gate_a_w"][0], sml["ab_gate_x_w"][0], sml["ssd_dt_bias"][0], sml["ssd_a_log"][0], sml["ssd_d"][0])
    for key, n in (("cb0", "ab_conv_b"), ("ba", "ab_gate_a_b"), ("bx", "ab_gate_x_b"), ("lam", "ab_lambda"),
                   ("qn_w", "mla_q_norm"), ("kn_w", "mla_kv_norm"), ("g0", "ab_ln_g"), ("b0", "ab_ln_b")):
        p[key] = sml[n][0]

    _, recv_early, recv, _, grad_x = _local_step(
        x[0], positions[0], loss_target[0], p, [big[n][0].astype(bf) for n in BIG_L1])

    me = 4 * lax.axis_index("x") + 2 * lax.axis_index("y") + lax.axis_index("c")
    parts = dict(recv_early, ab_w_in=jnp.where(me >= W0_LATE[0], recv[0], recv_early["ab_w_in"]),
                 mla_w_uq=recv[1], mla_w_ukv=recv[2])

    outs = {}
    kinds = ("grad", "delta", "new_m", "new_v")
    for n in BIG_L0 + BIG_L1:
        if n in COLUMN_SHARDED:
            rows, cols = big[n][0].shape
            wmv_t = [jnp.transpose(args[pre + n], (2, 0, 1)).reshape(cols * rows // LANES, LANES) for pre in ("", "m_", "v_")]
            for kind, res in zip(kinds, _adamw_cols(parts[n], *wmv_t, "adamw_" + n)):
                outs[kind, n] = jnp.transpose(res, (1, 2, 0))
            continue
        for kind, res in zip(kinds, _adamw(parts[n], *big[n], "adamw_" + n)):
            outs[kind, n] = res[None]
    res = _adamw_small(*recv[3:], [sml[n] for n in SMALL_NAMES])
    for i, n in enumerate(SMALL_NAMES):
        for k, kind in enumerate(kinds):
            outs[kind, n] = res[4 * i + k].reshape(args[n].shape)

    loss = res[4 * len(SMALL_NAMES)][0, 0]
    order = ["ab_w_in", "ab_conv_w", "ab_conv_b", "ab_gate_a_w", "ab_gate_a_b", "ab_gate_x_w", "ab_gate_x_b",
             "ab_lambda", "mla_q_norm", "mla_kv_norm", "mla_w_uq", "mla_w_ukv", "ab_w_out", "ab_ln_g", "ab_ln_b",
             "ssd_w_in", "ssd_conv_w", "ssd_conv_b", "ssd_dt_bias", "ssd_a_log", "ssd_d", "ssd_norm", "ssd_w_out",
             "ssd_ln_g", "ssd_ln_b"]
    return (loss, grad_x[None], *[outs[kind, n] for kind in ("grad", "delta", "new_m", "new_v") for n in order])


def _local_step(x, pos, target, p, l1_blocks):
    bf = MXU_DTYPE
    inv_freq = 10000.0 ** (-jnp.arange(0, 32, 2, dtype=F32) / 32)
    ang = pos.astype(F32)[:, None] * inv_freq
    cos, sin = jnp.cos(ang), jnp.sin(ang)
    zeros = lambda n: jnp.zeros((SEQ, n), F32)
    tc = jnp.concatenate([jnp.ones((SEQ, 64), F32), cos, cos, zeros(32)], axis=1)
    tsa = jnp.concatenate([zeros(64), -sin, zeros(48)], axis=1)
    tsb = jnp.concatenate([zeros(80), sin, zeros(32)], axis=1)

    w0p, wa, wxg = (p[k] for k in ("w0p", "wa", "wx"))
    cb0, ba, bx, lam = (p[k] for k in ("cb0", "ba", "bx", "lam"))
    qn_w, kn_w, g0, b0 = (p[k] for k in ("qn_w", "kn_w", "g0", "b0"))
    dt_bias, a_log, d_x = (p[k] for k in ("dt_bias", "a_log", "d_x"))
    tril = jnp.tril(jnp.ones((SSD_L, SSD_L), F32))
    expand_t = (jnp.arange(SSD_INNER)[:, None] // SSD_P == jnp.arange(LANES)[None, :]).astype(jnp.bfloat16)

    proj0, xb, l0_8 = _l0_in(x, w0p, bcast=p["l0_blocks"])
    wo0 = l0_8[0].reshape(D_MODEL, D_MODEL)
    wq, = _unshard(l0_8[1], MAP_WQ, (1024,), "unshard_wq")
    wkv, = _unshard(l0_8[2], MAP_WKV, (1536,), "unshard_wkv")
    cw0, cw1, cb1, nw, g1, b1 = _unshard_small([p["cw0_8"]] + list(l0_8[3:]))
    xc, h = _rglru_fwd(proj0, cw0, cb0, wa, ba, wxg, bx, lam)
    qn, kn, qc, kc, vc = _mla_fwd(proj0, qn_w, kn_w, wq, wkv, tc, tsa, tsb)
    o, lse, (w1_8,) = _flash_fwd(qc, kc, vc, bcast=l1_blocks[:1])
    w1z, w1x, w1d = _unshard(w1_8, MAP_W1, (2048, 3072, 128), "unshard_w1")
    y0, v0, x1, x1b = _l0_out(h, o, proj0, x, wo0, g0, b0)

    z, dt_raw = _l1_in(x1b, w1z, w1d)
    xbc, pre, act = _ssd_conv_fwd(x1b, w1x, cw1, cb1)
    ys, hprev, (wo1_8,) = _ssd_scan_fwd(act, dt_raw, dt_bias, a_log, d_x, tril, expand_t, bcast=l1_blocks[1:])
    wo1 = wo1_8.reshape(SSD_INNER, D_MODEL)
    dv1, dgb1, loss8, g_wo1 = _l1_out(ys, z, nw, wo1, x1, g1, b1, target)

    dys, dz, dnw, g_z = _l1_gate_bwd(dv1, wo1, ys, z, nw, x1b)
    dact, ddt_raw, dvec1, g_dt, (recv_wo1,) = _ssd_scan_bwd(
        dys, act, dt_raw, hprev, dt_bias, a_log, d_x, tril, expand_t, x1b,
        scatter=[g_wo1.astype(bf).reshape(N_DEV, 256, D_MODEL)])
    dxbc, dcw1, g_xbc = _ssd_conv_bwd(dact, pre, xbc, cw1, x1b)

    dv0, dgb0 = _l1_dx_ln(dz, dxbc, ddt_raw, dv1, v0, w1z, w1x, w1d, g0)
    dh, do, dgate, g_wo0, g_gate = _gate_bwd(dv0, wo0, h, o, proj0, y0, xb)
    dxr, g_wa, g_wx, dvec0, g_rnn = _rglru_bwd(dh, xc, h, proj0, cw0, wa, ba, wxg, bx, lam, xb)
    early = [_reshard([g_z, g_xbc, g_dt], MAP_W1, 644, bf, "reshard_w1"), g_wo0.astype(bf).reshape(N_DEV, 128, D_MODEL),
             (_reshard([g_rnn, g_gate], MAP_G0, 244, bf, "reshard_w0_early", shards=W0_EARLY), W0_EARLY)]
    dq, dk, dvv, (recv_w1, recv_wo0, recv_w0) = _flash_bwd(qc, kc, vc, o, do, lse, scatter=early)
    recv_early = {"ssd_w_in": recv_w1, "ssd_w_out": recv_wo1, "ab_w_out": recv_wo0, "ab_w_in": recv_w0}
    dtail, g_wq, g_wkv, dqnw, dknw, g_tail = _mla_bwd(dq, dk, dvv, proj0, qn, kn, qn_w, kn_w, wq, wkv, tc, tsa, tsb, xb)

    acc = {"g_rnn": g_rnn, "g_gate": g_gate, "g_tail": g_tail, "g_wq": g_wq, "g_wkv": g_wkv,
           "dvec0": dvec0, "g_wa": g_wa, "g_wx": g_wx, "dqnw": dqnw, "dknw": dknw, "dgb0": dgb0, "dvec1": dvec1,
           "dcw1": dcw1, "dnw": dnw, "dgb1": dgb1}
    late = [(_reshard([g_rnn, g_gate, g_tail], MAP_G0, 244, bf, "reshard_w0_late", shards=W0_LATE), W0_LATE),
            _reshard([g_wq], MAP_WQ, 96, bf, "reshard_wq"), _reshard([g_wkv], MAP_WKV, 128, bf, "reshard_wkv")]
    sm_slots, vec_rows, gates = _pack_small(dvec0, g_wa, g_wx, dqnw, dknw, dgb0, dvec1, dcw1, dnw, dgb1, loss8)
    dx, recv_late = _l0_dx(dxr, dgate, dtail, w0p, dv0, scatter=late + [sm_slots], bcast=[vec_rows, gates])
    return acc, recv_early, recv_late, loss8[0, 0], dx
```

```python
import math

import jax
import jax.numpy as jnp
from jax import lax
from jax.experimental import pallas as pl
from jax.experimental.pallas import tpu as pltpu

F32 = jnp.float32
MXU_DTYPE = jnp.bfloat16

N_DEV = 8
SEQ = 4096
D_MODEL = 1024
DN_ALPHA = 4.0 ** 0.25
RNN_W = 512
MLA_HEADS = 8
ATT_SCALE = 96.0 ** -0.5
ATT_C = ATT_SCALE * math.log2(math.e)
RG_C = 8.0
SSD_INNER = 2048
SSD_HEADS = 32
SSD_P = 64
SSD_GROUPS = 4
SSD_N = 128
SSD_L = 128
SSD_CONV = 3072
LANES = 128
SUBLANES = 8
VMEM_LIMIT = 56 * 1024 * 1024

ADAM_LR, ADAM_B1, ADAM_B2, ADAM_EPS, ADAM_WD, ADAM_STEP = 0.001, 0.9, 0.999, 1e-08, 0.01, 10

HIGHEST = lax.Precision.HIGHEST


def _params(sem, limit=VMEM_LIMIT):
    return pltpu.CompilerParams(dimension_semantics=sem, vmem_limit_bytes=limit)


def _dot(a, b):
    return lax.dot_general(a, b, (((1,), (0,)), ((), ())), preferred_element_type=F32)


def _dot_nt(a, b):
    return lax.dot_general(a, b, (((1,), (1,)), ((), ())), preferred_element_type=F32)


def _dot_tn(a, b):
    return lax.dot_general(a, b, (((0,), (0,)), ((), ())), preferred_element_type=F32)


def _dot_hi(a, b):
    return lax.dot_general(a, b, (((1,), (0,)), ((), ())), precision=HIGHEST, preferred_element_type=F32)


def _mx(v):
    return v.astype(MXU_DTYPE)


def _sigmoid(v):
    return 1.0 / (1.0 + jnp.exp(-v))


def _log1p_pos(e):
    poly = e * (1.0 - e * (0.5 - e * (1.0 / 3.0 - e * 0.25)))
    return jnp.where(e < 0.01, poly, jnp.log(1.0 + e))


def _softplus(v):
    return jnp.maximum(v, 0.0) + _log1p_pos(jnp.exp(-jnp.abs(v)))


def _neg_expm1(v):
    poly = -v * (1.0 + v * (0.5 + v * (1.0 / 6.0 + v * (1.0 / 24.0 + v * (1.0 / 120.0)))))
    return jnp.where(jnp.abs(v) < 0.1, poly, 1.0 - jnp.exp(v))


def _silu(v):
    return v * _sigmoid(v)


def _dsilu(v):
    s = _sigmoid(v)
    return s * (1.0 + v * (1.0 - s))


def _shift_down(blk, halo, s):
    if s == 0:
        return blk
    t = blk.shape[0]
    r = pltpu.roll(blk, s, 0)
    hr = pltpu.roll(halo, s, 0)
    row8 = lax.broadcasted_iota(jnp.int32, hr.shape, 0)
    head = jnp.where(row8 < s, hr, r[:SUBLANES])
    return jnp.concatenate([head, r[SUBLANES:]], axis=0) if t > SUBLANES else head


def _shift_up(blk, halo, s):
    if s == 0:
        return blk
    t = blk.shape[0]
    r = pltpu.roll(blk, t - s, 0)
    hr = pltpu.roll(halo, SUBLANES - s, 0)
    row8 = lax.broadcasted_iota(jnp.int32, hr.shape, 0)
    tail = jnp.where(row8 >= SUBLANES - s, hr, r[t - SUBLANES:])
    return jnp.concatenate([r[:t - SUBLANES], tail], axis=0) if t > SUBLANES else tail


def _scan_down(a, u):
    t = a.shape[0]
    row = lax.broadcasted_iota(jnp.int32, a.shape, 0)
    d = 1
    while d < t:
        keep = row >= d
        a_sh = jnp.where(keep, pltpu.roll(a, d, 0), 1.0)
        u_sh = jnp.where(keep, pltpu.roll(u, d, 0), 0.0)
        u = a * u_sh + u
        a = a * a_sh
        d *= 2
    return a, u


def _scan_up(a, u):
    t = a.shape[0]
    row = lax.broadcasted_iota(jnp.int32, a.shape, 0)
    d = 1
    while d < t:
        keep = row < t - d
        a_sh = jnp.where(keep, pltpu.roll(a, t - d, 0), 1.0)
        u_sh = jnp.where(keep, pltpu.roll(u, t - d, 0), 0.0)
        u = a * u_sh + u
        a = a * a_sh
        d *= 2
    return a, u


def _conv4(blk, halo, cw, cb):
    out = cb + blk * cw[3:4]
    for k in range(3):
        out = out + _shift_down(blk, halo, 3 - k) * cw[k:k + 1]
    return out


RG_T = 512
P0_RNN = 2


def _rg_gates(xc, wa, ba, wx, bx, lam):
    xcb = _mx(xc)
    r = _sigmoid(_dot(xcb, wa) + ba)
    ig = _sigmoid(_dot(xcb, wx) + bx)
    sp = _softplus(-lam)
    la = (-RG_C * r) * sp
    a = jnp.exp(la)
    mult = jnp.sqrt(_neg_expm1(2.0 * la))
    return r, ig, sp, a, mult


def _rglru_fwd(proj0, cw8, cb, wa, ba, wx, bx, lam):
    t, w = RG_T, RNN_W
    nb = SEQ // t

    def body(x_ref, halo_ref, cw_ref, cb_ref, wa_ref, ba_ref, wx_ref, bx_ref, lam_ref, xc_ref, h_ref, carry):
        i = pl.program_id(0)

        @pl.when(i == 0)
        def _():
            carry[...] = jnp.zeros_like(carry)

        blk = x_ref[...]
        halo = jnp.where(i > 0, halo_ref[...], 0.0)
        xc = _conv4(blk, halo, cw_ref[...], cb_ref[...])
        _, ig, _, a, mult = _rg_gates(xc, wa_ref[...], ba_ref[...], wx_ref[...], bx_ref[...], lam_ref[...])
        u = mult * (ig * xc)
        big_a, big_u = _scan_down(a, u)
        h = big_a * carry[SUBLANES - 1:SUBLANES, :] + big_u
        carry[...] = h[t - SUBLANES:]
        xc_ref[...] = xc
        h_ref[...] = h

    vec = pl.BlockSpec((1, w), lambda i: (0, 0))
    mat = pl.BlockSpec((w, w), lambda i: (0, 0))
    return pl.pallas_call(
        body, name="rglru_fwd", grid=(nb,),
        in_specs=[pl.BlockSpec((t, w), lambda i: (i, P0_RNN)),
                  pl.BlockSpec((SUBLANES, w), lambda i: (jnp.maximum(i * (t // SUBLANES) - 1, 0), P0_RNN)),
                  pl.BlockSpec((SUBLANES, w), lambda i: (0, 0)), vec, mat, vec, mat, vec, vec],
        out_specs=[pl.BlockSpec((t, w), lambda i: (i, 0)), pl.BlockSpec((t, w), lambda i: (i, 0))],
        out_shape=[jax.ShapeDtypeStruct((SEQ, w), F32), jax.ShapeDtypeStruct((SEQ, w), F32)],
        scratch_shapes=[pltpu.VMEM((SUBLANES, w), F32)],
        compiler_params=_params(("arbitrary",)),
    )(proj0, proj0, cw8, cb, wa, ba, wx, bx, lam)


def _rglru_bwd(dh, xc, h, proj0, cw8, wa, ba, wx, bx, lam, xb):
    t, w = RG_T, RNN_W
    nb = SEQ // t
    tb = t // SUBLANES

    def body(dh_ref, xc_ref, h_ref, hh_ref, x_ref, cw_ref, wa_ref, ba_ref, wx_ref, bx_ref, lam_ref, xb_ref,
             dx_ref, dwa_ref, dwx_ref, dvec_ref, gw_ref, gcarry, dxc_next):
        i = pl.program_id(0)
        rev = nb - 1 - i

        @pl.when(i == 0)
        def _():
            gcarry[...] = jnp.zeros_like(gcarry)
            dxc_next[...] = jnp.zeros_like(dxc_next)
            gw_ref[...] = jnp.zeros_like(gw_ref)
            dwa_ref[...] = jnp.zeros_like(dwa_ref)
            dwx_ref[...] = jnp.zeros_like(dwx_ref)
            dvec_ref[...] = jnp.zeros_like(dvec_ref)

        xc = xc_ref[...]
        wa_v, wx_v = wa_ref[...], wx_ref[...]
        lam_v = lam_ref[...]
        r, ig, sp, a, mult = _rg_gates(xc, wa_v, ba_ref[...], wx_v, bx_ref[...], lam_v)
        dhv = dh_ref[...]
        big_a, big_u = _scan_up(a, a * dhv)
        gg = big_a * gcarry[0:1, :] + big_u
        g = dhv + _shift_up(gg, gcarry[...], 1)
        gcarry[...] = gg[:SUBLANES]
        hhalo = jnp.where(rev > 0, hh_ref[...], 0.0)
        da = g * _shift_down(h_ref[...], hhalo, 1)
        d_mult = g * (ig * xc)
        d_i = g * (mult * xc)
        dxc = g * (mult * ig)
        d_la = da * a - d_mult * (a * a) / mult
        d_r = d_la * (-RG_C * sp)
        d_sp = jnp.sum(d_la * (-RG_C * r), axis=0, keepdims=True)
        d_pa = d_r * r * (1.0 - r)
        d_px = d_i * ig * (1.0 - ig)
        d_pab, d_pxb = _mx(d_pa), _mx(d_px)
        dxc = dxc + _dot_nt(d_pab, wa_v) + _dot_nt(d_pxb, wx_v)
        xcb = _mx(xc)
        dwa_ref[...] += _dot_tn(xcb, d_pab)
        dwx_ref[...] += _dot_tn(xcb, d_pxb)
        dvec_ref[0:1, :] += jnp.sum(d_pa, axis=0, keepdims=True)
        dvec_ref[1:2, :] += jnp.sum(d_px, axis=0, keepdims=True)
        dvec_ref[2:3, :] += d_sp * (-_sigmoid(-lam_v))
        dvec_ref[3:4, :] += jnp.sum(dxc, axis=0, keepdims=True)
        xblk = x_ref[...]
        cw = cw_ref[...]
        dx = dxc * cw[3:4]
        nxt = dxc_next[...]
        dvec_ref[7:8, :] += jnp.sum(dxc * xblk, axis=0, keepdims=True)
        for k in range(3):
            up = _shift_up(dxc, nxt, 3 - k)
            dvec_ref[4 + k:5 + k, :] += jnp.sum(up * xblk, axis=0, keepdims=True)
            dx = dx + up * cw[k:k + 1]
        dxc_next[...] = dxc[:SUBLANES]
        dxb = _mx(dx)
        dx_ref[...] = dxb
        gw_ref[...] += _dot_tn(xb_ref[...], dxb)

    blk = pl.BlockSpec((t, w), lambda i: (nb - 1 - i, 0))
    halo = pl.BlockSpec((SUBLANES, w), lambda i: (jnp.maximum((nb - 1 - i) * tb - 1, 0), 0))
    vec = pl.BlockSpec((1, w), lambda i: (0, 0))
    mat = pl.BlockSpec((w, w), lambda i: (0, 0))
    return pl.pallas_call(
        body, name="rglru_bwd", grid=(nb,),
        in_specs=[blk, blk, blk, halo, pl.BlockSpec((t, w), lambda i: (nb - 1 - i, P0_RNN)),
                  pl.BlockSpec((SUBLANES, w), lambda i: (0, 0)), mat, vec, mat, vec, vec,
                  pl.BlockSpec((t, D_MODEL), lambda i: (nb - 1 - i, 0))],
        out_specs=[blk, mat, mat, pl.BlockSpec((16, w), lambda i: (0, 0)), pl.BlockSpec((D_MODEL, w), lambda i: (0, 0))],
        out_shape=[jax.ShapeDtypeStruct((SEQ, w), MXU_DTYPE), jax.ShapeDtypeStruct((w, w), F32),
                   jax.ShapeDtypeStruct((w, w), F32), jax.ShapeDtypeStruct((16, w), F32),
                   jax.ShapeDtypeStruct((D_MODEL, w), F32)],
        scratch_shapes=[pltpu.VMEM((SUBLANES, w), F32), pltpu.VMEM((SUBLANES, w), F32)],
        compiler_params=_params(("arbitrary",)),
    )(dh, xc, h, h, proj0, cw8, wa, ba, wx, bx, lam, xb)


MLA_T = 512


def _rope(v, c, sa, sb):
    return v * c + pltpu.roll(v, LANES - 16, 1) * sa + pltpu.roll(v, 16, 1) * sb


def _rope_t(dv, c, sa, sb):
    return dv * c + pltpu.roll(dv * sa, 16, 1) + pltpu.roll(dv * sb, LANES - 16, 1)


def _rms(v, g, eps=1e-6):
    rs = lax.rsqrt(jnp.mean(v * v, axis=-1, keepdims=True) + eps)
    return v * rs * g, rs


def _mla_fwd(proj0, q_norm, kv_norm, wq, wkv, tc, tsa, tsb):
    t = MLA_T

    def body(cq_ref, ck_ref, qn_ref, kn_ref, wq_ref, wkv_ref, c_ref, sa_ref, sb_ref,
             oqn_ref, okn_ref, oq_ref, ok_ref, ov_ref):
        c, sa, sb = c_ref[...], sa_ref[...], sb_ref[...]
        ck = ck_ref[...]
        qn = _mx(_rms(cq_ref[...], qn_ref[...])[0])
        kn = _mx(_rms(ck[:, :LANES], kn_ref[...])[0])
        oqn_ref[...] = qn
        okn_ref[...] = kn
        krv = _rope(ck[:, LANES:], c, sa, sb)
        qraw = _dot(qn, wq_ref[...])
        kvraw = _dot(kn, wkv_ref[...])
        for hd in range(MLA_HEADS):
            sl = slice(hd * LANES, (hd + 1) * LANES)
            oq_ref[:, sl] = _mx(_rope(qraw[:, sl], c, sa, sb))
            ok_ref[:, sl] = _mx(kvraw[:, sl] + krv)
        ov_ref[...] = _mx(kvraw[:, 1024:])

    tab = pl.BlockSpec((t, LANES), lambda i: (i, 0))
    wide = pl.BlockSpec((t, 1024), lambda i: (i, 0))
    const = lambda shape: pl.BlockSpec(shape, lambda i: (0, 0))
    return pl.pallas_call(
        body, name="mla_fwd", grid=(SEQ // t,),
        in_specs=[pl.BlockSpec((t, 256), lambda i: (i, 6)), pl.BlockSpec((t, 256), lambda i: (i, 7)),
                  const((1, 256)), const((1, LANES)), const((256, 1024)), const((LANES, 1536)), tab, tab, tab],
        out_specs=[pl.BlockSpec((t, 256), lambda i: (i, 0)), tab, wide, wide, pl.BlockSpec((t, 512), lambda i: (i, 0))],
        out_shape=[jax.ShapeDtypeStruct((SEQ, 256), MXU_DTYPE), jax.ShapeDtypeStruct((SEQ, LANES), MXU_DTYPE),
                   jax.ShapeDtypeStruct((SEQ, 1024), MXU_DTYPE), jax.ShapeDtypeStruct((SEQ, 1024), MXU_DTYPE),
                   jax.ShapeDtypeStruct((SEQ, 512), MXU_DTYPE)],
        compiler_params=_params(("parallel",)),
    )(proj0, proj0, q_norm, kv_norm, wq, wkv, tc, tsa, tsb)


ATT_T = 1024


def _flash_fwd(q, k, v, bcast=()):
    t = ATT_T
    nb = SEQ // t

    steps = [(qi, ki) for qi in range(nb) for ki in range(qi + 1)]
    qi_tab = jnp.asarray([s[0] for s in steps], jnp.int32)
    ki_tab = jnp.asarray([s[1] for s in steps], jnp.int32)

    nx = len(bcast)

    def body(qi_ref, ki_ref, q_ref, k_ref, v_ref, *rest):
        x_refs, (o_ref, lse_ref), g_refs = rest[:nx], rest[nx:nx + 2], rest[nx + 2:2 * nx + 2]
        m_sc, acc_sc = rest[2 * nx + 2:2 * nx + 4]
        step = pl.program_id(1)
        qi, ki = qi_ref[step], ki_ref[step]
        if nx:
            copies = _peer_copies(x_refs, g_refs, rest[2 * nx + 4:], [])

            @pl.when((pl.program_id(0) == 0) & (step == 0))
            def _():
                for cp in copies:
                    cp.start()

        @pl.when(ki == 0)
        def _():
            m_sc[...] = jnp.full_like(m_sc, -jnp.inf)
            acc_sc[...] = jnp.zeros_like(acc_sc)

        def update(diagonal):
            vv = v_ref[...]
            lane_v = lax.broadcasted_iota(jnp.int32, vv.shape, 1)
            for hd in range(2):
                sl = slice(hd * LANES, (hd + 1) * LANES)
                st = _dot_nt(k_ref[:, sl], q_ref[:, sl])
                if diagonal:
                    st = jnp.where(lax.broadcasted_iota(jnp.int32, (t, t), 0)
                                   <= lax.broadcasted_iota(jnp.int32, (t, t), 1), st, -jnp.inf)
                m_prev = m_sc[hd:hd + 1, :]
                m_new = jnp.maximum(m_prev, jnp.max(st, axis=0, keepdims=True))
                pt = jnp.exp2((st - m_new) * ATT_C)
                m_sc[hd:hd + 1, :] = m_new
                vh = jnp.where((lane_v >= hd * 64) & (lane_v < (hd + 1) * 64), vv, jnp.ones_like(vv))
                acc_sc[hd] = acc_sc[hd] * jnp.exp2((m_prev - m_new) * ATT_C) + _dot_tn(vh, _mx(pt))

        @pl.when(ki < qi)
        def _():
            update(False)

        @pl.when(ki == qi)
        def _():
            update(True)
            a0, a1 = acc_sc[0], acc_sc[1]
            l0, l1 = a0[64:65, :], a1[0:1, :]
            first = lax.broadcasted_iota(jnp.int32, (LANES, t), 0) < 64
            o_ref[...] = jnp.where(first, a0 / l0, a1 / l1).T
            lse_ref[0, 0:1, :] = m_sc[0:1, :] * ATT_SCALE + jnp.log(l0)
            lse_ref[0, 1:2, :] = m_sc[1:2, :] * ATT_SCALE + jnp.log(l1)
            lse_ref[0, 2:SUBLANES, :] = jnp.zeros((SUBLANES - 2, t), F32)

        if nx:
            @pl.when((pl.program_id(0) == 3) & (step == len(steps) - 1))
            def _():
                for cp in copies:
                    cp.wait()

    grid_spec = pltpu.PrefetchScalarGridSpec(
        num_scalar_prefetch=2, grid=(4, len(steps)),
        in_specs=[pl.BlockSpec((t, 256), lambda p, s, qt, kt: (qt[s], p)),
                  pl.BlockSpec((t, 256), lambda p, s, qt, kt: (kt[s], p)),
                  pl.BlockSpec((t, LANES), lambda p, s, qt, kt: (kt[s], p))] + [ANY] * nx,
        out_specs=[pl.BlockSpec((t, LANES), lambda p, s, qt, kt: (qt[s], p)),
                   pl.BlockSpec((1, SUBLANES, t), lambda p, s, qt, kt: (p, 0, qt[s]))] + [ANY] * nx,
        scratch_shapes=[pltpu.VMEM((SUBLANES, t), F32), pltpu.VMEM((2, LANES, t), F32)]
        + (_exchange_sems(nx) if nx else []))
    res = pl.pallas_call(
        body, name="flash_fwd", grid_spec=grid_spec,
        out_shape=[jax.ShapeDtypeStruct((SEQ, 512), F32), jax.ShapeDtypeStruct((4, SUBLANES, SEQ), F32)]
        + _exchange_shapes([], bcast),
        compiler_params=_params(("arbitrary", "arbitrary")),
    )(qi_tab, ki_tab, q, k, v, *bcast)
    return res[0], res[1], res[2:]


def _flash_bwd(q, k, v, o, do, lse, scatter=()):
    t = ATT_T
    nb = SEQ // t

    steps = [(qi, ki) for ki in range(nb) for qi in range(ki, nb)]
    qi_tab = jnp.asarray([s[0] for s in steps], jnp.int32)
    ki_tab = jnp.asarray([s[1] for s in steps], jnp.int32)
    log2e = math.log2(math.e)

    sc_arrays, sc_ranges = _scatter_args(scatter)
    nx = len(sc_arrays)

    def body(qi_ref, ki_ref, q_ref, k_ref, v_ref, o_ref, do_ref, lse_ref, *rest):
        x_refs, (dq_ref, dk_ref, dv_ref), g_refs = rest[:nx], rest[nx:nx + 3], rest[nx + 3:2 * nx + 3]
        dkt_sc, dvt_sc = rest[2 * nx + 3:2 * nx + 5]
        step = pl.program_id(1)
        qi, ki = qi_ref[step], ki_ref[step]
        if nx:
            copies = _peer_copies(x_refs, g_refs, rest[2 * nx + 5:], sc_ranges)

            @pl.when((pl.program_id(0) == 0) & (step == 0))
            def _():
                for cp in copies:
                    cp.start()

        @pl.when(step == 0)
        def _():
            dq_ref[...] = jnp.zeros_like(dq_ref)

        @pl.when(qi == ki)
        def _():
            dkt_sc[...] = jnp.zeros_like(dkt_sc)
            dvt_sc[...] = jnp.zeros_like(dvt_sc)

        def update(diagonal):
            dov, ov, vv = do_ref[...], o_ref[...], v_ref[...]
            lse2 = (lse_ref[0] * log2e).T
            lane = lax.broadcasted_iota(jnp.int32, (t, LANES), 1)
            row_t = lax.broadcasted_iota(jnp.int32, (LANES, t), 0)
            prod = dov * ov
            do_b = _mx(dov)
            qrows = pl.ds(pl.multiple_of(qi * t, t), t)
            dvt_acc = jnp.zeros((LANES, t), F32)
            dkt_new, dq_new = [], []
            for hd in range(2):
                sl = slice(hd * LANES, (hd + 1) * LANES)
                mine = (lane >= hd * 64) & (lane < (hd + 1) * 64)
                qh, kh = q_ref[:, sl], k_ref[:, sl]
                p = jnp.exp2(_dot_nt(qh, kh) * ATT_C - lse2[:, hd:hd + 1])
                if diagonal:
                    p = jnp.where(lax.broadcasted_iota(jnp.int32, (t, t), 1)
                                  <= lax.broadcasted_iota(jnp.int32, (t, t), 0), p, 0.0)
                do_h = jnp.where(mine, dov, 0.0)
                delta = jnp.sum(jnp.where(mine, prod, 0.0), axis=1, keepdims=True)
                dp = _dot_nt(_mx(do_h), vv)
                ds = _mx(p * (dp - delta) * ATT_SCALE)
                dvt_acc = dvt_acc + jnp.where((row_t >= hd * 64) & (row_t < (hd + 1) * 64), _dot_tn(do_b, _mx(p)), 0.0)
                dkt_new.append(_dot_tn(qh, ds))
                dq_new.append(_dot(ds, kh))
            for hd in range(2):
                sl = slice(hd * LANES, (hd + 1) * LANES)
                dkt_sc[sl, :] += dkt_new[hd]
                dq_ref[qrows, sl] += dq_new[hd]
            dvt_sc[...] += dvt_acc

        @pl.when(qi > ki)
        def _():
            update(False)

        @pl.when(qi == ki)
        def _():
            update(True)

        @pl.when(qi == nb - 1)
        def _():
            dk_ref[...] = dkt_sc[...].T
            dv_ref[...] = dvt_sc[...].T

        if nx:
            @pl.when((pl.program_id(0) == 3) & (step == len(steps) - 1))
            def _():
                for cp in copies:
                    cp.wait()

    qmap = lambda p, s, qt, kt: (qt[s], p)
    kmap = lambda p, s, qt, kt: (kt[s], p)
    grid_spec = pltpu.PrefetchScalarGridSpec(
        num_scalar_prefetch=2, grid=(4, len(steps)),
        in_specs=[pl.BlockSpec((t, 256), qmap), pl.BlockSpec((t, 256), kmap), pl.BlockSpec((t, LANES), kmap),
                  pl.BlockSpec((t, LANES), qmap), pl.BlockSpec((t, LANES), qmap),
                  pl.BlockSpec((1, SUBLANES, t), lambda p, s, qt, kt: (p, 0, qt[s]))] + [ANY] * nx,
        out_specs=[pl.BlockSpec((SEQ, 256), lambda p, s, qt, kt: (0, p)), pl.BlockSpec((t, 256), kmap),
                   pl.BlockSpec((t, LANES), kmap)] + [ANY] * nx,
        scratch_shapes=[pltpu.VMEM((256, t), F32), pltpu.VMEM((LANES, t), F32)] + (_exchange_sems(nx) if nx else []))
    res = pl.pallas_call(
        body, name="flash_bwd", grid_spec=grid_spec,
        out_shape=[jax.ShapeDtypeStruct((SEQ, 1024), F32), jax.ShapeDtypeStruct((SEQ, 1024), F32),
                   jax.ShapeDtypeStruct((SEQ, 512), F32)] + _exchange_shapes(sc_arrays, []),
        compiler_params=_params(("arbitrary", "arbitrary")),
    )(qi_tab, ki_tab, q, k, v, o, do, lse, *sc_arrays)
    return res[0], res[1], res[2], res[3:]


def _rms_bwd(v, g, dy, eps=1e-6):
    rs = lax.rsqrt(jnp.mean(v * v, axis=-1, keepdims=True) + eps)
    xh = v * rs
    dxh = dy * g
    dv = rs * (dxh - xh * jnp.mean(dxh * xh, axis=-1, keepdims=True))
    return dv, jnp.sum(dy * xh, axis=0, keepdims=True)


def _mla_bwd(dq, dk, dv, proj0, qlat, klat, q_norm, kv_norm, wq, wkv, tc, tsa, tsb, xb):
    t = MLA_T

    def body(dq_ref, dk_ref, dv_ref, cq_ref, ck_ref, ql_ref, kl_ref, qn_ref, kn_ref, wq_ref, wkv_ref,
             c_ref, sa_ref, sb_ref, xb_ref, o_ref, gwq_ref, gwkv_ref, dgq_ref, dgk_ref, gwt_ref, oq_ref, okv_ref):
        @pl.when(pl.program_id(0) == 0)
        def _():
            dgq_ref[...] = jnp.zeros_like(dgq_ref)
            dgk_ref[...] = jnp.zeros_like(dgk_ref)
            gwq_ref[...] = jnp.zeros_like(gwq_ref)
            gwkv_ref[...] = jnp.zeros_like(gwkv_ref)
            gwt_ref[...] = jnp.zeros_like(gwt_ref)

        c, sa, sb = c_ref[...], sa_ref[...], sb_ref[...]
        lane = lax.broadcasted_iota(jnp.int32, (t, LANES), 1)
        dkr = jnp.zeros((t, LANES), F32)
        for hd in range(MLA_HEADS):
            sl = slice(hd * LANES, (hd + 1) * LANES)
            oq_ref[:, sl] = _mx(_rope_t(dq_ref[:, sl], c, sa, sb))
            dkh = dk_ref[:, sl]
            okv_ref[:, sl] = _mx(dkh)
            dkr = dkr + dkh
        okv_ref[:, 1024:] = _mx(dv_ref[...])
        dkr = _rope_t(jnp.where((lane >= 64) & (lane < 96), dkr, 0.0), c, sa, sb)
        dqraw, dkvraw = oq_ref[...], okv_ref[...]
        gwq_ref[...] += _dot_tn(ql_ref[...], dqraw)
        gwkv_ref[...] += _dot_tn(kl_ref[...], dkvraw)
        dqn = _dot_nt(dqraw, wq_ref[...])
        dkn = _dot_nt(dkvraw, wkv_ref[...])
        dcq, dgq = _rms_bwd(cq_ref[...], qn_ref[...], dqn)
        dck, dgk = _rms_bwd(ck_ref[:, :LANES], kn_ref[...], dkn)
        o_ref[:, :256] = _mx(dcq)
        o_ref[:, 256:384] = _mx(dck)
        o_ref[:, 384:] = _mx(dkr)
        gwt_ref[...] += _dot_tn(xb_ref[...], o_ref[...])
        dgq_ref[0:1, :] += dgq
        dgk_ref[0:1, :] += dgk

    tab = pl.BlockSpec((t, LANES), lambda i: (i, 0))
    wide = pl.BlockSpec((t, 1024), lambda i: (i, 0))
    const = lambda shape: pl.BlockSpec(shape, lambda i: (0, 0))
    return pl.pallas_call(
        body, name="mla_bwd", grid=(SEQ // t,),
        in_specs=[wide, wide, pl.BlockSpec((t, 512), lambda i: (i, 0)),
                  pl.BlockSpec((t, 256), lambda i: (i, 6)), pl.BlockSpec((t, 256), lambda i: (i, 7)),
                  pl.BlockSpec((t, 256), lambda i: (i, 0)), tab,
                  const((1, 256)), const((1, LANES)), const((256, 1024)), const((LANES, 1536)), tab, tab, tab, wide],
        out_specs=[pl.BlockSpec((t, 512), lambda i: (i, 0)), const((256, 1024)), const((LANES, 1536)),
                   const((SUBLANES, 256)), const((SUBLANES, LANES)), const((D_MODEL, 512))],
        out_shape=[jax.ShapeDtypeStruct((SEQ, 512), MXU_DTYPE), jax.ShapeDtypeStruct((256, 1024), F32),
                   jax.ShapeDtypeStruct((LANES, 1536), F32), jax.ShapeDtypeStruct((SUBLANES, 256), F32),
                   jax.ShapeDtypeStruct((SUBLANES, LANES), F32), jax.ShapeDtypeStruct((D_MODEL, 512), F32)],
        scratch_shapes=[pltpu.VMEM((t, 1024), MXU_DTYPE), pltpu.VMEM((t, 1536), MXU_DTYPE)],
        compiler_params=_params(("arbitrary",)),
    )(dq, dk, dv, proj0, proj0, qlat, klat, q_norm, kv_norm, wq, wkv, tc, tsa, tsb, xb)


LN_T = 512


def _ln(v, g, b, eps=1e-5):
    mu = jnp.mean(v, axis=-1, keepdims=True)
    xc = v - mu
    rs = lax.rsqrt(jnp.mean(xc * xc, axis=-1, keepdims=True) + eps)
    return xc * rs * g + b


def _ln_bwd(v, g, dy, eps=1e-5):
    mu = jnp.mean(v, axis=-1, keepdims=True)
    xc = v - mu
    rs = lax.rsqrt(jnp.mean(xc * xc, axis=-1, keepdims=True) + eps)
    xh = xc * rs
    dxh = dy * g
    dv = rs * (dxh - jnp.mean(dxh, axis=-1, keepdims=True) - xh * jnp.mean(dxh * xh, axis=-1, keepdims=True))
    return dv, jnp.sum(dy * xh, axis=0, keepdims=True), jnp.sum(dy, axis=0, keepdims=True)


def _l0_out(h, o, proj0, x, w_out, g, b):
    t = LN_T

    def body(h_ref, o_ref, ga_ref, gb_ref, x_ref, w_ref, g_ref, b_ref, y_ref, v_ref, x1_ref, x1b_ref):
        y = _mx(jnp.concatenate([h_ref[...] * _silu(ga_ref[...]), o_ref[...] * _silu(gb_ref[...])], axis=1))
        v = DN_ALPHA * x_ref[...] + _dot(y, w_ref[...])
        y_ref[...] = y
        v_ref[...] = v
        x1 = _ln(v, g_ref[...], b_ref[...])
        x1_ref[...] = x1
        x1b_ref[...] = _mx(x1)

    half = pl.BlockSpec((t, 512), lambda i: (i, 0))
    full = pl.BlockSpec((t, D_MODEL), lambda i: (i, 0))
    vec = pl.BlockSpec((1, D_MODEL), lambda i: (0, 0))
    return pl.pallas_call(
        body, name="l0_out", grid=(SEQ // t,),
        in_specs=[half, half, pl.BlockSpec((t, 512), lambda i: (i, 0)), pl.BlockSpec((t, 512), lambda i: (i, 1)), full,
                  pl.BlockSpec((D_MODEL, D_MODEL), lambda i: (0, 0)), vec, vec],
        out_specs=[full, full, full, full],
        out_shape=[jax.ShapeDtypeStruct((SEQ, D_MODEL), MXU_DTYPE), jax.ShapeDtypeStruct((SEQ, D_MODEL), F32),
                   jax.ShapeDtypeStruct((SEQ, D_MODEL), F32), jax.ShapeDtypeStruct((SEQ, D_MODEL), MXU_DTYPE)],
        compiler_params=_params(("parallel",)),
    )(h, o, proj0, proj0, x, w_out, g, b)


def _l1_in(x1b, w1z, w1d):
    t = 1024

    def body(x_ref, wz_ref, wd_ref, z_ref, dt_ref):
        xv = x_ref[...]
        z_ref[...] = _dot(xv, wz_ref[...])
        dt_ref[...] = _dot(xv, wd_ref[...])

    rows = lambda w: pl.BlockSpec((t, w), lambda i: (i, 0))
    const = lambda w: pl.BlockSpec((D_MODEL, w), lambda i: (0, 0))
    return pl.pallas_call(
        body, name="l1_in", grid=(SEQ // t,),
        in_specs=[rows(D_MODEL), const(SSD_INNER), const(LANES)],
        out_specs=[rows(SSD_INNER), rows(LANES)],
        out_shape=[jax.ShapeDtypeStruct((SEQ, SSD_INNER), F32), jax.ShapeDtypeStruct((SEQ, LANES), F32)],
        compiler_params=_params(("parallel",)),
    )(x1b, w1z, w1d)


def _l1_dx_ln(dz, dxbc, ddt, dv1, v0, w1z, w1x, w1d, g):
    t = LN_T

    def body(dz_ref, dx_ref, ddt_ref, dv1_ref, v_ref, wz_ref, wx_ref, wd_ref, g_ref, dv_ref, dgb_ref):
        @pl.when(pl.program_id(0) == 0)
        def _():
            dgb_ref[...] = jnp.zeros_like(dgb_ref)

        dy = (DN_ALPHA * dv1_ref[...] + _dot_nt(dz_ref[...], wz_ref[...]) + _dot_nt(dx_ref[...], wx_ref[...])
              + _dot_nt(_mx(ddt_ref[...]), wd_ref[...]))
        dv, dg, db = _ln_bwd(v_ref[...], g_ref[...], dy)
        dv_ref[...] = dv
        dgb_ref[0:1, :] += dg
        dgb_ref[1:2, :] += db

    rows = lambda w: pl.BlockSpec((t, w), lambda i: (i, 0))
    const = lambda w: pl.BlockSpec((D_MODEL, w), lambda i: (0, 0))
    return pl.pallas_call(
        body, name="l1_dx_ln", grid=(SEQ // t,),
        in_specs=[rows(SSD_INNER), rows(SSD_CONV), rows(LANES), rows(D_MODEL), rows(D_MODEL),
                  const(SSD_INNER), const(SSD_CONV), const(LANES), pl.BlockSpec((1, D_MODEL), lambda i: (0, 0))],
        out_specs=[rows(D_MODEL), pl.BlockSpec((SUBLANES, D_MODEL), lambda i: (0, 0))],
        out_shape=[jax.ShapeDtypeStruct((SEQ, D_MODEL), F32), jax.ShapeDtypeStruct((SUBLANES, D_MODEL), F32)],
        compiler_params=_params(("arbitrary",)),
    )(dz, dxbc, ddt, dv1, v0, w1z, w1x, w1d, g)


def _gate_bwd(dv0, w_out, h, o, proj0, y0, xb):
    t = LN_T

    def body(dv_ref, w_ref, h_ref, o_ref, ga_ref, gb_ref, y0_ref, xb_ref, dh_ref, do_ref, dg_ref, gwo_ref, gwg_ref):
        @pl.when(pl.program_id(0) == 0)
        def _():
            gwo_ref[...] = jnp.zeros_like(gwo_ref)
            gwg_ref[...] = jnp.zeros_like(gwg_ref)

        dvb = _mx(dv_ref[...])
        dy = _dot_nt(dvb, w_ref[...])
        ga, gb, dya, dyb = ga_ref[...], gb_ref[...], dy[:, :512], dy[:, 512:]
        dh_ref[...] = dya * _silu(ga)
        do_ref[...] = dyb * _silu(gb)
        dg_ref[:, :512] = _mx(dya * h_ref[...] * _dsilu(ga))
        dg_ref[:, 512:] = _mx(dyb * o_ref[...] * _dsilu(gb))
        gwo_ref[...] += _dot_tn(y0_ref[...], dvb)
        gwg_ref[...] += _dot_tn(xb_ref[...], dg_ref[...])

    half = pl.BlockSpec((t, 512), lambda i: (i, 0))
    half1 = pl.BlockSpec((t, 512), lambda i: (i, 1))
    full = pl.BlockSpec((t, 1024), lambda i: (i, 0))
    square = pl.BlockSpec((D_MODEL, D_MODEL), lambda i: (0, 0))
    return pl.pallas_call(
        body, name="gate_bwd", grid=(SEQ // t,),
        in_specs=[full, square, half, half, half, half1, full, full],
        out_specs=[half, half, full, square, square],
        out_shape=[jax.ShapeDtypeStruct((SEQ, 512), F32), jax.ShapeDtypeStruct((SEQ, 512), F32),
                   jax.ShapeDtypeStruct((SEQ, 1024), MXU_DTYPE), jax.ShapeDtypeStruct((D_MODEL, D_MODEL), F32),
                   jax.ShapeDtypeStruct((D_MODEL, D_MODEL), F32)],
        compiler_params=_params(("arbitrary",)),
    )(dv0, w_out, h, o, proj0, proj0, y0, xb)


CONV_T = 1024
CONV_CB = 1024


def _ssd_conv_fwd(x1b, w1x, cw8, cb):
    t, cbk = CONV_T, CONV_CB

    def body(x_ref, w_ref, cw_ref, cb_ref, xbc_ref, pre_ref, act_ref, carry):
        xbc = _dot(x_ref[...], w_ref[...])
        halo = jnp.where(pl.program_id(1) > 0, carry[...], 0.0)
        pre = _conv4(xbc, halo, cw_ref[...], cb_ref[...])
        carry[...] = xbc[t - SUBLANES:]
        xbc_ref[...] = xbc
        pre_ref[...] = pre
        act_ref[...] = _silu(pre)

    blk = pl.BlockSpec((t, cbk), lambda j, i: (i, j))
    out = jax.ShapeDtypeStruct((SEQ, SSD_CONV), F32)
    return pl.pallas_call(
        body, name="ssd_conv_fwd", grid=(SSD_CONV // cbk, SEQ // t),
        in_specs=[pl.BlockSpec((t, D_MODEL), lambda j, i: (i, 0)), pl.BlockSpec((D_MODEL, cbk), lambda j, i: (0, j)),
                  pl.BlockSpec((SUBLANES, cbk), lambda j, i: (0, j)), pl.BlockSpec((1, cbk), lambda j, i: (0, j))],
        out_specs=[blk, blk, blk], out_shape=[out, out, out],
        scratch_shapes=[pltpu.VMEM((SUBLANES, cbk), F32)],
        compiler_params=_params(("parallel", "arbitrary")),
    )(x1b, w1x, cw8, cb)


def _ssd_conv_bwd(dact, pre, xbc, cw8, x1b):
    t, cbk = CONV_T, CONV_CB
    tb = t // SUBLANES
    nb = SEQ // t

    def body(da_ref, dan_ref, pre_ref, pren_ref, x_ref, cw_ref, x1_ref, dx_ref, dcw_ref, gw_ref):
        i = pl.program_id(1)

        @pl.when(i == 0)
        def _():
            dcw_ref[...] = jnp.zeros_like(dcw_ref)
            gw_ref[...] = jnp.zeros_like(gw_ref)

        dpre = da_ref[...] * _dsilu(pre_ref[...])
        dpre_next = jnp.where(i < nb - 1, dan_ref[...] * _dsilu(pren_ref[...]), 0.0)
        xblk = x_ref[...]
        cw = cw_ref[...]
        dx = dpre * cw[3:4]
        dcw_ref[3:4, :] += jnp.sum(dpre * xblk, axis=0, keepdims=True)
        for k in range(3):
            up = _shift_up(dpre, dpre_next, 3 - k)
            dcw_ref[k:k + 1, :] += jnp.sum(up * xblk, axis=0, keepdims=True)
            dx = dx + up * cw[k:k + 1]
        dcw_ref[4:5, :] += jnp.sum(dpre, axis=0, keepdims=True)
        dxb = _mx(dx)
        dx_ref[...] = dxb
        gw_ref[...] += _dot_tn(x1_ref[...], dxb)

    blk = pl.BlockSpec((t, cbk), lambda j, i: (i, j))
    nxt = pl.BlockSpec((SUBLANES, cbk), lambda j, i: (jnp.minimum((i + 1) * tb, SEQ // SUBLANES - 1), j))
    acc = pl.BlockSpec((SUBLANES, cbk), lambda j, i: (0, j))
    return pl.pallas_call(
        body, name="ssd_conv_bwd", grid=(SSD_CONV // cbk, nb),
        in_specs=[blk, nxt, blk, nxt, blk, acc, pl.BlockSpec((t, D_MODEL), lambda j, i: (i, 0))],
        out_specs=[blk, acc, pl.BlockSpec((D_MODEL, cbk), lambda j, i: (0, j))],
        out_shape=[jax.ShapeDtypeStruct((SEQ, SSD_CONV), MXU_DTYPE), jax.ShapeDtypeStruct((SUBLANES, SSD_CONV), F32),
                   jax.ShapeDtypeStruct((D_MODEL, SSD_CONV), F32)],
        compiler_params=_params(("parallel", "arbitrary")),
    )(dact, dact, pre, pre, xbc, cw8, x1b)


def _ssd_common(dt_raw, bias, alog, tril, expand_t, xs):
    lane = lax.broadcasted_iota(jnp.int32, dt_raw.shape, 1)
    dt = jnp.where(lane < SSD_HEADS, _softplus(dt_raw + bias), 0.0)
    a_neg = -jnp.exp(alog)
    cs = _dot_hi(tril, dt * a_neg)
    dt_x = _expand_heads(dt, expand_t)
    ecs_x = _expand_heads(jnp.exp(cs), expand_t)
    ds_x = _expand_heads(jnp.exp(cs[SSD_L - 1:SSD_L, :] - cs), expand_t)
    return dt, a_neg, cs, dt_x, None, xs * dt_x, ds_x, ecs_x, ecs_x[SSD_L - 1:SSD_L, :]


def _expand_heads(v, expand_t):
    hi = v.astype(jnp.bfloat16)
    lo = (v - hi.astype(F32)).astype(jnp.bfloat16)
    return _dot_nt(hi, expand_t) + _dot_nt(lo, expand_t)


def _fold_heads(v, expand_t):
    hi = v.astype(jnp.bfloat16)
    lo = (v - hi.astype(F32)).astype(jnp.bfloat16)
    return _dot(hi, expand_t) + _dot(lo, expand_t)


def _ssd_decay(cs, cs_t, hh, causal):
    seg = cs[:, hh:hh + 1] - cs_t[hh:hh + 1, :]
    return jnp.where(causal, jnp.exp(jnp.where(causal, seg, 0.0)), 0.0)


def _ssd_scan_fwd(act, dt_raw, bias, alog, d_x, tril, expand_t, bcast=()):
    nc = SEQ // SSD_L
    gw = SSD_INNER // SSD_GROUPS
    n = len(bcast)

    def body(act_ref, dt_ref, bias_ref, alog_ref, dx_ref, tril_ref, et_ref, *rest):
        y_ref, hp_ref, h_sc = rest[n], rest[n + 1], rest[2 * n + 2]
        if n:
            copies = _peer_copies(rest[:n], rest[n + 2:2 * n + 2], rest[2 * n + 3:], [])

            @pl.when(pl.program_id(0) == 0)
            def _():
                for cp in copies:
                    cp.start()

            @pl.when(pl.program_id(0) == nc - 1)
            def _():
                for cp in copies:
                    cp.wait()

        @pl.when(pl.program_id(0) == 0)
        def _():
            h_sc[...] = jnp.zeros_like(h_sc)

        xs = act_ref[:, :SSD_INNER]
        _, _, cs, _, _, xdt, ds_x, ecs_x, elast = _ssd_common(
            dt_ref[...], bias_ref[...], alog_ref[...], tril_ref[...], et_ref[...], xs)
        cs_t = cs.T
        causal = (lax.broadcasted_iota(jnp.int32, (SSD_L, SSD_L), 0)
                  >= lax.broadcasted_iota(jnp.int32, (SSD_L, SSD_L), 1))
        lane = lax.broadcasted_iota(jnp.int32, (SSD_L, LANES), 1)
        xdt_b = _mx(xdt)
        xds_b = _mx(xdt * ds_x)
        hp_ref[0] = h_sc[...]
        for g in range(SSD_GROUPS):
            gs = slice(g * gw, (g + 1) * gw)
            bg = _mx(act_ref[:, SSD_INNER + g * SSD_N:SSD_INNER + (g + 1) * SSD_N])
            cg = _mx(act_ref[:, SSD_INNER + 512 + g * SSD_N:SSD_INNER + 512 + (g + 1) * SSD_N])
            cb = _dot_nt(cg, bg)
            hprev = h_sc[:, gs]
            yoff = _dot(cg, _mx(hprev)) * ecs_x[:, gs]
            h_sc[:, gs] = hprev * elast[:, gs] + _dot_tn(bg, xds_b[:, gs])
            for pr in range(4):
                ps = slice(g * gw + pr * LANES, g * gw + (pr + 1) * LANES)
                xp = xdt_b[:, ps]
                ydiag = jnp.zeros((SSD_L, LANES), F32)
                for j in range(2):
                    dm = _ssd_decay(cs, cs_t, g * 8 + pr * 2 + j, causal)
                    mine = (lane >= j * 64) & (lane < (j + 1) * 64)
                    ydiag = ydiag + _dot(_mx(cb * dm), jnp.where(mine, xp, jnp.zeros_like(xp)))
                y_ref[:, ps] = ydiag + yoff[:, pr * LANES:(pr + 1) * LANES] + dx_ref[:, ps] * xs[:, ps]

    const = lambda shape: pl.BlockSpec(shape, lambda c: (0, 0))
    res = pl.pallas_call(
        body, name="ssd_scan_fwd", grid=(nc,),
        in_specs=[pl.BlockSpec((SSD_L, SSD_CONV), lambda c: (c, 0)), pl.BlockSpec((SSD_L, LANES), lambda c: (c, 0)),
                  const((1, LANES)), const((1, LANES)), const((1, SSD_INNER)), const((SSD_L, SSD_L)),
                  const((SSD_INNER, LANES))] + [ANY] * n,
        out_specs=[pl.BlockSpec((SSD_L, SSD_INNER), lambda c: (c, 0)),
                   pl.BlockSpec((1, SSD_N, SSD_INNER), lambda c: (c, 0, 0))] + [ANY] * n,
        out_shape=[jax.ShapeDtypeStruct((SEQ, SSD_INNER), F32), jax.ShapeDtypeStruct((nc, SSD_N, SSD_INNER), F32)]
        + _exchange_shapes([], bcast),
        scratch_shapes=[pltpu.VMEM((SSD_N, SSD_INNER), F32)] + (_exchange_sems(n) if n else []),
        compiler_params=_params(("arbitrary",)),
    )(act, dt_raw, bias, alog, d_x, tril, expand_t, *bcast)
    return res[0], res[1], res[2:]


def _ssd_scan_bwd(dy, act, dt_raw, hprev_all, bias, alog, d_x, tril, expand_t, x1b, scatter=()):
    nc = SEQ // SSD_L
    gw = SSD_INNER // SSD_GROUPS
    sc_arrays, sc_ranges = _scatter_args(scatter)
    nx = len(sc_arrays)

    def body(dy_ref, act_ref, dt_ref, hp_ref, bias_ref, alog_ref, dx_ref, tril_ref, et_ref, x1_ref, *rest):
        dact_ref, ddt_ref, dvec_ref, gdt_ref = rest[nx:nx + 4]
        dh_sc, dd_sc, dcs_sc, dcst_sc = rest[2 * nx + 4:2 * nx + 8]
        i = pl.program_id(0)
        if nx:
            copies = _peer_copies(rest[:nx], rest[nx + 4:2 * nx + 4], rest[2 * nx + 8:], sc_ranges)

            @pl.when(i == 0)
            def _():
                for cp in copies:
                    cp.start()

        @pl.when(i == 0)
        def _():
            dh_sc[...] = jnp.zeros_like(dh_sc)
            dd_sc[...] = jnp.zeros_like(dd_sc)
            gdt_ref[...] = jnp.zeros_like(gdt_ref)
            dvec_ref[...] = jnp.zeros_like(dvec_ref)

        xs = act_ref[:, :SSD_INNER]
        dt_raw_v, bias_v = dt_ref[...], bias_ref[...]
        dt, a_neg, cs, dt_x, _, xdt, ds_x, ecs_x, elast = _ssd_common(
            dt_raw_v, bias_v, alog_ref[...], tril_ref[...], et_ref[...], xs)
        cs_t = cs.T
        rowi = lax.broadcasted_iota(jnp.int32, (SSD_L, SSD_L), 0)
        coli = lax.broadcasted_iota(jnp.int32, (SSD_L, SSD_L), 1)
        causal = rowi >= coli
        lane = lax.broadcasted_iota(jnp.int32, (SSD_L, LANES), 1)
        row_g = lax.broadcasted_iota(jnp.int32, (SSD_L, gw), 0)
        dyv = dy_ref[...]
        dd_sc[0:1, :] += jnp.sum(dyv * xs, axis=0, keepdims=True)
        xdt_b = _mx(xdt)
        xds = xdt * ds_x
        xds_b = _mx(xds)
        dy_b = _mx(dyv)
        dye_b = _mx(dyv * ecs_x)
        dcs_sc[...] = jnp.zeros_like(dcs_sc)
        dcst_sc[...] = jnp.zeros_like(dcst_sc)
        dcs_parts = []
        dxdt_parts = []
        for g in range(SSD_GROUPS):
            gs = slice(g * gw, (g + 1) * gw)
            bcol = slice(SSD_INNER + g * SSD_N, SSD_INNER + (g + 1) * SSD_N)
            ccol = slice(SSD_INNER + 512 + g * SSD_N, SSD_INNER + 512 + (g + 1) * SSD_N)
            bg, cg = _mx(act_ref[:, bcol]), _mx(act_ref[:, ccol])
            cb = _dot_nt(cg, bg)
            hp = hp_ref[0, :, gs]
            hp_b = _mx(hp)
            dh = dh_sc[:, gs]
            dh_b = _mx(dh)
            yoff = _dot(cg, hp_b) * ecs_x[:, gs]
            bdh = _dot(bg, dh_b)
            tt = xds[:, gs] * bdh
            last_row = (jnp.sum(tt, axis=0, keepdims=True)
                        + jnp.sum(dh * hp, axis=0, keepdims=True) * elast[:, gs])
            dcs_parts.append(dyv[:, gs] * yoff - tt + jnp.where(row_g == SSD_L - 1, last_row, 0.0))
            dc_g = _dot_nt(dye_b[:, gs], hp_b)
            db_g = _dot_nt(xds_b[:, gs], dh_b)
            dh_sc[:, gs] = _dot_tn(cg, dye_b[:, gs]) + dh * elast[:, gs]
            wsum = jnp.zeros((SSD_L, SSD_L), F32)
            dxdt_g = []
            for pr in range(4):
                ps = slice(g * gw + pr * LANES, g * gw + (pr + 1) * LANES)
                xp, dyp = xdt_b[:, ps], dy_b[:, ps]
                dxp = jnp.zeros((SSD_L, LANES), F32)
                for j in range(2):
                    hh = g * 8 + pr * 2 + j
                    dm = _ssd_decay(cs, cs_t, hh, causal)
                    mine = (lane >= j * 64) & (lane < (j + 1) * 64)
                    dy_h = jnp.where(mine, dyp, jnp.zeros_like(dyp))
                    wd = _dot_nt(dy_h, xp) * dm
                    wsum = wsum + wd
                    gmat = wd * cb
                    dcs_sc[:, hh:hh + 1] = jnp.sum(gmat, axis=1, keepdims=True)
                    dcst_sc[hh:hh + 1, :] = -jnp.sum(gmat, axis=0, keepdims=True)
                    dxp = dxp + _dot_tn(_mx(cb * dm), dy_h)
                dxdt_g.append(dxp)
            dxdt_parts.append(jnp.concatenate(dxdt_g, axis=1) + bdh * ds_x[:, gs])
            ws_b = _mx(wsum)
            dact_ref[:, ccol] = dc_g + _dot(ws_b, bg)
            dact_ref[:, bcol] = db_g + _dot_tn(ws_b, cg)
        dxdt = jnp.concatenate(dxdt_parts, axis=1)
        dcs_x = jnp.concatenate(dcs_parts, axis=1)
        et = et_ref[...]
        dcs_tot = dcs_sc[...] + dcst_sc[...].T + _fold_heads(dcs_x, et)
        da_dt = _dot_hi((coli >= rowi).astype(F32), dcs_tot)
        ddt = da_dt * a_neg + _fold_heads(dxdt * xs, et)
        ddt_raw = ddt * _sigmoid(dt_raw_v + bias_v)
        ddt_ref[...] = ddt_raw
        gdt_ref[...] += _dot_tn(x1_ref[...], _mx(ddt_raw))
        dvec_ref[0:1, :] += jnp.sum(ddt_raw, axis=0, keepdims=True)
        dvec_ref[1:2, :] += jnp.sum(da_dt * dt, axis=0, keepdims=True) * a_neg
        dact_ref[:, :SSD_INNER] = dyv * dx_ref[...] + dxdt * dt_x

        @pl.when(i == nc - 1)
        def _():
            dvec_ref[2:3, :] = _fold_heads(dd_sc[...], et)[0:1, :]
            if nx:
                for cp in copies:
                    cp.wait()

    const = lambda shape: pl.BlockSpec(shape, lambda c: (0, 0))
    rev = lambda c: (nc - 1 - c, 0)
    res = pl.pallas_call(
        body, name="ssd_scan_bwd", grid=(nc,),
        in_specs=[pl.BlockSpec((SSD_L, SSD_INNER), rev), pl.BlockSpec((SSD_L, SSD_CONV), rev),
                  pl.BlockSpec((SSD_L, LANES), rev),
                  pl.BlockSpec((1, SSD_N, SSD_INNER), lambda c: (nc - 1 - c, 0, 0)),
                  const((1, LANES)), const((1, LANES)), const((1, SSD_INNER)), const((SSD_L, SSD_L)),
                  const((SSD_INNER, LANES)), pl.BlockSpec((SSD_L, D_MODEL), rev)] + [ANY] * nx,
        out_specs=[pl.BlockSpec((SSD_L, SSD_CONV), rev), pl.BlockSpec((SSD_L, LANES), rev), const((SUBLANES, LANES)),
                   const((D_MODEL, LANES))] + [ANY] * nx,
        out_shape=[jax.ShapeDtypeStruct((SEQ, SSD_CONV), F32), jax.ShapeDtypeStruct((SEQ, LANES), F32),
                   jax.ShapeDtypeStruct((SUBLANES, LANES), F32), jax.ShapeDtypeStruct((D_MODEL, LANES), F32)]
        + _exchange_shapes(sc_arrays, []),
        scratch_shapes=[pltpu.VMEM((SSD_N, SSD_INNER), F32), pltpu.VMEM((SUBLANES, SSD_INNER), F32),
                        pltpu.VMEM((SSD_L, LANES), F32), pltpu.VMEM((LANES, SSD_L), F32)]
        + (_exchange_sems(nx) if nx else []),
        compiler_params=_params(("arbitrary",)),
    )(dy, act, dt_raw, hprev_all, bias, alog, d_x, tril, expand_t, x1b, *sc_arrays)
    return res[0], res[1], res[2], res[3], res[4:]


L1_T = 512


def _resident(shape):
    return pl.BlockSpec(shape, lambda i: (0, 0), pipeline_mode=pl.Buffered(1))


def _gated_norm(y, z, nw):
    y2 = y * _silu(z)
    gw = SSD_INNER // SSD_GROUPS
    outs, xhs, rss = [], [], []
    for g in range(SSD_GROUPS):
        gs = slice(g * gw, (g + 1) * gw)
        v = y2[:, gs]
        rs = lax.rsqrt(jnp.mean(v * v, axis=-1, keepdims=True) + 1e-6)
        xhs.append(v * rs)
        rss.append(rs)
        outs.append(v * rs * nw[:, gs])
    return outs, xhs, rss


def _l1_out(y, z, nw, w_out, x1, g, b, target):
    t = L1_T

    def body(y_ref, z_ref, nw_ref, w_ref, x1_ref, g_ref, b_ref, tg_ref, dv_ref, dgb_ref, loss_ref, gw_ref):
        @pl.when(pl.program_id(0) == 0)
        def _():
            dgb_ref[...] = jnp.zeros_like(dgb_ref)
            loss_ref[...] = jnp.zeros_like(loss_ref)
            gw_ref[...] = jnp.zeros_like(gw_ref)

        outs, _, _ = _gated_norm(y_ref[...], z_ref[...], nw_ref[...])
        yn = _mx(jnp.concatenate(outs, axis=1))
        v = DN_ALPHA * x1_ref[...] + _dot(yn, w_ref[...])
        gv = g_ref[...]
        err = _ln(v, gv, b_ref[...]) - tg_ref[...]
        rowsum = jnp.sum(err * err, axis=1, keepdims=True)
        loss_ref[...] += 0.5 * jnp.sum(rowsum, axis=0, keepdims=True) / D_MODEL
        dv, dg, db = _ln_bwd(v, gv, err / D_MODEL)
        dv_ref[...] = dv
        dgb_ref[0:1, :] += dg
        dgb_ref[1:2, :] += db
        gw_ref[...] += _dot_tn(yn, _mx(dv))

    wide = pl.BlockSpec((t, SSD_INNER), lambda i: (i, 0))
    full = pl.BlockSpec((t, D_MODEL), lambda i: (i, 0))
    vec = pl.BlockSpec((1, D_MODEL), lambda i: (0, 0))
    return pl.pallas_call(
        body, name="l1_out", grid=(SEQ // t,),
        in_specs=[wide, wide, pl.BlockSpec((1, SSD_INNER), lambda i: (0, 0)),
                  _resident((SSD_INNER, D_MODEL)), full, vec, vec, full],
        out_specs=[full, pl.BlockSpec((SUBLANES, D_MODEL), lambda i: (0, 0)),
                   pl.BlockSpec((SUBLANES, LANES), lambda i: (0, 0)), _resident((SSD_INNER, D_MODEL))],
        out_shape=[jax.ShapeDtypeStruct((SEQ, D_MODEL), F32), jax.ShapeDtypeStruct((SUBLANES, D_MODEL), F32),
                   jax.ShapeDtypeStruct((SUBLANES, LANES), F32), jax.ShapeDtypeStruct((SSD_INNER, D_MODEL), F32)],
        compiler_params=_params(("arbitrary",)),
    )(y, z, nw, w_out, x1, g, b, target)


def _l1_gate_bwd(dv1, w_out, y, z, nw, x1b):
    t = L1_T
    gw = SSD_INNER // SSD_GROUPS

    def body(dv_ref, w_ref, y_ref, z_ref, nw_ref, x1_ref, dy_ref, dz_ref, dnw_ref, gw_ref):
        @pl.when(pl.program_id(0) == 0)
        def _():
            dnw_ref[...] = jnp.zeros_like(dnw_ref)
            gw_ref[...] = jnp.zeros_like(gw_ref)

        dyn = _dot_nt(_mx(dv_ref[...]), w_ref[...])
        yv, zv, nwv = y_ref[...], z_ref[...], nw_ref[...]
        _, xhs, rss = _gated_norm(yv, zv, nwv)
        sz, dsz = _silu(zv), _dsilu(zv)
        for g in range(SSD_GROUPS):
            gs = slice(g * gw, (g + 1) * gw)
            d_out = dyn[:, gs]
            xh = xhs[g]
            dnw_ref[0:1, gs] += jnp.sum(d_out * xh, axis=0, keepdims=True)
            dxh = d_out * nwv[:, gs]
            dy2 = rss[g] * (dxh - xh * jnp.mean(dxh * xh, axis=-1, keepdims=True))
            dy_ref[:, gs] = dy2 * sz[:, gs]
            dz_ref[:, gs] = _mx(dy2 * yv[:, gs] * dsz[:, gs])
        gw_ref[...] += _dot_tn(x1_ref[...], dz_ref[...])

    wide = pl.BlockSpec((t, SSD_INNER), lambda i: (i, 0))
    return pl.pallas_call(
        body, name="l1_gate_bwd", grid=(SEQ // t,),
        in_specs=[pl.BlockSpec((t, D_MODEL), lambda i: (i, 0)), _resident((SSD_INNER, D_MODEL)),
                  wide, wide, pl.BlockSpec((1, SSD_INNER), lambda i: (0, 0)), pl.BlockSpec((t, D_MODEL), lambda i: (i, 0))],
        out_specs=[wide, wide, pl.BlockSpec((SUBLANES, SSD_INNER), lambda i: (0, 0)),
                   _resident((D_MODEL, SSD_INNER))],
        out_shape=[jax.ShapeDtypeStruct((SEQ, SSD_INNER), F32), jax.ShapeDtypeStruct((SEQ, SSD_INNER), MXU_DTYPE),
                   jax.ShapeDtypeStruct((SUBLANES, SSD_INNER), F32), jax.ShapeDtypeStruct((D_MODEL, SSD_INNER), F32)],
        compiler_params=_params(("arbitrary",)),
    )(dv1, w_out, y, z, nw, x1b)


MESH = pl.DeviceIdType.MESH
ANY = pl.BlockSpec(memory_space=pl.ANY)


def _flip(v, bit):
    return 1 - v if bit else v


def _all_gather(blocks, name):
    n = len(blocks)

    def body(*refs):
        x_refs, out_refs = refs[:n], refs[n:2 * n]
        send_sems, recv_sems, local_sems = refs[2 * n:]
        mx, my, mc = lax.axis_index("x"), lax.axis_index("y"), lax.axis_index("c")
        me, sibling = (mx, my, mc), (mx, my, 1 - mc)
        chips = [(1 - mx, my), (mx, 1 - my), (1 - mx, 1 - my)]

        def copy(a, k, block, to, own=False):
            px, py, pc = block
            slot = out_refs[a].at[4 * px + 2 * py + pc]
            return pltpu.make_async_remote_copy(
                src_ref=x_refs[a] if own else slot, dst_ref=slot,
                send_sem=send_sems.at[7 * a + k], recv_sem=recv_sems.at[7 * a + k], device_id=to, device_id_type=MESH)

        mine = [pltpu.make_async_copy(x_refs[a], out_refs[a].at[4 * mx + 2 * my + mc], local_sems.at[a])
                for a in range(n)]
        first = []
        for a in range(n):
            mine[a].start()
            first.append(copy(a, 0, me, sibling, own=True))
            first += [copy(a, 1 + j, me, (*chip, mc), own=True) for j, chip in enumerate(chips)]
        for cp in first:
            cp.start()
        passed = []
        for j, chip in enumerate(chips):
            for a in range(n):
                copy(a, 1 + j, (*chip, mc), me).wait_recv()
                fwd = copy(a, 4 + j, (*chip, mc), sibling)
                fwd.start()
                passed.append(fwd)
        for a in range(n):
            copy(a, 0, sibling, me).wait_recv()
            for j, chip in enumerate(chips):
                copy(a, 4 + j, (*chip, 1 - mc), me).wait_recv()
        for cp in first + passed:
            cp.wait_send()
        for cp in mine:
            cp.wait()

    return pl.pallas_call(
        body, name=name, in_specs=[ANY] * n, out_specs=[ANY] * n,
        out_shape=[jax.ShapeDtypeStruct((N_DEV,) + b.shape, b.dtype) for b in blocks],
        scratch_shapes=[pltpu.SemaphoreType.DMA((7 * n,)), pltpu.SemaphoreType.DMA((7 * n,)),
                        pltpu.SemaphoreType.DMA((n,))],
    )(*blocks)


def _l0_in(x, w0p, bcast=()):
    n = len(bcast)
    tm, tn = 1024, 1024
    gi, gj = SEQ // tm, 2048 // tn

    def body(x_ref, w_ref, *rest):
        o_ref, xb_ref = rest[n], rest[n + 1]
        i, j = pl.program_id(0), pl.program_id(1)
        if n:
            copies = _peer_copies(rest[:n], rest[n + 2:2 * n + 2], rest[2 * n + 2:], [])

            @pl.when((i == 0) & (j == 0))
            def _():
                for cp in copies:
                    cp.start()

        xb = _mx(x_ref[...])
        xb_ref[...] = xb
        o_ref[...] = _dot(xb, w_ref[...])

        if n:
            @pl.when((i == gi - 1) & (j == gj - 1))
            def _():
                for cp in copies:
                    cp.wait()

    res = pl.pallas_call(
        body, name="l0_in", grid=(gi, gj),
        in_specs=[pl.BlockSpec((tm, D_MODEL), lambda i, j: (i, 0)), pl.BlockSpec((D_MODEL, tn), lambda i, j: (0, j))]
        + [ANY] * n,
        out_specs=[pl.BlockSpec((tm, tn), lambda i, j: (i, j)), pl.BlockSpec((tm, D_MODEL), lambda i, j: (i, 0))]
        + [ANY] * n,
        out_shape=[jax.ShapeDtypeStruct((SEQ, 2048), F32), jax.ShapeDtypeStruct((SEQ, D_MODEL), MXU_DTYPE)]
        + _exchange_shapes([], bcast),
        scratch_shapes=_exchange_sems(n) if n else [],
        compiler_params=_params(("arbitrary", "arbitrary")),
    )(x, w0p, *bcast)
    return res[0], res[1], res[2:]


def _l0_dx(dxr, dgate, dtail, w0p, dv0, scatter=(), bcast=()):
    arrays, ranges = _scatter_args(scatter)
    n = len(arrays) + len(bcast)
    tm = 1024
    steps = SEQ // tm

    def body(dxr_ref, dg_ref, dt_ref, w_ref, dv_ref, *rest):
        o_ref = rest[n]
        i = pl.program_id(0)
        if n:
            copies = _peer_copies(rest[:n], rest[n + 1:2 * n + 1], rest[2 * n + 1:], ranges)

            @pl.when(i == 0)
            def _():
                for cp in copies:
                    cp.start()

        o_ref[...] = (DN_ALPHA * dv_ref[...] + _dot_nt(dg_ref[...], w_ref[:, 0:1024])
                      + _dot_nt(dxr_ref[...], w_ref[:, 1024:1536]) + _dot_nt(dt_ref[...], w_ref[:, 1536:2048]))

        if n:
            @pl.when(i == steps - 1)
            def _():
                for cp in copies:
                    cp.wait()

    rows = lambda w: pl.BlockSpec((tm, w), lambda i: (i, 0))
    res = pl.pallas_call(
        body, name="l0_dx", grid=(steps,),
        in_specs=[rows(512), rows(1024), rows(512), pl.BlockSpec((D_MODEL, 2048), lambda i: (0, 0)), rows(D_MODEL)]
        + [ANY] * n,
        out_specs=[rows(D_MODEL)] + [ANY] * n,
        out_shape=[jax.ShapeDtypeStruct((SEQ, D_MODEL), F32)] + _exchange_shapes(arrays, bcast),
        scratch_shapes=_exchange_sems(n) if n else [],
        compiler_params=_params(("arbitrary",)),
    )(dxr, dgate, dtail, w0p, dv0, *arrays, *bcast)
    return res[0], res[1:]


def _scatter_args(scatter):
    arrays = [s[0] if isinstance(s, tuple) else s for s in scatter]
    ranges = [s[1] if isinstance(s, tuple) else (0, N_DEV) for s in scatter]
    return arrays, ranges


def _exchange_shapes(scatter, bcast):
    return ([jax.ShapeDtypeStruct((N_DEV,) + a.shape[1:], a.dtype) for a in scatter]
            + [jax.ShapeDtypeStruct((N_DEV,) + a.shape, a.dtype) for a in bcast])


def _exchange_sems(n):
    return [pltpu.SemaphoreType.DMA((7 * n,)), pltpu.SemaphoreType.DMA((7 * n,)), pltpu.SemaphoreType.DMA((n,))]


class _GuardedCopy:
    def __init__(self, copy, send=None, recv=None, local=False):
        self.copy, self.send, self.recv, self.local = copy, send, recv, local

    @staticmethod
    def _run(pred, fn):
        if pred is None:
            fn()
        else:
            pl.when(pred)(fn)

    def start(self):
        self._run(self.send, self.copy.start)

    def wait(self):
        if self.local:
            self._run(self.send, self.copy.wait)
        else:
            self._run(self.send, self.copy.wait_send)
            self._run(self.recv, self.copy.wait_recv)


def _peer_copies(in_refs, out_refs, sems, ranges):
    send_sems, recv_sems, local_sems = sems
    n, ns = len(in_refs), len(ranges)
    mx, my, mc = lax.axis_index("x"), lax.axis_index("y"), lax.axis_index("c")
    me = 4 * mx + 2 * my + mc

    def src(a, slot):
        return in_refs[a].at[slot - ranges[a][0]] if a < ns else in_refs[a]

    def member(a, dev):
        if a >= ns or ranges[a] == (0, N_DEV):
            return None
        return (dev >= ranges[a][0]) & (dev < ranges[a][1])

    copies = [_GuardedCopy(pltpu.make_async_copy(src(a, me), out_refs[a].at[me], local_sems.at[a]),
                           send=member(a, me), local=True) for a in range(n)]
    for k in range(1, N_DEV):
        px, py, pc = _flip(mx, (k >> 2) & 1), _flip(my, (k >> 1) & 1), _flip(mc, k & 1)
        peer = 4 * px + 2 * py + pc
        for a in range(n):
            copies.append(_GuardedCopy(pltpu.make_async_remote_copy(
                src_ref=src(a, peer), dst_ref=out_refs[a].at[me],
                send_sem=send_sems.at[7 * a + k - 1], recv_sem=recv_sems.at[7 * a + k - 1],
                device_id=(px, py, pc), device_id_type=MESH), send=member(a, peer), recv=member(a, me)))
    return copies


def _segments(col_map, width):
    segs = []
    for lo, hi, arr, alo in col_map:
        for s in range(N_DEV):
            a, b = max(lo, s * width), min(hi, (s + 1) * width)
            if a < b:
                segs.append((s, a - s * width, b - a, arr, alo + a - lo))
    return segs


COPY_ROWS = 256


def _unshard(g8, col_map, widths, name):
    _, r, w = g8.shape
    rb = min(r, COPY_ROWS)
    segs = _segments(col_map, w)

    def body(g_ref, *o_refs):
        for o_ref in o_refs:
            o_ref[...] = jnp.zeros_like(o_ref)
        for s, llo, n, arr, alo in segs:
            o_refs[arr][:, alo:alo + n] = g_ref[s, :, llo:llo + n]

    return pl.pallas_call(
        body, name=name, grid=(r // rb,),
        in_specs=[pl.BlockSpec((N_DEV, rb, w), lambda i: (0, i, 0))],
        out_specs=[pl.BlockSpec((rb, n), lambda i: (i, 0)) for n in widths],
        out_shape=[jax.ShapeDtypeStruct((r, n), g8.dtype) for n in widths],
        compiler_params=_params(("parallel",)),
    )(g8)


def _reshard(srcs, col_map, w, dtype, name, shards=(0, N_DEV)):
    r = srcs[0].shape[0]
    rb = min(r, COPY_ROWS)
    lo, hi = shards
    segs = [sg for sg in _segments(col_map, w) if lo <= sg[0] < hi]

    def body(*refs):
        o_ref = refs[-1]
        for s, llo, n, arr, alo in segs:
            o_ref[s - lo, :, llo:llo + n] = refs[arr][:, alo:alo + n].astype(dtype)

    return pl.pallas_call(
        body, name=name, grid=(r // rb,),
        in_specs=[pl.BlockSpec((rb, a.shape[1]), lambda i: (i, 0)) for a in srcs],
        out_specs=pl.BlockSpec((hi - lo, rb, w), lambda i: (0, i, 0)),
        out_shape=jax.ShapeDtypeStruct((hi - lo, r, w), dtype),
        compiler_params=_params(("parallel",)),
    )(*srcs)


def _adamw(parts, w, m, v, name):
    r, c = w.shape
    tr = COPY_ROWS if r % COPY_ROWS == 0 else r

    def body(p_ref, w_ref, m_ref, v_ref, g_ref, d_ref, mo_ref, vo_ref):
        g = p_ref[0].astype(F32)
        for s in range(1, N_DEV):
            g = g + p_ref[s].astype(F32)
        g_ref[...] = g
        d_ref[...], mo_ref[...], vo_ref[...] = _adamw_math(g, w_ref[...], m_ref[...], v_ref[...])

    blk = pl.BlockSpec((tr, c), lambda i: (i, 0))
    out = jax.ShapeDtypeStruct((r, c), F32)
    return pl.pallas_call(
        body, name=name, grid=(r // tr,),
        in_specs=[pl.BlockSpec((N_DEV, tr, c), lambda i: (0, i, 0)), blk, blk, blk],
        out_specs=[blk, blk, blk, blk], out_shape=[out, out, out, out],
        compiler_params=_params(("parallel",)),
    )(parts, w, m, v)


def _adamw_cols(parts, wt, mt, vt, name):
    _, r, c = parts.shape
    per = r // LANES
    linear = wt.shape != (c, r)
    assert wt.shape == ((c * per, LANES) if linear else (c, r)) and (per == SUBLANES or not linear)
    n = min(c, LANES)
    starts = list(range(0, c - n + 1, LANES)) + ([c - n] if c % n else [])

    def body(p_ref, w_ref, m_ref, v_ref, g_ref, d_ref, mo_ref, vo_ref, gt_sc, pad_sc):
        for lo in starts:
            g = p_ref[0, :, lo:lo + n].astype(F32)
            for s in range(1, N_DEV):
                g = g + p_ref[s, :, lo:lo + n].astype(F32)
            if n < LANES:
                pad_sc[...] = jnp.zeros_like(pad_sc)
                pad_sc[:, 0:n] = g
                g = pad_sc[...]
            gt = g.T
            if linear:
                for k in range(per):
                    gt_sc[pl.ds(lo * per + k, n, stride=per), :] = gt[0:n, k * LANES:(k + 1) * LANES]
            else:
                gt_sc[lo:lo + n, :] = gt[0:n]
        g = gt_sc[...]
        d, mn, vn = _adamw_math(g, w_ref[...], m_ref[...], v_ref[...])
        for ref, val in ((g_ref, g), (d_ref, d), (mo_ref, mn), (vo_ref, vn)):
            ref[...] = val.reshape(c, 1, r) if linear else val

    out = jax.ShapeDtypeStruct((c, 1, r) if linear else (c, r), F32)
    return pl.pallas_call(
        body, name=name, out_shape=[out, out, out, out],
        scratch_shapes=[pltpu.VMEM(wt.shape, F32), pltpu.VMEM((r, LANES), F32)],
        compiler_params=pltpu.CompilerParams(vmem_limit_bytes=VMEM_LIMIT),
    )(parts, wt, mt, vt)


def _adamw_math(g, w, m, v):
    mn = ADAM_B1 * m + (1.0 - ADAM_B1) * g
    vn = ADAM_B2 * v + (1.0 - ADAM_B2) * (g * g)
    m_hat = mn / (1.0 - ADAM_B1 ** ADAM_STEP)
    v_hat = vn / (1.0 - ADAM_B2 ** ADAM_STEP)
    return -ADAM_LR * (m_hat / (jnp.sqrt(v_hat) + ADAM_EPS) + ADAM_WD * w), mn, vn


SMALL = (("ab_conv_w", 0, 4, 64), ("ssd_conv_w", 4, 4, 384), ("ssd_conv_b", 8, 1, 384), ("ssd_norm", 9, 1, 256),
         ("ssd_ln_g", 10, 1, 128), ("ssd_ln_b", 11, 1, 128))
VECS = (("ab_conv_b", 512), ("ab_gate_a_b", 512), ("ab_gate_x_b", 512), ("ab_lambda", 512), ("mla_q_norm", 256),
        ("mla_kv_norm", 128), ("ab_ln_g", 1024), ("ab_ln_b", 1024), ("ssd_dt_bias", 32), ("ssd_a_log", 32),
        ("ssd_d", 32))
GATES = ("ab_gate_a_w", "ab_gate_x_w")
SMALL_NAMES = tuple(n for n, *_ in SMALL) + tuple(n for n, _ in VECS) + GATES
VMEM_WHOLE = pl.BlockSpec(memory_space=pltpu.VMEM)


def _view2d(name, a):
    if name in GATES:
        return a.reshape(RNN_W, 64)
    return a[0] if a.ndim == 3 else a


def _unshard_small(g):
    widths = (512, 3072, 3072, 2048, 1024, 1024)

    def body(*refs):
        ins, outs = refs[:6], refs[6:]
        outs[0][...] = jnp.zeros_like(outs[0])
        outs[1][...] = jnp.zeros_like(outs[1])
        for (_, _, nr, c), i_ref, o_ref in zip(SMALL, ins, outs):
            for j in range(N_DEV):
                o_ref[0:nr, j * c:(j + 1) * c] = i_ref[j]

    return pl.pallas_call(
        body, name="unshard_small", in_specs=[VMEM_WHOLE] * 6, out_specs=[VMEM_WHOLE] * 6,
        out_shape=[jax.ShapeDtypeStruct((SUBLANES if nr == 4 else 1, w), F32) for (_, _, nr, _), w in zip(SMALL, widths)],
    )(*g)


def _prep_repl(ga, gx, dt_bias, a_log, d):
    def body(ga_ref, gx_ref, b_ref, al_ref, d_ref, wa_ref, wx_ref, b128_ref, al128_ref, dx_ref):
        wa_ref[...] = jnp.zeros_like(wa_ref)
        wx_ref[...] = jnp.zeros_like(wx_ref)
        for hd in range(8):
            hs = slice(hd * 64, (hd + 1) * 64)
            wa_ref[hs, hs] = _mx(ga_ref[hs, :])
            wx_ref[hs, hs] = _mx(gx_ref[hs, :])
        b128_ref[...] = jnp.zeros_like(b128_ref)
        al128_ref[...] = jnp.zeros_like(al128_ref)
        b128_ref[:, 0:SSD_HEADS] = b_ref[...]
        al128_ref[:, 0:SSD_HEADS] = al_ref[...]
        dv = d_ref[...]
        for hd in range(SSD_HEADS):
            dx_ref[:, hd * SSD_P:(hd + 1) * SSD_P] = jnp.broadcast_to(dv[:, hd:hd + 1], (1, SSD_P))

    return pl.pallas_call(
        body, name="prep_repl", in_specs=[VMEM_WHOLE] * 5, out_specs=[VMEM_WHOLE] * 5,
        out_shape=[jax.ShapeDtypeStruct((RNN_W, RNN_W), MXU_DTYPE), jax.ShapeDtypeStruct((RNN_W, RNN_W), MXU_DTYPE),
                   jax.ShapeDtypeStruct((1, LANES), F32), jax.ShapeDtypeStruct((1, LANES), F32),
                   jax.ShapeDtypeStruct((1, SSD_INNER), F32)],
    )(ga, gx, dt_bias, a_log, d)


LOSS_ROW = 11


def _pack_small(dvec0, g_wa, g_wx, dqnw, dknw, dgb0, dvec1, dcw1, dnw, dgb1, loss8):
    def body(dvec0_ref, gwa_ref, gwx_ref, dqn_ref, dkn_ref, dgb0_ref, dvec1_ref, dcw1_ref, dnw_ref, dgb1_ref,
             loss_ref, sm_ref, vec_ref, gg_ref):
        sm_ref[...] = jnp.zeros_like(sm_ref)
        vec_ref[...] = jnp.zeros_like(vec_ref)
        sharded = ((dvec0_ref, 4), (dcw1_ref, 0), (dcw1_ref, 4), (dnw_ref, 0), (dgb1_ref, 0), (dgb1_ref, 1))
        for (_, r0, nr, c), (src, sr) in zip(SMALL, sharded):
            for j in range(N_DEV):
                sm_ref[j, r0:r0 + nr, 0:c] = src[sr:sr + nr, j * c:(j + 1) * c]
        vectors = ((dvec0_ref, 3), (dvec0_ref, 0), (dvec0_ref, 1), (dvec0_ref, 2), (dqn_ref, 0), (dkn_ref, 0),
                   (dgb0_ref, 0), (dgb0_ref, 1), (dvec1_ref, 0), (dvec1_ref, 1), (dvec1_ref, 2))
        for row, ((_, c), (src, sr)) in enumerate(zip(VECS, vectors)):
            vec_ref[row:row + 1, 0:c] = src[sr:sr + 1, 0:c]
        vec_ref[LOSS_ROW:LOSS_ROW + 1, 0:LANES] = loss_ref[0:1, :]
        for hd in range(8):
            hs = slice(hd * 64, (hd + 1) * 64)
            gg_ref[hs, 0:64] = _mx(gwa_ref[hs, hs])
            gg_ref[hs, 64:128] = _mx(gwx_ref[hs, hs])

    return pl.pallas_call(
        body, name="pack_small", in_specs=[VMEM_WHOLE] * 11, out_specs=[VMEM_WHOLE] * 3,
        out_shape=[jax.ShapeDtypeStruct((N_DEV, 16, 384), F32), jax.ShapeDtypeStruct((16, 1024), F32),
                   jax.ShapeDtypeStruct((RNN_W, LANES), MXU_DTYPE)],
    )(dvec0, g_wa, g_wx, dqnw, dknw, dgb0, dvec1, dcw1, dnw, dgb1, loss8)


def _adamw_small(recv_sm, recv_vec, recv_gg, wmv):
    plan = ([(0, r0, nr, c) for _, r0, nr, c in SMALL] + [(1, row, 1, c) for row, (_, c) in enumerate(VECS)]
            + [(2, 0, RNN_W, 0), (2, 0, RNN_W, 64)])
    n = len(plan)

    def body(*refs):
        recv, ins, outs = refs[:3], refs[3:3 + 3 * n], refs[3 + 3 * n:]
        for i, (src, r0, nr, c) in enumerate(plan):
            cols = slice(c, c + 64) if src == 2 else slice(0, c)
            g = recv[src][0, r0:r0 + nr, cols].astype(F32)
            for s in range(1, N_DEV):
                g = g + recv[src][s, r0:r0 + nr, cols].astype(F32)
            w_ref, m_ref, v_ref = ins[3 * i:3 * i + 3]
            outs[4 * i][...] = g
            outs[4 * i + 1][...], outs[4 * i + 2][...], outs[4 * i + 3][...] = _adamw_math(
                g, w_ref[...], m_ref[...], v_ref[...])
        loss = recv[1][0, LOSS_ROW:LOSS_ROW + 1, 0:LANES]
        for s in range(1, N_DEV):
            loss = loss + recv[1][s, LOSS_ROW:LOSS_ROW + 1, 0:LANES]
        outs[4 * n][...] = loss

    flat = [a for t in wmv for a in t]
    return pl.pallas_call(
        body, name="adamw_small", in_specs=[VMEM_WHOLE] * (3 + 3 * n), out_specs=[VMEM_WHOLE] * (4 * n + 1),
        out_shape=[jax.ShapeDtypeStruct(t[0].shape, F32) for t in wmv for _ in range(4)]
        + [jax.ShapeDtypeStruct((1, LANES), F32)],
    )(recv_sm, recv_vec, recv_gg, *flat)


BIG_L0 = ("ab_w_in", "ab_w_out", "mla_w_uq", "mla_w_ukv")
BIG_L1 = ("ssd_w_in", "ssd_w_out")
COLUMN_SHARDED = ("ab_w_in", "mla_w_uq", "ssd_w_in")

MAP_W0 = ((0, 512, 0, 1024), (512, 1536, 0, 0), (1536, 1920, 0, 1536), (1920, 1952, 0, 1984))
MAP_W1 = ((0, 2048, 0, 0), (2048, 5120, 1, 0), (5120, 5152, 2, 0))
MAP_WQ = tuple((96 * hd, 96 * hd + 96, 0, 128 * hd) for hd in range(8))
MAP_WKV = (tuple((128 * hd, 128 * hd + 64, 0, 128 * hd) for hd in range(8))
           + tuple((128 * hd + 64, 128 * hd + 128, 0, 1024 + 64 * hd) for hd in range(8)))
MAP_G0 = ((0, 512, 0, 0), (512, 1536, 1, 0), (1536, 1920, 2, 0), (1920, 1952, 2, 448))
W0_EARLY, W0_LATE = (0, 6), (6, 8)


def kernel(x, positions, ab_w_in, ab_conv_w, ab_conv_b, ab_gate_a_w, ab_gate_a_b, ab_gate_x_w, ab_gate_x_b, ab_lambda, mla_q_norm, mla_kv_norm, mla_w_uq, mla_w_ukv, ab_w_out, ab_ln_g, ab_ln_b, ssd_w_in, ssd_conv_w, ssd_conv_b, ssd_dt_bias, ssd_a_log, ssd_d, ssd_norm, ssd_w_out, ssd_ln_g, ssd_ln_b, loss_target, m_ab_w_in, m_ab_conv_w, m_ab_conv_b, m_ab_gate_a_w, m_ab_gate_a_b, m_ab_gate_x_w, m_ab_gate_x_b, m_ab_lambda, m_mla_q_norm, m_mla_kv_norm, m_mla_w_uq, m_mla_w_ukv, m_ab_w_out, m_ab_ln_g, m_ab_ln_b, m_ssd_w_in, m_ssd_conv_w, m_ssd_conv_b, m_ssd_dt_bias, m_ssd_a_log, m_ssd_d, m_ssd_norm, m_ssd_w_out, m_ssd_ln_g, m_ssd_ln_b, v_ab_w_in, v_ab_conv_w, v_ab_conv_b, v_ab_gate_a_w, v_ab_gate_a_b, v_ab_gate_x_w, v_ab_gate_x_b, v_ab_lambda, v_mla_q_norm, v_mla_kv_norm, v_mla_w_uq, v_mla_w_ukv, v_ab_w_out, v_ab_ln_g, v_ab_ln_b, v_ssd_w_in, v_ssd_conv_w, v_ssd_conv_b, v_ssd_dt_bias, v_ssd_a_log, v_ssd_d, v_ssd_norm, v_ssd_w_out, v_ssd_ln_g, v_ssd_ln_b):
    args = dict(locals())
    bf = MXU_DTYPE
    big = {n: [args[pre + n][0] for pre in ("", "m_", "v_")] for n in BIG_L0 + BIG_L1}
    sml = {n: [_view2d(n, args[pre + n]) for pre in ("", "m_", "v_")] for n in SMALL_NAMES}

    w0_8, cw0_8 = _all_gather([big["ab_w_in"][0].astype(bf), sml["ab_conv_w"][0]], "gather_params")
    p = {"cw0_8": cw0_8, "l0_blocks": [big[n][0].astype(bf) for n in BIG_L0[1:]] + [sml[n][0] for n, *_ in SMALL[1:]]}
    p["w0p"], = _unshard(w0_8, MAP_W0, (2048,), "unshard_w0")
    p["wa"], p["wx"], p["dt_bias"], p["a_log"], p["d_x"] = _prep_repl(
        sml["ab_gate_a_w"][0], sml["ab_gate_x_w"][0], sml["ssd_dt_bias"][0], sml["ssd_a_log"][0], sml["ssd_d"][0])
    for key, n in (("cb0", "ab_conv_b"), ("ba", "ab_gate_a_b"), ("bx", "ab_gate_x_b"), ("lam", "ab_lambda"),
                   ("qn_w", "mla_q_norm"), ("kn_w", "mla_kv_norm"), ("g0", "ab_ln_g"), ("b0", "ab_ln_b")):
        p[key] = sml[n][0]

    _, recv_early, recv, _, grad_x = _local_step(
        x[0], positions[0], loss_target[0], p, [big[n][0].astype(bf) for n in BIG_L1])

    me = 4 * lax.axis_index("x") + 2 * lax.axis_index("y") + lax.axis_index("c")
    parts = dict(recv_early, ab_w_in=jnp.where(me >= W0_LATE[0], recv[0], recv_early["ab_w_in"]),
                 mla_w_uq=recv[1], mla_w_ukv=recv[2])

    outs = {}
    kinds = ("grad", "delta", "new_m", "new_v")
    for n in BIG_L0 + BIG_L1:
        if n in COLUMN_SHARDED:
            rows, cols = big[n][0].shape
            if rows == SUBLANES * LANES:
                wmv_t = [jnp.transpose(args[pre + n], (2, 0, 1)).reshape(cols * SUBLANES, LANES) for pre in ("", "m_", "v_")]
                back = lambda res: jnp.transpose(res, (1, 2, 0))
            else:
                wmv_t = [args[pre + n][0].T for pre in ("", "m_", "v_")]
                back = lambda res: res.T[None]
            for kind, res in zip(kinds, _adamw_cols(parts[n], *wmv_t, "adamw_" + n)):
                outs[kind, n] = back(res)
            continue
        for kind, res in zip(kinds, _adamw(parts[n], *big[n], "adamw_" + n)):
            outs[kind, n] = res[None]
    res = _adamw_small(*recv[3:], [sml[n] for n in SMALL_NAMES])
    for i, n in enumerate(SMALL_NAMES):
        for k, kind in enumerate(kinds):
            outs[kind, n] = res[4 * i + k].reshape(args[n].shape)

    loss = res[4 * len(SMALL_NAMES)][0, 0]
    order = ["ab_w_in", "ab_conv_w", "ab_conv_b", "ab_gate_a_w", "ab_gate_a_b", "ab_gate_x_w", "ab_gate_x_b",
             "ab_lambda", "mla_q_norm", "mla_kv_norm", "mla_w_uq", "mla_w_ukv", "ab_w_out", "ab_ln_g", "ab_ln_b",
             "ssd_w_in", "ssd_conv_w", "ssd_conv_b", "ssd_dt_bias", "ssd_a_log", "ssd_d", "ssd_norm", "ssd_w_out",
             "ssd_ln_g", "ssd_ln_b"]
    return (loss, grad_x[None], *[outs[kind, n] for kind in ("grad", "delta", "new_m", "new_v") for n in order])


def _local_step(x, pos, target, p, l1_blocks):
    bf = MXU_DTYPE
    inv_freq = 10000.0 ** (-jnp.arange(0, 32, 2, dtype=F32) / 32)
    ang = pos.astype(F32)[:, None] * inv_freq
    cos, sin = jnp.cos(ang), jnp.sin(ang)
    zeros = lambda n: jnp.zeros((SEQ, n), F32)
    tc = jnp.concatenate([jnp.ones((SEQ, 64), F32), cos, cos, zeros(32)], axis=1)
    tsa = jnp.concatenate([zeros(64), -sin, zeros(48)], axis=1)
    tsb = jnp.concatenate([zeros(80), sin, zeros(32)], axis=1)

    w0p, wa, wxg = (p[k] for k in ("w0p", "wa", "wx"))
    cb0, ba, bx, lam = (p[k] for k in ("cb0", "ba", "bx", "lam"))
    qn_w, kn_w, g0, b0 = (p[k] for k in ("qn_w", "kn_w", "g0", "b0"))
    dt_bias, a_log, d_x = (p[k] for k in ("dt_bias", "a_log", "d_x"))
    tril = jnp.tril(jnp.ones((SSD_L, SSD_L), F32))
    expand_t = (jnp.arange(SSD_INNER)[:, None] // SSD_P == jnp.arange(LANES)[None, :]).astype(jnp.bfloat16)

    proj0, xb, l0_8 = _l0_in(x, w0p, bcast=p["l0_blocks"])
    wo0 = l0_8[0].reshape(D_MODEL, D_MODEL)
    wq, = _unshard(l0_8[1], MAP_WQ, (1024,), "unshard_wq")
    wkv, = _unshard(l0_8[2], MAP_WKV, (1536,), "unshard_wkv")
    cw0, cw1, cb1, nw, g1, b1 = _unshard_small([p["cw0_8"]] + list(l0_8[3:]))
    xc, h = _rglru_fwd(proj0, cw0, cb0, wa, ba, wxg, bx, lam)
    qn, kn, qc, kc, vc = _mla_fwd(proj0, qn_w, kn_w, wq, wkv, tc, tsa, tsb)
    o, lse, (w1_8,) = _flash_fwd(qc, kc, vc, bcast=l1_blocks[:1])
    w1z, w1x, w1d = _unshard(w1_8, MAP_W1, (2048, 3072, 128), "unshard_w1")
    y0, v0, x1, x1b = _l0_out(h, o, proj0, x, wo0, g0, b0)

    z, dt_raw = _l1_in(x1b, w1z, w1d)
    xbc, pre, act = _ssd_conv_fwd(x1b, w1x, cw1, cb1)
    ys, hprev, (wo1_8,) = _ssd_scan_fwd(act, dt_raw, dt_bias, a_log, d_x, tril, expand_t, bcast=l1_blocks[1:])
    wo1 = wo1_8.reshape(SSD_INNER, D_MODEL)
    dv1, dgb1, loss8, g_wo1 = _l1_out(ys, z, nw, wo1, x1, g1, b1, target)

    dys, dz, dnw, g_z = _l1_gate_bwd(dv1, wo1, ys, z, nw, x1b)
    dact, ddt_raw, dvec1, g_dt, (recv_wo1,) = _ssd_scan_bwd(
        dys, act, dt_raw, hprev, dt_bias, a_log, d_x, tril, expand_t, x1b,
        scatter=[g_wo1.astype(bf).reshape(N_DEV, 256, D_MODEL)])
    dxbc, dcw1, g_xbc = _ssd_conv_bwd(dact, pre, xbc, cw1, x1b)

    dv0, dgb0 = _l1_dx_ln(dz, dxbc, ddt_raw, dv1, v0, w1z, w1x, w1d, g0)
    dh, do, dgate, g_wo0, g_gate = _gate_bwd(dv0, wo0, h, o, proj0, y0, xb)
    dxr, g_wa, g_wx, dvec0, g_rnn = _rglru_bwd(dh, xc, h, proj0, cw0, wa, ba, wxg, bx, lam, xb)
    early = [_reshard([g_z, g_xbc, g_dt], MAP_W1, 644, bf, "reshard_w1"), g_wo0.astype(bf).reshape(N_DEV, 128, D_MODEL),
             (_reshard([g_rnn, g_gate], MAP_G0, 244, bf, "reshard_w0_early", shards=W0_EARLY), W0_EARLY)]
    dq, dk, dvv, (recv_w1, recv_wo0, recv_w0) = _flash_bwd(qc, kc, vc, o, do, lse, scatter=early)
    recv_early = {"ssd_w_in": recv_w1, "ssd_w_out": recv_wo1, "ab_w_out": recv_wo0, "ab_w_in": recv_w0}
    dtail, g_wq, g_wkv, dqnw, dknw, g_tail = _mla_bwd(dq, dk, dvv, proj0, qn, kn, qn_w, kn_w, wq, wkv, tc, tsa, tsb, xb)

    acc = {"g_rnn": g_rnn, "g_gate": g_gate, "g_tail": g_tail, "g_wq": g_wq, "g_wkv": g_wkv,
           "dvec0": dvec0, "g_wa": g_wa, "g_wx": g_wx, "dqnw": dqnw, "dknw": dknw, "dgb0": dgb0, "dvec1": dvec1,
           "dcw1": dcw1, "dnw": dnw, "dgb1": dgb1}
    late = [(_reshard([g_rnn, g_gate, g_tail], MAP_G0, 244, bf, "reshard_w0_late", shards=W0_LATE), W0_LATE),
            _reshard([g_wq], MAP_WQ, 96, bf, "reshard_wq"), _reshard([g_wkv], MAP_WKV, 128, bf, "reshard_wkv")]
    sm_slots, vec_rows, gates = _pack_small(dvec0, g_wa, g_wx, dqnw, dknw, dgb0, dvec1, dcw1, dnw, dgb1, loss8)
    dx, recv_late = _l0_dx(dxr, dgate, dtail, w0p, dv0, scatter=late + [sm_slots], bcast=[vec_rows, gates])
    return acc, recv_early, recv_late, loss8[0, 0], dx
```

```python
import math

import jax
import jax.numpy as jnp
from jax import lax
from jax.experimental import pallas as pl
from jax.experimental.pallas import tpu as pltpu

F32 = jnp.float32
MXU_DTYPE = jnp.bfloat16

N_DEV = 8
SEQ = 4096
D_MODEL = 1024
DN_ALPHA = 4.0 ** 0.25
RNN_W = 512
MLA_HEADS = 8
ATT_SCALE = 96.0 ** -0.5
ATT_C = ATT_SCALE * math.log2(math.e)
RG_C = 8.0
SSD_INNER = 2048
SSD_HEADS = 32
SSD_P = 64
SSD_GROUPS = 4
SSD_N = 128
SSD_L = 128
SSD_CONV = 3072
LANES = 128
SUBLANES = 8
VMEM_LIMIT = 56 * 1024 * 1024

ADAM_LR, ADAM_B1, ADAM_B2, ADAM_EPS, ADAM_WD, ADAM_STEP = 0.001, 0.9, 0.999, 1e-08, 0.01, 10

HIGHEST = lax.Precision.HIGHEST


def _params(sem, limit=VMEM_LIMIT):
    return pltpu.CompilerParams(dimension_semantics=sem, vmem_limit_bytes=limit)


def _dot(a, b):
    return lax.dot_general(a, b, (((1,), (0,)), ((), ())), preferred_element_type=F32)


def _dot_nt(a, b):
    return lax.dot_general(a, b, (((1,), (1,)), ((), ())), preferred_element_type=F32)


def _dot_tn(a, b):
    return lax.dot_general(a, b, (((0,), (0,)), ((), ())), preferred_element_type=F32)


def _dot_hi(a, b):
    return lax.dot_general(a, b, (((1,), (0,)), ((), ())), precision=HIGHEST, preferred_element_type=F32)


def _mx(v):
    return v.astype(MXU_DTYPE)


def _sigmoid(v):
    return 1.0 / (1.0 + jnp.exp(-v))


def _log1p_pos(e):
    poly = e * (1.0 - e * (0.5 - e * (1.0 / 3.0 - e * 0.25)))
    return jnp.where(e < 0.01, poly, jnp.log(1.0 + e))


def _softplus(v):
    return jnp.maximum(v, 0.0) + _log1p_pos(jnp.exp(-jnp.abs(v)))


def _neg_expm1(v):
    poly = -v * (1.0 + v * (0.5 + v * (1.0 / 6.0 + v * (1.0 / 24.0 + v * (1.0 / 120.0)))))
    return jnp.where(jnp.abs(v) < 0.1, poly, 1.0 - jnp.exp(v))


def _silu(v):
    return v * _sigmoid(v)


def _dsilu(v):
    s = _sigmoid(v)
    return s * (1.0 + v * (1.0 - s))


def _shift_down(blk, halo, s):
    if s == 0:
        return blk
    t = blk.shape[0]
    r = pltpu.roll(blk, s, 0)
    hr = pltpu.roll(halo, s, 0)
    row8 = lax.broadcasted_iota(jnp.int32, hr.shape, 0)
    head = jnp.where(row8 < s, hr, r[:SUBLANES])
    return jnp.concatenate([head, r[SUBLANES:]], axis=0) if t > SUBLANES else head


def _shift_up(blk, halo, s):
    if s == 0:
        return blk
    t = blk.shape[0]
    r = pltpu.roll(blk, t - s, 0)
    hr = pltpu.roll(halo, SUBLANES - s, 0)
    row8 = lax.broadcasted_iota(jnp.int32, hr.shape, 0)
    tail = jnp.where(row8 >= SUBLANES - s, hr, r[t - SUBLANES:])
    return jnp.concatenate([r[:t - SUBLANES], tail], axis=0) if t > SUBLANES else tail


def _scan_down(a, u):
    t = a.shape[0]
    row = lax.broadcasted_iota(jnp.int32, a.shape, 0)
    d = 1
    while d < t:
        keep = row >= d
        a_sh = jnp.where(keep, pltpu.roll(a, d, 0), 1.0)
        u_sh = jnp.where(keep, pltpu.roll(u, d, 0), 0.0)
        u = a * u_sh + u
        a = a * a_sh
        d *= 2
    return a, u


def _scan_up(a, u):
    t = a.shape[0]
    row = lax.broadcasted_iota(jnp.int32, a.shape, 0)
    d = 1
    while d < t:
        keep = row < t - d
        a_sh = jnp.where(keep, pltpu.roll(a, t - d, 0), 1.0)
        u_sh = jnp.where(keep, pltpu.roll(u, t - d, 0), 0.0)
        u = a * u_sh + u
        a = a * a_sh
        d *= 2
    return a, u


def _conv4(blk, halo, cw, cb):
    out = cb + blk * cw[3:4]
    for k in range(3):
        out = out + _shift_down(blk, halo, 3 - k) * cw[k:k + 1]
    return out


RG_T = 512
P0_RNN = 2


def _rg_gates(xc, wa, ba, wx, bx, lam):
    xcb = _mx(xc)
    r = _sigmoid(_dot(xcb, wa) + ba)
    ig = _sigmoid(_dot(xcb, wx) + bx)
    sp = _softplus(-lam)
    la = (-RG_C * r) * sp
    a = jnp.exp(la)
    mult = jnp.sqrt(_neg_expm1(2.0 * la))
    return r, ig, sp, a, mult


def _rglru_fwd(proj0, cw8, cb, wa, ba, wx, bx, lam):
    t, w = RG_T, RNN_W
    nb = SEQ // t

    def body(x_ref, halo_ref, cw_ref, cb_ref, wa_ref, ba_ref, wx_ref, bx_ref, lam_ref, xc_ref, h_ref, carry):
        i = pl.program_id(0)

        @pl.when(i == 0)
        def _():
            carry[...] = jnp.zeros_like(carry)

        blk = x_ref[...]
        halo = jnp.where(i > 0, halo_ref[...], 0.0)
        xc = _conv4(blk, halo, cw_ref[...], cb_ref[...])
        _, ig, _, a, mult = _rg_gates(xc, wa_ref[...], ba_ref[...], wx_ref[...], bx_ref[...], lam_ref[...])
        u = mult * (ig * xc)
        big_a, big_u = _scan_down(a, u)
        h = big_a * carry[SUBLANES - 1:SUBLANES, :] + big_u
        carry[...] = h[t - SUBLANES:]
        xc_ref[...] = xc
        h_ref[...] = h

    vec = pl.BlockSpec((1, w), lambda i: (0, 0))
    mat = pl.BlockSpec((w, w), lambda i: (0, 0))
    return pl.pallas_call(
        body, name="rglru_fwd", grid=(nb,),
        in_specs=[pl.BlockSpec((t, w), lambda i: (i, P0_RNN)),
                  pl.BlockSpec((SUBLANES, w), lambda i: (jnp.maximum(i * (t // SUBLANES) - 1, 0), P0_RNN)),
                  pl.BlockSpec((SUBLANES, w), lambda i: (0, 0)), vec, mat, vec, mat, vec, vec],
        out_specs=[pl.BlockSpec((t, w), lambda i: (i, 0)), pl.BlockSpec((t, w), lambda i: (i, 0))],
        out_shape=[jax.ShapeDtypeStruct((SEQ, w), F32), jax.ShapeDtypeStruct((SEQ, w), F32)],
        scratch_shapes=[pltpu.VMEM((SUBLANES, w), F32)],
        compiler_params=_params(("arbitrary",)),
    )(proj0, proj0, cw8, cb, wa, ba, wx, bx, lam)


def _rglru_bwd(dh, xc, h, proj0, cw8, wa, ba, wx, bx, lam, xb):
    t, w = RG_T, RNN_W
    nb = SEQ // t
    tb = t // SUBLANES

    def body(dh_ref, xc_ref, h_ref, hh_ref, x_ref, cw_ref, wa_ref, ba_ref, wx_ref, bx_ref, lam_ref, xb_ref,
             dx_ref, dwa_ref, dwx_ref, dvec_ref, gw_ref, gcarry, dxc_next):
        i = pl.program_id(0)
        rev = nb - 1 - i

        @pl.when(i == 0)
        def _():
            gcarry[...] = jnp.zeros_like(gcarry)
            dxc_next[...] = jnp.zeros_like(dxc_next)
            gw_ref[...] = jnp.zeros_like(gw_ref)
            dwa_ref[...] = jnp.zeros_like(dwa_ref)
            dwx_ref[...] = jnp.zeros_like(dwx_ref)
            dvec_ref[...] = jnp.zeros_like(dvec_ref)

        xc = xc_ref[...]
        wa_v, wx_v = wa_ref[...], wx_ref[...]
        lam_v = lam_ref[...]
        r, ig, sp, a, mult = _rg_gates(xc, wa_v, ba_ref[...], wx_v, bx_ref[...], lam_v)
        dhv = dh_ref[...]
        big_a, big_u = _scan_up(a, a * dhv)
        gg = big_a * gcarry[0:1, :] + big_u
        g = dhv + _shift_up(gg, gcarry[...], 1)
        gcarry[...] = gg[:SUBLANES]
        hhalo = jnp.where(rev > 0, hh_ref[...], 0.0)
        da = g * _shift_down(h_ref[...], hhalo, 1)
        d_mult = g * (ig * xc)
        d_i = g * (mult * xc)
        dxc = g * (mult * ig)
        d_la = da * a - d_mult * (a * a) / mult
        d_r = d_la * (-RG_C * sp)
        d_sp = jnp.sum(d_la * (-RG_C * r), axis=0, keepdims=True)
        d_pa = d_r * r * (1.0 - r)
        d_px = d_i * ig * (1.0 - ig)
        d_pab, d_pxb = _mx(d_pa), _mx(d_px)
        dxc = dxc + _dot_nt(d_pab, wa_v) + _dot_nt(d_pxb, wx_v)
        xcb = _mx(xc)
        dwa_ref[...] += _dot_tn(xcb, d_pab)
        dwx_ref[...] += _dot_tn(xcb, d_pxb)
        dvec_ref[0:1, :] += jnp.sum(d_pa, axis=0, keepdims=True)
        dvec_ref[1:2, :] += jnp.sum(d_px, axis=0, keepdims=True)
        dvec_ref[2:3, :] += d_sp * (-_sigmoid(-lam_v))
        dvec_ref[3:4, :] += jnp.sum(dxc, axis=0, keepdims=True)
        xblk = x_ref[...]
        cw = cw_ref[...]
        dx = dxc * cw[3:4]
        nxt = dxc_next[...]
        dvec_ref[7:8, :] += jnp.sum(dxc * xblk, axis=0, keepdims=True)
        for k in range(3):
            up = _shift_up(dxc, nxt, 3 - k)
            dvec_ref[4 + k:5 + k, :] += jnp.sum(up * xblk, axis=0, keepdims=True)
            dx = dx + up * cw[k:k + 1]
        dxc_next[...] = dxc[:SUBLANES]
        dxb = _mx(dx)
        dx_ref[...] = dxb
        gw_ref[...] += _dot_tn(xb_ref[...], dxb)

    blk = pl.BlockSpec((t, w), lambda i: (nb - 1 - i, 0))
    halo = pl.BlockSpec((SUBLANES, w), lambda i: (jnp.maximum((nb - 1 - i) * tb - 1, 0), 0))
    vec = pl.BlockSpec((1, w), lambda i: (0, 0))
    mat = pl.BlockSpec((w, w), lambda i: (0, 0))
    return pl.pallas_call(
        body, name="rglru_bwd", grid=(nb,),
        in_specs=[blk, blk, blk, halo, pl.BlockSpec((t, w), lambda i: (nb - 1 - i, P0_RNN)),
                  pl.BlockSpec((SUBLANES, w), lambda i: (0, 0)), mat, vec, mat, vec, vec,
                  pl.BlockSpec((t, D_MODEL), lambda i: (nb - 1 - i, 0))],
        out_specs=[blk, mat, mat, pl.BlockSpec((16, w), lambda i: (0, 0)), pl.BlockSpec((D_MODEL, w), lambda i: (0, 0))],
        out_shape=[jax.ShapeDtypeStruct((SEQ, w), MXU_DTYPE), jax.ShapeDtypeStruct((w, w), F32),
                   jax.ShapeDtypeStruct((w, w), F32), jax.ShapeDtypeStruct((16, w), F32),
                   jax.ShapeDtypeStruct((D_MODEL, w), F32)],
        scratch_shapes=[pltpu.VMEM((SUBLANES, w), F32), pltpu.VMEM((SUBLANES, w), F32)],
        compiler_params=_params(("arbitrary",)),
    )(dh, xc, h, h, proj0, cw8, wa, ba, wx, bx, lam, xb)


MLA_T = 512


def _rope(v, c, sa, sb):
    return v * c + pltpu.roll(v, LANES - 16, 1) * sa + pltpu.roll(v, 16, 1) * sb


def _rope_t(dv, c, sa, sb):
    return dv * c + pltpu.roll(dv * sa, 16, 1) + pltpu.roll(dv * sb, LANES - 16, 1)


def _rms(v, g, eps=1e-6):
    rs = lax.rsqrt(jnp.mean(v * v, axis=-1, keepdims=True) + eps)
    return v * rs * g, rs


def _mla_fwd(proj0, q_norm, kv_norm, wq, wkv, tc, tsa, tsb):
    t = MLA_T

    def body(cq_ref, ck_ref, qn_ref, kn_ref, wq_ref, wkv_ref, c_ref, sa_ref, sb_ref,
             oqn_ref, okn_ref, oq_ref, ok_ref, ov_ref):
        c, sa, sb = c_ref[...].T, sa_ref[...].T, sb_ref[...].T
        ck = ck_ref[...]
        qn = _mx(_rms(cq_ref[...], qn_ref[...])[0])
        kn = _mx(_rms(ck[:, :LANES], kn_ref[...])[0])
        oqn_ref[...] = qn
        okn_ref[...] = kn
        krv = _rope(ck[:, LANES:], c, sa, sb)
        qraw = _dot(qn, wq_ref[...])
        kvraw = _dot(kn, wkv_ref[...])
        for hd in range(MLA_HEADS):
            sl = slice(hd * LANES, (hd + 1) * LANES)
            oq_ref[:, sl] = _mx(_rope(qraw[:, sl], c, sa, sb))
            ok_ref[:, sl] = _mx(kvraw[:, sl] + krv)
        ov_ref[...] = _mx(kvraw[:, 1024:])

    tab = pl.BlockSpec((t, LANES), lambda i: (i, 0))
    rot = pl.BlockSpec((LANES, t), lambda i: (0, i))
    wide = pl.BlockSpec((t, 1024), lambda i: (i, 0))
    const = lambda shape: pl.BlockSpec(shape, lambda i: (0, 0))
    return pl.pallas_call(
        body, name="mla_fwd", grid=(SEQ // t,),
        in_specs=[pl.BlockSpec((t, 256), lambda i: (i, 6)), pl.BlockSpec((t, 256), lambda i: (i, 7)),
                  const((1, 256)), const((1, LANES)), const((256, 1024)), const((LANES, 1536)), rot, rot, rot],
        out_specs=[pl.BlockSpec((t, 256), lambda i: (i, 0)), tab, wide, wide, pl.BlockSpec((t, 512), lambda i: (i, 0))],
        out_shape=[jax.ShapeDtypeStruct((SEQ, 256), MXU_DTYPE), jax.ShapeDtypeStruct((SEQ, LANES), MXU_DTYPE),
                   jax.ShapeDtypeStruct((SEQ, 1024), MXU_DTYPE), jax.ShapeDtypeStruct((SEQ, 1024), MXU_DTYPE),
                   jax.ShapeDtypeStruct((SEQ, 512), MXU_DTYPE)],
        compiler_params=_params(("parallel",)),
    )(proj0, proj0, q_norm, kv_norm, wq, wkv, tc, tsa, tsb)


ATT_T = 1024


def _flash_fwd(q, k, v, bcast=()):
    t = ATT_T
    nb = SEQ // t

    steps = [(qi, ki) for qi in range(nb) for ki in range(qi + 1)]
    qi_tab = jnp.asarray([s[0] for s in steps], jnp.int32)
    ki_tab = jnp.asarray([s[1] for s in steps], jnp.int32)

    nx = len(bcast)

    def body(qi_ref, ki_ref, q_ref, k_ref, v_ref, *rest):
        x_refs, (o_ref, lse_ref), g_refs = rest[:nx], rest[nx:nx + 2], rest[nx + 2:2 * nx + 2]
        m_sc, acc_sc = rest[2 * nx + 2:2 * nx + 4]
        step = pl.program_id(1)
        qi, ki = qi_ref[step], ki_ref[step]
        if nx:
            copies = _peer_copies(x_refs, g_refs, rest[2 * nx + 4:], [])

            @pl.when((pl.program_id(0) == 0) & (step == 0))
            def _():
                for cp in copies:
                    cp.start()

        @pl.when(ki == 0)
        def _():
            m_sc[...] = jnp.full_like(m_sc, -jnp.inf)
            acc_sc[...] = jnp.zeros_like(acc_sc)

        def update(diagonal):
            vv = v_ref[...]
            lane_v = lax.broadcasted_iota(jnp.int32, vv.shape, 1)
            for hd in range(2):
                sl = slice(hd * LANES, (hd + 1) * LANES)
                st = _dot_nt(k_ref[:, sl], q_ref[:, sl])
                if diagonal:
                    st = jnp.where(lax.broadcasted_iota(jnp.int32, (t, t), 0)
                                   <= lax.broadcasted_iota(jnp.int32, (t, t), 1), st, -jnp.inf)
                m_prev = m_sc[hd:hd + 1, :]
                m_new = jnp.maximum(m_prev, jnp.max(st, axis=0, keepdims=True))
                pt = jnp.exp2((st - m_new) * ATT_C)
                m_sc[hd:hd + 1, :] = m_new
                vh = jnp.where((lane_v >= hd * 64) & (lane_v < (hd + 1) * 64), vv, jnp.ones_like(vv))
                acc_sc[hd] = acc_sc[hd] * jnp.exp2((m_prev - m_new) * ATT_C) + _dot_tn(vh, _mx(pt))

        @pl.when(ki < qi)
        def _():
            update(False)

        @pl.when(ki == qi)
        def _():
            update(True)
            a0, a1 = acc_sc[0], acc_sc[1]
            l0, l1 = a0[64:65, :], a1[0:1, :]
            first = lax.broadcasted_iota(jnp.int32, (LANES, t), 0) < 64
            o_ref[...] = jnp.where(first, a0 / l0, a1 / l1).T
            lse_ref[0, 0:1, :] = m_sc[0:1, :] * ATT_SCALE + jnp.log(l0)
            lse_ref[0, 1:2, :] = m_sc[1:2, :] * ATT_SCALE + jnp.log(l1)
            lse_ref[0, 2:SUBLANES, :] = jnp.zeros((SUBLANES - 2, t), F32)

        if nx:
            @pl.when((pl.program_id(0) == 3) & (step == len(steps) - 1))
            def _():
                for cp in copies:
                    cp.wait()

    grid_spec = pltpu.PrefetchScalarGridSpec(
        num_scalar_prefetch=2, grid=(4, len(steps)),
        in_specs=[pl.BlockSpec((t, 256), lambda p, s, qt, kt: (qt[s], p)),
                  pl.BlockSpec((t, 256), lambda p, s, qt, kt: (kt[s], p)),
                  pl.BlockSpec((t, LANES), lambda p, s, qt, kt: (kt[s], p))] + [ANY] * nx,
        out_specs=[pl.BlockSpec((t, LANES), lambda p, s, qt, kt: (qt[s], p)),
                   pl.BlockSpec((1, SUBLANES, t), lambda p, s, qt, kt: (p, 0, qt[s]))] + [ANY] * nx,
        scratch_shapes=[pltpu.VMEM((SUBLANES, t), F32), pltpu.VMEM((2, LANES, t), F32)]
        + (_exchange_sems(nx) if nx else []))
    res = pl.pallas_call(
        body, name="flash_fwd", grid_spec=grid_spec,
        out_shape=[jax.ShapeDtypeStruct((SEQ, 512), F32), jax.ShapeDtypeStruct((4, SUBLANES, SEQ), F32)]
        + _exchange_shapes([], bcast),
        compiler_params=_params(("arbitrary", "arbitrary")),
    )(qi_tab, ki_tab, q, k, v, *bcast)
    return res[0], res[1], res[2:]


def _flash_bwd(q, k, v, o, do, lse, scatter=()):
    t = ATT_T
    nb = SEQ // t

    steps = [(qi, ki) for ki in range(nb) for qi in range(ki, nb)]
    qi_tab = jnp.asarray([s[0] for s in steps], jnp.int32)
    ki_tab = jnp.asarray([s[1] for s in steps], jnp.int32)
    log2e = math.log2(math.e)

    sc_arrays, sc_ranges = _scatter_args(scatter)
    nx = len(sc_arrays)

    def body(qi_ref, ki_ref, q_ref, k_ref, v_ref, o_ref, do_ref, lse_ref, *rest):
        x_refs, (dq_ref, dk_ref, dv_ref), g_refs = rest[:nx], rest[nx:nx + 3], rest[nx + 3:2 * nx + 3]
        dkt_sc, dvt_sc = rest[2 * nx + 3:2 * nx + 5]
        step = pl.program_id(1)
        qi, ki = qi_ref[step], ki_ref[step]
        if nx:
            copies = _peer_copies(x_refs, g_refs, rest[2 * nx + 5:], sc_ranges)

            @pl.when((pl.program_id(0) == 0) & (step == 0))
            def _():
                for cp in copies:
                    cp.start()

        @pl.when(step == 0)
        def _():
            dq_ref[...] = jnp.zeros_like(dq_ref)

        @pl.when(qi == ki)
        def _():
            dkt_sc[...] = jnp.zeros_like(dkt_sc)
            dvt_sc[...] = jnp.zeros_like(dvt_sc)

        def update(diagonal):
            dov, ov, vv = do_ref[...], o_ref[...], v_ref[...]
            lse2 = (lse_ref[0] * log2e).T
            lane = lax.broadcasted_iota(jnp.int32, (t, LANES), 1)
            row_t = lax.broadcasted_iota(jnp.int32, (LANES, t), 0)
            prod = dov * ov
            do_b = _mx(dov)
            qrows = pl.ds(pl.multiple_of(qi * t, t), t)
            dvt_acc = jnp.zeros((LANES, t), F32)
            dkt_new, dq_new = [], []
            for hd in range(2):
                sl = slice(hd * LANES, (hd + 1) * LANES)
                mine = (lane >= hd * 64) & (lane < (hd + 1) * 64)
                qh, kh = q_ref[:, sl], k_ref[:, sl]
                p = jnp.exp2(_dot_nt(qh, kh) * ATT_C - lse2[:, hd:hd + 1])
                if diagonal:
                    p = jnp.where(lax.broadcasted_iota(jnp.int32, (t, t), 1)
                                  <= lax.broadcasted_iota(jnp.int32, (t, t), 0), p, 0.0)
                do_h = jnp.where(mine, dov, 0.0)
                delta = jnp.sum(jnp.where(mine, prod, 0.0), axis=1, keepdims=True)
                dp = _dot_nt(_mx(do_h), vv)
                ds = _mx(p * (dp - delta) * ATT_SCALE)
                dvt_acc = dvt_acc + jnp.where((row_t >= hd * 64) & (row_t < (hd + 1) * 64), _dot_tn(do_b, _mx(p)), 0.0)
                dkt_new.append(_dot_tn(qh, ds))
                dq_new.append(_dot(ds, kh))
            for hd in range(2):
                sl = slice(hd * LANES, (hd + 1) * LANES)
                dkt_sc[sl, :] += dkt_new[hd]
                dq_ref[qrows, sl] += dq_new[hd]
            dvt_sc[...] += dvt_acc

        @pl.when(qi > ki)
        def _():
            update(False)

        @pl.when(qi == ki)
        def _():
            update(True)

        @pl.when(qi == nb - 1)
        def _():
            dk_ref[...] = dkt_sc[...].T
            dv_ref[...] = dvt_sc[...].T

        if nx:
            @pl.when((pl.program_id(0) == 3) & (step == len(steps) - 1))
            def _():
                for cp in copies:
                    cp.wait()

    qmap = lambda p, s, qt, kt: (qt[s], p)
    kmap = lambda p, s, qt, kt: (kt[s], p)
    grid_spec = pltpu.PrefetchScalarGridSpec(
        num_scalar_prefetch=2, grid=(4, len(steps)),
        in_specs=[pl.BlockSpec((t, 256), qmap), pl.BlockSpec((t, 256), kmap), pl.BlockSpec((t, LANES), kmap),
                  pl.BlockSpec((t, LANES), qmap), pl.BlockSpec((t, LANES), qmap),
                  pl.BlockSpec((1, SUBLANES, t), lambda p, s, qt, kt: (p, 0, qt[s]))] + [ANY] * nx,
        out_specs=[pl.BlockSpec((SEQ, 256), lambda p, s, qt, kt: (0, p)), pl.BlockSpec((t, 256), kmap),
                   pl.BlockSpec((t, LANES), kmap)] + [ANY] * nx,
        scratch_shapes=[pltpu.VMEM((256, t), F32), pltpu.VMEM((LANES, t), F32)] + (_exchange_sems(nx) if nx else []))
    res = pl.pallas_call(
        body, name="flash_bwd", grid_spec=grid_spec,
        out_shape=[jax.ShapeDtypeStruct((SEQ, 1024), F32), jax.ShapeDtypeStruct((SEQ, 1024), F32),
                   jax.ShapeDtypeStruct((SEQ, 512), F32)] + _exchange_shapes(sc_arrays, []),
        compiler_params=_params(("arbitrary", "arbitrary")),
    )(qi_tab, ki_tab, q, k, v, o, do, lse, *sc_arrays)
    return res[0], res[1], res[2], res[3:]


def _rms_bwd(v, g, dy, eps=1e-6):
    rs = lax.rsqrt(jnp.mean(v * v, axis=-1, keepdims=True) + eps)
    xh = v * rs
    dxh = dy * g
    dv = rs * (dxh - xh * jnp.mean(dxh * xh, axis=-1, keepdims=True))
    return dv, jnp.sum(dy * xh, axis=0, keepdims=True)


def _mla_bwd(dq, dk, dv, proj0, qlat, klat, q_norm, kv_norm, wq, wkv, tc, tsa, tsb, xb):
    t = MLA_T

    def body(dq_ref, dk_ref, dv_ref, cq_ref, ck_ref, ql_ref, kl_ref, qn_ref, kn_ref, wq_ref, wkv_ref,
             c_ref, sa_ref, sb_ref, xb_ref, o_ref, gwq_ref, gwkv_ref, dgq_ref, dgk_ref, gwt_ref, oq_ref, okv_ref):
        @pl.when(pl.program_id(0) == 0)
        def _():
            dgq_ref[...] = jnp.zeros_like(dgq_ref)
            dgk_ref[...] = jnp.zeros_like(dgk_ref)
            gwq_ref[...] = jnp.zeros_like(gwq_ref)
            gwkv_ref[...] = jnp.zeros_like(gwkv_ref)
            gwt_ref[...] = jnp.zeros_like(gwt_ref)

        c, sa, sb = c_ref[...].T, sa_ref[...].T, sb_ref[...].T
        lane = lax.broadcasted_iota(jnp.int32, (t, LANES), 1)
        dkr = jnp.zeros((t, LANES), F32)
        for hd in range(MLA_HEADS):
            sl = slice(hd * LANES, (hd + 1) * LANES)
            oq_ref[:, sl] = _mx(_rope_t(dq_ref[:, sl], c, sa, sb))
            dkh = dk_ref[:, sl]
            okv_ref[:, sl] = _mx(dkh)
            dkr = dkr + dkh
        okv_ref[:, 1024:] = _mx(dv_ref[...])
        dkr = _rope_t(jnp.where((lane >= 64) & (lane < 96), dkr, 0.0), c, sa, sb)
        dqraw, dkvraw = oq_ref[...], okv_ref[...]
        gwq_ref[...] += _dot_tn(ql_ref[...], dqraw)
        gwkv_ref[...] += _dot_tn(kl_ref[...], dkvraw)
        dqn = _dot_nt(dqraw, wq_ref[...])
        dkn = _dot_nt(dkvraw, wkv_ref[...])
        dcq, dgq = _rms_bwd(cq_ref[...], qn_ref[...], dqn)
        dck, dgk = _rms_bwd(ck_ref[:, :LANES], kn_ref[...], dkn)
        o_ref[:, :256] = _mx(dcq)
        o_ref[:, 256:384] = _mx(dck)
        o_ref[:, 384:] = _mx(dkr)
        gwt_ref[...] += _dot_tn(xb_ref[...], o_ref[...])
        dgq_ref[0:1, :] += dgq
        dgk_ref[0:1, :] += dgk

    tab = pl.BlockSpec((t, LANES), lambda i: (i, 0))
    rot = pl.BlockSpec((LANES, t), lambda i: (0, i))
    wide = pl.BlockSpec((t, 1024), lambda i: (i, 0))
    const = lambda shape: pl.BlockSpec(shape, lambda i: (0, 0))
    return pl.pallas_call(
        body, name="mla_bwd", grid=(SEQ // t,),
        in_specs=[wide, wide, pl.BlockSpec((t, 512), lambda i: (i, 0)),
                  pl.BlockSpec((t, 256), lambda i: (i, 6)), pl.BlockSpec((t, 256), lambda i: (i, 7)),
                  pl.BlockSpec((t, 256), lambda i: (i, 0)), tab,
                  const((1, 256)), const((1, LANES)), const((256, 1024)), const((LANES, 1536)), rot, rot, rot, wide],
        out_specs=[pl.BlockSpec((t, 512), lambda i: (i, 0)), const((256, 1024)), const((LANES, 1536)),
                   const((SUBLANES, 256)), const((SUBLANES, LANES)), const((D_MODEL, 512))],
        out_shape=[jax.ShapeDtypeStruct((SEQ, 512), MXU_DTYPE), jax.ShapeDtypeStruct((256, 1024), F32),
                   jax.ShapeDtypeStruct((LANES, 1536), F32), jax.ShapeDtypeStruct((SUBLANES, 256), F32),
                   jax.ShapeDtypeStruct((SUBLANES, LANES), F32), jax.ShapeDtypeStruct((D_MODEL, 512), F32)],
        scratch_shapes=[pltpu.VMEM((t, 1024), MXU_DTYPE), pltpu.VMEM((t, 1536), MXU_DTYPE)],
        compiler_params=_params(("arbitrary",)),
    )(dq, dk, dv, proj0, proj0, qlat, klat, q_norm, kv_norm, wq, wkv, tc, tsa, tsb, xb)


LN_T = 512


def _ln(v, g, b, eps=1e-5):
    mu = jnp.mean(v, axis=-1, keepdims=True)
    xc = v - mu
    rs = lax.rsqrt(jnp.mean(xc * xc, axis=-1, keepdims=True) + eps)
    return xc * rs * g + b


def _ln_bwd(v, g, dy, eps=1e-5):
    mu = jnp.mean(v, axis=-1, keepdims=True)
    xc = v - mu
    rs = lax.rsqrt(jnp.mean(xc * xc, axis=-1, keepdims=True) + eps)
    xh = xc * rs
    dxh = dy * g
    dv = rs * (dxh - jnp.mean(dxh, axis=-1, keepdims=True) - xh * jnp.mean(dxh * xh, axis=-1, keepdims=True))
    return dv, jnp.sum(dy * xh, axis=0, keepdims=True), jnp.sum(dy, axis=0, keepdims=True)


def _l0_out(h, o, proj0, x, w_out, g, b):
    t = LN_T

    def body(h_ref, o_ref, ga_ref, gb_ref, x_ref, w_ref, g_ref, b_ref, y_ref, v_ref, x1_ref, x1b_ref):
        y = _mx(jnp.concatenate([h_ref[...] * _silu(ga_ref[...]), o_ref[...] * _silu(gb_ref[...])], axis=1))
        v = DN_ALPHA * x_ref[...] + _dot(y, w_ref[...])
        y_ref[...] = y
        v_ref[...] = v
        x1 = _ln(v, g_ref[...], b_ref[...])
        x1_ref[...] = x1
        x1b_ref[...] = _mx(x1)

    half = pl.BlockSpec((t, 512), lambda i: (i, 0))
    full = pl.BlockSpec((t, D_MODEL), lambda i: (i, 0))
    vec = pl.BlockSpec((1, D_MODEL), lambda i: (0, 0))
    return pl.pallas_call(
        body, name="l0_out", grid=(SEQ // t,),
        in_specs=[half, half, pl.BlockSpec((t, 512), lambda i: (i, 0)), pl.BlockSpec((t, 512), lambda i: (i, 1)), full,
                  pl.BlockSpec((D_MODEL, D_MODEL), lambda i: (0, 0)), vec, vec],
        out_specs=[full, full, full, full],
        out_shape=[jax.ShapeDtypeStruct((SEQ, D_MODEL), MXU_DTYPE), jax.ShapeDtypeStruct((SEQ, D_MODEL), F32),
                   jax.ShapeDtypeStruct((SEQ, D_MODEL), F32), jax.ShapeDtypeStruct((SEQ, D_MODEL), MXU_DTYPE)],
        compiler_params=_params(("parallel",)),
    )(h, o, proj0, proj0, x, w_out, g, b)


def _l1_in(x1b, w1z, w1d):
    t = 1024

    def body(x_ref, wz_ref, wd_ref, z_ref, dt_ref):
        xv = x_ref[...]
        z_ref[...] = _dot(xv, wz_ref[...])
        dt_ref[...] = _dot(xv, wd_ref[...])

    rows = lambda w: pl.BlockSpec((t, w), lambda i: (i, 0))
    const = lambda w: pl.BlockSpec((D_MODEL, w), lambda i: (0, 0))
    return pl.pallas_call(
        body, name="l1_in", grid=(SEQ // t,),
        in_specs=[rows(D_MODEL), const(SSD_INNER), const(LANES)],
        out_specs=[rows(SSD_INNER), rows(LANES)],
        out_shape=[jax.ShapeDtypeStruct((SEQ, SSD_INNER), F32), jax.ShapeDtypeStruct((SEQ, LANES), F32)],
        compiler_params=_params(("parallel",)),
    )(x1b, w1z, w1d)


def _l1_dx_ln(dz, dxbc, ddt, dv1, v0, w1z, w1x, w1d, g):
    t = LN_T

    def body(dz_ref, dx_ref, ddt_ref, dv1_ref, v_ref, wz_ref, wx_ref, wd_ref, g_ref, dv_ref, dgb_ref):
        @pl.when(pl.program_id(0) == 0)
        def _():
            dgb_ref[...] = jnp.zeros_like(dgb_ref)

        dy = (DN_ALPHA * dv1_ref[...] + _dot_nt(dz_ref[...], wz_ref[...]) + _dot_nt(dx_ref[...], wx_ref[...])
              + _dot_nt(_mx(ddt_ref[...]), wd_ref[...]))
        dv, dg, db = _ln_bwd(v_ref[...], g_ref[...], dy)
        dv_ref[...] = dv
        dgb_ref[0:1, :] += dg
        dgb_ref[1:2, :] += db

    rows = lambda w: pl.BlockSpec((t, w), lambda i: (i, 0))
    const = lambda w: pl.BlockSpec((D_MODEL, w), lambda i: (0, 0))
    return pl.pallas_call(
        body, name="l1_dx_ln", grid=(SEQ // t,),
        in_specs=[rows(SSD_INNER), rows(SSD_CONV), rows(LANES), rows(D_MODEL), rows(D_MODEL),
                  const(SSD_INNER), const(SSD_CONV), const(LANES), pl.BlockSpec((1, D_MODEL), lambda i: (0, 0))],
        out_specs=[rows(D_MODEL), pl.BlockSpec((SUBLANES, D_MODEL), lambda i: (0, 0))],
        out_shape=[jax.ShapeDtypeStruct((SEQ, D_MODEL), F32), jax.ShapeDtypeStruct((SUBLANES, D_MODEL), F32)],
        compiler_params=_params(("arbitrary",)),
    )(dz, dxbc, ddt, dv1, v0, w1z, w1x, w1d, g)


def _gate_bwd(dv0, w_out, h, o, proj0, y0, xb):
    t = LN_T

    def body(dv_ref, w_ref, h_ref, o_ref, ga_ref, gb_ref, y0_ref, xb_ref, dh_ref, do_ref, dg_ref, gwo_ref, gwg_ref):
        @pl.when(pl.program_id(0) == 0)
        def _():
            gwo_ref[...] = jnp.zeros_like(gwo_ref)
            gwg_ref[...] = jnp.zeros_like(gwg_ref)

        dvb = _mx(dv_ref[...])
        dy = _dot_nt(dvb, w_ref[...])
        ga, gb, dya, dyb = ga_ref[...], gb_ref[...], dy[:, :512], dy[:, 512:]
        dh_ref[...] = dya * _silu(ga)
        do_ref[...] = dyb * _silu(gb)
        dg_ref[:, :512] = _mx(dya * h_ref[...] * _dsilu(ga))
        dg_ref[:, 512:] = _mx(dyb * o_ref[...] * _dsilu(gb))
        gwo_ref[...] += _dot_tn(y0_ref[...], dvb)
        gwg_ref[...] += _dot_tn(xb_ref[...], dg_ref[...])

    half = pl.BlockSpec((t, 512), lambda i: (i, 0))
    half1 = pl.BlockSpec((t, 512), lambda i: (i, 1))
    full = pl.BlockSpec((t, 1024), lambda i: (i, 0))
    square = pl.BlockSpec((D_MODEL, D_MODEL), lambda i: (0, 0))
    return pl.pallas_call(
        body, name="gate_bwd", grid=(SEQ // t,),
        in_specs=[full, square, half, half, half, half1, full, full],
        out_specs=[half, half, full, square, square],
        out_shape=[jax.ShapeDtypeStruct((SEQ, 512), F32), jax.ShapeDtypeStruct((SEQ, 512), F32),
                   jax.ShapeDtypeStruct((SEQ, 1024), MXU_DTYPE), jax.ShapeDtypeStruct((D_MODEL, D_MODEL), F32),
                   jax.ShapeDtypeStruct((D_MODEL, D_MODEL), F32)],
        compiler_params=_params(("arbitrary",)),
    )(dv0, w_out, h, o, proj0, proj0, y0, xb)


CONV_T = 1024
CONV_CB = 1024


def _ssd_conv_fwd(x1b, w1x, cw8, cb):
    t, cbk = CONV_T, CONV_CB

    def body(x_ref, w_ref, cw_ref, cb_ref, xbc_ref, pre_ref, act_ref, carry):
        xbc = _dot(x_ref[...], w_ref[...])
        halo = jnp.where(pl.program_id(1) > 0, carry[...], 0.0)
        pre = _conv4(xbc, halo, cw_ref[...], cb_ref[...])
        carry[...] = xbc[t - SUBLANES:]
        xbc_ref[...] = xbc
        pre_ref[...] = pre
        act_ref[...] = _silu(pre)

    blk = pl.BlockSpec((t, cbk), lambda j, i: (i, j))
    out = jax.ShapeDtypeStruct((SEQ, SSD_CONV), F32)
    return pl.pallas_call(
        body, name="ssd_conv_fwd", grid=(SSD_CONV // cbk, SEQ // t),
        in_specs=[pl.BlockSpec((t, D_MODEL), lambda j, i: (i, 0)), pl.BlockSpec((D_MODEL, cbk), lambda j, i: (0, j)),
                  pl.BlockSpec((SUBLANES, cbk), lambda j, i: (0, j)), pl.BlockSpec((1, cbk), lambda j, i: (0, j))],
        out_specs=[blk, blk, blk], out_shape=[out, out, out],
        scratch_shapes=[pltpu.VMEM((SUBLANES, cbk), F32)],
        compiler_params=_params(("parallel", "arbitrary")),
    )(x1b, w1x, cw8, cb)


def _ssd_conv_bwd(dact, pre, xbc, cw8, x1b):
    t, cbk = CONV_T, CONV_CB
    tb = t // SUBLANES
    nb = SEQ // t

    def body(da_ref, dan_ref, pre_ref, pren_ref, x_ref, cw_ref, x1_ref, dx_ref, dcw_ref, gw_ref):
        i = pl.program_id(1)

        @pl.when(i == 0)
        def _():
            dcw_ref[...] = jnp.zeros_like(dcw_ref)
            gw_ref[...] = jnp.zeros_like(gw_ref)

        dpre = da_ref[...] * _dsilu(pre_ref[...])
        dpre_next = jnp.where(i < nb - 1, dan_ref[...] * _dsilu(pren_ref[...]), 0.0)
        xblk = x_ref[...]
        cw = cw_ref[...]
        dx = dpre * cw[3:4]
        dcw_ref[3:4, :] += jnp.sum(dpre * xblk, axis=0, keepdims=True)
        for k in range(3):
            up = _shift_up(dpre, dpre_next, 3 - k)
            dcw_ref[k:k + 1, :] += jnp.sum(up * xblk, axis=0, keepdims=True)
            dx = dx + up * cw[k:k + 1]
        dcw_ref[4:5, :] += jnp.sum(dpre, axis=0, keepdims=True)
        dxb = _mx(dx)
        dx_ref[...] = dxb
        gw_ref[...] += _dot_tn(x1_ref[...], dxb)

    blk = pl.BlockSpec((t, cbk), lambda j, i: (i, j))
    nxt = pl.BlockSpec((SUBLANES, cbk), lambda j, i: (jnp.minimum((i + 1) * tb, SEQ // SUBLANES - 1), j))
    acc = pl.BlockSpec((SUBLANES, cbk), lambda j, i: (0, j))
    return pl.pallas_call(
        body, name="ssd_conv_bwd", grid=(SSD_CONV // cbk, nb),
        in_specs=[blk, nxt, blk, nxt, blk, acc, pl.BlockSpec((t, D_MODEL), lambda j, i: (i, 0))],
        out_specs=[blk, acc, pl.BlockSpec((D_MODEL, cbk), lambda j, i: (0, j))],
        out_shape=[jax.ShapeDtypeStruct((SEQ, SSD_CONV), MXU_DTYPE), jax.ShapeDtypeStruct((SUBLANES, SSD_CONV), F32),
                   jax.ShapeDtypeStruct((D_MODEL, SSD_CONV), F32)],
        compiler_params=_params(("parallel", "arbitrary")),
    )(dact, dact, pre, pre, xbc, cw8, x1b)


def _ssd_common(dt_raw, bias, alog, tril, expand_t, xs):
    lane = lax.broadcasted_iota(jnp.int32, dt_raw.shape, 1)
    dt = jnp.where(lane < SSD_HEADS, _softplus(dt_raw + bias), 0.0)
    a_neg = -jnp.exp(alog)
    cs = _dot_hi(tril, dt * a_neg)
    dt_x = _expand_heads(dt, expand_t)
    ecs_x = _expand_heads(jnp.exp(cs), expand_t)
    ds_x = _expand_heads(jnp.exp(cs[SSD_L - 1:SSD_L, :] - cs), expand_t)
    return dt, a_neg, cs, dt_x, None, xs * dt_x, ds_x, ecs_x, ecs_x[SSD_L - 1:SSD_L, :]


def _expand_heads(v, expand_t):
    hi = v.astype(jnp.bfloat16)
    lo = (v - hi.astype(F32)).astype(jnp.bfloat16)
    return _dot_nt(hi, expand_t) + _dot_nt(lo, expand_t)


def _fold_heads(v, expand_t):
    hi = v.astype(jnp.bfloat16)
    lo = (v - hi.astype(F32)).astype(jnp.bfloat16)
    return _dot(hi, expand_t) + _dot(lo, expand_t)


def _ssd_decay(cs, cs_t, hh, causal):
    seg = cs[:, hh:hh + 1] - cs_t[hh:hh + 1, :]
    return jnp.where(causal, jnp.exp(jnp.where(causal, seg, 0.0)), 0.0)


def _ssd_scan_fwd(act, dt_raw, bias, alog, d_x, tril, expand_t, bcast=()):
    nc = SEQ // SSD_L
    gw = SSD_INNER // SSD_GROUPS
    n = len(bcast)

    def body(act_ref, dt_ref, bias_ref, alog_ref, dx_ref, tril_ref, et_ref, *rest):
        y_ref, hp_ref, h_sc = rest[n], rest[n + 1], rest[2 * n + 2]
        if n:
            copies = _peer_copies(rest[:n], rest[n + 2:2 * n + 2], rest[2 * n + 3:], [])

            @pl.when(pl.program_id(0) == 0)
            def _():
                for cp in copies:
                    cp.start()

            @pl.when(pl.program_id(0) == nc - 1)
            def _():
                for cp in copies:
                    cp.wait()

        @pl.when(pl.program_id(0) == 0)
        def _():
            h_sc[...] = jnp.zeros_like(h_sc)

        xs = act_ref[:, :SSD_INNER]
        _, _, cs, _, _, xdt, ds_x, ecs_x, elast = _ssd_common(
            dt_ref[...], bias_ref[...], alog_ref[...], tril_ref[...], et_ref[...], xs)
        cs_t = cs.T
        causal = (lax.broadcasted_iota(jnp.int32, (SSD_L, SSD_L), 0)
                  >= lax.broadcasted_iota(jnp.int32, (SSD_L, SSD_L), 1))
        lane = lax.broadcasted_iota(jnp.int32, (SSD_L, LANES), 1)
        xdt_b = _mx(xdt)
        xds_b = _mx(xdt * ds_x)
        hp_ref[0] = h_sc[...]
        for g in range(SSD_GROUPS):
            gs = slice(g * gw, (g + 1) * gw)
            bg = _mx(act_ref[:, SSD_INNER + g * SSD_N:SSD_INNER + (g + 1) * SSD_N])
            cg = _mx(act_ref[:, SSD_INNER + 512 + g * SSD_N:SSD_INNER + 512 + (g + 1) * SSD_N])
            cb = _dot_nt(cg, bg)
            hprev = h_sc[:, gs]
            yoff = _dot(cg, _mx(hprev)) * ecs_x[:, gs]
            h_sc[:, gs] = hprev * elast[:, gs] + _dot_tn(bg, xds_b[:, gs])
            for pr in range(4):
                ps = slice(g * gw + pr * LANES, g * gw + (pr + 1) * LANES)
                xp = xdt_b[:, ps]
                ydiag = jnp.zeros((SSD_L, LANES), F32)
                for j in range(2):
                    dm = _ssd_decay(cs, cs_t, g * 8 + pr * 2 + j, causal)
                    mine = (lane >= j * 64) & (lane < (j + 1) * 64)
                    ydiag = ydiag + _dot(_mx(cb * dm), jnp.where(mine, xp, jnp.zeros_like(xp)))
                y_ref[:, ps] = ydiag + yoff[:, pr * LANES:(pr + 1) * LANES] + dx_ref[:, ps] * xs[:, ps]

    const = lambda shape: pl.BlockSpec(shape, lambda c: (0, 0))
    res = pl.pallas_call(
        body, name="ssd_scan_fwd", grid=(nc,),
        in_specs=[pl.BlockSpec((SSD_L, SSD_CONV), lambda c: (c, 0)), pl.BlockSpec((SSD_L, LANES), lambda c: (c, 0)),
                  const((1, LANES)), const((1, LANES)), const((1, SSD_INNER)), const((SSD_L, SSD_L)),
                  const((SSD_INNER, LANES))] + [ANY] * n,
        out_specs=[pl.BlockSpec((SSD_L, SSD_INNER), lambda c: (c, 0)),
                   pl.BlockSpec((1, SSD_N, SSD_INNER), lambda c: (c, 0, 0))] + [ANY] * n,
        out_shape=[jax.ShapeDtypeStruct((SEQ, SSD_INNER), F32), jax.ShapeDtypeStruct((nc, SSD_N, SSD_INNER), F32)]
        + _exchange_shapes([], bcast),
        scratch_shapes=[pltpu.VMEM((SSD_N, SSD_INNER), F32)] + (_exchange_sems(n) if n else []),
        compiler_params=_params(("arbitrary",)),
    )(act, dt_raw, bias, alog, d_x, tril, expand_t, *bcast)
    return res[0], res[1], res[2:]


def _ssd_scan_bwd(dy, act, dt_raw, hprev_all, bias, alog, d_x, tril, expand_t, x1b, scatter=()):
    nc = SEQ // SSD_L
    gw = SSD_INNER // SSD_GROUPS
    sc_arrays, sc_ranges = _scatter_args(scatter)
    nx = len(sc_arrays)

    def body(dy_ref, act_ref, dt_ref, hp_ref, bias_ref, alog_ref, dx_ref, tril_ref, et_ref, x1_ref, *rest):
        dact_ref, ddt_ref, dvec_ref, gdt_ref = rest[nx:nx + 4]
        dh_sc, dd_sc, dcs_sc, dcst_sc = rest[2 * nx + 4:2 * nx + 8]
        i = pl.program_id(0)
        if nx:
            copies = _peer_copies(rest[:nx], rest[nx + 4:2 * nx + 4], rest[2 * nx + 8:], sc_ranges)

            @pl.when(i == 0)
            def _():
                for cp in copies:
                    cp.start()

        @pl.when(i == 0)
        def _():
            dh_sc[...] = jnp.zeros_like(dh_sc)
            dd_sc[...] = jnp.zeros_like(dd_sc)
            gdt_ref[...] = jnp.zeros_like(gdt_ref)
            dvec_ref[...] = jnp.zeros_like(dvec_ref)

        xs = act_ref[:, :SSD_INNER]
        dt_raw_v, bias_v = dt_ref[...], bias_ref[...]
        dt, a_neg, cs, dt_x, _, xdt, ds_x, ecs_x, elast = _ssd_common(
            dt_raw_v, bias_v, alog_ref[...], tril_ref[...], et_ref[...], xs)
        cs_t = cs.T
        rowi = lax.broadcasted_iota(jnp.int32, (SSD_L, SSD_L), 0)
        coli = lax.broadcasted_iota(jnp.int32, (SSD_L, SSD_L), 1)
        causal = rowi >= coli
        lane = lax.broadcasted_iota(jnp.int32, (SSD_L, LANES), 1)
        row_g = lax.broadcasted_iota(jnp.int32, (SSD_L, gw), 0)
        dyv = dy_ref[...]
        dd_sc[0:1, :] += jnp.sum(dyv * xs, axis=0, keepdims=True)
        xdt_b = _mx(xdt)
        xds = xdt * ds_x
        xds_b = _mx(xds)
        dy_b = _mx(dyv)
        dye_b = _mx(dyv * ecs_x)
        dcs_sc[...] = jnp.zeros_like(dcs_sc)
        dcst_sc[...] = jnp.zeros_like(dcst_sc)
        dcs_parts = []
        dxdt_parts = []
        for g in range(SSD_GROUPS):
            gs = slice(g * gw, (g + 1) * gw)
            bcol = slice(SSD_INNER + g * SSD_N, SSD_INNER + (g + 1) * SSD_N)
            ccol = slice(SSD_INNER + 512 + g * SSD_N, SSD_INNER + 512 + (g + 1) * SSD_N)
            bg, cg = _mx(act_ref[:, bcol]), _mx(act_ref[:, ccol])
            cb = _dot_nt(cg, bg)
            hp = hp_ref[0, :, gs]
            hp_b = _mx(hp)
            dh = dh_sc[:, gs]
            dh_b = _mx(dh)
            yoff = _dot(cg, hp_b) * ecs_x[:, gs]
            bdh = _dot(bg, dh_b)
            tt = xds[:, gs] * bdh
            last_row = (jnp.sum(tt, axis=0, keepdims=True)
                        + jnp.sum(dh * hp, axis=0, keepdims=True) * elast[:, gs])
            dcs_parts.append(dyv[:, gs] * yoff - tt + jnp.where(row_g == SSD_L - 1, last_row, 0.0))
            dc_g = _dot_nt(dye_b[:, gs], hp_b)
            db_g = _dot_nt(xds_b[:, gs], dh_b)
            dh_sc[:, gs] = _dot_tn(cg, dye_b[:, gs]) + dh * elast[:, gs]
            wsum = jnp.zeros((SSD_L, SSD_L), F32)
            dxdt_g = []
            for pr in range(4):
                ps = slice(g * gw + pr * LANES, g * gw + (pr + 1) * LANES)
                xp, dyp = xdt_b[:, ps], dy_b[:, ps]
                dxp = jnp.zeros((SSD_L, LANES), F32)
                for j in range(2):
                    hh = g * 8 + pr * 2 + j
                    dm = _ssd_decay(cs, cs_t, hh, causal)
                    mine = (lane >= j * 64) & (lane < (j + 1) * 64)
                    dy_h = jnp.where(mine, dyp, jnp.zeros_like(dyp))
                    wd = _dot_nt(dy_h, xp) * dm
                    wsum = wsum + wd
                    gmat = wd * cb
                    dcs_sc[:, hh:hh + 1] = jnp.sum(gmat, axis=1, keepdims=True)
                    dcst_sc[hh:hh + 1, :] = -jnp.sum(gmat, axis=0, keepdims=True)
                    dxp = dxp + _dot_tn(_mx(cb * dm), dy_h)
                dxdt_g.append(dxp)
            dxdt_parts.append(jnp.concatenate(dxdt_g, axis=1) + bdh * ds_x[:, gs])
            ws_b = _mx(wsum)
            dact_ref[:, ccol] = dc_g + _dot(ws_b, bg)
            dact_ref[:, bcol] = db_g + _dot_tn(ws_b, cg)
        dxdt = jnp.concatenate(dxdt_parts, axis=1)
        dcs_x = jnp.concatenate(dcs_parts, axis=1)
        et = et_ref[...]
        dcs_tot = dcs_sc[...] + dcst_sc[...].T + _fold_heads(dcs_x, et)
        da_dt = _dot_hi((coli >= rowi).astype(F32), dcs_tot)
        ddt = da_dt * a_neg + _fold_heads(dxdt * xs, et)
        ddt_raw = ddt * _sigmoid(dt_raw_v + bias_v)
        ddt_ref[...] = ddt_raw
        gdt_ref[...] += _dot_tn(x1_ref[...], _mx(ddt_raw))
        dvec_ref[0:1, :] += jnp.sum(ddt_raw, axis=0, keepdims=True)
        dvec_ref[1:2, :] += jnp.sum(da_dt * dt, axis=0, keepdims=True) * a_neg
        dact_ref[:, :SSD_INNER] = dyv * dx_ref[...] + dxdt * dt_x

        @pl.when(i == nc - 1)
        def _():
            dvec_ref[2:3, :] = _fold_heads(dd_sc[...], et)[0:1, :]
            if nx:
                for cp in copies:
                    cp.wait()

    const = lambda shape: pl.BlockSpec(shape, lambda c: (0, 0))
    rev = lambda c: (nc - 1 - c, 0)
    res = pl.pallas_call(
        body, name="ssd_scan_bwd", grid=(nc,),
        in_specs=[pl.BlockSpec((SSD_L, SSD_INNER), rev), pl.BlockSpec((SSD_L, SSD_CONV), rev),
                  pl.BlockSpec((SSD_L, LANES), rev),
                  pl.BlockSpec((1, SSD_N, SSD_INNER), lambda c: (nc - 1 - c, 0, 0)),
                  const((1, LANES)), const((1, LANES)), const((1, SSD_INNER)), const((SSD_L, SSD_L)),
                  const((SSD_INNER, LANES)), pl.BlockSpec((SSD_L, D_MODEL), rev)] + [ANY] * nx,
        out_specs=[pl.BlockSpec((SSD_L, SSD_CONV), rev), pl.BlockSpec((SSD_L, LANES), rev), const((SUBLANES, LANES)),
                   const((D_MODEL, LANES))] + [ANY] * nx,
        out_shape=[jax.ShapeDtypeStruct((SEQ, SSD_CONV), F32), jax.ShapeDtypeStruct((SEQ, LANES), F32),
                   jax.ShapeDtypeStruct((SUBLANES, LANES), F32), jax.ShapeDtypeStruct((D_MODEL, LANES), F32)]
        + _exchange_shapes(sc_arrays, []),
        scratch_shapes=[pltpu.VMEM((SSD_N, SSD_INNER), F32), pltpu.VMEM((SUBLANES, SSD_INNER), F32),
                        pltpu.VMEM((SSD_L, LANES), F32), pltpu.VMEM((LANES, SSD_L), F32)]
        + (_exchange_sems(nx) if nx else []),
        compiler_params=_params(("arbitrary",)),
    )(dy, act, dt_raw, hprev_all, bias, alog, d_x, tril, expand_t, x1b, *sc_arrays)
    return res[0], res[1], res[2], res[3], res[4:]


L1_T = 512


def _resident(shape):
    return pl.BlockSpec(shape, lambda i: (0, 0), pipeline_mode=pl.Buffered(1))


def _gated_norm(y, z, nw):
    y2 = y * _silu(z)
    gw = SSD_INNER // SSD_GROUPS
    outs, xhs, rss = [], [], []
    for g in range(SSD_GROUPS):
        gs = slice(g * gw, (g + 1) * gw)
        v = y2[:, gs]
        rs = lax.rsqrt(jnp.mean(v * v, axis=-1, keepdims=True) + 1e-6)
        xhs.append(v * rs)
        rss.append(rs)
        outs.append(v * rs * nw[:, gs])
    return outs, xhs, rss


def _l1_out(y, z, nw, w_out, x1, g, b, target):
    t = L1_T

    def body(y_ref, z_ref, nw_ref, w_ref, x1_ref, g_ref, b_ref, tg_ref, dv_ref, dgb_ref, loss_ref, gw_ref, gw_sc):
        @pl.when(pl.program_id(0) == 0)
        def _():
            dgb_ref[...] = jnp.zeros_like(dgb_ref)
            loss_ref[...] = jnp.zeros_like(loss_ref)
            gw_sc[...] = jnp.zeros_like(gw_sc)

        outs, _, _ = _gated_norm(y_ref[...], z_ref[...], nw_ref[...])
        yn = _mx(jnp.concatenate(outs, axis=1))
        v = DN_ALPHA * x1_ref[...] + _dot(yn, w_ref[...])
        gv = g_ref[...]
        err = _ln(v, gv, b_ref[...]) - tg_ref[...]
        rowsum = jnp.sum(err * err, axis=1, keepdims=True)
        loss_ref[...] += 0.5 * jnp.sum(rowsum, axis=0, keepdims=True) / D_MODEL
        dv, dg, db = _ln_bwd(v, gv, err / D_MODEL)
        dv_ref[...] = dv
        dgb_ref[0:1, :] += dg
        dgb_ref[1:2, :] += db
        gw_sc[...] += _dot_tn(yn, _mx(dv))

        @pl.when(pl.program_id(0) == SEQ // t - 1)
        def _():
            gw_ref[...] = _mx(gw_sc[...])

    wide = pl.BlockSpec((t, SSD_INNER), lambda i: (i, 0))
    full = pl.BlockSpec((t, D_MODEL), lambda i: (i, 0))
    vec = pl.BlockSpec((1, D_MODEL), lambda i: (0, 0))
    return pl.pallas_call(
        body, name="l1_out", grid=(SEQ // t,),
        in_specs=[wide, wide, pl.BlockSpec((1, SSD_INNER), lambda i: (0, 0)),
                  _resident((SSD_INNER, D_MODEL)), full, vec, vec, full],
        out_specs=[full, pl.BlockSpec((SUBLANES, D_MODEL), lambda i: (0, 0)),
                   pl.BlockSpec((SUBLANES, LANES), lambda i: (0, 0)), _resident((SSD_INNER, D_MODEL))],
        out_shape=[jax.ShapeDtypeStruct((SEQ, D_MODEL), F32), jax.ShapeDtypeStruct((SUBLANES, D_MODEL), F32),
                   jax.ShapeDtypeStruct((SUBLANES, LANES), F32), jax.ShapeDtypeStruct((SSD_INNER, D_MODEL), MXU_DTYPE)],
        scratch_shapes=[pltpu.VMEM((SSD_INNER, D_MODEL), F32)],
        compiler_params=_params(("arbitrary",)),
    )(y, z, nw, w_out, x1, g, b, target)


def _l1_gate_bwd(dv1, w_out, y, z, nw, x1b):
    t = L1_T
    gw = SSD_INNER // SSD_GROUPS

    def body(dv_ref, w_ref, y_ref, z_ref, nw_ref, x1_ref, dy_ref, dz_ref, dnw_ref, gw_ref):
        @pl.when(pl.program_id(0) == 0)
        def _():
            dnw_ref[...] = jnp.zeros_like(dnw_ref)
            gw_ref[...] = jnp.zeros_like(gw_ref)

        dyn = _dot_nt(_mx(dv_ref[...]), w_ref[...])
        yv, zv, nwv = y_ref[...], z_ref[...], nw_ref[...]
        _, xhs, rss = _gated_norm(yv, zv, nwv)
        sz, dsz = _silu(zv), _dsilu(zv)
        for g in range(SSD_GROUPS):
            gs = slice(g * gw, (g + 1) * gw)
            d_out = dyn[:, gs]
            xh = xhs[g]
            dnw_ref[0:1, gs] += jnp.sum(d_out * xh, axis=0, keepdims=True)
            dxh = d_out * nwv[:, gs]
            dy2 = rss[g] * (dxh - xh * jnp.mean(dxh * xh, axis=-1, keepdims=True))
            dy_ref[:, gs] = dy2 * sz[:, gs]
            dz_ref[:, gs] = _mx(dy2 * yv[:, gs] * dsz[:, gs])
        gw_ref[...] += _dot_tn(x1_ref[...], dz_ref[...])

    wide = pl.BlockSpec((t, SSD_INNER), lambda i: (i, 0))
    return pl.pallas_call(
        body, name="l1_gate_bwd", grid=(SEQ // t,),
        in_specs=[pl.BlockSpec((t, D_MODEL), lambda i: (i, 0)), _resident((SSD_INNER, D_MODEL)),
                  wide, wide, pl.BlockSpec((1, SSD_INNER), lambda i: (0, 0)), pl.BlockSpec((t, D_MODEL), lambda i: (i, 0))],
        out_specs=[wide, wide, pl.BlockSpec((SUBLANES, SSD_INNER), lambda i: (0, 0)),
                   _resident((D_MODEL, SSD_INNER))],
        out_shape=[jax.ShapeDtypeStruct((SEQ, SSD_INNER), F32), jax.ShapeDtypeStruct((SEQ, SSD_INNER), MXU_DTYPE),
                   jax.ShapeDtypeStruct((SUBLANES, SSD_INNER), F32), jax.ShapeDtypeStruct((D_MODEL, SSD_INNER), F32)],
        compiler_params=_params(("arbitrary",)),
    )(dv1, w_out, y, z, nw, x1b)


MESH = pl.DeviceIdType.MESH
ANY = pl.BlockSpec(memory_space=pl.ANY)


def _flip(v, bit):
    return 1 - v if bit else v


def _all_gather(blocks, name):
    n = len(blocks)

    def body(*refs):
        x_refs, out_refs = refs[:n], refs[n:2 * n]
        send_sems, recv_sems, local_sems = refs[2 * n:]
        mx, my, mc = lax.axis_index("x"), lax.axis_index("y"), lax.axis_index("c")
        me, sibling = (mx, my, mc), (mx, my, 1 - mc)
        chips = [(1 - mx, my), (mx, 1 - my), (1 - mx, 1 - my)]

        def copy(a, k, block, to, own=False):
            px, py, pc = block
            slot = out_refs[a].at[4 * px + 2 * py + pc]
            return pltpu.make_async_remote_copy(
                src_ref=x_refs[a] if own else slot, dst_ref=slot,
                send_sem=send_sems.at[7 * a + k], recv_sem=recv_sems.at[7 * a + k], device_id=to, device_id_type=MESH)

        mine = [pltpu.make_async_copy(x_refs[a], out_refs[a].at[4 * mx + 2 * my + mc], local_sems.at[a])
                for a in range(n)]
        first = []
        for a in range(n):
            mine[a].start()
            first.append(copy(a, 0, me, sibling, own=True))
            first += [copy(a, 1 + j, me, (*chip, mc), own=True) for j, chip in enumerate(chips)]
        for cp in first:
            cp.start()
        passed = []
        for j, chip in enumerate(chips):
            for a in range(n):
                copy(a, 1 + j, (*chip, mc), me).wait_recv()
                fwd = copy(a, 4 + j, (*chip, mc), sibling)
                fwd.start()
                passed.append(fwd)
        for a in range(n):
            copy(a, 0, sibling, me).wait_recv()
            for j, chip in enumerate(chips):
                copy(a, 4 + j, (*chip, 1 - mc), me).wait_recv()
        for cp in first + passed:
            cp.wait_send()
        for cp in mine:
            cp.wait()

    return pl.pallas_call(
        body, name=name, in_specs=[ANY] * n, out_specs=[ANY] * n,
        out_shape=[jax.ShapeDtypeStruct((N_DEV,) + b.shape, b.dtype) for b in blocks],
        scratch_shapes=[pltpu.SemaphoreType.DMA((7 * n,)), pltpu.SemaphoreType.DMA((7 * n,)),
                        pltpu.SemaphoreType.DMA((n,))],
    )(*blocks)


def _l0_in(x, w0p, bcast=()):
    n = len(bcast)
    tm, tn = 1024, 1024
    gi, gj = SEQ // tm, 2048 // tn

    def body(x_ref, w_ref, *rest):
        o_ref, xb_ref = rest[n], rest[n + 1]
        i, j = pl.program_id(0), pl.program_id(1)
        if n:
            copies = _peer_copies(rest[:n], rest[n + 2:2 * n + 2], rest[2 * n + 2:], [])

            @pl.when((i == 0) & (j == 0))
            def _():
                for cp in copies:
                    cp.start()

        xb = _mx(x_ref[...])
        xb_ref[...] = xb
        o_ref[...] = _dot(xb, w_ref[...])

        if n:
            @pl.when((i == gi - 1) & (j == gj - 1))
            def _():
                for cp in copies:
                    cp.wait()

    res = pl.pallas_call(
        body, name="l0_in", grid=(gi, gj),
        in_specs=[pl.BlockSpec((tm, D_MODEL), lambda i, j: (i, 0)), pl.BlockSpec((D_MODEL, tn), lambda i, j: (0, j))]
        + [ANY] * n,
        out_specs=[pl.BlockSpec((tm, tn), lambda i, j: (i, j)), pl.BlockSpec((tm, D_MODEL), lambda i, j: (i, 0))]
        + [ANY] * n,
        out_shape=[jax.ShapeDtypeStruct((SEQ, 2048), F32), jax.ShapeDtypeStruct((SEQ, D_MODEL), MXU_DTYPE)]
        + _exchange_shapes([], bcast),
        scratch_shapes=_exchange_sems(n) if n else [],
        compiler_params=_params(("arbitrary", "arbitrary")),
    )(x, w0p, *bcast)
    return res[0], res[1], res[2:]


def _l0_dx(dxr, dgate, dtail, w0p, dv0, scatter=(), bcast=()):
    arrays, ranges = _scatter_args(scatter)
    n = len(arrays) + len(bcast)
    tm = 1024
    steps = SEQ // tm

    def body(dxr_ref, dg_ref, dt_ref, w_ref, dv_ref, *rest):
        o_ref = rest[n]
        i = pl.program_id(0)
        if n:
            copies = _peer_copies(rest[:n], rest[n + 1:2 * n + 1], rest[2 * n + 1:], ranges)

            @pl.when(i == 0)
            def _():
                for cp in copies:
                    cp.start()

        o_ref[...] = (DN_ALPHA * dv_ref[...] + _dot_nt(dg_ref[...], w_ref[:, 0:1024])
                      + _dot_nt(dxr_ref[...], w_ref[:, 1024:1536]) + _dot_nt(dt_ref[...], w_ref[:, 1536:2048]))

        if n:
            @pl.when(i == steps - 1)
            def _():
                for cp in copies:
                    cp.wait()

    rows = lambda w: pl.BlockSpec((tm, w), lambda i: (i, 0))
    res = pl.pallas_call(
        body, name="l0_dx", grid=(steps,),
        in_specs=[rows(512), rows(1024), rows(512), pl.BlockSpec((D_MODEL, 2048), lambda i: (0, 0)), rows(D_MODEL)]
        + [ANY] * n,
        out_specs=[rows(D_MODEL)] + [ANY] * n,
        out_shape=[jax.ShapeDtypeStruct((SEQ, D_MODEL), F32)] + _exchange_shapes(arrays, bcast),
        scratch_shapes=_exchange_sems(n) if n else [],
        compiler_params=_params(("arbitrary",)),
    )(dxr, dgate, dtail, w0p, dv0, *arrays, *bcast)
    return res[0], res[1:]


def _scatter_args(scatter):
    arrays = [s[0] if isinstance(s, tuple) else s for s in scatter]
    ranges = [s[1] if isinstance(s, tuple) else (0, N_DEV) for s in scatter]
    return arrays, ranges


def _exchange_shapes(scatter, bcast):
    return ([jax.ShapeDtypeStruct((N_DEV,) + a.shape[1:], a.dtype) for a in scatter]
            + [jax.ShapeDtypeStruct((N_DEV,) + a.shape, a.dtype) for a in bcast])


def _exchange_sems(n):
    return [pltpu.SemaphoreType.DMA((7 * n,)), pltpu.SemaphoreType.DMA((7 * n,)), pltpu.SemaphoreType.DMA((n,))]


class _GuardedCopy:
    def __init__(self, copy, send=None, recv=None, local=False):
        self.copy, self.send, self.recv, self.local = copy, send, recv, local

    @staticmethod
    def _run(pred, fn):
        if pred is None:
            fn()
        else:
            pl.when(pred)(fn)

    def start(self):
        self._run(self.send, self.copy.start)

    def wait(self):
        if self.local:
            self._run(self.send, self.copy.wait)
        else:
            self._run(self.send, self.copy.wait_send)
            self._run(self.recv, self.copy.wait_recv)


def _peer_copies(in_refs, out_refs, sems, ranges):
    send_sems, recv_sems, local_sems = sems
    n, ns = len(in_refs), len(ranges)
    mx, my, mc = lax.axis_index("x"), lax.axis_index("y"), lax.axis_index("c")
    me = 4 * mx + 2 * my + mc

    def src(a, slot):
        return in_refs[a].at[slot - ranges[a][0]] if a < ns else in_refs[a]

    def member(a, dev):
        if a >= ns or ranges[a] == (0, N_DEV):
            return None
        return (dev >= ranges[a][0]) & (dev < ranges[a][1])

    copies = [_GuardedCopy(pltpu.make_async_copy(src(a, me), out_refs[a].at[me], local_sems.at[a]),
                           send=member(a, me), local=True) for a in range(n)]
    for k in range(1, N_DEV):
        px, py, pc = _flip(mx, (k >> 2) & 1), _flip(my, (k >> 1) & 1), _flip(mc, k & 1)
        peer = 4 * px + 2 * py + pc
        for a in range(n):
            copies.append(_GuardedCopy(pltpu.make_async_remote_copy(
                src_ref=src(a, peer), dst_ref=out_refs[a].at[me],
                send_sem=send_sems.at[7 * a + k - 1], recv_sem=recv_sems.at[7 * a + k - 1],
                device_id=(px, py, pc), device_id_type=MESH), send=member(a, peer), recv=member(a, me)))
    return copies


def _segments(col_map, width):
    segs = []
    for lo, hi, arr, alo in col_map:
        for s in range(N_DEV):
            a, b = max(lo, s * width), min(hi, (s + 1) * width)
            if a < b:
                segs.append((s, a - s * width, b - a, arr, alo + a - lo))
    return segs


COPY_ROWS = 256


def _unshard(g8, col_map, widths, name):
    _, r, w = g8.shape
    rb = min(r, COPY_ROWS)
    segs = _segments(col_map, w)

    def body(g_ref, *o_refs):
        for o_ref in o_refs:
            o_ref[...] = jnp.zeros_like(o_ref)
        for s, llo, n, arr, alo in segs:
            o_refs[arr][:, alo:alo + n] = g_ref[s, :, llo:llo + n]

    return pl.pallas_call(
        body, name=name, grid=(r // rb,),
        in_specs=[pl.BlockSpec((N_DEV, rb, w), lambda i: (0, i, 0))],
        out_specs=[pl.BlockSpec((rb, n), lambda i: (i, 0)) for n in widths],
        out_shape=[jax.ShapeDtypeStruct((r, n), g8.dtype) for n in widths],
        compiler_params=_params(("parallel",)),
    )(g8)


def _reshard(srcs, col_map, w, dtype, name, shards=(0, N_DEV)):
    r = srcs[0].shape[0]
    rb = min(r, COPY_ROWS)
    lo, hi = shards
    segs = [sg for sg in _segments(col_map, w) if lo <= sg[0] < hi]

    def body(*refs):
        o_ref = refs[-1]
        for s, llo, n, arr, alo in segs:
            o_ref[s - lo, :, llo:llo + n] = refs[arr][:, alo:alo + n].astype(dtype)

    return pl.pallas_call(
        body, name=name, grid=(r // rb,),
        in_specs=[pl.BlockSpec((rb, a.shape[1]), lambda i: (i, 0)) for a in srcs],
        out_specs=pl.BlockSpec((hi - lo, rb, w), lambda i: (0, i, 0)),
        out_shape=jax.ShapeDtypeStruct((hi - lo, r, w), dtype),
        compiler_params=_params(("parallel",)),
    )(*srcs)


def _adamw(parts, w, m, v, name):
    r, c = w.shape
    tr = COPY_ROWS if r % COPY_ROWS == 0 else r

    def body(p_ref, w_ref, m_ref, v_ref, g_ref, d_ref, mo_ref, vo_ref):
        g = p_ref[0].astype(F32)
        for s in range(1, N_DEV):
            g = g + p_ref[s].astype(F32)
        g_ref[...] = g
        d_ref[...], mo_ref[...], vo_ref[...] = _adamw_math(g, w_ref[...], m_ref[...], v_ref[...])

    blk = pl.BlockSpec((tr, c), lambda i: (i, 0))
    out = jax.ShapeDtypeStruct((r, c), F32)
    return pl.pallas_call(
        body, name=name, grid=(r // tr,),
        in_specs=[pl.BlockSpec((N_DEV, tr, c), lambda i: (0, i, 0)), blk, blk, blk],
        out_specs=[blk, blk, blk, blk], out_shape=[out, out, out, out],
        compiler_params=_params(("parallel",)),
    )(parts, w, m, v)


def _adamw_cols(parts, wt, mt, vt, name):
    _, r, c = parts.shape
    per = r // LANES
    linear = wt.shape != (c, r)
    assert wt.shape == ((c * per, LANES) if linear else (c, r)) and (per == SUBLANES or not linear)
    n = min(c, LANES)
    starts = list(range(0, c - n + 1, LANES)) + ([c - n] if c % n else [])

    def body(p_ref, w_ref, m_ref, v_ref, g_ref, d_ref, mo_ref, vo_ref, gt_sc, pad_sc):
        for lo in starts:
            g = p_ref[0, :, lo:lo + n].astype(F32)
            for s in range(1, N_DEV):
                g = g + p_ref[s, :, lo:lo + n].astype(F32)
            if n < LANES:
                pad_sc[...] = jnp.zeros_like(pad_sc)
                pad_sc[:, 0:n] = g
                g = pad_sc[...]
            gt = g.T
            if linear:
                for k in range(per):
                    gt_sc[pl.ds(lo * per + k, n, stride=per), :] = gt[0:n, k * LANES:(k + 1) * LANES]
            else:
                gt_sc[lo:lo + n, :] = gt[0:n]
        g = gt_sc[...]
        d, mn, vn = _adamw_math(g, w_ref[...], m_ref[...], v_ref[...])
        for ref, val in ((g_ref, g), (d_ref, d), (mo_ref, mn), (vo_ref, vn)):
            ref[...] = val.reshape(c, 1, r) if linear else val

    out = jax.ShapeDtypeStruct((c, 1, r) if linear else (c, r), F32)
    return pl.pallas_call(
        body, name=name, out_shape=[out, out, out, out],
        scratch_shapes=[pltpu.VMEM(wt.shape, F32), pltpu.VMEM((r, LANES), F32)],
        compiler_params=pltpu.CompilerParams(vmem_limit_bytes=VMEM_LIMIT),
    )(parts, wt, mt, vt)


def _adamw_math(g, w, m, v):
    mn = ADAM_B1 * m + (1.0 - ADAM_B1) * g
    vn = ADAM_B2 * v + (1.0 - ADAM_B2) * (g * g)
    m_hat = mn / (1.0 - ADAM_B1 ** ADAM_STEP)
    v_hat = vn / (1.0 - ADAM_B2 ** ADAM_STEP)
    return -ADAM_LR * (m_hat / (jnp.sqrt(v_hat) + ADAM_EPS) + ADAM_WD * w), mn, vn


SMALL = (("ab_conv_w", 0, 4, 64), ("ssd_conv_w", 4, 4, 384), ("ssd_conv_b", 8, 1, 384), ("ssd_norm", 9, 1, 256),
         ("ssd_ln_g", 10, 1, 128), ("ssd_ln_b", 11, 1, 128))
VECS = (("ab_conv_b", 512), ("ab_gate_a_b", 512), ("ab_gate_x_b", 512), ("ab_lambda", 512), ("mla_q_norm", 256),
        ("mla_kv_norm", 128), ("ab_ln_g", 1024), ("ab_ln_b", 1024), ("ssd_dt_bias", 32), ("ssd_a_log", 32),
        ("ssd_d", 32))
GATES = ("ab_gate_a_w", "ab_gate_x_w")
SMALL_NAMES = tuple(n for n, *_ in SMALL) + tuple(n for n, _ in VECS) + GATES
VMEM_WHOLE = pl.BlockSpec(memory_space=pltpu.VMEM)


def _view2d(name, a):
    if name in GATES:
        return a.reshape(RNN_W, 64)
    return a[0] if a.ndim == 3 else a


def _unshard_small(g):
    widths = (512, 3072, 3072, 2048, 1024, 1024)

    def body(*refs):
        ins, outs = refs[:6], refs[6:]
        outs[0][...] = jnp.zeros_like(outs[0])
        outs[1][...] = jnp.zeros_like(outs[1])
        for (_, _, nr, c), i_ref, o_ref in zip(SMALL, ins, outs):
            for j in range(N_DEV):
                o_ref[0:nr, j * c:(j + 1) * c] = i_ref[j]

    return pl.pallas_call(
        body, name="unshard_small", in_specs=[VMEM_WHOLE] * 6, out_specs=[VMEM_WHOLE] * 6,
        out_shape=[jax.ShapeDtypeStruct((SUBLANES if nr == 4 else 1, w), F32) for (_, _, nr, _), w in zip(SMALL, widths)],
    )(*g)


def _prep_repl(ga, gx, dt_bias, a_log, d):
    def body(ga_ref, gx_ref, b_ref, al_ref, d_ref, wa_ref, wx_ref, b128_ref, al128_ref, dx_ref):
        wa_ref[...] = jnp.zeros_like(wa_ref)
        wx_ref[...] = jnp.zeros_like(wx_ref)
        for hd in range(8):
            hs = slice(hd * 64, (hd + 1) * 64)
            wa_ref[hs, hs] = _mx(ga_ref[hs, :])
            wx_ref[hs, hs] = _mx(gx_ref[hs, :])
        b128_ref[...] = jnp.zeros_like(b128_ref)
        al128_ref[...] = jnp.zeros_like(al128_ref)
        b128_ref[:, 0:SSD_HEADS] = b_ref[...]
        al128_ref[:, 0:SSD_HEADS] = al_ref[...]
        dv = d_ref[...]
        for hd in range(SSD_HEADS):
            dx_ref[:, hd * SSD_P:(hd + 1) * SSD_P] = jnp.broadcast_to(dv[:, hd:hd + 1], (1, SSD_P))

    return pl.pallas_call(
        body, name="prep_repl", in_specs=[VMEM_WHOLE] * 5, out_specs=[VMEM_WHOLE] * 5,
        out_shape=[jax.ShapeDtypeStruct((RNN_W, RNN_W), MXU_DTYPE), jax.ShapeDtypeStruct((RNN_W, RNN_W), MXU_DTYPE),
                   jax.ShapeDtypeStruct((1, LANES), F32), jax.ShapeDtypeStruct((1, LANES), F32),
                   jax.ShapeDtypeStruct((1, SSD_INNER), F32)],
    )(ga, gx, dt_bias, a_log, d)


LOSS_ROW = 11


def _pack_small(dvec0, g_wa, g_wx, dqnw, dknw, dgb0, dvec1, dcw1, dnw, dgb1, loss8):
    def body(dvec0_ref, gwa_ref, gwx_ref, dqn_ref, dkn_ref, dgb0_ref, dvec1_ref, dcw1_ref, dnw_ref, dgb1_ref,
             loss_ref, sm_ref, vec_ref, gg_ref):
        sm_ref[...] = jnp.zeros_like(sm_ref)
        vec_ref[...] = jnp.zeros_like(vec_ref)
        sharded = ((dvec0_ref, 4), (dcw1_ref, 0), (dcw1_ref, 4), (dnw_ref, 0), (dgb1_ref, 0), (dgb1_ref, 1))
        for (_, r0, nr, c), (src, sr) in zip(SMALL, sharded):
            for j in range(N_DEV):
                sm_ref[j, r0:r0 + nr, 0:c] = src[sr:sr + nr, j * c:(j + 1) * c]
        vectors = ((dvec0_ref, 3), (dvec0_ref, 0), (dvec0_ref, 1), (dvec0_ref, 2), (dqn_ref, 0), (dkn_ref, 0),
                   (dgb0_ref, 0), (dgb0_ref, 1), (dvec1_ref, 0), (dvec1_ref, 1), (dvec1_ref, 2))
        for row, ((_, c), (src, sr)) in enumerate(zip(VECS, vectors)):
            vec_ref[row:row + 1, 0:c] = src[sr:sr + 1, 0:c]
        vec_ref[LOSS_ROW:LOSS_ROW + 1, 0:LANES] = loss_ref[0:1, :]
        for hd in range(8):
            hs = slice(hd * 64, (hd + 1) * 64)
            gg_ref[hs, 0:64] = _mx(gwa_ref[hs, hs])
            gg_ref[hs, 64:128] = _mx(gwx_ref[hs, hs])

    return pl.pallas_call(
        body, name="pack_small", in_specs=[VMEM_WHOLE] * 11, out_specs=[VMEM_WHOLE] * 3,
        out_shape=[jax.ShapeDtypeStruct((N_DEV, 16, 384), F32), jax.ShapeDtypeStruct((16, 1024), F32),
                   jax.ShapeDtypeStruct((RNN_W, LANES), MXU_DTYPE)],
    )(dvec0, g_wa, g_wx, dqnw, dknw, dgb0, dvec1, dcw1, dnw, dgb1, loss8)


def _adamw_small(recv_sm, recv_vec, recv_gg, wmv):
    plan = ([(0, r0, nr, c) for _, r0, nr, c in SMALL] + [(1, row, 1, c) for row, (_, c) in enumerate(VECS)]
            + [(2, 0, RNN_W, 0), (2, 0, RNN_W, 64)])
    n = len(plan)

    def body(*refs):
        recv, ins, outs = refs[:3], refs[3:3 + 3 * n], refs[3 + 3 * n:]
        for i, (src, r0, nr, c) in enumerate(plan):
            cols = slice(c, c + 64) if src == 2 else slice(0, c)
            g = recv[src][0, r0:r0 + nr, cols].astype(F32)
            for s in range(1, N_DEV):
                g = g + recv[src][s, r0:r0 + nr, cols].astype(F32)
            w_ref, m_ref, v_ref = ins[3 * i:3 * i + 3]
            outs[4 * i][...] = g
            outs[4 * i + 1][...], outs[4 * i + 2][...], outs[4 * i + 3][...] = _adamw_math(
                g, w_ref[...], m_ref[...], v_ref[...])
        loss = recv[1][0, LOSS_ROW:LOSS_ROW + 1, 0:LANES]
        for s in range(1, N_DEV):
            loss = loss + recv[1][s, LOSS_ROW:LOSS_ROW + 1, 0:LANES]
        outs[4 * n][...] = loss

    flat = [a for t in wmv for a in t]
    return pl.pallas_call(
        body, name="adamw_small", in_specs=[VMEM_WHOLE] * (3 + 3 * n), out_specs=[VMEM_WHOLE] * (4 * n + 1),
        out_shape=[jax.ShapeDtypeStruct(t[0].shape, F32) for t in wmv for _ in range(4)]
        + [jax.ShapeDtypeStruct((1, LANES), F32)],
    )(recv_sm, recv_vec, recv_gg, *flat)


BIG_L0 = ("ab_w_in", "ab_w_out", "mla_w_uq", "mla_w_ukv")
BIG_L1 = ("ssd_w_in", "ssd_w_out")
COLUMN_SHARDED = ("ab_w_in", "mla_w_uq", "ssd_w_in")

MAP_W0 = ((0, 512, 0, 1024), (512, 1536, 0, 0), (1536, 1920, 0, 1536), (1920, 1952, 0, 1984))
MAP_W1 = ((0, 2048, 0, 0), (2048, 5120, 1, 0), (5120, 5152, 2, 0))
MAP_WQ = tuple((96 * hd, 96 * hd + 96, 0, 128 * hd) for hd in range(8))
MAP_WKV = (tuple((128 * hd, 128 * hd + 64, 0, 128 * hd) for hd in range(8))
           + tuple((128 * hd + 64, 128 * hd + 128, 0, 1024 + 64 * hd) for hd in range(8)))
MAP_G0 = ((0, 512, 0, 0), (512, 1536, 1, 0), (1536, 1920, 2, 0), (1920, 1952, 2, 448))
W0_EARLY, W0_LATE = (0, 6), (6, 8)


def kernel(x, positions, ab_w_in, ab_conv_w, ab_conv_b, ab_gate_a_w, ab_gate_a_b, ab_gate_x_w, ab_gate_x_b, ab_lambda, mla_q_norm, mla_kv_norm, mla_w_uq, mla_w_ukv, ab_w_out, ab_ln_g, ab_ln_b, ssd_w_in, ssd_conv_w, ssd_conv_b, ssd_dt_bias, ssd_a_log, ssd_d, ssd_norm, ssd_w_out, ssd_ln_g, ssd_ln_b, loss_target, m_ab_w_in, m_ab_conv_w, m_ab_conv_b, m_ab_gate_a_w, m_ab_gate_a_b, m_ab_gate_x_w, m_ab_gate_x_b, m_ab_lambda, m_mla_q_norm, m_mla_kv_norm, m_mla_w_uq, m_mla_w_ukv, m_ab_w_out, m_ab_ln_g, m_ab_ln_b, m_ssd_w_in, m_ssd_conv_w, m_ssd_conv_b, m_ssd_dt_bias, m_ssd_a_log, m_ssd_d, m_ssd_norm, m_ssd_w_out, m_ssd_ln_g, m_ssd_ln_b, v_ab_w_in, v_ab_conv_w, v_ab_conv_b, v_ab_gate_a_w, v_ab_gate_a_b, v_ab_gate_x_w, v_ab_gate_x_b, v_ab_lambda, v_mla_q_norm, v_mla_kv_norm, v_mla_w_uq, v_mla_w_ukv, v_ab_w_out, v_ab_ln_g, v_ab_ln_b, v_ssd_w_in, v_ssd_conv_w, v_ssd_conv_b, v_ssd_dt_bias, v_ssd_a_log, v_ssd_d, v_ssd_norm, v_ssd_w_out, v_ssd_ln_g, v_ssd_ln_b):
    args = dict(locals())
    bf = MXU_DTYPE
    big = {n: [args[pre + n][0] for pre in ("", "m_", "v_")] for n in BIG_L0 + BIG_L1}
    sml = {n: [_view2d(n, args[pre + n]) for pre in ("", "m_", "v_")] for n in SMALL_NAMES}

    w0_8, cw0_8 = _all_gather([big["ab_w_in"][0].astype(bf), sml["ab_conv_w"][0]], "gather_params")
    p = {"cw0_8": cw0_8, "l0_blocks": [big[n][0].astype(bf) for n in BIG_L0[1:]] + [sml[n][0] for n, *_ in SMALL[1:]]}
    p["w0p"], = _unshard(w0_8, MAP_W0, (2048,), "unshard_w0")
    p["wa"], p["wx"], p["dt_bias"], p["a_log"], p["d_x"] = _prep_repl(
        sml["ab_gate_a_w"][0], sml["ab_gate_x_w"][0], sml["ssd_dt_bias"][0], sml["ssd_a_log"][0], sml["ssd_d"][0])
    for key, n in (("cb0", "ab_conv_b"), ("ba", "ab_gate_a_b"), ("bx", "ab_gate_x_b"), ("lam", "ab_lambda"),
                   ("qn_w", "mla_q_norm"), ("kn_w", "mla_kv_norm"), ("g0", "ab_ln_g"), ("b0", "ab_ln_b")):
        p[key] = sml[n][0]

    _, recv_early, recv, _, grad_x = _local_step(
        x[0], positions[0], loss_target[0], p, [big[n][0].astype(bf) for n in BIG_L1])

    me = 4 * lax.axis_index("x") + 2 * lax.axis_index("y") + lax.axis_index("c")
    parts = dict(recv_early, ab_w_in=jnp.where(me >= W0_LATE[0], recv[0], recv_early["ab_w_in"]),
                 mla_w_uq=recv[1], mla_w_ukv=recv[2])

    outs = {}
    kinds = ("grad", "delta", "new_m", "new_v")
    for n in BIG_L0 + BIG_L1:
        if n in COLUMN_SHARDED:
            rows, cols = big[n][0].shape
            if rows == SUBLANES * LANES:
                wmv_t = [jnp.transpose(args[pre + n], (2, 0, 1)).reshape(cols * SUBLANES, LANES) for pre in ("", "m_", "v_")]
                back = lambda res: jnp.transpose(res, (1, 2, 0))
            else:
                wmv_t = [args[pre + n][0].T for pre in ("", "m_", "v_")]
                back = lambda res: res.T[None]
            for kind, res in zip(kinds, _adamw_cols(parts[n], *wmv_t, "adamw_" + n)):
                outs[kind, n] = back(res)
            continue
        for kind, res in zip(kinds, _adamw(parts[n], *big[n], "adamw_" + n)):
            outs[kind, n] = res[None]
    res = _adamw_small(*recv[3:], [sml[n] for n in SMALL_NAMES])
    for i, n in enumerate(SMALL_NAMES):
        for k, kind in enumerate(kinds):
            outs[kind, n] = res[4 * i + k].reshape(args[n].shape)

    loss = res[4 * len(SMALL_NAMES)][0, 0]
    order = ["ab_w_in", "ab_conv_w", "ab_conv_b", "ab_gate_a_w", "ab_gate_a_b", "ab_gate_x_w", "ab_gate_x_b",
             "ab_lambda", "mla_q_norm", "mla_kv_norm", "mla_w_uq", "mla_w_ukv", "ab_w_out", "ab_ln_g", "ab_ln_b",
             "ssd_w_in", "ssd_conv_w", "ssd_conv_b", "ssd_dt_bias", "ssd_a_log", "ssd_d", "ssd_norm", "ssd_w_out",
             "ssd_ln_g", "ssd_ln_b"]
    return (loss, grad_x[None], *[outs[kind, n] for kind in ("grad", "delta", "new_m", "new_v") for n in order])


def _local_step(x, pos, target, p, l1_blocks):
    bf = MXU_DTYPE
    inv_freq = 10000.0 ** (-jnp.arange(0, 32, 2, dtype=F32) / 32)
    ang = inv_freq[:, None] * pos.astype(F32)[None, :]
    cos, sin = jnp.cos(ang), jnp.sin(ang)
    zeros = lambda n: jnp.zeros((n, SEQ), F32)
    tc = jnp.concatenate([jnp.ones((64, SEQ), F32), cos, cos, zeros(32)], axis=0)
    tsa = jnp.concatenate([zeros(64), -sin, zeros(48)], axis=0)
    tsb = jnp.concatenate([zeros(80), sin, zeros(32)], axis=0)

    w0p, wa, wxg = (p[k] for k in ("w0p", "wa", "wx"))
    cb0, ba, bx, lam = (p[k] for k in ("cb0", "ba", "bx", "lam"))
    qn_w, kn_w, g0, b0 = (p[k] for k in ("qn_w", "kn_w", "g0", "b0"))
    dt_bias, a_log, d_x = (p[k] for k in ("dt_bias", "a_log", "d_x"))
    tril = jnp.tril(jnp.ones((SSD_L, SSD_L), F32))
    expand_t = (jnp.arange(SSD_INNER)[:, None] // SSD_P == jnp.arange(LANES)[None, :]).astype(jnp.bfloat16)

    proj0, xb, l0_8 = _l0_in(x, w0p, bcast=p["l0_blocks"])
    wo0 = l0_8[0].reshape(D_MODEL, D_MODEL)
    wq, = _unshard(l0_8[1], MAP_WQ, (1024,), "unshard_wq")
    wkv, = _unshard(l0_8[2], MAP_WKV, (1536,), "unshard_wkv")
    cw0, cw1, cb1, nw, g1, b1 = _unshard_small([p["cw0_8"]] + list(l0_8[3:]))
    xc, h = _rglru_fwd(proj0, cw0, cb0, wa, ba, wxg, bx, lam)
    qn, kn, qc, kc, vc = _mla_fwd(proj0, qn_w, kn_w, wq, wkv, tc, tsa, tsb)
    o, lse, (w1_8,) = _flash_fwd(qc, kc, vc, bcast=l1_blocks[:1])
    w1z, w1x, w1d = _unshard(w1_8, MAP_W1, (2048, 3072, 128), "unshard_w1")
    y0, v0, x1, x1b = _l0_out(h, o, proj0, x, wo0, g0, b0)

    z, dt_raw = _l1_in(x1b, w1z, w1d)
    xbc, pre, act = _ssd_conv_fwd(x1b, w1x, cw1, cb1)
    ys, hprev, (wo1_8,) = _ssd_scan_fwd(act, dt_raw, dt_bias, a_log, d_x, tril, expand_t, bcast=l1_blocks[1:])
    wo1 = wo1_8.reshape(SSD_INNER, D_MODEL)
    dv1, dgb1, loss8, g_wo1 = _l1_out(ys, z, nw, wo1, x1, g1, b1, target)

    dys, dz, dnw, g_z = _l1_gate_bwd(dv1, wo1, ys, z, nw, x1b)
    dact, ddt_raw, dvec1, g_dt, (recv_wo1,) = _ssd_scan_bwd(
        dys, act, dt_raw, hprev, dt_bias, a_log, d_x, tril, expand_t, x1b,
        scatter=[g_wo1.reshape(N_DEV, 256, D_MODEL)])
    dxbc, dcw1, g_xbc = _ssd_conv_bwd(dact, pre, xbc, cw1, x1b)

    dv0, dgb0 = _l1_dx_ln(dz, dxbc, ddt_raw, dv1, v0, w1z, w1x, w1d, g0)
    dh, do, dgate, g_wo0, g_gate = _gate_bwd(dv0, wo0, h, o, proj0, y0, xb)
    dxr, g_wa, g_wx, dvec0, g_rnn = _rglru_bwd(dh, xc, h, proj0, cw0, wa, ba, wxg, bx, lam, xb)
    early = [_reshard([g_z, g_xbc, g_dt], MAP_W1, 644, bf, "reshard_w1"), g_wo0.astype(bf).reshape(N_DEV, 128, D_MODEL),
             (_reshard([g_rnn, g_gate], MAP_G0, 244, bf, "reshard_w0_early", shards=W0_EARLY), W0_EARLY)]
    dq, dk, dvv, (recv_w1, recv_wo0, recv_w0) = _flash_bwd(qc, kc, vc, o, do, lse, scatter=early)
    recv_early = {"ssd_w_in": recv_w1, "ssd_w_out": recv_wo1, "ab_w_out": recv_wo0, "ab_w_in": recv_w0}
    dtail, g_wq, g_wkv, dqnw, dknw, g_tail = _mla_bwd(dq, dk, dvv, proj0, qn, kn, qn_w, kn_w, wq, wkv, tc, tsa, tsb, xb)

    acc = {"g_rnn": g_rnn, "g_gate": g_gate, "g_tail": g_tail, "g_wq": g_wq, "g_wkv": g_wkv,
           "dvec0": dvec0, "g_wa": g_wa, "g_wx": g_wx, "dqnw": dqnw, "dknw": dknw, "dgb0": dgb0, "dvec1": dvec1,
           "dcw1": dcw1, "dnw": dnw, "dgb1": dgb1}
    late = [(_reshard([g_rnn, g_gate, g_tail], MAP_G0, 244, bf, "reshard_w0_late", shards=W0_LATE), W0_LATE),
            _reshard([g_wq], MAP_WQ, 96, bf, "reshard_wq"), _reshard([g_wkv], MAP_WKV, 128, bf, "reshard_wkv")]
    sm_slots, vec_rows, gates = _pack_small(dvec0, g_wa, g_wx, dqnw, dknw, dgb0, dvec1, dcw1, dnw, dgb1, loss8)
    dx, recv_late = _l0_dx(dxr, dgate, dtail, w0p, dv0, scatter=late + [sm_slots], bcast=[vec_rows, gates])
    return acc, recv_early, recv_late, loss8[0, 0], dx
```

```python
import math

import jax
import jax.numpy as jnp
from jax import lax
from jax.experimental import pallas as pl
from jax.experimental.pallas import tpu as pltpu

F32 = jnp.float32
MXU_DTYPE = jnp.bfloat16

N_DEV = 8
SEQ = 4096
D_MODEL = 1024
DN_ALPHA = 4.0 ** 0.25
RNN_W = 512
MLA_HEADS = 8
ATT_SCALE = 96.0 ** -0.5
ATT_C = ATT_SCALE * math.log2(math.e)
RG_C = 8.0
SSD_INNER = 2048
SSD_HEADS = 32
SSD_P = 64
SSD_GROUPS = 4
SSD_N = 128
SSD_L = 128
SSD_CONV = 3072
LANES = 128
SUBLANES = 8
VMEM_LIMIT = 56 * 1024 * 1024

ADAM_LR, ADAM_B1, ADAM_B2, ADAM_EPS, ADAM_WD, ADAM_STEP = 0.001, 0.9, 0.999, 1e-08, 0.01, 10

HIGHEST = lax.Precision.HIGHEST


def _params(sem, limit=VMEM_LIMIT):
    return pltpu.CompilerParams(dimension_semantics=sem, vmem_limit_bytes=limit)


def _dot(a, b):
    return lax.dot_general(a, b, (((1,), (0,)), ((), ())), preferred_element_type=F32)


def _dot_nt(a, b):
    return lax.dot_general(a, b, (((1,), (1,)), ((), ())), preferred_element_type=F32)


def _dot_tn(a, b):
    return lax.dot_general(a, b, (((0,), (0,)), ((), ())), preferred_element_type=F32)


def _dot_hi(a, b):
    return lax.dot_general(a, b, (((1,), (0,)), ((), ())), precision=HIGHEST, preferred_element_type=F32)


def _mx(v):
    return v.astype(MXU_DTYPE)


def _sigmoid(v):
    return 1.0 / (1.0 + jnp.exp(-v))


def _log1p_pos(e):
    poly = e * (1.0 - e * (0.5 - e * (1.0 / 3.0 - e * 0.25)))
    return jnp.where(e < 0.01, poly, jnp.log(1.0 + e))


def _softplus(v):
    return jnp.maximum(v, 0.0) + _log1p_pos(jnp.exp(-jnp.abs(v)))


def _neg_expm1(v):
    poly = -v * (1.0 + v * (0.5 + v * (1.0 / 6.0 + v * (1.0 / 24.0 + v * (1.0 / 120.0)))))
    return jnp.where(jnp.abs(v) < 0.1, poly, 1.0 - jnp.exp(v))


def _silu(v):
    return v * _sigmoid(v)


def _dsilu(v):
    s = _sigmoid(v)
    return s * (1.0 + v * (1.0 - s))


def _shift_down(blk, halo, s):
    if s == 0:
        return blk
    t = blk.shape[0]
    r = pltpu.roll(blk, s, 0)
    hr = pltpu.roll(halo, s, 0)
    row8 = lax.broadcasted_iota(jnp.int32, hr.shape, 0)
    head = jnp.where(row8 < s, hr, r[:SUBLANES])
    return jnp.concatenate([head, r[SUBLANES:]], axis=0) if t > SUBLANES else head


def _shift_up(blk, halo, s):
    if s == 0:
        return blk
    t = blk.shape[0]
    r = pltpu.roll(blk, t - s, 0)
    hr = pltpu.roll(halo, SUBLANES - s, 0)
    row8 = lax.broadcasted_iota(jnp.int32, hr.shape, 0)
    tail = jnp.where(row8 >= SUBLANES - s, hr, r[t - SUBLANES:])
    return jnp.concatenate([r[:t - SUBLANES], tail], axis=0) if t > SUBLANES else tail


def _scan_down(a, u):
    t = a.shape[0]
    row = lax.broadcasted_iota(jnp.int32, a.shape, 0)
    d = 1
    while d < t:
        keep = row >= d
        a_sh = jnp.where(keep, pltpu.roll(a, d, 0), 1.0)
        u_sh = jnp.where(keep, pltpu.roll(u, d, 0), 0.0)
        u = a * u_sh + u
        a = a * a_sh
        d *= 2
    return a, u


def _scan_up(a, u):
    t = a.shape[0]
    row = lax.broadcasted_iota(jnp.int32, a.shape, 0)
    d = 1
    while d < t:
        keep = row < t - d
        a_sh = jnp.where(keep, pltpu.roll(a, t - d, 0), 1.0)
        u_sh = jnp.where(keep, pltpu.roll(u, t - d, 0), 0.0)
        u = a * u_sh + u
        a = a * a_sh
        d *= 2
    return a, u


def _conv4(blk, halo, cw, cb):
    out = cb + blk * cw[3:4]
    for k in range(3):
        out = out + _shift_down(blk, halo, 3 - k) * cw[k:k + 1]
    return out


RG_T = 512
P0_RNN = 2


def _rg_gates(xc, wa, ba, wx, bx, lam):
    xcb = _mx(xc)
    r = _sigmoid(_dot(xcb, wa) + ba)
    ig = _sigmoid(_dot(xcb, wx) + bx)
    sp = _softplus(-lam)
    la = (-RG_C * r) * sp
    a = jnp.exp(la)
    mult = jnp.sqrt(_neg_expm1(2.0 * la))
    return r, ig, sp, a, mult


def _rglru_fwd(proj0, cw8, cb, wa, ba, wx, bx, lam):
    t, w = RG_T, RNN_W
    nb = SEQ // t

    def body(x_ref, halo_ref, cw_ref, cb_ref, wa_ref, ba_ref, wx_ref, bx_ref, lam_ref, xc_ref, h_ref, carry):
        i = pl.program_id(0)

        @pl.when(i == 0)
        def _():
            carry[...] = jnp.zeros_like(carry)

        blk = x_ref[...]
        halo = jnp.where(i > 0, halo_ref[...], 0.0)
        xc = _conv4(blk, halo, cw_ref[...], cb_ref[...])
        _, ig, _, a, mult = _rg_gates(xc, wa_ref[...], ba_ref[...], wx_ref[...], bx_ref[...], lam_ref[...])
        u = mult * (ig * xc)
        big_a, big_u = _scan_down(a, u)
        h = big_a * carry[SUBLANES - 1:SUBLANES, :] + big_u
        carry[...] = h[t - SUBLANES:]
        xc_ref[...] = xc
        h_ref[...] = h

    vec = pl.BlockSpec((1, w), lambda i: (0, 0))
    mat = pl.BlockSpec((w, w), lambda i: (0, 0))
    return pl.pallas_call(
        body, name="rglru_fwd", grid=(nb,),
        in_specs=[pl.BlockSpec((t, w), lambda i: (i, P0_RNN)),
                  pl.BlockSpec((SUBLANES, w), lambda i: (jnp.maximum(i * (t // SUBLANES) - 1, 0), P0_RNN)),
                  pl.BlockSpec((SUBLANES, w), lambda i: (0, 0)), vec, mat, vec, mat, vec, vec],
        out_specs=[pl.BlockSpec((t, w), lambda i: (i, 0)), pl.BlockSpec((t, w), lambda i: (i, 0))],
        out_shape=[jax.ShapeDtypeStruct((SEQ, w), F32), jax.ShapeDtypeStruct((SEQ, w), F32)],
        scratch_shapes=[pltpu.VMEM((SUBLANES, w), F32)],
        compiler_params=_params(("arbitrary",)),
    )(proj0, proj0, cw8, cb, wa, ba, wx, bx, lam)


def _rglru_bwd(dh, xc, h, proj0, cw8, wa, ba, wx, bx, lam, xb):
    t, w = RG_T, RNN_W
    nb = SEQ // t
    tb = t // SUBLANES

    def body(dh_ref, xc_ref, h_ref, hh_ref, x_ref, cw_ref, wa_ref, ba_ref, wx_ref, bx_ref, lam_ref, xb_ref,
             dx_ref, dwa_ref, dwx_ref, dvec_ref, gw_ref, gcarry, dxc_next):
        i = pl.program_id(0)
        rev = nb - 1 - i

        @pl.when(i == 0)
        def _():
            gcarry[...] = jnp.zeros_like(gcarry)
            dxc_next[...] = jnp.zeros_like(dxc_next)
            gw_ref[...] = jnp.zeros_like(gw_ref)
            dwa_ref[...] = jnp.zeros_like(dwa_ref)
            dwx_ref[...] = jnp.zeros_like(dwx_ref)
            dvec_ref[...] = jnp.zeros_like(dvec_ref)

        xc = xc_ref[...]
        wa_v, wx_v = wa_ref[...], wx_ref[...]
        lam_v = lam_ref[...]
        r, ig, sp, a, mult = _rg_gates(xc, wa_v, ba_ref[...], wx_v, bx_ref[...], lam_v)
        dhv = dh_ref[...]
        big_a, big_u = _scan_up(a, a * dhv)
        gg = big_a * gcarry[0:1, :] + big_u
        g = dhv + _shift_up(gg, gcarry[...], 1)
        gcarry[...] = gg[:SUBLANES]
        hhalo = jnp.where(rev > 0, hh_ref[...], 0.0)
        da = g * _shift_down(h_ref[...], hhalo, 1)
        d_mult = g * (ig * xc)
        d_i = g * (mult * xc)
        dxc = g * (mult * ig)
        d_la = da * a - d_mult * (a * a) / mult
        d_r = d_la * (-RG_C * sp)
        d_sp = jnp.sum(d_la * (-RG_C * r), axis=0, keepdims=True)
        d_pa = d_r * r * (1.0 - r)
        d_px = d_i * ig * (1.0 - ig)
        d_pab, d_pxb = _mx(d_pa), _mx(d_px)
        dxc = dxc + _dot_nt(d_pab, wa_v) + _dot_nt(d_pxb, wx_v)
        xcb = _mx(xc)
        dwa_ref[...] += _dot_tn(xcb, d_pab)
        dwx_ref[...] += _dot_tn(xcb, d_pxb)
        dvec_ref[0:1, :] += jnp.sum(d_pa, axis=0, keepdims=True)
        dvec_ref[1:2, :] += jnp.sum(d_px, axis=0, keepdims=True)
        dvec_ref[2:3, :] += d_sp * (-_sigmoid(-lam_v))
        dvec_ref[3:4, :] += jnp.sum(dxc, axis=0, keepdims=True)
        xblk = x_ref[...]
        cw = cw_ref[...]
        dx = dxc * cw[3:4]
        nxt = dxc_next[...]
        dvec_ref[7:8, :] += jnp.sum(dxc * xblk, axis=0, keepdims=True)
        for k in range(3):
            up = _shift_up(dxc, nxt, 3 - k)
            dvec_ref[4 + k:5 + k, :] += jnp.sum(up * xblk, axis=0, keepdims=True)
            dx = dx + up * cw[k:k + 1]
        dxc_next[...] = dxc[:SUBLANES]
        dxb = _mx(dx)
        dx_ref[...] = dxb
        gw_ref[...] += _dot_tn(xb_ref[...], dxb)

    blk = pl.BlockSpec((t, w), lambda i: (nb - 1 - i, 0))
    halo = pl.BlockSpec((SUBLANES, w), lambda i: (jnp.maximum((nb - 1 - i) * tb - 1, 0), 0))
    vec = pl.BlockSpec((1, w), lambda i: (0, 0))
    mat = pl.BlockSpec((w, w), lambda i: (0, 0))
    return pl.pallas_call(
        body, name="rglru_bwd", grid=(nb,),
        in_specs=[blk, blk, blk, halo, pl.BlockSpec((t, w), lambda i: (nb - 1 - i, P0_RNN)),
                  pl.BlockSpec((SUBLANES, w), lambda i: (0, 0)), mat, vec, mat, vec, vec,
                  pl.BlockSpec((t, D_MODEL), lambda i: (nb - 1 - i, 0))],
        out_specs=[blk, mat, mat, pl.BlockSpec((16, w), lambda i: (0, 0)), pl.BlockSpec((D_MODEL, w), lambda i: (0, 0))],
        out_shape=[jax.ShapeDtypeStruct((SEQ, w), MXU_DTYPE), jax.ShapeDtypeStruct((w, w), F32),
                   jax.ShapeDtypeStruct((w, w), F32), jax.ShapeDtypeStruct((16, w), F32),
                   jax.ShapeDtypeStruct((D_MODEL, w), F32)],
        scratch_shapes=[pltpu.VMEM((SUBLANES, w), F32), pltpu.VMEM((SUBLANES, w), F32)],
        compiler_params=_params(("arbitrary",)),
    )(dh, xc, h, h, proj0, cw8, wa, ba, wx, bx, lam, xb)


MLA_T = 512


def _rope(v, c, sa, sb):
    return v * c + pltpu.roll(v, LANES - 16, 1) * sa + pltpu.roll(v, 16, 1) * sb


def _rope_t(dv, c, sa, sb):
    return dv * c + pltpu.roll(dv * sa, 16, 1) + pltpu.roll(dv * sb, LANES - 16, 1)


def _rms(v, g, eps=1e-6):
    rs = lax.rsqrt(jnp.mean(v * v, axis=-1, keepdims=True) + eps)
    return v * rs * g, rs


def _mla_fwd(proj0, q_norm, kv_norm, wq, wkv, tc, tsa, tsb):
    t = MLA_T

    def body(cq_ref, ck_ref, qn_ref, kn_ref, wq_ref, wkv_ref, c_ref, sa_ref, sb_ref,
             oqn_ref, okn_ref, oq_ref, ok_ref, ov_ref):
        c, sa, sb = c_ref[...].T, sa_ref[...].T, sb_ref[...].T
        ck = ck_ref[...]
        qn = _mx(_rms(cq_ref[...], qn_ref[...])[0])
        kn = _mx(_rms(ck[:, :LANES], kn_ref[...])[0])
        oqn_ref[...] = qn
        okn_ref[...] = kn
        krv = _rope(ck[:, LANES:], c, sa, sb)
        qraw = _dot(qn, wq_ref[...])
        kvraw = _dot(kn, wkv_ref[...])
        for hd in range(MLA_HEADS):
            sl = slice(hd * LANES, (hd + 1) * LANES)
            oq_ref[:, sl] = _mx(_rope(qraw[:, sl], c, sa, sb))
            ok_ref[:, sl] = _mx(kvraw[:, sl] + krv)
        ov_ref[...] = _mx(kvraw[:, 1024:])

    tab = pl.BlockSpec((t, LANES), lambda i: (i, 0))
    rot = pl.BlockSpec((LANES, t), lambda i: (0, i))
    wide = pl.BlockSpec((t, 1024), lambda i: (i, 0))
    const = lambda shape: pl.BlockSpec(shape, lambda i: (0, 0))
    return pl.pallas_call(
        body, name="mla_fwd", grid=(SEQ // t,),
        in_specs=[pl.BlockSpec((t, 256), lambda i: (i, 6)), pl.BlockSpec((t, 256), lambda i: (i, 7)),
                  const((1, 256)), const((1, LANES)), const((256, 1024)), const((LANES, 1536)), rot, rot, rot],
        out_specs=[pl.BlockSpec((t, 256), lambda i: (i, 0)), tab, wide, wide, pl.BlockSpec((t, 512), lambda i: (i, 0))],
        out_shape=[jax.ShapeDtypeStruct((SEQ, 256), MXU_DTYPE), jax.ShapeDtypeStruct((SEQ, LANES), MXU_DTYPE),
                   jax.ShapeDtypeStruct((SEQ, 1024), MXU_DTYPE), jax.ShapeDtypeStruct((SEQ, 1024), MXU_DTYPE),
                   jax.ShapeDtypeStruct((SEQ, 512), MXU_DTYPE)],
        compiler_params=_params(("parallel",)),
    )(proj0, proj0, q_norm, kv_norm, wq, wkv, tc, tsa, tsb)


ATT_T = 1024


def _flash_fwd(q, k, v, bcast=()):
    t = ATT_T
    nb = SEQ // t

    steps = [(qi, ki) for qi in range(nb) for ki in range(qi + 1)]
    qi_tab = jnp.asarray([s[0] for s in steps], jnp.int32)
    ki_tab = jnp.asarray([s[1] for s in steps], jnp.int32)

    nx = len(bcast)

    def body(qi_ref, ki_ref, q_ref, k_ref, v_ref, *rest):
        x_refs, (o_ref, lse_ref), g_refs = rest[:nx], rest[nx:nx + 2], rest[nx + 2:2 * nx + 2]
        m_sc, acc_sc = rest[2 * nx + 2:2 * nx + 4]
        step = pl.program_id(1)
        qi, ki = qi_ref[step], ki_ref[step]
        if nx:
            copies = _peer_copies(x_refs, g_refs, rest[2 * nx + 4:], [])

            @pl.when((pl.program_id(0) == 0) & (step == 0))
            def _():
                for cp in copies:
                    cp.start()

        @pl.when(ki == 0)
        def _():
            m_sc[...] = jnp.full_like(m_sc, -jnp.inf)
            acc_sc[...] = jnp.zeros_like(acc_sc)

        def update(q0, nq, nk, masked):
            vv = v_ref[0:nk, :]
            lane_v = lax.broadcasted_iota(jnp.int32, vv.shape, 1)
            qs = slice(q0, q0 + nq)
            for hd in range(2):
                sl = slice(hd * LANES, (hd + 1) * LANES)
                st = _dot_nt(k_ref[0:nk, sl], q_ref[qs, sl])
                if masked:
                    st = jnp.where(lax.broadcasted_iota(jnp.int32, (nk, nq), 0)
                                   <= lax.broadcasted_iota(jnp.int32, (nk, nq), 1) + q0, st, -jnp.inf)
                m_prev = m_sc[hd:hd + 1, qs]
                m_new = jnp.maximum(m_prev, jnp.max(st, axis=0, keepdims=True))
                pt = jnp.exp2((st - m_new) * ATT_C)
                m_sc[hd:hd + 1, qs] = m_new
                vh = jnp.where((lane_v >= hd * 64) & (lane_v < (hd + 1) * 64), vv, jnp.ones_like(vv))
                acc_sc[hd, :, qs] = acc_sc[hd, :, qs] * jnp.exp2((m_prev - m_new) * ATT_C) + _dot_tn(vh, _mx(pt))

        @pl.when(ki < qi)
        def _():
            update(0, t, t, False)

        @pl.when(ki == qi)
        def _():
            update(0, t // 2, t // 2, True)
            update(t // 2, t // 2, t, True)
            a0, a1 = acc_sc[0], acc_sc[1]
            l0, l1 = a0[64:65, :], a1[0:1, :]
            first = lax.broadcasted_iota(jnp.int32, (LANES, t), 0) < 64
            o_ref[...] = jnp.where(first, a0 / l0, a1 / l1).T
            lse_ref[0, 0:1, :] = m_sc[0:1, :] * ATT_SCALE + jnp.log(l0)
            lse_ref[0, 1:2, :] = m_sc[1:2, :] * ATT_SCALE + jnp.log(l1)
            lse_ref[0, 2:SUBLANES, :] = jnp.zeros((SUBLANES - 2, t), F32)

        if nx:
            @pl.when((pl.program_id(0) == 3) & (step == len(steps) - 1))
            def _():
                for cp in copies:
                    cp.wait()

    grid_spec = pltpu.PrefetchScalarGridSpec(
        num_scalar_prefetch=2, grid=(4, len(steps)),
        in_specs=[pl.BlockSpec((t, 256), lambda p, s, qt, kt: (qt[s], p)),
                  pl.BlockSpec((t, 256), lambda p, s, qt, kt: (kt[s], p)),
                  pl.BlockSpec((t, LANES), lambda p, s, qt, kt: (kt[s], p))] + [ANY] * nx,
        out_specs=[pl.BlockSpec((t, LANES), lambda p, s, qt, kt: (qt[s], p)),
                   pl.BlockSpec((1, SUBLANES, t), lambda p, s, qt, kt: (p, 0, qt[s]))] + [ANY] * nx,
        scratch_shapes=[pltpu.VMEM((SUBLANES, t), F32), pltpu.VMEM((2, LANES, t), F32)]
        + (_exchange_sems(nx) if nx else []))
    res = pl.pallas_call(
        body, name="flash_fwd", grid_spec=grid_spec,
        out_shape=[jax.ShapeDtypeStruct((SEQ, 512), F32), jax.ShapeDtypeStruct((4, SUBLANES, SEQ), F32)]
        + _exchange_shapes([], bcast),
        compiler_params=_params(("arbitrary", "arbitrary")),
    )(qi_tab, ki_tab, q, k, v, *bcast)
    return res[0], res[1], res[2:]


def _flash_bwd(q, k, v, o, do, lse, scatter=()):
    t = ATT_T
    nb = SEQ // t

    steps = [(qi, ki) for ki in range(nb) for qi in range(ki, nb)]
    qi_tab = jnp.asarray([s[0] for s in steps], jnp.int32)
    ki_tab = jnp.asarray([s[1] for s in steps], jnp.int32)
    log2e = math.log2(math.e)

    sc_arrays, sc_ranges = _scatter_args(scatter)
    nx = len(sc_arrays)

    def body(qi_ref, ki_ref, q_ref, k_ref, v_ref, o_ref, do_ref, lse_ref, *rest):
        x_refs, (dq_ref, dk_ref, dv_ref), g_refs = rest[:nx], rest[nx:nx + 3], rest[nx + 3:2 * nx + 3]
        dkt_sc, dvt_sc = rest[2 * nx + 3:2 * nx + 5]
        step = pl.program_id(1)
        qi, ki = qi_ref[step], ki_ref[step]
        if nx:
            copies = _peer_copies(x_refs, g_refs, rest[2 * nx + 5:], sc_ranges)

            @pl.when((pl.program_id(0) == 0) & (step == 0))
            def _():
                for cp in copies:
                    cp.start()

        @pl.when(step == 0)
        def _():
            dq_ref[...] = jnp.zeros_like(dq_ref)

        @pl.when(qi == ki)
        def _():
            dkt_sc[...] = jnp.zeros_like(dkt_sc)
            dvt_sc[...] = jnp.zeros_like(dvt_sc)

        def update(q0, nq, nk, masked):
            qs = slice(q0, q0 + nq)
            dov, ov, vv = do_ref[qs, :], o_ref[qs, :], v_ref[0:nk, :]
            lse2 = (lse_ref[0, :, qs] * log2e).T
            lane = lax.broadcasted_iota(jnp.int32, (nq, LANES), 1)
            row_t = lax.broadcasted_iota(jnp.int32, (LANES, nk), 0)
            prod = dov * ov
            do_b = _mx(dov)
            qrows = pl.ds(pl.multiple_of(qi * t + q0, nq), nq)
            dvt_acc = jnp.zeros((LANES, nk), F32)
            dkt_new, dq_new = [], []
            for hd in range(2):
                sl = slice(hd * LANES, (hd + 1) * LANES)
                mine = (lane >= hd * 64) & (lane < (hd + 1) * 64)
                qh, kh = q_ref[qs, sl], k_ref[0:nk, sl]
                p = jnp.exp2(_dot_nt(qh, kh) * ATT_C - lse2[:, hd:hd + 1])
                if masked:
                    p = jnp.where(lax.broadcasted_iota(jnp.int32, (nq, nk), 1)
                                  <= lax.broadcasted_iota(jnp.int32, (nq, nk), 0) + q0, p, 0.0)
                do_h = jnp.where(mine, dov, 0.0)
                delta = jnp.sum(jnp.where(mine, prod, 0.0), axis=1, keepdims=True)
                dp = _dot_nt(_mx(do_h), vv)
                ds = _mx(p * (dp - delta) * ATT_SCALE)
                dvt_acc = dvt_acc + jnp.where((row_t >= hd * 64) & (row_t < (hd + 1) * 64), _dot_tn(do_b, _mx(p)), 0.0)
                dkt_new.append(_dot_tn(qh, ds))
                dq_new.append(_dot(ds, kh))
            for hd in range(2):
                sl = slice(hd * LANES, (hd + 1) * LANES)
                dkt_sc[sl, 0:nk] += dkt_new[hd]
                dq_ref[qrows, sl] += dq_new[hd]
            dvt_sc[:, 0:nk] += dvt_acc

        @pl.when(qi > ki)
        def _():
            update(0, t, t, False)

        @pl.when(qi == ki)
        def _():
            update(0, t // 2, t // 2, True)
            update(t // 2, t // 2, t, True)

        @pl.when(qi == nb - 1)
        def _():
            dk_ref[...] = dkt_sc[...].T
            dv_ref[...] = dvt_sc[...].T

        if nx:
            @pl.when((pl.program_id(0) == 3) & (step == len(steps) - 1))
            def _():
                for cp in copies:
                    cp.wait()

    qmap = lambda p, s, qt, kt: (qt[s], p)
    kmap = lambda p, s, qt, kt: (kt[s], p)
    grid_spec = pltpu.PrefetchScalarGridSpec(
        num_scalar_prefetch=2, grid=(4, len(steps)),
        in_specs=[pl.BlockSpec((t, 256), qmap), pl.BlockSpec((t, 256), kmap), pl.BlockSpec((t, LANES), kmap),
                  pl.BlockSpec((t, LANES), qmap), pl.BlockSpec((t, LANES), qmap),
                  pl.BlockSpec((1, SUBLANES, t), lambda p, s, qt, kt: (p, 0, qt[s]))] + [ANY] * nx,
        out_specs=[pl.BlockSpec((SEQ, 256), lambda p, s, qt, kt: (0, p)), pl.BlockSpec((t, 256), kmap),
                   pl.BlockSpec((t, LANES), kmap)] + [ANY] * nx,
        scratch_shapes=[pltpu.VMEM((256, t), F32), pltpu.VMEM((LANES, t), F32)] + (_exchange_sems(nx) if nx else []))
    res = pl.pallas_call(
        body, name="flash_bwd", grid_spec=grid_spec,
        out_shape=[jax.ShapeDtypeStruct((SEQ, 1024), F32), jax.ShapeDtypeStruct((SEQ, 1024), F32),
                   jax.ShapeDtypeStruct((SEQ, 512), F32)] + _exchange_shapes(sc_arrays, []),
        compiler_params=_params(("arbitrary", "arbitrary")),
    )(qi_tab, ki_tab, q, k, v, o, do, lse, *sc_arrays)
    return res[0], res[1], res[2], res[3:]


def _rms_bwd(v, g, dy, eps=1e-6):
    rs = lax.rsqrt(jnp.mean(v * v, axis=-1, keepdims=True) + eps)
    xh = v * rs
    dxh = dy * g
    dv = rs * (dxh - xh * jnp.mean(dxh * xh, axis=-1, keepdims=True))
    return dv, jnp.sum(dy * xh, axis=0, keepdims=True)


def _mla_bwd(dq, dk, dv, proj0, qlat, klat, q_norm, kv_norm, wq, wkv, tc, tsa, tsb, xb):
    t = MLA_T

    def body(dq_ref, dk_ref, dv_ref, cq_ref, ck_ref, ql_ref, kl_ref, qn_ref, kn_ref, wq_ref, wkv_ref,
             c_ref, sa_ref, sb_ref, xb_ref, o_ref, gwq_ref, gwkv_ref, dgq_ref, dgk_ref, gwt_ref, oq_ref, okv_ref):
        @pl.when(pl.program_id(0) == 0)
        def _():
            dgq_ref[...] = jnp.zeros_like(dgq_ref)
            dgk_ref[...] = jnp.zeros_like(dgk_ref)
            gwq_ref[...] = jnp.zeros_like(gwq_ref)
            gwkv_ref[...] = jnp.zeros_like(gwkv_ref)
            gwt_ref[...] = jnp.zeros_like(gwt_ref)

        c, sa, sb = c_ref[...].T, sa_ref[...].T, sb_ref[...].T
        lane = lax.broadcasted_iota(jnp.int32, (t, LANES), 1)
        dkr = jnp.zeros((t, LANES), F32)
        for hd in range(MLA_HEADS):
            sl = slice(hd * LANES, (hd + 1) * LANES)
            oq_ref[:, sl] = _mx(_rope_t(dq_ref[:, sl], c, sa, sb))
            dkh = dk_ref[:, sl]
            okv_ref[:, sl] = _mx(dkh)
            dkr = dkr + dkh
        okv_ref[:, 1024:] = _mx(dv_ref[...])
        dkr = _rope_t(jnp.where((lane >= 64) & (lane < 96), dkr, 0.0), c, sa, sb)
        dqraw, dkvraw = oq_ref[...], okv_ref[...]
        gwq_ref[...] += _dot_tn(ql_ref[...], dqraw)
        gwkv_ref[...] += _dot_tn(kl_ref[...], dkvraw)
        dqn = _dot_nt(dqraw, wq_ref[...])
        dkn = _dot_nt(dkvraw, wkv_ref[...])
        dcq, dgq = _rms_bwd(cq_ref[...], qn_ref[...], dqn)
        dck, dgk = _rms_bwd(ck_ref[:, :LANES], kn_ref[...], dkn)
        o_ref[:, :256] = _mx(dcq)
        o_ref[:, 256:384] = _mx(dck)
        o_ref[:, 384:] = _mx(dkr)
        gwt_ref[...] += _dot_tn(xb_ref[...], o_ref[...])
        dgq_ref[0:1, :] += dgq
        dgk_ref[0:1, :] += dgk

    tab = pl.BlockSpec((t, LANES), lambda i: (i, 0))
    rot = pl.BlockSpec((LANES, t), lambda i: (0, i))
    wide = pl.BlockSpec((t, 1024), lambda i: (i, 0))
    const = lambda shape: pl.BlockSpec(shape, lambda i: (0, 0))
    return pl.pallas_call(
        body, name="mla_bwd", grid=(SEQ // t,),
        in_specs=[wide, wide, pl.BlockSpec((t, 512), lambda i: (i, 0)),
                  pl.BlockSpec((t, 256), lambda i: (i, 6)), pl.BlockSpec((t, 256), lambda i: (i, 7)),
                  pl.BlockSpec((t, 256), lambda i: (i, 0)), tab,
                  const((1, 256)), const((1, LANES)), const((256, 1024)), const((LANES, 1536)), rot, rot, rot, wide],
        out_specs=[pl.BlockSpec((t, 512), lambda i: (i, 0)), const((256, 1024)), const((LANES, 1536)),
                   const((SUBLANES, 256)), const((SUBLANES, LANES)), const((D_MODEL, 512))],
        out_shape=[jax.ShapeDtypeStruct((SEQ, 512), MXU_DTYPE), jax.ShapeDtypeStruct((256, 1024), F32),
                   jax.ShapeDtypeStruct((LANES, 1536), F32), jax.ShapeDtypeStruct((SUBLANES, 256), F32),
                   jax.ShapeDtypeStruct((SUBLANES, LANES), F32), jax.ShapeDtypeStruct((D_MODEL, 512), F32)],
        scratch_shapes=[pltpu.VMEM((t, 1024), MXU_DTYPE), pltpu.VMEM((t, 1536), MXU_DTYPE)],
        compiler_params=_params(("arbitrary",)),
    )(dq, dk, dv, proj0, proj0, qlat, klat, q_norm, kv_norm, wq, wkv, tc, tsa, tsb, xb)


LN_T = 512


def _ln(v, g, b, eps=1e-5):
    mu = jnp.mean(v, axis=-1, keepdims=True)
    xc = v - mu
    rs = lax.rsqrt(jnp.mean(xc * xc, axis=-1, keepdims=True) + eps)
    return xc * rs * g + b


def _ln_bwd(v, g, dy, eps=1e-5):
    mu = jnp.mean(v, axis=-1, keepdims=True)
    xc = v - mu
    rs = lax.rsqrt(jnp.mean(xc * xc, axis=-1, keepdims=True) + eps)
    xh = xc * rs
    dxh = dy * g
    dv = rs * (dxh - jnp.mean(dxh, axis=-1, keepdims=True) - xh * jnp.mean(dxh * xh, axis=-1, keepdims=True))
    return dv, jnp.sum(dy * xh, axis=0, keepdims=True), jnp.sum(dy, axis=0, keepdims=True)


def _l0_out(h, o, proj0, x, w_out, g, b):
    t = LN_T

    def body(h_ref, o_ref, ga_ref, gb_ref, x_ref, w_ref, g_ref, b_ref, y_ref, v_ref, x1_ref, x1b_ref):
        y = _mx(jnp.concatenate([h_ref[...] * _silu(ga_ref[...]), o_ref[...] * _silu(gb_ref[...])], axis=1))
        v = DN_ALPHA * x_ref[...] + _dot(y, w_ref[...])
        y_ref[...] = y
        v_ref[...] = v
        x1 = _ln(v, g_ref[...], b_ref[...])
        x1_ref[...] = x1
        x1b_ref[...] = _mx(x1)

    half = pl.BlockSpec((t, 512), lambda i: (i, 0))
    full = pl.BlockSpec((t, D_MODEL), lambda i: (i, 0))
    vec = pl.BlockSpec((1, D_MODEL), lambda i: (0, 0))
    return pl.pallas_call(
        body, name="l0_out", grid=(SEQ // t,),
        in_specs=[half, half, pl.BlockSpec((t, 512), lambda i: (i, 0)), pl.BlockSpec((t, 512), lambda i: (i, 1)), full,
                  pl.BlockSpec((D_MODEL, D_MODEL), lambda i: (0, 0)), vec, vec],
        out_specs=[full, full, full, full],
        out_shape=[jax.ShapeDtypeStruct((SEQ, D_MODEL), MXU_DTYPE), jax.ShapeDtypeStruct((SEQ, D_MODEL), F32),
                   jax.ShapeDtypeStruct((SEQ, D_MODEL), F32), jax.ShapeDtypeStruct((SEQ, D_MODEL), MXU_DTYPE)],
        compiler_params=_params(("parallel",)),
    )(h, o, proj0, proj0, x, w_out, g, b)


def _l1_in(x1b, w1z, w1d):
    t = 1024

    def body(x_ref, wz_ref, wd_ref, z_ref, dt_ref):
        xv = x_ref[...]
        z_ref[...] = _dot(xv, wz_ref[...])
        dt_ref[...] = _dot(xv, wd_ref[...])

    rows = lambda w: pl.BlockSpec((t, w), lambda i: (i, 0))
    const = lambda w: pl.BlockSpec((D_MODEL, w), lambda i: (0, 0))
    return pl.pallas_call(
        body, name="l1_in", grid=(SEQ // t,),
        in_specs=[rows(D_MODEL), const(SSD_INNER), const(LANES)],
        out_specs=[rows(SSD_INNER), rows(LANES)],
        out_shape=[jax.ShapeDtypeStruct((SEQ, SSD_INNER), F32), jax.ShapeDtypeStruct((SEQ, LANES), F32)],
        compiler_params=_params(("parallel",)),
    )(x1b, w1z, w1d)


def _l1_dx_ln(dz, dxbc, ddt, dv1, v0, w1z, w1x, w1d, g):
    t = LN_T

    def body(dz_ref, dx_ref, ddt_ref, dv1_ref, v_ref, wz_ref, wx_ref, wd_ref, g_ref, dv_ref, dgb_ref):
        @pl.when(pl.program_id(0) == 0)
        def _():
            dgb_ref[...] = jnp.zeros_like(dgb_ref)

        dy = (DN_ALPHA * dv1_ref[...] + _dot_nt(dz_ref[...], wz_ref[...]) + _dot_nt(dx_ref[...], wx_ref[...])
              + _dot_nt(_mx(ddt_ref[...]), wd_ref[...]))
        dv, dg, db = _ln_bwd(v_ref[...], g_ref[...], dy)
        dv_ref[...] = dv
        dgb_ref[0:1, :] += dg
        dgb_ref[1:2, :] += db

    rows = lambda w: pl.BlockSpec((t, w), lambda i: (i, 0))
    const = lambda w: pl.BlockSpec((D_MODEL, w), lambda i: (0, 0))
    return pl.pallas_call(
        body, name="l1_dx_ln", grid=(SEQ // t,),
        in_specs=[rows(SSD_INNER), rows(SSD_CONV), rows(LANES), rows(D_MODEL), rows(D_MODEL),
                  const(SSD_INNER), const(SSD_CONV), const(LANES), pl.BlockSpec((1, D_MODEL), lambda i: (0, 0))],
        out_specs=[rows(D_MODEL), pl.BlockSpec((SUBLANES, D_MODEL), lambda i: (0, 0))],
        out_shape=[jax.ShapeDtypeStruct((SEQ, D_MODEL), F32), jax.ShapeDtypeStruct((SUBLANES, D_MODEL), F32)],
        compiler_params=_params(("arbitrary",)),
    )(dz, dxbc, ddt, dv1, v0, w1z, w1x, w1d, g)


def _gate_bwd(dv0, w_out, h, o, proj0, y0, xb):
    t = LN_T

    def body(dv_ref, w_ref, h_ref, o_ref, ga_ref, gb_ref, y0_ref, xb_ref, dh_ref, do_ref, dg_ref, gwo_ref, gwg_ref):
        @pl.when(pl.program_id(0) == 0)
        def _():
            gwo_ref[...] = jnp.zeros_like(gwo_ref)
            gwg_ref[...] = jnp.zeros_like(gwg_ref)

        dvb = _mx(dv_ref[...])
        dy = _dot_nt(dvb, w_ref[...])
        ga, gb, dya, dyb = ga_ref[...], gb_ref[...], dy[:, :512], dy[:, 512:]
        dh_ref[...] = dya * _silu(ga)
        do_ref[...] = dyb * _silu(gb)
        dg_ref[:, :512] = _mx(dya * h_ref[...] * _dsilu(ga))
        dg_ref[:, 512:] = _mx(dyb * o_ref[...] * _dsilu(gb))
        gwo_ref[...] += _dot_tn(y0_ref[...], dvb)
        gwg_ref[...] += _dot_tn(xb_ref[...], dg_ref[...])

    half = pl.BlockSpec((t, 512), lambda i: (i, 0))
    half1 = pl.BlockSpec((t, 512), lambda i: (i, 1))
    full = pl.BlockSpec((t, 1024), lambda i: (i, 0))
    square = pl.BlockSpec((D_MODEL, D_MODEL), lambda i: (0, 0))
    return pl.pallas_call(
        body, name="gate_bwd", grid=(SEQ // t,),
        in_specs=[full, square, half, half, half, half1, full, full],
        out_specs=[half, half, full, square, square],
        out_shape=[jax.ShapeDtypeStruct((SEQ, 512), F32), jax.ShapeDtypeStruct((SEQ, 512), F32),
                   jax.ShapeDtypeStruct((SEQ, 1024), MXU_DTYPE), jax.ShapeDtypeStruct((D_MODEL, D_MODEL), F32),
                   jax.ShapeDtypeStruct((D_MODEL, D_MODEL), F32)],
        compiler_params=_params(("arbitrary",)),
    )(dv0, w_out, h, o, proj0, proj0, y0, xb)


CONV_T = 1024
CONV_CB = 1024


def _ssd_conv_fwd(x1b, w1x, cw8, cb):
    t, cbk = CONV_T, CONV_CB

    def body(x_ref, w_ref, cw_ref, cb_ref, xbc_ref, pre_ref, act_ref, carry):
        xbc = _dot(x_ref[...], w_ref[...])
        halo = jnp.where(pl.program_id(1) > 0, carry[...], 0.0)
        pre = _conv4(xbc, halo, cw_ref[...], cb_ref[...])
        carry[...] = xbc[t - SUBLANES:]
        xbc_ref[...] = xbc
        pre_ref[...] = pre
        act_ref[...] = _silu(pre)

    blk = pl.BlockSpec((t, cbk), lambda j, i: (i, j))
    out = jax.ShapeDtypeStruct((SEQ, SSD_CONV), F32)
    return pl.pallas_call(
        body, name="ssd_conv_fwd", grid=(SSD_CONV // cbk, SEQ // t),
        in_specs=[pl.BlockSpec((t, D_MODEL), lambda j, i: (i, 0)), pl.BlockSpec((D_MODEL, cbk), lambda j, i: (0, j)),
                  pl.BlockSpec((SUBLANES, cbk), lambda j, i: (0, j)), pl.BlockSpec((1, cbk), lambda j, i: (0, j))],
        out_specs=[blk, blk, blk], out_shape=[out, out, out],
        scratch_shapes=[pltpu.VMEM((SUBLANES, cbk), F32)],
        compiler_params=_params(("parallel", "arbitrary")),
    )(x1b, w1x, cw8, cb)


def _ssd_conv_bwd(dact, pre, xbc, cw8, x1b):
    t, cbk = CONV_T, CONV_CB
    tb = t // SUBLANES
    nb = SEQ // t

    def body(da_ref, dan_ref, pre_ref, pren_ref, x_ref, cw_ref, x1_ref, dx_ref, dcw_ref, gw_ref):
        i = pl.program_id(1)

        @pl.when(i == 0)
        def _():
            dcw_ref[...] = jnp.zeros_like(dcw_ref)
            gw_ref[...] = jnp.zeros_like(gw_ref)

        dpre = da_ref[...] * _dsilu(pre_ref[...])
        dpre_next = jnp.where(i < nb - 1, dan_ref[...] * _dsilu(pren_ref[...]), 0.0)
        xblk = x_ref[...]
        cw = cw_ref[...]
        dx = dpre * cw[3:4]
        dcw_ref[3:4, :] += jnp.sum(dpre * xblk, axis=0, keepdims=True)
        for k in range(3):
            up = _shift_up(dpre, dpre_next, 3 - k)
            dcw_ref[k:k + 1, :] += jnp.sum(up * xblk, axis=0, keepdims=True)
            dx = dx + up * cw[k:k + 1]
        dcw_ref[4:5, :] += jnp.sum(dpre, axis=0, keepdims=True)
        dxb = _mx(dx)
        dx_ref[...] = dxb
        gw_ref[...] += _dot_tn(x1_ref[...], dxb)

    blk = pl.BlockSpec((t, cbk), lambda j, i: (i, j))
    nxt = pl.BlockSpec((SUBLANES, cbk), lambda j, i: (jnp.minimum((i + 1) * tb, SEQ // SUBLANES - 1), j))
    acc = pl.BlockSpec((SUBLANES, cbk), lambda j, i: (0, j))
    return pl.pallas_call(
        body, name="ssd_conv_bwd", grid=(SSD_CONV // cbk, nb),
        in_specs=[blk, nxt, blk, nxt, blk, acc, pl.BlockSpec((t, D_MODEL), lambda j, i: (i, 0))],
        out_specs=[blk, acc, pl.BlockSpec((D_MODEL, cbk), lambda j, i: (0, j))],
        out_shape=[jax.ShapeDtypeStruct((SEQ, SSD_CONV), MXU_DTYPE), jax.ShapeDtypeStruct((SUBLANES, SSD_CONV), F32),
                   jax.ShapeDtypeStruct((D_MODEL, SSD_CONV), F32)],
        compiler_params=_params(("parallel", "arbitrary")),
    )(dact, dact, pre, pre, xbc, cw8, x1b)


def _ssd_common(dt_raw, bias, alog, tril, expand_t, xs):
    lane = lax.broadcasted_iota(jnp.int32, dt_raw.shape, 1)
    dt = jnp.where(lane < SSD_HEADS, _softplus(dt_raw + bias), 0.0)
    a_neg = -jnp.exp(alog)
    cs = _dot_hi(tril, dt * a_neg)
    dt_x = _expand_heads(dt, expand_t)
    ecs_x = _expand_heads(jnp.exp(cs), expand_t)
    ds_x = _expand_heads(jnp.exp(cs[SSD_L - 1:SSD_L, :] - cs), expand_t)
    return dt, a_neg, cs, dt_x, None, xs * dt_x, ds_x, ecs_x, ecs_x[SSD_L - 1:SSD_L, :]


def _expand_heads(v, expand_t):
    hi = v.astype(jnp.bfloat16)
    lo = (v - hi.astype(F32)).astype(jnp.bfloat16)
    return _dot_nt(hi, expand_t) + _dot_nt(lo, expand_t)


def _fold_heads(v, expand_t):
    hi = v.astype(jnp.bfloat16)
    lo = (v - hi.astype(F32)).astype(jnp.bfloat16)
    return _dot(hi, expand_t) + _dot(lo, expand_t)


def _ssd_decay(cs, cs_t, hh, causal):
    seg = cs[:, hh:hh + 1] - cs_t[hh:hh + 1, :]
    return jnp.where(causal, jnp.exp(jnp.where(causal, seg, 0.0)), 0.0)


def _ssd_scan_fwd(act, dt_raw, bias, alog, d_x, tril, expand_t, bcast=()):
    nc = SEQ // SSD_L
    gw = SSD_INNER // SSD_GROUPS
    n = len(bcast)

    def body(act_ref, dt_ref, bias_ref, alog_ref, dx_ref, tril_ref, et_ref, *rest):
        y_ref, hp_ref, h_sc = rest[n], rest[n + 1], rest[2 * n + 2]
        if n:
            copies = _peer_copies(rest[:n], rest[n + 2:2 * n + 2], rest[2 * n + 3:], [])

            @pl.when(pl.program_id(0) == 0)
            def _():
                for cp in copies:
                    cp.start()

            @pl.when(pl.program_id(0) == nc - 1)
            def _():
                for cp in copies:
                    cp.wait()

        @pl.when(pl.program_id(0) == 0)
        def _():
            h_sc[...] = jnp.zeros_like(h_sc)

        xs = act_ref[:, :SSD_INNER]
        _, _, cs, _, _, xdt, ds_x, ecs_x, elast = _ssd_common(
            dt_ref[...], bias_ref[...], alog_ref[...], tril_ref[...], et_ref[...], xs)
        cs_t = cs.T
        causal = (lax.broadcasted_iota(jnp.int32, (SSD_L, SSD_L), 0)
                  >= lax.broadcasted_iota(jnp.int32, (SSD_L, SSD_L), 1))
        lane = lax.broadcasted_iota(jnp.int32, (SSD_L, LANES), 1)
        xdt_b = _mx(xdt)
        xds_b = _mx(xdt * ds_x)
        hp_ref[0] = h_sc[...]
        for g in range(SSD_GROUPS):
            gs = slice(g * gw, (g + 1) * gw)
            bg = _mx(act_ref[:, SSD_INNER + g * SSD_N:SSD_INNER + (g + 1) * SSD_N])
            cg = _mx(act_ref[:, SSD_INNER + 512 + g * SSD_N:SSD_INNER + 512 + (g + 1) * SSD_N])
            cb = _dot_nt(cg, bg)
            hprev = h_sc[:, gs]
            yoff = _dot(cg, _mx(hprev)) * ecs_x[:, gs]
            h_sc[:, gs] = hprev * elast[:, gs] + _dot_tn(bg, xds_b[:, gs])
            for pr in range(4):
                ps = slice(g * gw + pr * LANES, g * gw + (pr + 1) * LANES)
                xp = xdt_b[:, ps]
                ydiag = jnp.zeros((SSD_L, LANES), F32)
                for j in range(2):
                    dm = _ssd_decay(cs, cs_t, g * 8 + pr * 2 + j, causal)
                    mine = (lane >= j * 64) & (lane < (j + 1) * 64)
                    ydiag = ydiag + _dot(_mx(cb * dm), jnp.where(mine, xp, jnp.zeros_like(xp)))
                y_ref[:, ps] = ydiag + yoff[:, pr * LANES:(pr + 1) * LANES] + dx_ref[:, ps] * xs[:, ps]

    const = lambda shape: pl.BlockSpec(shape, lambda c: (0, 0))
    res = pl.pallas_call(
        body, name="ssd_scan_fwd", grid=(nc,),
        in_specs=[pl.BlockSpec((SSD_L, SSD_CONV), lambda c: (c, 0)), pl.BlockSpec((SSD_L, LANES), lambda c: (c, 0)),
                  const((1, LANES)), const((1, LANES)), const((1, SSD_INNER)), const((SSD_L, SSD_L)),
                  const((SSD_INNER, LANES))] + [ANY] * n,
        out_specs=[pl.BlockSpec((SSD_L, SSD_INNER), lambda c: (c, 0)),
                   pl.BlockSpec((1, SSD_N, SSD_INNER), lambda c: (c, 0, 0))] + [ANY] * n,
        out_shape=[jax.ShapeDtypeStruct((SEQ, SSD_INNER), F32), jax.ShapeDtypeStruct((nc, SSD_N, SSD_INNER), F32)]
        + _exchange_shapes([], bcast),
        scratch_shapes=[pltpu.VMEM((SSD_N, SSD_INNER), F32)] + (_exchange_sems(n) if n else []),
        compiler_params=_params(("arbitrary",)),
    )(act, dt_raw, bias, alog, d_x, tril, expand_t, *bcast)
    return res[0], res[1], res[2:]


def _ssd_scan_bwd(dy, act, dt_raw, hprev_all, bias, alog, d_x, tril, expand_t, x1b, scatter=()):
    nc = SEQ // SSD_L
    gw = SSD_INNER // SSD_GROUPS
    sc_arrays, sc_ranges = _scatter_args(scatter)
    nx = len(sc_arrays)

    def body(dy_ref, act_ref, dt_ref, hp_ref, bias_ref, alog_ref, dx_ref, tril_ref, et_ref, x1_ref, *rest):
        dact_ref, ddt_ref, dvec_ref, gdt_ref = rest[nx:nx + 4]
        dh_sc, dd_sc, dcs_sc, dcst_sc = rest[2 * nx + 4:2 * nx + 8]
        i = pl.program_id(0)
        if nx:
            copies = _peer_copies(rest[:nx], rest[nx + 4:2 * nx + 4], rest[2 * nx + 8:], sc_ranges)

            @pl.when(i == 0)
            def _():
                for cp in copies:
                    cp.start()

        @pl.when(i == 0)
        def _():
            dh_sc[...] = jnp.zeros_like(dh_sc)
            dd_sc[...] = jnp.zeros_like(dd_sc)
            gdt_ref[...] = jnp.zeros_like(gdt_ref)
            dvec_ref[...] = jnp.zeros_like(dvec_ref)

        xs = act_ref[:, :SSD_INNER]
        dt_raw_v, bias_v = dt_ref[...], bias_ref[...]
        dt, a_neg, cs, dt_x, _, xdt, ds_x, ecs_x, elast = _ssd_common(
            dt_raw_v, bias_v, alog_ref[...], tril_ref[...], et_ref[...], xs)
        cs_t = cs.T
        rowi = lax.broadcasted_iota(jnp.int32, (SSD_L, SSD_L), 0)
        coli = lax.broadcasted_iota(jnp.int32, (SSD_L, SSD_L), 1)
        causal = rowi >= coli
        lane = lax.broadcasted_iota(jnp.int32, (SSD_L, LANES), 1)
        row_g = lax.broadcasted_iota(jnp.int32, (SSD_L, gw), 0)
        dyv = dy_ref[...]
        dd_sc[0:1, :] += jnp.sum(dyv * xs, axis=0, keepdims=True)
        xdt_b = _mx(xdt)
        xds = xdt * ds_x
        xds_b = _mx(xds)
        dy_b = _mx(dyv)
        dye_b = _mx(dyv * ecs_x)
        dcs_sc[...] = jnp.zeros_like(dcs_sc)
        dcst_sc[...] = jnp.zeros_like(dcst_sc)
        dcs_parts = []
        dxdt_parts = []
        for g in range(SSD_GROUPS):
            gs = slice(g * gw, (g + 1) * gw)
            bcol = slice(SSD_INNER + g * SSD_N, SSD_INNER + (g + 1) * SSD_N)
            ccol = slice(SSD_INNER + 512 + g * SSD_N, SSD_INNER + 512 + (g + 1) * SSD_N)
            bg, cg = _mx(act_ref[:, bcol]), _mx(act_ref[:, ccol])
            cb = _dot_nt(cg, bg)
            hp = hp_ref[0, :, gs]
            hp_b = _mx(hp)
            dh = dh_sc[:, gs]
            dh_b = _mx(dh)
            yoff = _dot(cg, hp_b) * ecs_x[:, gs]
            bdh = _dot(bg, dh_b)
            tt = xds[:, gs] * bdh
            last_row = (jnp.sum(tt, axis=0, keepdims=True)
                        + jnp.sum(dh * hp, axis=0, keepdims=True) * elast[:, gs])
            dcs_parts.append(dyv[:, gs] * yoff - tt + jnp.where(row_g == SSD_L - 1, last_row, 0.0))
            dc_g = _dot_nt(dye_b[:, gs], hp_b)
            db_g = _dot_nt(xds_b[:, gs], dh_b)
            dh_sc[:, gs] = _dot_tn(cg, dye_b[:, gs]) + dh * elast[:, gs]
            wsum = jnp.zeros((SSD_L, SSD_L), F32)
            dxdt_g = []
            for pr in range(4):
                ps = slice(g * gw + pr * LANES, g * gw + (pr + 1) * LANES)
                xp, dyp = xdt_b[:, ps], dy_b[:, ps]
                dxp = jnp.zeros((SSD_L, LANES), F32)
                for j in range(2):
                    hh = g * 8 + pr * 2 + j
                    dm = _ssd_decay(cs, cs_t, hh, causal)
                    mine = (lane >= j * 64) & (lane < (j + 1) * 64)
                    dy_h = jnp.where(mine, dyp, jnp.zeros_like(dyp))
                    wd = _dot_nt(dy_h, xp) * dm
                    wsum = wsum + wd
                    gmat = wd * cb
                    dcs_sc[:, hh:hh + 1] = jnp.sum(gmat, axis=1, keepdims=True)
                    dcst_sc[hh:hh + 1, :] = -jnp.sum(gmat, axis=0, keepdims=True)
                    dxp = dxp + _dot_tn(_mx(cb * dm), dy_h)
                dxdt_g.append(dxp)
            dxdt_parts.append(jnp.concatenate(dxdt_g, axis=1) + bdh * ds_x[:, gs])
            ws_b = _mx(wsum)
            dact_ref[:, ccol] = dc_g + _dot(ws_b, bg)
            dact_ref[:, bcol] = db_g + _dot_tn(ws_b, cg)
        dxdt = jnp.concatenate(dxdt_parts, axis=1)
        dcs_x = jnp.concatenate(dcs_parts, axis=1)
        et = et_ref[...]
        dcs_tot = dcs_sc[...] + dcst_sc[...].T + _fold_heads(dcs_x, et)
        da_dt = _dot_hi((coli >= rowi).astype(F32), dcs_tot)
        ddt = da_dt * a_neg + _fold_heads(dxdt * xs, et)
        ddt_raw = ddt * _sigmoid(dt_raw_v + bias_v)
        ddt_ref[...] = ddt_raw
        gdt_ref[...] += _dot_tn(x1_ref[...], _mx(ddt_raw))
        dvec_ref[0:1, :] += jnp.sum(ddt_raw, axis=0, keepdims=True)
        dvec_ref[1:2, :] += jnp.sum(da_dt * dt, axis=0, keepdims=True) * a_neg
        dact_ref[:, :SSD_INNER] = dyv * dx_ref[...] + dxdt * dt_x

        @pl.when(i == nc - 1)
        def _():
            dvec_ref[2:3, :] = _fold_heads(dd_sc[...], et)[0:1, :]
            if nx:
                for cp in copies:
                    cp.wait()

    const = lambda shape: pl.BlockSpec(shape, lambda c: (0, 0))
    rev = lambda c: (nc - 1 - c, 0)
    res = pl.pallas_call(
        body, name="ssd_scan_bwd", grid=(nc,),
        in_specs=[pl.BlockSpec((SSD_L, SSD_INNER), rev), pl.BlockSpec((SSD_L, SSD_CONV), rev),
                  pl.BlockSpec((SSD_L, LANES), rev),
                  pl.BlockSpec((1, SSD_N, SSD_INNER), lambda c: (nc - 1 - c, 0, 0)),
                  const((1, LANES)), const((1, LANES)), const((1, SSD_INNER)), const((SSD_L, SSD_L)),
                  const((SSD_INNER, LANES)), pl.BlockSpec((SSD_L, D_MODEL), rev)] + [ANY] * nx,
        out_specs=[pl.BlockSpec((SSD_L, SSD_CONV), rev), pl.BlockSpec((SSD_L, LANES), rev), const((SUBLANES, LANES)),
                   const((D_MODEL, LANES))] + [ANY] * nx,
        out_shape=[jax.ShapeDtypeStruct((SEQ, SSD_CONV), F32), jax.ShapeDtypeStruct((SEQ, LANES), F32),
                   jax.ShapeDtypeStruct((SUBLANES, LANES), F32), jax.ShapeDtypeStruct((D_MODEL, LANES), F32)]
        + _exchange_shapes(sc_arrays, []),
        scratch_shapes=[pltpu.VMEM((SSD_N, SSD_INNER), F32), pltpu.VMEM((SUBLANES, SSD_INNER), F32),
                        pltpu.VMEM((SSD_L, LANES), F32), pltpu.VMEM((LANES, SSD_L), F32)]
        + (_exchange_sems(nx) if nx else []),
        compiler_params=_params(("arbitrary",)),
    )(dy, act, dt_raw, hprev_all, bias, alog, d_x, tril, expand_t, x1b, *sc_arrays)
    return res[0], res[1], res[2], res[3], res[4:]


L1_T = 512


def _resident(shape):
    return pl.BlockSpec(shape, lambda i: (0, 0), pipeline_mode=pl.Buffered(1))


def _gated_norm(y, z, nw):
    y2 = y * _silu(z)
    gw = SSD_INNER // SSD_GROUPS
    outs, xhs, rss = [], [], []
    for g in range(SSD_GROUPS):
        gs = slice(g * gw, (g + 1) * gw)
        v = y2[:, gs]
        rs = lax.rsqrt(jnp.mean(v * v, axis=-1, keepdims=True) + 1e-6)
        xhs.append(v * rs)
        rss.append(rs)
        outs.append(v * rs * nw[:, gs])
    return outs, xhs, rss


def _l1_out(y, z, nw, w_out, x1, g, b, target):
    t = L1_T

    def body(y_ref, z_ref, nw_ref, w_ref, x1_ref, g_ref, b_ref, tg_ref, dv_ref, dgb_ref, loss_ref, gw_ref, gw_sc):
        @pl.when(pl.program_id(0) == 0)
        def _():
            dgb_ref[...] = jnp.zeros_like(dgb_ref)
            loss_ref[...] = jnp.zeros_like(loss_ref)
            gw_sc[...] = jnp.zeros_like(gw_sc)

        outs, _, _ = _gated_norm(y_ref[...], z_ref[...], nw_ref[...])
        yn = _mx(jnp.concatenate(outs, axis=1))
        v = DN_ALPHA * x1_ref[...] + _dot(yn, w_ref[...])
        gv = g_ref[...]
        err = _ln(v, gv, b_ref[...]) - tg_ref[...]
        rowsum = jnp.sum(err * err, axis=1, keepdims=True)
        loss_ref[...] += 0.5 * jnp.sum(rowsum, axis=0, keepdims=True) / D_MODEL
        dv, dg, db = _ln_bwd(v, gv, err / D_MODEL)
        dv_ref[...] = dv
        dgb_ref[0:1, :] += dg
        dgb_ref[1:2, :] += db
        gw_sc[...] += _dot_tn(yn, _mx(dv))

        @pl.when(pl.program_id(0) == SEQ // t - 1)
        def _():
            gw_ref[...] = _mx(gw_sc[...])

    wide = pl.BlockSpec((t, SSD_INNER), lambda i: (i, 0))
    full = pl.BlockSpec((t, D_MODEL), lambda i: (i, 0))
    vec = pl.BlockSpec((1, D_MODEL), lambda i: (0, 0))
    return pl.pallas_call(
        body, name="l1_out", grid=(SEQ // t,),
        in_specs=[wide, wide, pl.BlockSpec((1, SSD_INNER), lambda i: (0, 0)),
                  _resident((SSD_INNER, D_MODEL)), full, vec, vec, full],
        out_specs=[full, pl.BlockSpec((SUBLANES, D_MODEL), lambda i: (0, 0)),
                   pl.BlockSpec((SUBLANES, LANES), lambda i: (0, 0)), _resident((SSD_INNER, D_MODEL))],
        out_shape=[jax.ShapeDtypeStruct((SEQ, D_MODEL), F32), jax.ShapeDtypeStruct((SUBLANES, D_MODEL), F32),
                   jax.ShapeDtypeStruct((SUBLANES, LANES), F32), jax.ShapeDtypeStruct((SSD_INNER, D_MODEL), MXU_DTYPE)],
        scratch_shapes=[pltpu.VMEM((SSD_INNER, D_MODEL), F32)],
        compiler_params=_params(("arbitrary",)),
    )(y, z, nw, w_out, x1, g, b, target)


def _l1_gate_bwd(dv1, w_out, y, z, nw, x1b):
    t = L1_T
    gw = SSD_INNER // SSD_GROUPS

    def body(dv_ref, w_ref, y_ref, z_ref, nw_ref, x1_ref, dy_ref, dz_ref, dnw_ref, gw_ref):
        @pl.when(pl.program_id(0) == 0)
        def _():
            dnw_ref[...] = jnp.zeros_like(dnw_ref)
            gw_ref[...] = jnp.zeros_like(gw_ref)

        dyn = _dot_nt(_mx(dv_ref[...]), w_ref[...])
        yv, zv, nwv = y_ref[...], z_ref[...], nw_ref[...]
        _, xhs, rss = _gated_norm(yv, zv, nwv)
        sz, dsz = _silu(zv), _dsilu(zv)
        for g in range(SSD_GROUPS):
            gs = slice(g * gw, (g + 1) * gw)
            d_out = dyn[:, gs]
            xh = xhs[g]
            dnw_ref[0:1, gs] += jnp.sum(d_out * xh, axis=0, keepdims=True)
            dxh = d_out * nwv[:, gs]
            dy2 = rss[g] * (dxh - xh * jnp.mean(dxh * xh, axis=-1, keepdims=True))
            dy_ref[:, gs] = dy2 * sz[:, gs]
            dz_ref[:, gs] = _mx(dy2 * yv[:, gs] * dsz[:, gs])
        gw_ref[...] += _dot_tn(x1_ref[...], dz_ref[...])

    wide = pl.BlockSpec((t, SSD_INNER), lambda i: (i, 0))
    return pl.pallas_call(
        body, name="l1_gate_bwd", grid=(SEQ // t,),
        in_specs=[pl.BlockSpec((t, D_MODEL), lambda i: (i, 0)), _resident((SSD_INNER, D_MODEL)),
                  wide, wide, pl.BlockSpec((1, SSD_INNER), lambda i: (0, 0)), pl.BlockSpec((t, D_MODEL), lambda i: (i, 0))],
        out_specs=[wide, wide, pl.BlockSpec((SUBLANES, SSD_INNER), lambda i: (0, 0)),
                   _resident((D_MODEL, SSD_INNER))],
        out_shape=[jax.ShapeDtypeStruct((SEQ, SSD_INNER), F32), jax.ShapeDtypeStruct((SEQ, SSD_INNER), MXU_DTYPE),
                   jax.ShapeDtypeStruct((SUBLANES, SSD_INNER), F32), jax.ShapeDtypeStruct((D_MODEL, SSD_INNER), F32)],
        compiler_params=_params(("arbitrary",)),
    )(dv1, w_out, y, z, nw, x1b)


MESH = pl.DeviceIdType.MESH
ANY = pl.BlockSpec(memory_space=pl.ANY)


def _flip(v, bit):
    return 1 - v if bit else v


def _all_gather(blocks, name):
    n = len(blocks)

    def body(*refs):
        x_refs, out_refs = refs[:n], refs[n:2 * n]
        send_sems, recv_sems, local_sems = refs[2 * n:]
        mx, my, mc = lax.axis_index("x"), lax.axis_index("y"), lax.axis_index("c")
        me, sibling = (mx, my, mc), (mx, my, 1 - mc)
        chips = [(1 - mx, my), (mx, 1 - my), (1 - mx, 1 - my)]

        def copy(a, k, block, to, own=False):
            px, py, pc = block
            slot = out_refs[a].at[4 * px + 2 * py + pc]
            return pltpu.make_async_remote_copy(
                src_ref=x_refs[a] if own else slot, dst_ref=slot,
                send_sem=send_sems.at[7 * a + k], recv_sem=recv_sems.at[7 * a + k], device_id=to, device_id_type=MESH)

        mine = [pltpu.make_async_copy(x_refs[a], out_refs[a].at[4 * mx + 2 * my + mc], local_sems.at[a])
                for a in range(n)]
        first = []
        for a in range(n):
            mine[a].start()
            first.append(copy(a, 0, me, sibling, own=True))
            first += [copy(a, 1 + j, me, (*chip, mc), own=True) for j, chip in enumerate(chips)]
        for cp in first:
            cp.start()
        passed = []
        for j, chip in enumerate(chips):
            for a in range(n):
                copy(a, 1 + j, (*chip, mc), me).wait_recv()
                fwd = copy(a, 4 + j, (*chip, mc), sibling)
                fwd.start()
                passed.append(fwd)
        for a in range(n):
            copy(a, 0, sibling, me).wait_recv()
            for j, chip in enumerate(chips):
                copy(a, 4 + j, (*chip, 1 - mc), me).wait_recv()
        for cp in first + passed:
            cp.wait_send()
        for cp in mine:
            cp.wait()

    return pl.pallas_call(
        body, name=name, in_specs=[ANY] * n, out_specs=[ANY] * n,
        out_shape=[jax.ShapeDtypeStruct((N_DEV,) + b.shape, b.dtype) for b in blocks],
        scratch_shapes=[pltpu.SemaphoreType.DMA((7 * n,)), pltpu.SemaphoreType.DMA((7 * n,)),
                        pltpu.SemaphoreType.DMA((n,))],
    )(*blocks)


def _l0_in(x, w0p, bcast=()):
    n = len(bcast)
    tm, tn = 1024, 1024
    gi, gj = SEQ // tm, 2048 // tn

    def body(x_ref, w_ref, *rest):
        o_ref, xb_ref = rest[n], rest[n + 1]
        i, j = pl.program_id(0), pl.program_id(1)
        if n:
            copies = _peer_copies(rest[:n], rest[n + 2:2 * n + 2], rest[2 * n + 2:], [])

            @pl.when((i == 0) & (j == 0))
            def _():
                for cp in copies:
                    cp.start()

        xb = _mx(x_ref[...])
        xb_ref[...] = xb
        o_ref[...] = _dot(xb, w_ref[...])

        if n:
            @pl.when((i == gi - 1) & (j == gj - 1))
            def _():
                for cp in copies:
                    cp.wait()

    res = pl.pallas_call(
        body, name="l0_in", grid=(gi, gj),
        in_specs=[pl.BlockSpec((tm, D_MODEL), lambda i, j: (i, 0)), pl.BlockSpec((D_MODEL, tn), lambda i, j: (0, j))]
        + [ANY] * n,
        out_specs=[pl.BlockSpec((tm, tn), lambda i, j: (i, j)), pl.BlockSpec((tm, D_MODEL), lambda i, j: (i, 0))]
        + [ANY] * n,
        out_shape=[jax.ShapeDtypeStruct((SEQ, 2048), F32), jax.ShapeDtypeStruct((SEQ, D_MODEL), MXU_DTYPE)]
        + _exchange_shapes([], bcast),
        scratch_shapes=_exchange_sems(n) if n else [],
        compiler_params=_params(("arbitrary", "arbitrary")),
    )(x, w0p, *bcast)
    return res[0], res[1], res[2:]


def _l0_dx(dxr, dgate, dtail, w0p, dv0, scatter=(), bcast=()):
    arrays, ranges = _scatter_args(scatter)
    n = len(arrays) + len(bcast)
    tm = 1024
    steps = SEQ // tm

    def body(dxr_ref, dg_ref, dt_ref, w_ref, dv_ref, *rest):
        o_ref = rest[n]
        i = pl.program_id(0)
        if n:
            copies = _peer_copies(rest[:n], rest[n + 1:2 * n + 1], rest[2 * n + 1:], ranges)

            @pl.when(i == 0)
            def _():
                for cp in copies:
                    cp.start()

        o_ref[...] = (DN_ALPHA * dv_ref[...] + _dot_nt(dg_ref[...], w_ref[:, 0:1024])
                      + _dot_nt(dxr_ref[...], w_ref[:, 1024:1536]) + _dot_nt(dt_ref[...], w_ref[:, 1536:2048]))

        if n:
            @pl.when(i == steps - 1)
            def _():
                for cp in copies:
                    cp.wait()

    rows = lambda w: pl.BlockSpec((tm, w), lambda i: (i, 0))
    res = pl.pallas_call(
        body, name="l0_dx", grid=(steps,),
        in_specs=[rows(512), rows(1024), rows(512), pl.BlockSpec((D_MODEL, 2048), lambda i: (0, 0)), rows(D_MODEL)]
        + [ANY] * n,
        out_specs=[rows(D_MODEL)] + [ANY] * n,
        out_shape=[jax.ShapeDtypeStruct((SEQ, D_MODEL), F32)] + _exchange_shapes(arrays, bcast),
        scratch_shapes=_exchange_sems(n) if n else [],
        compiler_params=_params(("arbitrary",)),
    )(dxr, dgate, dtail, w0p, dv0, *arrays, *bcast)
    return res[0], res[1:]


def _scatter_args(scatter):
    arrays = [s[0] if isinstance(s, tuple) else s for s in scatter]
    ranges = [s[1] if isinstance(s, tuple) else (0, N_DEV) for s in scatter]
    return arrays, ranges


def _exchange_shapes(scatter, bcast):
    return ([jax.ShapeDtypeStruct((N_DEV,) + a.shape[1:], a.dtype) for a in scatter]
            + [jax.ShapeDtypeStruct((N_DEV,) + a.shape, a.dtype) for a in bcast])


def _exchange_sems(n):
    return [pltpu.SemaphoreType.DMA((7 * n,)), pltpu.SemaphoreType.DMA((7 * n,)), pltpu.SemaphoreType.DMA((n,))]


class _GuardedCopy:
    def __init__(self, copy, send=None, recv=None, local=False):
        self.copy, self.send, self.recv, self.local = copy, send, recv, local

    @staticmethod
    def _run(pred, fn):
        if pred is None:
            fn()
        else:
            pl.when(pred)(fn)

    def start(self):
        self._run(self.send, self.copy.start)

    def wait(self):
        if self.local:
            self._run(self.send, self.copy.wait)
        else:
            self._run(self.send, self.copy.wait_send)
            self._run(self.recv, self.copy.wait_recv)


def _peer_copies(in_refs, out_refs, sems, ranges):
    send_sems, recv_sems, local_sems = sems
    n, ns = len(in_refs), len(ranges)
    mx, my, mc = lax.axis_index("x"), lax.axis_index("y"), lax.axis_index("c")
    me = 4 * mx + 2 * my + mc

    def src(a, slot):
        return in_refs[a].at[slot - ranges[a][0]] if a < ns else in_refs[a]

    def member(a, dev):
        if a >= ns or ranges[a] == (0, N_DEV):
            return None
        return (dev >= ranges[a][0]) & (dev < ranges[a][1])

    copies = [_GuardedCopy(pltpu.make_async_copy(src(a, me), out_refs[a].at[me], local_sems.at[a]),
                           send=member(a, me), local=True) for a in range(n)]
    for k in range(1, N_DEV):
        px, py, pc = _flip(mx, (k >> 2) & 1), _flip(my, (k >> 1) & 1), _flip(mc, k & 1)
        peer = 4 * px + 2 * py + pc
        for a in range(n):
            copies.append(_GuardedCopy(pltpu.make_async_remote_copy(
                src_ref=src(a, peer), dst_ref=out_refs[a].at[me],
                send_sem=send_sems.at[7 * a + k - 1], recv_sem=recv_sems.at[7 * a + k - 1],
                device_id=(px, py, pc), device_id_type=MESH), send=member(a, peer), recv=member(a, me)))
    return copies


def _segments(col_map, width):
    segs = []
    for lo, hi, arr, alo in col_map:
        for s in range(N_DEV):
            a, b = max(lo, s * width), min(hi, (s + 1) * width)
            if a < b:
                segs.append((s, a - s * width, b - a, arr, alo + a - lo))
    return segs


COPY_ROWS = 256


def _unshard(g8, col_map, widths, name):
    _, r, w = g8.shape
    rb = min(r, COPY_ROWS)
    segs = _segments(col_map, w)

    def body(g_ref, *o_refs):
        for o_ref in o_refs:
            o_ref[...] = jnp.zeros_like(o_ref)
        for s, llo, n, arr, alo in segs:
            o_refs[arr][:, alo:alo + n] = g_ref[s, :, llo:llo + n]

    return pl.pallas_call(
        body, name=name, grid=(r // rb,),
        in_specs=[pl.BlockSpec((N_DEV, rb, w), lambda i: (0, i, 0))],
        out_specs=[pl.BlockSpec((rb, n), lambda i: (i, 0)) for n in widths],
        out_shape=[jax.ShapeDtypeStruct((r, n), g8.dtype) for n in widths],
        compiler_params=_params(("parallel",)),
    )(g8)


def _reshard(srcs, col_map, w, dtype, name, shards=(0, N_DEV)):
    r = srcs[0].shape[0]
    rb = min(r, COPY_ROWS)
    lo, hi = shards
    segs = [sg for sg in _segments(col_map, w) if lo <= sg[0] < hi]

    def body(*refs):
        o_ref = refs[-1]
        for s, llo, n, arr, alo in segs:
            o_ref[s - lo, :, llo:llo + n] = refs[arr][:, alo:alo + n].astype(dtype)

    return pl.pallas_call(
        body, name=name, grid=(r // rb,),
        in_specs=[pl.BlockSpec((rb, a.shape[1]), lambda i: (i, 0)) for a in srcs],
        out_specs=pl.BlockSpec((hi - lo, rb, w), lambda i: (0, i, 0)),
        out_shape=jax.ShapeDtypeStruct((hi - lo, r, w), dtype),
        compiler_params=_params(("parallel",)),
    )(*srcs)


def _adamw(parts, w, m, v, name):
    r, c = w.shape
    tr = COPY_ROWS if r % COPY_ROWS == 0 else r

    def body(p_ref, w_ref, m_ref, v_ref, g_ref, d_ref, mo_ref, vo_ref):
        g = p_ref[0].astype(F32)
        for s in range(1, N_DEV):
            g = g + p_ref[s].astype(F32)
        g_ref[...] = g
        d_ref[...], mo_ref[...], vo_ref[...] = _adamw_math(g, w_ref[...], m_ref[...], v_ref[...])

    blk = pl.BlockSpec((tr, c), lambda i: (i, 0))
    out = jax.ShapeDtypeStruct((r, c), F32)
    return pl.pallas_call(
        body, name=name, grid=(r // tr,),
        in_specs=[pl.BlockSpec((N_DEV, tr, c), lambda i: (0, i, 0)), blk, blk, blk],
        out_specs=[blk, blk, blk, blk], out_shape=[out, out, out, out],
        compiler_params=_params(("parallel",)),
    )(parts, w, m, v)


def _adamw_cols(parts, wt, mt, vt, name):
    _, r, c = parts.shape
    per = r // LANES
    linear = wt.shape != (c, r)
    assert wt.shape == ((c * per, LANES) if linear else (c, r)) and (per == SUBLANES or not linear)
    n = min(c, LANES)
    starts = list(range(0, c - n + 1, LANES)) + ([c - n] if c % n else [])

    def body(p_ref, w_ref, m_ref, v_ref, g_ref, d_ref, mo_ref, vo_ref, gt_sc, pad_sc):
        for lo in starts:
            g = p_ref[0, :, lo:lo + n].astype(F32)
            for s in range(1, N_DEV):
                g = g + p_ref[s, :, lo:lo + n].astype(F32)
            if n < LANES:
                pad_sc[...] = jnp.zeros_like(pad_sc)
                pad_sc[:, 0:n] = g
                g = pad_sc[...]
            gt = g.T
            if linear:
                for k in range(per):
                    gt_sc[pl.ds(lo * per + k, n, stride=per), :] = gt[0:n, k * LANES:(k + 1) * LANES]
            else:
                gt_sc[lo:lo + n, :] = gt[0:n]
        g = gt_sc[...]
        d, mn, vn = _adamw_math(g, w_ref[...], m_ref[...], v_ref[...])
        for ref, val in ((g_ref, g), (d_ref, d), (mo_ref, mn), (vo_ref, vn)):
            ref[...] = val.reshape(c, 1, r) if linear else val

    out = jax.ShapeDtypeStruct((c, 1, r) if linear else (c, r), F32)
    return pl.pallas_call(
        body, name=name, out_shape=[out, out, out, out],
        scratch_shapes=[pltpu.VMEM(wt.shape, F32), pltpu.VMEM((r, LANES), F32)],
        compiler_params=pltpu.CompilerParams(vmem_limit_bytes=VMEM_LIMIT),
    )(parts, wt, mt, vt)


def _adamw_math(g, w, m, v):
    mn = ADAM_B1 * m + (1.0 - ADAM_B1) * g
    vn = ADAM_B2 * v + (1.0 - ADAM_B2) * (g * g)
    m_hat = mn / (1.0 - ADAM_B1 ** ADAM_STEP)
    v_hat = vn / (1.0 - ADAM_B2 ** ADAM_STEP)
    return -ADAM_LR * (m_hat / (jnp.sqrt(v_hat) + ADAM_EPS) + ADAM_WD * w), mn, vn


SMALL = (("ab_conv_w", 0, 4, 64), ("ssd_conv_w", 4, 4, 384), ("ssd_conv_b", 8, 1, 384), ("ssd_norm", 9, 1, 256),
         ("ssd_ln_g", 10, 1, 128), ("ssd_ln_b", 11, 1, 128))
VECS = (("ab_conv_b", 512), ("ab_gate_a_b", 512), ("ab_gate_x_b", 512), ("ab_lambda", 512), ("mla_q_norm", 256),
        ("mla_kv_norm", 128), ("ab_ln_g", 1024), ("ab_ln_b", 1024), ("ssd_dt_bias", 32), ("ssd_a_log", 32),
        ("ssd_d", 32))
GATES = ("ab_gate_a_w", "ab_gate_x_w")
SMALL_NAMES = tuple(n for n, *_ in SMALL) + tuple(n for n, _ in VECS) + GATES
VMEM_WHOLE = pl.BlockSpec(memory_space=pltpu.VMEM)


def _view2d(name, a):
    if name in GATES:
        return a.reshape(RNN_W, 64)
    return a[0] if a.ndim == 3 else a


def _unshard_small(g):
    widths = (512, 3072, 3072, 2048, 1024, 1024)

    def body(*refs):
        ins, outs = refs[:6], refs[6:]
        outs[0][...] = jnp.zeros_like(outs[0])
        outs[1][...] = jnp.zeros_like(outs[1])
        for (_, _, nr, c), i_ref, o_ref in zip(SMALL, ins, outs):
            for j in range(N_DEV):
                o_ref[0:nr, j * c:(j + 1) * c] = i_ref[j]

    return pl.pallas_call(
        body, name="unshard_small", in_specs=[VMEM_WHOLE] * 6, out_specs=[VMEM_WHOLE] * 6,
        out_shape=[jax.ShapeDtypeStruct((SUBLANES if nr == 4 else 1, w), F32) for (_, _, nr, _), w in zip(SMALL, widths)],
    )(*g)


def _prep_repl(ga, gx, dt_bias, a_log, d):
    def body(ga_ref, gx_ref, b_ref, al_ref, d_ref, wa_ref, wx_ref, b128_ref, al128_ref, dx_ref):
        wa_ref[...] = jnp.zeros_like(wa_ref)
        wx_ref[...] = jnp.zeros_like(wx_ref)
        for hd in range(8):
            hs = slice(hd * 64, (hd + 1) * 64)
            wa_ref[hs, hs] = _mx(ga_ref[hs, :])
            wx_ref[hs, hs] = _mx(gx_ref[hs, :])
        b128_ref[...] = jnp.zeros_like(b128_ref)
        al128_ref[...] = jnp.zeros_like(al128_ref)
        b128_ref[:, 0:SSD_HEADS] = b_ref[...]
        al128_ref[:, 0:SSD_HEADS] = al_ref[...]
        dv = d_ref[...]
        for hd in range(SSD_HEADS):
            dx_ref[:, hd * SSD_P:(hd + 1) * SSD_P] = jnp.broadcast_to(dv[:, hd:hd + 1], (1, SSD_P))

    return pl.pallas_call(
        body, name="prep_repl", in_specs=[VMEM_WHOLE] * 5, out_specs=[VMEM_WHOLE] * 5,
        out_shape=[jax.ShapeDtypeStruct((RNN_W, RNN_W), MXU_DTYPE), jax.ShapeDtypeStruct((RNN_W, RNN_W), MXU_DTYPE),
                   jax.ShapeDtypeStruct((1, LANES), F32), jax.ShapeDtypeStruct((1, LANES), F32),
                   jax.ShapeDtypeStruct((1, SSD_INNER), F32)],
    )(ga, gx, dt_bias, a_log, d)


LOSS_ROW = 11


def _pack_small(dvec0, g_wa, g_wx, dqnw, dknw, dgb0, dvec1, dcw1, dnw, dgb1, loss8):
    def body(dvec0_ref, gwa_ref, gwx_ref, dqn_ref, dkn_ref, dgb0_ref, dvec1_ref, dcw1_ref, dnw_ref, dgb1_ref,
             loss_ref, sm_ref, vec_ref, gg_ref):
        sm_ref[...] = jnp.zeros_like(sm_ref)
        vec_ref[...] = jnp.zeros_like(vec_ref)
        sharded = ((dvec0_ref, 4), (dcw1_ref, 0), (dcw1_ref, 4), (dnw_ref, 0), (dgb1_ref, 0), (dgb1_ref, 1))
        for (_, r0, nr, c), (src, sr) in zip(SMALL, sharded):
            for j in range(N_DEV):
                sm_ref[j, r0:r0 + nr, 0:c] = src[sr:sr + nr, j * c:(j + 1) * c]
        vectors = ((dvec0_ref, 3), (dvec0_ref, 0), (dvec0_ref, 1), (dvec0_ref, 2), (dqn_ref, 0), (dkn_ref, 0),
                   (dgb0_ref, 0), (dgb0_ref, 1), (dvec1_ref, 0), (dvec1_ref, 1), (dvec1_ref, 2))
        for row, ((_, c), (src, sr)) in enumerate(zip(VECS, vectors)):
            vec_ref[row:row + 1, 0:c] = src[sr:sr + 1, 0:c]
        vec_ref[LOSS_ROW:LOSS_ROW + 1, 0:LANES] = loss_ref[0:1, :]
        for hd in range(8):
            hs = slice(hd * 64, (hd + 1) * 64)
            gg_ref[hs, 0:64] = _mx(gwa_ref[hs, hs])
            gg_ref[hs, 64:128] = _mx(gwx_ref[hs, hs])

    return pl.pallas_call(
        body, name="pack_small", in_specs=[VMEM_WHOLE] * 11, out_specs=[VMEM_WHOLE] * 3,
        out_shape=[jax.ShapeDtypeStruct((N_DEV, 16, 384), F32), jax.ShapeDtypeStruct((16, 1024), F32),
                   jax.ShapeDtypeStruct((RNN_W, LANES), MXU_DTYPE)],
    )(dvec0, g_wa, g_wx, dqnw, dknw, dgb0, dvec1, dcw1, dnw, dgb1, loss8)


def _adamw_small(recv_sm, recv_vec, recv_gg, wmv):
    plan = ([(0, r0, nr, c) for _, r0, nr, c in SMALL] + [(1, row, 1, c) for row, (_, c) in enumerate(VECS)]
            + [(2, 0, RNN_W, 0), (2, 0, RNN_W, 64)])
    n = len(plan)

    def body(*refs):
        recv, ins, outs = refs[:3], refs[3:3 + 3 * n], refs[3 + 3 * n:]
        for i, (src, r0, nr, c) in enumerate(plan):
            cols = slice(c, c + 64) if src == 2 else slice(0, c)
            g = recv[src][0, r0:r0 + nr, cols].astype(F32)
            for s in range(1, N_DEV):
                g = g + recv[src][s, r0:r0 + nr, cols].astype(F32)
            w_ref, m_ref, v_ref = ins[3 * i:3 * i + 3]
            outs[4 * i][...] = g
            outs[4 * i + 1][...], outs[4 * i + 2][...], outs[4 * i + 3][...] = _adamw_math(
                g, w_ref[...], m_ref[...], v_ref[...])
        loss = recv[1][0, LOSS_ROW:LOSS_ROW + 1, 0:LANES]
        for s in range(1, N_DEV):
            loss = loss + recv[1][s, LOSS_ROW:LOSS_ROW + 1, 0:LANES]
        outs[4 * n][...] = loss

    flat = [a for t in wmv for a in t]
    return pl.pallas_call(
        body, name="adamw_small", in_specs=[VMEM_WHOLE] * (3 + 3 * n), out_specs=[VMEM_WHOLE] * (4 * n + 1),
        out_shape=[jax.ShapeDtypeStruct(t[0].shape, F32) for t in wmv for _ in range(4)]
        + [jax.ShapeDtypeStruct((1, LANES), F32)],
    )(recv_sm, recv_vec, recv_gg, *flat)


BIG_L0 = ("ab_w_in", "ab_w_out", "mla_w_uq", "mla_w_ukv")
BIG_L1 = ("ssd_w_in", "ssd_w_out")
COLUMN_SHARDED = ("ab_w_in", "mla_w_uq", "ssd_w_in")

MAP_W0 = ((0, 512, 0, 1024), (512, 1536, 0, 0), (1536, 1920, 0, 1536), (1920, 1952, 0, 1984))
MAP_W1 = ((0, 2048, 0, 0), (2048, 5120, 1, 0), (5120, 5152, 2, 0))
MAP_WQ = tuple((96 * hd, 96 * hd + 96, 0, 128 * hd) for hd in range(8))
MAP_WKV = (tuple((128 * hd, 128 * hd + 64, 0, 128 * hd) for hd in range(8))
           + tuple((128 * hd + 64, 128 * hd + 128, 0, 1024 + 64 * hd) for hd in range(8)))
MAP_G0 = ((0, 512, 0, 0), (512, 1536, 1, 0), (1536, 1920, 2, 0), (1920, 1952, 2, 448))
W0_EARLY, W0_LATE = (0, 6), (6, 8)


def kernel(x, positions, ab_w_in, ab_conv_w, ab_conv_b, ab_gate_a_w, ab_gate_a_b, ab_gate_x_w, ab_gate_x_b, ab_lambda, mla_q_norm, mla_kv_norm, mla_w_uq, mla_w_ukv, ab_w_out, ab_ln_g, ab_ln_b, ssd_w_in, ssd_conv_w, ssd_conv_b, ssd_dt_bias, ssd_a_log, ssd_d, ssd_norm, ssd_w_out, ssd_ln_g, ssd_ln_b, loss_target, m_ab_w_in, m_ab_conv_w, m_ab_conv_b, m_ab_gate_a_w, m_ab_gate_a_b, m_ab_gate_x_w, m_ab_gate_x_b, m_ab_lambda, m_mla_q_norm, m_mla_kv_norm, m_mla_w_uq, m_mla_w_ukv, m_ab_w_out, m_ab_ln_g, m_ab_ln_b, m_ssd_w_in, m_ssd_conv_w, m_ssd_conv_b, m_ssd_dt_bias, m_ssd_a_log, m_ssd_d, m_ssd_norm, m_ssd_w_out, m_ssd_ln_g, m_ssd_ln_b, v_ab_w_in, v_ab_conv_w, v_ab_conv_b, v_ab_gate_a_w, v_ab_gate_a_b, v_ab_gate_x_w, v_ab_gate_x_b, v_ab_lambda, v_mla_q_norm, v_mla_kv_norm, v_mla_w_uq, v_mla_w_ukv, v_ab_w_out, v_ab_ln_g, v_ab_ln_b, v_ssd_w_in, v_ssd_conv_w, v_ssd_conv_b, v_ssd_dt_bias, v_ssd_a_log, v_ssd_d, v_ssd_norm, v_ssd_w_out, v_ssd_ln_g, v_ssd_ln_b):
    args = dict(locals())
    bf = MXU_DTYPE
    big = {n: [args[pre + n][0] for pre in ("", "m_", "v_")] for n in BIG_L0 + BIG_L1}
    sml = {n: [_view2d(n, args[pre + n]) for pre in ("", "m_", "v_")] for n in SMALL_NAMES}

    w0_8, cw0_8 = _all_gather([big["ab_w_in"][0].astype(bf), sml["ab_conv_w"][0]], "gather_params")
    p = {"cw0_8": cw0_8, "l0_blocks": [big[n][0].astype(bf) for n in BIG_L0[1:]] + [sml[n][0] for n, *_ in SMALL[1:]]}
    p["w0p"], = _unshard(w0_8, MAP_W0, (2048,), "unshard_w0")
    p["wa"], p["wx"], p["dt_bias"], p["a_log"], p["d_x"] = _prep_repl(
        sml["ab_gate_a_w"][0], sml["ab_gate_x_w"][0], sml["ssd_dt_bias"][0], sml["ssd_a_log"][0], sml["ssd_d"][0])
    for key, n in (("cb0", "ab_conv_b"), ("ba", "ab_gate_a_b"), ("bx", "ab_gate_x_b"), ("lam", "ab_lambda"),
                   ("qn_w", "mla_q_norm"), ("kn_w", "mla_kv_norm"), ("g0", "ab_ln_g"), ("b0", "ab_ln_b")):
        p[key] = sml[n][0]

    _, recv_early, recv, _, grad_x = _local_step(
        x[0], positions[0], loss_target[0], p, [big[n][0].astype(bf) for n in BIG_L1])

    me = 4 * lax.axis_index("x") + 2 * lax.axis_index("y") + lax.axis_index("c")
    parts = dict(recv_early, ab_w_in=jnp.where(me >= W0_LATE[0], recv[0], recv_early["ab_w_in"]),
                 mla_w_uq=recv[1], mla_w_ukv=recv[2])

    outs = {}
    kinds = ("grad", "delta", "new_m", "new_v")
    for n in BIG_L0 + BIG_L1:
        if n in COLUMN_SHARDED:
            rows, cols = big[n][0].shape
            if rows == SUBLANES * LANES:
                wmv_t = [jnp.transpose(args[pre + n], (2, 0, 1)).reshape(cols * SUBLANES, LANES) for pre in ("", "m_", "v_")]
                back = lambda res: jnp.transpose(res, (1, 2, 0))
            else:
                wmv_t = [args[pre + n][0].T for pre in ("", "m_", "v_")]
                back = lambda res: res.T[None]
            for kind, res in zip(kinds, _adamw_cols(parts[n], *wmv_t, "adamw_" + n)):
                outs[kind, n] = back(res)
            continue
        for kind, res in zip(kinds, _adamw(parts[n], *big[n], "adamw_" + n)):
            outs[kind, n] = res[None]
    res = _adamw_small(*recv[3:], [sml[n] for n in SMALL_NAMES])
    for i, n in enumerate(SMALL_NAMES):
        for k, kind in enumerate(kinds):
            outs[kind, n] = res[4 * i + k].reshape(args[n].shape)

    loss = res[4 * len(SMALL_NAMES)][0, 0]
    order = ["ab_w_in", "ab_conv_w", "ab_conv_b", "ab_gate_a_w", "ab_gate_a_b", "ab_gate_x_w", "ab_gate_x_b",
             "ab_lambda", "mla_q_norm", "mla_kv_norm", "mla_w_uq", "mla_w_ukv", "ab_w_out", "ab_ln_g", "ab_ln_b",
             "ssd_w_in", "ssd_conv_w", "ssd_conv_b", "ssd_dt_bias", "ssd_a_log", "ssd_d", "ssd_norm", "ssd_w_out",
             "ssd_ln_g", "ssd_ln_b"]
    return (loss, grad_x[None], *[outs[kind, n] for kind in ("grad", "delta", "new_m", "new_v") for n in order])


def _local_step(x, pos, target, p, l1_blocks):
    bf = MXU_DTYPE
    inv_freq = 10000.0 ** (-jnp.arange(0, 32, 2, dtype=F32) / 32)
    ang = inv_freq[:, None] * pos.astype(F32)[None, :]
    cos, sin = jnp.cos(ang), jnp.sin(ang)
    zeros = lambda n: jnp.zeros((n, SEQ), F32)
    tc = jnp.concatenate([jnp.ones((64, SEQ), F32), cos, cos, zeros(32)], axis=0)
    tsa = jnp.concatenate([zeros(64), -sin, zeros(48)], axis=0)
    tsb = jnp.concatenate([zeros(80), sin, zeros(32)], axis=0)

    w0p, wa, wxg = (p[k] for k in ("w0p", "wa", "wx"))
    cb0, ba, bx, lam = (p[k] for k in ("cb0", "ba", "bx", "lam"))
    qn_w, kn_w, g0, b0 = (p[k] for k in ("qn_w", "kn_w", "g0", "b0"))
    dt_bias, a_log, d_x = (p[k] for k in ("dt_bias", "a_log", "d_x"))
    tril = jnp.tril(jnp.ones((SSD_L, SSD_L), F32))
    expand_t = (jnp.arange(SSD_INNER)[:, None] // SSD_P == jnp.arange(LANES)[None, :]).astype(jnp.bfloat16)

    proj0, xb, l0_8 = _l0_in(x, w0p, bcast=p["l0_blocks"])
    wo0 = l0_8[0].reshape(D_MODEL, D_MODEL)
    wq, = _unshard(l0_8[1], MAP_WQ, (1024,), "unshard_wq")
    wkv, = _unshard(l0_8[2], MAP_WKV, (1536,), "unshard_wkv")
    cw0, cw1, cb1, nw, g1, b1 = _unshard_small([p["cw0_8"]] + list(l0_8[3:]))
    xc, h = _rglru_fwd(proj0, cw0, cb0, wa, ba, wxg, bx, lam)
    qn, kn, qc, kc, vc = _mla_fwd(proj0, qn_w, kn_w, wq, wkv, tc, tsa, tsb)
    o, lse, (w1_8,) = _flash_fwd(qc, kc, vc, bcast=l1_blocks[:1])
    w1z, w1x, w1d = _unshard(w1_8, MAP_W1, (2048, 3072, 128), "unshard_w1")
    y0, v0, x1, x1b = _l0_out(h, o, proj0, x, wo0, g0, b0)

    z, dt_raw = _l1_in(x1b, w1z, w1d)
    xbc, pre, act = _ssd_conv_fwd(x1b, w1x, cw1, cb1)
    ys, hprev, (wo1_8,) = _ssd_scan_fwd(act, dt_raw, dt_bias, a_log, d_x, tril, expand_t, bcast=l1_blocks[1:])
    wo1 = wo1_8.reshape(SSD_INNER, D_MODEL)
    dv1, dgb1, loss8, g_wo1 = _l1_out(ys, z, nw, wo1, x1, g1, b1, target)

    dys, dz, dnw, g_z = _l1_gate_bwd(dv1, wo1, ys, z, nw, x1b)
    dact, ddt_raw, dvec1, g_dt, (recv_wo1,) = _ssd_scan_bwd(
        dys, act, dt_raw, hprev, dt_bias, a_log, d_x, tril, expand_t, x1b,
        scatter=[g_wo1.reshape(N_DEV, 256, D_MODEL)])
    dxbc, dcw1, g_xbc = _ssd_conv_bwd(dact, pre, xbc, cw1, x1b)

    dv0, dgb0 = _l1_dx_ln(dz, dxbc, ddt_raw, dv1, v0, w1z, w1x, w1d, g0)
    dh, do, dgate, g_wo0, g_gate = _gate_bwd(dv0, wo0, h, o, proj0, y0, xb)
    dxr, g_wa, g_wx, dvec0, g_rnn = _rglru_bwd(dh, xc, h, proj0, cw0, wa, ba, wxg, bx, lam, xb)
    early = [_reshard([g_z, g_xbc, g_dt], MAP_W1, 644, bf, "reshard_w1"), g_wo0.astype(bf).reshape(N_DEV, 128, D_MODEL),
             (_reshard([g_rnn, g_gate], MAP_G0, 244, bf, "reshard_w0_early", shards=W0_EARLY), W0_EARLY)]
    dq, dk, dvv, (recv_w1, recv_wo0, recv_w0) = _flash_bwd(qc, kc, vc, o, do, lse, scatter=early)
    recv_early = {"ssd_w_in": recv_w1, "ssd_w_out": recv_wo1, "ab_w_out": recv_wo0, "ab_w_in": recv_w0}
    dtail, g_wq, g_wkv, dqnw, dknw, g_tail = _mla_bwd(dq, dk, dvv, proj0, qn, kn, qn_w, kn_w, wq, wkv, tc, tsa, tsb, xb)

    acc = {"g_rnn": g_rnn, "g_gate": g_gate, "g_tail": g_tail, "g_wq": g_wq, "g_wkv": g_wkv,
           "dvec0": dvec0, "g_wa": g_wa, "g_wx": g_wx, "dqnw": dqnw, "dknw": dknw, "dgb0": dgb0, "dvec1": dvec1,
           "dcw1": dcw1, "dnw": dnw, "dgb1": dgb1}
    late = [(_reshard([g_rnn, g_gate, g_tail], MAP_G0, 244, bf, "reshard_w0_late", shards=W0_LATE), W0_LATE),
            _reshard([g_wq], MAP_WQ, 96, bf, "reshard_wq"), _reshard([g_wkv], MAP_WKV, 128, bf, "reshard_wkv")]
    sm_slots, vec_rows, gates = _pack_small(dvec0, g_wa, g_wx, dqnw, dknw, dgb0, dvec1, dcw1, dnw, dgb1, loss8)
    dx, recv_late = _l0_dx(dxr, dgate, dtail, w0p, dv0, scatter=late + [sm_slots], bcast=[vec_rows, gates])
    return acc, recv_early, recv_late, loss8[0, 0], dx
```

```python
import math

import jax
import jax.numpy as jnp
from jax import lax
from jax.experimental import pallas as pl
from jax.experimental.pallas import tpu as pltpu

F32 = jnp.float32
MXU_DTYPE = jnp.bfloat16

N_DEV = 8
SEQ = 4096
D_MODEL = 1024
DN_ALPHA = 4.0 ** 0.25
RNN_W = 512
MLA_HEADS = 8
ATT_SCALE = 96.0 ** -0.5
ATT_C = ATT_SCALE * math.log2(math.e)
RG_C = 8.0
SSD_INNER = 2048
SSD_HEADS = 32
SSD_P = 64
SSD_GROUPS = 4
SSD_N = 128
SSD_L = 128
SSD_CONV = 3072
LANES = 128
SUBLANES = 8
VMEM_LIMIT = 56 * 1024 * 1024

ADAM_LR, ADAM_B1, ADAM_B2, ADAM_EPS, ADAM_WD, ADAM_STEP = 0.001, 0.9, 0.999, 1e-08, 0.01, 10

HIGHEST = lax.Precision.HIGHEST


def _params(sem, limit=VMEM_LIMIT):
    return pltpu.CompilerParams(dimension_semantics=sem, vmem_limit_bytes=limit)


def _dot(a, b):
    return lax.dot_general(a, b, (((1,), (0,)), ((), ())), preferred_element_type=F32)


def _dot_nt(a, b):
    return lax.dot_general(a, b, (((1,), (1,)), ((), ())), preferred_element_type=F32)


def _dot_tn(a, b):
    return lax.dot_general(a, b, (((0,), (0,)), ((), ())), preferred_element_type=F32)


def _dot_hi(a, b):
    return lax.dot_general(a, b, (((1,), (0,)), ((), ())), precision=HIGHEST, preferred_element_type=F32)


def _mx(v):
    return v.astype(MXU_DTYPE)


def _sigmoid(v):
    return 1.0 / (1.0 + jnp.exp(-v))


def _log1p_pos(e):
    poly = e * (1.0 - e * (0.5 - e * (1.0 / 3.0 - e * 0.25)))
    return jnp.where(e < 0.01, poly, jnp.log(1.0 + e))


def _softplus(v):
    return jnp.maximum(v, 0.0) + _log1p_pos(jnp.exp(-jnp.abs(v)))


def _neg_expm1(v):
    poly = -v * (1.0 + v * (0.5 + v * (1.0 / 6.0 + v * (1.0 / 24.0 + v * (1.0 / 120.0)))))
    return jnp.where(jnp.abs(v) < 0.1, poly, 1.0 - jnp.exp(v))


def _silu(v):
    return v * _sigmoid(v)


def _dsilu(v):
    s = _sigmoid(v)
    return s * (1.0 + v * (1.0 - s))


def _shift_down(blk, halo, s):
    if s == 0:
        return blk
    t = blk.shape[0]
    r = pltpu.roll(blk, s, 0)
    hr = pltpu.roll(halo, s, 0)
    row8 = lax.broadcasted_iota(jnp.int32, hr.shape, 0)
    head = jnp.where(row8 < s, hr, r[:SUBLANES])
    return jnp.concatenate([head, r[SUBLANES:]], axis=0) if t > SUBLANES else head


def _shift_up(blk, halo, s):
    if s == 0:
        return blk
    t = blk.shape[0]
    r = pltpu.roll(blk, t - s, 0)
    hr = pltpu.roll(halo, SUBLANES - s, 0)
    row8 = lax.broadcasted_iota(jnp.int32, hr.shape, 0)
    tail = jnp.where(row8 >= SUBLANES - s, hr, r[t - SUBLANES:])
    return jnp.concatenate([r[:t - SUBLANES], tail], axis=0) if t > SUBLANES else tail


def _scan_down(a, u):
    t = a.shape[0]
    row = lax.broadcasted_iota(jnp.int32, a.shape, 0)
    d = 1
    while d < t:
        keep = row >= d
        a_sh = jnp.where(keep, pltpu.roll(a, d, 0), 1.0)
        u_sh = jnp.where(keep, pltpu.roll(u, d, 0), 0.0)
        u = a * u_sh + u
        a = a * a_sh
        d *= 2
    return a, u


def _scan_up(a, u):
    t = a.shape[0]
    row = lax.broadcasted_iota(jnp.int32, a.shape, 0)
    d = 1
    while d < t:
        keep = row < t - d
        a_sh = jnp.where(keep, pltpu.roll(a, t - d, 0), 1.0)
        u_sh = jnp.where(keep, pltpu.roll(u, t - d, 0), 0.0)
        u = a * u_sh + u
        a = a * a_sh
        d *= 2
    return a, u


def _conv4(blk, halo, cw, cb):
    out = cb + blk * cw[3:4]
    for k in range(3):
        out = out + _shift_down(blk, halo, 3 - k) * cw[k:k + 1]
    return out


RG_T = 512
P0_RNN = 2


def _rg_gates(xc, wa, ba, wx, bx, lam):
    xcb = _mx(xc)
    r = _sigmoid(_dot(xcb, wa) + ba)
    ig = _sigmoid(_dot(xcb, wx) + bx)
    sp = _softplus(-lam)
    la = (-RG_C * r) * sp
    a = jnp.exp(la)
    mult = jnp.sqrt(_neg_expm1(2.0 * la))
    return r, ig, sp, a, mult


def _rglru_fwd(proj0, cw8, cb, wa, ba, wx, bx, lam):
    t, w = RG_T, RNN_W
    nb = SEQ // t

    def body(x_ref, halo_ref, cw_ref, cb_ref, wa_ref, ba_ref, wx_ref, bx_ref, lam_ref, xc_ref, h_ref, carry):
        i = pl.program_id(0)

        @pl.when(i == 0)
        def _():
            carry[...] = jnp.zeros_like(carry)

        blk = x_ref[...]
        halo = jnp.where(i > 0, halo_ref[...], 0.0)
        xc = _conv4(blk, halo, cw_ref[...], cb_ref[...])
        _, ig, _, a, mult = _rg_gates(xc, wa_ref[...], ba_ref[...], wx_ref[...], bx_ref[...], lam_ref[...])
        u = mult * (ig * xc)
        big_a, big_u = _scan_down(a, u)
        h = big_a * carry[SUBLANES - 1:SUBLANES, :] + big_u
        carry[...] = h[t - SUBLANES:]
        xc_ref[...] = xc
        h_ref[...] = h

    vec = pl.BlockSpec((1, w), lambda i: (0, 0))
    mat = pl.BlockSpec((w, w), lambda i: (0, 0))
    return pl.pallas_call(
        body, name="rglru_fwd", grid=(nb,),
        in_specs=[pl.BlockSpec((t, w), lambda i: (i, P0_RNN)),
                  pl.BlockSpec((SUBLANES, w), lambda i: (jnp.maximum(i * (t // SUBLANES) - 1, 0), P0_RNN)),
                  pl.BlockSpec((SUBLANES, w), lambda i: (0, 0)), vec, mat, vec, mat, vec, vec],
        out_specs=[pl.BlockSpec((t, w), lambda i: (i, 0)), pl.BlockSpec((t, w), lambda i: (i, 0))],
        out_shape=[jax.ShapeDtypeStruct((SEQ, w), F32), jax.ShapeDtypeStruct((SEQ, w), F32)],
        scratch_shapes=[pltpu.VMEM((SUBLANES, w), F32)],
        compiler_params=_params(("arbitrary",)),
    )(proj0, proj0, cw8, cb, wa, ba, wx, bx, lam)


def _rglru_bwd(dh, xc, h, proj0, cw8, wa, ba, wx, bx, lam, xb):
    t, w = RG_T, RNN_W
    nb = SEQ // t
    tb = t // SUBLANES

    def body(dh_ref, xc_ref, h_ref, hh_ref, x_ref, cw_ref, wa_ref, ba_ref, wx_ref, bx_ref, lam_ref, xb_ref,
             dx_ref, dwa_ref, dwx_ref, dvec_ref, gw_ref, gcarry, dxc_next):
        i = pl.program_id(0)
        rev = nb - 1 - i

        @pl.when(i == 0)
        def _():
            gcarry[...] = jnp.zeros_like(gcarry)
            dxc_next[...] = jnp.zeros_like(dxc_next)
            gw_ref[...] = jnp.zeros_like(gw_ref)
            dwa_ref[...] = jnp.zeros_like(dwa_ref)
            dwx_ref[...] = jnp.zeros_like(dwx_ref)
            dvec_ref[...] = jnp.zeros_like(dvec_ref)

        xc = xc_ref[...]
        wa_v, wx_v = wa_ref[...], wx_ref[...]
        lam_v = lam_ref[...]
        r, ig, sp, a, mult = _rg_gates(xc, wa_v, ba_ref[...], wx_v, bx_ref[...], lam_v)
        dhv = dh_ref[...]
        big_a, big_u = _scan_up(a, a * dhv)
        gg = big_a * gcarry[0:1, :] + big_u
        g = dhv + _shift_up(gg, gcarry[...], 1)
        gcarry[...] = gg[:SUBLANES]
        hhalo = jnp.where(rev > 0, hh_ref[...], 0.0)
        da = g * _shift_down(h_ref[...], hhalo, 1)
        d_mult = g * (ig * xc)
        d_i = g * (mult * xc)
        dxc = g * (mult * ig)
        d_la = da * a - d_mult * (a * a) / mult
        d_r = d_la * (-RG_C * sp)
        d_sp = jnp.sum(d_la * (-RG_C * r), axis=0, keepdims=True)
        d_pa = d_r * r * (1.0 - r)
        d_px = d_i * ig * (1.0 - ig)
        d_pab, d_pxb = _mx(d_pa), _mx(d_px)
        dxc = dxc + _dot_nt(d_pab, wa_v) + _dot_nt(d_pxb, wx_v)
        xcb = _mx(xc)
        dwa_ref[...] += _dot_tn(xcb, d_pab)
        dwx_ref[...] += _dot_tn(xcb, d_pxb)
        dvec_ref[0:1, :] += jnp.sum(d_pa, axis=0, keepdims=True)
        dvec_ref[1:2, :] += jnp.sum(d_px, axis=0, keepdims=True)
        dvec_ref[2:3, :] += d_sp * (-_sigmoid(-lam_v))
        dvec_ref[3:4, :] += jnp.sum(dxc, axis=0, keepdims=True)
        xblk = x_ref[...]
        cw = cw_ref[...]
        dx = dxc * cw[3:4]
        nxt = dxc_next[...]
        dvec_ref[7:8, :] += jnp.sum(dxc * xblk, axis=0, keepdims=True)
        for k in range(3):
            up = _shift_up(dxc, nxt, 3 - k)
            dvec_ref[4 + k:5 + k, :] += jnp.sum(up * xblk, axis=0, keepdims=True)
            dx = dx + up * cw[k:k + 1]
        dxc_next[...] = dxc[:SUBLANES]
        dxb = _mx(dx)
        dx_ref[...] = dxb
        gw_ref[...] += _dot_tn(xb_ref[...], dxb)

    blk = pl.BlockSpec((t, w), lambda i: (nb - 1 - i, 0))
    halo = pl.BlockSpec((SUBLANES, w), lambda i: (jnp.maximum((nb - 1 - i) * tb - 1, 0), 0))
    vec = pl.BlockSpec((1, w), lambda i: (0, 0))
    mat = pl.BlockSpec((w, w), lambda i: (0, 0))
    return pl.pallas_call(
        body, name="rglru_bwd", grid=(nb,),
        in_specs=[blk, blk, blk, halo, pl.BlockSpec((t, w), lambda i: (nb - 1 - i, P0_RNN)),
                  pl.BlockSpec((SUBLANES, w), lambda i: (0, 0)), mat, vec, mat, vec, vec,
                  pl.BlockSpec((t, D_MODEL), lambda i: (nb - 1 - i, 0))],
        out_specs=[blk, mat, mat, pl.BlockSpec((16, w), lambda i: (0, 0)), pl.BlockSpec((D_MODEL, w), lambda i: (0, 0))],
        out_shape=[jax.ShapeDtypeStruct((SEQ, w), MXU_DTYPE), jax.ShapeDtypeStruct((w, w), F32),
                   jax.ShapeDtypeStruct((w, w), F32), jax.ShapeDtypeStruct((16, w), F32),
                   jax.ShapeDtypeStruct((D_MODEL, w), F32)],
        scratch_shapes=[pltpu.VMEM((SUBLANES, w), F32), pltpu.VMEM((SUBLANES, w), F32)],
        compiler_params=_params(("arbitrary",)),
    )(dh, xc, h, h, proj0, cw8, wa, ba, wx, bx, lam, xb)


MLA_T = 512


def _rope(v, c, sa, sb):
    return v * c + pltpu.roll(v, LANES - 16, 1) * sa + pltpu.roll(v, 16, 1) * sb


def _rope_t(dv, c, sa, sb):
    return dv * c + pltpu.roll(dv * sa, 16, 1) + pltpu.roll(dv * sb, LANES - 16, 1)


def _rms(v, g, eps=1e-6):
    rs = lax.rsqrt(jnp.mean(v * v, axis=-1, keepdims=True) + eps)
    return v * rs * g, rs


def _mla_fwd(proj0, q_norm, kv_norm, wq, wkv, tc, tsa, tsb):
    t = MLA_T

    def body(cq_ref, ck_ref, qn_ref, kn_ref, wq_ref, wkv_ref, c_ref, sa_ref, sb_ref,
             oqn_ref, okn_ref, oq_ref, ok_ref, ov_ref):
        c, sa, sb = c_ref[...].T, sa_ref[...].T, sb_ref[...].T
        ck = ck_ref[...]
        qn = _mx(_rms(cq_ref[...], qn_ref[...])[0])
        kn = _mx(_rms(ck[:, :LANES], kn_ref[...])[0])
        oqn_ref[...] = qn
        okn_ref[...] = kn
        krv = _rope(ck[:, LANES:], c, sa, sb)
        qraw = _dot(qn, wq_ref[...])
        kvraw = _dot(kn, wkv_ref[...])
        for hd in range(MLA_HEADS):
            sl = slice(hd * LANES, (hd + 1) * LANES)
            oq_ref[:, sl] = _mx(_rope(qraw[:, sl], c, sa, sb))
            ok_ref[:, sl] = _mx(kvraw[:, sl] + krv)
        ov_ref[...] = _mx(kvraw[:, 1024:])

    tab = pl.BlockSpec((t, LANES), lambda i: (i, 0))
    rot = pl.BlockSpec((LANES, t), lambda i: (0, i))
    wide = pl.BlockSpec((t, 1024), lambda i: (i, 0))
    const = lambda shape: pl.BlockSpec(shape, lambda i: (0, 0))
    return pl.pallas_call(
        body, name="mla_fwd", grid=(SEQ // t,),
        in_specs=[pl.BlockSpec((t, 256), lambda i: (i, 6)), pl.BlockSpec((t, 256), lambda i: (i, 7)),
                  const((1, 256)), const((1, LANES)), const((256, 1024)), const((LANES, 1536)), rot, rot, rot],
        out_specs=[pl.BlockSpec((t, 256), lambda i: (i, 0)), tab, wide, wide, pl.BlockSpec((t, 512), lambda i: (i, 0))],
        out_shape=[jax.ShapeDtypeStruct((SEQ, 256), MXU_DTYPE), jax.ShapeDtypeStruct((SEQ, LANES), MXU_DTYPE),
                   jax.ShapeDtypeStruct((SEQ, 1024), MXU_DTYPE), jax.ShapeDtypeStruct((SEQ, 1024), MXU_DTYPE),
                   jax.ShapeDtypeStruct((SEQ, 512), MXU_DTYPE)],
        compiler_params=_params(("parallel",)),
    )(proj0, proj0, q_norm, kv_norm, wq, wkv, tc, tsa, tsb)


ATT_T = 1024


def _flash_fwd(q, k, v, bcast=()):
    t = ATT_T
    nb = SEQ // t

    steps = [(qi, ki) for qi in range(nb) for ki in range(qi + 1)]
    qi_tab = jnp.asarray([s[0] for s in steps], jnp.int32)
    ki_tab = jnp.asarray([s[1] for s in steps], jnp.int32)

    nx = len(bcast)

    def body(qi_ref, ki_ref, q_ref, k_ref, v_ref, *rest):
        x_refs, (o_ref, lse_ref), g_refs = rest[:nx], rest[nx:nx + 2], rest[nx + 2:2 * nx + 2]
        m_sc, acc_sc = rest[2 * nx + 2:2 * nx + 4]
        step = pl.program_id(1)
        qi, ki = qi_ref[step], ki_ref[step]
        if nx:
            copies = _peer_copies(x_refs, g_refs, rest[2 * nx + 4:], [])

            @pl.when((pl.program_id(0) == 0) & (step == 0))
            def _():
                for cp in copies:
                    cp.start()

        @pl.when(ki == 0)
        def _():
            m_sc[...] = jnp.full_like(m_sc, -jnp.inf)
            acc_sc[...] = jnp.zeros_like(acc_sc)

        def update(q0, nq, nk, masked):
            vv = v_ref[0:nk, :]
            lane_v = lax.broadcasted_iota(jnp.int32, vv.shape, 1)
            qs = slice(q0, q0 + nq)
            for hd in range(2):
                sl = slice(hd * LANES, (hd + 1) * LANES)
                st = _dot_nt(k_ref[0:nk, sl], q_ref[qs, sl])
                if masked:
                    st = jnp.where(lax.broadcasted_iota(jnp.int32, (nk, nq), 0)
                                   <= lax.broadcasted_iota(jnp.int32, (nk, nq), 1) + q0, st, -jnp.inf)
                m_prev = m_sc[hd:hd + 1, qs]
                m_new = jnp.maximum(m_prev, jnp.max(st, axis=0, keepdims=True))
                pt = jnp.exp2((st - m_new) * ATT_C)
                m_sc[hd:hd + 1, qs] = m_new
                vh = jnp.where((lane_v >= hd * 64) & (lane_v < (hd + 1) * 64), vv, jnp.ones_like(vv))
                acc_sc[hd, :, qs] = acc_sc[hd, :, qs] * jnp.exp2((m_prev - m_new) * ATT_C) + _dot_tn(vh, _mx(pt))

        @pl.when(ki < qi)
        def _():
            update(0, t, t, False)

        @pl.when(ki == qi)
        def _():
            update(0, t // 2, t // 2, True)
            update(t // 2, t // 2, t, True)
            a0, a1 = acc_sc[0], acc_sc[1]
            l0, l1 = a0[64:65, :], a1[0:1, :]
            first = lax.broadcasted_iota(jnp.int32, (LANES, t), 0) < 64
            o_ref[...] = jnp.where(first, a0 / l0, a1 / l1).T
            lse_ref[0, 0:1, :] = m_sc[0:1, :] * ATT_SCALE + jnp.log(l0)
            lse_ref[0, 1:2, :] = m_sc[1:2, :] * ATT_SCALE + jnp.log(l1)
            lse_ref[0, 2:SUBLANES, :] = jnp.zeros((SUBLANES - 2, t), F32)

        if nx:
            @pl.when((pl.program_id(0) == 3) & (step == len(steps) - 1))
            def _():
                for cp in copies:
                    cp.wait()

    grid_spec = pltpu.PrefetchScalarGridSpec(
        num_scalar_prefetch=2, grid=(4, len(steps)),
        in_specs=[pl.BlockSpec((t, 256), lambda p, s, qt, kt: (qt[s], p)),
                  pl.BlockSpec((t, 256), lambda p, s, qt, kt: (kt[s], p)),
                  pl.BlockSpec((t, LANES), lambda p, s, qt, kt: (kt[s], p))] + [ANY] * nx,
        out_specs=[pl.BlockSpec((t, LANES), lambda p, s, qt, kt: (qt[s], p)),
                   pl.BlockSpec((1, SUBLANES, t), lambda p, s, qt, kt: (p, 0, qt[s]))] + [ANY] * nx,
        scratch_shapes=[pltpu.VMEM((SUBLANES, t), F32), pltpu.VMEM((2, LANES, t), F32)]
        + (_exchange_sems(nx) if nx else []))
    res = pl.pallas_call(
        body, name="flash_fwd", grid_spec=grid_spec,
        out_shape=[jax.ShapeDtypeStruct((SEQ, 512), F32), jax.ShapeDtypeStruct((4, SUBLANES, SEQ), F32)]
        + _exchange_shapes([], bcast),
        compiler_params=_params(("arbitrary", "arbitrary")),
    )(qi_tab, ki_tab, q, k, v, *bcast)
    return res[0], res[1], res[2:]


def _flash_bwd(q, k, v, o, do, lse, scatter=()):
    t = ATT_T
    nb = SEQ // t

    steps = [(qi, ki) for ki in range(nb) for qi in range(ki, nb)]
    qi_tab = jnp.asarray([s[0] for s in steps], jnp.int32)
    ki_tab = jnp.asarray([s[1] for s in steps], jnp.int32)
    log2e = math.log2(math.e)

    sc_arrays, sc_ranges = _scatter_args(scatter)
    nx = len(sc_arrays)

    def body(qi_ref, ki_ref, q_ref, k_ref, v_ref, o_ref, do_ref, lse_ref, *rest):
        x_refs, (dq_ref, dk_ref, dv_ref), g_refs = rest[:nx], rest[nx:nx + 3], rest[nx + 3:2 * nx + 3]
        dkt_sc, dvt_sc = rest[2 * nx + 3:2 * nx + 5]
        step = pl.program_id(1)
        qi, ki = qi_ref[step], ki_ref[step]
        if nx:
            copies = _peer_copies(x_refs, g_refs, rest[2 * nx + 5:], sc_ranges)

            @pl.when((pl.program_id(0) == 0) & (step == 0))
            def _():
                for cp in copies:
                    cp.start()

        @pl.when(step == 0)
        def _():
            dq_ref[...] = jnp.zeros_like(dq_ref)

        @pl.when(qi == ki)
        def _():
            dkt_sc[...] = jnp.zeros_like(dkt_sc)
            dvt_sc[...] = jnp.zeros_like(dvt_sc)

        def update(q0, nq, nk, masked):
            qs = slice(q0, q0 + nq)
            dov, ov, vv = do_ref[qs, :], o_ref[qs, :], v_ref[0:nk, :]
            lse2 = (lse_ref[0, :, qs] * log2e).T
            lane = lax.broadcasted_iota(jnp.int32, (nq, LANES), 1)
            row_t = lax.broadcasted_iota(jnp.int32, (LANES, nk), 0)
            prod = dov * ov
            do_b = _mx(dov)
            qrows = pl.ds(pl.multiple_of(qi * t + q0, nq), nq)
            dvt_acc = jnp.zeros((LANES, nk), F32)
            dkt_new, dq_new = [], []
            for hd in range(2):
                sl = slice(hd * LANES, (hd + 1) * LANES)
                mine = (lane >= hd * 64) & (lane < (hd + 1) * 64)
                qh, kh = q_ref[qs, sl], k_ref[0:nk, sl]
                p = jnp.exp2(_dot_nt(qh, kh) * ATT_C - lse2[:, hd:hd + 1])
                if masked:
                    p = jnp.where(lax.broadcasted_iota(jnp.int32, (nq, nk), 1)
                                  <= lax.broadcasted_iota(jnp.int32, (nq, nk), 0) + q0, p, 0.0)
                do_h = jnp.where(mine, dov, 0.0)
                delta = jnp.sum(jnp.where(mine, prod, 0.0), axis=1, keepdims=True)
                dp = _dot_nt(_mx(do_h), vv)
                ds = _mx(p * (dp - delta) * ATT_SCALE)
                dvt_acc = dvt_acc + jnp.where((row_t >= hd * 64) & (row_t < (hd + 1) * 64), _dot_tn(do_b, _mx(p)), 0.0)
                dkt_new.append(_dot_tn(qh, ds))
                dq_new.append(_dot(ds, kh))
            for hd in range(2):
                sl = slice(hd * LANES, (hd + 1) * LANES)
                dkt_sc[sl, 0:nk] += dkt_new[hd]
                dq_ref[qrows, sl] += dq_new[hd]
            dvt_sc[:, 0:nk] += dvt_acc

        @pl.when(qi > ki)
        def _():
            update(0, t, t, False)

        @pl.when(qi == ki)
        def _():
            update(0, t // 2, t // 2, True)
            update(t // 2, t // 2, t, True)

        @pl.when(qi == nb - 1)
        def _():
            dk_ref[...] = dkt_sc[...].T
            dv_ref[...] = dvt_sc[...].T

        if nx:
            @pl.when((pl.program_id(0) == 3) & (step == len(steps) - 1))
            def _():
                for cp in copies:
                    cp.wait()

    qmap = lambda p, s, qt, kt: (qt[s], p)
    kmap = lambda p, s, qt, kt: (kt[s], p)
    grid_spec = pltpu.PrefetchScalarGridSpec(
        num_scalar_prefetch=2, grid=(4, len(steps)),
        in_specs=[pl.BlockSpec((t, 256), qmap), pl.BlockSpec((t, 256), kmap), pl.BlockSpec((t, LANES), kmap),
                  pl.BlockSpec((t, LANES), qmap), pl.BlockSpec((t, LANES), qmap),
                  pl.BlockSpec((1, SUBLANES, t), lambda p, s, qt, kt: (p, 0, qt[s]))] + [ANY] * nx,
        out_specs=[pl.BlockSpec((SEQ, 256), lambda p, s, qt, kt: (0, p)), pl.BlockSpec((t, 256), kmap),
                   pl.BlockSpec((t, LANES), kmap)] + [ANY] * nx,
        scratch_shapes=[pltpu.VMEM((256, t), F32), pltpu.VMEM((LANES, t), F32)] + (_exchange_sems(nx) if nx else []))
    res = pl.pallas_call(
        body, name="flash_bwd", grid_spec=grid_spec,
        out_shape=[jax.ShapeDtypeStruct((SEQ, 1024), F32), jax.ShapeDtypeStruct((SEQ, 1024), F32),
                   jax.ShapeDtypeStruct((SEQ, 512), F32)] + _exchange_shapes(sc_arrays, []),
        compiler_params=_params(("arbitrary", "arbitrary")),
    )(qi_tab, ki_tab, q, k, v, o, do, lse, *sc_arrays)
    return res[0], res[1], res[2], res[3:]


def _rms_bwd(v, g, dy, eps=1e-6):
    rs = lax.rsqrt(jnp.mean(v * v, axis=-1, keepdims=True) + eps)
    xh = v * rs
    dxh = dy * g
    dv = rs * (dxh - xh * jnp.mean(dxh * xh, axis=-1, keepdims=True))
    return dv, jnp.sum(dy * xh, axis=0, keepdims=True)


def _mla_bwd(dq, dk, dv, proj0, qlat, klat, q_norm, kv_norm, wq, wkv, tc, tsa, tsb, xb):
    t = MLA_T

    def body(dq_ref, dk_ref, dv_ref, cq_ref, ck_ref, ql_ref, kl_ref, qn_ref, kn_ref, wq_ref, wkv_ref,
             c_ref, sa_ref, sb_ref, xb_ref, o_ref, gwq_ref, gwkv_ref, dgq_ref, dgk_ref, gwt_ref, oq_ref, okv_ref):
        @pl.when(pl.program_id(0) == 0)
        def _():
            dgq_ref[...] = jnp.zeros_like(dgq_ref)
            dgk_ref[...] = jnp.zeros_like(dgk_ref)
            gwq_ref[...] = jnp.zeros_like(gwq_ref)
            gwkv_ref[...] = jnp.zeros_like(gwkv_ref)
            gwt_ref[...] = jnp.zeros_like(gwt_ref)

        c, sa, sb = c_ref[...].T, sa_ref[...].T, sb_ref[...].T
        lane = lax.broadcasted_iota(jnp.int32, (t, LANES), 1)
        dkr = jnp.zeros((t, LANES), F32)
        for hd in range(MLA_HEADS):
            sl = slice(hd * LANES, (hd + 1) * LANES)
            oq_ref[:, sl] = _mx(_rope_t(dq_ref[:, sl], c, sa, sb))
            dkh = dk_ref[:, sl]
            okv_ref[:, sl] = _mx(dkh)
            dkr = dkr + dkh
        okv_ref[:, 1024:] = _mx(dv_ref[...])
        dkr = _rope_t(jnp.where((lane >= 64) & (lane < 96), dkr, 0.0), c, sa, sb)
        dqraw, dkvraw = oq_ref[...], okv_ref[...]
        gwq_ref[...] += _dot_tn(ql_ref[...], dqraw)
        gwkv_ref[...] += _dot_tn(kl_ref[...], dkvraw)
        dqn = _dot_nt(dqraw, wq_ref[...])
        dkn = _dot_nt(dkvraw, wkv_ref[...])
        dcq, dgq = _rms_bwd(cq_ref[...], qn_ref[...], dqn)
        dck, dgk = _rms_bwd(ck_ref[:, :LANES], kn_ref[...], dkn)
        o_ref[:, :256] = _mx(dcq)
        o_ref[:, 256:384] = _mx(dck)
        o_ref[:, 384:] = _mx(dkr)
        gwt_ref[...] += _dot_tn(xb_ref[...], o_ref[...])
        dgq_ref[0:1, :] += dgq
        dgk_ref[0:1, :] += dgk

    tab = pl.BlockSpec((t, LANES), lambda i: (i, 0))
    rot = pl.BlockSpec((LANES, t), lambda i: (0, i))
    wide = pl.BlockSpec((t, 1024), lambda i: (i, 0))
    const = lambda shape: pl.BlockSpec(shape, lambda i: (0, 0))
    return pl.pallas_call(
        body, name="mla_bwd", grid=(SEQ // t,),
        in_specs=[wide, wide, pl.BlockSpec((t, 512), lambda i: (i, 0)),
                  pl.BlockSpec((t, 256), lambda i: (i, 6)), pl.BlockSpec((t, 256), lambda i: (i, 7)),
                  pl.BlockSpec((t, 256), lambda i: (i, 0)), tab,
                  const((1, 256)), const((1, LANES)), const((256, 1024)), const((LANES, 1536)), rot, rot, rot, wide],
        out_specs=[pl.BlockSpec((t, 512), lambda i: (i, 0)), const((256, 1024)), const((LANES, 1536)),
                   const((SUBLANES, 256)), const((SUBLANES, LANES)), const((D_MODEL, 512))],
        out_shape=[jax.ShapeDtypeStruct((SEQ, 512), MXU_DTYPE), jax.ShapeDtypeStruct((256, 1024), F32),
                   jax.ShapeDtypeStruct((LANES, 1536), F32), jax.ShapeDtypeStruct((SUBLANES, 256), F32),
                   jax.ShapeDtypeStruct((SUBLANES, LANES), F32), jax.ShapeDtypeStruct((D_MODEL, 512), F32)],
        scratch_shapes=[pltpu.VMEM((t, 1024), MXU_DTYPE), pltpu.VMEM((t, 1536), MXU_DTYPE)],
        compiler_params=_params(("arbitrary",)),
    )(dq, dk, dv, proj0, proj0, qlat, klat, q_norm, kv_norm, wq, wkv, tc, tsa, tsb, xb)


LN_T = 512


def _ln(v, g, b, eps=1e-5):
    mu = jnp.mean(v, axis=-1, keepdims=True)
    xc = v - mu
    rs = lax.rsqrt(jnp.mean(xc * xc, axis=-1, keepdims=True) + eps)
    return xc * rs * g + b


def _ln_bwd(v, g, dy, eps=1e-5):
    mu = jnp.mean(v, axis=-1, keepdims=True)
    xc = v - mu
    rs = lax.rsqrt(jnp.mean(xc * xc, axis=-1, keepdims=True) + eps)
    xh = xc * rs
    dxh = dy * g
    dv = rs * (dxh - jnp.mean(dxh, axis=-1, keepdims=True) - xh * jnp.mean(dxh * xh, axis=-1, keepdims=True))
    return dv, jnp.sum(dy * xh, axis=0, keepdims=True), jnp.sum(dy, axis=0, keepdims=True)


def _l0_out(h, o, proj0, x, w_out, g, b):
    t = LN_T

    def body(h_ref, o_ref, ga_ref, gb_ref, x_ref, w_ref, g_ref, b_ref, y_ref, v_ref, x1_ref, x1b_ref):
        y = _mx(jnp.concatenate([h_ref[...] * _silu(ga_ref[...]), o_ref[...] * _silu(gb_ref[...])], axis=1))
        v = DN_ALPHA * x_ref[...] + _dot(y, w_ref[...])
        y_ref[...] = y
        v_ref[...] = v
        x1 = _ln(v, g_ref[...], b_ref[...])
        x1_ref[...] = x1
        x1b_ref[...] = _mx(x1)

    half = pl.BlockSpec((t, 512), lambda i: (i, 0))
    full = pl.BlockSpec((t, D_MODEL), lambda i: (i, 0))
    vec = pl.BlockSpec((1, D_MODEL), lambda i: (0, 0))
    return pl.pallas_call(
        body, name="l0_out", grid=(SEQ // t,),
        in_specs=[half, half, pl.BlockSpec((t, 512), lambda i: (i, 0)), pl.BlockSpec((t, 512), lambda i: (i, 1)), full,
                  pl.BlockSpec((D_MODEL, D_MODEL), lambda i: (0, 0)), vec, vec],
        out_specs=[full, full, full, full],
        out_shape=[jax.ShapeDtypeStruct((SEQ, D_MODEL), MXU_DTYPE), jax.ShapeDtypeStruct((SEQ, D_MODEL), F32),
                   jax.ShapeDtypeStruct((SEQ, D_MODEL), F32), jax.ShapeDtypeStruct((SEQ, D_MODEL), MXU_DTYPE)],
        compiler_params=_params(("parallel",)),
    )(h, o, proj0, proj0, x, w_out, g, b)


def _l1_in(x1b, w1z, w1d):
    t = 1024

    def body(x_ref, wz_ref, wd_ref, z_ref, dt_ref):
        xv = x_ref[...]
        z_ref[...] = _dot(xv, wz_ref[...])
        dt_ref[...] = _dot(xv, wd_ref[...])

    rows = lambda w: pl.BlockSpec((t, w), lambda i: (i, 0))
    const = lambda w: pl.BlockSpec((D_MODEL, w), lambda i: (0, 0))
    return pl.pallas_call(
        body, name="l1_in", grid=(SEQ // t,),
        in_specs=[rows(D_MODEL), const(SSD_INNER), const(LANES)],
        out_specs=[rows(SSD_INNER), rows(LANES)],
        out_shape=[jax.ShapeDtypeStruct((SEQ, SSD_INNER), F32), jax.ShapeDtypeStruct((SEQ, LANES), F32)],
        compiler_params=_params(("parallel",)),
    )(x1b, w1z, w1d)


def _l1_dx_ln(dz, dxbc, ddt, dv1, v0, w1z, w1x, w1d, g):
    t = LN_T

    def body(dz_ref, dx_ref, ddt_ref, dv1_ref, v_ref, wz_ref, wx_ref, wd_ref, g_ref, dv_ref, dgb_ref):
        @pl.when(pl.program_id(0) == 0)
        def _():
            dgb_ref[...] = jnp.zeros_like(dgb_ref)

        dy = (DN_ALPHA * dv1_ref[...] + _dot_nt(dz_ref[...], wz_ref[...]) + _dot_nt(dx_ref[...], wx_ref[...])
              + _dot_nt(_mx(ddt_ref[...]), wd_ref[...]))
        dv, dg, db = _ln_bwd(v_ref[...], g_ref[...], dy)
        dv_ref[...] = dv
        dgb_ref[0:1, :] += dg
        dgb_ref[1:2, :] += db

    rows = lambda w: pl.BlockSpec((t, w), lambda i: (i, 0))
    const = lambda w: pl.BlockSpec((D_MODEL, w), lambda i: (0, 0))
    return pl.pallas_call(
        body, name="l1_dx_ln", grid=(SEQ // t,),
        in_specs=[rows(SSD_INNER), rows(SSD_CONV), rows(LANES), rows(D_MODEL), rows(D_MODEL),
                  const(SSD_INNER), const(SSD_CONV), const(LANES), pl.BlockSpec((1, D_MODEL), lambda i: (0, 0))],
        out_specs=[rows(D_MODEL), pl.BlockSpec((SUBLANES, D_MODEL), lambda i: (0, 0))],
        out_shape=[jax.ShapeDtypeStruct((SEQ, D_MODEL), F32), jax.ShapeDtypeStruct((SUBLANES, D_MODEL), F32)],
        compiler_params=_params(("arbitrary",)),
    )(dz, dxbc, ddt, dv1, v0, w1z, w1x, w1d, g)


def _gate_bwd(dv0, w_out, h, o, proj0, y0, xb):
    t = LN_T

    def body(dv_ref, w_ref, h_ref, o_ref, ga_ref, gb_ref, y0_ref, xb_ref, dh_ref, do_ref, dg_ref, gwo_ref, gwg_ref):
        @pl.when(pl.program_id(0) == 0)
        def _():
            gwo_ref[...] = jnp.zeros_like(gwo_ref)
            gwg_ref[...] = jnp.zeros_like(gwg_ref)

        dvb = _mx(dv_ref[...])
        dy = _dot_nt(dvb, w_ref[...])
        ga, gb, dya, dyb = ga_ref[...], gb_ref[...], dy[:, :512], dy[:, 512:]
        dh_ref[...] = dya * _silu(ga)
        do_ref[...] = dyb * _silu(gb)
        dg_ref[:, :512] = _mx(dya * h_ref[...] * _dsilu(ga))
        dg_ref[:, 512:] = _mx(dyb * o_ref[...] * _dsilu(gb))
        gwo_ref[...] += _dot_tn(y0_ref[...], dvb)
        gwg_ref[...] += _dot_tn(xb_ref[...], dg_ref[...])

    half = pl.BlockSpec((t, 512), lambda i: (i, 0))
    half1 = pl.BlockSpec((t, 512), lambda i: (i, 1))
    full = pl.BlockSpec((t, 1024), lambda i: (i, 0))
    square = pl.BlockSpec((D_MODEL, D_MODEL), lambda i: (0, 0))
    return pl.pallas_call(
        body, name="gate_bwd", grid=(SEQ // t,),
        in_specs=[full, square, half, half, half, half1, full, full],
        out_specs=[half, half, full, square, square],
        out_shape=[jax.ShapeDtypeStruct((SEQ, 512), F32), jax.ShapeDtypeStruct((SEQ, 512), F32),
                   jax.ShapeDtypeStruct((SEQ, 1024), MXU_DTYPE), jax.ShapeDtypeStruct((D_MODEL, D_MODEL), F32),
                   jax.ShapeDtypeStruct((D_MODEL, D_MODEL), F32)],
        compiler_params=_params(("arbitrary",)),
    )(dv0, w_out, h, o, proj0, proj0, y0, xb)


CONV_T = 1024
CONV_CB = 1024


def _ssd_conv_fwd(x1b, w1x, cw8, cb):
    t, cbk = CONV_T, CONV_CB

    def body(x_ref, w_ref, cw_ref, cb_ref, xbc_ref, pre_ref, act_ref, carry):
        xbc = _dot(x_ref[...], w_ref[...])
        halo = jnp.where(pl.program_id(1) > 0, carry[...], 0.0)
        pre = _conv4(xbc, halo, cw_ref[...], cb_ref[...])
        carry[...] = xbc[t - SUBLANES:]
        xbc_ref[...] = xbc
        pre_ref[...] = pre
        act_ref[...] = _silu(pre)

    blk = pl.BlockSpec((t, cbk), lambda j, i: (i, j))
    out = jax.ShapeDtypeStruct((SEQ, SSD_CONV), F32)
    return pl.pallas_call(
        body, name="ssd_conv_fwd", grid=(SSD_CONV // cbk, SEQ // t),
        in_specs=[pl.BlockSpec((t, D_MODEL), lambda j, i: (i, 0)), pl.BlockSpec((D_MODEL, cbk), lambda j, i: (0, j)),
                  pl.BlockSpec((SUBLANES, cbk), lambda j, i: (0, j)), pl.BlockSpec((1, cbk), lambda j, i: (0, j))],
        out_specs=[blk, blk, blk], out_shape=[out, out, out],
        scratch_shapes=[pltpu.VMEM((SUBLANES, cbk), F32)],
        compiler_params=_params(("parallel", "arbitrary")),
    )(x1b, w1x, cw8, cb)


def _ssd_conv_bwd(dact, pre, xbc, cw8, x1b):
    t, cbk = CONV_T, CONV_CB
    tb = t // SUBLANES
    nb = SEQ // t

    def body(da_ref, dan_ref, pre_ref, pren_ref, x_ref, cw_ref, x1_ref, dx_ref, dcw_ref, gw_ref):
        i = pl.program_id(1)

        @pl.when(i == 0)
        def _():
            dcw_ref[...] = jnp.zeros_like(dcw_ref)
            gw_ref[...] = jnp.zeros_like(gw_ref)

        dpre = da_ref[...] * _dsilu(pre_ref[...])
        dpre_next = jnp.where(i < nb - 1, dan_ref[...] * _dsilu(pren_ref[...]), 0.0)
        xblk = x_ref[...]
        cw = cw_ref[...]
        dx = dpre * cw[3:4]
        dcw_ref[3:4, :] += jnp.sum(dpre * xblk, axis=0, keepdims=True)
        for k in range(3):
            up = _shift_up(dpre, dpre_next, 3 - k)
            dcw_ref[k:k + 1, :] += jnp.sum(up * xblk, axis=0, keepdims=True)
            dx = dx + up * cw[k:k + 1]
        dcw_ref[4:5, :] += jnp.sum(dpre, axis=0, keepdims=True)
        dxb = _mx(dx)
        dx_ref[...] = dxb
        gw_ref[...] += _dot_tn(x1_ref[...], dxb)

    blk = pl.BlockSpec((t, cbk), lambda j, i: (i, j))
    nxt = pl.BlockSpec((SUBLANES, cbk), lambda j, i: (jnp.minimum((i + 1) * tb, SEQ // SUBLANES - 1), j))
    acc = pl.BlockSpec((SUBLANES, cbk), lambda j, i: (0, j))
    return pl.pallas_call(
        body, name="ssd_conv_bwd", grid=(SSD_CONV // cbk, nb),
        in_specs=[blk, nxt, blk, nxt, blk, acc, pl.BlockSpec((t, D_MODEL), lambda j, i: (i, 0))],
        out_specs=[blk, acc, pl.BlockSpec((D_MODEL, cbk), lambda j, i: (0, j))],
        out_shape=[jax.ShapeDtypeStruct((SEQ, SSD_CONV), MXU_DTYPE), jax.ShapeDtypeStruct((SUBLANES, SSD_CONV), F32),
                   jax.ShapeDtypeStruct((D_MODEL, SSD_CONV), F32)],
        compiler_params=_params(("parallel", "arbitrary")),
    )(dact, dact, pre, pre, xbc, cw8, x1b)


def _ssd_common(dt_raw, bias, alog, tril, expand_t, xs):
    lane = lax.broadcasted_iota(jnp.int32, dt_raw.shape, 1)
    dt = jnp.where(lane < SSD_HEADS, _softplus(dt_raw + bias), 0.0)
    a_neg = -jnp.exp(alog)
    cs = _dot_hi(tril, dt * a_neg)
    dt_x = _expand_heads(dt, expand_t)
    ecs_x = _expand_heads(jnp.exp(cs), expand_t)
    ds_x = _expand_heads(jnp.exp(cs[SSD_L - 1:SSD_L, :] - cs), expand_t)
    return dt, a_neg, cs, dt_x, None, xs * dt_x, ds_x, ecs_x, ecs_x[SSD_L - 1:SSD_L, :]


def _expand_heads(v, expand_t):
    hi = v.astype(jnp.bfloat16)
    lo = (v - hi.astype(F32)).astype(jnp.bfloat16)
    return _dot_nt(hi, expand_t) + _dot_nt(lo, expand_t)


def _fold_heads(v, expand_t):
    hi = v.astype(jnp.bfloat16)
    lo = (v - hi.astype(F32)).astype(jnp.bfloat16)
    return _dot(hi, expand_t) + _dot(lo, expand_t)


def _ssd_decay(cs, cs_t, hh, causal):
    seg = cs[:, hh:hh + 1] - cs_t[hh:hh + 1, :]
    return jnp.where(causal, jnp.exp(jnp.where(causal, seg, 0.0)), 0.0)


def _ssd_scan_fwd(act, dt_raw, bias, alog, d_x, tril, expand_t, bcast=()):
    nc = SEQ // SSD_L
    gw = SSD_INNER // SSD_GROUPS
    n = len(bcast)

    def body(act_ref, dt_ref, bias_ref, alog_ref, dx_ref, tril_ref, et_ref, *rest):
        y_ref, hp_ref, h_sc = rest[n], rest[n + 1], rest[2 * n + 2]
        if n:
            copies = _peer_copies(rest[:n], rest[n + 2:2 * n + 2], rest[2 * n + 3:], [])

            @pl.when(pl.program_id(0) == 0)
            def _():
                for cp in copies:
                    cp.start()

            @pl.when(pl.program_id(0) == nc - 1)
            def _():
                for cp in copies:
                    cp.wait()

        @pl.when(pl.program_id(0) == 0)
        def _():
            h_sc[...] = jnp.zeros_like(h_sc)

        xs = act_ref[:, :SSD_INNER]
        _, _, cs, _, _, xdt, ds_x, ecs_x, elast = _ssd_common(
            dt_ref[...], bias_ref[...], alog_ref[...], tril_ref[...], et_ref[...], xs)
        cs_t = cs.T
        causal = (lax.broadcasted_iota(jnp.int32, (SSD_L, SSD_L), 0)
                  >= lax.broadcasted_iota(jnp.int32, (SSD_L, SSD_L), 1))
        lane = lax.broadcasted_iota(jnp.int32, (SSD_L, LANES), 1)
        xdt_b = _mx(xdt)
        xds_b = _mx(xdt * ds_x)
        hp_ref[0] = h_sc[...]
        for g in range(SSD_GROUPS):
            gs = slice(g * gw, (g + 1) * gw)
            bg = _mx(act_ref[:, SSD_INNER + g * SSD_N:SSD_INNER + (g + 1) * SSD_N])
            cg = _mx(act_ref[:, SSD_INNER + 512 + g * SSD_N:SSD_INNER + 512 + (g + 1) * SSD_N])
            cb = _dot_nt(cg, bg)
            hprev = h_sc[:, gs]
            yoff = _dot(cg, _mx(hprev)) * ecs_x[:, gs]
            h_sc[:, gs] = hprev * elast[:, gs] + _dot_tn(bg, xds_b[:, gs])
            for pr in range(4):
                ps = slice(g * gw + pr * LANES, g * gw + (pr + 1) * LANES)
                xp = xdt_b[:, ps]
                ydiag = jnp.zeros((SSD_L, LANES), F32)
                for j in range(2):
                    dm = _ssd_decay(cs, cs_t, g * 8 + pr * 2 + j, causal)
                    mine = (lane >= j * 64) & (lane < (j + 1) * 64)
                    ydiag = ydiag + _dot(_mx(cb * dm), jnp.where(mine, xp, jnp.zeros_like(xp)))
                y_ref[:, ps] = ydiag + yoff[:, pr * LANES:(pr + 1) * LANES] + dx_ref[:, ps] * xs[:, ps]

    const = lambda shape: pl.BlockSpec(shape, lambda c: (0, 0))
    res = pl.pallas_call(
        body, name="ssd_scan_fwd", grid=(nc,),
        in_specs=[pl.BlockSpec((SSD_L, SSD_CONV), lambda c: (c, 0)), pl.BlockSpec((SSD_L, LANES), lambda c: (c, 0)),
                  const((1, LANES)), const((1, LANES)), const((1, SSD_INNER)), const((SSD_L, SSD_L)),
                  const((SSD_INNER, LANES))] + [ANY] * n,
        out_specs=[pl.BlockSpec((SSD_L, SSD_INNER), lambda c: (c, 0)),
                   pl.BlockSpec((1, SSD_N, SSD_INNER), lambda c: (c, 0, 0))] + [ANY] * n,
        out_shape=[jax.ShapeDtypeStruct((SEQ, SSD_INNER), F32), jax.ShapeDtypeStruct((nc, SSD_N, SSD_INNER), F32)]
        + _exchange_shapes([], bcast),
        scratch_shapes=[pltpu.VMEM((SSD_N, SSD_INNER), F32)] + (_exchange_sems(n) if n else []),
        compiler_params=_params(("arbitrary",)),
    )(act, dt_raw, bias, alog, d_x, tril, expand_t, *bcast)
    return res[0], res[1], res[2:]


def _ssd_scan_bwd(dy, act, dt_raw, hprev_all, bias, alog, d_x, tril, expand_t, x1b, scatter=()):
    nc = SEQ // SSD_L
    gw = SSD_INNER // SSD_GROUPS
    sc_arrays, sc_ranges = _scatter_args(scatter)
    nx = len(sc_arrays)

    def body(dy_ref, act_ref, dt_ref, hp_ref, bias_ref, alog_ref, dx_ref, tril_ref, et_ref, x1_ref, *rest):
        dact_ref, ddt_ref, dvec_ref, gdt_ref = rest[nx:nx + 4]
        dh_sc, dd_sc, dcs_sc, dcst_sc = rest[2 * nx + 4:2 * nx + 8]
        i = pl.program_id(0)
        if nx:
            copies = _peer_copies(rest[:nx], rest[nx + 4:2 * nx + 4], rest[2 * nx + 8:], sc_ranges)

            @pl.when(i == 0)
            def _():
                for cp in copies:
                    cp.start()

        @pl.when(i == 0)
        def _():
            dh_sc[...] = jnp.zeros_like(dh_sc)
            dd_sc[...] = jnp.zeros_like(dd_sc)
            gdt_ref[...] = jnp.zeros_like(gdt_ref)
            dvec_ref[...] = jnp.zeros_like(dvec_ref)

        xs = act_ref[:, :SSD_INNER]
        dt_raw_v, bias_v = dt_ref[...], bias_ref[...]
        dt, a_neg, cs, dt_x, _, xdt, ds_x, ecs_x, elast = _ssd_common(
            dt_raw_v, bias_v, alog_ref[...], tril_ref[...], et_ref[...], xs)
        cs_t = cs.T
        rowi = lax.broadcasted_iota(jnp.int32, (SSD_L, SSD_L), 0)
        coli = lax.broadcasted_iota(jnp.int32, (SSD_L, SSD_L), 1)
        causal = rowi >= coli
        lane = lax.broadcasted_iota(jnp.int32, (SSD_L, LANES), 1)
        row_g = lax.broadcasted_iota(jnp.int32, (SSD_L, gw), 0)
        dyv = dy_ref[...]
        dd_sc[0:1, :] += jnp.sum(dyv * xs, axis=0, keepdims=True)
        xdt_b = _mx(xdt)
        xds = xdt * ds_x
        xds_b = _mx(xds)
        dy_b = _mx(dyv)
        dye_b = _mx(dyv * ecs_x)
        dcs_sc[...] = jnp.zeros_like(dcs_sc)
        dcst_sc[...] = jnp.zeros_like(dcst_sc)
        dcs_parts = []
        dxdt_parts = []
        for g in range(SSD_GROUPS):
            gs = slice(g * gw, (g + 1) * gw)
            bcol = slice(SSD_INNER + g * SSD_N, SSD_INNER + (g + 1) * SSD_N)
            ccol = slice(SSD_INNER + 512 + g * SSD_N, SSD_INNER + 512 + (g + 1) * SSD_N)
            bg, cg = _mx(act_ref[:, bcol]), _mx(act_ref[:, ccol])
            cb = _dot_nt(cg, bg)
            hp = hp_ref[0, :, gs]
            hp_b = _mx(hp)
            dh = dh_sc[:, gs]
            dh_b = _mx(dh)
            yoff = _dot(cg, hp_b) * ecs_x[:, gs]
            bdh = _dot(bg, dh_b)
            tt = xds[:, gs] * bdh
            last_row = (jnp.sum(tt, axis=0, keepdims=True)
                        + jnp.sum(dh * hp, axis=0, keepdims=True) * elast[:, gs])
            dcs_parts.append(dyv[:, gs] * yoff - tt + jnp.where(row_g == SSD_L - 1, last_row, 0.0))
            dc_g = _dot_nt(dye_b[:, gs], hp_b)
            db_g = _dot_nt(xds_b[:, gs], dh_b)
            dh_sc[:, gs] = _dot_tn(cg, dye_b[:, gs]) + dh * elast[:, gs]
            wsum = jnp.zeros((SSD_L, SSD_L), F32)
            dxdt_g = []
            for pr in range(4):
                ps = slice(g * gw + pr * LANES, g * gw + (pr + 1) * LANES)
                xp, dyp = xdt_b[:, ps], dy_b[:, ps]
                dxp = jnp.zeros((SSD_L, LANES), F32)
                for j in range(2):
                    hh = g * 8 + pr * 2 + j
                    dm = _ssd_decay(cs, cs_t, hh, causal)
                    mine = (lane >= j * 64) & (lane < (j + 1) * 64)
                    dy_h = jnp.where(mine, dyp, jnp.zeros_like(dyp))
                    wd = _dot_nt(dy_h, xp) * dm
                    wsum = wsum + wd
                    gmat = wd * cb
                    dcs_sc[:, hh:hh + 1] = jnp.sum(gmat, axis=1, keepdims=True)
                    dcst_sc[hh:hh + 1, :] = -jnp.sum(gmat, axis=0, keepdims=True)
                    dxp = dxp + _dot_tn(_mx(cb * dm), dy_h)
                dxdt_g.append(dxp)
            dxdt_parts.append(jnp.concatenate(dxdt_g, axis=1) + bdh * ds_x[:, gs])
            ws_b = _mx(wsum)
            dact_ref[:, ccol] = dc_g + _dot(ws_b, bg)
            dact_ref[:, bcol] = db_g + _dot_tn(ws_b, cg)
        dxdt = jnp.concatenate(dxdt_parts, axis=1)
        dcs_x = jnp.concatenate(dcs_parts, axis=1)
        et = et_ref[...]
        dcs_tot = dcs_sc[...] + dcst_sc[...].T + _fold_heads(dcs_x, et)
        da_dt = _dot_hi((coli >= rowi).astype(F32), dcs_tot)
        ddt = da_dt * a_neg + _fold_heads(dxdt * xs, et)
        ddt_raw = ddt * _sigmoid(dt_raw_v + bias_v)
        ddt_ref[...] = ddt_raw
        gdt_ref[...] += _dot_tn(x1_ref[...], _mx(ddt_raw))
        dvec_ref[0:1, :] += jnp.sum(ddt_raw, axis=0, keepdims=True)
        dvec_ref[1:2, :] += jnp.sum(da_dt * dt, axis=0, keepdims=True) * a_neg
        dact_ref[:, :SSD_INNER] = dyv * dx_ref[...] + dxdt * dt_x

        @pl.when(i == nc - 1)
        def _():
            dvec_ref[2:3, :] = _fold_heads(dd_sc[...], et)[0:1, :]
            if nx:
                for cp in copies:
                    cp.wait()

    const = lambda shape: pl.BlockSpec(shape, lambda c: (0, 0))
    rev = lambda c: (nc - 1 - c, 0)
    res = pl.pallas_call(
        body, name="ssd_scan_bwd", grid=(nc,),
        in_specs=[pl.BlockSpec((SSD_L, SSD_INNER), rev), pl.BlockSpec((SSD_L, SSD_CONV), rev),
                  pl.BlockSpec((SSD_L, LANES), rev),
                  pl.BlockSpec((1, SSD_N, SSD_INNER), lambda c: (nc - 1 - c, 0, 0)),
                  const((1, LANES)), const((1, LANES)), const((1, SSD_INNER)), const((SSD_L, SSD_L)),
                  const((SSD_INNER, LANES)), pl.BlockSpec((SSD_L, D_MODEL), rev)] + [ANY] * nx,
        out_specs=[pl.BlockSpec((SSD_L, SSD_CONV), rev), pl.BlockSpec((SSD_L, LANES), rev), const((SUBLANES, LANES)),
                   const((D_MODEL, LANES))] + [ANY] * nx,
        out_shape=[jax.ShapeDtypeStruct((SEQ, SSD_CONV), F32), jax.ShapeDtypeStruct((SEQ, LANES), F32),
                   jax.ShapeDtypeStruct((SUBLANES, LANES), F32), jax.ShapeDtypeStruct((D_MODEL, LANES), F32)]
        + _exchange_shapes(sc_arrays, []),
        scratch_shapes=[pltpu.VMEM((SSD_N, SSD_INNER), F32), pltpu.VMEM((SUBLANES, SSD_INNER), F32),
                        pltpu.VMEM((SSD_L, LANES), F32), pltpu.VMEM((LANES, SSD_L), F32)]
        + (_exchange_sems(nx) if nx else []),
        compiler_params=_params(("arbitrary",)),
    )(dy, act, dt_raw, hprev_all, bias, alog, d_x, tril, expand_t, x1b, *sc_arrays)
    return res[0], res[1], res[2], res[3], res[4:]


L1_T = 512


def _resident(shape):
    return pl.BlockSpec(shape, lambda i: (0, 0), pipeline_mode=pl.Buffered(1))


def _gated_norm(y, z, nw):
    y2 = y * _silu(z)
    gw = SSD_INNER // SSD_GROUPS
    outs, xhs, rss = [], [], []
    for g in range(SSD_GROUPS):
        gs = slice(g * gw, (g + 1) * gw)
        v = y2[:, gs]
        rs = lax.rsqrt(jnp.mean(v * v, axis=-1, keepdims=True) + 1e-6)
        xhs.append(v * rs)
        rss.append(rs)
        outs.append(v * rs * nw[:, gs])
    return outs, xhs, rss


def _l1_out(y, z, nw, w_out, x1, g, b, target):
    t = L1_T

    def body(y_ref, z_ref, nw_ref, w_ref, x1_ref, g_ref, b_ref, tg_ref, dv_ref, dgb_ref, loss_ref, gw_ref, gw_sc):
        @pl.when(pl.program_id(0) == 0)
        def _():
            dgb_ref[...] = jnp.zeros_like(dgb_ref)
            loss_ref[...] = jnp.zeros_like(loss_ref)
            gw_sc[...] = jnp.zeros_like(gw_sc)

        outs, _, _ = _gated_norm(y_ref[...], z_ref[...], nw_ref[...])
        yn = _mx(jnp.concatenate(outs, axis=1))
        v = DN_ALPHA * x1_ref[...] + _dot(yn, w_ref[...])
        gv = g_ref[...]
        err = _ln(v, gv, b_ref[...]) - tg_ref[...]
        rowsum = jnp.sum(err * err, axis=1, keepdims=True)
        loss_ref[...] += 0.5 * jnp.sum(rowsum, axis=0, keepdims=True) / D_MODEL
        dv, dg, db = _ln_bwd(v, gv, err / D_MODEL)
        dv_ref[...] = dv
        dgb_ref[0:1, :] += dg
        dgb_ref[1:2, :] += db
        gw_sc[...] += _dot_tn(yn, _mx(dv))

        @pl.when(pl.program_id(0) == SEQ // t - 1)
        def _():
            gw_ref[...] = _mx(gw_sc[...])

    wide = pl.BlockSpec((t, SSD_INNER), lambda i: (i, 0))
    full = pl.BlockSpec((t, D_MODEL), lambda i: (i, 0))
    vec = pl.BlockSpec((1, D_MODEL), lambda i: (0, 0))
    return pl.pallas_call(
        body, name="l1_out", grid=(SEQ // t,),
        in_specs=[wide, wide, pl.BlockSpec((1, SSD_INNER), lambda i: (0, 0)),
                  _resident((SSD_INNER, D_MODEL)), full, vec, vec, full],
        out_specs=[full, pl.BlockSpec((SUBLANES, D_MODEL), lambda i: (0, 0)),
                   pl.BlockSpec((SUBLANES, LANES), lambda i: (0, 0)), _resident((SSD_INNER, D_MODEL))],
        out_shape=[jax.ShapeDtypeStruct((SEQ, D_MODEL), F32), jax.ShapeDtypeStruct((SUBLANES, D_MODEL), F32),
                   jax.ShapeDtypeStruct((SUBLANES, LANES), F32), jax.ShapeDtypeStruct((SSD_INNER, D_MODEL), MXU_DTYPE)],
        scratch_shapes=[pltpu.VMEM((SSD_INNER, D_MODEL), F32)],
        compiler_params=_params(("arbitrary",)),
    )(y, z, nw, w_out, x1, g, b, target)


def _l1_gate_bwd(dv1, w_out, y, z, nw, x1b):
    t = L1_T
    gw = SSD_INNER // SSD_GROUPS

    def body(dv_ref, w_ref, y_ref, z_ref, nw_ref, x1_ref, dy_ref, dz_ref, dnw_ref, gw_ref):
        @pl.when(pl.program_id(0) == 0)
        def _():
            dnw_ref[...] = jnp.zeros_like(dnw_ref)
            gw_ref[...] = jnp.zeros_like(gw_ref)

        dyn = _dot_nt(_mx(dv_ref[...]), w_ref[...])
        yv, zv, nwv = y_ref[...], z_ref[...], nw_ref[...]
        _, xhs, rss = _gated_norm(yv, zv, nwv)
        sz, dsz = _silu(zv), _dsilu(zv)
        for g in range(SSD_GROUPS):
            gs = slice(g * gw, (g + 1) * gw)
            d_out = dyn[:, gs]
            xh = xhs[g]
            dnw_ref[0:1, gs] += jnp.sum(d_out * xh, axis=0, keepdims=True)
            dxh = d_out * nwv[:, gs]
            dy2 = rss[g] * (dxh - xh * jnp.mean(dxh * xh, axis=-1, keepdims=True))
            dy_ref[:, gs] = dy2 * sz[:, gs]
            dz_ref[:, gs] = _mx(dy2 * yv[:, gs] * dsz[:, gs])
        gw_ref[...] += _dot_tn(x1_ref[...], dz_ref[...])

    wide = pl.BlockSpec((t, SSD_INNER), lambda i: (i, 0))
    return pl.pallas_call(
        body, name="l1_gate_bwd", grid=(SEQ // t,),
        in_specs=[pl.BlockSpec((t, D_MODEL), lambda i: (i, 0)), _resident((SSD_INNER, D_MODEL)),
                  wide, wide, pl.BlockSpec((1, SSD_INNER), lambda i: (0, 0)), pl.BlockSpec((t, D_MODEL), lambda i: (i, 0))],
        out_specs=[wide, wide, pl.BlockSpec((SUBLANES, SSD_INNER), lambda i: (0, 0)),
                   _resident((D_MODEL, SSD_INNER))],
        out_shape=[jax.ShapeDtypeStruct((SEQ, SSD_INNER), F32), jax.ShapeDtypeStruct((SEQ, SSD_INNER), MXU_DTYPE),
                   jax.ShapeDtypeStruct((SUBLANES, SSD_INNER), F32), jax.ShapeDtypeStruct((D_MODEL, SSD_INNER), F32)],
        compiler_params=_params(("arbitrary",)),
    )(dv1, w_out, y, z, nw, x1b)


MESH = pl.DeviceIdType.MESH
ANY = pl.BlockSpec(memory_space=pl.ANY)


def _flip(v, bit):
    return 1 - v if bit else v


def _all_gather(blocks, name):
    n = len(blocks)

    def body(*refs):
        x_refs, out_refs = refs[:n], refs[n:2 * n]
        send_sems, recv_sems, local_sems = refs[2 * n:]
        mx, my, mc = lax.axis_index("x"), lax.axis_index("y"), lax.axis_index("c")
        me, sibling = (mx, my, mc), (mx, my, 1 - mc)
        chips = [(1 - mx, my), (mx, 1 - my), (1 - mx, 1 - my)]

        def copy(a, k, block, to, own=False):
            px, py, pc = block
            slot = out_refs[a].at[4 * px + 2 * py + pc]
            return pltpu.make_async_remote_copy(
                src_ref=x_refs[a] if own else slot, dst_ref=slot,
                send_sem=send_sems.at[7 * a + k], recv_sem=recv_sems.at[7 * a + k], device_id=to, device_id_type=MESH)

        mine = [pltpu.make_async_copy(x_refs[a], out_refs[a].at[4 * mx + 2 * my + mc], local_sems.at[a])
                for a in range(n)]
        first = []
        for a in range(n):
            mine[a].start()
            first.append(copy(a, 0, me, sibling, own=True))
            first += [copy(a, 1 + j, me, (*chip, mc), own=True) for j, chip in enumerate(chips)]
        for cp in first:
            cp.start()
        passed = []
        for j, chip in enumerate(chips):
            for a in range(n):
                copy(a, 1 + j, (*chip, mc), me).wait_recv()
                fwd = copy(a, 4 + j, (*chip, mc), sibling)
                fwd.start()
                passed.append(fwd)
        for a in range(n):
            copy(a, 0, sibling, me).wait_recv()
            for j, chip in enumerate(chips):
                copy(a, 4 + j, (*chip, 1 - mc), me).wait_recv()
        for cp in first + passed:
            cp.wait_send()
        for cp in mine:
            cp.wait()

    return pl.pallas_call(
        body, name=name, in_specs=[ANY] * n, out_specs=[ANY] * n,
        out_shape=[jax.ShapeDtypeStruct((N_DEV,) + b.shape, b.dtype) for b in blocks],
        scratch_shapes=[pltpu.SemaphoreType.DMA((7 * n,)), pltpu.SemaphoreType.DMA((7 * n,)),
                        pltpu.SemaphoreType.DMA((n,))],
    )(*blocks)


def _l0_in(x, w0p, bcast=()):
    n = len(bcast)
    tm, tn = 1024, 1024
    gi, gj = SEQ // tm, 2048 // tn

    def body(x_ref, w_ref, *rest):
        o_ref, xb_ref = rest[n], rest[n + 1]
        i, j = pl.program_id(0), pl.program_id(1)
        if n:
            copies = _peer_copies(rest[:n], rest[n + 2:2 * n + 2], rest[2 * n + 2:], [])

            @pl.when((i == 0) & (j == 0))
            def _():
                for cp in copies:
                    cp.start()

        xb = _mx(x_ref[...])
        xb_ref[...] = xb
        o_ref[...] = _dot(xb, w_ref[...])

        if n:
            @pl.when((i == gi - 1) & (j == gj - 1))
            def _():
                for cp in copies:
                    cp.wait()

    res = pl.pallas_call(
        body, name="l0_in", grid=(gi, gj),
        in_specs=[pl.BlockSpec((tm, D_MODEL), lambda i, j: (i, 0)), pl.BlockSpec((D_MODEL, tn), lambda i, j: (0, j))]
        + [ANY] * n,
        out_specs=[pl.BlockSpec((tm, tn), lambda i, j: (i, j)), pl.BlockSpec((tm, D_MODEL), lambda i, j: (i, 0))]
        + [ANY] * n,
        out_shape=[jax.ShapeDtypeStruct((SEQ, 2048), F32), jax.ShapeDtypeStruct((SEQ, D_MODEL), MXU_DTYPE)]
        + _exchange_shapes([], bcast),
        scratch_shapes=_exchange_sems(n) if n else [],
        compiler_params=_params(("arbitrary", "arbitrary")),
    )(x, w0p, *bcast)
    return res[0], res[1], res[2:]


def _l0_dx(dxr, dgate, dtail, w0p, dv0, scatter=(), bcast=()):
    arrays, ranges = _scatter_args(scatter)
    n = len(arrays) + len(bcast)
    tm = 1024
    steps = SEQ // tm

    def body(dxr_ref, dg_ref, dt_ref, w_ref, dv_ref, *rest):
        o_ref = rest[n]
        i = pl.program_id(0)
        if n:
            copies = _peer_copies(rest[:n], rest[n + 1:2 * n + 1], rest[2 * n + 1:], ranges)

            @pl.when(i == 0)
            def _():
                for cp in copies:
                    cp.start()

        o_ref[...] = (DN_ALPHA * dv_ref[...] + _dot_nt(dg_ref[...], w_ref[:, 0:1024])
                      + _dot_nt(dxr_ref[...], w_ref[:, 1024:1536]) + _dot_nt(dt_ref[...], w_ref[:, 1536:2048]))

        if n:
            @pl.when(i == steps - 1)
            def _():
                for cp in copies:
                    cp.wait()

    rows = lambda w: pl.BlockSpec((tm, w), lambda i: (i, 0))
    res = pl.pallas_call(
        body, name="l0_dx", grid=(steps,),
        in_specs=[rows(512), rows(1024), rows(512), pl.BlockSpec((D_MODEL, 2048), lambda i: (0, 0)), rows(D_MODEL)]
        + [ANY] * n,
        out_specs=[rows(D_MODEL)] + [ANY] * n,
        out_shape=[jax.ShapeDtypeStruct((SEQ, D_MODEL), F32)] + _exchange_shapes(arrays, bcast),
        scratch_shapes=_exchange_sems(n) if n else [],
        compiler_params=_params(("arbitrary",)),
    )(dxr, dgate, dtail, w0p, dv0, *arrays, *bcast)
    return res[0], res[1:]


def _scatter_args(scatter):
    arrays = [s[0] if isinstance(s, tuple) else s for s in scatter]
    ranges = [s[1] if isinstance(s, tuple) else (0, N_DEV) for s in scatter]
    return arrays, ranges


def _exchange_shapes(scatter, bcast):
    return ([jax.ShapeDtypeStruct((N_DEV,) + a.shape[1:], a.dtype) for a in scatter]
            + [jax.ShapeDtypeStruct((N_DEV,) + a.shape, a.dtype) for a in bcast])


def _exchange_sems(n):
    return [pltpu.SemaphoreType.DMA((7 * n,)), pltpu.SemaphoreType.DMA((7 * n,)), pltpu.SemaphoreType.DMA((n,))]


class _GuardedCopy:
    def __init__(self, copy, send=None, recv=None, local=False):
        self.copy, self.send, self.recv, self.local = copy, send, recv, local

    @staticmethod
    def _run(pred, fn):
        if pred is None:
            fn()
        else:
            pl.when(pred)(fn)

    def start(self):
        self._run(self.send, self.copy.start)

    def wait(self):
        if self.local:
            self._run(self.send, self.copy.wait)
        else:
            self._run(self.send, self.copy.wait_send)
            self._run(self.recv, self.copy.wait_recv)


def _peer_copies(in_refs, out_refs, sems, ranges):
    send_sems, recv_sems, local_sems = sems
    n, ns = len(in_refs), len(ranges)
    mx, my, mc = lax.axis_index("x"), lax.axis_index("y"), lax.axis_index("c")
    me = 4 * mx + 2 * my + mc

    def src(a, slot):
        return in_refs[a].at[slot - ranges[a][0]] if a < ns else in_refs[a]

    def member(a, dev):
        if a >= ns or ranges[a] == (0, N_DEV):
            return None
        return (dev >= ranges[a][0]) & (dev < ranges[a][1])

    copies = [_GuardedCopy(pltpu.make_async_copy(src(a, me), out_refs[a].at[me], local_sems.at[a]),
                           send=member(a, me), local=True) for a in range(n)]
    for k in range(1, N_DEV):
        px, py, pc = _flip(mx, (k >> 2) & 1), _flip(my, (k >> 1) & 1), _flip(mc, k & 1)
        peer = 4 * px + 2 * py + pc
        for a in range(n):
            copies.append(_GuardedCopy(pltpu.make_async_remote_copy(
                src_ref=src(a, peer), dst_ref=out_refs[a].at[me],
                send_sem=send_sems.at[7 * a + k - 1], recv_sem=recv_sems.at[7 * a + k - 1],
                device_id=(px, py, pc), device_id_type=MESH), send=member(a, peer), recv=member(a, me)))
    return copies


def _segments(col_map, width):
    segs = []
    for lo, hi, arr, alo in col_map:
        for s in range(N_DEV):
            a, b = max(lo, s * width), min(hi, (s + 1) * width)
            if a < b:
                segs.append((s, a - s * width, b - a, arr, alo + a - lo))
    return segs


COPY_ROWS = 256


def _unshard(g8, col_map, widths, name):
    parts = list(g8) if isinstance(g8, (list, tuple)) else [g8]
    w = parts[0].shape[2]
    r = sum(a.shape[1] for a in parts)
    rb = min(r, COPY_ROWS)
    assert all(a.shape[1] % rb == 0 for a in parts)
    first = [sum(a.shape[1] for a in parts[:k]) // rb for k in range(len(parts))]
    count = [a.shape[1] // rb for a in parts]
    segs = _segments(col_map, w)
    part_map = lambda b0, nb: (lambda i: (0, jnp.clip(i - b0, 0, nb - 1), 0))

    def body(*refs):
        g_refs, o_refs = refs[:len(parts)], refs[len(parts):]
        i = pl.program_id(0)
        for o_ref in o_refs:
            o_ref[...] = jnp.zeros_like(o_ref)
        for g_ref, b0, nb in zip(g_refs, first, count):
            @pl.when((i >= b0) & (i < b0 + nb))
            def _():
                for s, llo, n, arr, alo in segs:
                    o_refs[arr][:, alo:alo + n] = g_ref[s, :, llo:llo + n]

    return pl.pallas_call(
        body, name=name, grid=(r // rb,),
        in_specs=[pl.BlockSpec((N_DEV, rb, w), part_map(b0, nb)) for b0, nb in zip(first, count)],
        out_specs=[pl.BlockSpec((rb, n), lambda i: (i, 0)) for n in widths],
        out_shape=[jax.ShapeDtypeStruct((r, n), parts[0].dtype) for n in widths],
        compiler_params=_params(("parallel",)),
    )(*parts)


def _reshard(srcs, col_map, w, dtype, name, shards=(0, N_DEV)):
    r = srcs[0].shape[0]
    rb = min(r, COPY_ROWS)
    lo, hi = shards
    segs = [sg for sg in _segments(col_map, w) if lo <= sg[0] < hi]

    def body(*refs):
        o_ref = refs[-1]
        for s, llo, n, arr, alo in segs:
            o_ref[s - lo, :, llo:llo + n] = refs[arr][:, alo:alo + n].astype(dtype)

    return pl.pallas_call(
        body, name=name, grid=(r // rb,),
        in_specs=[pl.BlockSpec((rb, a.shape[1]), lambda i: (i, 0)) for a in srcs],
        out_specs=pl.BlockSpec((hi - lo, rb, w), lambda i: (0, i, 0)),
        out_shape=jax.ShapeDtypeStruct((hi - lo, r, w), dtype),
        compiler_params=_params(("parallel",)),
    )(*srcs)


def _adamw(parts, w, m, v, name):
    r, c = w.shape
    tr = COPY_ROWS if r % COPY_ROWS == 0 else r

    def body(p_ref, w_ref, m_ref, v_ref, g_ref, d_ref, mo_ref, vo_ref):
        g = p_ref[0].astype(F32)
        for s in range(1, N_DEV):
            g = g + p_ref[s].astype(F32)
        g_ref[...] = g
        d_ref[...], mo_ref[...], vo_ref[...] = _adamw_math(g, w_ref[...], m_ref[...], v_ref[...])

    blk = pl.BlockSpec((tr, c), lambda i: (i, 0))
    out = jax.ShapeDtypeStruct((r, c), F32)
    return pl.pallas_call(
        body, name=name, grid=(r // tr,),
        in_specs=[pl.BlockSpec((N_DEV, tr, c), lambda i: (0, i, 0)), blk, blk, blk],
        out_specs=[blk, blk, blk, blk], out_shape=[out, out, out, out],
        compiler_params=_params(("parallel",)),
    )(parts, w, m, v)


def _adamw_cols(parts, wt, mt, vt, name):
    _, r, c = parts.shape
    per = r // LANES
    linear = wt.shape != (c, r)
    assert wt.shape == ((c * per, LANES) if linear else (c, r)) and (per == SUBLANES or not linear)
    n = min(c, LANES)
    starts = list(range(0, c - n + 1, LANES)) + ([c - n] if c % n else [])

    def body(p_ref, w_ref, m_ref, v_ref, g_ref, d_ref, mo_ref, vo_ref, gt_sc, pad_sc):
        for lo in starts:
            g = p_ref[0, :, lo:lo + n].astype(F32)
            for s in range(1, N_DEV):
                g = g + p_ref[s, :, lo:lo + n].astype(F32)
            if n < LANES:
                pad_sc[...] = jnp.zeros_like(pad_sc)
                pad_sc[:, 0:n] = g
                g = pad_sc[...]
            gt = g.T
            if linear:
                for k in range(per):
                    gt_sc[pl.ds(lo * per + k, n, stride=per), :] = gt[0:n, k * LANES:(k + 1) * LANES]
            else:
                gt_sc[lo:lo + n, :] = gt[0:n]
        g = gt_sc[...]
        d, mn, vn = _adamw_math(g, w_ref[...], m_ref[...], v_ref[...])
        for ref, val in ((g_ref, g), (d_ref, d), (mo_ref, mn), (vo_ref, vn)):
            ref[...] = val.reshape(c, 1, r) if linear else val

    out = jax.ShapeDtypeStruct((c, 1, r) if linear else (c, r), F32)
    return pl.pallas_call(
        body, name=name, out_shape=[out, out, out, out],
        scratch_shapes=[pltpu.VMEM(wt.shape, F32), pltpu.VMEM((r, LANES), F32)],
        compiler_params=pltpu.CompilerParams(vmem_limit_bytes=VMEM_LIMIT),
    )(parts, wt, mt, vt)


def _adamw_math(g, w, m, v):
    mn = ADAM_B1 * m + (1.0 - ADAM_B1) * g
    vn = ADAM_B2 * v + (1.0 - ADAM_B2) * (g * g)
    m_hat = mn / (1.0 - ADAM_B1 ** ADAM_STEP)
    v_hat = vn / (1.0 - ADAM_B2 ** ADAM_STEP)
    return -ADAM_LR * (m_hat / (jnp.sqrt(v_hat) + ADAM_EPS) + ADAM_WD * w), mn, vn


SMALL = (("ab_conv_w", 0, 4, 64), ("ssd_conv_w", 4, 4, 384), ("ssd_conv_b", 8, 1, 384), ("ssd_norm", 9, 1, 256),
         ("ssd_ln_g", 10, 1, 128), ("ssd_ln_b", 11, 1, 128))
VECS = (("ab_conv_b", 512), ("ab_gate_a_b", 512), ("ab_gate_x_b", 512), ("ab_lambda", 512), ("mla_q_norm", 256),
        ("mla_kv_norm", 128), ("ab_ln_g", 1024), ("ab_ln_b", 1024), ("ssd_dt_bias", 32), ("ssd_a_log", 32),
        ("ssd_d", 32))
GATES = ("ab_gate_a_w", "ab_gate_x_w")
SMALL_NAMES = tuple(n for n, *_ in SMALL) + tuple(n for n, _ in VECS) + GATES
VMEM_WHOLE = pl.BlockSpec(memory_space=pltpu.VMEM)


def _view2d(name, a):
    if name in GATES:
        return a.reshape(RNN_W, 64)
    return a[0] if a.ndim == 3 else a


def _unshard_small(g):
    widths = (512, 3072, 3072, 2048, 1024, 1024)

    def body(*refs):
        ins, outs = refs[:6], refs[6:]
        outs[0][...] = jnp.zeros_like(outs[0])
        outs[1][...] = jnp.zeros_like(outs[1])
        for (_, _, nr, c), i_ref, o_ref in zip(SMALL, ins, outs):
            for j in range(N_DEV):
                o_ref[0:nr, j * c:(j + 1) * c] = i_ref[j]

    return pl.pallas_call(
        body, name="unshard_small", in_specs=[VMEM_WHOLE] * 6, out_specs=[VMEM_WHOLE] * 6,
        out_shape=[jax.ShapeDtypeStruct((SUBLANES if nr == 4 else 1, w), F32) for (_, _, nr, _), w in zip(SMALL, widths)],
    )(*g)


def _prep_repl(ga, gx, dt_bias, a_log, d):
    def body(ga_ref, gx_ref, b_ref, al_ref, d_ref, wa_ref, wx_ref, b128_ref, al128_ref, dx_ref):
        wa_ref[...] = jnp.zeros_like(wa_ref)
        wx_ref[...] = jnp.zeros_like(wx_ref)
        for hd in range(8):
            hs = slice(hd * 64, (hd + 1) * 64)
            wa_ref[hs, hs] = _mx(ga_ref[hs, :])
            wx_ref[hs, hs] = _mx(gx_ref[hs, :])
        b128_ref[...] = jnp.zeros_like(b128_ref)
        al128_ref[...] = jnp.zeros_like(al128_ref)
        b128_ref[:, 0:SSD_HEADS] = b_ref[...]
        al128_ref[:, 0:SSD_HEADS] = al_ref[...]
        dv = d_ref[...]
        for hd in range(SSD_HEADS):
            dx_ref[:, hd * SSD_P:(hd + 1) * SSD_P] = jnp.broadcast_to(dv[:, hd:hd + 1], (1, SSD_P))

    return pl.pallas_call(
        body, name="prep_repl", in_specs=[VMEM_WHOLE] * 5, out_specs=[VMEM_WHOLE] * 5,
        out_shape=[jax.ShapeDtypeStruct((RNN_W, RNN_W), MXU_DTYPE), jax.ShapeDtypeStruct((RNN_W, RNN_W), MXU_DTYPE),
                   jax.ShapeDtypeStruct((1, LANES), F32), jax.ShapeDtypeStruct((1, LANES), F32),
                   jax.ShapeDtypeStruct((1, SSD_INNER), F32)],
    )(ga, gx, dt_bias, a_log, d)


LOSS_ROW = 11


def _pack_small(dvec0, g_wa, g_wx, dqnw, dknw, dgb0, dvec1, dcw1, dnw, dgb1, loss8):
    def body(dvec0_ref, gwa_ref, gwx_ref, dqn_ref, dkn_ref, dgb0_ref, dvec1_ref, dcw1_ref, dnw_ref, dgb1_ref,
             loss_ref, sm_ref, vec_ref, gg_ref):
        sm_ref[...] = jnp.zeros_like(sm_ref)
        vec_ref[...] = jnp.zeros_like(vec_ref)
        sharded = ((dvec0_ref, 4), (dcw1_ref, 0), (dcw1_ref, 4), (dnw_ref, 0), (dgb1_ref, 0), (dgb1_ref, 1))
        for (_, r0, nr, c), (src, sr) in zip(SMALL, sharded):
            for j in range(N_DEV):
                sm_ref[j, r0:r0 + nr, 0:c] = src[sr:sr + nr, j * c:(j + 1) * c]
        vectors = ((dvec0_ref, 3), (dvec0_ref, 0), (dvec0_ref, 1), (dvec0_ref, 2), (dqn_ref, 0), (dkn_ref, 0),
                   (dgb0_ref, 0), (dgb0_ref, 1), (dvec1_ref, 0), (dvec1_ref, 1), (dvec1_ref, 2))
        for row, ((_, c), (src, sr)) in enumerate(zip(VECS, vectors)):
            vec_ref[row:row + 1, 0:c] = src[sr:sr + 1, 0:c]
        vec_ref[LOSS_ROW:LOSS_ROW + 1, 0:LANES] = loss_ref[0:1, :]
        for hd in range(8):
            hs = slice(hd * 64, (hd + 1) * 64)
            gg_ref[hs, 0:64] = _mx(gwa_ref[hs, hs])
            gg_ref[hs, 64:128] = _mx(gwx_ref[hs, hs])

    return pl.pallas_call(
        body, name="pack_small", in_specs=[VMEM_WHOLE] * 11, out_specs=[VMEM_WHOLE] * 3,
        out_shape=[jax.ShapeDtypeStruct((N_DEV, 16, 384), F32), jax.ShapeDtypeStruct((16, 1024), F32),
                   jax.ShapeDtypeStruct((RNN_W, LANES), MXU_DTYPE)],
    )(dvec0, g_wa, g_wx, dqnw, dknw, dgb0, dvec1, dcw1, dnw, dgb1, loss8)


def _adamw_small(recv_sm, recv_vec, recv_gg, wmv):
    plan = ([(0, r0, nr, c) for _, r0, nr, c in SMALL] + [(1, row, 1, c) for row, (_, c) in enumerate(VECS)]
            + [(2, 0, RNN_W, 0), (2, 0, RNN_W, 64)])
    n = len(plan)

    def body(*refs):
        recv, ins, outs = refs[:3], refs[3:3 + 3 * n], refs[3 + 3 * n:]
        for i, (src, r0, nr, c) in enumerate(plan):
            cols = slice(c, c + 64) if src == 2 else slice(0, c)
            g = recv[src][0, r0:r0 + nr, cols].astype(F32)
            for s in range(1, N_DEV):
                g = g + recv[src][s, r0:r0 + nr, cols].astype(F32)
            w_ref, m_ref, v_ref = ins[3 * i:3 * i + 3]
            outs[4 * i][...] = g
            outs[4 * i + 1][...], outs[4 * i + 2][...], outs[4 * i + 3][...] = _adamw_math(
                g, w_ref[...], m_ref[...], v_ref[...])
        loss = recv[1][0, LOSS_ROW:LOSS_ROW + 1, 0:LANES]
        for s in range(1, N_DEV):
            loss = loss + recv[1][s, LOSS_ROW:LOSS_ROW + 1, 0:LANES]
        outs[4 * n][...] = loss

    flat = [a for t in wmv for a in t]
    return pl.pallas_call(
        body, name="adamw_small", in_specs=[VMEM_WHOLE] * (3 + 3 * n), out_specs=[VMEM_WHOLE] * (4 * n + 1),
        out_shape=[jax.ShapeDtypeStruct(t[0].shape, F32) for t in wmv for _ in range(4)]
        + [jax.ShapeDtypeStruct((1, LANES), F32)],
    )(recv_sm, recv_vec, recv_gg, *flat)


BIG_L0 = ("ab_w_in", "ab_w_out", "mla_w_uq", "mla_w_ukv")
BIG_L1 = ("ssd_w_in", "ssd_w_out")
COLUMN_SHARDED = ("ab_w_in", "mla_w_uq", "ssd_w_in")
W1_EARLY_ROWS = 256

MAP_W0 = ((0, 512, 0, 1024), (512, 1536, 0, 0), (1536, 1920, 0, 1536), (1920, 1952, 0, 1984))
MAP_W1 = ((0, 2048, 0, 0), (2048, 5120, 1, 0), (5120, 5152, 2, 0))
MAP_WQ = tuple((96 * hd, 96 * hd + 96, 0, 128 * hd) for hd in range(8))
MAP_WKV = (tuple((128 * hd, 128 * hd + 64, 0, 128 * hd) for hd in range(8))
           + tuple((128 * hd + 64, 128 * hd + 128, 0, 1024 + 64 * hd) for hd in range(8)))
MAP_G0 = ((0, 512, 0, 0), (512, 1536, 1, 0), (1536, 1920, 2, 0), (1920, 1952, 2, 448))
W0_EARLY, W0_LATE = (0, 6), (6, 8)


def kernel(x, positions, ab_w_in, ab_conv_w, ab_conv_b, ab_gate_a_w, ab_gate_a_b, ab_gate_x_w, ab_gate_x_b, ab_lambda, mla_q_norm, mla_kv_norm, mla_w_uq, mla_w_ukv, ab_w_out, ab_ln_g, ab_ln_b, ssd_w_in, ssd_conv_w, ssd_conv_b, ssd_dt_bias, ssd_a_log, ssd_d, ssd_norm, ssd_w_out, ssd_ln_g, ssd_ln_b, loss_target, m_ab_w_in, m_ab_conv_w, m_ab_conv_b, m_ab_gate_a_w, m_ab_gate_a_b, m_ab_gate_x_w, m_ab_gate_x_b, m_ab_lambda, m_mla_q_norm, m_mla_kv_norm, m_mla_w_uq, m_mla_w_ukv, m_ab_w_out, m_ab_ln_g, m_ab_ln_b, m_ssd_w_in, m_ssd_conv_w, m_ssd_conv_b, m_ssd_dt_bias, m_ssd_a_log, m_ssd_d, m_ssd_norm, m_ssd_w_out, m_ssd_ln_g, m_ssd_ln_b, v_ab_w_in, v_ab_conv_w, v_ab_conv_b, v_ab_gate_a_w, v_ab_gate_a_b, v_ab_gate_x_w, v_ab_gate_x_b, v_ab_lambda, v_mla_q_norm, v_mla_kv_norm, v_mla_w_uq, v_mla_w_ukv, v_ab_w_out, v_ab_ln_g, v_ab_ln_b, v_ssd_w_in, v_ssd_conv_w, v_ssd_conv_b, v_ssd_dt_bias, v_ssd_a_log, v_ssd_d, v_ssd_norm, v_ssd_w_out, v_ssd_ln_g, v_ssd_ln_b):
    args = dict(locals())
    bf = MXU_DTYPE
    big = {n: [args[pre + n][0] for pre in ("", "m_", "v_")] for n in BIG_L0 + BIG_L1}
    sml = {n: [_view2d(n, args[pre + n]) for pre in ("", "m_", "v_")] for n in SMALL_NAMES}

    w0_8, cw0_8 = _all_gather([big["ab_w_in"][0].astype(bf), sml["ab_conv_w"][0]], "gather_params")
    p = {"cw0_8": cw0_8, "l0_blocks": [big[n][0].astype(bf) for n in BIG_L0[1:]] + [sml[n][0] for n, *_ in SMALL[1:]]}
    p["w0p"], = _unshard(w0_8, MAP_W0, (2048,), "unshard_w0")
    p["wa"], p["wx"], p["dt_bias"], p["a_log"], p["d_x"] = _prep_repl(
        sml["ab_gate_a_w"][0], sml["ab_gate_x_w"][0], sml["ssd_dt_bias"][0], sml["ssd_a_log"][0], sml["ssd_d"][0])
    for key, n in (("cb0", "ab_conv_b"), ("ba", "ab_gate_a_b"), ("bx", "ab_gate_x_b"), ("lam", "ab_lambda"),
                   ("qn_w", "mla_q_norm"), ("kn_w", "mla_kv_norm"), ("g0", "ab_ln_g"), ("b0", "ab_ln_b")):
        p[key] = sml[n][0]

    _, recv_early, recv, _, grad_x = _local_step(
        x[0], positions[0], loss_target[0], p, [big[n][0].astype(bf) for n in BIG_L1])

    me = 4 * lax.axis_index("x") + 2 * lax.axis_index("y") + lax.axis_index("c")
    parts = dict(recv_early, ab_w_in=jnp.where(me >= W0_LATE[0], recv[0], recv_early["ab_w_in"]),
                 mla_w_uq=recv[1], mla_w_ukv=recv[2])

    outs = {}
    kinds = ("grad", "delta", "new_m", "new_v")
    for n in BIG_L0 + BIG_L1:
        if n in COLUMN_SHARDED:
            rows, cols = big[n][0].shape
            if rows == SUBLANES * LANES:
                wmv_t = [jnp.transpose(args[pre + n], (2, 0, 1)).reshape(cols * SUBLANES, LANES) for pre in ("", "m_", "v_")]
                back = lambda res: jnp.transpose(res, (1, 2, 0))
            else:
                wmv_t = [args[pre + n][0].T for pre in ("", "m_", "v_")]
                back = lambda res: res.T[None]
            for kind, res in zip(kinds, _adamw_cols(parts[n], *wmv_t, "adamw_" + n)):
                outs[kind, n] = back(res)
            continue
        for kind, res in zip(kinds, _adamw(parts[n], *big[n], "adamw_" + n)):
            outs[kind, n] = res[None]
    res = _adamw_small(*recv[3:], [sml[n] for n in SMALL_NAMES])
    for i, n in enumerate(SMALL_NAMES):
        for k, kind in enumerate(kinds):
            outs[kind, n] = res[4 * i + k].reshape(args[n].shape)

    loss = res[4 * len(SMALL_NAMES)][0, 0]
    order = ["ab_w_in", "ab_conv_w", "ab_conv_b", "ab_gate_a_w", "ab_gate_a_b", "ab_gate_x_w", "ab_gate_x_b",
             "ab_lambda", "mla_q_norm", "mla_kv_norm", "mla_w_uq", "mla_w_ukv", "ab_w_out", "ab_ln_g", "ab_ln_b",
             "ssd_w_in", "ssd_conv_w", "ssd_conv_b", "ssd_dt_bias", "ssd_a_log", "ssd_d", "ssd_norm", "ssd_w_out",
             "ssd_ln_g", "ssd_ln_b"]
    return (loss, grad_x[None], *[outs[kind, n] for kind in ("grad", "delta", "new_m", "new_v") for n in order])


def _local_step(x, pos, target, p, l1_blocks):
    bf = MXU_DTYPE
    inv_freq = 10000.0 ** (-jnp.arange(0, 32, 2, dtype=F32) / 32)
    ang = inv_freq[:, None] * pos.astype(F32)[None, :]
    cos, sin = jnp.cos(ang), jnp.sin(ang)
    zeros = lambda n: jnp.zeros((n, SEQ), F32)
    tc = jnp.concatenate([jnp.ones((64, SEQ), F32), cos, cos, zeros(32)], axis=0)
    tsa = jnp.concatenate([zeros(64), -sin, zeros(48)], axis=0)
    tsb = jnp.concatenate([zeros(80), sin, zeros(32)], axis=0)

    w0p, wa, wxg = (p[k] for k in ("w0p", "wa", "wx"))
    cb0, ba, bx, lam = (p[k] for k in ("cb0", "ba", "bx", "lam"))
    qn_w, kn_w, g0, b0 = (p[k] for k in ("qn_w", "kn_w", "g0", "b0"))
    dt_bias, a_log, d_x = (p[k] for k in ("dt_bias", "a_log", "d_x"))
    tril = jnp.tril(jnp.ones((SSD_L, SSD_L), F32))
    expand_t = (jnp.arange(SSD_INNER)[:, None] // SSD_P == jnp.arange(LANES)[None, :]).astype(jnp.bfloat16)

    w1_blk = l1_blocks[0]
    proj0, xb, l0_8 = _l0_in(x, w0p, bcast=p["l0_blocks"] + [w1_blk[:W1_EARLY_ROWS]])
    wo0 = l0_8[0].reshape(D_MODEL, D_MODEL)
    wq, = _unshard(l0_8[1], MAP_WQ, (1024,), "unshard_wq")
    wkv, = _unshard(l0_8[2], MAP_WKV, (1536,), "unshard_wkv")
    cw0, cw1, cb1, nw, g1, b1 = _unshard_small([p["cw0_8"]] + list(l0_8[3:-1]))
    xc, h = _rglru_fwd(proj0, cw0, cb0, wa, ba, wxg, bx, lam)
    qn, kn, qc, kc, vc = _mla_fwd(proj0, qn_w, kn_w, wq, wkv, tc, tsa, tsb)
    o, lse, (w1_rest,) = _flash_fwd(qc, kc, vc, bcast=[w1_blk[W1_EARLY_ROWS:]])
    w1z, w1x, w1d = _unshard([l0_8[-1], w1_rest], MAP_W1, (2048, 3072, 128), "unshard_w1")
    y0, v0, x1, x1b = _l0_out(h, o, proj0, x, wo0, g0, b0)

    z, dt_raw = _l1_in(x1b, w1z, w1d)
    xbc, pre, act = _ssd_conv_fwd(x1b, w1x, cw1, cb1)
    ys, hprev, (wo1_8,) = _ssd_scan_fwd(act, dt_raw, dt_bias, a_log, d_x, tril, expand_t, bcast=l1_blocks[1:])
    wo1 = wo1_8.reshape(SSD_INNER, D_MODEL)
    dv1, dgb1, loss8, g_wo1 = _l1_out(ys, z, nw, wo1, x1, g1, b1, target)

    dys, dz, dnw, g_z = _l1_gate_bwd(dv1, wo1, ys, z, nw, x1b)
    dact, ddt_raw, dvec1, g_dt, (recv_wo1,) = _ssd_scan_bwd(
        dys, act, dt_raw, hprev, dt_bias, a_log, d_x, tril, expand_t, x1b,
        scatter=[g_wo1.reshape(N_DEV, 256, D_MODEL)])
    dxbc, dcw1, g_xbc = _ssd_conv_bwd(dact, pre, xbc, cw1, x1b)

    dv0, dgb0 = _l1_dx_ln(dz, dxbc, ddt_raw, dv1, v0, w1z, w1x, w1d, g0)
    dh, do, dgate, g_wo0, g_gate = _gate_bwd(dv0, wo0, h, o, proj0, y0, xb)
    dxr, g_wa, g_wx, dvec0, g_rnn = _rglru_bwd(dh, xc, h, proj0, cw0, wa, ba, wxg, bx, lam, xb)
    early = [_reshard([g_z, g_xbc, g_dt], MAP_W1, 644, bf, "reshard_w1"), g_wo0.astype(bf).reshape(N_DEV, 128, D_MODEL),
             (_reshard([g_rnn, g_gate], MAP_G0, 244, bf, "reshard_w0_early", shards=W0_EARLY), W0_EARLY)]
    dq, dk, dvv, (recv_w1, recv_wo0, recv_w0) = _flash_bwd(qc, kc, vc, o, do, lse, scatter=early)
    recv_early = {"ssd_w_in": recv_w1, "ssd_w_out": recv_wo1, "ab_w_out": recv_wo0, "ab_w_in": recv_w0}
    dtail, g_wq, g_wkv, dqnw, dknw, g_tail = _mla_bwd(dq, dk, dvv, proj0, qn, kn, qn_w, kn_w, wq, wkv, tc, tsa, tsb, xb)

    acc = {"g_rnn": g_rnn, "g_gate": g_gate, "g_tail": g_tail, "g_wq": g_wq, "g_wkv": g_wkv,
           "dvec0": dvec0, "g_wa": g_wa, "g_wx": g_wx, "dqnw": dqnw, "dknw": dknw, "dgb0": dgb0, "dvec1": dvec1,
           "dcw1": dcw1, "dnw": dnw, "dgb1": dgb1}
    late = [(_reshard([g_rnn, g_gate, g_tail], MAP_G0, 244, bf, "reshard_w0_late", shards=W0_LATE), W0_LATE),
            _reshard([g_wq], MAP_WQ, 96, bf, "reshard_wq"), _reshard([g_wkv], MAP_WKV, 128, bf, "reshard_wkv")]
    sm_slots, vec_rows, gates = _pack_small(dvec0, g_wa, g_wx, dqnw, dknw, dgb0, dvec1, dcw1, dnw, dgb1, loss8)
    dx, recv_late = _l0_dx(dxr, dgate, dtail, w0p, dv0, scatter=late + [sm_slots], bcast=[vec_rows, gates])
    return acc, recv_early, recv_late, loss8[0, 0], dx
```

```python
import math

import jax
import jax.numpy as jnp
from jax import lax
from jax.experimental import pallas as pl
from jax.experimental.pallas import tpu as pltpu

F32 = jnp.float32
MXU_DTYPE = jnp.bfloat16

N_DEV = 8
SEQ = 4096
D_MODEL = 1024
DN_ALPHA = 4.0 ** 0.25
RNN_W = 512
MLA_HEADS = 8
ATT_SCALE = 96.0 ** -0.5
ATT_C = ATT_SCALE * math.log2(math.e)
RG_C = 8.0
SSD_INNER = 2048
SSD_HEADS = 32
SSD_P = 64
SSD_GROUPS = 4
SSD_N = 128
SSD_L = 128
SSD_CONV = 3072
LANES = 128
SUBLANES = 8
VMEM_LIMIT = 56 * 1024 * 1024

ADAM_LR, ADAM_B1, ADAM_B2, ADAM_EPS, ADAM_WD, ADAM_STEP = 0.001, 0.9, 0.999, 1e-08, 0.01, 10

HIGHEST = lax.Precision.HIGHEST


def _params(sem, limit=VMEM_LIMIT):
    return pltpu.CompilerParams(dimension_semantics=sem, vmem_limit_bytes=limit)


def _dot(a, b):
    return lax.dot_general(a, b, (((1,), (0,)), ((), ())), preferred_element_type=F32)


def _dot_nt(a, b):
    return lax.dot_general(a, b, (((1,), (1,)), ((), ())), preferred_element_type=F32)


def _dot_tn(a, b):
    return lax.dot_general(a, b, (((0,), (0,)), ((), ())), preferred_element_type=F32)


def _dot_hi(a, b):
    return lax.dot_general(a, b, (((1,), (0,)), ((), ())), precision=HIGHEST, preferred_element_type=F32)


def _mx(v):
    return v.astype(MXU_DTYPE)


def _sigmoid(v):
    return 1.0 / (1.0 + jnp.exp(-v))


def _log1p_pos(e):
    poly = e * (1.0 - e * (0.5 - e * (1.0 / 3.0 - e * 0.25)))
    return jnp.where(e < 0.01, poly, jnp.log(1.0 + e))


def _softplus(v):
    return jnp.maximum(v, 0.0) + _log1p_pos(jnp.exp(-jnp.abs(v)))


def _neg_expm1(v):
    poly = -v * (1.0 + v * (0.5 + v * (1.0 / 6.0 + v * (1.0 / 24.0 + v * (1.0 / 120.0)))))
    return jnp.where(jnp.abs(v) < 0.1, poly, 1.0 - jnp.exp(v))


def _silu(v):
    return v * _sigmoid(v)


def _dsilu(v):
    s = _sigmoid(v)
    return s * (1.0 + v * (1.0 - s))


def _shift_down(blk, halo, s):
    if s == 0:
        return blk
    t = blk.shape[0]
    r = pltpu.roll(blk, s, 0)
    hr = pltpu.roll(halo, s, 0)
    row8 = lax.broadcasted_iota(jnp.int32, hr.shape, 0)
    head = jnp.where(row8 < s, hr, r[:SUBLANES])
    return jnp.concatenate([head, r[SUBLANES:]], axis=0) if t > SUBLANES else head


def _shift_up(blk, halo, s):
    if s == 0:
        return blk
    t = blk.shape[0]
    r = pltpu.roll(blk, t - s, 0)
    hr = pltpu.roll(halo, SUBLANES - s, 0)
    row8 = lax.broadcasted_iota(jnp.int32, hr.shape, 0)
    tail = jnp.where(row8 >= SUBLANES - s, hr, r[t - SUBLANES:])
    return jnp.concatenate([r[:t - SUBLANES], tail], axis=0) if t > SUBLANES else tail


def _scan_down(a, u):
    t = a.shape[0]
    row = lax.broadcasted_iota(jnp.int32, a.shape, 0)
    d = 1
    while d < t:
        keep = row >= d
        a_sh = jnp.where(keep, pltpu.roll(a, d, 0), 1.0)
        u_sh = jnp.where(keep, pltpu.roll(u, d, 0), 0.0)
        u = a * u_sh + u
        a = a * a_sh
        d *= 2
    return a, u


def _scan_up(a, u):
    t = a.shape[0]
    row = lax.broadcasted_iota(jnp.int32, a.shape, 0)
    d = 1
    while d < t:
        keep = row < t - d
        a_sh = jnp.where(keep, pltpu.roll(a, t - d, 0), 1.0)
        u_sh = jnp.where(keep, pltpu.roll(u, t - d, 0), 0.0)
        u = a * u_sh + u
        a = a * a_sh
        d *= 2
    return a, u


def _conv4(blk, halo, cw, cb):
    out = cb + blk * cw[3:4]
    for k in range(3):
        out = out + _shift_down(blk, halo, 3 - k) * cw[k:k + 1]
    return out


RG_T = 512
P0_RNN = 2


def _rg_gates(xc, wa, ba, wx, bx, lam):
    xcb = _mx(xc)
    r = _sigmoid(_dot(xcb, wa) + ba)
    ig = _sigmoid(_dot(xcb, wx) + bx)
    sp = _softplus(-lam)
    la = (-RG_C * r) * sp
    a = jnp.exp(la)
    mult = jnp.sqrt(_neg_expm1(2.0 * la))
    return r, ig, sp, a, mult


def _rglru_fwd(proj0, cw8, cb, wa, ba, wx, bx, lam):
    t, w = RG_T, RNN_W
    nb = SEQ // t

    def body(x_ref, halo_ref, cw_ref, cb_ref, wa_ref, ba_ref, wx_ref, bx_ref, lam_ref, xc_ref, h_ref, carry):
        i = pl.program_id(0)

        @pl.when(i == 0)
        def _():
            carry[...] = jnp.zeros_like(carry)

        blk = x_ref[...]
        halo = jnp.where(i > 0, halo_ref[...], 0.0)
        xc = _conv4(blk, halo, cw_ref[...], cb_ref[...])
        _, ig, _, a, mult = _rg_gates(xc, wa_ref[...], ba_ref[...], wx_ref[...], bx_ref[...], lam_ref[...])
        u = mult * (ig * xc)
        big_a, big_u = _scan_down(a, u)
        h = big_a * carry[SUBLANES - 1:SUBLANES, :] + big_u
        carry[...] = h[t - SUBLANES:]
        xc_ref[...] = xc
        h_ref[...] = h

    vec = pl.BlockSpec((1, w), lambda i: (0, 0))
    mat = pl.BlockSpec((w, w), lambda i: (0, 0))
    return pl.pallas_call(
        body, name="rglru_fwd", grid=(nb,),
        in_specs=[pl.BlockSpec((t, w), lambda i: (i, P0_RNN)),
                  pl.BlockSpec((SUBLANES, w), lambda i: (jnp.maximum(i * (t // SUBLANES) - 1, 0), P0_RNN)),
                  pl.BlockSpec((SUBLANES, w), lambda i: (0, 0)), vec, mat, vec, mat, vec, vec],
        out_specs=[pl.BlockSpec((t, w), lambda i: (i, 0)), pl.BlockSpec((t, w), lambda i: (i, 0))],
        out_shape=[jax.ShapeDtypeStruct((SEQ, w), F32), jax.ShapeDtypeStruct((SEQ, w), F32)],
        scratch_shapes=[pltpu.VMEM((SUBLANES, w), F32)],
        compiler_params=_params(("arbitrary",)),
    )(proj0, proj0, cw8, cb, wa, ba, wx, bx, lam)


def _rglru_bwd(dh, xc, h, proj0, cw8, wa, ba, wx, bx, lam, xb):
    t, w = RG_T, RNN_W
    nb = SEQ // t
    tb = t // SUBLANES

    def body(dh_ref, xc_ref, h_ref, hh_ref, x_ref, cw_ref, wa_ref, ba_ref, wx_ref, bx_ref, lam_ref, xb_ref,
             dx_ref, dwa_ref, dwx_ref, dvec_ref, gw_ref, gcarry, dxc_next):
        i = pl.program_id(0)
        rev = nb - 1 - i

        @pl.when(i == 0)
        def _():
            gcarry[...] = jnp.zeros_like(gcarry)
            dxc_next[...] = jnp.zeros_like(dxc_next)
            gw_ref[...] = jnp.zeros_like(gw_ref)
            dwa_ref[...] = jnp.zeros_like(dwa_ref)
            dwx_ref[...] = jnp.zeros_like(dwx_ref)
            dvec_ref[...] = jnp.zeros_like(dvec_ref)

        xc = xc_ref[...]
        wa_v, wx_v = wa_ref[...], wx_ref[...]
        lam_v = lam_ref[...]
        r, ig, sp, a, mult = _rg_gates(xc, wa_v, ba_ref[...], wx_v, bx_ref[...], lam_v)
        dhv = dh_ref[...]
        big_a, big_u = _scan_up(a, a * dhv)
        gg = big_a * gcarry[0:1, :] + big_u
        g = dhv + _shift_up(gg, gcarry[...], 1)
        gcarry[...] = gg[:SUBLANES]
        hhalo = jnp.where(rev > 0, hh_ref[...], 0.0)
        da = g * _shift_down(h_ref[...], hhalo, 1)
        d_mult = g * (ig * xc)
        d_i = g * (mult * xc)
        dxc = g * (mult * ig)
        d_la = da * a - d_mult * (a * a) / mult
        d_r = d_la * (-RG_C * sp)
        d_sp = jnp.sum(d_la * (-RG_C * r), axis=0, keepdims=True)
        d_pa = d_r * r * (1.0 - r)
        d_px = d_i * ig * (1.0 - ig)
        d_pab, d_pxb = _mx(d_pa), _mx(d_px)
        dxc = dxc + _dot_nt(d_pab, wa_v) + _dot_nt(d_pxb, wx_v)
        xcb = _mx(xc)
        dwa_ref[...] += _dot_tn(xcb, d_pab)
        dwx_ref[...] += _dot_tn(xcb, d_pxb)
        dvec_ref[0:1, :] += jnp.sum(d_pa, axis=0, keepdims=True)
        dvec_ref[1:2, :] += jnp.sum(d_px, axis=0, keepdims=True)
        dvec_ref[2:3, :] += d_sp * (-_sigmoid(-lam_v))
        dvec_ref[3:4, :] += jnp.sum(dxc, axis=0, keepdims=True)
        xblk = x_ref[...]
        cw = cw_ref[...]
        dx = dxc * cw[3:4]
        nxt = dxc_next[...]
        dvec_ref[7:8, :] += jnp.sum(dxc * xblk, axis=0, keepdims=True)
        for k in range(3):
            up = _shift_up(dxc, nxt, 3 - k)
            dvec_ref[4 + k:5 + k, :] += jnp.sum(up * xblk, axis=0, keepdims=True)
            dx = dx + up * cw[k:k + 1]
        dxc_next[...] = dxc[:SUBLANES]
        dxb = _mx(dx)
        dx_ref[...] = dxb
        gw_ref[...] += _dot_tn(xb_ref[...], dxb)

    blk = pl.BlockSpec((t, w), lambda i: (nb - 1 - i, 0))
    halo = pl.BlockSpec((SUBLANES, w), lambda i: (jnp.maximum((nb - 1 - i) * tb - 1, 0), 0))
    vec = pl.BlockSpec((1, w), lambda i: (0, 0))
    mat = pl.BlockSpec((w, w), lambda i: (0, 0))
    return pl.pallas_call(
        body, name="rglru_bwd", grid=(nb,),
        in_specs=[blk, blk, blk, halo, pl.BlockSpec((t, w), lambda i: (nb - 1 - i, P0_RNN)),
                  pl.BlockSpec((SUBLANES, w), lambda i: (0, 0)), mat, vec, mat, vec, vec,
                  pl.BlockSpec((t, D_MODEL), lambda i: (nb - 1 - i, 0))],
        out_specs=[blk, mat, mat, pl.BlockSpec((16, w), lambda i: (0, 0)), pl.BlockSpec((D_MODEL, w), lambda i: (0, 0))],
        out_shape=[jax.ShapeDtypeStruct((SEQ, w), MXU_DTYPE), jax.ShapeDtypeStruct((w, w), F32),
                   jax.ShapeDtypeStruct((w, w), F32), jax.ShapeDtypeStruct((16, w), F32),
                   jax.ShapeDtypeStruct((D_MODEL, w), F32)],
        scratch_shapes=[pltpu.VMEM((SUBLANES, w), F32), pltpu.VMEM((SUBLANES, w), F32)],
        compiler_params=_params(("arbitrary",)),
    )(dh, xc, h, h, proj0, cw8, wa, ba, wx, bx, lam, xb)


MLA_T = 512


def _rope(v, c, sa, sb):
    return v * c + pltpu.roll(v, LANES - 16, 1) * sa + pltpu.roll(v, 16, 1) * sb


def _rope_t(dv, c, sa, sb):
    return dv * c + pltpu.roll(dv * sa, 16, 1) + pltpu.roll(dv * sb, LANES - 16, 1)


def _rms(v, g, eps=1e-6):
    rs = lax.rsqrt(jnp.mean(v * v, axis=-1, keepdims=True) + eps)
    return v * rs * g, rs


def _mla_fwd(proj0, q_norm, kv_norm, wq, wkv, tc, tsa, tsb):
    t = MLA_T

    def body(cq_ref, ck_ref, qn_ref, kn_ref, wq_ref, wkv_ref, c_ref, sa_ref, sb_ref,
             oqn_ref, okn_ref, oq_ref, ok_ref, ov_ref):
        c, sa, sb = c_ref[...].T, sa_ref[...].T, sb_ref[...].T
        ck = ck_ref[...]
        qn = _mx(_rms(cq_ref[...], qn_ref[...])[0])
        kn = _mx(_rms(ck[:, :LANES], kn_ref[...])[0])
        oqn_ref[...] = qn
        okn_ref[...] = kn
        krv = _rope(ck[:, LANES:], c, sa, sb)
        qraw = _dot(qn, wq_ref[...])
        kvraw = _dot(kn, wkv_ref[...])
        for hd in range(MLA_HEADS):
            sl = slice(hd * LANES, (hd + 1) * LANES)
            oq_ref[:, sl] = _mx(_rope(qraw[:, sl], c, sa, sb))
            ok_ref[:, sl] = _mx(kvraw[:, sl] + krv)
        ov_ref[...] = _mx(kvraw[:, 1024:])

    tab = pl.BlockSpec((t, LANES), lambda i: (i, 0))
    rot = pl.BlockSpec((LANES, t), lambda i: (0, i))
    wide = pl.BlockSpec((t, 1024), lambda i: (i, 0))
    const = lambda shape: pl.BlockSpec(shape, lambda i: (0, 0))
    return pl.pallas_call(
        body, name="mla_fwd", grid=(SEQ // t,),
        in_specs=[pl.BlockSpec((t, 256), lambda i: (i, 6)), pl.BlockSpec((t, 256), lambda i: (i, 7)),
                  const((1, 256)), const((1, LANES)), const((256, 1024)), const((LANES, 1536)), rot, rot, rot],
        out_specs=[pl.BlockSpec((t, 256), lambda i: (i, 0)), tab, wide, wide, pl.BlockSpec((t, 512), lambda i: (i, 0))],
        out_shape=[jax.ShapeDtypeStruct((SEQ, 256), MXU_DTYPE), jax.ShapeDtypeStruct((SEQ, LANES), MXU_DTYPE),
                   jax.ShapeDtypeStruct((SEQ, 1024), MXU_DTYPE), jax.ShapeDtypeStruct((SEQ, 1024), MXU_DTYPE),
                   jax.ShapeDtypeStruct((SEQ, 512), MXU_DTYPE)],
        compiler_params=_params(("parallel",)),
    )(proj0, proj0, q_norm, kv_norm, wq, wkv, tc, tsa, tsb)


ATT_T = 1024


def _flash_fwd(q, k, v, bcast=()):
    t = ATT_T
    nb = SEQ // t

    steps = [(qi, ki) for qi in range(nb) for ki in range(qi + 1)]
    qi_tab = jnp.asarray([s[0] for s in steps], jnp.int32)
    ki_tab = jnp.asarray([s[1] for s in steps], jnp.int32)

    nx = len(bcast)

    def body(qi_ref, ki_ref, q_ref, k_ref, v_ref, *rest):
        x_refs, (o_ref, lse_ref), g_refs = rest[:nx], rest[nx:nx + 2], rest[nx + 2:2 * nx + 2]
        m_sc, acc_sc = rest[2 * nx + 2:2 * nx + 4]
        step = pl.program_id(1)
        qi, ki = qi_ref[step], ki_ref[step]
        if nx:
            copies = _peer_copies(x_refs, g_refs, rest[2 * nx + 4:], [])

            @pl.when((pl.program_id(0) == 0) & (step == 0))
            def _():
                for cp in copies:
                    cp.start()

        @pl.when(ki == 0)
        def _():
            m_sc[...] = jnp.full_like(m_sc, -jnp.inf)
            acc_sc[...] = jnp.zeros_like(acc_sc)

        def update(q0, nq, nk, masked):
            vv = v_ref[0:nk, :]
            lane_v = lax.broadcasted_iota(jnp.int32, vv.shape, 1)
            qs = slice(q0, q0 + nq)
            for hd in range(2):
                sl = slice(hd * LANES, (hd + 1) * LANES)
                st = _dot_nt(k_ref[0:nk, sl], q_ref[qs, sl])
                if masked:
                    st = jnp.where(lax.broadcasted_iota(jnp.int32, (nk, nq), 0)
                                   <= lax.broadcasted_iota(jnp.int32, (nk, nq), 1) + q0, st, -jnp.inf)
                m_prev = m_sc[hd:hd + 1, qs]
                m_new = jnp.maximum(m_prev, jnp.max(st, axis=0, keepdims=True))
                pt = jnp.exp2((st - m_new) * ATT_C)
                m_sc[hd:hd + 1, qs] = m_new
                vh = jnp.where((lane_v >= hd * 64) & (lane_v < (hd + 1) * 64), vv, jnp.ones_like(vv))
                acc_sc[hd, :, qs] = acc_sc[hd, :, qs] * jnp.exp2((m_prev - m_new) * ATT_C) + _dot_tn(vh, _mx(pt))

        @pl.when(ki < qi)
        def _():
            update(0, t, t, False)

        @pl.when(ki == qi)
        def _():
            update(0, t // 2, t // 2, True)
            update(t // 2, t // 2, t, True)
            a0, a1 = acc_sc[0], acc_sc[1]
            l0, l1 = a0[64:65, :], a1[0:1, :]
            first = lax.broadcasted_iota(jnp.int32, (LANES, t), 0) < 64
            o_ref[...] = jnp.where(first, a0 / l0, a1 / l1).T
            lse_ref[0, 0:1, :] = m_sc[0:1, :] * ATT_SCALE + jnp.log(l0)
            lse_ref[0, 1:2, :] = m_sc[1:2, :] * ATT_SCALE + jnp.log(l1)
            lse_ref[0, 2:SUBLANES, :] = jnp.zeros((SUBLANES - 2, t), F32)

        if nx:
            @pl.when((pl.program_id(0) == 3) & (step == len(steps) - 1))
            def _():
                for cp in copies:
                    cp.wait()

    grid_spec = pltpu.PrefetchScalarGridSpec(
        num_scalar_prefetch=2, grid=(4, len(steps)),
        in_specs=[pl.BlockSpec((t, 256), lambda p, s, qt, kt: (qt[s], p)),
                  pl.BlockSpec((t, 256), lambda p, s, qt, kt: (kt[s], p)),
                  pl.BlockSpec((t, LANES), lambda p, s, qt, kt: (kt[s], p))] + [ANY] * nx,
        out_specs=[pl.BlockSpec((t, LANES), lambda p, s, qt, kt: (qt[s], p)),
                   pl.BlockSpec((1, SUBLANES, t), lambda p, s, qt, kt: (p, 0, qt[s]))] + [ANY] * nx,
        scratch_shapes=[pltpu.VMEM((SUBLANES, t), F32), pltpu.VMEM((2, LANES, t), F32)]
        + (_exchange_sems(nx) if nx else []))
    res = pl.pallas_call(
        body, name="flash_fwd", grid_spec=grid_spec,
        out_shape=[jax.ShapeDtypeStruct((SEQ, 512), F32), jax.ShapeDtypeStruct((4, SUBLANES, SEQ), F32)]
        + _exchange_shapes([], bcast),
        compiler_params=_params(("arbitrary", "arbitrary")),
    )(qi_tab, ki_tab, q, k, v, *bcast)
    return res[0], res[1], res[2:]


def _flash_bwd(q, k, v, o, do, lse, scatter=()):
    t = ATT_T
    nb = SEQ // t

    steps = [(qi, ki) for ki in range(nb) for qi in range(ki, nb)]
    qi_tab = jnp.asarray([s[0] for s in steps], jnp.int32)
    ki_tab = jnp.asarray([s[1] for s in steps], jnp.int32)
    log2e = math.log2(math.e)

    sc_arrays, sc_ranges = _scatter_args(scatter)
    nx = len(sc_arrays)

    def body(qi_ref, ki_ref, q_ref, k_ref, v_ref, o_ref, do_ref, lse_ref, *rest):
        x_refs, (dq_ref, dk_ref, dv_ref), g_refs = rest[:nx], rest[nx:nx + 3], rest[nx + 3:2 * nx + 3]
        dkt_sc, dvt_sc = rest[2 * nx + 3:2 * nx + 5]
        step = pl.program_id(1)
        qi, ki = qi_ref[step], ki_ref[step]
        if nx:
            copies = _peer_copies(x_refs, g_refs, rest[2 * nx + 5:], sc_ranges)

            @pl.when((pl.program_id(0) == 0) & (step == 0))
            def _():
                for cp in copies:
                    cp.start()

        @pl.when(step == 0)
        def _():
            dq_ref[...] = jnp.zeros_like(dq_ref)

        @pl.when(qi == ki)
        def _():
            dkt_sc[...] = jnp.zeros_like(dkt_sc)
            dvt_sc[...] = jnp.zeros_like(dvt_sc)

        def update(q0, nq, nk, masked):
            qs = slice(q0, q0 + nq)
            dov, ov, vv = do_ref[qs, :], o_ref[qs, :], v_ref[0:nk, :]
            lse2 = (lse_ref[0, :, qs] * log2e).T
            lane = lax.broadcasted_iota(jnp.int32, (nq, LANES), 1)
            row_t = lax.broadcasted_iota(jnp.int32, (LANES, nk), 0)
            prod = dov * ov
            do_b = _mx(dov)
            qrows = pl.ds(pl.multiple_of(qi * t + q0, nq), nq)
            dvt_acc = jnp.zeros((LANES, nk), F32)
            dkt_new, dq_new = [], []
            for hd in range(2):
                sl = slice(hd * LANES, (hd + 1) * LANES)
                mine = (lane >= hd * 64) & (lane < (hd + 1) * 64)
                qh, kh = q_ref[qs, sl], k_ref[0:nk, sl]
                p = jnp.exp2(_dot_nt(qh, kh) * ATT_C - lse2[:, hd:hd + 1])
                if masked:
                    p = jnp.where(lax.broadcasted_iota(jnp.int32, (nq, nk), 1)
                                  <= lax.broadcasted_iota(jnp.int32, (nq, nk), 0) + q0, p, 0.0)
                do_h = jnp.where(mine, dov, 0.0)
                delta = jnp.sum(jnp.where(mine, prod, 0.0), axis=1, keepdims=True)
                dp = _dot_nt(_mx(do_h), vv)
                ds = _mx(p * (dp - delta) * ATT_SCALE)
                dvt_acc = dvt_acc + jnp.where((row_t >= hd * 64) & (row_t < (hd + 1) * 64), _dot_tn(do_b, _mx(p)), 0.0)
                dkt_new.append(_dot_tn(qh, ds))
                dq_new.append(_dot(ds, kh))
            for hd in range(2):
                sl = slice(hd * LANES, (hd + 1) * LANES)
                dkt_sc[sl, 0:nk] += dkt_new[hd]
                dq_ref[qrows, sl] += dq_new[hd]
            dvt_sc[:, 0:nk] += dvt_acc

        @pl.when(qi > ki)
        def _():
            update(0, t, t, False)

        @pl.when(qi == ki)
        def _():
            update(0, t // 2, t // 2, True)
            update(t // 2, t // 2, t, True)

        @pl.when(qi == nb - 1)
        def _():
            dk_ref[...] = dkt_sc[...].T
            dv_ref[...] = dvt_sc[...].T

        if nx:
            @pl.when((pl.program_id(0) == 3) & (step == len(steps) - 1))
            def _():
                for cp in copies:
                    cp.wait()

    qmap = lambda p, s, qt, kt: (qt[s], p)
    kmap = lambda p, s, qt, kt: (kt[s], p)
    grid_spec = pltpu.PrefetchScalarGridSpec(
        num_scalar_prefetch=2, grid=(4, len(steps)),
        in_specs=[pl.BlockSpec((t, 256), qmap), pl.BlockSpec((t, 256), kmap), pl.BlockSpec((t, LANES), kmap),
                  pl.BlockSpec((t, LANES), qmap), pl.BlockSpec((t, LANES), qmap),
                  pl.BlockSpec((1, SUBLANES, t), lambda p, s, qt, kt: (p, 0, qt[s]))] + [ANY] * nx,
        out_specs=[pl.BlockSpec((SEQ, 256), lambda p, s, qt, kt: (0, p)), pl.BlockSpec((t, 256), kmap),
                   pl.BlockSpec((t, LANES), kmap)] + [ANY] * nx,
        scratch_shapes=[pltpu.VMEM((256, t), F32), pltpu.VMEM((LANES, t), F32)] + (_exchange_sems(nx) if nx else []))
    res = pl.pallas_call(
        body, name="flash_bwd", grid_spec=grid_spec,
        out_shape=[jax.ShapeDtypeStruct((SEQ, 1024), F32), jax.ShapeDtypeStruct((SEQ, 1024), F32),
                   jax.ShapeDtypeStruct((SEQ, 512), F32)] + _exchange_shapes(sc_arrays, []),
        compiler_params=_params(("arbitrary", "arbitrary")),
    )(qi_tab, ki_tab, q, k, v, o, do, lse, *sc_arrays)
    return res[0], res[1], res[2], res[3:]


def _rms_bwd(v, g, dy, eps=1e-6):
    rs = lax.rsqrt(jnp.mean(v * v, axis=-1, keepdims=True) + eps)
    xh = v * rs
    dxh = dy * g
    dv = rs * (dxh - xh * jnp.mean(dxh * xh, axis=-1, keepdims=True))
    return dv, jnp.sum(dy * xh, axis=0, keepdims=True)


def _mla_bwd(dq, dk, dv, proj0, qlat, klat, q_norm, kv_norm, wq, wkv, tc, tsa, tsb, xb):
    t = MLA_T

    def body(dq_ref, dk_ref, dv_ref, cq_ref, ck_ref, ql_ref, kl_ref, qn_ref, kn_ref, wq_ref, wkv_ref,
             c_ref, sa_ref, sb_ref, xb_ref, o_ref, gwq_ref, gwkv_ref, dgq_ref, dgk_ref, gwt_ref, oq_ref, okv_ref):
        @pl.when(pl.program_id(0) == 0)
        def _():
            dgq_ref[...] = jnp.zeros_like(dgq_ref)
            dgk_ref[...] = jnp.zeros_like(dgk_ref)
            gwq_ref[...] = jnp.zeros_like(gwq_ref)
            gwkv_ref[...] = jnp.zeros_like(gwkv_ref)
            gwt_ref[...] = jnp.zeros_like(gwt_ref)

        c, sa, sb = c_ref[...].T, sa_ref[...].T, sb_ref[...].T
        lane = lax.broadcasted_iota(jnp.int32, (t, LANES), 1)
        dkr = jnp.zeros((t, LANES), F32)
        for hd in range(MLA_HEADS):
            sl = slice(hd * LANES, (hd + 1) * LANES)
            oq_ref[:, sl] = _mx(_rope_t(dq_ref[:, sl], c, sa, sb))
            dkh = dk_ref[:, sl]
            okv_ref[:, sl] = _mx(dkh)
            dkr = dkr + dkh
        okv_ref[:, 1024:] = _mx(dv_ref[...])
        dkr = _rope_t(jnp.where((lane >= 64) & (lane < 96), dkr, 0.0), c, sa, sb)
        dqraw, dkvraw = oq_ref[...], okv_ref[...]
        gwq_ref[...] += _dot_tn(ql_ref[...], dqraw)
        gwkv_ref[...] += _dot_tn(kl_ref[...], dkvraw)
        dqn = _dot_nt(dqraw, wq_ref[...])
        dkn = _dot_nt(dkvraw, wkv_ref[...])
        dcq, dgq = _rms_bwd(cq_ref[...], qn_ref[...], dqn)
        dck, dgk = _rms_bwd(ck_ref[:, :LANES], kn_ref[...], dkn)
        o_ref[:, :256] = _mx(dcq)
        o_ref[:, 256:384] = _mx(dck)
        o_ref[:, 384:] = _mx(dkr)
        gwt_ref[...] += _dot_tn(xb_ref[...], o_ref[...])
        dgq_ref[0:1, :] += dgq
        dgk_ref[0:1, :] += dgk

    tab = pl.BlockSpec((t, LANES), lambda i: (i, 0))
    rot = pl.BlockSpec((LANES, t), lambda i: (0, i))
    wide = pl.BlockSpec((t, 1024), lambda i: (i, 0))
    const = lambda shape: pl.BlockSpec(shape, lambda i: (0, 0))
    return pl.pallas_call(
        body, name="mla_bwd", grid=(SEQ // t,),
        in_specs=[wide, wide, pl.BlockSpec((t, 512), lambda i: (i, 0)),
                  pl.BlockSpec((t, 256), lambda i: (i, 6)), pl.BlockSpec((t, 256), lambda i: (i, 7)),
                  pl.BlockSpec((t, 256), lambda i: (i, 0)), tab,
                  const((1, 256)), const((1, LANES)), const((256, 1024)), const((LANES, 1536)), rot, rot, rot, wide],
        out_specs=[pl.BlockSpec((t, 512), lambda i: (i, 0)), const((256, 1024)), const((LANES, 1536)),
                   const((SUBLANES, 256)), const((SUBLANES, LANES)), const((D_MODEL, 512))],
        out_shape=[jax.ShapeDtypeStruct((SEQ, 512), MXU_DTYPE), jax.ShapeDtypeStruct((256, 1024), F32),
                   jax.ShapeDtypeStruct((LANES, 1536), F32), jax.ShapeDtypeStruct((SUBLANES, 256), F32),
                   jax.ShapeDtypeStruct((SUBLANES, LANES), F32), jax.ShapeDtypeStruct((D_MODEL, 512), F32)],
        scratch_shapes=[pltpu.VMEM((t, 1024), MXU_DTYPE), pltpu.VMEM((t, 1536), MXU_DTYPE)],
        compiler_params=_params(("arbitrary",)),
    )(dq, dk, dv, proj0, proj0, qlat, klat, q_norm, kv_norm, wq, wkv, tc, tsa, tsb, xb)


LN_T = 512


def _ln(v, g, b, eps=1e-5):
    mu = jnp.mean(v, axis=-1, keepdims=True)
    xc = v - mu
    rs = lax.rsqrt(jnp.mean(xc * xc, axis=-1, keepdims=True) + eps)
    return xc * rs * g + b


def _ln_bwd(v, g, dy, eps=1e-5):
    mu = jnp.mean(v, axis=-1, keepdims=True)
    xc = v - mu
    rs = lax.rsqrt(jnp.mean(xc * xc, axis=-1, keepdims=True) + eps)
    xh = xc * rs
    dxh = dy * g
    dv = rs * (dxh - jnp.mean(dxh, axis=-1, keepdims=True) - xh * jnp.mean(dxh * xh, axis=-1, keepdims=True))
    return dv, jnp.sum(dy * xh, axis=0, keepdims=True), jnp.sum(dy, axis=0, keepdims=True)


def _l0_out(h, o, proj0, x, w_out, g, b):
    t = LN_T

    def body(h_ref, o_ref, ga_ref, gb_ref, x_ref, w_ref, g_ref, b_ref, y_ref, v_ref, x1_ref, x1b_ref):
        y = _mx(jnp.concatenate([h_ref[...] * _silu(ga_ref[...]), o_ref[...] * _silu(gb_ref[...])], axis=1))
        v = DN_ALPHA * x_ref[...] + _dot(y, w_ref[...])
        y_ref[...] = y
        v_ref[...] = v
        x1 = _ln(v, g_ref[...], b_ref[...])
        x1_ref[...] = x1
        x1b_ref[...] = _mx(x1)

    half = pl.BlockSpec((t, 512), lambda i: (i, 0))
    full = pl.BlockSpec((t, D_MODEL), lambda i: (i, 0))
    vec = pl.BlockSpec((1, D_MODEL), lambda i: (0, 0))
    return pl.pallas_call(
        body, name="l0_out", grid=(SEQ // t,),
        in_specs=[half, half, pl.BlockSpec((t, 512), lambda i: (i, 0)), pl.BlockSpec((t, 512), lambda i: (i, 1)), full,
                  pl.BlockSpec((D_MODEL, D_MODEL), lambda i: (0, 0)), vec, vec],
        out_specs=[full, full, full, full],
        out_shape=[jax.ShapeDtypeStruct((SEQ, D_MODEL), MXU_DTYPE), jax.ShapeDtypeStruct((SEQ, D_MODEL), F32),
                   jax.ShapeDtypeStruct((SEQ, D_MODEL), F32), jax.ShapeDtypeStruct((SEQ, D_MODEL), MXU_DTYPE)],
        compiler_params=_params(("parallel",)),
    )(h, o, proj0, proj0, x, w_out, g, b)


def _l1_in(x1b, w1z, w1d):
    t = 1024

    def body(x_ref, wz_ref, wd_ref, z_ref, dt_ref):
        xv = x_ref[...]
        z_ref[...] = _dot(xv, wz_ref[...])
        dt_ref[...] = _dot(xv, wd_ref[...])

    rows = lambda w: pl.BlockSpec((t, w), lambda i: (i, 0))
    const = lambda w: pl.BlockSpec((D_MODEL, w), lambda i: (0, 0))
    return pl.pallas_call(
        body, name="l1_in", grid=(SEQ // t,),
        in_specs=[rows(D_MODEL), const(SSD_INNER), const(LANES)],
        out_specs=[rows(SSD_INNER), rows(LANES)],
        out_shape=[jax.ShapeDtypeStruct((SEQ, SSD_INNER), F32), jax.ShapeDtypeStruct((SEQ, LANES), F32)],
        compiler_params=_params(("parallel",)),
    )(x1b, w1z, w1d)


def _l1_dx_ln(dz, dxbc, ddt, dv1, v0, w1z, w1x, w1d, g):
    t = LN_T

    def body(dz_ref, dx_ref, ddt_ref, dv1_ref, v_ref, wz_ref, wx_ref, wd_ref, g_ref, dv_ref, dgb_ref):
        @pl.when(pl.program_id(0) == 0)
        def _():
            dgb_ref[...] = jnp.zeros_like(dgb_ref)

        dy = (DN_ALPHA * dv1_ref[...] + _dot_nt(dz_ref[...], wz_ref[...]) + _dot_nt(dx_ref[...], wx_ref[...])
              + _dot_nt(_mx(ddt_ref[...]), wd_ref[...]))
        dv, dg, db = _ln_bwd(v_ref[...], g_ref[...], dy)
        dv_ref[...] = dv
        dgb_ref[0:1, :] += dg
        dgb_ref[1:2, :] += db

    rows = lambda w: pl.BlockSpec((t, w), lambda i: (i, 0))
    const = lambda w: pl.BlockSpec((D_MODEL, w), lambda i: (0, 0))
    return pl.pallas_call(
        body, name="l1_dx_ln", grid=(SEQ // t,),
        in_specs=[rows(SSD_INNER), rows(SSD_CONV), rows(LANES), rows(D_MODEL), rows(D_MODEL),
                  const(SSD_INNER), const(SSD_CONV), const(LANES), pl.BlockSpec((1, D_MODEL), lambda i: (0, 0))],
        out_specs=[rows(D_MODEL), pl.BlockSpec((SUBLANES, D_MODEL), lambda i: (0, 0))],
        out_shape=[jax.ShapeDtypeStruct((SEQ, D_MODEL), F32), jax.ShapeDtypeStruct((SUBLANES, D_MODEL), F32)],
        compiler_params=_params(("arbitrary",)),
    )(dz, dxbc, ddt, dv1, v0, w1z, w1x, w1d, g)


def _gate_bwd(dv0, w_out, h, o, proj0, y0, xb):
    t = LN_T

    def body(dv_ref, w_ref, h_ref, o_ref, ga_ref, gb_ref, y0_ref, xb_ref, dh_ref, do_ref, dg_ref, gwo_ref, gwg_ref):
        @pl.when(pl.program_id(0) == 0)
        def _():
            gwo_ref[...] = jnp.zeros_like(gwo_ref)
            gwg_ref[...] = jnp.zeros_like(gwg_ref)

        dvb = _mx(dv_ref[...])
        dy = _dot_nt(dvb, w_ref[...])
        ga, gb, dya, dyb = ga_ref[...], gb_ref[...], dy[:, :512], dy[:, 512:]
        dh_ref[...] = dya * _silu(ga)
        do_ref[...] = dyb * _silu(gb)
        dg_ref[:, :512] = _mx(dya * h_ref[...] * _dsilu(ga))
        dg_ref[:, 512:] = _mx(dyb * o_ref[...] * _dsilu(gb))
        gwo_ref[...] += _dot_tn(y0_ref[...], dvb)
        gwg_ref[...] += _dot_tn(xb_ref[...], dg_ref[...])

    half = pl.BlockSpec((t, 512), lambda i: (i, 0))
    half1 = pl.BlockSpec((t, 512), lambda i: (i, 1))
    full = pl.BlockSpec((t, 1024), lambda i: (i, 0))
    square = pl.BlockSpec((D_MODEL, D_MODEL), lambda i: (0, 0))
    return pl.pallas_call(
        body, name="gate_bwd", grid=(SEQ // t,),
        in_specs=[full, square, half, half, half, half1, full, full],
        out_specs=[half, half, full, square, square],
        out_shape=[jax.ShapeDtypeStruct((SEQ, 512), F32), jax.ShapeDtypeStruct((SEQ, 512), F32),
                   jax.ShapeDtypeStruct((SEQ, 1024), MXU_DTYPE), jax.ShapeDtypeStruct((D_MODEL, D_MODEL), F32),
                   jax.ShapeDtypeStruct((D_MODEL, D_MODEL), F32)],
        compiler_params=_params(("arbitrary",)),
    )(dv0, w_out, h, o, proj0, proj0, y0, xb)


CONV_T = 1024
CONV_CB = 1024


def _ssd_conv_fwd(x1b, w1x, cw8, cb):
    t, cbk = CONV_T, CONV_CB

    def body(x_ref, w_ref, cw_ref, cb_ref, xbc_ref, pre_ref, act_ref, carry):
        xbc = _dot(x_ref[...], w_ref[...])
        halo = jnp.where(pl.program_id(1) > 0, carry[...], 0.0)
        pre = _conv4(xbc, halo, cw_ref[...], cb_ref[...])
        carry[...] = xbc[t - SUBLANES:]
        xbc_ref[...] = xbc
        pre_ref[...] = pre
        act_ref[...] = _silu(pre)

    blk = pl.BlockSpec((t, cbk), lambda j, i: (i, j))
    out = jax.ShapeDtypeStruct((SEQ, SSD_CONV), F32)
    return pl.pallas_call(
        body, name="ssd_conv_fwd", grid=(SSD_CONV // cbk, SEQ // t),
        in_specs=[pl.BlockSpec((t, D_MODEL), lambda j, i: (i, 0)), pl.BlockSpec((D_MODEL, cbk), lambda j, i: (0, j)),
                  pl.BlockSpec((SUBLANES, cbk), lambda j, i: (0, j)), pl.BlockSpec((1, cbk), lambda j, i: (0, j))],
        out_specs=[blk, blk, blk], out_shape=[out, out, out],
        scratch_shapes=[pltpu.VMEM((SUBLANES, cbk), F32)],
        compiler_params=_params(("parallel", "arbitrary")),
    )(x1b, w1x, cw8, cb)


def _ssd_conv_bwd(dact, pre, xbc, cw8, x1b):
    t, cbk = CONV_T, CONV_CB
    tb = t // SUBLANES
    nb = SEQ // t

    def body(da_ref, dan_ref, pre_ref, pren_ref, x_ref, cw_ref, x1_ref, dx_ref, dcw_ref, gw_ref):
        i = pl.program_id(1)

        @pl.when(i == 0)
        def _():
            dcw_ref[...] = jnp.zeros_like(dcw_ref)
            gw_ref[...] = jnp.zeros_like(gw_ref)

        dpre = da_ref[...] * _dsilu(pre_ref[...])
        dpre_next = jnp.where(i < nb - 1, dan_ref[...] * _dsilu(pren_ref[...]), 0.0)
        xblk = x_ref[...]
        cw = cw_ref[...]
        dx = dpre * cw[3:4]
        dcw_ref[3:4, :] += jnp.sum(dpre * xblk, axis=0, keepdims=True)
        for k in range(3):
            up = _shift_up(dpre, dpre_next, 3 - k)
            dcw_ref[k:k + 1, :] += jnp.sum(up * xblk, axis=0, keepdims=True)
            dx = dx + up * cw[k:k + 1]
        dcw_ref[4:5, :] += jnp.sum(dpre, axis=0, keepdims=True)
        dxb = _mx(dx)
        dx_ref[...] = dxb
        gw_ref[...] += _dot_tn(x1_ref[...], dxb)

    blk = pl.BlockSpec((t, cbk), lambda j, i: (i, j))
    nxt = pl.BlockSpec((SUBLANES, cbk), lambda j, i: (jnp.minimum((i + 1) * tb, SEQ // SUBLANES - 1), j))
    acc = pl.BlockSpec((SUBLANES, cbk), lambda j, i: (0, j))
    return pl.pallas_call(
        body, name="ssd_conv_bwd", grid=(SSD_CONV // cbk, nb),
        in_specs=[blk, nxt, blk, nxt, blk, acc, pl.BlockSpec((t, D_MODEL), lambda j, i: (i, 0))],
        out_specs=[blk, acc, pl.BlockSpec((D_MODEL, cbk), lambda j, i: (0, j))],
        out_shape=[jax.ShapeDtypeStruct((SEQ, SSD_CONV), MXU_DTYPE), jax.ShapeDtypeStruct((SUBLANES, SSD_CONV), F32),
                   jax.ShapeDtypeStruct((D_MODEL, SSD_CONV), F32)],
        compiler_params=_params(("parallel", "arbitrary")),
    )(dact, dact, pre, pre, xbc, cw8, x1b)


def _ssd_common(dt_raw, bias, alog, tril, expand_t, xs):
    lane = lax.broadcasted_iota(jnp.int32, dt_raw.shape, 1)
    dt = jnp.where(lane < SSD_HEADS, _softplus(dt_raw + bias), 0.0)
    a_neg = -jnp.exp(alog)
    cs = _dot_hi(tril, dt * a_neg)
    dt_x = _expand_heads(dt, expand_t)
    ecs_x = _expand_heads(jnp.exp(cs), expand_t)
    ds_x = _expand_heads(jnp.exp(cs[SSD_L - 1:SSD_L, :] - cs), expand_t)
    return dt, a_neg, cs, dt_x, None, xs * dt_x, ds_x, ecs_x, ecs_x[SSD_L - 1:SSD_L, :]


def _expand_heads(v, expand_t):
    hi = v.astype(jnp.bfloat16)
    lo = (v - hi.astype(F32)).astype(jnp.bfloat16)
    return _dot_nt(hi, expand_t) + _dot_nt(lo, expand_t)


def _fold_heads(v, expand_t):
    hi = v.astype(jnp.bfloat16)
    lo = (v - hi.astype(F32)).astype(jnp.bfloat16)
    return _dot(hi, expand_t) + _dot(lo, expand_t)


def _ssd_decay(cs, cs_t, hh, causal):
    seg = cs[:, hh:hh + 1] - cs_t[hh:hh + 1, :]
    return jnp.where(causal, jnp.exp(jnp.where(causal, seg, 0.0)), 0.0)


def _ssd_scan_fwd(act, dt_raw, bias, alog, d_x, tril, expand_t, bcast=()):
    nc = SEQ // SSD_L
    gw = SSD_INNER // SSD_GROUPS
    n = len(bcast)

    def body(act_ref, dt_ref, bias_ref, alog_ref, dx_ref, tril_ref, et_ref, *rest):
        y_ref, hp_ref, h_sc = rest[n], rest[n + 1], rest[2 * n + 2]
        if n:
            copies = _peer_copies(rest[:n], rest[n + 2:2 * n + 2], rest[2 * n + 3:], [])

            @pl.when(pl.program_id(0) == 0)
            def _():
                for cp in copies:
                    cp.start()

            @pl.when(pl.program_id(0) == nc - 1)
            def _():
                for cp in copies:
                    cp.wait()

        @pl.when(pl.program_id(0) == 0)
        def _():
            h_sc[...] = jnp.zeros_like(h_sc)

        xs = act_ref[:, :SSD_INNER]
        _, _, cs, _, _, xdt, ds_x, ecs_x, elast = _ssd_common(
            dt_ref[...], bias_ref[...], alog_ref[...], tril_ref[...], et_ref[...], xs)
        cs_t = cs.T
        causal = (lax.broadcasted_iota(jnp.int32, (SSD_L, SSD_L), 0)
                  >= lax.broadcasted_iota(jnp.int32, (SSD_L, SSD_L), 1))
        lane = lax.broadcasted_iota(jnp.int32, (SSD_L, LANES), 1)
        xdt_b = _mx(xdt)
        xds_b = _mx(xdt * ds_x)
        hp_ref[0] = h_sc[...]
        for g in range(SSD_GROUPS):
            gs = slice(g * gw, (g + 1) * gw)
            bg = _mx(act_ref[:, SSD_INNER + g * SSD_N:SSD_INNER + (g + 1) * SSD_N])
            cg = _mx(act_ref[:, SSD_INNER + 512 + g * SSD_N:SSD_INNER + 512 + (g + 1) * SSD_N])
            cb = _dot_nt(cg, bg)
            hprev = h_sc[:, gs]
            yoff = _dot(cg, _mx(hprev)) * ecs_x[:, gs]
            h_sc[:, gs] = hprev * elast[:, gs] + _dot_tn(bg, xds_b[:, gs])
            for pr in range(4):
                ps = slice(g * gw + pr * LANES, g * gw + (pr + 1) * LANES)
                xp = xdt_b[:, ps]
                ydiag = jnp.zeros((SSD_L, LANES), F32)
                for j in range(2):
                    dm = _ssd_decay(cs, cs_t, g * 8 + pr * 2 + j, causal)
                    mine = (lane >= j * 64) & (lane < (j + 1) * 64)
                    ydiag = ydiag + _dot(_mx(cb * dm), jnp.where(mine, xp, jnp.zeros_like(xp)))
                y_ref[:, ps] = ydiag + yoff[:, pr * LANES:(pr + 1) * LANES] + dx_ref[:, ps] * xs[:, ps]

    const = lambda shape: pl.BlockSpec(shape, lambda c: (0, 0))
    res = pl.pallas_call(
        body, name="ssd_scan_fwd", grid=(nc,),
        in_specs=[pl.BlockSpec((SSD_L, SSD_CONV), lambda c: (c, 0)), pl.BlockSpec((SSD_L, LANES), lambda c: (c, 0)),
                  const((1, LANES)), const((1, LANES)), const((1, SSD_INNER)), const((SSD_L, SSD_L)),
                  const((SSD_INNER, LANES))] + [ANY] * n,
        out_specs=[pl.BlockSpec((SSD_L, SSD_INNER), lambda c: (c, 0)),
                   pl.BlockSpec((1, SSD_N, SSD_INNER), lambda c: (c, 0, 0))] + [ANY] * n,
        out_shape=[jax.ShapeDtypeStruct((SEQ, SSD_INNER), F32), jax.ShapeDtypeStruct((nc, SSD_N, SSD_INNER), F32)]
        + _exchange_shapes([], bcast),
        scratch_shapes=[pltpu.VMEM((SSD_N, SSD_INNER), F32)] + (_exchange_sems(n) if n else []),
        compiler_params=_params(("arbitrary",)),
    )(act, dt_raw, bias, alog, d_x, tril, expand_t, *bcast)
    return res[0], res[1], res[2:]


def _ssd_scan_bwd(dy, act, dt_raw, hprev_all, bias, alog, d_x, tril, expand_t, x1b, scatter=()):
    nc = SEQ // SSD_L
    gw = SSD_INNER // SSD_GROUPS
    sc_arrays, sc_ranges = _scatter_args(scatter)
    nx = len(sc_arrays)

    def body(dy_ref, act_ref, dt_ref, hp_ref, bias_ref, alog_ref, dx_ref, tril_ref, et_ref, x1_ref, *rest):
        dact_ref, ddt_ref, dvec_ref, gdt_ref = rest[nx:nx + 4]
        dh_sc, dd_sc, dcs_sc, dcst_sc = rest[2 * nx + 4:2 * nx + 8]
        i = pl.program_id(0)
        if nx:
            copies = _peer_copies(rest[:nx], rest[nx + 4:2 * nx + 4], rest[2 * nx + 8:], sc_ranges)

            @pl.when(i == 0)
            def _():
                for cp in copies:
                    cp.start()

        @pl.when(i == 0)
        def _():
            dh_sc[...] = jnp.zeros_like(dh_sc)
            dd_sc[...] = jnp.zeros_like(dd_sc)
            gdt_ref[...] = jnp.zeros_like(gdt_ref)
            dvec_ref[...] = jnp.zeros_like(dvec_ref)

        xs = act_ref[:, :SSD_INNER]
        dt_raw_v, bias_v = dt_ref[...], bias_ref[...]
        dt, a_neg, cs, dt_x, _, xdt, ds_x, ecs_x, elast = _ssd_common(
            dt_raw_v, bias_v, alog_ref[...], tril_ref[...], et_ref[...], xs)
        cs_t = cs.T
        rowi = lax.broadcasted_iota(jnp.int32, (SSD_L, SSD_L), 0)
        coli = lax.broadcasted_iota(jnp.int32, (SSD_L, SSD_L), 1)
        causal = rowi >= coli
        lane = lax.broadcasted_iota(jnp.int32, (SSD_L, LANES), 1)
        row_g = lax.broadcasted_iota(jnp.int32, (SSD_L, gw), 0)
        dyv = dy_ref[...]
        dd_sc[0:1, :] += jnp.sum(dyv * xs, axis=0, keepdims=True)
        xdt_b = _mx(xdt)
        xds = xdt * ds_x
        xds_b = _mx(xds)
        dy_b = _mx(dyv)
        dye_b = _mx(dyv * ecs_x)
        dcs_sc[...] = jnp.zeros_like(dcs_sc)
        dcst_sc[...] = jnp.zeros_like(dcst_sc)
        dcs_parts = []
        dxdt_parts = []
        for g in range(SSD_GROUPS):
            gs = slice(g * gw, (g + 1) * gw)
            bcol = slice(SSD_INNER + g * SSD_N, SSD_INNER + (g + 1) * SSD_N)
            ccol = slice(SSD_INNER + 512 + g * SSD_N, SSD_INNER + 512 + (g + 1) * SSD_N)
            bg, cg = _mx(act_ref[:, bcol]), _mx(act_ref[:, ccol])
            cb = _dot_nt(cg, bg)
            hp = hp_ref[0, :, gs]
            hp_b = _mx(hp)
            dh = dh_sc[:, gs]
            dh_b = _mx(dh)
            yoff = _dot(cg, hp_b) * ecs_x[:, gs]
            bdh = _dot(bg, dh_b)
            tt = xds[:, gs] * bdh
            last_row = (jnp.sum(tt, axis=0, keepdims=True)
                        + jnp.sum(dh * hp, axis=0, keepdims=True) * elast[:, gs])
            dcs_parts.append(dyv[:, gs] * yoff - tt + jnp.where(row_g == SSD_L - 1, last_row, 0.0))
            dc_g = _dot_nt(dye_b[:, gs], hp_b)
            db_g = _dot_nt(xds_b[:, gs], dh_b)
            dh_sc[:, gs] = _dot_tn(cg, dye_b[:, gs]) + dh * elast[:, gs]
            wsum = jnp.zeros((SSD_L, SSD_L), F32)
            dxdt_g = []
            for pr in range(4):
                ps = slice(g * gw + pr * LANES, g * gw + (pr + 1) * LANES)
                xp, dyp = xdt_b[:, ps], dy_b[:, ps]
                dxp = jnp.zeros((SSD_L, LANES), F32)
                for j in range(2):
                    hh = g * 8 + pr * 2 + j
                    dm = _ssd_decay(cs, cs_t, hh, causal)
                    mine = (lane >= j * 64) & (lane < (j + 1) * 64)
                    dy_h = jnp.where(mine, dyp, jnp.zeros_like(dyp))
                    wd = _dot_nt(dy_h, xp) * dm
                    wsum = wsum + wd
                    gmat = wd * cb
                    dcs_sc[:, hh:hh + 1] = jnp.sum(gmat, axis=1, keepdims=True)
                    dcst_sc[hh:hh + 1, :] = -jnp.sum(gmat, axis=0, keepdims=True)
                    dxp = dxp + _dot_tn(_mx(cb * dm), dy_h)
                dxdt_g.append(dxp)
            dxdt_parts.append(jnp.concatenate(dxdt_g, axis=1) + bdh * ds_x[:, gs])
            ws_b = _mx(wsum)
            dact_ref[:, ccol] = dc_g + _dot(ws_b, bg)
            dact_ref[:, bcol] = db_g + _dot_tn(ws_b, cg)
        dxdt = jnp.concatenate(dxdt_parts, axis=1)
        dcs_x = jnp.concatenate(dcs_parts, axis=1)
        et = et_ref[...]
        dcs_tot = dcs_sc[...] + dcst_sc[...].T + _fold_heads(dcs_x, et)
        da_dt = _dot_hi((coli >= rowi).astype(F32), dcs_tot)
        ddt = da_dt * a_neg + _fold_heads(dxdt * xs, et)
        ddt_raw = ddt * _sigmoid(dt_raw_v + bias_v)
        ddt_ref[...] = ddt_raw
        gdt_ref[...] += _dot_tn(x1_ref[...], _mx(ddt_raw))
        dvec_ref[0:1, :] += jnp.sum(ddt_raw, axis=0, keepdims=True)
        dvec_ref[1:2, :] += jnp.sum(da_dt * dt, axis=0, keepdims=True) * a_neg
        dact_ref[:, :SSD_INNER] = dyv * dx_ref[...] + dxdt * dt_x

        @pl.when(i == nc - 1)
        def _():
            dvec_ref[2:3, :] = _fold_heads(dd_sc[...], et)[0:1, :]
            if nx:
                for cp in copies:
                    cp.wait()

    const = lambda shape: pl.BlockSpec(shape, lambda c: (0, 0))
    rev = lambda c: (nc - 1 - c, 0)
    res = pl.pallas_call(
        body, name="ssd_scan_bwd", grid=(nc,),
        in_specs=[pl.BlockSpec((SSD_L, SSD_INNER), rev), pl.BlockSpec((SSD_L, SSD_CONV), rev),
                  pl.BlockSpec((SSD_L, LANES), rev),
                  pl.BlockSpec((1, SSD_N, SSD_INNER), lambda c: (nc - 1 - c, 0, 0)),
                  const((1, LANES)), const((1, LANES)), const((1, SSD_INNER)), const((SSD_L, SSD_L)),
                  const((SSD_INNER, LANES)), pl.BlockSpec((SSD_L, D_MODEL), rev)] + [ANY] * nx,
        out_specs=[pl.BlockSpec((SSD_L, SSD_CONV), rev), pl.BlockSpec((SSD_L, LANES), rev), const((SUBLANES, LANES)),
                   const((D_MODEL, LANES))] + [ANY] * nx,
        out_shape=[jax.ShapeDtypeStruct((SEQ, SSD_CONV), F32), jax.ShapeDtypeStruct((SEQ, LANES), F32),
                   jax.ShapeDtypeStruct((SUBLANES, LANES), F32), jax.ShapeDtypeStruct((D_MODEL, LANES), F32)]
        + _exchange_shapes(sc_arrays, []),
        scratch_shapes=[pltpu.VMEM((SSD_N, SSD_INNER), F32), pltpu.VMEM((SUBLANES, SSD_INNER), F32),
                        pltpu.VMEM((SSD_L, LANES), F32), pltpu.VMEM((LANES, SSD_L), F32)]
        + (_exchange_sems(nx) if nx else []),
        compiler_params=_params(("arbitrary",)),
    )(dy, act, dt_raw, hprev_all, bias, alog, d_x, tril, expand_t, x1b, *sc_arrays)
    return res[0], res[1], res[2], res[3], res[4:]


L1_T = 512


def _resident(shape):
    return pl.BlockSpec(shape, lambda i: (0, 0), pipeline_mode=pl.Buffered(1))


def _gated_norm(y, z, nw):
    y2 = y * _silu(z)
    gw = SSD_INNER // SSD_GROUPS
    outs, xhs, rss = [], [], []
    for g in range(SSD_GROUPS):
        gs = slice(g * gw, (g + 1) * gw)
        v = y2[:, gs]
        rs = lax.rsqrt(jnp.mean(v * v, axis=-1, keepdims=True) + 1e-6)
        xhs.append(v * rs)
        rss.append(rs)
        outs.append(v * rs * nw[:, gs])
    return outs, xhs, rss


def _l1_out(y, z, nw, w_out, x1, g, b, target):
    t = L1_T

    def body(y_ref, z_ref, nw_ref, w_ref, x1_ref, g_ref, b_ref, tg_ref, dv_ref, dgb_ref, loss_ref, gw_ref, gw_sc):
        @pl.when(pl.program_id(0) == 0)
        def _():
            dgb_ref[...] = jnp.zeros_like(dgb_ref)
            loss_ref[...] = jnp.zeros_like(loss_ref)
            gw_sc[...] = jnp.zeros_like(gw_sc)

        outs, _, _ = _gated_norm(y_ref[...], z_ref[...], nw_ref[...])
        yn = _mx(jnp.concatenate(outs, axis=1))
        v = DN_ALPHA * x1_ref[...] + _dot(yn, w_ref[...])
        gv = g_ref[...]
        err = _ln(v, gv, b_ref[...]) - tg_ref[...]
        rowsum = jnp.sum(err * err, axis=1, keepdims=True)
        loss_ref[...] += 0.5 * jnp.sum(rowsum, axis=0, keepdims=True) / D_MODEL
        dv, dg, db = _ln_bwd(v, gv, err / D_MODEL)
        dv_ref[...] = dv
        dgb_ref[0:1, :] += dg
        dgb_ref[1:2, :] += db
        gw_sc[...] += _dot_tn(yn, _mx(dv))

        @pl.when(pl.program_id(0) == SEQ // t - 1)
        def _():
            gw_ref[...] = _mx(gw_sc[...])

    wide = pl.BlockSpec((t, SSD_INNER), lambda i: (i, 0))
    full = pl.BlockSpec((t, D_MODEL), lambda i: (i, 0))
    vec = pl.BlockSpec((1, D_MODEL), lambda i: (0, 0))
    return pl.pallas_call(
        body, name="l1_out", grid=(SEQ // t,),
        in_specs=[wide, wide, pl.BlockSpec((1, SSD_INNER), lambda i: (0, 0)),
                  _resident((SSD_INNER, D_MODEL)), full, vec, vec, full],
        out_specs=[full, pl.BlockSpec((SUBLANES, D_MODEL), lambda i: (0, 0)),
                   pl.BlockSpec((SUBLANES, LANES), lambda i: (0, 0)), _resident((SSD_INNER, D_MODEL))],
        out_shape=[jax.ShapeDtypeStruct((SEQ, D_MODEL), F32), jax.ShapeDtypeStruct((SUBLANES, D_MODEL), F32),
                   jax.ShapeDtypeStruct((SUBLANES, LANES), F32), jax.ShapeDtypeStruct((SSD_INNER, D_MODEL), MXU_DTYPE)],
        scratch_shapes=[pltpu.VMEM((SSD_INNER, D_MODEL), F32)],
        compiler_params=_params(("arbitrary",)),
    )(y, z, nw, w_out, x1, g, b, target)


def _l1_gate_bwd(dv1, w_out, y, z, nw, x1b):
    t = L1_T
    gw = SSD_INNER // SSD_GROUPS

    def body(dv_ref, w_ref, y_ref, z_ref, nw_ref, x1_ref, dy_ref, dz_ref, dnw_ref, gw_ref):
        @pl.when(pl.program_id(0) == 0)
        def _():
            dnw_ref[...] = jnp.zeros_like(dnw_ref)
            gw_ref[...] = jnp.zeros_like(gw_ref)

        dyn = _dot_nt(_mx(dv_ref[...]), w_ref[...])
        yv, zv, nwv = y_ref[...], z_ref[...], nw_ref[...]
        _, xhs, rss = _gated_norm(yv, zv, nwv)
        sz, dsz = _silu(zv), _dsilu(zv)
        for g in range(SSD_GROUPS):
            gs = slice(g * gw, (g + 1) * gw)
            d_out = dyn[:, gs]
            xh = xhs[g]
            dnw_ref[0:1, gs] += jnp.sum(d_out * xh, axis=0, keepdims=True)
            dxh = d_out * nwv[:, gs]
            dy2 = rss[g] * (dxh - xh * jnp.mean(dxh * xh, axis=-1, keepdims=True))
            dy_ref[:, gs] = dy2 * sz[:, gs]
            dz_ref[:, gs] = _mx(dy2 * yv[:, gs] * dsz[:, gs])
        gw_ref[...] += _dot_tn(x1_ref[...], dz_ref[...])

    wide = pl.BlockSpec((t, SSD_INNER), lambda i: (i, 0))
    return pl.pallas_call(
        body, name="l1_gate_bwd", grid=(SEQ // t,),
        in_specs=[pl.BlockSpec((t, D_MODEL), lambda i: (i, 0)), _resident((SSD_INNER, D_MODEL)),
                  wide, wide, pl.BlockSpec((1, SSD_INNER), lambda i: (0, 0)), pl.BlockSpec((t, D_MODEL), lambda i: (i, 0))],
        out_specs=[wide, wide, pl.BlockSpec((SUBLANES, SSD_INNER), lambda i: (0, 0)),
                   _resident((D_MODEL, SSD_INNER))],
        out_shape=[jax.ShapeDtypeStruct((SEQ, SSD_INNER), F32), jax.ShapeDtypeStruct((SEQ, SSD_INNER), MXU_DTYPE),
                   jax.ShapeDtypeStruct((SUBLANES, SSD_INNER), F32), jax.ShapeDtypeStruct((D_MODEL, SSD_INNER), F32)],
        compiler_params=_params(("arbitrary",)),
    )(dv1, w_out, y, z, nw, x1b)


MESH = pl.DeviceIdType.MESH
ANY = pl.BlockSpec(memory_space=pl.ANY)


def _flip(v, bit):
    return 1 - v if bit else v


def _all_gather(blocks, name):
    n = len(blocks)

    def body(*refs):
        x_refs, out_refs = refs[:n], refs[n:2 * n]
        send_sems, recv_sems, local_sems = refs[2 * n:]
        mx, my, mc = lax.axis_index("x"), lax.axis_index("y"), lax.axis_index("c")
        me, sibling = (mx, my, mc), (mx, my, 1 - mc)
        chips = [(1 - mx, my), (mx, 1 - my), (1 - mx, 1 - my)]

        def copy(a, k, block, to, own=False):
            px, py, pc = block
            slot = out_refs[a].at[4 * px + 2 * py + pc]
            return pltpu.make_async_remote_copy(
                src_ref=x_refs[a] if own else slot, dst_ref=slot,
                send_sem=send_sems.at[7 * a + k], recv_sem=recv_sems.at[7 * a + k], device_id=to, device_id_type=MESH)

        mine = [pltpu.make_async_copy(x_refs[a], out_refs[a].at[4 * mx + 2 * my + mc], local_sems.at[a])
                for a in range(n)]
        first = []
        for a in range(n):
            mine[a].start()
            first.append(copy(a, 0, me, sibling, own=True))
            first += [copy(a, 1 + j, me, (*chip, mc), own=True) for j, chip in enumerate(chips)]
        for cp in first:
            cp.start()
        passed = []
        for j, chip in enumerate(chips):
            for a in range(n):
                copy(a, 1 + j, (*chip, mc), me).wait_recv()
                fwd = copy(a, 4 + j, (*chip, mc), sibling)
                fwd.start()
                passed.append(fwd)
        for a in range(n):
            copy(a, 0, sibling, me).wait_recv()
            for j, chip in enumerate(chips):
                copy(a, 4 + j, (*chip, 1 - mc), me).wait_recv()
        for cp in first + passed:
            cp.wait_send()
        for cp in mine:
            cp.wait()

    return pl.pallas_call(
        body, name=name, in_specs=[ANY] * n, out_specs=[ANY] * n,
        out_shape=[jax.ShapeDtypeStruct((N_DEV,) + b.shape, b.dtype) for b in blocks],
        scratch_shapes=[pltpu.SemaphoreType.DMA((7 * n,)), pltpu.SemaphoreType.DMA((7 * n,)),
                        pltpu.SemaphoreType.DMA((n,))],
    )(*blocks)


def _l0_in(x, w0p, bcast=()):
    n = len(bcast)
    tm, tn = 1024, 1024
    gi, gj = SEQ // tm, 2048 // tn

    def body(x_ref, w_ref, *rest):
        o_ref, xb_ref = rest[n], rest[n + 1]
        i, j = pl.program_id(0), pl.program_id(1)
        if n:
            copies = _peer_copies(rest[:n], rest[n + 2:2 * n + 2], rest[2 * n + 2:], [])

            @pl.when((i == 0) & (j == 0))
            def _():
                for cp in copies:
                    cp.start()

        xb = _mx(x_ref[...])
        xb_ref[...] = xb
        o_ref[...] = _dot(xb, w_ref[...])

        if n:
            @pl.when((i == gi - 1) & (j == gj - 1))
            def _():
                for cp in copies:
                    cp.wait()

    res = pl.pallas_call(
        body, name="l0_in", grid=(gi, gj),
        in_specs=[pl.BlockSpec((tm, D_MODEL), lambda i, j: (i, 0)), pl.BlockSpec((D_MODEL, tn), lambda i, j: (0, j))]
        + [ANY] * n,
        out_specs=[pl.BlockSpec((tm, tn), lambda i, j: (i, j)), pl.BlockSpec((tm, D_MODEL), lambda i, j: (i, 0))]
        + [ANY] * n,
        out_shape=[jax.ShapeDtypeStruct((SEQ, 2048), F32), jax.ShapeDtypeStruct((SEQ, D_MODEL), MXU_DTYPE)]
        + _exchange_shapes([], bcast),
        scratch_shapes=_exchange_sems(n) if n else [],
        compiler_params=_params(("arbitrary", "arbitrary")),
    )(x, w0p, *bcast)
    return res[0], res[1], res[2:]


def _l0_dx(dxr, dgate, dtail, w0p, dv0, scatter=(), bcast=(), into=None):
    arrays, ranges = _scatter_args(scatter)
    n = len(arrays) + len(bcast)
    k = 0 if into is None else 1
    tm = 1024
    steps = SEQ // tm

    def body(dxr_ref, dg_ref, dt_ref, w_ref, dv_ref, *rest):
        o_ref = rest[n + k]
        i = pl.program_id(0)
        if n:
            copies = _peer_copies(rest[:n], rest[n + k + 1:2 * n + k + 1], rest[2 * n + k + 1:], ranges)

            @pl.when(i == 0)
            def _():
                for cp in copies:
                    cp.start()

        o_ref[...] = (DN_ALPHA * dv_ref[...] + _dot_nt(dg_ref[...], w_ref[:, 0:1024])
                      + _dot_nt(dxr_ref[...], w_ref[:, 1024:1536]) + _dot_nt(dt_ref[...], w_ref[:, 1536:2048]))

        if n:
            @pl.when(i == steps - 1)
            def _():
                for cp in copies:
                    cp.wait()

    rows = lambda w: pl.BlockSpec((tm, w), lambda i: (i, 0))
    res = pl.pallas_call(
        body, name="l0_dx", grid=(steps,),
        in_specs=[rows(512), rows(1024), rows(512), pl.BlockSpec((D_MODEL, 2048), lambda i: (0, 0)), rows(D_MODEL)]
        + [ANY] * (n + k),
        out_specs=[rows(D_MODEL)] + [ANY] * n,
        out_shape=[jax.ShapeDtypeStruct((SEQ, D_MODEL), F32)] + _exchange_shapes(arrays, bcast),
        scratch_shapes=_exchange_sems(n) if n else [],
        input_output_aliases={5 + n: 1} if k else {},
        compiler_params=_params(("arbitrary",)),
    )(dxr, dgate, dtail, w0p, dv0, *arrays, *bcast, *([into] if k else []))
    return res[0], res[1:]


def _scatter_args(scatter):
    arrays = [s[0] if isinstance(s, tuple) else s for s in scatter]
    ranges = [s[1] if isinstance(s, tuple) else (0, N_DEV) for s in scatter]
    return arrays, ranges


def _exchange_shapes(scatter, bcast):
    return ([jax.ShapeDtypeStruct((N_DEV,) + a.shape[1:], a.dtype) for a in scatter]
            + [jax.ShapeDtypeStruct((N_DEV,) + a.shape, a.dtype) for a in bcast])


def _exchange_sems(n):
    return [pltpu.SemaphoreType.DMA((7 * n,)), pltpu.SemaphoreType.DMA((7 * n,)), pltpu.SemaphoreType.DMA((n,))]


class _GuardedCopy:
    def __init__(self, copy, send=None, recv=None, local=False):
        self.copy, self.send, self.recv, self.local = copy, send, recv, local

    @staticmethod
    def _run(pred, fn):
        if pred is None:
            fn()
        else:
            pl.when(pred)(fn)

    def start(self):
        self._run(self.send, self.copy.start)

    def wait(self):
        if self.local:
            self._run(self.send, self.copy.wait)
        else:
            self._run(self.send, self.copy.wait_send)
            self._run(self.recv, self.copy.wait_recv)


def _peer_copies(in_refs, out_refs, sems, ranges):
    send_sems, recv_sems, local_sems = sems
    n, ns = len(in_refs), len(ranges)
    mx, my, mc = lax.axis_index("x"), lax.axis_index("y"), lax.axis_index("c")
    me = 4 * mx + 2 * my + mc

    def src(a, slot):
        return in_refs[a].at[slot - ranges[a][0]] if a < ns else in_refs[a]

    def member(a, dev):
        if a >= ns or ranges[a] == (0, N_DEV):
            return None
        return (dev >= ranges[a][0]) & (dev < ranges[a][1])

    copies = [_GuardedCopy(pltpu.make_async_copy(src(a, me), out_refs[a].at[me], local_sems.at[a]),
                           send=member(a, me), local=True) for a in range(n)]
    for k in range(1, N_DEV):
        px, py, pc = _flip(mx, (k >> 2) & 1), _flip(my, (k >> 1) & 1), _flip(mc, k & 1)
        peer = 4 * px + 2 * py + pc
        for a in range(n):
            copies.append(_GuardedCopy(pltpu.make_async_remote_copy(
                src_ref=src(a, peer), dst_ref=out_refs[a].at[me],
                send_sem=send_sems.at[7 * a + k - 1], recv_sem=recv_sems.at[7 * a + k - 1],
                device_id=(px, py, pc), device_id_type=MESH), send=member(a, peer), recv=member(a, me)))
    return copies


def _segments(col_map, width):
    segs = []
    for lo, hi, arr, alo in col_map:
        for s in range(N_DEV):
            a, b = max(lo, s * width), min(hi, (s + 1) * width)
            if a < b:
                segs.append((s, a - s * width, b - a, arr, alo + a - lo))
    return segs


COPY_ROWS = 256


def _unshard(g8, col_map, widths, name):
    _, r, w = g8.shape
    rb = min(r, COPY_ROWS)
    segs = _segments(col_map, w)

    def body(g_ref, *o_refs):
        for o_ref in o_refs:
            o_ref[...] = jnp.zeros_like(o_ref)
        for s, llo, n, arr, alo in segs:
            o_refs[arr][:, alo:alo + n] = g_ref[s, :, llo:llo + n]

    return pl.pallas_call(
        body, name=name, grid=(r // rb,),
        in_specs=[pl.BlockSpec((N_DEV, rb, w), lambda i: (0, i, 0))],
        out_specs=[pl.BlockSpec((rb, n), lambda i: (i, 0)) for n in widths],
        out_shape=[jax.ShapeDtypeStruct((r, n), g8.dtype) for n in widths],
        compiler_params=_params(("parallel",)),
    )(g8)


def _reshard(srcs, col_map, w, dtype, name, shards=(0, N_DEV)):
    r = srcs[0].shape[0]
    rb = min(r, COPY_ROWS)
    lo, hi = shards
    segs = [sg for sg in _segments(col_map, w) if lo <= sg[0] < hi]

    def body(*refs):
        o_ref = refs[-1]
        for s, llo, n, arr, alo in segs:
            o_ref[s - lo, :, llo:llo + n] = refs[arr][:, alo:alo + n].astype(dtype)

    return pl.pallas_call(
        body, name=name, grid=(r // rb,),
        in_specs=[pl.BlockSpec((rb, a.shape[1]), lambda i: (i, 0)) for a in srcs],
        out_specs=pl.BlockSpec((hi - lo, rb, w), lambda i: (0, i, 0)),
        out_shape=jax.ShapeDtypeStruct((hi - lo, r, w), dtype),
        compiler_params=_params(("parallel",)),
    )(*srcs)


def _adamw(parts, w, m, v, name):
    r, c = w.shape
    tr = COPY_ROWS if r % COPY_ROWS == 0 else r

    def body(p_ref, w_ref, m_ref, v_ref, g_ref, d_ref, mo_ref, vo_ref):
        g = p_ref[0].astype(F32)
        for s in range(1, N_DEV):
            g = g + p_ref[s].astype(F32)
        g_ref[...] = g
        d_ref[...], mo_ref[...], vo_ref[...] = _adamw_math(g, w_ref[...], m_ref[...], v_ref[...])

    blk = pl.BlockSpec((tr, c), lambda i: (i, 0))
    out = jax.ShapeDtypeStruct((r, c), F32)
    return pl.pallas_call(
        body, name=name, grid=(r // tr,),
        in_specs=[pl.BlockSpec((N_DEV, tr, c), lambda i: (0, i, 0)), blk, blk, blk],
        out_specs=[blk, blk, blk, blk], out_shape=[out, out, out, out],
        compiler_params=_params(("parallel",)),
    )(parts, w, m, v)


def _adamw_cols(parts, wt, mt, vt, name):
    _, r, c = parts.shape
    per = r // LANES
    linear = wt.shape != (c, r)
    assert wt.shape == ((c * per, LANES) if linear else (c, r)) and (per == SUBLANES or not linear)
    n = min(c, LANES)
    starts = list(range(0, c - n + 1, LANES)) + ([c - n] if c % n else [])

    def body(p_ref, w_ref, m_ref, v_ref, g_ref, d_ref, mo_ref, vo_ref, gt_sc, pad_sc):
        for lo in starts:
            g = p_ref[0, :, lo:lo + n].astype(F32)
            for s in range(1, N_DEV):
                g = g + p_ref[s, :, lo:lo + n].astype(F32)
            if n < LANES:
                pad_sc[...] = jnp.zeros_like(pad_sc)
                pad_sc[:, 0:n] = g
                g = pad_sc[...]
            gt = g.T
            if linear:
                for k in range(per):
                    gt_sc[pl.ds(lo * per + k, n, stride=per), :] = gt[0:n, k * LANES:(k + 1) * LANES]
            else:
                gt_sc[lo:lo + n, :] = gt[0:n]
        g = gt_sc[...]
        d, mn, vn = _adamw_math(g, w_ref[...], m_ref[...], v_ref[...])
        for ref, val in ((g_ref, g), (d_ref, d), (mo_ref, mn), (vo_ref, vn)):
            ref[...] = val.reshape(c, 1, r) if linear else val

    out = jax.ShapeDtypeStruct((c, 1, r) if linear else (c, r), F32)
    return pl.pallas_call(
        body, name=name, out_shape=[out, out, out, out],
        scratch_shapes=[pltpu.VMEM(wt.shape, F32), pltpu.VMEM((r, LANES), F32)],
        compiler_params=pltpu.CompilerParams(vmem_limit_bytes=VMEM_LIMIT),
    )(parts, wt, mt, vt)


def _adamw_math(g, w, m, v):
    mn = ADAM_B1 * m + (1.0 - ADAM_B1) * g
    vn = ADAM_B2 * v + (1.0 - ADAM_B2) * (g * g)
    m_hat = mn / (1.0 - ADAM_B1 ** ADAM_STEP)
    v_hat = vn / (1.0 - ADAM_B2 ** ADAM_STEP)
    return -ADAM_LR * (m_hat / (jnp.sqrt(v_hat) + ADAM_EPS) + ADAM_WD * w), mn, vn


SMALL = (("ab_conv_w", 0, 4, 64), ("ssd_conv_w", 4, 4, 384), ("ssd_conv_b", 8, 1, 384), ("ssd_norm", 9, 1, 256),
         ("ssd_ln_g", 10, 1, 128), ("ssd_ln_b", 11, 1, 128))
VECS = (("ab_conv_b", 512), ("ab_gate_a_b", 512), ("ab_gate_x_b", 512), ("ab_lambda", 512), ("mla_q_norm", 256),
        ("mla_kv_norm", 128), ("ab_ln_g", 1024), ("ab_ln_b", 1024), ("ssd_dt_bias", 32), ("ssd_a_log", 32),
        ("ssd_d", 32))
GATES = ("ab_gate_a_w", "ab_gate_x_w")
SMALL_NAMES = tuple(n for n, *_ in SMALL) + tuple(n for n, _ in VECS) + GATES
VMEM_WHOLE = pl.BlockSpec(memory_space=pltpu.VMEM)


def _view2d(name, a):
    if name in GATES:
        return a.reshape(RNN_W, 64)
    return a[0] if a.ndim == 3 else a


def _unshard_small(g):
    widths = (512, 3072, 3072, 2048, 1024, 1024)

    def body(*refs):
        ins, outs = refs[:6], refs[6:]
        outs[0][...] = jnp.zeros_like(outs[0])
        outs[1][...] = jnp.zeros_like(outs[1])
        for (_, _, nr, c), i_ref, o_ref in zip(SMALL, ins, outs):
            for j in range(N_DEV):
                o_ref[0:nr, j * c:(j + 1) * c] = i_ref[j]

    return pl.pallas_call(
        body, name="unshard_small", in_specs=[VMEM_WHOLE] * 6, out_specs=[VMEM_WHOLE] * 6,
        out_shape=[jax.ShapeDtypeStruct((SUBLANES if nr == 4 else 1, w), F32) for (_, _, nr, _), w in zip(SMALL, widths)],
    )(*g)


def _prep_repl(ga, gx, dt_bias, a_log, d):
    def body(ga_ref, gx_ref, b_ref, al_ref, d_ref, wa_ref, wx_ref, b128_ref, al128_ref, dx_ref):
        wa_ref[...] = jnp.zeros_like(wa_ref)
        wx_ref[...] = jnp.zeros_like(wx_ref)
        for hd in range(8):
            hs = slice(hd * 64, (hd + 1) * 64)
            wa_ref[hs, hs] = _mx(ga_ref[hs, :])
            wx_ref[hs, hs] = _mx(gx_ref[hs, :])
        b128_ref[...] = jnp.zeros_like(b128_ref)
        al128_ref[...] = jnp.zeros_like(al128_ref)
        b128_ref[:, 0:SSD_HEADS] = b_ref[...]
        al128_ref[:, 0:SSD_HEADS] = al_ref[...]
        dv = d_ref[...]
        for hd in range(SSD_HEADS):
            dx_ref[:, hd * SSD_P:(hd + 1) * SSD_P] = jnp.broadcast_to(dv[:, hd:hd + 1], (1, SSD_P))

    return pl.pallas_call(
        body, name="prep_repl", in_specs=[VMEM_WHOLE] * 5, out_specs=[VMEM_WHOLE] * 5,
        out_shape=[jax.ShapeDtypeStruct((RNN_W, RNN_W), MXU_DTYPE), jax.ShapeDtypeStruct((RNN_W, RNN_W), MXU_DTYPE),
                   jax.ShapeDtypeStruct((1, LANES), F32), jax.ShapeDtypeStruct((1, LANES), F32),
                   jax.ShapeDtypeStruct((1, SSD_INNER), F32)],
    )(ga, gx, dt_bias, a_log, d)


LOSS_ROW = 11


def _pack_small(dvec0, g_wa, g_wx, dqnw, dknw, dgb0, dvec1, dcw1, dnw, dgb1, loss8):
    def body(dvec0_ref, gwa_ref, gwx_ref, dqn_ref, dkn_ref, dgb0_ref, dvec1_ref, dcw1_ref, dnw_ref, dgb1_ref,
             loss_ref, sm_ref, vec_ref, gg_ref):
        sm_ref[...] = jnp.zeros_like(sm_ref)
        vec_ref[...] = jnp.zeros_like(vec_ref)
        sharded = ((dvec0_ref, 4), (dcw1_ref, 0), (dcw1_ref, 4), (dnw_ref, 0), (dgb1_ref, 0), (dgb1_ref, 1))
        for (_, r0, nr, c), (src, sr) in zip(SMALL, sharded):
            for j in range(N_DEV):
                sm_ref[j, r0:r0 + nr, 0:c] = src[sr:sr + nr, j * c:(j + 1) * c]
        vectors = ((dvec0_ref, 3), (dvec0_ref, 0), (dvec0_ref, 1), (dvec0_ref, 2), (dqn_ref, 0), (dkn_ref, 0),
                   (dgb0_ref, 0), (dgb0_ref, 1), (dvec1_ref, 0), (dvec1_ref, 1), (dvec1_ref, 2))
        for row, ((_, c), (src, sr)) in enumerate(zip(VECS, vectors)):
            vec_ref[row:row + 1, 0:c] = src[sr:sr + 1, 0:c]
        vec_ref[LOSS_ROW:LOSS_ROW + 1, 0:LANES] = loss_ref[0:1, :]
        for hd in range(8):
            hs = slice(hd * 64, (hd + 1) * 64)
            gg_ref[hs, 0:64] = _mx(gwa_ref[hs, hs])
            gg_ref[hs, 64:128] = _mx(gwx_ref[hs, hs])

    return pl.pallas_call(
        body, name="pack_small", in_specs=[VMEM_WHOLE] * 11, out_specs=[VMEM_WHOLE] * 3,
        out_shape=[jax.ShapeDtypeStruct((N_DEV, 16, 384), F32), jax.ShapeDtypeStruct((16, 1024), F32),
                   jax.ShapeDtypeStruct((RNN_W, LANES), MXU_DTYPE)],
    )(dvec0, g_wa, g_wx, dqnw, dknw, dgb0, dvec1, dcw1, dnw, dgb1, loss8)


def _adamw_small(recv_sm, recv_vec, recv_gg, wmv):
    plan = ([(0, r0, nr, c) for _, r0, nr, c in SMALL] + [(1, row, 1, c) for row, (_, c) in enumerate(VECS)]
            + [(2, 0, RNN_W, 0), (2, 0, RNN_W, 64)])
    n = len(plan)

    def body(*refs):
        recv, ins, outs = refs[:3], refs[3:3 + 3 * n], refs[3 + 3 * n:]
        for i, (src, r0, nr, c) in enumerate(plan):
            cols = slice(c, c + 64) if src == 2 else slice(0, c)
            g = recv[src][0, r0:r0 + nr, cols].astype(F32)
            for s in range(1, N_DEV):
                g = g + recv[src][s, r0:r0 + nr, cols].astype(F32)
            w_ref, m_ref, v_ref = ins[3 * i:3 * i + 3]
            outs[4 * i][...] = g
            outs[4 * i + 1][...], outs[4 * i + 2][...], outs[4 * i + 3][...] = _adamw_math(
                g, w_ref[...], m_ref[...], v_ref[...])
        loss = recv[1][0, LOSS_ROW:LOSS_ROW + 1, 0:LANES]
        for s in range(1, N_DEV):
            loss = loss + recv[1][s, LOSS_ROW:LOSS_ROW + 1, 0:LANES]
        outs[4 * n][...] = loss

    flat = [a for t in wmv for a in t]
    return pl.pallas_call(
        body, name="adamw_small", in_specs=[VMEM_WHOLE] * (3 + 3 * n), out_specs=[VMEM_WHOLE] * (4 * n + 1),
        out_shape=[jax.ShapeDtypeStruct(t[0].shape, F32) for t in wmv for _ in range(4)]
        + [jax.ShapeDtypeStruct((1, LANES), F32)],
    )(recv_sm, recv_vec, recv_gg, *flat)


BIG_L0 = ("ab_w_in", "ab_w_out", "mla_w_uq", "mla_w_ukv")
BIG_L1 = ("ssd_w_in", "ssd_w_out")
COLUMN_SHARDED = ("ab_w_in", "mla_w_uq", "ssd_w_in")

MAP_W0 = ((0, 512, 0, 1024), (512, 1536, 0, 0), (1536, 1920, 0, 1536), (1920, 1952, 0, 1984))
MAP_W1 = ((0, 2048, 0, 0), (2048, 5120, 1, 0), (5120, 5152, 2, 0))
MAP_WQ = tuple((96 * hd, 96 * hd + 96, 0, 128 * hd) for hd in range(8))
MAP_WKV = (tuple((128 * hd, 128 * hd + 64, 0, 128 * hd) for hd in range(8))
           + tuple((128 * hd + 64, 128 * hd + 128, 0, 1024 + 64 * hd) for hd in range(8)))
MAP_G0 = ((0, 512, 0, 0), (512, 1536, 1, 0), (1536, 1920, 2, 0), (1920, 1952, 2, 448))
W0_EARLY, W0_LATE = (0, 6), (6, 8)


def kernel(x, positions, ab_w_in, ab_conv_w, ab_conv_b, ab_gate_a_w, ab_gate_a_b, ab_gate_x_w, ab_gate_x_b, ab_lambda, mla_q_norm, mla_kv_norm, mla_w_uq, mla_w_ukv, ab_w_out, ab_ln_g, ab_ln_b, ssd_w_in, ssd_conv_w, ssd_conv_b, ssd_dt_bias, ssd_a_log, ssd_d, ssd_norm, ssd_w_out, ssd_ln_g, ssd_ln_b, loss_target, m_ab_w_in, m_ab_conv_w, m_ab_conv_b, m_ab_gate_a_w, m_ab_gate_a_b, m_ab_gate_x_w, m_ab_gate_x_b, m_ab_lambda, m_mla_q_norm, m_mla_kv_norm, m_mla_w_uq, m_mla_w_ukv, m_ab_w_out, m_ab_ln_g, m_ab_ln_b, m_ssd_w_in, m_ssd_conv_w, m_ssd_conv_b, m_ssd_dt_bias, m_ssd_a_log, m_ssd_d, m_ssd_norm, m_ssd_w_out, m_ssd_ln_g, m_ssd_ln_b, v_ab_w_in, v_ab_conv_w, v_ab_conv_b, v_ab_gate_a_w, v_ab_gate_a_b, v_ab_gate_x_w, v_ab_gate_x_b, v_ab_lambda, v_mla_q_norm, v_mla_kv_norm, v_mla_w_uq, v_mla_w_ukv, v_ab_w_out, v_ab_ln_g, v_ab_ln_b, v_ssd_w_in, v_ssd_conv_w, v_ssd_conv_b, v_ssd_dt_bias, v_ssd_a_log, v_ssd_d, v_ssd_norm, v_ssd_w_out, v_ssd_ln_g, v_ssd_ln_b):
    args = dict(locals())
    bf = MXU_DTYPE
    big = {n: [args[pre + n][0] for pre in ("", "m_", "v_")] for n in BIG_L0 + BIG_L1}
    sml = {n: [_view2d(n, args[pre + n]) for pre in ("", "m_", "v_")] for n in SMALL_NAMES}

    w0_8, cw0_8 = _all_gather([big["ab_w_in"][0].astype(bf), sml["ab_conv_w"][0]], "gather_params")
    p = {"cw0_8": cw0_8, "l0_blocks": [big[n][0].astype(bf) for n in BIG_L0[1:]] + [sml[n][0] for n, *_ in SMALL[1:]]}
    p["w0p"], = _unshard(w0_8, MAP_W0, (2048,), "unshard_w0")
    p["wa"], p["wx"], p["dt_bias"], p["a_log"], p["d_x"] = _prep_repl(
        sml["ab_gate_a_w"][0], sml["ab_gate_x_w"][0], sml["ssd_dt_bias"][0], sml["ssd_a_log"][0], sml["ssd_d"][0])
    for key, n in (("cb0", "ab_conv_b"), ("ba", "ab_gate_a_b"), ("bx", "ab_gate_x_b"), ("lam", "ab_lambda"),
                   ("qn_w", "mla_q_norm"), ("kn_w", "mla_kv_norm"), ("g0", "ab_ln_g"), ("b0", "ab_ln_b")):
        p[key] = sml[n][0]

    _, recv_early, recv, _, grad_x = _local_step(
        x[0], positions[0], loss_target[0], p, [big[n][0].astype(bf) for n in BIG_L1])

    parts = dict(recv_early, ab_w_in=recv[0], mla_w_uq=recv[1], mla_w_ukv=recv[2])

    outs = {}
    kinds = ("grad", "delta", "new_m", "new_v")
    for n in BIG_L0 + BIG_L1:
        if n in COLUMN_SHARDED:
            rows, cols = big[n][0].shape
            if rows == SUBLANES * LANES:
                wmv_t = [jnp.transpose(args[pre + n], (2, 0, 1)).reshape(cols * SUBLANES, LANES) for pre in ("", "m_", "v_")]
                back = lambda res: jnp.transpose(res, (1, 2, 0))
            else:
                wmv_t = [args[pre + n][0].T for pre in ("", "m_", "v_")]
                back = lambda res: res.T[None]
            for kind, res in zip(kinds, _adamw_cols(parts[n], *wmv_t, "adamw_" + n)):
                outs[kind, n] = back(res)
            continue
        for kind, res in zip(kinds, _adamw(parts[n], *big[n], "adamw_" + n)):
            outs[kind, n] = res[None]
    res = _adamw_small(*recv[3:], [sml[n] for n in SMALL_NAMES])
    for i, n in enumerate(SMALL_NAMES):
        for k, kind in enumerate(kinds):
            outs[kind, n] = res[4 * i + k].reshape(args[n].shape)

    loss = res[4 * len(SMALL_NAMES)][0, 0]
    order = ["ab_w_in", "ab_conv_w", "ab_conv_b", "ab_gate_a_w", "ab_gate_a_b", "ab_gate_x_w", "ab_gate_x_b",
             "ab_lambda", "mla_q_norm", "mla_kv_norm", "mla_w_uq", "mla_w_ukv", "ab_w_out", "ab_ln_g", "ab_ln_b",
             "ssd_w_in", "ssd_conv_w", "ssd_conv_b", "ssd_dt_bias", "ssd_a_log", "ssd_d", "ssd_norm", "ssd_w_out",
             "ssd_ln_g", "ssd_ln_b"]
    return (loss, grad_x[None], *[outs[kind, n] for kind in ("grad", "delta", "new_m", "new_v") for n in order])


def _local_step(x, pos, target, p, l1_blocks):
    bf = MXU_DTYPE
    inv_freq = 10000.0 ** (-jnp.arange(0, 32, 2, dtype=F32) / 32)
    ang = inv_freq[:, None] * pos.astype(F32)[None, :]
    cos, sin = jnp.cos(ang), jnp.sin(ang)
    zeros = lambda n: jnp.zeros((n, SEQ), F32)
    tc = jnp.concatenate([jnp.ones((64, SEQ), F32), cos, cos, zeros(32)], axis=0)
    tsa = jnp.concatenate([zeros(64), -sin, zeros(48)], axis=0)
    tsb = jnp.concatenate([zeros(80), sin, zeros(32)], axis=0)

    w0p, wa, wxg = (p[k] for k in ("w0p", "wa", "wx"))
    cb0, ba, bx, lam = (p[k] for k in ("cb0", "ba", "bx", "lam"))
    qn_w, kn_w, g0, b0 = (p[k] for k in ("qn_w", "kn_w", "g0", "b0"))
    dt_bias, a_log, d_x = (p[k] for k in ("dt_bias", "a_log", "d_x"))
    tril = jnp.tril(jnp.ones((SSD_L, SSD_L), F32))
    expand_t = (jnp.arange(SSD_INNER)[:, None] // SSD_P == jnp.arange(LANES)[None, :]).astype(jnp.bfloat16)

    proj0, xb, l0_8 = _l0_in(x, w0p, bcast=p["l0_blocks"])
    wo0 = l0_8[0].reshape(D_MODEL, D_MODEL)
    wq, = _unshard(l0_8[1], MAP_WQ, (1024,), "unshard_wq")
    wkv, = _unshard(l0_8[2], MAP_WKV, (1536,), "unshard_wkv")
    cw0, cw1, cb1, nw, g1, b1 = _unshard_small([p["cw0_8"]] + list(l0_8[3:]))
    xc, h = _rglru_fwd(proj0, cw0, cb0, wa, ba, wxg, bx, lam)
    qn, kn, qc, kc, vc = _mla_fwd(proj0, qn_w, kn_w, wq, wkv, tc, tsa, tsb)
    o, lse, (w1_8,) = _flash_fwd(qc, kc, vc, bcast=l1_blocks[:1])
    w1z, w1x, w1d = _unshard(w1_8, MAP_W1, (2048, 3072, 128), "unshard_w1")
    y0, v0, x1, x1b = _l0_out(h, o, proj0, x, wo0, g0, b0)

    z, dt_raw = _l1_in(x1b, w1z, w1d)
    xbc, pre, act = _ssd_conv_fwd(x1b, w1x, cw1, cb1)
    ys, hprev, (wo1_8,) = _ssd_scan_fwd(act, dt_raw, dt_bias, a_log, d_x, tril, expand_t, bcast=l1_blocks[1:])
    wo1 = wo1_8.reshape(SSD_INNER, D_MODEL)
    dv1, dgb1, loss8, g_wo1 = _l1_out(ys, z, nw, wo1, x1, g1, b1, target)

    dys, dz, dnw, g_z = _l1_gate_bwd(dv1, wo1, ys, z, nw, x1b)
    dact, ddt_raw, dvec1, g_dt, (recv_wo1,) = _ssd_scan_bwd(
        dys, act, dt_raw, hprev, dt_bias, a_log, d_x, tril, expand_t, x1b,
        scatter=[g_wo1.reshape(N_DEV, 256, D_MODEL)])
    dxbc, dcw1, g_xbc = _ssd_conv_bwd(dact, pre, xbc, cw1, x1b)

    dv0, dgb0 = _l1_dx_ln(dz, dxbc, ddt_raw, dv1, v0, w1z, w1x, w1d, g0)
    dh, do, dgate, g_wo0, g_gate = _gate_bwd(dv0, wo0, h, o, proj0, y0, xb)
    dxr, g_wa, g_wx, dvec0, g_rnn = _rglru_bwd(dh, xc, h, proj0, cw0, wa, ba, wxg, bx, lam, xb)
    early = [_reshard([g_z, g_xbc, g_dt], MAP_W1, 644, bf, "reshard_w1"), g_wo0.astype(bf).reshape(N_DEV, 128, D_MODEL),
             (_reshard([g_rnn, g_gate], MAP_G0, 244, bf, "reshard_w0_early", shards=W0_EARLY), W0_EARLY)]
    dq, dk, dvv, (recv_w1, recv_wo0, recv_w0) = _flash_bwd(qc, kc, vc, o, do, lse, scatter=early)
    recv_early = {"ssd_w_in": recv_w1, "ssd_w_out": recv_wo1, "ab_w_out": recv_wo0, "ab_w_in": recv_w0}
    dtail, g_wq, g_wkv, dqnw, dknw, g_tail = _mla_bwd(dq, dk, dvv, proj0, qn, kn, qn_w, kn_w, wq, wkv, tc, tsa, tsb, xb)

    acc = {"g_rnn": g_rnn, "g_gate": g_gate, "g_tail": g_tail, "g_wq": g_wq, "g_wkv": g_wkv,
           "dvec0": dvec0, "g_wa": g_wa, "g_wx": g_wx, "dqnw": dqnw, "dknw": dknw, "dgb0": dgb0, "dvec1": dvec1,
           "dcw1": dcw1, "dnw": dnw, "dgb1": dgb1}
    late = [(_reshard([g_rnn, g_gate, g_tail], MAP_G0, 244, bf, "reshard_w0_late", shards=W0_LATE), W0_LATE),
            _reshard([g_wq], MAP_WQ, 96, bf, "reshard_wq"), _reshard([g_wkv], MAP_WKV, 128, bf, "reshard_wkv")]
    sm_slots, vec_rows, gates = _pack_small(dvec0, g_wa, g_wx, dqnw, dknw, dgb0, dvec1, dcw1, dnw, dgb1, loss8)
    dx, recv_late = _l0_dx(dxr, dgate, dtail, w0p, dv0, scatter=late + [sm_slots], bcast=[vec_rows, gates], into=recv_w0)
    del recv_early["ab_w_in"]
    return acc, recv_early, recv_late, loss8[0, 0], dx
```
